```python
import math
import jax, jax.numpy as jnp
from jax import lax
import numpy as np

D_MODEL = 1024
BATCH = 16
SEQ = 2048
DEPTH = 1

HEAD_DIM = 64
GROUPS = ((128, 1), (512, 4), (2048, 16))
N_GROUPS = len(GROUPS)
HEADS_PER_GROUP = 8
N_HEADS = N_GROUPS * HEADS_PER_GROUP
ATTN_WIDTH = N_HEADS * HEAD_DIM
ATTN_OUT_WIDTH = HEADS_PER_GROUP * HEAD_DIM
Q_BLOCK = 128
CONV_WIDTH = D_MODEL
CONV_KERNEL = 31
N_BRANCHES = 2
D_FF = -(-8 * D_MODEL // (3 * 256)) * 256
IN_WIDTH = 3 * ATTN_WIDTH + 2 * CONV_WIDTH + N_BRANCHES * D_MODEL
RMS_EPS = 1e-6
LN_EPS = 1e-5

kernel_name = "hybrid_dilated_attn_conformer_conv_gated"


def _alibi_slope_list(n):
    def pow2(m):
        start = 2.0 ** (-8.0 / m)
        return [start ** (i + 1) for i in range(m)]
    if math.log2(n).is_integer():
        return pow2(n)
    c = 2 ** math.floor(math.log2(n))
    return pow2(c) + _alibi_slope_list(2 * c)[0::2][: n - c]


def _alibi_slopes():
    s = sorted(_alibi_slope_list(N_HEADS), reverse=True)
    return np.asarray(s, dtype=np.float32).reshape(N_GROUPS, HEADS_PER_GROUP)


def _rmsnorm(x, g):
    x32 = x.astype(jnp.float32)
    y = x32 * lax.rsqrt(jnp.mean(x32 * x32, axis=-1, keepdims=True) + RMS_EPS)
    return (y * g.astype(jnp.float32)).astype(x.dtype)


def _layernorm(x, g, b):
    x32 = x.astype(jnp.float32)
    mu = jnp.mean(x32, axis=-1, keepdims=True)
    var = jnp.mean(jnp.square(x32 - mu), axis=-1, keepdims=True)
    y = (x32 - mu) * lax.rsqrt(var + LN_EPS)
    return (y * g.astype(jnp.float32) + b.astype(jnp.float32)).astype(x.dtype)


def _dilated_group(q, k, v, slopes, window, dilation):
    B, S, Hg, hd = q.shape
    r = dilation
    L = S // r
    n_back = window // r
    assert n_back <= Q_BLOCK
    nb = -(-L // Q_BLOCK)
    Lp = nb * Q_BLOCK

    def to_sub(t):
        return t.reshape(B, L, r, Hg, hd).transpose(0, 2, 3, 1, 4)

    qb = jnp.pad(to_sub(q), ((0, 0), (0, 0), (0, 0), (0, Lp - L), (0, 0)))
    qb = qb.reshape(B, r, Hg, nb, Q_BLOCK, hd)

    def band(t):
        t = jnp.pad(to_sub(t), ((0, 0), (0, 0), (0, 0), (Q_BLOCK, Lp - L), (0, 0)))
        t = t.reshape(B, r, Hg, nb + 1, Q_BLOCK, hd)
        return jnp.concatenate([t[:, :, :, :-1], t[:, :, :, 1:]], axis=4)

    kb, vb = band(k), band(v)
    qi = jnp.arange(Q_BLOCK)[:, None]
    kj = jnp.arange(2 * Q_BLOCK)[None, :]
    rel = Q_BLOCK + qi - kj
    kpos = (jnp.arange(nb)[:, None, None] - 1) * Q_BLOCK + kj[None]
    valid = (rel >= 0) & (rel <= n_back) & (kpos >= 0)
    dist = (rel * r).astype(jnp.float32)

    s = jnp.einsum('brhnqd,brhnkd->brhnqk', qb, kb).astype(jnp.float32) * (hd ** -0.5)
    s = s - slopes.astype(jnp.float32)[:, None, None, None] * dist
    s = jnp.where(valid, s, -jnp.inf)
    m = jnp.max(s, axis=-1, keepdims=True)
    p = jnp.exp(s - m)
    denom = jnp.sum(p, axis=-1)
    o = jnp.einsum('brhnqk,brhnkd->brhnqd', p, vb.astype(jnp.float32)) / denom[..., None]
    lse = m[..., 0] + jnp.log(denom)

    o = o.reshape(B, r, Hg, Lp, hd)[:, :, :, :L].transpose(0, 3, 1, 2, 4).reshape(B, S, Hg, hd)
    lse = lse.reshape(B, r, Hg, Lp)[:, :, :, :L].transpose(0, 3, 1, 2).reshape(B, S, Hg)
    return o, lse


def _causal_depthwise_conv(u, w, b):
    C = u.shape[-1]
    y = lax.conv_general_dilated(
        u, w[:, None, :], window_strides=(1,), padding=[(CONV_KERNEL - 1, 0)],
        dimension_numbers=('NWC', 'WIO', 'NWC'), feature_group_count=C)
    return y + b


def _fwd_setup_inputs(seed: int = 0) -> dict:
    key = jax.random.key(seed)
    ks = jax.random.split(key, 17)
    f32 = jnp.float32

    def w(k, shape, fan_in):
        return jax.random.normal(k, shape, f32) * (fan_in ** -0.5)

    def gain(k, shape):
        return 1.0 + 0.05 * jax.random.normal(k, shape, f32)

    D = DEPTH
    return {
        "x": jax.random.normal(ks[0], (BATCH, SEQ, D_MODEL), f32),
        "norm1_g": gain(ks[1], (D, D_MODEL)),
        "w_in": w(ks[2], (D, D_MODEL, IN_WIDTH), D_MODEL),
        "gate_b": 0.1 * jax.random.normal(ks[3], (D, N_BRANCHES * D_MODEL), f32),
        "conv_w": w(ks[4], (D, CONV_KERNEL, CONV_WIDTH), CONV_KERNEL),
        "conv_b": 0.02 * jax.random.normal(ks[5], (D, CONV_WIDTH), f32),
        "conv_ln_g": gain(ks[6], (D, CONV_WIDTH)),
        "conv_ln_b": 0.02 * jax.random.normal(ks[7], (D, CONV_WIDTH), f32),
        "w_conv_out": w(ks[8], (D, CONV_WIDTH, D_MODEL), CONV_WIDTH),
        "w_attn_out": w(ks[9], (D, ATTN_OUT_WIDTH, D_MODEL), ATTN_OUT_WIDTH),
        "w_o": w(ks[10], (D, D_MODEL, D_MODEL), D_MODEL),
        "norm2_g": gain(ks[11], (D, D_MODEL)),
        "w_ffn_gate": w(ks[12], (D, D_MODEL, D_FF), D_MODEL),
        "w_ffn_up": w(ks[13], (D, D_MODEL, D_FF), D_MODEL),
        "w_ffn_down": w(ks[14], (D, D_FF, D_MODEL), D_FF),
        "norm_f_g": gain(ks[15], (D_MODEL,)),
    }


def _fwd_reference(x, norm1_g, w_in, gate_b, conv_w, conv_b, conv_ln_g, conv_ln_b,
              w_conv_out, w_attn_out, w_o, norm2_g, w_ffn_gate, w_ffn_up,
              w_ffn_down, norm_f_g):
    B, S, _ = x.shape
    slopes = jnp.asarray(_alibi_slopes())
    splits = [ATTN_WIDTH, 2 * ATTN_WIDTH, 3 * ATTN_WIDTH, 3 * ATTN_WIDTH + 2 * CONV_WIDTH]
    for l in range(DEPTH):
        h = _rmsnorm(x, norm1_g[l])
        proj = h @ w_in[l]
        q, k, v, u, g_logits = jnp.split(proj, splits, axis=-1)
        q = q.reshape(B, S, N_GROUPS, HEADS_PER_GROUP, HEAD_DIM)
        k = k.reshape(B, S, N_GROUPS, HEADS_PER_GROUP, HEAD_DIM)
        v = v.reshape(B, S, N_GROUPS, HEADS_PER_GROUP, HEAD_DIM)

        outs, lses = [], []
        for g, (window, dilation) in enumerate(GROUPS):
            o, lse = _dilated_group(q[:, :, g], k[:, :, g], v[:, :, g], slopes[g], window, dilation)
            outs.append(o)
            lses.append(lse)
        alpha = jax.nn.softmax(jnp.stack(lses, axis=0), axis=0)
        y_attn = jnp.sum(alpha[..., None] * jnp.stack(outs, axis=0), axis=0)
        y_attn = y_attn.reshape(B, S, ATTN_OUT_WIDTH).astype(x.dtype) @ w_attn_out[l]

        ua, ub = jnp.split(u, 2, axis=-1)
        c = ua * jax.nn.sigmoid(ub)
        c = _causal_depthwise_conv(c, conv_w[l], conv_b[l])
        c = jax.nn.silu(_layernorm(c, conv_ln_g[l], conv_ln_b[l]))
        y_conv = c @ w_conv_out[l]

        gates = jax.nn.sigmoid(g_logits + gate_b[l])
        g_attn, g_conv = jnp.split(gates, 2, axis=-1)
        x = x + (g_attn * y_attn + g_conv * y_conv) @ w_o[l]

        h2 = _rmsnorm(x, norm2_g[l])
        x = x + (jax.nn.silu(h2 @ w_ffn_gate[l]) * (h2 @ w_ffn_up[l])) @ w_ffn_down[l]
    return _rmsnorm(x, norm_f_g)


import jax as _jax
import jax.numpy as _jnp

TWIN_FORMAT = 'train_step'
FWD_PARAMS = ['x', 'norm1_g', 'w_in', 'gate_b', 'conv_w', 'conv_b', 'conv_ln_g', 'conv_ln_b', 'w_conv_out', 'w_attn_out', 'w_o', 'norm2_g', 'w_ffn_gate', 'w_ffn_up', 'w_ffn_down', 'norm_f_g']
TWIN_WEIGHTS = ['norm1_g', 'w_in', 'gate_b', 'conv_w', 'conv_b', 'conv_ln_g', 'conv_ln_b', 'w_conv_out', 'w_attn_out', 'w_o', 'norm2_g', 'w_ffn_gate', 'w_ffn_up', 'w_ffn_down', 'norm_f_g']
TWIN_DIFF_INPUT = 'x'
TWIN_INPUTS = ['x', 'norm1_g', 'w_in', 'gate_b', 'conv_w', 'conv_b', 'conv_ln_g', 'conv_ln_b', 'w_conv_out', 'w_attn_out', 'w_o', 'norm2_g', 'w_ffn_gate', 'w_ffn_up', 'w_ffn_down', 'norm_f_g', 'loss_target', 'm_norm1_g', 'm_w_in', 'm_gate_b', 'm_conv_w', 'm_conv_b', 'm_conv_ln_g', 'm_conv_ln_b', 'm_w_conv_out', 'm_w_attn_out', 'm_w_o', 'm_norm2_g', 'm_w_ffn_gate', 'm_w_ffn_up', 'm_w_ffn_down', 'm_norm_f_g', 'v_norm1_g', 'v_w_in', 'v_gate_b', 'v_conv_w', 'v_conv_b', 'v_conv_ln_g', 'v_conv_ln_b', 'v_w_conv_out', 'v_w_attn_out', 'v_w_o', 'v_norm2_g', 'v_w_ffn_gate', 'v_w_ffn_up', 'v_w_ffn_down', 'v_norm_f_g']
TWIN_OUTPUTS = ['loss', 'grad_x', 'grad_norm1_g', 'grad_w_in', 'grad_gate_b', 'grad_conv_w', 'grad_conv_b', 'grad_conv_ln_g', 'grad_conv_ln_b', 'grad_w_conv_out', 'grad_w_attn_out', 'grad_w_o', 'grad_norm2_g', 'grad_w_ffn_gate', 'grad_w_ffn_up', 'grad_w_ffn_down', 'grad_norm_f_g', 'delta_norm1_g', 'delta_w_in', 'delta_gate_b', 'delta_conv_w', 'delta_conv_b', 'delta_conv_ln_g', 'delta_conv_ln_b', 'delta_w_conv_out', 'delta_w_attn_out', 'delta_w_o', 'delta_norm2_g', 'delta_w_ffn_gate', 'delta_w_ffn_up', 'delta_w_ffn_down', 'delta_norm_f_g', 'new_m_norm1_g', 'new_m_w_in', 'new_m_gate_b', 'new_m_conv_w', 'new_m_conv_b', 'new_m_conv_ln_g', 'new_m_conv_ln_b', 'new_m_w_conv_out', 'new_m_w_attn_out', 'new_m_w_o', 'new_m_norm2_g', 'new_m_w_ffn_gate', 'new_m_w_ffn_up', 'new_m_w_ffn_down', 'new_m_norm_f_g', 'new_v_norm1_g', 'new_v_w_in', 'new_v_gate_b', 'new_v_conv_w', 'new_v_conv_b', 'new_v_conv_ln_g', 'new_v_conv_ln_b', 'new_v_w_conv_out', 'new_v_w_attn_out', 'new_v_w_o', 'new_v_norm2_g', 'new_v_w_ffn_gate', 'new_v_w_ffn_up', 'new_v_w_ffn_down', 'new_v_norm_f_g']
TWIN_LEAF_KINDS = {'loss': 'loss', 'grad_x': 'grad_x', 'grad_norm1_g': 'grad_w', 'grad_w_in': 'grad_w', 'grad_gate_b': 'grad_w', 'grad_conv_w': 'grad_w', 'grad_conv_b': 'grad_w', 'grad_conv_ln_g': 'grad_w', 'grad_conv_ln_b': 'grad_w', 'grad_w_conv_out': 'grad_w', 'grad_w_attn_out': 'grad_w', 'grad_w_o': 'grad_w', 'grad_norm2_g': 'grad_w', 'grad_w_ffn_gate': 'grad_w', 'grad_w_ffn_up': 'grad_w', 'grad_w_ffn_down': 'grad_w', 'grad_norm_f_g': 'grad_w', 'delta_norm1_g': 'delta_w', 'delta_w_in': 'delta_w', 'delta_gate_b': 'delta_w', 'delta_conv_w': 'delta_w', 'delta_conv_b': 'delta_w', 'delta_conv_ln_g': 'delta_w', 'delta_conv_ln_b': 'delta_w', 'delta_w_conv_out': 'delta_w', 'delta_w_attn_out': 'delta_w', 'delta_w_o': 'delta_w', 'delta_norm2_g': 'delta_w', 'delta_w_ffn_gate': 'delta_w', 'delta_w_ffn_up': 'delta_w', 'delta_w_ffn_down': 'delta_w', 'delta_norm_f_g': 'delta_w', 'new_m_norm1_g': 'new_m', 'new_m_w_in': 'new_m', 'new_m_gate_b': 'new_m', 'new_m_conv_w': 'new_m', 'new_m_conv_b': 'new_m', 'new_m_conv_ln_g': 'new_m', 'new_m_conv_ln_b': 'new_m', 'new_m_w_conv_out': 'new_m', 'new_m_w_attn_out': 'new_m', 'new_m_w_o': 'new_m', 'new_m_norm2_g': 'new_m', 'new_m_w_ffn_gate': 'new_m', 'new_m_w_ffn_up': 'new_m', 'new_m_w_ffn_down': 'new_m', 'new_m_norm_f_g': 'new_m', 'new_v_norm1_g': 'new_v', 'new_v_w_in': 'new_v', 'new_v_gate_b': 'new_v', 'new_v_conv_w': 'new_v', 'new_v_conv_b': 'new_v', 'new_v_conv_ln_g': 'new_v', 'new_v_conv_ln_b': 'new_v', 'new_v_w_conv_out': 'new_v', 'new_v_w_attn_out': 'new_v', 'new_v_w_o': 'new_v', 'new_v_norm2_g': 'new_v', 'new_v_w_ffn_gate': 'new_v', 'new_v_w_ffn_up': 'new_v', 'new_v_w_ffn_down': 'new_v', 'new_v_norm_f_g': 'new_v'}


def _forward(args):
    return _fwd_reference(*[args[k] for k in FWD_PARAMS])


def _output_shape():
    out = _jax.eval_shape(lambda: _forward(_fwd_setup_inputs(0)))
    return out.shape, out.dtype

N_MICROBATCH = 1
ADAM_LR = 0.001
ADAM_B1 = 0.9
ADAM_B2 = 0.999
ADAM_EPS = 1e-08
ADAM_WD = 0.01
ADAM_STEP = 10
PER_EXAMPLE_BATCH_AXIS = {'x': 0, 'loss_target': 0}
SHARED_INPUTS = []
_WEIGHT_DTYPES = {'norm1_g': _jnp.float32, 'w_in': _jnp.float32, 'gate_b': _jnp.float32, 'conv_w': _jnp.float32, 'conv_b': _jnp.float32, 'conv_ln_g': _jnp.float32, 'conv_ln_b': _jnp.float32, 'w_conv_out': _jnp.float32, 'w_attn_out': _jnp.float32, 'w_o': _jnp.float32, 'norm2_g': _jnp.float32, 'w_ffn_gate': _jnp.float32, 'w_ffn_up': _jnp.float32, 'w_ffn_down': _jnp.float32, 'norm_f_g': _jnp.float32}
MOMENT_SCALE = {'norm1_g': 9.812451e-02, 'w_in': 3.204795e-02, 'gate_b': 1.971308e-02, 'conv_w': 6.283108e-02, 'conv_b': 1.329806e-01, 'conv_ln_g': 8.139494e-02, 'conv_ln_b': 7.534295e-02, 'w_conv_out': 6.200308e-02, 'w_attn_out': 3.872811e-02, 'w_o': 7.218959e-02, 'norm2_g': 1.300206e-01, 'w_ffn_gate': 5.439753e-02, 'w_ffn_up': 5.279643e-02, 'w_ffn_down': 8.772363e-02, 'norm_f_g': 3.200943e+01}


def _to_microbatches(a, axis):
    t = _jnp.moveaxis(a, axis, 0)
    t = t.reshape((N_MICROBATCH, t.shape[0] // N_MICROBATCH) + t.shape[1:])
    return _jnp.moveaxis(t, 1, axis + 1)


def setup_inputs(seed: int = 0) -> dict:
    inp = _fwd_setup_inputs(seed)
    key = _jax.random.fold_in(_jax.random.key(seed), 7919)
    shape, _ = _output_shape()
    out = dict(inp)
    out["loss_target"] = _jax.random.normal(_jax.random.fold_in(key, 0), shape, _jnp.float32)
    for i, name in enumerate(TWIN_WEIGHTS):
        w = inp[name].astype(_jnp.float32)
        if MOMENT_SCALE is None:
            s = _jnp.sqrt(_jnp.mean(_jnp.square(w)) + 1e-30)
        else:
            s = MOMENT_SCALE[name]
        km, kv = _jax.random.split(_jax.random.fold_in(key, i + 1))
        out[name] = w
        out["m_" + name] = s * _jax.random.normal(km, w.shape, _jnp.float32)
        out["v_" + name] = (s * s) * _jax.random.uniform(kv, w.shape, _jnp.float32, 0.5, 1.5)
    if N_MICROBATCH > 1:
        for name, axis in PER_EXAMPLE_BATCH_AXIS.items():
            out[name] = _to_microbatches(out[name], axis)
    return {'x': out['x'], 'norm1_g': out['norm1_g'], 'w_in': out['w_in'], 'gate_b': out['gate_b'], 'conv_w': out['conv_w'], 'conv_b': out['conv_b'], 'conv_ln_g': out['conv_ln_g'], 'conv_ln_b': out['conv_ln_b'], 'w_conv_out': out['w_conv_out'], 'w_attn_out': out['w_attn_out'], 'w_o': out['w_o'], 'norm2_g': out['norm2_g'], 'w_ffn_gate': out['w_ffn_gate'], 'w_ffn_up': out['w_ffn_up'], 'w_ffn_down': out['w_ffn_down'], 'norm_f_g': out['norm_f_g'], 'loss_target': out['loss_target'], 'm_norm1_g': out['m_norm1_g'], 'm_w_in': out['m_w_in'], 'm_gate_b': out['m_gate_b'], 'm_conv_w': out['m_conv_w'], 'm_conv_b': out['m_conv_b'], 'm_conv_ln_g': out['m_conv_ln_g'], 'm_conv_ln_b': out['m_conv_ln_b'], 'm_w_conv_out': out['m_w_conv_out'], 'm_w_attn_out': out['m_w_attn_out'], 'm_w_o': out['m_w_o'], 'm_norm2_g': out['m_norm2_g'], 'm_w_ffn_gate': out['m_w_ffn_gate'], 'm_w_ffn_up': out['m_w_ffn_up'], 'm_w_ffn_down': out['m_w_ffn_down'], 'm_norm_f_g': out['m_norm_f_g'], 'v_norm1_g': out['v_norm1_g'], 'v_w_in': out['v_w_in'], 'v_gate_b': out['v_gate_b'], 'v_conv_w': out['v_conv_w'], 'v_conv_b': out['v_conv_b'], 'v_conv_ln_g': out['v_conv_ln_g'], 'v_conv_ln_b': out['v_conv_ln_b'], 'v_w_conv_out': out['v_w_conv_out'], 'v_w_attn_out': out['v_w_attn_out'], 'v_w_o': out['v_w_o'], 'v_norm2_g': out['v_norm2_g'], 'v_w_ffn_gate': out['v_w_ffn_gate'], 'v_w_ffn_up': out['v_w_ffn_up'], 'v_w_ffn_down': out['v_w_ffn_down'], 'v_norm_f_g': out['v_norm_f_g']}


def _loss(weights, diff, rest, loss_target):
    with _jax.named_scope("forward"):
        args = {**rest, TWIN_DIFF_INPUT: diff, **{k: w.astype(_WEIGHT_DTYPES[k]) for k, w in weights.items()}}
        y = _forward(args)
    with _jax.named_scope("loss_head"):
        err = _jnp.square(y.astype(_jnp.float32) - loss_target)
        return 0.5 * _jnp.sum(_jnp.mean(err, axis=-1)) if err.ndim else 0.5 * err


def _adamw(w, g, m, v):
    m = ADAM_B1 * m + (1.0 - ADAM_B1) * g
    v = ADAM_B2 * v + (1.0 - ADAM_B2) * _jnp.square(g)
    m_hat = m / (1.0 - ADAM_B1 ** ADAM_STEP)
    v_hat = v / (1.0 - ADAM_B2 ** ADAM_STEP)
    delta = -ADAM_LR * (m_hat / (_jnp.sqrt(v_hat) + ADAM_EPS) + ADAM_WD * w)
    return delta, m, v


def reference(x, norm1_g, w_in, gate_b, conv_w, conv_b, conv_ln_g, conv_ln_b, w_conv_out, w_attn_out, w_o, norm2_g, w_ffn_gate, w_ffn_up, w_ffn_down, norm_f_g, loss_target, m_norm1_g, m_w_in, m_gate_b, m_conv_w, m_conv_b, m_conv_ln_g, m_conv_ln_b, m_w_conv_out, m_w_attn_out, m_w_o, m_norm2_g, m_w_ffn_gate, m_w_ffn_up, m_w_ffn_down, m_norm_f_g, v_norm1_g, v_w_in, v_gate_b, v_conv_w, v_conv_b, v_conv_ln_g, v_conv_ln_b, v_w_conv_out, v_w_attn_out, v_w_o, v_norm2_g, v_w_ffn_gate, v_w_ffn_up, v_w_ffn_down, v_norm_f_g):
    given = dict(x=x, norm1_g=norm1_g, w_in=w_in, gate_b=gate_b, conv_w=conv_w, conv_b=conv_b, conv_ln_g=conv_ln_g, conv_ln_b=conv_ln_b, w_conv_out=w_conv_out, w_attn_out=w_attn_out, w_o=w_o, norm2_g=norm2_g, w_ffn_gate=w_ffn_gate, w_ffn_up=w_ffn_up, w_ffn_down=w_ffn_down, norm_f_g=norm_f_g, loss_target=loss_target, m_norm1_g=m_norm1_g, m_w_in=m_w_in, m_gate_b=m_gate_b, m_conv_w=m_conv_w, m_conv_b=m_conv_b, m_conv_ln_g=m_conv_ln_g, m_conv_ln_b=m_conv_ln_b, m_w_conv_out=m_w_conv_out, m_w_attn_out=m_w_attn_out, m_w_o=m_w_o, m_norm2_g=m_norm2_g, m_w_ffn_gate=m_w_ffn_gate, m_w_ffn_up=m_w_ffn_up, m_w_ffn_down=m_w_ffn_down, m_norm_f_g=m_norm_f_g, v_norm1_g=v_norm1_g, v_w_in=v_w_in, v_gate_b=v_gate_b, v_conv_w=v_conv_w, v_conv_b=v_conv_b, v_conv_ln_g=v_conv_ln_g, v_conv_ln_b=v_conv_ln_b, v_w_conv_out=v_w_conv_out, v_w_attn_out=v_w_attn_out, v_w_o=v_w_o, v_norm2_g=v_norm2_g, v_w_ffn_gate=v_w_ffn_gate, v_w_ffn_up=v_w_ffn_up, v_w_ffn_down=v_w_ffn_down, v_norm_f_g=v_norm_f_g)
    weights = {n: given[n] for n in TWIN_WEIGHTS}
    shared = {n: given[n] for n in SHARED_INPUTS}
    per_example = {n: given[n] for n in ['x']}
    grad_fn = _jax.value_and_grad(_loss, argnums=(0, 1))

    def one_microbatch(ex, loss_target):
        ex = dict(ex)
        diff = ex.pop(TWIN_DIFF_INPUT)
        return grad_fn(weights, diff, {**shared, **ex}, loss_target)

    if N_MICROBATCH == 1:
        loss, (grad_w, grad_x) = one_microbatch(per_example, given["loss_target"])
    else:
        def body(carry, xs):
            loss_sum, grad_sum = carry
            l_k, (gw_k, gx_k) = one_microbatch(xs[0], xs[1])
            with _jax.named_scope("update"):
                return (loss_sum + l_k, _jax.tree.map(_jnp.add, grad_sum, gw_k)), gx_k

        init = (_jnp.zeros((), _jnp.float32), _jax.tree.map(_jnp.zeros_like, weights))
        (loss, grad_w), grad_x = _jax.lax.scan(body, init, (per_example, given["loss_target"]))
    with _jax.named_scope("update"):
        delta_w, new_m, new_v = {}, {}, {}
        for n in TWIN_WEIGHTS:
            delta_w[n], new_m[n], new_v[n] = _adamw(weights[n], grad_w[n], given["m_" + n], given["v_" + n])
    return (loss, grad_x, *[grad_w[n] for n in TWIN_WEIGHTS], *[delta_w[n] for n in TWIN_WEIGHTS],
            *[new_m[n] for n in TWIN_WEIGHTS], *[new_v[n] for n in TWIN_WEIGHTS])
```

```python
import functools
import math

import numpy as np
import jax
import jax.numpy as jnp
from jax import lax
from jax.experimental import pallas as pl
from jax.experimental.pallas import tpu as pltpu

f32 = jnp.float32
bf16 = jnp.bfloat16
SDS = jax.ShapeDtypeStruct
MESH = pl.DeviceIdType.MESH

D_MODEL = 1024
SEQ = 2048
HEAD_DIM = 64
HEADS = 8
GROUPS = ((128, 1), (512, 4), (2048, 16))
GW = HEADS * HEAD_DIM
ATTN_W = len(GROUPS) * GW
Q_BLOCK = 128
CONV_K = 31
D_FF = 2816
IN_W = 3 * ATTN_W + 2 * D_MODEL + 2 * D_MODEL
N_CHIPS = 4
IN_S = IN_W // N_CHIPS
FF_S = D_FF // N_CHIPS
RMS_EPS = 1e-6
LN_EPS = 1e-5
LR, B1, B2, ADAM_EPS, WD, STEP = 0.001, 0.9, 0.999, 1e-08, 0.01, 10
NEG = -1e30
LANES = 128
VMEM_LIMIT = 48 * 2 ** 20
CB = 512
UA_CB, UB_CB, GA_CB, GC_CB = 9, 11, 13, 15


def _alibi_slope_list(n):
    def pow2(m):
        start = 2.0 ** (-8.0 / m)
        return [start ** (i + 1) for i in range(m)]
    if math.log2(n).is_integer():
        return pow2(n)
    c = 2 ** math.floor(math.log2(n))
    return pow2(c) + _alibi_slope_list(2 * c)[0::2][: n - c]


_SLOPES = np.asarray(sorted(_alibi_slope_list(len(GROUPS) * HEADS), reverse=True), dtype=np.float32).reshape(len(GROUPS), HEADS)


def _cp(sem=None, vmem=VMEM_LIMIT):
    return pltpu.CompilerParams(dimension_semantics=sem, vmem_limit_bytes=vmem)


def _sigmoid(x):
    return 1.0 / (1.0 + jnp.exp(-x))


def _rms_fwd(x, g, name):
    T = x.shape[0]
    tm = 512

    def body(x_ref, g_ref, o_ref):
        xv = x_ref[...]
        r = lax.rsqrt(jnp.mean(xv * xv, axis=-1, keepdims=True) + RMS_EPS)
        o_ref[...] = (xv * r * g_ref[...]).astype(o_ref.dtype)

    row = pl.BlockSpec((tm, D_MODEL), lambda i: (i, 0))
    vec = pl.BlockSpec((1, D_MODEL), lambda i: (0, 0))
    return pl.pallas_call(body, grid=(T // tm,), in_specs=[row, vec], out_specs=row,
                          out_shape=SDS((T, D_MODEL), bf16), name=name, compiler_params=_cp(("parallel",)))(x, g)


def _rms_bwd(dy, x, g, dres, name):
    T = x.shape[0]
    tm = 512

    def body(dy_ref, x_ref, g_ref, r_ref, dx_ref, dg_ref):
        xv = x_ref[...]
        r = lax.rsqrt(jnp.mean(xv * xv, axis=-1, keepdims=True) + RMS_EPS)
        xh = xv * r
        dyv = dy_ref[...]
        dxh = dyv * g_ref[...]
        dx_ref[...] = r_ref[...] + r * (dxh - xh * jnp.mean(dxh * xh, axis=-1, keepdims=True))
        part = jnp.sum(dyv * xh, axis=0, keepdims=True)

        @pl.when(pl.program_id(0) == 0)
        def _():
            dg_ref[...] = part

        @pl.when(pl.program_id(0) > 0)
        def _():
            dg_ref[...] += part

    row = pl.BlockSpec((tm, D_MODEL), lambda i: (i, 0))
    vec = pl.BlockSpec((1, D_MODEL), lambda i: (0, 0))
    return pl.pallas_call(body, grid=(T // tm,), in_specs=[row, row, vec, row], out_specs=[row, vec],
                          out_shape=(SDS((T, D_MODEL), f32), SDS((1, D_MODEL), f32)), name=name,
                          compiler_params=_cp(("arbitrary",)))(dy, x, g, dres)


def _final_loss(x2, gf, target):
    T = x2.shape[0]
    tm = 512

    def body(x_ref, g_ref, t_ref, loss_ref, dx_ref, dg_ref):
        xv = x_ref[...]
        gv = g_ref[...]
        r = lax.rsqrt(jnp.mean(xv * xv, axis=-1, keepdims=True) + RMS_EPS)
        xh = xv * r
        e = xh * gv - t_ref[...]
        part_l = jnp.broadcast_to(0.5 * jnp.sum(jnp.mean(e * e, axis=-1, keepdims=True), axis=0, keepdims=True), (1, LANES))
        dy = e * (1.0 / D_MODEL)
        dxh = dy * gv
        dx_ref[...] = r * (dxh - xh * jnp.mean(dxh * xh, axis=-1, keepdims=True))
        part_g = jnp.sum(dy * xh, axis=0, keepdims=True)

        @pl.when(pl.program_id(0) == 0)
        def _():
            loss_ref[...] = part_l
            dg_ref[...] = part_g

        @pl.when(pl.program_id(0) > 0)
        def _():
            loss_ref[...] += part_l
            dg_ref[...] += part_g

    row = pl.BlockSpec((tm, D_MODEL), lambda i: (i, 0))
    vec = pl.BlockSpec((1, D_MODEL), lambda i: (0, 0))
    lvec = pl.BlockSpec((1, LANES), lambda i: (0, 0))
    return pl.pallas_call(body, grid=(T // tm,), in_specs=[row, vec, row], out_specs=[lvec, row, vec],
                          out_shape=(SDS((1, LANES), f32), SDS((T, D_MODEL), f32), SDS((1, D_MODEL), f32)),
                          name="final_loss", compiler_params=_cp(("arbitrary",)))(x2, gf, target)


def _ln_silu_fwd(c1, g, b):
    T = c1.shape[0]
    tm = 512

    def body(c_ref, g_ref, b_ref, o_ref):
        cv = c_ref[...]
        mu = jnp.mean(cv, axis=-1, keepdims=True)
        cc = cv - mu
        var = jnp.mean(cc * cc, axis=-1, keepdims=True)
        c2 = cc * lax.rsqrt(var + LN_EPS) * g_ref[...] + b_ref[...]
        o_ref[...] = (c2 * _sigmoid(c2)).astype(o_ref.dtype)

    row = pl.BlockSpec((tm, D_MODEL), lambda i: (i, 0))
    vec = pl.BlockSpec((1, D_MODEL), lambda i: (0, 0))
    return pl.pallas_call(body, grid=(T // tm,), in_specs=[row, vec, vec], out_specs=row,
                          out_shape=SDS((T, D_MODEL), bf16), name="ln_silu_fwd", compiler_params=_cp(("parallel",)))(c1, g, b)


def _ln_silu_bwd(dc3, c1, g, b):
    T = c1.shape[0]
    tm = 512

    def body(d_ref, c_ref, g_ref, b_ref, dc_ref, dg_ref, db_ref):
        cv = c_ref[...]
        gv = g_ref[...]
        mu = jnp.mean(cv, axis=-1, keepdims=True)
        cc = cv - mu
        var = jnp.mean(cc * cc, axis=-1, keepdims=True)
        rs = lax.rsqrt(var + LN_EPS)
        xh = cc * rs
        c2 = xh * gv + b_ref[...]
        sg = _sigmoid(c2)
        dc2 = d_ref[...].astype(f32) * (sg * (1.0 + c2 * (1.0 - sg)))
        dxh = dc2 * gv
        dc_ref[...] = rs * (dxh - jnp.mean(dxh, axis=-1, keepdims=True) - xh * jnp.mean(dxh * xh, axis=-1, keepdims=True))
        pg = jnp.sum(dc2 * xh, axis=0, keepdims=True)
        pb = jnp.sum(dc2, axis=0, keepdims=True)

        @pl.when(pl.program_id(0) == 0)
        def _():
            dg_ref[...] = pg
            db_ref[...] = pb

        @pl.when(pl.program_id(0) > 0)
        def _():
            dg_ref[...] += pg
            db_ref[...] += pb

    row = pl.BlockSpec((tm, D_MODEL), lambda i: (i, 0))
    vec = pl.BlockSpec((1, D_MODEL), lambda i: (0, 0))
    return pl.pallas_call(body, grid=(T // tm,), in_specs=[row, row, vec, vec], out_specs=[row, vec, vec],
                          out_shape=(SDS((T, D_MODEL), f32), SDS((1, D_MODEL), f32), SDS((1, D_MODEL), f32)),
                          name="ln_silu_bwd", compiler_params=_cp(("arbitrary",)))(dc3, c1, g, b)


NN = (((1,), (0,)), ((), ()))
NT = (((1,), (1,)), ((), ()))
TN = (((0,), (0,)), ((), ()))


def _mm(name, a, b, *, grid, a_spec, b_spec, o_spec, o_shape, o_dtype, dims, acc_shape=None, k_axis=None,
        res=None, res_spec=None, sem=None):
    nk = 1 if k_axis is None else grid[k_axis]

    def body(*refs):
        if res is None:
            a_ref, b_ref, o_ref = refs[:3]
            r_ref, scr = None, refs[3:]
        else:
            a_ref, b_ref, r_ref, o_ref = refs[:4]
            scr = refs[4:]
        p = lax.dot_general(a_ref[...].astype(bf16), b_ref[...].astype(bf16), dims, preferred_element_type=f32)
        if nk == 1:
            if r_ref is not None:
                p = p + r_ref[...]
            o_ref[...] = p.astype(o_dtype)
            return
        acc = scr[0]
        k = pl.program_id(k_axis)

        @pl.when(k == 0)
        def _():
            acc[...] = p

        @pl.when(k > 0)
        def _():
            acc[...] += p

        @pl.when(k == nk - 1)
        def _():
            t = acc[...]
            if r_ref is not None:
                t = t + r_ref[...]
            o_ref[...] = t.astype(o_dtype)

    ins = [a, b] + ([] if res is None else [res])
    specs = [a_spec, b_spec] + ([] if res is None else [res_spec])
    scratch = [] if nk == 1 else [pltpu.VMEM(acc_shape, f32)]
    return pl.pallas_call(body, grid=grid, in_specs=specs, out_specs=o_spec, out_shape=SDS(o_shape, o_dtype),
                          scratch_shapes=scratch, name=name, compiler_params=_cp(sem))(*ins)


def _mm_nn_full(name, a, b, o_dtype, res=None, tm=512):
    T, K = a.shape
    N = b.shape[1]
    return _mm(name, a, b, grid=(T // tm,), a_spec=pl.BlockSpec((tm, K), lambda i: (i, 0)),
               b_spec=pl.BlockSpec((K, N), lambda i: (0, 0)), o_spec=pl.BlockSpec((tm, N), lambda i: (i, 0)),
               o_shape=(T, N), o_dtype=o_dtype, dims=NN, res=res,
               res_spec=None if res is None else pl.BlockSpec((tm, N), lambda i: (i, 0)), sem=("parallel",))


def _mm_nt_full(name, a, b, o_dtype, tm=512):
    T, N = a.shape
    K = b.shape[0]
    return _mm(name, a, b, grid=(T // tm,), a_spec=pl.BlockSpec((tm, N), lambda i: (i, 0)),
               b_spec=pl.BlockSpec((K, N), lambda i: (0, 0)), o_spec=pl.BlockSpec((tm, K), lambda i: (i, 0)),
               o_shape=(T, K), o_dtype=o_dtype, dims=NT, sem=("parallel",))


def _mm_tn_tokens(name, a, b, o_dtype, tk=512):
    T, K = a.shape
    N = b.shape[1]
    return _mm(name, a, b, grid=(T // tk,), a_spec=pl.BlockSpec((tk, K), lambda k: (k, 0)),
               b_spec=pl.BlockSpec((tk, N), lambda k: (k, 0)), o_spec=pl.BlockSpec((K, N), lambda k: (0, 0)),
               o_shape=(K, N), o_dtype=o_dtype, dims=TN, acc_shape=(K, N), k_axis=0, sem=("arbitrary",))


def _ffn_up(h2, wg, wu):
    T = h2.shape[0]
    tm = 512

    def body(h_ref, wg_ref, wu_ref, a_ref, b_ref, f_ref):
        hv = h_ref[...]
        av = jnp.dot(hv, wg_ref[...], preferred_element_type=f32)
        bv = jnp.dot(hv, wu_ref[...], preferred_element_type=f32)
        a_ref[...] = av.astype(bf16)
        b_ref[...] = bv.astype(bf16)
        f_ref[...] = (av * _sigmoid(av) * bv).astype(bf16)

    wspec = pl.BlockSpec((None, D_MODEL, FF_S), lambda s, m: (s, 0, 0))
    ospec = pl.BlockSpec((None, tm, FF_S), lambda s, m: (s, m, 0))
    osd = SDS((N_CHIPS, T, FF_S), bf16)
    return pl.pallas_call(body, grid=(N_CHIPS, T // tm),
                          in_specs=[pl.BlockSpec((tm, D_MODEL), lambda s, m: (m, 0)), wspec, wspec],
                          out_specs=[ospec, ospec, ospec], out_shape=(osd, osd, osd), name="ffn_up",
                          compiler_params=_cp(("parallel", "parallel")))(h2, wg, wu)


def _ffn_down_bwd(dx2, wd, a, b):
    T = dx2.shape[0]
    tm = 512

    def body(d_ref, w_ref, a_ref, b_ref, da_ref, db_ref):
        df = lax.dot_general(d_ref[...].astype(bf16), w_ref[...], NT, preferred_element_type=f32)
        av = a_ref[...].astype(f32)
        sg = _sigmoid(av)
        da_ref[...] = (df * b_ref[...].astype(f32) * (sg * (1.0 + av * (1.0 - sg)))).astype(bf16)
        db_ref[...] = (df * av * sg).astype(bf16)

    aspec = pl.BlockSpec((None, tm, FF_S), lambda s, m: (s, m, 0))
    osd = SDS((N_CHIPS, T, FF_S), bf16)
    return pl.pallas_call(body, grid=(N_CHIPS, T // tm),
                          in_specs=[pl.BlockSpec((tm, D_MODEL), lambda s, m: (m, 0)),
                                    pl.BlockSpec((None, FF_S, D_MODEL), lambda s, m: (s, 0, 0)), aspec, aspec],
                          out_specs=[aspec, aspec], out_shape=(osd, osd), name="ffn_down_bwd",
                          compiler_params=_cp(("parallel", "parallel")))(dx2, wd, a, b)


def _ffn_dh2(da, wg, db, wu):
    T = da.shape[1]
    tm = 1024

    def body(da_ref, wg_ref, db_ref, wu_ref, o_ref, acc):
        p = lax.dot_general(da_ref[...], wg_ref[...], NT, preferred_element_type=f32)
        p = p + lax.dot_general(db_ref[...], wu_ref[...], NT, preferred_element_type=f32)
        s = pl.program_id(1)

        @pl.when(s == 0)
        def _():
            acc[...] = p

        @pl.when(s > 0)
        def _():
            acc[...] += p

        @pl.when(s == N_CHIPS - 1)
        def _():
            o_ref[...] = acc[...]

    aspec = pl.BlockSpec((None, tm, FF_S), lambda m, s: (s, m, 0))
    wspec = pl.BlockSpec((None, D_MODEL, FF_S), lambda m, s: (s, 0, 0))
    return pl.pallas_call(body, grid=(T // tm, N_CHIPS), in_specs=[aspec, wspec, aspec, wspec],
                          out_specs=pl.BlockSpec((tm, D_MODEL), lambda m, s: (m, 0)), out_shape=SDS((T, D_MODEL), f32),
                          scratch_shapes=[pltpu.VMEM((tm, D_MODEL), f32)], name="ffn_dh2",
                          compiler_params=_cp(("parallel", "arbitrary")))(da, wg, db, wu)


def _merge_fwd(proj, gate_b, ya, yc):
    T = proj.shape[0]
    tm = 512

    def body(ga_ref, gc_ref, ba_ref, bc_ref, ya_ref, yc_ref, o_ref):
        sa = _sigmoid(ga_ref[...].astype(f32) + ba_ref[...])
        sc = _sigmoid(gc_ref[...].astype(f32) + bc_ref[...])
        o_ref[...] = (sa * ya_ref[...].astype(f32) + sc * yc_ref[...].astype(f32)).astype(bf16)

    blk = pl.BlockSpec((tm, CB), lambda i, j: (i, j))
    return pl.pallas_call(
        body, grid=(T // tm, 2),
        in_specs=[pl.BlockSpec((tm, CB), lambda i, j: (i, GA_CB + j)), pl.BlockSpec((tm, CB), lambda i, j: (i, GC_CB + j)),
                  pl.BlockSpec((1, CB), lambda i, j: (0, j)), pl.BlockSpec((1, CB), lambda i, j: (0, 2 + j)), blk, blk],
        out_specs=blk, out_shape=SDS((T, D_MODEL), bf16), name="merge_fwd",
        compiler_params=_cp(("parallel", "parallel")))(proj, proj, gate_b, gate_b, ya, yc)


def _merge_bwd(dm, proj, gate_b, ya, yc):
    T = proj.shape[0]
    tm = 512

    def body(dm_ref, ga_ref, gc_ref, ba_ref, bc_ref, ya_ref, yc_ref, dya_ref, dyc_ref, dga_ref, dgc_ref, dba_ref, dbc_ref):
        dmv = dm_ref[...].astype(f32)
        sa = _sigmoid(ga_ref[...].astype(f32) + ba_ref[...])
        sc = _sigmoid(gc_ref[...].astype(f32) + bc_ref[...])
        dya_ref[...] = (dmv * sa).astype(bf16)
        dyc_ref[...] = (dmv * sc).astype(bf16)
        dga = dmv * ya_ref[...].astype(f32) * sa * (1.0 - sa)
        dgc = dmv * yc_ref[...].astype(f32) * sc * (1.0 - sc)
        dga_ref[...] = dga.astype(bf16)
        dgc_ref[...] = dgc.astype(bf16)
        pa = jnp.sum(dga, axis=0, keepdims=True)
        pc = jnp.sum(dgc, axis=0, keepdims=True)

        @pl.when(pl.program_id(1) == 0)
        def _():
            dba_ref[...] = pa
            dbc_ref[...] = pc

        @pl.when(pl.program_id(1) > 0)
        def _():
            dba_ref[...] += pa
            dbc_ref[...] += pc

    blk = pl.BlockSpec((tm, CB), lambda j, i: (i, j))
    vec = pl.BlockSpec((1, CB), lambda j, i: (0, j))
    big = SDS((T, D_MODEL), bf16)
    small = SDS((1, D_MODEL), f32)
    return pl.pallas_call(
        body, grid=(2, T // tm),
        in_specs=[blk, pl.BlockSpec((tm, CB), lambda j, i: (i, GA_CB + j)), pl.BlockSpec((tm, CB), lambda j, i: (i, GC_CB + j)),
                  vec, pl.BlockSpec((1, CB), lambda j, i: (0, 2 + j)), blk, blk],
        out_specs=[blk, blk, blk, blk, vec, vec], out_shape=(big, big, big, big, small, small), name="merge_bwd",
        compiler_params=_cp(("parallel", "arbitrary")))(dm, proj, proj, gate_b, gate_b, ya, yc)


CONV_TS = 256
CONV_HALO = 32
CONV_RC = 64


def _glu_conv_fwd(proj3, w, bias):
    B = proj3.shape[0]
    nt = SEQ // CONV_TS
    hb = CONV_TS // CONV_HALO

    def body(ua_ref, ub_ref, ha_ref, hb_ref, w_ref, b_ref, o_ref, win):
        i = pl.program_id(2)
        c0 = ua_ref[...].astype(f32) * _sigmoid(ub_ref[...].astype(f32))
        halo = ha_ref[...].astype(f32) * _sigmoid(hb_ref[...].astype(f32))
        win[0:CONV_HALO, :] = jnp.where(i > 0, halo, 0.0)
        win[CONV_HALO:, :] = c0
        for r0 in range(0, CONV_TS, CONV_RC):
            acc = jnp.zeros((CONV_RC, CB), f32) + b_ref[...]
            for k in range(CONV_K):
                acc = acc + win[pl.ds(r0 + CONV_HALO - (CONV_K - 1) + k, CONV_RC), :] * w_ref[k:k + 1, :]
            o_ref[r0:r0 + CONV_RC, :] = acc

    def cur(cb):
        return pl.BlockSpec((None, CONV_TS, CB), lambda b, j, i: (b, i, cb + j))

    def prev(cb):
        return pl.BlockSpec((None, CONV_HALO, CB), lambda b, j, i: (b, jnp.maximum(i * hb - 1, 0), cb + j))

    return pl.pallas_call(
        body, grid=(B, 2, nt),
        in_specs=[cur(UA_CB), cur(UB_CB), prev(UA_CB), prev(UB_CB),
                  pl.BlockSpec((CONV_K, CB), lambda b, j, i: (0, j)), pl.BlockSpec((1, CB), lambda b, j, i: (0, j))],
        out_specs=pl.BlockSpec((None, CONV_TS, CB), lambda b, j, i: (b, i, j)),
        out_shape=SDS((B, SEQ, D_MODEL), f32), scratch_shapes=[pltpu.VMEM((CONV_TS + CONV_HALO, CB), f32)],
        name="glu_conv_fwd", compiler_params=_cp(("parallel", "parallel", "parallel")))(proj3, proj3, proj3, proj3, w, bias)


def _glu_conv_bwd(dc1, proj3, w):
    B = proj3.shape[0]
    nt = SEQ // CONV_TS
    hb = CONV_TS // CONV_HALO

    def body(d_ref, dn_ref, ua_ref, ub_ref, ha_ref, hb_ref, w_ref, dua_ref, dub_ref, dw_ref, db_ref, winc, wind, accw):
        b = pl.program_id(1)
        i = pl.program_id(2)
        first = jnp.logical_and(b == 0, i == 0)
        last = jnp.logical_and(b == B - 1, i == nt - 1)

        @pl.when(first)
        def _():
            accw[...] = jnp.zeros_like(accw)
            db_ref[...] = jnp.zeros_like(db_ref)

        halo = ha_ref[...].astype(f32) * _sigmoid(hb_ref[...].astype(f32))
        winc[0:CONV_HALO, :] = jnp.where(i > 0, halo, 0.0)
        winc[CONV_HALO:, :] = ua_ref[...].astype(f32) * _sigmoid(ub_ref[...].astype(f32))
        wind[0:CONV_TS, :] = d_ref[...]
        wind[CONV_TS:, :] = jnp.where(i < nt - 1, dn_ref[...], 0.0)
        db_ref[...] += jnp.sum(d_ref[...], axis=0, keepdims=True)
        for r0 in range(0, CONV_TS, CONV_RC):
            dc0 = jnp.zeros((CONV_RC, CB), f32)
            for k in range(CONV_K):
                dc0 = dc0 + wind[pl.ds(r0 + (CONV_K - 1) - k, CONV_RC), :] * w_ref[k:k + 1, :]
            uav = ua_ref[r0:r0 + CONV_RC, :].astype(f32)
            sg = _sigmoid(ub_ref[r0:r0 + CONV_RC, :].astype(f32))
            dua_ref[r0:r0 + CONV_RC, :] = (dc0 * sg).astype(bf16)
            dub_ref[r0:r0 + CONV_RC, :] = (dc0 * uav * sg * (1.0 - sg)).astype(bf16)
            dv = wind[r0:r0 + CONV_RC, :]
            for k in range(CONV_K):
                prod = dv * winc[pl.ds(r0 + CONV_HALO - (CONV_K - 1) + k, CONV_RC), :]
                accw[k] += jnp.sum(prod.reshape(CONV_RC // 8, 8, CB), axis=0)

        @pl.when(last)
        def _():
            for k in range(CONV_K):
                dw_ref[k:k + 1, :] = jnp.sum(accw[k], axis=0, keepdims=True)
            dw_ref[CONV_K:, :] = jnp.zeros((CONV_HALO - CONV_K, CB), f32)

    def cur(cb):
        return pl.BlockSpec((None, CONV_TS, CB), lambda j, b, i: (b, i, cb + j))

    def prev(cb):
        return pl.BlockSpec((None, CONV_HALO, CB), lambda j, b, i: (b, jnp.maximum(i * hb - 1, 0), cb + j))

    nxt = pl.BlockSpec((None, CONV_HALO, CB), lambda j, b, i: (b, jnp.minimum((i + 1) * hb, SEQ // CONV_HALO - 1), j))
    big = SDS((B, SEQ, D_MODEL), bf16)
    return pl.pallas_call(
        body, grid=(2, B, nt),
        in_specs=[cur(0), nxt, cur(UA_CB), cur(UB_CB), prev(UA_CB), prev(UB_CB), pl.BlockSpec((CONV_K, CB), lambda j, b, i: (0, j))],
        out_specs=[cur(0), cur(0), pl.BlockSpec((CONV_HALO, CB), lambda j, b, i: (0, j)), pl.BlockSpec((1, CB), lambda j, b, i: (0, j))],
        out_shape=(big, big, SDS((CONV_HALO, D_MODEL), f32), SDS((1, D_MODEL), f32)),
        scratch_shapes=[pltpu.VMEM((CONV_TS + CONV_HALO, CB), f32), pltpu.VMEM((CONV_TS + CONV_HALO, CB), f32),
                        pltpu.VMEM((CONV_K, 8, CB), f32)],
        name="glu_conv_bwd", compiler_params=_cp(("parallel", "arbitrary", "arbitrary")))(dc1, dc1, proj3, proj3, proj3, proj3, w)


def _band(first, dil):
    kw = Q_BLOCK if first else 2 * Q_BLOCK
    qi = lax.broadcasted_iota(jnp.int32, (Q_BLOCK, kw), 0)
    kj = lax.broadcasted_iota(jnp.int32, (Q_BLOCK, kw), 1)
    rel = qi - kj + (0 if first else Q_BLOCK)
    valid = jnp.logical_and(rel >= 0, rel <= Q_BLOCK)
    return valid, rel.astype(f32) * float(dil)


def _scores(q, k, valid, dist, slope):
    s = lax.dot_general(q, k, NT, preferred_element_type=f32) * (HEAD_DIM ** -0.5) - slope * dist
    return jnp.where(valid, s, NEG)


def _attn_fwd(g, q4, k4, v4, qcb, kcb, vcb):
    _, dil = GROUPS[g]
    B, r, L, _ = q4.shape
    nb = L // Q_BLOCK
    slopes = [float(s) for s in _SLOPES[g]]

    def body(q_ref, k_ref, v_ref, o_ref, lse_ref):
        lane = lax.broadcasted_iota(jnp.int32, (Q_BLOCK, LANES), 1)

        def block(n, first):
            q0 = 0 if first else pl.multiple_of(n * Q_BLOCK, Q_BLOCK)
            k0 = 0 if first else pl.multiple_of((n - 1) * Q_BLOCK, Q_BLOCK)
            kw = Q_BLOCK if first else 2 * Q_BLOCK
            valid, dist = _band(first, dil)
            st = jnp.zeros((Q_BLOCK, LANES), f32)
            for h in range(HEADS):
                cols = slice(h * HEAD_DIM, (h + 1) * HEAD_DIM)
                s = _scores(q_ref[pl.ds(q0, Q_BLOCK), cols], k_ref[pl.ds(k0, kw), cols], valid, dist, slopes[h])
                m = jnp.max(s, axis=-1, keepdims=True)
                p = jnp.exp(s - m)
                den = jnp.sum(p, axis=-1, keepdims=True)
                o = jnp.dot(p.astype(bf16), v_ref[pl.ds(k0, kw), cols], preferred_element_type=f32) / den
                o_ref[pl.ds(q0, Q_BLOCK), cols] = o.astype(bf16)
                st = jnp.where(lane == h, m + jnp.log(den), st)
            lse_ref[pl.ds(q0, Q_BLOCK), :] = st

        block(0, True)
        if nb > 1:
            def step(n, carry):
                block(n, False)
                return carry
            lax.fori_loop(1, nb, step, 0)

    def spec(cb):
        return pl.BlockSpec((None, None, L, GW), lambda b, c: (b, c, 0, cb))

    return pl.pallas_call(
        body, grid=(B, r), in_specs=[spec(qcb), spec(kcb), spec(vcb)],
        out_specs=[spec(0), pl.BlockSpec((None, None, L, LANES), lambda b, c: (b, c, 0, 0))],
        out_shape=(SDS((B, r, L, GW), bf16), SDS((B, r, L, LANES), f32)), name=f"attn_fwd_g{g}",
        compiler_params=_cp(("parallel", "parallel")))(q4, k4, v4)


def _attn_bwd(g, q4, k4, v4, qcb, kcb, vcb, do4, lse4, dl4):
    _, dil = GROUPS[g]
    B, r, L, _ = q4.shape
    nb = L // Q_BLOCK
    slopes = [float(s) for s in _SLOPES[g]]
    scale = HEAD_DIM ** -0.5

    def body(q_ref, k_ref, v_ref, do_ref, lse_ref, dl_ref, dq_ref, dk_ref, dv_ref, dk_acc, dv_acc):
        dk_acc[...] = jnp.zeros_like(dk_acc)
        dv_acc[...] = jnp.zeros_like(dv_acc)

        def block(n, first):
            q0 = 0 if first else pl.multiple_of(n * Q_BLOCK, Q_BLOCK)
            k0 = 0 if first else pl.multiple_of((n - 1) * Q_BLOCK, Q_BLOCK)
            kw = Q_BLOCK if first else 2 * Q_BLOCK
            valid, dist = _band(first, dil)
            for h in range(HEADS):
                cols = slice(h * HEAD_DIM, (h + 1) * HEAD_DIM)
                qv = q_ref[pl.ds(q0, Q_BLOCK), cols]
                kv = k_ref[pl.ds(k0, kw), cols]
                vv = v_ref[pl.ds(k0, kw), cols]
                dov = do_ref[pl.ds(q0, Q_BLOCK), cols]
                s = _scores(qv, kv, valid, dist, slopes[h])
                p = jnp.exp(s - lse_ref[pl.ds(q0, Q_BLOCK), h:h + 1])
                dp = lax.dot_general(dov, vv, NT, preferred_element_type=f32)
                ds = (p * (dp - dl_ref[pl.ds(q0, Q_BLOCK), h:h + 1])).astype(bf16)
                dq_ref[pl.ds(q0, Q_BLOCK), cols] = (jnp.dot(ds, kv, preferred_element_type=f32) * scale).astype(bf16)
                dk_acc[pl.ds(k0, kw), cols] += lax.dot_general(ds, qv, TN, preferred_element_type=f32) * scale
                dv_acc[pl.ds(k0, kw), cols] += lax.dot_general(p.astype(bf16), dov, TN, preferred_element_type=f32)

        block(0, True)
        if nb > 1:
            def step(n, carry):
                block(n, False)
                return carry
            lax.fori_loop(1, nb, step, 0)
        dk_ref[...] = dk_acc[...].astype(bf16)
        dv_ref[...] = dv_acc[...].astype(bf16)

    def spec(cb):
        return pl.BlockSpec((None, None, L, GW), lambda b, c: (b, c, 0, cb))

    st = pl.BlockSpec((None, None, L, LANES), lambda b, c: (b, c, 0, 0))
    osd = SDS((B, r, L, GW), bf16)
    return pl.pallas_call(
        body, grid=(B, r), in_specs=[spec(qcb), spec(kcb), spec(vcb), spec(0), st, st],
        out_specs=[spec(0), spec(0), spec(0)], out_shape=(osd, osd, osd),
        scratch_shapes=[pltpu.VMEM((L, GW), f32), pltpu.VMEM((L, GW), f32)], name=f"attn_bwd_g{g}",
        compiler_params=_cp(("parallel", "parallel")))(q4, k4, v4, do4, lse4, dl4)


def _attn_mix(os_, lses):
    T = os_[0].shape[0]
    tm = 512
    ng = len(os_)

    def body(*refs):
        o_refs, l_refs = refs[:ng], refs[ng:2 * ng]
        y_ref, lt_ref = refs[2 * ng:]
        ls = [l_ref[...] for l_ref in l_refs]
        m = functools.reduce(jnp.maximum, ls)
        ws = [jnp.exp(l - m) for l in ls]
        den = functools.reduce(lambda u, v: u + v, ws)
        alphas = [w / den for w in ws]
        lt_ref[...] = m + jnp.log(den)
        for h in range(HEADS):
            cols = slice(h * HEAD_DIM, (h + 1) * HEAD_DIM)
            acc = alphas[0][:, h:h + 1] * o_refs[0][:, cols].astype(f32)
            for gi in range(1, ng):
                acc = acc + alphas[gi][:, h:h + 1] * o_refs[gi][:, cols].astype(f32)
            y_ref[:, cols] = acc.astype(bf16)

    ob = pl.BlockSpec((tm, GW), lambda i: (i, 0))
    lb = pl.BlockSpec((tm, LANES), lambda i: (i, 0))
    return pl.pallas_call(body, grid=(T // tm,), in_specs=[ob] * ng + [lb] * ng, out_specs=[ob, lb],
                          out_shape=(SDS((T, GW), bf16), SDS((T, LANES), f32)), name="attn_mix",
                          compiler_params=_cp(("parallel",)))(*os_, *lses)


def _attn_delta(dmix, mix):
    T = dmix.shape[0]
    tm = 512

    def body(d_ref, y_ref, o_ref):
        lane = lax.broadcasted_iota(jnp.int32, (tm, LANES), 1)
        acc = jnp.zeros((tm, LANES), f32)
        for h in range(HEADS):
            cols = slice(h * HEAD_DIM, (h + 1) * HEAD_DIM)
            dl = jnp.sum(d_ref[:, cols].astype(f32) * y_ref[:, cols].astype(f32), axis=-1, keepdims=True)
            acc = jnp.where(lane == h, dl, acc)
        o_ref[...] = acc

    ob = pl.BlockSpec((tm, GW), lambda i: (i, 0))
    return pl.pallas_call(body, grid=(T // tm,), in_specs=[ob, ob], out_specs=pl.BlockSpec((tm, LANES), lambda i: (i, 0)),
                          out_shape=SDS((T, LANES), f32), name="attn_delta", compiler_params=_cp(("parallel",)))(dmix, mix)


def _add_pairs(a, b, name):
    n, h, C = a.shape

    def body(a_ref, b_ref, o_ref):
        o_ref[...] = (a_ref[...].astype(f32) + b_ref[...].astype(f32)).astype(bf16)

    blk = pl.BlockSpec((None, h, C), lambda s: (s, 0, 0))
    return pl.pallas_call(body, grid=(n,), in_specs=[blk, blk], out_specs=blk, out_shape=SDS((n, h, C), bf16), name=name,
                          compiler_params=_cp(("parallel",)))(a, b)


def _sum4(q, name):
    _, h, C = q.shape

    def body(q_ref, o_ref):
        t = q_ref[0].astype(f32) + q_ref[1].astype(f32)
        t = t + q_ref[2].astype(f32)
        o_ref[...] = t + q_ref[3].astype(f32)

    return pl.pallas_call(body, grid=(1,), in_specs=[pl.BlockSpec((N_CHIPS, h, C), lambda i: (0, 0, 0))],
                          out_specs=pl.BlockSpec((h, C), lambda i: (0, 0)), out_shape=SDS((h, C), f32), name=name,
                          compiler_params=_cp(("arbitrary",)))(q)


def _adamw(w, g, m, v, name):
    R, C = w.shape
    rt = R
    for cand in (512, 256, 128, 64, 32, 16, 8):
        if R % cand == 0 and cand * C * 4 <= 2 ** 21:
            rt = cand
            break
    c1 = 1.0 / (1.0 - B1 ** STEP)
    c2 = 1.0 / (1.0 - B2 ** STEP)

    def body(w_ref, g_ref, m_ref, v_ref, d_ref, nm_ref, nv_ref):
        gv = g_ref[...]
        nm = B1 * m_ref[...] + (1.0 - B1) * gv
        nv = B2 * v_ref[...] + (1.0 - B2) * (gv * gv)
        nm_ref[...] = nm
        nv_ref[...] = nv
        d_ref[...] = -LR * ((nm * c1) / (jnp.sqrt(nv * c2) + ADAM_EPS) + WD * w_ref[...])

    blk = pl.BlockSpec((rt, C), lambda i: (i, 0))
    sd = SDS((R, C), f32)
    return pl.pallas_call(body, grid=(R // rt,), in_specs=[blk] * 4, out_specs=[blk] * 3, out_shape=(sd, sd, sd), name=name,
                          compiler_params=_cp(("parallel",)))(w, g, m, v)


def _coords():
    return lax.axis_index("x"), lax.axis_index("y"), lax.axis_index("c")


def _other_chips(x, y):
    return [(1 - x, y), (x, 1 - y), (1 - x, 1 - y)]


HBM = pl.BlockSpec(memory_space=pl.ANY)


def _allgather_weights(shards):
    n = len(shards)
    halves = [s.shape[0] // 2 for s in shards]

    def body(*refs):
        ins, outs = refs[:n], refs[n:2 * n]
        loc_sem, send1, recv1, send2, recv2 = refs[2 * n:]
        x, y, c = _coords()
        me = 2 * x + y
        chips = _other_chips(x, y)
        local = []
        for w in range(n):
            cp = pltpu.make_async_copy(ins[w], outs[w].at[me], loc_sem.at[w])
            cp.start()
            local.append(cp)
        sends = []
        for w in range(n):
            mine = pl.ds(c * halves[w], halves[w])
            for j, (px, py) in enumerate(chips):
                cp = pltpu.make_async_remote_copy(src_ref=ins[w].at[mine, :], dst_ref=outs[w].at[me, mine, :],
                                                  send_sem=send1.at[w, j], recv_sem=recv1.at[w, j],
                                                  device_id=(px, py, c), device_id_type=MESH)
                cp.start()
                sends.append(cp)
        for w in range(n):
            mine = pl.ds(c * halves[w], halves[w])
            for j, (px, py) in enumerate(chips):
                blk = outs[w].at[2 * px + py, mine, :]
                pltpu.make_async_remote_copy(src_ref=blk, dst_ref=blk, send_sem=send1.at[w, j], recv_sem=recv1.at[w, j],
                                             device_id=(px, py, c), device_id_type=MESH).wait_recv()
                cp = pltpu.make_async_remote_copy(src_ref=blk, dst_ref=blk, send_sem=send2.at[w, j], recv_sem=recv2.at[w, j],
                                                  device_id=(x, y, 1 - c), device_id_type=MESH)
                cp.start()
                sends.append(cp)
        for w in range(n):
            theirs = pl.ds((1 - c) * halves[w], halves[w])
            for j, (px, py) in enumerate(chips):
                blk = outs[w].at[2 * px + py, theirs, :]
                pltpu.make_async_remote_copy(src_ref=blk, dst_ref=blk, send_sem=send2.at[w, j], recv_sem=recv2.at[w, j],
                                             device_id=(x, y, 1 - c), device_id_type=MESH).wait_recv()
        for cp in sends:
            cp.wait_send()
        for cp in local:
            cp.wait()

    return pl.pallas_call(
        body, in_specs=[HBM] * n, out_specs=[HBM] * n,
        out_shape=tuple(SDS((N_CHIPS,) + s.shape, s.dtype) for s in shards),
        scratch_shapes=[pltpu.SemaphoreType.DMA((n,))] + [pltpu.SemaphoreType.DMA((n, 3))] * 4,
        name="allgather_weights", compiler_params=pltpu.CompilerParams(has_side_effects=True))(*shards)


def _pair_exchange(parts):
    n = len(parts)
    halves = [p.shape[1] // 2 for p in parts]

    def body(*refs):
        ins, mine_o, got_o = refs[:n], refs[n:2 * n], refs[2 * n:3 * n]
        loc_sem, send, recv = refs[3 * n:]
        x, y, c = _coords()
        cps = []
        for w in range(n):
            keep = pl.ds(c * halves[w], halves[w])
            give = pl.ds((1 - c) * halves[w], halves[w])
            lc = pltpu.make_async_copy(ins[w].at[:, keep, :], mine_o[w], loc_sem.at[w])
            lc.start()
            rc = pltpu.make_async_remote_copy(src_ref=ins[w].at[:, give, :], dst_ref=got_o[w], send_sem=send.at[w],
                                              recv_sem=recv.at[w], device_id=(x, y, 1 - c), device_id_type=MESH)
            rc.start()
            cps.append((lc, rc))
        for lc, rc in cps:
            rc.wait()
            lc.wait()

    hs = tuple(SDS((N_CHIPS, p.shape[1] // 2, p.shape[2]), p.dtype) for p in parts)
    return pl.pallas_call(body, in_specs=[HBM] * n, out_specs=[HBM] * (2 * n), out_shape=hs + hs,
                          scratch_shapes=[pltpu.SemaphoreType.DMA((n,))] * 3, name="grad_pair_exchange",
                          compiler_params=pltpu.CompilerParams(has_side_effects=True))(*parts)


def _chip_exchange(sums):
    n = len(sums)

    def body(*refs):
        ins, outs = refs[:n], refs[n:2 * n]
        loc_sem, send, recv = refs[2 * n:]
        x, y, c = _coords()
        me = 2 * x + y
        chips = _other_chips(x, y)
        cps = []
        for w in range(n):
            lc = pltpu.make_async_copy(ins[w].at[me], outs[w].at[0], loc_sem.at[w])
            lc.start()
            cps.append((lc, None))
            for j, (px, py) in enumerate(chips):
                rc = pltpu.make_async_remote_copy(src_ref=ins[w].at[2 * px + py], dst_ref=outs[w].at[1 + j],
                                                  send_sem=send.at[w, j], recv_sem=recv.at[w, j],
                                                  device_id=(px, py, c), device_id_type=MESH)
                rc.start()
                cps.append((None, rc))
        for lc, rc in cps:
            if rc is not None:
                rc.wait()
            else:
                lc.wait()

    return pl.pallas_call(body, in_specs=[HBM] * n, out_specs=[HBM] * n, out_shape=tuple(SDS(s.shape, s.dtype) for s in sums),
                          scratch_shapes=[pltpu.SemaphoreType.DMA((n,)), pltpu.SemaphoreType.DMA((n, 3)), pltpu.SemaphoreType.DMA((n, 3))],
                          name="grad_chip_exchange", compiler_params=pltpu.CompilerParams(has_side_effects=True))(*sums)


def _pair_gather(halves_):
    n = len(halves_)

    def body(*refs):
        ins, outs = refs[:n], refs[n:2 * n]
        loc_sem, send, recv = refs[2 * n:]
        x, y, c = _coords()
        cps = []
        for w in range(n):
            h = halves_[w].shape[0]
            rows = pl.ds(c * h, h)
            lc = pltpu.make_async_copy(ins[w], outs[w].at[rows, :], loc_sem.at[w])
            lc.start()
            rc = pltpu.make_async_remote_copy(src_ref=ins[w], dst_ref=outs[w].at[rows, :], send_sem=send.at[w],
                                              recv_sem=recv.at[w], device_id=(x, y, 1 - c), device_id_type=MESH)
            rc.start()
            cps.append((lc, rc))
        for w, (lc, rc) in enumerate(cps):
            h = halves_[w].shape[0]
            other = outs[w].at[pl.ds((1 - c) * h, h), :]
            pltpu.make_async_remote_copy(src_ref=ins[w], dst_ref=other, send_sem=send.at[w], recv_sem=recv.at[w],
                                         device_id=(x, y, 1 - c), device_id_type=MESH).wait_recv()
            rc.wait_send()
            lc.wait()

    return pl.pallas_call(body, in_specs=[HBM] * n, out_specs=[HBM] * n,
                          out_shape=tuple(SDS((2 * a.shape[0], a.shape[1]), a.dtype) for a in halves_),
                          scratch_shapes=[pltpu.SemaphoreType.DMA((n,))] * 3, name="grad_pair_gather",
                          compiler_params=pltpu.CompilerParams(has_side_effects=True))(*halves_)


def _small_allreduce(vec, name):
    R = vec.shape[0]
    nd = 8

    def body(v_ref, o_ref, buf, send, recv):
        x, y, c = _coords()
        me = 4 * x + 2 * y + c
        buf[me] = v_ref[...]
        cps = []
        for k in range(1, nd):
            kx, ky, kc = (k >> 2) & 1, (k >> 1) & 1, k & 1
            tx = x + kx - 2 * x * kx
            ty = y + ky - 2 * y * ky
            tc = c + kc - 2 * c * kc
            cp = pltpu.make_async_remote_copy(src_ref=v_ref, dst_ref=buf.at[me], send_sem=send.at[k], recv_sem=recv.at[k],
                                              device_id=(tx, ty, tc), device_id_type=MESH)
            cp.start()
            cps.append((cp, 4 * tx + 2 * ty + tc))
        for k, (cp, src) in zip(range(1, nd), cps):
            pltpu.make_async_remote_copy(src_ref=v_ref, dst_ref=buf.at[src], send_sem=send.at[k], recv_sem=recv.at[k],
                                         device_id=(x, y, c), device_id_type=MESH).wait_recv()
        for cp, _ in cps:
            cp.wait_send()
        acc = buf[0]
        for d in range(1, nd):
            acc = acc + buf[d]
        o_ref[...] = acc

    vm = pl.BlockSpec(memory_space=pltpu.VMEM)
    return pl.pallas_call(body, in_specs=[vm], out_specs=vm, out_shape=SDS((R, LANES), f32),
                          scratch_shapes=[pltpu.VMEM((nd, R, LANES), f32), pltpu.SemaphoreType.DMA((nd,)), pltpu.SemaphoreType.DMA((nd,))],
                          name=name, compiler_params=pltpu.CompilerParams(has_side_effects=True))(vec)


def _perm(a, r):
    B, S_, C = a.shape
    return a.reshape(B, S_ // r, r, C).transpose(0, 2, 1, 3)


def _unperm(a):
    B, r, L, C = a.shape
    return a.transpose(0, 2, 1, 3).reshape(B, L * r, C)


def _rows128(a, rows):
    flat = a.reshape(-1)
    return jnp.pad(flat, (0, rows * LANES - flat.shape[0])).reshape(rows, LANES)


def _local_step(x, target, norm1_g, gate_b, conv_w, conv_b, conv_ln_g, conv_ln_b, norm2_g, norm_f_g,
                w_in, w_conv_out, w_attn_out, w_o, w_gate, w_up, w_down):
    B = x.shape[0]
    T = B * SEQ
    xf = x.reshape(T, D_MODEL)
    tf = target.reshape(T, D_MODEL)
    w_conv_out_f = w_conv_out.reshape(D_MODEL, D_MODEL)
    w_o_f = w_o.reshape(D_MODEL, D_MODEL)

    h = _rms_fwd(xf, norm1_g, "rms1_fwd")
    proj = _mm("in_proj", h, w_in, grid=(N_CHIPS, T // 512),
               a_spec=pl.BlockSpec((512, D_MODEL), lambda s, m: (m, 0)),
               b_spec=pl.BlockSpec((None, D_MODEL, IN_S), lambda s, m: (s, 0, 0)),
               o_spec=pl.BlockSpec((512, IN_S), lambda s, m: (m, s)), o_shape=(T, IN_W), o_dtype=bf16, dims=NN,
               sem=("parallel", "parallel"))
    proj3 = proj.reshape(B, SEQ, IN_W)
    proj4 = proj.reshape(B, 1, SEQ, IN_W)

    qkv = [None]
    for g in range(1, len(GROUPS)):
        cols = [proj3[:, :, base + g * GW: base + (g + 1) * GW] for base in (0, ATTN_W, 2 * ATTN_W)]
        qkv.append(_perm(jnp.concatenate(cols, axis=-1), GROUPS[g][1]))

    def qkv_args(g):
        return (proj4, proj4, proj4, 0, 3, 6) if g == 0 else (qkv[g], qkv[g], qkv[g], 0, 1, 2)

    outs, lses = [], []
    for g in range(len(GROUPS)):
        o4, l4 = _attn_fwd(g, *qkv_args(g))
        outs.append(_unperm(o4).reshape(T, GW))
        lses.append(_unperm(l4).reshape(T, LANES))
    mix, lse_tot = _attn_mix(outs, lses)
    y_attn = _mm("attn_out", mix, w_attn_out, grid=(N_CHIPS, T // 512),
                 a_spec=pl.BlockSpec((512, GW), lambda s, m: (m, 0)),
                 b_spec=pl.BlockSpec((None, GW, D_MODEL // N_CHIPS), lambda s, m: (s, 0, 0)),
                 o_spec=pl.BlockSpec((512, D_MODEL // N_CHIPS), lambda s, m: (m, s)), o_shape=(T, D_MODEL), o_dtype=bf16,
                 dims=NN, sem=("parallel", "parallel"))

    c1 = _glu_conv_fwd(proj3, conv_w, conv_b).reshape(T, D_MODEL)
    c3 = _ln_silu_fwd(c1, conv_ln_g, conv_ln_b)
    y_conv = _mm_nn_full("conv_out", c3, w_conv_out_f, bf16)

    merged = _merge_fwd(proj, gate_b, y_attn, y_conv)
    x1 = _mm_nn_full("o_proj", merged, w_o_f, f32, res=xf)

    h2 = _rms_fwd(x1, norm2_g, "rms2_fwd")
    fa, fb, ff = _ffn_up(h2, w_gate, w_up)
    x2 = _mm("ffn_down", ff, w_down, grid=(T // 1024, N_CHIPS),
             a_spec=pl.BlockSpec((None, 1024, FF_S), lambda m, s: (s, m, 0)),
             b_spec=pl.BlockSpec((None, FF_S, D_MODEL), lambda m, s: (s, 0, 0)),
             o_spec=pl.BlockSpec((1024, D_MODEL), lambda m, s: (m, 0)), o_shape=(T, D_MODEL), o_dtype=f32, dims=NN,
             acc_shape=(1024, D_MODEL), k_axis=1, res=x1, res_spec=pl.BlockSpec((1024, D_MODEL), lambda m, s: (m, 0)),
             sem=("parallel", "arbitrary"))

    loss, dx2, d_gf = _final_loss(x2, norm_f_g.reshape(1, D_MODEL), tf)

    d_w_down = _mm("d_w_down", ff, dx2, grid=(N_CHIPS, T // 512),
                   a_spec=pl.BlockSpec((None, 512, FF_S), lambda s, k: (s, k, 0)),
                   b_spec=pl.BlockSpec((512, D_MODEL), lambda s, k: (k, 0)),
                   o_spec=pl.BlockSpec((None, FF_S, D_MODEL), lambda s, k: (s, 0, 0)), o_shape=(N_CHIPS, FF_S, D_MODEL),
                   o_dtype=bf16, dims=TN, acc_shape=(FF_S, D_MODEL), k_axis=1, sem=("parallel", "arbitrary"))
    da, db = _ffn_down_bwd(dx2, w_down, fa, fb)

    def d_w_ff(name, dz):
        return _mm(name, h2, dz, grid=(N_CHIPS, T // 512),
                   a_spec=pl.BlockSpec((512, D_MODEL), lambda s, k: (k, 0)),
                   b_spec=pl.BlockSpec((None, 512, FF_S), lambda s, k: (s, k, 0)),
                   o_spec=pl.BlockSpec((None, D_MODEL, FF_S), lambda s, k: (s, 0, 0)), o_shape=(N_CHIPS, D_MODEL, FF_S),
                   o_dtype=bf16, dims=TN, acc_shape=(D_MODEL, FF_S), k_axis=1, sem=("parallel", "arbitrary"))

    d_w_gate = d_w_ff("d_w_gate", da)
    d_w_up = d_w_ff("d_w_up", db)
    dh2 = _ffn_dh2(da, w_gate, db, w_up)
    dx1, d_g2 = _rms_bwd(dh2, x1, norm2_g, dx2, "rms2_bwd")

    d_w_o = _mm_tn_tokens("d_w_o", merged, dx1, bf16).reshape(N_CHIPS, D_MODEL // N_CHIPS, D_MODEL)
    dmerged = _mm_nt_full("d_merged", dx1, w_o_f, bf16)
    dya, dyc, dga, dgc, d_gba, d_gbc = _merge_bwd(dmerged, proj, gate_b, y_attn, y_conv)

    d_w_conv_out = _mm_tn_tokens("d_w_conv_out", c3, dyc, bf16).reshape(N_CHIPS, D_MODEL // N_CHIPS, D_MODEL)
    dc3 = _mm_nt_full("d_c3", dyc, w_conv_out_f, bf16)
    dc1, d_ln_g, d_ln_b = _ln_silu_bwd(dc3, c1, conv_ln_g, conv_ln_b)
    dua, dub, d_conv_w, d_conv_b = _glu_conv_bwd(dc1.reshape(B, SEQ, D_MODEL), proj3, conv_w)

    d_w_attn_out = _mm("d_w_attn_out", mix, dya, grid=(N_CHIPS, T // 512),
                       a_spec=pl.BlockSpec((512, GW), lambda s, k: (k, 0)),
                       b_spec=pl.BlockSpec((512, D_MODEL // N_CHIPS), lambda s, k: (k, s)),
                       o_spec=pl.BlockSpec((None, GW, D_MODEL // N_CHIPS), lambda s, k: (s, 0, 0)),
                       o_shape=(N_CHIPS, GW, D_MODEL // N_CHIPS), o_dtype=bf16, dims=TN, acc_shape=(GW, D_MODEL // N_CHIPS),
                       k_axis=1, sem=("parallel", "arbitrary"))
    dmix = _mm("d_mix", dya, w_attn_out, grid=(T // 512, N_CHIPS),
               a_spec=pl.BlockSpec((512, D_MODEL // N_CHIPS), lambda m, s: (m, s)),
               b_spec=pl.BlockSpec((None, GW, D_MODEL // N_CHIPS), lambda m, s: (s, 0, 0)),
               o_spec=pl.BlockSpec((512, GW), lambda m, s: (m, 0)), o_shape=(T, GW), o_dtype=bf16, dims=NT,
               acc_shape=(512, GW), k_axis=1, sem=("parallel", "arbitrary"))
    delta = _attn_delta(dmix, mix)
    dmix3 = dmix.reshape(B, SEQ, GW)
    lse3 = lse_tot.reshape(B, SEQ, LANES)
    delta3 = delta.reshape(B, SEQ, LANES)
    dqs, dks, dvs = [], [], []
    for g in range(len(GROUPS)):
        r = GROUPS[g][1]
        dq4, dk4, dv4 = _attn_bwd(g, *qkv_args(g), _perm(dmix3, r), _perm(lse3, r), _perm(delta3, r))
        dqs.append(_unperm(dq4))
        dks.append(_unperm(dk4))
        dvs.append(_unperm(dv4))
    dproj = jnp.concatenate(dqs + dks + dvs + [dua, dub, dga.reshape(B, SEQ, D_MODEL), dgc.reshape(B, SEQ, D_MODEL)],
                            axis=-1).reshape(T, IN_W)

    d_w_in = _mm("d_w_in", h, dproj, grid=(N_CHIPS, T // 512),
                 a_spec=pl.BlockSpec((512, D_MODEL), lambda s, k: (k, 0)),
                 b_spec=pl.BlockSpec((512, IN_S), lambda s, k: (k, s)),
                 o_spec=pl.BlockSpec((None, D_MODEL, IN_S), lambda s, k: (s, 0, 0)), o_shape=(N_CHIPS, D_MODEL, IN_S),
                 o_dtype=bf16, dims=TN, acc_shape=(D_MODEL, IN_S), k_axis=1, sem=("parallel", "arbitrary"))
    dh = _mm("d_h", dproj, w_in, grid=(T // 512, N_CHIPS),
             a_spec=pl.BlockSpec((512, IN_S), lambda m, s: (m, s)),
             b_spec=pl.BlockSpec((None, D_MODEL, IN_S), lambda m, s: (s, 0, 0)),
             o_spec=pl.BlockSpec((512, D_MODEL), lambda m, s: (m, 0)), o_shape=(T, D_MODEL), o_dtype=f32, dims=NT,
             acc_shape=(512, D_MODEL), k_axis=1, sem=("parallel", "arbitrary"))
    dx, d_g1 = _rms_bwd(dh, xf, norm1_g, dx1, "rms1_bwd")

    big = dict(w_in=d_w_in, w_conv_out=d_w_conv_out, w_attn_out=d_w_attn_out, w_o=d_w_o, w_ffn_gate=d_w_gate,
               w_ffn_up=d_w_up, w_ffn_down=d_w_down)
    small = dict(norm1_g=d_g1, gate_b=jnp.concatenate([d_gba, d_gbc], axis=-1), conv_b=d_conv_b, conv_ln_g=d_ln_g,
                 conv_ln_b=d_ln_b, norm2_g=d_g2, norm_f_g=d_gf, conv_w=d_conv_w)
    return loss, dx.reshape(B, SEQ, D_MODEL), big, small


BIG = ("w_in", "w_conv_out", "w_attn_out", "w_o", "w_ffn_gate", "w_ffn_up", "w_ffn_down")
SMALL = ("norm1_g", "gate_b", "conv_b", "conv_ln_g", "conv_ln_b", "norm2_g", "norm_f_g")
SMALL_ROWS = {"norm1_g": 8, "gate_b": 16, "conv_b": 8, "conv_ln_g": 8, "conv_ln_b": 8, "norm2_g": 8, "norm_f_g": 8}
LOSS_ROWS = 8
CONVW_ROWS = 32 * D_MODEL // LANES


def kernel(x, norm1_g, w_in, gate_b, conv_w, conv_b, conv_ln_g, conv_ln_b, w_conv_out, w_attn_out, w_o, norm2_g, w_ffn_gate, w_ffn_up, w_ffn_down, norm_f_g, loss_target, m_norm1_g, m_w_in, m_gate_b, m_conv_w, m_conv_b, m_conv_ln_g, m_conv_ln_b, m_w_conv_out, m_w_attn_out, m_w_o, m_norm2_g, m_w_ffn_gate, m_w_ffn_up, m_w_ffn_down, m_norm_f_g, v_norm1_g, v_w_in, v_gate_b, v_conv_w, v_conv_b, v_conv_ln_g, v_conv_ln_b, v_w_conv_out, v_w_attn_out, v_w_o, v_norm2_g, v_w_ffn_gate, v_w_ffn_up, v_w_ffn_down, v_norm_f_g):
    W = dict(norm1_g=norm1_g, w_in=w_in, gate_b=gate_b, conv_w=conv_w, conv_b=conv_b, conv_ln_g=conv_ln_g, conv_ln_b=conv_ln_b,
             w_conv_out=w_conv_out, w_attn_out=w_attn_out, w_o=w_o, norm2_g=norm2_g, w_ffn_gate=w_ffn_gate, w_ffn_up=w_ffn_up,
             w_ffn_down=w_ffn_down, norm_f_g=norm_f_g)
    M = dict(norm1_g=m_norm1_g, w_in=m_w_in, gate_b=m_gate_b, conv_w=m_conv_w, conv_b=m_conv_b, conv_ln_g=m_conv_ln_g,
             conv_ln_b=m_conv_ln_b, w_conv_out=m_w_conv_out, w_attn_out=m_w_attn_out, w_o=m_w_o, norm2_g=m_norm2_g,
             w_ffn_gate=m_w_ffn_gate, w_ffn_up=m_w_ffn_up, w_ffn_down=m_w_ffn_down, norm_f_g=m_norm_f_g)
    V = dict(norm1_g=v_norm1_g, w_in=v_w_in, gate_b=v_gate_b, conv_w=v_conv_w, conv_b=v_conv_b, conv_ln_g=v_conv_ln_g,
             conv_ln_b=v_conv_ln_b, w_conv_out=v_w_conv_out, w_attn_out=v_w_attn_out, w_o=v_w_o, norm2_g=v_norm2_g,
             w_ffn_gate=v_w_ffn_gate, w_ffn_up=v_w_ffn_up, w_ffn_down=v_w_ffn_down, norm_f_g=v_norm_f_g)
    order = list(W)
    shard2d = {n: W[n].reshape(W[n].shape[-2:]) for n in BIG}
    chip = 2 * lax.axis_index("x") + lax.axis_index("y")

    cw = jnp.zeros((CONV_K, D_MODEL), f32)
    cw = lax.dynamic_update_slice(cw, 0.5 * conv_w.reshape(CONV_K, D_MODEL // N_CHIPS), (0, chip * (D_MODEL // N_CHIPS)))
    conv_w_full = _small_allreduce(_rows128(cw, CONVW_ROWS), "conv_w_gather")[: CONV_K * D_MODEL // LANES].reshape(CONV_K, D_MODEL)

    gathered = _allgather_weights([shard2d[n].astype(bf16) for n in BIG])
    full = dict(zip(BIG, gathered))

    loss, grad_x, gbig, gsmall = _local_step(
        x, loss_target, norm1_g, gate_b, conv_w_full, conv_b, conv_ln_g, conv_ln_b, norm2_g, norm_f_g,
        full["w_in"], full["w_conv_out"], full["w_attn_out"], full["w_o"], full["w_ffn_gate"], full["w_ffn_up"], full["w_ffn_down"])

    parts = [gbig[n] for n in BIG]
    pe = _pair_exchange(parts)
    sums = [_add_pairs(a, b, f"pair_sum_{n}") for n, a, b in zip(BIG, pe[:len(BIG)], pe[len(BIG):])]
    ce = _chip_exchange(sums)
    reduced = _pair_gather([_sum4(q, f"chip_sum_{n}") for n, q in zip(BIG, ce)])
    grads = dict(zip(BIG, reduced))

    pieces = [_rows128(loss, LOSS_ROWS)] + [_rows128(gsmall[n], SMALL_ROWS[n]) for n in SMALL] + [_rows128(gsmall["conv_w"], CONVW_ROWS)]
    tot = _small_allreduce(jnp.concatenate(pieces, axis=0), "small_allreduce")
    loss_out = tot[0, 0]
    row = LOSS_ROWS
    for n in SMALL:
        grads[n] = tot[row: row + W[n].size // LANES].reshape(W[n].shape)
        row += SMALL_ROWS[n]
    dcw = tot[row: row + CONV_K * D_MODEL // LANES].reshape(CONV_K, D_MODEL)
    grads["conv_w"] = lax.dynamic_slice(dcw, (0, chip * (D_MODEL // N_CHIPS)), (CONV_K, D_MODEL // N_CHIPS))

    delta, new_m, new_v = {}, {}, {}
    for n in BIG:
        d, nm, nv = _adamw(shard2d[n], grads[n], M[n].reshape(shard2d[n].shape), V[n].reshape(shard2d[n].shape), f"adamw_{n}")
        delta[n], new_m[n], new_v[n] = d.reshape(W[n].shape), nm.reshape(W[n].shape), nv.reshape(W[n].shape)
        grads[n] = grads[n].reshape(W[n].shape)

    def pack(src):
        return jnp.concatenate([_rows128(src[n], SMALL_ROWS[n]) for n in SMALL], axis=0)

    d, nm, nv = _adamw(pack(W), pack(grads), pack(M), pack(V), "adamw_small")
    row = 0
    for n in SMALL:
        k = W[n].size // LANES
        delta[n], new_m[n], new_v[n] = (t[row: row + k].reshape(W[n].shape) for t in (d, nm, nv))
        row += SMALL_ROWS[n]

    def pad32(a):
        return jnp.pad(a.reshape(CONV_K, D_MODEL // N_CHIPS), ((0, 1), (0, 0)))

    d, nm, nv = _adamw(pad32(conv_w), pad32(grads["conv_w"]), pad32(m_conv_w), pad32(v_conv_w), "adamw_conv_w")
    delta["conv_w"], new_m["conv_w"], new_v["conv_w"] = (t[:CONV_K].reshape(conv_w.shape) for t in (d, nm, nv))
    grads["conv_w"] = grads["conv_w"].reshape(conv_w.shape)

    return (loss_out, grad_x, *[grads[n] for n in order], *[delta[n] for n in order],
            *[new_m[n] for n in order], *[new_v[n] for n in order])
```

```python
import functools
import math

import numpy as np
import jax
import jax.numpy as jnp
from jax import lax
from jax.experimental import pallas as pl
from jax.experimental.pallas import tpu as pltpu

f32 = jnp.float32
bf16 = jnp.bfloat16
SDS = jax.ShapeDtypeStruct
MESH = pl.DeviceIdType.MESH

D_MODEL = 1024
SEQ = 2048
HEAD_DIM = 64
HEADS = 8
GROUPS = ((128, 1), (512, 4), (2048, 16))
GW = HEADS * HEAD_DIM
ATTN_W = len(GROUPS) * GW
Q_BLOCK = 128
CONV_K = 31
D_FF = 2816
IN_W = 3 * ATTN_W + 2 * D_MODEL + 2 * D_MODEL
N_CHIPS = 4
IN_S = IN_W // N_CHIPS
FF_S = D_FF // N_CHIPS
RMS_EPS = 1e-6
LN_EPS = 1e-5
LR, B1, B2, ADAM_EPS, WD, STEP = 0.001, 0.9, 0.999, 1e-08, 0.01, 10
NEG = -1e30
LANES = 128
VMEM_LIMIT = 48 * 2 ** 20
CB = 512
UA_CB, UB_CB, GA_CB, GC_CB = 9, 11, 13, 15


def _alibi_slope_list(n):
    def pow2(m):
        start = 2.0 ** (-8.0 / m)
        return [start ** (i + 1) for i in range(m)]
    if math.log2(n).is_integer():
        return pow2(n)
    c = 2 ** math.floor(math.log2(n))
    return pow2(c) + _alibi_slope_list(2 * c)[0::2][: n - c]


_SLOPES = np.asarray(sorted(_alibi_slope_list(len(GROUPS) * HEADS), reverse=True), dtype=np.float32).reshape(len(GROUPS), HEADS)


def _cp(sem=None, vmem=VMEM_LIMIT):
    return pltpu.CompilerParams(dimension_semantics=sem, vmem_limit_bytes=vmem)


def _sigmoid(x):
    return 1.0 / (1.0 + jnp.exp(-x))


def _rms_fwd(x, g, name):
    T = x.shape[0]
    tm = 512

    def body(x_ref, g_ref, o_ref):
        xv = x_ref[...]
        r = lax.rsqrt(jnp.mean(xv * xv, axis=-1, keepdims=True) + RMS_EPS)
        o_ref[...] = (xv * r * g_ref[...]).astype(o_ref.dtype)

    row = pl.BlockSpec((tm, D_MODEL), lambda i: (i, 0))
    vec = pl.BlockSpec((1, D_MODEL), lambda i: (0, 0))
    return pl.pallas_call(body, grid=(T // tm,), in_specs=[row, vec], out_specs=row,
                          out_shape=SDS((T, D_MODEL), bf16), name=name, compiler_params=_cp(("parallel",)))(x, g)


def _rms_bwd(dy, x, g, dres, name):
    T = x.shape[0]
    tm = 512

    def body(dy_ref, x_ref, g_ref, r_ref, dx_ref, dg_ref):
        xv = x_ref[...]
        r = lax.rsqrt(jnp.mean(xv * xv, axis=-1, keepdims=True) + RMS_EPS)
        xh = xv * r
        dyv = dy_ref[...]
        dxh = dyv * g_ref[...]
        dx_ref[...] = r_ref[...] + r * (dxh - xh * jnp.mean(dxh * xh, axis=-1, keepdims=True))
        part = jnp.sum(dyv * xh, axis=0, keepdims=True)

        @pl.when(pl.program_id(0) == 0)
        def _():
            dg_ref[...] = part

        @pl.when(pl.program_id(0) > 0)
        def _():
            dg_ref[...] += part

    row = pl.BlockSpec((tm, D_MODEL), lambda i: (i, 0))
    vec = pl.BlockSpec((1, D_MODEL), lambda i: (0, 0))
    return pl.pallas_call(body, grid=(T // tm,), in_specs=[row, row, vec, row], out_specs=[row, vec],
                          out_shape=(SDS((T, D_MODEL), f32), SDS((1, D_MODEL), f32)), name=name,
                          compiler_params=_cp(("arbitrary",)))(dy, x, g, dres)


def _final_loss(x2, gf, target):
    T = x2.shape[0]
    tm = 512

    def body(x_ref, g_ref, t_ref, loss_ref, dx_ref, dg_ref):
        xv = x_ref[...]
        gv = g_ref[...]
        r = lax.rsqrt(jnp.mean(xv * xv, axis=-1, keepdims=True) + RMS_EPS)
        xh = xv * r
        e = xh * gv - t_ref[...]
        part_l = jnp.broadcast_to(0.5 * jnp.sum(jnp.mean(e * e, axis=-1, keepdims=True), axis=0, keepdims=True), (1, LANES))
        dy = e * (1.0 / D_MODEL)
        dxh = dy * gv
        dx_ref[...] = r * (dxh - xh * jnp.mean(dxh * xh, axis=-1, keepdims=True))
        part_g = jnp.sum(dy * xh, axis=0, keepdims=True)

        @pl.when(pl.program_id(0) == 0)
        def _():
            loss_ref[...] = part_l
            dg_ref[...] = part_g

        @pl.when(pl.program_id(0) > 0)
        def _():
            loss_ref[...] += part_l
            dg_ref[...] += part_g

    row = pl.BlockSpec((tm, D_MODEL), lambda i: (i, 0))
    vec = pl.BlockSpec((1, D_MODEL), lambda i: (0, 0))
    lvec = pl.BlockSpec((1, LANES), lambda i: (0, 0))
    return pl.pallas_call(body, grid=(T // tm,), in_specs=[row, vec, row], out_specs=[lvec, row, vec],
                          out_shape=(SDS((1, LANES), f32), SDS((T, D_MODEL), f32), SDS((1, D_MODEL), f32)),
                          name="final_loss", compiler_params=_cp(("arbitrary",)))(x2, gf, target)


def _ln_silu_fwd(c1, g, b):
    T = c1.shape[0]
    tm = 512

    def body(c_ref, g_ref, b_ref, o_ref):
        cv = c_ref[...]
        mu = jnp.mean(cv, axis=-1, keepdims=True)
        cc = cv - mu
        var = jnp.mean(cc * cc, axis=-1, keepdims=True)
        c2 = cc * lax.rsqrt(var + LN_EPS) * g_ref[...] + b_ref[...]
        o_ref[...] = (c2 * _sigmoid(c2)).astype(o_ref.dtype)

    row = pl.BlockSpec((tm, D_MODEL), lambda i: (i, 0))
    vec = pl.BlockSpec((1, D_MODEL), lambda i: (0, 0))
    return pl.pallas_call(body, grid=(T // tm,), in_specs=[row, vec, vec], out_specs=row,
                          out_shape=SDS((T, D_MODEL), bf16), name="ln_silu_fwd", compiler_params=_cp(("parallel",)))(c1, g, b)


def _ln_silu_bwd(dc3, c1, g, b):
    T = c1.shape[0]
    tm = 512

    def body(d_ref, c_ref, g_ref, b_ref, dc_ref, dg_ref, db_ref):
        cv = c_ref[...]
        gv = g_ref[...]
        mu = jnp.mean(cv, axis=-1, keepdims=True)
        cc = cv - mu
        var = jnp.mean(cc * cc, axis=-1, keepdims=True)
        rs = lax.rsqrt(var + LN_EPS)
        xh = cc * rs
        c2 = xh * gv + b_ref[...]
        sg = _sigmoid(c2)
        dc2 = d_ref[...].astype(f32) * (sg * (1.0 + c2 * (1.0 - sg)))
        dxh = dc2 * gv
        dc_ref[...] = rs * (dxh - jnp.mean(dxh, axis=-1, keepdims=True) - xh * jnp.mean(dxh * xh, axis=-1, keepdims=True))
        pg = jnp.sum(dc2 * xh, axis=0, keepdims=True)
        pb = jnp.sum(dc2, axis=0, keepdims=True)

        @pl.when(pl.program_id(0) == 0)
        def _():
            dg_ref[...] = pg
            db_ref[...] = pb

        @pl.when(pl.program_id(0) > 0)
        def _():
            dg_ref[...] += pg
            db_ref[...] += pb

    row = pl.BlockSpec((tm, D_MODEL), lambda i: (i, 0))
    vec = pl.BlockSpec((1, D_MODEL), lambda i: (0, 0))
    return pl.pallas_call(body, grid=(T // tm,), in_specs=[row, row, vec, vec], out_specs=[row, vec, vec],
                          out_shape=(SDS((T, D_MODEL), f32), SDS((1, D_MODEL), f32), SDS((1, D_MODEL), f32)),
                          name="ln_silu_bwd", compiler_params=_cp(("arbitrary",)))(dc3, c1, g, b)


NN = (((1,), (0,)), ((), ()))
NT = (((1,), (1,)), ((), ()))
TN = (((0,), (0,)), ((), ()))


def _mm(name, a, b, *, grid, a_spec, b_spec, o_spec, o_shape, o_dtype, dims, acc_shape=None, k_axis=None,
        res=None, res_spec=None, sem=None):
    nk = 1 if k_axis is None else grid[k_axis]

    def body(*refs):
        if res is None:
            a_ref, b_ref, o_ref = refs[:3]
            r_ref, scr = None, refs[3:]
        else:
            a_ref, b_ref, r_ref, o_ref = refs[:4]
            scr = refs[4:]
        p = lax.dot_general(a_ref[...].astype(bf16), b_ref[...].astype(bf16), dims, preferred_element_type=f32)
        if nk == 1:
            if r_ref is not None:
                p = p + r_ref[...]
            o_ref[...] = p.astype(o_dtype)
            return
        acc = scr[0]
        k = pl.program_id(k_axis)

        @pl.when(k == 0)
        def _():
            acc[...] = p

        @pl.when(k > 0)
        def _():
            acc[...] += p

        @pl.when(k == nk - 1)
        def _():
            t = acc[...]
            if r_ref is not None:
                t = t + r_ref[...]
            o_ref[...] = t.astype(o_dtype)

    ins = [a, b] + ([] if res is None else [res])
    specs = [a_spec, b_spec] + ([] if res is None else [res_spec])
    scratch = [] if nk == 1 else [pltpu.VMEM(acc_shape, f32)]
    return pl.pallas_call(body, grid=grid, in_specs=specs, out_specs=o_spec, out_shape=SDS(o_shape, o_dtype),
                          scratch_shapes=scratch, name=name, compiler_params=_cp(sem))(*ins)


def _mm_nn_full(name, a, b, o_dtype, res=None, tm=512):
    T, K = a.shape
    N = b.shape[1]
    return _mm(name, a, b, grid=(T // tm,), a_spec=pl.BlockSpec((tm, K), lambda i: (i, 0)),
               b_spec=pl.BlockSpec((K, N), lambda i: (0, 0)), o_spec=pl.BlockSpec((tm, N), lambda i: (i, 0)),
               o_shape=(T, N), o_dtype=o_dtype, dims=NN, res=res,
               res_spec=None if res is None else pl.BlockSpec((tm, N), lambda i: (i, 0)), sem=("parallel",))


def _mm_nt_full(name, a, b, o_dtype, tm=512):
    T, N = a.shape
    K = b.shape[0]
    return _mm(name, a, b, grid=(T // tm,), a_spec=pl.BlockSpec((tm, N), lambda i: (i, 0)),
               b_spec=pl.BlockSpec((K, N), lambda i: (0, 0)), o_spec=pl.BlockSpec((tm, K), lambda i: (i, 0)),
               o_shape=(T, K), o_dtype=o_dtype, dims=NT, sem=("parallel",))


def _mm_tn_tokens(name, a, b, o_dtype, tk=512):
    T, K = a.shape
    N = b.shape[1]
    return _mm(name, a, b, grid=(T // tk,), a_spec=pl.BlockSpec((tk, K), lambda k: (k, 0)),
               b_spec=pl.BlockSpec((tk, N), lambda k: (k, 0)), o_spec=pl.BlockSpec((K, N), lambda k: (0, 0)),
               o_shape=(K, N), o_dtype=o_dtype, dims=TN, acc_shape=(K, N), k_axis=0, sem=("arbitrary",))


def _ffn_up(h2, wg, wu):
    T = h2.shape[0]
    tm = 512

    def body(h_ref, wg_ref, wu_ref, a_ref, b_ref, f_ref):
        hv = h_ref[...]
        av = jnp.dot(hv, wg_ref[...], preferred_element_type=f32)
        bv = jnp.dot(hv, wu_ref[...], preferred_element_type=f32)
        a_ref[...] = av.astype(bf16)
        b_ref[...] = bv.astype(bf16)
        f_ref[...] = (av * _sigmoid(av) * bv).astype(bf16)

    wspec = pl.BlockSpec((None, D_MODEL, FF_S), lambda s, m: (s, 0, 0))
    ospec = pl.BlockSpec((None, tm, FF_S), lambda s, m: (s, m, 0))
    osd = SDS((N_CHIPS, T, FF_S), bf16)
    return pl.pallas_call(body, grid=(N_CHIPS, T // tm),
                          in_specs=[pl.BlockSpec((tm, D_MODEL), lambda s, m: (m, 0)), wspec, wspec],
                          out_specs=[ospec, ospec, ospec], out_shape=(osd, osd, osd), name="ffn_up",
                          compiler_params=_cp(("parallel", "parallel")))(h2, wg, wu)


def _ffn_down_bwd(dx2, wd, a, b):
    T = dx2.shape[0]
    tm = 512

    def body(d_ref, w_ref, a_ref, b_ref, da_ref, db_ref):
        df = lax.dot_general(d_ref[...].astype(bf16), w_ref[...], NT, preferred_element_type=f32)
        av = a_ref[...].astype(f32)
        sg = _sigmoid(av)
        da_ref[...] = (df * b_ref[...].astype(f32) * (sg * (1.0 + av * (1.0 - sg)))).astype(bf16)
        db_ref[...] = (df * av * sg).astype(bf16)

    aspec = pl.BlockSpec((None, tm, FF_S), lambda s, m: (s, m, 0))
    osd = SDS((N_CHIPS, T, FF_S), bf16)
    return pl.pallas_call(body, grid=(N_CHIPS, T // tm),
                          in_specs=[pl.BlockSpec((tm, D_MODEL), lambda s, m: (m, 0)),
                                    pl.BlockSpec((None, FF_S, D_MODEL), lambda s, m: (s, 0, 0)), aspec, aspec],
                          out_specs=[aspec, aspec], out_shape=(osd, osd), name="ffn_down_bwd",
                          compiler_params=_cp(("parallel", "parallel")))(dx2, wd, a, b)


def _ffn_dh2(da, wg, db, wu):
    T = da.shape[1]
    tm = 1024

    def body(da_ref, wg_ref, db_ref, wu_ref, o_ref, acc):
        p = lax.dot_general(da_ref[...], wg_ref[...], NT, preferred_element_type=f32)
        p = p + lax.dot_general(db_ref[...], wu_ref[...], NT, preferred_element_type=f32)
        s = pl.program_id(1)

        @pl.when(s == 0)
        def _():
            acc[...] = p

        @pl.when(s > 0)
        def _():
            acc[...] += p

        @pl.when(s == N_CHIPS - 1)
        def _():
            o_ref[...] = acc[...]

    aspec = pl.BlockSpec((None, tm, FF_S), lambda m, s: (s, m, 0))
    wspec = pl.BlockSpec((None, D_MODEL, FF_S), lambda m, s: (s, 0, 0))
    return pl.pallas_call(body, grid=(T // tm, N_CHIPS), in_specs=[aspec, wspec, aspec, wspec],
                          out_specs=pl.BlockSpec((tm, D_MODEL), lambda m, s: (m, 0)), out_shape=SDS((T, D_MODEL), f32),
                          scratch_shapes=[pltpu.VMEM((tm, D_MODEL), f32)], name="ffn_dh2",
                          compiler_params=_cp(("parallel", "arbitrary")))(da, wg, db, wu)


def _merge_fwd(proj, gate_b, ya, yc):
    T = proj.shape[0]
    tm = 512

    def body(ga_ref, gc_ref, ba_ref, bc_ref, ya_ref, yc_ref, o_ref):
        sa = _sigmoid(ga_ref[...].astype(f32) + ba_ref[...])
        sc = _sigmoid(gc_ref[...].astype(f32) + bc_ref[...])
        o_ref[...] = (sa * ya_ref[...].astype(f32) + sc * yc_ref[...].astype(f32)).astype(bf16)

    blk = pl.BlockSpec((tm, CB), lambda i, j: (i, j))
    return pl.pallas_call(
        body, grid=(T // tm, 2),
        in_specs=[pl.BlockSpec((tm, CB), lambda i, j: (i, GA_CB + j)), pl.BlockSpec((tm, CB), lambda i, j: (i, GC_CB + j)),
                  pl.BlockSpec((1, CB), lambda i, j: (0, j)), pl.BlockSpec((1, CB), lambda i, j: (0, 2 + j)), blk, blk],
        out_specs=blk, out_shape=SDS((T, D_MODEL), bf16), name="merge_fwd",
        compiler_params=_cp(("parallel", "parallel")))(proj, proj, gate_b, gate_b, ya, yc)


def _merge_bwd(dm, proj, gate_b, ya, yc):
    T = proj.shape[0]
    tm = 512

    def body(dm_ref, ga_ref, gc_ref, ba_ref, bc_ref, ya_ref, yc_ref, dya_ref, dyc_ref, dga_ref, dgc_ref, dba_ref, dbc_ref):
        dmv = dm_ref[...].astype(f32)
        sa = _sigmoid(ga_ref[...].astype(f32) + ba_ref[...])
        sc = _sigmoid(gc_ref[...].astype(f32) + bc_ref[...])
        dya_ref[...] = (dmv * sa).astype(bf16)
        dyc_ref[...] = (dmv * sc).astype(bf16)
        dga = dmv * ya_ref[...].astype(f32) * sa * (1.0 - sa)
        dgc = dmv * yc_ref[...].astype(f32) * sc * (1.0 - sc)
        dga_ref[...] = dga.astype(bf16)
        dgc_ref[...] = dgc.astype(bf16)
        pa = jnp.sum(dga, axis=0, keepdims=True)
        pc = jnp.sum(dgc, axis=0, keepdims=True)

        @pl.when(pl.program_id(1) == 0)
        def _():
            dba_ref[...] = pa
            dbc_ref[...] = pc

        @pl.when(pl.program_id(1) > 0)
        def _():
            dba_ref[...] += pa
            dbc_ref[...] += pc

    blk = pl.BlockSpec((tm, CB), lambda j, i: (i, j))
    vec = pl.BlockSpec((1, CB), lambda j, i: (0, j))
    big = SDS((T, D_MODEL), bf16)
    small = SDS((1, D_MODEL), f32)
    return pl.pallas_call(
        body, grid=(2, T // tm),
        in_specs=[blk, pl.BlockSpec((tm, CB), lambda j, i: (i, GA_CB + j)), pl.BlockSpec((tm, CB), lambda j, i: (i, GC_CB + j)),
                  vec, pl.BlockSpec((1, CB), lambda j, i: (0, 2 + j)), blk, blk],
        out_specs=[blk, blk, blk, blk, vec, vec], out_shape=(big, big, big, big, small, small), name="merge_bwd",
        compiler_params=_cp(("parallel", "arbitrary")))(dm, proj, proj, gate_b, gate_b, ya, yc)


CONV_TS = 256
CONV_HALO = 32
CONV_RC = 64


def _glu_conv_fwd(proj3, w, bias):
    B = proj3.shape[0]
    nt = SEQ // CONV_TS
    hb = CONV_TS // CONV_HALO

    def body(ua_ref, ub_ref, ha_ref, hb_ref, w_ref, b_ref, o_ref, win):
        i = pl.program_id(2)
        c0 = ua_ref[...].astype(f32) * _sigmoid(ub_ref[...].astype(f32))
        halo = ha_ref[...].astype(f32) * _sigmoid(hb_ref[...].astype(f32))
        win[0:CONV_HALO, :] = jnp.where(i > 0, halo, 0.0)
        win[CONV_HALO:, :] = c0
        for r0 in range(0, CONV_TS, CONV_RC):
            acc = jnp.zeros((CONV_RC, CB), f32) + b_ref[...]
            for k in range(CONV_K):
                acc = acc + win[pl.ds(r0 + CONV_HALO - (CONV_K - 1) + k, CONV_RC), :] * w_ref[k:k + 1, :]
            o_ref[r0:r0 + CONV_RC, :] = acc

    def cur(cb):
        return pl.BlockSpec((None, CONV_TS, CB), lambda b, j, i: (b, i, cb + j))

    def prev(cb):
        return pl.BlockSpec((None, CONV_HALO, CB), lambda b, j, i: (b, jnp.maximum(i * hb - 1, 0), cb + j))

    return pl.pallas_call(
        body, grid=(B, 2, nt),
        in_specs=[cur(UA_CB), cur(UB_CB), prev(UA_CB), prev(UB_CB),
                  pl.BlockSpec((CONV_K, CB), lambda b, j, i: (0, j)), pl.BlockSpec((1, CB), lambda b, j, i: (0, j))],
        out_specs=pl.BlockSpec((None, CONV_TS, CB), lambda b, j, i: (b, i, j)),
        out_shape=SDS((B, SEQ, D_MODEL), f32), scratch_shapes=[pltpu.VMEM((CONV_TS + CONV_HALO, CB), f32)],
        name="glu_conv_fwd", compiler_params=_cp(("parallel", "parallel", "parallel")))(proj3, proj3, proj3, proj3, w, bias)


def _glu_conv_bwd(dc1, proj3, w):
    B = proj3.shape[0]
    nt = SEQ // CONV_TS
    hb = CONV_TS // CONV_HALO

    def body(d_ref, dn_ref, ua_ref, ub_ref, ha_ref, hb_ref, w_ref, dua_ref, dub_ref, dw_ref, db_ref, winc, wind, accw):
        b = pl.program_id(1)
        i = pl.program_id(2)
        first = jnp.logical_and(b == 0, i == 0)
        last = jnp.logical_and(b == B - 1, i == nt - 1)

        @pl.when(first)
        def _():
            accw[...] = jnp.zeros_like(accw)
            db_ref[...] = jnp.zeros_like(db_ref)

        halo = ha_ref[...].astype(f32) * _sigmoid(hb_ref[...].astype(f32))
        winc[0:CONV_HALO, :] = jnp.where(i > 0, halo, 0.0)
        winc[CONV_HALO:, :] = ua_ref[...].astype(f32) * _sigmoid(ub_ref[...].astype(f32))
        wind[0:CONV_TS, :] = d_ref[...]
        wind[CONV_TS:, :] = jnp.where(i < nt - 1, dn_ref[...], 0.0)
        db_ref[...] += jnp.sum(d_ref[...], axis=0, keepdims=True)
        for r0 in range(0, CONV_TS, CONV_RC):
            dc0 = jnp.zeros((CONV_RC, CB), f32)
            for k in range(CONV_K):
                dc0 = dc0 + wind[pl.ds(r0 + (CONV_K - 1) - k, CONV_RC), :] * w_ref[k:k + 1, :]
            uav = ua_ref[r0:r0 + CONV_RC, :].astype(f32)
            sg = _sigmoid(ub_ref[r0:r0 + CONV_RC, :].astype(f32))
            dua_ref[r0:r0 + CONV_RC, :] = (dc0 * sg).astype(bf16)
            dub_ref[r0:r0 + CONV_RC, :] = (dc0 * uav * sg * (1.0 - sg)).astype(bf16)
            dv = wind[r0:r0 + CONV_RC, :]
            for k in range(CONV_K):
                prod = dv * winc[pl.ds(r0 + CONV_HALO - (CONV_K - 1) + k, CONV_RC), :]
                accw[k] += jnp.sum(prod.reshape(CONV_RC // 8, 8, CB), axis=0)

        @pl.when(last)
        def _():
            for k in range(CONV_K):
                dw_ref[k:k + 1, :] = jnp.sum(accw[k], axis=0, keepdims=True)
            dw_ref[CONV_K:, :] = jnp.zeros((CONV_HALO - CONV_K, CB), f32)

    def cur(cb):
        return pl.BlockSpec((None, CONV_TS, CB), lambda j, b, i: (b, i, cb + j))

    def prev(cb):
        return pl.BlockSpec((None, CONV_HALO, CB), lambda j, b, i: (b, jnp.maximum(i * hb - 1, 0), cb + j))

    nxt = pl.BlockSpec((None, CONV_HALO, CB), lambda j, b, i: (b, jnp.minimum((i + 1) * hb, SEQ // CONV_HALO - 1), j))
    big = SDS((B, SEQ, D_MODEL), bf16)
    return pl.pallas_call(
        body, grid=(2, B, nt),
        in_specs=[cur(0), nxt, cur(UA_CB), cur(UB_CB), prev(UA_CB), prev(UB_CB), pl.BlockSpec((CONV_K, CB), lambda j, b, i: (0, j))],
        out_specs=[cur(0), cur(0), pl.BlockSpec((CONV_HALO, CB), lambda j, b, i: (0, j)), pl.BlockSpec((1, CB), lambda j, b, i: (0, j))],
        out_shape=(big, big, SDS((CONV_HALO, D_MODEL), f32), SDS((1, D_MODEL), f32)),
        scratch_shapes=[pltpu.VMEM((CONV_TS + CONV_HALO, CB), f32), pltpu.VMEM((CONV_TS + CONV_HALO, CB), f32),
                        pltpu.VMEM((CONV_K, 8, CB), f32)],
        name="glu_conv_bwd", compiler_params=_cp(("parallel", "arbitrary", "arbitrary")))(dc1, dc1, proj3, proj3, proj3, proj3, w)


def _band(first, dil):
    kw = Q_BLOCK if first else 2 * Q_BLOCK
    qi = lax.broadcasted_iota(jnp.int32, (Q_BLOCK, kw), 0)
    kj = lax.broadcasted_iota(jnp.int32, (Q_BLOCK, kw), 1)
    rel = qi - kj + (0 if first else Q_BLOCK)
    valid = jnp.logical_and(rel >= 0, rel <= Q_BLOCK)
    return valid, rel.astype(f32) * float(dil)


def _scores(q, k, valid, dist, slope):
    s = lax.dot_general(q, k, NT, preferred_element_type=f32) * (HEAD_DIM ** -0.5) - slope * dist
    return jnp.where(valid, s, NEG)


def _attn_fwd(g, q4, k4, v4, qcb, kcb, vcb):
    _, dil = GROUPS[g]
    B, r, L, _ = q4.shape
    nb = L // Q_BLOCK
    slopes = [float(s) for s in _SLOPES[g]]

    def body(q_ref, k_ref, v_ref, o_ref, lse_ref):
        lane = lax.broadcasted_iota(jnp.int32, (Q_BLOCK, LANES), 1)

        def block(n, first):
            q0 = 0 if first else pl.multiple_of(n * Q_BLOCK, Q_BLOCK)
            k0 = 0 if first else pl.multiple_of((n - 1) * Q_BLOCK, Q_BLOCK)
            kw = Q_BLOCK if first else 2 * Q_BLOCK
            valid, dist = _band(first, dil)
            st = jnp.zeros((Q_BLOCK, LANES), f32)
            for h in range(HEADS):
                cols = slice(h * HEAD_DIM, (h + 1) * HEAD_DIM)
                s = _scores(q_ref[pl.ds(q0, Q_BLOCK), cols], k_ref[pl.ds(k0, kw), cols], valid, dist, slopes[h])
                m = jnp.max(s, axis=-1, keepdims=True)
                p = jnp.exp(s - m)
                den = jnp.sum(p, axis=-1, keepdims=True)
                o = jnp.dot(p.astype(bf16), v_ref[pl.ds(k0, kw), cols], preferred_element_type=f32) / den
                o_ref[pl.ds(q0, Q_BLOCK), cols] = o.astype(bf16)
                st = jnp.where(lane == h, m + jnp.log(den), st)
            lse_ref[pl.ds(q0, Q_BLOCK), :] = st

        block(0, True)
        if nb > 1:
            def step(n, carry):
                block(n, False)
                return carry
            lax.fori_loop(1, nb, step, 0)

    def spec(cb):
        return pl.BlockSpec((None, None, L, GW), lambda b, c: (b, c, 0, cb))

    return pl.pallas_call(
        body, grid=(B, r), in_specs=[spec(qcb), spec(kcb), spec(vcb)],
        out_specs=[spec(0), pl.BlockSpec((None, None, L, LANES), lambda b, c: (b, c, 0, 0))],
        out_shape=(SDS((B, r, L, GW), bf16), SDS((B, r, L, LANES), f32)), name=f"attn_fwd_g{g}",
        compiler_params=_cp(("parallel", "parallel")))(q4, k4, v4)


def _attn_bwd(g, q4, k4, v4, qcb, kcb, vcb, do4, lse4, dl4):
    _, dil = GROUPS[g]
    B, r, L, _ = q4.shape
    nb = L // Q_BLOCK
    slopes = [float(s) for s in _SLOPES[g]]
    scale = HEAD_DIM ** -0.5

    def body(q_ref, k_ref, v_ref, do_ref, lse_ref, dl_ref, dq_ref, dk_ref, dv_ref, dk_acc, dv_acc):
        dk_acc[...] = jnp.zeros_like(dk_acc)
        dv_acc[...] = jnp.zeros_like(dv_acc)

        def block(n, first):
            q0 = 0 if first else pl.multiple_of(n * Q_BLOCK, Q_BLOCK)
            k0 = 0 if first else pl.multiple_of((n - 1) * Q_BLOCK, Q_BLOCK)
            kw = Q_BLOCK if first else 2 * Q_BLOCK
            valid, dist = _band(first, dil)
            for h in range(HEADS):
                cols = slice(h * HEAD_DIM, (h + 1) * HEAD_DIM)
                qv = q_ref[pl.ds(q0, Q_BLOCK), cols]
                kv = k_ref[pl.ds(k0, kw), cols]
                vv = v_ref[pl.ds(k0, kw), cols]
                dov = do_ref[pl.ds(q0, Q_BLOCK), cols]
                s = _scores(qv, kv, valid, dist, slopes[h])
                p = jnp.exp(s - lse_ref[pl.ds(q0, Q_BLOCK), h:h + 1])
                dp = lax.dot_general(dov, vv, NT, preferred_element_type=f32)
                ds = (p * (dp - dl_ref[pl.ds(q0, Q_BLOCK), h:h + 1])).astype(bf16)
                dq_ref[pl.ds(q0, Q_BLOCK), cols] = (jnp.dot(ds, kv, preferred_element_type=f32) * scale).astype(bf16)
                dk_acc[pl.ds(k0, kw), cols] += lax.dot_general(ds, qv, TN, preferred_element_type=f32) * scale
                dv_acc[pl.ds(k0, kw), cols] += lax.dot_general(p.astype(bf16), dov, TN, preferred_element_type=f32)

        block(0, True)
        if nb > 1:
            def step(n, carry):
                block(n, False)
                return carry
            lax.fori_loop(1, nb, step, 0)
        dk_ref[...] = dk_acc[...].astype(bf16)
        dv_ref[...] = dv_acc[...].astype(bf16)

    def spec(cb):
        return pl.BlockSpec((None, None, L, GW), lambda b, c: (b, c, 0, cb))

    st = pl.BlockSpec((None, None, L, LANES), lambda b, c: (b, c, 0, 0))
    osd = SDS((B, r, L, GW), bf16)
    return pl.pallas_call(
        body, grid=(B, r), in_specs=[spec(qcb), spec(kcb), spec(vcb), spec(0), st, st],
        out_specs=[spec(0), spec(0), spec(0)], out_shape=(osd, osd, osd),
        scratch_shapes=[pltpu.VMEM((L, GW), f32), pltpu.VMEM((L, GW), f32)], name=f"attn_bwd_g{g}",
        compiler_params=_cp(("parallel", "parallel")))(q4, k4, v4, do4, lse4, dl4)


def _attn_mix(os_, lses):
    T = os_[0].shape[0]
    tm = 512
    ng = len(os_)

    def body(*refs):
        o_refs, l_refs = refs[:ng], refs[ng:2 * ng]
        y_ref, lt_ref = refs[2 * ng:]
        ls = [l_ref[...] for l_ref in l_refs]
        m = functools.reduce(jnp.maximum, ls)
        ws = [jnp.exp(l - m) for l in ls]
        den = functools.reduce(lambda u, v: u + v, ws)
        alphas = [w / den for w in ws]
        lt_ref[...] = m + jnp.log(den)
        for h in range(HEADS):
            cols = slice(h * HEAD_DIM, (h + 1) * HEAD_DIM)
            acc = alphas[0][:, h:h + 1] * o_refs[0][:, cols].astype(f32)
            for gi in range(1, ng):
                acc = acc + alphas[gi][:, h:h + 1] * o_refs[gi][:, cols].astype(f32)
            y_ref[:, cols] = acc.astype(bf16)

    ob = pl.BlockSpec((tm, GW), lambda i: (i, 0))
    lb = pl.BlockSpec((tm, LANES), lambda i: (i, 0))
    return pl.pallas_call(body, grid=(T // tm,), in_specs=[ob] * ng + [lb] * ng, out_specs=[ob, lb],
                          out_shape=(SDS((T, GW), bf16), SDS((T, LANES), f32)), name="attn_mix",
                          compiler_params=_cp(("parallel",)))(*os_, *lses)


def _attn_delta(dmix, mix):
    T = dmix.shape[0]
    tm = 512

    def body(d_ref, y_ref, o_ref):
        lane = lax.broadcasted_iota(jnp.int32, (tm, LANES), 1)
        acc = jnp.zeros((tm, LANES), f32)
        for h in range(HEADS):
            cols = slice(h * HEAD_DIM, (h + 1) * HEAD_DIM)
            dl = jnp.sum(d_ref[:, cols].astype(f32) * y_ref[:, cols].astype(f32), axis=-1, keepdims=True)
            acc = jnp.where(lane == h, dl, acc)
        o_ref[...] = acc

    ob = pl.BlockSpec((tm, GW), lambda i: (i, 0))
    return pl.pallas_call(body, grid=(T // tm,), in_specs=[ob, ob], out_specs=pl.BlockSpec((tm, LANES), lambda i: (i, 0)),
                          out_shape=SDS((T, LANES), f32), name="attn_delta", compiler_params=_cp(("parallel",)))(dmix, mix)


def _add_pairs(part, got, core, name):
    n, h, C = got.shape

    def body(c_ref, a_ref, b_ref, o_ref):
        o_ref[...] = (a_ref[...].astype(f32) + b_ref[...].astype(f32)).astype(bf16)

    blk = pl.BlockSpec((None, h, C), lambda s, c_ref: (s, 0, 0))
    own = pl.BlockSpec((None, h, C), lambda s, c_ref: (s, c_ref[0], 0))
    spec = pltpu.PrefetchScalarGridSpec(num_scalar_prefetch=1, grid=(n,), in_specs=[own, blk], out_specs=blk)
    return pl.pallas_call(body, grid_spec=spec, out_shape=SDS((n, h, C), bf16), name=name,
                          compiler_params=_cp(("parallel",)))(core, part, got)


def _sum4(sums, got, chip, core, name):
    _, h, C = sums.shape

    def body(s_ref, c_ref, own_ref, q_ref, o_ref):
        t = own_ref[...].astype(f32) + q_ref[0].astype(f32)
        t = t + q_ref[1].astype(f32)
        o_ref[...] = t + q_ref[2].astype(f32)

    spec = pltpu.PrefetchScalarGridSpec(
        num_scalar_prefetch=2, grid=(1,),
        in_specs=[pl.BlockSpec((None, h, C), lambda i, s_ref, c_ref: (s_ref[0], 0, 0)),
                  pl.BlockSpec((N_CHIPS - 1, h, C), lambda i, s_ref, c_ref: (0, 0, 0))],
        out_specs=pl.BlockSpec((h, C), lambda i, s_ref, c_ref: (c_ref[0], 0)))
    return pl.pallas_call(body, grid_spec=spec, out_shape=SDS((2 * h, C), f32), name=name,
                          compiler_params=_cp(("arbitrary",)))(chip, core, sums, got)


def _adamw(w, g, m, v, name):
    R, C = w.shape
    rt = R
    for cand in (512, 256, 128, 64, 32, 16, 8):
        if R % cand == 0 and cand * C * 4 <= 2 ** 21:
            rt = cand
            break
    c1 = 1.0 / (1.0 - B1 ** STEP)
    c2 = 1.0 / (1.0 - B2 ** STEP)

    def body(w_ref, g_ref, m_ref, v_ref, d_ref, nm_ref, nv_ref):
        gv = g_ref[...]
        nm = B1 * m_ref[...] + (1.0 - B1) * gv
        nv = B2 * v_ref[...] + (1.0 - B2) * (gv * gv)
        nm_ref[...] = nm
        nv_ref[...] = nv
        d_ref[...] = -LR * ((nm * c1) / (jnp.sqrt(nv * c2) + ADAM_EPS) + WD * w_ref[...])

    blk = pl.BlockSpec((rt, C), lambda i: (i, 0))
    sd = SDS((R, C), f32)
    return pl.pallas_call(body, grid=(R // rt,), in_specs=[blk] * 4, out_specs=[blk] * 3, out_shape=(sd, sd, sd), name=name,
                          compiler_params=_cp(("parallel",)))(w, g, m, v)


def _coords():
    return lax.axis_index("x"), lax.axis_index("y"), lax.axis_index("c")


def _other_chips(x, y):
    return [(1 - x, y), (x, 1 - y), (1 - x, 1 - y)]


HBM = pl.BlockSpec(memory_space=pl.ANY)


def _allgather_weights(shards, slots):
    n = len(shards)
    halves = [s.shape[0] // 2 for s in shards]

    def body(*refs):
        ins, outs = refs[:n], refs[2 * n:3 * n]
        send1, recv1, send2, recv2 = refs[3 * n:]
        x, y, c = _coords()
        me = 2 * x + y
        chips = _other_chips(x, y)
        sends = []
        for w in range(n):
            mine = pl.ds(c * halves[w], halves[w])
            for j, (px, py) in enumerate(chips):
                cp = pltpu.make_async_remote_copy(src_ref=ins[w].at[mine, :], dst_ref=outs[w].at[me, mine, :],
                                                  send_sem=send1.at[w, j], recv_sem=recv1.at[w, j],
                                                  device_id=(px, py, c), device_id_type=MESH)
                cp.start()
                sends.append(cp)
        for w in range(n):
            mine = pl.ds(c * halves[w], halves[w])
            for j, (px, py) in enumerate(chips):
                blk = outs[w].at[2 * px + py, mine, :]
                pltpu.make_async_remote_copy(src_ref=blk, dst_ref=blk, send_sem=send1.at[w, j], recv_sem=recv1.at[w, j],
                                             device_id=(px, py, c), device_id_type=MESH).wait_recv()
                cp = pltpu.make_async_remote_copy(src_ref=blk, dst_ref=blk, send_sem=send2.at[w, j], recv_sem=recv2.at[w, j],
                                                  device_id=(x, y, 1 - c), device_id_type=MESH)
                cp.start()
                sends.append(cp)
        for w in range(n):
            theirs = pl.ds((1 - c) * halves[w], halves[w])
            for j, (px, py) in enumerate(chips):
                blk = outs[w].at[2 * px + py, theirs, :]
                pltpu.make_async_remote_copy(src_ref=blk, dst_ref=blk, send_sem=send2.at[w, j], recv_sem=recv2.at[w, j],
                                             device_id=(x, y, 1 - c), device_id_type=MESH).wait_recv()
        for cp in sends:
            cp.wait_send()

    return pl.pallas_call(
        body, in_specs=[HBM] * (2 * n), out_specs=[HBM] * n,
        out_shape=tuple(SDS((N_CHIPS,) + s.shape, s.dtype) for s in shards),
        input_output_aliases={n + w: w for w in range(n)},
        scratch_shapes=[pltpu.SemaphoreType.DMA((n, 3))] * 4,
        name="allgather_weights", compiler_params=pltpu.CompilerParams(has_side_effects=True))(*shards, *slots)


def _pair_exchange(parts):
    n = len(parts)
    halves = [p.shape[1] // 2 for p in parts]

    def body(*refs):
        ins, got_o = refs[:n], refs[n:2 * n]
        send, recv = refs[2 * n:]
        x, y, c = _coords()
        cps = []
        for w in range(n):
            give = pl.ds((1 - c) * halves[w], halves[w])
            rc = pltpu.make_async_remote_copy(src_ref=ins[w].at[:, give, :], dst_ref=got_o[w], send_sem=send.at[w],
                                              recv_sem=recv.at[w], device_id=(x, y, 1 - c), device_id_type=MESH)
            rc.start()
            cps.append(rc)
        for rc in cps:
            rc.wait()

    hs = tuple(SDS((N_CHIPS, p.shape[1] // 2, p.shape[2]), p.dtype) for p in parts)
    return pl.pallas_call(body, in_specs=[HBM] * n, out_specs=[HBM] * n, out_shape=hs,
                          scratch_shapes=[pltpu.SemaphoreType.DMA((n,))] * 2, name="grad_pair_exchange",
                          compiler_params=pltpu.CompilerParams(has_side_effects=True))(*parts)


def _chip_exchange(sums):
    n = len(sums)

    def body(*refs):
        ins, outs = refs[:n], refs[n:2 * n]
        send, recv = refs[2 * n:]
        x, y, c = _coords()
        chips = _other_chips(x, y)
        cps = []
        for w in range(n):
            for j, (px, py) in enumerate(chips):
                rc = pltpu.make_async_remote_copy(src_ref=ins[w].at[2 * px + py], dst_ref=outs[w].at[j],
                                                  send_sem=send.at[w, j], recv_sem=recv.at[w, j],
                                                  device_id=(px, py, c), device_id_type=MESH)
                rc.start()
                cps.append(rc)
        for rc in cps:
            rc.wait()

    return pl.pallas_call(body, in_specs=[HBM] * n, out_specs=[HBM] * n,
                          out_shape=tuple(SDS((N_CHIPS - 1,) + s.shape[1:], s.dtype) for s in sums),
                          scratch_shapes=[pltpu.SemaphoreType.DMA((n, 3)), pltpu.SemaphoreType.DMA((n, 3))],
                          name="grad_chip_exchange", compiler_params=pltpu.CompilerParams(has_side_effects=True))(*sums)


def _pair_gather(bufs):
    n = len(bufs)

    def body(*refs):
        outs = refs[n:2 * n]
        send, recv = refs[2 * n:]
        x, y, c = _coords()
        cps = []
        for w in range(n):
            h = bufs[w].shape[0] // 2
            rows = outs[w].at[pl.ds(c * h, h), :]
            rc = pltpu.make_async_remote_copy(src_ref=rows, dst_ref=rows, send_sem=send.at[w], recv_sem=recv.at[w],
                                              device_id=(x, y, 1 - c), device_id_type=MESH)
            rc.start()
            cps.append(rc)
        for w, rc in enumerate(cps):
            h = bufs[w].shape[0] // 2
            other = outs[w].at[pl.ds((1 - c) * h, h), :]
            pltpu.make_async_remote_copy(src_ref=other, dst_ref=other, send_sem=send.at[w], recv_sem=recv.at[w],
                                         device_id=(x, y, 1 - c), device_id_type=MESH).wait_recv()
            rc.wait_send()

    return pl.pallas_call(body, in_specs=[HBM] * n, out_specs=[HBM] * n, out_shape=tuple(SDS(a.shape, a.dtype) for a in bufs),
                          input_output_aliases={w: w for w in range(n)},
                          scratch_shapes=[pltpu.SemaphoreType.DMA((n,))] * 2, name="grad_pair_gather",
                          compiler_params=pltpu.CompilerParams(has_side_effects=True))(*bufs)


def _small_allreduce(vec, name):
    R = vec.shape[0]
    nd = 8

    def body(v_ref, o_ref, buf, send, recv):
        x, y, c = _coords()
        me = 4 * x + 2 * y + c
        buf[me] = v_ref[...]
        cps = []
        for k in range(1, nd):
            kx, ky, kc = (k >> 2) & 1, (k >> 1) & 1, k & 1
            tx = x + kx - 2 * x * kx
            ty = y + ky - 2 * y * ky
            tc = c + kc - 2 * c * kc
            cp = pltpu.make_async_remote_copy(src_ref=v_ref, dst_ref=buf.at[me], send_sem=send.at[k], recv_sem=recv.at[k],
                                              device_id=(tx, ty, tc), device_id_type=MESH)
            cp.start()
            cps.append((cp, 4 * tx + 2 * ty + tc))
        for k, (cp, src) in zip(range(1, nd), cps):
            pltpu.make_async_remote_copy(src_ref=v_ref, dst_ref=buf.at[src], send_sem=send.at[k], recv_sem=recv.at[k],
                                         device_id=(x, y, c), device_id_type=MESH).wait_recv()
        for cp, _ in cps:
            cp.wait_send()
        acc = buf[0]
        for d in range(1, nd):
            acc = acc + buf[d]
        o_ref[...] = acc

    vm = pl.BlockSpec(memory_space=pltpu.VMEM)
    return pl.pallas_call(body, in_specs=[vm], out_specs=vm, out_shape=SDS((R, LANES), f32),
                          scratch_shapes=[pltpu.VMEM((nd, R, LANES), f32), pltpu.SemaphoreType.DMA((nd,)), pltpu.SemaphoreType.DMA((nd,))],
                          name=name, compiler_params=pltpu.CompilerParams(has_side_effects=True))(vec)


def _perm(a, r):
    B, S_, C = a.shape
    return a.reshape(B, S_ // r, r, C).transpose(0, 2, 1, 3)


def _unperm(a):
    B, r, L, C = a.shape
    return a.transpose(0, 2, 1, 3).reshape(B, L * r, C)


def _rows128(a, rows):
    flat = a.reshape(-1)
    return jnp.pad(flat, (0, rows * LANES - flat.shape[0])).reshape(rows, LANES)


def _local_step(x, target, norm1_g, gate_b, conv_w, conv_b, conv_ln_g, conv_ln_b, norm2_g, norm_f_g,
                w_in, w_conv_out, w_attn_out, w_o, w_gate, w_up, w_down):
    B = x.shape[0]
    T = B * SEQ
    xf = x.reshape(T, D_MODEL)
    tf = target.reshape(T, D_MODEL)
    w_conv_out_f = w_conv_out.reshape(D_MODEL, D_MODEL)
    w_o_f = w_o.reshape(D_MODEL, D_MODEL)

    h = _rms_fwd(xf, norm1_g, "rms1_fwd")
    proj = _mm("in_proj", h, w_in, grid=(N_CHIPS, T // 512),
               a_spec=pl.BlockSpec((512, D_MODEL), lambda s, m: (m, 0)),
               b_spec=pl.BlockSpec((None, D_MODEL, IN_S), lambda s, m: (s, 0, 0)),
               o_spec=pl.BlockSpec((512, IN_S), lambda s, m: (m, s)), o_shape=(T, IN_W), o_dtype=bf16, dims=NN,
               sem=("parallel", "parallel"))
    proj3 = proj.reshape(B, SEQ, IN_W)
    proj4 = proj.reshape(B, 1, SEQ, IN_W)

    qkv = [None]
    for g in range(1, len(GROUPS)):
        cols = [proj3[:, :, base + g * GW: base + (g + 1) * GW] for base in (0, ATTN_W, 2 * ATTN_W)]
        qkv.append(_perm(jnp.concatenate(cols, axis=-1), GROUPS[g][1]))

    def qkv_args(g):
        return (proj4, proj4, proj4, 0, 3, 6) if g == 0 else (qkv[g], qkv[g], qkv[g], 0, 1, 2)

    outs, lses = [], []
    for g in range(len(GROUPS)):
        o4, l4 = _attn_fwd(g, *qkv_args(g))
        outs.append(_unperm(o4).reshape(T, GW))
        lses.append(_unperm(l4).reshape(T, LANES))
    mix, lse_tot = _attn_mix(outs, lses)
    y_attn = _mm("attn_out", mix, w_attn_out, grid=(N_CHIPS, T // 512),
                 a_spec=pl.BlockSpec((512, GW), lambda s, m: (m, 0)),
                 b_spec=pl.BlockSpec((None, GW, D_MODEL // N_CHIPS), lambda s, m: (s, 0, 0)),
                 o_spec=pl.BlockSpec((512, D_MODEL // N_CHIPS), lambda s, m: (m, s)), o_shape=(T, D_MODEL), o_dtype=bf16,
                 dims=NN, sem=("parallel", "parallel"))

    c1 = _glu_conv_fwd(proj3, conv_w, conv_b).reshape(T, D_MODEL)
    c3 = _ln_silu_fwd(c1, conv_ln_g, conv_ln_b)
    y_conv = _mm_nn_full("conv_out", c3, w_conv_out_f, bf16)

    merged = _merge_fwd(proj, gate_b, y_attn, y_conv)
    x1 = _mm_nn_full("o_proj", merged, w_o_f, f32, res=xf)

    h2 = _rms_fwd(x1, norm2_g, "rms2_fwd")
    fa, fb, ff = _ffn_up(h2, w_gate, w_up)
    x2 = _mm("ffn_down", ff, w_down, grid=(T // 1024, N_CHIPS),
             a_spec=pl.BlockSpec((None, 1024, FF_S), lambda m, s: (s, m, 0)),
             b_spec=pl.BlockSpec((None, FF_S, D_MODEL), lambda m, s: (s, 0, 0)),
             o_spec=pl.BlockSpec((1024, D_MODEL), lambda m, s: (m, 0)), o_shape=(T, D_MODEL), o_dtype=f32, dims=NN,
             acc_shape=(1024, D_MODEL), k_axis=1, res=x1, res_spec=pl.BlockSpec((1024, D_MODEL), lambda m, s: (m, 0)),
             sem=("parallel", "arbitrary"))

    loss, dx2, d_gf = _final_loss(x2, norm_f_g.reshape(1, D_MODEL), tf)

    d_w_down = _mm("d_w_down", ff, dx2, grid=(N_CHIPS, T // 512),
                   a_spec=pl.BlockSpec((None, 512, FF_S), lambda s, k: (s, k, 0)),
                   b_spec=pl.BlockSpec((512, D_MODEL), lambda s, k: (k, 0)),
                   o_spec=pl.BlockSpec((None, FF_S, D_MODEL), lambda s, k: (s, 0, 0)), o_shape=(N_CHIPS, FF_S, D_MODEL),
                   o_dtype=bf16, dims=TN, acc_shape=(FF_S, D_MODEL), k_axis=1, sem=("parallel", "arbitrary"))
    da, db = _ffn_down_bwd(dx2, w_down, fa, fb)

    def d_w_ff(name, dz):
        return _mm(name, h2, dz, grid=(N_CHIPS, T // 512),
                   a_spec=pl.BlockSpec((512, D_MODEL), lambda s, k: (k, 0)),
                   b_spec=pl.BlockSpec((None, 512, FF_S), lambda s, k: (s, k, 0)),
                   o_spec=pl.BlockSpec((None, D_MODEL, FF_S), lambda s, k: (s, 0, 0)), o_shape=(N_CHIPS, D_MODEL, FF_S),
                   o_dtype=bf16, dims=TN, acc_shape=(D_MODEL, FF_S), k_axis=1, sem=("parallel", "arbitrary"))

    d_w_gate = d_w_ff("d_w_gate", da)
    d_w_up = d_w_ff("d_w_up", db)
    dh2 = _ffn_dh2(da, w_gate, db, w_up)
    dx1, d_g2 = _rms_bwd(dh2, x1, norm2_g, dx2, "rms2_bwd")

    d_w_o = _mm_tn_tokens("d_w_o", merged, dx1, bf16).reshape(N_CHIPS, D_MODEL // N_CHIPS, D_MODEL)
    dmerged = _mm_nt_full("d_merged", dx1, w_o_f, bf16)
    dya, dyc, dga, dgc, d_gba, d_gbc = _merge_bwd(dmerged, proj, gate_b, y_attn, y_conv)

    d_w_conv_out = _mm_tn_tokens("d_w_conv_out", c3, dyc, bf16).reshape(N_CHIPS, D_MODEL // N_CHIPS, D_MODEL)
    dc3 = _mm_nt_full("d_c3", dyc, w_conv_out_f, bf16)
    dc1, d_ln_g, d_ln_b = _ln_silu_bwd(dc3, c1, conv_ln_g, conv_ln_b)
    dua, dub, d_conv_w, d_conv_b = _glu_conv_bwd(dc1.reshape(B, SEQ, D_MODEL), proj3, conv_w)

    d_w_attn_out = _mm("d_w_attn_out", mix, dya, grid=(N_CHIPS, T // 512),
                       a_spec=pl.BlockSpec((512, GW), lambda s, k: (k, 0)),
                       b_spec=pl.BlockSpec((512, D_MODEL // N_CHIPS), lambda s, k: (k, s)),
                       o_spec=pl.BlockSpec((None, GW, D_MODEL // N_CHIPS), lambda s, k: (s, 0, 0)),
                       o_shape=(N_CHIPS, GW, D_MODEL // N_CHIPS), o_dtype=bf16, dims=TN, acc_shape=(GW, D_MODEL // N_CHIPS),
                       k_axis=1, sem=("parallel", "arbitrary"))
    dmix = _mm("d_mix", dya, w_attn_out, grid=(T // 512, N_CHIPS),
               a_spec=pl.BlockSpec((512, D_MODEL // N_CHIPS), lambda m, s: (m, s)),
               b_spec=pl.BlockSpec((None, GW, D_MODEL // N_CHIPS), lambda m, s: (s, 0, 0)),
               o_spec=pl.BlockSpec((512, GW), lambda m, s: (m, 0)), o_shape=(T, GW), o_dtype=bf16, dims=NT,
               acc_shape=(512, GW), k_axis=1, sem=("parallel", "arbitrary"))
    delta = _attn_delta(dmix, mix)
    dmix3 = dmix.reshape(B, SEQ, GW)
    lse3 = lse_tot.reshape(B, SEQ, LANES)
    delta3 = delta.reshape(B, SEQ, LANES)
    dqs, dks, dvs = [], [], []
    for g in range(len(GROUPS)):
        r = GROUPS[g][1]
        dq4, dk4, dv4 = _attn_bwd(g, *qkv_args(g), _perm(dmix3, r), _perm(lse3, r), _perm(delta3, r))
        dqs.append(_unperm(dq4))
        dks.append(_unperm(dk4))
        dvs.append(_unperm(dv4))
    dproj = jnp.concatenate(dqs + dks + dvs + [dua, dub, dga.reshape(B, SEQ, D_MODEL), dgc.reshape(B, SEQ, D_MODEL)],
                            axis=-1).reshape(T, IN_W)

    d_w_in = _mm("d_w_in", h, dproj, grid=(N_CHIPS, T // 512),
                 a_spec=pl.BlockSpec((512, D_MODEL), lambda s, k: (k, 0)),
                 b_spec=pl.BlockSpec((512, IN_S), lambda s, k: (k, s)),
                 o_spec=pl.BlockSpec((None, D_MODEL, IN_S), lambda s, k: (s, 0, 0)), o_shape=(N_CHIPS, D_MODEL, IN_S),
                 o_dtype=bf16, dims=TN, acc_shape=(D_MODEL, IN_S), k_axis=1, sem=("parallel", "arbitrary"))
    dh = _mm("d_h", dproj, w_in, grid=(T // 512, N_CHIPS),
             a_spec=pl.BlockSpec((512, IN_S), lambda m, s: (m, s)),
             b_spec=pl.BlockSpec((None, D_MODEL, IN_S), lambda m, s: (s, 0, 0)),
             o_spec=pl.BlockSpec((512, D_MODEL), lambda m, s: (m, 0)), o_shape=(T, D_MODEL), o_dtype=f32, dims=NT,
             acc_shape=(512, D_MODEL), k_axis=1, sem=("parallel", "arbitrary"))
    dx, d_g1 = _rms_bwd(dh, xf, norm1_g, dx1, "rms1_bwd")

    big = dict(w_in=d_w_in, w_conv_out=d_w_conv_out, w_attn_out=d_w_attn_out, w_o=d_w_o, w_ffn_gate=d_w_gate,
               w_ffn_up=d_w_up, w_ffn_down=d_w_down)
    small = dict(norm1_g=d_g1, gate_b=jnp.concatenate([d_gba, d_gbc], axis=-1), conv_b=d_conv_b, conv_ln_g=d_ln_g,
                 conv_ln_b=d_ln_b, norm2_g=d_g2, norm_f_g=d_gf, conv_w=d_conv_w)
    return loss, dx.reshape(B, SEQ, D_MODEL), big, small


BIG = ("w_in", "w_conv_out", "w_attn_out", "w_o", "w_ffn_gate", "w_ffn_up", "w_ffn_down")
SMALL = ("norm1_g", "gate_b", "conv_b", "conv_ln_g", "conv_ln_b", "norm2_g", "norm_f_g")
SMALL_ROWS = {"norm1_g": 8, "gate_b": 16, "conv_b": 8, "conv_ln_g": 8, "conv_ln_b": 8, "norm2_g": 8, "norm_f_g": 8}
LOSS_ROWS = 8
CONVW_ROWS = 32 * D_MODEL // LANES


def kernel(x, norm1_g, w_in, gate_b, conv_w, conv_b, conv_ln_g, conv_ln_b, w_conv_out, w_attn_out, w_o, norm2_g, w_ffn_gate, w_ffn_up, w_ffn_down, norm_f_g, loss_target, m_norm1_g, m_w_in, m_gate_b, m_conv_w, m_conv_b, m_conv_ln_g, m_conv_ln_b, m_w_conv_out, m_w_attn_out, m_w_o, m_norm2_g, m_w_ffn_gate, m_w_ffn_up, m_w_ffn_down, m_norm_f_g, v_norm1_g, v_w_in, v_gate_b, v_conv_w, v_conv_b, v_conv_ln_g, v_conv_ln_b, v_w_conv_out, v_w_attn_out, v_w_o, v_norm2_g, v_w_ffn_gate, v_w_ffn_up, v_w_ffn_down, v_norm_f_g):
    W = dict(norm1_g=norm1_g, w_in=w_in, gate_b=gate_b, conv_w=conv_w, conv_b=conv_b, conv_ln_g=conv_ln_g, conv_ln_b=conv_ln_b,
             w_conv_out=w_conv_out, w_attn_out=w_attn_out, w_o=w_o, norm2_g=norm2_g, w_ffn_gate=w_ffn_gate, w_ffn_up=w_ffn_up,
             w_ffn_down=w_ffn_down, norm_f_g=norm_f_g)
    M = dict(norm1_g=m_norm1_g, w_in=m_w_in, gate_b=m_gate_b, conv_w=m_conv_w, conv_b=m_conv_b, conv_ln_g=m_conv_ln_g,
             conv_ln_b=m_conv_ln_b, w_conv_out=m_w_conv_out, w_attn_out=m_w_attn_out, w_o=m_w_o, norm2_g=m_norm2_g,
             w_ffn_gate=m_w_ffn_gate, w_ffn_up=m_w_ffn_up, w_ffn_down=m_w_ffn_down, norm_f_g=m_norm_f_g)
    V = dict(norm1_g=v_norm1_g, w_in=v_w_in, gate_b=v_gate_b, conv_w=v_conv_w, conv_b=v_conv_b, conv_ln_g=v_conv_ln_g,
             conv_ln_b=v_conv_ln_b, w_conv_out=v_w_conv_out, w_attn_out=v_w_attn_out, w_o=v_w_o, norm2_g=v_norm2_g,
             w_ffn_gate=v_w_ffn_gate, w_ffn_up=v_w_ffn_up, w_ffn_down=v_w_ffn_down, norm_f_g=v_norm_f_g)
    order = list(W)
    shard2d = {n: W[n].reshape(W[n].shape[-2:]) for n in BIG}
    chip = 2 * lax.axis_index("x") + lax.axis_index("y")

    cw = jnp.zeros((CONV_K, D_MODEL), f32)
    cw = lax.dynamic_update_slice(cw, 0.5 * conv_w.reshape(CONV_K, D_MODEL // N_CHIPS), (0, chip * (D_MODEL // N_CHIPS)))
    conv_w_full = _small_allreduce(_rows128(cw, CONVW_ROWS), "conv_w_gather")[: CONV_K * D_MODEL // LANES].reshape(CONV_K, D_MODEL)

    core = lax.axis_index("c").astype(jnp.int32).reshape(1)
    chip1 = chip.astype(jnp.int32).reshape(1)
    shards = [shard2d[n].astype(bf16) for n in BIG]
    slots = [lax.dynamic_update_slice(lax.empty((N_CHIPS,) + s.shape, bf16), s[None], (chip, 0, 0)) for s in shards]
    gathered = _allgather_weights(shards, slots)
    full = dict(zip(BIG, gathered))

    loss, grad_x, gbig, gsmall = _local_step(
        x, loss_target, norm1_g, gate_b, conv_w_full, conv_b, conv_ln_g, conv_ln_b, norm2_g, norm_f_g,
        full["w_in"], full["w_conv_out"], full["w_attn_out"], full["w_o"], full["w_ffn_gate"], full["w_ffn_up"], full["w_ffn_down"])

    parts = [gbig[n] for n in BIG]
    pe = _pair_exchange(parts)
    sums = [_add_pairs(p, got, core, f"pair_sum_{n}") for n, p, got in zip(BIG, parts, pe)]
    ce = _chip_exchange(sums)
    reduced = _pair_gather([_sum4(s, got, chip1, core, f"chip_sum_{n}") for n, s, got in zip(BIG, sums, ce)])
    grads = dict(zip(BIG, reduced))

    pieces = [_rows128(loss, LOSS_ROWS)] + [_rows128(gsmall[n], SMALL_ROWS[n]) for n in SMALL] + [_rows128(gsmall["conv_w"], CONVW_ROWS)]
    tot = _small_allreduce(jnp.concatenate(pieces, axis=0), "small_allreduce")
    loss_out = tot[0, 0]
    row = LOSS_ROWS
    for n in SMALL:
        grads[n] = tot[row: row + W[n].size // LANES].reshape(W[n].shape)
        row += SMALL_ROWS[n]
    dcw = tot[row: row + CONV_K * D_MODEL // LANES].reshape(CONV_K, D_MODEL)
    grads["conv_w"] = lax.dynamic_slice(dcw, (0, chip * (D_MODEL // N_CHIPS)), (CONV_K, D_MODEL // N_CHIPS))

    delta, new_m, new_v = {}, {}, {}
    for n in BIG:
        d, nm, nv = _adamw(shard2d[n], grads[n], M[n].reshape(shard2d[n].shape), V[n].reshape(shard2d[n].shape), f"adamw_{n}")
        delta[n], new_m[n], new_v[n] = d.reshape(W[n].shape), nm.reshape(W[n].shape), nv.reshape(W[n].shape)
        grads[n] = grads[n].reshape(W[n].shape)

    def pack(src):
        return jnp.concatenate([_rows128(src[n], SMALL_ROWS[n]) for n in SMALL], axis=0)

    d, nm, nv = _adamw(pack(W), pack(grads), pack(M), pack(V), "adamw_small")
    row = 0
    for n in SMALL:
        k = W[n].size // LANES
        delta[n], new_m[n], new_v[n] = (t[row: row + k].reshape(W[n].shape) for t in (d, nm, nv))
        row += SMALL_ROWS[n]

    def pad32(a):
        return jnp.pad(a.reshape(CONV_K, D_MODEL // N_CHIPS), ((0, 1), (0, 0)))

    d, nm, nv = _adamw(pad32(conv_w), pad32(grads["conv_w"]), pad32(m_conv_w), pad32(v_conv_w), "adamw_conv_w")
    delta["conv_w"], new_m["conv_w"], new_v["conv_w"] = (t[:CONV_K].reshape(conv_w.shape) for t in (d, nm, nv))
    grads["conv_w"] = grads["conv_w"].reshape(conv_w.shape)

    return (loss_out, grad_x, *[grads[n] for n in order], *[delta[n] for n in order],
            *[new_m[n] for n in order], *[new_v[n] for n in order])
```

```python
import functools
import math
from typing import Callable, NamedTuple

import numpy as np
import jax
import jax.numpy as jnp
from jax import lax
from jax.experimental import pallas as pl
from jax.experimental.pallas import tpu as pltpu

f32 = jnp.float32
bf16 = jnp.bfloat16
SDS = jax.ShapeDtypeStruct
MESH = pl.DeviceIdType.MESH

D_MODEL = 1024
SEQ = 2048
HEAD_DIM = 64
HEADS = 8
GROUPS = ((128, 1), (512, 4), (2048, 16))
GW = HEADS * HEAD_DIM
ATTN_W = len(GROUPS) * GW
Q_BLOCK = 128
CONV_K = 31
D_FF = 2816
IN_W = 3 * ATTN_W + 2 * D_MODEL + 2 * D_MODEL
N_CHIPS = 4
IN_S = IN_W // N_CHIPS
FF_S = D_FF // N_CHIPS
RMS_EPS = 1e-6
LN_EPS = 1e-5
LR, B1, B2, ADAM_EPS, WD, STEP = 0.001, 0.9, 0.999, 1e-08, 0.01, 10
NEG = -1e30
LANES = 128
VMEM_LIMIT = 48 * 2 ** 20
CB = 512
UA_CB, UB_CB, GA_CB, GC_CB = 9, 11, 13, 15


def _alibi_slope_list(n):
    def pow2(m):
        start = 2.0 ** (-8.0 / m)
        return [start ** (i + 1) for i in range(m)]
    if math.log2(n).is_integer():
        return pow2(n)
    c = 2 ** math.floor(math.log2(n))
    return pow2(c) + _alibi_slope_list(2 * c)[0::2][: n - c]


_SLOPES = np.asarray(sorted(_alibi_slope_list(len(GROUPS) * HEADS), reverse=True), dtype=np.float32).reshape(len(GROUPS), HEADS)


def _cp(sem=None, vmem=VMEM_LIMIT):
    return pltpu.CompilerParams(dimension_semantics=sem, vmem_limit_bytes=vmem)


def _sigmoid(x):
    return 1.0 / (1.0 + jnp.exp(-x))


HBM = pl.BlockSpec(memory_space=pl.ANY)


class _Exchange(NamedTuple):
    ins: list
    out_shape: list
    scratch: list
    aliases: dict
    start: Callable
    wait: Callable


def _launch(body, *, name, grid, in_specs, out_specs, out_shape, args, scratch_shapes=(), sem=None, comm=None):
    if comm is None:
        return pl.pallas_call(body, grid=grid, in_specs=in_specs, out_specs=out_specs, out_shape=out_shape,
                              scratch_shapes=list(scratch_shapes), name=name, compiler_params=_cp(sem))(*args)
    multi = isinstance(out_shape, (tuple, list))
    m_out = list(out_shape) if multi else [out_shape]
    m_ospec = list(out_specs) if multi else [out_specs]
    n_in, n_out, n_scr = len(in_specs), len(m_out), len(scratch_shapes)
    nc_in, nc_out = len(comm.ins), len(comm.out_shape)

    def hosted(*refs):
        bounds = np.cumsum([0, n_in, nc_in, n_out, nc_out, n_scr])
        mi, ci, mo, co, ms = (refs[a:b] for a, b in zip(bounds[:-1], bounds[1:]))
        cs = refs[bounds[-1]:]
        ids = [pl.program_id(a) for a in range(len(grid))]
        first = functools.reduce(jnp.logical_and, [i == 0 for i in ids])
        last = functools.reduce(jnp.logical_and, [i == g - 1 for i, g in zip(ids, grid)])

        @pl.when(first)
        def _():
            comm.start(ci, co, cs)

        body(*mi, *mo, *ms)

        @pl.when(last)
        def _():
            comm.wait(ci, co, cs)

    res = pl.pallas_call(
        hosted, grid=grid, in_specs=list(in_specs) + [HBM] * nc_in, out_specs=m_ospec + [HBM] * nc_out,
        out_shape=tuple(m_out) + tuple(comm.out_shape), scratch_shapes=list(scratch_shapes) + list(comm.scratch),
        input_output_aliases={n_in + i: n_out + o for i, o in comm.aliases.items()}, name=name + "_comm",
        compiler_params=pltpu.CompilerParams(dimension_semantics=("arbitrary",) * len(grid), vmem_limit_bytes=VMEM_LIMIT,
                                             has_side_effects=True))(*args, *comm.ins)
    return (tuple(res[:n_out]) if multi else res[0]), tuple(res[n_out:])


def _run_exchange(ex, name):
    n_in = len(ex.ins)

    def body(*refs):
        ins, outs, sems = refs[:n_in], refs[n_in:n_in + len(ex.out_shape)], refs[n_in + len(ex.out_shape):]
        ex.start(ins, outs, sems)
        ex.wait(ins, outs, sems)

    return pl.pallas_call(body, in_specs=[HBM] * n_in, out_specs=[HBM] * len(ex.out_shape), out_shape=tuple(ex.out_shape),
                          scratch_shapes=list(ex.scratch), input_output_aliases=dict(ex.aliases), name=name,
                          compiler_params=pltpu.CompilerParams(has_side_effects=True))(*ex.ins)


def _rms_fwd(x, g, name):
    T = x.shape[0]
    tm = 512

    def body(x_ref, g_ref, o_ref):
        xv = x_ref[...]
        r = lax.rsqrt(jnp.mean(xv * xv, axis=-1, keepdims=True) + RMS_EPS)
        o_ref[...] = (xv * r * g_ref[...]).astype(o_ref.dtype)

    row = pl.BlockSpec((tm, D_MODEL), lambda i: (i, 0))
    vec = pl.BlockSpec((1, D_MODEL), lambda i: (0, 0))
    return pl.pallas_call(body, grid=(T // tm,), in_specs=[row, vec], out_specs=row,
                          out_shape=SDS((T, D_MODEL), bf16), name=name, compiler_params=_cp(("parallel",)))(x, g)


def _rms_bwd(dy, x, g, dres, name):
    T = x.shape[0]
    tm = 512

    def body(dy_ref, x_ref, g_ref, r_ref, dx_ref, dg_ref):
        xv = x_ref[...]
        r = lax.rsqrt(jnp.mean(xv * xv, axis=-1, keepdims=True) + RMS_EPS)
        xh = xv * r
        dyv = dy_ref[...]
        dxh = dyv * g_ref[...]
        dx_ref[...] = r_ref[...] + r * (dxh - xh * jnp.mean(dxh * xh, axis=-1, keepdims=True))
        part = jnp.sum(dyv * xh, axis=0, keepdims=True)

        @pl.when(pl.program_id(0) == 0)
        def _():
            dg_ref[...] = part

        @pl.when(pl.program_id(0) > 0)
        def _():
            dg_ref[...] += part

    row = pl.BlockSpec((tm, D_MODEL), lambda i: (i, 0))
    vec = pl.BlockSpec((1, D_MODEL), lambda i: (0, 0))
    return pl.pallas_call(body, grid=(T // tm,), in_specs=[row, row, vec, row], out_specs=[row, vec],
                          out_shape=(SDS((T, D_MODEL), f32), SDS((1, D_MODEL), f32)), name=name,
                          compiler_params=_cp(("arbitrary",)))(dy, x, g, dres)


def _final_loss(x2, gf, target):
    T = x2.shape[0]
    tm = 512

    def body(x_ref, g_ref, t_ref, loss_ref, dx_ref, dg_ref):
        xv = x_ref[...]
        gv = g_ref[...]
        r = lax.rsqrt(jnp.mean(xv * xv, axis=-1, keepdims=True) + RMS_EPS)
        xh = xv * r
        e = xh * gv - t_ref[...]
        part_l = jnp.broadcast_to(0.5 * jnp.sum(jnp.mean(e * e, axis=-1, keepdims=True), axis=0, keepdims=True), (1, LANES))
        dy = e * (1.0 / D_MODEL)
        dxh = dy * gv
        dx_ref[...] = r * (dxh - xh * jnp.mean(dxh * xh, axis=-1, keepdims=True))
        part_g = jnp.sum(dy * xh, axis=0, keepdims=True)

        @pl.when(pl.program_id(0) == 0)
        def _():
            loss_ref[...] = part_l
            dg_ref[...] = part_g

        @pl.when(pl.program_id(0) > 0)
        def _():
            loss_ref[...] += part_l
            dg_ref[...] += part_g

    row = pl.BlockSpec((tm, D_MODEL), lambda i: (i, 0))
    vec = pl.BlockSpec((1, D_MODEL), lambda i: (0, 0))
    lvec = pl.BlockSpec((1, LANES), lambda i: (0, 0))
    return pl.pallas_call(body, grid=(T // tm,), in_specs=[row, vec, row], out_specs=[lvec, row, vec],
                          out_shape=(SDS((1, LANES), f32), SDS((T, D_MODEL), f32), SDS((1, D_MODEL), f32)),
                          name="final_loss", compiler_params=_cp(("arbitrary",)))(x2, gf, target)


def _ln_silu_fwd(c1, g, b):
    T = c1.shape[0]
    tm = 512

    def body(c_ref, g_ref, b_ref, o_ref):
        cv = c_ref[...]
        mu = jnp.mean(cv, axis=-1, keepdims=True)
        cc = cv - mu
        var = jnp.mean(cc * cc, axis=-1, keepdims=True)
        c2 = cc * lax.rsqrt(var + LN_EPS) * g_ref[...] + b_ref[...]
        o_ref[...] = (c2 * _sigmoid(c2)).astype(o_ref.dtype)

    row = pl.BlockSpec((tm, D_MODEL), lambda i: (i, 0))
    vec = pl.BlockSpec((1, D_MODEL), lambda i: (0, 0))
    return pl.pallas_call(body, grid=(T // tm,), in_specs=[row, vec, vec], out_specs=row,
                          out_shape=SDS((T, D_MODEL), bf16), name="ln_silu_fwd", compiler_params=_cp(("parallel",)))(c1, g, b)


def _ln_silu_bwd(dc3, c1, g, b):
    T = c1.shape[0]
    tm = 512

    def body(d_ref, c_ref, g_ref, b_ref, dc_ref, dg_ref, db_ref):
        cv = c_ref[...]
        gv = g_ref[...]
        mu = jnp.mean(cv, axis=-1, keepdims=True)
        cc = cv - mu
        var = jnp.mean(cc * cc, axis=-1, keepdims=True)
        rs = lax.rsqrt(var + LN_EPS)
        xh = cc * rs
        c2 = xh * gv + b_ref[...]
        sg = _sigmoid(c2)
        dc2 = d_ref[...].astype(f32) * (sg * (1.0 + c2 * (1.0 - sg)))
        dxh = dc2 * gv
        dc_ref[...] = rs * (dxh - jnp.mean(dxh, axis=-1, keepdims=True) - xh * jnp.mean(dxh * xh, axis=-1, keepdims=True))
        pg = jnp.sum(dc2 * xh, axis=0, keepdims=True)
        pb = jnp.sum(dc2, axis=0, keepdims=True)

        @pl.when(pl.program_id(0) == 0)
        def _():
            dg_ref[...] = pg
            db_ref[...] = pb

        @pl.when(pl.program_id(0) > 0)
        def _():
            dg_ref[...] += pg
            db_ref[...] += pb

    row = pl.BlockSpec((tm, D_MODEL), lambda i: (i, 0))
    vec = pl.BlockSpec((1, D_MODEL), lambda i: (0, 0))
    return pl.pallas_call(body, grid=(T // tm,), in_specs=[row, row, vec, vec], out_specs=[row, vec, vec],
                          out_shape=(SDS((T, D_MODEL), f32), SDS((1, D_MODEL), f32), SDS((1, D_MODEL), f32)),
                          name="ln_silu_bwd", compiler_params=_cp(("arbitrary",)))(dc3, c1, g, b)


NN = (((1,), (0,)), ((), ()))
NT = (((1,), (1,)), ((), ()))
TN = (((0,), (0,)), ((), ()))


def _mm(name, a, b, *, grid, a_spec, b_spec, o_spec, o_shape, o_dtype, dims, acc_shape=None, k_axis=None,
        res=None, res_spec=None, sem=None, comm=None):
    nk = 1 if k_axis is None else grid[k_axis]

    def body(*refs):
        if res is None:
            a_ref, b_ref, o_ref = refs[:3]
            r_ref, scr = None, refs[3:]
        else:
            a_ref, b_ref, r_ref, o_ref = refs[:4]
            scr = refs[4:]
        p = lax.dot_general(a_ref[...].astype(bf16), b_ref[...].astype(bf16), dims, preferred_element_type=f32)
        if nk == 1:
            if r_ref is not None:
                p = p + r_ref[...]
            o_ref[...] = p.astype(o_dtype)
            return
        acc = scr[0]
        k = pl.program_id(k_axis)

        @pl.when(k == 0)
        def _():
            acc[...] = p

        @pl.when(k > 0)
        def _():
            acc[...] += p

        @pl.when(k == nk - 1)
        def _():
            t = acc[...]
            if r_ref is not None:
                t = t + r_ref[...]
            o_ref[...] = t.astype(o_dtype)

    ins = [a, b] + ([] if res is None else [res])
    specs = [a_spec, b_spec] + ([] if res is None else [res_spec])
    scratch = [] if nk == 1 else [pltpu.VMEM(acc_shape, f32)]
    return _launch(body, name=name, grid=grid, in_specs=specs, out_specs=o_spec, out_shape=SDS(o_shape, o_dtype),
                   args=ins, scratch_shapes=scratch, sem=sem, comm=comm)


def _mm_nn_full(name, a, b, o_dtype, res=None, tm=512):
    T, K = a.shape
    N = b.shape[1]
    return _mm(name, a, b, grid=(T // tm,), a_spec=pl.BlockSpec((tm, K), lambda i: (i, 0)),
               b_spec=pl.BlockSpec((K, N), lambda i: (0, 0)), o_spec=pl.BlockSpec((tm, N), lambda i: (i, 0)),
               o_shape=(T, N), o_dtype=o_dtype, dims=NN, res=res,
               res_spec=None if res is None else pl.BlockSpec((tm, N), lambda i: (i, 0)), sem=("parallel",))


def _mm_nt_full(name, a, b, o_dtype, tm=512):
    T, N = a.shape
    K = b.shape[0]
    return _mm(name, a, b, grid=(T // tm,), a_spec=pl.BlockSpec((tm, N), lambda i: (i, 0)),
               b_spec=pl.BlockSpec((K, N), lambda i: (0, 0)), o_spec=pl.BlockSpec((tm, K), lambda i: (i, 0)),
               o_shape=(T, K), o_dtype=o_dtype, dims=NT, sem=("parallel",))


def _mm_tn_tokens(name, a, b, o_dtype, tk=512):
    T, K = a.shape
    N = b.shape[1]
    return _mm(name, a, b, grid=(T // tk,), a_spec=pl.BlockSpec((tk, K), lambda k: (k, 0)),
               b_spec=pl.BlockSpec((tk, N), lambda k: (k, 0)), o_spec=pl.BlockSpec((K, N), lambda k: (0, 0)),
               o_shape=(K, N), o_dtype=o_dtype, dims=TN, acc_shape=(K, N), k_axis=0, sem=("arbitrary",))


def _ffn_up(h2, wg, wu):
    T = h2.shape[0]
    tm = 512

    def body(h_ref, wg_ref, wu_ref, a_ref, b_ref, f_ref):
        hv = h_ref[...]
        av = jnp.dot(hv, wg_ref[...], preferred_element_type=f32)
        bv = jnp.dot(hv, wu_ref[...], preferred_element_type=f32)
        a_ref[...] = av.astype(bf16)
        b_ref[...] = bv.astype(bf16)
        f_ref[...] = (av * _sigmoid(av) * bv).astype(bf16)

    wspec = pl.BlockSpec((None, D_MODEL, FF_S), lambda s, m: (s, 0, 0))
    ospec = pl.BlockSpec((None, tm, FF_S), lambda s, m: (s, m, 0))
    osd = SDS((N_CHIPS, T, FF_S), bf16)
    return pl.pallas_call(body, grid=(N_CHIPS, T // tm),
                          in_specs=[pl.BlockSpec((tm, D_MODEL), lambda s, m: (m, 0)), wspec, wspec],
                          out_specs=[ospec, ospec, ospec], out_shape=(osd, osd, osd), name="ffn_up",
                          compiler_params=_cp(("parallel", "parallel")))(h2, wg, wu)


def _ffn_down_bwd(dx2, wd, a, b):
    T = dx2.shape[0]
    tm = 512

    def body(d_ref, w_ref, a_ref, b_ref, da_ref, db_ref):
        df = lax.dot_general(d_ref[...].astype(bf16), w_ref[...], NT, preferred_element_type=f32)
        av = a_ref[...].astype(f32)
        sg = _sigmoid(av)
        da_ref[...] = (df * b_ref[...].astype(f32) * (sg * (1.0 + av * (1.0 - sg)))).astype(bf16)
        db_ref[...] = (df * av * sg).astype(bf16)

    aspec = pl.BlockSpec((None, tm, FF_S), lambda s, m: (s, m, 0))
    osd = SDS((N_CHIPS, T, FF_S), bf16)
    return pl.pallas_call(body, grid=(N_CHIPS, T // tm),
                          in_specs=[pl.BlockSpec((tm, D_MODEL), lambda s, m: (m, 0)),
                                    pl.BlockSpec((None, FF_S, D_MODEL), lambda s, m: (s, 0, 0)), aspec, aspec],
                          out_specs=[aspec, aspec], out_shape=(osd, osd), name="ffn_down_bwd",
                          compiler_params=_cp(("parallel", "parallel")))(dx2, wd, a, b)


def _ffn_dh2(da, wg, db, wu):
    T = da.shape[1]
    tm = 1024

    def body(da_ref, wg_ref, db_ref, wu_ref, o_ref, acc):
        p = lax.dot_general(da_ref[...], wg_ref[...], NT, preferred_element_type=f32)
        p = p + lax.dot_general(db_ref[...], wu_ref[...], NT, preferred_element_type=f32)
        s = pl.program_id(1)

        @pl.when(s == 0)
        def _():
            acc[...] = p

        @pl.when(s > 0)
        def _():
            acc[...] += p

        @pl.when(s == N_CHIPS - 1)
        def _():
            o_ref[...] = acc[...]

    aspec = pl.BlockSpec((None, tm, FF_S), lambda m, s: (s, m, 0))
    wspec = pl.BlockSpec((None, D_MODEL, FF_S), lambda m, s: (s, 0, 0))
    return pl.pallas_call(body, grid=(T // tm, N_CHIPS), in_specs=[aspec, wspec, aspec, wspec],
                          out_specs=pl.BlockSpec((tm, D_MODEL), lambda m, s: (m, 0)), out_shape=SDS((T, D_MODEL), f32),
                          scratch_shapes=[pltpu.VMEM((tm, D_MODEL), f32)], name="ffn_dh2",
                          compiler_params=_cp(("parallel", "arbitrary")))(da, wg, db, wu)


def _merge_fwd(proj, gate_b, ya, yc):
    T = proj.shape[0]
    tm = 512

    def body(ga_ref, gc_ref, ba_ref, bc_ref, ya_ref, yc_ref, o_ref):
        sa = _sigmoid(ga_ref[...].astype(f32) + ba_ref[...])
        sc = _sigmoid(gc_ref[...].astype(f32) + bc_ref[...])
        o_ref[...] = (sa * ya_ref[...].astype(f32) + sc * yc_ref[...].astype(f32)).astype(bf16)

    blk = pl.BlockSpec((tm, CB), lambda i, j: (i, j))
    return pl.pallas_call(
        body, grid=(T // tm, 2),
        in_specs=[pl.BlockSpec((tm, CB), lambda i, j: (i, GA_CB + j)), pl.BlockSpec((tm, CB), lambda i, j: (i, GC_CB + j)),
                  pl.BlockSpec((1, CB), lambda i, j: (0, j)), pl.BlockSpec((1, CB), lambda i, j: (0, 2 + j)), blk, blk],
        out_specs=blk, out_shape=SDS((T, D_MODEL), bf16), name="merge_fwd",
        compiler_params=_cp(("parallel", "parallel")))(proj, proj, gate_b, gate_b, ya, yc)


def _merge_bwd(dm, proj, gate_b, ya, yc):
    T = proj.shape[0]
    tm = 512

    def body(dm_ref, ga_ref, gc_ref, ba_ref, bc_ref, ya_ref, yc_ref, dya_ref, dyc_ref, dga_ref, dgc_ref, dba_ref, dbc_ref):
        dmv = dm_ref[...].astype(f32)
        sa = _sigmoid(ga_ref[...].astype(f32) + ba_ref[...])
        sc = _sigmoid(gc_ref[...].astype(f32) + bc_ref[...])
        dya_ref[...] = (dmv * sa).astype(bf16)
        dyc_ref[...] = (dmv * sc).astype(bf16)
        dga = dmv * ya_ref[...].astype(f32) * sa * (1.0 - sa)
        dgc = dmv * yc_ref[...].astype(f32) * sc * (1.0 - sc)
        dga_ref[...] = dga.astype(bf16)
        dgc_ref[...] = dgc.astype(bf16)
        pa = jnp.sum(dga, axis=0, keepdims=True)
        pc = jnp.sum(dgc, axis=0, keepdims=True)

        @pl.when(pl.program_id(1) == 0)
        def _():
            dba_ref[...] = pa
            dbc_ref[...] = pc

        @pl.when(pl.program_id(1) > 0)
        def _():
            dba_ref[...] += pa
            dbc_ref[...] += pc

    blk = pl.BlockSpec((tm, CB), lambda j, i: (i, j))
    vec = pl.BlockSpec((1, CB), lambda j, i: (0, j))
    big = SDS((T, D_MODEL), bf16)
    small = SDS((1, D_MODEL), f32)
    return pl.pallas_call(
        body, grid=(2, T // tm),
        in_specs=[blk, pl.BlockSpec((tm, CB), lambda j, i: (i, GA_CB + j)), pl.BlockSpec((tm, CB), lambda j, i: (i, GC_CB + j)),
                  vec, pl.BlockSpec((1, CB), lambda j, i: (0, 2 + j)), blk, blk],
        out_specs=[blk, blk, blk, blk, vec, vec], out_shape=(big, big, big, big, small, small), name="merge_bwd",
        compiler_params=_cp(("parallel", "arbitrary")))(dm, proj, proj, gate_b, gate_b, ya, yc)


CONV_TS = 256
CONV_HALO = 32
CONV_RC = 64


def _glu_conv_fwd(proj3, w, bias):
    B = proj3.shape[0]
    nt = SEQ // CONV_TS
    hb = CONV_TS // CONV_HALO

    def body(ua_ref, ub_ref, ha_ref, hb_ref, w_ref, b_ref, o_ref, win):
        i = pl.program_id(2)
        c0 = ua_ref[...].astype(f32) * _sigmoid(ub_ref[...].astype(f32))
        halo = ha_ref[...].astype(f32) * _sigmoid(hb_ref[...].astype(f32))
        win[0:CONV_HALO, :] = jnp.where(i > 0, halo, 0.0)
        win[CONV_HALO:, :] = c0
        for r0 in range(0, CONV_TS, CONV_RC):
            acc = jnp.zeros((CONV_RC, CB), f32) + b_ref[...]
            for k in range(CONV_K):
                acc = acc + win[pl.ds(r0 + CONV_HALO - (CONV_K - 1) + k, CONV_RC), :] * w_ref[k:k + 1, :]
            o_ref[r0:r0 + CONV_RC, :] = acc

    def cur(cb):
        return pl.BlockSpec((None, CONV_TS, CB), lambda b, j, i: (b, i, cb + j))

    def prev(cb):
        return pl.BlockSpec((None, CONV_HALO, CB), lambda b, j, i: (b, jnp.maximum(i * hb - 1, 0), cb + j))

    return pl.pallas_call(
        body, grid=(B, 2, nt),
        in_specs=[cur(UA_CB), cur(UB_CB), prev(UA_CB), prev(UB_CB),
                  pl.BlockSpec((CONV_K, CB), lambda b, j, i: (0, j)), pl.BlockSpec((1, CB), lambda b, j, i: (0, j))],
        out_specs=pl.BlockSpec((None, CONV_TS, CB), lambda b, j, i: (b, i, j)),
        out_shape=SDS((B, SEQ, D_MODEL), f32), scratch_shapes=[pltpu.VMEM((CONV_TS + CONV_HALO, CB), f32)],
        name="glu_conv_fwd", compiler_params=_cp(("parallel", "parallel", "parallel")))(proj3, proj3, proj3, proj3, w, bias)


def _glu_conv_bwd(dc1, proj3, w, comm=None):
    B = proj3.shape[0]
    nt = SEQ // CONV_TS
    hb = CONV_TS // CONV_HALO

    def body(d_ref, dn_ref, ua_ref, ub_ref, ha_ref, hb_ref, w_ref, dua_ref, dub_ref, dw_ref, db_ref, winc, wind, accw):
        b = pl.program_id(1)
        i = pl.program_id(2)
        first = jnp.logical_and(b == 0, i == 0)
        last = jnp.logical_and(b == B - 1, i == nt - 1)

        @pl.when(first)
        def _():
            accw[...] = jnp.zeros_like(accw)
            db_ref[...] = jnp.zeros_like(db_ref)

        halo = ha_ref[...].astype(f32) * _sigmoid(hb_ref[...].astype(f32))
        winc[0:CONV_HALO, :] = jnp.where(i > 0, halo, 0.0)
        winc[CONV_HALO:, :] = ua_ref[...].astype(f32) * _sigmoid(ub_ref[...].astype(f32))
        wind[0:CONV_TS, :] = d_ref[...]
        wind[CONV_TS:, :] = jnp.where(i < nt - 1, dn_ref[...], 0.0)
        db_ref[...] += jnp.sum(d_ref[...], axis=0, keepdims=True)
        for r0 in range(0, CONV_TS, CONV_RC):
            dc0 = jnp.zeros((CONV_RC, CB), f32)
            for k in range(CONV_K):
                dc0 = dc0 + wind[pl.ds(r0 + (CONV_K - 1) - k, CONV_RC), :] * w_ref[k:k + 1, :]
            uav = ua_ref[r0:r0 + CONV_RC, :].astype(f32)
            sg = _sigmoid(ub_ref[r0:r0 + CONV_RC, :].astype(f32))
            dua_ref[r0:r0 + CONV_RC, :] = (dc0 * sg).astype(bf16)
            dub_ref[r0:r0 + CONV_RC, :] = (dc0 * uav * sg * (1.0 - sg)).astype(bf16)
            dv = wind[r0:r0 + CONV_RC, :]
            for k in range(CONV_K):
                prod = dv * winc[pl.ds(r0 + CONV_HALO - (CONV_K - 1) + k, CONV_RC), :]
                accw[k] += jnp.sum(prod.reshape(CONV_RC // 8, 8, CB), axis=0)

        @pl.when(last)
        def _():
            for k in range(CONV_K):
                dw_ref[k:k + 1, :] = jnp.sum(accw[k], axis=0, keepdims=True)
            dw_ref[CONV_K:, :] = jnp.zeros((CONV_HALO - CONV_K, CB), f32)

    def cur(cb):
        return pl.BlockSpec((None, CONV_TS, CB), lambda j, b, i: (b, i, cb + j))

    def prev(cb):
        return pl.BlockSpec((None, CONV_HALO, CB), lambda j, b, i: (b, jnp.maximum(i * hb - 1, 0), cb + j))

    nxt = pl.BlockSpec((None, CONV_HALO, CB), lambda j, b, i: (b, jnp.minimum((i + 1) * hb, SEQ // CONV_HALO - 1), j))
    big = SDS((B, SEQ, D_MODEL), bf16)
    return _launch(
        body, name="glu_conv_bwd", grid=(2, B, nt),
        in_specs=[cur(0), nxt, cur(UA_CB), cur(UB_CB), prev(UA_CB), prev(UB_CB), pl.BlockSpec((CONV_K, CB), lambda j, b, i: (0, j))],
        out_specs=[cur(0), cur(0), pl.BlockSpec((CONV_HALO, CB), lambda j, b, i: (0, j)), pl.BlockSpec((1, CB), lambda j, b, i: (0, j))],
        out_shape=(big, big, SDS((CONV_HALO, D_MODEL), f32), SDS((1, D_MODEL), f32)),
        args=(dc1, dc1, proj3, proj3, proj3, proj3, w),
        scratch_shapes=[pltpu.VMEM((CONV_TS + CONV_HALO, CB), f32), pltpu.VMEM((CONV_TS + CONV_HALO, CB), f32),
                        pltpu.VMEM((CONV_K, 8, CB), f32)],
        sem=("parallel", "arbitrary", "arbitrary"), comm=comm)


def _band(first, dil):
    kw = Q_BLOCK if first else 2 * Q_BLOCK
    qi = lax.broadcasted_iota(jnp.int32, (Q_BLOCK, kw), 0)
    kj = lax.broadcasted_iota(jnp.int32, (Q_BLOCK, kw), 1)
    rel = qi - kj + (0 if first else Q_BLOCK)
    valid = jnp.logical_and(rel >= 0, rel <= Q_BLOCK)
    return valid, rel.astype(f32) * float(dil)


def _scores(q, k, valid, dist, slope):
    s = lax.dot_general(q, k, NT, preferred_element_type=f32) * (HEAD_DIM ** -0.5) - slope * dist
    return jnp.where(valid, s, NEG)


def _attn_fwd(g, q4, k4, v4, qcb, kcb, vcb, comm=None):
    _, dil = GROUPS[g]
    B, r, L, _ = q4.shape
    nb = L // Q_BLOCK
    slopes = [float(s) for s in _SLOPES[g]]

    def body(q_ref, k_ref, v_ref, o_ref, lse_ref):
        lane = lax.broadcasted_iota(jnp.int32, (Q_BLOCK, LANES), 1)

        def block(n, first):
            q0 = 0 if first else pl.multiple_of(n * Q_BLOCK, Q_BLOCK)
            k0 = 0 if first else pl.multiple_of((n - 1) * Q_BLOCK, Q_BLOCK)
            kw = Q_BLOCK if first else 2 * Q_BLOCK
            valid, dist = _band(first, dil)
            st = jnp.zeros((Q_BLOCK, LANES), f32)
            for h in range(HEADS):
                cols = slice(h * HEAD_DIM, (h + 1) * HEAD_DIM)
                s = _scores(q_ref[pl.ds(q0, Q_BLOCK), cols], k_ref[pl.ds(k0, kw), cols], valid, dist, slopes[h])
                m = jnp.max(s, axis=-1, keepdims=True)
                p = jnp.exp(s - m)
                den = jnp.sum(p, axis=-1, keepdims=True)
                o = jnp.dot(p.astype(bf16), v_ref[pl.ds(k0, kw), cols], preferred_element_type=f32) / den
                o_ref[pl.ds(q0, Q_BLOCK), cols] = o.astype(bf16)
                st = jnp.where(lane == h, m + jnp.log(den), st)
            lse_ref[pl.ds(q0, Q_BLOCK), :] = st

        block(0, True)
        if nb > 1:
            def step(n, carry):
                block(n, False)
                return carry
            lax.fori_loop(1, nb, step, 0)

    def spec(cb):
        return pl.BlockSpec((None, None, L, GW), lambda b, c: (b, c, 0, cb))

    return _launch(
        body, name=f"attn_fwd_g{g}", grid=(B, r), in_specs=[spec(qcb), spec(kcb), spec(vcb)],
        out_specs=[spec(0), pl.BlockSpec((None, None, L, LANES), lambda b, c: (b, c, 0, 0))],
        out_shape=(SDS((B, r, L, GW), bf16), SDS((B, r, L, LANES), f32)), args=(q4, k4, v4),
        sem=("parallel", "parallel"), comm=comm)


def _attn_bwd(g, q4, k4, v4, qcb, kcb, vcb, do4, lse4, dl4, comm=None):
    _, dil = GROUPS[g]
    B, r, L, _ = q4.shape
    nb = L // Q_BLOCK
    slopes = [float(s) for s in _SLOPES[g]]
    scale = HEAD_DIM ** -0.5

    def body(q_ref, k_ref, v_ref, do_ref, lse_ref, dl_ref, dq_ref, dk_ref, dv_ref, dk_acc, dv_acc):
        dk_acc[...] = jnp.zeros_like(dk_acc)
        dv_acc[...] = jnp.zeros_like(dv_acc)

        def block(n, first):
            q0 = 0 if first else pl.multiple_of(n * Q_BLOCK, Q_BLOCK)
            k0 = 0 if first else pl.multiple_of((n - 1) * Q_BLOCK, Q_BLOCK)
            kw = Q_BLOCK if first else 2 * Q_BLOCK
            valid, dist = _band(first, dil)
            for h in range(HEADS):
                cols = slice(h * HEAD_DIM, (h + 1) * HEAD_DIM)
                qv = q_ref[pl.ds(q0, Q_BLOCK), cols]
                kv = k_ref[pl.ds(k0, kw), cols]
                vv = v_ref[pl.ds(k0, kw), cols]
                dov = do_ref[pl.ds(q0, Q_BLOCK), cols]
                s = _scores(qv, kv, valid, dist, slopes[h])
                p = jnp.exp(s - lse_ref[pl.ds(q0, Q_BLOCK), h:h + 1])
                dp = lax.dot_general(dov, vv, NT, preferred_element_type=f32)
                ds = (p * (dp - dl_ref[pl.ds(q0, Q_BLOCK), h:h + 1])).astype(bf16)
                dq_ref[pl.ds(q0, Q_BLOCK), cols] = (jnp.dot(ds, kv, preferred_element_type=f32) * scale).astype(bf16)
                dk_acc[pl.ds(k0, kw), cols] += lax.dot_general(ds, qv, TN, preferred_element_type=f32) * scale
                dv_acc[pl.ds(k0, kw), cols] += lax.dot_general(p.astype(bf16), dov, TN, preferred_element_type=f32)

        block(0, True)
        if nb > 1:
            def step(n, carry):
                block(n, False)
                return carry
            lax.fori_loop(1, nb, step, 0)
        dk_ref[...] = dk_acc[...].astype(bf16)
        dv_ref[...] = dv_acc[...].astype(bf16)

    def spec(cb):
        return pl.BlockSpec((None, None, L, GW), lambda b, c: (b, c, 0, cb))

    st = pl.BlockSpec((None, None, L, LANES), lambda b, c: (b, c, 0, 0))
    osd = SDS((B, r, L, GW), bf16)
    return _launch(
        body, name=f"attn_bwd_g{g}", grid=(B, r), in_specs=[spec(qcb), spec(kcb), spec(vcb), spec(0), st, st],
        out_specs=[spec(0), spec(0), spec(0)], out_shape=(osd, osd, osd), args=(q4, k4, v4, do4, lse4, dl4),
        scratch_shapes=[pltpu.VMEM((L, GW), f32), pltpu.VMEM((L, GW), f32)], sem=("parallel", "parallel"), comm=comm)


def _attn_mix(os_, lses):
    T = os_[0].shape[0]
    tm = 512
    ng = len(os_)

    def body(*refs):
        o_refs, l_refs = refs[:ng], refs[ng:2 * ng]
        y_ref, lt_ref = refs[2 * ng:]
        ls = [l_ref[...] for l_ref in l_refs]
        m = functools.reduce(jnp.maximum, ls)
        ws = [jnp.exp(l - m) for l in ls]
        den = functools.reduce(lambda u, v: u + v, ws)
        alphas = [w / den for w in ws]
        lt_ref[...] = m + jnp.log(den)
        for h in range(HEADS):
            cols = slice(h * HEAD_DIM, (h + 1) * HEAD_DIM)
            acc = alphas[0][:, h:h + 1] * o_refs[0][:, cols].astype(f32)
            for gi in range(1, ng):
                acc = acc + alphas[gi][:, h:h + 1] * o_refs[gi][:, cols].astype(f32)
            y_ref[:, cols] = acc.astype(bf16)

    ob = pl.BlockSpec((tm, GW), lambda i: (i, 0))
    lb = pl.BlockSpec((tm, LANES), lambda i: (i, 0))
    return pl.pallas_call(body, grid=(T // tm,), in_specs=[ob] * ng + [lb] * ng, out_specs=[ob, lb],
                          out_shape=(SDS((T, GW), bf16), SDS((T, LANES), f32)), name="attn_mix",
                          compiler_params=_cp(("parallel",)))(*os_, *lses)


def _attn_delta(dmix, mix):
    T = dmix.shape[0]
    tm = 512

    def body(d_ref, y_ref, o_ref):
        lane = lax.broadcasted_iota(jnp.int32, (tm, LANES), 1)
        acc = jnp.zeros((tm, LANES), f32)
        for h in range(HEADS):
            cols = slice(h * HEAD_DIM, (h + 1) * HEAD_DIM)
            dl = jnp.sum(d_ref[:, cols].astype(f32) * y_ref[:, cols].astype(f32), axis=-1, keepdims=True)
            acc = jnp.where(lane == h, dl, acc)
        o_ref[...] = acc

    ob = pl.BlockSpec((tm, GW), lambda i: (i, 0))
    return pl.pallas_call(body, grid=(T // tm,), in_specs=[ob, ob], out_specs=pl.BlockSpec((tm, LANES), lambda i: (i, 0)),
                          out_shape=SDS((T, LANES), f32), name="attn_delta", compiler_params=_cp(("parallel",)))(dmix, mix)


def _add_pairs(part, got, core, name):
    n, h, C = got.shape

    def body(c_ref, a_ref, b_ref, o_ref):
        o_ref[...] = (a_ref[...].astype(f32) + b_ref[...].astype(f32)).astype(bf16)

    blk = pl.BlockSpec((None, h, C), lambda s, c_ref: (s, 0, 0))
    own = pl.BlockSpec((None, h, C), lambda s, c_ref: (s, c_ref[0], 0))
    spec = pltpu.PrefetchScalarGridSpec(num_scalar_prefetch=1, grid=(n,), in_specs=[own, blk], out_specs=blk)
    return pl.pallas_call(body, grid_spec=spec, out_shape=SDS((n, h, C), bf16), name=name,
                          compiler_params=_cp(("parallel",)))(core, part, got)


def _sum4(sums, got, chip, core, name):
    _, h, C = sums.shape

    def body(s_ref, c_ref, own_ref, q_ref, o_ref):
        t = own_ref[...].astype(f32) + q_ref[0].astype(f32)
        t = t + q_ref[1].astype(f32)
        o_ref[...] = t + q_ref[2].astype(f32)

    spec = pltpu.PrefetchScalarGridSpec(
        num_scalar_prefetch=2, grid=(1,),
        in_specs=[pl.BlockSpec((None, h, C), lambda i, s_ref, c_ref: (s_ref[0], 0, 0)),
                  pl.BlockSpec((N_CHIPS - 1, h, C), lambda i, s_ref, c_ref: (0, 0, 0))],
        out_specs=pl.BlockSpec((h, C), lambda i, s_ref, c_ref: (c_ref[0], 0)))
    return pl.pallas_call(body, grid_spec=spec, out_shape=SDS((2 * h, C), f32), name=name,
                          compiler_params=_cp(("arbitrary",)))(chip, core, sums, got)


def _adamw(w, g, m, v, name):
    R, C = w.shape
    rt = R
    for cand in (512, 256, 128, 64, 32, 16, 8):
        if R % cand == 0 and cand * C * 4 <= 2 ** 21:
            rt = cand
            break
    c1 = 1.0 / (1.0 - B1 ** STEP)
    c2 = 1.0 / (1.0 - B2 ** STEP)

    def body(w_ref, g_ref, m_ref, v_ref, d_ref, nm_ref, nv_ref):
        gv = g_ref[...]
        nm = B1 * m_ref[...] + (1.0 - B1) * gv
        nv = B2 * v_ref[...] + (1.0 - B2) * (gv * gv)
        nm_ref[...] = nm
        nv_ref[...] = nv
        d_ref[...] = -LR * ((nm * c1) / (jnp.sqrt(nv * c2) + ADAM_EPS) + WD * w_ref[...])

    blk = pl.BlockSpec((rt, C), lambda i: (i, 0))
    sd = SDS((R, C), f32)
    return pl.pallas_call(body, grid=(R // rt,), in_specs=[blk] * 4, out_specs=[blk] * 3, out_shape=(sd, sd, sd), name=name,
                          compiler_params=_cp(("parallel",)))(w, g, m, v)


def _coords():
    return lax.axis_index("x"), lax.axis_index("y"), lax.axis_index("c")


def _other_chips(x, y):
    return [(1 - x, y), (x, 1 - y), (1 - x, 1 - y)]


def _allgather_weights(shards, slots):
    n = len(shards)
    halves = [s.shape[0] // 2 for s in shards]

    def body(*refs):
        ins, outs = refs[:n], refs[2 * n:3 * n]
        send1, recv1, send2, recv2 = refs[3 * n:]
        x, y, c = _coords()
        me = 2 * x + y
        chips = _other_chips(x, y)
        sends = []
        for w in range(n):
            mine = pl.ds(c * halves[w], halves[w])
            for j, (px, py) in enumerate(chips):
                cp = pltpu.make_async_remote_copy(src_ref=ins[w].at[mine, :], dst_ref=outs[w].at[me, mine, :],
                                                  send_sem=send1.at[w, j], recv_sem=recv1.at[w, j],
                                                  device_id=(px, py, c), device_id_type=MESH)
                cp.start()
                sends.append(cp)
        for w in range(n):
            mine = pl.ds(c * halves[w], halves[w])
            for j, (px, py) in enumerate(chips):
                blk = outs[w].at[2 * px + py, mine, :]
                pltpu.make_async_remote_copy(src_ref=blk, dst_ref=blk, send_sem=send1.at[w, j], recv_sem=recv1.at[w, j],
                                             device_id=(px, py, c), device_id_type=MESH).wait_recv()
                cp = pltpu.make_async_remote_copy(src_ref=blk, dst_ref=blk, send_sem=send2.at[w, j], recv_sem=recv2.at[w, j],
                                                  device_id=(x, y, 1 - c), device_id_type=MESH)
                cp.start()
                sends.append(cp)
        for w in range(n):
            theirs = pl.ds((1 - c) * halves[w], halves[w])
            for j, (px, py) in enumerate(chips):
                blk = outs[w].at[2 * px + py, theirs, :]
                pltpu.make_async_remote_copy(src_ref=blk, dst_ref=blk, send_sem=send2.at[w, j], recv_sem=recv2.at[w, j],
                                             device_id=(x, y, 1 - c), device_id_type=MESH).wait_recv()
        for cp in sends:
            cp.wait_send()

    return pl.pallas_call(
        body, in_specs=[HBM] * (2 * n), out_specs=[HBM] * n,
        out_shape=tuple(SDS((N_CHIPS,) + s.shape, s.dtype) for s in shards),
        input_output_aliases={n + w: w for w in range(n)},
        scratch_shapes=[pltpu.SemaphoreType.DMA((n, 3))] * 4,
        name="allgather_weights", compiler_params=pltpu.CompilerParams(has_side_effects=True))(*shards, *slots)


def _pair_exchange(parts, name):
    n = len(parts)
    halves = [p.shape[1] // 2 for p in parts]

    def body(*refs):
        ins, got_o = refs[:n], refs[n:2 * n]
        send, recv = refs[2 * n:]
        x, y, c = _coords()
        cps = []
        for w in range(n):
            give = pl.ds((1 - c) * halves[w], halves[w])
            rc = pltpu.make_async_remote_copy(src_ref=ins[w].at[:, give, :], dst_ref=got_o[w], send_sem=send.at[w],
                                              recv_sem=recv.at[w], device_id=(x, y, 1 - c), device_id_type=MESH)
            rc.start()
            cps.append(rc)
        for rc in cps:
            rc.wait()

    hs = tuple(SDS((N_CHIPS, p.shape[1] // 2, p.shape[2]), p.dtype) for p in parts)
    return pl.pallas_call(body, in_specs=[HBM] * n, out_specs=[HBM] * n, out_shape=hs,
                          scratch_shapes=[pltpu.SemaphoreType.DMA((n,))] * 2, name=name,
                          compiler_params=pltpu.CompilerParams(has_side_effects=True))(*parts)


def _x_chip_exchange(sums):
    n = len(sums)

    def copies(ins, outs, sems):
        send, recv = sems
        x, y, c = _coords()
        return [pltpu.make_async_remote_copy(src_ref=ins[w].at[2 * px + py], dst_ref=outs[w].at[j], send_sem=send.at[w, j],
                                             recv_sem=recv.at[w, j], device_id=(px, py, c), device_id_type=MESH)
                for w in range(n) for j, (px, py) in enumerate(_other_chips(x, y))]

    def start(ins, outs, sems):
        for cp in copies(ins, outs, sems):
            cp.start()

    def wait(ins, outs, sems):
        for cp in copies(ins, outs, sems):
            cp.wait()

    return _Exchange(ins=list(sums), out_shape=[SDS((N_CHIPS - 1,) + s.shape[1:], s.dtype) for s in sums],
                     scratch=[pltpu.SemaphoreType.DMA((n, 3)), pltpu.SemaphoreType.DMA((n, 3))], aliases={}, start=start, wait=wait)


def _x_gather_ici(shards, slots):
    n = len(shards)
    halves = [s.shape[0] // 2 for s in shards]

    def copies(ins, outs, sems):
        send, recv = sems
        x, y, c = _coords()
        me = 2 * x + y
        out = []
        for w in range(n):
            mine = pl.ds(c * halves[w], halves[w])
            for j, (px, py) in enumerate(_other_chips(x, y)):
                snd = pltpu.make_async_remote_copy(src_ref=ins[w].at[mine, :], dst_ref=outs[w].at[me, mine, :], send_sem=send.at[w, j],
                                                   recv_sem=recv.at[w, j], device_id=(px, py, c), device_id_type=MESH)
                got = outs[w].at[2 * px + py, mine, :]
                rcv = pltpu.make_async_remote_copy(src_ref=got, dst_ref=got, send_sem=send.at[w, j], recv_sem=recv.at[w, j],
                                                   device_id=(px, py, c), device_id_type=MESH)
                out.append((snd, rcv))
        return out

    def start(ins, outs, sems):
        for snd, _ in copies(ins, outs, sems):
            snd.start()

    def wait(ins, outs, sems):
        for snd, rcv in copies(ins, outs, sems):
            rcv.wait_recv()
            snd.wait_send()

    return _Exchange(ins=list(shards) + list(slots), out_shape=[SDS(s.shape, s.dtype) for s in slots],
                     scratch=[pltpu.SemaphoreType.DMA((n, 3)), pltpu.SemaphoreType.DMA((n, 3))],
                     aliases={n + w: w for w in range(n)}, start=start, wait=wait)


def _x_gather_fwd(bufs):
    n = len(bufs)
    halves = [b.shape[1] // 2 for b in bufs]

    def copies(ins, outs, sems):
        send, recv = sems
        x, y, c = _coords()
        out = []
        for w in range(n):
            for j, (px, py) in enumerate(_other_chips(x, y)):
                mine = outs[w].at[2 * px + py, pl.ds(c * halves[w], halves[w]), :]
                theirs = outs[w].at[2 * px + py, pl.ds((1 - c) * halves[w], halves[w]), :]
                snd = pltpu.make_async_remote_copy(src_ref=mine, dst_ref=mine, send_sem=send.at[w, j], recv_sem=recv.at[w, j],
                                                   device_id=(x, y, 1 - c), device_id_type=MESH)
                rcv = pltpu.make_async_remote_copy(src_ref=theirs, dst_ref=theirs, send_sem=send.at[w, j], recv_sem=recv.at[w, j],
                                                   device_id=(x, y, 1 - c), device_id_type=MESH)
                out.append((snd, rcv))
        return out

    def start(ins, outs, sems):
        for snd, _ in copies(ins, outs, sems):
            snd.start()

    def wait(ins, outs, sems):
        for snd, rcv in copies(ins, outs, sems):
            rcv.wait_recv()
            snd.wait_send()

    return _Exchange(ins=list(bufs), out_shape=[SDS(b.shape, b.dtype) for b in bufs],
                     scratch=[pltpu.SemaphoreType.DMA((n, 3)), pltpu.SemaphoreType.DMA((n, 3))],
                     aliases={w: w for w in range(n)}, start=start, wait=wait)


def _pair_gather(bufs):
    n = len(bufs)

    def body(*refs):
        outs = refs[n:2 * n]
        send, recv = refs[2 * n:]
        x, y, c = _coords()
        cps = []
        for w in range(n):
            h = bufs[w].shape[0] // 2
            rows = outs[w].at[pl.ds(c * h, h), :]
            rc = pltpu.make_async_remote_copy(src_ref=rows, dst_ref=rows, send_sem=send.at[w], recv_sem=recv.at[w],
                                              device_id=(x, y, 1 - c), device_id_type=MESH)
            rc.start()
            cps.append(rc)
        for w, rc in enumerate(cps):
            h = bufs[w].shape[0] // 2
            other = outs[w].at[pl.ds((1 - c) * h, h), :]
            pltpu.make_async_remote_copy(src_ref=other, dst_ref=other, send_sem=send.at[w], recv_sem=recv.at[w],
                                         device_id=(x, y, 1 - c), device_id_type=MESH).wait_recv()
            rc.wait_send()

    return pl.pallas_call(body, in_specs=[HBM] * n, out_specs=[HBM] * n, out_shape=tuple(SDS(a.shape, a.dtype) for a in bufs),
                          input_output_aliases={w: w for w in range(n)},
                          scratch_shapes=[pltpu.SemaphoreType.DMA((n,))] * 2, name="grad_pair_gather",
                          compiler_params=pltpu.CompilerParams(has_side_effects=True))(*bufs)


def _small_allreduce(vec, name):
    R = vec.shape[0]
    nd = 8

    def body(v_ref, o_ref, buf, send, recv):
        x, y, c = _coords()
        me = 4 * x + 2 * y + c
        buf[me] = v_ref[...]
        cps = []
        for k in range(1, nd):
            kx, ky, kc = (k >> 2) & 1, (k >> 1) & 1, k & 1
            tx = x + kx - 2 * x * kx
            ty = y + ky - 2 * y * ky
            tc = c + kc - 2 * c * kc
            cp = pltpu.make_async_remote_copy(src_ref=v_ref, dst_ref=buf.at[me], send_sem=send.at[k], recv_sem=recv.at[k],
                                              device_id=(tx, ty, tc), device_id_type=MESH)
            cp.start()
            cps.append((cp, 4 * tx + 2 * ty + tc))
        for k, (cp, src) in zip(range(1, nd), cps):
            pltpu.make_async_remote_copy(src_ref=v_ref, dst_ref=buf.at[src], send_sem=send.at[k], recv_sem=recv.at[k],
                                         device_id=(x, y, c), device_id_type=MESH).wait_recv()
        for cp, _ in cps:
            cp.wait_send()
        acc = buf[0]
        for d in range(1, nd):
            acc = acc + buf[d]
        o_ref[...] = acc

    vm = pl.BlockSpec(memory_space=pltpu.VMEM)
    return pl.pallas_call(body, in_specs=[vm], out_specs=vm, out_shape=SDS((R, LANES), f32),
                          scratch_shapes=[pltpu.VMEM((nd, R, LANES), f32), pltpu.SemaphoreType.DMA((nd,)), pltpu.SemaphoreType.DMA((nd,))],
                          name=name, compiler_params=pltpu.CompilerParams(has_side_effects=True))(vec)


def _perm(a, r):
    B, S_, C = a.shape
    return a.reshape(B, S_ // r, r, C).transpose(0, 2, 1, 3)


def _unperm(a):
    B, r, L, C = a.shape
    return a.transpose(0, 2, 1, 3).reshape(B, L * r, C)


def _rows128(a, rows):
    flat = a.reshape(-1)
    return jnp.pad(flat, (0, rows * LANES - flat.shape[0])).reshape(rows, LANES)


GATHER_1 = ("w_conv_out", "w_attn_out", "w_o", "w_ffn_gate")
GATHER_2 = ("w_ffn_up", "w_ffn_down")
REDUCE_A = ("w_ffn_gate", "w_ffn_up", "w_ffn_down")
REDUCE_B = ("w_conv_out", "w_attn_out", "w_o")
REDUCE_C = ("w_in",)


def _step(x, target, norm1_g, gate_b, conv_w, conv_b, conv_ln_g, conv_ln_b, norm2_g, norm_f_g, shards, slots, chip1, core):
    B = x.shape[0]
    T = B * SEQ
    xf = x.reshape(T, D_MODEL)
    tf = target.reshape(T, D_MODEL)

    (w_in,) = _allgather_weights([shards["w_in"]], [slots["w_in"]])
    h = _rms_fwd(xf, norm1_g, "rms1_fwd")
    proj, got1 = _mm("in_proj", h, w_in, grid=(N_CHIPS, T // 512),
                     a_spec=pl.BlockSpec((512, D_MODEL), lambda s, m: (m, 0)),
                     b_spec=pl.BlockSpec((None, D_MODEL, IN_S), lambda s, m: (s, 0, 0)),
                     o_spec=pl.BlockSpec((512, IN_S), lambda s, m: (m, s)), o_shape=(T, IN_W), o_dtype=bf16, dims=NN,
                     comm=_x_gather_ici([shards[n] for n in GATHER_1], [slots[n] for n in GATHER_1]))
    proj3 = proj.reshape(B, SEQ, IN_W)
    proj4 = proj.reshape(B, 1, SEQ, IN_W)

    qkv = [None]
    for g in range(1, len(GROUPS)):
        cols = [proj3[:, :, base + g * GW: base + (g + 1) * GW] for base in (0, ATTN_W, 2 * ATTN_W)]
        qkv.append(_perm(jnp.concatenate(cols, axis=-1), GROUPS[g][1]))

    def qkv_args(g):
        return (proj4, proj4, proj4, 0, 3, 6) if g == 0 else (qkv[g], qkv[g], qkv[g], 0, 1, 2)

    (o4_0, l4_0), full1 = _attn_fwd(0, *qkv_args(0), comm=_x_gather_fwd(list(got1)))
    (o4_1, l4_1), got2 = _attn_fwd(1, *qkv_args(1), comm=_x_gather_ici([shards[n] for n in GATHER_2], [slots[n] for n in GATHER_2]))
    (o4_2, l4_2), full2 = _attn_fwd(2, *qkv_args(2), comm=_x_gather_fwd(list(got2)))
    full = dict(zip(GATHER_1 + GATHER_2, full1 + full2))
    w_conv_out, w_attn_out, w_o = full["w_conv_out"], full["w_attn_out"], full["w_o"]
    w_gate, w_up, w_down = full["w_ffn_gate"], full["w_ffn_up"], full["w_ffn_down"]
    w_conv_out_f = w_conv_out.reshape(D_MODEL, D_MODEL)
    w_o_f = w_o.reshape(D_MODEL, D_MODEL)
    outs = [_unperm(o4).reshape(T, GW) for o4 in (o4_0, o4_1, o4_2)]
    lses = [_unperm(l4).reshape(T, LANES) for l4 in (l4_0, l4_1, l4_2)]
    mix, lse_tot = _attn_mix(outs, lses)
    y_attn = _mm("attn_out", mix, w_attn_out, grid=(N_CHIPS, T // 512),
                 a_spec=pl.BlockSpec((512, GW), lambda s, m: (m, 0)),
                 b_spec=pl.BlockSpec((None, GW, D_MODEL // N_CHIPS), lambda s, m: (s, 0, 0)),
                 o_spec=pl.BlockSpec((512, D_MODEL // N_CHIPS), lambda s, m: (m, s)), o_shape=(T, D_MODEL), o_dtype=bf16,
                 dims=NN, sem=("parallel", "parallel"))

    c1 = _glu_conv_fwd(proj3, conv_w, conv_b).reshape(T, D_MODEL)
    c3 = _ln_silu_fwd(c1, conv_ln_g, conv_ln_b)
    y_conv = _mm_nn_full("conv_out", c3, w_conv_out_f, bf16)

    merged = _merge_fwd(proj, gate_b, y_attn, y_conv)
    x1 = _mm_nn_full("o_proj", merged, w_o_f, f32, res=xf)

    h2 = _rms_fwd(x1, norm2_g, "rms2_fwd")
    fa, fb, ff = _ffn_up(h2, w_gate, w_up)
    x2 = _mm("ffn_down", ff, w_down, grid=(T // 1024, N_CHIPS),
             a_spec=pl.BlockSpec((None, 1024, FF_S), lambda m, s: (s, m, 0)),
             b_spec=pl.BlockSpec((None, FF_S, D_MODEL), lambda m, s: (s, 0, 0)),
             o_spec=pl.BlockSpec((1024, D_MODEL), lambda m, s: (m, 0)), o_shape=(T, D_MODEL), o_dtype=f32, dims=NN,
             acc_shape=(1024, D_MODEL), k_axis=1, res=x1, res_spec=pl.BlockSpec((1024, D_MODEL), lambda m, s: (m, 0)),
             sem=("parallel", "arbitrary"))

    loss, dx2, d_gf = _final_loss(x2, norm_f_g.reshape(1, D_MODEL), tf)

    d_w_down = _mm("d_w_down", ff, dx2, grid=(N_CHIPS, T // 512),
                   a_spec=pl.BlockSpec((None, 512, FF_S), lambda s, k: (s, k, 0)),
                   b_spec=pl.BlockSpec((512, D_MODEL), lambda s, k: (k, 0)),
                   o_spec=pl.BlockSpec((None, FF_S, D_MODEL), lambda s, k: (s, 0, 0)), o_shape=(N_CHIPS, FF_S, D_MODEL),
                   o_dtype=bf16, dims=TN, acc_shape=(FF_S, D_MODEL), k_axis=1, sem=("parallel", "arbitrary"))
    da, db = _ffn_down_bwd(dx2, w_down, fa, fb)

    def d_w_ff(name, dz):
        return _mm(name, h2, dz, grid=(N_CHIPS, T // 512),
                   a_spec=pl.BlockSpec((512, D_MODEL), lambda s, k: (k, 0)),
                   b_spec=pl.BlockSpec((None, 512, FF_S), lambda s, k: (s, k, 0)),
                   o_spec=pl.BlockSpec((None, D_MODEL, FF_S), lambda s, k: (s, 0, 0)), o_shape=(N_CHIPS, D_MODEL, FF_S),
                   o_dtype=bf16, dims=TN, acc_shape=(D_MODEL, FF_S), k_axis=1, sem=("parallel", "arbitrary"))

    d_w_gate = d_w_ff("d_w_gate", da)
    d_w_up = d_w_ff("d_w_up", db)
    part = dict(w_ffn_gate=d_w_gate, w_ffn_up=d_w_up, w_ffn_down=d_w_down)

    def pair_sums(names, tag):
        got = _pair_exchange([part[n] for n in names], f"grad_pair_exchange_{tag}")
        return [_add_pairs(part[n], g_, core, f"pair_sum_{n}") for n, g_ in zip(names, got)]

    sums_a = pair_sums(REDUCE_A, "a")
    dh2 = _ffn_dh2(da, w_gate, db, w_up)
    dx1, d_g2 = _rms_bwd(dh2, x1, norm2_g, dx2, "rms2_bwd")

    d_w_o = _mm_tn_tokens("d_w_o", merged, dx1, bf16).reshape(N_CHIPS, D_MODEL // N_CHIPS, D_MODEL)
    dmerged = _mm_nt_full("d_merged", dx1, w_o_f, bf16)
    dya, dyc, dga, dgc, d_gba, d_gbc = _merge_bwd(dmerged, proj, gate_b, y_attn, y_conv)

    d_w_conv_out = _mm_tn_tokens("d_w_conv_out", c3, dyc, bf16).reshape(N_CHIPS, D_MODEL // N_CHIPS, D_MODEL)
    dc3 = _mm_nt_full("d_c3", dyc, w_conv_out_f, bf16)
    dc1, d_ln_g, d_ln_b = _ln_silu_bwd(dc3, c1, conv_ln_g, conv_ln_b)
    (dua, dub, d_conv_w, d_conv_b), got_a = _glu_conv_bwd(dc1.reshape(B, SEQ, D_MODEL), proj3, conv_w, comm=_x_chip_exchange(sums_a))

    d_w_attn_out = _mm("d_w_attn_out", mix, dya, grid=(N_CHIPS, T // 512),
                       a_spec=pl.BlockSpec((512, GW), lambda s, k: (k, 0)),
                       b_spec=pl.BlockSpec((512, D_MODEL // N_CHIPS), lambda s, k: (k, s)),
                       o_spec=pl.BlockSpec((None, GW, D_MODEL // N_CHIPS), lambda s, k: (s, 0, 0)),
                       o_shape=(N_CHIPS, GW, D_MODEL // N_CHIPS), o_dtype=bf16, dims=TN, acc_shape=(GW, D_MODEL // N_CHIPS),
                       k_axis=1, sem=("parallel", "arbitrary"))
    dmix = _mm("d_mix", dya, w_attn_out, grid=(T // 512, N_CHIPS),
               a_spec=pl.BlockSpec((512, D_MODEL // N_CHIPS), lambda m, s: (m, s)),
               b_spec=pl.BlockSpec((None, GW, D_MODEL // N_CHIPS), lambda m, s: (s, 0, 0)),
               o_spec=pl.BlockSpec((512, GW), lambda m, s: (m, 0)), o_shape=(T, GW), o_dtype=bf16, dims=NT,
               acc_shape=(512, GW), k_axis=1, sem=("parallel", "arbitrary"))
    delta = _attn_delta(dmix, mix)
    dmix3 = dmix.reshape(B, SEQ, GW)
    lse3 = lse_tot.reshape(B, SEQ, LANES)
    delta3 = delta.reshape(B, SEQ, LANES)
    part.update(w_conv_out=d_w_conv_out, w_attn_out=d_w_attn_out, w_o=d_w_o)
    sums_b = pair_sums(REDUCE_B, "b")
    dqs, dks, dvs = [], [], []
    got_b = None
    for g in range(len(GROUPS)):
        r = GROUPS[g][1]
        res = _attn_bwd(g, *qkv_args(g), _perm(dmix3, r), _perm(lse3, r), _perm(delta3, r),
                        comm=_x_chip_exchange(sums_b) if g == 0 else None)
        if g == 0:
            res, got_b = res
        dq4, dk4, dv4 = res
        dqs.append(_unperm(dq4))
        dks.append(_unperm(dk4))
        dvs.append(_unperm(dv4))
    dproj = jnp.concatenate(dqs + dks + dvs + [dua, dub, dga.reshape(B, SEQ, D_MODEL), dgc.reshape(B, SEQ, D_MODEL)],
                            axis=-1).reshape(T, IN_W)

    d_w_in = _mm("d_w_in", h, dproj, grid=(N_CHIPS, T // 512),
                 a_spec=pl.BlockSpec((512, D_MODEL), lambda s, k: (k, 0)),
                 b_spec=pl.BlockSpec((512, IN_S), lambda s, k: (k, s)),
                 o_spec=pl.BlockSpec((None, D_MODEL, IN_S), lambda s, k: (s, 0, 0)), o_shape=(N_CHIPS, D_MODEL, IN_S),
                 o_dtype=bf16, dims=TN, acc_shape=(D_MODEL, IN_S), k_axis=1, sem=("parallel", "arbitrary"))
    part.update(w_in=d_w_in)
    sums_c = pair_sums(REDUCE_C, "c")
    dh, got_c = _mm("d_h", dproj, w_in, grid=(T // 512, N_CHIPS),
                    a_spec=pl.BlockSpec((512, IN_S), lambda m, s: (m, s)),
                    b_spec=pl.BlockSpec((None, D_MODEL, IN_S), lambda m, s: (s, 0, 0)),
                    o_spec=pl.BlockSpec((512, D_MODEL), lambda m, s: (m, 0)), o_shape=(T, D_MODEL), o_dtype=f32, dims=NT,
                    acc_shape=(512, D_MODEL), k_axis=1, comm=_x_chip_exchange(sums_c))
    dx, d_g1 = _rms_bwd(dh, xf, norm1_g, dx1, "rms1_bwd")

    names = REDUCE_A + REDUCE_B + REDUCE_C
    halves = [_sum4(s, g_, chip1, core, f"chip_sum_{n}") for n, s, g_ in zip(names, sums_a + sums_b + sums_c, got_a + got_b + got_c)]
    big = dict(zip(names, _pair_gather(halves)))
    small = dict(norm1_g=d_g1, gate_b=jnp.concatenate([d_gba, d_gbc], axis=-1), conv_b=d_conv_b, conv_ln_g=d_ln_g,
                 conv_ln_b=d_ln_b, norm2_g=d_g2, norm_f_g=d_gf, conv_w=d_conv_w)
    return loss, dx.reshape(B, SEQ, D_MODEL), big, small


BIG = ("w_in", "w_conv_out", "w_attn_out", "w_o", "w_ffn_gate", "w_ffn_up", "w_ffn_down")
SMALL = ("norm1_g", "gate_b", "conv_b", "conv_ln_g", "conv_ln_b", "norm2_g", "norm_f_g")
SMALL_ROWS = {"norm1_g": 8, "gate_b": 16, "conv_b": 8, "conv_ln_g": 8, "conv_ln_b": 8, "norm2_g": 8, "norm_f_g": 8}
LOSS_ROWS = 8
CONVW_ROWS = 32 * D_MODEL // LANES


def kernel(x, norm1_g, w_in, gate_b, conv_w, conv_b, conv_ln_g, conv_ln_b, w_conv_out, w_attn_out, w_o, norm2_g, w_ffn_gate, w_ffn_up, w_ffn_down, norm_f_g, loss_target, m_norm1_g, m_w_in, m_gate_b, m_conv_w, m_conv_b, m_conv_ln_g, m_conv_ln_b, m_w_conv_out, m_w_attn_out, m_w_o, m_norm2_g, m_w_ffn_gate, m_w_ffn_up, m_w_ffn_down, m_norm_f_g, v_norm1_g, v_w_in, v_gate_b, v_conv_w, v_conv_b, v_conv_ln_g, v_conv_ln_b, v_w_conv_out, v_w_attn_out, v_w_o, v_norm2_g, v_w_ffn_gate, v_w_ffn_up, v_w_ffn_down, v_norm_f_g):
    W = dict(norm1_g=norm1_g, w_in=w_in, gate_b=gate_b, conv_w=conv_w, conv_b=conv_b, conv_ln_g=conv_ln_g, conv_ln_b=conv_ln_b,
             w_conv_out=w_conv_out, w_attn_out=w_attn_out, w_o=w_o, norm2_g=norm2_g, w_ffn_gate=w_ffn_gate, w_ffn_up=w_ffn_up,
             w_ffn_down=w_ffn_down, norm_f_g=norm_f_g)
    M = dict(norm1_g=m_norm1_g, w_in=m_w_in, gate_b=m_gate_b, conv_w=m_conv_w, conv_b=m_conv_b, conv_ln_g=m_conv_ln_g,
             conv_ln_b=m_conv_ln_b, w_conv_out=m_w_conv_out, w_attn_out=m_w_attn_out, w_o=m_w_o, norm2_g=m_norm2_g,
             w_ffn_gate=m_w_ffn_gate, w_ffn_up=m_w_ffn_up, w_ffn_down=m_w_ffn_down, norm_f_g=m_norm_f_g)
    V = dict(norm1_g=v_norm1_g, w_in=v_w_in, gate_b=v_gate_b, conv_w=v_conv_w, conv_b=v_conv_b, conv_ln_g=v_conv_ln_g,
             conv_ln_b=v_conv_ln_b, w_conv_out=v_w_conv_out, w_attn_out=v_w_attn_out, w_o=v_w_o, norm2_g=v_norm2_g,
             w_ffn_gate=v_w_ffn_gate, w_ffn_up=v_w_ffn_up, w_ffn_down=v_w_ffn_down, norm_f_g=v_norm_f_g)
    order = list(W)
    shard2d = {n: W[n].reshape(W[n].shape[-2:]) for n in BIG}
    chip = 2 * lax.axis_index("x") + lax.axis_index("y")

    cw = jnp.zeros((CONV_K, D_MODEL), f32)
    cw = lax.dynamic_update_slice(cw, 0.5 * conv_w.reshape(CONV_K, D_MODEL // N_CHIPS), (0, chip * (D_MODEL // N_CHIPS)))
    conv_w_full = _small_allreduce(_rows128(cw, CONVW_ROWS), "conv_w_gather")[: CONV_K * D_MODEL // LANES].reshape(CONV_K, D_MODEL)

    core = lax.axis_index("c").astype(jnp.int32).reshape(1)
    chip1 = chip.astype(jnp.int32).reshape(1)
    shards = {n: shard2d[n].astype(bf16) for n in BIG}
    slots = {n: lax.dynamic_update_slice(lax.empty((N_CHIPS,) + s.shape, bf16), s[None], (chip, 0, 0)) for n, s in shards.items()}

    loss, grad_x, grads, gsmall = _step(x, loss_target, norm1_g, gate_b, conv_w_full, conv_b, conv_ln_g, conv_ln_b, norm2_g,
                                        norm_f_g, shards, slots, chip1, core)

    pieces = [_rows128(loss, LOSS_ROWS)] + [_rows128(gsmall[n], SMALL_ROWS[n]) for n in SMALL] + [_rows128(gsmall["conv_w"], CONVW_ROWS)]
    tot = _small_allreduce(jnp.concatenate(pieces, axis=0), "small_allreduce")
    loss_out = tot[0, 0]
    row = LOSS_ROWS
    for n in SMALL:
        grads[n] = tot[row: row + W[n].size // LANES].reshape(W[n].shape)
        row += SMALL_ROWS[n]
    dcw = tot[row: row + CONV_K * D_MODEL // LANES].reshape(CONV_K, D_MODEL)
    grads["conv_w"] = lax.dynamic_slice(dcw, (0, chip * (D_MODEL // N_CHIPS)), (CONV_K, D_MODEL // N_CHIPS))

    delta, new_m, new_v = {}, {}, {}
    for n in BIG:
        d, nm, nv = _adamw(shard2d[n], grads[n], M[n].reshape(shard2d[n].shape), V[n].reshape(shard2d[n].shape), f"adamw_{n}")
        delta[n], new_m[n], new_v[n] = d.reshape(W[n].shape), nm.reshape(W[n].shape), nv.reshape(W[n].shape)
        grads[n] = grads[n].reshape(W[n].shape)

    def pack(src):
        return jnp.concatenate([_rows128(src[n], SMALL_ROWS[n]) for n in SMALL], axis=0)

    d, nm, nv = _adamw(pack(W), pack(grads), pack(M), pack(V), "adamw_small")
    row = 0
    for n in SMALL:
        k = W[n].size // LANES
        delta[n], new_m[n], new_v[n] = (t[row: row + k].reshape(W[n].shape) for t in (d, nm, nv))
        row += SMALL_ROWS[n]

    def pad32(a):
        return jnp.pad(a.reshape(CONV_K, D_MODEL // N_CHIPS), ((0, 1), (0, 0)))

    d, nm, nv = _adamw(pad32(conv_w), pad32(grads["conv_w"]), pad32(m_conv_w), pad32(v_conv_w), "adamw_conv_w")
    delta["conv_w"], new_m["conv_w"], new_v["conv_w"] = (t[:CONV_K].reshape(conv_w.shape) for t in (d, nm, nv))
    grads["conv_w"] = grads["conv_w"].reshape(conv_w.shape)

    return (loss_out, grad_x, *[grads[n] for n in order], *[delta[n] for n in order],
            *[new_m[n] for n in order], *[new_v[n] for n in order])
```

```python
import functools
import math
from typing import Callable, NamedTuple

import numpy as np
import jax
import jax.numpy as jnp
from jax import lax
from jax.experimental import pallas as pl
from jax.experimental.pallas import tpu as pltpu

f32 = jnp.float32
bf16 = jnp.bfloat16
SDS = jax.ShapeDtypeStruct
MESH = pl.DeviceIdType.MESH

D_MODEL = 1024
SEQ = 2048
HEAD_DIM = 64
HEADS = 8
GROUPS = ((128, 1), (512, 4), (2048, 16))
GW = HEADS * HEAD_DIM
ATTN_W = len(GROUPS) * GW
Q_BLOCK = 128
CONV_K = 31
D_FF = 2816
IN_W = 3 * ATTN_W + 2 * D_MODEL + 2 * D_MODEL
N_CHIPS = 4
IN_S = IN_W // N_CHIPS
FF_S = D_FF // N_CHIPS
RMS_EPS = 1e-6
LN_EPS = 1e-5
LR, B1, B2, ADAM_EPS, WD, STEP = 0.001, 0.9, 0.999, 1e-08, 0.01, 10
NEG = -1e30
LANES = 128
VMEM_LIMIT = 48 * 2 ** 20
CB = 512
UA_CB, UB_CB, GA_CB, GC_CB = 9, 11, 13, 15


def _alibi_slope_list(n):
    def pow2(m):
        start = 2.0 ** (-8.0 / m)
        return [start ** (i + 1) for i in range(m)]
    if math.log2(n).is_integer():
        return pow2(n)
    c = 2 ** math.floor(math.log2(n))
    return pow2(c) + _alibi_slope_list(2 * c)[0::2][: n - c]


_SLOPES = np.asarray(sorted(_alibi_slope_list(len(GROUPS) * HEADS), reverse=True), dtype=np.float32).reshape(len(GROUPS), HEADS)


def _cp(sem=None, vmem=VMEM_LIMIT):
    return pltpu.CompilerParams(dimension_semantics=sem, vmem_limit_bytes=vmem)


def _sigmoid(x):
    return 1.0 / (1.0 + jnp.exp(-x))


HBM = pl.BlockSpec(memory_space=pl.ANY)


class _Exchange(NamedTuple):
    ins: list
    out_shape: list
    scratch: list
    aliases: dict
    start: Callable
    wait: Callable


def _launch(body, *, name, grid, in_specs, out_specs, out_shape, args, scratch_shapes=(), sem=None, comm=None):
    if comm is None:
        return pl.pallas_call(body, grid=grid, in_specs=in_specs, out_specs=out_specs, out_shape=out_shape,
                              scratch_shapes=list(scratch_shapes), name=name, compiler_params=_cp(sem))(*args)
    multi = isinstance(out_shape, (tuple, list))
    m_out = list(out_shape) if multi else [out_shape]
    m_ospec = list(out_specs) if multi else [out_specs]
    n_in, n_out, n_scr = len(in_specs), len(m_out), len(scratch_shapes)
    nc_in, nc_out = len(comm.ins), len(comm.out_shape)

    def hosted(*refs):
        bounds = np.cumsum([0, n_in, nc_in, n_out, nc_out, n_scr])
        mi, ci, mo, co, ms = (refs[a:b] for a, b in zip(bounds[:-1], bounds[1:]))
        cs = refs[bounds[-1]:]
        ids = [pl.program_id(a) for a in range(len(grid))]
        first = functools.reduce(jnp.logical_and, [i == 0 for i in ids])
        last = functools.reduce(jnp.logical_and, [i == g - 1 for i, g in zip(ids, grid)])

        @pl.when(first)
        def _():
            comm.start(ci, co, cs)

        body(*mi, *mo, *ms)

        @pl.when(last)
        def _():
            comm.wait(ci, co, cs)

    res = pl.pallas_call(
        hosted, grid=grid, in_specs=list(in_specs) + [HBM] * nc_in, out_specs=m_ospec + [HBM] * nc_out,
        out_shape=tuple(m_out) + tuple(comm.out_shape), scratch_shapes=list(scratch_shapes) + list(comm.scratch),
        input_output_aliases={n_in + i: n_out + o for i, o in comm.aliases.items()}, name=name + "_comm",
        compiler_params=pltpu.CompilerParams(dimension_semantics=("arbitrary",) * len(grid), vmem_limit_bytes=VMEM_LIMIT,
                                             has_side_effects=True))(*args, *comm.ins)
    return (tuple(res[:n_out]) if multi else res[0]), tuple(res[n_out:])


def _run_exchange(ex, name):
    n_in = len(ex.ins)

    def body(*refs):
        ins, outs, sems = refs[:n_in], refs[n_in:n_in + len(ex.out_shape)], refs[n_in + len(ex.out_shape):]
        ex.start(ins, outs, sems)
        ex.wait(ins, outs, sems)

    return pl.pallas_call(body, in_specs=[HBM] * n_in, out_specs=[HBM] * len(ex.out_shape), out_shape=tuple(ex.out_shape),
                          scratch_shapes=list(ex.scratch), input_output_aliases=dict(ex.aliases), name=name,
                          compiler_params=pltpu.CompilerParams(has_side_effects=True))(*ex.ins)


def _rms_fwd(x, g, name):
    T = x.shape[0]
    tm = 512

    def body(x_ref, g_ref, o_ref):
        xv = x_ref[...]
        r = lax.rsqrt(jnp.mean(xv * xv, axis=-1, keepdims=True) + RMS_EPS)
        o_ref[...] = (xv * r * g_ref[...]).astype(o_ref.dtype)

    row = pl.BlockSpec((tm, D_MODEL), lambda i: (i, 0))
    vec = pl.BlockSpec((1, D_MODEL), lambda i: (0, 0))
    return pl.pallas_call(body, grid=(T // tm,), in_specs=[row, vec], out_specs=row,
                          out_shape=SDS((T, D_MODEL), bf16), name=name, compiler_params=_cp(("parallel",)))(x, g)


def _rms_bwd(dy, x, g, dres, name):
    T = x.shape[0]
    tm = 512

    def body(dy_ref, x_ref, g_ref, r_ref, dx_ref, dg_ref):
        xv = x_ref[...]
        r = lax.rsqrt(jnp.mean(xv * xv, axis=-1, keepdims=True) + RMS_EPS)
        xh = xv * r
        dyv = dy_ref[...]
        dxh = dyv * g_ref[...]
        dx_ref[...] = r_ref[...] + r * (dxh - xh * jnp.mean(dxh * xh, axis=-1, keepdims=True))
        part = jnp.sum(dyv * xh, axis=0, keepdims=True)

        @pl.when(pl.program_id(0) == 0)
        def _():
            dg_ref[...] = part

        @pl.when(pl.program_id(0) > 0)
        def _():
            dg_ref[...] += part

    row = pl.BlockSpec((tm, D_MODEL), lambda i: (i, 0))
    vec = pl.BlockSpec((1, D_MODEL), lambda i: (0, 0))
    return pl.pallas_call(body, grid=(T // tm,), in_specs=[row, row, vec, row], out_specs=[row, vec],
                          out_shape=(SDS((T, D_MODEL), f32), SDS((1, D_MODEL), f32)), name=name,
                          compiler_params=_cp(("arbitrary",)))(dy, x, g, dres)


def _final_loss(x2, gf, target):
    T = x2.shape[0]
    tm = 512

    def body(x_ref, g_ref, t_ref, loss_ref, dx_ref, dg_ref):
        xv = x_ref[...]
        gv = g_ref[...]
        r = lax.rsqrt(jnp.mean(xv * xv, axis=-1, keepdims=True) + RMS_EPS)
        xh = xv * r
        e = xh * gv - t_ref[...]
        part_l = jnp.broadcast_to(0.5 * jnp.sum(jnp.mean(e * e, axis=-1, keepdims=True), axis=0, keepdims=True), (1, LANES))
        dy = e * (1.0 / D_MODEL)
        dxh = dy * gv
        dx_ref[...] = r * (dxh - xh * jnp.mean(dxh * xh, axis=-1, keepdims=True))
        part_g = jnp.sum(dy * xh, axis=0, keepdims=True)

        @pl.when(pl.program_id(0) == 0)
        def _():
            loss_ref[...] = part_l
            dg_ref[...] = part_g

        @pl.when(pl.program_id(0) > 0)
        def _():
            loss_ref[...] += part_l
            dg_ref[...] += part_g

    row = pl.BlockSpec((tm, D_MODEL), lambda i: (i, 0))
    vec = pl.BlockSpec((1, D_MODEL), lambda i: (0, 0))
    lvec = pl.BlockSpec((1, LANES), lambda i: (0, 0))
    return pl.pallas_call(body, grid=(T // tm,), in_specs=[row, vec, row], out_specs=[lvec, row, vec],
                          out_shape=(SDS((1, LANES), f32), SDS((T, D_MODEL), f32), SDS((1, D_MODEL), f32)),
                          name="final_loss", compiler_params=_cp(("arbitrary",)))(x2, gf, target)


def _ln_silu_fwd(c1, g, b):
    T = c1.shape[0]
    tm = 512

    def body(c_ref, g_ref, b_ref, o_ref):
        cv = c_ref[...]
        mu = jnp.mean(cv, axis=-1, keepdims=True)
        cc = cv - mu
        var = jnp.mean(cc * cc, axis=-1, keepdims=True)
        c2 = cc * lax.rsqrt(var + LN_EPS) * g_ref[...] + b_ref[...]
        o_ref[...] = (c2 * _sigmoid(c2)).astype(o_ref.dtype)

    row = pl.BlockSpec((tm, D_MODEL), lambda i: (i, 0))
    vec = pl.BlockSpec((1, D_MODEL), lambda i: (0, 0))
    return pl.pallas_call(body, grid=(T // tm,), in_specs=[row, vec, vec], out_specs=row,
                          out_shape=SDS((T, D_MODEL), bf16), name="ln_silu_fwd", compiler_params=_cp(("parallel",)))(c1, g, b)


def _ln_silu_bwd(dc3, c1, g, b):
    T = c1.shape[0]
    tm = 512

    def body(d_ref, c_ref, g_ref, b_ref, dc_ref, dg_ref, db_ref):
        cv = c_ref[...]
        gv = g_ref[...]
        mu = jnp.mean(cv, axis=-1, keepdims=True)
        cc = cv - mu
        var = jnp.mean(cc * cc, axis=-1, keepdims=True)
        rs = lax.rsqrt(var + LN_EPS)
        xh = cc * rs
        c2 = xh * gv + b_ref[...]
        sg = _sigmoid(c2)
        dc2 = d_ref[...].astype(f32) * (sg * (1.0 + c2 * (1.0 - sg)))
        dxh = dc2 * gv
        dc_ref[...] = rs * (dxh - jnp.mean(dxh, axis=-1, keepdims=True) - xh * jnp.mean(dxh * xh, axis=-1, keepdims=True))
        pg = jnp.sum(dc2 * xh, axis=0, keepdims=True)
        pb = jnp.sum(dc2, axis=0, keepdims=True)

        @pl.when(pl.program_id(0) == 0)
        def _():
            dg_ref[...] = pg
            db_ref[...] = pb

        @pl.when(pl.program_id(0) > 0)
        def _():
            dg_ref[...] += pg
            db_ref[...] += pb

    row = pl.BlockSpec((tm, D_MODEL), lambda i: (i, 0))
    vec = pl.BlockSpec((1, D_MODEL), lambda i: (0, 0))
    return pl.pallas_call(body, grid=(T // tm,), in_specs=[row, row, vec, vec], out_specs=[row, vec, vec],
                          out_shape=(SDS((T, D_MODEL), f32), SDS((1, D_MODEL), f32), SDS((1, D_MODEL), f32)),
                          name="ln_silu_bwd", compiler_params=_cp(("arbitrary",)))(dc3, c1, g, b)


NN = (((1,), (0,)), ((), ()))
NT = (((1,), (1,)), ((), ()))
TN = (((0,), (0,)), ((), ()))


def _mm(name, a, b, *, grid, a_spec, b_spec, o_spec, o_shape, o_dtype, dims, acc_shape=None, k_axis=None,
        res=None, res_spec=None, sem=None, comm=None):
    nk = 1 if k_axis is None else grid[k_axis]

    def body(*refs):
        if res is None:
            a_ref, b_ref, o_ref = refs[:3]
            r_ref, scr = None, refs[3:]
        else:
            a_ref, b_ref, r_ref, o_ref = refs[:4]
            scr = refs[4:]
        p = lax.dot_general(a_ref[...].astype(bf16), b_ref[...].astype(bf16), dims, preferred_element_type=f32)
        if nk == 1:
            if r_ref is not None:
                p = p + r_ref[...]
            o_ref[...] = p.astype(o_dtype)
            return
        acc = scr[0]
        k = pl.program_id(k_axis)

        @pl.when(k == 0)
        def _():
            acc[...] = p

        @pl.when(k > 0)
        def _():
            acc[...] += p

        @pl.when(k == nk - 1)
        def _():
            t = acc[...]
            if r_ref is not None:
                t = t + r_ref[...]
            o_ref[...] = t.astype(o_dtype)

    ins = [a, b] + ([] if res is None else [res])
    specs = [a_spec, b_spec] + ([] if res is None else [res_spec])
    scratch = [] if nk == 1 else [pltpu.VMEM(acc_shape, f32)]
    return _launch(body, name=name, grid=grid, in_specs=specs, out_specs=o_spec, out_shape=SDS(o_shape, o_dtype),
                   args=ins, scratch_shapes=scratch, sem=sem, comm=comm)


def _mm_nn_full(name, a, b, o_dtype, res=None, tm=512):
    T, K = a.shape
    N = b.shape[1]
    return _mm(name, a, b, grid=(T // tm,), a_spec=pl.BlockSpec((tm, K), lambda i: (i, 0)),
               b_spec=pl.BlockSpec((K, N), lambda i: (0, 0)), o_spec=pl.BlockSpec((tm, N), lambda i: (i, 0)),
               o_shape=(T, N), o_dtype=o_dtype, dims=NN, res=res,
               res_spec=None if res is None else pl.BlockSpec((tm, N), lambda i: (i, 0)), sem=("parallel",))


def _mm_nt_full(name, a, b, o_dtype, tm=512):
    T, N = a.shape
    K = b.shape[0]
    return _mm(name, a, b, grid=(T // tm,), a_spec=pl.BlockSpec((tm, N), lambda i: (i, 0)),
               b_spec=pl.BlockSpec((K, N), lambda i: (0, 0)), o_spec=pl.BlockSpec((tm, K), lambda i: (i, 0)),
               o_shape=(T, K), o_dtype=o_dtype, dims=NT, sem=("parallel",))


def _mm_tn_tokens(name, a, b, o_dtype, tk=1024):
    T, K = a.shape
    N = b.shape[1]
    return _mm(name, a, b, grid=(T // tk,), a_spec=pl.BlockSpec((tk, K), lambda k: (k, 0)),
               b_spec=pl.BlockSpec((tk, N), lambda k: (k, 0)), o_spec=pl.BlockSpec((K, N), lambda k: (0, 0)),
               o_shape=(K, N), o_dtype=o_dtype, dims=TN, acc_shape=(K, N), k_axis=0, sem=("arbitrary",))


def _ffn_up(h2, wg, wu):
    T = h2.shape[0]
    tm = 512

    def body(h_ref, wg_ref, wu_ref, a_ref, b_ref, f_ref):
        hv = h_ref[...]
        av = jnp.dot(hv, wg_ref[...], preferred_element_type=f32)
        bv = jnp.dot(hv, wu_ref[...], preferred_element_type=f32)
        a_ref[...] = av.astype(bf16)
        b_ref[...] = bv.astype(bf16)
        f_ref[...] = (av * _sigmoid(av) * bv).astype(bf16)

    wspec = pl.BlockSpec((None, D_MODEL, FF_S), lambda s, m: (s, 0, 0))
    ospec = pl.BlockSpec((None, tm, FF_S), lambda s, m: (s, m, 0))
    osd = SDS((N_CHIPS, T, FF_S), bf16)
    return pl.pallas_call(body, grid=(N_CHIPS, T // tm),
                          in_specs=[pl.BlockSpec((tm, D_MODEL), lambda s, m: (m, 0)), wspec, wspec],
                          out_specs=[ospec, ospec, ospec], out_shape=(osd, osd, osd), name="ffn_up",
                          compiler_params=_cp(("parallel", "parallel")))(h2, wg, wu)


def _ffn_down_bwd(dx2, wd, a, b):
    T = dx2.shape[0]
    tm = 512

    def body(d_ref, w_ref, a_ref, b_ref, da_ref, db_ref):
        df = lax.dot_general(d_ref[...].astype(bf16), w_ref[...], NT, preferred_element_type=f32)
        av = a_ref[...].astype(f32)
        sg = _sigmoid(av)
        da_ref[...] = (df * b_ref[...].astype(f32) * (sg * (1.0 + av * (1.0 - sg)))).astype(bf16)
        db_ref[...] = (df * av * sg).astype(bf16)

    aspec = pl.BlockSpec((None, tm, FF_S), lambda s, m: (s, m, 0))
    osd = SDS((N_CHIPS, T, FF_S), bf16)
    return pl.pallas_call(body, grid=(N_CHIPS, T // tm),
                          in_specs=[pl.BlockSpec((tm, D_MODEL), lambda s, m: (m, 0)),
                                    pl.BlockSpec((None, FF_S, D_MODEL), lambda s, m: (s, 0, 0)), aspec, aspec],
                          out_specs=[aspec, aspec], out_shape=(osd, osd), name="ffn_down_bwd",
                          compiler_params=_cp(("parallel", "parallel")))(dx2, wd, a, b)


def _ffn_dh2(da, wg, db, wu):
    T = da.shape[1]
    tm = 1024

    def body(da_ref, wg_ref, db_ref, wu_ref, o_ref, acc):
        p = lax.dot_general(da_ref[...], wg_ref[...], NT, preferred_element_type=f32)
        p = p + lax.dot_general(db_ref[...], wu_ref[...], NT, preferred_element_type=f32)
        s = pl.program_id(1)

        @pl.when(s == 0)
        def _():
            acc[...] = p

        @pl.when(s > 0)
        def _():
            acc[...] += p

        @pl.when(s == N_CHIPS - 1)
        def _():
            o_ref[...] = acc[...]

    aspec = pl.BlockSpec((None, tm, FF_S), lambda m, s: (s, m, 0))
    wspec = pl.BlockSpec((None, D_MODEL, FF_S), lambda m, s: (s, 0, 0))
    return pl.pallas_call(body, grid=(T // tm, N_CHIPS), in_specs=[aspec, wspec, aspec, wspec],
                          out_specs=pl.BlockSpec((tm, D_MODEL), lambda m, s: (m, 0)), out_shape=SDS((T, D_MODEL), f32),
                          scratch_shapes=[pltpu.VMEM((tm, D_MODEL), f32)], name="ffn_dh2",
                          compiler_params=_cp(("parallel", "arbitrary")))(da, wg, db, wu)


def _merge_fwd(proj, gate_b, ya, yc):
    T = proj.shape[0]
    tm = 512

    def body(ga_ref, gc_ref, ba_ref, bc_ref, ya_ref, yc_ref, o_ref):
        sa = _sigmoid(ga_ref[...].astype(f32) + ba_ref[...])
        sc = _sigmoid(gc_ref[...].astype(f32) + bc_ref[...])
        o_ref[...] = (sa * ya_ref[...].astype(f32) + sc * yc_ref[...].astype(f32)).astype(bf16)

    blk = pl.BlockSpec((tm, CB), lambda i, j: (i, j))
    return pl.pallas_call(
        body, grid=(T // tm, 2),
        in_specs=[pl.BlockSpec((tm, CB), lambda i, j: (i, GA_CB + j)), pl.BlockSpec((tm, CB), lambda i, j: (i, GC_CB + j)),
                  pl.BlockSpec((1, CB), lambda i, j: (0, j)), pl.BlockSpec((1, CB), lambda i, j: (0, 2 + j)), blk, blk],
        out_specs=blk, out_shape=SDS((T, D_MODEL), bf16), name="merge_fwd",
        compiler_params=_cp(("parallel", "parallel")))(proj, proj, gate_b, gate_b, ya, yc)


def _merge_bwd(dm, proj, gate_b, ya, yc):
    T = proj.shape[0]
    tm = 512

    def body(dm_ref, ga_ref, gc_ref, ba_ref, bc_ref, ya_ref, yc_ref, dya_ref, dyc_ref, dga_ref, dgc_ref, dba_ref, dbc_ref):
        dmv = dm_ref[...].astype(f32)
        sa = _sigmoid(ga_ref[...].astype(f32) + ba_ref[...])
        sc = _sigmoid(gc_ref[...].astype(f32) + bc_ref[...])
        dya_ref[...] = (dmv * sa).astype(bf16)
        dyc_ref[...] = (dmv * sc).astype(bf16)
        dga = dmv * ya_ref[...].astype(f32) * sa * (1.0 - sa)
        dgc = dmv * yc_ref[...].astype(f32) * sc * (1.0 - sc)
        dga_ref[...] = dga.astype(bf16)
        dgc_ref[...] = dgc.astype(bf16)
        pa = jnp.sum(dga, axis=0, keepdims=True)
        pc = jnp.sum(dgc, axis=0, keepdims=True)

        @pl.when(pl.program_id(1) == 0)
        def _():
            dba_ref[...] = pa
            dbc_ref[...] = pc

        @pl.when(pl.program_id(1) > 0)
        def _():
            dba_ref[...] += pa
            dbc_ref[...] += pc

    blk = pl.BlockSpec((tm, CB), lambda j, i: (i, j))
    vec = pl.BlockSpec((1, CB), lambda j, i: (0, j))
    big = SDS((T, D_MODEL), bf16)
    small = SDS((1, D_MODEL), f32)
    return pl.pallas_call(
        body, grid=(2, T // tm),
        in_specs=[blk, pl.BlockSpec((tm, CB), lambda j, i: (i, GA_CB + j)), pl.BlockSpec((tm, CB), lambda j, i: (i, GC_CB + j)),
                  vec, pl.BlockSpec((1, CB), lambda j, i: (0, 2 + j)), blk, blk],
        out_specs=[blk, blk, blk, blk, vec, vec], out_shape=(big, big, big, big, small, small), name="merge_bwd",
        compiler_params=_cp(("parallel", "arbitrary")))(dm, proj, proj, gate_b, gate_b, ya, yc)


CONV_TS = 256
CONV_HALO = 32
CONV_RC = 64
CONV_WIN = CONV_TS + CONV_HALO
SUBLANES = 8


def _fill_shifted(win, sh):
    for b in range(1, SUBLANES):
        sh[b - 1] = win[pl.ds(b, CONV_WIN - SUBLANES), :]


def _rows_at(win, sh, row):
    a, b = divmod(row, SUBLANES)
    if b == 0:
        return win[pl.ds(row, CONV_RC), :]
    return sh[b - 1, pl.ds(a * SUBLANES, CONV_RC), :]


def _glu_conv_fwd(proj3, w, bias):
    B = proj3.shape[0]
    nt = SEQ // CONV_TS
    hb = CONV_TS // CONV_HALO

    def body(ua_ref, ub_ref, ha_ref, hb_ref, w_ref, b_ref, o_ref, win, sh):
        i = pl.program_id(2)
        c0 = ua_ref[...].astype(f32) * _sigmoid(ub_ref[...].astype(f32))
        halo = ha_ref[...].astype(f32) * _sigmoid(hb_ref[...].astype(f32))
        win[0:CONV_HALO, :] = jnp.where(i > 0, halo, 0.0)
        win[CONV_HALO:, :] = c0
        _fill_shifted(win, sh)
        for r0 in range(0, CONV_TS, CONV_RC):
            acc = jnp.zeros((CONV_RC, CB), f32) + b_ref[...]
            for k in range(CONV_K):
                acc = acc + _rows_at(win, sh, r0 + CONV_HALO - (CONV_K - 1) + k) * w_ref[k:k + 1, :]
            o_ref[r0:r0 + CONV_RC, :] = acc

    def cur(cb):
        return pl.BlockSpec((None, CONV_TS, CB), lambda b, j, i: (b, i, cb + j))

    def prev(cb):
        return pl.BlockSpec((None, CONV_HALO, CB), lambda b, j, i: (b, jnp.maximum(i * hb - 1, 0), cb + j))

    return pl.pallas_call(
        body, grid=(B, 2, nt),
        in_specs=[cur(UA_CB), cur(UB_CB), prev(UA_CB), prev(UB_CB),
                  pl.BlockSpec((CONV_K, CB), lambda b, j, i: (0, j)), pl.BlockSpec((1, CB), lambda b, j, i: (0, j))],
        out_specs=pl.BlockSpec((None, CONV_TS, CB), lambda b, j, i: (b, i, j)),
        out_shape=SDS((B, SEQ, D_MODEL), f32),
        scratch_shapes=[pltpu.VMEM((CONV_WIN, CB), f32), pltpu.VMEM((SUBLANES - 1, CONV_WIN - SUBLANES, CB), f32)],
        name="glu_conv_fwd", compiler_params=_cp(("parallel", "parallel", "parallel")))(proj3, proj3, proj3, proj3, w, bias)


def _glu_conv_bwd(dc1, proj3, w, comm=None):
    B = proj3.shape[0]
    nt = SEQ // CONV_TS
    hb = CONV_TS // CONV_HALO

    def body(d_ref, dn_ref, ua_ref, ub_ref, ha_ref, hb_ref, w_ref, dua_ref, dub_ref, dw_ref, db_ref, winc, wind, accw, shc, shd):
        b = pl.program_id(1)
        i = pl.program_id(2)
        first = jnp.logical_and(b == 0, i == 0)
        last = jnp.logical_and(b == B - 1, i == nt - 1)

        @pl.when(first)
        def _():
            accw[...] = jnp.zeros_like(accw)
            db_ref[...] = jnp.zeros_like(db_ref)

        halo = ha_ref[...].astype(f32) * _sigmoid(hb_ref[...].astype(f32))
        winc[0:CONV_HALO, :] = jnp.where(i > 0, halo, 0.0)
        winc[CONV_HALO:, :] = ua_ref[...].astype(f32) * _sigmoid(ub_ref[...].astype(f32))
        wind[0:CONV_TS, :] = d_ref[...]
        wind[CONV_TS:, :] = jnp.where(i < nt - 1, dn_ref[...], 0.0)
        db_ref[...] += jnp.sum(d_ref[...], axis=0, keepdims=True)
        _fill_shifted(winc, shc)
        _fill_shifted(wind, shd)
        for r0 in range(0, CONV_TS, CONV_RC):
            dc0 = jnp.zeros((CONV_RC, CB), f32)
            for k in range(CONV_K):
                dc0 = dc0 + _rows_at(wind, shd, r0 + (CONV_K - 1) - k) * w_ref[k:k + 1, :]
            uav = ua_ref[r0:r0 + CONV_RC, :].astype(f32)
            sg = _sigmoid(ub_ref[r0:r0 + CONV_RC, :].astype(f32))
            dua_ref[r0:r0 + CONV_RC, :] = (dc0 * sg).astype(bf16)
            dub_ref[r0:r0 + CONV_RC, :] = (dc0 * uav * sg * (1.0 - sg)).astype(bf16)
            dv = wind[r0:r0 + CONV_RC, :]
            for k in range(CONV_K):
                prod = dv * _rows_at(winc, shc, r0 + CONV_HALO - (CONV_K - 1) + k)
                accw[k] += jnp.sum(prod.reshape(CONV_RC // 8, 8, CB), axis=0)

        @pl.when(last)
        def _():
            for k in range(CONV_K):
                dw_ref[k:k + 1, :] = jnp.sum(accw[k], axis=0, keepdims=True)
            dw_ref[CONV_K:, :] = jnp.zeros((CONV_HALO - CONV_K, CB), f32)

    def cur(cb):
        return pl.BlockSpec((None, CONV_TS, CB), lambda j, b, i: (b, i, cb + j))

    def prev(cb):
        return pl.BlockSpec((None, CONV_HALO, CB), lambda j, b, i: (b, jnp.maximum(i * hb - 1, 0), cb + j))

    nxt = pl.BlockSpec((None, CONV_HALO, CB), lambda j, b, i: (b, jnp.minimum((i + 1) * hb, SEQ // CONV_HALO - 1), j))
    big = SDS((B, SEQ, D_MODEL), bf16)
    return _launch(
        body, name="glu_conv_bwd", grid=(2, B, nt),
        in_specs=[cur(0), nxt, cur(UA_CB), cur(UB_CB), prev(UA_CB), prev(UB_CB), pl.BlockSpec((CONV_K, CB), lambda j, b, i: (0, j))],
        out_specs=[cur(0), cur(0), pl.BlockSpec((CONV_HALO, CB), lambda j, b, i: (0, j)), pl.BlockSpec((1, CB), lambda j, b, i: (0, j))],
        out_shape=(big, big, SDS((CONV_HALO, D_MODEL), f32), SDS((1, D_MODEL), f32)),
        args=(dc1, dc1, proj3, proj3, proj3, proj3, w),
        scratch_shapes=[pltpu.VMEM((CONV_WIN, CB), f32), pltpu.VMEM((CONV_WIN, CB), f32), pltpu.VMEM((CONV_K, SUBLANES, CB), f32),
                        pltpu.VMEM((SUBLANES - 1, CONV_WIN - SUBLANES, CB), f32),
                        pltpu.VMEM((SUBLANES - 1, CONV_WIN - SUBLANES, CB), f32)],
        sem=("parallel", "arbitrary", "arbitrary"), comm=comm)


def _band(first, dil):
    kw = Q_BLOCK if first else 2 * Q_BLOCK
    qi = lax.broadcasted_iota(jnp.int32, (Q_BLOCK, kw), 0)
    kj = lax.broadcasted_iota(jnp.int32, (Q_BLOCK, kw), 1)
    rel = qi - kj + (0 if first else Q_BLOCK)
    valid = jnp.logical_and(rel >= 0, rel <= Q_BLOCK)
    return valid, rel.astype(f32) * float(dil)


def _scores(q, k, valid, dist, slope):
    s = lax.dot_general(q, k, NT, preferred_element_type=f32) * (HEAD_DIM ** -0.5) - slope * dist
    return jnp.where(valid, s, NEG)


def _attn_fwd(g, q4, k4, v4, qcb, kcb, vcb, comm=None):
    _, dil = GROUPS[g]
    B, r, L, _ = q4.shape
    nb = L // Q_BLOCK
    slopes = [float(s) for s in _SLOPES[g]]

    def body(q_ref, k_ref, v_ref, o_ref, lse_ref):
        lane = lax.broadcasted_iota(jnp.int32, (Q_BLOCK, LANES), 1)

        def block(n, first):
            q0 = 0 if first else pl.multiple_of(n * Q_BLOCK, Q_BLOCK)
            k0 = 0 if first else pl.multiple_of((n - 1) * Q_BLOCK, Q_BLOCK)
            kw = Q_BLOCK if first else 2 * Q_BLOCK
            valid, dist = _band(first, dil)
            st = jnp.zeros((Q_BLOCK, LANES), f32)
            for h in range(HEADS):
                cols = slice(h * HEAD_DIM, (h + 1) * HEAD_DIM)
                s = _scores(q_ref[pl.ds(q0, Q_BLOCK), cols], k_ref[pl.ds(k0, kw), cols], valid, dist, slopes[h])
                m = jnp.max(s, axis=-1, keepdims=True)
                p = jnp.exp(s - m)
                den = jnp.sum(p, axis=-1, keepdims=True)
                o = jnp.dot(p.astype(bf16), v_ref[pl.ds(k0, kw), cols], preferred_element_type=f32) / den
                o_ref[pl.ds(q0, Q_BLOCK), cols] = o.astype(bf16)
                st = jnp.where(lane == h, m + jnp.log(den), st)
            lse_ref[pl.ds(q0, Q_BLOCK), :] = st

        block(0, True)
        if nb > 1:
            def step(n, carry):
                block(n, False)
                return carry
            lax.fori_loop(1, nb, step, 0)

    def spec(cb):
        return pl.BlockSpec((None, None, L, GW), lambda b, c: (b, c, 0, cb))

    return _launch(
        body, name=f"attn_fwd_g{g}", grid=(B, r), in_specs=[spec(qcb), spec(kcb), spec(vcb)],
        out_specs=[spec(0), pl.BlockSpec((None, None, L, LANES), lambda b, c: (b, c, 0, 0))],
        out_shape=(SDS((B, r, L, GW), bf16), SDS((B, r, L, LANES), f32)), args=(q4, k4, v4),
        sem=("parallel", "parallel"), comm=comm)


def _attn_bwd(g, q4, k4, v4, qcb, kcb, vcb, do4, lse4, dl4, comm=None):
    _, dil = GROUPS[g]
    B, r, L, _ = q4.shape
    nb = L // Q_BLOCK
    slopes = [float(s) for s in _SLOPES[g]]
    scale = HEAD_DIM ** -0.5

    def body(q_ref, k_ref, v_ref, do_ref, lse_ref, dl_ref, dq_ref, dk_ref, dv_ref, dk_acc, dv_acc):
        dk_acc[...] = jnp.zeros_like(dk_acc)
        dv_acc[...] = jnp.zeros_like(dv_acc)

        def block(n, first):
            q0 = 0 if first else pl.multiple_of(n * Q_BLOCK, Q_BLOCK)
            k0 = 0 if first else pl.multiple_of((n - 1) * Q_BLOCK, Q_BLOCK)
            kw = Q_BLOCK if first else 2 * Q_BLOCK
            valid, dist = _band(first, dil)
            for h in range(HEADS):
                cols = slice(h * HEAD_DIM, (h + 1) * HEAD_DIM)
                qv = q_ref[pl.ds(q0, Q_BLOCK), cols]
                kv = k_ref[pl.ds(k0, kw), cols]
                vv = v_ref[pl.ds(k0, kw), cols]
                dov = do_ref[pl.ds(q0, Q_BLOCK), cols]
                s = _scores(qv, kv, valid, dist, slopes[h])
                p = jnp.exp(s - lse_ref[pl.ds(q0, Q_BLOCK), h:h + 1])
                dp = lax.dot_general(dov, vv, NT, preferred_element_type=f32)
                ds = (p * (dp - dl_ref[pl.ds(q0, Q_BLOCK), h:h + 1])).astype(bf16)
                dq_ref[pl.ds(q0, Q_BLOCK), cols] = (jnp.dot(ds, kv, preferred_element_type=f32) * scale).astype(bf16)
                dk_acc[pl.ds(k0, kw), cols] += lax.dot_general(ds, qv, TN, preferred_element_type=f32) * scale
                dv_acc[pl.ds(k0, kw), cols] += lax.dot_general(p.astype(bf16), dov, TN, preferred_element_type=f32)

        block(0, True)
        if nb > 1:
            def step(n, carry):
                block(n, False)
                return carry
            lax.fori_loop(1, nb, step, 0)
        dk_ref[...] = dk_acc[...].astype(bf16)
        dv_ref[...] = dv_acc[...].astype(bf16)

    def spec(cb):
        return pl.BlockSpec((None, None, L, GW), lambda b, c: (b, c, 0, cb))

    st = pl.BlockSpec((None, None, L, LANES), lambda b, c: (b, c, 0, 0))
    osd = SDS((B, r, L, GW), bf16)
    return _launch(
        body, name=f"attn_bwd_g{g}", grid=(B, r), in_specs=[spec(qcb), spec(kcb), spec(vcb), spec(0), st, st],
        out_specs=[spec(0), spec(0), spec(0)], out_shape=(osd, osd, osd), args=(q4, k4, v4, do4, lse4, dl4),
        scratch_shapes=[pltpu.VMEM((L, GW), f32), pltpu.VMEM((L, GW), f32)], sem=("parallel", "parallel"), comm=comm)


def _attn_mix(os_, lses):
    T = os_[0].shape[0]
    tm = 512
    ng = len(os_)

    def body(*refs):
        o_refs, l_refs = refs[:ng], refs[ng:2 * ng]
        y_ref, lt_ref = refs[2 * ng:]
        ls = [l_ref[...] for l_ref in l_refs]
        m = functools.reduce(jnp.maximum, ls)
        ws = [jnp.exp(l - m) for l in ls]
        den = functools.reduce(lambda u, v: u + v, ws)
        alphas = [w / den for w in ws]
        lt_ref[...] = m + jnp.log(den)
        for h in range(HEADS):
            cols = slice(h * HEAD_DIM, (h + 1) * HEAD_DIM)
            acc = alphas[0][:, h:h + 1] * o_refs[0][:, cols].astype(f32)
            for gi in range(1, ng):
                acc = acc + alphas[gi][:, h:h + 1] * o_refs[gi][:, cols].astype(f32)
            y_ref[:, cols] = acc.astype(bf16)

    ob = pl.BlockSpec((tm, GW), lambda i: (i, 0))
    lb = pl.BlockSpec((tm, LANES), lambda i: (i, 0))
    return pl.pallas_call(body, grid=(T // tm,), in_specs=[ob] * ng + [lb] * ng, out_specs=[ob, lb],
                          out_shape=(SDS((T, GW), bf16), SDS((T, LANES), f32)), name="attn_mix",
                          compiler_params=_cp(("parallel",)))(*os_, *lses)


def _attn_delta(dmix, mix):
    T = dmix.shape[0]
    tm = 512

    def body(d_ref, y_ref, o_ref):
        lane = lax.broadcasted_iota(jnp.int32, (tm, LANES), 1)
        acc = jnp.zeros((tm, LANES), f32)
        for h in range(HEADS):
            cols = slice(h * HEAD_DIM, (h + 1) * HEAD_DIM)
            dl = jnp.sum(d_ref[:, cols].astype(f32) * y_ref[:, cols].astype(f32), axis=-1, keepdims=True)
            acc = jnp.where(lane == h, dl, acc)
        o_ref[...] = acc

    ob = pl.BlockSpec((tm, GW), lambda i: (i, 0))
    return pl.pallas_call(body, grid=(T // tm,), in_specs=[ob, ob], out_specs=pl.BlockSpec((tm, LANES), lambda i: (i, 0)),
                          out_shape=SDS((T, LANES), f32), name="attn_delta", compiler_params=_cp(("parallel",)))(dmix, mix)


def _add_pairs(part, got, core, name):
    n, h, C = got.shape

    def body(c_ref, a_ref, b_ref, o_ref):
        o_ref[...] = (a_ref[...].astype(f32) + b_ref[...].astype(f32)).astype(bf16)

    blk = pl.BlockSpec((None, h, C), lambda s, c_ref: (s, 0, 0))
    own = pl.BlockSpec((None, h, C), lambda s, c_ref: (s, c_ref[0], 0))
    spec = pltpu.PrefetchScalarGridSpec(num_scalar_prefetch=1, grid=(n,), in_specs=[own, blk], out_specs=blk)
    return pl.pallas_call(body, grid_spec=spec, out_shape=SDS((n, h, C), bf16), name=name,
                          compiler_params=_cp(("parallel",)))(core, part, got)


def _sum4(sums, got, chip, core, name):
    _, h, C = sums.shape

    def body(s_ref, c_ref, own_ref, q_ref, o_ref):
        t = own_ref[...].astype(f32) + q_ref[0].astype(f32)
        t = t + q_ref[1].astype(f32)
        o_ref[...] = t + q_ref[2].astype(f32)

    spec = pltpu.PrefetchScalarGridSpec(
        num_scalar_prefetch=2, grid=(1,),
        in_specs=[pl.BlockSpec((None, h, C), lambda i, s_ref, c_ref: (s_ref[0], 0, 0)),
                  pl.BlockSpec((N_CHIPS - 1, h, C), lambda i, s_ref, c_ref: (0, 0, 0))],
        out_specs=pl.BlockSpec((h, C), lambda i, s_ref, c_ref: (c_ref[0], 0)))
    return pl.pallas_call(body, grid_spec=spec, out_shape=SDS((2 * h, C), f32), name=name,
                          compiler_params=_cp(("arbitrary",)))(chip, core, sums, got)


def _adamw(w, g, m, v, name):
    R, C = w.shape
    rt = R
    for cand in (512, 256, 128, 64, 32, 16, 8):
        if R % cand == 0 and cand * C * 4 <= 2 ** 21:
            rt = cand
            break
    c1 = 1.0 / (1.0 - B1 ** STEP)
    c2 = 1.0 / (1.0 - B2 ** STEP)

    def body(w_ref, g_ref, m_ref, v_ref, d_ref, nm_ref, nv_ref):
        gv = g_ref[...]
        nm = B1 * m_ref[...] + (1.0 - B1) * gv
        nv = B2 * v_ref[...] + (1.0 - B2) * (gv * gv)
        nm_ref[...] = nm
        nv_ref[...] = nv
        d_ref[...] = -LR * ((nm * c1) / (jnp.sqrt(nv * c2) + ADAM_EPS) + WD * w_ref[...])

    blk = pl.BlockSpec((rt, C), lambda i: (i, 0))
    sd = SDS((R, C), f32)
    return pl.pallas_call(body, grid=(R // rt,), in_specs=[blk] * 4, out_specs=[blk] * 3, out_shape=(sd, sd, sd), name=name,
                          compiler_params=_cp(("parallel",)))(w, g, m, v)


def _coords():
    return lax.axis_index("x"), lax.axis_index("y"), lax.axis_index("c")


def _other_chips(x, y):
    return [(1 - x, y), (x, 1 - y), (1 - x, 1 - y)]


def _allgather_weights(shards, slots):
    n = len(shards)
    halves = [s.shape[0] // 2 for s in shards]

    def body(*refs):
        ins, outs = refs[:n], refs[2 * n:3 * n]
        send1, recv1, send2, recv2 = refs[3 * n:]
        x, y, c = _coords()
        me = 2 * x + y
        chips = _other_chips(x, y)
        sends = []
        for w in range(n):
            mine = pl.ds(c * halves[w], halves[w])
            for j, (px, py) in enumerate(chips):
                cp = pltpu.make_async_remote_copy(src_ref=ins[w].at[mine, :], dst_ref=outs[w].at[me, mine, :],
                                                  send_sem=send1.at[w, j], recv_sem=recv1.at[w, j],
                                                  device_id=(px, py, c), device_id_type=MESH)
                cp.start()
                sends.append(cp)
        for w in range(n):
            mine = pl.ds(c * halves[w], halves[w])
            for j, (px, py) in enumerate(chips):
                blk = outs[w].at[2 * px + py, mine, :]
                pltpu.make_async_remote_copy(src_ref=blk, dst_ref=blk, send_sem=send1.at[w, j], recv_sem=recv1.at[w, j],
                                             device_id=(px, py, c), device_id_type=MESH).wait_recv()
                cp = pltpu.make_async_remote_copy(src_ref=blk, dst_ref=blk, send_sem=send2.at[w, j], recv_sem=recv2.at[w, j],
                                                  device_id=(x, y, 1 - c), device_id_type=MESH)
                cp.start()
                sends.append(cp)
        for w in range(n):
            theirs = pl.ds((1 - c) * halves[w], halves[w])
            for j, (px, py) in enumerate(chips):
                blk = outs[w].at[2 * px + py, theirs, :]
                pltpu.make_async_remote_copy(src_ref=blk, dst_ref=blk, send_sem=send2.at[w, j], recv_sem=recv2.at[w, j],
                                             device_id=(x, y, 1 - c), device_id_type=MESH).wait_recv()
        for cp in sends:
            cp.wait_send()

    return pl.pallas_call(
        body, in_specs=[HBM] * (2 * n), out_specs=[HBM] * n,
        out_shape=tuple(SDS((N_CHIPS,) + s.shape, s.dtype) for s in shards),
        input_output_aliases={n + w: w for w in range(n)},
        scratch_shapes=[pltpu.SemaphoreType.DMA((n, 3))] * 4,
        name="allgather_weights", compiler_params=pltpu.CompilerParams(has_side_effects=True))(*shards, *slots)


def _pair_exchange(parts, name):
    n = len(parts)
    halves = [p.shape[1] // 2 for p in parts]

    def body(*refs):
        ins, got_o = refs[:n], refs[n:2 * n]
        send, recv = refs[2 * n:]
        x, y, c = _coords()
        cps = []
        for w in range(n):
            give = pl.ds((1 - c) * halves[w], halves[w])
            rc = pltpu.make_async_remote_copy(src_ref=ins[w].at[:, give, :], dst_ref=got_o[w], send_sem=send.at[w],
                                              recv_sem=recv.at[w], device_id=(x, y, 1 - c), device_id_type=MESH)
            rc.start()
            cps.append(rc)
        for rc in cps:
            rc.wait()

    hs = tuple(SDS((N_CHIPS, p.shape[1] // 2, p.shape[2]), p.dtype) for p in parts)
    return pl.pallas_call(body, in_specs=[HBM] * n, out_specs=[HBM] * n, out_shape=hs,
                          scratch_shapes=[pltpu.SemaphoreType.DMA((n,))] * 2, name=name,
                          compiler_params=pltpu.CompilerParams(has_side_effects=True))(*parts)


def _x_chip_exchange(sums):
    n = len(sums)

    def copies(ins, outs, sems):
        send, recv = sems
        x, y, c = _coords()
        return [pltpu.make_async_remote_copy(src_ref=ins[w].at[2 * px + py], dst_ref=outs[w].at[j], send_sem=send.at[w, j],
                                             recv_sem=recv.at[w, j], device_id=(px, py, c), device_id_type=MESH)
                for w in range(n) for j, (px, py) in enumerate(_other_chips(x, y))]

    def start(ins, outs, sems):
        for cp in copies(ins, outs, sems):
            cp.start()

    def wait(ins, outs, sems):
        for cp in copies(ins, outs, sems):
            cp.wait()

    return _Exchange(ins=list(sums), out_shape=[SDS((N_CHIPS - 1,) + s.shape[1:], s.dtype) for s in sums],
                     scratch=[pltpu.SemaphoreType.DMA((n, 3)), pltpu.SemaphoreType.DMA((n, 3))], aliases={}, start=start, wait=wait)


def _x_gather_ici(shards, slots):
    n = len(shards)
    halves = [s.shape[0] // 2 for s in shards]

    def copies(ins, outs, sems):
        send, recv = sems
        x, y, c = _coords()
        me = 2 * x + y
        out = []
        for w in range(n):
            mine = pl.ds(c * halves[w], halves[w])
            for j, (px, py) in enumerate(_other_chips(x, y)):
                snd = pltpu.make_async_remote_copy(src_ref=ins[w].at[mine, :], dst_ref=outs[w].at[me, mine, :], send_sem=send.at[w, j],
                                                   recv_sem=recv.at[w, j], device_id=(px, py, c), device_id_type=MESH)
                got = outs[w].at[2 * px + py, mine, :]
                rcv = pltpu.make_async_remote_copy(src_ref=got, dst_ref=got, send_sem=send.at[w, j], recv_sem=recv.at[w, j],
                                                   device_id=(px, py, c), device_id_type=MESH)
                out.append((snd, rcv))
        return out

    def start(ins, outs, sems):
        for snd, _ in copies(ins, outs, sems):
            snd.start()

    def wait(ins, outs, sems):
        for snd, rcv in copies(ins, outs, sems):
            rcv.wait_recv()
            snd.wait_send()

    return _Exchange(ins=list(shards) + list(slots), out_shape=[SDS(s.shape, s.dtype) for s in slots],
                     scratch=[pltpu.SemaphoreType.DMA((n, 3)), pltpu.SemaphoreType.DMA((n, 3))],
                     aliases={n + w: w for w in range(n)}, start=start, wait=wait)


def _x_gather_fwd(bufs):
    n = len(bufs)
    halves = [b.shape[1] // 2 for b in bufs]

    def copies(ins, outs, sems):
        send, recv = sems
        x, y, c = _coords()
        out = []
        for w in range(n):
            for j, (px, py) in enumerate(_other_chips(x, y)):
                mine = outs[w].at[2 * px + py, pl.ds(c * halves[w], halves[w]), :]
                theirs = outs[w].at[2 * px + py, pl.ds((1 - c) * halves[w], halves[w]), :]
                snd = pltpu.make_async_remote_copy(src_ref=mine, dst_ref=mine, send_sem=send.at[w, j], recv_sem=recv.at[w, j],
                                                   device_id=(x, y, 1 - c), device_id_type=MESH)
                rcv = pltpu.make_async_remote_copy(src_ref=theirs, dst_ref=theirs, send_sem=send.at[w, j], recv_sem=recv.at[w, j],
                                                   device_id=(x, y, 1 - c), device_id_type=MESH)
                out.append((snd, rcv))
        return out

    def start(ins, outs, sems):
        for snd, _ in copies(ins, outs, sems):
            snd.start()

    def wait(ins, outs, sems):
        for snd, rcv in copies(ins, outs, sems):
            rcv.wait_recv()
            snd.wait_send()

    return _Exchange(ins=list(bufs), out_shape=[SDS(b.shape, b.dtype) for b in bufs],
                     scratch=[pltpu.SemaphoreType.DMA((n, 3)), pltpu.SemaphoreType.DMA((n, 3))],
                     aliases={w: w for w in range(n)}, start=start, wait=wait)


def _pair_gather(bufs):
    n = len(bufs)

    def body(*refs):
        outs = refs[n:2 * n]
        send, recv = refs[2 * n:]
        x, y, c = _coords()
        cps = []
        for w in range(n):
            h = bufs[w].shape[0] // 2
            rows = outs[w].at[pl.ds(c * h, h), :]
            rc = pltpu.make_async_remote_copy(src_ref=rows, dst_ref=rows, send_sem=send.at[w], recv_sem=recv.at[w],
                                              device_id=(x, y, 1 - c), device_id_type=MESH)
            rc.start()
            cps.append(rc)
        for w, rc in enumerate(cps):
            h = bufs[w].shape[0] // 2
            other = outs[w].at[pl.ds((1 - c) * h, h), :]
            pltpu.make_async_remote_copy(src_ref=other, dst_ref=other, send_sem=send.at[w], recv_sem=recv.at[w],
                                         device_id=(x, y, 1 - c), device_id_type=MESH).wait_recv()
            rc.wait_send()

    return pl.pallas_call(body, in_specs=[HBM] * n, out_specs=[HBM] * n, out_shape=tuple(SDS(a.shape, a.dtype) for a in bufs),
                          input_output_aliases={w: w for w in range(n)},
                          scratch_shapes=[pltpu.SemaphoreType.DMA((n,))] * 2, name="grad_pair_gather",
                          compiler_params=pltpu.CompilerParams(has_side_effects=True))(*bufs)


def _small_allreduce(vec, name):
    R = vec.shape[0]
    nd = 8

    def body(v_ref, o_ref, buf, send, recv):
        x, y, c = _coords()
        me = 4 * x + 2 * y + c
        buf[me] = v_ref[...]
        cps = []
        for k in range(1, nd):
            kx, ky, kc = (k >> 2) & 1, (k >> 1) & 1, k & 1
            tx = x + kx - 2 * x * kx
            ty = y + ky - 2 * y * ky
            tc = c + kc - 2 * c * kc
            cp = pltpu.make_async_remote_copy(src_ref=v_ref, dst_ref=buf.at[me], send_sem=send.at[k], recv_sem=recv.at[k],
                                              device_id=(tx, ty, tc), device_id_type=MESH)
            cp.start()
            cps.append((cp, 4 * tx + 2 * ty + tc))
        for k, (cp, src) in zip(range(1, nd), cps):
            pltpu.make_async_remote_copy(src_ref=v_ref, dst_ref=buf.at[src], send_sem=send.at[k], recv_sem=recv.at[k],
                                         device_id=(x, y, c), device_id_type=MESH).wait_recv()
        for cp, _ in cps:
            cp.wait_send()
        acc = buf[0]
        for d in range(1, nd):
            acc = acc + buf[d]
        o_ref[...] = acc

    vm = pl.BlockSpec(memory_space=pltpu.VMEM)
    return pl.pallas_call(body, in_specs=[vm], out_specs=vm, out_shape=SDS((R, LANES), f32),
                          scratch_shapes=[pltpu.VMEM((nd, R, LANES), f32), pltpu.SemaphoreType.DMA((nd,)), pltpu.SemaphoreType.DMA((nd,))],
                          name=name, compiler_params=pltpu.CompilerParams(has_side_effects=True))(vec)


def _perm(a, r):
    B, S_, C = a.shape
    return a.reshape(B, S_ // r, r, C).transpose(0, 2, 1, 3)


def _unperm(a):
    B, r, L, C = a.shape
    return a.transpose(0, 2, 1, 3).reshape(B, L * r, C)


def _rows128(a, rows):
    flat = a.reshape(-1)
    return jnp.pad(flat, (0, rows * LANES - flat.shape[0])).reshape(rows, LANES)


GATHER_1 = ("w_conv_out", "w_attn_out", "w_o", "w_ffn_gate")
GATHER_2 = ("w_ffn_up", "w_ffn_down")
REDUCE_A = ("w_ffn_gate", "w_ffn_up", "w_ffn_down")
REDUCE_B = ("w_conv_out", "w_attn_out", "w_o")
REDUCE_C = ("w_in",)


def _step(x, target, norm1_g, gate_b, conv_w, conv_b, conv_ln_g, conv_ln_b, norm2_g, norm_f_g, shards, slots, chip1, core):
    B = x.shape[0]
    T = B * SEQ
    xf = x.reshape(T, D_MODEL)
    tf = target.reshape(T, D_MODEL)

    (w_in,) = _allgather_weights([shards["w_in"]], [slots["w_in"]])
    h = _rms_fwd(xf, norm1_g, "rms1_fwd")
    proj, got1 = _mm("in_proj", h, w_in, grid=(N_CHIPS, T // 512),
                     a_spec=pl.BlockSpec((512, D_MODEL), lambda s, m: (m, 0)),
                     b_spec=pl.BlockSpec((None, D_MODEL, IN_S), lambda s, m: (s, 0, 0)),
                     o_spec=pl.BlockSpec((512, IN_S), lambda s, m: (m, s)), o_shape=(T, IN_W), o_dtype=bf16, dims=NN,
                     comm=_x_gather_ici([shards[n] for n in GATHER_1], [slots[n] for n in GATHER_1]))
    proj3 = proj.reshape(B, SEQ, IN_W)
    proj4 = proj.reshape(B, 1, SEQ, IN_W)

    qkv = [None]
    for g in range(1, len(GROUPS)):
        cols = [proj3[:, :, base + g * GW: base + (g + 1) * GW] for base in (0, ATTN_W, 2 * ATTN_W)]
        qkv.append(_perm(jnp.concatenate(cols, axis=-1), GROUPS[g][1]))

    def qkv_args(g):
        return (proj4, proj4, proj4, 0, 3, 6) if g == 0 else (qkv[g], qkv[g], qkv[g], 0, 1, 2)

    (o4_0, l4_0), full1 = _attn_fwd(0, *qkv_args(0), comm=_x_gather_fwd(list(got1)))
    (o4_1, l4_1), got2 = _attn_fwd(1, *qkv_args(1), comm=_x_gather_ici([shards[n] for n in GATHER_2], [slots[n] for n in GATHER_2]))
    (o4_2, l4_2), full2 = _attn_fwd(2, *qkv_args(2), comm=_x_gather_fwd(list(got2)))
    full = dict(zip(GATHER_1 + GATHER_2, full1 + full2))
    w_conv_out, w_attn_out, w_o = full["w_conv_out"], full["w_attn_out"], full["w_o"]
    w_gate, w_up, w_down = full["w_ffn_gate"], full["w_ffn_up"], full["w_ffn_down"]
    w_conv_out_f = w_conv_out.reshape(D_MODEL, D_MODEL)
    w_o_f = w_o.reshape(D_MODEL, D_MODEL)
    outs = [_unperm(o4).reshape(T, GW) for o4 in (o4_0, o4_1, o4_2)]
    lses = [_unperm(l4).reshape(T, LANES) for l4 in (l4_0, l4_1, l4_2)]
    mix, lse_tot = _attn_mix(outs, lses)
    y_attn = _mm("attn_out", mix, w_attn_out, grid=(N_CHIPS, T // 512),
                 a_spec=pl.BlockSpec((512, GW), lambda s, m: (m, 0)),
                 b_spec=pl.BlockSpec((None, GW, D_MODEL // N_CHIPS), lambda s, m: (s, 0, 0)),
                 o_spec=pl.BlockSpec((512, D_MODEL // N_CHIPS), lambda s, m: (m, s)), o_shape=(T, D_MODEL), o_dtype=bf16,
                 dims=NN, sem=("parallel", "parallel"))

    c1 = _glu_conv_fwd(proj3, conv_w, conv_b).reshape(T, D_MODEL)
    c3 = _ln_silu_fwd(c1, conv_ln_g, conv_ln_b)
    y_conv = _mm_nn_full("conv_out", c3, w_conv_out_f, bf16)

    merged = _merge_fwd(proj, gate_b, y_attn, y_conv)
    x1 = _mm_nn_full("o_proj", merged, w_o_f, f32, res=xf)

    h2 = _rms_fwd(x1, norm2_g, "rms2_fwd")
    fa, fb, ff = _ffn_up(h2, w_gate, w_up)
    x2 = _mm("ffn_down", ff, w_down, grid=(T // 1024, N_CHIPS),
             a_spec=pl.BlockSpec((None, 1024, FF_S), lambda m, s: (s, m, 0)),
             b_spec=pl.BlockSpec((None, FF_S, D_MODEL), lambda m, s: (s, 0, 0)),
             o_spec=pl.BlockSpec((1024, D_MODEL), lambda m, s: (m, 0)), o_shape=(T, D_MODEL), o_dtype=f32, dims=NN,
             acc_shape=(1024, D_MODEL), k_axis=1, res=x1, res_spec=pl.BlockSpec((1024, D_MODEL), lambda m, s: (m, 0)),
             sem=("parallel", "arbitrary"))

    loss, dx2, d_gf = _final_loss(x2, norm_f_g.reshape(1, D_MODEL), tf)

    d_w_down = _mm("d_w_down", ff, dx2, grid=(N_CHIPS, T // 1024),
                   a_spec=pl.BlockSpec((None, 1024, FF_S), lambda s, k: (s, k, 0)),
                   b_spec=pl.BlockSpec((1024, D_MODEL), lambda s, k: (k, 0)),
                   o_spec=pl.BlockSpec((None, FF_S, D_MODEL), lambda s, k: (s, 0, 0)), o_shape=(N_CHIPS, FF_S, D_MODEL),
                   o_dtype=bf16, dims=TN, acc_shape=(FF_S, D_MODEL), k_axis=1, sem=("parallel", "arbitrary"))
    da, db = _ffn_down_bwd(dx2, w_down, fa, fb)

    def d_w_ff(name, dz):
        return _mm(name, h2, dz, grid=(N_CHIPS, T // 2048),
                   a_spec=pl.BlockSpec((2048, D_MODEL), lambda s, k: (k, 0)),
                   b_spec=pl.BlockSpec((None, 2048, FF_S), lambda s, k: (s, k, 0)),
                   o_spec=pl.BlockSpec((None, D_MODEL, FF_S), lambda s, k: (s, 0, 0)), o_shape=(N_CHIPS, D_MODEL, FF_S),
                   o_dtype=bf16, dims=TN, acc_shape=(D_MODEL, FF_S), k_axis=1, sem=("parallel", "arbitrary"))

    d_w_gate = d_w_ff("d_w_gate", da)
    d_w_up = d_w_ff("d_w_up", db)
    part = dict(w_ffn_gate=d_w_gate, w_ffn_up=d_w_up, w_ffn_down=d_w_down)

    def pair_sums(names, tag):
        got = _pair_exchange([part[n] for n in names], f"grad_pair_exchange_{tag}")
        return [_add_pairs(part[n], g_, core, f"pair_sum_{n}") for n, g_ in zip(names, got)]

    sums_a = pair_sums(REDUCE_A, "a")
    dh2 = _ffn_dh2(da, w_gate, db, w_up)
    dx1, d_g2 = _rms_bwd(dh2, x1, norm2_g, dx2, "rms2_bwd")

    d_w_o = _mm_tn_tokens("d_w_o", merged, dx1, bf16).reshape(N_CHIPS, D_MODEL // N_CHIPS, D_MODEL)
    dmerged = _mm_nt_full("d_merged", dx1, w_o_f, bf16)
    dya, dyc, dga, dgc, d_gba, d_gbc = _merge_bwd(dmerged, proj, gate_b, y_attn, y_conv)

    d_w_conv_out = _mm_tn_tokens("d_w_conv_out", c3, dyc, bf16).reshape(N_CHIPS, D_MODEL // N_CHIPS, D_MODEL)
    dc3 = _mm_nt_full("d_c3", dyc, w_conv_out_f, bf16)
    dc1, d_ln_g, d_ln_b = _ln_silu_bwd(dc3, c1, conv_ln_g, conv_ln_b)
    (dua, dub, d_conv_w, d_conv_b), got_a = _glu_conv_bwd(dc1.reshape(B, SEQ, D_MODEL), proj3, conv_w, comm=_x_chip_exchange(sums_a))

    d_w_attn_out = _mm("d_w_attn_out", mix, dya, grid=(N_CHIPS, T // 512),
                       a_spec=pl.BlockSpec((512, GW), lambda s, k: (k, 0)),
                       b_spec=pl.BlockSpec((512, D_MODEL // N_CHIPS), lambda s, k: (k, s)),
                       o_spec=pl.BlockSpec((None, GW, D_MODEL // N_CHIPS), lambda s, k: (s, 0, 0)),
                       o_shape=(N_CHIPS, GW, D_MODEL // N_CHIPS), o_dtype=bf16, dims=TN, acc_shape=(GW, D_MODEL // N_CHIPS),
                       k_axis=1, sem=("parallel", "arbitrary"))
    dmix = _mm("d_mix", dya, w_attn_out, grid=(T // 512, N_CHIPS),
               a_spec=pl.BlockSpec((512, D_MODEL // N_CHIPS), lambda m, s: (m, s)),
               b_spec=pl.BlockSpec((None, GW, D_MODEL // N_CHIPS), lambda m, s: (s, 0, 0)),
               o_spec=pl.BlockSpec((512, GW), lambda m, s: (m, 0)), o_shape=(T, GW), o_dtype=bf16, dims=NT,
               acc_shape=(512, GW), k_axis=1, sem=("parallel", "arbitrary"))
    delta = _attn_delta(dmix, mix)
    dmix3 = dmix.reshape(B, SEQ, GW)
    lse3 = lse_tot.reshape(B, SEQ, LANES)
    delta3 = delta.reshape(B, SEQ, LANES)
    part.update(w_conv_out=d_w_conv_out, w_attn_out=d_w_attn_out, w_o=d_w_o)
    sums_b = pair_sums(REDUCE_B, "b")
    dqs, dks, dvs = [], [], []
    got_b = None
    for g in range(len(GROUPS)):
        r = GROUPS[g][1]
        res = _attn_bwd(g, *qkv_args(g), _perm(dmix3, r), _perm(lse3, r), _perm(delta3, r),
                        comm=_x_chip_exchange(sums_b) if g == 0 else None)
        if g == 0:
            res, got_b = res
        dq4, dk4, dv4 = res
        dqs.append(_unperm(dq4))
        dks.append(_unperm(dk4))
        dvs.append(_unperm(dv4))
    dproj = jnp.concatenate(dqs + dks + dvs + [dua, dub, dga.reshape(B, SEQ, D_MODEL), dgc.reshape(B, SEQ, D_MODEL)],
                            axis=-1).reshape(T, IN_W)

    d_w_in = _mm("d_w_in", h, dproj, grid=(N_CHIPS, T // 1024),
                 a_spec=pl.BlockSpec((1024, D_MODEL), lambda s, k: (k, 0)),
                 b_spec=pl.BlockSpec((1024, IN_S), lambda s, k: (k, s)),
                 o_spec=pl.BlockSpec((None, D_MODEL, IN_S), lambda s, k: (s, 0, 0)), o_shape=(N_CHIPS, D_MODEL, IN_S),
                 o_dtype=bf16, dims=TN, acc_shape=(D_MODEL, IN_S), k_axis=1, sem=("parallel", "arbitrary"))
    part.update(w_in=d_w_in)
    sums_c = pair_sums(REDUCE_C, "c")
    dh, got_c = _mm("d_h", dproj, w_in, grid=(T // 512, N_CHIPS),
                    a_spec=pl.BlockSpec((512, IN_S), lambda m, s: (m, s)),
                    b_spec=pl.BlockSpec((None, D_MODEL, IN_S), lambda m, s: (s, 0, 0)),
                    o_spec=pl.BlockSpec((512, D_MODEL), lambda m, s: (m, 0)), o_shape=(T, D_MODEL), o_dtype=f32, dims=NT,
                    acc_shape=(512, D_MODEL), k_axis=1, comm=_x_chip_exchange(sums_c))
    dx, d_g1 = _rms_bwd(dh, xf, norm1_g, dx1, "rms1_bwd")

    names = REDUCE_A + REDUCE_B + REDUCE_C
    halves = [_sum4(s, g_, chip1, core, f"chip_sum_{n}") for n, s, g_ in zip(names, sums_a + sums_b + sums_c, got_a + got_b + got_c)]
    big = dict(zip(names, _pair_gather(halves)))
    small = dict(norm1_g=d_g1, gate_b=jnp.concatenate([d_gba, d_gbc], axis=-1), conv_b=d_conv_b, conv_ln_g=d_ln_g,
                 conv_ln_b=d_ln_b, norm2_g=d_g2, norm_f_g=d_gf, conv_w=d_conv_w)
    return loss, dx.reshape(B, SEQ, D_MODEL), big, small


BIG = ("w_in", "w_conv_out", "w_attn_out", "w_o", "w_ffn_gate", "w_ffn_up", "w_ffn_down")
SMALL = ("norm1_g", "gate_b", "conv_b", "conv_ln_g", "conv_ln_b", "norm2_g", "norm_f_g")
SMALL_ROWS = {"norm1_g": 8, "gate_b": 16, "conv_b": 8, "conv_ln_g": 8, "conv_ln_b": 8, "norm2_g": 8, "norm_f_g": 8}
LOSS_ROWS = 8
CONVW_ROWS = 32 * D_MODEL // LANES


def kernel(x, norm1_g, w_in, gate_b, conv_w, conv_b, conv_ln_g, conv_ln_b, w_conv_out, w_attn_out, w_o, norm2_g, w_ffn_gate, w_ffn_up, w_ffn_down, norm_f_g, loss_target, m_norm1_g, m_w_in, m_gate_b, m_conv_w, m_conv_b, m_conv_ln_g, m_conv_ln_b, m_w_conv_out, m_w_attn_out, m_w_o, m_norm2_g, m_w_ffn_gate, m_w_ffn_up, m_w_ffn_down, m_norm_f_g, v_norm1_g, v_w_in, v_gate_b, v_conv_w, v_conv_b, v_conv_ln_g, v_conv_ln_b, v_w_conv_out, v_w_attn_out, v_w_o, v_norm2_g, v_w_ffn_gate, v_w_ffn_up, v_w_ffn_down, v_norm_f_g):
    W = dict(norm1_g=norm1_g, w_in=w_in, gate_b=gate_b, conv_w=conv_w, conv_b=conv_b, conv_ln_g=conv_ln_g, conv_ln_b=conv_ln_b,
             w_conv_out=w_conv_out, w_attn_out=w_attn_out, w_o=w_o, norm2_g=norm2_g, w_ffn_gate=w_ffn_gate, w_ffn_up=w_ffn_up,
             w_ffn_down=w_ffn_down, norm_f_g=norm_f_g)
    M = dict(norm1_g=m_norm1_g, w_in=m_w_in, gate_b=m_gate_b, conv_w=m_conv_w, conv_b=m_conv_b, conv_ln_g=m_conv_ln_g,
             conv_ln_b=m_conv_ln_b, w_conv_out=m_w_conv_out, w_attn_out=m_w_attn_out, w_o=m_w_o, norm2_g=m_norm2_g,
             w_ffn_gate=m_w_ffn_gate, w_ffn_up=m_w_ffn_up, w_ffn_down=m_w_ffn_down, norm_f_g=m_norm_f_g)
    V = dict(norm1_g=v_norm1_g, w_in=v_w_in, gate_b=v_gate_b, conv_w=v_conv_w, conv_b=v_conv_b, conv_ln_g=v_conv_ln_g,
             conv_ln_b=v_conv_ln_b, w_conv_out=v_w_conv_out, w_attn_out=v_w_attn_out, w_o=v_w_o, norm2_g=v_norm2_g,
             w_ffn_gate=v_w_ffn_gate, w_ffn_up=v_w_ffn_up, w_ffn_down=v_w_ffn_down, norm_f_g=v_norm_f_g)
    order = list(W)
    shard2d = {n: W[n].reshape(W[n].shape[-2:]) for n in BIG}
    chip = 2 * lax.axis_index("x") + lax.axis_index("y")

    cw = jnp.zeros((CONV_K, D_MODEL), f32)
    cw = lax.dynamic_update_slice(cw, 0.5 * conv_w.reshape(CONV_K, D_MODEL // N_CHIPS), (0, chip * (D_MODEL // N_CHIPS)))
    conv_w_full = _small_allreduce(_rows128(cw, CONVW_ROWS), "conv_w_gather")[: CONV_K * D_MODEL // LANES].reshape(CONV_K, D_MODEL)

    core = lax.axis_index("c").astype(jnp.int32).reshape(1)
    chip1 = chip.astype(jnp.int32).reshape(1)
    shards = {n: shard2d[n].astype(bf16) for n in BIG}
    slots = {n: lax.dynamic_update_slice(lax.empty((N_CHIPS,) + s.shape, bf16), s[None], (chip, 0, 0)) for n, s in shards.items()}

    loss, grad_x, grads, gsmall = _step(x, loss_target, norm1_g, gate_b, conv_w_full, conv_b, conv_ln_g, conv_ln_b, norm2_g,
                                        norm_f_g, shards, slots, chip1, core)

    pieces = [_rows128(loss, LOSS_ROWS)] + [_rows128(gsmall[n], SMALL_ROWS[n]) for n in SMALL] + [_rows128(gsmall["conv_w"], CONVW_ROWS)]
    tot = _small_allreduce(jnp.concatenate(pieces, axis=0), "small_allreduce")
    loss_out = tot[0, 0]
    row = LOSS_ROWS
    for n in SMALL:
        grads[n] = tot[row: row + W[n].size // LANES].reshape(W[n].shape)
        row += SMALL_ROWS[n]
    dcw = tot[row: row + CONV_K * D_MODEL // LANES].reshape(CONV_K, D_MODEL)
    grads["conv_w"] = lax.dynamic_slice(dcw, (0, chip * (D_MODEL // N_CHIPS)), (CONV_K, D_MODEL // N_CHIPS))

    delta, new_m, new_v = {}, {}, {}
    for n in BIG:
        d, nm, nv = _adamw(shard2d[n], grads[n], M[n].reshape(shard2d[n].shape), V[n].reshape(shard2d[n].shape), f"adamw_{n}")
        delta[n], new_m[n], new_v[n] = d.reshape(W[n].shape), nm.reshape(W[n].shape), nv.reshape(W[n].shape)
        grads[n] = grads[n].reshape(W[n].shape)

    def pack(src):
        return jnp.concatenate([_rows128(src[n], SMALL_ROWS[n]) for n in SMALL], axis=0)

    d, nm, nv = _adamw(pack(W), pack(grads), pack(M), pack(V), "adamw_small")
    row = 0
    for n in SMALL:
        k = W[n].size // LANES
        delta[n], new_m[n], new_v[n] = (t[row: row + k].reshape(W[n].shape) for t in (d, nm, nv))
        row += SMALL_ROWS[n]

    def pad32(a):
        return jnp.pad(a.reshape(CONV_K, D_MODEL // N_CHIPS), ((0, 1), (0, 0)))

    d, nm, nv = _adamw(pad32(conv_w), pad32(grads["conv_w"]), pad32(m_conv_w), pad32(v_conv_w), "adamw_conv_w")
    delta["conv_w"], new_m["conv_w"], new_v["conv_w"] = (t[:CONV_K].reshape(conv_w.shape) for t in (d, nm, nv))
    grads["conv_w"] = grads["conv_w"].reshape(conv_w.shape)

    return (loss_out, grad_x, *[grads[n] for n in order], *[delta[n] for n in order],
            *[new_m[n] for n in order], *[new_v[n] for n in order])
```

```python
import functools
import math
from typing import Callable, NamedTuple

import numpy as np
import jax
import jax.numpy as jnp
from jax import lax
from jax.experimental import pallas as pl
from jax.experimental.pallas import tpu as pltpu

f32 = jnp.float32
bf16 = jnp.bfloat16
SDS = jax.ShapeDtypeStruct
MESH = pl.DeviceIdType.MESH

D_MODEL = 1024
SEQ = 2048
HEAD_DIM = 64
HEADS = 8
GROUPS = ((128, 1), (512, 4), (2048, 16))
GW = HEADS * HEAD_DIM
ATTN_W = len(GROUPS) * GW
Q_BLOCK = 128
CONV_K = 31
D_FF = 2816
IN_W = 3 * ATTN_W + 2 * D_MODEL + 2 * D_MODEL
N_CHIPS = 4
IN_S = IN_W // N_CHIPS
FF_S = D_FF // N_CHIPS
RMS_EPS = 1e-6
LN_EPS = 1e-5
LR, B1, B2, ADAM_EPS, WD, STEP = 0.001, 0.9, 0.999, 1e-08, 0.01, 10
NEG = -1e30
LANES = 128
VMEM_LIMIT = 48 * 2 ** 20
CB = 512
UA_CB, UB_CB, GA_CB, GC_CB = 9, 11, 13, 15


def _alibi_slope_list(n):
    def pow2(m):
        start = 2.0 ** (-8.0 / m)
        return [start ** (i + 1) for i in range(m)]
    if math.log2(n).is_integer():
        return pow2(n)
    c = 2 ** math.floor(math.log2(n))
    return pow2(c) + _alibi_slope_list(2 * c)[0::2][: n - c]


_SLOPES = np.asarray(sorted(_alibi_slope_list(len(GROUPS) * HEADS), reverse=True), dtype=np.float32).reshape(len(GROUPS), HEADS)


def _cp(sem=None, vmem=VMEM_LIMIT):
    return pltpu.CompilerParams(dimension_semantics=sem, vmem_limit_bytes=vmem)


def _sigmoid(x):
    return 1.0 / (1.0 + jnp.exp(-x))


HBM = pl.BlockSpec(memory_space=pl.ANY)


class _Exchange(NamedTuple):
    ins: list
    out_shape: list
    scratch: list
    aliases: dict
    start: Callable
    wait: Callable


def _launch(body, *, name, grid, in_specs, out_specs, out_shape, args, scratch_shapes=(), sem=None, comm=None):
    if comm is None:
        return pl.pallas_call(body, grid=grid, in_specs=in_specs, out_specs=out_specs, out_shape=out_shape,
                              scratch_shapes=list(scratch_shapes), name=name, compiler_params=_cp(sem))(*args)
    multi = isinstance(out_shape, (tuple, list))
    m_out = list(out_shape) if multi else [out_shape]
    m_ospec = list(out_specs) if multi else [out_specs]
    n_in, n_out, n_scr = len(in_specs), len(m_out), len(scratch_shapes)
    nc_in, nc_out = len(comm.ins), len(comm.out_shape)

    def hosted(*refs):
        bounds = np.cumsum([0, n_in, nc_in, n_out, nc_out, n_scr])
        mi, ci, mo, co, ms = (refs[a:b] for a, b in zip(bounds[:-1], bounds[1:]))
        cs = refs[bounds[-1]:]
        ids = [pl.program_id(a) for a in range(len(grid))]
        first = functools.reduce(jnp.logical_and, [i == 0 for i in ids])
        last = functools.reduce(jnp.logical_and, [i == g - 1 for i, g in zip(ids, grid)])

        @pl.when(first)
        def _():
            comm.start(ci, co, cs)

        body(*mi, *mo, *ms)

        @pl.when(last)
        def _():
            comm.wait(ci, co, cs)

    res = pl.pallas_call(
        hosted, grid=grid, in_specs=list(in_specs) + [HBM] * nc_in, out_specs=m_ospec + [HBM] * nc_out,
        out_shape=tuple(m_out) + tuple(comm.out_shape), scratch_shapes=list(scratch_shapes) + list(comm.scratch),
        input_output_aliases={n_in + i: n_out + o for i, o in comm.aliases.items()}, name=name + "_comm",
        compiler_params=pltpu.CompilerParams(dimension_semantics=("arbitrary",) * len(grid), vmem_limit_bytes=VMEM_LIMIT,
                                             has_side_effects=True))(*args, *comm.ins)
    return (tuple(res[:n_out]) if multi else res[0]), tuple(res[n_out:])


def _run_exchange(ex, name):
    n_in = len(ex.ins)

    def body(*refs):
        ins, outs, sems = refs[:n_in], refs[n_in:n_in + len(ex.out_shape)], refs[n_in + len(ex.out_shape):]
        ex.start(ins, outs, sems)
        ex.wait(ins, outs, sems)

    return pl.pallas_call(body, in_specs=[HBM] * n_in, out_specs=[HBM] * len(ex.out_shape), out_shape=tuple(ex.out_shape),
                          scratch_shapes=list(ex.scratch), input_output_aliases=dict(ex.aliases), name=name,
                          compiler_params=pltpu.CompilerParams(has_side_effects=True))(*ex.ins)


def _rms_fwd(x, g, name):
    T = x.shape[0]
    tm = 512

    def body(x_ref, g_ref, o_ref):
        xv = x_ref[...]
        r = lax.rsqrt(jnp.mean(xv * xv, axis=-1, keepdims=True) + RMS_EPS)
        o_ref[...] = (xv * r * g_ref[...]).astype(o_ref.dtype)

    row = pl.BlockSpec((tm, D_MODEL), lambda i: (i, 0))
    vec = pl.BlockSpec((1, D_MODEL), lambda i: (0, 0))
    return pl.pallas_call(body, grid=(T // tm,), in_specs=[row, vec], out_specs=row,
                          out_shape=SDS((T, D_MODEL), bf16), name=name, compiler_params=_cp(("parallel",)))(x, g)


def _rms_bwd(dy, x, g, dres, name):
    T = x.shape[0]
    tm = 512

    def body(dy_ref, x_ref, g_ref, r_ref, dx_ref, dg_ref):
        xv = x_ref[...]
        r = lax.rsqrt(jnp.mean(xv * xv, axis=-1, keepdims=True) + RMS_EPS)
        xh = xv * r
        dyv = dy_ref[...]
        dxh = dyv * g_ref[...]
        dx_ref[...] = r_ref[...] + r * (dxh - xh * jnp.mean(dxh * xh, axis=-1, keepdims=True))
        part = jnp.sum(dyv * xh, axis=0, keepdims=True)

        @pl.when(pl.program_id(0) == 0)
        def _():
            dg_ref[...] = part

        @pl.when(pl.program_id(0) > 0)
        def _():
            dg_ref[...] += part

    row = pl.BlockSpec((tm, D_MODEL), lambda i: (i, 0))
    vec = pl.BlockSpec((1, D_MODEL), lambda i: (0, 0))
    return pl.pallas_call(body, grid=(T // tm,), in_specs=[row, row, vec, row], out_specs=[row, vec],
                          out_shape=(SDS((T, D_MODEL), f32), SDS((1, D_MODEL), f32)), name=name,
                          compiler_params=_cp(("arbitrary",)))(dy, x, g, dres)


def _final_loss(x2, gf, target):
    T = x2.shape[0]
    tm = 512

    def body(x_ref, g_ref, t_ref, loss_ref, dx_ref, dg_ref):
        xv = x_ref[...]
        gv = g_ref[...]
        r = lax.rsqrt(jnp.mean(xv * xv, axis=-1, keepdims=True) + RMS_EPS)
        xh = xv * r
        e = xh * gv - t_ref[...]
        part_l = jnp.broadcast_to(0.5 * jnp.sum(jnp.mean(e * e, axis=-1, keepdims=True), axis=0, keepdims=True), (1, LANES))
        dy = e * (1.0 / D_MODEL)
        dxh = dy * gv
        dx_ref[...] = r * (dxh - xh * jnp.mean(dxh * xh, axis=-1, keepdims=True))
        part_g = jnp.sum(dy * xh, axis=0, keepdims=True)

        @pl.when(pl.program_id(0) == 0)
        def _():
            loss_ref[...] = part_l
            dg_ref[...] = part_g

        @pl.when(pl.program_id(0) > 0)
        def _():
            loss_ref[...] += part_l
            dg_ref[...] += part_g

    row = pl.BlockSpec((tm, D_MODEL), lambda i: (i, 0))
    vec = pl.BlockSpec((1, D_MODEL), lambda i: (0, 0))
    lvec = pl.BlockSpec((1, LANES), lambda i: (0, 0))
    return pl.pallas_call(body, grid=(T // tm,), in_specs=[row, vec, row], out_specs=[lvec, row, vec],
                          out_shape=(SDS((1, LANES), f32), SDS((T, D_MODEL), f32), SDS((1, D_MODEL), f32)),
                          name="final_loss", compiler_params=_cp(("arbitrary",)))(x2, gf, target)


def _ln_silu_fwd(c1, g, b):
    T = c1.shape[0]
    tm = 512

    def body(c_ref, g_ref, b_ref, o_ref):
        cv = c_ref[...]
        mu = jnp.mean(cv, axis=-1, keepdims=True)
        cc = cv - mu
        var = jnp.mean(cc * cc, axis=-1, keepdims=True)
        c2 = cc * lax.rsqrt(var + LN_EPS) * g_ref[...] + b_ref[...]
        o_ref[...] = (c2 * _sigmoid(c2)).astype(o_ref.dtype)

    row = pl.BlockSpec((tm, D_MODEL), lambda i: (i, 0))
    vec = pl.BlockSpec((1, D_MODEL), lambda i: (0, 0))
    return pl.pallas_call(body, grid=(T // tm,), in_specs=[row, vec, vec], out_specs=row,
                          out_shape=SDS((T, D_MODEL), bf16), name="ln_silu_fwd", compiler_params=_cp(("parallel",)))(c1, g, b)


def _ln_silu_bwd(dc3, c1, g, b):
    T = c1.shape[0]
    tm = 512

    def body(d_ref, c_ref, g_ref, b_ref, dc_ref, dg_ref, db_ref):
        cv = c_ref[...]
        gv = g_ref[...]
        mu = jnp.mean(cv, axis=-1, keepdims=True)
        cc = cv - mu
        var = jnp.mean(cc * cc, axis=-1, keepdims=True)
        rs = lax.rsqrt(var + LN_EPS)
        xh = cc * rs
        c2 = xh * gv + b_ref[...]
        sg = _sigmoid(c2)
        dc2 = d_ref[...].astype(f32) * (sg * (1.0 + c2 * (1.0 - sg)))
        dxh = dc2 * gv
        dc_ref[...] = rs * (dxh - jnp.mean(dxh, axis=-1, keepdims=True) - xh * jnp.mean(dxh * xh, axis=-1, keepdims=True))
        pg = jnp.sum(dc2 * xh, axis=0, keepdims=True)
        pb = jnp.sum(dc2, axis=0, keepdims=True)

        @pl.when(pl.program_id(0) == 0)
        def _():
            dg_ref[...] = pg
            db_ref[...] = pb

        @pl.when(pl.program_id(0) > 0)
        def _():
            dg_ref[...] += pg
            db_ref[...] += pb

    row = pl.BlockSpec((tm, D_MODEL), lambda i: (i, 0))
    vec = pl.BlockSpec((1, D_MODEL), lambda i: (0, 0))
    return pl.pallas_call(body, grid=(T // tm,), in_specs=[row, row, vec, vec], out_specs=[row, vec, vec],
                          out_shape=(SDS((T, D_MODEL), f32), SDS((1, D_MODEL), f32), SDS((1, D_MODEL), f32)),
                          name="ln_silu_bwd", compiler_params=_cp(("arbitrary",)))(dc3, c1, g, b)


NN = (((1,), (0,)), ((), ()))
NT = (((1,), (1,)), ((), ()))
TN = (((0,), (0,)), ((), ()))


def _mm(name, a, b, *, grid, a_spec, b_spec, o_spec, o_shape, o_dtype, dims, acc_shape=None, k_axis=None,
        res=None, res_spec=None, sem=None, comm=None):
    nk = 1 if k_axis is None else grid[k_axis]

    def body(*refs):
        if res is None:
            a_ref, b_ref, o_ref = refs[:3]
            r_ref, scr = None, refs[3:]
        else:
            a_ref, b_ref, r_ref, o_ref = refs[:4]
            scr = refs[4:]
        p = lax.dot_general(a_ref[...].astype(bf16), b_ref[...].astype(bf16), dims, preferred_element_type=f32)
        if nk == 1:
            if r_ref is not None:
                p = p + r_ref[...]
            o_ref[...] = p.astype(o_dtype)
            return
        acc = scr[0]
        k = pl.program_id(k_axis)

        @pl.when(k == 0)
        def _():
            acc[...] = p

        @pl.when(k > 0)
        def _():
            acc[...] += p

        @pl.when(k == nk - 1)
        def _():
            t = acc[...]
            if r_ref is not None:
                t = t + r_ref[...]
            o_ref[...] = t.astype(o_dtype)

    ins = [a, b] + ([] if res is None else [res])
    specs = [a_spec, b_spec] + ([] if res is None else [res_spec])
    scratch = [] if nk == 1 else [pltpu.VMEM(acc_shape, f32)]
    return _launch(body, name=name, grid=grid, in_specs=specs, out_specs=o_spec, out_shape=SDS(o_shape, o_dtype),
                   args=ins, scratch_shapes=scratch, sem=sem, comm=comm)


def _mm_nn_full(name, a, b, o_dtype, res=None, tm=512):
    T, K = a.shape
    N = b.shape[1]
    return _mm(name, a, b, grid=(T // tm,), a_spec=pl.BlockSpec((tm, K), lambda i: (i, 0)),
               b_spec=pl.BlockSpec((K, N), lambda i: (0, 0)), o_spec=pl.BlockSpec((tm, N), lambda i: (i, 0)),
               o_shape=(T, N), o_dtype=o_dtype, dims=NN, res=res,
               res_spec=None if res is None else pl.BlockSpec((tm, N), lambda i: (i, 0)), sem=("parallel",))


def _mm_nt_full(name, a, b, o_dtype, tm=512):
    T, N = a.shape
    K = b.shape[0]
    return _mm(name, a, b, grid=(T // tm,), a_spec=pl.BlockSpec((tm, N), lambda i: (i, 0)),
               b_spec=pl.BlockSpec((K, N), lambda i: (0, 0)), o_spec=pl.BlockSpec((tm, K), lambda i: (i, 0)),
               o_shape=(T, K), o_dtype=o_dtype, dims=NT, sem=("parallel",))


def _mm_tn_tokens(name, a, b, o_dtype, tk=1024):
    T, K = a.shape
    N = b.shape[1]
    return _mm(name, a, b, grid=(T // tk,), a_spec=pl.BlockSpec((tk, K), lambda k: (k, 0)),
               b_spec=pl.BlockSpec((tk, N), lambda k: (k, 0)), o_spec=pl.BlockSpec((K, N), lambda k: (0, 0)),
               o_shape=(K, N), o_dtype=o_dtype, dims=TN, acc_shape=(K, N), k_axis=0, sem=("arbitrary",))


def _ffn_up(h2, wg, wu):
    T = h2.shape[0]
    tm = 512

    def body(h_ref, wg_ref, wu_ref, a_ref, b_ref, f_ref):
        hv = h_ref[...]
        av = jnp.dot(hv, wg_ref[...], preferred_element_type=f32)
        bv = jnp.dot(hv, wu_ref[...], preferred_element_type=f32)
        a_ref[...] = av.astype(bf16)
        b_ref[...] = bv.astype(bf16)
        f_ref[...] = (av * _sigmoid(av) * bv).astype(bf16)

    wspec = pl.BlockSpec((None, D_MODEL, FF_S), lambda s, m: (s, 0, 0))
    ospec = pl.BlockSpec((None, tm, FF_S), lambda s, m: (s, m, 0))
    osd = SDS((N_CHIPS, T, FF_S), bf16)
    return pl.pallas_call(body, grid=(N_CHIPS, T // tm),
                          in_specs=[pl.BlockSpec((tm, D_MODEL), lambda s, m: (m, 0)), wspec, wspec],
                          out_specs=[ospec, ospec, ospec], out_shape=(osd, osd, osd), name="ffn_up",
                          compiler_params=_cp(("parallel", "parallel")))(h2, wg, wu)


def _ffn_down_bwd(dx2, wd, a, b):
    T = dx2.shape[0]
    tm = 512

    def body(d_ref, w_ref, a_ref, b_ref, da_ref, db_ref):
        df = lax.dot_general(d_ref[...].astype(bf16), w_ref[...], NT, preferred_element_type=f32)
        av = a_ref[...].astype(f32)
        sg = _sigmoid(av)
        da_ref[...] = (df * b_ref[...].astype(f32) * (sg * (1.0 + av * (1.0 - sg)))).astype(bf16)
        db_ref[...] = (df * av * sg).astype(bf16)

    aspec = pl.BlockSpec((None, tm, FF_S), lambda s, m: (s, m, 0))
    osd = SDS((N_CHIPS, T, FF_S), bf16)
    return pl.pallas_call(body, grid=(N_CHIPS, T // tm),
                          in_specs=[pl.BlockSpec((tm, D_MODEL), lambda s, m: (m, 0)),
                                    pl.BlockSpec((None, FF_S, D_MODEL), lambda s, m: (s, 0, 0)), aspec, aspec],
                          out_specs=[aspec, aspec], out_shape=(osd, osd), name="ffn_down_bwd",
                          compiler_params=_cp(("parallel", "parallel")))(dx2, wd, a, b)


def _ffn_dh2(da, wg, db, wu):
    T = da.shape[1]
    tm = 1024

    def body(da_ref, wg_ref, db_ref, wu_ref, o_ref, acc):
        p = lax.dot_general(da_ref[...], wg_ref[...], NT, preferred_element_type=f32)
        p = p + lax.dot_general(db_ref[...], wu_ref[...], NT, preferred_element_type=f32)
        s = pl.program_id(1)

        @pl.when(s == 0)
        def _():
            acc[...] = p

        @pl.when(s > 0)
        def _():
            acc[...] += p

        @pl.when(s == N_CHIPS - 1)
        def _():
            o_ref[...] = acc[...]

    aspec = pl.BlockSpec((None, tm, FF_S), lambda m, s: (s, m, 0))
    wspec = pl.BlockSpec((None, D_MODEL, FF_S), lambda m, s: (s, 0, 0))
    return pl.pallas_call(body, grid=(T // tm, N_CHIPS), in_specs=[aspec, wspec, aspec, wspec],
                          out_specs=pl.BlockSpec((tm, D_MODEL), lambda m, s: (m, 0)), out_shape=SDS((T, D_MODEL), f32),
                          scratch_shapes=[pltpu.VMEM((tm, D_MODEL), f32)], name="ffn_dh2",
                          compiler_params=_cp(("parallel", "arbitrary")))(da, wg, db, wu)


def _merge_fwd(proj, gate_b, ya, yc):
    T = proj.shape[0]
    tm = 512

    def body(ga_ref, gc_ref, ba_ref, bc_ref, ya_ref, yc_ref, o_ref):
        sa = _sigmoid(ga_ref[...].astype(f32) + ba_ref[...])
        sc = _sigmoid(gc_ref[...].astype(f32) + bc_ref[...])
        o_ref[...] = (sa * ya_ref[...].astype(f32) + sc * yc_ref[...].astype(f32)).astype(bf16)

    blk = pl.BlockSpec((tm, CB), lambda i, j: (i, j))
    return pl.pallas_call(
        body, grid=(T // tm, 2),
        in_specs=[pl.BlockSpec((tm, CB), lambda i, j: (i, GA_CB + j)), pl.BlockSpec((tm, CB), lambda i, j: (i, GC_CB + j)),
                  pl.BlockSpec((1, CB), lambda i, j: (0, j)), pl.BlockSpec((1, CB), lambda i, j: (0, 2 + j)), blk, blk],
        out_specs=blk, out_shape=SDS((T, D_MODEL), bf16), name="merge_fwd",
        compiler_params=_cp(("parallel", "parallel")))(proj, proj, gate_b, gate_b, ya, yc)


def _merge_bwd(dm, proj, gate_b, ya, yc):
    T = proj.shape[0]
    tm = 512

    def body(dm_ref, ga_ref, gc_ref, ba_ref, bc_ref, ya_ref, yc_ref, dya_ref, dyc_ref, dga_ref, dgc_ref, dba_ref, dbc_ref):
        dmv = dm_ref[...].astype(f32)
        sa = _sigmoid(ga_ref[...].astype(f32) + ba_ref[...])
        sc = _sigmoid(gc_ref[...].astype(f32) + bc_ref[...])
        dya_ref[...] = (dmv * sa).astype(bf16)
        dyc_ref[...] = (dmv * sc).astype(bf16)
        dga = dmv * ya_ref[...].astype(f32) * sa * (1.0 - sa)
        dgc = dmv * yc_ref[...].astype(f32) * sc * (1.0 - sc)
        dga_ref[...] = dga.astype(bf16)
        dgc_ref[...] = dgc.astype(bf16)
        pa = jnp.sum(dga, axis=0, keepdims=True)
        pc = jnp.sum(dgc, axis=0, keepdims=True)

        @pl.when(pl.program_id(1) == 0)
        def _():
            dba_ref[...] = pa
            dbc_ref[...] = pc

        @pl.when(pl.program_id(1) > 0)
        def _():
            dba_ref[...] += pa
            dbc_ref[...] += pc

    blk = pl.BlockSpec((tm, CB), lambda j, i: (i, j))
    vec = pl.BlockSpec((1, CB), lambda j, i: (0, j))
    big = SDS((T, D_MODEL), bf16)
    small = SDS((1, D_MODEL), f32)
    return pl.pallas_call(
        body, grid=(2, T // tm),
        in_specs=[blk, pl.BlockSpec((tm, CB), lambda j, i: (i, GA_CB + j)), pl.BlockSpec((tm, CB), lambda j, i: (i, GC_CB + j)),
                  vec, pl.BlockSpec((1, CB), lambda j, i: (0, 2 + j)), blk, blk],
        out_specs=[blk, blk, blk, blk, vec, vec], out_shape=(big, big, big, big, small, small), name="merge_bwd",
        compiler_params=_cp(("parallel", "arbitrary")))(dm, proj, proj, gate_b, gate_b, ya, yc)


CONV_TS = 256
CONV_HALO = 32
CONV_RC = 64
CONV_WIN = CONV_TS + CONV_HALO
SUBLANES = 8


def _fill_shifted(win, sh):
    for b in range(1, SUBLANES):
        sh[b - 1] = win[pl.ds(b, CONV_WIN - SUBLANES), :]


def _rows_at(win, sh, row):
    a, b = divmod(row, SUBLANES)
    if b == 0:
        return win[pl.ds(row, CONV_RC), :]
    return sh[b - 1, pl.ds(a * SUBLANES, CONV_RC), :]


def _glu_conv_fwd(proj3, w, bias):
    B = proj3.shape[0]
    nt = SEQ // CONV_TS
    hb = CONV_TS // CONV_HALO

    def body(ua_ref, ub_ref, ha_ref, hb_ref, w_ref, b_ref, o_ref, win, sh):
        i = pl.program_id(2)
        c0 = ua_ref[...].astype(f32) * _sigmoid(ub_ref[...].astype(f32))
        halo = ha_ref[...].astype(f32) * _sigmoid(hb_ref[...].astype(f32))
        win[0:CONV_HALO, :] = jnp.where(i > 0, halo, 0.0)
        win[CONV_HALO:, :] = c0
        _fill_shifted(win, sh)
        for r0 in range(0, CONV_TS, CONV_RC):
            acc = jnp.zeros((CONV_RC, CB), f32) + b_ref[...]
            for k in range(CONV_K):
                acc = acc + _rows_at(win, sh, r0 + CONV_HALO - (CONV_K - 1) + k) * w_ref[k:k + 1, :]
            o_ref[r0:r0 + CONV_RC, :] = acc

    def cur(cb):
        return pl.BlockSpec((None, CONV_TS, CB), lambda b, j, i: (b, i, cb + j))

    def prev(cb):
        return pl.BlockSpec((None, CONV_HALO, CB), lambda b, j, i: (b, jnp.maximum(i * hb - 1, 0), cb + j))

    return pl.pallas_call(
        body, grid=(B, 2, nt),
        in_specs=[cur(UA_CB), cur(UB_CB), prev(UA_CB), prev(UB_CB),
                  pl.BlockSpec((CONV_K, CB), lambda b, j, i: (0, j)), pl.BlockSpec((1, CB), lambda b, j, i: (0, j))],
        out_specs=pl.BlockSpec((None, CONV_TS, CB), lambda b, j, i: (b, i, j)),
        out_shape=SDS((B, SEQ, D_MODEL), f32),
        scratch_shapes=[pltpu.VMEM((CONV_WIN, CB), f32), pltpu.VMEM((SUBLANES - 1, CONV_WIN - SUBLANES, CB), f32)],
        name="glu_conv_fwd", compiler_params=_cp(("parallel", "parallel", "parallel")))(proj3, proj3, proj3, proj3, w, bias)


def _glu_conv_bwd(dc1, proj3, w, comm=None):
    B = proj3.shape[0]
    nt = SEQ // CONV_TS
    hb = CONV_TS // CONV_HALO

    def body(d_ref, dn_ref, ua_ref, ub_ref, ha_ref, hb_ref, w_ref, dua_ref, dub_ref, dw_ref, db_ref, winc, wind, accw, shc, shd):
        b = pl.program_id(1)
        i = pl.program_id(2)
        first = jnp.logical_and(b == 0, i == 0)
        last = jnp.logical_and(b == B - 1, i == nt - 1)

        @pl.when(first)
        def _():
            accw[...] = jnp.zeros_like(accw)
            db_ref[...] = jnp.zeros_like(db_ref)

        halo = ha_ref[...].astype(f32) * _sigmoid(hb_ref[...].astype(f32))
        winc[0:CONV_HALO, :] = jnp.where(i > 0, halo, 0.0)
        winc[CONV_HALO:, :] = ua_ref[...].astype(f32) * _sigmoid(ub_ref[...].astype(f32))
        wind[0:CONV_TS, :] = d_ref[...]
        wind[CONV_TS:, :] = jnp.where(i < nt - 1, dn_ref[...], 0.0)
        db_ref[...] += jnp.sum(d_ref[...], axis=0, keepdims=True)
        _fill_shifted(winc, shc)
        _fill_shifted(wind, shd)
        for r0 in range(0, CONV_TS, CONV_RC):
            dc0 = jnp.zeros((CONV_RC, CB), f32)
            for k in range(CONV_K):
                dc0 = dc0 + _rows_at(wind, shd, r0 + (CONV_K - 1) - k) * w_ref[k:k + 1, :]
            uav = ua_ref[r0:r0 + CONV_RC, :].astype(f32)
            sg = _sigmoid(ub_ref[r0:r0 + CONV_RC, :].astype(f32))
            dua_ref[r0:r0 + CONV_RC, :] = (dc0 * sg).astype(bf16)
            dub_ref[r0:r0 + CONV_RC, :] = (dc0 * uav * sg * (1.0 - sg)).astype(bf16)
            dv = wind[r0:r0 + CONV_RC, :]
            for k in range(CONV_K):
                prod = dv * _rows_at(winc, shc, r0 + CONV_HALO - (CONV_K - 1) + k)
                accw[k] += jnp.sum(prod.reshape(CONV_RC // 8, 8, CB), axis=0)

        @pl.when(last)
        def _():
            for k in range(CONV_K):
                dw_ref[k:k + 1, :] = jnp.sum(accw[k], axis=0, keepdims=True)
            dw_ref[CONV_K:, :] = jnp.zeros((CONV_HALO - CONV_K, CB), f32)

    def cur(cb):
        return pl.BlockSpec((None, CONV_TS, CB), lambda j, b, i: (b, i, cb + j))

    def prev(cb):
        return pl.BlockSpec((None, CONV_HALO, CB), lambda j, b, i: (b, jnp.maximum(i * hb - 1, 0), cb + j))

    nxt = pl.BlockSpec((None, CONV_HALO, CB), lambda j, b, i: (b, jnp.minimum((i + 1) * hb, SEQ // CONV_HALO - 1), j))
    big = SDS((B, SEQ, D_MODEL), bf16)
    return _launch(
        body, name="glu_conv_bwd", grid=(2, B, nt),
        in_specs=[cur(0), nxt, cur(UA_CB), cur(UB_CB), prev(UA_CB), prev(UB_CB), pl.BlockSpec((CONV_K, CB), lambda j, b, i: (0, j))],
        out_specs=[cur(0), cur(0), pl.BlockSpec((CONV_HALO, CB), lambda j, b, i: (0, j)), pl.BlockSpec((1, CB), lambda j, b, i: (0, j))],
        out_shape=(big, big, SDS((CONV_HALO, D_MODEL), f32), SDS((1, D_MODEL), f32)),
        args=(dc1, dc1, proj3, proj3, proj3, proj3, w),
        scratch_shapes=[pltpu.VMEM((CONV_WIN, CB), f32), pltpu.VMEM((CONV_WIN, CB), f32), pltpu.VMEM((CONV_K, SUBLANES, CB), f32),
                        pltpu.VMEM((SUBLANES - 1, CONV_WIN - SUBLANES, CB), f32),
                        pltpu.VMEM((SUBLANES - 1, CONV_WIN - SUBLANES, CB), f32)],
        sem=("parallel", "arbitrary", "arbitrary"), comm=comm)


def _band(first, dil):
    kw = Q_BLOCK if first else 2 * Q_BLOCK
    qi = lax.broadcasted_iota(jnp.int32, (Q_BLOCK, kw), 0)
    kj = lax.broadcasted_iota(jnp.int32, (Q_BLOCK, kw), 1)
    rel = qi - kj + (0 if first else Q_BLOCK)
    valid = jnp.logical_and(rel >= 0, rel <= Q_BLOCK)
    return valid, rel.astype(f32) * float(dil)


def _scores(q, k, valid, dist, slope):
    s = lax.dot_general(q, k, NT, preferred_element_type=f32) * (HEAD_DIM ** -0.5) - slope * dist
    return jnp.where(valid, s, NEG)


def _attn_fwd(g, q4, k4, v4, qcb, kcb, vcb, comm=None):
    _, dil = GROUPS[g]
    B, r, L, _ = q4.shape
    nb = L // Q_BLOCK
    slopes = [float(s) for s in _SLOPES[g]]

    def body(q_ref, k_ref, v_ref, o_ref, lse_ref):
        lane = lax.broadcasted_iota(jnp.int32, (Q_BLOCK, LANES), 1)

        def block(n, first):
            q0 = 0 if first else pl.multiple_of(n * Q_BLOCK, Q_BLOCK)
            k0 = 0 if first else pl.multiple_of((n - 1) * Q_BLOCK, Q_BLOCK)
            kw = Q_BLOCK if first else 2 * Q_BLOCK
            valid, dist = _band(first, dil)
            st = jnp.zeros((Q_BLOCK, LANES), f32)
            for h in range(HEADS):
                cols = slice(h * HEAD_DIM, (h + 1) * HEAD_DIM)
                s = _scores(q_ref[pl.ds(q0, Q_BLOCK), cols], k_ref[pl.ds(k0, kw), cols], valid, dist, slopes[h])
                m = jnp.max(s, axis=-1, keepdims=True)
                p = jnp.exp(s - m)
                den = jnp.sum(p, axis=-1, keepdims=True)
                o = jnp.dot(p.astype(bf16), v_ref[pl.ds(k0, kw), cols], preferred_element_type=f32) / den
                o_ref[pl.ds(q0, Q_BLOCK), cols] = o.astype(bf16)
                st = jnp.where(lane == h, m + jnp.log(den), st)
            lse_ref[pl.ds(q0, Q_BLOCK), :] = st

        block(0, True)
        if nb > 1:
            def step(n, carry):
                block(n, False)
                return carry
            lax.fori_loop(1, nb, step, 0)

    def spec(cb):
        return pl.BlockSpec((None, None, L, GW), lambda b, c: (b, c, 0, cb))

    return _launch(
        body, name=f"attn_fwd_g{g}", grid=(B, r), in_specs=[spec(qcb), spec(kcb), spec(vcb)],
        out_specs=[spec(0), pl.BlockSpec((None, None, L, LANES), lambda b, c: (b, c, 0, 0))],
        out_shape=(SDS((B, r, L, GW), bf16), SDS((B, r, L, LANES), f32)), args=(q4, k4, v4),
        sem=("parallel", "parallel"), comm=comm)


def _attn_bwd(g, q4, k4, v4, qcb, kcb, vcb, do4, lse4, dl4, comm=None):
    _, dil = GROUPS[g]
    B, r, L, _ = q4.shape
    nb = L // Q_BLOCK
    slopes = [float(s) for s in _SLOPES[g]]
    scale = HEAD_DIM ** -0.5

    def body(q_ref, k_ref, v_ref, do_ref, lse_ref, dl_ref, dq_ref, dk_ref, dv_ref, dk_acc, dv_acc):
        dk_acc[...] = jnp.zeros_like(dk_acc)
        dv_acc[...] = jnp.zeros_like(dv_acc)

        def block(n, first):
            q0 = 0 if first else pl.multiple_of(n * Q_BLOCK, Q_BLOCK)
            k0 = 0 if first else pl.multiple_of((n - 1) * Q_BLOCK, Q_BLOCK)
            kw = Q_BLOCK if first else 2 * Q_BLOCK
            valid, dist = _band(first, dil)
            for h in range(HEADS):
                cols = slice(h * HEAD_DIM, (h + 1) * HEAD_DIM)
                qv = q_ref[pl.ds(q0, Q_BLOCK), cols]
                kv = k_ref[pl.ds(k0, kw), cols]
                vv = v_ref[pl.ds(k0, kw), cols]
                dov = do_ref[pl.ds(q0, Q_BLOCK), cols]
                s = _scores(qv, kv, valid, dist, slopes[h])
                p = jnp.exp(s - lse_ref[pl.ds(q0, Q_BLOCK), h:h + 1])
                dp = lax.dot_general(dov, vv, NT, preferred_element_type=f32)
                ds = (p * (dp - dl_ref[pl.ds(q0, Q_BLOCK), h:h + 1])).astype(bf16)
                dq_ref[pl.ds(q0, Q_BLOCK), cols] = (jnp.dot(ds, kv, preferred_element_type=f32) * scale).astype(bf16)
                dk_acc[pl.ds(k0, kw), cols] += lax.dot_general(ds, qv, TN, preferred_element_type=f32) * scale
                dv_acc[pl.ds(k0, kw), cols] += lax.dot_general(p.astype(bf16), dov, TN, preferred_element_type=f32)

        block(0, True)
        if nb > 1:
            def step(n, carry):
                block(n, False)
                return carry
            lax.fori_loop(1, nb, step, 0)
        dk_ref[...] = dk_acc[...].astype(bf16)
        dv_ref[...] = dv_acc[...].astype(bf16)

    def spec(cb):
        return pl.BlockSpec((None, None, L, GW), lambda b, c: (b, c, 0, cb))

    st = pl.BlockSpec((None, None, L, LANES), lambda b, c: (b, c, 0, 0))
    osd = SDS((B, r, L, GW), bf16)
    return _launch(
        body, name=f"attn_bwd_g{g}", grid=(B, r), in_specs=[spec(qcb), spec(kcb), spec(vcb), spec(0), st, st],
        out_specs=[spec(0), spec(0), spec(0)], out_shape=(osd, osd, osd), args=(q4, k4, v4, do4, lse4, dl4),
        scratch_shapes=[pltpu.VMEM((L, GW), f32), pltpu.VMEM((L, GW), f32)], sem=("parallel", "parallel"), comm=comm)


RT = 512
RCH = GW // LANES
DILS = tuple(d for _, d in GROUPS[1:])


def _res_spec(r, width):
    return pl.BlockSpec((None, r, RT // r, width), lambda b, i, *_: (b, 0, i, 0))


def _tok_spec(width, cb=0):
    return pl.BlockSpec((None, RT, width), lambda b, i, *_: (b, i, cb))


def _to_residues(res_ref, scr, r, width):
    for c in range(r):
        for jj in range(width // LANES):
            res_ref[c, :, jj * LANES:(jj + 1) * LANES] = scr[jj, pl.ds(c, RT // r, stride=r), :].astype(res_ref.dtype)


def _from_residues(scr, res_ref, r, width):
    for c in range(r):
        for jj in range(width // LANES):
            scr[jj, pl.ds(c, RT // r, stride=r), :] = res_ref[c, :, jj * LANES:(jj + 1) * LANES].astype(f32)


def _qkv_to_residues(proj3, g):
    r = GROUPS[g][1]
    B = proj3.shape[0]

    def body(x_ref, o_ref, scr):
        x = x_ref[...].astype(f32)
        for jj in range(RCH):
            scr[jj] = x[:, jj * LANES:(jj + 1) * LANES]
        _to_residues(o_ref, scr, r, GW)

    return pl.pallas_call(
        body, grid=(B, SEQ // RT, 3), in_specs=[pl.BlockSpec((None, RT, GW), lambda b, i, p: (b, i, 3 * p + g))],
        out_specs=pl.BlockSpec((None, r, RT // r, GW), lambda b, i, p: (b, 0, i, p)),
        out_shape=SDS((B, r, SEQ // r, ATTN_W), bf16), scratch_shapes=[pltpu.VMEM((RCH, RT, LANES), f32)],
        name=f"qkv_to_residues_g{g}", compiler_params=_cp(("parallel", "parallel", "parallel")))(proj3)


def _attn_mix(o0, l0, o_res, l_res):
    B = o0.shape[0]

    def body(o0_ref, l0_ref, o1_ref, o2_ref, l1_ref, l2_ref, y_ref, lt_ref, lt1_ref, lt2_ref, so, sl):
        for gi, (o_ref, l_ref, r) in enumerate(((o1_ref, l1_ref, DILS[0]), (o2_ref, l2_ref, DILS[1]))):
            _from_residues(so.at[gi], o_ref, r, GW)
            _from_residues(sl.at[gi:gi + 1], l_ref, r, LANES)
        ls = [l0_ref[...], sl[0], sl[1]]
        m = functools.reduce(jnp.maximum, ls)
        ws = [jnp.exp(l - m) for l in ls]
        den = ws[0] + ws[1] + ws[2]
        alphas = [w / den for w in ws]
        lt = m + jnp.log(den)
        lt_ref[...] = lt
        sl[2] = lt
        _to_residues(lt1_ref, sl.at[2:3], DILS[0], LANES)
        _to_residues(lt2_ref, sl.at[2:3], DILS[1], LANES)
        for h in range(HEADS):
            cols = slice(h * HEAD_DIM, (h + 1) * HEAD_DIM)
            jj, lo = divmod(h * HEAD_DIM, LANES)
            acc = alphas[0][:, h:h + 1] * o0_ref[:, cols].astype(f32)
            for gi in range(2):
                acc = acc + alphas[gi + 1][:, h:h + 1] * so[gi, jj, :, lo:lo + HEAD_DIM]
            y_ref[:, cols] = acc.astype(bf16)

    in_specs = [_tok_spec(GW), _tok_spec(LANES), _res_spec(DILS[0], GW), _res_spec(DILS[1], GW), _res_spec(DILS[0], LANES), _res_spec(DILS[1], LANES)]
    out_specs = [_tok_spec(GW), _tok_spec(LANES), _res_spec(DILS[0], LANES), _res_spec(DILS[1], LANES)]
    return pl.pallas_call(
        body, grid=(B, SEQ // RT), in_specs=in_specs, out_specs=out_specs,
        out_shape=(SDS((B, SEQ, GW), bf16), SDS((B, SEQ, LANES), f32)) + tuple(SDS((B, r, SEQ // r, LANES), f32) for r in DILS),
        scratch_shapes=[pltpu.VMEM((2, RCH, RT, LANES), f32), pltpu.VMEM((3, RT, LANES), f32)],
        name="attn_mix", compiler_params=_cp(("parallel", "parallel")))(o0, l0, *o_res, *l_res)


def _attn_delta(dmix, mix):
    B = dmix.shape[0]

    def body(d_ref, y_ref, dl_ref, dl1_ref, dl2_ref, dm1_ref, dm2_ref, sx, sd):
        lane = lax.broadcasted_iota(jnp.int32, (RT, LANES), 1)
        acc = jnp.zeros((RT, LANES), f32)
        dv = d_ref[...].astype(f32)
        for jj in range(RCH):
            sx[jj] = dv[:, jj * LANES:(jj + 1) * LANES]
        for h in range(HEADS):
            cols = slice(h * HEAD_DIM, (h + 1) * HEAD_DIM)
            dl = jnp.sum(dv[:, cols] * y_ref[:, cols].astype(f32), axis=-1, keepdims=True)
            acc = jnp.where(lane == h, dl, acc)
        dl_ref[...] = acc
        sd[0] = acc
        _to_residues(dl1_ref, sd, DILS[0], LANES)
        _to_residues(dl2_ref, sd, DILS[1], LANES)
        _to_residues(dm1_ref, sx, DILS[0], GW)
        _to_residues(dm2_ref, sx, DILS[1], GW)

    return pl.pallas_call(
        body, grid=(B, SEQ // RT), in_specs=[_tok_spec(GW), _tok_spec(GW)],
        out_specs=[_tok_spec(LANES), _res_spec(DILS[0], LANES), _res_spec(DILS[1], LANES), _res_spec(DILS[0], GW), _res_spec(DILS[1], GW)],
        out_shape=(SDS((B, SEQ, LANES), f32),) + tuple(SDS((B, r, SEQ // r, LANES), f32) for r in DILS)
        + tuple(SDS((B, r, SEQ // r, GW), bf16) for r in DILS),
        scratch_shapes=[pltpu.VMEM((RCH, RT, LANES), f32), pltpu.VMEM((1, RT, LANES), f32)],
        name="attn_delta", compiler_params=_cp(("parallel", "parallel")))(dmix, mix)


N_CB = IN_W // CB


def _assemble_dproj(dqkv, dua, dub, dga, dgc):
    B = dua.shape[0]
    ng = len(GROUPS)
    flat = [dqkv[g][p] for p in range(3) for g in range(ng)]
    wide = [dua, dub, dga, dgc]

    def body(*refs):
        srcs, wides, o_ref, scr = refs[:3 * ng], refs[3 * ng:3 * ng + 4], refs[3 * ng + 4], refs[3 * ng + 5]
        j = pl.program_id(2)
        for jv in range(3 * ng):
            g = jv % ng

            @pl.when(j == jv)
            def _(jv=jv, g=g):
                if g == 0:
                    o_ref[...] = srcs[jv][...]
                else:
                    _from_residues(scr, srcs[jv], GROUPS[g][1], GW)
                    for jj in range(RCH):
                        o_ref[:, jj * LANES:(jj + 1) * LANES] = scr[jj].astype(bf16)
        for wv in range(4):
            @pl.when(jnp.logical_and(j >= 3 * ng + 2 * wv, j < 3 * ng + 2 * wv + 2))
            def _(wv=wv):
                o_ref[...] = wides[wv][...]

    in_specs = [_tok_spec(GW) if (jv % ng) == 0 else _res_spec(GROUPS[jv % ng][1], GW) for jv in range(3 * ng)]
    in_specs += [pl.BlockSpec((None, RT, CB), lambda b, i, j, wv=wv: (b, i, jnp.clip(j - (3 * ng + 2 * wv), 0, 1))) for wv in range(4)]
    return pl.pallas_call(
        body, grid=(B, SEQ // RT, N_CB), in_specs=in_specs, out_specs=pl.BlockSpec((None, RT, CB), lambda b, i, j: (b, i, j)),
        out_shape=SDS((B, SEQ, IN_W), bf16), scratch_shapes=[pltpu.VMEM((RCH, RT, LANES), f32)],
        name="assemble_dproj", compiler_params=_cp(("parallel", "parallel", "arbitrary")))(*flat, *wide)


def _add_pairs(part, got, core, name):
    n, h, C = got.shape

    def body(c_ref, a_ref, b_ref, o_ref):
        o_ref[...] = (a_ref[...].astype(f32) + b_ref[...].astype(f32)).astype(bf16)

    blk = pl.BlockSpec((None, h, C), lambda s, c_ref: (s, 0, 0))
    own = pl.BlockSpec((None, h, C), lambda s, c_ref: (s, c_ref[0], 0))
    spec = pltpu.PrefetchScalarGridSpec(num_scalar_prefetch=1, grid=(n,), in_specs=[own, blk], out_specs=blk)
    return pl.pallas_call(body, grid_spec=spec, out_shape=SDS((n, h, C), bf16), name=name,
                          compiler_params=_cp(("parallel",)))(core, part, got)


def _sum4(sums, got, chip, core, name):
    _, h, C = sums.shape

    def body(s_ref, c_ref, own_ref, q_ref, o_ref):
        t = own_ref[...].astype(f32) + q_ref[0].astype(f32)
        t = t + q_ref[1].astype(f32)
        o_ref[...] = t + q_ref[2].astype(f32)

    spec = pltpu.PrefetchScalarGridSpec(
        num_scalar_prefetch=2, grid=(1,),
        in_specs=[pl.BlockSpec((None, h, C), lambda i, s_ref, c_ref: (s_ref[0], 0, 0)),
                  pl.BlockSpec((N_CHIPS - 1, h, C), lambda i, s_ref, c_ref: (0, 0, 0))],
        out_specs=pl.BlockSpec((h, C), lambda i, s_ref, c_ref: (c_ref[0], 0)))
    return pl.pallas_call(body, grid_spec=spec, out_shape=SDS((2 * h, C), f32), name=name,
                          compiler_params=_cp(("arbitrary",)))(chip, core, sums, got)


def _adamw(w, g, m, v, name):
    R, C = w.shape
    rt = R
    for cand in (512, 256, 128, 64, 32, 16, 8):
        if R % cand == 0 and cand * C * 4 <= 2 ** 21:
            rt = cand
            break
    c1 = 1.0 / (1.0 - B1 ** STEP)
    c2 = 1.0 / (1.0 - B2 ** STEP)

    def body(w_ref, g_ref, m_ref, v_ref, d_ref, nm_ref, nv_ref):
        gv = g_ref[...]
        nm = B1 * m_ref[...] + (1.0 - B1) * gv
        nv = B2 * v_ref[...] + (1.0 - B2) * (gv * gv)
        nm_ref[...] = nm
        nv_ref[...] = nv
        d_ref[...] = -LR * ((nm * c1) / (jnp.sqrt(nv * c2) + ADAM_EPS) + WD * w_ref[...])

    blk = pl.BlockSpec((rt, C), lambda i: (i, 0))
    sd = SDS((R, C), f32)
    return pl.pallas_call(body, grid=(R // rt,), in_specs=[blk] * 4, out_specs=[blk] * 3, out_shape=(sd, sd, sd), name=name,
                          compiler_params=_cp(("parallel",)))(w, g, m, v)


def _coords():
    return lax.axis_index("x"), lax.axis_index("y"), lax.axis_index("c")


def _other_chips(x, y):
    return [(1 - x, y), (x, 1 - y), (1 - x, 1 - y)]


def _allgather_weights(shards, slots):
    n = len(shards)
    halves = [s.shape[0] // 2 for s in shards]

    def body(*refs):
        ins, outs = refs[:n], refs[2 * n:3 * n]
        send1, recv1, send2, recv2 = refs[3 * n:]
        x, y, c = _coords()
        me = 2 * x + y
        chips = _other_chips(x, y)
        sends = []
        for w in range(n):
            mine = pl.ds(c * halves[w], halves[w])
            for j, (px, py) in enumerate(chips):
                cp = pltpu.make_async_remote_copy(src_ref=ins[w].at[mine, :], dst_ref=outs[w].at[me, mine, :],
                                                  send_sem=send1.at[w, j], recv_sem=recv1.at[w, j],
                                                  device_id=(px, py, c), device_id_type=MESH)
                cp.start()
                sends.append(cp)
        for w in range(n):
            mine = pl.ds(c * halves[w], halves[w])
            for j, (px, py) in enumerate(chips):
                blk = outs[w].at[2 * px + py, mine, :]
                pltpu.make_async_remote_copy(src_ref=blk, dst_ref=blk, send_sem=send1.at[w, j], recv_sem=recv1.at[w, j],
                                             device_id=(px, py, c), device_id_type=MESH).wait_recv()
                cp = pltpu.make_async_remote_copy(src_ref=blk, dst_ref=blk, send_sem=send2.at[w, j], recv_sem=recv2.at[w, j],
                                                  device_id=(x, y, 1 - c), device_id_type=MESH)
                cp.start()
                sends.append(cp)
        for w in range(n):
            theirs = pl.ds((1 - c) * halves[w], halves[w])
            for j, (px, py) in enumerate(chips):
                blk = outs[w].at[2 * px + py, theirs, :]
                pltpu.make_async_remote_copy(src_ref=blk, dst_ref=blk, send_sem=send2.at[w, j], recv_sem=recv2.at[w, j],
                                             device_id=(x, y, 1 - c), device_id_type=MESH).wait_recv()
        for cp in sends:
            cp.wait_send()

    return pl.pallas_call(
        body, in_specs=[HBM] * (2 * n), out_specs=[HBM] * n,
        out_shape=tuple(SDS((N_CHIPS,) + s.shape, s.dtype) for s in shards),
        input_output_aliases={n + w: w for w in range(n)},
        scratch_shapes=[pltpu.SemaphoreType.DMA((n, 3))] * 4,
        name="allgather_weights", compiler_params=pltpu.CompilerParams(has_side_effects=True))(*shards, *slots)


def _pair_exchange(parts, name):
    n = len(parts)
    halves = [p.shape[1] // 2 for p in parts]

    def body(*refs):
        ins, got_o = refs[:n], refs[n:2 * n]
        send, recv = refs[2 * n:]
        x, y, c = _coords()
        cps = []
        for w in range(n):
            give = pl.ds((1 - c) * halves[w], halves[w])
            rc = pltpu.make_async_remote_copy(src_ref=ins[w].at[:, give, :], dst_ref=got_o[w], send_sem=send.at[w],
                                              recv_sem=recv.at[w], device_id=(x, y, 1 - c), device_id_type=MESH)
            rc.start()
            cps.append(rc)
        for rc in cps:
            rc.wait()

    hs = tuple(SDS((N_CHIPS, p.shape[1] // 2, p.shape[2]), p.dtype) for p in parts)
    return pl.pallas_call(body, in_specs=[HBM] * n, out_specs=[HBM] * n, out_shape=hs,
                          scratch_shapes=[pltpu.SemaphoreType.DMA((n,))] * 2, name=name,
                          compiler_params=pltpu.CompilerParams(has_side_effects=True))(*parts)


def _x_chip_exchange(sums):
    n = len(sums)

    def copies(ins, outs, sems):
        send, recv = sems
        x, y, c = _coords()
        return [pltpu.make_async_remote_copy(src_ref=ins[w].at[2 * px + py], dst_ref=outs[w].at[j], send_sem=send.at[w, j],
                                             recv_sem=recv.at[w, j], device_id=(px, py, c), device_id_type=MESH)
                for w in range(n) for j, (px, py) in enumerate(_other_chips(x, y))]

    def start(ins, outs, sems):
        for cp in copies(ins, outs, sems):
            cp.start()

    def wait(ins, outs, sems):
        for cp in copies(ins, outs, sems):
            cp.wait()

    return _Exchange(ins=list(sums), out_shape=[SDS((N_CHIPS - 1,) + s.shape[1:], s.dtype) for s in sums],
                     scratch=[pltpu.SemaphoreType.DMA((n, 3)), pltpu.SemaphoreType.DMA((n, 3))], aliases={}, start=start, wait=wait)


def _x_gather_ici(shards, slots):
    n = len(shards)
    halves = [s.shape[0] // 2 for s in shards]

    def copies(ins, outs, sems):
        send, recv = sems
        x, y, c = _coords()
        me = 2 * x + y
        out = []
        for w in range(n):
            mine = pl.ds(c * halves[w], halves[w])
            for j, (px, py) in enumerate(_other_chips(x, y)):
                snd = pltpu.make_async_remote_copy(src_ref=ins[w].at[mine, :], dst_ref=outs[w].at[me, mine, :], send_sem=send.at[w, j],
                                                   recv_sem=recv.at[w, j], device_id=(px, py, c), device_id_type=MESH)
                got = outs[w].at[2 * px + py, mine, :]
                rcv = pltpu.make_async_remote_copy(src_ref=got, dst_ref=got, send_sem=send.at[w, j], recv_sem=recv.at[w, j],
                                                   device_id=(px, py, c), device_id_type=MESH)
                out.append((snd, rcv))
        return out

    def start(ins, outs, sems):
        for snd, _ in copies(ins, outs, sems):
            snd.start()

    def wait(ins, outs, sems):
        for snd, rcv in copies(ins, outs, sems):
            rcv.wait_recv()
            snd.wait_send()

    return _Exchange(ins=list(shards) + list(slots), out_shape=[SDS(s.shape, s.dtype) for s in slots],
                     scratch=[pltpu.SemaphoreType.DMA((n, 3)), pltpu.SemaphoreType.DMA((n, 3))],
                     aliases={n + w: w for w in range(n)}, start=start, wait=wait)


def _x_gather_fwd(bufs):
    n = len(bufs)
    halves = [b.shape[1] // 2 for b in bufs]

    def copies(ins, outs, sems):
        send, recv = sems
        x, y, c = _coords()
        out = []
        for w in range(n):
            for j, (px, py) in enumerate(_other_chips(x, y)):
                mine = outs[w].at[2 * px + py, pl.ds(c * halves[w], halves[w]), :]
                theirs = outs[w].at[2 * px + py, pl.ds((1 - c) * halves[w], halves[w]), :]
                snd = pltpu.make_async_remote_copy(src_ref=mine, dst_ref=mine, send_sem=send.at[w, j], recv_sem=recv.at[w, j],
                                                   device_id=(x, y, 1 - c), device_id_type=MESH)
                rcv = pltpu.make_async_remote_copy(src_ref=theirs, dst_ref=theirs, send_sem=send.at[w, j], recv_sem=recv.at[w, j],
                                                   device_id=(x, y, 1 - c), device_id_type=MESH)
                out.append((snd, rcv))
        return out

    def start(ins, outs, sems):
        for snd, _ in copies(ins, outs, sems):
            snd.start()

    def wait(ins, outs, sems):
        for snd, rcv in copies(ins, outs, sems):
            rcv.wait_recv()
            snd.wait_send()

    return _Exchange(ins=list(bufs), out_shape=[SDS(b.shape, b.dtype) for b in bufs],
                     scratch=[pltpu.SemaphoreType.DMA((n, 3)), pltpu.SemaphoreType.DMA((n, 3))],
                     aliases={w: w for w in range(n)}, start=start, wait=wait)


def _pair_gather(bufs):
    n = len(bufs)

    def body(*refs):
        outs = refs[n:2 * n]
        send, recv = refs[2 * n:]
        x, y, c = _coords()
        cps = []
        for w in range(n):
            h = bufs[w].shape[0] // 2
            rows = outs[w].at[pl.ds(c * h, h), :]
            rc = pltpu.make_async_remote_copy(src_ref=rows, dst_ref=rows, send_sem=send.at[w], recv_sem=recv.at[w],
                                              device_id=(x, y, 1 - c), device_id_type=MESH)
            rc.start()
            cps.append(rc)
        for w, rc in enumerate(cps):
            h = bufs[w].shape[0] // 2
            other = outs[w].at[pl.ds((1 - c) * h, h), :]
            pltpu.make_async_remote_copy(src_ref=other, dst_ref=other, send_sem=send.at[w], recv_sem=recv.at[w],
                                         device_id=(x, y, 1 - c), device_id_type=MESH).wait_recv()
            rc.wait_send()

    return pl.pallas_call(body, in_specs=[HBM] * n, out_specs=[HBM] * n, out_shape=tuple(SDS(a.shape, a.dtype) for a in bufs),
                          input_output_aliases={w: w for w in range(n)},
                          scratch_shapes=[pltpu.SemaphoreType.DMA((n,))] * 2, name="grad_pair_gather",
                          compiler_params=pltpu.CompilerParams(has_side_effects=True))(*bufs)


def _small_allreduce(vec, name):
    R = vec.shape[0]
    nd = 8

    def body(v_ref, o_ref, buf, send, recv):
        x, y, c = _coords()
        me = 4 * x + 2 * y + c
        buf[me] = v_ref[...]
        cps = []
        for k in range(1, nd):
            kx, ky, kc = (k >> 2) & 1, (k >> 1) & 1, k & 1
            tx = x + kx - 2 * x * kx
            ty = y + ky - 2 * y * ky
            tc = c + kc - 2 * c * kc
            cp = pltpu.make_async_remote_copy(src_ref=v_ref, dst_ref=buf.at[me], send_sem=send.at[k], recv_sem=recv.at[k],
                                              device_id=(tx, ty, tc), device_id_type=MESH)
            cp.start()
            cps.append((cp, 4 * tx + 2 * ty + tc))
        for k, (cp, src) in zip(range(1, nd), cps):
            pltpu.make_async_remote_copy(src_ref=v_ref, dst_ref=buf.at[src], send_sem=send.at[k], recv_sem=recv.at[k],
                                         device_id=(x, y, c), device_id_type=MESH).wait_recv()
        for cp, _ in cps:
            cp.wait_send()
        acc = buf[0]
        for d in range(1, nd):
            acc = acc + buf[d]
        o_ref[...] = acc

    vm = pl.BlockSpec(memory_space=pltpu.VMEM)
    return pl.pallas_call(body, in_specs=[vm], out_specs=vm, out_shape=SDS((R, LANES), f32),
                          scratch_shapes=[pltpu.VMEM((nd, R, LANES), f32), pltpu.SemaphoreType.DMA((nd,)), pltpu.SemaphoreType.DMA((nd,))],
                          name=name, compiler_params=pltpu.CompilerParams(has_side_effects=True))(vec)


def _rows128(a, rows):
    flat = a.reshape(-1)
    return jnp.pad(flat, (0, rows * LANES - flat.shape[0])).reshape(rows, LANES)


GATHER_1 = ("w_conv_out", "w_attn_out", "w_o", "w_ffn_gate")
GATHER_2 = ("w_ffn_up", "w_ffn_down")
REDUCE_A = ("w_ffn_gate", "w_ffn_up", "w_ffn_down")
REDUCE_B = ("w_conv_out", "w_attn_out", "w_o")
REDUCE_C = ("w_in",)


def _step(x, target, norm1_g, gate_b, conv_w, conv_b, conv_ln_g, conv_ln_b, norm2_g, norm_f_g, shards, slots, chip1, core):
    B = x.shape[0]
    T = B * SEQ
    xf = x.reshape(T, D_MODEL)
    tf = target.reshape(T, D_MODEL)

    (w_in,) = _allgather_weights([shards["w_in"]], [slots["w_in"]])
    h = _rms_fwd(xf, norm1_g, "rms1_fwd")
    proj, got1 = _mm("in_proj", h, w_in, grid=(N_CHIPS, T // 512),
                     a_spec=pl.BlockSpec((512, D_MODEL), lambda s, m: (m, 0)),
                     b_spec=pl.BlockSpec((None, D_MODEL, IN_S), lambda s, m: (s, 0, 0)),
                     o_spec=pl.BlockSpec((512, IN_S), lambda s, m: (m, s)), o_shape=(T, IN_W), o_dtype=bf16, dims=NN,
                     comm=_x_gather_ici([shards[n] for n in GATHER_1], [slots[n] for n in GATHER_1]))
    proj3 = proj.reshape(B, SEQ, IN_W)
    proj4 = proj.reshape(B, 1, SEQ, IN_W)

    qkv = [None] + [_qkv_to_residues(proj3, g) for g in range(1, len(GROUPS))]

    def qkv_args(g):
        return (proj4, proj4, proj4, 0, 3, 6) if g == 0 else (qkv[g], qkv[g], qkv[g], 0, 1, 2)

    (o4_0, l4_0), full1 = _attn_fwd(0, *qkv_args(0), comm=_x_gather_fwd(list(got1)))
    (o4_1, l4_1), got2 = _attn_fwd(1, *qkv_args(1), comm=_x_gather_ici([shards[n] for n in GATHER_2], [slots[n] for n in GATHER_2]))
    (o4_2, l4_2), full2 = _attn_fwd(2, *qkv_args(2), comm=_x_gather_fwd(list(got2)))
    full = dict(zip(GATHER_1 + GATHER_2, full1 + full2))
    w_conv_out, w_attn_out, w_o = full["w_conv_out"], full["w_attn_out"], full["w_o"]
    w_gate, w_up, w_down = full["w_ffn_gate"], full["w_ffn_up"], full["w_ffn_down"]
    w_conv_out_f = w_conv_out.reshape(D_MODEL, D_MODEL)
    w_o_f = w_o.reshape(D_MODEL, D_MODEL)
    mix3, lse3, lse_r1, lse_r2 = _attn_mix(o4_0.reshape(B, SEQ, GW), l4_0.reshape(B, SEQ, LANES), [o4_1, o4_2], [l4_1, l4_2])
    mix = mix3.reshape(T, GW)
    y_attn = _mm("attn_out", mix, w_attn_out, grid=(N_CHIPS, T // 512),
                 a_spec=pl.BlockSpec((512, GW), lambda s, m: (m, 0)),
                 b_spec=pl.BlockSpec((None, GW, D_MODEL // N_CHIPS), lambda s, m: (s, 0, 0)),
                 o_spec=pl.BlockSpec((512, D_MODEL // N_CHIPS), lambda s, m: (m, s)), o_shape=(T, D_MODEL), o_dtype=bf16,
                 dims=NN, sem=("parallel", "parallel"))

    c1 = _glu_conv_fwd(proj3, conv_w, conv_b).reshape(T, D_MODEL)
    c3 = _ln_silu_fwd(c1, conv_ln_g, conv_ln_b)
    y_conv = _mm_nn_full("conv_out", c3, w_conv_out_f, bf16)

    merged = _merge_fwd(proj, gate_b, y_attn, y_conv)
    x1 = _mm_nn_full("o_proj", merged, w_o_f, f32, res=xf)

    h2 = _rms_fwd(x1, norm2_g, "rms2_fwd")
    fa, fb, ff = _ffn_up(h2, w_gate, w_up)
    x2 = _mm("ffn_down", ff, w_down, grid=(T // 1024, N_CHIPS),
             a_spec=pl.BlockSpec((None, 1024, FF_S), lambda m, s: (s, m, 0)),
             b_spec=pl.BlockSpec((None, FF_S, D_MODEL), lambda m, s: (s, 0, 0)),
             o_spec=pl.BlockSpec((1024, D_MODEL), lambda m, s: (m, 0)), o_shape=(T, D_MODEL), o_dtype=f32, dims=NN,
             acc_shape=(1024, D_MODEL), k_axis=1, res=x1, res_spec=pl.BlockSpec((1024, D_MODEL), lambda m, s: (m, 0)),
             sem=("parallel", "arbitrary"))

    loss, dx2, d_gf = _final_loss(x2, norm_f_g.reshape(1, D_MODEL), tf)

    d_w_down = _mm("d_w_down", ff, dx2, grid=(N_CHIPS, T // 1024),
                   a_spec=pl.BlockSpec((None, 1024, FF_S), lambda s, k: (s, k, 0)),
                   b_spec=pl.BlockSpec((1024, D_MODEL), lambda s, k: (k, 0)),
                   o_spec=pl.BlockSpec((None, FF_S, D_MODEL), lambda s, k: (s, 0, 0)), o_shape=(N_CHIPS, FF_S, D_MODEL),
                   o_dtype=bf16, dims=TN, acc_shape=(FF_S, D_MODEL), k_axis=1, sem=("parallel", "arbitrary"))
    da, db = _ffn_down_bwd(dx2, w_down, fa, fb)

    def d_w_ff(name, dz):
        return _mm(name, h2, dz, grid=(N_CHIPS, T // 2048),
                   a_spec=pl.BlockSpec((2048, D_MODEL), lambda s, k: (k, 0)),
                   b_spec=pl.BlockSpec((None, 2048, FF_S), lambda s, k: (s, k, 0)),
                   o_spec=pl.BlockSpec((None, D_MODEL, FF_S), lambda s, k: (s, 0, 0)), o_shape=(N_CHIPS, D_MODEL, FF_S),
                   o_dtype=bf16, dims=TN, acc_shape=(D_MODEL, FF_S), k_axis=1, sem=("parallel", "arbitrary"))

    d_w_gate = d_w_ff("d_w_gate", da)
    d_w_up = d_w_ff("d_w_up", db)
    part = dict(w_ffn_gate=d_w_gate, w_ffn_up=d_w_up, w_ffn_down=d_w_down)

    def pair_sums(names, tag):
        got = _pair_exchange([part[n] for n in names], f"grad_pair_exchange_{tag}")
        return [_add_pairs(part[n], g_, core, f"pair_sum_{n}") for n, g_ in zip(names, got)]

    sums_a = pair_sums(REDUCE_A, "a")
    dh2 = _ffn_dh2(da, w_gate, db, w_up)
    dx1, d_g2 = _rms_bwd(dh2, x1, norm2_g, dx2, "rms2_bwd")

    d_w_o = _mm_tn_tokens("d_w_o", merged, dx1, bf16).reshape(N_CHIPS, D_MODEL // N_CHIPS, D_MODEL)
    dmerged = _mm_nt_full("d_merged", dx1, w_o_f, bf16)
    dya, dyc, dga, dgc, d_gba, d_gbc = _merge_bwd(dmerged, proj, gate_b, y_attn, y_conv)

    d_w_conv_out = _mm_tn_tokens("d_w_conv_out", c3, dyc, bf16).reshape(N_CHIPS, D_MODEL // N_CHIPS, D_MODEL)
    dc3 = _mm_nt_full("d_c3", dyc, w_conv_out_f, bf16)
    dc1, d_ln_g, d_ln_b = _ln_silu_bwd(dc3, c1, conv_ln_g, conv_ln_b)
    (dua, dub, d_conv_w, d_conv_b), got_a = _glu_conv_bwd(dc1.reshape(B, SEQ, D_MODEL), proj3, conv_w, comm=_x_chip_exchange(sums_a))

    d_w_attn_out = _mm("d_w_attn_out", mix, dya, grid=(N_CHIPS, T // 512),
                       a_spec=pl.BlockSpec((512, GW), lambda s, k: (k, 0)),
                       b_spec=pl.BlockSpec((512, D_MODEL // N_CHIPS), lambda s, k: (k, s)),
                       o_spec=pl.BlockSpec((None, GW, D_MODEL // N_CHIPS), lambda s, k: (s, 0, 0)),
                       o_shape=(N_CHIPS, GW, D_MODEL // N_CHIPS), o_dtype=bf16, dims=TN, acc_shape=(GW, D_MODEL // N_CHIPS),
                       k_axis=1, sem=("parallel", "arbitrary"))
    dmix = _mm("d_mix", dya, w_attn_out, grid=(T // 512, N_CHIPS),
               a_spec=pl.BlockSpec((512, D_MODEL // N_CHIPS), lambda m, s: (m, s)),
               b_spec=pl.BlockSpec((None, GW, D_MODEL // N_CHIPS), lambda m, s: (s, 0, 0)),
               o_spec=pl.BlockSpec((512, GW), lambda m, s: (m, 0)), o_shape=(T, GW), o_dtype=bf16, dims=NT,
               acc_shape=(512, GW), k_axis=1, sem=("parallel", "arbitrary"))
    dmix3 = dmix.reshape(B, SEQ, GW)
    delta3, delta_r1, delta_r2, dmix_r1, dmix_r2 = _attn_delta(dmix3, mix3)
    part.update(w_conv_out=d_w_conv_out, w_attn_out=d_w_attn_out, w_o=d_w_o)
    sums_b = pair_sums(REDUCE_B, "b")
    one = (B, 1, SEQ)
    (dq0, dk0, dv0), got_b = _attn_bwd(0, *qkv_args(0), dmix3.reshape(one + (GW,)), lse3.reshape(one + (LANES,)),
                                       delta3.reshape(one + (LANES,)), comm=_x_chip_exchange(sums_b))
    dqkv = [tuple(t.reshape(B, SEQ, GW) for t in (dq0, dk0, dv0)),
            _attn_bwd(1, *qkv_args(1), dmix_r1, lse_r1, delta_r1), _attn_bwd(2, *qkv_args(2), dmix_r2, lse_r2, delta_r2)]
    dproj = _assemble_dproj(dqkv, dua, dub, dga.reshape(B, SEQ, D_MODEL), dgc.reshape(B, SEQ, D_MODEL)).reshape(T, IN_W)

    d_w_in = _mm("d_w_in", h, dproj, grid=(N_CHIPS, T // 1024),
                 a_spec=pl.BlockSpec((1024, D_MODEL), lambda s, k: (k, 0)),
                 b_spec=pl.BlockSpec((1024, IN_S), lambda s, k: (k, s)),
                 o_spec=pl.BlockSpec((None, D_MODEL, IN_S), lambda s, k: (s, 0, 0)), o_shape=(N_CHIPS, D_MODEL, IN_S),
                 o_dtype=bf16, dims=TN, acc_shape=(D_MODEL, IN_S), k_axis=1, sem=("parallel", "arbitrary"))
    part.update(w_in=d_w_in)
    sums_c = pair_sums(REDUCE_C, "c")
    dh, got_c = _mm("d_h", dproj, w_in, grid=(T // 512, N_CHIPS),
                    a_spec=pl.BlockSpec((512, IN_S), lambda m, s: (m, s)),
                    b_spec=pl.BlockSpec((None, D_MODEL, IN_S), lambda m, s: (s, 0, 0)),
                    o_spec=pl.BlockSpec((512, D_MODEL), lambda m, s: (m, 0)), o_shape=(T, D_MODEL), o_dtype=f32, dims=NT,
                    acc_shape=(512, D_MODEL), k_axis=1, comm=_x_chip_exchange(sums_c))
    dx, d_g1 = _rms_bwd(dh, xf, norm1_g, dx1, "rms1_bwd")

    names = REDUCE_A + REDUCE_B + REDUCE_C
    halves = [_sum4(s, g_, chip1, core, f"chip_sum_{n}") for n, s, g_ in zip(names, sums_a + sums_b + sums_c, got_a + got_b + got_c)]
    big = dict(zip(names, _pair_gather(halves)))
    small = dict(norm1_g=d_g1, gate_b=jnp.concatenate([d_gba, d_gbc], axis=-1), conv_b=d_conv_b, conv_ln_g=d_ln_g,
                 conv_ln_b=d_ln_b, norm2_g=d_g2, norm_f_g=d_gf, conv_w=d_conv_w)
    return loss, dx.reshape(B, SEQ, D_MODEL), big, small


BIG = ("w_in", "w_conv_out", "w_attn_out", "w_o", "w_ffn_gate", "w_ffn_up", "w_ffn_down")
SMALL = ("norm1_g", "gate_b", "conv_b", "conv_ln_g", "conv_ln_b", "norm2_g", "norm_f_g")
SMALL_ROWS = {"norm1_g": 8, "gate_b": 16, "conv_b": 8, "conv_ln_g": 8, "conv_ln_b": 8, "norm2_g": 8, "norm_f_g": 8}
LOSS_ROWS = 8
CONVW_ROWS = 32 * D_MODEL // LANES


def kernel(x, norm1_g, w_in, gate_b, conv_w, conv_b, conv_ln_g, conv_ln_b, w_conv_out, w_attn_out, w_o, norm2_g, w_ffn_gate, w_ffn_up, w_ffn_down, norm_f_g, loss_target, m_norm1_g, m_w_in, m_gate_b, m_conv_w, m_conv_b, m_conv_ln_g, m_conv_ln_b, m_w_conv_out, m_w_attn_out, m_w_o, m_norm2_g, m_w_ffn_gate, m_w_ffn_up, m_w_ffn_down, m_norm_f_g, v_norm1_g, v_w_in, v_gate_b, v_conv_w, v_conv_b, v_conv_ln_g, v_conv_ln_b, v_w_conv_out, v_w_attn_out, v_w_o, v_norm2_g, v_w_ffn_gate, v_w_ffn_up, v_w_ffn_down, v_norm_f_g):
    W = dict(norm1_g=norm1_g, w_in=w_in, gate_b=gate_b, conv_w=conv_w, conv_b=conv_b, conv_ln_g=conv_ln_g, conv_ln_b=conv_ln_b,
             w_conv_out=w_conv_out, w_attn_out=w_attn_out, w_o=w_o, norm2_g=norm2_g, w_ffn_gate=w_ffn_gate, w_ffn_up=w_ffn_up,
             w_ffn_down=w_ffn_down, norm_f_g=norm_f_g)
    M = dict(norm1_g=m_norm1_g, w_in=m_w_in, gate_b=m_gate_b, conv_w=m_conv_w, conv_b=m_conv_b, conv_ln_g=m_conv_ln_g,
             conv_ln_b=m_conv_ln_b, w_conv_out=m_w_conv_out, w_attn_out=m_w_attn_out, w_o=m_w_o, norm2_g=m_norm2_g,
             w_ffn_gate=m_w_ffn_gate, w_ffn_up=m_w_ffn_up, w_ffn_down=m_w_ffn_down, norm_f_g=m_norm_f_g)
    V = dict(norm1_g=v_norm1_g, w_in=v_w_in, gate_b=v_gate_b, conv_w=v_conv_w, conv_b=v_conv_b, conv_ln_g=v_conv_ln_g,
             conv_ln_b=v_conv_ln_b, w_conv_out=v_w_conv_out, w_attn_out=v_w_attn_out, w_o=v_w_o, norm2_g=v_norm2_g,
             w_ffn_gate=v_w_ffn_gate, w_ffn_up=v_w_ffn_up, w_ffn_down=v_w_ffn_down, norm_f_g=v_norm_f_g)
    order = list(W)
    shard2d = {n: W[n].reshape(W[n].shape[-2:]) for n in BIG}
    chip = 2 * lax.axis_index("x") + lax.axis_index("y")

    cw = jnp.zeros((CONV_K, D_MODEL), f32)
    cw = lax.dynamic_update_slice(cw, 0.5 * conv_w.reshape(CONV_K, D_MODEL // N_CHIPS), (0, chip * (D_MODEL // N_CHIPS)))
    conv_w_full = _small_allreduce(_rows128(cw, CONVW_ROWS), "conv_w_gather")[: CONV_K * D_MODEL // LANES].reshape(CONV_K, D_MODEL)

    core = lax.axis_index("c").astype(jnp.int32).reshape(1)
    chip1 = chip.astype(jnp.int32).reshape(1)
    shards = {n: shard2d[n].astype(bf16) for n in BIG}
    slots = {n: lax.dynamic_update_slice(lax.empty((N_CHIPS,) + s.shape, bf16), s[None], (chip, 0, 0)) for n, s in shards.items()}

    loss, grad_x, grads, gsmall = _step(x, loss_target, norm1_g, gate_b, conv_w_full, conv_b, conv_ln_g, conv_ln_b, norm2_g,
                                        norm_f_g, shards, slots, chip1, core)

    pieces = [_rows128(loss, LOSS_ROWS)] + [_rows128(gsmall[n], SMALL_ROWS[n]) for n in SMALL] + [_rows128(gsmall["conv_w"], CONVW_ROWS)]
    tot = _small_allreduce(jnp.concatenate(pieces, axis=0), "small_allreduce")
    loss_out = tot[0, 0]
    row = LOSS_ROWS
    for n in SMALL:
        grads[n] = tot[row: row + W[n].size // LANES].reshape(W[n].shape)
        row += SMALL_ROWS[n]
    dcw = tot[row: row + CONV_K * D_MODEL // LANES].reshape(CONV_K, D_MODEL)
    grads["conv_w"] = lax.dynamic_slice(dcw, (0, chip * (D_MODEL // N_CHIPS)), (CONV_K, D_MODEL // N_CHIPS))

    delta, new_m, new_v = {}, {}, {}
    for n in BIG:
        d, nm, nv = _adamw(shard2d[n], grads[n], M[n].reshape(shard2d[n].shape), V[n].reshape(shard2d[n].shape), f"adamw_{n}")
        delta[n], new_m[n], new_v[n] = d.reshape(W[n].shape), nm.reshape(W[n].shape), nv.reshape(W[n].shape)
        grads[n] = grads[n].reshape(W[n].shape)

    def pack(src):
        return jnp.concatenate([_rows128(src[n], SMALL_ROWS[n]) for n in SMALL], axis=0)

    d, nm, nv = _adamw(pack(W), pack(grads), pack(M), pack(V), "adamw_small")
    row = 0
    for n in SMALL:
        k = W[n].size // LANES
        delta[n], new_m[n], new_v[n] = (t[row: row + k].reshape(W[n].shape) for t in (d, nm, nv))
        row += SMALL_ROWS[n]

    def pad32(a):
        return jnp.pad(a.reshape(CONV_K, D_MODEL // N_CHIPS), ((0, 1), (0, 0)))

    d, nm, nv = _adamw(pad32(conv_w), pad32(grads["conv_w"]), pad32(m_conv_w), pad32(v_conv_w), "adamw_conv_w")
    delta["conv_w"], new_m["conv_w"], new_v["conv_w"] = (t[:CONV_K].reshape(conv_w.shape) for t in (d, nm, nv))
    grads["conv_w"] = grads["conv_w"].reshape(conv_w.shape)

    return (loss_out, grad_x, *[grads[n] for n in order], *[delta[n] for n in order],
            *[new_m[n] for n in order], *[new_v[n] for n in order])
```

```python
import functools
import math
from typing import Callable, NamedTuple

import numpy as np
import jax
import jax.numpy as jnp
from jax import lax
from jax.experimental import pallas as pl
from jax.experimental.pallas import tpu as pltpu

f32 = jnp.float32
bf16 = jnp.bfloat16
SDS = jax.ShapeDtypeStruct
MESH = pl.DeviceIdType.MESH

D_MODEL = 1024
SEQ = 2048
HEAD_DIM = 64
HEADS = 8
GROUPS = ((128, 1), (512, 4), (2048, 16))
GW = HEADS * HEAD_DIM
ATTN_W = len(GROUPS) * GW
Q_BLOCK = 128
CONV_K = 31
D_FF = 2816
IN_W = 3 * ATTN_W + 2 * D_MODEL + 2 * D_MODEL
N_CHIPS = 4
IN_S = IN_W // N_CHIPS
FF_S = D_FF // N_CHIPS
RMS_EPS = 1e-6
LN_EPS = 1e-5
LR, B1, B2, ADAM_EPS, WD, STEP = 0.001, 0.9, 0.999, 1e-08, 0.01, 10
NEG = -1e30
LANES = 128
VMEM_LIMIT = 48 * 2 ** 20
CB = 512
UA_CB, UB_CB, GA_CB, GC_CB = 9, 11, 13, 15


def _alibi_slope_list(n):
    def pow2(m):
        start = 2.0 ** (-8.0 / m)
        return [start ** (i + 1) for i in range(m)]
    if math.log2(n).is_integer():
        return pow2(n)
    c = 2 ** math.floor(math.log2(n))
    return pow2(c) + _alibi_slope_list(2 * c)[0::2][: n - c]


_SLOPES = np.asarray(sorted(_alibi_slope_list(len(GROUPS) * HEADS), reverse=True), dtype=np.float32).reshape(len(GROUPS), HEADS)


def _cp(sem=None, vmem=VMEM_LIMIT):
    return pltpu.CompilerParams(dimension_semantics=sem, vmem_limit_bytes=vmem)


def _sigmoid(x):
    return 1.0 / (1.0 + jnp.exp(-x))


HBM = pl.BlockSpec(memory_space=pl.ANY)


class _Exchange(NamedTuple):
    ins: list
    out_shape: list
    scratch: list
    aliases: dict
    start: Callable
    wait: Callable


def _launch(body, *, name, grid, in_specs, out_specs, out_shape, args, scratch_shapes=(), sem=None, comm=None):
    if comm is None:
        return pl.pallas_call(body, grid=grid, in_specs=in_specs, out_specs=out_specs, out_shape=out_shape,
                              scratch_shapes=list(scratch_shapes), name=name, compiler_params=_cp(sem))(*args)
    multi = isinstance(out_shape, (tuple, list))
    m_out = list(out_shape) if multi else [out_shape]
    m_ospec = list(out_specs) if multi else [out_specs]
    n_in, n_out, n_scr = len(in_specs), len(m_out), len(scratch_shapes)
    nc_in, nc_out = len(comm.ins), len(comm.out_shape)

    def hosted(*refs):
        bounds = np.cumsum([0, n_in, nc_in, n_out, nc_out, n_scr])
        mi, ci, mo, co, ms = (refs[a:b] for a, b in zip(bounds[:-1], bounds[1:]))
        cs = refs[bounds[-1]:]
        ids = [pl.program_id(a) for a in range(len(grid))]
        first = functools.reduce(jnp.logical_and, [i == 0 for i in ids])
        last = functools.reduce(jnp.logical_and, [i == g - 1 for i, g in zip(ids, grid)])

        @pl.when(first)
        def _():
            comm.start(ci, co, cs)

        body(*mi, *mo, *ms)

        @pl.when(last)
        def _():
            comm.wait(ci, co, cs)

    res = pl.pallas_call(
        hosted, grid=grid, in_specs=list(in_specs) + [HBM] * nc_in, out_specs=m_ospec + [HBM] * nc_out,
        out_shape=tuple(m_out) + tuple(comm.out_shape), scratch_shapes=list(scratch_shapes) + list(comm.scratch),
        input_output_aliases={n_in + i: n_out + o for i, o in comm.aliases.items()}, name=name + "_comm",
        compiler_params=pltpu.CompilerParams(dimension_semantics=("arbitrary",) * len(grid), vmem_limit_bytes=VMEM_LIMIT,
                                             has_side_effects=True))(*args, *comm.ins)
    return (tuple(res[:n_out]) if multi else res[0]), tuple(res[n_out:])


def _run_exchange(ex, name):
    n_in = len(ex.ins)

    def body(*refs):
        ins, outs, sems = refs[:n_in], refs[n_in:n_in + len(ex.out_shape)], refs[n_in + len(ex.out_shape):]
        ex.start(ins, outs, sems)
        ex.wait(ins, outs, sems)

    return pl.pallas_call(body, in_specs=[HBM] * n_in, out_specs=[HBM] * len(ex.out_shape), out_shape=tuple(ex.out_shape),
                          scratch_shapes=list(ex.scratch), input_output_aliases=dict(ex.aliases), name=name,
                          compiler_params=pltpu.CompilerParams(has_side_effects=True))(*ex.ins)


def _rms_fwd(x, g, name):
    T = x.shape[0]
    tm = 512

    def body(x_ref, g_ref, o_ref):
        xv = x_ref[...]
        r = lax.rsqrt(jnp.mean(xv * xv, axis=-1, keepdims=True) + RMS_EPS)
        o_ref[...] = (xv * r * g_ref[...]).astype(o_ref.dtype)

    row = pl.BlockSpec((tm, D_MODEL), lambda i: (i, 0))
    vec = pl.BlockSpec((1, D_MODEL), lambda i: (0, 0))
    return pl.pallas_call(body, grid=(T // tm,), in_specs=[row, vec], out_specs=row,
                          out_shape=SDS((T, D_MODEL), bf16), name=name, compiler_params=_cp(("parallel",)))(x, g)


def _rms_bwd(dy, x, g, dres, name):
    T = x.shape[0]
    tm = 512

    def body(dy_ref, x_ref, g_ref, r_ref, dx_ref, dg_ref):
        xv = x_ref[...]
        r = lax.rsqrt(jnp.mean(xv * xv, axis=-1, keepdims=True) + RMS_EPS)
        xh = xv * r
        dyv = dy_ref[...]
        dxh = dyv * g_ref[...]
        dx_ref[...] = r_ref[...] + r * (dxh - xh * jnp.mean(dxh * xh, axis=-1, keepdims=True))
        part = jnp.sum(dyv * xh, axis=0, keepdims=True)

        @pl.when(pl.program_id(0) == 0)
        def _():
            dg_ref[...] = part

        @pl.when(pl.program_id(0) > 0)
        def _():
            dg_ref[...] += part

    row = pl.BlockSpec((tm, D_MODEL), lambda i: (i, 0))
    vec = pl.BlockSpec((1, D_MODEL), lambda i: (0, 0))
    return pl.pallas_call(body, grid=(T // tm,), in_specs=[row, row, vec, row], out_specs=[row, vec],
                          out_shape=(SDS((T, D_MODEL), f32), SDS((1, D_MODEL), f32)), name=name,
                          compiler_params=_cp(("arbitrary",)))(dy, x, g, dres)


def _final_loss(x2, gf, target):
    T = x2.shape[0]
    tm = 512

    def body(x_ref, g_ref, t_ref, loss_ref, dx_ref, dg_ref):
        xv = x_ref[...]
        gv = g_ref[...]
        r = lax.rsqrt(jnp.mean(xv * xv, axis=-1, keepdims=True) + RMS_EPS)
        xh = xv * r
        e = xh * gv - t_ref[...]
        part_l = jnp.broadcast_to(0.5 * jnp.sum(jnp.mean(e * e, axis=-1, keepdims=True), axis=0, keepdims=True), (1, LANES))
        dy = e * (1.0 / D_MODEL)
        dxh = dy * gv
        dx_ref[...] = r * (dxh - xh * jnp.mean(dxh * xh, axis=-1, keepdims=True))
        part_g = jnp.sum(dy * xh, axis=0, keepdims=True)

        @pl.when(pl.program_id(0) == 0)
        def _():
            loss_ref[...] = part_l
            dg_ref[...] = part_g

        @pl.when(pl.program_id(0) > 0)
        def _():
            loss_ref[...] += part_l
            dg_ref[...] += part_g

    row = pl.BlockSpec((tm, D_MODEL), lambda i: (i, 0))
    vec = pl.BlockSpec((1, D_MODEL), lambda i: (0, 0))
    lvec = pl.BlockSpec((1, LANES), lambda i: (0, 0))
    return pl.pallas_call(body, grid=(T // tm,), in_specs=[row, vec, row], out_specs=[lvec, row, vec],
                          out_shape=(SDS((1, LANES), f32), SDS((T, D_MODEL), f32), SDS((1, D_MODEL), f32)),
                          name="final_loss", compiler_params=_cp(("arbitrary",)))(x2, gf, target)


def _ln_silu_fwd(c1, g, b):
    T = c1.shape[0]
    tm = 512

    def body(c_ref, g_ref, b_ref, o_ref):
        cv = c_ref[...]
        mu = jnp.mean(cv, axis=-1, keepdims=True)
        cc = cv - mu
        var = jnp.mean(cc * cc, axis=-1, keepdims=True)
        c2 = cc * lax.rsqrt(var + LN_EPS) * g_ref[...] + b_ref[...]
        o_ref[...] = (c2 * _sigmoid(c2)).astype(o_ref.dtype)

    row = pl.BlockSpec((tm, D_MODEL), lambda i: (i, 0))
    vec = pl.BlockSpec((1, D_MODEL), lambda i: (0, 0))
    return pl.pallas_call(body, grid=(T // tm,), in_specs=[row, vec, vec], out_specs=row,
                          out_shape=SDS((T, D_MODEL), bf16), name="ln_silu_fwd", compiler_params=_cp(("parallel",)))(c1, g, b)


def _ln_silu_bwd(dc3, c1, g, b):
    T = c1.shape[0]
    tm = 512

    def body(d_ref, c_ref, g_ref, b_ref, dc_ref, dg_ref, db_ref):
        cv = c_ref[...]
        gv = g_ref[...]
        mu = jnp.mean(cv, axis=-1, keepdims=True)
        cc = cv - mu
        var = jnp.mean(cc * cc, axis=-1, keepdims=True)
        rs = lax.rsqrt(var + LN_EPS)
        xh = cc * rs
        c2 = xh * gv + b_ref[...]
        sg = _sigmoid(c2)
        dc2 = d_ref[...].astype(f32) * (sg * (1.0 + c2 * (1.0 - sg)))
        dxh = dc2 * gv
        dc_ref[...] = rs * (dxh - jnp.mean(dxh, axis=-1, keepdims=True) - xh * jnp.mean(dxh * xh, axis=-1, keepdims=True))
        pg = jnp.sum(dc2 * xh, axis=0, keepdims=True)
        pb = jnp.sum(dc2, axis=0, keepdims=True)

        @pl.when(pl.program_id(0) == 0)
        def _():
            dg_ref[...] = pg
            db_ref[...] = pb

        @pl.when(pl.program_id(0) > 0)
        def _():
            dg_ref[...] += pg
            db_ref[...] += pb

    row = pl.BlockSpec((tm, D_MODEL), lambda i: (i, 0))
    vec = pl.BlockSpec((1, D_MODEL), lambda i: (0, 0))
    return pl.pallas_call(body, grid=(T // tm,), in_specs=[row, row, vec, vec], out_specs=[row, vec, vec],
                          out_shape=(SDS((T, D_MODEL), f32), SDS((1, D_MODEL), f32), SDS((1, D_MODEL), f32)),
                          name="ln_silu_bwd", compiler_params=_cp(("arbitrary",)))(dc3, c1, g, b)


NN = (((1,), (0,)), ((), ()))
NT = (((1,), (1,)), ((), ()))
TN = (((0,), (0,)), ((), ()))


def _mm(name, a, b, *, grid, a_spec, b_spec, o_spec, o_shape, o_dtype, dims, acc_shape=None, k_axis=None,
        res=None, res_spec=None, sem=None, comm=None):
    nk = 1 if k_axis is None else grid[k_axis]

    def body(*refs):
        if res is None:
            a_ref, b_ref, o_ref = refs[:3]
            r_ref, scr = None, refs[3:]
        else:
            a_ref, b_ref, r_ref, o_ref = refs[:4]
            scr = refs[4:]
        p = lax.dot_general(a_ref[...].astype(bf16), b_ref[...].astype(bf16), dims, preferred_element_type=f32)
        if nk == 1:
            if r_ref is not None:
                p = p + r_ref[...]
            o_ref[...] = p.astype(o_dtype)
            return
        acc = scr[0]
        k = pl.program_id(k_axis)

        @pl.when(k == 0)
        def _():
            acc[...] = p

        @pl.when(k > 0)
        def _():
            acc[...] += p

        @pl.when(k == nk - 1)
        def _():
            t = acc[...]
            if r_ref is not None:
                t = t + r_ref[...]
            o_ref[...] = t.astype(o_dtype)

    ins = [a, b] + ([] if res is None else [res])
    specs = [a_spec, b_spec] + ([] if res is None else [res_spec])
    scratch = [] if nk == 1 else [pltpu.VMEM(acc_shape, f32)]
    return _launch(body, name=name, grid=grid, in_specs=specs, out_specs=o_spec, out_shape=SDS(o_shape, o_dtype),
                   args=ins, scratch_shapes=scratch, sem=sem, comm=comm)


def _mm_nn_full(name, a, b, o_dtype, res=None, tm=512):
    T, K = a.shape
    N = b.shape[1]
    return _mm(name, a, b, grid=(T // tm,), a_spec=pl.BlockSpec((tm, K), lambda i: (i, 0)),
               b_spec=pl.BlockSpec((K, N), lambda i: (0, 0)), o_spec=pl.BlockSpec((tm, N), lambda i: (i, 0)),
               o_shape=(T, N), o_dtype=o_dtype, dims=NN, res=res,
               res_spec=None if res is None else pl.BlockSpec((tm, N), lambda i: (i, 0)), sem=("parallel",))


def _mm_nt_full(name, a, b, o_dtype, tm=512):
    T, N = a.shape
    K = b.shape[0]
    return _mm(name, a, b, grid=(T // tm,), a_spec=pl.BlockSpec((tm, N), lambda i: (i, 0)),
               b_spec=pl.BlockSpec((K, N), lambda i: (0, 0)), o_spec=pl.BlockSpec((tm, K), lambda i: (i, 0)),
               o_shape=(T, K), o_dtype=o_dtype, dims=NT, sem=("parallel",))


def _mm_tn_tokens(name, a, b, o_dtype, tk=1024):
    T, K = a.shape
    N = b.shape[1]
    return _mm(name, a, b, grid=(T // tk,), a_spec=pl.BlockSpec((tk, K), lambda k: (k, 0)),
               b_spec=pl.BlockSpec((tk, N), lambda k: (k, 0)), o_spec=pl.BlockSpec((K, N), lambda k: (0, 0)),
               o_shape=(K, N), o_dtype=o_dtype, dims=TN, acc_shape=(K, N), k_axis=0, sem=("arbitrary",))


def _ffn_up(h2, wg, wu):
    T = h2.shape[0]
    tm = 512

    def body(h_ref, wg_ref, wu_ref, a_ref, b_ref, f_ref):
        hv = h_ref[...]
        av = lax.dot_general(hv, wg_ref[...], NT, preferred_element_type=f32)
        bv = lax.dot_general(hv, wu_ref[...], NT, preferred_element_type=f32)
        a_ref[...] = av.astype(bf16)
        b_ref[...] = bv.astype(bf16)
        f_ref[...] = (av * _sigmoid(av) * bv).astype(bf16)

    wspec = pl.BlockSpec((None, FF_S, D_MODEL), lambda s, m: (s, 0, 0))
    ospec = pl.BlockSpec((None, tm, FF_S), lambda s, m: (s, m, 0))
    osd = SDS((N_CHIPS, T, FF_S), bf16)
    return pl.pallas_call(body, grid=(N_CHIPS, T // tm),
                          in_specs=[pl.BlockSpec((tm, D_MODEL), lambda s, m: (m, 0)), wspec, wspec],
                          out_specs=[ospec, ospec, ospec], out_shape=(osd, osd, osd), name="ffn_up",
                          compiler_params=_cp(("parallel", "parallel")))(h2, wg, wu)


def _ffn_down_bwd(dx2, wd, a, b):
    T = dx2.shape[0]
    tm = 512

    def body(d_ref, w_ref, a_ref, b_ref, da_ref, db_ref):
        df = lax.dot_general(d_ref[...].astype(bf16), w_ref[...], NT, preferred_element_type=f32)
        av = a_ref[...].astype(f32)
        sg = _sigmoid(av)
        da_ref[...] = (df * b_ref[...].astype(f32) * (sg * (1.0 + av * (1.0 - sg)))).astype(bf16)
        db_ref[...] = (df * av * sg).astype(bf16)

    aspec = pl.BlockSpec((None, tm, FF_S), lambda s, m: (s, m, 0))
    osd = SDS((N_CHIPS, T, FF_S), bf16)
    return pl.pallas_call(body, grid=(N_CHIPS, T // tm),
                          in_specs=[pl.BlockSpec((tm, D_MODEL), lambda s, m: (m, 0)),
                                    pl.BlockSpec((None, FF_S, D_MODEL), lambda s, m: (s, 0, 0)), aspec, aspec],
                          out_specs=[aspec, aspec], out_shape=(osd, osd), name="ffn_down_bwd",
                          compiler_params=_cp(("parallel", "parallel")))(dx2, wd, a, b)


def _ffn_dh2(da, wg, db, wu):
    T = da.shape[1]
    tm = 1024

    def body(da_ref, wg_ref, db_ref, wu_ref, o_ref, acc):
        p = jnp.dot(da_ref[...], wg_ref[...], preferred_element_type=f32)
        p = p + jnp.dot(db_ref[...], wu_ref[...], preferred_element_type=f32)
        s = pl.program_id(1)

        @pl.when(s == 0)
        def _():
            acc[...] = p

        @pl.when(s > 0)
        def _():
            acc[...] += p

        @pl.when(s == N_CHIPS - 1)
        def _():
            o_ref[...] = acc[...]

    aspec = pl.BlockSpec((None, tm, FF_S), lambda m, s: (s, m, 0))
    wspec = pl.BlockSpec((None, FF_S, D_MODEL), lambda m, s: (s, 0, 0))
    return pl.pallas_call(body, grid=(T // tm, N_CHIPS), in_specs=[aspec, wspec, aspec, wspec],
                          out_specs=pl.BlockSpec((tm, D_MODEL), lambda m, s: (m, 0)), out_shape=SDS((T, D_MODEL), f32),
                          scratch_shapes=[pltpu.VMEM((tm, D_MODEL), f32)], name="ffn_dh2",
                          compiler_params=_cp(("parallel", "arbitrary")))(da, wg, db, wu)


def _merge_fwd(proj, gate_b, ya, yc):
    T = proj.shape[0]
    tm = 512

    def body(ga_ref, gc_ref, ba_ref, bc_ref, ya_ref, yc_ref, o_ref):
        sa = _sigmoid(ga_ref[...].astype(f32) + ba_ref[...])
        sc = _sigmoid(gc_ref[...].astype(f32) + bc_ref[...])
        o_ref[...] = (sa * ya_ref[...].astype(f32) + sc * yc_ref[...].astype(f32)).astype(bf16)

    blk = pl.BlockSpec((tm, CB), lambda i, j: (i, j))
    return pl.pallas_call(
        body, grid=(T // tm, 2),
        in_specs=[pl.BlockSpec((tm, CB), lambda i, j: (i, GA_CB + j)), pl.BlockSpec((tm, CB), lambda i, j: (i, GC_CB + j)),
                  pl.BlockSpec((1, CB), lambda i, j: (0, j)), pl.BlockSpec((1, CB), lambda i, j: (0, 2 + j)), blk, blk],
        out_specs=blk, out_shape=SDS((T, D_MODEL), bf16), name="merge_fwd",
        compiler_params=_cp(("parallel", "parallel")))(proj, proj, gate_b, gate_b, ya, yc)


def _merge_bwd(dm, proj, gate_b, ya, yc):
    T = proj.shape[0]
    tm = 512

    def body(dm_ref, ga_ref, gc_ref, ba_ref, bc_ref, ya_ref, yc_ref, dya_ref, dyc_ref, dga_ref, dgc_ref, dba_ref, dbc_ref):
        dmv = dm_ref[...].astype(f32)
        sa = _sigmoid(ga_ref[...].astype(f32) + ba_ref[...])
        sc = _sigmoid(gc_ref[...].astype(f32) + bc_ref[...])
        dya_ref[...] = (dmv * sa).astype(bf16)
        dyc_ref[...] = (dmv * sc).astype(bf16)
        dga = dmv * ya_ref[...].astype(f32) * sa * (1.0 - sa)
        dgc = dmv * yc_ref[...].astype(f32) * sc * (1.0 - sc)
        dga_ref[...] = dga.astype(bf16)
        dgc_ref[...] = dgc.astype(bf16)
        pa = jnp.sum(dga, axis=0, keepdims=True)
        pc = jnp.sum(dgc, axis=0, keepdims=True)

        @pl.when(pl.program_id(1) == 0)
        def _():
            dba_ref[...] = pa
            dbc_ref[...] = pc

        @pl.when(pl.program_id(1) > 0)
        def _():
            dba_ref[...] += pa
            dbc_ref[...] += pc

    blk = pl.BlockSpec((tm, CB), lambda j, i: (i, j))
    vec = pl.BlockSpec((1, CB), lambda j, i: (0, j))
    big = SDS((T, D_MODEL), bf16)
    small = SDS((1, D_MODEL), f32)
    return pl.pallas_call(
        body, grid=(2, T // tm),
        in_specs=[blk, pl.BlockSpec((tm, CB), lambda j, i: (i, GA_CB + j)), pl.BlockSpec((tm, CB), lambda j, i: (i, GC_CB + j)),
                  vec, pl.BlockSpec((1, CB), lambda j, i: (0, 2 + j)), blk, blk],
        out_specs=[blk, blk, blk, blk, vec, vec], out_shape=(big, big, big, big, small, small), name="merge_bwd",
        compiler_params=_cp(("parallel", "arbitrary")))(dm, proj, proj, gate_b, gate_b, ya, yc)


CONV_TS = 256
CONV_HALO = 32
CONV_RC = 64
CONV_WIN = CONV_TS + CONV_HALO
SUBLANES = 8


def _fill_shifted(win, sh):
    for b in range(1, SUBLANES):
        sh[b - 1] = win[pl.ds(b, CONV_WIN - SUBLANES), :]


def _rows_at(win, sh, row):
    a, b = divmod(row, SUBLANES)
    if b == 0:
        return win[pl.ds(row, CONV_RC), :]
    return sh[b - 1, pl.ds(a * SUBLANES, CONV_RC), :]


def _glu_conv_fwd(proj3, w, bias):
    B = proj3.shape[0]
    nt = SEQ // CONV_TS
    hb = CONV_TS // CONV_HALO

    def body(ua_ref, ub_ref, ha_ref, hb_ref, w_ref, b_ref, o_ref, win, sh):
        i = pl.program_id(2)
        c0 = ua_ref[...].astype(f32) * _sigmoid(ub_ref[...].astype(f32))
        halo = ha_ref[...].astype(f32) * _sigmoid(hb_ref[...].astype(f32))
        win[0:CONV_HALO, :] = jnp.where(i > 0, halo, 0.0)
        win[CONV_HALO:, :] = c0
        _fill_shifted(win, sh)
        for r0 in range(0, CONV_TS, CONV_RC):
            acc = jnp.zeros((CONV_RC, CB), f32) + b_ref[...]
            for k in range(CONV_K):
                acc = acc + _rows_at(win, sh, r0 + CONV_HALO - (CONV_K - 1) + k) * w_ref[k:k + 1, :]
            o_ref[r0:r0 + CONV_RC, :] = acc

    def cur(cb):
        return pl.BlockSpec((None, CONV_TS, CB), lambda b, j, i: (b, i, cb + j))

    def prev(cb):
        return pl.BlockSpec((None, CONV_HALO, CB), lambda b, j, i: (b, jnp.maximum(i * hb - 1, 0), cb + j))

    return pl.pallas_call(
        body, grid=(B, 2, nt),
        in_specs=[cur(UA_CB), cur(UB_CB), prev(UA_CB), prev(UB_CB),
                  pl.BlockSpec((CONV_K, CB), lambda b, j, i: (0, j)), pl.BlockSpec((1, CB), lambda b, j, i: (0, j))],
        out_specs=pl.BlockSpec((None, CONV_TS, CB), lambda b, j, i: (b, i, j)),
        out_shape=SDS((B, SEQ, D_MODEL), f32),
        scratch_shapes=[pltpu.VMEM((CONV_WIN, CB), f32), pltpu.VMEM((SUBLANES - 1, CONV_WIN - SUBLANES, CB), f32)],
        name="glu_conv_fwd", compiler_params=_cp(("parallel", "parallel", "parallel")))(proj3, proj3, proj3, proj3, w, bias)


def _glu_conv_bwd(dc1, proj3, w, comm=None):
    B = proj3.shape[0]
    nt = SEQ // CONV_TS
    hb = CONV_TS // CONV_HALO

    def body(d_ref, dn_ref, ua_ref, ub_ref, ha_ref, hb_ref, w_ref, dua_ref, dub_ref, dw_ref, db_ref, winc, wind, accw, shc, shd):
        b = pl.program_id(1)
        i = pl.program_id(2)
        first = jnp.logical_and(b == 0, i == 0)
        last = jnp.logical_and(b == B - 1, i == nt - 1)

        @pl.when(first)
        def _():
            accw[...] = jnp.zeros_like(accw)
            db_ref[...] = jnp.zeros_like(db_ref)

        halo = ha_ref[...].astype(f32) * _sigmoid(hb_ref[...].astype(f32))
        winc[0:CONV_HALO, :] = jnp.where(i > 0, halo, 0.0)
        winc[CONV_HALO:, :] = ua_ref[...].astype(f32) * _sigmoid(ub_ref[...].astype(f32))
        wind[0:CONV_TS, :] = d_ref[...]
        wind[CONV_TS:, :] = jnp.where(i < nt - 1, dn_ref[...], 0.0)
        db_ref[...] += jnp.sum(d_ref[...], axis=0, keepdims=True)
        _fill_shifted(winc, shc)
        _fill_shifted(wind, shd)
        for r0 in range(0, CONV_TS, CONV_RC):
            dc0 = jnp.zeros((CONV_RC, CB), f32)
            for k in range(CONV_K):
                dc0 = dc0 + _rows_at(wind, shd, r0 + (CONV_K - 1) - k) * w_ref[k:k + 1, :]
            uav = ua_ref[r0:r0 + CONV_RC, :].astype(f32)
            sg = _sigmoid(ub_ref[r0:r0 + CONV_RC, :].astype(f32))
            dua_ref[r0:r0 + CONV_RC, :] = (dc0 * sg).astype(bf16)
            dub_ref[r0:r0 + CONV_RC, :] = (dc0 * uav * sg * (1.0 - sg)).astype(bf16)
            dv = wind[r0:r0 + CONV_RC, :]
            for k in range(CONV_K):
                prod = dv * _rows_at(winc, shc, r0 + CONV_HALO - (CONV_K - 1) + k)
                accw[k] += jnp.sum(prod.reshape(CONV_RC // 8, 8, CB), axis=0)

        @pl.when(last)
        def _():
            for k in range(CONV_K):
                dw_ref[k:k + 1, :] = jnp.sum(accw[k], axis=0, keepdims=True)
            dw_ref[CONV_K:, :] = jnp.zeros((CONV_HALO - CONV_K, CB), f32)

    def cur(cb):
        return pl.BlockSpec((None, CONV_TS, CB), lambda j, b, i: (b, i, cb + j))

    def prev(cb):
        return pl.BlockSpec((None, CONV_HALO, CB), lambda j, b, i: (b, jnp.maximum(i * hb - 1, 0), cb + j))

    nxt = pl.BlockSpec((None, CONV_HALO, CB), lambda j, b, i: (b, jnp.minimum((i + 1) * hb, SEQ // CONV_HALO - 1), j))
    big = SDS((B, SEQ, D_MODEL), bf16)
    return _launch(
        body, name="glu_conv_bwd", grid=(2, B, nt),
        in_specs=[cur(0), nxt, cur(UA_CB), cur(UB_CB), prev(UA_CB), prev(UB_CB), pl.BlockSpec((CONV_K, CB), lambda j, b, i: (0, j))],
        out_specs=[cur(0), cur(0), pl.BlockSpec((CONV_HALO, CB), lambda j, b, i: (0, j)), pl.BlockSpec((1, CB), lambda j, b, i: (0, j))],
        out_shape=(big, big, SDS((CONV_HALO, D_MODEL), f32), SDS((1, D_MODEL), f32)),
        args=(dc1, dc1, proj3, proj3, proj3, proj3, w),
        scratch_shapes=[pltpu.VMEM((CONV_WIN, CB), f32), pltpu.VMEM((CONV_WIN, CB), f32), pltpu.VMEM((CONV_K, SUBLANES, CB), f32),
                        pltpu.VMEM((SUBLANES - 1, CONV_WIN - SUBLANES, CB), f32),
                        pltpu.VMEM((SUBLANES - 1, CONV_WIN - SUBLANES, CB), f32)],
        sem=("parallel", "arbitrary", "arbitrary"), comm=comm)


def _band(first, dil):
    kw = Q_BLOCK if first else 2 * Q_BLOCK
    qi = lax.broadcasted_iota(jnp.int32, (Q_BLOCK, kw), 0)
    kj = lax.broadcasted_iota(jnp.int32, (Q_BLOCK, kw), 1)
    rel = qi - kj + (0 if first else Q_BLOCK)
    valid = jnp.logical_and(rel >= 0, rel <= Q_BLOCK)
    return valid, rel.astype(f32) * float(dil)


def _scores(q, k, valid, dist, slope):
    s = lax.dot_general(q, k, NT, preferred_element_type=f32) * (HEAD_DIM ** -0.5) - slope * dist
    return jnp.where(valid, s, NEG)


def _attn_fwd(g, q4, k4, v4, qcb, kcb, vcb, comm=None):
    _, dil = GROUPS[g]
    B, r, L, _ = q4.shape
    nb = L // Q_BLOCK
    slopes = [float(s) for s in _SLOPES[g]]

    def body(q_ref, k_ref, v_ref, o_ref, lse_ref):
        lane = lax.broadcasted_iota(jnp.int32, (Q_BLOCK, LANES), 1)

        def block(n, first):
            q0 = 0 if first else pl.multiple_of(n * Q_BLOCK, Q_BLOCK)
            k0 = 0 if first else pl.multiple_of((n - 1) * Q_BLOCK, Q_BLOCK)
            kw = Q_BLOCK if first else 2 * Q_BLOCK
            valid, dist = _band(first, dil)
            st = jnp.zeros((Q_BLOCK, LANES), f32)
            for h in range(HEADS):
                cols = slice(h * HEAD_DIM, (h + 1) * HEAD_DIM)
                s = _scores(q_ref[pl.ds(q0, Q_BLOCK), cols], k_ref[pl.ds(k0, kw), cols], valid, dist, slopes[h])
                m = jnp.max(s, axis=-1, keepdims=True)
                p = jnp.exp(s - m)
                den = jnp.sum(p, axis=-1, keepdims=True)
                o = jnp.dot(p.astype(bf16), v_ref[pl.ds(k0, kw), cols], preferred_element_type=f32) / den
                o_ref[pl.ds(q0, Q_BLOCK), cols] = o.astype(bf16)
                st = jnp.where(lane == h, m + jnp.log(den), st)
            lse_ref[pl.ds(q0, Q_BLOCK), :] = st

        block(0, True)
        if nb > 1:
            def step(n, carry):
                block(n, False)
                return carry
            lax.fori_loop(1, nb, step, 0)

    def spec(cb):
        return pl.BlockSpec((None, None, L, GW), lambda b, c: (b, c, 0, cb))

    return _launch(
        body, name=f"attn_fwd_g{g}", grid=(B, r), in_specs=[spec(qcb), spec(kcb), spec(vcb)],
        out_specs=[spec(0), pl.BlockSpec((None, None, L, LANES), lambda b, c: (b, c, 0, 0))],
        out_shape=(SDS((B, r, L, GW), bf16), SDS((B, r, L, LANES), f32)), args=(q4, k4, v4),
        sem=("parallel", "parallel"), comm=comm)


def _attn_bwd(g, q4, k4, v4, qcb, kcb, vcb, do4, lse4, dl4, comm=None):
    _, dil = GROUPS[g]
    B, r, L, _ = q4.shape
    nb = L // Q_BLOCK
    slopes = [float(s) for s in _SLOPES[g]]
    scale = HEAD_DIM ** -0.5

    def body(q_ref, k_ref, v_ref, do_ref, lse_ref, dl_ref, dq_ref, dk_ref, dv_ref, dk_acc, dv_acc):
        dk_acc[...] = jnp.zeros_like(dk_acc)
        dv_acc[...] = jnp.zeros_like(dv_acc)

        def block(n, first):
            q0 = 0 if first else pl.multiple_of(n * Q_BLOCK, Q_BLOCK)
            k0 = 0 if first else pl.multiple_of((n - 1) * Q_BLOCK, Q_BLOCK)
            kw = Q_BLOCK if first else 2 * Q_BLOCK
            valid, dist = _band(first, dil)
            for h in range(HEADS):
                cols = slice(h * HEAD_DIM, (h + 1) * HEAD_DIM)
                qv = q_ref[pl.ds(q0, Q_BLOCK), cols]
                kv = k_ref[pl.ds(k0, kw), cols]
                vv = v_ref[pl.ds(k0, kw), cols]
                dov = do_ref[pl.ds(q0, Q_BLOCK), cols]
                s = _scores(qv, kv, valid, dist, slopes[h])
                p = jnp.exp(s - lse_ref[pl.ds(q0, Q_BLOCK), h:h + 1])
                dp = lax.dot_general(dov, vv, NT, preferred_element_type=f32)
                ds = (p * (dp - dl_ref[pl.ds(q0, Q_BLOCK), h:h + 1])).astype(bf16)
                dq_ref[pl.ds(q0, Q_BLOCK), cols] = (jnp.dot(ds, kv, preferred_element_type=f32) * scale).astype(bf16)
                dk_acc[pl.ds(k0, kw), cols] += lax.dot_general(ds, qv, TN, preferred_element_type=f32) * scale
                dv_acc[pl.ds(k0, kw), cols] += lax.dot_general(p.astype(bf16), dov, TN, preferred_element_type=f32)

        block(0, True)
        if nb > 1:
            def step(n, carry):
                block(n, False)
                return carry
            lax.fori_loop(1, nb, step, 0)
        dk_ref[...] = dk_acc[...].astype(bf16)
        dv_ref[...] = dv_acc[...].astype(bf16)

    def spec(cb):
        return pl.BlockSpec((None, None, L, GW), lambda b, c: (b, c, 0, cb))

    st = pl.BlockSpec((None, None, L, LANES), lambda b, c: (b, c, 0, 0))
    osd = SDS((B, r, L, GW), bf16)
    return _launch(
        body, name=f"attn_bwd_g{g}", grid=(B, r), in_specs=[spec(qcb), spec(kcb), spec(vcb), spec(0), st, st],
        out_specs=[spec(0), spec(0), spec(0)], out_shape=(osd, osd, osd), args=(q4, k4, v4, do4, lse4, dl4),
        scratch_shapes=[pltpu.VMEM((L, GW), f32), pltpu.VMEM((L, GW), f32)], sem=("parallel", "parallel"), comm=comm)


RT = 512
RCH = GW // LANES
DILS = tuple(d for _, d in GROUPS[1:])


def _res_spec(r, width):
    return pl.BlockSpec((None, r, RT // r, width), lambda b, i, *_: (b, 0, i, 0))


def _tok_spec(width, cb=0):
    return pl.BlockSpec((None, RT, width), lambda b, i, *_: (b, i, cb))


def _to_residues(res_ref, scr, r, width):
    for c in range(r):
        for jj in range(width // LANES):
            res_ref[c, :, jj * LANES:(jj + 1) * LANES] = scr[jj, pl.ds(c, RT // r, stride=r), :].astype(res_ref.dtype)


def _from_residues(scr, res_ref, r, width):
    for c in range(r):
        for jj in range(width // LANES):
            scr[jj, pl.ds(c, RT // r, stride=r), :] = res_ref[c, :, jj * LANES:(jj + 1) * LANES].astype(f32)


def _qkv_to_residues(proj3, g):
    r = GROUPS[g][1]
    B = proj3.shape[0]

    def body(q_ref, k_ref, v_ref, o_ref, scr):
        for p, x_ref in enumerate((q_ref, k_ref, v_ref)):
            x = x_ref[...].astype(f32)
            for jj in range(RCH):
                scr[p * RCH + jj] = x[:, jj * LANES:(jj + 1) * LANES]
        _to_residues(o_ref, scr, r, ATTN_W)

    return pl.pallas_call(
        body, grid=(B, SEQ // RT), in_specs=[_tok_spec(GW, 3 * p + g) for p in range(3)], out_specs=_res_spec(r, ATTN_W),
        out_shape=SDS((B, r, SEQ // r, ATTN_W), bf16), scratch_shapes=[pltpu.VMEM((3 * RCH, RT, LANES), f32)],
        name=f"qkv_to_residues_g{g}", compiler_params=_cp(("parallel", "parallel")))(proj3, proj3, proj3)


def _attn_mix(o0, l0, o_res, l_res):
    B = o0.shape[0]

    def body(o0_ref, l0_ref, o1_ref, o2_ref, l1_ref, l2_ref, y_ref, lt_ref, lt1_ref, lt2_ref, so, sl):
        for gi, (o_ref, l_ref, r) in enumerate(((o1_ref, l1_ref, DILS[0]), (o2_ref, l2_ref, DILS[1]))):
            _from_residues(so.at[gi], o_ref, r, GW)
            _from_residues(sl.at[gi:gi + 1], l_ref, r, LANES)
        ls = [l0_ref[...], sl[0], sl[1]]
        m = functools.reduce(jnp.maximum, ls)
        ws = [jnp.exp(l - m) for l in ls]
        den = ws[0] + ws[1] + ws[2]
        alphas = [w / den for w in ws]
        lt = m + jnp.log(den)
        lt_ref[...] = lt
        sl[2] = lt
        _to_residues(lt1_ref, sl.at[2:3], DILS[0], LANES)
        _to_residues(lt2_ref, sl.at[2:3], DILS[1], LANES)
        for h in range(HEADS):
            cols = slice(h * HEAD_DIM, (h + 1) * HEAD_DIM)
            jj, lo = divmod(h * HEAD_DIM, LANES)
            acc = alphas[0][:, h:h + 1] * o0_ref[:, cols].astype(f32)
            for gi in range(2):
                acc = acc + alphas[gi + 1][:, h:h + 1] * so[gi, jj, :, lo:lo + HEAD_DIM]
            y_ref[:, cols] = acc.astype(bf16)

    in_specs = [_tok_spec(GW), _tok_spec(LANES), _res_spec(DILS[0], GW), _res_spec(DILS[1], GW), _res_spec(DILS[0], LANES), _res_spec(DILS[1], LANES)]
    out_specs = [_tok_spec(GW), _tok_spec(LANES), _res_spec(DILS[0], LANES), _res_spec(DILS[1], LANES)]
    return pl.pallas_call(
        body, grid=(B, SEQ // RT), in_specs=in_specs, out_specs=out_specs,
        out_shape=(SDS((B, SEQ, GW), bf16), SDS((B, SEQ, LANES), f32)) + tuple(SDS((B, r, SEQ // r, LANES), f32) for r in DILS),
        scratch_shapes=[pltpu.VMEM((2, RCH, RT, LANES), f32), pltpu.VMEM((3, RT, LANES), f32)],
        name="attn_mix", compiler_params=_cp(("parallel", "parallel")))(o0, l0, *o_res, *l_res)


def _attn_delta(dmix, mix):
    B = dmix.shape[0]

    def body(d_ref, y_ref, dl_ref, dl1_ref, dl2_ref, dm1_ref, dm2_ref, sx, sd):
        lane = lax.broadcasted_iota(jnp.int32, (RT, LANES), 1)
        acc = jnp.zeros((RT, LANES), f32)
        dv = d_ref[...].astype(f32)
        for jj in range(RCH):
            sx[jj] = dv[:, jj * LANES:(jj + 1) * LANES]
        for h in range(HEADS):
            cols = slice(h * HEAD_DIM, (h + 1) * HEAD_DIM)
            dl = jnp.sum(dv[:, cols] * y_ref[:, cols].astype(f32), axis=-1, keepdims=True)
            acc = jnp.where(lane == h, dl, acc)
        dl_ref[...] = acc
        sd[0] = acc
        _to_residues(dl1_ref, sd, DILS[0], LANES)
        _to_residues(dl2_ref, sd, DILS[1], LANES)
        _to_residues(dm1_ref, sx, DILS[0], GW)
        _to_residues(dm2_ref, sx, DILS[1], GW)

    return pl.pallas_call(
        body, grid=(B, SEQ // RT), in_specs=[_tok_spec(GW), _tok_spec(GW)],
        out_specs=[_tok_spec(LANES), _res_spec(DILS[0], LANES), _res_spec(DILS[1], LANES), _res_spec(DILS[0], GW), _res_spec(DILS[1], GW)],
        out_shape=(SDS((B, SEQ, LANES), f32),) + tuple(SDS((B, r, SEQ // r, LANES), f32) for r in DILS)
        + tuple(SDS((B, r, SEQ // r, GW), bf16) for r in DILS),
        scratch_shapes=[pltpu.VMEM((RCH, RT, LANES), f32), pltpu.VMEM((1, RT, LANES), f32)],
        name="attn_delta", compiler_params=_cp(("parallel", "parallel")))(dmix, mix)


N_CB = IN_W // CB


def _assemble_dproj(dqkv, dua, dub, dga, dgc):
    B = dua.shape[0]
    ng = len(GROUPS)
    flat = [dqkv[g][p] for p in range(3) for g in range(ng)]
    wide = [dua, dub, dga, dgc]

    def body(*refs):
        srcs, wides, o_ref, scr = refs[:3 * ng], refs[3 * ng:3 * ng + 4], refs[3 * ng + 4], refs[3 * ng + 5]
        for jv in range(3 * ng):
            g = jv % ng
            if g == 0:
                o_ref[:, jv * GW:(jv + 1) * GW] = srcs[jv][...]
            else:
                _from_residues(scr, srcs[jv], GROUPS[g][1], GW)
                for jj in range(RCH):
                    o_ref[:, jv * GW + jj * LANES:jv * GW + (jj + 1) * LANES] = scr[jj].astype(bf16)
        for wv in range(4):
            lo = 3 * ATTN_W + wv * D_MODEL
            o_ref[:, lo:lo + D_MODEL] = wides[wv][...]

    in_specs = [_tok_spec(GW) if (jv % ng) == 0 else _res_spec(GROUPS[jv % ng][1], GW) for jv in range(3 * ng)]
    in_specs += [_tok_spec(D_MODEL)] * 4
    return pl.pallas_call(
        body, grid=(B, SEQ // RT), in_specs=in_specs, out_specs=_tok_spec(IN_W),
        out_shape=SDS((B, SEQ, IN_W), bf16), scratch_shapes=[pltpu.VMEM((RCH, RT, LANES), f32)],
        name="assemble_dproj", compiler_params=_cp(("parallel", "parallel")))(*flat, *wide)


def _add_pairs(part, got, core, name):
    n, h, C = got.shape

    def body(c_ref, a_ref, b_ref, o_ref):
        o_ref[...] = (a_ref[...].astype(f32) + b_ref[...].astype(f32)).astype(bf16)

    blk = pl.BlockSpec((None, h, C), lambda s, c_ref: (s, 0, 0))
    own = pl.BlockSpec((None, h, C), lambda s, c_ref: (s, c_ref[0], 0))
    spec = pltpu.PrefetchScalarGridSpec(num_scalar_prefetch=1, grid=(n,), in_specs=[own, blk], out_specs=blk)
    return pl.pallas_call(body, grid_spec=spec, out_shape=SDS((n, h, C), bf16), name=name,
                          compiler_params=_cp(("parallel",)))(core, part, got)


def _sum4(sums, got, chip, core, name):
    _, h, C = sums.shape

    def body(s_ref, c_ref, own_ref, q_ref, o_ref):
        t = own_ref[...].astype(f32) + q_ref[0].astype(f32)
        t = t + q_ref[1].astype(f32)
        o_ref[...] = t + q_ref[2].astype(f32)

    spec = pltpu.PrefetchScalarGridSpec(
        num_scalar_prefetch=2, grid=(1,),
        in_specs=[pl.BlockSpec((None, h, C), lambda i, s_ref, c_ref: (s_ref[0], 0, 0)),
                  pl.BlockSpec((N_CHIPS - 1, h, C), lambda i, s_ref, c_ref: (0, 0, 0))],
        out_specs=pl.BlockSpec((h, C), lambda i, s_ref, c_ref: (c_ref[0], 0)))
    return pl.pallas_call(body, grid_spec=spec, out_shape=SDS((2 * h, C), f32), name=name,
                          compiler_params=_cp(("arbitrary",)))(chip, core, sums, got)


def _adamw(w, g, m, v, name):
    R, C = w.shape
    rt = R
    for cand in (512, 256, 128, 64, 32, 16, 8):
        if R % cand == 0 and cand * C * 4 <= 2 ** 21:
            rt = cand
            break
    c1 = 1.0 / (1.0 - B1 ** STEP)
    c2 = 1.0 / (1.0 - B2 ** STEP)

    def body(w_ref, g_ref, m_ref, v_ref, d_ref, nm_ref, nv_ref):
        gv = g_ref[...]
        nm = B1 * m_ref[...] + (1.0 - B1) * gv
        nv = B2 * v_ref[...] + (1.0 - B2) * (gv * gv)
        nm_ref[...] = nm
        nv_ref[...] = nv
        d_ref[...] = -LR * ((nm * c1) / (jnp.sqrt(nv * c2) + ADAM_EPS) + WD * w_ref[...])

    blk = pl.BlockSpec((rt, C), lambda i: (i, 0))
    sd = SDS((R, C), f32)
    return pl.pallas_call(body, grid=(R // rt,), in_specs=[blk] * 4, out_specs=[blk] * 3, out_shape=(sd, sd, sd), name=name,
                          compiler_params=_cp(("parallel",)))(w, g, m, v)


def _coords():
    return lax.axis_index("x"), lax.axis_index("y"), lax.axis_index("c")


def _other_chips(x, y):
    return [(1 - x, y), (x, 1 - y), (1 - x, 1 - y)]


def _allgather_weights(shards, slots):
    n = len(shards)
    halves = [s.shape[0] // 2 for s in shards]

    def body(*refs):
        ins, outs = refs[:n], refs[2 * n:3 * n]
        send1, recv1, send2, recv2 = refs[3 * n:]
        x, y, c = _coords()
        me = 2 * x + y
        chips = _other_chips(x, y)
        sends = []
        for w in range(n):
            mine = pl.ds(c * halves[w], halves[w])
            for j, (px, py) in enumerate(chips):
                cp = pltpu.make_async_remote_copy(src_ref=ins[w].at[mine, :], dst_ref=outs[w].at[me, mine, :],
                                                  send_sem=send1.at[w, j], recv_sem=recv1.at[w, j],
                                                  device_id=(px, py, c), device_id_type=MESH)
                cp.start()
                sends.append(cp)
        for w in range(n):
            mine = pl.ds(c * halves[w], halves[w])
            for j, (px, py) in enumerate(chips):
                blk = outs[w].at[2 * px + py, mine, :]
                pltpu.make_async_remote_copy(src_ref=blk, dst_ref=blk, send_sem=send1.at[w, j], recv_sem=recv1.at[w, j],
                                             device_id=(px, py, c), device_id_type=MESH).wait_recv()
                cp = pltpu.make_async_remote_copy(src_ref=blk, dst_ref=blk, send_sem=send2.at[w, j], recv_sem=recv2.at[w, j],
                                                  device_id=(x, y, 1 - c), device_id_type=MESH)
                cp.start()
                sends.append(cp)
        for w in range(n):
            theirs = pl.ds((1 - c) * halves[w], halves[w])
            for j, (px, py) in enumerate(chips):
                blk = outs[w].at[2 * px + py, theirs, :]
                pltpu.make_async_remote_copy(src_ref=blk, dst_ref=blk, send_sem=send2.at[w, j], recv_sem=recv2.at[w, j],
                                             device_id=(x, y, 1 - c), device_id_type=MESH).wait_recv()
        for cp in sends:
            cp.wait_send()

    return pl.pallas_call(
        body, in_specs=[HBM] * (2 * n), out_specs=[HBM] * n,
        out_shape=tuple(SDS((N_CHIPS,) + s.shape, s.dtype) for s in shards),
        input_output_aliases={n + w: w for w in range(n)},
        scratch_shapes=[pltpu.SemaphoreType.DMA((n, 3))] * 4,
        name="allgather_weights", compiler_params=pltpu.CompilerParams(has_side_effects=True))(*shards, *slots)


def _pair_exchange(parts, name):
    n = len(parts)
    halves = [p.shape[1] // 2 for p in parts]

    def body(*refs):
        ins, got_o = refs[:n], refs[n:2 * n]
        send, recv = refs[2 * n:]
        x, y, c = _coords()
        cps = []
        for w in range(n):
            give = pl.ds((1 - c) * halves[w], halves[w])
            rc = pltpu.make_async_remote_copy(src_ref=ins[w].at[:, give, :], dst_ref=got_o[w], send_sem=send.at[w],
                                              recv_sem=recv.at[w], device_id=(x, y, 1 - c), device_id_type=MESH)
            rc.start()
            cps.append(rc)
        for rc in cps:
            rc.wait()

    hs = tuple(SDS((N_CHIPS, p.shape[1] // 2, p.shape[2]), p.dtype) for p in parts)
    return pl.pallas_call(body, in_specs=[HBM] * n, out_specs=[HBM] * n, out_shape=hs,
                          scratch_shapes=[pltpu.SemaphoreType.DMA((n,))] * 2, name=name,
                          compiler_params=pltpu.CompilerParams(has_side_effects=True))(*parts)


def _x_chip_exchange(sums):
    n = len(sums)

    def copies(ins, outs, sems):
        send, recv = sems
        x, y, c = _coords()
        return [pltpu.make_async_remote_copy(src_ref=ins[w].at[2 * px + py], dst_ref=outs[w].at[j], send_sem=send.at[w, j],
                                             recv_sem=recv.at[w, j], device_id=(px, py, c), device_id_type=MESH)
                for w in range(n) for j, (px, py) in enumerate(_other_chips(x, y))]

    def start(ins, outs, sems):
        for cp in copies(ins, outs, sems):
            cp.start()

    def wait(ins, outs, sems):
        for cp in copies(ins, outs, sems):
            cp.wait()

    return _Exchange(ins=list(sums), out_shape=[SDS((N_CHIPS - 1,) + s.shape[1:], s.dtype) for s in sums],
                     scratch=[pltpu.SemaphoreType.DMA((n, 3)), pltpu.SemaphoreType.DMA((n, 3))], aliases={}, start=start, wait=wait)


def _x_gather_ici(shards, slots):
    n = len(shards)
    halves = [s.shape[0] // 2 for s in shards]

    def copies(ins, outs, sems):
        send, recv = sems
        x, y, c = _coords()
        me = 2 * x + y
        out = []
        for w in range(n):
            mine = pl.ds(c * halves[w], halves[w])
            for j, (px, py) in enumerate(_other_chips(x, y)):
                snd = pltpu.make_async_remote_copy(src_ref=ins[w].at[mine, :], dst_ref=outs[w].at[me, mine, :], send_sem=send.at[w, j],
                                                   recv_sem=recv.at[w, j], device_id=(px, py, c), device_id_type=MESH)
                got = outs[w].at[2 * px + py, mine, :]
                rcv = pltpu.make_async_remote_copy(src_ref=got, dst_ref=got, send_sem=send.at[w, j], recv_sem=recv.at[w, j],
                                                   device_id=(px, py, c), device_id_type=MESH)
                out.append((snd, rcv))
        return out

    def start(ins, outs, sems):
        for snd, _ in copies(ins, outs, sems):
            snd.start()

    def wait(ins, outs, sems):
        for snd, rcv in copies(ins, outs, sems):
            rcv.wait_recv()
            snd.wait_send()

    return _Exchange(ins=list(shards) + list(slots), out_shape=[SDS(s.shape, s.dtype) for s in slots],
                     scratch=[pltpu.SemaphoreType.DMA((n, 3)), pltpu.SemaphoreType.DMA((n, 3))],
                     aliases={n + w: w for w in range(n)}, start=start, wait=wait)


def _x_gather_fwd(bufs):
    n = len(bufs)
    halves = [b.shape[1] // 2 for b in bufs]

    def copies(ins, outs, sems):
        send, recv = sems
        x, y, c = _coords()
        out = []
        for w in range(n):
            for j, (px, py) in enumerate(_other_chips(x, y)):
                mine = outs[w].at[2 * px + py, pl.ds(c * halves[w], halves[w]), :]
                theirs = outs[w].at[2 * px + py, pl.ds((1 - c) * halves[w], halves[w]), :]
                snd = pltpu.make_async_remote_copy(src_ref=mine, dst_ref=mine, send_sem=send.at[w, j], recv_sem=recv.at[w, j],
                                                   device_id=(x, y, 1 - c), device_id_type=MESH)
                rcv = pltpu.make_async_remote_copy(src_ref=theirs, dst_ref=theirs, send_sem=send.at[w, j], recv_sem=recv.at[w, j],
                                                   device_id=(x, y, 1 - c), device_id_type=MESH)
                out.append((snd, rcv))
        return out

    def start(ins, outs, sems):
        for snd, _ in copies(ins, outs, sems):
            snd.start()

    def wait(ins, outs, sems):
        for snd, rcv in copies(ins, outs, sems):
            rcv.wait_recv()
            snd.wait_send()

    return _Exchange(ins=list(bufs), out_shape=[SDS(b.shape, b.dtype) for b in bufs],
                     scratch=[pltpu.SemaphoreType.DMA((n, 3)), pltpu.SemaphoreType.DMA((n, 3))],
                     aliases={w: w for w in range(n)}, start=start, wait=wait)


def _pair_gather(bufs):
    n = len(bufs)

    def body(*refs):
        outs = refs[n:2 * n]
        send, recv = refs[2 * n:]
        x, y, c = _coords()
        cps = []
        for w in range(n):
            h = bufs[w].shape[0] // 2
            rows = outs[w].at[pl.ds(c * h, h), :]
            rc = pltpu.make_async_remote_copy(src_ref=rows, dst_ref=rows, send_sem=send.at[w], recv_sem=recv.at[w],
                                              device_id=(x, y, 1 - c), device_id_type=MESH)
            rc.start()
            cps.append(rc)
        for w, rc in enumerate(cps):
            h = bufs[w].shape[0] // 2
            other = outs[w].at[pl.ds((1 - c) * h, h), :]
            pltpu.make_async_remote_copy(src_ref=other, dst_ref=other, send_sem=send.at[w], recv_sem=recv.at[w],
                                         device_id=(x, y, 1 - c), device_id_type=MESH).wait_recv()
            rc.wait_send()

    return pl.pallas_call(body, in_specs=[HBM] * n, out_specs=[HBM] * n, out_shape=tuple(SDS(a.shape, a.dtype) for a in bufs),
                          input_output_aliases={w: w for w in range(n)},
                          scratch_shapes=[pltpu.SemaphoreType.DMA((n,))] * 2, name="grad_pair_gather",
                          compiler_params=pltpu.CompilerParams(has_side_effects=True))(*bufs)


def _small_allreduce(vec, name):
    R = vec.shape[0]
    nd = 8

    def body(v_ref, o_ref, buf, send, recv):
        x, y, c = _coords()
        me = 4 * x + 2 * y + c
        buf[me] = v_ref[...]
        cps = []
        for k in range(1, nd):
            kx, ky, kc = (k >> 2) & 1, (k >> 1) & 1, k & 1
            tx = x + kx - 2 * x * kx
            ty = y + ky - 2 * y * ky
            tc = c + kc - 2 * c * kc
            cp = pltpu.make_async_remote_copy(src_ref=v_ref, dst_ref=buf.at[me], send_sem=send.at[k], recv_sem=recv.at[k],
                                              device_id=(tx, ty, tc), device_id_type=MESH)
            cp.start()
            cps.append((cp, 4 * tx + 2 * ty + tc))
        for k, (cp, src) in zip(range(1, nd), cps):
            pltpu.make_async_remote_copy(src_ref=v_ref, dst_ref=buf.at[src], send_sem=send.at[k], recv_sem=recv.at[k],
                                         device_id=(x, y, c), device_id_type=MESH).wait_recv()
        for cp, _ in cps:
            cp.wait_send()
        acc = buf[0]
        for d in range(1, nd):
            acc = acc + buf[d]
        o_ref[...] = acc

    vm = pl.BlockSpec(memory_space=pltpu.VMEM)
    return pl.pallas_call(body, in_specs=[vm], out_specs=vm, out_shape=SDS((R, LANES), f32),
                          scratch_shapes=[pltpu.VMEM((nd, R, LANES), f32), pltpu.SemaphoreType.DMA((nd,)), pltpu.SemaphoreType.DMA((nd,))],
                          name=name, compiler_params=pltpu.CompilerParams(has_side_effects=True))(vec)


def _rows128(a, rows):
    flat = a.reshape(-1)
    return jnp.pad(flat, (0, rows * LANES - flat.shape[0])).reshape(rows, LANES)


GATHER_1 = ("w_conv_out", "w_attn_out", "w_o", "w_ffn_gate")
GATHER_2 = ("w_ffn_up", "w_ffn_down")
REDUCE_A = ("w_ffn_gate", "w_ffn_up", "w_ffn_down")
REDUCE_B = ("w_conv_out", "w_attn_out", "w_o")
REDUCE_C = ("w_in",)


def _step(x, target, norm1_g, gate_b, conv_w, conv_b, conv_ln_g, conv_ln_b, norm2_g, norm_f_g, shards, slots, chip1, core):
    B = x.shape[0]
    T = B * SEQ
    xf = x.reshape(T, D_MODEL)
    tf = target.reshape(T, D_MODEL)

    (w_in,) = _allgather_weights([shards["w_in"]], [slots["w_in"]])
    h = _rms_fwd(xf, norm1_g, "rms1_fwd")
    proj, got1 = _mm("in_proj", h, w_in, grid=(N_CHIPS, T // 512),
                     a_spec=pl.BlockSpec((512, D_MODEL), lambda s, m: (m, 0)),
                     b_spec=pl.BlockSpec((None, D_MODEL, IN_S), lambda s, m: (s, 0, 0)),
                     o_spec=pl.BlockSpec((512, IN_S), lambda s, m: (m, s)), o_shape=(T, IN_W), o_dtype=bf16, dims=NN,
                     comm=_x_gather_ici([shards[n] for n in GATHER_1], [slots[n] for n in GATHER_1]))
    proj3 = proj.reshape(B, SEQ, IN_W)
    proj4 = proj.reshape(B, 1, SEQ, IN_W)

    qkv = [None] + [_qkv_to_residues(proj3, g) for g in range(1, len(GROUPS))]

    def qkv_args(g):
        return (proj4, proj4, proj4, 0, 3, 6) if g == 0 else (qkv[g], qkv[g], qkv[g], 0, 1, 2)

    (o4_0, l4_0), full1 = _attn_fwd(0, *qkv_args(0), comm=_x_gather_fwd(list(got1)))
    (o4_1, l4_1), got2 = _attn_fwd(1, *qkv_args(1), comm=_x_gather_ici([shards[n] for n in GATHER_2], [slots[n] for n in GATHER_2]))
    (o4_2, l4_2), full2 = _attn_fwd(2, *qkv_args(2), comm=_x_gather_fwd(list(got2)))
    full = dict(zip(GATHER_1 + GATHER_2, full1 + full2))
    w_conv_out, w_attn_out, w_o = full["w_conv_out"], full["w_attn_out"], full["w_o"]
    w_gate, w_up, w_down = full["w_ffn_gate"], full["w_ffn_up"], full["w_ffn_down"]
    w_conv_out_f = w_conv_out.reshape(D_MODEL, D_MODEL)
    w_o_f = w_o.reshape(D_MODEL, D_MODEL)
    mix3, lse3, lse_r1, lse_r2 = _attn_mix(o4_0.reshape(B, SEQ, GW), l4_0.reshape(B, SEQ, LANES), [o4_1, o4_2], [l4_1, l4_2])
    mix = mix3.reshape(T, GW)
    y_attn = _mm("attn_out", mix, w_attn_out, grid=(N_CHIPS, T // 512),
                 a_spec=pl.BlockSpec((512, GW), lambda s, m: (m, 0)),
                 b_spec=pl.BlockSpec((None, GW, D_MODEL // N_CHIPS), lambda s, m: (s, 0, 0)),
                 o_spec=pl.BlockSpec((512, D_MODEL // N_CHIPS), lambda s, m: (m, s)), o_shape=(T, D_MODEL), o_dtype=bf16,
                 dims=NN, sem=("parallel", "parallel"))

    c1 = _glu_conv_fwd(proj3, conv_w, conv_b).reshape(T, D_MODEL)
    c3 = _ln_silu_fwd(c1, conv_ln_g, conv_ln_b)
    y_conv = _mm_nn_full("conv_out", c3, w_conv_out_f, bf16)

    merged = _merge_fwd(proj, gate_b, y_attn, y_conv)
    x1 = _mm_nn_full("o_proj", merged, w_o_f, f32, res=xf)

    h2 = _rms_fwd(x1, norm2_g, "rms2_fwd")
    fa, fb, ff = _ffn_up(h2, w_gate, w_up)
    x2 = _mm("ffn_down", ff, w_down, grid=(T // 1024, N_CHIPS),
             a_spec=pl.BlockSpec((None, 1024, FF_S), lambda m, s: (s, m, 0)),
             b_spec=pl.BlockSpec((None, FF_S, D_MODEL), lambda m, s: (s, 0, 0)),
             o_spec=pl.BlockSpec((1024, D_MODEL), lambda m, s: (m, 0)), o_shape=(T, D_MODEL), o_dtype=f32, dims=NN,
             acc_shape=(1024, D_MODEL), k_axis=1, res=x1, res_spec=pl.BlockSpec((1024, D_MODEL), lambda m, s: (m, 0)),
             sem=("parallel", "arbitrary"))

    loss, dx2, d_gf = _final_loss(x2, norm_f_g.reshape(1, D_MODEL), tf)

    d_w_down = _mm("d_w_down", ff, dx2, grid=(N_CHIPS, T // 1024),
                   a_spec=pl.BlockSpec((None, 1024, FF_S), lambda s, k: (s, k, 0)),
                   b_spec=pl.BlockSpec((1024, D_MODEL), lambda s, k: (k, 0)),
                   o_spec=pl.BlockSpec((None, FF_S, D_MODEL), lambda s, k: (s, 0, 0)), o_shape=(N_CHIPS, FF_S, D_MODEL),
                   o_dtype=bf16, dims=TN, acc_shape=(FF_S, D_MODEL), k_axis=1, sem=("parallel", "arbitrary"))
    da, db = _ffn_down_bwd(dx2, w_down, fa, fb)

    def d_w_ff(name, dz):
        return _mm(name, dz, h2, grid=(N_CHIPS, T // 2048),
                   a_spec=pl.BlockSpec((None, 2048, FF_S), lambda s, k: (s, k, 0)),
                   b_spec=pl.BlockSpec((2048, D_MODEL), lambda s, k: (k, 0)),
                   o_spec=pl.BlockSpec((None, FF_S, D_MODEL), lambda s, k: (s, 0, 0)), o_shape=(N_CHIPS, FF_S, D_MODEL),
                   o_dtype=bf16, dims=TN, acc_shape=(FF_S, D_MODEL), k_axis=1, sem=("parallel", "arbitrary"))

    d_w_gate = d_w_ff("d_w_gate", da)
    d_w_up = d_w_ff("d_w_up", db)
    part = dict(w_ffn_gate=d_w_gate, w_ffn_up=d_w_up, w_ffn_down=d_w_down)

    def pair_sums(names, tag):
        got = _pair_exchange([part[n] for n in names], f"grad_pair_exchange_{tag}")
        return [_add_pairs(part[n], g_, core, f"pair_sum_{n}") for n, g_ in zip(names, got)]

    sums_a = pair_sums(REDUCE_A, "a")
    dh2 = _ffn_dh2(da, w_gate, db, w_up)
    dx1, d_g2 = _rms_bwd(dh2, x1, norm2_g, dx2, "rms2_bwd")

    d_w_o = _mm_tn_tokens("d_w_o", merged, dx1, bf16).reshape(N_CHIPS, D_MODEL // N_CHIPS, D_MODEL)
    dmerged = _mm_nt_full("d_merged", dx1, w_o_f, bf16)
    dya, dyc, dga, dgc, d_gba, d_gbc = _merge_bwd(dmerged, proj, gate_b, y_attn, y_conv)

    d_w_conv_out = _mm_tn_tokens("d_w_conv_out", c3, dyc, bf16).reshape(N_CHIPS, D_MODEL // N_CHIPS, D_MODEL)
    dc3 = _mm_nt_full("d_c3", dyc, w_conv_out_f, bf16)
    dc1, d_ln_g, d_ln_b = _ln_silu_bwd(dc3, c1, conv_ln_g, conv_ln_b)
    (dua, dub, d_conv_w, d_conv_b), got_a = _glu_conv_bwd(dc1.reshape(B, SEQ, D_MODEL), proj3, conv_w, comm=_x_chip_exchange(sums_a))

    d_w_attn_out = _mm("d_w_attn_out", mix, dya, grid=(N_CHIPS, T // 512),
                       a_spec=pl.BlockSpec((512, GW), lambda s, k: (k, 0)),
                       b_spec=pl.BlockSpec((512, D_MODEL // N_CHIPS), lambda s, k: (k, s)),
                       o_spec=pl.BlockSpec((None, GW, D_MODEL // N_CHIPS), lambda s, k: (s, 0, 0)),
                       o_shape=(N_CHIPS, GW, D_MODEL // N_CHIPS), o_dtype=bf16, dims=TN, acc_shape=(GW, D_MODEL // N_CHIPS),
                       k_axis=1, sem=("parallel", "arbitrary"))
    dmix = _mm("d_mix", dya, w_attn_out, grid=(T // 512, N_CHIPS),
               a_spec=pl.BlockSpec((512, D_MODEL // N_CHIPS), lambda m, s: (m, s)),
               b_spec=pl.BlockSpec((None, GW, D_MODEL // N_CHIPS), lambda m, s: (s, 0, 0)),
               o_spec=pl.BlockSpec((512, GW), lambda m, s: (m, 0)), o_shape=(T, GW), o_dtype=bf16, dims=NT,
               acc_shape=(512, GW), k_axis=1, sem=("parallel", "arbitrary"))
    dmix3 = dmix.reshape(B, SEQ, GW)
    delta3, delta_r1, delta_r2, dmix_r1, dmix_r2 = _attn_delta(dmix3, mix3)
    part.update(w_conv_out=d_w_conv_out, w_attn_out=d_w_attn_out, w_o=d_w_o)
    sums_b = pair_sums(REDUCE_B, "b")
    one = (B, 1, SEQ)
    (dq0, dk0, dv0), got_b = _attn_bwd(0, *qkv_args(0), dmix3.reshape(one + (GW,)), lse3.reshape(one + (LANES,)),
                                       delta3.reshape(one + (LANES,)), comm=_x_chip_exchange(sums_b))
    dqkv = [tuple(t.reshape(B, SEQ, GW) for t in (dq0, dk0, dv0)),
            _attn_bwd(1, *qkv_args(1), dmix_r1, lse_r1, delta_r1), _attn_bwd(2, *qkv_args(2), dmix_r2, lse_r2, delta_r2)]
    dproj = _assemble_dproj(dqkv, dua, dub, dga.reshape(B, SEQ, D_MODEL), dgc.reshape(B, SEQ, D_MODEL)).reshape(T, IN_W)

    d_w_in = _mm("d_w_in", h, dproj, grid=(N_CHIPS, T // 1024),
                 a_spec=pl.BlockSpec((1024, D_MODEL), lambda s, k: (k, 0)),
                 b_spec=pl.BlockSpec((1024, IN_S), lambda s, k: (k, s)),
                 o_spec=pl.BlockSpec((None, D_MODEL, IN_S), lambda s, k: (s, 0, 0)), o_shape=(N_CHIPS, D_MODEL, IN_S),
                 o_dtype=bf16, dims=TN, acc_shape=(D_MODEL, IN_S), k_axis=1, sem=("parallel", "arbitrary"))
    part.update(w_in=d_w_in)
    sums_c = pair_sums(REDUCE_C, "c")
    dh, got_c = _mm("d_h", dproj, w_in, grid=(T // 512, N_CHIPS),
                    a_spec=pl.BlockSpec((512, IN_S), lambda m, s: (m, s)),
                    b_spec=pl.BlockSpec((None, D_MODEL, IN_S), lambda m, s: (s, 0, 0)),
                    o_spec=pl.BlockSpec((512, D_MODEL), lambda m, s: (m, 0)), o_shape=(T, D_MODEL), o_dtype=f32, dims=NT,
                    acc_shape=(512, D_MODEL), k_axis=1, comm=_x_chip_exchange(sums_c))
    dx, d_g1 = _rms_bwd(dh, xf, norm1_g, dx1, "rms1_bwd")

    names = REDUCE_A + REDUCE_B + REDUCE_C
    halves = [_sum4(s, g_, chip1, core, f"chip_sum_{n}") for n, s, g_ in zip(names, sums_a + sums_b + sums_c, got_a + got_b + got_c)]
    big = dict(zip(names, _pair_gather(halves)))
    small = dict(norm1_g=d_g1, gate_b=jnp.concatenate([d_gba, d_gbc], axis=-1), conv_b=d_conv_b, conv_ln_g=d_ln_g,
                 conv_ln_b=d_ln_b, norm2_g=d_g2, norm_f_g=d_gf, conv_w=d_conv_w)
    return loss, dx.reshape(B, SEQ, D_MODEL), big, small


BIG = ("w_in", "w_conv_out", "w_attn_out", "w_o", "w_ffn_gate", "w_ffn_up", "w_ffn_down")
TRANSPOSED = ("w_ffn_gate", "w_ffn_up")
SMALL = ("norm1_g", "gate_b", "conv_b", "conv_ln_g", "conv_ln_b", "norm2_g", "norm_f_g")
SMALL_ROWS = {"norm1_g": 8, "gate_b": 16, "conv_b": 8, "conv_ln_g": 8, "conv_ln_b": 8, "norm2_g": 8, "norm_f_g": 8}
LOSS_ROWS = 8
CONVW_ROWS = 32 * D_MODEL // LANES


def kernel(x, norm1_g, w_in, gate_b, conv_w, conv_b, conv_ln_g, conv_ln_b, w_conv_out, w_attn_out, w_o, norm2_g, w_ffn_gate, w_ffn_up, w_ffn_down, norm_f_g, loss_target, m_norm1_g, m_w_in, m_gate_b, m_conv_w, m_conv_b, m_conv_ln_g, m_conv_ln_b, m_w_conv_out, m_w_attn_out, m_w_o, m_norm2_g, m_w_ffn_gate, m_w_ffn_up, m_w_ffn_down, m_norm_f_g, v_norm1_g, v_w_in, v_gate_b, v_conv_w, v_conv_b, v_conv_ln_g, v_conv_ln_b, v_w_conv_out, v_w_attn_out, v_w_o, v_norm2_g, v_w_ffn_gate, v_w_ffn_up, v_w_ffn_down, v_norm_f_g):
    W = dict(norm1_g=norm1_g, w_in=w_in, gate_b=gate_b, conv_w=conv_w, conv_b=conv_b, conv_ln_g=conv_ln_g, conv_ln_b=conv_ln_b,
             w_conv_out=w_conv_out, w_attn_out=w_attn_out, w_o=w_o, norm2_g=norm2_g, w_ffn_gate=w_ffn_gate, w_ffn_up=w_ffn_up,
             w_ffn_down=w_ffn_down, norm_f_g=norm_f_g)
    M = dict(norm1_g=m_norm1_g, w_in=m_w_in, gate_b=m_gate_b, conv_w=m_conv_w, conv_b=m_conv_b, conv_ln_g=m_conv_ln_g,
             conv_ln_b=m_conv_ln_b, w_conv_out=m_w_conv_out, w_attn_out=m_w_attn_out, w_o=m_w_o, norm2_g=m_norm2_g,
             w_ffn_gate=m_w_ffn_gate, w_ffn_up=m_w_ffn_up, w_ffn_down=m_w_ffn_down, norm_f_g=m_norm_f_g)
    V = dict(norm1_g=v_norm1_g, w_in=v_w_in, gate_b=v_gate_b, conv_w=v_conv_w, conv_b=v_conv_b, conv_ln_g=v_conv_ln_g,
             conv_ln_b=v_conv_ln_b, w_conv_out=v_w_conv_out, w_attn_out=v_w_attn_out, w_o=v_w_o, norm2_g=v_norm2_g,
             w_ffn_gate=v_w_ffn_gate, w_ffn_up=v_w_ffn_up, w_ffn_down=v_w_ffn_down, norm_f_g=v_norm_f_g)
    order = list(W)

    def as2d(n, a):
        a = a.reshape(a.shape[-2:])
        return a.T if n in TRANSPOSED else a

    def from2d(n, a):
        return (a.T if n in TRANSPOSED else a).reshape(W[n].shape)

    shard2d = {n: as2d(n, W[n]) for n in BIG}
    chip = 2 * lax.axis_index("x") + lax.axis_index("y")

    cw = jnp.zeros((CONV_K, D_MODEL), f32)
    cw = lax.dynamic_update_slice(cw, 0.5 * conv_w.reshape(CONV_K, D_MODEL // N_CHIPS), (0, chip * (D_MODEL // N_CHIPS)))
    conv_w_full = _small_allreduce(_rows128(cw, CONVW_ROWS), "conv_w_gather")[: CONV_K * D_MODEL // LANES].reshape(CONV_K, D_MODEL)

    core = lax.axis_index("c").astype(jnp.int32).reshape(1)
    chip1 = chip.astype(jnp.int32).reshape(1)
    shards = {n: shard2d[n].astype(bf16) for n in BIG}
    slots = {n: lax.dynamic_update_slice(lax.empty((N_CHIPS,) + s.shape, bf16), s[None], (chip, 0, 0)) for n, s in shards.items()}

    loss, grad_x, grads, gsmall = _step(x, loss_target, norm1_g, gate_b, conv_w_full, conv_b, conv_ln_g, conv_ln_b, norm2_g,
                                        norm_f_g, shards, slots, chip1, core)

    pieces = [_rows128(loss, LOSS_ROWS)] + [_rows128(gsmall[n], SMALL_ROWS[n]) for n in SMALL] + [_rows128(gsmall["conv_w"], CONVW_ROWS)]
    tot = _small_allreduce(jnp.concatenate(pieces, axis=0), "small_allreduce")
    loss_out = tot[0, 0]
    row = LOSS_ROWS
    for n in SMALL:
        grads[n] = tot[row: row + W[n].size // LANES].reshape(W[n].shape)
        row += SMALL_ROWS[n]
    dcw = tot[row: row + CONV_K * D_MODEL // LANES].reshape(CONV_K, D_MODEL)
    grads["conv_w"] = lax.dynamic_slice(dcw, (0, chip * (D_MODEL // N_CHIPS)), (CONV_K, D_MODEL // N_CHIPS))

    delta, new_m, new_v = {}, {}, {}
    for n in BIG:
        d, nm, nv = _adamw(shard2d[n], grads[n], as2d(n, M[n]), as2d(n, V[n]), f"adamw_{n}")
        delta[n], new_m[n], new_v[n], grads[n] = (from2d(n, t) for t in (d, nm, nv, grads[n]))

    def pack(src):
        return jnp.concatenate([_rows128(src[n], SMALL_ROWS[n]) for n in SMALL], axis=0)

    d, nm, nv = _adamw(pack(W), pack(grads), pack(M), pack(V), "adamw_small")
    row = 0
    for n in SMALL:
        k = W[n].size // LANES
        delta[n], new_m[n], new_v[n] = (t[row: row + k].reshape(W[n].shape) for t in (d, nm, nv))
        row += SMALL_ROWS[n]

    def pad32(a):
        return jnp.pad(a.reshape(CONV_K, D_MODEL // N_CHIPS), ((0, 1), (0, 0)))

    d, nm, nv = _adamw(pad32(conv_w), pad32(grads["conv_w"]), pad32(m_conv_w), pad32(v_conv_w), "adamw_conv_w")
    delta["conv_w"], new_m["conv_w"], new_v["conv_w"] = (t[:CONV_K].reshape(conv_w.shape) for t in (d, nm, nv))
    grads["conv_w"] = grads["conv_w"].reshape(conv_w.shape)

    return (loss_out, grad_x, *[grads[n] for n in order], *[delta[n] for n in order],
            *[new_m[n] for n in order], *[new_v[n] for n in order])
```

```python
import functools
import math
from typing import Callable, NamedTuple

import numpy as np
import jax
import jax.numpy as jnp
from jax import lax
from jax.experimental import pallas as pl
from jax.experimental.pallas import tpu as pltpu

f32 = jnp.float32
bf16 = jnp.bfloat16
SDS = jax.ShapeDtypeStruct
MESH = pl.DeviceIdType.MESH

D_MODEL = 1024
SEQ = 2048
HEAD_DIM = 64
HEADS = 8
GROUPS = ((128, 1), (512, 4), (2048, 16))
GW = HEADS * HEAD_DIM
ATTN_W = len(GROUPS) * GW
Q_BLOCK = 128
CONV_K = 31
D_FF = 2816
IN_W = 3 * ATTN_W + 2 * D_MODEL + 2 * D_MODEL
N_CHIPS = 4
IN_S = IN_W // N_CHIPS
FF_S = D_FF // N_CHIPS
RMS_EPS = 1e-6
LN_EPS = 1e-5
LR, B1, B2, ADAM_EPS, WD, STEP = 0.001, 0.9, 0.999, 1e-08, 0.01, 10
NEG = -1e30
LANES = 128
VMEM_LIMIT = 48 * 2 ** 20
CB = 512
UA_CB, UB_CB, GA_CB, GC_CB = 9, 11, 13, 15


def _alibi_slope_list(n):
    def pow2(m):
        start = 2.0 ** (-8.0 / m)
        return [start ** (i + 1) for i in range(m)]
    if math.log2(n).is_integer():
        return pow2(n)
    c = 2 ** math.floor(math.log2(n))
    return pow2(c) + _alibi_slope_list(2 * c)[0::2][: n - c]


_SLOPES = np.asarray(sorted(_alibi_slope_list(len(GROUPS) * HEADS), reverse=True), dtype=np.float32).reshape(len(GROUPS), HEADS)


def _cp(sem=None, vmem=VMEM_LIMIT):
    return pltpu.CompilerParams(dimension_semantics=sem, vmem_limit_bytes=vmem)


def _sigmoid(x):
    return 1.0 / (1.0 + jnp.exp(-x))


HBM = pl.BlockSpec(memory_space=pl.ANY)


class _Exchange(NamedTuple):
    ins: list
    out_shape: list
    scratch: list
    aliases: dict
    start: Callable
    wait: Callable


def _launch(body, *, name, grid, in_specs, out_specs, out_shape, args, scratch_shapes=(), sem=None, comm=None):
    if comm is None:
        return pl.pallas_call(body, grid=grid, in_specs=in_specs, out_specs=out_specs, out_shape=out_shape,
                              scratch_shapes=list(scratch_shapes), name=name, compiler_params=_cp(sem))(*args)
    multi = isinstance(out_shape, (tuple, list))
    m_out = list(out_shape) if multi else [out_shape]
    m_ospec = list(out_specs) if multi else [out_specs]
    n_in, n_out, n_scr = len(in_specs), len(m_out), len(scratch_shapes)
    nc_in, nc_out = len(comm.ins), len(comm.out_shape)

    def hosted(*refs):
        bounds = np.cumsum([0, n_in, nc_in, n_out, nc_out, n_scr])
        mi, ci, mo, co, ms = (refs[a:b] for a, b in zip(bounds[:-1], bounds[1:]))
        cs = refs[bounds[-1]:]
        ids = [pl.program_id(a) for a in range(len(grid))]
        first = functools.reduce(jnp.logical_and, [i == 0 for i in ids])
        last = functools.reduce(jnp.logical_and, [i == g - 1 for i, g in zip(ids, grid)])

        @pl.when(first)
        def _():
            comm.start(ci, co, cs)

        body(*mi, *mo, *ms)

        @pl.when(last)
        def _():
            comm.wait(ci, co, cs)

    res = pl.pallas_call(
        hosted, grid=grid, in_specs=list(in_specs) + [HBM] * nc_in, out_specs=m_ospec + [HBM] * nc_out,
        out_shape=tuple(m_out) + tuple(comm.out_shape), scratch_shapes=list(scratch_shapes) + list(comm.scratch),
        input_output_aliases={n_in + i: n_out + o for i, o in comm.aliases.items()}, name=name + "_comm",
        compiler_params=pltpu.CompilerParams(dimension_semantics=("arbitrary",) * len(grid), vmem_limit_bytes=VMEM_LIMIT,
                                             has_side_effects=True))(*args, *comm.ins)
    return (tuple(res[:n_out]) if multi else res[0]), tuple(res[n_out:])


def _run_exchange(ex, name):
    n_in = len(ex.ins)

    def body(*refs):
        ins, outs, sems = refs[:n_in], refs[n_in:n_in + len(ex.out_shape)], refs[n_in + len(ex.out_shape):]
        ex.start(ins, outs, sems)
        ex.wait(ins, outs, sems)

    return pl.pallas_call(body, in_specs=[HBM] * n_in, out_specs=[HBM] * len(ex.out_shape), out_shape=tuple(ex.out_shape),
                          scratch_shapes=list(ex.scratch), input_output_aliases=dict(ex.aliases), name=name,
                          compiler_params=pltpu.CompilerParams(has_side_effects=True))(*ex.ins)


def _rms_fwd(x, g, name):
    T = x.shape[0]
    tm = 512

    def body(x_ref, g_ref, o_ref):
        xv = x_ref[...]
        r = lax.rsqrt(jnp.mean(xv * xv, axis=-1, keepdims=True) + RMS_EPS)
        o_ref[...] = (xv * r * g_ref[...]).astype(o_ref.dtype)

    row = pl.BlockSpec((tm, D_MODEL), lambda i: (i, 0))
    vec = pl.BlockSpec((1, D_MODEL), lambda i: (0, 0))
    return pl.pallas_call(body, grid=(T // tm,), in_specs=[row, vec], out_specs=row,
                          out_shape=SDS((T, D_MODEL), bf16), name=name, compiler_params=_cp(("parallel",)))(x, g)


def _rms_bwd(dy, x, g, dres, name):
    T = x.shape[0]
    tm = 512

    def body(dy_ref, x_ref, g_ref, r_ref, dx_ref, dg_ref):
        xv = x_ref[...]
        r = lax.rsqrt(jnp.mean(xv * xv, axis=-1, keepdims=True) + RMS_EPS)
        xh = xv * r
        dyv = dy_ref[...]
        dxh = dyv * g_ref[...]
        dx_ref[...] = r_ref[...] + r * (dxh - xh * jnp.mean(dxh * xh, axis=-1, keepdims=True))
        part = jnp.sum(dyv * xh, axis=0, keepdims=True)

        @pl.when(pl.program_id(0) == 0)
        def _():
            dg_ref[...] = part

        @pl.when(pl.program_id(0) > 0)
        def _():
            dg_ref[...] += part

    row = pl.BlockSpec((tm, D_MODEL), lambda i: (i, 0))
    vec = pl.BlockSpec((1, D_MODEL), lambda i: (0, 0))
    return pl.pallas_call(body, grid=(T // tm,), in_specs=[row, row, vec, row], out_specs=[row, vec],
                          out_shape=(SDS((T, D_MODEL), f32), SDS((1, D_MODEL), f32)), name=name,
                          compiler_params=_cp(("arbitrary",)))(dy, x, g, dres)


def _final_loss(x2, gf, target):
    T = x2.shape[0]
    tm = 512

    def body(x_ref, g_ref, t_ref, loss_ref, dx_ref, dg_ref):
        xv = x_ref[...]
        gv = g_ref[...]
        r = lax.rsqrt(jnp.mean(xv * xv, axis=-1, keepdims=True) + RMS_EPS)
        xh = xv * r
        e = xh * gv - t_ref[...]
        part_l = jnp.broadcast_to(0.5 * jnp.sum(jnp.mean(e * e, axis=-1, keepdims=True), axis=0, keepdims=True), (1, LANES))
        dy = e * (1.0 / D_MODEL)
        dxh = dy * gv
        dx_ref[...] = r * (dxh - xh * jnp.mean(dxh * xh, axis=-1, keepdims=True))
        part_g = jnp.sum(dy * xh, axis=0, keepdims=True)

        @pl.when(pl.program_id(0) == 0)
        def _():
            loss_ref[...] = part_l
            dg_ref[...] = part_g

        @pl.when(pl.program_id(0) > 0)
        def _():
            loss_ref[...] += part_l
            dg_ref[...] += part_g

    row = pl.BlockSpec((tm, D_MODEL), lambda i: (i, 0))
    vec = pl.BlockSpec((1, D_MODEL), lambda i: (0, 0))
    lvec = pl.BlockSpec((1, LANES), lambda i: (0, 0))
    return pl.pallas_call(body, grid=(T // tm,), in_specs=[row, vec, row], out_specs=[lvec, row, vec],
                          out_shape=(SDS((1, LANES), f32), SDS((T, D_MODEL), f32), SDS((1, D_MODEL), f32)),
                          name="final_loss", compiler_params=_cp(("arbitrary",)))(x2, gf, target)


def _ln_silu_fwd(c1, g, b):
    T = c1.shape[0]
    tm = 512

    def body(c_ref, g_ref, b_ref, o_ref):
        cv = c_ref[...]
        mu = jnp.mean(cv, axis=-1, keepdims=True)
        cc = cv - mu
        var = jnp.mean(cc * cc, axis=-1, keepdims=True)
        c2 = cc * lax.rsqrt(var + LN_EPS) * g_ref[...] + b_ref[...]
        o_ref[...] = (c2 * _sigmoid(c2)).astype(o_ref.dtype)

    row = pl.BlockSpec((tm, D_MODEL), lambda i: (i, 0))
    vec = pl.BlockSpec((1, D_MODEL), lambda i: (0, 0))
    return pl.pallas_call(body, grid=(T // tm,), in_specs=[row, vec, vec], out_specs=row,
                          out_shape=SDS((T, D_MODEL), bf16), name="ln_silu_fwd", compiler_params=_cp(("parallel",)))(c1, g, b)


def _ln_silu_bwd(dc3, c1, g, b):
    T = c1.shape[0]
    tm = 512

    def body(d_ref, c_ref, g_ref, b_ref, dc_ref, dg_ref, db_ref):
        cv = c_ref[...]
        gv = g_ref[...]
        mu = jnp.mean(cv, axis=-1, keepdims=True)
        cc = cv - mu
        var = jnp.mean(cc * cc, axis=-1, keepdims=True)
        rs = lax.rsqrt(var + LN_EPS)
        xh = cc * rs
        c2 = xh * gv + b_ref[...]
        sg = _sigmoid(c2)
        dc2 = d_ref[...].astype(f32) * (sg * (1.0 + c2 * (1.0 - sg)))
        dxh = dc2 * gv
        dc_ref[...] = rs * (dxh - jnp.mean(dxh, axis=-1, keepdims=True) - xh * jnp.mean(dxh * xh, axis=-1, keepdims=True))
        pg = jnp.sum(dc2 * xh, axis=0, keepdims=True)
        pb = jnp.sum(dc2, axis=0, keepdims=True)

        @pl.when(pl.program_id(0) == 0)
        def _():
            dg_ref[...] = pg
            db_ref[...] = pb

        @pl.when(pl.program_id(0) > 0)
        def _():
            dg_ref[...] += pg
            db_ref[...] += pb

    row = pl.BlockSpec((tm, D_MODEL), lambda i: (i, 0))
    vec = pl.BlockSpec((1, D_MODEL), lambda i: (0, 0))
    return pl.pallas_call(body, grid=(T // tm,), in_specs=[row, row, vec, vec], out_specs=[row, vec, vec],
                          out_shape=(SDS((T, D_MODEL), f32), SDS((1, D_MODEL), f32), SDS((1, D_MODEL), f32)),
                          name="ln_silu_bwd", compiler_params=_cp(("arbitrary",)))(dc3, c1, g, b)


NN = (((1,), (0,)), ((), ()))
NT = (((1,), (1,)), ((), ()))
TN = (((0,), (0,)), ((), ()))


def _mm(name, a, b, *, grid, a_spec, b_spec, o_spec, o_shape, o_dtype, dims, acc_shape=None, k_axis=None,
        res=None, res_spec=None, sem=None, comm=None):
    nk = 1 if k_axis is None else grid[k_axis]

    def body(*refs):
        if res is None:
            a_ref, b_ref, o_ref = refs[:3]
            r_ref, scr = None, refs[3:]
        else:
            a_ref, b_ref, r_ref, o_ref = refs[:4]
            scr = refs[4:]
        p = lax.dot_general(a_ref[...].astype(bf16), b_ref[...].astype(bf16), dims, preferred_element_type=f32)
        if nk == 1:
            if r_ref is not None:
                p = p + r_ref[...]
            o_ref[...] = p.astype(o_dtype)
            return
        acc = scr[0]
        k = pl.program_id(k_axis)

        @pl.when(k == 0)
        def _():
            acc[...] = p

        @pl.when(k > 0)
        def _():
            acc[...] += p

        @pl.when(k == nk - 1)
        def _():
            t = acc[...]
            if r_ref is not None:
                t = t + r_ref[...]
            o_ref[...] = t.astype(o_dtype)

    ins = [a, b] + ([] if res is None else [res])
    specs = [a_spec, b_spec] + ([] if res is None else [res_spec])
    scratch = [] if nk == 1 else [pltpu.VMEM(acc_shape, f32)]
    return _launch(body, name=name, grid=grid, in_specs=specs, out_specs=o_spec, out_shape=SDS(o_shape, o_dtype),
                   args=ins, scratch_shapes=scratch, sem=sem, comm=comm)


def _mm_nn_full(name, a, b, o_dtype, res=None, tm=512):
    T, K = a.shape
    N = b.shape[1]
    return _mm(name, a, b, grid=(T // tm,), a_spec=pl.BlockSpec((tm, K), lambda i: (i, 0)),
               b_spec=pl.BlockSpec((K, N), lambda i: (0, 0)), o_spec=pl.BlockSpec((tm, N), lambda i: (i, 0)),
               o_shape=(T, N), o_dtype=o_dtype, dims=NN, res=res,
               res_spec=None if res is None else pl.BlockSpec((tm, N), lambda i: (i, 0)), sem=("parallel",))


def _mm_nt_full(name, a, b, o_dtype, tm=512):
    T, N = a.shape
    K = b.shape[0]
    return _mm(name, a, b, grid=(T // tm,), a_spec=pl.BlockSpec((tm, N), lambda i: (i, 0)),
               b_spec=pl.BlockSpec((K, N), lambda i: (0, 0)), o_spec=pl.BlockSpec((tm, K), lambda i: (i, 0)),
               o_shape=(T, K), o_dtype=o_dtype, dims=NT, sem=("parallel",))


def _mm_tn_tokens(name, a, b, o_dtype, tk=1024):
    T, K = a.shape
    N = b.shape[1]
    return _mm(name, a, b, grid=(T // tk,), a_spec=pl.BlockSpec((tk, K), lambda k: (k, 0)),
               b_spec=pl.BlockSpec((tk, N), lambda k: (k, 0)), o_spec=pl.BlockSpec((K, N), lambda k: (0, 0)),
               o_shape=(K, N), o_dtype=o_dtype, dims=TN, acc_shape=(K, N), k_axis=0, sem=("arbitrary",))


def _ffn_up(h2, wg, wu):
    T = h2.shape[0]
    tm = 512

    def body(h_ref, wg_ref, wu_ref, a_ref, b_ref, f_ref):
        hv = h_ref[...]
        av = lax.dot_general(hv, wg_ref[...], NT, preferred_element_type=f32)
        bv = lax.dot_general(hv, wu_ref[...], NT, preferred_element_type=f32)
        a_ref[...] = av.astype(bf16)
        b_ref[...] = bv.astype(bf16)
        f_ref[...] = (av * _sigmoid(av) * bv).astype(bf16)

    wspec = pl.BlockSpec((None, FF_S, D_MODEL), lambda s, m: (s, 0, 0))
    ospec = pl.BlockSpec((None, tm, FF_S), lambda s, m: (s, m, 0))
    osd = SDS((N_CHIPS, T, FF_S), bf16)
    return pl.pallas_call(body, grid=(N_CHIPS, T // tm),
                          in_specs=[pl.BlockSpec((tm, D_MODEL), lambda s, m: (m, 0)), wspec, wspec],
                          out_specs=[ospec, ospec, ospec], out_shape=(osd, osd, osd), name="ffn_up",
                          compiler_params=_cp(("parallel", "parallel")))(h2, wg, wu)


def _ffn_down_bwd(dx2, wd, a, b):
    T = dx2.shape[0]
    tm = 512

    def body(d_ref, w_ref, a_ref, b_ref, da_ref, db_ref):
        df = lax.dot_general(d_ref[...].astype(bf16), w_ref[...], NT, preferred_element_type=f32)
        av = a_ref[...].astype(f32)
        sg = _sigmoid(av)
        da_ref[...] = (df * b_ref[...].astype(f32) * (sg * (1.0 + av * (1.0 - sg)))).astype(bf16)
        db_ref[...] = (df * av * sg).astype(bf16)

    aspec = pl.BlockSpec((None, tm, FF_S), lambda s, m: (s, m, 0))
    osd = SDS((N_CHIPS, T, FF_S), bf16)
    return pl.pallas_call(body, grid=(N_CHIPS, T // tm),
                          in_specs=[pl.BlockSpec((tm, D_MODEL), lambda s, m: (m, 0)),
                                    pl.BlockSpec((None, FF_S, D_MODEL), lambda s, m: (s, 0, 0)), aspec, aspec],
                          out_specs=[aspec, aspec], out_shape=(osd, osd), name="ffn_down_bwd",
                          compiler_params=_cp(("parallel", "parallel")))(dx2, wd, a, b)


def _ffn_dh2(da, wg, db, wu):
    T = da.shape[1]
    tm = 1024

    def body(da_ref, wg_ref, db_ref, wu_ref, o_ref, acc):
        p = jnp.dot(da_ref[...], wg_ref[...], preferred_element_type=f32)
        p = p + jnp.dot(db_ref[...], wu_ref[...], preferred_element_type=f32)
        s = pl.program_id(1)

        @pl.when(s == 0)
        def _():
            acc[...] = p

        @pl.when(s > 0)
        def _():
            acc[...] += p

        @pl.when(s == N_CHIPS - 1)
        def _():
            o_ref[...] = acc[...]

    aspec = pl.BlockSpec((None, tm, FF_S), lambda m, s: (s, m, 0))
    wspec = pl.BlockSpec((None, FF_S, D_MODEL), lambda m, s: (s, 0, 0))
    return pl.pallas_call(body, grid=(T // tm, N_CHIPS), in_specs=[aspec, wspec, aspec, wspec],
                          out_specs=pl.BlockSpec((tm, D_MODEL), lambda m, s: (m, 0)), out_shape=SDS((T, D_MODEL), f32),
                          scratch_shapes=[pltpu.VMEM((tm, D_MODEL), f32)], name="ffn_dh2",
                          compiler_params=_cp(("parallel", "arbitrary")))(da, wg, db, wu)


def _merge_fwd(proj, gate_b, ya, yc):
    T = proj.shape[0]
    tm = 512

    def body(ga_ref, gc_ref, ba_ref, bc_ref, ya_ref, yc_ref, o_ref):
        sa = _sigmoid(ga_ref[...].astype(f32) + ba_ref[...])
        sc = _sigmoid(gc_ref[...].astype(f32) + bc_ref[...])
        o_ref[...] = (sa * ya_ref[...].astype(f32) + sc * yc_ref[...].astype(f32)).astype(bf16)

    blk = pl.BlockSpec((tm, CB), lambda i, j: (i, j))
    return pl.pallas_call(
        body, grid=(T // tm, 2),
        in_specs=[pl.BlockSpec((tm, CB), lambda i, j: (i, GA_CB + j)), pl.BlockSpec((tm, CB), lambda i, j: (i, GC_CB + j)),
                  pl.BlockSpec((1, CB), lambda i, j: (0, j)), pl.BlockSpec((1, CB), lambda i, j: (0, 2 + j)), blk, blk],
        out_specs=blk, out_shape=SDS((T, D_MODEL), bf16), name="merge_fwd",
        compiler_params=_cp(("parallel", "parallel")))(proj, proj, gate_b, gate_b, ya, yc)


def _merge_bwd(dm, proj, gate_b, ya, yc):
    T = proj.shape[0]
    tm = 512

    def body(dm_ref, ga_ref, gc_ref, ba_ref, bc_ref, ya_ref, yc_ref, dya_ref, dyc_ref, dga_ref, dgc_ref, dba_ref, dbc_ref):
        dmv = dm_ref[...].astype(f32)
        sa = _sigmoid(ga_ref[...].astype(f32) + ba_ref[...])
        sc = _sigmoid(gc_ref[...].astype(f32) + bc_ref[...])
        dya_ref[...] = (dmv * sa).astype(bf16)
        dyc_ref[...] = (dmv * sc).astype(bf16)
        dga = dmv * ya_ref[...].astype(f32) * sa * (1.0 - sa)
        dgc = dmv * yc_ref[...].astype(f32) * sc * (1.0 - sc)
        dga_ref[...] = dga.astype(bf16)
        dgc_ref[...] = dgc.astype(bf16)
        pa = jnp.sum(dga, axis=0, keepdims=True)
        pc = jnp.sum(dgc, axis=0, keepdims=True)

        @pl.when(pl.program_id(1) == 0)
        def _():
            dba_ref[...] = pa
            dbc_ref[...] = pc

        @pl.when(pl.program_id(1) > 0)
        def _():
            dba_ref[...] += pa
            dbc_ref[...] += pc

    blk = pl.BlockSpec((tm, CB), lambda j, i: (i, j))
    vec = pl.BlockSpec((1, CB), lambda j, i: (0, j))
    big = SDS((T, D_MODEL), bf16)
    small = SDS((1, D_MODEL), f32)
    return pl.pallas_call(
        body, grid=(2, T // tm),
        in_specs=[blk, pl.BlockSpec((tm, CB), lambda j, i: (i, GA_CB + j)), pl.BlockSpec((tm, CB), lambda j, i: (i, GC_CB + j)),
                  vec, pl.BlockSpec((1, CB), lambda j, i: (0, 2 + j)), blk, blk],
        out_specs=[blk, blk, blk, blk, vec, vec], out_shape=(big, big, big, big, small, small), name="merge_bwd",
        compiler_params=_cp(("parallel", "arbitrary")))(dm, proj, proj, gate_b, gate_b, ya, yc)


CONV_TS = 256
CONV_HALO = 32
CONV_RC = 64
CONV_WIN = CONV_TS + CONV_HALO
SUBLANES = 8


def _fill_shifted(win, sh):
    for b in range(1, SUBLANES):
        sh[b - 1] = win[pl.ds(b, CONV_WIN - SUBLANES), :]


def _rows_at(win, sh, row):
    a, b = divmod(row, SUBLANES)
    if b == 0:
        return win[pl.ds(row, CONV_RC), :]
    return sh[b - 1, pl.ds(a * SUBLANES, CONV_RC), :]


def _glu_conv_fwd(proj3, w, bias):
    B = proj3.shape[0]
    nt = SEQ // CONV_TS
    hb = CONV_TS // CONV_HALO

    def body(ua_ref, ub_ref, ha_ref, hb_ref, w_ref, b_ref, o_ref, win, sh):
        i = pl.program_id(2)
        c0 = ua_ref[...].astype(f32) * _sigmoid(ub_ref[...].astype(f32))
        halo = ha_ref[...].astype(f32) * _sigmoid(hb_ref[...].astype(f32))
        win[0:CONV_HALO, :] = jnp.where(i > 0, halo, 0.0)
        win[CONV_HALO:, :] = c0
        _fill_shifted(win, sh)
        for r0 in range(0, CONV_TS, CONV_RC):
            acc = jnp.zeros((CONV_RC, CB), f32) + b_ref[...]
            for k in range(CONV_K):
                acc = acc + _rows_at(win, sh, r0 + CONV_HALO - (CONV_K - 1) + k) * w_ref[k:k + 1, :]
            o_ref[r0:r0 + CONV_RC, :] = acc

    def cur(cb):
        return pl.BlockSpec((None, CONV_TS, CB), lambda b, j, i: (b, i, cb + j))

    def prev(cb):
        return pl.BlockSpec((None, CONV_HALO, CB), lambda b, j, i: (b, jnp.maximum(i * hb - 1, 0), cb + j))

    return pl.pallas_call(
        body, grid=(B, 2, nt),
        in_specs=[cur(UA_CB), cur(UB_CB), prev(UA_CB), prev(UB_CB),
                  pl.BlockSpec((CONV_K, CB), lambda b, j, i: (0, j)), pl.BlockSpec((1, CB), lambda b, j, i: (0, j))],
        out_specs=pl.BlockSpec((None, CONV_TS, CB), lambda b, j, i: (b, i, j)),
        out_shape=SDS((B, SEQ, D_MODEL), f32),
        scratch_shapes=[pltpu.VMEM((CONV_WIN, CB), f32), pltpu.VMEM((SUBLANES - 1, CONV_WIN - SUBLANES, CB), f32)],
        name="glu_conv_fwd", compiler_params=_cp(("parallel", "parallel", "parallel")))(proj3, proj3, proj3, proj3, w, bias)


def _glu_conv_bwd(dc1, proj3, w, comm=None):
    B = proj3.shape[0]
    nt = SEQ // CONV_TS
    hb = CONV_TS // CONV_HALO

    def body(d_ref, dn_ref, ua_ref, ub_ref, ha_ref, hb_ref, w_ref, dua_ref, dub_ref, dw_ref, db_ref, winc, wind, accw, shc, shd):
        b = pl.program_id(1)
        i = pl.program_id(2)
        first = jnp.logical_and(b == 0, i == 0)
        last = jnp.logical_and(b == B - 1, i == nt - 1)

        @pl.when(first)
        def _():
            accw[...] = jnp.zeros_like(accw)
            db_ref[...] = jnp.zeros_like(db_ref)

        halo = ha_ref[...].astype(f32) * _sigmoid(hb_ref[...].astype(f32))
        winc[0:CONV_HALO, :] = jnp.where(i > 0, halo, 0.0)
        winc[CONV_HALO:, :] = ua_ref[...].astype(f32) * _sigmoid(ub_ref[...].astype(f32))
        wind[0:CONV_TS, :] = d_ref[...]
        wind[CONV_TS:, :] = jnp.where(i < nt - 1, dn_ref[...], 0.0)
        db_ref[...] += jnp.sum(d_ref[...], axis=0, keepdims=True)
        _fill_shifted(winc, shc)
        _fill_shifted(wind, shd)
        for r0 in range(0, CONV_TS, CONV_RC):
            dc0 = jnp.zeros((CONV_RC, CB), f32)
            for k in range(CONV_K):
                dc0 = dc0 + _rows_at(wind, shd, r0 + (CONV_K - 1) - k) * w_ref[k:k + 1, :]
            uav = ua_ref[r0:r0 + CONV_RC, :].astype(f32)
            sg = _sigmoid(ub_ref[r0:r0 + CONV_RC, :].astype(f32))
            dua_ref[r0:r0 + CONV_RC, :] = (dc0 * sg).astype(bf16)
            dub_ref[r0:r0 + CONV_RC, :] = (dc0 * uav * sg * (1.0 - sg)).astype(bf16)
            dv = wind[r0:r0 + CONV_RC, :]
            for k in range(CONV_K):
                prod = dv * _rows_at(winc, shc, r0 + CONV_HALO - (CONV_K - 1) + k)
                accw[k] += jnp.sum(prod.reshape(CONV_RC // 8, 8, CB), axis=0)

        @pl.when(last)
        def _():
            for k in range(CONV_K):
                dw_ref[k:k + 1, :] = jnp.sum(accw[k], axis=0, keepdims=True)
            dw_ref[CONV_K:, :] = jnp.zeros((CONV_HALO - CONV_K, CB), f32)

    def cur(cb):
        return pl.BlockSpec((None, CONV_TS, CB), lambda j, b, i: (b, i, cb + j))

    def prev(cb):
        return pl.BlockSpec((None, CONV_HALO, CB), lambda j, b, i: (b, jnp.maximum(i * hb - 1, 0), cb + j))

    nxt = pl.BlockSpec((None, CONV_HALO, CB), lambda j, b, i: (b, jnp.minimum((i + 1) * hb, SEQ // CONV_HALO - 1), j))
    big = SDS((B, SEQ, D_MODEL), bf16)
    return _launch(
        body, name="glu_conv_bwd", grid=(2, B, nt),
        in_specs=[cur(0), nxt, cur(UA_CB), cur(UB_CB), prev(UA_CB), prev(UB_CB), pl.BlockSpec((CONV_K, CB), lambda j, b, i: (0, j))],
        out_specs=[cur(0), cur(0), pl.BlockSpec((CONV_HALO, CB), lambda j, b, i: (0, j)), pl.BlockSpec((1, CB), lambda j, b, i: (0, j))],
        out_shape=(big, big, SDS((CONV_HALO, D_MODEL), f32), SDS((1, D_MODEL), f32)),
        args=(dc1, dc1, proj3, proj3, proj3, proj3, w),
        scratch_shapes=[pltpu.VMEM((CONV_WIN, CB), f32), pltpu.VMEM((CONV_WIN, CB), f32), pltpu.VMEM((CONV_K, SUBLANES, CB), f32),
                        pltpu.VMEM((SUBLANES - 1, CONV_WIN - SUBLANES, CB), f32),
                        pltpu.VMEM((SUBLANES - 1, CONV_WIN - SUBLANES, CB), f32)],
        sem=("parallel", "arbitrary", "arbitrary"), comm=comm)


def _band(first, dil):
    kw = Q_BLOCK if first else 2 * Q_BLOCK
    qi = lax.broadcasted_iota(jnp.int32, (Q_BLOCK, kw), 0)
    kj = lax.broadcasted_iota(jnp.int32, (Q_BLOCK, kw), 1)
    rel = qi - kj + (0 if first else Q_BLOCK)
    valid = jnp.logical_and(rel >= 0, rel <= Q_BLOCK)
    return valid, rel.astype(f32) * float(dil)


def _bias(first, dil, slope):
    valid, dist = _band(first, dil)
    return jnp.where(valid, -slope * dist, NEG)


def _scores(q, k, bias):
    return lax.dot_general(q, k, NT, preferred_element_type=f32) * (HEAD_DIM ** -0.5) + bias


def _pair_cols(hp):
    return slice(hp * LANES, (hp + 1) * LANES)


def _half(x2, e):
    lane = lax.broadcasted_iota(jnp.int32, (1, LANES), 1)
    keep = (lane < HEAD_DIM) if e == 0 else (lane >= HEAD_DIM)
    return jnp.where(keep, x2, jnp.zeros_like(x2))


def _attn_fwd(g, q4, k4, v4, qcb, kcb, vcb, comm=None):
    _, dil = GROUPS[g]
    B, r, L, _ = q4.shape
    nb = L // Q_BLOCK
    slopes = [float(s) for s in _SLOPES[g]]

    def body(q_ref, k_ref, v_ref, o_ref, lse_ref, bias, s_scr, p_scr):
        lane = lax.broadcasted_iota(jnp.int32, (Q_BLOCK, LANES), 1)
        if nb > 1:
            for h in range(HEADS):
                bias[h] = _bias(False, dil, slopes[h])

        def block(n, first):
            q0 = 0 if first else pl.multiple_of(n * Q_BLOCK, Q_BLOCK)
            k0 = 0 if first else pl.multiple_of((n - 1) * Q_BLOCK, Q_BLOCK)
            kw = Q_BLOCK if first else 2 * Q_BLOCK
            for hp in range(HEADS // 2):
                q2 = q_ref[pl.ds(q0, Q_BLOCK), _pair_cols(hp)]
                k2 = k_ref[pl.ds(k0, kw), _pair_cols(hp)]
                for e in range(2):
                    h = 2 * hp + e
                    b_h = _bias(True, dil, slopes[h]) if first else bias[h]
                    s_scr[h, :, :kw] = _scores(_half(q2, e), k2, b_h)
            st = jnp.zeros((Q_BLOCK, LANES), f32)
            dens = jnp.ones((Q_BLOCK, LANES), f32)
            for h in range(HEADS):
                s = s_scr[h, :, :kw]
                m = jnp.max(s, axis=-1, keepdims=True)
                p = jnp.exp(s - m)
                den = jnp.sum(p, axis=-1, keepdims=True)
                p_scr[h, :, :kw] = p.astype(bf16)
                st = jnp.where(lane == h, m + jnp.log(den), st)
                dens = jnp.where(lane == h, den, dens)
            lse_ref[pl.ds(q0, Q_BLOCK), :] = st
            inv = 1.0 / dens
            for hp in range(HEADS // 2):
                v2 = v_ref[pl.ds(k0, kw), _pair_cols(hp)]
                o2 = sum(jnp.dot(p_scr[2 * hp + e, :, :kw], _half(v2, e), preferred_element_type=f32) * inv[:, 2 * hp + e:2 * hp + e + 1]
                         for e in range(2))
                o_ref[pl.ds(q0, Q_BLOCK), _pair_cols(hp)] = o2.astype(bf16)

        block(0, True)
        if nb > 1:
            def step(n, carry):
                block(n, False)
                return carry
            lax.fori_loop(1, nb, step, 0)

    def spec(cb):
        return pl.BlockSpec((None, None, L, GW), lambda b, c: (b, c, 0, cb))

    return _launch(
        body, name=f"attn_fwd_g{g}", grid=(B, r), in_specs=[spec(qcb), spec(kcb), spec(vcb)],
        out_specs=[spec(0), pl.BlockSpec((None, None, L, LANES), lambda b, c: (b, c, 0, 0))],
        out_shape=(SDS((B, r, L, GW), bf16), SDS((B, r, L, LANES), f32)), args=(q4, k4, v4),
        scratch_shapes=[pltpu.VMEM((HEADS, Q_BLOCK, 2 * Q_BLOCK), f32), pltpu.VMEM((HEADS, Q_BLOCK, 2 * Q_BLOCK), f32),
                        pltpu.VMEM((HEADS, Q_BLOCK, 2 * Q_BLOCK), bf16)],
        sem=("parallel", "parallel"), comm=comm)


def _attn_bwd(g, q4, k4, v4, qcb, kcb, vcb, do4, lse4, dl4, comm=None):
    _, dil = GROUPS[g]
    B, r, L, _ = q4.shape
    nb = L // Q_BLOCK
    slopes = [float(s) for s in _SLOPES[g]]
    scale = HEAD_DIM ** -0.5

    def body(q_ref, k_ref, v_ref, do_ref, lse_ref, dl_ref, dq_ref, dk_ref, dv_ref, dk_acc, dv_acc, bias, s_scr, dp_scr, p_scr, ds_scr):
        dk_acc[...] = jnp.zeros_like(dk_acc)
        dv_acc[...] = jnp.zeros_like(dv_acc)
        if nb > 1:
            for h in range(HEADS):
                bias[h] = _bias(False, dil, slopes[h])

        def block(n, first):
            q0 = 0 if first else pl.multiple_of(n * Q_BLOCK, Q_BLOCK)
            k0 = 0 if first else pl.multiple_of((n - 1) * Q_BLOCK, Q_BLOCK)
            kw = Q_BLOCK if first else 2 * Q_BLOCK
            for hp in range(HEADS // 2):
                q2 = q_ref[pl.ds(q0, Q_BLOCK), _pair_cols(hp)]
                k2 = k_ref[pl.ds(k0, kw), _pair_cols(hp)]
                v2 = v_ref[pl.ds(k0, kw), _pair_cols(hp)]
                do2 = do_ref[pl.ds(q0, Q_BLOCK), _pair_cols(hp)]
                for e in range(2):
                    h = 2 * hp + e
                    b_h = _bias(True, dil, slopes[h]) if first else bias[h]
                    s_scr[h, :, :kw] = _scores(_half(q2, e), k2, b_h)
                    dp_scr[h, :, :kw] = lax.dot_general(_half(do2, e), v2, NT, preferred_element_type=f32)
            for h in range(HEADS):
                p = jnp.exp(s_scr[h, :, :kw] - lse_ref[pl.ds(q0, Q_BLOCK), h:h + 1])
                p_scr[h, :, :kw] = p.astype(bf16)
                ds_scr[h, :, :kw] = (p * (dp_scr[h, :, :kw] - dl_ref[pl.ds(q0, Q_BLOCK), h:h + 1])).astype(bf16)
            for hp in range(HEADS // 2):
                cols = _pair_cols(hp)
                q2 = q_ref[pl.ds(q0, Q_BLOCK), cols]
                k2 = k_ref[pl.ds(k0, kw), cols]
                do2 = do_ref[pl.ds(q0, Q_BLOCK), cols]
                ds = [ds_scr[2 * hp + e, :, :kw] for e in range(2)]
                dq2 = sum(jnp.dot(ds[e], _half(k2, e), preferred_element_type=f32) for e in range(2))
                dq_ref[pl.ds(q0, Q_BLOCK), cols] = (dq2 * scale).astype(bf16)
                dk2 = sum(lax.dot_general(ds[e], _half(q2, e), TN, preferred_element_type=f32) for e in range(2))
                dk_acc[pl.ds(k0, kw), cols] += dk2 * scale
                dv2 = sum(lax.dot_general(p_scr[2 * hp + e, :, :kw], _half(do2, e), TN, preferred_element_type=f32) for e in range(2))
                dv_acc[pl.ds(k0, kw), cols] += dv2

        block(0, True)
        if nb > 1:
            def step(n, carry):
                block(n, False)
                return carry
            lax.fori_loop(1, nb, step, 0)
        dk_ref[...] = dk_acc[...].astype(bf16)
        dv_ref[...] = dv_acc[...].astype(bf16)

    def spec(cb):
        return pl.BlockSpec((None, None, L, GW), lambda b, c: (b, c, 0, cb))

    st = pl.BlockSpec((None, None, L, LANES), lambda b, c: (b, c, 0, 0))
    osd = SDS((B, r, L, GW), bf16)
    return _launch(
        body, name=f"attn_bwd_g{g}", grid=(B, r), in_specs=[spec(qcb), spec(kcb), spec(vcb), spec(0), st, st],
        out_specs=[spec(0), spec(0), spec(0)], out_shape=(osd, osd, osd), args=(q4, k4, v4, do4, lse4, dl4),
        scratch_shapes=[pltpu.VMEM((L, GW), f32), pltpu.VMEM((L, GW), f32)]
        + [pltpu.VMEM((HEADS, Q_BLOCK, 2 * Q_BLOCK), f32)] * 3 + [pltpu.VMEM((HEADS, Q_BLOCK, 2 * Q_BLOCK), bf16)] * 2,
        sem=("parallel", "parallel"), comm=comm)


RT = 512
RCH = GW // LANES
DILS = tuple(d for _, d in GROUPS[1:])


def _res_spec(r, width):
    return pl.BlockSpec((None, r, RT // r, width), lambda b, i, *_: (b, 0, i, 0))


def _tok_spec(width, cb=0):
    return pl.BlockSpec((None, RT, width), lambda b, i, *_: (b, i, cb))


def _to_residues(res_ref, scr, r, width):
    for c in range(r):
        for jj in range(width // LANES):
            res_ref[c, :, jj * LANES:(jj + 1) * LANES] = scr[jj, pl.ds(c, RT // r, stride=r), :].astype(res_ref.dtype)


def _from_residues(scr, res_ref, r, width):
    for c in range(r):
        for jj in range(width // LANES):
            scr[jj, pl.ds(c, RT // r, stride=r), :] = res_ref[c, :, jj * LANES:(jj + 1) * LANES].astype(f32)


def _qkv_to_residues(proj3, g):
    r = GROUPS[g][1]
    B = proj3.shape[0]

    def body(q_ref, k_ref, v_ref, o_ref, scr):
        for p, x_ref in enumerate((q_ref, k_ref, v_ref)):
            x = x_ref[...].astype(f32)
            for jj in range(RCH):
                scr[p * RCH + jj] = x[:, jj * LANES:(jj + 1) * LANES]
        _to_residues(o_ref, scr, r, ATTN_W)

    return pl.pallas_call(
        body, grid=(B, SEQ // RT), in_specs=[_tok_spec(GW, 3 * p + g) for p in range(3)], out_specs=_res_spec(r, ATTN_W),
        out_shape=SDS((B, r, SEQ // r, ATTN_W), bf16), scratch_shapes=[pltpu.VMEM((3 * RCH, RT, LANES), f32)],
        name=f"qkv_to_residues_g{g}", compiler_params=_cp(("parallel", "parallel")))(proj3, proj3, proj3)


def _attn_mix(o0, l0, o_res, l_res):
    B = o0.shape[0]

    def body(o0_ref, l0_ref, o1_ref, o2_ref, l1_ref, l2_ref, y_ref, lt_ref, lt1_ref, lt2_ref, so, sl):
        for gi, (o_ref, l_ref, r) in enumerate(((o1_ref, l1_ref, DILS[0]), (o2_ref, l2_ref, DILS[1]))):
            _from_residues(so.at[gi], o_ref, r, GW)
            _from_residues(sl.at[gi:gi + 1], l_ref, r, LANES)
        ls = [l0_ref[...], sl[0], sl[1]]
        m = functools.reduce(jnp.maximum, ls)
        ws = [jnp.exp(l - m) for l in ls]
        den = ws[0] + ws[1] + ws[2]
        alphas = [w / den for w in ws]
        lt = m + jnp.log(den)
        lt_ref[...] = lt
        sl[2] = lt
        _to_residues(lt1_ref, sl.at[2:3], DILS[0], LANES)
        _to_residues(lt2_ref, sl.at[2:3], DILS[1], LANES)
        for h in range(HEADS):
            cols = slice(h * HEAD_DIM, (h + 1) * HEAD_DIM)
            jj, lo = divmod(h * HEAD_DIM, LANES)
            acc = alphas[0][:, h:h + 1] * o0_ref[:, cols].astype(f32)
            for gi in range(2):
                acc = acc + alphas[gi + 1][:, h:h + 1] * so[gi, jj, :, lo:lo + HEAD_DIM]
            y_ref[:, cols] = acc.astype(bf16)

    in_specs = [_tok_spec(GW), _tok_spec(LANES), _res_spec(DILS[0], GW), _res_spec(DILS[1], GW), _res_spec(DILS[0], LANES), _res_spec(DILS[1], LANES)]
    out_specs = [_tok_spec(GW), _tok_spec(LANES), _res_spec(DILS[0], LANES), _res_spec(DILS[1], LANES)]
    return pl.pallas_call(
        body, grid=(B, SEQ // RT), in_specs=in_specs, out_specs=out_specs,
        out_shape=(SDS((B, SEQ, GW), bf16), SDS((B, SEQ, LANES), f32)) + tuple(SDS((B, r, SEQ // r, LANES), f32) for r in DILS),
        scratch_shapes=[pltpu.VMEM((2, RCH, RT, LANES), f32), pltpu.VMEM((3, RT, LANES), f32)],
        name="attn_mix", compiler_params=_cp(("parallel", "parallel")))(o0, l0, *o_res, *l_res)


def _attn_delta(dmix, mix):
    B = dmix.shape[0]

    def body(d_ref, y_ref, dl_ref, dl1_ref, dl2_ref, dm1_ref, dm2_ref, sx, sd):
        lane = lax.broadcasted_iota(jnp.int32, (RT, LANES), 1)
        acc = jnp.zeros((RT, LANES), f32)
        dv = d_ref[...].astype(f32)
        for jj in range(RCH):
            sx[jj] = dv[:, jj * LANES:(jj + 1) * LANES]
        for h in range(HEADS):
            cols = slice(h * HEAD_DIM, (h + 1) * HEAD_DIM)
            dl = jnp.sum(dv[:, cols] * y_ref[:, cols].astype(f32), axis=-1, keepdims=True)
            acc = jnp.where(lane == h, dl, acc)
        dl_ref[...] = acc
        sd[0] = acc
        _to_residues(dl1_ref, sd, DILS[0], LANES)
        _to_residues(dl2_ref, sd, DILS[1], LANES)
        _to_residues(dm1_ref, sx, DILS[0], GW)
        _to_residues(dm2_ref, sx, DILS[1], GW)

    return pl.pallas_call(
        body, grid=(B, SEQ // RT), in_specs=[_tok_spec(GW), _tok_spec(GW)],
        out_specs=[_tok_spec(LANES), _res_spec(DILS[0], LANES), _res_spec(DILS[1], LANES), _res_spec(DILS[0], GW), _res_spec(DILS[1], GW)],
        out_shape=(SDS((B, SEQ, LANES), f32),) + tuple(SDS((B, r, SEQ // r, LANES), f32) for r in DILS)
        + tuple(SDS((B, r, SEQ // r, GW), bf16) for r in DILS),
        scratch_shapes=[pltpu.VMEM((RCH, RT, LANES), f32), pltpu.VMEM((1, RT, LANES), f32)],
        name="attn_delta", compiler_params=_cp(("parallel", "parallel")))(dmix, mix)


N_CB = IN_W // CB


def _assemble_dproj(dqkv, dua, dub, dga, dgc):
    B = dua.shape[0]
    ng = len(GROUPS)
    flat = [dqkv[g][p] for p in range(3) for g in range(ng)]
    wide = [dua, dub, dga, dgc]

    def body(*refs):
        srcs, wides, o_ref, scr = refs[:3 * ng], refs[3 * ng:3 * ng + 4], refs[3 * ng + 4], refs[3 * ng + 5]
        for jv in range(3 * ng):
            g = jv % ng
            if g == 0:
                o_ref[:, jv * GW:(jv + 1) * GW] = srcs[jv][...]
            else:
                _from_residues(scr, srcs[jv], GROUPS[g][1], GW)
                for jj in range(RCH):
                    o_ref[:, jv * GW + jj * LANES:jv * GW + (jj + 1) * LANES] = scr[jj].astype(bf16)
        for wv in range(4):
            lo = 3 * ATTN_W + wv * D_MODEL
            o_ref[:, lo:lo + D_MODEL] = wides[wv][...]

    in_specs = [_tok_spec(GW) if (jv % ng) == 0 else _res_spec(GROUPS[jv % ng][1], GW) for jv in range(3 * ng)]
    in_specs += [_tok_spec(D_MODEL)] * 4
    return pl.pallas_call(
        body, grid=(B, SEQ // RT), in_specs=in_specs, out_specs=_tok_spec(IN_W),
        out_shape=SDS((B, SEQ, IN_W), bf16), scratch_shapes=[pltpu.VMEM((RCH, RT, LANES), f32)],
        name="assemble_dproj", compiler_params=_cp(("parallel", "parallel")))(*flat, *wide)


def _add_pairs(part, got, core, name):
    n, h, C = got.shape

    def body(c_ref, a_ref, b_ref, o_ref):
        o_ref[...] = (a_ref[...].astype(f32) + b_ref[...].astype(f32)).astype(bf16)

    blk = pl.BlockSpec((None, h, C), lambda s, c_ref: (s, 0, 0))
    own = pl.BlockSpec((None, h, C), lambda s, c_ref: (s, c_ref[0], 0))
    spec = pltpu.PrefetchScalarGridSpec(num_scalar_prefetch=1, grid=(n,), in_specs=[own, blk], out_specs=blk)
    return pl.pallas_call(body, grid_spec=spec, out_shape=SDS((n, h, C), bf16), name=name,
                          compiler_params=_cp(("parallel",)))(core, part, got)


def _sum4(sums, got, chip, core, name):
    _, h, C = sums.shape

    def body(s_ref, c_ref, own_ref, q_ref, o_ref):
        t = own_ref[...].astype(f32) + q_ref[0].astype(f32)
        t = t + q_ref[1].astype(f32)
        o_ref[...] = t + q_ref[2].astype(f32)

    spec = pltpu.PrefetchScalarGridSpec(
        num_scalar_prefetch=2, grid=(1,),
        in_specs=[pl.BlockSpec((None, h, C), lambda i, s_ref, c_ref: (s_ref[0], 0, 0)),
                  pl.BlockSpec((N_CHIPS - 1, h, C), lambda i, s_ref, c_ref: (0, 0, 0))],
        out_specs=pl.BlockSpec((h, C), lambda i, s_ref, c_ref: (c_ref[0], 0)))
    return pl.pallas_call(body, grid_spec=spec, out_shape=SDS((2 * h, C), f32), name=name,
                          compiler_params=_cp(("arbitrary",)))(chip, core, sums, got)


def _adamw(w, g, m, v, name):
    R, C = w.shape
    rt = R
    for cand in (512, 256, 128, 64, 32, 16, 8):
        if R % cand == 0 and cand * C * 4 <= 2 ** 21:
            rt = cand
            break
    c1 = 1.0 / (1.0 - B1 ** STEP)
    c2 = 1.0 / (1.0 - B2 ** STEP)

    def body(w_ref, g_ref, m_ref, v_ref, d_ref, nm_ref, nv_ref):
        gv = g_ref[...]
        nm = B1 * m_ref[...] + (1.0 - B1) * gv
        nv = B2 * v_ref[...] + (1.0 - B2) * (gv * gv)
        nm_ref[...] = nm
        nv_ref[...] = nv
        d_ref[...] = -LR * ((nm * c1) / (jnp.sqrt(nv * c2) + ADAM_EPS) + WD * w_ref[...])

    blk = pl.BlockSpec((rt, C), lambda i: (i, 0))
    sd = SDS((R, C), f32)
    return pl.pallas_call(body, grid=(R // rt,), in_specs=[blk] * 4, out_specs=[blk] * 3, out_shape=(sd, sd, sd), name=name,
                          compiler_params=_cp(("parallel",)))(w, g, m, v)


def _coords():
    return lax.axis_index("x"), lax.axis_index("y"), lax.axis_index("c")


def _other_chips(x, y):
    return [(1 - x, y), (x, 1 - y), (1 - x, 1 - y)]


def _allgather_weights(shards, slots):
    n = len(shards)
    halves = [s.shape[0] // 2 for s in shards]

    def body(*refs):
        ins, outs = refs[:n], refs[2 * n:3 * n]
        send1, recv1, send2, recv2 = refs[3 * n:]
        x, y, c = _coords()
        me = 2 * x + y
        chips = _other_chips(x, y)
        sends = []
        for w in range(n):
            mine = pl.ds(c * halves[w], halves[w])
            for j, (px, py) in enumerate(chips):
                cp = pltpu.make_async_remote_copy(src_ref=ins[w].at[mine, :], dst_ref=outs[w].at[me, mine, :],
                                                  send_sem=send1.at[w, j], recv_sem=recv1.at[w, j],
                                                  device_id=(px, py, c), device_id_type=MESH)
                cp.start()
                sends.append(cp)
        for w in range(n):
            mine = pl.ds(c * halves[w], halves[w])
            for j, (px, py) in enumerate(chips):
                blk = outs[w].at[2 * px + py, mine, :]
                pltpu.make_async_remote_copy(src_ref=blk, dst_ref=blk, send_sem=send1.at[w, j], recv_sem=recv1.at[w, j],
                                             device_id=(px, py, c), device_id_type=MESH).wait_recv()
                cp = pltpu.make_async_remote_copy(src_ref=blk, dst_ref=blk, send_sem=send2.at[w, j], recv_sem=recv2.at[w, j],
                                                  device_id=(x, y, 1 - c), device_id_type=MESH)
                cp.start()
                sends.append(cp)
        for w in range(n):
            theirs = pl.ds((1 - c) * halves[w], halves[w])
            for j, (px, py) in enumerate(chips):
                blk = outs[w].at[2 * px + py, theirs, :]
                pltpu.make_async_remote_copy(src_ref=blk, dst_ref=blk, send_sem=send2.at[w, j], recv_sem=recv2.at[w, j],
                                             device_id=(x, y, 1 - c), device_id_type=MESH).wait_recv()
        for cp in sends:
            cp.wait_send()

    return pl.pallas_call(
        body, in_specs=[HBM] * (2 * n), out_specs=[HBM] * n,
        out_shape=tuple(SDS((N_CHIPS,) + s.shape, s.dtype) for s in shards),
        input_output_aliases={n + w: w for w in range(n)},
        scratch_shapes=[pltpu.SemaphoreType.DMA((n, 3))] * 4,
        name="allgather_weights", compiler_params=pltpu.CompilerParams(has_side_effects=True))(*shards, *slots)


def _pair_exchange(parts, name):
    n = len(parts)
    halves = [p.shape[1] // 2 for p in parts]

    def body(*refs):
        ins, got_o = refs[:n], refs[n:2 * n]
        send, recv = refs[2 * n:]
        x, y, c = _coords()
        cps = []
        for w in range(n):
            give = pl.ds((1 - c) * halves[w], halves[w])
            rc = pltpu.make_async_remote_copy(src_ref=ins[w].at[:, give, :], dst_ref=got_o[w], send_sem=send.at[w],
                                              recv_sem=recv.at[w], device_id=(x, y, 1 - c), device_id_type=MESH)
            rc.start()
            cps.append(rc)
        for rc in cps:
            rc.wait()

    hs = tuple(SDS((N_CHIPS, p.shape[1] // 2, p.shape[2]), p.dtype) for p in parts)
    return pl.pallas_call(body, in_specs=[HBM] * n, out_specs=[HBM] * n, out_shape=hs,
                          scratch_shapes=[pltpu.SemaphoreType.DMA((n,))] * 2, name=name,
                          compiler_params=pltpu.CompilerParams(has_side_effects=True))(*parts)


def _x_chip_exchange(sums):
    n = len(sums)

    def copies(ins, outs, sems):
        send, recv = sems
        x, y, c = _coords()
        return [pltpu.make_async_remote_copy(src_ref=ins[w].at[2 * px + py], dst_ref=outs[w].at[j], send_sem=send.at[w, j],
                                             recv_sem=recv.at[w, j], device_id=(px, py, c), device_id_type=MESH)
                for w in range(n) for j, (px, py) in enumerate(_other_chips(x, y))]

    def start(ins, outs, sems):
        for cp in copies(ins, outs, sems):
            cp.start()

    def wait(ins, outs, sems):
        for cp in copies(ins, outs, sems):
            cp.wait()

    return _Exchange(ins=list(sums), out_shape=[SDS((N_CHIPS - 1,) + s.shape[1:], s.dtype) for s in sums],
                     scratch=[pltpu.SemaphoreType.DMA((n, 3)), pltpu.SemaphoreType.DMA((n, 3))], aliases={}, start=start, wait=wait)


def _x_gather_ici(shards, slots):
    n = len(shards)
    halves = [s.shape[0] // 2 for s in shards]

    def copies(ins, outs, sems):
        send, recv = sems
        x, y, c = _coords()
        me = 2 * x + y
        out = []
        for w in range(n):
            mine = pl.ds(c * halves[w], halves[w])
            for j, (px, py) in enumerate(_other_chips(x, y)):
                snd = pltpu.make_async_remote_copy(src_ref=ins[w].at[mine, :], dst_ref=outs[w].at[me, mine, :], send_sem=send.at[w, j],
                                                   recv_sem=recv.at[w, j], device_id=(px, py, c), device_id_type=MESH)
                got = outs[w].at[2 * px + py, mine, :]
                rcv = pltpu.make_async_remote_copy(src_ref=got, dst_ref=got, send_sem=send.at[w, j], recv_sem=recv.at[w, j],
                                                   device_id=(px, py, c), device_id_type=MESH)
                out.append((snd, rcv))
        return out

    def start(ins, outs, sems):
        for snd, _ in copies(ins, outs, sems):
            snd.start()

    def wait(ins, outs, sems):
        for snd, rcv in copies(ins, outs, sems):
            rcv.wait_recv()
            snd.wait_send()

    return _Exchange(ins=list(shards) + list(slots), out_shape=[SDS(s.shape, s.dtype) for s in slots],
                     scratch=[pltpu.SemaphoreType.DMA((n, 3)), pltpu.SemaphoreType.DMA((n, 3))],
                     aliases={n + w: w for w in range(n)}, start=start, wait=wait)


def _x_gather_fwd(bufs):
    n = len(bufs)
    halves = [b.shape[1] // 2 for b in bufs]

    def copies(ins, outs, sems):
        send, recv = sems
        x, y, c = _coords()
        out = []
        for w in range(n):
            for j, (px, py) in enumerate(_other_chips(x, y)):
                mine = outs[w].at[2 * px + py, pl.ds(c * halves[w], halves[w]), :]
                theirs = outs[w].at[2 * px + py, pl.ds((1 - c) * halves[w], halves[w]), :]
                snd = pltpu.make_async_remote_copy(src_ref=mine, dst_ref=mine, send_sem=send.at[w, j], recv_sem=recv.at[w, j],
                                                   device_id=(x, y, 1 - c), device_id_type=MESH)
                rcv = pltpu.make_async_remote_copy(src_ref=theirs, dst_ref=theirs, send_sem=send.at[w, j], recv_sem=recv.at[w, j],
                                                   device_id=(x, y, 1 - c), device_id_type=MESH)
                out.append((snd, rcv))
        return out

    def start(ins, outs, sems):
        for snd, _ in copies(ins, outs, sems):
            snd.start()

    def wait(ins, outs, sems):
        for snd, rcv in copies(ins, outs, sems):
            rcv.wait_recv()
            snd.wait_send()

    return _Exchange(ins=list(bufs), out_shape=[SDS(b.shape, b.dtype) for b in bufs],
                     scratch=[pltpu.SemaphoreType.DMA((n, 3)), pltpu.SemaphoreType.DMA((n, 3))],
                     aliases={w: w for w in range(n)}, start=start, wait=wait)


def _pair_gather(bufs):
    n = len(bufs)

    def body(*refs):
        outs = refs[n:2 * n]
        send, recv = refs[2 * n:]
        x, y, c = _coords()
        cps = []
        for w in range(n):
            h = bufs[w].shape[0] // 2
            rows = outs[w].at[pl.ds(c * h, h), :]
            rc = pltpu.make_async_remote_copy(src_ref=rows, dst_ref=rows, send_sem=send.at[w], recv_sem=recv.at[w],
                                              device_id=(x, y, 1 - c), device_id_type=MESH)
            rc.start()
            cps.append(rc)
        for w, rc in enumerate(cps):
            h = bufs[w].shape[0] // 2
            other = outs[w].at[pl.ds((1 - c) * h, h), :]
            pltpu.make_async_remote_copy(src_ref=other, dst_ref=other, send_sem=send.at[w], recv_sem=recv.at[w],
                                         device_id=(x, y, 1 - c), device_id_type=MESH).wait_recv()
            rc.wait_send()

    return pl.pallas_call(body, in_specs=[HBM] * n, out_specs=[HBM] * n, out_shape=tuple(SDS(a.shape, a.dtype) for a in bufs),
                          input_output_aliases={w: w for w in range(n)},
                          scratch_shapes=[pltpu.SemaphoreType.DMA((n,))] * 2, name="grad_pair_gather",
                          compiler_params=pltpu.CompilerParams(has_side_effects=True))(*bufs)


def _small_allreduce(vec, name):
    R = vec.shape[0]
    nd = 8

    def body(v_ref, o_ref, buf, send, recv):
        x, y, c = _coords()
        me = 4 * x + 2 * y + c
        buf[me] = v_ref[...]
        cps = []
        for k in range(1, nd):
            kx, ky, kc = (k >> 2) & 1, (k >> 1) & 1, k & 1
            tx = x + kx - 2 * x * kx
            ty = y + ky - 2 * y * ky
            tc = c + kc - 2 * c * kc
            cp = pltpu.make_async_remote_copy(src_ref=v_ref, dst_ref=buf.at[me], send_sem=send.at[k], recv_sem=recv.at[k],
                                              device_id=(tx, ty, tc), device_id_type=MESH)
            cp.start()
            cps.append((cp, 4 * tx + 2 * ty + tc))
        for k, (cp, src) in zip(range(1, nd), cps):
            pltpu.make_async_remote_copy(src_ref=v_ref, dst_ref=buf.at[src], send_sem=send.at[k], recv_sem=recv.at[k],
                                         device_id=(x, y, c), device_id_type=MESH).wait_recv()
        for cp, _ in cps:
            cp.wait_send()
        acc = buf[0]
        for d in range(1, nd):
            acc = acc + buf[d]
        o_ref[...] = acc

    vm = pl.BlockSpec(memory_space=pltpu.VMEM)
    return pl.pallas_call(body, in_specs=[vm], out_specs=vm, out_shape=SDS((R, LANES), f32),
                          scratch_shapes=[pltpu.VMEM((nd, R, LANES), f32), pltpu.SemaphoreType.DMA((nd,)), pltpu.SemaphoreType.DMA((nd,))],
                          name=name, compiler_params=pltpu.CompilerParams(has_side_effects=True))(vec)


def _rows128(a, rows):
    flat = a.reshape(-1)
    return jnp.pad(flat, (0, rows * LANES - flat.shape[0])).reshape(rows, LANES)


GATHER_1 = ("w_conv_out", "w_attn_out", "w_o", "w_ffn_gate")
GATHER_2 = ("w_ffn_up", "w_ffn_down")
REDUCE_A = ("w_ffn_gate", "w_ffn_up", "w_ffn_down")
REDUCE_B = ("w_conv_out", "w_attn_out", "w_o")
REDUCE_C = ("w_in",)


def _step(x, target, norm1_g, gate_b, conv_w, conv_b, conv_ln_g, conv_ln_b, norm2_g, norm_f_g, shards, slots, chip1, core):
    B = x.shape[0]
    T = B * SEQ
    xf = x.reshape(T, D_MODEL)
    tf = target.reshape(T, D_MODEL)

    (w_in,) = _allgather_weights([shards["w_in"]], [slots["w_in"]])
    h = _rms_fwd(xf, norm1_g, "rms1_fwd")
    proj, got1 = _mm("in_proj", h, w_in, grid=(N_CHIPS, T // 512),
                     a_spec=pl.BlockSpec((512, D_MODEL), lambda s, m: (m, 0)),
                     b_spec=pl.BlockSpec((None, D_MODEL, IN_S), lambda s, m: (s, 0, 0)),
                     o_spec=pl.BlockSpec((512, IN_S), lambda s, m: (m, s)), o_shape=(T, IN_W), o_dtype=bf16, dims=NN,
                     comm=_x_gather_ici([shards[n] for n in GATHER_1], [slots[n] for n in GATHER_1]))
    proj3 = proj.reshape(B, SEQ, IN_W)
    proj4 = proj.reshape(B, 1, SEQ, IN_W)

    qkv = [None] + [_qkv_to_residues(proj3, g) for g in range(1, len(GROUPS))]

    def qkv_args(g):
        return (proj4, proj4, proj4, 0, 3, 6) if g == 0 else (qkv[g], qkv[g], qkv[g], 0, 1, 2)

    (o4_0, l4_0), full1 = _attn_fwd(0, *qkv_args(0), comm=_x_gather_fwd(list(got1)))
    (o4_1, l4_1), got2 = _attn_fwd(1, *qkv_args(1), comm=_x_gather_ici([shards[n] for n in GATHER_2], [slots[n] for n in GATHER_2]))
    (o4_2, l4_2), full2 = _attn_fwd(2, *qkv_args(2), comm=_x_gather_fwd(list(got2)))
    full = dict(zip(GATHER_1 + GATHER_2, full1 + full2))
    w_conv_out, w_attn_out, w_o = full["w_conv_out"], full["w_attn_out"], full["w_o"]
    w_gate, w_up, w_down = full["w_ffn_gate"], full["w_ffn_up"], full["w_ffn_down"]
    w_conv_out_f = w_conv_out.reshape(D_MODEL, D_MODEL)
    w_o_f = w_o.reshape(D_MODEL, D_MODEL)
    mix3, lse3, lse_r1, lse_r2 = _attn_mix(o4_0.reshape(B, SEQ, GW), l4_0.reshape(B, SEQ, LANES), [o4_1, o4_2], [l4_1, l4_2])
    mix = mix3.reshape(T, GW)
    y_attn = _mm("attn_out", mix, w_attn_out, grid=(N_CHIPS, T // 512),
                 a_spec=pl.BlockSpec((512, GW), lambda s, m: (m, 0)),
                 b_spec=pl.BlockSpec((None, GW, D_MODEL // N_CHIPS), lambda s, m: (s, 0, 0)),
                 o_spec=pl.BlockSpec((512, D_MODEL // N_CHIPS), lambda s, m: (m, s)), o_shape=(T, D_MODEL), o_dtype=bf16,
                 dims=NN, sem=("parallel", "parallel"))

    c1 = _glu_conv_fwd(proj3, conv_w, conv_b).reshape(T, D_MODEL)
    c3 = _ln_silu_fwd(c1, conv_ln_g, conv_ln_b)
    y_conv = _mm_nn_full("conv_out", c3, w_conv_out_f, bf16)

    merged = _merge_fwd(proj, gate_b, y_attn, y_conv)
    x1 = _mm_nn_full("o_proj", merged, w_o_f, f32, res=xf)

    h2 = _rms_fwd(x1, norm2_g, "rms2_fwd")
    fa, fb, ff = _ffn_up(h2, w_gate, w_up)
    x2 = _mm("ffn_down", ff, w_down, grid=(T // 1024, N_CHIPS),
             a_spec=pl.BlockSpec((None, 1024, FF_S), lambda m, s: (s, m, 0)),
             b_spec=pl.BlockSpec((None, FF_S, D_MODEL), lambda m, s: (s, 0, 0)),
             o_spec=pl.BlockSpec((1024, D_MODEL), lambda m, s: (m, 0)), o_shape=(T, D_MODEL), o_dtype=f32, dims=NN,
             acc_shape=(1024, D_MODEL), k_axis=1, res=x1, res_spec=pl.BlockSpec((1024, D_MODEL), lambda m, s: (m, 0)),
             sem=("parallel", "arbitrary"))

    loss, dx2, d_gf = _final_loss(x2, norm_f_g.reshape(1, D_MODEL), tf)

    d_w_down = _mm("d_w_down", ff, dx2, grid=(N_CHIPS, T // 1024),
                   a_spec=pl.BlockSpec((None, 1024, FF_S), lambda s, k: (s, k, 0)),
                   b_spec=pl.BlockSpec((1024, D_MODEL), lambda s, k: (k, 0)),
                   o_spec=pl.BlockSpec((None, FF_S, D_MODEL), lambda s, k: (s, 0, 0)), o_shape=(N_CHIPS, FF_S, D_MODEL),
                   o_dtype=bf16, dims=TN, acc_shape=(FF_S, D_MODEL), k_axis=1, sem=("parallel", "arbitrary"))
    da, db = _ffn_down_bwd(dx2, w_down, fa, fb)

    def d_w_ff(name, dz):
        return _mm(name, dz, h2, grid=(N_CHIPS, T // 2048),
                   a_spec=pl.BlockSpec((None, 2048, FF_S), lambda s, k: (s, k, 0)),
                   b_spec=pl.BlockSpec((2048, D_MODEL), lambda s, k: (k, 0)),
                   o_spec=pl.BlockSpec((None, FF_S, D_MODEL), lambda s, k: (s, 0, 0)), o_shape=(N_CHIPS, FF_S, D_MODEL),
                   o_dtype=bf16, dims=TN, acc_shape=(FF_S, D_MODEL), k_axis=1, sem=("parallel", "arbitrary"))

    d_w_gate = d_w_ff("d_w_gate", da)
    d_w_up = d_w_ff("d_w_up", db)
    part = dict(w_ffn_gate=d_w_gate, w_ffn_up=d_w_up, w_ffn_down=d_w_down)

    def pair_sums(names, tag):
        got = _pair_exchange([part[n] for n in names], f"grad_pair_exchange_{tag}")
        return [_add_pairs(part[n], g_, core, f"pair_sum_{n}") for n, g_ in zip(names, got)]

    sums_a = pair_sums(REDUCE_A, "a")
    dh2 = _ffn_dh2(da, w_gate, db, w_up)
    dx1, d_g2 = _rms_bwd(dh2, x1, norm2_g, dx2, "rms2_bwd")

    d_w_o = _mm_tn_tokens("d_w_o", merged, dx1, bf16).reshape(N_CHIPS, D_MODEL // N_CHIPS, D_MODEL)
    dmerged = _mm_nt_full("d_merged", dx1, w_o_f, bf16)
    dya, dyc, dga, dgc, d_gba, d_gbc = _merge_bwd(dmerged, proj, gate_b, y_attn, y_conv)

    d_w_conv_out = _mm_tn_tokens("d_w_conv_out", c3, dyc, bf16).reshape(N_CHIPS, D_MODEL // N_CHIPS, D_MODEL)
    dc3 = _mm_nt_full("d_c3", dyc, w_conv_out_f, bf16)
    dc1, d_ln_g, d_ln_b = _ln_silu_bwd(dc3, c1, conv_ln_g, conv_ln_b)
    (dua, dub, d_conv_w, d_conv_b), got_a = _glu_conv_bwd(dc1.reshape(B, SEQ, D_MODEL), proj3, conv_w, comm=_x_chip_exchange(sums_a))

    d_w_attn_out = _mm("d_w_attn_out", mix, dya, grid=(N_CHIPS, T // 512),
                       a_spec=pl.BlockSpec((512, GW), lambda s, k: (k, 0)),
                       b_spec=pl.BlockSpec((512, D_MODEL // N_CHIPS), lambda s, k: (k, s)),
                       o_spec=pl.BlockSpec((None, GW, D_MODEL // N_CHIPS), lambda s, k: (s, 0, 0)),
                       o_shape=(N_CHIPS, GW, D_MODEL // N_CHIPS), o_dtype=bf16, dims=TN, acc_shape=(GW, D_MODEL // N_CHIPS),
                       k_axis=1, sem=("parallel", "arbitrary"))
    dmix = _mm("d_mix", dya, w_attn_out, grid=(T // 512, N_CHIPS),
               a_spec=pl.BlockSpec((512, D_MODEL // N_CHIPS), lambda m, s: (m, s)),
               b_spec=pl.BlockSpec((None, GW, D_MODEL // N_CHIPS), lambda m, s: (s, 0, 0)),
               o_spec=pl.BlockSpec((512, GW), lambda m, s: (m, 0)), o_shape=(T, GW), o_dtype=bf16, dims=NT,
               acc_shape=(512, GW), k_axis=1, sem=("parallel", "arbitrary"))
    dmix3 = dmix.reshape(B, SEQ, GW)
    delta3, delta_r1, delta_r2, dmix_r1, dmix_r2 = _attn_delta(dmix3, mix3)
    part.update(w_conv_out=d_w_conv_out, w_attn_out=d_w_attn_out, w_o=d_w_o)
    sums_b = pair_sums(REDUCE_B, "b")
    one = (B, 1, SEQ)
    (dq0, dk0, dv0), got_b = _attn_bwd(0, *qkv_args(0), dmix3.reshape(one + (GW,)), lse3.reshape(one + (LANES,)),
                                       delta3.reshape(one + (LANES,)), comm=_x_chip_exchange(sums_b))
    dqkv = [tuple(t.reshape(B, SEQ, GW) for t in (dq0, dk0, dv0)),
            _attn_bwd(1, *qkv_args(1), dmix_r1, lse_r1, delta_r1), _attn_bwd(2, *qkv_args(2), dmix_r2, lse_r2, delta_r2)]
    dproj = _assemble_dproj(dqkv, dua, dub, dga.reshape(B, SEQ, D_MODEL), dgc.reshape(B, SEQ, D_MODEL)).reshape(T, IN_W)

    d_w_in = _mm("d_w_in", h, dproj, grid=(N_CHIPS, T // 1024),
                 a_spec=pl.BlockSpec((1024, D_MODEL), lambda s, k: (k, 0)),
                 b_spec=pl.BlockSpec((1024, IN_S), lambda s, k: (k, s)),
                 o_spec=pl.BlockSpec((None, D_MODEL, IN_S), lambda s, k: (s, 0, 0)), o_shape=(N_CHIPS, D_MODEL, IN_S),
                 o_dtype=bf16, dims=TN, acc_shape=(D_MODEL, IN_S), k_axis=1, sem=("parallel", "arbitrary"))
    part.update(w_in=d_w_in)
    sums_c = pair_sums(REDUCE_C, "c")
    dh, got_c = _mm("d_h", dproj, w_in, grid=(T // 512, N_CHIPS),
                    a_spec=pl.BlockSpec((512, IN_S), lambda m, s: (m, s)),
                    b_spec=pl.BlockSpec((None, D_MODEL, IN_S), lambda m, s: (s, 0, 0)),
                    o_spec=pl.BlockSpec((512, D_MODEL), lambda m, s: (m, 0)), o_shape=(T, D_MODEL), o_dtype=f32, dims=NT,
                    acc_shape=(512, D_MODEL), k_axis=1, comm=_x_chip_exchange(sums_c))
    dx, d_g1 = _rms_bwd(dh, xf, norm1_g, dx1, "rms1_bwd")

    names = REDUCE_A + REDUCE_B + REDUCE_C
    halves = [_sum4(s, g_, chip1, core, f"chip_sum_{n}") for n, s, g_ in zip(names, sums_a + sums_b + sums_c, got_a + got_b + got_c)]
    big = dict(zip(names, _pair_gather(halves)))
    small = dict(norm1_g=d_g1, gate_b=jnp.concatenate([d_gba, d_gbc], axis=-1), conv_b=d_conv_b, conv_ln_g=d_ln_g,
                 conv_ln_b=d_ln_b, norm2_g=d_g2, norm_f_g=d_gf, conv_w=d_conv_w)
    return loss, dx.reshape(B, SEQ, D_MODEL), big, small


BIG = ("w_in", "w_conv_out", "w_attn_out", "w_o", "w_ffn_gate", "w_ffn_up", "w_ffn_down")
TRANSPOSED = ("w_ffn_gate", "w_ffn_up")
SMALL = ("norm1_g", "gate_b", "conv_b", "conv_ln_g", "conv_ln_b", "norm2_g", "norm_f_g")
SMALL_ROWS = {"norm1_g": 8, "gate_b": 16, "conv_b": 8, "conv_ln_g": 8, "conv_ln_b": 8, "norm2_g": 8, "norm_f_g": 8}
LOSS_ROWS = 8
CONVW_ROWS = 32 * D_MODEL // LANES


def kernel(x, norm1_g, w_in, gate_b, conv_w, conv_b, conv_ln_g, conv_ln_b, w_conv_out, w_attn_out, w_o, norm2_g, w_ffn_gate, w_ffn_up, w_ffn_down, norm_f_g, loss_target, m_norm1_g, m_w_in, m_gate_b, m_conv_w, m_conv_b, m_conv_ln_g, m_conv_ln_b, m_w_conv_out, m_w_attn_out, m_w_o, m_norm2_g, m_w_ffn_gate, m_w_ffn_up, m_w_ffn_down, m_norm_f_g, v_norm1_g, v_w_in, v_gate_b, v_conv_w, v_conv_b, v_conv_ln_g, v_conv_ln_b, v_w_conv_out, v_w_attn_out, v_w_o, v_norm2_g, v_w_ffn_gate, v_w_ffn_up, v_w_ffn_down, v_norm_f_g):
    W = dict(norm1_g=norm1_g, w_in=w_in, gate_b=gate_b, conv_w=conv_w, conv_b=conv_b, conv_ln_g=conv_ln_g, conv_ln_b=conv_ln_b,
             w_conv_out=w_conv_out, w_attn_out=w_attn_out, w_o=w_o, norm2_g=norm2_g, w_ffn_gate=w_ffn_gate, w_ffn_up=w_ffn_up,
             w_ffn_down=w_ffn_down, norm_f_g=norm_f_g)
    M = dict(norm1_g=m_norm1_g, w_in=m_w_in, gate_b=m_gate_b, conv_w=m_conv_w, conv_b=m_conv_b, conv_ln_g=m_conv_ln_g,
             conv_ln_b=m_conv_ln_b, w_conv_out=m_w_conv_out, w_attn_out=m_w_attn_out, w_o=m_w_o, norm2_g=m_norm2_g,
             w_ffn_gate=m_w_ffn_gate, w_ffn_up=m_w_ffn_up, w_ffn_down=m_w_ffn_down, norm_f_g=m_norm_f_g)
    V = dict(norm1_g=v_norm1_g, w_in=v_w_in, gate_b=v_gate_b, conv_w=v_conv_w, conv_b=v_conv_b, conv_ln_g=v_conv_ln_g,
             conv_ln_b=v_conv_ln_b, w_conv_out=v_w_conv_out, w_attn_out=v_w_attn_out, w_o=v_w_o, norm2_g=v_norm2_g,
             w_ffn_gate=v_w_ffn_gate, w_ffn_up=v_w_ffn_up, w_ffn_down=v_w_ffn_down, norm_f_g=v_norm_f_g)
    order = list(W)

    def as2d(n, a):
        a = a.reshape(a.shape[-2:])
        return a.T if n in TRANSPOSED else a

    def from2d(n, a):
        return (a.T if n in TRANSPOSED else a).reshape(W[n].shape)

    shard2d = {n: as2d(n, W[n]) for n in BIG}
    chip = 2 * lax.axis_index("x") + lax.axis_index("y")

    cw = jnp.zeros((CONV_K, D_MODEL), f32)
    cw = lax.dynamic_update_slice(cw, 0.5 * conv_w.reshape(CONV_K, D_MODEL // N_CHIPS), (0, chip * (D_MODEL // N_CHIPS)))
    conv_w_full = _small_allreduce(_rows128(cw, CONVW_ROWS), "conv_w_gather")[: CONV_K * D_MODEL // LANES].reshape(CONV_K, D_MODEL)

    core = lax.axis_index("c").astype(jnp.int32).reshape(1)
    chip1 = chip.astype(jnp.int32).reshape(1)
    shards = {n: shard2d[n].astype(bf16) for n in BIG}
    slots = {n: lax.dynamic_update_slice(lax.empty((N_CHIPS,) + s.shape, bf16), s[None], (chip, 0, 0)) for n, s in shards.items()}

    loss, grad_x, grads, gsmall = _step(x, loss_target, norm1_g, gate_b, conv_w_full, conv_b, conv_ln_g, conv_ln_b, norm2_g,
                                        norm_f_g, shards, slots, chip1, core)

    pieces = [_rows128(loss, LOSS_ROWS)] + [_rows128(gsmall[n], SMALL_ROWS[n]) for n in SMALL] + [_rows128(gsmall["conv_w"], CONVW_ROWS)]
    tot = _small_allreduce(jnp.concatenate(pieces, axis=0), "small_allreduce")
    loss_out = tot[0, 0]
    row = LOSS_ROWS
    for n in SMALL:
        grads[n] = tot[row: row + W[n].size // LANES].reshape(W[n].shape)
        row += SMALL_ROWS[n]
    dcw = tot[row: row + CONV_K * D_MODEL // LANES].reshape(CONV_K, D_MODEL)
    grads["conv_w"] = lax.dynamic_slice(dcw, (0, chip * (D_MODEL // N_CHIPS)), (CONV_K, D_MODEL // N_CHIPS))

    delta, new_m, new_v = {}, {}, {}
    for n in BIG:
        d, nm, nv = _adamw(shard2d[n], grads[n], as2d(n, M[n]), as2d(n, V[n]), f"adamw_{n}")
        delta[n], new_m[n], new_v[n], grads[n] = (from2d(n, t) for t in (d, nm, nv, grads[n]))

    def pack(src):
        return jnp.concatenate([_rows128(src[n], SMALL_ROWS[n]) for n in SMALL], axis=0)

    d, nm, nv = _adamw(pack(W), pack(grads), pack(M), pack(V), "adamw_small")
    row = 0
    for n in SMALL:
        k = W[n].size // LANES
        delta[n], new_m[n], new_v[n] = (t[row: row + k].reshape(W[n].shape) for t in (d, nm, nv))
        row += SMALL_ROWS[n]

    def pad32(a):
        return jnp.pad(a.reshape(CONV_K, D_MODEL // N_CHIPS), ((0, 1), (0, 0)))

    d, nm, nv = _adamw(pad32(conv_w), pad32(grads["conv_w"]), pad32(m_conv_w), pad32(v_conv_w), "adamw_conv_w")
    delta["conv_w"], new_m["conv_w"], new_v["conv_w"] = (t[:CONV_K].reshape(conv_w.shape) for t in (d, nm, nv))
    grads["conv_w"] = grads["conv_w"].reshape(conv_w.shape)

    return (loss_out, grad_x, *[grads[n] for n in order], *[delta[n] for n in order],
            *[new_m[n] for n in order], *[new_v[n] for n in order])
```

```python
import functools
import math
from typing import Callable, NamedTuple

import numpy as np
import jax
import jax.numpy as jnp
from jax import lax
from jax.experimental import pallas as pl
from jax.experimental.pallas import tpu as pltpu

f32 = jnp.float32
bf16 = jnp.bfloat16
SDS = jax.ShapeDtypeStruct
MESH = pl.DeviceIdType.MESH

D_MODEL = 1024
SEQ = 2048
HEAD_DIM = 64
HEADS = 8
GROUPS = ((128, 1), (512, 4), (2048, 16))
GW = HEADS * HEAD_DIM
ATTN_W = len(GROUPS) * GW
Q_BLOCK = 128
CONV_K = 31
D_FF = 2816
IN_W = 3 * ATTN_W + 2 * D_MODEL + 2 * D_MODEL
N_CHIPS = 4
IN_S = IN_W // N_CHIPS
FF_S = D_FF // N_CHIPS
RMS_EPS = 1e-6
LN_EPS = 1e-5
LR, B1, B2, ADAM_EPS, WD, STEP = 0.001, 0.9, 0.999, 1e-08, 0.01, 10
NEG = -1e30
LANES = 128
VMEM_LIMIT = 48 * 2 ** 20
CB = 512
UA_CB, UB_CB, GA_CB, GC_CB = 9, 11, 13, 15


def _alibi_slope_list(n):
    def pow2(m):
        start = 2.0 ** (-8.0 / m)
        return [start ** (i + 1) for i in range(m)]
    if math.log2(n).is_integer():
        return pow2(n)
    c = 2 ** math.floor(math.log2(n))
    return pow2(c) + _alibi_slope_list(2 * c)[0::2][: n - c]


_SLOPES = np.asarray(sorted(_alibi_slope_list(len(GROUPS) * HEADS), reverse=True), dtype=np.float32).reshape(len(GROUPS), HEADS)


def _cp(sem=None, vmem=VMEM_LIMIT):
    return pltpu.CompilerParams(dimension_semantics=sem, vmem_limit_bytes=vmem)


def _sigmoid(x):
    return 1.0 / (1.0 + jnp.exp(-x))


HBM = pl.BlockSpec(memory_space=pl.ANY)


class _Exchange(NamedTuple):
    ins: list
    out_shape: list
    scratch: list
    aliases: dict
    start: Callable
    wait: Callable


def _launch(body, *, name, grid, in_specs, out_specs, out_shape, args, scratch_shapes=(), sem=None, comm=None):
    if comm is None:
        return pl.pallas_call(body, grid=grid, in_specs=in_specs, out_specs=out_specs, out_shape=out_shape,
                              scratch_shapes=list(scratch_shapes), name=name, compiler_params=_cp(sem))(*args)
    multi = isinstance(out_shape, (tuple, list))
    m_out = list(out_shape) if multi else [out_shape]
    m_ospec = list(out_specs) if multi else [out_specs]
    n_in, n_out, n_scr = len(in_specs), len(m_out), len(scratch_shapes)
    nc_in, nc_out = len(comm.ins), len(comm.out_shape)

    def hosted(*refs):
        bounds = np.cumsum([0, n_in, nc_in, n_out, nc_out, n_scr])
        mi, ci, mo, co, ms = (refs[a:b] for a, b in zip(bounds[:-1], bounds[1:]))
        cs = refs[bounds[-1]:]
        ids = [pl.program_id(a) for a in range(len(grid))]
        first = functools.reduce(jnp.logical_and, [i == 0 for i in ids])
        last = functools.reduce(jnp.logical_and, [i == g - 1 for i, g in zip(ids, grid)])

        @pl.when(first)
        def _():
            comm.start(ci, co, cs)

        body(*mi, *mo, *ms)

        @pl.when(last)
        def _():
            comm.wait(ci, co, cs)

    res = pl.pallas_call(
        hosted, grid=grid, in_specs=list(in_specs) + [HBM] * nc_in, out_specs=m_ospec + [HBM] * nc_out,
        out_shape=tuple(m_out) + tuple(comm.out_shape), scratch_shapes=list(scratch_shapes) + list(comm.scratch),
        input_output_aliases={n_in + i: n_out + o for i, o in comm.aliases.items()}, name=name + "_comm",
        compiler_params=pltpu.CompilerParams(dimension_semantics=("arbitrary",) * len(grid), vmem_limit_bytes=VMEM_LIMIT,
                                             has_side_effects=True))(*args, *comm.ins)
    return (tuple(res[:n_out]) if multi else res[0]), tuple(res[n_out:])


def _run_exchange(ex, name):
    n_in = len(ex.ins)

    def body(*refs):
        ins, outs, sems = refs[:n_in], refs[n_in:n_in + len(ex.out_shape)], refs[n_in + len(ex.out_shape):]
        ex.start(ins, outs, sems)
        ex.wait(ins, outs, sems)

    return pl.pallas_call(body, in_specs=[HBM] * n_in, out_specs=[HBM] * len(ex.out_shape), out_shape=tuple(ex.out_shape),
                          scratch_shapes=list(ex.scratch), input_output_aliases=dict(ex.aliases), name=name,
                          compiler_params=pltpu.CompilerParams(has_side_effects=True))(*ex.ins)


def _rms_fwd(x, g, name):
    T = x.shape[0]
    tm = 512

    def body(x_ref, g_ref, o_ref):
        xv = x_ref[...]
        r = lax.rsqrt(jnp.mean(xv * xv, axis=-1, keepdims=True) + RMS_EPS)
        o_ref[...] = (xv * r * g_ref[...]).astype(o_ref.dtype)

    row = pl.BlockSpec((tm, D_MODEL), lambda i: (i, 0))
    vec = pl.BlockSpec((1, D_MODEL), lambda i: (0, 0))
    return pl.pallas_call(body, grid=(T // tm,), in_specs=[row, vec], out_specs=row,
                          out_shape=SDS((T, D_MODEL), bf16), name=name, compiler_params=_cp(("parallel",)))(x, g)


def _rms_bwd(dy, x, g, dres, name):
    T = x.shape[0]
    tm = 512

    def body(dy_ref, x_ref, g_ref, r_ref, dx_ref, dg_ref):
        xv = x_ref[...]
        r = lax.rsqrt(jnp.mean(xv * xv, axis=-1, keepdims=True) + RMS_EPS)
        xh = xv * r
        dyv = dy_ref[...]
        dxh = dyv * g_ref[...]
        dx_ref[...] = r_ref[...] + r * (dxh - xh * jnp.mean(dxh * xh, axis=-1, keepdims=True))
        part = jnp.sum(dyv * xh, axis=0, keepdims=True)

        @pl.when(pl.program_id(0) == 0)
        def _():
            dg_ref[...] = part

        @pl.when(pl.program_id(0) > 0)
        def _():
            dg_ref[...] += part

    row = pl.BlockSpec((tm, D_MODEL), lambda i: (i, 0))
    vec = pl.BlockSpec((1, D_MODEL), lambda i: (0, 0))
    return pl.pallas_call(body, grid=(T // tm,), in_specs=[row, row, vec, row], out_specs=[row, vec],
                          out_shape=(SDS((T, D_MODEL), f32), SDS((1, D_MODEL), f32)), name=name,
                          compiler_params=_cp(("arbitrary",)))(dy, x, g, dres)


def _final_loss(x2, gf, target):
    T = x2.shape[0]
    tm = 512

    def body(x_ref, g_ref, t_ref, loss_ref, dx_ref, dg_ref):
        xv = x_ref[...]
        gv = g_ref[...]
        r = lax.rsqrt(jnp.mean(xv * xv, axis=-1, keepdims=True) + RMS_EPS)
        xh = xv * r
        e = xh * gv - t_ref[...]
        part_l = jnp.broadcast_to(0.5 * jnp.sum(jnp.mean(e * e, axis=-1, keepdims=True), axis=0, keepdims=True), (1, LANES))
        dy = e * (1.0 / D_MODEL)
        dxh = dy * gv
        dx_ref[...] = r * (dxh - xh * jnp.mean(dxh * xh, axis=-1, keepdims=True))
        part_g = jnp.sum(dy * xh, axis=0, keepdims=True)

        @pl.when(pl.program_id(0) == 0)
        def _():
            loss_ref[...] = part_l
            dg_ref[...] = part_g

        @pl.when(pl.program_id(0) > 0)
        def _():
            loss_ref[...] += part_l
            dg_ref[...] += part_g

    row = pl.BlockSpec((tm, D_MODEL), lambda i: (i, 0))
    vec = pl.BlockSpec((1, D_MODEL), lambda i: (0, 0))
    lvec = pl.BlockSpec((1, LANES), lambda i: (0, 0))
    return pl.pallas_call(body, grid=(T // tm,), in_specs=[row, vec, row], out_specs=[lvec, row, vec],
                          out_shape=(SDS((1, LANES), f32), SDS((T, D_MODEL), f32), SDS((1, D_MODEL), f32)),
                          name="final_loss", compiler_params=_cp(("arbitrary",)))(x2, gf, target)


def _ln_silu_fwd(c1, g, b):
    T = c1.shape[0]
    tm = 512

    def body(c_ref, g_ref, b_ref, o_ref):
        cv = c_ref[...]
        mu = jnp.mean(cv, axis=-1, keepdims=True)
        cc = cv - mu
        var = jnp.mean(cc * cc, axis=-1, keepdims=True)
        c2 = cc * lax.rsqrt(var + LN_EPS) * g_ref[...] + b_ref[...]
        o_ref[...] = (c2 * _sigmoid(c2)).astype(o_ref.dtype)

    row = pl.BlockSpec((tm, D_MODEL), lambda i: (i, 0))
    vec = pl.BlockSpec((1, D_MODEL), lambda i: (0, 0))
    return pl.pallas_call(body, grid=(T // tm,), in_specs=[row, vec, vec], out_specs=row,
                          out_shape=SDS((T, D_MODEL), bf16), name="ln_silu_fwd", compiler_params=_cp(("parallel",)))(c1, g, b)


def _ln_silu_bwd(dc3, c1, g, b):
    T = c1.shape[0]
    tm = 512

    def body(d_ref, c_ref, g_ref, b_ref, dc_ref, dg_ref, db_ref):
        cv = c_ref[...]
        gv = g_ref[...]
        mu = jnp.mean(cv, axis=-1, keepdims=True)
        cc = cv - mu
        var = jnp.mean(cc * cc, axis=-1, keepdims=True)
        rs = lax.rsqrt(var + LN_EPS)
        xh = cc * rs
        c2 = xh * gv + b_ref[...]
        sg = _sigmoid(c2)
        dc2 = d_ref[...].astype(f32) * (sg * (1.0 + c2 * (1.0 - sg)))
        dxh = dc2 * gv
        dc_ref[...] = rs * (dxh - jnp.mean(dxh, axis=-1, keepdims=True) - xh * jnp.mean(dxh * xh, axis=-1, keepdims=True))
        pg = jnp.sum(dc2 * xh, axis=0, keepdims=True)
        pb = jnp.sum(dc2, axis=0, keepdims=True)

        @pl.when(pl.program_id(0) == 0)
        def _():
            dg_ref[...] = pg
            db_ref[...] = pb

        @pl.when(pl.program_id(0) > 0)
        def _():
            dg_ref[...] += pg
            db_ref[...] += pb

    row = pl.BlockSpec((tm, D_MODEL), lambda i: (i, 0))
    vec = pl.BlockSpec((1, D_MODEL), lambda i: (0, 0))
    return pl.pallas_call(body, grid=(T // tm,), in_specs=[row, row, vec, vec], out_specs=[row, vec, vec],
                          out_shape=(SDS((T, D_MODEL), f32), SDS((1, D_MODEL), f32), SDS((1, D_MODEL), f32)),
                          name="ln_silu_bwd", compiler_params=_cp(("arbitrary",)))(dc3, c1, g, b)


NN = (((1,), (0,)), ((), ()))
NT = (((1,), (1,)), ((), ()))
TN = (((0,), (0,)), ((), ()))


def _mm(name, a, b, *, grid, a_spec, b_spec, o_spec, o_shape, o_dtype, dims, acc_shape=None, k_axis=None,
        res=None, res_spec=None, sem=None, comm=None):
    nk = 1 if k_axis is None else grid[k_axis]

    def body(*refs):
        if res is None:
            a_ref, b_ref, o_ref = refs[:3]
            r_ref, scr = None, refs[3:]
        else:
            a_ref, b_ref, r_ref, o_ref = refs[:4]
            scr = refs[4:]
        p = lax.dot_general(a_ref[...].astype(bf16), b_ref[...].astype(bf16), dims, preferred_element_type=f32)
        if nk == 1:
            if r_ref is not None:
                p = p + r_ref[...]
            o_ref[...] = p.astype(o_dtype)
            return
        acc = scr[0]
        k = pl.program_id(k_axis)

        @pl.when(k == 0)
        def _():
            acc[...] = p

        @pl.when(k > 0)
        def _():
            acc[...] += p

        @pl.when(k == nk - 1)
        def _():
            t = acc[...]
            if r_ref is not None:
                t = t + r_ref[...]
            o_ref[...] = t.astype(o_dtype)

    ins = [a, b] + ([] if res is None else [res])
    specs = [a_spec, b_spec] + ([] if res is None else [res_spec])
    scratch = [] if nk == 1 else [pltpu.VMEM(acc_shape, f32)]
    return _launch(body, name=name, grid=grid, in_specs=specs, out_specs=o_spec, out_shape=SDS(o_shape, o_dtype),
                   args=ins, scratch_shapes=scratch, sem=sem, comm=comm)


def _mm_nn_full(name, a, b, o_dtype, res=None, tm=1024, comm=None):
    T, K = a.shape
    N = b.shape[1]
    return _mm(name, a, b, grid=(T // tm,), a_spec=pl.BlockSpec((tm, K), lambda i: (i, 0)),
               b_spec=pl.BlockSpec((K, N), lambda i: (0, 0)), o_spec=pl.BlockSpec((tm, N), lambda i: (i, 0)),
               o_shape=(T, N), o_dtype=o_dtype, dims=NN, res=res,
               res_spec=None if res is None else pl.BlockSpec((tm, N), lambda i: (i, 0)), sem=("parallel",), comm=comm)


def _mm_nt_full(name, a, b, o_dtype, tm=1024):
    T, N = a.shape
    K = b.shape[0]
    return _mm(name, a, b, grid=(T // tm,), a_spec=pl.BlockSpec((tm, N), lambda i: (i, 0)),
               b_spec=pl.BlockSpec((K, N), lambda i: (0, 0)), o_spec=pl.BlockSpec((tm, K), lambda i: (i, 0)),
               o_shape=(T, K), o_dtype=o_dtype, dims=NT, sem=("parallel",))


def _mm_tn_tokens(name, a, b, o_dtype, tk=1024):
    T, K = a.shape
    N = b.shape[1]
    return _mm(name, a, b, grid=(T // tk,), a_spec=pl.BlockSpec((tk, K), lambda k: (k, 0)),
               b_spec=pl.BlockSpec((tk, N), lambda k: (k, 0)), o_spec=pl.BlockSpec((K, N), lambda k: (0, 0)),
               o_shape=(K, N), o_dtype=o_dtype, dims=TN, acc_shape=(K, N), k_axis=0, sem=("arbitrary",))


def _ffn_up(h2, wg, wu):
    T = h2.shape[0]
    tm = 1024

    def body(h_ref, wg_ref, wu_ref, a_ref, b_ref, f_ref):
        hv = h_ref[...]
        av = lax.dot_general(hv, wg_ref[...], NT, preferred_element_type=f32)
        bv = lax.dot_general(hv, wu_ref[...], NT, preferred_element_type=f32)
        a_ref[...] = av.astype(bf16)
        b_ref[...] = bv.astype(bf16)
        f_ref[...] = (av * _sigmoid(av) * bv).astype(bf16)

    wspec = pl.BlockSpec((None, FF_S, D_MODEL), lambda s, m: (s, 0, 0))
    ospec = pl.BlockSpec((None, tm, FF_S), lambda s, m: (s, m, 0))
    osd = SDS((N_CHIPS, T, FF_S), bf16)
    return pl.pallas_call(body, grid=(N_CHIPS, T // tm),
                          in_specs=[pl.BlockSpec((tm, D_MODEL), lambda s, m: (m, 0)), wspec, wspec],
                          out_specs=[ospec, ospec, ospec], out_shape=(osd, osd, osd), name="ffn_up",
                          compiler_params=_cp(("parallel", "parallel")))(h2, wg, wu)


def _ffn_down_bwd(dx2, wd, a, b):
    T = dx2.shape[0]
    tm = 1024

    def body(d_ref, w_ref, a_ref, b_ref, da_ref, db_ref):
        df = lax.dot_general(d_ref[...].astype(bf16), w_ref[...], NT, preferred_element_type=f32)
        av = a_ref[...].astype(f32)
        sg = _sigmoid(av)
        da_ref[...] = (df * b_ref[...].astype(f32) * (sg * (1.0 + av * (1.0 - sg)))).astype(bf16)
        db_ref[...] = (df * av * sg).astype(bf16)

    aspec = pl.BlockSpec((None, tm, FF_S), lambda m, s: (s, m, 0))
    osd = SDS((N_CHIPS, T, FF_S), bf16)
    return pl.pallas_call(body, grid=(T // tm, N_CHIPS),
                          in_specs=[pl.BlockSpec((tm, D_MODEL), lambda m, s: (m, 0)),
                                    pl.BlockSpec((None, FF_S, D_MODEL), lambda m, s: (s, 0, 0)), aspec, aspec],
                          out_specs=[aspec, aspec], out_shape=(osd, osd), name="ffn_down_bwd",
                          compiler_params=_cp(("parallel", "parallel")))(dx2, wd, a, b)


def _ffn_dh2(da, wg, db, wu):
    T = da.shape[1]
    tm = 1024

    def body(da_ref, wg_ref, db_ref, wu_ref, o_ref, acc):
        p = jnp.dot(da_ref[...], wg_ref[...], preferred_element_type=f32)
        p = p + jnp.dot(db_ref[...], wu_ref[...], preferred_element_type=f32)
        s = pl.program_id(1)

        @pl.when(s == 0)
        def _():
            acc[...] = p

        @pl.when(s > 0)
        def _():
            acc[...] += p

        @pl.when(s == N_CHIPS - 1)
        def _():
            o_ref[...] = acc[...]

    aspec = pl.BlockSpec((None, tm, FF_S), lambda m, s: (s, m, 0))
    wspec = pl.BlockSpec((None, FF_S, D_MODEL), lambda m, s: (s, 0, 0))
    return pl.pallas_call(body, grid=(T // tm, N_CHIPS), in_specs=[aspec, wspec, aspec, wspec],
                          out_specs=pl.BlockSpec((tm, D_MODEL), lambda m, s: (m, 0)), out_shape=SDS((T, D_MODEL), f32),
                          scratch_shapes=[pltpu.VMEM((tm, D_MODEL), f32)], name="ffn_dh2",
                          compiler_params=_cp(("parallel", "arbitrary")))(da, wg, db, wu)


def _merge_fwd(proj, gate_b, ya, yc):
    T = proj.shape[0]
    tm = 512

    def body(ga_ref, gc_ref, ba_ref, bc_ref, ya_ref, yc_ref, o_ref):
        sa = _sigmoid(ga_ref[...].astype(f32) + ba_ref[...])
        sc = _sigmoid(gc_ref[...].astype(f32) + bc_ref[...])
        o_ref[...] = (sa * ya_ref[...].astype(f32) + sc * yc_ref[...].astype(f32)).astype(bf16)

    blk = pl.BlockSpec((tm, CB), lambda i, j: (i, j))
    return pl.pallas_call(
        body, grid=(T // tm, 2),
        in_specs=[pl.BlockSpec((tm, CB), lambda i, j: (i, GA_CB + j)), pl.BlockSpec((tm, CB), lambda i, j: (i, GC_CB + j)),
                  pl.BlockSpec((1, CB), lambda i, j: (0, j)), pl.BlockSpec((1, CB), lambda i, j: (0, 2 + j)), blk, blk],
        out_specs=blk, out_shape=SDS((T, D_MODEL), bf16), name="merge_fwd",
        compiler_params=_cp(("parallel", "parallel")))(proj, proj, gate_b, gate_b, ya, yc)


def _merge_bwd(dm, proj, gate_b, ya, yc):
    T = proj.shape[0]
    tm = 512

    def body(dm_ref, ga_ref, gc_ref, ba_ref, bc_ref, ya_ref, yc_ref, dya_ref, dyc_ref, dga_ref, dgc_ref, dba_ref, dbc_ref):
        dmv = dm_ref[...].astype(f32)
        sa = _sigmoid(ga_ref[...].astype(f32) + ba_ref[...])
        sc = _sigmoid(gc_ref[...].astype(f32) + bc_ref[...])
        dya_ref[...] = (dmv * sa).astype(bf16)
        dyc_ref[...] = (dmv * sc).astype(bf16)
        dga = dmv * ya_ref[...].astype(f32) * sa * (1.0 - sa)
        dgc = dmv * yc_ref[...].astype(f32) * sc * (1.0 - sc)
        dga_ref[...] = dga.astype(bf16)
        dgc_ref[...] = dgc.astype(bf16)
        pa = jnp.sum(dga, axis=0, keepdims=True)
        pc = jnp.sum(dgc, axis=0, keepdims=True)

        @pl.when(pl.program_id(1) == 0)
        def _():
            dba_ref[...] = pa
            dbc_ref[...] = pc

        @pl.when(pl.program_id(1) > 0)
        def _():
            dba_ref[...] += pa
            dbc_ref[...] += pc

    blk = pl.BlockSpec((tm, CB), lambda j, i: (i, j))
    vec = pl.BlockSpec((1, CB), lambda j, i: (0, j))
    big = SDS((T, D_MODEL), bf16)
    small = SDS((1, D_MODEL), f32)
    return pl.pallas_call(
        body, grid=(2, T // tm),
        in_specs=[blk, pl.BlockSpec((tm, CB), lambda j, i: (i, GA_CB + j)), pl.BlockSpec((tm, CB), lambda j, i: (i, GC_CB + j)),
                  vec, pl.BlockSpec((1, CB), lambda j, i: (0, 2 + j)), blk, blk],
        out_specs=[blk, blk, blk, blk, vec, vec], out_shape=(big, big, big, big, small, small), name="merge_bwd",
        compiler_params=_cp(("parallel", "arbitrary")))(dm, proj, proj, gate_b, gate_b, ya, yc)


CONV_TS = 256
CONV_HALO = 32
CONV_RC = 64
CONV_WIN = CONV_TS + CONV_HALO
SUBLANES = 8


def _fill_shifted(win, sh):
    for b in range(1, SUBLANES):
        sh[b - 1] = win[pl.ds(b, CONV_WIN - SUBLANES), :]


def _rows_at(win, sh, row):
    a, b = divmod(row, SUBLANES)
    if b == 0:
        return win[pl.ds(row, CONV_RC), :]
    return sh[b - 1, pl.ds(a * SUBLANES, CONV_RC), :]


def _glu_conv_fwd(proj3, w, bias, comm=None):
    B = proj3.shape[0]
    nt = SEQ // CONV_TS
    hb = CONV_TS // CONV_HALO

    def body(ua_ref, ub_ref, ha_ref, hb_ref, w_ref, b_ref, o_ref, win, sh):
        i = pl.program_id(2)
        c0 = ua_ref[...].astype(f32) * _sigmoid(ub_ref[...].astype(f32))
        halo = ha_ref[...].astype(f32) * _sigmoid(hb_ref[...].astype(f32))
        win[0:CONV_HALO, :] = jnp.where(i > 0, halo, 0.0)
        win[CONV_HALO:, :] = c0
        _fill_shifted(win, sh)
        for r0 in range(0, CONV_TS, CONV_RC):
            acc = jnp.zeros((CONV_RC, CB), f32) + b_ref[...]
            for k in range(CONV_K):
                acc = acc + _rows_at(win, sh, r0 + CONV_HALO - (CONV_K - 1) + k) * w_ref[k:k + 1, :]
            o_ref[r0:r0 + CONV_RC, :] = acc

    def cur(cb):
        return pl.BlockSpec((None, CONV_TS, CB), lambda b, j, i: (b, i, cb + j))

    def prev(cb):
        return pl.BlockSpec((None, CONV_HALO, CB), lambda b, j, i: (b, jnp.maximum(i * hb - 1, 0), cb + j))

    return _launch(
        body, name="glu_conv_fwd", grid=(B, 2, nt),
        in_specs=[cur(UA_CB), cur(UB_CB), prev(UA_CB), prev(UB_CB),
                  pl.BlockSpec((CONV_K, CB), lambda b, j, i: (0, j)), pl.BlockSpec((1, CB), lambda b, j, i: (0, j))],
        out_specs=pl.BlockSpec((None, CONV_TS, CB), lambda b, j, i: (b, i, j)),
        out_shape=SDS((B, SEQ, D_MODEL), f32), args=(proj3, proj3, proj3, proj3, w, bias),
        scratch_shapes=[pltpu.VMEM((CONV_WIN, CB), f32), pltpu.VMEM((SUBLANES - 1, CONV_WIN - SUBLANES, CB), f32)],
        sem=("parallel", "parallel", "parallel"), comm=comm)


def _glu_conv_bwd(dc1, proj3, w, comm=None):
    B = proj3.shape[0]
    nt = SEQ // CONV_TS
    hb = CONV_TS // CONV_HALO

    def body(d_ref, dn_ref, ua_ref, ub_ref, ha_ref, hb_ref, w_ref, dua_ref, dub_ref, dw_ref, db_ref, winc, wind, accw, shc, shd):
        b = pl.program_id(1)
        i = pl.program_id(2)
        first = jnp.logical_and(b == 0, i == 0)
        last = jnp.logical_and(b == B - 1, i == nt - 1)

        @pl.when(first)
        def _():
            accw[...] = jnp.zeros_like(accw)
            db_ref[...] = jnp.zeros_like(db_ref)

        halo = ha_ref[...].astype(f32) * _sigmoid(hb_ref[...].astype(f32))
        winc[0:CONV_HALO, :] = jnp.where(i > 0, halo, 0.0)
        winc[CONV_HALO:, :] = ua_ref[...].astype(f32) * _sigmoid(ub_ref[...].astype(f32))
        wind[0:CONV_TS, :] = d_ref[...]
        wind[CONV_TS:, :] = jnp.where(i < nt - 1, dn_ref[...], 0.0)
        db_ref[...] += jnp.sum(d_ref[...], axis=0, keepdims=True)
        _fill_shifted(winc, shc)
        _fill_shifted(wind, shd)
        for r0 in range(0, CONV_TS, CONV_RC):
            dc0 = jnp.zeros((CONV_RC, CB), f32)
            for k in range(CONV_K):
                dc0 = dc0 + _rows_at(wind, shd, r0 + (CONV_K - 1) - k) * w_ref[k:k + 1, :]
            uav = ua_ref[r0:r0 + CONV_RC, :].astype(f32)
            sg = _sigmoid(ub_ref[r0:r0 + CONV_RC, :].astype(f32))
            dua_ref[r0:r0 + CONV_RC, :] = (dc0 * sg).astype(bf16)
            dub_ref[r0:r0 + CONV_RC, :] = (dc0 * uav * sg * (1.0 - sg)).astype(bf16)
            dv = wind[r0:r0 + CONV_RC, :]
            for k in range(CONV_K):
                prod = dv * _rows_at(winc, shc, r0 + CONV_HALO - (CONV_K - 1) + k)
                accw[k] += jnp.sum(prod.reshape(CONV_RC // 8, 8, CB), axis=0)

        @pl.when(last)
        def _():
            for k in range(CONV_K):
                dw_ref[k:k + 1, :] = jnp.sum(accw[k], axis=0, keepdims=True)
            dw_ref[CONV_K:, :] = jnp.zeros((CONV_HALO - CONV_K, CB), f32)

    def cur(cb):
        return pl.BlockSpec((None, CONV_TS, CB), lambda j, b, i: (b, i, cb + j))

    def prev(cb):
        return pl.BlockSpec((None, CONV_HALO, CB), lambda j, b, i: (b, jnp.maximum(i * hb - 1, 0), cb + j))

    nxt = pl.BlockSpec((None, CONV_HALO, CB), lambda j, b, i: (b, jnp.minimum((i + 1) * hb, SEQ // CONV_HALO - 1), j))
    big = SDS((B, SEQ, D_MODEL), bf16)
    return _launch(
        body, name="glu_conv_bwd", grid=(2, B, nt),
        in_specs=[cur(0), nxt, cur(UA_CB), cur(UB_CB), prev(UA_CB), prev(UB_CB), pl.BlockSpec((CONV_K, CB), lambda j, b, i: (0, j))],
        out_specs=[cur(0), cur(0), pl.BlockSpec((CONV_HALO, CB), lambda j, b, i: (0, j)), pl.BlockSpec((1, CB), lambda j, b, i: (0, j))],
        out_shape=(big, big, SDS((CONV_HALO, D_MODEL), f32), SDS((1, D_MODEL), f32)),
        args=(dc1, dc1, proj3, proj3, proj3, proj3, w),
        scratch_shapes=[pltpu.VMEM((CONV_WIN, CB), f32), pltpu.VMEM((CONV_WIN, CB), f32), pltpu.VMEM((CONV_K, SUBLANES, CB), f32),
                        pltpu.VMEM((SUBLANES - 1, CONV_WIN - SUBLANES, CB), f32),
                        pltpu.VMEM((SUBLANES - 1, CONV_WIN - SUBLANES, CB), f32)],
        sem=("parallel", "arbitrary", "arbitrary"), comm=comm)


def _band(first, dil):
    kw = Q_BLOCK if first else 2 * Q_BLOCK
    qi = lax.broadcasted_iota(jnp.int32, (Q_BLOCK, kw), 0)
    kj = lax.broadcasted_iota(jnp.int32, (Q_BLOCK, kw), 1)
    rel = qi - kj + (0 if first else Q_BLOCK)
    valid = jnp.logical_and(rel >= 0, rel <= Q_BLOCK)
    return valid, rel.astype(f32) * float(dil)


def _bias(first, dil, slope):
    valid, dist = _band(first, dil)
    return jnp.where(valid, -slope * dist, NEG)


def _scores(q, k, bias):
    return lax.dot_general(q, k, NT, preferred_element_type=f32) * (HEAD_DIM ** -0.5) + bias


def _pair_cols(hp):
    return slice(hp * LANES, (hp + 1) * LANES)


def _half(x2, e):
    lane = lax.broadcasted_iota(jnp.int32, (1, LANES), 1)
    keep = (lane < HEAD_DIM) if e == 0 else (lane >= HEAD_DIM)
    return jnp.where(keep, x2, jnp.zeros_like(x2))


def _attn_fwd(g, q4, k4, v4, qcb, kcb, vcb, comm=None):
    _, dil = GROUPS[g]
    B, r, L, _ = q4.shape
    nb = L // Q_BLOCK
    slopes = [float(s) for s in _SLOPES[g]]

    def body(q_ref, k_ref, v_ref, o_ref, lse_ref, bias, s_scr, p_scr):
        lane = lax.broadcasted_iota(jnp.int32, (Q_BLOCK, LANES), 1)
        if nb > 1:
            for h in range(HEADS):
                bias[h] = _bias(False, dil, slopes[h])

        def block(n, first):
            q0 = 0 if first else pl.multiple_of(n * Q_BLOCK, Q_BLOCK)
            k0 = 0 if first else pl.multiple_of((n - 1) * Q_BLOCK, Q_BLOCK)
            kw = Q_BLOCK if first else 2 * Q_BLOCK
            for hp in range(HEADS // 2):
                q2 = q_ref[pl.ds(q0, Q_BLOCK), _pair_cols(hp)]
                k2 = k_ref[pl.ds(k0, kw), _pair_cols(hp)]
                for e in range(2):
                    h = 2 * hp + e
                    b_h = _bias(True, dil, slopes[h]) if first else bias[h]
                    s_scr[h, :, :kw] = _scores(_half(q2, e), k2, b_h)
            st = jnp.zeros((Q_BLOCK, LANES), f32)
            dens = jnp.ones((Q_BLOCK, LANES), f32)
            for h in range(HEADS):
                s = s_scr[h, :, :kw]
                m = jnp.max(s, axis=-1, keepdims=True)
                p = jnp.exp(s - m)
                den = jnp.sum(p, axis=-1, keepdims=True)
                p_scr[h, :, :kw] = p.astype(bf16)
                st = jnp.where(lane == h, m + jnp.log(den), st)
                dens = jnp.where(lane == h, den, dens)
            lse_ref[pl.ds(q0, Q_BLOCK), :] = st
            inv = 1.0 / dens
            for hp in range(HEADS // 2):
                v2 = v_ref[pl.ds(k0, kw), _pair_cols(hp)]
                o2 = sum(jnp.dot(p_scr[2 * hp + e, :, :kw], _half(v2, e), preferred_element_type=f32) * inv[:, 2 * hp + e:2 * hp + e + 1]
                         for e in range(2))
                o_ref[pl.ds(q0, Q_BLOCK), _pair_cols(hp)] = o2.astype(bf16)

        block(0, True)
        if nb > 1:
            def step(n, carry):
                block(n, False)
                return carry
            lax.fori_loop(1, nb, step, 0)

    def spec(cb):
        return pl.BlockSpec((None, None, L, GW), lambda b, c: (b, c, 0, cb))

    return _launch(
        body, name=f"attn_fwd_g{g}", grid=(B, r), in_specs=[spec(qcb), spec(kcb), spec(vcb)],
        out_specs=[spec(0), pl.BlockSpec((None, None, L, LANES), lambda b, c: (b, c, 0, 0))],
        out_shape=(SDS((B, r, L, GW), bf16), SDS((B, r, L, LANES), f32)), args=(q4, k4, v4),
        scratch_shapes=[pltpu.VMEM((HEADS, Q_BLOCK, 2 * Q_BLOCK), f32), pltpu.VMEM((HEADS, Q_BLOCK, 2 * Q_BLOCK), f32),
                        pltpu.VMEM((HEADS, Q_BLOCK, 2 * Q_BLOCK), bf16)],
        sem=("parallel", "parallel"), comm=comm)


def _attn_bwd(g, q4, k4, v4, qcb, kcb, vcb, do4, lse4, dl4, comm=None):
    _, dil = GROUPS[g]
    B, r, L, _ = q4.shape
    nb = L // Q_BLOCK
    slopes = [float(s) for s in _SLOPES[g]]
    scale = HEAD_DIM ** -0.5

    def body(q_ref, k_ref, v_ref, do_ref, lse_ref, dl_ref, dq_ref, dk_ref, dv_ref, dk_acc, dv_acc, bias, s_scr, dp_scr, p_scr, ds_scr):
        dk_acc[...] = jnp.zeros_like(dk_acc)
        dv_acc[...] = jnp.zeros_like(dv_acc)
        if nb > 1:
            for h in range(HEADS):
                bias[h] = _bias(False, dil, slopes[h])

        def block(n, first):
            q0 = 0 if first else pl.multiple_of(n * Q_BLOCK, Q_BLOCK)
            k0 = 0 if first else pl.multiple_of((n - 1) * Q_BLOCK, Q_BLOCK)
            kw = Q_BLOCK if first else 2 * Q_BLOCK
            for hp in range(HEADS // 2):
                q2 = q_ref[pl.ds(q0, Q_BLOCK), _pair_cols(hp)]
                k2 = k_ref[pl.ds(k0, kw), _pair_cols(hp)]
                v2 = v_ref[pl.ds(k0, kw), _pair_cols(hp)]
                do2 = do_ref[pl.ds(q0, Q_BLOCK), _pair_cols(hp)]
                for e in range(2):
                    h = 2 * hp + e
                    b_h = _bias(True, dil, slopes[h]) if first else bias[h]
                    s_scr[h, :, :kw] = _scores(_half(q2, e), k2, b_h)
                    dp_scr[h, :, :kw] = lax.dot_general(_half(do2, e), v2, NT, preferred_element_type=f32)
            for h in range(HEADS):
                p = jnp.exp(s_scr[h, :, :kw] - lse_ref[pl.ds(q0, Q_BLOCK), h:h + 1])
                p_scr[h, :, :kw] = p.astype(bf16)
                ds_scr[h, :, :kw] = (p * (dp_scr[h, :, :kw] - dl_ref[pl.ds(q0, Q_BLOCK), h:h + 1])).astype(bf16)
            for hp in range(HEADS // 2):
                cols = _pair_cols(hp)
                q2 = q_ref[pl.ds(q0, Q_BLOCK), cols]
                k2 = k_ref[pl.ds(k0, kw), cols]
                do2 = do_ref[pl.ds(q0, Q_BLOCK), cols]
                ds = [ds_scr[2 * hp + e, :, :kw] for e in range(2)]
                dq2 = sum(jnp.dot(ds[e], _half(k2, e), preferred_element_type=f32) for e in range(2))
                dq_ref[pl.ds(q0, Q_BLOCK), cols] = (dq2 * scale).astype(bf16)
                dk2 = sum(lax.dot_general(ds[e], _half(q2, e), TN, preferred_element_type=f32) for e in range(2))
                dk_acc[pl.ds(k0, kw), cols] += dk2 * scale
                dv2 = sum(lax.dot_general(p_scr[2 * hp + e, :, :kw], _half(do2, e), TN, preferred_element_type=f32) for e in range(2))
                dv_acc[pl.ds(k0, kw), cols] += dv2

        block(0, True)
        if nb > 1:
            def step(n, carry):
                block(n, False)
                return carry
            lax.fori_loop(1, nb, step, 0)
        dk_ref[...] = dk_acc[...].astype(bf16)
        dv_ref[...] = dv_acc[...].astype(bf16)

    def spec(cb):
        return pl.BlockSpec((None, None, L, GW), lambda b, c: (b, c, 0, cb))

    st = pl.BlockSpec((None, None, L, LANES), lambda b, c: (b, c, 0, 0))
    osd = SDS((B, r, L, GW), bf16)
    return _launch(
        body, name=f"attn_bwd_g{g}", grid=(B, r), in_specs=[spec(qcb), spec(kcb), spec(vcb), spec(0), st, st],
        out_specs=[spec(0), spec(0), spec(0)], out_shape=(osd, osd, osd), args=(q4, k4, v4, do4, lse4, dl4),
        scratch_shapes=[pltpu.VMEM((L, GW), f32), pltpu.VMEM((L, GW), f32)]
        + [pltpu.VMEM((HEADS, Q_BLOCK, 2 * Q_BLOCK), f32)] * 3 + [pltpu.VMEM((HEADS, Q_BLOCK, 2 * Q_BLOCK), bf16)] * 2,
        sem=("parallel", "parallel"), comm=comm)


RT = 512
RCH = GW // LANES
DILS = tuple(d for _, d in GROUPS[1:])


def _res_spec(r, width):
    return pl.BlockSpec((None, r, RT // r, width), lambda b, i, *_: (b, 0, i, 0))


def _tok_spec(width, cb=0):
    return pl.BlockSpec((None, RT, width), lambda b, i, *_: (b, i, cb))


def _to_residues(res_ref, scr, r, width):
    for c in range(r):
        for jj in range(width // LANES):
            res_ref[c, :, jj * LANES:(jj + 1) * LANES] = scr[jj, pl.ds(c, RT // r, stride=r), :].astype(res_ref.dtype)


def _from_residues(scr, res_ref, r, width):
    for c in range(r):
        for jj in range(width // LANES):
            scr[jj, pl.ds(c, RT // r, stride=r), :] = res_ref[c, :, jj * LANES:(jj + 1) * LANES].astype(f32)


def _qkv_to_residues(proj3, g):
    r = GROUPS[g][1]
    B = proj3.shape[0]

    def body(q_ref, k_ref, v_ref, o_ref, scr):
        for p, x_ref in enumerate((q_ref, k_ref, v_ref)):
            x = x_ref[...].astype(f32)
            for jj in range(RCH):
                scr[p * RCH + jj] = x[:, jj * LANES:(jj + 1) * LANES]
        _to_residues(o_ref, scr, r, ATTN_W)

    return pl.pallas_call(
        body, grid=(B, SEQ // RT), in_specs=[_tok_spec(GW, 3 * p + g) for p in range(3)], out_specs=_res_spec(r, ATTN_W),
        out_shape=SDS((B, r, SEQ // r, ATTN_W), bf16), scratch_shapes=[pltpu.VMEM((3 * RCH, RT, LANES), f32)],
        name=f"qkv_to_residues_g{g}", compiler_params=_cp(("parallel", "parallel")))(proj3, proj3, proj3)


def _attn_mix(o0, l0, o_res, l_res):
    B = o0.shape[0]

    def body(o0_ref, l0_ref, o1_ref, o2_ref, l1_ref, l2_ref, y_ref, lt_ref, lt1_ref, lt2_ref, so, sl):
        for gi, (o_ref, l_ref, r) in enumerate(((o1_ref, l1_ref, DILS[0]), (o2_ref, l2_ref, DILS[1]))):
            _from_residues(so.at[gi], o_ref, r, GW)
            _from_residues(sl.at[gi:gi + 1], l_ref, r, LANES)
        ls = [l0_ref[...], sl[0], sl[1]]
        m = functools.reduce(jnp.maximum, ls)
        ws = [jnp.exp(l - m) for l in ls]
        den = ws[0] + ws[1] + ws[2]
        alphas = [w / den for w in ws]
        lt = m + jnp.log(den)
        lt_ref[...] = lt
        sl[2] = lt
        _to_residues(lt1_ref, sl.at[2:3], DILS[0], LANES)
        _to_residues(lt2_ref, sl.at[2:3], DILS[1], LANES)
        for h in range(HEADS):
            cols = slice(h * HEAD_DIM, (h + 1) * HEAD_DIM)
            jj, lo = divmod(h * HEAD_DIM, LANES)
            acc = alphas[0][:, h:h + 1] * o0_ref[:, cols].astype(f32)
            for gi in range(2):
                acc = acc + alphas[gi + 1][:, h:h + 1] * so[gi, jj, :, lo:lo + HEAD_DIM]
            y_ref[:, cols] = acc.astype(bf16)

    in_specs = [_tok_spec(GW), _tok_spec(LANES), _res_spec(DILS[0], GW), _res_spec(DILS[1], GW), _res_spec(DILS[0], LANES), _res_spec(DILS[1], LANES)]
    out_specs = [_tok_spec(GW), _tok_spec(LANES), _res_spec(DILS[0], LANES), _res_spec(DILS[1], LANES)]
    return pl.pallas_call(
        body, grid=(B, SEQ // RT), in_specs=in_specs, out_specs=out_specs,
        out_shape=(SDS((B, SEQ, GW), bf16), SDS((B, SEQ, LANES), f32)) + tuple(SDS((B, r, SEQ // r, LANES), f32) for r in DILS),
        scratch_shapes=[pltpu.VMEM((2, RCH, RT, LANES), f32), pltpu.VMEM((3, RT, LANES), f32)],
        name="attn_mix", compiler_params=_cp(("parallel", "parallel")))(o0, l0, *o_res, *l_res)


def _attn_delta(dmix, mix):
    B = dmix.shape[0]

    def body(d_ref, y_ref, dl_ref, dl1_ref, dl2_ref, dm1_ref, dm2_ref, sx, sd):
        lane = lax.broadcasted_iota(jnp.int32, (RT, LANES), 1)
        acc = jnp.zeros((RT, LANES), f32)
        dv = d_ref[...].astype(f32)
        for jj in range(RCH):
            sx[jj] = dv[:, jj * LANES:(jj + 1) * LANES]
        for h in range(HEADS):
            cols = slice(h * HEAD_DIM, (h + 1) * HEAD_DIM)
            dl = jnp.sum(dv[:, cols] * y_ref[:, cols].astype(f32), axis=-1, keepdims=True)
            acc = jnp.where(lane == h, dl, acc)
        dl_ref[...] = acc
        sd[0] = acc
        _to_residues(dl1_ref, sd, DILS[0], LANES)
        _to_residues(dl2_ref, sd, DILS[1], LANES)
        _to_residues(dm1_ref, sx, DILS[0], GW)
        _to_residues(dm2_ref, sx, DILS[1], GW)

    return pl.pallas_call(
        body, grid=(B, SEQ // RT), in_specs=[_tok_spec(GW), _tok_spec(GW)],
        out_specs=[_tok_spec(LANES), _res_spec(DILS[0], LANES), _res_spec(DILS[1], LANES), _res_spec(DILS[0], GW), _res_spec(DILS[1], GW)],
        out_shape=(SDS((B, SEQ, LANES), f32),) + tuple(SDS((B, r, SEQ // r, LANES), f32) for r in DILS)
        + tuple(SDS((B, r, SEQ // r, GW), bf16) for r in DILS),
        scratch_shapes=[pltpu.VMEM((RCH, RT, LANES), f32), pltpu.VMEM((1, RT, LANES), f32)],
        name="attn_delta", compiler_params=_cp(("parallel", "parallel")))(dmix, mix)


N_CB = IN_W // CB


def _assemble_dproj(dqkv, dua, dub, dga, dgc):
    B = dua.shape[0]
    ng = len(GROUPS)
    flat = [dqkv[g][p] for p in range(3) for g in range(ng)]
    wide = [dua, dub, dga, dgc]

    def body(*refs):
        srcs, wides, o_ref, scr = refs[:3 * ng], refs[3 * ng:3 * ng + 4], refs[3 * ng + 4], refs[3 * ng + 5]
        for jv in range(3 * ng):
            g = jv % ng
            if g == 0:
                o_ref[:, jv * GW:(jv + 1) * GW] = srcs[jv][...]
            else:
                _from_residues(scr, srcs[jv], GROUPS[g][1], GW)
                for jj in range(RCH):
                    o_ref[:, jv * GW + jj * LANES:jv * GW + (jj + 1) * LANES] = scr[jj].astype(bf16)
        for wv in range(4):
            lo = 3 * ATTN_W + wv * D_MODEL
            o_ref[:, lo:lo + D_MODEL] = wides[wv][...]

    in_specs = [_tok_spec(GW) if (jv % ng) == 0 else _res_spec(GROUPS[jv % ng][1], GW) for jv in range(3 * ng)]
    in_specs += [_tok_spec(D_MODEL)] * 4
    return pl.pallas_call(
        body, grid=(B, SEQ // RT), in_specs=in_specs, out_specs=_tok_spec(IN_W),
        out_shape=SDS((B, SEQ, IN_W), bf16), scratch_shapes=[pltpu.VMEM((RCH, RT, LANES), f32)],
        name="assemble_dproj", compiler_params=_cp(("parallel", "parallel")))(*flat, *wide)


def _add_pairs(part, got, core, name):
    n, h, C = got.shape

    def body(c_ref, a_ref, b_ref, o_ref):
        o_ref[...] = (a_ref[...].astype(f32) + b_ref[...].astype(f32)).astype(bf16)

    blk = pl.BlockSpec((None, h, C), lambda s, c_ref: (s, 0, 0))
    own = pl.BlockSpec((None, h, C), lambda s, c_ref: (s, c_ref[0], 0))
    spec = pltpu.PrefetchScalarGridSpec(num_scalar_prefetch=1, grid=(n,), in_specs=[own, blk], out_specs=blk)
    return pl.pallas_call(body, grid_spec=spec, out_shape=SDS((n, h, C), bf16), name=name,
                          compiler_params=_cp(("parallel",)))(core, part, got)


def _sum4(sums, got, chip, core, name):
    _, h, C = sums.shape

    def body(s_ref, c_ref, own_ref, q_ref, o_ref):
        t = own_ref[...].astype(f32) + q_ref[0].astype(f32)
        t = t + q_ref[1].astype(f32)
        o_ref[...] = t + q_ref[2].astype(f32)

    spec = pltpu.PrefetchScalarGridSpec(
        num_scalar_prefetch=2, grid=(1,),
        in_specs=[pl.BlockSpec((None, h, C), lambda i, s_ref, c_ref: (s_ref[0], 0, 0)),
                  pl.BlockSpec((N_CHIPS - 1, h, C), lambda i, s_ref, c_ref: (0, 0, 0))],
        out_specs=pl.BlockSpec((h, C), lambda i, s_ref, c_ref: (c_ref[0], 0)))
    return pl.pallas_call(body, grid_spec=spec, out_shape=SDS((2 * h, C), f32), name=name,
                          compiler_params=_cp(("arbitrary",)))(chip, core, sums, got)


def _adamw(w, g, m, v, name):
    R, C = w.shape
    rt = R
    for cand in (512, 256, 128, 64, 32, 16, 8):
        if R % cand == 0 and cand * C * 4 <= 2 ** 21:
            rt = cand
            break
    c1 = 1.0 / (1.0 - B1 ** STEP)
    c2 = 1.0 / (1.0 - B2 ** STEP)

    def body(w_ref, g_ref, m_ref, v_ref, d_ref, nm_ref, nv_ref):
        gv = g_ref[...]
        nm = B1 * m_ref[...] + (1.0 - B1) * gv
        nv = B2 * v_ref[...] + (1.0 - B2) * (gv * gv)
        nm_ref[...] = nm
        nv_ref[...] = nv
        d_ref[...] = -LR * ((nm * c1) / (jnp.sqrt(nv * c2) + ADAM_EPS) + WD * w_ref[...])

    blk = pl.BlockSpec((rt, C), lambda i: (i, 0))
    sd = SDS((R, C), f32)
    return pl.pallas_call(body, grid=(R // rt,), in_specs=[blk] * 4, out_specs=[blk] * 3, out_shape=(sd, sd, sd), name=name,
                          compiler_params=_cp(("parallel",)))(w, g, m, v)


def _coords():
    return lax.axis_index("x"), lax.axis_index("y"), lax.axis_index("c")


def _other_chips(x, y):
    return [(1 - x, y), (x, 1 - y), (1 - x, 1 - y)]


def _allgather_weights(shards, slots):
    n = len(shards)
    halves = [s.shape[0] // 2 for s in shards]

    def body(*refs):
        ins, outs = refs[:n], refs[2 * n:3 * n]
        send1, recv1, send2, recv2 = refs[3 * n:]
        x, y, c = _coords()
        me = 2 * x + y
        chips = _other_chips(x, y)
        sends = []
        for w in range(n):
            mine = pl.ds(c * halves[w], halves[w])
            for j, (px, py) in enumerate(chips):
                cp = pltpu.make_async_remote_copy(src_ref=ins[w].at[mine, :], dst_ref=outs[w].at[me, mine, :],
                                                  send_sem=send1.at[w, j], recv_sem=recv1.at[w, j],
                                                  device_id=(px, py, c), device_id_type=MESH)
                cp.start()
                sends.append(cp)
        for w in range(n):
            mine = pl.ds(c * halves[w], halves[w])
            for j, (px, py) in enumerate(chips):
                blk = outs[w].at[2 * px + py, mine, :]
                pltpu.make_async_remote_copy(src_ref=blk, dst_ref=blk, send_sem=send1.at[w, j], recv_sem=recv1.at[w, j],
                                             device_id=(px, py, c), device_id_type=MESH).wait_recv()
                cp = pltpu.make_async_remote_copy(src_ref=blk, dst_ref=blk, send_sem=send2.at[w, j], recv_sem=recv2.at[w, j],
                                                  device_id=(x, y, 1 - c), device_id_type=MESH)
                cp.start()
                sends.append(cp)
        for w in range(n):
            theirs = pl.ds((1 - c) * halves[w], halves[w])
            for j, (px, py) in enumerate(chips):
                blk = outs[w].at[2 * px + py, theirs, :]
                pltpu.make_async_remote_copy(src_ref=blk, dst_ref=blk, send_sem=send2.at[w, j], recv_sem=recv2.at[w, j],
                                             device_id=(x, y, 1 - c), device_id_type=MESH).wait_recv()
        for cp in sends:
            cp.wait_send()

    return pl.pallas_call(
        body, in_specs=[HBM] * (2 * n), out_specs=[HBM] * n,
        out_shape=tuple(SDS((N_CHIPS,) + s.shape, s.dtype) for s in shards),
        input_output_aliases={n + w: w for w in range(n)},
        scratch_shapes=[pltpu.SemaphoreType.DMA((n, 3))] * 4,
        name="allgather_weights", compiler_params=pltpu.CompilerParams(has_side_effects=True))(*shards, *slots)


def _pair_exchange(parts, name):
    n = len(parts)
    halves = [p.shape[1] // 2 for p in parts]

    def body(*refs):
        ins, got_o = refs[:n], refs[n:2 * n]
        send, recv = refs[2 * n:]
        x, y, c = _coords()
        cps = []
        for w in range(n):
            give = pl.ds((1 - c) * halves[w], halves[w])
            rc = pltpu.make_async_remote_copy(src_ref=ins[w].at[:, give, :], dst_ref=got_o[w], send_sem=send.at[w],
                                              recv_sem=recv.at[w], device_id=(x, y, 1 - c), device_id_type=MESH)
            rc.start()
            cps.append(rc)
        for rc in cps:
            rc.wait()

    hs = tuple(SDS((N_CHIPS, p.shape[1] // 2, p.shape[2]), p.dtype) for p in parts)
    return pl.pallas_call(body, in_specs=[HBM] * n, out_specs=[HBM] * n, out_shape=hs,
                          scratch_shapes=[pltpu.SemaphoreType.DMA((n,))] * 2, name=name,
                          compiler_params=pltpu.CompilerParams(has_side_effects=True))(*parts)


def _x_chip_exchange(sums):
    n = len(sums)

    def copies(ins, outs, sems):
        send, recv = sems
        x, y, c = _coords()
        return [pltpu.make_async_remote_copy(src_ref=ins[w].at[2 * px + py], dst_ref=outs[w].at[j], send_sem=send.at[w, j],
                                             recv_sem=recv.at[w, j], device_id=(px, py, c), device_id_type=MESH)
                for w in range(n) for j, (px, py) in enumerate(_other_chips(x, y))]

    def start(ins, outs, sems):
        for cp in copies(ins, outs, sems):
            cp.start()

    def wait(ins, outs, sems):
        for cp in copies(ins, outs, sems):
            cp.wait()

    return _Exchange(ins=list(sums), out_shape=[SDS((N_CHIPS - 1,) + s.shape[1:], s.dtype) for s in sums],
                     scratch=[pltpu.SemaphoreType.DMA((n, 3)), pltpu.SemaphoreType.DMA((n, 3))], aliases={}, start=start, wait=wait)


def _x_gather_ici(shards, slots):
    n = len(shards)
    halves = [s.shape[0] // 2 for s in shards]

    def copies(ins, outs, sems):
        send, recv = sems
        x, y, c = _coords()
        me = 2 * x + y
        out = []
        for w in range(n):
            mine = pl.ds(c * halves[w], halves[w])
            for j, (px, py) in enumerate(_other_chips(x, y)):
                snd = pltpu.make_async_remote_copy(src_ref=ins[w].at[mine, :], dst_ref=outs[w].at[me, mine, :], send_sem=send.at[w, j],
                                                   recv_sem=recv.at[w, j], device_id=(px, py, c), device_id_type=MESH)
                got = outs[w].at[2 * px + py, mine, :]
                rcv = pltpu.make_async_remote_copy(src_ref=got, dst_ref=got, send_sem=send.at[w, j], recv_sem=recv.at[w, j],
                                                   device_id=(px, py, c), device_id_type=MESH)
                out.append((snd, rcv))
        return out

    def start(ins, outs, sems):
        for snd, _ in copies(ins, outs, sems):
            snd.start()

    def wait(ins, outs, sems):
        for snd, rcv in copies(ins, outs, sems):
            rcv.wait_recv()
            snd.wait_send()

    return _Exchange(ins=list(shards) + list(slots), out_shape=[SDS(s.shape, s.dtype) for s in slots],
                     scratch=[pltpu.SemaphoreType.DMA((n, 3)), pltpu.SemaphoreType.DMA((n, 3))],
                     aliases={n + w: w for w in range(n)}, start=start, wait=wait)


def _x_gather_fwd(bufs):
    n = len(bufs)
    halves = [b.shape[1] // 2 for b in bufs]

    def copies(ins, outs, sems):
        send, recv = sems
        x, y, c = _coords()
        out = []
        for w in range(n):
            for j, (px, py) in enumerate(_other_chips(x, y)):
                mine = outs[w].at[2 * px + py, pl.ds(c * halves[w], halves[w]), :]
                theirs = outs[w].at[2 * px + py, pl.ds((1 - c) * halves[w], halves[w]), :]
                snd = pltpu.make_async_remote_copy(src_ref=mine, dst_ref=mine, send_sem=send.at[w, j], recv_sem=recv.at[w, j],
                                                   device_id=(x, y, 1 - c), device_id_type=MESH)
                rcv = pltpu.make_async_remote_copy(src_ref=theirs, dst_ref=theirs, send_sem=send.at[w, j], recv_sem=recv.at[w, j],
                                                   device_id=(x, y, 1 - c), device_id_type=MESH)
                out.append((snd, rcv))
        return out

    def start(ins, outs, sems):
        for snd, _ in copies(ins, outs, sems):
            snd.start()

    def wait(ins, outs, sems):
        for snd, rcv in copies(ins, outs, sems):
            rcv.wait_recv()
            snd.wait_send()

    return _Exchange(ins=list(bufs), out_shape=[SDS(b.shape, b.dtype) for b in bufs],
                     scratch=[pltpu.SemaphoreType.DMA((n, 3)), pltpu.SemaphoreType.DMA((n, 3))],
                     aliases={w: w for w in range(n)}, start=start, wait=wait)


def _pair_gather(bufs):
    n = len(bufs)

    def body(*refs):
        outs = refs[n:2 * n]
        send, recv = refs[2 * n:]
        x, y, c = _coords()
        cps = []
        for w in range(n):
            h = bufs[w].shape[0] // 2
            rows = outs[w].at[pl.ds(c * h, h), :]
            rc = pltpu.make_async_remote_copy(src_ref=rows, dst_ref=rows, send_sem=send.at[w], recv_sem=recv.at[w],
                                              device_id=(x, y, 1 - c), device_id_type=MESH)
            rc.start()
            cps.append(rc)
        for w, rc in enumerate(cps):
            h = bufs[w].shape[0] // 2
            other = outs[w].at[pl.ds((1 - c) * h, h), :]
            pltpu.make_async_remote_copy(src_ref=other, dst_ref=other, send_sem=send.at[w], recv_sem=recv.at[w],
                                         device_id=(x, y, 1 - c), device_id_type=MESH).wait_recv()
            rc.wait_send()

    return pl.pallas_call(body, in_specs=[HBM] * n, out_specs=[HBM] * n, out_shape=tuple(SDS(a.shape, a.dtype) for a in bufs),
                          input_output_aliases={w: w for w in range(n)},
                          scratch_shapes=[pltpu.SemaphoreType.DMA((n,))] * 2, name="grad_pair_gather",
                          compiler_params=pltpu.CompilerParams(has_side_effects=True))(*bufs)


def _small_allreduce(vec, name):
    R = vec.shape[0]
    nd = 8

    def body(v_ref, o_ref, buf, send, recv):
        x, y, c = _coords()
        me = 4 * x + 2 * y + c
        buf[me] = v_ref[...]
        cps = []
        for k in range(1, nd):
            kx, ky, kc = (k >> 2) & 1, (k >> 1) & 1, k & 1
            tx = x + kx - 2 * x * kx
            ty = y + ky - 2 * y * ky
            tc = c + kc - 2 * c * kc
            cp = pltpu.make_async_remote_copy(src_ref=v_ref, dst_ref=buf.at[me], send_sem=send.at[k], recv_sem=recv.at[k],
                                              device_id=(tx, ty, tc), device_id_type=MESH)
            cp.start()
            cps.append((cp, 4 * tx + 2 * ty + tc))
        for k, (cp, src) in zip(range(1, nd), cps):
            pltpu.make_async_remote_copy(src_ref=v_ref, dst_ref=buf.at[src], send_sem=send.at[k], recv_sem=recv.at[k],
                                         device_id=(x, y, c), device_id_type=MESH).wait_recv()
        for cp, _ in cps:
            cp.wait_send()
        acc = buf[0]
        for d in range(1, nd):
            acc = acc + buf[d]
        o_ref[...] = acc

    vm = pl.BlockSpec(memory_space=pltpu.VMEM)
    return pl.pallas_call(body, in_specs=[vm], out_specs=vm, out_shape=SDS((R, LANES), f32),
                          scratch_shapes=[pltpu.VMEM((nd, R, LANES), f32), pltpu.SemaphoreType.DMA((nd,)), pltpu.SemaphoreType.DMA((nd,))],
                          name=name, compiler_params=pltpu.CompilerParams(has_side_effects=True))(vec)


def _rows128(a, rows):
    flat = a.reshape(-1)
    return jnp.pad(flat, (0, rows * LANES - flat.shape[0])).reshape(rows, LANES)


GATHER_1 = ("w_conv_out", "w_attn_out", "w_o", "w_ffn_gate")
GATHER_2 = ("w_ffn_up", "w_ffn_down")
REDUCE_A = ("w_ffn_gate", "w_ffn_up", "w_ffn_down")
REDUCE_B = ("w_conv_out", "w_attn_out", "w_o")
REDUCE_C = ("w_in",)


def _step(x, target, norm1_g, gate_b, conv_w, conv_b, conv_ln_g, conv_ln_b, norm2_g, norm_f_g, shards, slots, chip1, core):
    B = x.shape[0]
    T = B * SEQ
    xf = x.reshape(T, D_MODEL)
    tf = target.reshape(T, D_MODEL)

    (w_in,) = _allgather_weights([shards["w_in"]], [slots["w_in"]])
    h = _rms_fwd(xf, norm1_g, "rms1_fwd")
    proj, got1 = _mm("in_proj", h, w_in, grid=(N_CHIPS, T // 1024),
                     a_spec=pl.BlockSpec((1024, D_MODEL), lambda s, m: (m, 0)),
                     b_spec=pl.BlockSpec((None, D_MODEL, IN_S), lambda s, m: (s, 0, 0)),
                     o_spec=pl.BlockSpec((1024, IN_S), lambda s, m: (m, s)), o_shape=(T, IN_W), o_dtype=bf16, dims=NN,
                     comm=_x_gather_ici([shards[n] for n in GATHER_1], [slots[n] for n in GATHER_1]))
    proj3 = proj.reshape(B, SEQ, IN_W)
    proj4 = proj.reshape(B, 1, SEQ, IN_W)

    qkv = [None] + [_qkv_to_residues(proj3, g) for g in range(1, len(GROUPS))]

    def qkv_args(g):
        return (proj4, proj4, proj4, 0, 3, 6) if g == 0 else (qkv[g], qkv[g], qkv[g], 0, 1, 2)

    (o4_0, l4_0), full1 = _attn_fwd(0, *qkv_args(0), comm=_x_gather_fwd(list(got1)))
    o4_1, l4_1 = _attn_fwd(1, *qkv_args(1))
    o4_2, l4_2 = _attn_fwd(2, *qkv_args(2))
    full = dict(zip(GATHER_1, full1))
    w_conv_out, w_attn_out, w_o, w_gate = (full[n] for n in GATHER_1)
    w_conv_out_f = w_conv_out.reshape(D_MODEL, D_MODEL)
    w_o_f = w_o.reshape(D_MODEL, D_MODEL)
    mix3, lse3, lse_r1, lse_r2 = _attn_mix(o4_0.reshape(B, SEQ, GW), l4_0.reshape(B, SEQ, LANES), [o4_1, o4_2], [l4_1, l4_2])
    mix = mix3.reshape(T, GW)
    y_attn = _mm("attn_out", mix, w_attn_out, grid=(N_CHIPS, T // 512),
                 a_spec=pl.BlockSpec((512, GW), lambda s, m: (m, 0)),
                 b_spec=pl.BlockSpec((None, GW, D_MODEL // N_CHIPS), lambda s, m: (s, 0, 0)),
                 o_spec=pl.BlockSpec((512, D_MODEL // N_CHIPS), lambda s, m: (m, s)), o_shape=(T, D_MODEL), o_dtype=bf16,
                 dims=NN, sem=("parallel", "parallel"))

    c1, got2 = _glu_conv_fwd(proj3, conv_w, conv_b, comm=_x_gather_ici([shards[n] for n in GATHER_2], [slots[n] for n in GATHER_2]))
    c1 = c1.reshape(T, D_MODEL)
    c3 = _ln_silu_fwd(c1, conv_ln_g, conv_ln_b)
    y_conv, (w_up, w_down) = _mm_nn_full("conv_out", c3, w_conv_out_f, bf16, comm=_x_gather_fwd(list(got2)))

    merged = _merge_fwd(proj, gate_b, y_attn, y_conv)
    x1 = _mm_nn_full("o_proj", merged, w_o_f, f32, res=xf)

    h2 = _rms_fwd(x1, norm2_g, "rms2_fwd")
    fa, fb, ff = _ffn_up(h2, w_gate, w_up)
    x2 = _mm("ffn_down", ff, w_down, grid=(T // 1024, N_CHIPS),
             a_spec=pl.BlockSpec((None, 1024, FF_S), lambda m, s: (s, m, 0)),
             b_spec=pl.BlockSpec((None, FF_S, D_MODEL), lambda m, s: (s, 0, 0)),
             o_spec=pl.BlockSpec((1024, D_MODEL), lambda m, s: (m, 0)), o_shape=(T, D_MODEL), o_dtype=f32, dims=NN,
             acc_shape=(1024, D_MODEL), k_axis=1, res=x1, res_spec=pl.BlockSpec((1024, D_MODEL), lambda m, s: (m, 0)),
             sem=("parallel", "arbitrary"))

    loss, dx2, d_gf = _final_loss(x2, norm_f_g.reshape(1, D_MODEL), tf)

    d_w_down = _mm("d_w_down", ff, dx2, grid=(N_CHIPS, T // 1024),
                   a_spec=pl.BlockSpec((None, 1024, FF_S), lambda s, k: (s, k, 0)),
                   b_spec=pl.BlockSpec((1024, D_MODEL), lambda s, k: (k, 0)),
                   o_spec=pl.BlockSpec((None, FF_S, D_MODEL), lambda s, k: (s, 0, 0)), o_shape=(N_CHIPS, FF_S, D_MODEL),
                   o_dtype=bf16, dims=TN, acc_shape=(FF_S, D_MODEL), k_axis=1, sem=("parallel", "arbitrary"))
    da, db = _ffn_down_bwd(dx2, w_down, fa, fb)

    def d_w_ff(name, dz):
        return _mm(name, dz, h2, grid=(N_CHIPS, T // 2048),
                   a_spec=pl.BlockSpec((None, 2048, FF_S), lambda s, k: (s, k, 0)),
                   b_spec=pl.BlockSpec((2048, D_MODEL), lambda s, k: (k, 0)),
                   o_spec=pl.BlockSpec((None, FF_S, D_MODEL), lambda s, k: (s, 0, 0)), o_shape=(N_CHIPS, FF_S, D_MODEL),
                   o_dtype=bf16, dims=TN, acc_shape=(FF_S, D_MODEL), k_axis=1, sem=("parallel", "arbitrary"))

    d_w_gate = d_w_ff("d_w_gate", da)
    d_w_up = d_w_ff("d_w_up", db)
    part = dict(w_ffn_gate=d_w_gate, w_ffn_up=d_w_up, w_ffn_down=d_w_down)

    def pair_sums(names, tag):
        got = _pair_exchange([part[n] for n in names], f"grad_pair_exchange_{tag}")
        return [_add_pairs(part[n], g_, core, f"pair_sum_{n}") for n, g_ in zip(names, got)]

    sums_a = pair_sums(REDUCE_A, "a")
    dh2 = _ffn_dh2(da, w_gate, db, w_up)
    dx1, d_g2 = _rms_bwd(dh2, x1, norm2_g, dx2, "rms2_bwd")

    d_w_o = _mm_tn_tokens("d_w_o", merged, dx1, bf16).reshape(N_CHIPS, D_MODEL // N_CHIPS, D_MODEL)
    dmerged = _mm_nt_full("d_merged", dx1, w_o_f, bf16)
    dya, dyc, dga, dgc, d_gba, d_gbc = _merge_bwd(dmerged, proj, gate_b, y_attn, y_conv)

    d_w_conv_out = _mm_tn_tokens("d_w_conv_out", c3, dyc, bf16).reshape(N_CHIPS, D_MODEL // N_CHIPS, D_MODEL)
    dc3 = _mm_nt_full("d_c3", dyc, w_conv_out_f, bf16)
    dc1, d_ln_g, d_ln_b = _ln_silu_bwd(dc3, c1, conv_ln_g, conv_ln_b)
    (dua, dub, d_conv_w, d_conv_b), got_a = _glu_conv_bwd(dc1.reshape(B, SEQ, D_MODEL), proj3, conv_w, comm=_x_chip_exchange(sums_a))

    d_w_attn_out = _mm("d_w_attn_out", mix, dya, grid=(N_CHIPS, T // 512),
                       a_spec=pl.BlockSpec((512, GW), lambda s, k: (k, 0)),
                       b_spec=pl.BlockSpec((512, D_MODEL // N_CHIPS), lambda s, k: (k, s)),
                       o_spec=pl.BlockSpec((None, GW, D_MODEL // N_CHIPS), lambda s, k: (s, 0, 0)),
                       o_shape=(N_CHIPS, GW, D_MODEL // N_CHIPS), o_dtype=bf16, dims=TN, acc_shape=(GW, D_MODEL // N_CHIPS),
                       k_axis=1, sem=("parallel", "arbitrary"))
    dmix = _mm("d_mix", dya, w_attn_out, grid=(T // 512, N_CHIPS),
               a_spec=pl.BlockSpec((512, D_MODEL // N_CHIPS), lambda m, s: (m, s)),
               b_spec=pl.BlockSpec((None, GW, D_MODEL // N_CHIPS), lambda m, s: (s, 0, 0)),
               o_spec=pl.BlockSpec((512, GW), lambda m, s: (m, 0)), o_shape=(T, GW), o_dtype=bf16, dims=NT,
               acc_shape=(512, GW), k_axis=1, sem=("parallel", "arbitrary"))
    dmix3 = dmix.reshape(B, SEQ, GW)
    delta3, delta_r1, delta_r2, dmix_r1, dmix_r2 = _attn_delta(dmix3, mix3)
    part.update(w_conv_out=d_w_conv_out, w_attn_out=d_w_attn_out, w_o=d_w_o)
    sums_b = pair_sums(REDUCE_B, "b")
    one = (B, 1, SEQ)
    (dq0, dk0, dv0), got_b = _attn_bwd(0, *qkv_args(0), dmix3.reshape(one + (GW,)), lse3.reshape(one + (LANES,)),
                                       delta3.reshape(one + (LANES,)), comm=_x_chip_exchange(sums_b))
    dqkv = [tuple(t.reshape(B, SEQ, GW) for t in (dq0, dk0, dv0)),
            _attn_bwd(1, *qkv_args(1), dmix_r1, lse_r1, delta_r1), _attn_bwd(2, *qkv_args(2), dmix_r2, lse_r2, delta_r2)]
    dproj = _assemble_dproj(dqkv, dua, dub, dga.reshape(B, SEQ, D_MODEL), dgc.reshape(B, SEQ, D_MODEL)).reshape(T, IN_W)

    d_w_in = _mm("d_w_in", h, dproj, grid=(N_CHIPS, T // 1024),
                 a_spec=pl.BlockSpec((1024, D_MODEL), lambda s, k: (k, 0)),
                 b_spec=pl.BlockSpec((1024, IN_S), lambda s, k: (k, s)),
                 o_spec=pl.BlockSpec((None, D_MODEL, IN_S), lambda s, k: (s, 0, 0)), o_shape=(N_CHIPS, D_MODEL, IN_S),
                 o_dtype=bf16, dims=TN, acc_shape=(D_MODEL, IN_S), k_axis=1, sem=("parallel", "arbitrary"))
    part.update(w_in=d_w_in)
    sums_c = pair_sums(REDUCE_C, "c")
    dh, got_c = _mm("d_h", dproj, w_in, grid=(T // 1024, N_CHIPS),
                    a_spec=pl.BlockSpec((1024, IN_S), lambda m, s: (m, s)),
                    b_spec=pl.BlockSpec((None, D_MODEL, IN_S), lambda m, s: (s, 0, 0)),
                    o_spec=pl.BlockSpec((1024, D_MODEL), lambda m, s: (m, 0)), o_shape=(T, D_MODEL), o_dtype=f32, dims=NT,
                    acc_shape=(1024, D_MODEL), k_axis=1, comm=_x_chip_exchange(sums_c))
    dx, d_g1 = _rms_bwd(dh, xf, norm1_g, dx1, "rms1_bwd")

    names = REDUCE_A + REDUCE_B + REDUCE_C
    halves = [_sum4(s, g_, chip1, core, f"chip_sum_{n}") for n, s, g_ in zip(names, sums_a + sums_b + sums_c, got_a + got_b + got_c)]
    big = dict(zip(names, _pair_gather(halves)))
    small = dict(norm1_g=d_g1, gate_b=jnp.concatenate([d_gba, d_gbc], axis=-1), conv_b=d_conv_b, conv_ln_g=d_ln_g,
                 conv_ln_b=d_ln_b, norm2_g=d_g2, norm_f_g=d_gf, conv_w=d_conv_w)
    return loss, dx.reshape(B, SEQ, D_MODEL), big, small


BIG = ("w_in", "w_conv_out", "w_attn_out", "w_o", "w_ffn_gate", "w_ffn_up", "w_ffn_down")
TRANSPOSED = ("w_ffn_gate", "w_ffn_up")
SMALL = ("norm1_g", "gate_b", "conv_b", "conv_ln_g", "conv_ln_b", "norm2_g", "norm_f_g")
SMALL_ROWS = {"norm1_g": 8, "gate_b": 16, "conv_b": 8, "conv_ln_g": 8, "conv_ln_b": 8, "norm2_g": 8, "norm_f_g": 8}
LOSS_ROWS = 8
CONVW_ROWS = 32 * D_MODEL // LANES


def kernel(x, norm1_g, w_in, gate_b, conv_w, conv_b, conv_ln_g, conv_ln_b, w_conv_out, w_attn_out, w_o, norm2_g, w_ffn_gate, w_ffn_up, w_ffn_down, norm_f_g, loss_target, m_norm1_g, m_w_in, m_gate_b, m_conv_w, m_conv_b, m_conv_ln_g, m_conv_ln_b, m_w_conv_out, m_w_attn_out, m_w_o, m_norm2_g, m_w_ffn_gate, m_w_ffn_up, m_w_ffn_down, m_norm_f_g, v_norm1_g, v_w_in, v_gate_b, v_conv_w, v_conv_b, v_conv_ln_g, v_conv_ln_b, v_w_conv_out, v_w_attn_out, v_w_o, v_norm2_g, v_w_ffn_gate, v_w_ffn_up, v_w_ffn_down, v_norm_f_g):
    W = dict(norm1_g=norm1_g, w_in=w_in, gate_b=gate_b, conv_w=conv_w, conv_b=conv_b, conv_ln_g=conv_ln_g, conv_ln_b=conv_ln_b,
             w_conv_out=w_conv_out, w_attn_out=w_attn_out, w_o=w_o, norm2_g=norm2_g, w_ffn_gate=w_ffn_gate, w_ffn_up=w_ffn_up,
             w_ffn_down=w_ffn_down, norm_f_g=norm_f_g)
    M = dict(norm1_g=m_norm1_g, w_in=m_w_in, gate_b=m_gate_b, conv_w=m_conv_w, conv_b=m_conv_b, conv_ln_g=m_conv_ln_g,
             conv_ln_b=m_conv_ln_b, w_conv_out=m_w_conv_out, w_attn_out=m_w_attn_out, w_o=m_w_o, norm2_g=m_norm2_g,
             w_ffn_gate=m_w_ffn_gate, w_ffn_up=m_w_ffn_up, w_ffn_down=m_w_ffn_down, norm_f_g=m_norm_f_g)
    V = dict(norm1_g=v_norm1_g, w_in=v_w_in, gate_b=v_gate_b, conv_w=v_conv_w, conv_b=v_conv_b, conv_ln_g=v_conv_ln_g,
             conv_ln_b=v_conv_ln_b, w_conv_out=v_w_conv_out, w_attn_out=v_w_attn_out, w_o=v_w_o, norm2_g=v_norm2_g,
             w_ffn_gate=v_w_ffn_gate, w_ffn_up=v_w_ffn_up, w_ffn_down=v_w_ffn_down, norm_f_g=v_norm_f_g)
    order = list(W)

    def as2d(n, a):
        a = a.reshape(a.shape[-2:])
        return a.T if n in TRANSPOSED else a

    def from2d(n, a):
        return (a.T if n in TRANSPOSED else a).reshape(W[n].shape)

    shard2d = {n: as2d(n, W[n]) for n in BIG}
    chip = 2 * lax.axis_index("x") + lax.axis_index("y")

    cw = jnp.zeros((CONV_K, D_MODEL), f32)
    cw = lax.dynamic_update_slice(cw, 0.5 * conv_w.reshape(CONV_K, D_MODEL // N_CHIPS), (0, chip * (D_MODEL // N_CHIPS)))
    conv_w_full = _small_allreduce(_rows128(cw, CONVW_ROWS), "conv_w_gather")[: CONV_K * D_MODEL // LANES].reshape(CONV_K, D_MODEL)

    core = lax.axis_index("c").astype(jnp.int32).reshape(1)
    chip1 = chip.astype(jnp.int32).reshape(1)
    shards = {n: shard2d[n].astype(bf16) for n in BIG}
    slots = {n: lax.dynamic_update_slice(lax.empty((N_CHIPS,) + s.shape, bf16), s[None], (chip, 0, 0)) for n, s in shards.items()}

    loss, grad_x, grads, gsmall = _step(x, loss_target, norm1_g, gate_b, conv_w_full, conv_b, conv_ln_g, conv_ln_b, norm2_g,
                                        norm_f_g, shards, slots, chip1, core)

    pieces = [_rows128(loss, LOSS_ROWS)] + [_rows128(gsmall[n], SMALL_ROWS[n]) for n in SMALL] + [_rows128(gsmall["conv_w"], CONVW_ROWS)]
    tot = _small_allreduce(jnp.concatenate(pieces, axis=0), "small_allreduce")
    loss_out = tot[0, 0]
    row = LOSS_ROWS
    for n in SMALL:
        grads[n] = tot[row: row + W[n].size // LANES].reshape(W[n].shape)
        row += SMALL_ROWS[n]
    dcw = tot[row: row + CONV_K * D_MODEL // LANES].reshape(CONV_K, D_MODEL)
    grads["conv_w"] = lax.dynamic_slice(dcw, (0, chip * (D_MODEL // N_CHIPS)), (CONV_K, D_MODEL // N_CHIPS))

    delta, new_m, new_v = {}, {}, {}
    for n in BIG:
        d, nm, nv = _adamw(shard2d[n], grads[n], as2d(n, M[n]), as2d(n, V[n]), f"adamw_{n}")
        delta[n], new_m[n], new_v[n], grads[n] = (from2d(n, t) for t in (d, nm, nv, grads[n]))

    def pack(src):
        return jnp.concatenate([_rows128(src[n], SMALL_ROWS[n]) for n in SMALL], axis=0)

    d, nm, nv = _adamw(pack(W), pack(grads), pack(M), pack(V), "adamw_small")
    row = 0
    for n in SMALL:
        k = W[n].size // LANES
        delta[n], new_m[n], new_v[n] = (t[row: row + k].reshape(W[n].shape) for t in (d, nm, nv))
        row += SMALL_ROWS[n]

    def pad32(a):
        return jnp.pad(a.reshape(CONV_K, D_MODEL // N_CHIPS), ((0, 1), (0, 0)))

    d, nm, nv = _adamw(pad32(conv_w), pad32(grads["conv_w"]), pad32(m_conv_w), pad32(v_conv_w), "adamw_conv_w")
    delta["conv_w"], new_m["conv_w"], new_v["conv_w"] = (t[:CONV_K].reshape(conv_w.shape) for t in (d, nm, nv))
    grads["conv_w"] = grads["conv_w"].reshape(conv_w.shape)

    return (loss_out, grad_x, *[grads[n] for n in order], *[delta[n] for n in order],
            *[new_m[n] for n in order], *[new_v[n] for n in order])
```

```python
import functools
import math
from typing import Callable, NamedTuple

import numpy as np
import jax
import jax.numpy as jnp
from jax import lax
from jax.experimental import pallas as pl
from jax.experimental.pallas import tpu as pltpu

f32 = jnp.float32
bf16 = jnp.bfloat16
SDS = jax.ShapeDtypeStruct
MESH = pl.DeviceIdType.MESH

D_MODEL = 1024
SEQ = 2048
HEAD_DIM = 64
HEADS = 8
GROUPS = ((128, 1), (512, 4), (2048, 16))
GW = HEADS * HEAD_DIM
ATTN_W = len(GROUPS) * GW
Q_BLOCK = 128
CONV_K = 31
D_FF = 2816
IN_W = 3 * ATTN_W + 2 * D_MODEL + 2 * D_MODEL
N_CHIPS = 4
IN_S = IN_W // N_CHIPS
FF_S = D_FF // N_CHIPS
RMS_EPS = 1e-6
LN_EPS = 1e-5
LR, B1, B2, ADAM_EPS, WD, STEP = 0.001, 0.9, 0.999, 1e-08, 0.01, 10
NEG = -1e30
LANES = 128
VMEM_LIMIT = 48 * 2 ** 20
CB = 512
UA_CB, UB_CB, GA_CB, GC_CB = 9, 11, 13, 15


def _alibi_slope_list(n):
    def pow2(m):
        start = 2.0 ** (-8.0 / m)
        return [start ** (i + 1) for i in range(m)]
    if math.log2(n).is_integer():
        return pow2(n)
    c = 2 ** math.floor(math.log2(n))
    return pow2(c) + _alibi_slope_list(2 * c)[0::2][: n - c]


_SLOPES = np.asarray(sorted(_alibi_slope_list(len(GROUPS) * HEADS), reverse=True), dtype=np.float32).reshape(len(GROUPS), HEADS)


def _cp(sem=None, vmem=VMEM_LIMIT):
    return pltpu.CompilerParams(dimension_semantics=sem, vmem_limit_bytes=vmem)


def _sigmoid(x):
    return 1.0 / (1.0 + jnp.exp(-x))


HBM = pl.BlockSpec(memory_space=pl.ANY)


class _Exchange(NamedTuple):
    ins: list
    out_shape: list
    scratch: list
    aliases: dict
    start: Callable
    wait: Callable


def _launch(body, *, name, grid, in_specs, out_specs, out_shape, args, scratch_shapes=(), sem=None, comm=None):
    if comm is None:
        return pl.pallas_call(body, grid=grid, in_specs=in_specs, out_specs=out_specs, out_shape=out_shape,
                              scratch_shapes=list(scratch_shapes), name=name, compiler_params=_cp(sem))(*args)
    multi = isinstance(out_shape, (tuple, list))
    m_out = list(out_shape) if multi else [out_shape]
    m_ospec = list(out_specs) if multi else [out_specs]
    n_in, n_out, n_scr = len(in_specs), len(m_out), len(scratch_shapes)
    nc_in, nc_out = len(comm.ins), len(comm.out_shape)

    def hosted(*refs):
        bounds = np.cumsum([0, n_in, nc_in, n_out, nc_out, n_scr])
        mi, ci, mo, co, ms = (refs[a:b] for a, b in zip(bounds[:-1], bounds[1:]))
        cs = refs[bounds[-1]:]
        ids = [pl.program_id(a) for a in range(len(grid))]
        first = functools.reduce(jnp.logical_and, [i == 0 for i in ids])
        last = functools.reduce(jnp.logical_and, [i == g - 1 for i, g in zip(ids, grid)])

        @pl.when(first)
        def _():
            comm.start(ci, co, cs)

        body(*mi, *mo, *ms)

        @pl.when(last)
        def _():
            comm.wait(ci, co, cs)

    res = pl.pallas_call(
        hosted, grid=grid, in_specs=list(in_specs) + [HBM] * nc_in, out_specs=m_ospec + [HBM] * nc_out,
        out_shape=tuple(m_out) + tuple(comm.out_shape), scratch_shapes=list(scratch_shapes) + list(comm.scratch),
        input_output_aliases={n_in + i: n_out + o for i, o in comm.aliases.items()}, name=name + "_comm",
        compiler_params=pltpu.CompilerParams(dimension_semantics=("arbitrary",) * len(grid), vmem_limit_bytes=VMEM_LIMIT,
                                             has_side_effects=True))(*args, *comm.ins)
    return (tuple(res[:n_out]) if multi else res[0]), tuple(res[n_out:])


def _run_exchange(ex, name):
    n_in = len(ex.ins)

    def body(*refs):
        ins, outs, sems = refs[:n_in], refs[n_in:n_in + len(ex.out_shape)], refs[n_in + len(ex.out_shape):]
        ex.start(ins, outs, sems)
        ex.wait(ins, outs, sems)

    return pl.pallas_call(body, in_specs=[HBM] * n_in, out_specs=[HBM] * len(ex.out_shape), out_shape=tuple(ex.out_shape),
                          scratch_shapes=list(ex.scratch), input_output_aliases=dict(ex.aliases), name=name,
                          compiler_params=pltpu.CompilerParams(has_side_effects=True))(*ex.ins)


def _rmsnorm(xv, gv):
    return xv * lax.rsqrt(jnp.mean(xv * xv, axis=-1, keepdims=True) + RMS_EPS) * gv


def _rms_fwd(x, g, name):
    T = x.shape[0]
    tm = 512

    def body(x_ref, g_ref, o_ref, ot_ref):
        hv = _rmsnorm(x_ref[...], g_ref[...])
        o_ref[...] = hv.astype(bf16)
        ot_ref[...] = hv.T.astype(bf16)

    row = pl.BlockSpec((tm, D_MODEL), lambda i: (i, 0))
    vec = pl.BlockSpec((1, D_MODEL), lambda i: (0, 0))
    return pl.pallas_call(body, grid=(T // tm,), in_specs=[row, vec], out_specs=[row, pl.BlockSpec((D_MODEL, tm), lambda i: (0, i))],
                          out_shape=(SDS((T, D_MODEL), bf16), SDS((D_MODEL, T), bf16)), name=name,
                          compiler_params=_cp(("parallel",)))(x, g)


def _rms_bwd(dy, x, g, dres, name):
    T = x.shape[0]
    tm = 512

    def body(dy_ref, x_ref, g_ref, r_ref, dx_ref, dg_ref):
        xv = x_ref[...]
        r = lax.rsqrt(jnp.mean(xv * xv, axis=-1, keepdims=True) + RMS_EPS)
        xh = xv * r
        dyv = dy_ref[...]
        dxh = dyv * g_ref[...]
        dx_ref[...] = r_ref[...] + r * (dxh - xh * jnp.mean(dxh * xh, axis=-1, keepdims=True))
        part = jnp.sum(dyv * xh, axis=0, keepdims=True)

        @pl.when(pl.program_id(0) == 0)
        def _():
            dg_ref[...] = part

        @pl.when(pl.program_id(0) > 0)
        def _():
            dg_ref[...] += part

    row = pl.BlockSpec((tm, D_MODEL), lambda i: (i, 0))
    vec = pl.BlockSpec((1, D_MODEL), lambda i: (0, 0))
    return pl.pallas_call(body, grid=(T // tm,), in_specs=[row, row, vec, row], out_specs=[row, vec],
                          out_shape=(SDS((T, D_MODEL), f32), SDS((1, D_MODEL), f32)), name=name,
                          compiler_params=_cp(("arbitrary",)))(dy, x, g, dres)


def _ffn_down_loss(ff, wd, x1, gf, target):
    T = x1.shape[0]
    tm = 1024
    nm = T // tm

    def body(f_ref, w_ref, x_ref, g_ref, t_ref, loss_ref, dx_ref, dg_ref, acc):
        m, s = pl.program_id(0), pl.program_id(1)
        p = jnp.dot(f_ref[...], w_ref[...], preferred_element_type=f32)

        @pl.when(s == 0)
        def _():
            acc[...] = p

        @pl.when(s > 0)
        def _():
            acc[...] += p

        @pl.when(s == N_CHIPS - 1)
        def _():
            xv = acc[...] + x_ref[...]
            gv = g_ref[...]
            r = lax.rsqrt(jnp.mean(xv * xv, axis=-1, keepdims=True) + RMS_EPS)
            xh = xv * r
            e = xh * gv - t_ref[...]
            part_l = jnp.broadcast_to(0.5 * jnp.sum(jnp.mean(e * e, axis=-1, keepdims=True), axis=0, keepdims=True), (1, LANES))
            dy = e * (1.0 / D_MODEL)
            dxh = dy * gv
            dx_ref[...] = r * (dxh - xh * jnp.mean(dxh * xh, axis=-1, keepdims=True))
            part_g = jnp.sum(dy * xh, axis=0, keepdims=True)

            @pl.when(m == 0)
            def _():
                loss_ref[...] = part_l
                dg_ref[...] = part_g

            @pl.when(m > 0)
            def _():
                loss_ref[...] += part_l
                dg_ref[...] += part_g

    row = pl.BlockSpec((tm, D_MODEL), lambda m, s: (m, 0))
    vec = pl.BlockSpec((1, D_MODEL), lambda m, s: (0, 0))
    return pl.pallas_call(
        body, grid=(nm, N_CHIPS),
        in_specs=[pl.BlockSpec((None, tm, FF_S), lambda m, s: (s, m, 0)), pl.BlockSpec((None, FF_S, D_MODEL), lambda m, s: (s, 0, 0)),
                  row, vec, row],
        out_specs=[pl.BlockSpec((1, LANES), lambda m, s: (0, 0)), row, vec],
        out_shape=(SDS((1, LANES), f32), SDS((T, D_MODEL), f32), SDS((1, D_MODEL), f32)),
        scratch_shapes=[pltpu.VMEM((tm, D_MODEL), f32)], name="ffn_down_loss", compiler_params=_cp(("arbitrary", "arbitrary")))(
            ff, wd, x1, gf, target)


def _o_proj_rms(merged, w_o, x, g2):
    T = x.shape[0]
    tm = 1024

    def body(a_ref, w_ref, x_ref, g_ref, x1_ref, h2_ref):
        x1 = jnp.dot(a_ref[...], w_ref[...], preferred_element_type=f32) + x_ref[...]
        x1_ref[...] = x1
        h2_ref[...] = _rmsnorm(x1, g_ref[...]).astype(bf16)

    row = pl.BlockSpec((tm, D_MODEL), lambda i: (i, 0))
    return pl.pallas_call(body, grid=(T // tm,),
                          in_specs=[row, pl.BlockSpec((D_MODEL, D_MODEL), lambda i: (0, 0)), row, pl.BlockSpec((1, D_MODEL), lambda i: (0, 0))],
                          out_specs=[row, row], out_shape=(SDS((T, D_MODEL), f32), SDS((T, D_MODEL), bf16)), name="o_proj_rms",
                          compiler_params=_cp(("parallel",)))(merged, w_o, x, g2)


def _ln_silu_fwd(c1, g, b):
    T = c1.shape[0]
    tm = 512

    def body(c_ref, g_ref, b_ref, o_ref):
        cv = c_ref[...]
        mu = jnp.mean(cv, axis=-1, keepdims=True)
        cc = cv - mu
        var = jnp.mean(cc * cc, axis=-1, keepdims=True)
        c2 = cc * lax.rsqrt(var + LN_EPS) * g_ref[...] + b_ref[...]
        o_ref[...] = (c2 * _sigmoid(c2)).astype(o_ref.dtype)

    row = pl.BlockSpec((tm, D_MODEL), lambda i: (i, 0))
    vec = pl.BlockSpec((1, D_MODEL), lambda i: (0, 0))
    return pl.pallas_call(body, grid=(T // tm,), in_specs=[row, vec, vec], out_specs=row,
                          out_shape=SDS((T, D_MODEL), bf16), name="ln_silu_fwd", compiler_params=_cp(("parallel",)))(c1, g, b)


def _ln_silu_bwd(dc3, c1, g, b):
    T = c1.shape[0]
    tm = 512

    def body(d_ref, c_ref, g_ref, b_ref, dc_ref, dg_ref, db_ref):
        cv = c_ref[...]
        gv = g_ref[...]
        mu = jnp.mean(cv, axis=-1, keepdims=True)
        cc = cv - mu
        var = jnp.mean(cc * cc, axis=-1, keepdims=True)
        rs = lax.rsqrt(var + LN_EPS)
        xh = cc * rs
        c2 = xh * gv + b_ref[...]
        sg = _sigmoid(c2)
        dc2 = d_ref[...].astype(f32) * (sg * (1.0 + c2 * (1.0 - sg)))
        dxh = dc2 * gv
        dc_ref[...] = rs * (dxh - jnp.mean(dxh, axis=-1, keepdims=True) - xh * jnp.mean(dxh * xh, axis=-1, keepdims=True))
        pg = jnp.sum(dc2 * xh, axis=0, keepdims=True)
        pb = jnp.sum(dc2, axis=0, keepdims=True)

        @pl.when(pl.program_id(0) == 0)
        def _():
            dg_ref[...] = pg
            db_ref[...] = pb

        @pl.when(pl.program_id(0) > 0)
        def _():
            dg_ref[...] += pg
            db_ref[...] += pb

    row = pl.BlockSpec((tm, D_MODEL), lambda i: (i, 0))
    vec = pl.BlockSpec((1, D_MODEL), lambda i: (0, 0))
    return pl.pallas_call(body, grid=(T // tm,), in_specs=[row, row, vec, vec], out_specs=[row, vec, vec],
                          out_shape=(SDS((T, D_MODEL), f32), SDS((1, D_MODEL), f32), SDS((1, D_MODEL), f32)),
                          name="ln_silu_bwd", compiler_params=_cp(("arbitrary",)))(dc3, c1, g, b)


NN = (((1,), (0,)), ((), ()))
NT = (((1,), (1,)), ((), ()))
TN = (((0,), (0,)), ((), ()))


def _mm(name, a, b, *, grid, a_spec, b_spec, o_spec, o_shape, o_dtype, dims, acc_shape=None, k_axis=None,
        res=None, res_spec=None, sem=None, comm=None):
    nk = 1 if k_axis is None else grid[k_axis]

    def body(*refs):
        if res is None:
            a_ref, b_ref, o_ref = refs[:3]
            r_ref, scr = None, refs[3:]
        else:
            a_ref, b_ref, r_ref, o_ref = refs[:4]
            scr = refs[4:]
        p = lax.dot_general(a_ref[...].astype(bf16), b_ref[...].astype(bf16), dims, preferred_element_type=f32)
        if nk == 1:
            if r_ref is not None:
                p = p + r_ref[...]
            o_ref[...] = p.astype(o_dtype)
            return
        acc = scr[0]
        k = pl.program_id(k_axis)

        @pl.when(k == 0)
        def _():
            acc[...] = p

        @pl.when(k > 0)
        def _():
            acc[...] += p

        @pl.when(k == nk - 1)
        def _():
            t = acc[...]
            if r_ref is not None:
                t = t + r_ref[...]
            o_ref[...] = t.astype(o_dtype)

    ins = [a, b] + ([] if res is None else [res])
    specs = [a_spec, b_spec] + ([] if res is None else [res_spec])
    scratch = [] if nk == 1 else [pltpu.VMEM(acc_shape, f32)]
    return _launch(body, name=name, grid=grid, in_specs=specs, out_specs=o_spec, out_shape=SDS(o_shape, o_dtype),
                   args=ins, scratch_shapes=scratch, sem=sem, comm=comm)


def _mm_nn_full(name, a, b, o_dtype, res=None, tm=1024, comm=None):
    T, K = a.shape
    N = b.shape[1]
    return _mm(name, a, b, grid=(T // tm,), a_spec=pl.BlockSpec((tm, K), lambda i: (i, 0)),
               b_spec=pl.BlockSpec((K, N), lambda i: (0, 0)), o_spec=pl.BlockSpec((tm, N), lambda i: (i, 0)),
               o_shape=(T, N), o_dtype=o_dtype, dims=NN, res=res,
               res_spec=None if res is None else pl.BlockSpec((tm, N), lambda i: (i, 0)), sem=("parallel",), comm=comm)


def _mm_nt_full(name, a, b, o_dtype, tm=1024):
    T, N = a.shape
    K = b.shape[0]
    return _mm(name, a, b, grid=(T // tm,), a_spec=pl.BlockSpec((tm, N), lambda i: (i, 0)),
               b_spec=pl.BlockSpec((K, N), lambda i: (0, 0)), o_spec=pl.BlockSpec((tm, K), lambda i: (i, 0)),
               o_shape=(T, K), o_dtype=o_dtype, dims=NT, sem=("parallel",))


def _mm_tn_tokens(name, a, b, o_dtype, tk=1024):
    T, K = a.shape
    N = b.shape[1]
    return _mm(name, a, b, grid=(T // tk,), a_spec=pl.BlockSpec((tk, K), lambda k: (k, 0)),
               b_spec=pl.BlockSpec((tk, N), lambda k: (k, 0)), o_spec=pl.BlockSpec((K, N), lambda k: (0, 0)),
               o_shape=(K, N), o_dtype=o_dtype, dims=TN, acc_shape=(K, N), k_axis=0, sem=("arbitrary",))


def _ffn_up(h2, wg, wu):
    T = h2.shape[0]
    tm = 1024

    def body(h_ref, wg_ref, wu_ref, a_ref, b_ref, f_ref):
        hv = h_ref[...]
        av = lax.dot_general(hv, wg_ref[...], NT, preferred_element_type=f32)
        bv = lax.dot_general(hv, wu_ref[...], NT, preferred_element_type=f32)
        a_ref[...] = av.astype(bf16)
        b_ref[...] = bv.astype(bf16)
        f_ref[...] = (av * _sigmoid(av) * bv).astype(bf16)

    wspec = pl.BlockSpec((None, FF_S, D_MODEL), lambda s, m: (s, 0, 0))
    ospec = pl.BlockSpec((None, tm, FF_S), lambda s, m: (s, m, 0))
    osd = SDS((N_CHIPS, T, FF_S), bf16)
    return pl.pallas_call(body, grid=(N_CHIPS, T // tm),
                          in_specs=[pl.BlockSpec((tm, D_MODEL), lambda s, m: (m, 0)), wspec, wspec],
                          out_specs=[ospec, ospec, ospec], out_shape=(osd, osd, osd), name="ffn_up",
                          compiler_params=_cp(("parallel", "parallel")))(h2, wg, wu)


def _ffn_down_bwd(dx2, wd, a, b):
    T = dx2.shape[0]
    tm = 1024

    def body(d_ref, w_ref, a_ref, b_ref, da_ref, db_ref):
        df = lax.dot_general(d_ref[...].astype(bf16), w_ref[...], NT, preferred_element_type=f32)
        av = a_ref[...].astype(f32)
        sg = _sigmoid(av)
        da_ref[...] = (df * b_ref[...].astype(f32) * (sg * (1.0 + av * (1.0 - sg)))).astype(bf16)
        db_ref[...] = (df * av * sg).astype(bf16)

    aspec = pl.BlockSpec((None, tm, FF_S), lambda m, s: (s, m, 0))
    osd = SDS((N_CHIPS, T, FF_S), bf16)
    return pl.pallas_call(body, grid=(T // tm, N_CHIPS),
                          in_specs=[pl.BlockSpec((tm, D_MODEL), lambda m, s: (m, 0)),
                                    pl.BlockSpec((None, FF_S, D_MODEL), lambda m, s: (s, 0, 0)), aspec, aspec],
                          out_specs=[aspec, aspec], out_shape=(osd, osd), name="ffn_down_bwd",
                          compiler_params=_cp(("parallel", "parallel")))(dx2, wd, a, b)


def _ffn_dh2(da, wg, db, wu, comm=None):
    T = da.shape[1]
    tm = 1024

    def body(da_ref, wg_ref, db_ref, wu_ref, o_ref, acc):
        p = jnp.dot(da_ref[...], wg_ref[...], preferred_element_type=f32)
        p = p + jnp.dot(db_ref[...], wu_ref[...], preferred_element_type=f32)
        s = pl.program_id(1)

        @pl.when(s == 0)
        def _():
            acc[...] = p

        @pl.when(s > 0)
        def _():
            acc[...] += p

        @pl.when(s == N_CHIPS - 1)
        def _():
            o_ref[...] = acc[...]

    aspec = pl.BlockSpec((None, tm, FF_S), lambda m, s: (s, m, 0))
    wspec = pl.BlockSpec((None, FF_S, D_MODEL), lambda m, s: (s, 0, 0))
    return _launch(body, name="ffn_dh2", grid=(T // tm, N_CHIPS), in_specs=[aspec, wspec, aspec, wspec],
                   out_specs=pl.BlockSpec((tm, D_MODEL), lambda m, s: (m, 0)), out_shape=SDS((T, D_MODEL), f32),
                   args=(da, wg, db, wu), scratch_shapes=[pltpu.VMEM((tm, D_MODEL), f32)], sem=("parallel", "arbitrary"), comm=comm)


def _merge_fwd(proj, gate_b, ya, yc):
    T = proj.shape[0]
    tm = 512

    def body(ga_ref, gc_ref, ba_ref, bc_ref, ya_ref, yc_ref, o_ref):
        sa = _sigmoid(ga_ref[...].astype(f32) + ba_ref[...])
        sc = _sigmoid(gc_ref[...].astype(f32) + bc_ref[...])
        o_ref[...] = (sa * ya_ref[...].astype(f32) + sc * yc_ref[...].astype(f32)).astype(bf16)

    blk = pl.BlockSpec((tm, CB), lambda i, j: (i, j))
    return pl.pallas_call(
        body, grid=(T // tm, 2),
        in_specs=[pl.BlockSpec((tm, CB), lambda i, j: (i, GA_CB + j)), pl.BlockSpec((tm, CB), lambda i, j: (i, GC_CB + j)),
                  pl.BlockSpec((1, CB), lambda i, j: (0, j)), pl.BlockSpec((1, CB), lambda i, j: (0, 2 + j)), blk, blk],
        out_specs=blk, out_shape=SDS((T, D_MODEL), bf16), name="merge_fwd",
        compiler_params=_cp(("parallel", "parallel")))(proj, proj, gate_b, gate_b, ya, yc)


def _merge_bwd(dm, proj, gate_b, ya, yc):
    T = proj.shape[0]
    tm = 512

    def body(dm_ref, ga_ref, gc_ref, ba_ref, bc_ref, ya_ref, yc_ref, dya_ref, dyc_ref, dga_ref, dgc_ref, dba_ref, dbc_ref):
        dmv = dm_ref[...].astype(f32)
        sa = _sigmoid(ga_ref[...].astype(f32) + ba_ref[...])
        sc = _sigmoid(gc_ref[...].astype(f32) + bc_ref[...])
        dya_ref[...] = (dmv * sa).astype(bf16)
        dyc_ref[...] = (dmv * sc).astype(bf16)
        dga = dmv * ya_ref[...].astype(f32) * sa * (1.0 - sa)
        dgc = dmv * yc_ref[...].astype(f32) * sc * (1.0 - sc)
        dga_ref[...] = dga.astype(bf16)
        dgc_ref[...] = dgc.astype(bf16)
        pa = jnp.sum(dga, axis=0, keepdims=True)
        pc = jnp.sum(dgc, axis=0, keepdims=True)

        @pl.when(pl.program_id(1) == 0)
        def _():
            dba_ref[...] = pa
            dbc_ref[...] = pc

        @pl.when(pl.program_id(1) > 0)
        def _():
            dba_ref[...] += pa
            dbc_ref[...] += pc

    blk = pl.BlockSpec((tm, CB), lambda j, i: (i, j))
    vec = pl.BlockSpec((1, CB), lambda j, i: (0, j))
    big = SDS((T, D_MODEL), bf16)
    small = SDS((1, D_MODEL), f32)
    return pl.pallas_call(
        body, grid=(2, T // tm),
        in_specs=[blk, pl.BlockSpec((tm, CB), lambda j, i: (i, GA_CB + j)), pl.BlockSpec((tm, CB), lambda j, i: (i, GC_CB + j)),
                  vec, pl.BlockSpec((1, CB), lambda j, i: (0, 2 + j)), blk, blk],
        out_specs=[blk, blk, blk, blk, vec, vec], out_shape=(big, big, big, big, small, small), name="merge_bwd",
        compiler_params=_cp(("parallel", "arbitrary")))(dm, proj, proj, gate_b, gate_b, ya, yc)


CONV_TS = 256
CONV_HALO = 32
CONV_RC = 64
CONV_WIN = CONV_TS + CONV_HALO
SUBLANES = 8


def _fill_shifted(win, sh):
    for b in range(1, SUBLANES):
        sh[b - 1] = win[pl.ds(b, CONV_WIN - SUBLANES), :]


def _rows_at(win, sh, row):
    a, b = divmod(row, SUBLANES)
    if b == 0:
        return win[pl.ds(row, CONV_RC), :]
    return sh[b - 1, pl.ds(a * SUBLANES, CONV_RC), :]


def _glu_conv_fwd(proj3, w, bias, comm=None):
    B = proj3.shape[0]
    nt = SEQ // CONV_TS
    hb = CONV_TS // CONV_HALO

    def body(ua_ref, ub_ref, ha_ref, hb_ref, w_ref, b_ref, o_ref, win, sh):
        i = pl.program_id(2)
        c0 = ua_ref[...].astype(f32) * _sigmoid(ub_ref[...].astype(f32))
        halo = ha_ref[...].astype(f32) * _sigmoid(hb_ref[...].astype(f32))
        win[0:CONV_HALO, :] = jnp.where(i > 0, halo, 0.0)
        win[CONV_HALO:, :] = c0
        _fill_shifted(win, sh)
        for r0 in range(0, CONV_TS, CONV_RC):
            acc = jnp.zeros((CONV_RC, CB), f32) + b_ref[...]
            for k in range(CONV_K):
                acc = acc + _rows_at(win, sh, r0 + CONV_HALO - (CONV_K - 1) + k) * w_ref[k:k + 1, :]
            o_ref[r0:r0 + CONV_RC, :] = acc

    def cur(cb):
        return pl.BlockSpec((None, CONV_TS, CB), lambda b, j, i: (b, i, cb + j))

    def prev(cb):
        return pl.BlockSpec((None, CONV_HALO, CB), lambda b, j, i: (b, jnp.maximum(i * hb - 1, 0), cb + j))

    return _launch(
        body, name="glu_conv_fwd", grid=(B, 2, nt),
        in_specs=[cur(UA_CB), cur(UB_CB), prev(UA_CB), prev(UB_CB),
                  pl.BlockSpec((CONV_K, CB), lambda b, j, i: (0, j)), pl.BlockSpec((1, CB), lambda b, j, i: (0, j))],
        out_specs=pl.BlockSpec((None, CONV_TS, CB), lambda b, j, i: (b, i, j)),
        out_shape=SDS((B, SEQ, D_MODEL), f32), args=(proj3, proj3, proj3, proj3, w, bias),
        scratch_shapes=[pltpu.VMEM((CONV_WIN, CB), f32), pltpu.VMEM((SUBLANES - 1, CONV_WIN - SUBLANES, CB), f32)],
        sem=("parallel", "parallel", "parallel"), comm=comm)


def _glu_conv_bwd(dc1, proj3, w, comm=None):
    B = proj3.shape[0]
    nt = SEQ // CONV_TS
    hb = CONV_TS // CONV_HALO

    def body(d_ref, dn_ref, ua_ref, ub_ref, ha_ref, hb_ref, w_ref, dua_ref, dub_ref, dw_ref, db_ref, winc, wind, accw, shc, shd):
        b = pl.program_id(1)
        i = pl.program_id(2)
        first = jnp.logical_and(b == 0, i == 0)
        last = jnp.logical_and(b == B - 1, i == nt - 1)

        @pl.when(first)
        def _():
            accw[...] = jnp.zeros_like(accw)
            db_ref[...] = jnp.zeros_like(db_ref)

        halo = ha_ref[...].astype(f32) * _sigmoid(hb_ref[...].astype(f32))
        winc[0:CONV_HALO, :] = jnp.where(i > 0, halo, 0.0)
        winc[CONV_HALO:, :] = ua_ref[...].astype(f32) * _sigmoid(ub_ref[...].astype(f32))
        wind[0:CONV_TS, :] = d_ref[...]
        wind[CONV_TS:, :] = jnp.where(i < nt - 1, dn_ref[...], 0.0)
        db_ref[...] += jnp.sum(d_ref[...], axis=0, keepdims=True)
        _fill_shifted(winc, shc)
        _fill_shifted(wind, shd)
        for r0 in range(0, CONV_TS, CONV_RC):
            dc0 = jnp.zeros((CONV_RC, CB), f32)
            for k in range(CONV_K):
                dc0 = dc0 + _rows_at(wind, shd, r0 + (CONV_K - 1) - k) * w_ref[k:k + 1, :]
            uav = ua_ref[r0:r0 + CONV_RC, :].astype(f32)
            sg = _sigmoid(ub_ref[r0:r0 + CONV_RC, :].astype(f32))
            dua_ref[r0:r0 + CONV_RC, :] = (dc0 * sg).astype(bf16)
            dub_ref[r0:r0 + CONV_RC, :] = (dc0 * uav * sg * (1.0 - sg)).astype(bf16)
            dv = wind[r0:r0 + CONV_RC, :]
            for k in range(CONV_K):
                prod = dv * _rows_at(winc, shc, r0 + CONV_HALO - (CONV_K - 1) + k)
                accw[k] += jnp.sum(prod.reshape(CONV_RC // 8, 8, CB), axis=0)

        @pl.when(last)
        def _():
            for k in range(CONV_K):
                dw_ref[k:k + 1, :] = jnp.sum(accw[k], axis=0, keepdims=True)
            dw_ref[CONV_K:, :] = jnp.zeros((CONV_HALO - CONV_K, CB), f32)

    def cur(cb):
        return pl.BlockSpec((None, CONV_TS, CB), lambda j, b, i: (b, i, cb + j))

    def prev(cb):
        return pl.BlockSpec((None, CONV_HALO, CB), lambda j, b, i: (b, jnp.maximum(i * hb - 1, 0), cb + j))

    nxt = pl.BlockSpec((None, CONV_HALO, CB), lambda j, b, i: (b, jnp.minimum((i + 1) * hb, SEQ // CONV_HALO - 1), j))
    big = SDS((B, SEQ, D_MODEL), bf16)
    return _launch(
        body, name="glu_conv_bwd", grid=(2, B, nt),
        in_specs=[cur(0), nxt, cur(UA_CB), cur(UB_CB), prev(UA_CB), prev(UB_CB), pl.BlockSpec((CONV_K, CB), lambda j, b, i: (0, j))],
        out_specs=[cur(0), cur(0), pl.BlockSpec((CONV_HALO, CB), lambda j, b, i: (0, j)), pl.BlockSpec((1, CB), lambda j, b, i: (0, j))],
        out_shape=(big, big, SDS((CONV_HALO, D_MODEL), f32), SDS((1, D_MODEL), f32)),
        args=(dc1, dc1, proj3, proj3, proj3, proj3, w),
        scratch_shapes=[pltpu.VMEM((CONV_WIN, CB), f32), pltpu.VMEM((CONV_WIN, CB), f32), pltpu.VMEM((CONV_K, SUBLANES, CB), f32),
                        pltpu.VMEM((SUBLANES - 1, CONV_WIN - SUBLANES, CB), f32),
                        pltpu.VMEM((SUBLANES - 1, CONV_WIN - SUBLANES, CB), f32)],
        sem=("parallel", "arbitrary", "arbitrary"), comm=comm)


def _band(first, dil):
    kw = Q_BLOCK if first else 2 * Q_BLOCK
    qi = lax.broadcasted_iota(jnp.int32, (Q_BLOCK, kw), 0)
    kj = lax.broadcasted_iota(jnp.int32, (Q_BLOCK, kw), 1)
    rel = qi - kj + (0 if first else Q_BLOCK)
    valid = jnp.logical_and(rel >= 0, rel <= Q_BLOCK)
    return valid, rel.astype(f32) * float(dil)


def _bias(first, dil, slope):
    valid, dist = _band(first, dil)
    return jnp.where(valid, -slope * dist, NEG)


def _scores(q, k, bias):
    return lax.dot_general(q, k, NT, preferred_element_type=f32) * (HEAD_DIM ** -0.5) + bias


def _pair_cols(hp):
    return slice(hp * LANES, (hp + 1) * LANES)


def _half(x2, e):
    lane = lax.broadcasted_iota(jnp.int32, (1, LANES), 1)
    keep = (lane < HEAD_DIM) if e == 0 else (lane >= HEAD_DIM)
    return jnp.where(keep, x2, jnp.zeros_like(x2))


def _attn_fwd(g, q4, k4, v4, qcb, kcb, vcb, comm=None):
    _, dil = GROUPS[g]
    B, r, L, _ = q4.shape
    nb = L // Q_BLOCK
    slopes = [float(s) for s in _SLOPES[g]]

    def body(q_ref, k_ref, v_ref, o_ref, lse_ref, bias, s_scr, p_scr):
        lane = lax.broadcasted_iota(jnp.int32, (Q_BLOCK, LANES), 1)
        if nb > 1:
            for h in range(HEADS):
                bias[h] = _bias(False, dil, slopes[h])

        def block(n, first):
            q0 = 0 if first else pl.multiple_of(n * Q_BLOCK, Q_BLOCK)
            k0 = 0 if first else pl.multiple_of((n - 1) * Q_BLOCK, Q_BLOCK)
            kw = Q_BLOCK if first else 2 * Q_BLOCK
            for hp in range(HEADS // 2):
                q2 = q_ref[pl.ds(q0, Q_BLOCK), _pair_cols(hp)]
                k2 = k_ref[pl.ds(k0, kw), _pair_cols(hp)]
                for e in range(2):
                    h = 2 * hp + e
                    b_h = _bias(True, dil, slopes[h]) if first else bias[h]
                    s_scr[h, :, :kw] = _scores(_half(q2, e), k2, b_h)
            st = jnp.zeros((Q_BLOCK, LANES), f32)
            dens = jnp.ones((Q_BLOCK, LANES), f32)
            for h in range(HEADS):
                s = s_scr[h, :, :kw]
                m = jnp.max(s, axis=-1, keepdims=True)
                p = jnp.exp(s - m)
                den = jnp.sum(p, axis=-1, keepdims=True)
                p_scr[h, :, :kw] = p.astype(bf16)
                st = jnp.where(lane == h, m + jnp.log(den), st)
                dens = jnp.where(lane == h, den, dens)
            lse_ref[pl.ds(q0, Q_BLOCK), :] = st
            inv = 1.0 / dens
            for hp in range(HEADS // 2):
                v2 = v_ref[pl.ds(k0, kw), _pair_cols(hp)]
                o2 = sum(jnp.dot(p_scr[2 * hp + e, :, :kw], _half(v2, e), preferred_element_type=f32) * inv[:, 2 * hp + e:2 * hp + e + 1]
                         for e in range(2))
                o_ref[pl.ds(q0, Q_BLOCK), _pair_cols(hp)] = o2.astype(bf16)

        block(0, True)
        if nb > 1:
            def step(n, carry):
                block(n, False)
                return carry
            lax.fori_loop(1, nb, step, 0)

    def spec(cb):
        return pl.BlockSpec((None, None, L, GW), lambda b, c: (b, c, 0, cb))

    return _launch(
        body, name=f"attn_fwd_g{g}", grid=(B, r), in_specs=[spec(qcb), spec(kcb), spec(vcb)],
        out_specs=[spec(0), pl.BlockSpec((None, None, L, LANES), lambda b, c: (b, c, 0, 0))],
        out_shape=(SDS((B, r, L, GW), bf16), SDS((B, r, L, LANES), f32)), args=(q4, k4, v4),
        scratch_shapes=[pltpu.VMEM((HEADS, Q_BLOCK, 2 * Q_BLOCK), f32), pltpu.VMEM((HEADS, Q_BLOCK, 2 * Q_BLOCK), f32),
                        pltpu.VMEM((HEADS, Q_BLOCK, 2 * Q_BLOCK), bf16)],
        sem=("parallel", "parallel"), comm=comm)


def _attn_bwd(g, q4, k4, v4, qcb, kcb, vcb, do4, lse4, dl4, comm=None):
    _, dil = GROUPS[g]
    B, r, L, _ = q4.shape
    nb = L // Q_BLOCK
    slopes = [float(s) for s in _SLOPES[g]]
    scale = HEAD_DIM ** -0.5

    def body(q_ref, k_ref, v_ref, do_ref, lse_ref, dl_ref, dq_ref, dk_ref, dv_ref, dk_acc, dv_acc, bias, s_scr, dp_scr, p_scr, ds_scr):
        dk_acc[...] = jnp.zeros_like(dk_acc)
        dv_acc[...] = jnp.zeros_like(dv_acc)
        if nb > 1:
            for h in range(HEADS):
                bias[h] = _bias(False, dil, slopes[h])

        def block(n, first):
            q0 = 0 if first else pl.multiple_of(n * Q_BLOCK, Q_BLOCK)
            k0 = 0 if first else pl.multiple_of((n - 1) * Q_BLOCK, Q_BLOCK)
            kw = Q_BLOCK if first else 2 * Q_BLOCK
            for hp in range(HEADS // 2):
                q2 = q_ref[pl.ds(q0, Q_BLOCK), _pair_cols(hp)]
                k2 = k_ref[pl.ds(k0, kw), _pair_cols(hp)]
                v2 = v_ref[pl.ds(k0, kw), _pair_cols(hp)]
                do2 = do_ref[pl.ds(q0, Q_BLOCK), _pair_cols(hp)]
                for e in range(2):
                    h = 2 * hp + e
                    b_h = _bias(True, dil, slopes[h]) if first else bias[h]
                    s_scr[h, :, :kw] = _scores(_half(q2, e), k2, b_h)
                    dp_scr[h, :, :kw] = lax.dot_general(_half(do2, e), v2, NT, preferred_element_type=f32)
            for h in range(HEADS):
                p = jnp.exp(s_scr[h, :, :kw] - lse_ref[pl.ds(q0, Q_BLOCK), h:h + 1])
                p_scr[h, :, :kw] = p.astype(bf16)
                ds_scr[h, :, :kw] = (p * (dp_scr[h, :, :kw] - dl_ref[pl.ds(q0, Q_BLOCK), h:h + 1])).astype(bf16)
            for hp in range(HEADS // 2):
                cols = _pair_cols(hp)
                q2 = q_ref[pl.ds(q0, Q_BLOCK), cols]
                k2 = k_ref[pl.ds(k0, kw), cols]
                do2 = do_ref[pl.ds(q0, Q_BLOCK), cols]
                ds = [ds_scr[2 * hp + e, :, :kw] for e in range(2)]
                dq2 = sum(jnp.dot(ds[e], _half(k2, e), preferred_element_type=f32) for e in range(2))
                dq_ref[pl.ds(q0, Q_BLOCK), cols] = (dq2 * scale).astype(bf16)
                dk2 = sum(lax.dot_general(ds[e], _half(q2, e), TN, preferred_element_type=f32) for e in range(2))
                dk_acc[pl.ds(k0, kw), cols] += dk2 * scale
                dv2 = sum(lax.dot_general(p_scr[2 * hp + e, :, :kw], _half(do2, e), TN, preferred_element_type=f32) for e in range(2))
                dv_acc[pl.ds(k0, kw), cols] += dv2

        block(0, True)
        if nb > 1:
            def step(n, carry):
                block(n, False)
                return carry
            lax.fori_loop(1, nb, step, 0)
        dk_ref[...] = dk_acc[...].astype(bf16)
        dv_ref[...] = dv_acc[...].astype(bf16)

    def spec(cb):
        return pl.BlockSpec((None, None, L, GW), lambda b, c: (b, c, 0, cb))

    st = pl.BlockSpec((None, None, L, LANES), lambda b, c: (b, c, 0, 0))
    osd = SDS((B, r, L, GW), bf16)
    return _launch(
        body, name=f"attn_bwd_g{g}", grid=(B, r), in_specs=[spec(qcb), spec(kcb), spec(vcb), spec(0), st, st],
        out_specs=[spec(0), spec(0), spec(0)], out_shape=(osd, osd, osd), args=(q4, k4, v4, do4, lse4, dl4),
        scratch_shapes=[pltpu.VMEM((L, GW), f32), pltpu.VMEM((L, GW), f32)]
        + [pltpu.VMEM((HEADS, Q_BLOCK, 2 * Q_BLOCK), f32)] * 3 + [pltpu.VMEM((HEADS, Q_BLOCK, 2 * Q_BLOCK), bf16)] * 2,
        sem=("parallel", "parallel"), comm=comm)


RT = 512
RCH = GW // LANES
DILS = tuple(d for _, d in GROUPS[1:])


def _res_spec(r, width):
    return pl.BlockSpec((None, r, RT // r, width), lambda b, i, *_: (b, 0, i, 0))


def _tok_spec(width, cb=0):
    return pl.BlockSpec((None, RT, width), lambda b, i, *_: (b, i, cb))


def _to_residues(res_ref, scr, r, width):
    for c in range(r):
        for jj in range(width // LANES):
            res_ref[c, :, jj * LANES:(jj + 1) * LANES] = scr[jj, pl.ds(c, RT // r, stride=r), :].astype(res_ref.dtype)


def _from_residues(scr, res_ref, r, width):
    for c in range(r):
        for jj in range(width // LANES):
            scr[jj, pl.ds(c, RT // r, stride=r), :] = res_ref[c, :, jj * LANES:(jj + 1) * LANES].astype(f32)


def _qkv_to_residues(proj3, g):
    r = GROUPS[g][1]
    B = proj3.shape[0]

    def body(q_ref, k_ref, v_ref, o_ref, scr):
        for p, x_ref in enumerate((q_ref, k_ref, v_ref)):
            x = x_ref[...].astype(f32)
            for jj in range(RCH):
                scr[p * RCH + jj] = x[:, jj * LANES:(jj + 1) * LANES]
        _to_residues(o_ref, scr, r, ATTN_W)

    return pl.pallas_call(
        body, grid=(B, SEQ // RT), in_specs=[_tok_spec(GW, 3 * p + g) for p in range(3)], out_specs=_res_spec(r, ATTN_W),
        out_shape=SDS((B, r, SEQ // r, ATTN_W), bf16), scratch_shapes=[pltpu.VMEM((3 * RCH, RT, LANES), f32)],
        name=f"qkv_to_residues_g{g}", compiler_params=_cp(("parallel", "parallel")))(proj3, proj3, proj3)


def _attn_mix(o0, l0, o_res, l_res):
    B = o0.shape[0]

    def body(o0_ref, l0_ref, o1_ref, o2_ref, l1_ref, l2_ref, y_ref, lt_ref, lt1_ref, lt2_ref, so, sl):
        for gi, (o_ref, l_ref, r) in enumerate(((o1_ref, l1_ref, DILS[0]), (o2_ref, l2_ref, DILS[1]))):
            _from_residues(so.at[gi], o_ref, r, GW)
            _from_residues(sl.at[gi:gi + 1], l_ref, r, LANES)
        ls = [l0_ref[...], sl[0], sl[1]]
        m = functools.reduce(jnp.maximum, ls)
        ws = [jnp.exp(l - m) for l in ls]
        den = ws[0] + ws[1] + ws[2]
        alphas = [w / den for w in ws]
        lt = m + jnp.log(den)
        lt_ref[...] = lt
        sl[2] = lt
        _to_residues(lt1_ref, sl.at[2:3], DILS[0], LANES)
        _to_residues(lt2_ref, sl.at[2:3], DILS[1], LANES)
        for h in range(HEADS):
            cols = slice(h * HEAD_DIM, (h + 1) * HEAD_DIM)
            jj, lo = divmod(h * HEAD_DIM, LANES)
            acc = alphas[0][:, h:h + 1] * o0_ref[:, cols].astype(f32)
            for gi in range(2):
                acc = acc + alphas[gi + 1][:, h:h + 1] * so[gi, jj, :, lo:lo + HEAD_DIM]
            y_ref[:, cols] = acc.astype(bf16)

    in_specs = [_tok_spec(GW), _tok_spec(LANES), _res_spec(DILS[0], GW), _res_spec(DILS[1], GW), _res_spec(DILS[0], LANES), _res_spec(DILS[1], LANES)]
    out_specs = [_tok_spec(GW), _tok_spec(LANES), _res_spec(DILS[0], LANES), _res_spec(DILS[1], LANES)]
    return pl.pallas_call(
        body, grid=(B, SEQ // RT), in_specs=in_specs, out_specs=out_specs,
        out_shape=(SDS((B, SEQ, GW), bf16), SDS((B, SEQ, LANES), f32)) + tuple(SDS((B, r, SEQ // r, LANES), f32) for r in DILS),
        scratch_shapes=[pltpu.VMEM((2, RCH, RT, LANES), f32), pltpu.VMEM((3, RT, LANES), f32)],
        name="attn_mix", compiler_params=_cp(("parallel", "parallel")))(o0, l0, *o_res, *l_res)


def _attn_delta(dmix, mix):
    B = dmix.shape[0]

    def body(d_ref, y_ref, dl_ref, dl1_ref, dl2_ref, dm1_ref, dm2_ref, sx, sd):
        lane = lax.broadcasted_iota(jnp.int32, (RT, LANES), 1)
        acc = jnp.zeros((RT, LANES), f32)
        dv = d_ref[...].astype(f32)
        for jj in range(RCH):
            sx[jj] = dv[:, jj * LANES:(jj + 1) * LANES]
        for h in range(HEADS):
            cols = slice(h * HEAD_DIM, (h + 1) * HEAD_DIM)
            dl = jnp.sum(dv[:, cols] * y_ref[:, cols].astype(f32), axis=-1, keepdims=True)
            acc = jnp.where(lane == h, dl, acc)
        dl_ref[...] = acc
        sd[0] = acc
        _to_residues(dl1_ref, sd, DILS[0], LANES)
        _to_residues(dl2_ref, sd, DILS[1], LANES)
        _to_residues(dm1_ref, sx, DILS[0], GW)
        _to_residues(dm2_ref, sx, DILS[1], GW)

    return pl.pallas_call(
        body, grid=(B, SEQ // RT), in_specs=[_tok_spec(GW), _tok_spec(GW)],
        out_specs=[_tok_spec(LANES), _res_spec(DILS[0], LANES), _res_spec(DILS[1], LANES), _res_spec(DILS[0], GW), _res_spec(DILS[1], GW)],
        out_shape=(SDS((B, SEQ, LANES), f32),) + tuple(SDS((B, r, SEQ // r, LANES), f32) for r in DILS)
        + tuple(SDS((B, r, SEQ // r, GW), bf16) for r in DILS),
        scratch_shapes=[pltpu.VMEM((RCH, RT, LANES), f32), pltpu.VMEM((1, RT, LANES), f32)],
        name="attn_delta", compiler_params=_cp(("parallel", "parallel")))(dmix, mix)


N_CB = IN_W // CB


def _assemble_dproj(dqkv, dua, dub, dga, dgc):
    B = dua.shape[0]
    ng = len(GROUPS)
    flat = [dqkv[g][p] for p in range(3) for g in range(ng)]
    wide = [dua, dub, dga, dgc]

    def body(*refs):
        srcs, wides, o_ref, scr = refs[:3 * ng], refs[3 * ng:3 * ng + 4], refs[3 * ng + 4], refs[3 * ng + 5]
        for jv in range(3 * ng):
            g = jv % ng
            if g == 0:
                o_ref[:, jv * GW:(jv + 1) * GW] = srcs[jv][...]
            else:
                _from_residues(scr, srcs[jv], GROUPS[g][1], GW)
                for jj in range(RCH):
                    o_ref[:, jv * GW + jj * LANES:jv * GW + (jj + 1) * LANES] = scr[jj].astype(bf16)
        for wv in range(4):
            lo = 3 * ATTN_W + wv * D_MODEL
            o_ref[:, lo:lo + D_MODEL] = wides[wv][...]

    in_specs = [_tok_spec(GW) if (jv % ng) == 0 else _res_spec(GROUPS[jv % ng][1], GW) for jv in range(3 * ng)]
    in_specs += [_tok_spec(D_MODEL)] * 4
    return pl.pallas_call(
        body, grid=(B, SEQ // RT), in_specs=in_specs, out_specs=_tok_spec(IN_W),
        out_shape=SDS((B, SEQ, IN_W), bf16), scratch_shapes=[pltpu.VMEM((RCH, RT, LANES), f32)],
        name="assemble_dproj", compiler_params=_cp(("parallel", "parallel")))(*flat, *wide)


def _add_pairs(part, got, core, name):
    n, h, C = got.shape

    def body(c_ref, a_ref, b_ref, o_ref):
        o_ref[...] = (a_ref[...].astype(f32) + b_ref[...].astype(f32)).astype(bf16)

    blk = pl.BlockSpec((None, h, C), lambda s, c_ref: (s, 0, 0))
    own = pl.BlockSpec((None, h, C), lambda s, c_ref: (s, c_ref[0], 0))
    spec = pltpu.PrefetchScalarGridSpec(num_scalar_prefetch=1, grid=(n,), in_specs=[own, blk], out_specs=blk)
    return pl.pallas_call(body, grid_spec=spec, out_shape=SDS((n, h, C), bf16), name=name,
                          compiler_params=_cp(("parallel",)))(core, part, got)


def _sum4(sums, got, chip, core, name):
    _, h, C = sums.shape

    def body(s_ref, c_ref, own_ref, q_ref, o_ref):
        t = own_ref[...].astype(f32) + q_ref[0].astype(f32)
        t = t + q_ref[1].astype(f32)
        o_ref[...] = t + q_ref[2].astype(f32)

    spec = pltpu.PrefetchScalarGridSpec(
        num_scalar_prefetch=2, grid=(1,),
        in_specs=[pl.BlockSpec((None, h, C), lambda i, s_ref, c_ref: (s_ref[0], 0, 0)),
                  pl.BlockSpec((N_CHIPS - 1, h, C), lambda i, s_ref, c_ref: (0, 0, 0))],
        out_specs=pl.BlockSpec((h, C), lambda i, s_ref, c_ref: (c_ref[0], 0)))
    return pl.pallas_call(body, grid_spec=spec, out_shape=SDS((2 * h, C), f32), name=name,
                          compiler_params=_cp(("arbitrary",)))(chip, core, sums, got)


def _adamw(w, g, m, v, name):
    R, C = w.shape
    rt = R
    for cand in (512, 256, 128, 64, 32, 16, 8):
        if R % cand == 0 and cand * C * 4 <= 2 ** 21:
            rt = cand
            break
    c1 = 1.0 / (1.0 - B1 ** STEP)
    c2 = 1.0 / (1.0 - B2 ** STEP)

    def body(w_ref, g_ref, m_ref, v_ref, d_ref, nm_ref, nv_ref):
        gv = g_ref[...]
        nm = B1 * m_ref[...] + (1.0 - B1) * gv
        nv = B2 * v_ref[...] + (1.0 - B2) * (gv * gv)
        nm_ref[...] = nm
        nv_ref[...] = nv
        d_ref[...] = -LR * ((nm * c1) / (jnp.sqrt(nv * c2) + ADAM_EPS) + WD * w_ref[...])

    blk = pl.BlockSpec((rt, C), lambda i: (i, 0))
    sd = SDS((R, C), f32)
    return pl.pallas_call(body, grid=(R // rt,), in_specs=[blk] * 4, out_specs=[blk] * 3, out_shape=(sd, sd, sd), name=name,
                          compiler_params=_cp(("parallel",)))(w, g, m, v)


def _coords():
    return lax.axis_index("x"), lax.axis_index("y"), lax.axis_index("c")


def _other_chips(x, y):
    return [(1 - x, y), (x, 1 - y), (1 - x, 1 - y)]


def _allgather_weights(shards, slots):
    n = len(shards)
    halves = [s.shape[0] // 2 for s in shards]

    def body(*refs):
        ins, outs = refs[:n], refs[2 * n:3 * n]
        send1, recv1, send2, recv2 = refs[3 * n:]
        x, y, c = _coords()
        me = 2 * x + y
        chips = _other_chips(x, y)
        sends = []
        for w in range(n):
            mine = pl.ds(c * halves[w], halves[w])
            for j, (px, py) in enumerate(chips):
                cp = pltpu.make_async_remote_copy(src_ref=ins[w].at[mine, :], dst_ref=outs[w].at[me, mine, :],
                                                  send_sem=send1.at[w, j], recv_sem=recv1.at[w, j],
                                                  device_id=(px, py, c), device_id_type=MESH)
                cp.start()
                sends.append(cp)
        for w in range(n):
            mine = pl.ds(c * halves[w], halves[w])
            for j, (px, py) in enumerate(chips):
                blk = outs[w].at[2 * px + py, mine, :]
                pltpu.make_async_remote_copy(src_ref=blk, dst_ref=blk, send_sem=send1.at[w, j], recv_sem=recv1.at[w, j],
                                             device_id=(px, py, c), device_id_type=MESH).wait_recv()
                cp = pltpu.make_async_remote_copy(src_ref=blk, dst_ref=blk, send_sem=send2.at[w, j], recv_sem=recv2.at[w, j],
                                                  device_id=(x, y, 1 - c), device_id_type=MESH)
                cp.start()
                sends.append(cp)
        for w in range(n):
            theirs = pl.ds((1 - c) * halves[w], halves[w])
            for j, (px, py) in enumerate(chips):
                blk = outs[w].at[2 * px + py, theirs, :]
                pltpu.make_async_remote_copy(src_ref=blk, dst_ref=blk, send_sem=send2.at[w, j], recv_sem=recv2.at[w, j],
                                             device_id=(x, y, 1 - c), device_id_type=MESH).wait_recv()
        for cp in sends:
            cp.wait_send()

    return pl.pallas_call(
        body, in_specs=[HBM] * (2 * n), out_specs=[HBM] * n,
        out_shape=tuple(SDS((N_CHIPS,) + s.shape, s.dtype) for s in shards),
        input_output_aliases={n + w: w for w in range(n)},
        scratch_shapes=[pltpu.SemaphoreType.DMA((n, 3))] * 4,
        name="allgather_weights", compiler_params=pltpu.CompilerParams(has_side_effects=True))(*shards, *slots)


def _x_pair_exchange(parts):
    n = len(parts)
    halves = [p.shape[1] // 2 for p in parts]

    def copies(ins, outs, sems):
        send, recv = sems
        x, y, c = _coords()
        return [pltpu.make_async_remote_copy(src_ref=ins[w].at[:, pl.ds((1 - c) * halves[w], halves[w]), :], dst_ref=outs[w],
                                             send_sem=send.at[w], recv_sem=recv.at[w], device_id=(x, y, 1 - c), device_id_type=MESH)
                for w in range(n)]

    def start(ins, outs, sems):
        for cp in copies(ins, outs, sems):
            cp.start()

    def wait(ins, outs, sems):
        for cp in copies(ins, outs, sems):
            cp.wait()

    return _Exchange(ins=list(parts), out_shape=[SDS((N_CHIPS, p.shape[1] // 2, p.shape[2]), p.dtype) for p in parts],
                     scratch=[pltpu.SemaphoreType.DMA((n,))] * 2, aliases={}, start=start, wait=wait)


def _x_chip_exchange(sums):
    n = len(sums)

    def copies(ins, outs, sems):
        send, recv = sems
        x, y, c = _coords()
        return [pltpu.make_async_remote_copy(src_ref=ins[w].at[2 * px + py], dst_ref=outs[w].at[j], send_sem=send.at[w, j],
                                             recv_sem=recv.at[w, j], device_id=(px, py, c), device_id_type=MESH)
                for w in range(n) for j, (px, py) in enumerate(_other_chips(x, y))]

    def start(ins, outs, sems):
        for cp in copies(ins, outs, sems):
            cp.start()

    def wait(ins, outs, sems):
        for cp in copies(ins, outs, sems):
            cp.wait()

    return _Exchange(ins=list(sums), out_shape=[SDS((N_CHIPS - 1,) + s.shape[1:], s.dtype) for s in sums],
                     scratch=[pltpu.SemaphoreType.DMA((n, 3)), pltpu.SemaphoreType.DMA((n, 3))], aliases={}, start=start, wait=wait)


def _x_gather_ici(shards, slots):
    n = len(shards)
    halves = [s.shape[0] // 2 for s in shards]

    def copies(ins, outs, sems):
        send, recv = sems
        x, y, c = _coords()
        me = 2 * x + y
        out = []
        for w in range(n):
            mine = pl.ds(c * halves[w], halves[w])
            for j, (px, py) in enumerate(_other_chips(x, y)):
                snd = pltpu.make_async_remote_copy(src_ref=ins[w].at[mine, :], dst_ref=outs[w].at[me, mine, :], send_sem=send.at[w, j],
                                                   recv_sem=recv.at[w, j], device_id=(px, py, c), device_id_type=MESH)
                got = outs[w].at[2 * px + py, mine, :]
                rcv = pltpu.make_async_remote_copy(src_ref=got, dst_ref=got, send_sem=send.at[w, j], recv_sem=recv.at[w, j],
                                                   device_id=(px, py, c), device_id_type=MESH)
                out.append((snd, rcv))
        return out

    def start(ins, outs, sems):
        for snd, _ in copies(ins, outs, sems):
            snd.start()

    def wait(ins, outs, sems):
        for snd, rcv in copies(ins, outs, sems):
            rcv.wait_recv()
            snd.wait_send()

    return _Exchange(ins=list(shards) + list(slots), out_shape=[SDS(s.shape, s.dtype) for s in slots],
                     scratch=[pltpu.SemaphoreType.DMA((n, 3)), pltpu.SemaphoreType.DMA((n, 3))],
                     aliases={n + w: w for w in range(n)}, start=start, wait=wait)


def _x_gather_fwd(bufs):
    n = len(bufs)
    halves = [b.shape[1] // 2 for b in bufs]

    def copies(ins, outs, sems):
        send, recv = sems
        x, y, c = _coords()
        out = []
        for w in range(n):
            for j, (px, py) in enumerate(_other_chips(x, y)):
                mine = outs[w].at[2 * px + py, pl.ds(c * halves[w], halves[w]), :]
                theirs = outs[w].at[2 * px + py, pl.ds((1 - c) * halves[w], halves[w]), :]
                snd = pltpu.make_async_remote_copy(src_ref=mine, dst_ref=mine, send_sem=send.at[w, j], recv_sem=recv.at[w, j],
                                                   device_id=(x, y, 1 - c), device_id_type=MESH)
                rcv = pltpu.make_async_remote_copy(src_ref=theirs, dst_ref=theirs, send_sem=send.at[w, j], recv_sem=recv.at[w, j],
                                                   device_id=(x, y, 1 - c), device_id_type=MESH)
                out.append((snd, rcv))
        return out

    def start(ins, outs, sems):
        for snd, _ in copies(ins, outs, sems):
            snd.start()

    def wait(ins, outs, sems):
        for snd, rcv in copies(ins, outs, sems):
            rcv.wait_recv()
            snd.wait_send()

    return _Exchange(ins=list(bufs), out_shape=[SDS(b.shape, b.dtype) for b in bufs],
                     scratch=[pltpu.SemaphoreType.DMA((n, 3)), pltpu.SemaphoreType.DMA((n, 3))],
                     aliases={w: w for w in range(n)}, start=start, wait=wait)


def _pair_gather(bufs):
    n = len(bufs)

    def body(*refs):
        outs = refs[n:2 * n]
        send, recv = refs[2 * n:]
        x, y, c = _coords()
        cps = []
        for w in range(n):
            h = bufs[w].shape[0] // 2
            rows = outs[w].at[pl.ds(c * h, h), :]
            rc = pltpu.make_async_remote_copy(src_ref=rows, dst_ref=rows, send_sem=send.at[w], recv_sem=recv.at[w],
                                              device_id=(x, y, 1 - c), device_id_type=MESH)
            rc.start()
            cps.append(rc)
        for w, rc in enumerate(cps):
            h = bufs[w].shape[0] // 2
            other = outs[w].at[pl.ds((1 - c) * h, h), :]
            pltpu.make_async_remote_copy(src_ref=other, dst_ref=other, send_sem=send.at[w], recv_sem=recv.at[w],
                                         device_id=(x, y, 1 - c), device_id_type=MESH).wait_recv()
            rc.wait_send()

    return pl.pallas_call(body, in_specs=[HBM] * n, out_specs=[HBM] * n, out_shape=tuple(SDS(a.shape, a.dtype) for a in bufs),
                          input_output_aliases={w: w for w in range(n)},
                          scratch_shapes=[pltpu.SemaphoreType.DMA((n,))] * 2, name="grad_pair_gather",
                          compiler_params=pltpu.CompilerParams(has_side_effects=True))(*bufs)


def _small_allreduce(vec, name):
    R = vec.shape[0]
    nd = 8

    def body(v_ref, o_ref, buf, send, recv):
        x, y, c = _coords()
        me = 4 * x + 2 * y + c
        buf[me] = v_ref[...]
        cps = []
        for k in range(1, nd):
            kx, ky, kc = (k >> 2) & 1, (k >> 1) & 1, k & 1
            tx = x + kx - 2 * x * kx
            ty = y + ky - 2 * y * ky
            tc = c + kc - 2 * c * kc
            cp = pltpu.make_async_remote_copy(src_ref=v_ref, dst_ref=buf.at[me], send_sem=send.at[k], recv_sem=recv.at[k],
                                              device_id=(tx, ty, tc), device_id_type=MESH)
            cp.start()
            cps.append((cp, 4 * tx + 2 * ty + tc))
        for k, (cp, src) in zip(range(1, nd), cps):
            pltpu.make_async_remote_copy(src_ref=v_ref, dst_ref=buf.at[src], send_sem=send.at[k], recv_sem=recv.at[k],
                                         device_id=(x, y, c), device_id_type=MESH).wait_recv()
        for cp, _ in cps:
            cp.wait_send()
        acc = buf[0]
        for d in range(1, nd):
            acc = acc + buf[d]
        o_ref[...] = acc

    vm = pl.BlockSpec(memory_space=pltpu.VMEM)
    return pl.pallas_call(body, in_specs=[vm], out_specs=vm, out_shape=SDS((R, LANES), f32),
                          scratch_shapes=[pltpu.VMEM((nd, R, LANES), f32), pltpu.SemaphoreType.DMA((nd,)), pltpu.SemaphoreType.DMA((nd,))],
                          name=name, compiler_params=pltpu.CompilerParams(has_side_effects=True))(vec)


def _rows128(a, rows):
    flat = a.reshape(-1)
    return jnp.pad(flat, (0, rows * LANES - flat.shape[0])).reshape(rows, LANES)


GATHER_1 = ("w_conv_out", "w_attn_out", "w_o", "w_ffn_gate")
GATHER_2 = ("w_ffn_up", "w_ffn_down")
REDUCE_A = ("w_ffn_gate", "w_ffn_up", "w_ffn_down")
REDUCE_B = ("w_conv_out", "w_attn_out", "w_o")
REDUCE_C = ("w_in",)


def _step(x, target, norm1_g, gate_b, conv_w, conv_b, conv_ln_g, conv_ln_b, norm2_g, norm_f_g, shards, slots, chip1, core):
    B = x.shape[0]
    T = B * SEQ
    xf = x.reshape(T, D_MODEL)
    tf = target.reshape(T, D_MODEL)

    (w_in,) = _allgather_weights([shards["w_in"]], [slots["w_in"]])
    h, h_t = _rms_fwd(xf, norm1_g, "rms1_fwd")
    proj, got1 = _mm("in_proj", h, w_in, grid=(N_CHIPS, T // 1024),
                     a_spec=pl.BlockSpec((1024, D_MODEL), lambda s, m: (m, 0)),
                     b_spec=pl.BlockSpec((None, D_MODEL, IN_S), lambda s, m: (s, 0, 0)),
                     o_spec=pl.BlockSpec((1024, IN_S), lambda s, m: (m, s)), o_shape=(T, IN_W), o_dtype=bf16, dims=NN,
                     comm=_x_gather_ici([shards[n] for n in GATHER_1], [slots[n] for n in GATHER_1]))
    proj3 = proj.reshape(B, SEQ, IN_W)
    proj4 = proj.reshape(B, 1, SEQ, IN_W)

    qkv = [None] + [_qkv_to_residues(proj3, g) for g in range(1, len(GROUPS))]

    def qkv_args(g):
        return (proj4, proj4, proj4, 0, 3, 6) if g == 0 else (qkv[g], qkv[g], qkv[g], 0, 1, 2)

    (o4_0, l4_0), full1 = _attn_fwd(0, *qkv_args(0), comm=_x_gather_fwd(list(got1)))
    o4_1, l4_1 = _attn_fwd(1, *qkv_args(1))
    o4_2, l4_2 = _attn_fwd(2, *qkv_args(2))
    full = dict(zip(GATHER_1, full1))
    w_conv_out, w_attn_out, w_o, w_gate = (full[n] for n in GATHER_1)
    w_conv_out_f = w_conv_out.reshape(D_MODEL, D_MODEL)
    w_o_f = w_o.reshape(D_MODEL, D_MODEL)
    mix3, lse3, lse_r1, lse_r2 = _attn_mix(o4_0.reshape(B, SEQ, GW), l4_0.reshape(B, SEQ, LANES), [o4_1, o4_2], [l4_1, l4_2])
    mix = mix3.reshape(T, GW)
    y_attn = _mm("attn_out", mix, w_attn_out, grid=(N_CHIPS, T // 512),
                 a_spec=pl.BlockSpec((512, GW), lambda s, m: (m, 0)),
                 b_spec=pl.BlockSpec((None, GW, D_MODEL // N_CHIPS), lambda s, m: (s, 0, 0)),
                 o_spec=pl.BlockSpec((512, D_MODEL // N_CHIPS), lambda s, m: (m, s)), o_shape=(T, D_MODEL), o_dtype=bf16,
                 dims=NN, sem=("parallel", "parallel"))

    c1, got2 = _glu_conv_fwd(proj3, conv_w, conv_b, comm=_x_gather_ici([shards[n] for n in GATHER_2], [slots[n] for n in GATHER_2]))
    c1 = c1.reshape(T, D_MODEL)
    c3 = _ln_silu_fwd(c1, conv_ln_g, conv_ln_b)
    y_conv, (w_up, w_down) = _mm_nn_full("conv_out", c3, w_conv_out_f, bf16, comm=_x_gather_fwd(list(got2)))

    merged = _merge_fwd(proj, gate_b, y_attn, y_conv)
    x1, h2 = _o_proj_rms(merged, w_o_f, xf, norm2_g)
    fa, fb, ff = _ffn_up(h2, w_gate, w_up)
    loss, dx2, d_gf = _ffn_down_loss(ff, w_down, x1, norm_f_g.reshape(1, D_MODEL), tf)

    d_w_down = _mm("d_w_down", ff, dx2, grid=(N_CHIPS, T // 1024),
                   a_spec=pl.BlockSpec((None, 1024, FF_S), lambda s, k: (s, k, 0)),
                   b_spec=pl.BlockSpec((1024, D_MODEL), lambda s, k: (k, 0)),
                   o_spec=pl.BlockSpec((None, FF_S, D_MODEL), lambda s, k: (s, 0, 0)), o_shape=(N_CHIPS, FF_S, D_MODEL),
                   o_dtype=bf16, dims=TN, acc_shape=(FF_S, D_MODEL), k_axis=1, sem=("parallel", "arbitrary"))
    da, db = _ffn_down_bwd(dx2, w_down, fa, fb)

    def d_w_ff(name, dz):
        return _mm(name, dz, h2, grid=(N_CHIPS, T // 2048),
                   a_spec=pl.BlockSpec((None, 2048, FF_S), lambda s, k: (s, k, 0)),
                   b_spec=pl.BlockSpec((2048, D_MODEL), lambda s, k: (k, 0)),
                   o_spec=pl.BlockSpec((None, FF_S, D_MODEL), lambda s, k: (s, 0, 0)), o_shape=(N_CHIPS, FF_S, D_MODEL),
                   o_dtype=bf16, dims=TN, acc_shape=(FF_S, D_MODEL), k_axis=1, sem=("parallel", "arbitrary"))

    d_w_gate = d_w_ff("d_w_gate", da)
    d_w_up = d_w_ff("d_w_up", db)
    part = dict(w_ffn_gate=d_w_gate, w_ffn_up=d_w_up, w_ffn_down=d_w_down)

    def pair_sums(names, got):
        return [_add_pairs(part[n], g_, core, f"pair_sum_{n}") for n, g_ in zip(names, got)]

    dh2, got = _ffn_dh2(da, w_gate, db, w_up, comm=_x_pair_exchange([part[n] for n in REDUCE_A]))
    sums_a = pair_sums(REDUCE_A, got)
    dx1, d_g2 = _rms_bwd(dh2, x1, norm2_g, dx2, "rms2_bwd")

    d_w_o = _mm_tn_tokens("d_w_o", merged, dx1, bf16).reshape(N_CHIPS, D_MODEL // N_CHIPS, D_MODEL)
    dmerged = _mm_nt_full("d_merged", dx1, w_o_f, bf16)
    dya, dyc, dga, dgc, d_gba, d_gbc = _merge_bwd(dmerged, proj, gate_b, y_attn, y_conv)

    d_w_conv_out = _mm_tn_tokens("d_w_conv_out", c3, dyc, bf16).reshape(N_CHIPS, D_MODEL // N_CHIPS, D_MODEL)
    dc3 = _mm_nt_full("d_c3", dyc, w_conv_out_f, bf16)
    dc1, d_ln_g, d_ln_b = _ln_silu_bwd(dc3, c1, conv_ln_g, conv_ln_b)
    (dua, dub, d_conv_w, d_conv_b), got_a = _glu_conv_bwd(dc1.reshape(B, SEQ, D_MODEL), proj3, conv_w, comm=_x_chip_exchange(sums_a))

    d_w_attn_out = _mm("d_w_attn_out", mix, dya, grid=(N_CHIPS, T // 512),
                       a_spec=pl.BlockSpec((512, GW), lambda s, k: (k, 0)),
                       b_spec=pl.BlockSpec((512, D_MODEL // N_CHIPS), lambda s, k: (k, s)),
                       o_spec=pl.BlockSpec((None, GW, D_MODEL // N_CHIPS), lambda s, k: (s, 0, 0)),
                       o_shape=(N_CHIPS, GW, D_MODEL // N_CHIPS), o_dtype=bf16, dims=TN, acc_shape=(GW, D_MODEL // N_CHIPS),
                       k_axis=1, sem=("parallel", "arbitrary"))
    part.update(w_conv_out=d_w_conv_out, w_attn_out=d_w_attn_out, w_o=d_w_o)
    dmix, got = _mm("d_mix", dya, w_attn_out, grid=(T // 1024, N_CHIPS),
                    a_spec=pl.BlockSpec((1024, D_MODEL // N_CHIPS), lambda m, s: (m, s)),
                    b_spec=pl.BlockSpec((None, GW, D_MODEL // N_CHIPS), lambda m, s: (s, 0, 0)),
                    o_spec=pl.BlockSpec((1024, GW), lambda m, s: (m, 0)), o_shape=(T, GW), o_dtype=bf16, dims=NT,
                    acc_shape=(1024, GW), k_axis=1, comm=_x_pair_exchange([part[n] for n in REDUCE_B]))
    sums_b = pair_sums(REDUCE_B, got)
    dmix3 = dmix.reshape(B, SEQ, GW)
    delta3, delta_r1, delta_r2, dmix_r1, dmix_r2 = _attn_delta(dmix3, mix3)
    one = (B, 1, SEQ)
    (dq0, dk0, dv0), got_b = _attn_bwd(0, *qkv_args(0), dmix3.reshape(one + (GW,)), lse3.reshape(one + (LANES,)),
                                       delta3.reshape(one + (LANES,)), comm=_x_chip_exchange(sums_b))
    dqkv = [tuple(t.reshape(B, SEQ, GW) for t in (dq0, dk0, dv0)),
            _attn_bwd(1, *qkv_args(1), dmix_r1, lse_r1, delta_r1), _attn_bwd(2, *qkv_args(2), dmix_r2, lse_r2, delta_r2)]
    dproj = _assemble_dproj(dqkv, dua, dub, dga.reshape(B, SEQ, D_MODEL), dgc.reshape(B, SEQ, D_MODEL)).reshape(T, IN_W)

    d_w_in = _mm("d_w_in", h_t, dproj, grid=(N_CHIPS, T // 1024),
                 a_spec=pl.BlockSpec((D_MODEL, 1024), lambda s, k: (0, k)),
                 b_spec=pl.BlockSpec((1024, IN_S), lambda s, k: (k, s)),
                 o_spec=pl.BlockSpec((None, D_MODEL, IN_S), lambda s, k: (s, 0, 0)), o_shape=(N_CHIPS, D_MODEL, IN_S),
                 o_dtype=bf16, dims=NN, acc_shape=(D_MODEL, IN_S), k_axis=1, sem=("parallel", "arbitrary"))
    part.update(w_in=d_w_in)
    sums_c = pair_sums(REDUCE_C, _run_exchange(_x_pair_exchange([d_w_in]), "grad_pair_exchange_c"))
    dh, got_c = _mm("d_h", dproj, w_in, grid=(T // 1024, N_CHIPS),
                    a_spec=pl.BlockSpec((1024, IN_S), lambda m, s: (m, s)),
                    b_spec=pl.BlockSpec((None, D_MODEL, IN_S), lambda m, s: (s, 0, 0)),
                    o_spec=pl.BlockSpec((1024, D_MODEL), lambda m, s: (m, 0)), o_shape=(T, D_MODEL), o_dtype=f32, dims=NT,
                    acc_shape=(1024, D_MODEL), k_axis=1, comm=_x_chip_exchange(sums_c))
    dx, d_g1 = _rms_bwd(dh, xf, norm1_g, dx1, "rms1_bwd")

    names = REDUCE_A + REDUCE_B + REDUCE_C
    halves = [_sum4(s, g_, chip1, core, f"chip_sum_{n}") for n, s, g_ in zip(names, sums_a + sums_b + sums_c, got_a + got_b + got_c)]
    big = dict(zip(names, _pair_gather(halves)))
    small = dict(norm1_g=d_g1, gate_b=jnp.concatenate([d_gba, d_gbc], axis=-1), conv_b=d_conv_b, conv_ln_g=d_ln_g,
                 conv_ln_b=d_ln_b, norm2_g=d_g2, norm_f_g=d_gf, conv_w=d_conv_w)
    return loss, dx.reshape(B, SEQ, D_MODEL), big, small


BIG = ("w_in", "w_conv_out", "w_attn_out", "w_o", "w_ffn_gate", "w_ffn_up", "w_ffn_down")
TRANSPOSED = ("w_ffn_gate", "w_ffn_up")
SMALL = ("norm1_g", "gate_b", "conv_b", "conv_ln_g", "conv_ln_b", "norm2_g", "norm_f_g")
SMALL_ROWS = {"norm1_g": 8, "gate_b": 16, "conv_b": 8, "conv_ln_g": 8, "conv_ln_b": 8, "norm2_g": 8, "norm_f_g": 8}
LOSS_ROWS = 8
CONVW_ROWS = 32 * D_MODEL // LANES


def kernel(x, norm1_g, w_in, gate_b, conv_w, conv_b, conv_ln_g, conv_ln_b, w_conv_out, w_attn_out, w_o, norm2_g, w_ffn_gate, w_ffn_up, w_ffn_down, norm_f_g, loss_target, m_norm1_g, m_w_in, m_gate_b, m_conv_w, m_conv_b, m_conv_ln_g, m_conv_ln_b, m_w_conv_out, m_w_attn_out, m_w_o, m_norm2_g, m_w_ffn_gate, m_w_ffn_up, m_w_ffn_down, m_norm_f_g, v_norm1_g, v_w_in, v_gate_b, v_conv_w, v_conv_b, v_conv_ln_g, v_conv_ln_b, v_w_conv_out, v_w_attn_out, v_w_o, v_norm2_g, v_w_ffn_gate, v_w_ffn_up, v_w_ffn_down, v_norm_f_g):
    W = dict(norm1_g=norm1_g, w_in=w_in, gate_b=gate_b, conv_w=conv_w, conv_b=conv_b, conv_ln_g=conv_ln_g, conv_ln_b=conv_ln_b,
             w_conv_out=w_conv_out, w_attn_out=w_attn_out, w_o=w_o, norm2_g=norm2_g, w_ffn_gate=w_ffn_gate, w_ffn_up=w_ffn_up,
             w_ffn_down=w_ffn_down, norm_f_g=norm_f_g)
    M = dict(norm1_g=m_norm1_g, w_in=m_w_in, gate_b=m_gate_b, conv_w=m_conv_w, conv_b=m_conv_b, conv_ln_g=m_conv_ln_g,
             conv_ln_b=m_conv_ln_b, w_conv_out=m_w_conv_out, w_attn_out=m_w_attn_out, w_o=m_w_o, norm2_g=m_norm2_g,
             w_ffn_gate=m_w_ffn_gate, w_ffn_up=m_w_ffn_up, w_ffn_down=m_w_ffn_down, norm_f_g=m_norm_f_g)
    V = dict(norm1_g=v_norm1_g, w_in=v_w_in, gate_b=v_gate_b, conv_w=v_conv_w, conv_b=v_conv_b, conv_ln_g=v_conv_ln_g,
             conv_ln_b=v_conv_ln_b, w_conv_out=v_w_conv_out, w_attn_out=v_w_attn_out, w_o=v_w_o, norm2_g=v_norm2_g,
             w_ffn_gate=v_w_ffn_gate, w_ffn_up=v_w_ffn_up, w_ffn_down=v_w_ffn_down, norm_f_g=v_norm_f_g)
    order = list(W)

    def as2d(n, a):
        a = a.reshape(a.shape[-2:])
        return a.T if n in TRANSPOSED else a

    def from2d(n, a):
        return (a.T if n in TRANSPOSED else a).reshape(W[n].shape)

    shard2d = {n: as2d(n, W[n]) for n in BIG}
    chip = 2 * lax.axis_index("x") + lax.axis_index("y")

    cw = jnp.zeros((CONV_K, D_MODEL), f32)
    cw = lax.dynamic_update_slice(cw, 0.5 * conv_w.reshape(CONV_K, D_MODEL // N_CHIPS), (0, chip * (D_MODEL // N_CHIPS)))
    conv_w_full = _small_allreduce(_rows128(cw, CONVW_ROWS), "conv_w_gather")[: CONV_K * D_MODEL // LANES].reshape(CONV_K, D_MODEL)

    core = lax.axis_index("c").astype(jnp.int32).reshape(1)
    chip1 = chip.astype(jnp.int32).reshape(1)
    shards = {n: shard2d[n].astype(bf16) for n in BIG}
    slots = {n: lax.dynamic_update_slice(lax.empty((N_CHIPS,) + s.shape, bf16), s[None], (chip, 0, 0)) for n, s in shards.items()}

    loss, grad_x, grads, gsmall = _step(x, loss_target, norm1_g, gate_b, conv_w_full, conv_b, conv_ln_g, conv_ln_b, norm2_g,
                                        norm_f_g, shards, slots, chip1, core)

    pieces = [_rows128(loss, LOSS_ROWS)] + [_rows128(gsmall[n], SMALL_ROWS[n]) for n in SMALL] + [_rows128(gsmall["conv_w"], CONVW_ROWS)]
    tot = _small_allreduce(jnp.concatenate(pieces, axis=0), "small_allreduce")
    loss_out = tot[0, 0]
    row = LOSS_ROWS
    for n in SMALL:
        grads[n] = tot[row: row + W[n].size // LANES].reshape(W[n].shape)
        row += SMALL_ROWS[n]
    dcw = tot[row: row + CONV_K * D_MODEL // LANES].reshape(CONV_K, D_MODEL)
    grads["conv_w"] = lax.dynamic_slice(dcw, (0, chip * (D_MODEL // N_CHIPS)), (CONV_K, D_MODEL // N_CHIPS))

    delta, new_m, new_v = {}, {}, {}
    for n in BIG:
        d, nm, nv = _adamw(shard2d[n], grads[n], as2d(n, M[n]), as2d(n, V[n]), f"adamw_{n}")
        delta[n], new_m[n], new_v[n], grads[n] = (from2d(n, t) for t in (d, nm, nv, grads[n]))

    def pack(src):
        return jnp.concatenate([_rows128(src[n], SMALL_ROWS[n]) for n in SMALL], axis=0)

    d, nm, nv = _adamw(pack(W), pack(grads), pack(M), pack(V), "adamw_small")
    row = 0
    for n in SMALL:
        k = W[n].size // LANES
        delta[n], new_m[n], new_v[n] = (t[row: row + k].reshape(W[n].shape) for t in (d, nm, nv))
        row += SMALL_ROWS[n]

    def pad32(a):
        return jnp.pad(a.reshape(CONV_K, D_MODEL // N_CHIPS), ((0, 1), (0, 0)))

    d, nm, nv = _adamw(pad32(conv_w), pad32(grads["conv_w"]), pad32(m_conv_w), pad32(v_conv_w), "adamw_conv_w")
    delta["conv_w"], new_m["conv_w"], new_v["conv_w"] = (t[:CONV_K].reshape(conv_w.shape) for t in (d, nm, nv))
    grads["conv_w"] = grads["conv_w"].reshape(conv_w.shape)

    return (loss_out, grad_x, *[grads[n] for n in order], *[delta[n] for n in order],
            *[new_m[n] for n in order], *[new_v[n] for n in order])
```

```python
import functools
import math
from typing import Callable, NamedTuple

import numpy as np
import jax
import jax.numpy as jnp
from jax import lax
from jax.experimental import pallas as pl
from jax.experimental.pallas import tpu as pltpu

f32 = jnp.float32
bf16 = jnp.bfloat16
SDS = jax.ShapeDtypeStruct
MESH = pl.DeviceIdType.MESH

D_MODEL = 1024
SEQ = 2048
HEAD_DIM = 64
HEADS = 8
GROUPS = ((128, 1), (512, 4), (2048, 16))
GW = HEADS * HEAD_DIM
ATTN_W = len(GROUPS) * GW
Q_BLOCK = 128
CONV_K = 31
D_FF = 2816
IN_W = 3 * ATTN_W + 2 * D_MODEL + 2 * D_MODEL
N_CHIPS = 4
IN_S = IN_W // N_CHIPS
FF_S = D_FF // N_CHIPS
RMS_EPS = 1e-6
LN_EPS = 1e-5
LR, B1, B2, ADAM_EPS, WD, STEP = 0.001, 0.9, 0.999, 1e-08, 0.01, 10
NEG = -1e30
LANES = 128
VMEM_LIMIT = 52 * 2 ** 20
CB = 512
UA_CB, UB_CB, GA_CB, GC_CB = 9, 11, 13, 15


def _alibi_slope_list(n):
    def pow2(m):
        start = 2.0 ** (-8.0 / m)
        return [start ** (i + 1) for i in range(m)]
    if math.log2(n).is_integer():
        return pow2(n)
    c = 2 ** math.floor(math.log2(n))
    return pow2(c) + _alibi_slope_list(2 * c)[0::2][: n - c]


_SLOPES = np.asarray(sorted(_alibi_slope_list(len(GROUPS) * HEADS), reverse=True), dtype=np.float32).reshape(len(GROUPS), HEADS)


def _cp(sem=None, vmem=VMEM_LIMIT):
    return pltpu.CompilerParams(dimension_semantics=sem, vmem_limit_bytes=vmem)


def _sigmoid(x):
    return 1.0 / (1.0 + jnp.exp(-x))


HBM = pl.BlockSpec(memory_space=pl.ANY)


class _Exchange(NamedTuple):
    ins: list
    out_shape: list
    scratch: list
    aliases: dict
    start: Callable
    wait: Callable


def _launch(body, *, name, grid, in_specs, out_specs, out_shape, args, scratch_shapes=(), sem=None, comm=None):
    if comm is None:
        return pl.pallas_call(body, grid=grid, in_specs=in_specs, out_specs=out_specs, out_shape=out_shape,
                              scratch_shapes=list(scratch_shapes), name=name, compiler_params=_cp(sem))(*args)
    multi = isinstance(out_shape, (tuple, list))
    m_out = list(out_shape) if multi else [out_shape]
    m_ospec = list(out_specs) if multi else [out_specs]
    n_in, n_out, n_scr = len(in_specs), len(m_out), len(scratch_shapes)
    nc_in, nc_out = len(comm.ins), len(comm.out_shape)

    def hosted(*refs):
        bounds = np.cumsum([0, n_in, nc_in, n_out, nc_out, n_scr])
        mi, ci, mo, co, ms = (refs[a:b] for a, b in zip(bounds[:-1], bounds[1:]))
        cs = refs[bounds[-1]:]
        ids = [pl.program_id(a) for a in range(len(grid))]
        first = functools.reduce(jnp.logical_and, [i == 0 for i in ids])
        last = functools.reduce(jnp.logical_and, [i == g - 1 for i, g in zip(ids, grid)])

        @pl.when(first)
        def _():
            comm.start(ci, co, cs)

        body(*mi, *mo, *ms)

        @pl.when(last)
        def _():
            comm.wait(ci, co, cs)

    res = pl.pallas_call(
        hosted, grid=grid, in_specs=list(in_specs) + [HBM] * nc_in, out_specs=m_ospec + [HBM] * nc_out,
        out_shape=tuple(m_out) + tuple(comm.out_shape), scratch_shapes=list(scratch_shapes) + list(comm.scratch),
        input_output_aliases={n_in + i: n_out + o for i, o in comm.aliases.items()}, name=name + "_comm",
        compiler_params=pltpu.CompilerParams(dimension_semantics=("arbitrary",) * len(grid), vmem_limit_bytes=VMEM_LIMIT,
                                             has_side_effects=True))(*args, *comm.ins)
    return (tuple(res[:n_out]) if multi else res[0]), tuple(res[n_out:])


def _run_exchange(ex, name):
    n_in = len(ex.ins)

    def body(*refs):
        ins, outs, sems = refs[:n_in], refs[n_in:n_in + len(ex.out_shape)], refs[n_in + len(ex.out_shape):]
        ex.start(ins, outs, sems)
        ex.wait(ins, outs, sems)

    return pl.pallas_call(body, in_specs=[HBM] * n_in, out_specs=[HBM] * len(ex.out_shape), out_shape=tuple(ex.out_shape),
                          scratch_shapes=list(ex.scratch), input_output_aliases=dict(ex.aliases), name=name,
                          compiler_params=pltpu.CompilerParams(has_side_effects=True))(*ex.ins)


def _rmsnorm(xv, gv):
    return xv * lax.rsqrt(jnp.mean(xv * xv, axis=-1, keepdims=True) + RMS_EPS) * gv


def _rms_fwd(x, g, name):
    T = x.shape[0]
    tm = 512

    def body(x_ref, g_ref, o_ref, ot_ref):
        hv = _rmsnorm(x_ref[...], g_ref[...])
        o_ref[...] = hv.astype(bf16)
        ot_ref[...] = hv.T.astype(bf16)

    row = pl.BlockSpec((tm, D_MODEL), lambda i: (i, 0))
    vec = pl.BlockSpec((1, D_MODEL), lambda i: (0, 0))
    return pl.pallas_call(body, grid=(T // tm,), in_specs=[row, vec], out_specs=[row, pl.BlockSpec((D_MODEL, tm), lambda i: (0, i))],
                          out_shape=(SDS((T, D_MODEL), bf16), SDS((D_MODEL, T), bf16)), name=name,
                          compiler_params=_cp(("parallel",)))(x, g)


def _rms_bwd_tile(dyv, xv, gv, dres):
    r = lax.rsqrt(jnp.mean(xv * xv, axis=-1, keepdims=True) + RMS_EPS)
    xh = xv * r
    dxh = dyv * gv
    dx = dres + r * (dxh - xh * jnp.mean(dxh * xh, axis=-1, keepdims=True))
    return dx, jnp.sum(dyv * xh, axis=0, keepdims=True)


def _accumulate(first, refs_parts):
    @pl.when(first)
    def _():
        for ref, part in refs_parts:
            ref[...] = part

    @pl.when(jnp.logical_not(first))
    def _():
        for ref, part in refs_parts:
            ref[...] += part


def _mm_rms_bwd(name, ops, x, g, dres, *, tm, a_spec, b_spec, dims, comm=None):
    T = x.shape[0]
    n = len(ops)

    def body(*refs):
        ab_refs = refs[:2 * n]
        x_ref, g_ref, r_ref, dx_ref, dg_ref, acc = refs[2 * n:]
        m, s = pl.program_id(0), pl.program_id(1)
        p = sum(lax.dot_general(ab_refs[2 * i][...], ab_refs[2 * i + 1][...], dims, preferred_element_type=f32) for i in range(n))

        @pl.when(s == 0)
        def _():
            acc[...] = p

        @pl.when(s > 0)
        def _():
            acc[...] += p

        @pl.when(s == N_CHIPS - 1)
        def _():
            dx, part = _rms_bwd_tile(acc[...], x_ref[...], g_ref[...], r_ref[...])
            dx_ref[...] = dx
            _accumulate(m == 0, [(dg_ref, part)])

    row = pl.BlockSpec((tm, D_MODEL), lambda m, s: (m, 0))
    vec = pl.BlockSpec((1, D_MODEL), lambda m, s: (0, 0))
    return _launch(body, name=name, grid=(T // tm, N_CHIPS), in_specs=[a_spec, b_spec] * n + [row, vec, row], out_specs=[row, vec],
                   out_shape=(SDS((T, D_MODEL), f32), SDS((1, D_MODEL), f32)), args=tuple(t for ab in ops for t in ab) + (x, g, dres),
                   scratch_shapes=[pltpu.VMEM((tm, D_MODEL), f32)], sem=("arbitrary", "arbitrary"), comm=comm)


def _ffn_down_loss(ff, wd, x1, gf, target):
    T = x1.shape[0]
    tm = 1024
    nm = T // tm

    def body(f_ref, w_ref, x_ref, g_ref, t_ref, loss_ref, dx_ref, dg_ref, acc):
        m, s = pl.program_id(0), pl.program_id(1)
        p = jnp.dot(f_ref[...], w_ref[...], preferred_element_type=f32)

        @pl.when(s == 0)
        def _():
            acc[...] = p

        @pl.when(s > 0)
        def _():
            acc[...] += p

        @pl.when(s == N_CHIPS - 1)
        def _():
            xv = acc[...] + x_ref[...]
            gv = g_ref[...]
            r = lax.rsqrt(jnp.mean(xv * xv, axis=-1, keepdims=True) + RMS_EPS)
            xh = xv * r
            e = xh * gv - t_ref[...]
            part_l = jnp.broadcast_to(0.5 * jnp.sum(jnp.mean(e * e, axis=-1, keepdims=True), axis=0, keepdims=True), (1, LANES))
            dy = e * (1.0 / D_MODEL)
            dxh = dy * gv
            dx_ref[...] = r * (dxh - xh * jnp.mean(dxh * xh, axis=-1, keepdims=True))
            part_g = jnp.sum(dy * xh, axis=0, keepdims=True)

            @pl.when(m == 0)
            def _():
                loss_ref[...] = part_l
                dg_ref[...] = part_g

            @pl.when(m > 0)
            def _():
                loss_ref[...] += part_l
                dg_ref[...] += part_g

    row = pl.BlockSpec((tm, D_MODEL), lambda m, s: (m, 0))
    vec = pl.BlockSpec((1, D_MODEL), lambda m, s: (0, 0))
    return pl.pallas_call(
        body, grid=(nm, N_CHIPS),
        in_specs=[pl.BlockSpec((None, tm, FF_S), lambda m, s: (s, m, 0)), pl.BlockSpec((None, FF_S, D_MODEL), lambda m, s: (s, 0, 0)),
                  row, vec, row],
        out_specs=[pl.BlockSpec((1, LANES), lambda m, s: (0, 0)), row, vec],
        out_shape=(SDS((1, LANES), f32), SDS((T, D_MODEL), f32), SDS((1, D_MODEL), f32)),
        scratch_shapes=[pltpu.VMEM((tm, D_MODEL), f32)], name="ffn_down_loss", compiler_params=_cp(("arbitrary", "arbitrary")))(
            ff, wd, x1, gf, target)


def _o_proj_rms(merged, w_o, x, g2):
    T = x.shape[0]
    tm = 1024

    def body(a_ref, w_ref, x_ref, g_ref, x1_ref, h2_ref):
        x1 = jnp.dot(a_ref[...], w_ref[...], preferred_element_type=f32) + x_ref[...]
        x1_ref[...] = x1
        h2_ref[...] = _rmsnorm(x1, g_ref[...]).astype(bf16)

    row = pl.BlockSpec((tm, D_MODEL), lambda i: (i, 0))
    return pl.pallas_call(body, grid=(T // tm,),
                          in_specs=[row, pl.BlockSpec((D_MODEL, D_MODEL), lambda i: (0, 0)), row, pl.BlockSpec((1, D_MODEL), lambda i: (0, 0))],
                          out_specs=[row, row], out_shape=(SDS((T, D_MODEL), f32), SDS((T, D_MODEL), bf16)), name="o_proj_rms",
                          compiler_params=_cp(("parallel",)))(merged, w_o, x, g2)


def _ln_silu_fwd(c1, g, b):
    T = c1.shape[0]
    tm = 512

    def body(c_ref, g_ref, b_ref, o_ref):
        cv = c_ref[...]
        mu = jnp.mean(cv, axis=-1, keepdims=True)
        cc = cv - mu
        var = jnp.mean(cc * cc, axis=-1, keepdims=True)
        c2 = cc * lax.rsqrt(var + LN_EPS) * g_ref[...] + b_ref[...]
        o_ref[...] = (c2 * _sigmoid(c2)).astype(o_ref.dtype)

    row = pl.BlockSpec((tm, D_MODEL), lambda i: (i, 0))
    vec = pl.BlockSpec((1, D_MODEL), lambda i: (0, 0))
    return pl.pallas_call(body, grid=(T // tm,), in_specs=[row, vec, vec], out_specs=row,
                          out_shape=SDS((T, D_MODEL), bf16), name="ln_silu_fwd", compiler_params=_cp(("parallel",)))(c1, g, b)


def _conv_out_bwd(dyc, w, c1, g, b):
    T = c1.shape[0]
    tm = 1024

    def body(d_ref, w_ref, c_ref, g_ref, b_ref, dc_ref, dg_ref, db_ref):
        dc3 = lax.dot_general(d_ref[...], w_ref[...], NT, preferred_element_type=f32)
        cv = c_ref[...]
        gv = g_ref[...]
        mu = jnp.mean(cv, axis=-1, keepdims=True)
        cc = cv - mu
        var = jnp.mean(cc * cc, axis=-1, keepdims=True)
        rs = lax.rsqrt(var + LN_EPS)
        xh = cc * rs
        c2 = xh * gv + b_ref[...]
        sg = _sigmoid(c2)
        dc2 = dc3 * (sg * (1.0 + c2 * (1.0 - sg)))
        dxh = dc2 * gv
        dc_ref[...] = rs * (dxh - jnp.mean(dxh, axis=-1, keepdims=True) - xh * jnp.mean(dxh * xh, axis=-1, keepdims=True))
        _accumulate(pl.program_id(0) == 0, [(dg_ref, jnp.sum(dc2 * xh, axis=0, keepdims=True)), (db_ref, jnp.sum(dc2, axis=0, keepdims=True))])

    row = pl.BlockSpec((tm, D_MODEL), lambda i: (i, 0))
    vec = pl.BlockSpec((1, D_MODEL), lambda i: (0, 0))
    return pl.pallas_call(body, grid=(T // tm,), in_specs=[row, pl.BlockSpec((D_MODEL, D_MODEL), lambda i: (0, 0)), row, vec, vec],
                          out_specs=[row, vec, vec],
                          out_shape=(SDS((T, D_MODEL), f32), SDS((1, D_MODEL), f32), SDS((1, D_MODEL), f32)),
                          name="conv_out_bwd", compiler_params=_cp(("arbitrary",)))(dyc, w, c1, g, b)


NN = (((1,), (0,)), ((), ()))
NT = (((1,), (1,)), ((), ()))
TN = (((0,), (0,)), ((), ()))


def _mm(name, a, b, *, grid, a_spec, b_spec, o_spec, o_shape, o_dtype, dims, acc_shape=None, k_axis=None,
        res=None, res_spec=None, sem=None, comm=None):
    nk = 1 if k_axis is None else grid[k_axis]

    def body(*refs):
        if res is None:
            a_ref, b_ref, o_ref = refs[:3]
            r_ref, scr = None, refs[3:]
        else:
            a_ref, b_ref, r_ref, o_ref = refs[:4]
            scr = refs[4:]
        p = lax.dot_general(a_ref[...].astype(bf16), b_ref[...].astype(bf16), dims, preferred_element_type=f32)
        if nk == 1:
            if r_ref is not None:
                p = p + r_ref[...]
            o_ref[...] = p.astype(o_dtype)
            return
        acc = scr[0]
        k = pl.program_id(k_axis)

        @pl.when(k == 0)
        def _():
            acc[...] = p

        @pl.when(k > 0)
        def _():
            acc[...] += p

        @pl.when(k == nk - 1)
        def _():
            t = acc[...]
            if r_ref is not None:
                t = t + r_ref[...]
            o_ref[...] = t.astype(o_dtype)

    ins = [a, b] + ([] if res is None else [res])
    specs = [a_spec, b_spec] + ([] if res is None else [res_spec])
    scratch = [] if nk == 1 else [pltpu.VMEM(acc_shape, f32)]
    return _launch(body, name=name, grid=grid, in_specs=specs, out_specs=o_spec, out_shape=SDS(o_shape, o_dtype),
                   args=ins, scratch_shapes=scratch, sem=sem, comm=comm)


def _mm_nn_full(name, a, b, o_dtype, res=None, tm=1024, comm=None):
    T, K = a.shape
    N = b.shape[1]
    return _mm(name, a, b, grid=(T // tm,), a_spec=pl.BlockSpec((tm, K), lambda i: (i, 0)),
               b_spec=pl.BlockSpec((K, N), lambda i: (0, 0)), o_spec=pl.BlockSpec((tm, N), lambda i: (i, 0)),
               o_shape=(T, N), o_dtype=o_dtype, dims=NN, res=res,
               res_spec=None if res is None else pl.BlockSpec((tm, N), lambda i: (i, 0)), sem=("parallel",), comm=comm)


def _mm_nt_full(name, a, b, o_dtype, tm=1024):
    T, N = a.shape
    K = b.shape[0]
    return _mm(name, a, b, grid=(T // tm,), a_spec=pl.BlockSpec((tm, N), lambda i: (i, 0)),
               b_spec=pl.BlockSpec((K, N), lambda i: (0, 0)), o_spec=pl.BlockSpec((tm, K), lambda i: (i, 0)),
               o_shape=(T, K), o_dtype=o_dtype, dims=NT, sem=("parallel",))


def _mm_tn_tokens(name, a, b, o_dtype, tk=1024):
    T, K = a.shape
    N = b.shape[1]
    return _mm(name, a, b, grid=(T // tk,), a_spec=pl.BlockSpec((tk, K), lambda k: (k, 0)),
               b_spec=pl.BlockSpec((tk, N), lambda k: (k, 0)), o_spec=pl.BlockSpec((K, N), lambda k: (0, 0)),
               o_shape=(K, N), o_dtype=o_dtype, dims=TN, acc_shape=(K, N), k_axis=0, sem=("arbitrary",))


def _ffn_up(h2, wg, wu):
    T = h2.shape[0]
    tm = 1024

    def body(h_ref, wg_ref, wu_ref, a_ref, b_ref, f_ref):
        hv = h_ref[...]
        av = lax.dot_general(hv, wg_ref[...], NT, preferred_element_type=f32)
        bv = lax.dot_general(hv, wu_ref[...], NT, preferred_element_type=f32)
        a_ref[...] = av.astype(bf16)
        b_ref[...] = bv.astype(bf16)
        f_ref[...] = (av * _sigmoid(av) * bv).astype(bf16)

    wspec = pl.BlockSpec((None, FF_S, D_MODEL), lambda s, m: (s, 0, 0))
    ospec = pl.BlockSpec((None, tm, FF_S), lambda s, m: (s, m, 0))
    osd = SDS((N_CHIPS, T, FF_S), bf16)
    return pl.pallas_call(body, grid=(N_CHIPS, T // tm),
                          in_specs=[pl.BlockSpec((tm, D_MODEL), lambda s, m: (m, 0)), wspec, wspec],
                          out_specs=[ospec, ospec, ospec], out_shape=(osd, osd, osd), name="ffn_up",
                          compiler_params=_cp(("parallel", "parallel")))(h2, wg, wu)


def _ffn_down_bwd(dx2, wd, a, b):
    T = dx2.shape[0]
    tm = 1024

    def body(d_ref, w_ref, a_ref, b_ref, da_ref, db_ref):
        df = lax.dot_general(d_ref[...].astype(bf16), w_ref[...], NT, preferred_element_type=f32)
        av = a_ref[...].astype(f32)
        sg = _sigmoid(av)
        da_ref[...] = (df * b_ref[...].astype(f32) * (sg * (1.0 + av * (1.0 - sg)))).astype(bf16)
        db_ref[...] = (df * av * sg).astype(bf16)

    aspec = pl.BlockSpec((None, tm, FF_S), lambda m, s: (s, m, 0))
    osd = SDS((N_CHIPS, T, FF_S), bf16)
    return pl.pallas_call(body, grid=(T // tm, N_CHIPS),
                          in_specs=[pl.BlockSpec((tm, D_MODEL), lambda m, s: (m, 0)),
                                    pl.BlockSpec((None, FF_S, D_MODEL), lambda m, s: (s, 0, 0)), aspec, aspec],
                          out_specs=[aspec, aspec], out_shape=(osd, osd), name="ffn_down_bwd",
                          compiler_params=_cp(("parallel", "parallel")))(dx2, wd, a, b)


def _merge_fwd(proj, gate_b, ya, yc):
    T = proj.shape[0]
    tm = 512

    def body(ga_ref, gc_ref, ba_ref, bc_ref, ya_ref, yc_ref, o_ref):
        sa = _sigmoid(ga_ref[...].astype(f32) + ba_ref[...])
        sc = _sigmoid(gc_ref[...].astype(f32) + bc_ref[...])
        o_ref[...] = (sa * ya_ref[...].astype(f32) + sc * yc_ref[...].astype(f32)).astype(bf16)

    blk = pl.BlockSpec((tm, CB), lambda i, j: (i, j))
    return pl.pallas_call(
        body, grid=(T // tm, 2),
        in_specs=[pl.BlockSpec((tm, CB), lambda i, j: (i, GA_CB + j)), pl.BlockSpec((tm, CB), lambda i, j: (i, GC_CB + j)),
                  pl.BlockSpec((1, CB), lambda i, j: (0, j)), pl.BlockSpec((1, CB), lambda i, j: (0, 2 + j)), blk, blk],
        out_specs=blk, out_shape=SDS((T, D_MODEL), bf16), name="merge_fwd",
        compiler_params=_cp(("parallel", "parallel")))(proj, proj, gate_b, gate_b, ya, yc)


def _merge_bwd(dm, proj, gate_b, ya, yc):
    T = proj.shape[0]
    tm = 512

    def body(dm_ref, ga_ref, gc_ref, ba_ref, bc_ref, ya_ref, yc_ref, dya_ref, dyc_ref, dga_ref, dgc_ref, dba_ref, dbc_ref):
        dmv = dm_ref[...].astype(f32)
        sa = _sigmoid(ga_ref[...].astype(f32) + ba_ref[...])
        sc = _sigmoid(gc_ref[...].astype(f32) + bc_ref[...])
        dya_ref[...] = (dmv * sa).astype(bf16)
        dyc_ref[...] = (dmv * sc).astype(bf16)
        dga = dmv * ya_ref[...].astype(f32) * sa * (1.0 - sa)
        dgc = dmv * yc_ref[...].astype(f32) * sc * (1.0 - sc)
        dga_ref[...] = dga.astype(bf16)
        dgc_ref[...] = dgc.astype(bf16)
        pa = jnp.sum(dga, axis=0, keepdims=True)
        pc = jnp.sum(dgc, axis=0, keepdims=True)

        @pl.when(pl.program_id(1) == 0)
        def _():
            dba_ref[...] = pa
            dbc_ref[...] = pc

        @pl.when(pl.program_id(1) > 0)
        def _():
            dba_ref[...] += pa
            dbc_ref[...] += pc

    blk = pl.BlockSpec((tm, CB), lambda j, i: (i, j))
    vec = pl.BlockSpec((1, CB), lambda j, i: (0, j))
    big = SDS((T, D_MODEL), bf16)
    small = SDS((1, D_MODEL), f32)
    return pl.pallas_call(
        body, grid=(2, T // tm),
        in_specs=[blk, pl.BlockSpec((tm, CB), lambda j, i: (i, GA_CB + j)), pl.BlockSpec((tm, CB), lambda j, i: (i, GC_CB + j)),
                  vec, pl.BlockSpec((1, CB), lambda j, i: (0, 2 + j)), blk, blk],
        out_specs=[blk, blk, blk, blk, vec, vec], out_shape=(big, big, big, big, small, small), name="merge_bwd",
        compiler_params=_cp(("parallel", "arbitrary")))(dm, proj, proj, gate_b, gate_b, ya, yc)


CONV_TS = 256
CONV_HALO = 32
CONV_RC = 64
CONV_WIN = CONV_TS + CONV_HALO
SUBLANES = 8


def _fill_shifted(win, sh):
    for b in range(1, SUBLANES):
        sh[b - 1] = win[pl.ds(b, CONV_WIN - SUBLANES), :]


def _rows_at(win, sh, row):
    a, b = divmod(row, SUBLANES)
    if b == 0:
        return win[pl.ds(row, CONV_RC), :]
    return sh[b - 1, pl.ds(a * SUBLANES, CONV_RC), :]


def _glu_conv_fwd(proj3, w, bias, comm=None):
    B = proj3.shape[0]
    nt = SEQ // CONV_TS
    hb = CONV_TS // CONV_HALO

    def body(ua_ref, ub_ref, ha_ref, hb_ref, w_ref, b_ref, o_ref, win, sh):
        i = pl.program_id(2)
        c0 = ua_ref[...].astype(f32) * _sigmoid(ub_ref[...].astype(f32))
        halo = ha_ref[...].astype(f32) * _sigmoid(hb_ref[...].astype(f32))
        win[0:CONV_HALO, :] = jnp.where(i > 0, halo, 0.0)
        win[CONV_HALO:, :] = c0
        _fill_shifted(win, sh)
        for r0 in range(0, CONV_TS, CONV_RC):
            acc = jnp.zeros((CONV_RC, CB), f32) + b_ref[...]
            for k in range(CONV_K):
                acc = acc + _rows_at(win, sh, r0 + CONV_HALO - (CONV_K - 1) + k) * w_ref[k:k + 1, :]
            o_ref[r0:r0 + CONV_RC, :] = acc

    def cur(cb):
        return pl.BlockSpec((None, CONV_TS, CB), lambda b, j, i: (b, i, cb + j))

    def prev(cb):
        return pl.BlockSpec((None, CONV_HALO, CB), lambda b, j, i: (b, jnp.maximum(i * hb - 1, 0), cb + j))

    return _launch(
        body, name="glu_conv_fwd", grid=(B, 2, nt),
        in_specs=[cur(UA_CB), cur(UB_CB), prev(UA_CB), prev(UB_CB),
                  pl.BlockSpec((CONV_K, CB), lambda b, j, i: (0, j)), pl.BlockSpec((1, CB), lambda b, j, i: (0, j))],
        out_specs=pl.BlockSpec((None, CONV_TS, CB), lambda b, j, i: (b, i, j)),
        out_shape=SDS((B, SEQ, D_MODEL), f32), args=(proj3, proj3, proj3, proj3, w, bias),
        scratch_shapes=[pltpu.VMEM((CONV_WIN, CB), f32), pltpu.VMEM((SUBLANES - 1, CONV_WIN - SUBLANES, CB), f32)],
        sem=("parallel", "parallel", "parallel"), comm=comm)


def _glu_conv_bwd(dc1, proj3, w, comm=None):
    B = proj3.shape[0]
    nt = SEQ // CONV_TS
    hb = CONV_TS // CONV_HALO

    def body(d_ref, dn_ref, ua_ref, ub_ref, ha_ref, hb_ref, w_ref, dua_ref, dub_ref, dw_ref, db_ref, winc, wind, accw, shc, shd):
        b = pl.program_id(1)
        i = pl.program_id(2)
        first = jnp.logical_and(b == 0, i == 0)
        last = jnp.logical_and(b == B - 1, i == nt - 1)

        @pl.when(first)
        def _():
            accw[...] = jnp.zeros_like(accw)
            db_ref[...] = jnp.zeros_like(db_ref)

        halo = ha_ref[...].astype(f32) * _sigmoid(hb_ref[...].astype(f32))
        winc[0:CONV_HALO, :] = jnp.where(i > 0, halo, 0.0)
        winc[CONV_HALO:, :] = ua_ref[...].astype(f32) * _sigmoid(ub_ref[...].astype(f32))
        wind[0:CONV_TS, :] = d_ref[...]
        wind[CONV_TS:, :] = jnp.where(i < nt - 1, dn_ref[...], 0.0)
        db_ref[...] += jnp.sum(d_ref[...], axis=0, keepdims=True)
        _fill_shifted(winc, shc)
        _fill_shifted(wind, shd)
        for r0 in range(0, CONV_TS, CONV_RC):
            dc0 = jnp.zeros((CONV_RC, CB), f32)
            for k in range(CONV_K):
                dc0 = dc0 + _rows_at(wind, shd, r0 + (CONV_K - 1) - k) * w_ref[k:k + 1, :]
            uav = ua_ref[r0:r0 + CONV_RC, :].astype(f32)
            sg = _sigmoid(ub_ref[r0:r0 + CONV_RC, :].astype(f32))
            dua_ref[r0:r0 + CONV_RC, :] = (dc0 * sg).astype(bf16)
            dub_ref[r0:r0 + CONV_RC, :] = (dc0 * uav * sg * (1.0 - sg)).astype(bf16)
            dv = wind[r0:r0 + CONV_RC, :]
            for k in range(CONV_K):
                prod = dv * _rows_at(winc, shc, r0 + CONV_HALO - (CONV_K - 1) + k)
                accw[k] += jnp.sum(prod.reshape(CONV_RC // 8, 8, CB), axis=0)

        @pl.when(last)
        def _():
            for k in range(CONV_K):
                dw_ref[k:k + 1, :] = jnp.sum(accw[k], axis=0, keepdims=True)
            dw_ref[CONV_K:, :] = jnp.zeros((CONV_HALO - CONV_K, CB), f32)

    def cur(cb):
        return pl.BlockSpec((None, CONV_TS, CB), lambda j, b, i: (b, i, cb + j))

    def prev(cb):
        return pl.BlockSpec((None, CONV_HALO, CB), lambda j, b, i: (b, jnp.maximum(i * hb - 1, 0), cb + j))

    nxt = pl.BlockSpec((None, CONV_HALO, CB), lambda j, b, i: (b, jnp.minimum((i + 1) * hb, SEQ // CONV_HALO - 1), j))
    big = SDS((B, SEQ, D_MODEL), bf16)
    return _launch(
        body, name="glu_conv_bwd", grid=(2, B, nt),
        in_specs=[cur(0), nxt, cur(UA_CB), cur(UB_CB), prev(UA_CB), prev(UB_CB), pl.BlockSpec((CONV_K, CB), lambda j, b, i: (0, j))],
        out_specs=[cur(0), cur(0), pl.BlockSpec((CONV_HALO, CB), lambda j, b, i: (0, j)), pl.BlockSpec((1, CB), lambda j, b, i: (0, j))],
        out_shape=(big, big, SDS((CONV_HALO, D_MODEL), f32), SDS((1, D_MODEL), f32)),
        args=(dc1, dc1, proj3, proj3, proj3, proj3, w),
        scratch_shapes=[pltpu.VMEM((CONV_WIN, CB), f32), pltpu.VMEM((CONV_WIN, CB), f32), pltpu.VMEM((CONV_K, SUBLANES, CB), f32),
                        pltpu.VMEM((SUBLANES - 1, CONV_WIN - SUBLANES, CB), f32),
                        pltpu.VMEM((SUBLANES - 1, CONV_WIN - SUBLANES, CB), f32)],
        sem=("parallel", "arbitrary", "arbitrary"), comm=comm)


def _band(first, dil):
    kw = Q_BLOCK if first else 2 * Q_BLOCK
    qi = lax.broadcasted_iota(jnp.int32, (Q_BLOCK, kw), 0)
    kj = lax.broadcasted_iota(jnp.int32, (Q_BLOCK, kw), 1)
    rel = qi - kj + (0 if first else Q_BLOCK)
    valid = jnp.logical_and(rel >= 0, rel <= Q_BLOCK)
    return valid, rel.astype(f32) * float(dil)


def _bias(first, dil, slope):
    valid, dist = _band(first, dil)
    return jnp.where(valid, -slope * dist, NEG)


def _scores(q, k, bias):
    return lax.dot_general(q, k, NT, preferred_element_type=f32) * (HEAD_DIM ** -0.5) + bias


def _pair_cols(hp):
    return slice(hp * LANES, (hp + 1) * LANES)


def _half(x2, e):
    lane = lax.broadcasted_iota(jnp.int32, (1, LANES), 1)
    keep = (lane < HEAD_DIM) if e == 0 else (lane >= HEAD_DIM)
    return jnp.where(keep, x2, jnp.zeros_like(x2))


def _attn_fwd(g, q4, k4, v4, qcb, kcb, vcb, comm=None):
    _, dil = GROUPS[g]
    B, r, L, _ = q4.shape
    nb = L // Q_BLOCK
    slopes = [float(s) for s in _SLOPES[g]]

    def body(q_ref, k_ref, v_ref, o_ref, lse_ref, bias, s_scr, p_scr):
        lane = lax.broadcasted_iota(jnp.int32, (Q_BLOCK, LANES), 1)
        if nb > 1:
            for h in range(HEADS):
                bias[h] = _bias(False, dil, slopes[h])

        def block(n, first):
            q0 = 0 if first else pl.multiple_of(n * Q_BLOCK, Q_BLOCK)
            k0 = 0 if first else pl.multiple_of((n - 1) * Q_BLOCK, Q_BLOCK)
            kw = Q_BLOCK if first else 2 * Q_BLOCK
            for hp in range(HEADS // 2):
                q2 = q_ref[pl.ds(q0, Q_BLOCK), _pair_cols(hp)]
                k2 = k_ref[pl.ds(k0, kw), _pair_cols(hp)]
                for e in range(2):
                    h = 2 * hp + e
                    b_h = _bias(True, dil, slopes[h]) if first else bias[h]
                    s_scr[h, :, :kw] = _scores(_half(q2, e), k2, b_h)
            st = jnp.zeros((Q_BLOCK, LANES), f32)
            dens = jnp.ones((Q_BLOCK, LANES), f32)
            for h in range(HEADS):
                s = s_scr[h, :, :kw]
                m = jnp.max(s, axis=-1, keepdims=True)
                p = jnp.exp(s - m)
                den = jnp.sum(p, axis=-1, keepdims=True)
                p_scr[h, :, :kw] = p.astype(bf16)
                st = jnp.where(lane == h, m + jnp.log(den), st)
                dens = jnp.where(lane == h, den, dens)
            lse_ref[pl.ds(q0, Q_BLOCK), :] = st
            inv = 1.0 / dens
            for hp in range(HEADS // 2):
                v2 = v_ref[pl.ds(k0, kw), _pair_cols(hp)]
                o2 = sum(jnp.dot(p_scr[2 * hp + e, :, :kw], _half(v2, e), preferred_element_type=f32) * inv[:, 2 * hp + e:2 * hp + e + 1]
                         for e in range(2))
                o_ref[pl.ds(q0, Q_BLOCK), _pair_cols(hp)] = o2.astype(bf16)

        block(0, True)
        if nb > 1:
            def step(n, carry):
                block(n, False)
                return carry
            lax.fori_loop(1, nb, step, 0)

    def spec(cb):
        return pl.BlockSpec((None, None, L, GW), lambda b, c: (b, c, 0, cb))

    return _launch(
        body, name=f"attn_fwd_g{g}", grid=(B, r), in_specs=[spec(qcb), spec(kcb), spec(vcb)],
        out_specs=[spec(0), pl.BlockSpec((None, None, L, LANES), lambda b, c: (b, c, 0, 0))],
        out_shape=(SDS((B, r, L, GW), bf16), SDS((B, r, L, LANES), f32)), args=(q4, k4, v4),
        scratch_shapes=[pltpu.VMEM((HEADS, Q_BLOCK, 2 * Q_BLOCK), f32), pltpu.VMEM((HEADS, Q_BLOCK, 2 * Q_BLOCK), f32),
                        pltpu.VMEM((HEADS, Q_BLOCK, 2 * Q_BLOCK), bf16)],
        sem=("parallel", "parallel"), comm=comm)


def _attn_bwd(g, q4, k4, v4, qcb, kcb, vcb, do4, lse4, dl4, comm=None):
    _, dil = GROUPS[g]
    B, r, L, _ = q4.shape
    nb = L // Q_BLOCK
    slopes = [float(s) for s in _SLOPES[g]]
    scale = HEAD_DIM ** -0.5

    def body(q_ref, k_ref, v_ref, do_ref, lse_ref, dl_ref, dq_ref, dk_ref, dv_ref, dk_acc, dv_acc, bias, s_scr, dp_scr, p_scr, ds_scr):
        dk_acc[...] = jnp.zeros_like(dk_acc)
        dv_acc[...] = jnp.zeros_like(dv_acc)
        if nb > 1:
            for h in range(HEADS):
                bias[h] = _bias(False, dil, slopes[h])

        def block(n, first):
            q0 = 0 if first else pl.multiple_of(n * Q_BLOCK, Q_BLOCK)
            k0 = 0 if first else pl.multiple_of((n - 1) * Q_BLOCK, Q_BLOCK)
            kw = Q_BLOCK if first else 2 * Q_BLOCK
            for hp in range(HEADS // 2):
                q2 = q_ref[pl.ds(q0, Q_BLOCK), _pair_cols(hp)]
                k2 = k_ref[pl.ds(k0, kw), _pair_cols(hp)]
                v2 = v_ref[pl.ds(k0, kw), _pair_cols(hp)]
                do2 = do_ref[pl.ds(q0, Q_BLOCK), _pair_cols(hp)]
                for e in range(2):
                    h = 2 * hp + e
                    b_h = _bias(True, dil, slopes[h]) if first else bias[h]
                    s_scr[h, :, :kw] = _scores(_half(q2, e), k2, b_h)
                    dp_scr[h, :, :kw] = lax.dot_general(_half(do2, e), v2, NT, preferred_element_type=f32)
            for h in range(HEADS):
                p = jnp.exp(s_scr[h, :, :kw] - lse_ref[pl.ds(q0, Q_BLOCK), h:h + 1])
                p_scr[h, :, :kw] = p.astype(bf16)
                ds_scr[h, :, :kw] = (p * (dp_scr[h, :, :kw] - dl_ref[pl.ds(q0, Q_BLOCK), h:h + 1])).astype(bf16)
            for hp in range(HEADS // 2):
                cols = _pair_cols(hp)
                q2 = q_ref[pl.ds(q0, Q_BLOCK), cols]
                k2 = k_ref[pl.ds(k0, kw), cols]
                do2 = do_ref[pl.ds(q0, Q_BLOCK), cols]
                ds = [ds_scr[2 * hp + e, :, :kw] for e in range(2)]
                dq2 = sum(jnp.dot(ds[e], _half(k2, e), preferred_element_type=f32) for e in range(2))
                dq_ref[pl.ds(q0, Q_BLOCK), cols] = (dq2 * scale).astype(bf16)
                dk2 = sum(lax.dot_general(ds[e], _half(q2, e), TN, preferred_element_type=f32) for e in range(2))
                dk_acc[pl.ds(k0, kw), cols] += dk2 * scale
                dv2 = sum(lax.dot_general(p_scr[2 * hp + e, :, :kw], _half(do2, e), TN, preferred_element_type=f32) for e in range(2))
                dv_acc[pl.ds(k0, kw), cols] += dv2

        block(0, True)
        if nb > 1:
            def step(n, carry):
                block(n, False)
                return carry
            lax.fori_loop(1, nb, step, 0)
        dk_ref[...] = dk_acc[...].astype(bf16)
        dv_ref[...] = dv_acc[...].astype(bf16)

    def spec(cb):
        return pl.BlockSpec((None, None, L, GW), lambda b, c: (b, c, 0, cb))

    st = pl.BlockSpec((None, None, L, LANES), lambda b, c: (b, c, 0, 0))
    osd = SDS((B, r, L, GW), bf16)
    return _launch(
        body, name=f"attn_bwd_g{g}", grid=(B, r), in_specs=[spec(qcb), spec(kcb), spec(vcb), spec(0), st, st],
        out_specs=[spec(0), spec(0), spec(0)], out_shape=(osd, osd, osd), args=(q4, k4, v4, do4, lse4, dl4),
        scratch_shapes=[pltpu.VMEM((L, GW), f32), pltpu.VMEM((L, GW), f32)]
        + [pltpu.VMEM((HEADS, Q_BLOCK, 2 * Q_BLOCK), f32)] * 3 + [pltpu.VMEM((HEADS, Q_BLOCK, 2 * Q_BLOCK), bf16)] * 2,
        sem=("parallel", "parallel"), comm=comm)


RT = 512
RCH = GW // LANES
DILS = tuple(d for _, d in GROUPS[1:])


def _res_spec(r, width):
    return pl.BlockSpec((None, r, RT // r, width), lambda b, i, *_: (b, 0, i, 0))


def _tok_spec(width, cb=0):
    return pl.BlockSpec((None, RT, width), lambda b, i, *_: (b, i, cb))


def _to_residues(res_ref, scr, r, width):
    for c in range(r):
        for jj in range(width // LANES):
            res_ref[c, :, jj * LANES:(jj + 1) * LANES] = scr[jj, pl.ds(c, RT // r, stride=r), :].astype(res_ref.dtype)


def _from_residues(scr, res_ref, r, width):
    for c in range(r):
        for jj in range(width // LANES):
            scr[jj, pl.ds(c, RT // r, stride=r), :] = res_ref[c, :, jj * LANES:(jj + 1) * LANES].astype(f32)


def _qkv_to_residues(proj3, g):
    r = GROUPS[g][1]
    B = proj3.shape[0]

    def body(q_ref, k_ref, v_ref, o_ref, scr):
        for p, x_ref in enumerate((q_ref, k_ref, v_ref)):
            x = x_ref[...].astype(f32)
            for jj in range(RCH):
                scr[p * RCH + jj] = x[:, jj * LANES:(jj + 1) * LANES]
        _to_residues(o_ref, scr, r, ATTN_W)

    return pl.pallas_call(
        body, grid=(B, SEQ // RT), in_specs=[_tok_spec(GW, 3 * p + g) for p in range(3)], out_specs=_res_spec(r, ATTN_W),
        out_shape=SDS((B, r, SEQ // r, ATTN_W), bf16), scratch_shapes=[pltpu.VMEM((3 * RCH, RT, LANES), f32)],
        name=f"qkv_to_residues_g{g}", compiler_params=_cp(("parallel", "parallel")))(proj3, proj3, proj3)


def _attn_mix(o0, l0, o_res, l_res):
    B = o0.shape[0]

    def body(o0_ref, l0_ref, o1_ref, o2_ref, l1_ref, l2_ref, y_ref, lt_ref, lt1_ref, lt2_ref, so, sl):
        for gi, (o_ref, l_ref, r) in enumerate(((o1_ref, l1_ref, DILS[0]), (o2_ref, l2_ref, DILS[1]))):
            _from_residues(so.at[gi], o_ref, r, GW)
            _from_residues(sl.at[gi:gi + 1], l_ref, r, LANES)
        ls = [l0_ref[...], sl[0], sl[1]]
        m = functools.reduce(jnp.maximum, ls)
        ws = [jnp.exp(l - m) for l in ls]
        den = ws[0] + ws[1] + ws[2]
        alphas = [w / den for w in ws]
        lt = m + jnp.log(den)
        lt_ref[...] = lt
        sl[2] = lt
        _to_residues(lt1_ref, sl.at[2:3], DILS[0], LANES)
        _to_residues(lt2_ref, sl.at[2:3], DILS[1], LANES)
        for h in range(HEADS):
            cols = slice(h * HEAD_DIM, (h + 1) * HEAD_DIM)
            jj, lo = divmod(h * HEAD_DIM, LANES)
            acc = alphas[0][:, h:h + 1] * o0_ref[:, cols].astype(f32)
            for gi in range(2):
                acc = acc + alphas[gi + 1][:, h:h + 1] * so[gi, jj, :, lo:lo + HEAD_DIM]
            y_ref[:, cols] = acc.astype(bf16)

    in_specs = [_tok_spec(GW), _tok_spec(LANES), _res_spec(DILS[0], GW), _res_spec(DILS[1], GW), _res_spec(DILS[0], LANES), _res_spec(DILS[1], LANES)]
    out_specs = [_tok_spec(GW), _tok_spec(LANES), _res_spec(DILS[0], LANES), _res_spec(DILS[1], LANES)]
    return pl.pallas_call(
        body, grid=(B, SEQ // RT), in_specs=in_specs, out_specs=out_specs,
        out_shape=(SDS((B, SEQ, GW), bf16), SDS((B, SEQ, LANES), f32)) + tuple(SDS((B, r, SEQ // r, LANES), f32) for r in DILS),
        scratch_shapes=[pltpu.VMEM((2, RCH, RT, LANES), f32), pltpu.VMEM((3, RT, LANES), f32)],
        name="attn_mix", compiler_params=_cp(("parallel", "parallel")))(o0, l0, *o_res, *l_res)


def _attn_delta(dmix, mix):
    B = dmix.shape[0]

    def body(d_ref, y_ref, dl_ref, dl1_ref, dl2_ref, dm1_ref, dm2_ref, sx, sd):
        lane = lax.broadcasted_iota(jnp.int32, (RT, LANES), 1)
        acc = jnp.zeros((RT, LANES), f32)
        dv = d_ref[...].astype(f32)
        for jj in range(RCH):
            sx[jj] = dv[:, jj * LANES:(jj + 1) * LANES]
        for h in range(HEADS):
            cols = slice(h * HEAD_DIM, (h + 1) * HEAD_DIM)
            dl = jnp.sum(dv[:, cols] * y_ref[:, cols].astype(f32), axis=-1, keepdims=True)
            acc = jnp.where(lane == h, dl, acc)
        dl_ref[...] = acc
        sd[0] = acc
        _to_residues(dl1_ref, sd, DILS[0], LANES)
        _to_residues(dl2_ref, sd, DILS[1], LANES)
        _to_residues(dm1_ref, sx, DILS[0], GW)
        _to_residues(dm2_ref, sx, DILS[1], GW)

    return pl.pallas_call(
        body, grid=(B, SEQ // RT), in_specs=[_tok_spec(GW), _tok_spec(GW)],
        out_specs=[_tok_spec(LANES), _res_spec(DILS[0], LANES), _res_spec(DILS[1], LANES), _res_spec(DILS[0], GW), _res_spec(DILS[1], GW)],
        out_shape=(SDS((B, SEQ, LANES), f32),) + tuple(SDS((B, r, SEQ // r, LANES), f32) for r in DILS)
        + tuple(SDS((B, r, SEQ // r, GW), bf16) for r in DILS),
        scratch_shapes=[pltpu.VMEM((RCH, RT, LANES), f32), pltpu.VMEM((1, RT, LANES), f32)],
        name="attn_delta", compiler_params=_cp(("parallel", "parallel")))(dmix, mix)


N_CB = IN_W // CB


def _assemble_dproj(dqkv, dua, dub, dga, dgc):
    B = dua.shape[0]
    ng = len(GROUPS)
    flat = [dqkv[g][p] for p in range(3) for g in range(ng)]
    wide = [dua, dub, dga, dgc]

    def body(*refs):
        srcs, wides, o_ref, scr = refs[:3 * ng], refs[3 * ng:3 * ng + 4], refs[3 * ng + 4], refs[3 * ng + 5]
        for jv in range(3 * ng):
            g = jv % ng
            if g == 0:
                o_ref[:, jv * GW:(jv + 1) * GW] = srcs[jv][...]
            else:
                _from_residues(scr, srcs[jv], GROUPS[g][1], GW)
                for jj in range(RCH):
                    o_ref[:, jv * GW + jj * LANES:jv * GW + (jj + 1) * LANES] = scr[jj].astype(bf16)
        for wv in range(4):
            lo = 3 * ATTN_W + wv * D_MODEL
            o_ref[:, lo:lo + D_MODEL] = wides[wv][...]

    in_specs = [_tok_spec(GW) if (jv % ng) == 0 else _res_spec(GROUPS[jv % ng][1], GW) for jv in range(3 * ng)]
    in_specs += [_tok_spec(D_MODEL)] * 4
    return pl.pallas_call(
        body, grid=(B, SEQ // RT), in_specs=in_specs, out_specs=_tok_spec(IN_W),
        out_shape=SDS((B, SEQ, IN_W), bf16), scratch_shapes=[pltpu.VMEM((RCH, RT, LANES), f32)],
        name="assemble_dproj", compiler_params=_cp(("parallel", "parallel")))(*flat, *wide)


def _add_pairs(part, got, core, name):
    n, h, C = got.shape

    def body(c_ref, a_ref, b_ref, o_ref):
        o_ref[...] = (a_ref[...].astype(f32) + b_ref[...].astype(f32)).astype(bf16)

    blk = pl.BlockSpec((None, h, C), lambda s, c_ref: (s, 0, 0))
    own = pl.BlockSpec((None, h, C), lambda s, c_ref: (s, c_ref[0], 0))
    spec = pltpu.PrefetchScalarGridSpec(num_scalar_prefetch=1, grid=(n,), in_specs=[own, blk], out_specs=blk)
    return pl.pallas_call(body, grid_spec=spec, out_shape=SDS((n, h, C), bf16), name=name,
                          compiler_params=_cp(("parallel",)))(core, part, got)


def _sum4(sums, got, chip, core, name):
    _, h, C = sums.shape

    def body(s_ref, c_ref, own_ref, q_ref, o_ref):
        t = own_ref[...].astype(f32) + q_ref[0].astype(f32)
        t = t + q_ref[1].astype(f32)
        o_ref[...] = t + q_ref[2].astype(f32)

    spec = pltpu.PrefetchScalarGridSpec(
        num_scalar_prefetch=2, grid=(1,),
        in_specs=[pl.BlockSpec((None, h, C), lambda i, s_ref, c_ref: (s_ref[0], 0, 0)),
                  pl.BlockSpec((N_CHIPS - 1, h, C), lambda i, s_ref, c_ref: (0, 0, 0))],
        out_specs=pl.BlockSpec((h, C), lambda i, s_ref, c_ref: (c_ref[0], 0)))
    return pl.pallas_call(body, grid_spec=spec, out_shape=SDS((2 * h, C), f32), name=name,
                          compiler_params=_cp(("arbitrary",)))(chip, core, sums, got)


def _adamw(w, g, m, v, name):
    R, C = w.shape
    rt = R
    for cand in (512, 256, 128, 64, 32, 16, 8):
        if R % cand == 0 and cand * C * 4 <= 2 ** 21:
            rt = cand
            break
    c1 = 1.0 / (1.0 - B1 ** STEP)
    c2 = 1.0 / (1.0 - B2 ** STEP)

    def body(w_ref, g_ref, m_ref, v_ref, d_ref, nm_ref, nv_ref):
        gv = g_ref[...]
        nm = B1 * m_ref[...] + (1.0 - B1) * gv
        nv = B2 * v_ref[...] + (1.0 - B2) * (gv * gv)
        nm_ref[...] = nm
        nv_ref[...] = nv
        d_ref[...] = -LR * ((nm * c1) / (jnp.sqrt(nv * c2) + ADAM_EPS) + WD * w_ref[...])

    blk = pl.BlockSpec((rt, C), lambda i: (i, 0))
    sd = SDS((R, C), f32)
    return pl.pallas_call(body, grid=(R // rt,), in_specs=[blk] * 4, out_specs=[blk] * 3, out_shape=(sd, sd, sd), name=name,
                          compiler_params=_cp(("parallel",)))(w, g, m, v)


def _coords():
    return lax.axis_index("x"), lax.axis_index("y"), lax.axis_index("c")


def _other_chips(x, y):
    return [(1 - x, y), (x, 1 - y), (1 - x, 1 - y)]


def _allgather_weights(shards, slots):
    n = len(shards)
    halves = [s.shape[0] // 2 for s in shards]

    def body(*refs):
        ins, outs = refs[:n], refs[2 * n:3 * n]
        send1, recv1, send2, recv2 = refs[3 * n:]
        x, y, c = _coords()
        me = 2 * x + y
        chips = _other_chips(x, y)
        sends = []
        for w in range(n):
            mine = pl.ds(c * halves[w], halves[w])
            for j, (px, py) in enumerate(chips):
                cp = pltpu.make_async_remote_copy(src_ref=ins[w].at[mine, :], dst_ref=outs[w].at[me, mine, :],
                                                  send_sem=send1.at[w, j], recv_sem=recv1.at[w, j],
                                                  device_id=(px, py, c), device_id_type=MESH)
                cp.start()
                sends.append(cp)
        for w in range(n):
            mine = pl.ds(c * halves[w], halves[w])
            for j, (px, py) in enumerate(chips):
                blk = outs[w].at[2 * px + py, mine, :]
                pltpu.make_async_remote_copy(src_ref=blk, dst_ref=blk, send_sem=send1.at[w, j], recv_sem=recv1.at[w, j],
                                             device_id=(px, py, c), device_id_type=MESH).wait_recv()
                cp = pltpu.make_async_remote_copy(src_ref=blk, dst_ref=blk, send_sem=send2.at[w, j], recv_sem=recv2.at[w, j],
                                                  device_id=(x, y, 1 - c), device_id_type=MESH)
                cp.start()
                sends.append(cp)
        for w in range(n):
            theirs = pl.ds((1 - c) * halves[w], halves[w])
            for j, (px, py) in enumerate(chips):
                blk = outs[w].at[2 * px + py, theirs, :]
                pltpu.make_async_remote_copy(src_ref=blk, dst_ref=blk, send_sem=send2.at[w, j], recv_sem=recv2.at[w, j],
                                             device_id=(x, y, 1 - c), device_id_type=MESH).wait_recv()
        for cp in sends:
            cp.wait_send()

    return pl.pallas_call(
        body, in_specs=[HBM] * (2 * n), out_specs=[HBM] * n,
        out_shape=tuple(SDS((N_CHIPS,) + s.shape, s.dtype) for s in shards),
        input_output_aliases={n + w: w for w in range(n)},
        scratch_shapes=[pltpu.SemaphoreType.DMA((n, 3))] * 4,
        name="allgather_weights", compiler_params=pltpu.CompilerParams(has_side_effects=True))(*shards, *slots)


def _x_pair_exchange(parts):
    n = len(parts)
    halves = [p.shape[1] // 2 for p in parts]

    def copies(ins, outs, sems):
        send, recv = sems
        x, y, c = _coords()
        return [pltpu.make_async_remote_copy(src_ref=ins[w].at[:, pl.ds((1 - c) * halves[w], halves[w]), :], dst_ref=outs[w],
                                             send_sem=send.at[w], recv_sem=recv.at[w], device_id=(x, y, 1 - c), device_id_type=MESH)
                for w in range(n)]

    def start(ins, outs, sems):
        for cp in copies(ins, outs, sems):
            cp.start()

    def wait(ins, outs, sems):
        for cp in copies(ins, outs, sems):
            cp.wait()

    return _Exchange(ins=list(parts), out_shape=[SDS((N_CHIPS, p.shape[1] // 2, p.shape[2]), p.dtype) for p in parts],
                     scratch=[pltpu.SemaphoreType.DMA((n,))] * 2, aliases={}, start=start, wait=wait)


def _x_chip_exchange(sums):
    n = len(sums)

    def copies(ins, outs, sems):
        send, recv = sems
        x, y, c = _coords()
        return [pltpu.make_async_remote_copy(src_ref=ins[w].at[2 * px + py], dst_ref=outs[w].at[j], send_sem=send.at[w, j],
                                             recv_sem=recv.at[w, j], device_id=(px, py, c), device_id_type=MESH)
                for w in range(n) for j, (px, py) in enumerate(_other_chips(x, y))]

    def start(ins, outs, sems):
        for cp in copies(ins, outs, sems):
            cp.start()

    def wait(ins, outs, sems):
        for cp in copies(ins, outs, sems):
            cp.wait()

    return _Exchange(ins=list(sums), out_shape=[SDS((N_CHIPS - 1,) + s.shape[1:], s.dtype) for s in sums],
                     scratch=[pltpu.SemaphoreType.DMA((n, 3)), pltpu.SemaphoreType.DMA((n, 3))], aliases={}, start=start, wait=wait)


def _x_gather_ici(shards, slots):
    n = len(shards)
    halves = [s.shape[0] // 2 for s in shards]

    def copies(ins, outs, sems):
        send, recv = sems
        x, y, c = _coords()
        me = 2 * x + y
        out = []
        for w in range(n):
            mine = pl.ds(c * halves[w], halves[w])
            for j, (px, py) in enumerate(_other_chips(x, y)):
                snd = pltpu.make_async_remote_copy(src_ref=ins[w].at[mine, :], dst_ref=outs[w].at[me, mine, :], send_sem=send.at[w, j],
                                                   recv_sem=recv.at[w, j], device_id=(px, py, c), device_id_type=MESH)
                got = outs[w].at[2 * px + py, mine, :]
                rcv = pltpu.make_async_remote_copy(src_ref=got, dst_ref=got, send_sem=send.at[w, j], recv_sem=recv.at[w, j],
                                                   device_id=(px, py, c), device_id_type=MESH)
                out.append((snd, rcv))
        return out

    def start(ins, outs, sems):
        for snd, _ in copies(ins, outs, sems):
            snd.start()

    def wait(ins, outs, sems):
        for snd, rcv in copies(ins, outs, sems):
            rcv.wait_recv()
            snd.wait_send()

    return _Exchange(ins=list(shards) + list(slots), out_shape=[SDS(s.shape, s.dtype) for s in slots],
                     scratch=[pltpu.SemaphoreType.DMA((n, 3)), pltpu.SemaphoreType.DMA((n, 3))],
                     aliases={n + w: w for w in range(n)}, start=start, wait=wait)


def _x_gather_fwd(bufs):
    n = len(bufs)
    halves = [b.shape[1] // 2 for b in bufs]

    def copies(ins, outs, sems):
        send, recv = sems
        x, y, c = _coords()
        out = []
        for w in range(n):
            for j, (px, py) in enumerate(_other_chips(x, y)):
                mine = outs[w].at[2 * px + py, pl.ds(c * halves[w], halves[w]), :]
                theirs = outs[w].at[2 * px + py, pl.ds((1 - c) * halves[w], halves[w]), :]
                snd = pltpu.make_async_remote_copy(src_ref=mine, dst_ref=mine, send_sem=send.at[w, j], recv_sem=recv.at[w, j],
                                                   device_id=(x, y, 1 - c), device_id_type=MESH)
                rcv = pltpu.make_async_remote_copy(src_ref=theirs, dst_ref=theirs, send_sem=send.at[w, j], recv_sem=recv.at[w, j],
                                                   device_id=(x, y, 1 - c), device_id_type=MESH)
                out.append((snd, rcv))
        return out

    def start(ins, outs, sems):
        for snd, _ in copies(ins, outs, sems):
            snd.start()

    def wait(ins, outs, sems):
        for snd, rcv in copies(ins, outs, sems):
            rcv.wait_recv()
            snd.wait_send()

    return _Exchange(ins=list(bufs), out_shape=[SDS(b.shape, b.dtype) for b in bufs],
                     scratch=[pltpu.SemaphoreType.DMA((n, 3)), pltpu.SemaphoreType.DMA((n, 3))],
                     aliases={w: w for w in range(n)}, start=start, wait=wait)


def _pair_gather(bufs):
    n = len(bufs)

    def body(*refs):
        outs = refs[n:2 * n]
        send, recv = refs[2 * n:]
        x, y, c = _coords()
        cps = []
        for w in range(n):
            h = bufs[w].shape[0] // 2
            rows = outs[w].at[pl.ds(c * h, h), :]
            rc = pltpu.make_async_remote_copy(src_ref=rows, dst_ref=rows, send_sem=send.at[w], recv_sem=recv.at[w],
                                              device_id=(x, y, 1 - c), device_id_type=MESH)
            rc.start()
            cps.append(rc)
        for w, rc in enumerate(cps):
            h = bufs[w].shape[0] // 2
            other = outs[w].at[pl.ds((1 - c) * h, h), :]
            pltpu.make_async_remote_copy(src_ref=other, dst_ref=other, send_sem=send.at[w], recv_sem=recv.at[w],
                                         device_id=(x, y, 1 - c), device_id_type=MESH).wait_recv()
            rc.wait_send()

    return pl.pallas_call(body, in_specs=[HBM] * n, out_specs=[HBM] * n, out_shape=tuple(SDS(a.shape, a.dtype) for a in bufs),
                          input_output_aliases={w: w for w in range(n)},
                          scratch_shapes=[pltpu.SemaphoreType.DMA((n,))] * 2, name="grad_pair_gather",
                          compiler_params=pltpu.CompilerParams(has_side_effects=True))(*bufs)


def _small_allreduce(vec, name):
    R = vec.shape[0]
    nd = 8

    def body(v_ref, o_ref, buf, send, recv):
        x, y, c = _coords()
        me = 4 * x + 2 * y + c
        buf[me] = v_ref[...]
        cps = []
        for k in range(1, nd):
            kx, ky, kc = (k >> 2) & 1, (k >> 1) & 1, k & 1
            tx = x + kx - 2 * x * kx
            ty = y + ky - 2 * y * ky
            tc = c + kc - 2 * c * kc
            cp = pltpu.make_async_remote_copy(src_ref=v_ref, dst_ref=buf.at[me], send_sem=send.at[k], recv_sem=recv.at[k],
                                              device_id=(tx, ty, tc), device_id_type=MESH)
            cp.start()
            cps.append((cp, 4 * tx + 2 * ty + tc))
        for k, (cp, src) in zip(range(1, nd), cps):
            pltpu.make_async_remote_copy(src_ref=v_ref, dst_ref=buf.at[src], send_sem=send.at[k], recv_sem=recv.at[k],
                                         device_id=(x, y, c), device_id_type=MESH).wait_recv()
        for cp, _ in cps:
            cp.wait_send()
        acc = buf[0]
        for d in range(1, nd):
            acc = acc + buf[d]
        o_ref[...] = acc

    vm = pl.BlockSpec(memory_space=pltpu.VMEM)
    return pl.pallas_call(body, in_specs=[vm], out_specs=vm, out_shape=SDS((R, LANES), f32),
                          scratch_shapes=[pltpu.VMEM((nd, R, LANES), f32), pltpu.SemaphoreType.DMA((nd,)), pltpu.SemaphoreType.DMA((nd,))],
                          name=name, compiler_params=pltpu.CompilerParams(has_side_effects=True))(vec)


def _rows128(a, rows):
    flat = a.reshape(-1)
    return jnp.pad(flat, (0, rows * LANES - flat.shape[0])).reshape(rows, LANES)


GATHER_1 = ("w_conv_out", "w_attn_out", "w_o", "w_ffn_gate")
GATHER_2 = ("w_ffn_up", "w_ffn_down")
REDUCE_A = ("w_ffn_gate", "w_ffn_up", "w_ffn_down")
REDUCE_B = ("w_conv_out", "w_attn_out", "w_o")
REDUCE_C = ("w_in",)


def _step(x, target, norm1_g, gate_b, conv_w, conv_b, conv_ln_g, conv_ln_b, norm2_g, norm_f_g, shards, slots, chip1, core):
    B = x.shape[0]
    T = B * SEQ
    xf = x.reshape(T, D_MODEL)
    tf = target.reshape(T, D_MODEL)

    (w_in,) = _allgather_weights([shards["w_in"]], [slots["w_in"]])
    h, h_t = _rms_fwd(xf, norm1_g, "rms1_fwd")
    proj, got1 = _mm("in_proj", h, w_in, grid=(N_CHIPS, T // 1024),
                     a_spec=pl.BlockSpec((1024, D_MODEL), lambda s, m: (m, 0)),
                     b_spec=pl.BlockSpec((None, D_MODEL, IN_S), lambda s, m: (s, 0, 0)),
                     o_spec=pl.BlockSpec((1024, IN_S), lambda s, m: (m, s)), o_shape=(T, IN_W), o_dtype=bf16, dims=NN,
                     comm=_x_gather_ici([shards[n] for n in GATHER_1], [slots[n] for n in GATHER_1]))
    proj3 = proj.reshape(B, SEQ, IN_W)
    proj4 = proj.reshape(B, 1, SEQ, IN_W)

    qkv = [None] + [_qkv_to_residues(proj3, g) for g in range(1, len(GROUPS))]

    def qkv_args(g):
        return (proj4, proj4, proj4, 0, 3, 6) if g == 0 else (qkv[g], qkv[g], qkv[g], 0, 1, 2)

    (o4_0, l4_0), full1 = _attn_fwd(0, *qkv_args(0), comm=_x_gather_fwd(list(got1)))
    o4_1, l4_1 = _attn_fwd(1, *qkv_args(1))
    o4_2, l4_2 = _attn_fwd(2, *qkv_args(2))
    full = dict(zip(GATHER_1, full1))
    w_conv_out, w_attn_out, w_o, w_gate = (full[n] for n in GATHER_1)
    w_conv_out_f = w_conv_out.reshape(D_MODEL, D_MODEL)
    w_o_f = w_o.reshape(D_MODEL, D_MODEL)
    mix3, lse3, lse_r1, lse_r2 = _attn_mix(o4_0.reshape(B, SEQ, GW), l4_0.reshape(B, SEQ, LANES), [o4_1, o4_2], [l4_1, l4_2])
    mix = mix3.reshape(T, GW)
    y_attn = _mm("attn_out", mix, w_attn_out, grid=(N_CHIPS, T // 512),
                 a_spec=pl.BlockSpec((512, GW), lambda s, m: (m, 0)),
                 b_spec=pl.BlockSpec((None, GW, D_MODEL // N_CHIPS), lambda s, m: (s, 0, 0)),
                 o_spec=pl.BlockSpec((512, D_MODEL // N_CHIPS), lambda s, m: (m, s)), o_shape=(T, D_MODEL), o_dtype=bf16,
                 dims=NN, sem=("parallel", "parallel"))

    c1, got2 = _glu_conv_fwd(proj3, conv_w, conv_b, comm=_x_gather_ici([shards[n] for n in GATHER_2], [slots[n] for n in GATHER_2]))
    c1 = c1.reshape(T, D_MODEL)
    c3 = _ln_silu_fwd(c1, conv_ln_g, conv_ln_b)
    y_conv, (w_up, w_down) = _mm_nn_full("conv_out", c3, w_conv_out_f, bf16, comm=_x_gather_fwd(list(got2)))

    merged = _merge_fwd(proj, gate_b, y_attn, y_conv)
    x1, h2 = _o_proj_rms(merged, w_o_f, xf, norm2_g)
    fa, fb, ff = _ffn_up(h2, w_gate, w_up)
    loss, dx2, d_gf = _ffn_down_loss(ff, w_down, x1, norm_f_g.reshape(1, D_MODEL), tf)

    d_w_down = _mm("d_w_down", ff, dx2, grid=(N_CHIPS, T // 1024),
                   a_spec=pl.BlockSpec((None, 1024, FF_S), lambda s, k: (s, k, 0)),
                   b_spec=pl.BlockSpec((1024, D_MODEL), lambda s, k: (k, 0)),
                   o_spec=pl.BlockSpec((None, FF_S, D_MODEL), lambda s, k: (s, 0, 0)), o_shape=(N_CHIPS, FF_S, D_MODEL),
                   o_dtype=bf16, dims=TN, acc_shape=(FF_S, D_MODEL), k_axis=1, sem=("parallel", "arbitrary"))
    da, db = _ffn_down_bwd(dx2, w_down, fa, fb)

    def d_w_ff(name, dz):
        return _mm(name, dz, h2, grid=(N_CHIPS, T // 2048),
                   a_spec=pl.BlockSpec((None, 2048, FF_S), lambda s, k: (s, k, 0)),
                   b_spec=pl.BlockSpec((2048, D_MODEL), lambda s, k: (k, 0)),
                   o_spec=pl.BlockSpec((None, FF_S, D_MODEL), lambda s, k: (s, 0, 0)), o_shape=(N_CHIPS, FF_S, D_MODEL),
                   o_dtype=bf16, dims=TN, acc_shape=(FF_S, D_MODEL), k_axis=1, sem=("parallel", "arbitrary"))

    d_w_gate = d_w_ff("d_w_gate", da)
    d_w_up = d_w_ff("d_w_up", db)
    part = dict(w_ffn_gate=d_w_gate, w_ffn_up=d_w_up, w_ffn_down=d_w_down)

    def pair_sums(names, got):
        return [_add_pairs(part[n], g_, core, f"pair_sum_{n}") for n, g_ in zip(names, got)]

    (dx1, d_g2), got = _mm_rms_bwd("ffn_dh2", [(da, w_gate), (db, w_up)], x1, norm2_g, dx2, tm=1024,
                                   a_spec=pl.BlockSpec((None, 1024, FF_S), lambda m, s: (s, m, 0)),
                                   b_spec=pl.BlockSpec((None, FF_S, D_MODEL), lambda m, s: (s, 0, 0)), dims=NN,
                                   comm=_x_pair_exchange([part[n] for n in REDUCE_A]))
    sums_a = pair_sums(REDUCE_A, got)

    d_w_o = _mm_tn_tokens("d_w_o", merged, dx1, bf16).reshape(N_CHIPS, D_MODEL // N_CHIPS, D_MODEL)
    dmerged = _mm_nt_full("d_merged", dx1, w_o_f, bf16)
    dya, dyc, dga, dgc, d_gba, d_gbc = _merge_bwd(dmerged, proj, gate_b, y_attn, y_conv)

    d_w_conv_out = _mm_tn_tokens("d_w_conv_out", c3, dyc, bf16).reshape(N_CHIPS, D_MODEL // N_CHIPS, D_MODEL)
    dc1, d_ln_g, d_ln_b = _conv_out_bwd(dyc, w_conv_out_f, c1, conv_ln_g, conv_ln_b)
    (dua, dub, d_conv_w, d_conv_b), got_a = _glu_conv_bwd(dc1.reshape(B, SEQ, D_MODEL), proj3, conv_w, comm=_x_chip_exchange(sums_a))

    d_w_attn_out = _mm("d_w_attn_out", mix, dya, grid=(N_CHIPS, T // 512),
                       a_spec=pl.BlockSpec((512, GW), lambda s, k: (k, 0)),
                       b_spec=pl.BlockSpec((512, D_MODEL // N_CHIPS), lambda s, k: (k, s)),
                       o_spec=pl.BlockSpec((None, GW, D_MODEL // N_CHIPS), lambda s, k: (s, 0, 0)),
                       o_shape=(N_CHIPS, GW, D_MODEL // N_CHIPS), o_dtype=bf16, dims=TN, acc_shape=(GW, D_MODEL // N_CHIPS),
                       k_axis=1, sem=("parallel", "arbitrary"))
    part.update(w_conv_out=d_w_conv_out, w_attn_out=d_w_attn_out, w_o=d_w_o)
    dmix, got = _mm("d_mix", dya, w_attn_out, grid=(T // 1024, N_CHIPS),
                    a_spec=pl.BlockSpec((1024, D_MODEL // N_CHIPS), lambda m, s: (m, s)),
                    b_spec=pl.BlockSpec((None, GW, D_MODEL // N_CHIPS), lambda m, s: (s, 0, 0)),
                    o_spec=pl.BlockSpec((1024, GW), lambda m, s: (m, 0)), o_shape=(T, GW), o_dtype=bf16, dims=NT,
                    acc_shape=(1024, GW), k_axis=1, comm=_x_pair_exchange([part[n] for n in REDUCE_B]))
    sums_b = pair_sums(REDUCE_B, got)
    dmix3 = dmix.reshape(B, SEQ, GW)
    delta3, delta_r1, delta_r2, dmix_r1, dmix_r2 = _attn_delta(dmix3, mix3)
    one = (B, 1, SEQ)
    (dq0, dk0, dv0), got_b = _attn_bwd(0, *qkv_args(0), dmix3.reshape(one + (GW,)), lse3.reshape(one + (LANES,)),
                                       delta3.reshape(one + (LANES,)), comm=_x_chip_exchange(sums_b))
    dqkv = [tuple(t.reshape(B, SEQ, GW) for t in (dq0, dk0, dv0)),
            _attn_bwd(1, *qkv_args(1), dmix_r1, lse_r1, delta_r1), _attn_bwd(2, *qkv_args(2), dmix_r2, lse_r2, delta_r2)]
    dproj = _assemble_dproj(dqkv, dua, dub, dga.reshape(B, SEQ, D_MODEL), dgc.reshape(B, SEQ, D_MODEL)).reshape(T, IN_W)

    d_w_in = _mm("d_w_in", h_t, dproj, grid=(N_CHIPS, T // 1024),
                 a_spec=pl.BlockSpec((D_MODEL, 1024), lambda s, k: (0, k)),
                 b_spec=pl.BlockSpec((1024, IN_S), lambda s, k: (k, s)),
                 o_spec=pl.BlockSpec((None, D_MODEL, IN_S), lambda s, k: (s, 0, 0)), o_shape=(N_CHIPS, D_MODEL, IN_S),
                 o_dtype=bf16, dims=NN, acc_shape=(D_MODEL, IN_S), k_axis=1, sem=("parallel", "arbitrary"))
    part.update(w_in=d_w_in)
    sums_c = pair_sums(REDUCE_C, _run_exchange(_x_pair_exchange([d_w_in]), "grad_pair_exchange_c"))
    (dx, d_g1), got_c = _mm_rms_bwd("d_h", [(dproj, w_in)], xf, norm1_g, dx1, tm=512,
                                    a_spec=pl.BlockSpec((512, IN_S), lambda m, s: (m, s)),
                                    b_spec=pl.BlockSpec((None, D_MODEL, IN_S), lambda m, s: (s, 0, 0)), dims=NT,
                                    comm=_x_chip_exchange(sums_c))

    names = REDUCE_A + REDUCE_B + REDUCE_C
    halves = [_sum4(s, g_, chip1, core, f"chip_sum_{n}") for n, s, g_ in zip(names, sums_a + sums_b + sums_c, got_a + got_b + got_c)]
    big = dict(zip(names, _pair_gather(halves)))
    small = dict(norm1_g=d_g1, gate_b=jnp.concatenate([d_gba, d_gbc], axis=-1), conv_b=d_conv_b, conv_ln_g=d_ln_g,
                 conv_ln_b=d_ln_b, norm2_g=d_g2, norm_f_g=d_gf, conv_w=d_conv_w)
    return loss, dx.reshape(B, SEQ, D_MODEL), big, small


BIG = ("w_in", "w_conv_out", "w_attn_out", "w_o", "w_ffn_gate", "w_ffn_up", "w_ffn_down")
TRANSPOSED = ("w_ffn_gate", "w_ffn_up")
SMALL = ("norm1_g", "gate_b", "conv_b", "conv_ln_g", "conv_ln_b", "norm2_g", "norm_f_g")
SMALL_ROWS = {"norm1_g": 8, "gate_b": 16, "conv_b": 8, "conv_ln_g": 8, "conv_ln_b": 8, "norm2_g": 8, "norm_f_g": 8}
LOSS_ROWS = 8
CONVW_ROWS = 32 * D_MODEL // LANES


def kernel(x, norm1_g, w_in, gate_b, conv_w, conv_b, conv_ln_g, conv_ln_b, w_conv_out, w_attn_out, w_o, norm2_g, w_ffn_gate, w_ffn_up, w_ffn_down, norm_f_g, loss_target, m_norm1_g, m_w_in, m_gate_b, m_conv_w, m_conv_b, m_conv_ln_g, m_conv_ln_b, m_w_conv_out, m_w_attn_out, m_w_o, m_norm2_g, m_w_ffn_gate, m_w_ffn_up, m_w_ffn_down, m_norm_f_g, v_norm1_g, v_w_in, v_gate_b, v_conv_w, v_conv_b, v_conv_ln_g, v_conv_ln_b, v_w_conv_out, v_w_attn_out, v_w_o, v_norm2_g, v_w_ffn_gate, v_w_ffn_up, v_w_ffn_down, v_norm_f_g):
    W = dict(norm1_g=norm1_g, w_in=w_in, gate_b=gate_b, conv_w=conv_w, conv_b=conv_b, conv_ln_g=conv_ln_g, conv_ln_b=conv_ln_b,
             w_conv_out=w_conv_out, w_attn_out=w_attn_out, w_o=w_o, norm2_g=norm2_g, w_ffn_gate=w_ffn_gate, w_ffn_up=w_ffn_up,
             w_ffn_down=w_ffn_down, norm_f_g=norm_f_g)
    M = dict(norm1_g=m_norm1_g, w_in=m_w_in, gate_b=m_gate_b, conv_w=m_conv_w, conv_b=m_conv_b, conv_ln_g=m_conv_ln_g,
             conv_ln_b=m_conv_ln_b, w_conv_out=m_w_conv_out, w_attn_out=m_w_attn_out, w_o=m_w_o, norm2_g=m_norm2_g,
             w_ffn_gate=m_w_ffn_gate, w_ffn_up=m_w_ffn_up, w_ffn_down=m_w_ffn_down, norm_f_g=m_norm_f_g)
    V = dict(norm1_g=v_norm1_g, w_in=v_w_in, gate_b=v_gate_b, conv_w=v_conv_w, conv_b=v_conv_b, conv_ln_g=v_conv_ln_g,
             conv_ln_b=v_conv_ln_b, w_conv_out=v_w_conv_out, w_attn_out=v_w_attn_out, w_o=v_w_o, norm2_g=v_norm2_g,
             w_ffn_gate=v_w_ffn_gate, w_ffn_up=v_w_ffn_up, w_ffn_down=v_w_ffn_down, norm_f_g=v_norm_f_g)
    order = list(W)

    def as2d(n, a):
        a = a.reshape(a.shape[-2:])
        return a.T if n in TRANSPOSED else a

    def from2d(n, a):
        return (a.T if n in TRANSPOSED else a).reshape(W[n].shape)

    shard2d = {n: as2d(n, W[n]) for n in BIG}
    chip = 2 * lax.axis_index("x") + lax.axis_index("y")

    cw = jnp.zeros((CONV_K, D_MODEL), f32)
    cw = lax.dynamic_update_slice(cw, 0.5 * conv_w.reshape(CONV_K, D_MODEL // N_CHIPS), (0, chip * (D_MODEL // N_CHIPS)))
    conv_w_full = _small_allreduce(_rows128(cw, CONVW_ROWS), "conv_w_gather")[: CONV_K * D_MODEL // LANES].reshape(CONV_K, D_MODEL)

    core = lax.axis_index("c").astype(jnp.int32).reshape(1)
    chip1 = chip.astype(jnp.int32).reshape(1)
    shards = {n: shard2d[n].astype(bf16) for n in BIG}
    slots = {n: lax.dynamic_update_slice(lax.empty((N_CHIPS,) + s.shape, bf16), s[None], (chip, 0, 0)) for n, s in shards.items()}

    loss, grad_x, grads, gsmall = _step(x, loss_target, norm1_g, gate_b, conv_w_full, conv_b, conv_ln_g, conv_ln_b, norm2_g,
                                        norm_f_g, shards, slots, chip1, core)

    pieces = [_rows128(loss, LOSS_ROWS)] + [_rows128(gsmall[n], SMALL_ROWS[n]) for n in SMALL] + [_rows128(gsmall["conv_w"], CONVW_ROWS)]
    tot = _small_allreduce(jnp.concatenate(pieces, axis=0), "small_allreduce")
    loss_out = tot[0, 0]
    row = LOSS_ROWS
    for n in SMALL:
        grads[n] = tot[row: row + W[n].size // LANES].reshape(W[n].shape)
        row += SMALL_ROWS[n]
    dcw = tot[row: row + CONV_K * D_MODEL // LANES].reshape(CONV_K, D_MODEL)
    grads["conv_w"] = lax.dynamic_slice(dcw, (0, chip * (D_MODEL // N_CHIPS)), (CONV_K, D_MODEL // N_CHIPS))

    delta, new_m, new_v = {}, {}, {}
    for n in BIG:
        d, nm, nv = _adamw(shard2d[n], grads[n], as2d(n, M[n]), as2d(n, V[n]), f"adamw_{n}")
        delta[n], new_m[n], new_v[n], grads[n] = (from2d(n, t) for t in (d, nm, nv, grads[n]))

    def pack(src):
        return jnp.concatenate([_rows128(src[n], SMALL_ROWS[n]) for n in SMALL], axis=0)

    d, nm, nv = _adamw(pack(W), pack(grads), pack(M), pack(V), "adamw_small")
    row = 0
    for n in SMALL:
        k = W[n].size // LANES
        delta[n], new_m[n], new_v[n] = (t[row: row + k].reshape(W[n].shape) for t in (d, nm, nv))
        row += SMALL_ROWS[n]

    def pad32(a):
        return jnp.pad(a.reshape(CONV_K, D_MODEL // N_CHIPS), ((0, 1), (0, 0)))

    d, nm, nv = _adamw(pad32(conv_w), pad32(grads["conv_w"]), pad32(m_conv_w), pad32(v_conv_w), "adamw_conv_w")
    delta["conv_w"], new_m["conv_w"], new_v["conv_w"] = (t[:CONV_K].reshape(conv_w.shape) for t in (d, nm, nv))
    grads["conv_w"] = grads["conv_w"].reshape(conv_w.shape)

    return (loss_out, grad_x, *[grads[n] for n in order], *[delta[n] for n in order],
            *[new_m[n] for n in order], *[new_v[n] for n in order])
```

```python
import functools
import math
from typing import Callable, NamedTuple

import numpy as np
import jax
import jax.numpy as jnp
from jax import lax
from jax.experimental import pallas as pl
from jax.experimental.pallas import tpu as pltpu

f32 = jnp.float32
bf16 = jnp.bfloat16
SDS = jax.ShapeDtypeStruct
MESH = pl.DeviceIdType.MESH

D_MODEL = 1024
SEQ = 2048
HEAD_DIM = 64
HEADS = 8
GROUPS = ((128, 1), (512, 4), (2048, 16))
GW = HEADS * HEAD_DIM
ATTN_W = len(GROUPS) * GW
Q_BLOCK = 128
CONV_K = 31
D_FF = 2816
IN_W = 3 * ATTN_W + 2 * D_MODEL + 2 * D_MODEL
N_CHIPS = 4
IN_S = IN_W // N_CHIPS
FF_S = D_FF // N_CHIPS
RMS_EPS = 1e-6
LN_EPS = 1e-5
LR, B1, B2, ADAM_EPS, WD, STEP = 0.001, 0.9, 0.999, 1e-08, 0.01, 10
NEG = -1e30
LANES = 128
VMEM_LIMIT = 48 * 2 ** 20
VMEM_LIMIT_BIG = 52 * 2 ** 20
CB = 512
UA_CB, UB_CB, GA_CB, GC_CB = 9, 11, 13, 15


def _alibi_slope_list(n):
    def pow2(m):
        start = 2.0 ** (-8.0 / m)
        return [start ** (i + 1) for i in range(m)]
    if math.log2(n).is_integer():
        return pow2(n)
    c = 2 ** math.floor(math.log2(n))
    return pow2(c) + _alibi_slope_list(2 * c)[0::2][: n - c]


_SLOPES = np.asarray(sorted(_alibi_slope_list(len(GROUPS) * HEADS), reverse=True), dtype=np.float32).reshape(len(GROUPS), HEADS)


def _cp(sem=None, vmem=VMEM_LIMIT):
    return pltpu.CompilerParams(dimension_semantics=sem, vmem_limit_bytes=vmem)


def _sigmoid(x):
    return 1.0 / (1.0 + jnp.exp(-x))


HBM = pl.BlockSpec(memory_space=pl.ANY)


class _Exchange(NamedTuple):
    ins: list
    out_shape: list
    scratch: list
    aliases: dict
    start: Callable
    wait: Callable


def _launch(body, *, name, grid, in_specs, out_specs, out_shape, args, scratch_shapes=(), sem=None, comm=None, vmem=VMEM_LIMIT):
    if comm is None:
        return pl.pallas_call(body, grid=grid, in_specs=in_specs, out_specs=out_specs, out_shape=out_shape,
                              scratch_shapes=list(scratch_shapes), name=name, compiler_params=_cp(sem, vmem))(*args)
    multi = isinstance(out_shape, (tuple, list))
    m_out = list(out_shape) if multi else [out_shape]
    m_ospec = list(out_specs) if multi else [out_specs]
    n_in, n_out, n_scr = len(in_specs), len(m_out), len(scratch_shapes)
    nc_in, nc_out = len(comm.ins), len(comm.out_shape)

    def hosted(*refs):
        bounds = np.cumsum([0, n_in, nc_in, n_out, nc_out, n_scr])
        mi, ci, mo, co, ms = (refs[a:b] for a, b in zip(bounds[:-1], bounds[1:]))
        cs = refs[bounds[-1]:]
        ids = [pl.program_id(a) for a in range(len(grid))]
        first = functools.reduce(jnp.logical_and, [i == 0 for i in ids])
        last = functools.reduce(jnp.logical_and, [i == g - 1 for i, g in zip(ids, grid)])

        @pl.when(first)
        def _():
            comm.start(ci, co, cs)

        body(*mi, *mo, *ms)

        @pl.when(last)
        def _():
            comm.wait(ci, co, cs)

    res = pl.pallas_call(
        hosted, grid=grid, in_specs=list(in_specs) + [HBM] * nc_in, out_specs=m_ospec + [HBM] * nc_out,
        out_shape=tuple(m_out) + tuple(comm.out_shape), scratch_shapes=list(scratch_shapes) + list(comm.scratch),
        input_output_aliases={n_in + i: n_out + o for i, o in comm.aliases.items()}, name=name + "_comm",
        compiler_params=pltpu.CompilerParams(dimension_semantics=("arbitrary",) * len(grid), vmem_limit_bytes=vmem,
                                             has_side_effects=True))(*args, *comm.ins)
    return (tuple(res[:n_out]) if multi else res[0]), tuple(res[n_out:])


def _run_exchange(ex, name):
    n_in = len(ex.ins)

    def body(*refs):
        ins, outs, sems = refs[:n_in], refs[n_in:n_in + len(ex.out_shape)], refs[n_in + len(ex.out_shape):]
        ex.start(ins, outs, sems)
        ex.wait(ins, outs, sems)

    return pl.pallas_call(body, in_specs=[HBM] * n_in, out_specs=[HBM] * len(ex.out_shape), out_shape=tuple(ex.out_shape),
                          scratch_shapes=list(ex.scratch), input_output_aliases=dict(ex.aliases), name=name,
                          compiler_params=pltpu.CompilerParams(has_side_effects=True))(*ex.ins)


def _rmsnorm(xv, gv):
    return xv * lax.rsqrt(jnp.mean(xv * xv, axis=-1, keepdims=True) + RMS_EPS) * gv


def _rms_fwd(x, g, name):
    T = x.shape[0]
    tm = 512

    def body(x_ref, g_ref, o_ref, ot_ref):
        hv = _rmsnorm(x_ref[...], g_ref[...])
        o_ref[...] = hv.astype(bf16)
        ot_ref[...] = hv.T.astype(bf16)

    row = pl.BlockSpec((tm, D_MODEL), lambda i: (i, 0))
    vec = pl.BlockSpec((1, D_MODEL), lambda i: (0, 0))
    return pl.pallas_call(body, grid=(T // tm,), in_specs=[row, vec], out_specs=[row, pl.BlockSpec((D_MODEL, tm), lambda i: (0, i))],
                          out_shape=(SDS((T, D_MODEL), bf16), SDS((D_MODEL, T), bf16)), name=name,
                          compiler_params=_cp(("parallel",)))(x, g)


def _rms_bwd_tile(dyv, xv, gv, dres):
    r = lax.rsqrt(jnp.mean(xv * xv, axis=-1, keepdims=True) + RMS_EPS)
    xh = xv * r
    dxh = dyv * gv
    dx = dres + r * (dxh - xh * jnp.mean(dxh * xh, axis=-1, keepdims=True))
    return dx, jnp.sum(dyv * xh, axis=0, keepdims=True)


def _accumulate(first, refs_parts):
    @pl.when(first)
    def _():
        for ref, part in refs_parts:
            ref[...] = part

    @pl.when(jnp.logical_not(first))
    def _():
        for ref, part in refs_parts:
            ref[...] += part


def _mm_rms_bwd(name, ops, x, g, dres, *, tm, a_spec, b_spec, dims, comm=None, vmem=VMEM_LIMIT):
    T = x.shape[0]
    n = len(ops)

    def body(*refs):
        ab_refs = refs[:2 * n]
        x_ref, g_ref, r_ref, dx_ref, dg_ref, acc = refs[2 * n:]
        m, s = pl.program_id(0), pl.program_id(1)
        p = sum(lax.dot_general(ab_refs[2 * i][...], ab_refs[2 * i + 1][...], dims, preferred_element_type=f32) for i in range(n))

        @pl.when(s == 0)
        def _():
            acc[...] = p

        @pl.when(s > 0)
        def _():
            acc[...] += p

        @pl.when(s == N_CHIPS - 1)
        def _():
            dx, part = _rms_bwd_tile(acc[...], x_ref[...], g_ref[...], r_ref[...])
            dx_ref[...] = dx
            _accumulate(m == 0, [(dg_ref, part)])

    row = pl.BlockSpec((tm, D_MODEL), lambda m, s: (m, 0))
    vec = pl.BlockSpec((1, D_MODEL), lambda m, s: (0, 0))
    return _launch(body, name=name, grid=(T // tm, N_CHIPS), in_specs=[a_spec, b_spec] * n + [row, vec, row], out_specs=[row, vec],
                   out_shape=(SDS((T, D_MODEL), f32), SDS((1, D_MODEL), f32)), args=tuple(t for ab in ops for t in ab) + (x, g, dres),
                   scratch_shapes=[pltpu.VMEM((tm, D_MODEL), f32)], sem=("arbitrary", "arbitrary"), comm=comm, vmem=vmem)


def _ffn_down_loss(ff, wd, x1, gf, target):
    T = x1.shape[0]
    tm = 1024
    nm = T // tm

    def body(f_ref, w_ref, x_ref, g_ref, t_ref, loss_ref, dx_ref, dg_ref, acc):
        m, s = pl.program_id(0), pl.program_id(1)
        p = jnp.dot(f_ref[...], w_ref[...], preferred_element_type=f32)

        @pl.when(s == 0)
        def _():
            acc[...] = p

        @pl.when(s > 0)
        def _():
            acc[...] += p

        @pl.when(s == N_CHIPS - 1)
        def _():
            xv = acc[...] + x_ref[...]
            gv = g_ref[...]
            r = lax.rsqrt(jnp.mean(xv * xv, axis=-1, keepdims=True) + RMS_EPS)
            xh = xv * r
            e = xh * gv - t_ref[...]
            part_l = jnp.broadcast_to(0.5 * jnp.sum(jnp.mean(e * e, axis=-1, keepdims=True), axis=0, keepdims=True), (1, LANES))
            dy = e * (1.0 / D_MODEL)
            dxh = dy * gv
            dx_ref[...] = r * (dxh - xh * jnp.mean(dxh * xh, axis=-1, keepdims=True))
            part_g = jnp.sum(dy * xh, axis=0, keepdims=True)

            @pl.when(m == 0)
            def _():
                loss_ref[...] = part_l
                dg_ref[...] = part_g

            @pl.when(m > 0)
            def _():
                loss_ref[...] += part_l
                dg_ref[...] += part_g

    row = pl.BlockSpec((tm, D_MODEL), lambda m, s: (m, 0))
    vec = pl.BlockSpec((1, D_MODEL), lambda m, s: (0, 0))
    return pl.pallas_call(
        body, grid=(nm, N_CHIPS),
        in_specs=[pl.BlockSpec((None, tm, FF_S), lambda m, s: (s, m, 0)), pl.BlockSpec((None, FF_S, D_MODEL), lambda m, s: (s, 0, 0)),
                  row, vec, row],
        out_specs=[pl.BlockSpec((1, LANES), lambda m, s: (0, 0)), row, vec],
        out_shape=(SDS((1, LANES), f32), SDS((T, D_MODEL), f32), SDS((1, D_MODEL), f32)),
        scratch_shapes=[pltpu.VMEM((tm, D_MODEL), f32)], name="ffn_down_loss", compiler_params=_cp(("arbitrary", "arbitrary")))(
            ff, wd, x1, gf, target)


def _o_proj_rms(merged, w_o, x, g2):
    T = x.shape[0]
    tm = 1024

    def body(a_ref, w_ref, x_ref, g_ref, x1_ref, h2_ref):
        x1 = jnp.dot(a_ref[...], w_ref[...], preferred_element_type=f32) + x_ref[...]
        x1_ref[...] = x1
        h2_ref[...] = _rmsnorm(x1, g_ref[...]).astype(bf16)

    row = pl.BlockSpec((tm, D_MODEL), lambda i: (i, 0))
    return pl.pallas_call(body, grid=(T // tm,),
                          in_specs=[row, pl.BlockSpec((D_MODEL, D_MODEL), lambda i: (0, 0)), row, pl.BlockSpec((1, D_MODEL), lambda i: (0, 0))],
                          out_specs=[row, row], out_shape=(SDS((T, D_MODEL), f32), SDS((T, D_MODEL), bf16)), name="o_proj_rms",
                          compiler_params=_cp(("parallel",)))(merged, w_o, x, g2)


def _ln_silu_fwd(c1, g, b):
    T = c1.shape[0]
    tm = 512

    def body(c_ref, g_ref, b_ref, o_ref):
        cv = c_ref[...]
        mu = jnp.mean(cv, axis=-1, keepdims=True)
        cc = cv - mu
        var = jnp.mean(cc * cc, axis=-1, keepdims=True)
        c2 = cc * lax.rsqrt(var + LN_EPS) * g_ref[...] + b_ref[...]
        o_ref[...] = (c2 * _sigmoid(c2)).astype(o_ref.dtype)

    row = pl.BlockSpec((tm, D_MODEL), lambda i: (i, 0))
    vec = pl.BlockSpec((1, D_MODEL), lambda i: (0, 0))
    return pl.pallas_call(body, grid=(T // tm,), in_specs=[row, vec, vec], out_specs=row,
                          out_shape=SDS((T, D_MODEL), bf16), name="ln_silu_fwd", compiler_params=_cp(("parallel",)))(c1, g, b)


def _conv_out_bwd(dyc, w, c1, g, b):
    T = c1.shape[0]
    tm = 1024

    def body(d_ref, w_ref, c_ref, g_ref, b_ref, dc_ref, dg_ref, db_ref):
        dc3 = lax.dot_general(d_ref[...], w_ref[...], NT, preferred_element_type=f32)
        cv = c_ref[...]
        gv = g_ref[...]
        mu = jnp.mean(cv, axis=-1, keepdims=True)
        cc = cv - mu
        var = jnp.mean(cc * cc, axis=-1, keepdims=True)
        rs = lax.rsqrt(var + LN_EPS)
        xh = cc * rs
        c2 = xh * gv + b_ref[...]
        sg = _sigmoid(c2)
        dc2 = dc3 * (sg * (1.0 + c2 * (1.0 - sg)))
        dxh = dc2 * gv
        dc_ref[...] = rs * (dxh - jnp.mean(dxh, axis=-1, keepdims=True) - xh * jnp.mean(dxh * xh, axis=-1, keepdims=True))
        _accumulate(pl.program_id(0) == 0, [(dg_ref, jnp.sum(dc2 * xh, axis=0, keepdims=True)), (db_ref, jnp.sum(dc2, axis=0, keepdims=True))])

    row = pl.BlockSpec((tm, D_MODEL), lambda i: (i, 0))
    vec = pl.BlockSpec((1, D_MODEL), lambda i: (0, 0))
    return pl.pallas_call(body, grid=(T // tm,), in_specs=[row, pl.BlockSpec((D_MODEL, D_MODEL), lambda i: (0, 0)), row, vec, vec],
                          out_specs=[row, vec, vec],
                          out_shape=(SDS((T, D_MODEL), f32), SDS((1, D_MODEL), f32), SDS((1, D_MODEL), f32)),
                          name="conv_out_bwd", compiler_params=_cp(("arbitrary",)))(dyc, w, c1, g, b)


NN = (((1,), (0,)), ((), ()))
NT = (((1,), (1,)), ((), ()))
TN = (((0,), (0,)), ((), ()))


def _mm(name, a, b, *, grid, a_spec, b_spec, o_spec, o_shape, o_dtype, dims, acc_shape=None, k_axis=None,
        res=None, res_spec=None, sem=None, comm=None):
    nk = 1 if k_axis is None else grid[k_axis]

    def body(*refs):
        if res is None:
            a_ref, b_ref, o_ref = refs[:3]
            r_ref, scr = None, refs[3:]
        else:
            a_ref, b_ref, r_ref, o_ref = refs[:4]
            scr = refs[4:]
        p = lax.dot_general(a_ref[...].astype(bf16), b_ref[...].astype(bf16), dims, preferred_element_type=f32)
        if nk == 1:
            if r_ref is not None:
                p = p + r_ref[...]
            o_ref[...] = p.astype(o_dtype)
            return
        acc = scr[0]
        k = pl.program_id(k_axis)

        @pl.when(k == 0)
        def _():
            acc[...] = p

        @pl.when(k > 0)
        def _():
            acc[...] += p

        @pl.when(k == nk - 1)
        def _():
            t = acc[...]
            if r_ref is not None:
                t = t + r_ref[...]
            o_ref[...] = t.astype(o_dtype)

    ins = [a, b] + ([] if res is None else [res])
    specs = [a_spec, b_spec] + ([] if res is None else [res_spec])
    scratch = [] if nk == 1 else [pltpu.VMEM(acc_shape, f32)]
    return _launch(body, name=name, grid=grid, in_specs=specs, out_specs=o_spec, out_shape=SDS(o_shape, o_dtype),
                   args=ins, scratch_shapes=scratch, sem=sem, comm=comm)


def _mm_nn_full(name, a, b, o_dtype, res=None, tm=1024, comm=None):
    T, K = a.shape
    N = b.shape[1]
    return _mm(name, a, b, grid=(T // tm,), a_spec=pl.BlockSpec((tm, K), lambda i: (i, 0)),
               b_spec=pl.BlockSpec((K, N), lambda i: (0, 0)), o_spec=pl.BlockSpec((tm, N), lambda i: (i, 0)),
               o_shape=(T, N), o_dtype=o_dtype, dims=NN, res=res,
               res_spec=None if res is None else pl.BlockSpec((tm, N), lambda i: (i, 0)), sem=("parallel",), comm=comm)


def _mm_nt_full(name, a, b, o_dtype, tm=1024):
    T, N = a.shape
    K = b.shape[0]
    return _mm(name, a, b, grid=(T // tm,), a_spec=pl.BlockSpec((tm, N), lambda i: (i, 0)),
               b_spec=pl.BlockSpec((K, N), lambda i: (0, 0)), o_spec=pl.BlockSpec((tm, K), lambda i: (i, 0)),
               o_shape=(T, K), o_dtype=o_dtype, dims=NT, sem=("parallel",))


def _mm_tn_tokens(name, a, b, o_dtype, tk=1024):
    T, K = a.shape
    N = b.shape[1]
    return _mm(name, a, b, grid=(T // tk,), a_spec=pl.BlockSpec((tk, K), lambda k: (k, 0)),
               b_spec=pl.BlockSpec((tk, N), lambda k: (k, 0)), o_spec=pl.BlockSpec((K, N), lambda k: (0, 0)),
               o_shape=(K, N), o_dtype=o_dtype, dims=TN, acc_shape=(K, N), k_axis=0, sem=("arbitrary",))


def _ffn_up(h2, wg, wu):
    T = h2.shape[0]
    tm = 1024

    def body(h_ref, wg_ref, wu_ref, a_ref, b_ref, f_ref):
        hv = h_ref[...]
        av = lax.dot_general(hv, wg_ref[...], NT, preferred_element_type=f32)
        bv = lax.dot_general(hv, wu_ref[...], NT, preferred_element_type=f32)
        a_ref[...] = av.astype(bf16)
        b_ref[...] = bv.astype(bf16)
        f_ref[...] = (av * _sigmoid(av) * bv).astype(bf16)

    wspec = pl.BlockSpec((None, FF_S, D_MODEL), lambda s, m: (s, 0, 0))
    ospec = pl.BlockSpec((None, tm, FF_S), lambda s, m: (s, m, 0))
    osd = SDS((N_CHIPS, T, FF_S), bf16)
    return pl.pallas_call(body, grid=(N_CHIPS, T // tm),
                          in_specs=[pl.BlockSpec((tm, D_MODEL), lambda s, m: (m, 0)), wspec, wspec],
                          out_specs=[ospec, ospec, ospec], out_shape=(osd, osd, osd), name="ffn_up",
                          compiler_params=_cp(("parallel", "parallel")))(h2, wg, wu)


def _ffn_down_bwd(dx2, wd, a, b):
    T = dx2.shape[0]
    tm = 1024

    def body(d_ref, w_ref, a_ref, b_ref, da_ref, db_ref):
        df = lax.dot_general(d_ref[...].astype(bf16), w_ref[...], NT, preferred_element_type=f32)
        av = a_ref[...].astype(f32)
        sg = _sigmoid(av)
        da_ref[...] = (df * b_ref[...].astype(f32) * (sg * (1.0 + av * (1.0 - sg)))).astype(bf16)
        db_ref[...] = (df * av * sg).astype(bf16)

    aspec = pl.BlockSpec((None, tm, FF_S), lambda m, s: (s, m, 0))
    osd = SDS((N_CHIPS, T, FF_S), bf16)
    return pl.pallas_call(body, grid=(T // tm, N_CHIPS),
                          in_specs=[pl.BlockSpec((tm, D_MODEL), lambda m, s: (m, 0)),
                                    pl.BlockSpec((None, FF_S, D_MODEL), lambda m, s: (s, 0, 0)), aspec, aspec],
                          out_specs=[aspec, aspec], out_shape=(osd, osd), name="ffn_down_bwd",
                          compiler_params=_cp(("parallel", "parallel")))(dx2, wd, a, b)


def _merge_fwd(proj, gate_b, ya, yc):
    T = proj.shape[0]
    tm = 512

    def body(ga_ref, gc_ref, ba_ref, bc_ref, ya_ref, yc_ref, o_ref):
        sa = _sigmoid(ga_ref[...].astype(f32) + ba_ref[...])
        sc = _sigmoid(gc_ref[...].astype(f32) + bc_ref[...])
        o_ref[...] = (sa * ya_ref[...].astype(f32) + sc * yc_ref[...].astype(f32)).astype(bf16)

    blk = pl.BlockSpec((tm, CB), lambda i, j: (i, j))
    return pl.pallas_call(
        body, grid=(T // tm, 2),
        in_specs=[pl.BlockSpec((tm, CB), lambda i, j: (i, GA_CB + j)), pl.BlockSpec((tm, CB), lambda i, j: (i, GC_CB + j)),
                  pl.BlockSpec((1, CB), lambda i, j: (0, j)), pl.BlockSpec((1, CB), lambda i, j: (0, 2 + j)), blk, blk],
        out_specs=blk, out_shape=SDS((T, D_MODEL), bf16), name="merge_fwd",
        compiler_params=_cp(("parallel", "parallel")))(proj, proj, gate_b, gate_b, ya, yc)


def _merge_bwd(dm, proj, gate_b, ya, yc):
    T = proj.shape[0]
    tm = 512

    def body(dm_ref, ga_ref, gc_ref, ba_ref, bc_ref, ya_ref, yc_ref, dya_ref, dyc_ref, dga_ref, dgc_ref, dba_ref, dbc_ref):
        dmv = dm_ref[...].astype(f32)
        sa = _sigmoid(ga_ref[...].astype(f32) + ba_ref[...])
        sc = _sigmoid(gc_ref[...].astype(f32) + bc_ref[...])
        dya_ref[...] = (dmv * sa).astype(bf16)
        dyc_ref[...] = (dmv * sc).astype(bf16)
        dga = dmv * ya_ref[...].astype(f32) * sa * (1.0 - sa)
        dgc = dmv * yc_ref[...].astype(f32) * sc * (1.0 - sc)
        dga_ref[...] = dga.astype(bf16)
        dgc_ref[...] = dgc.astype(bf16)
        pa = jnp.sum(dga, axis=0, keepdims=True)
        pc = jnp.sum(dgc, axis=0, keepdims=True)

        @pl.when(pl.program_id(1) == 0)
        def _():
            dba_ref[...] = pa
            dbc_ref[...] = pc

        @pl.when(pl.program_id(1) > 0)
        def _():
            dba_ref[...] += pa
            dbc_ref[...] += pc

    blk = pl.BlockSpec((tm, CB), lambda j, i: (i, j))
    vec = pl.BlockSpec((1, CB), lambda j, i: (0, j))
    big = SDS((T, D_MODEL), bf16)
    small = SDS((1, D_MODEL), f32)
    return pl.pallas_call(
        body, grid=(2, T // tm),
        in_specs=[blk, pl.BlockSpec((tm, CB), lambda j, i: (i, GA_CB + j)), pl.BlockSpec((tm, CB), lambda j, i: (i, GC_CB + j)),
                  vec, pl.BlockSpec((1, CB), lambda j, i: (0, 2 + j)), blk, blk],
        out_specs=[blk, blk, blk, blk, vec, vec], out_shape=(big, big, big, big, small, small), name="merge_bwd",
        compiler_params=_cp(("parallel", "arbitrary")))(dm, proj, proj, gate_b, gate_b, ya, yc)


CONV_TS = 256
CONV_HALO = 32
CONV_RC = 64
CONV_WIN = CONV_TS + CONV_HALO
SUBLANES = 8


def _fill_shifted(win, sh):
    for b in range(1, SUBLANES):
        sh[b - 1] = win[pl.ds(b, CONV_WIN - SUBLANES), :]


def _rows_at(win, sh, row):
    a, b = divmod(row, SUBLANES)
    if b == 0:
        return win[pl.ds(row, CONV_RC), :]
    return sh[b - 1, pl.ds(a * SUBLANES, CONV_RC), :]


def _glu_conv_fwd(proj3, w, bias, comm=None):
    B = proj3.shape[0]
    nt = SEQ // CONV_TS
    hb = CONV_TS // CONV_HALO

    def body(ua_ref, ub_ref, ha_ref, hb_ref, w_ref, b_ref, o_ref, win, sh):
        i = pl.program_id(2)
        c0 = ua_ref[...].astype(f32) * _sigmoid(ub_ref[...].astype(f32))
        halo = ha_ref[...].astype(f32) * _sigmoid(hb_ref[...].astype(f32))
        win[0:CONV_HALO, :] = jnp.where(i > 0, halo, 0.0)
        win[CONV_HALO:, :] = c0
        _fill_shifted(win, sh)
        for r0 in range(0, CONV_TS, CONV_RC):
            acc = jnp.zeros((CONV_RC, CB), f32) + b_ref[...]
            for k in range(CONV_K):
                acc = acc + _rows_at(win, sh, r0 + CONV_HALO - (CONV_K - 1) + k) * w_ref[k:k + 1, :]
            o_ref[r0:r0 + CONV_RC, :] = acc

    def cur(cb):
        return pl.BlockSpec((None, CONV_TS, CB), lambda b, j, i: (b, i, cb + j))

    def prev(cb):
        return pl.BlockSpec((None, CONV_HALO, CB), lambda b, j, i: (b, jnp.maximum(i * hb - 1, 0), cb + j))

    return _launch(
        body, name="glu_conv_fwd", grid=(B, 2, nt),
        in_specs=[cur(UA_CB), cur(UB_CB), prev(UA_CB), prev(UB_CB),
                  pl.BlockSpec((CONV_K, CB), lambda b, j, i: (0, j)), pl.BlockSpec((1, CB), lambda b, j, i: (0, j))],
        out_specs=pl.BlockSpec((None, CONV_TS, CB), lambda b, j, i: (b, i, j)),
        out_shape=SDS((B, SEQ, D_MODEL), f32), args=(proj3, proj3, proj3, proj3, w, bias),
        scratch_shapes=[pltpu.VMEM((CONV_WIN, CB), f32), pltpu.VMEM((SUBLANES - 1, CONV_WIN - SUBLANES, CB), f32)],
        sem=("parallel", "parallel", "parallel"), comm=comm)


def _glu_conv_bwd(dc1, proj3, w, comm=None):
    B = proj3.shape[0]
    nt = SEQ // CONV_TS
    hb = CONV_TS // CONV_HALO

    def body(d_ref, dn_ref, ua_ref, ub_ref, ha_ref, hb_ref, w_ref, dua_ref, dub_ref, dw_ref, db_ref, winc, wind, accw, shc, shd):
        b = pl.program_id(1)
        i = pl.program_id(2)
        first = jnp.logical_and(b == 0, i == 0)
        last = jnp.logical_and(b == B - 1, i == nt - 1)

        @pl.when(first)
        def _():
            accw[...] = jnp.zeros_like(accw)
            db_ref[...] = jnp.zeros_like(db_ref)

        halo = ha_ref[...].astype(f32) * _sigmoid(hb_ref[...].astype(f32))
        winc[0:CONV_HALO, :] = jnp.where(i > 0, halo, 0.0)
        winc[CONV_HALO:, :] = ua_ref[...].astype(f32) * _sigmoid(ub_ref[...].astype(f32))
        wind[0:CONV_TS, :] = d_ref[...]
        wind[CONV_TS:, :] = jnp.where(i < nt - 1, dn_ref[...], 0.0)
        db_ref[...] += jnp.sum(d_ref[...], axis=0, keepdims=True)
        _fill_shifted(winc, shc)
        _fill_shifted(wind, shd)
        for r0 in range(0, CONV_TS, CONV_RC):
            dc0 = jnp.zeros((CONV_RC, CB), f32)
            for k in range(CONV_K):
                dc0 = dc0 + _rows_at(wind, shd, r0 + (CONV_K - 1) - k) * w_ref[k:k + 1, :]
            uav = ua_ref[r0:r0 + CONV_RC, :].astype(f32)
            sg = _sigmoid(ub_ref[r0:r0 + CONV_RC, :].astype(f32))
            dua_ref[r0:r0 + CONV_RC, :] = (dc0 * sg).astype(bf16)
            dub_ref[r0:r0 + CONV_RC, :] = (dc0 * uav * sg * (1.0 - sg)).astype(bf16)
            dv = wind[r0:r0 + CONV_RC, :]
            for k in range(CONV_K):
                prod = dv * _rows_at(winc, shc, r0 + CONV_HALO - (CONV_K - 1) + k)
                accw[k] += jnp.sum(prod.reshape(CONV_RC // 8, 8, CB), axis=0)

        @pl.when(last)
        def _():
            for k in range(CONV_K):
                dw_ref[k:k + 1, :] = jnp.sum(accw[k], axis=0, keepdims=True)
            dw_ref[CONV_K:, :] = jnp.zeros((CONV_HALO - CONV_K, CB), f32)

    def cur(cb):
        return pl.BlockSpec((None, CONV_TS, CB), lambda j, b, i: (b, i, cb + j))

    def prev(cb):
        return pl.BlockSpec((None, CONV_HALO, CB), lambda j, b, i: (b, jnp.maximum(i * hb - 1, 0), cb + j))

    nxt = pl.BlockSpec((None, CONV_HALO, CB), lambda j, b, i: (b, jnp.minimum((i + 1) * hb, SEQ // CONV_HALO - 1), j))
    big = SDS((B, SEQ, D_MODEL), bf16)
    return _launch(
        body, name="glu_conv_bwd", grid=(2, B, nt),
        in_specs=[cur(0), nxt, cur(UA_CB), cur(UB_CB), prev(UA_CB), prev(UB_CB), pl.BlockSpec((CONV_K, CB), lambda j, b, i: (0, j))],
        out_specs=[cur(0), cur(0), pl.BlockSpec((CONV_HALO, CB), lambda j, b, i: (0, j)), pl.BlockSpec((1, CB), lambda j, b, i: (0, j))],
        out_shape=(big, big, SDS((CONV_HALO, D_MODEL), f32), SDS((1, D_MODEL), f32)),
        args=(dc1, dc1, proj3, proj3, proj3, proj3, w),
        scratch_shapes=[pltpu.VMEM((CONV_WIN, CB), f32), pltpu.VMEM((CONV_WIN, CB), f32), pltpu.VMEM((CONV_K, SUBLANES, CB), f32),
                        pltpu.VMEM((SUBLANES - 1, CONV_WIN - SUBLANES, CB), f32),
                        pltpu.VMEM((SUBLANES - 1, CONV_WIN - SUBLANES, CB), f32)],
        sem=("parallel", "arbitrary", "arbitrary"), comm=comm)


def _band(first, dil):
    kw = Q_BLOCK if first else 2 * Q_BLOCK
    qi = lax.broadcasted_iota(jnp.int32, (Q_BLOCK, kw), 0)
    kj = lax.broadcasted_iota(jnp.int32, (Q_BLOCK, kw), 1)
    rel = qi - kj + (0 if first else Q_BLOCK)
    valid = jnp.logical_and(rel >= 0, rel <= Q_BLOCK)
    return valid, rel.astype(f32) * float(dil)


def _bias(first, dil, slope):
    valid, dist = _band(first, dil)
    return jnp.where(valid, -slope * dist, NEG)


def _scores(q, k, bias):
    return lax.dot_general(q, k, NT, preferred_element_type=f32) * (HEAD_DIM ** -0.5) + bias


def _pair_cols(hp):
    return slice(hp * LANES, (hp + 1) * LANES)


def _half(x2, e):
    lane = lax.broadcasted_iota(jnp.int32, (1, LANES), 1)
    keep = (lane < HEAD_DIM) if e == 0 else (lane >= HEAD_DIM)
    return jnp.where(keep, x2, jnp.zeros_like(x2))


def _attn_fwd(g, q4, k4, v4, qcb, kcb, vcb, comm=None):
    _, dil = GROUPS[g]
    B, r, L, _ = q4.shape
    nb = L // Q_BLOCK
    slopes = [float(s) for s in _SLOPES[g]]

    def body(q_ref, k_ref, v_ref, o_ref, lse_ref, bias, s_scr, p_scr):
        lane = lax.broadcasted_iota(jnp.int32, (Q_BLOCK, LANES), 1)
        if nb > 1:
            for h in range(HEADS):
                bias[h] = _bias(False, dil, slopes[h])

        def block(n, first):
            q0 = 0 if first else pl.multiple_of(n * Q_BLOCK, Q_BLOCK)
            k0 = 0 if first else pl.multiple_of((n - 1) * Q_BLOCK, Q_BLOCK)
            kw = Q_BLOCK if first else 2 * Q_BLOCK
            for hp in range(HEADS // 2):
                q2 = q_ref[pl.ds(q0, Q_BLOCK), _pair_cols(hp)]
                k2 = k_ref[pl.ds(k0, kw), _pair_cols(hp)]
                for e in range(2):
                    h = 2 * hp + e
                    b_h = _bias(True, dil, slopes[h]) if first else bias[h]
                    s_scr[h, :, :kw] = _scores(_half(q2, e), k2, b_h)
            st = jnp.zeros((Q_BLOCK, LANES), f32)
            dens = jnp.ones((Q_BLOCK, LANES), f32)
            for h in range(HEADS):
                s = s_scr[h, :, :kw]
                m = jnp.max(s, axis=-1, keepdims=True)
                p = jnp.exp(s - m)
                den = jnp.sum(p, axis=-1, keepdims=True)
                p_scr[h, :, :kw] = p.astype(bf16)
                st = jnp.where(lane == h, m + jnp.log(den), st)
                dens = jnp.where(lane == h, den, dens)
            lse_ref[pl.ds(q0, Q_BLOCK), :] = st
            inv = 1.0 / dens
            for hp in range(HEADS // 2):
                v2 = v_ref[pl.ds(k0, kw), _pair_cols(hp)]
                o2 = sum(jnp.dot(p_scr[2 * hp + e, :, :kw], _half(v2, e), preferred_element_type=f32) * inv[:, 2 * hp + e:2 * hp + e + 1]
                         for e in range(2))
                o_ref[pl.ds(q0, Q_BLOCK), _pair_cols(hp)] = o2.astype(bf16)

        block(0, True)
        if nb > 1:
            def step(n, carry):
                block(n, False)
                return carry
            lax.fori_loop(1, nb, step, 0)

    def spec(cb):
        return pl.BlockSpec((None, None, L, GW), lambda b, c: (b, c, 0, cb))

    return _launch(
        body, name=f"attn_fwd_g{g}", grid=(B, r), in_specs=[spec(qcb), spec(kcb), spec(vcb)],
        out_specs=[spec(0), pl.BlockSpec((None, None, L, LANES), lambda b, c: (b, c, 0, 0))],
        out_shape=(SDS((B, r, L, GW), bf16), SDS((B, r, L, LANES), f32)), args=(q4, k4, v4),
        scratch_shapes=[pltpu.VMEM((HEADS, Q_BLOCK, 2 * Q_BLOCK), f32), pltpu.VMEM((HEADS, Q_BLOCK, 2 * Q_BLOCK), f32),
                        pltpu.VMEM((HEADS, Q_BLOCK, 2 * Q_BLOCK), bf16)],
        sem=("parallel", "parallel"), comm=comm)


def _attn_bwd(g, q4, k4, v4, qcb, kcb, vcb, do4, lse4, dl4, comm=None):
    _, dil = GROUPS[g]
    B, r, L, _ = q4.shape
    nb = L // Q_BLOCK
    slopes = [float(s) for s in _SLOPES[g]]
    scale = HEAD_DIM ** -0.5

    def body(q_ref, k_ref, v_ref, do_ref, lse_ref, dl_ref, dq_ref, dk_ref, dv_ref, dk_acc, dv_acc, bias, s_scr, dp_scr, p_scr, ds_scr):
        dk_acc[...] = jnp.zeros_like(dk_acc)
        dv_acc[...] = jnp.zeros_like(dv_acc)
        if nb > 1:
            for h in range(HEADS):
                bias[h] = _bias(False, dil, slopes[h])

        def block(n, first):
            q0 = 0 if first else pl.multiple_of(n * Q_BLOCK, Q_BLOCK)
            k0 = 0 if first else pl.multiple_of((n - 1) * Q_BLOCK, Q_BLOCK)
            kw = Q_BLOCK if first else 2 * Q_BLOCK
            for hp in range(HEADS // 2):
                q2 = q_ref[pl.ds(q0, Q_BLOCK), _pair_cols(hp)]
                k2 = k_ref[pl.ds(k0, kw), _pair_cols(hp)]
                v2 = v_ref[pl.ds(k0, kw), _pair_cols(hp)]
                do2 = do_ref[pl.ds(q0, Q_BLOCK), _pair_cols(hp)]
                for e in range(2):
                    h = 2 * hp + e
                    b_h = _bias(True, dil, slopes[h]) if first else bias[h]
                    s_scr[h, :, :kw] = _scores(_half(q2, e), k2, b_h)
                    dp_scr[h, :, :kw] = lax.dot_general(_half(do2, e), v2, NT, preferred_element_type=f32)
            for h in range(HEADS):
                p = jnp.exp(s_scr[h, :, :kw] - lse_ref[pl.ds(q0, Q_BLOCK), h:h + 1])
                p_scr[h, :, :kw] = p.astype(bf16)
                ds_scr[h, :, :kw] = (p * (dp_scr[h, :, :kw] - dl_ref[pl.ds(q0, Q_BLOCK), h:h + 1])).astype(bf16)
            for hp in range(HEADS // 2):
                cols = _pair_cols(hp)
                q2 = q_ref[pl.ds(q0, Q_BLOCK), cols]
                k2 = k_ref[pl.ds(k0, kw), cols]
                do2 = do_ref[pl.ds(q0, Q_BLOCK), cols]
                ds = [ds_scr[2 * hp + e, :, :kw] for e in range(2)]
                dq2 = sum(jnp.dot(ds[e], _half(k2, e), preferred_element_type=f32) for e in range(2))
                dq_ref[pl.ds(q0, Q_BLOCK), cols] = (dq2 * scale).astype(bf16)
                dk2 = sum(lax.dot_general(ds[e], _half(q2, e), TN, preferred_element_type=f32) for e in range(2))
                dk_acc[pl.ds(k0, kw), cols] += dk2 * scale
                dv2 = sum(lax.dot_general(p_scr[2 * hp + e, :, :kw], _half(do2, e), TN, preferred_element_type=f32) for e in range(2))
                dv_acc[pl.ds(k0, kw), cols] += dv2

        block(0, True)
        if nb > 1:
            def step(n, carry):
                block(n, False)
                return carry
            lax.fori_loop(1, nb, step, 0)
        dk_ref[...] = dk_acc[...].astype(bf16)
        dv_ref[...] = dv_acc[...].astype(bf16)

    def spec(cb):
        return pl.BlockSpec((None, None, L, GW), lambda b, c: (b, c, 0, cb))

    st = pl.BlockSpec((None, None, L, LANES), lambda b, c: (b, c, 0, 0))
    osd = SDS((B, r, L, GW), bf16)
    return _launch(
        body, name=f"attn_bwd_g{g}", grid=(B, r), in_specs=[spec(qcb), spec(kcb), spec(vcb), spec(0), st, st],
        out_specs=[spec(0), spec(0), spec(0)], out_shape=(osd, osd, osd), args=(q4, k4, v4, do4, lse4, dl4),
        scratch_shapes=[pltpu.VMEM((L, GW), f32), pltpu.VMEM((L, GW), f32)]
        + [pltpu.VMEM((HEADS, Q_BLOCK, 2 * Q_BLOCK), f32)] * 3 + [pltpu.VMEM((HEADS, Q_BLOCK, 2 * Q_BLOCK), bf16)] * 2,
        sem=("parallel", "parallel"), comm=comm)


RT = 512
RCH = GW // LANES
DILS = tuple(d for _, d in GROUPS[1:])


def _res_spec(r, width):
    return pl.BlockSpec((None, r, RT // r, width), lambda b, i, *_: (b, 0, i, 0))


def _tok_spec(width, cb=0):
    return pl.BlockSpec((None, RT, width), lambda b, i, *_: (b, i, cb))


def _to_residues(res_ref, scr, r, width):
    for c in range(r):
        for jj in range(width // LANES):
            res_ref[c, :, jj * LANES:(jj + 1) * LANES] = scr[jj, pl.ds(c, RT // r, stride=r), :].astype(res_ref.dtype)


def _from_residues(scr, res_ref, r, width):
    for c in range(r):
        for jj in range(width // LANES):
            scr[jj, pl.ds(c, RT // r, stride=r), :] = res_ref[c, :, jj * LANES:(jj + 1) * LANES].astype(f32)


def _qkv_to_residues(proj3, g):
    r = GROUPS[g][1]
    B = proj3.shape[0]

    def body(q_ref, k_ref, v_ref, o_ref, scr):
        for p, x_ref in enumerate((q_ref, k_ref, v_ref)):
            x = x_ref[...].astype(f32)
            for jj in range(RCH):
                scr[p * RCH + jj] = x[:, jj * LANES:(jj + 1) * LANES]
        _to_residues(o_ref, scr, r, ATTN_W)

    return pl.pallas_call(
        body, grid=(B, SEQ // RT), in_specs=[_tok_spec(GW, 3 * p + g) for p in range(3)], out_specs=_res_spec(r, ATTN_W),
        out_shape=SDS((B, r, SEQ // r, ATTN_W), bf16), scratch_shapes=[pltpu.VMEM((3 * RCH, RT, LANES), f32)],
        name=f"qkv_to_residues_g{g}", compiler_params=_cp(("parallel", "parallel")))(proj3, proj3, proj3)


def _attn_mix(o0, l0, o_res, l_res):
    B = o0.shape[0]

    def body(o0_ref, l0_ref, o1_ref, o2_ref, l1_ref, l2_ref, y_ref, lt_ref, lt1_ref, lt2_ref, so, sl):
        for gi, (o_ref, l_ref, r) in enumerate(((o1_ref, l1_ref, DILS[0]), (o2_ref, l2_ref, DILS[1]))):
            _from_residues(so.at[gi], o_ref, r, GW)
            _from_residues(sl.at[gi:gi + 1], l_ref, r, LANES)
        ls = [l0_ref[...], sl[0], sl[1]]
        m = functools.reduce(jnp.maximum, ls)
        ws = [jnp.exp(l - m) for l in ls]
        den = ws[0] + ws[1] + ws[2]
        alphas = [w / den for w in ws]
        lt = m + jnp.log(den)
        lt_ref[...] = lt
        sl[2] = lt
        _to_residues(lt1_ref, sl.at[2:3], DILS[0], LANES)
        _to_residues(lt2_ref, sl.at[2:3], DILS[1], LANES)
        for h in range(HEADS):
            cols = slice(h * HEAD_DIM, (h + 1) * HEAD_DIM)
            jj, lo = divmod(h * HEAD_DIM, LANES)
            acc = alphas[0][:, h:h + 1] * o0_ref[:, cols].astype(f32)
            for gi in range(2):
                acc = acc + alphas[gi + 1][:, h:h + 1] * so[gi, jj, :, lo:lo + HEAD_DIM]
            y_ref[:, cols] = acc.astype(bf16)

    in_specs = [_tok_spec(GW), _tok_spec(LANES), _res_spec(DILS[0], GW), _res_spec(DILS[1], GW), _res_spec(DILS[0], LANES), _res_spec(DILS[1], LANES)]
    out_specs = [_tok_spec(GW), _tok_spec(LANES), _res_spec(DILS[0], LANES), _res_spec(DILS[1], LANES)]
    return pl.pallas_call(
        body, grid=(B, SEQ // RT), in_specs=in_specs, out_specs=out_specs,
        out_shape=(SDS((B, SEQ, GW), bf16), SDS((B, SEQ, LANES), f32)) + tuple(SDS((B, r, SEQ // r, LANES), f32) for r in DILS),
        scratch_shapes=[pltpu.VMEM((2, RCH, RT, LANES), f32), pltpu.VMEM((3, RT, LANES), f32)],
        name="attn_mix", compiler_params=_cp(("parallel", "parallel")))(o0, l0, *o_res, *l_res)


def _attn_delta(dmix, mix):
    B = dmix.shape[0]

    def body(d_ref, y_ref, dl_ref, dl1_ref, dl2_ref, dm1_ref, dm2_ref, sx, sd):
        lane = lax.broadcasted_iota(jnp.int32, (RT, LANES), 1)
        acc = jnp.zeros((RT, LANES), f32)
        dv = d_ref[...].astype(f32)
        for jj in range(RCH):
            sx[jj] = dv[:, jj * LANES:(jj + 1) * LANES]
        for h in range(HEADS):
            cols = slice(h * HEAD_DIM, (h + 1) * HEAD_DIM)
            dl = jnp.sum(dv[:, cols] * y_ref[:, cols].astype(f32), axis=-1, keepdims=True)
            acc = jnp.where(lane == h, dl, acc)
        dl_ref[...] = acc
        sd[0] = acc
        _to_residues(dl1_ref, sd, DILS[0], LANES)
        _to_residues(dl2_ref, sd, DILS[1], LANES)
        _to_residues(dm1_ref, sx, DILS[0], GW)
        _to_residues(dm2_ref, sx, DILS[1], GW)

    return pl.pallas_call(
        body, grid=(B, SEQ // RT), in_specs=[_tok_spec(GW), _tok_spec(GW)],
        out_specs=[_tok_spec(LANES), _res_spec(DILS[0], LANES), _res_spec(DILS[1], LANES), _res_spec(DILS[0], GW), _res_spec(DILS[1], GW)],
        out_shape=(SDS((B, SEQ, LANES), f32),) + tuple(SDS((B, r, SEQ // r, LANES), f32) for r in DILS)
        + tuple(SDS((B, r, SEQ // r, GW), bf16) for r in DILS),
        scratch_shapes=[pltpu.VMEM((RCH, RT, LANES), f32), pltpu.VMEM((1, RT, LANES), f32)],
        name="attn_delta", compiler_params=_cp(("parallel", "parallel")))(dmix, mix)


N_CB = IN_W // CB


def _assemble_dproj(dqkv, dua, dub, dga, dgc):
    B = dua.shape[0]
    ng = len(GROUPS)
    flat = [dqkv[g][p] for p in range(3) for g in range(ng)]
    wide = [dua, dub, dga, dgc]

    def body(*refs):
        srcs, wides, o_ref, scr = refs[:3 * ng], refs[3 * ng:3 * ng + 4], refs[3 * ng + 4], refs[3 * ng + 5]
        for jv in range(3 * ng):
            g = jv % ng
            if g == 0:
                o_ref[:, jv * GW:(jv + 1) * GW] = srcs[jv][...]
            else:
                _from_residues(scr, srcs[jv], GROUPS[g][1], GW)
                for jj in range(RCH):
                    o_ref[:, jv * GW + jj * LANES:jv * GW + (jj + 1) * LANES] = scr[jj].astype(bf16)
        for wv in range(4):
            lo = 3 * ATTN_W + wv * D_MODEL
            o_ref[:, lo:lo + D_MODEL] = wides[wv][...]

    in_specs = [_tok_spec(GW) if (jv % ng) == 0 else _res_spec(GROUPS[jv % ng][1], GW) for jv in range(3 * ng)]
    in_specs += [_tok_spec(D_MODEL)] * 4
    return pl.pallas_call(
        body, grid=(B, SEQ // RT), in_specs=in_specs, out_specs=_tok_spec(IN_W),
        out_shape=SDS((B, SEQ, IN_W), bf16), scratch_shapes=[pltpu.VMEM((RCH, RT, LANES), f32)],
        name="assemble_dproj", compiler_params=_cp(("parallel", "parallel")))(*flat, *wide)


def _same_shape_groups(names, arrays):
    groups = {}
    for n, a in zip(names, arrays):
        groups.setdefault(a.shape, ([], []))
        groups[a.shape][0].append(n)
        groups[a.shape][1].append(a)
    return list(groups.values())


def _add_pairs(parts, gots, core, name):
    k = len(parts)
    n, h, C = gots[0].shape

    def body(c_ref, *refs):
        for i in range(k):
            refs[2 * k + i][...] = (refs[i][...].astype(f32) + refs[k + i][...].astype(f32)).astype(bf16)

    blk = pl.BlockSpec((None, h, C), lambda s, c_ref: (s, 0, 0))
    own = pl.BlockSpec((None, h, C), lambda s, c_ref: (s, c_ref[0], 0))
    spec = pltpu.PrefetchScalarGridSpec(num_scalar_prefetch=1, grid=(n,), in_specs=[own] * k + [blk] * k, out_specs=[blk] * k)
    return pl.pallas_call(body, grid_spec=spec, out_shape=tuple(SDS((n, h, C), bf16) for _ in range(k)), name=name,
                          compiler_params=_cp(("parallel",)))(core, *parts, *gots)


def _sum4(sums, gots, chip, core, name):
    k = len(sums)
    _, h, C = sums[0].shape

    def body(s_ref, c_ref, *refs):
        for i in range(k):
            q_ref = refs[k + i]
            t = refs[i][...].astype(f32) + q_ref[0].astype(f32)
            t = t + q_ref[1].astype(f32)
            refs[2 * k + i][...] = t + q_ref[2].astype(f32)

    spec = pltpu.PrefetchScalarGridSpec(
        num_scalar_prefetch=2, grid=(1,),
        in_specs=[pl.BlockSpec((None, h, C), lambda i, s_ref, c_ref: (s_ref[0], 0, 0))] * k
        + [pl.BlockSpec((N_CHIPS - 1, h, C), lambda i, s_ref, c_ref: (0, 0, 0))] * k,
        out_specs=[pl.BlockSpec((h, C), lambda i, s_ref, c_ref: (c_ref[0], 0))] * k)
    return pl.pallas_call(body, grid_spec=spec, out_shape=tuple(SDS((2 * h, C), f32) for _ in range(k)), name=name,
                          compiler_params=_cp(("arbitrary",)))(chip, core, *sums, *gots)


def _adamw(ws, gs, ms, vs, name):
    k = len(ws)
    R, C = ws[0].shape
    rt = R
    for cand in (512, 256, 128, 64, 32, 16, 8):
        if R % cand == 0 and cand * C * 4 <= 2 ** 21:
            rt = cand
            break
    c1 = 1.0 / (1.0 - B1 ** STEP)
    c2 = 1.0 / (1.0 - B2 ** STEP)

    def body(*refs):
        for i in range(k):
            w_ref, g_ref, m_ref, v_ref = (refs[j * k + i] for j in range(4))
            d_ref, nm_ref, nv_ref = (refs[(4 + j) * k + i] for j in range(3))
            gv = g_ref[...]
            nm = B1 * m_ref[...] + (1.0 - B1) * gv
            nv = B2 * v_ref[...] + (1.0 - B2) * (gv * gv)
            nm_ref[...] = nm
            nv_ref[...] = nv
            d_ref[...] = -LR * ((nm * c1) / (jnp.sqrt(nv * c2) + ADAM_EPS) + WD * w_ref[...])

    blk = pl.BlockSpec((rt, C), lambda i: (i, 0))
    sd = SDS((R, C), f32)
    res = pl.pallas_call(body, grid=(R // rt,), in_specs=[blk] * (4 * k), out_specs=[blk] * (3 * k), out_shape=(sd,) * (3 * k),
                         name=name, compiler_params=_cp(("parallel",)))(*ws, *gs, *ms, *vs)
    return res[:k], res[k:2 * k], res[2 * k:]


def _coords():
    return lax.axis_index("x"), lax.axis_index("y"), lax.axis_index("c")


def _other_chips(x, y):
    return [(1 - x, y), (x, 1 - y), (1 - x, 1 - y)]


def _allgather_weights(shards, slots):
    n = len(shards)
    halves = [s.shape[0] // 2 for s in shards]

    def body(*refs):
        ins, outs = refs[:n], refs[2 * n:3 * n]
        send1, recv1, send2, recv2 = refs[3 * n:]
        x, y, c = _coords()
        me = 2 * x + y
        chips = _other_chips(x, y)
        sends = []
        for w in range(n):
            mine = pl.ds(c * halves[w], halves[w])
            for j, (px, py) in enumerate(chips):
                cp = pltpu.make_async_remote_copy(src_ref=ins[w].at[mine, :], dst_ref=outs[w].at[me, mine, :],
                                                  send_sem=send1.at[w, j], recv_sem=recv1.at[w, j],
                                                  device_id=(px, py, c), device_id_type=MESH)
                cp.start()
                sends.append(cp)
        for w in range(n):
            mine = pl.ds(c * halves[w], halves[w])
            for j, (px, py) in enumerate(chips):
                blk = outs[w].at[2 * px + py, mine, :]
                pltpu.make_async_remote_copy(src_ref=blk, dst_ref=blk, send_sem=send1.at[w, j], recv_sem=recv1.at[w, j],
                                             device_id=(px, py, c), device_id_type=MESH).wait_recv()
                cp = pltpu.make_async_remote_copy(src_ref=blk, dst_ref=blk, send_sem=send2.at[w, j], recv_sem=recv2.at[w, j],
                                                  device_id=(x, y, 1 - c), device_id_type=MESH)
                cp.start()
                sends.append(cp)
        for w in range(n):
            theirs = pl.ds((1 - c) * halves[w], halves[w])
            for j, (px, py) in enumerate(chips):
                blk = outs[w].at[2 * px + py, theirs, :]
                pltpu.make_async_remote_copy(src_ref=blk, dst_ref=blk, send_sem=send2.at[w, j], recv_sem=recv2.at[w, j],
                                             device_id=(x, y, 1 - c), device_id_type=MESH).wait_recv()
        for cp in sends:
            cp.wait_send()

    return pl.pallas_call(
        body, in_specs=[HBM] * (2 * n), out_specs=[HBM] * n,
        out_shape=tuple(SDS((N_CHIPS,) + s.shape, s.dtype) for s in shards),
        input_output_aliases={n + w: w for w in range(n)},
        scratch_shapes=[pltpu.SemaphoreType.DMA((n, 3))] * 4,
        name="allgather_weights", compiler_params=pltpu.CompilerParams(has_side_effects=True))(*shards, *slots)


def _x_pair_exchange(parts):
    n = len(parts)
    halves = [p.shape[1] // 2 for p in parts]

    def copies(ins, outs, sems):
        send, recv = sems
        x, y, c = _coords()
        return [pltpu.make_async_remote_copy(src_ref=ins[w].at[:, pl.ds((1 - c) * halves[w], halves[w]), :], dst_ref=outs[w],
                                             send_sem=send.at[w], recv_sem=recv.at[w], device_id=(x, y, 1 - c), device_id_type=MESH)
                for w in range(n)]

    def start(ins, outs, sems):
        for cp in copies(ins, outs, sems):
            cp.start()

    def wait(ins, outs, sems):
        for cp in copies(ins, outs, sems):
            cp.wait()

    return _Exchange(ins=list(parts), out_shape=[SDS((N_CHIPS, p.shape[1] // 2, p.shape[2]), p.dtype) for p in parts],
                     scratch=[pltpu.SemaphoreType.DMA((n,))] * 2, aliases={}, start=start, wait=wait)


def _x_chip_exchange(sums):
    n = len(sums)

    def copies(ins, outs, sems):
        send, recv = sems
        x, y, c = _coords()
        return [pltpu.make_async_remote_copy(src_ref=ins[w].at[2 * px + py], dst_ref=outs[w].at[j], send_sem=send.at[w, j],
                                             recv_sem=recv.at[w, j], device_id=(px, py, c), device_id_type=MESH)
                for w in range(n) for j, (px, py) in enumerate(_other_chips(x, y))]

    def start(ins, outs, sems):
        for cp in copies(ins, outs, sems):
            cp.start()

    def wait(ins, outs, sems):
        for cp in copies(ins, outs, sems):
            cp.wait()

    return _Exchange(ins=list(sums), out_shape=[SDS((N_CHIPS - 1,) + s.shape[1:], s.dtype) for s in sums],
                     scratch=[pltpu.SemaphoreType.DMA((n, 3)), pltpu.SemaphoreType.DMA((n, 3))], aliases={}, start=start, wait=wait)


def _x_gather_ici(shards, slots):
    n = len(shards)
    halves = [s.shape[0] // 2 for s in shards]

    def copies(ins, outs, sems):
        send, recv = sems
        x, y, c = _coords()
        me = 2 * x + y
        out = []
        for w in range(n):
            mine = pl.ds(c * halves[w], halves[w])
            for j, (px, py) in enumerate(_other_chips(x, y)):
                snd = pltpu.make_async_remote_copy(src_ref=ins[w].at[mine, :], dst_ref=outs[w].at[me, mine, :], send_sem=send.at[w, j],
                                                   recv_sem=recv.at[w, j], device_id=(px, py, c), device_id_type=MESH)
                got = outs[w].at[2 * px + py, mine, :]
                rcv = pltpu.make_async_remote_copy(src_ref=got, dst_ref=got, send_sem=send.at[w, j], recv_sem=recv.at[w, j],
                                                   device_id=(px, py, c), device_id_type=MESH)
                out.append((snd, rcv))
        return out

    def start(ins, outs, sems):
        for snd, _ in copies(ins, outs, sems):
            snd.start()

    def wait(ins, outs, sems):
        for snd, rcv in copies(ins, outs, sems):
            rcv.wait_recv()
            snd.wait_send()

    return _Exchange(ins=list(shards) + list(slots), out_shape=[SDS(s.shape, s.dtype) for s in slots],
                     scratch=[pltpu.SemaphoreType.DMA((n, 3)), pltpu.SemaphoreType.DMA((n, 3))],
                     aliases={n + w: w for w in range(n)}, start=start, wait=wait)


def _x_gather_fwd(bufs):
    n = len(bufs)
    halves = [b.shape[1] // 2 for b in bufs]

    def copies(ins, outs, sems):
        send, recv = sems
        x, y, c = _coords()
        out = []
        for w in range(n):
            for j, (px, py) in enumerate(_other_chips(x, y)):
                mine = outs[w].at[2 * px + py, pl.ds(c * halves[w], halves[w]), :]
                theirs = outs[w].at[2 * px + py, pl.ds((1 - c) * halves[w], halves[w]), :]
                snd = pltpu.make_async_remote_copy(src_ref=mine, dst_ref=mine, send_sem=send.at[w, j], recv_sem=recv.at[w, j],
                                                   device_id=(x, y, 1 - c), device_id_type=MESH)
                rcv = pltpu.make_async_remote_copy(src_ref=theirs, dst_ref=theirs, send_sem=send.at[w, j], recv_sem=recv.at[w, j],
                                                   device_id=(x, y, 1 - c), device_id_type=MESH)
                out.append((snd, rcv))
        return out

    def start(ins, outs, sems):
        for snd, _ in copies(ins, outs, sems):
            snd.start()

    def wait(ins, outs, sems):
        for snd, rcv in copies(ins, outs, sems):
            rcv.wait_recv()
            snd.wait_send()

    return _Exchange(ins=list(bufs), out_shape=[SDS(b.shape, b.dtype) for b in bufs],
                     scratch=[pltpu.SemaphoreType.DMA((n, 3)), pltpu.SemaphoreType.DMA((n, 3))],
                     aliases={w: w for w in range(n)}, start=start, wait=wait)


def _pair_gather(bufs):
    n = len(bufs)

    def body(*refs):
        outs = refs[n:2 * n]
        send, recv = refs[2 * n:]
        x, y, c = _coords()
        cps = []
        for w in range(n):
            h = bufs[w].shape[0] // 2
            rows = outs[w].at[pl.ds(c * h, h), :]
            rc = pltpu.make_async_remote_copy(src_ref=rows, dst_ref=rows, send_sem=send.at[w], recv_sem=recv.at[w],
                                              device_id=(x, y, 1 - c), device_id_type=MESH)
            rc.start()
            cps.append(rc)
        for w, rc in enumerate(cps):
            h = bufs[w].shape[0] // 2
            other = outs[w].at[pl.ds((1 - c) * h, h), :]
            pltpu.make_async_remote_copy(src_ref=other, dst_ref=other, send_sem=send.at[w], recv_sem=recv.at[w],
                                         device_id=(x, y, 1 - c), device_id_type=MESH).wait_recv()
            rc.wait_send()

    return pl.pallas_call(body, in_specs=[HBM] * n, out_specs=[HBM] * n, out_shape=tuple(SDS(a.shape, a.dtype) for a in bufs),
                          input_output_aliases={w: w for w in range(n)},
                          scratch_shapes=[pltpu.SemaphoreType.DMA((n,))] * 2, name="grad_pair_gather",
                          compiler_params=pltpu.CompilerParams(has_side_effects=True))(*bufs)


def _small_allreduce(vec, name):
    R = vec.shape[0]
    nd = 8

    def body(v_ref, o_ref, buf, send, recv):
        x, y, c = _coords()
        me = 4 * x + 2 * y + c
        buf[me] = v_ref[...]
        cps = []
        for k in range(1, nd):
            kx, ky, kc = (k >> 2) & 1, (k >> 1) & 1, k & 1
            tx = x + kx - 2 * x * kx
            ty = y + ky - 2 * y * ky
            tc = c + kc - 2 * c * kc
            cp = pltpu.make_async_remote_copy(src_ref=v_ref, dst_ref=buf.at[me], send_sem=send.at[k], recv_sem=recv.at[k],
                                              device_id=(tx, ty, tc), device_id_type=MESH)
            cp.start()
            cps.append((cp, 4 * tx + 2 * ty + tc))
        for k, (cp, src) in zip(range(1, nd), cps):
            pltpu.make_async_remote_copy(src_ref=v_ref, dst_ref=buf.at[src], send_sem=send.at[k], recv_sem=recv.at[k],
                                         device_id=(x, y, c), device_id_type=MESH).wait_recv()
        for cp, _ in cps:
            cp.wait_send()
        acc = buf[0]
        for d in range(1, nd):
            acc = acc + buf[d]
        o_ref[...] = acc

    vm = pl.BlockSpec(memory_space=pltpu.VMEM)
    return pl.pallas_call(body, in_specs=[vm], out_specs=vm, out_shape=SDS((R, LANES), f32),
                          scratch_shapes=[pltpu.VMEM((nd, R, LANES), f32), pltpu.SemaphoreType.DMA((nd,)), pltpu.SemaphoreType.DMA((nd,))],
                          name=name, compiler_params=pltpu.CompilerParams(has_side_effects=True))(vec)


def _rows128(a, rows):
    flat = a.reshape(-1)
    return jnp.pad(flat, (0, rows * LANES - flat.shape[0])).reshape(rows, LANES)


GATHER_1 = ("w_conv_out", "w_attn_out", "w_o", "w_ffn_gate")
GATHER_2 = ("w_ffn_up", "w_ffn_down")
REDUCE_A = ("w_ffn_gate", "w_ffn_up", "w_ffn_down")
REDUCE_B = ("w_conv_out", "w_attn_out", "w_o")
REDUCE_C = ("w_in",)


def _step(x, target, norm1_g, gate_b, conv_w, conv_b, conv_ln_g, conv_ln_b, norm2_g, norm_f_g, shards, slots, chip1, core):
    B = x.shape[0]
    T = B * SEQ
    xf = x.reshape(T, D_MODEL)
    tf = target.reshape(T, D_MODEL)

    (w_in,) = _allgather_weights([shards["w_in"]], [slots["w_in"]])
    h, h_t = _rms_fwd(xf, norm1_g, "rms1_fwd")
    proj, got1 = _mm("in_proj", h, w_in, grid=(N_CHIPS, T // 1024),
                     a_spec=pl.BlockSpec((1024, D_MODEL), lambda s, m: (m, 0)),
                     b_spec=pl.BlockSpec((None, D_MODEL, IN_S), lambda s, m: (s, 0, 0)),
                     o_spec=pl.BlockSpec((1024, IN_S), lambda s, m: (m, s)), o_shape=(T, IN_W), o_dtype=bf16, dims=NN,
                     comm=_x_gather_ici([shards[n] for n in GATHER_1], [slots[n] for n in GATHER_1]))
    proj3 = proj.reshape(B, SEQ, IN_W)
    proj4 = proj.reshape(B, 1, SEQ, IN_W)

    qkv = [None] + [_qkv_to_residues(proj3, g) for g in range(1, len(GROUPS))]

    def qkv_args(g):
        return (proj4, proj4, proj4, 0, 3, 6) if g == 0 else (qkv[g], qkv[g], qkv[g], 0, 1, 2)

    (o4_0, l4_0), full1 = _attn_fwd(0, *qkv_args(0), comm=_x_gather_fwd(list(got1)))
    o4_1, l4_1 = _attn_fwd(1, *qkv_args(1))
    o4_2, l4_2 = _attn_fwd(2, *qkv_args(2))
    full = dict(zip(GATHER_1, full1))
    w_conv_out, w_attn_out, w_o, w_gate = (full[n] for n in GATHER_1)
    w_conv_out_f = w_conv_out.reshape(D_MODEL, D_MODEL)
    w_o_f = w_o.reshape(D_MODEL, D_MODEL)
    mix3, lse3, lse_r1, lse_r2 = _attn_mix(o4_0.reshape(B, SEQ, GW), l4_0.reshape(B, SEQ, LANES), [o4_1, o4_2], [l4_1, l4_2])
    mix = mix3.reshape(T, GW)
    y_attn = _mm("attn_out", mix, w_attn_out, grid=(N_CHIPS, T // 512),
                 a_spec=pl.BlockSpec((512, GW), lambda s, m: (m, 0)),
                 b_spec=pl.BlockSpec((None, GW, D_MODEL // N_CHIPS), lambda s, m: (s, 0, 0)),
                 o_spec=pl.BlockSpec((512, D_MODEL // N_CHIPS), lambda s, m: (m, s)), o_shape=(T, D_MODEL), o_dtype=bf16,
                 dims=NN, sem=("parallel", "parallel"))

    c1, got2 = _glu_conv_fwd(proj3, conv_w, conv_b, comm=_x_gather_ici([shards[n] for n in GATHER_2], [slots[n] for n in GATHER_2]))
    c1 = c1.reshape(T, D_MODEL)
    c3 = _ln_silu_fwd(c1, conv_ln_g, conv_ln_b)
    y_conv, (w_up, w_down) = _mm_nn_full("conv_out", c3, w_conv_out_f, bf16, comm=_x_gather_fwd(list(got2)))

    merged = _merge_fwd(proj, gate_b, y_attn, y_conv)
    x1, h2 = _o_proj_rms(merged, w_o_f, xf, norm2_g)
    fa, fb, ff = _ffn_up(h2, w_gate, w_up)
    loss, dx2, d_gf = _ffn_down_loss(ff, w_down, x1, norm_f_g.reshape(1, D_MODEL), tf)

    d_w_down = _mm("d_w_down", ff, dx2, grid=(N_CHIPS, T // 1024),
                   a_spec=pl.BlockSpec((None, 1024, FF_S), lambda s, k: (s, k, 0)),
                   b_spec=pl.BlockSpec((1024, D_MODEL), lambda s, k: (k, 0)),
                   o_spec=pl.BlockSpec((None, FF_S, D_MODEL), lambda s, k: (s, 0, 0)), o_shape=(N_CHIPS, FF_S, D_MODEL),
                   o_dtype=bf16, dims=TN, acc_shape=(FF_S, D_MODEL), k_axis=1, sem=("parallel", "arbitrary"))
    da, db = _ffn_down_bwd(dx2, w_down, fa, fb)

    def d_w_ff(name, dz):
        return _mm(name, dz, h2, grid=(N_CHIPS, T // 2048),
                   a_spec=pl.BlockSpec((None, 2048, FF_S), lambda s, k: (s, k, 0)),
                   b_spec=pl.BlockSpec((2048, D_MODEL), lambda s, k: (k, 0)),
                   o_spec=pl.BlockSpec((None, FF_S, D_MODEL), lambda s, k: (s, 0, 0)), o_shape=(N_CHIPS, FF_S, D_MODEL),
                   o_dtype=bf16, dims=TN, acc_shape=(FF_S, D_MODEL), k_axis=1, sem=("parallel", "arbitrary"))

    d_w_gate = d_w_ff("d_w_gate", da)
    d_w_up = d_w_ff("d_w_up", db)
    part = dict(w_ffn_gate=d_w_gate, w_ffn_up=d_w_up, w_ffn_down=d_w_down)

    def pair_sums(names, got):
        out = {}
        for ns, gs in _same_shape_groups(names, got):
            out.update(zip(ns, _add_pairs([part[n] for n in ns], gs, core, "pair_sum_" + ns[0])))
        return [out[n] for n in names]

    (dx1, d_g2), got = _mm_rms_bwd("ffn_dh2", [(da, w_gate), (db, w_up)], x1, norm2_g, dx2, tm=1024,
                                   a_spec=pl.BlockSpec((None, 1024, FF_S), lambda m, s: (s, m, 0)),
                                   b_spec=pl.BlockSpec((None, FF_S, D_MODEL), lambda m, s: (s, 0, 0)), dims=NN,
                                   comm=_x_pair_exchange([part[n] for n in REDUCE_A]), vmem=VMEM_LIMIT_BIG)
    sums_a = pair_sums(REDUCE_A, got)

    d_w_o = _mm_tn_tokens("d_w_o", merged, dx1, bf16).reshape(N_CHIPS, D_MODEL // N_CHIPS, D_MODEL)
    dmerged = _mm_nt_full("d_merged", dx1, w_o_f, bf16)
    dya, dyc, dga, dgc, d_gba, d_gbc = _merge_bwd(dmerged, proj, gate_b, y_attn, y_conv)

    d_w_conv_out = _mm_tn_tokens("d_w_conv_out", c3, dyc, bf16).reshape(N_CHIPS, D_MODEL // N_CHIPS, D_MODEL)
    dc1, d_ln_g, d_ln_b = _conv_out_bwd(dyc, w_conv_out_f, c1, conv_ln_g, conv_ln_b)
    (dua, dub, d_conv_w, d_conv_b), got_a = _glu_conv_bwd(dc1.reshape(B, SEQ, D_MODEL), proj3, conv_w, comm=_x_chip_exchange(sums_a))

    d_w_attn_out = _mm("d_w_attn_out", mix, dya, grid=(N_CHIPS, T // 512),
                       a_spec=pl.BlockSpec((512, GW), lambda s, k: (k, 0)),
                       b_spec=pl.BlockSpec((512, D_MODEL // N_CHIPS), lambda s, k: (k, s)),
                       o_spec=pl.BlockSpec((None, GW, D_MODEL // N_CHIPS), lambda s, k: (s, 0, 0)),
                       o_shape=(N_CHIPS, GW, D_MODEL // N_CHIPS), o_dtype=bf16, dims=TN, acc_shape=(GW, D_MODEL // N_CHIPS),
                       k_axis=1, sem=("parallel", "arbitrary"))
    part.update(w_conv_out=d_w_conv_out, w_attn_out=d_w_attn_out, w_o=d_w_o)
    dmix, got = _mm("d_mix", dya, w_attn_out, grid=(T // 1024, N_CHIPS),
                    a_spec=pl.BlockSpec((1024, D_MODEL // N_CHIPS), lambda m, s: (m, s)),
                    b_spec=pl.BlockSpec((None, GW, D_MODEL // N_CHIPS), lambda m, s: (s, 0, 0)),
                    o_spec=pl.BlockSpec((1024, GW), lambda m, s: (m, 0)), o_shape=(T, GW), o_dtype=bf16, dims=NT,
                    acc_shape=(1024, GW), k_axis=1, comm=_x_pair_exchange([part[n] for n in REDUCE_B]))
    sums_b = pair_sums(REDUCE_B, got)
    dmix3 = dmix.reshape(B, SEQ, GW)
    delta3, delta_r1, delta_r2, dmix_r1, dmix_r2 = _attn_delta(dmix3, mix3)
    one = (B, 1, SEQ)
    (dq0, dk0, dv0), got_b = _attn_bwd(0, *qkv_args(0), dmix3.reshape(one + (GW,)), lse3.reshape(one + (LANES,)),
                                       delta3.reshape(one + (LANES,)), comm=_x_chip_exchange(sums_b))
    dqkv = [tuple(t.reshape(B, SEQ, GW) for t in (dq0, dk0, dv0)),
            _attn_bwd(1, *qkv_args(1), dmix_r1, lse_r1, delta_r1), _attn_bwd(2, *qkv_args(2), dmix_r2, lse_r2, delta_r2)]
    dproj = _assemble_dproj(dqkv, dua, dub, dga.reshape(B, SEQ, D_MODEL), dgc.reshape(B, SEQ, D_MODEL)).reshape(T, IN_W)

    d_w_in = _mm("d_w_in", h_t, dproj, grid=(N_CHIPS, T // 1024),
                 a_spec=pl.BlockSpec((D_MODEL, 1024), lambda s, k: (0, k)),
                 b_spec=pl.BlockSpec((1024, IN_S), lambda s, k: (k, s)),
                 o_spec=pl.BlockSpec((None, D_MODEL, IN_S), lambda s, k: (s, 0, 0)), o_shape=(N_CHIPS, D_MODEL, IN_S),
                 o_dtype=bf16, dims=NN, acc_shape=(D_MODEL, IN_S), k_axis=1, sem=("parallel", "arbitrary"))
    part.update(w_in=d_w_in)
    sums_c = pair_sums(REDUCE_C, _run_exchange(_x_pair_exchange([d_w_in]), "grad_pair_exchange_c"))
    (dx, d_g1), got_c = _mm_rms_bwd("d_h", [(dproj, w_in)], xf, norm1_g, dx1, tm=512,
                                    a_spec=pl.BlockSpec((512, IN_S), lambda m, s: (m, s)),
                                    b_spec=pl.BlockSpec((None, D_MODEL, IN_S), lambda m, s: (s, 0, 0)), dims=NT,
                                    comm=_x_chip_exchange(sums_c))

    names = REDUCE_A + REDUCE_B + REDUCE_C
    sums = dict(zip(names, sums_a + sums_b + sums_c))
    halves = {}
    for ns, gs in _same_shape_groups(names, got_a + got_b + got_c):
        halves.update(zip(ns, _sum4([sums[n] for n in ns], gs, chip1, core, "chip_sum_" + ns[0])))
    big = dict(zip(names, _pair_gather([halves[n] for n in names])))
    small = dict(norm1_g=d_g1, gate_b=jnp.concatenate([d_gba, d_gbc], axis=-1), conv_b=d_conv_b, conv_ln_g=d_ln_g,
                 conv_ln_b=d_ln_b, norm2_g=d_g2, norm_f_g=d_gf, conv_w=d_conv_w)
    return loss, dx.reshape(B, SEQ, D_MODEL), big, small


BIG = ("w_in", "w_conv_out", "w_attn_out", "w_o", "w_ffn_gate", "w_ffn_up", "w_ffn_down")
TRANSPOSED = ("w_ffn_gate", "w_ffn_up")
SMALL = ("norm1_g", "gate_b", "conv_b", "conv_ln_g", "conv_ln_b", "norm2_g", "norm_f_g")
SMALL_ROWS = {"norm1_g": 8, "gate_b": 16, "conv_b": 8, "conv_ln_g": 8, "conv_ln_b": 8, "norm2_g": 8, "norm_f_g": 8}
LOSS_ROWS = 8
CONVW_ROWS = 32 * D_MODEL // LANES


def kernel(x, norm1_g, w_in, gate_b, conv_w, conv_b, conv_ln_g, conv_ln_b, w_conv_out, w_attn_out, w_o, norm2_g, w_ffn_gate, w_ffn_up, w_ffn_down, norm_f_g, loss_target, m_norm1_g, m_w_in, m_gate_b, m_conv_w, m_conv_b, m_conv_ln_g, m_conv_ln_b, m_w_conv_out, m_w_attn_out, m_w_o, m_norm2_g, m_w_ffn_gate, m_w_ffn_up, m_w_ffn_down, m_norm_f_g, v_norm1_g, v_w_in, v_gate_b, v_conv_w, v_conv_b, v_conv_ln_g, v_conv_ln_b, v_w_conv_out, v_w_attn_out, v_w_o, v_norm2_g, v_w_ffn_gate, v_w_ffn_up, v_w_ffn_down, v_norm_f_g):
    W = dict(norm1_g=norm1_g, w_in=w_in, gate_b=gate_b, conv_w=conv_w, conv_b=conv_b, conv_ln_g=conv_ln_g, conv_ln_b=conv_ln_b,
             w_conv_out=w_conv_out, w_attn_out=w_attn_out, w_o=w_o, norm2_g=norm2_g, w_ffn_gate=w_ffn_gate, w_ffn_up=w_ffn_up,
             w_ffn_down=w_ffn_down, norm_f_g=norm_f_g)
    M = dict(norm1_g=m_norm1_g, w_in=m_w_in, gate_b=m_gate_b, conv_w=m_conv_w, conv_b=m_conv_b, conv_ln_g=m_conv_ln_g,
             conv_ln_b=m_conv_ln_b, w_conv_out=m_w_conv_out, w_attn_out=m_w_attn_out, w_o=m_w_o, norm2_g=m_norm2_g,
             w_ffn_gate=m_w_ffn_gate, w_ffn_up=m_w_ffn_up, w_ffn_down=m_w_ffn_down, norm_f_g=m_norm_f_g)
    V = dict(norm1_g=v_norm1_g, w_in=v_w_in, gate_b=v_gate_b, conv_w=v_conv_w, conv_b=v_conv_b, conv_ln_g=v_conv_ln_g,
             conv_ln_b=v_conv_ln_b, w_conv_out=v_w_conv_out, w_attn_out=v_w_attn_out, w_o=v_w_o, norm2_g=v_norm2_g,
             w_ffn_gate=v_w_ffn_gate, w_ffn_up=v_w_ffn_up, w_ffn_down=v_w_ffn_down, norm_f_g=v_norm_f_g)
    order = list(W)

    def as2d(n, a):
        a = a.reshape(a.shape[-2:])
        return a.T if n in TRANSPOSED else a

    def from2d(n, a):
        return (a.T if n in TRANSPOSED else a).reshape(W[n].shape)

    shard2d = {n: as2d(n, W[n]) for n in BIG}
    chip = 2 * lax.axis_index("x") + lax.axis_index("y")

    cw = jnp.zeros((CONV_K, D_MODEL), f32)
    cw = lax.dynamic_update_slice(cw, 0.5 * conv_w.reshape(CONV_K, D_MODEL // N_CHIPS), (0, chip * (D_MODEL // N_CHIPS)))
    conv_w_full = _small_allreduce(_rows128(cw, CONVW_ROWS), "conv_w_gather")[: CONV_K * D_MODEL // LANES].reshape(CONV_K, D_MODEL)

    core = lax.axis_index("c").astype(jnp.int32).reshape(1)
    chip1 = chip.astype(jnp.int32).reshape(1)
    shards = {n: shard2d[n].astype(bf16) for n in BIG}
    slots = {n: lax.dynamic_update_slice(lax.empty((N_CHIPS,) + s.shape, bf16), s[None], (chip, 0, 0)) for n, s in shards.items()}

    loss, grad_x, grads, gsmall = _step(x, loss_target, norm1_g, gate_b, conv_w_full, conv_b, conv_ln_g, conv_ln_b, norm2_g,
                                        norm_f_g, shards, slots, chip1, core)

    pieces = [_rows128(loss, LOSS_ROWS)] + [_rows128(gsmall[n], SMALL_ROWS[n]) for n in SMALL] + [_rows128(gsmall["conv_w"], CONVW_ROWS)]
    tot = _small_allreduce(jnp.concatenate(pieces, axis=0), "small_allreduce")
    loss_out = tot[0, 0]
    row = LOSS_ROWS
    for n in SMALL:
        grads[n] = tot[row: row + W[n].size // LANES].reshape(W[n].shape)
        row += SMALL_ROWS[n]
    dcw = tot[row: row + CONV_K * D_MODEL // LANES].reshape(CONV_K, D_MODEL)
    grads["conv_w"] = lax.dynamic_slice(dcw, (0, chip * (D_MODEL // N_CHIPS)), (CONV_K, D_MODEL // N_CHIPS))

    delta, new_m, new_v = {}, {}, {}
    for ns, ws in _same_shape_groups(BIG, [shard2d[n] for n in BIG]):
        ds, nms, nvs = _adamw(ws, [grads[n] for n in ns], [as2d(n, M[n]) for n in ns], [as2d(n, V[n]) for n in ns], "adamw_" + ns[0])
        for n, d, nm, nv in zip(ns, ds, nms, nvs):
            delta[n], new_m[n], new_v[n], grads[n] = (from2d(n, t) for t in (d, nm, nv, grads[n]))

    def pack(src):
        return jnp.concatenate([_rows128(src[n], SMALL_ROWS[n]) for n in SMALL], axis=0)

    (d,), (nm,), (nv,) = _adamw([pack(W)], [pack(grads)], [pack(M)], [pack(V)], "adamw_small")
    row = 0
    for n in SMALL:
        k = W[n].size // LANES
        delta[n], new_m[n], new_v[n] = (t[row: row + k].reshape(W[n].shape) for t in (d, nm, nv))
        row += SMALL_ROWS[n]

    def pad32(a):
        return jnp.pad(a.reshape(CONV_K, D_MODEL // N_CHIPS), ((0, 1), (0, 0)))

    (d,), (nm,), (nv,) = _adamw([pad32(conv_w)], [pad32(grads["conv_w"])], [pad32(m_conv_w)], [pad32(v_conv_w)], "adamw_conv_w")
    delta["conv_w"], new_m["conv_w"], new_v["conv_w"] = (t[:CONV_K].reshape(conv_w.shape) for t in (d, nm, nv))
    grads["conv_w"] = grads["conv_w"].reshape(conv_w.shape)

    return (loss_out, grad_x, *[grads[n] for n in order], *[delta[n] for n in order],
            *[new_m[n] for n in order], *[new_v[n] for n in order])
```

```python
import functools
import math
from typing import Callable, NamedTuple

import numpy as np
import jax
import jax.numpy as jnp
from jax import lax
from jax.experimental import pallas as pl
from jax.experimental.pallas import tpu as pltpu

f32 = jnp.float32
bf16 = jnp.bfloat16
SDS = jax.ShapeDtypeStruct
MESH = pl.DeviceIdType.MESH

D_MODEL = 1024
SEQ = 2048
HEAD_DIM = 64
HEADS = 8
GROUPS = ((128, 1), (512, 4), (2048, 16))
GW = HEADS * HEAD_DIM
ATTN_W = len(GROUPS) * GW
Q_BLOCK = 128
CONV_K = 31
D_FF = 2816
IN_W = 3 * ATTN_W + 2 * D_MODEL + 2 * D_MODEL
N_CHIPS = 4
IN_S = IN_W // N_CHIPS
FF_S = D_FF // N_CHIPS
RMS_EPS = 1e-6
LN_EPS = 1e-5
LR, B1, B2, ADAM_EPS, WD, STEP = 0.001, 0.9, 0.999, 1e-08, 0.01, 10
NEG = -1e30
LANES = 128
VMEM_LIMIT = 48 * 2 ** 20
VMEM_LIMIT_BIG = 52 * 2 ** 20
CB = 512
UA_CB, UB_CB, GA_CB, GC_CB = 9, 11, 13, 15


def _alibi_slope_list(n):
    def pow2(m):
        start = 2.0 ** (-8.0 / m)
        return [start ** (i + 1) for i in range(m)]
    if math.log2(n).is_integer():
        return pow2(n)
    c = 2 ** math.floor(math.log2(n))
    return pow2(c) + _alibi_slope_list(2 * c)[0::2][: n - c]


_SLOPES = np.asarray(sorted(_alibi_slope_list(len(GROUPS) * HEADS), reverse=True), dtype=np.float32).reshape(len(GROUPS), HEADS)


def _cp(sem=None, vmem=VMEM_LIMIT):
    return pltpu.CompilerParams(dimension_semantics=sem, vmem_limit_bytes=vmem)


def _sigmoid(x):
    return 1.0 / (1.0 + jnp.exp(-x))


HBM = pl.BlockSpec(memory_space=pl.ANY)


class _Exchange(NamedTuple):
    ins: list
    out_shape: list
    scratch: list
    aliases: dict
    start: Callable
    wait: Callable


def _launch(body, *, name, grid, in_specs, out_specs, out_shape, args, scratch_shapes=(), sem=None, comm=None, vmem=VMEM_LIMIT):
    if comm is None:
        return pl.pallas_call(body, grid=grid, in_specs=in_specs, out_specs=out_specs, out_shape=out_shape,
                              scratch_shapes=list(scratch_shapes), name=name, compiler_params=_cp(sem, vmem))(*args)
    multi = isinstance(out_shape, (tuple, list))
    m_out = list(out_shape) if multi else [out_shape]
    m_ospec = list(out_specs) if multi else [out_specs]
    n_in, n_out, n_scr = len(in_specs), len(m_out), len(scratch_shapes)
    nc_in, nc_out = len(comm.ins), len(comm.out_shape)

    def hosted(*refs):
        bounds = np.cumsum([0, n_in, nc_in, n_out, nc_out, n_scr])
        mi, ci, mo, co, ms = (refs[a:b] for a, b in zip(bounds[:-1], bounds[1:]))
        cs = refs[bounds[-1]:]
        ids = [pl.program_id(a) for a in range(len(grid))]
        first = functools.reduce(jnp.logical_and, [i == 0 for i in ids])
        last = functools.reduce(jnp.logical_and, [i == g - 1 for i, g in zip(ids, grid)])

        @pl.when(first)
        def _():
            comm.start(ci, co, cs)

        body(*mi, *mo, *ms)

        @pl.when(last)
        def _():
            comm.wait(ci, co, cs)

    res = pl.pallas_call(
        hosted, grid=grid, in_specs=list(in_specs) + [HBM] * nc_in, out_specs=m_ospec + [HBM] * nc_out,
        out_shape=tuple(m_out) + tuple(comm.out_shape), scratch_shapes=list(scratch_shapes) + list(comm.scratch),
        input_output_aliases={n_in + i: n_out + o for i, o in comm.aliases.items()}, name=name + "_comm",
        compiler_params=pltpu.CompilerParams(dimension_semantics=("arbitrary",) * len(grid), vmem_limit_bytes=vmem,
                                             has_side_effects=True))(*args, *comm.ins)
    return (tuple(res[:n_out]) if multi else res[0]), tuple(res[n_out:])


def _run_exchange(ex, name):
    n_in = len(ex.ins)

    def body(*refs):
        ins, outs, sems = refs[:n_in], refs[n_in:n_in + len(ex.out_shape)], refs[n_in + len(ex.out_shape):]
        ex.start(ins, outs, sems)
        ex.wait(ins, outs, sems)

    return pl.pallas_call(body, in_specs=[HBM] * n_in, out_specs=[HBM] * len(ex.out_shape), out_shape=tuple(ex.out_shape),
                          scratch_shapes=list(ex.scratch), input_output_aliases=dict(ex.aliases), name=name,
                          compiler_params=pltpu.CompilerParams(has_side_effects=True))(*ex.ins)


def _rmsnorm(xv, gv):
    return xv * lax.rsqrt(jnp.mean(xv * xv, axis=-1, keepdims=True) + RMS_EPS) * gv


def _rms_fwd(x, g, name):
    T = x.shape[0]
    tm = 512

    def body(x_ref, g_ref, o_ref, ot_ref):
        hv = _rmsnorm(x_ref[...], g_ref[...])
        o_ref[...] = hv.astype(bf16)
        ot_ref[...] = hv.T.astype(bf16)

    row = pl.BlockSpec((tm, D_MODEL), lambda i: (i, 0))
    vec = pl.BlockSpec((1, D_MODEL), lambda i: (0, 0))
    return pl.pallas_call(body, grid=(T // tm,), in_specs=[row, vec], out_specs=[row, pl.BlockSpec((D_MODEL, tm), lambda i: (0, i))],
                          out_shape=(SDS((T, D_MODEL), bf16), SDS((D_MODEL, T), bf16)), name=name,
                          compiler_params=_cp(("parallel",)))(x, g)


def _rms_bwd_tile(dyv, xv, gv, dres):
    r = lax.rsqrt(jnp.mean(xv * xv, axis=-1, keepdims=True) + RMS_EPS)
    xh = xv * r
    dxh = dyv * gv
    dx = dres + r * (dxh - xh * jnp.mean(dxh * xh, axis=-1, keepdims=True))
    return dx, jnp.sum(dyv * xh, axis=0, keepdims=True)


def _accumulate(first, refs_parts):
    @pl.when(first)
    def _():
        for ref, part in refs_parts:
            ref[...] = part

    @pl.when(jnp.logical_not(first))
    def _():
        for ref, part in refs_parts:
            ref[...] += part


def _mm_rms_bwd(name, ops, x, g, dres, *, tm, a_spec, b_spec, dims, comm=None, vmem=VMEM_LIMIT):
    T = x.shape[0]
    n = len(ops)

    def body(*refs):
        ab_refs = refs[:2 * n]
        x_ref, g_ref, r_ref, dx_ref, dg_ref, acc = refs[2 * n:]
        m, s = pl.program_id(0), pl.program_id(1)
        p = sum(lax.dot_general(ab_refs[2 * i][...], ab_refs[2 * i + 1][...], dims, preferred_element_type=f32) for i in range(n))

        @pl.when(s == 0)
        def _():
            acc[...] = p

        @pl.when(s > 0)
        def _():
            acc[...] += p

        @pl.when(s == N_CHIPS - 1)
        def _():
            dx, part = _rms_bwd_tile(acc[...], x_ref[...], g_ref[...], r_ref[...])
            dx_ref[...] = dx
            _accumulate(m == 0, [(dg_ref, part)])

    row = pl.BlockSpec((tm, D_MODEL), lambda m, s: (m, 0))
    vec = pl.BlockSpec((1, D_MODEL), lambda m, s: (0, 0))
    return _launch(body, name=name, grid=(T // tm, N_CHIPS), in_specs=[a_spec, b_spec] * n + [row, vec, row], out_specs=[row, vec],
                   out_shape=(SDS((T, D_MODEL), f32), SDS((1, D_MODEL), f32)), args=tuple(t for ab in ops for t in ab) + (x, g, dres),
                   scratch_shapes=[pltpu.VMEM((tm, D_MODEL), f32)], sem=("arbitrary", "arbitrary"), comm=comm, vmem=vmem)


def _ffn_down_loss(ff, wd, x1, gf, target):
    T = x1.shape[0]
    tm = 1024
    nm = T // tm

    def body(f_ref, w_ref, x_ref, g_ref, t_ref, loss_ref, dx_ref, dg_ref, acc):
        m, s = pl.program_id(0), pl.program_id(1)
        p = jnp.dot(f_ref[...], w_ref[...], preferred_element_type=f32)

        @pl.when(s == 0)
        def _():
            acc[...] = p

        @pl.when(s > 0)
        def _():
            acc[...] += p

        @pl.when(s == N_CHIPS - 1)
        def _():
            xv = acc[...] + x_ref[...]
            gv = g_ref[...]
            r = lax.rsqrt(jnp.mean(xv * xv, axis=-1, keepdims=True) + RMS_EPS)
            xh = xv * r
            e = xh * gv - t_ref[...]
            part_l = jnp.broadcast_to(0.5 * jnp.sum(jnp.mean(e * e, axis=-1, keepdims=True), axis=0, keepdims=True), (1, LANES))
            dy = e * (1.0 / D_MODEL)
            dxh = dy * gv
            dx_ref[...] = r * (dxh - xh * jnp.mean(dxh * xh, axis=-1, keepdims=True))
            part_g = jnp.sum(dy * xh, axis=0, keepdims=True)

            @pl.when(m == 0)
            def _():
                loss_ref[...] = part_l
                dg_ref[...] = part_g

            @pl.when(m > 0)
            def _():
                loss_ref[...] += part_l
                dg_ref[...] += part_g

    row = pl.BlockSpec((tm, D_MODEL), lambda m, s: (m, 0))
    vec = pl.BlockSpec((1, D_MODEL), lambda m, s: (0, 0))
    return pl.pallas_call(
        body, grid=(nm, N_CHIPS),
        in_specs=[pl.BlockSpec((None, tm, FF_S), lambda m, s: (s, m, 0)), pl.BlockSpec((None, FF_S, D_MODEL), lambda m, s: (s, 0, 0)),
                  row, vec, row],
        out_specs=[pl.BlockSpec((1, LANES), lambda m, s: (0, 0)), row, vec],
        out_shape=(SDS((1, LANES), f32), SDS((T, D_MODEL), f32), SDS((1, D_MODEL), f32)),
        scratch_shapes=[pltpu.VMEM((tm, D_MODEL), f32)], name="ffn_down_loss", compiler_params=_cp(("arbitrary", "arbitrary")))(
            ff, wd, x1, gf, target)


def _o_proj_rms(merged, w_o, x, g2):
    T = x.shape[0]
    tm = 1024

    def body(a_ref, w_ref, x_ref, g_ref, x1_ref, h2_ref):
        x1 = jnp.dot(a_ref[...], w_ref[...], preferred_element_type=f32) + x_ref[...]
        x1_ref[...] = x1
        h2_ref[...] = _rmsnorm(x1, g_ref[...]).astype(bf16)

    row = pl.BlockSpec((tm, D_MODEL), lambda i: (i, 0))
    return pl.pallas_call(body, grid=(T // tm,),
                          in_specs=[row, pl.BlockSpec((D_MODEL, D_MODEL), lambda i: (0, 0)), row, pl.BlockSpec((1, D_MODEL), lambda i: (0, 0))],
                          out_specs=[row, row], out_shape=(SDS((T, D_MODEL), f32), SDS((T, D_MODEL), bf16)), name="o_proj_rms",
                          compiler_params=_cp(("parallel",)))(merged, w_o, x, g2)


def _ln_silu_fwd(c1, g, b):
    T = c1.shape[0]
    tm = 512

    def body(c_ref, g_ref, b_ref, o_ref):
        cv = c_ref[...]
        mu = jnp.mean(cv, axis=-1, keepdims=True)
        cc = cv - mu
        var = jnp.mean(cc * cc, axis=-1, keepdims=True)
        c2 = cc * lax.rsqrt(var + LN_EPS) * g_ref[...] + b_ref[...]
        o_ref[...] = (c2 * _sigmoid(c2)).astype(o_ref.dtype)

    row = pl.BlockSpec((tm, D_MODEL), lambda i: (i, 0))
    vec = pl.BlockSpec((1, D_MODEL), lambda i: (0, 0))
    return pl.pallas_call(body, grid=(T // tm,), in_specs=[row, vec, vec], out_specs=row,
                          out_shape=SDS((T, D_MODEL), bf16), name="ln_silu_fwd", compiler_params=_cp(("parallel",)))(c1, g, b)


def _conv_out_bwd(dyc, w, c1, g, b):
    T = c1.shape[0]
    tm = 1024

    def body(d_ref, w_ref, c_ref, g_ref, b_ref, dc_ref, dg_ref, db_ref):
        dc3 = lax.dot_general(d_ref[...], w_ref[...], NT, preferred_element_type=f32)
        cv = c_ref[...]
        gv = g_ref[...]
        mu = jnp.mean(cv, axis=-1, keepdims=True)
        cc = cv - mu
        var = jnp.mean(cc * cc, axis=-1, keepdims=True)
        rs = lax.rsqrt(var + LN_EPS)
        xh = cc * rs
        c2 = xh * gv + b_ref[...]
        sg = _sigmoid(c2)
        dc2 = dc3 * (sg * (1.0 + c2 * (1.0 - sg)))
        dxh = dc2 * gv
        dc_ref[...] = rs * (dxh - jnp.mean(dxh, axis=-1, keepdims=True) - xh * jnp.mean(dxh * xh, axis=-1, keepdims=True))
        _accumulate(pl.program_id(0) == 0, [(dg_ref, jnp.sum(dc2 * xh, axis=0, keepdims=True)), (db_ref, jnp.sum(dc2, axis=0, keepdims=True))])

    row = pl.BlockSpec((tm, D_MODEL), lambda i: (i, 0))
    vec = pl.BlockSpec((1, D_MODEL), lambda i: (0, 0))
    return pl.pallas_call(body, grid=(T // tm,), in_specs=[row, pl.BlockSpec((D_MODEL, D_MODEL), lambda i: (0, 0)), row, vec, vec],
                          out_specs=[row, vec, vec],
                          out_shape=(SDS((T, D_MODEL), f32), SDS((1, D_MODEL), f32), SDS((1, D_MODEL), f32)),
                          name="conv_out_bwd", compiler_params=_cp(("arbitrary",)))(dyc, w, c1, g, b)


NN = (((1,), (0,)), ((), ()))
NT = (((1,), (1,)), ((), ()))
TN = (((0,), (0,)), ((), ()))


def _mm(name, a, b, *, grid, a_spec, b_spec, o_spec, o_shape, o_dtype, dims, acc_shape=None, k_axis=None,
        res=None, res_spec=None, sem=None, comm=None):
    nk = 1 if k_axis is None else grid[k_axis]

    def body(*refs):
        if res is None:
            a_ref, b_ref, o_ref = refs[:3]
            r_ref, scr = None, refs[3:]
        else:
            a_ref, b_ref, r_ref, o_ref = refs[:4]
            scr = refs[4:]
        p = lax.dot_general(a_ref[...].astype(bf16), b_ref[...].astype(bf16), dims, preferred_element_type=f32)
        if nk == 1:
            if r_ref is not None:
                p = p + r_ref[...]
            o_ref[...] = p.astype(o_dtype)
            return
        acc = scr[0]
        k = pl.program_id(k_axis)

        @pl.when(k == 0)
        def _():
            acc[...] = p

        @pl.when(k > 0)
        def _():
            acc[...] += p

        @pl.when(k == nk - 1)
        def _():
            t = acc[...]
            if r_ref is not None:
                t = t + r_ref[...]
            o_ref[...] = t.astype(o_dtype)

    ins = [a, b] + ([] if res is None else [res])
    specs = [a_spec, b_spec] + ([] if res is None else [res_spec])
    scratch = [] if nk == 1 else [pltpu.VMEM(acc_shape, f32)]
    return _launch(body, name=name, grid=grid, in_specs=specs, out_specs=o_spec, out_shape=SDS(o_shape, o_dtype),
                   args=ins, scratch_shapes=scratch, sem=sem, comm=comm)


def _mm_nn_full(name, a, b, o_dtype, res=None, tm=1024, comm=None):
    T, K = a.shape
    N = b.shape[1]
    return _mm(name, a, b, grid=(T // tm,), a_spec=pl.BlockSpec((tm, K), lambda i: (i, 0)),
               b_spec=pl.BlockSpec((K, N), lambda i: (0, 0)), o_spec=pl.BlockSpec((tm, N), lambda i: (i, 0)),
               o_shape=(T, N), o_dtype=o_dtype, dims=NN, res=res,
               res_spec=None if res is None else pl.BlockSpec((tm, N), lambda i: (i, 0)), sem=("parallel",), comm=comm)


def _mm_nt_full(name, a, b, o_dtype, tm=1024):
    T, N = a.shape
    K = b.shape[0]
    return _mm(name, a, b, grid=(T // tm,), a_spec=pl.BlockSpec((tm, N), lambda i: (i, 0)),
               b_spec=pl.BlockSpec((K, N), lambda i: (0, 0)), o_spec=pl.BlockSpec((tm, K), lambda i: (i, 0)),
               o_shape=(T, K), o_dtype=o_dtype, dims=NT, sem=("parallel",))


def _mm_tn_tokens(name, a, b, o_dtype, tk=1024):
    T, K = a.shape
    N = b.shape[1]
    return _mm(name, a, b, grid=(T // tk,), a_spec=pl.BlockSpec((tk, K), lambda k: (k, 0)),
               b_spec=pl.BlockSpec((tk, N), lambda k: (k, 0)), o_spec=pl.BlockSpec((K, N), lambda k: (0, 0)),
               o_shape=(K, N), o_dtype=o_dtype, dims=TN, acc_shape=(K, N), k_axis=0, sem=("arbitrary",))


def _ffn_up(h2, wg, wu):
    T = h2.shape[0]
    tm = 1024

    def body(h_ref, wg_ref, wu_ref, a_ref, b_ref, f_ref):
        hv = h_ref[...]
        av = lax.dot_general(hv, wg_ref[...], NT, preferred_element_type=f32)
        bv = lax.dot_general(hv, wu_ref[...], NT, preferred_element_type=f32)
        a_ref[...] = av.astype(bf16)
        b_ref[...] = bv.astype(bf16)
        f_ref[...] = (av * _sigmoid(av) * bv).astype(bf16)

    wspec = pl.BlockSpec((None, FF_S, D_MODEL), lambda s, m: (s, 0, 0))
    ospec = pl.BlockSpec((None, tm, FF_S), lambda s, m: (s, m, 0))
    osd = SDS((N_CHIPS, T, FF_S), bf16)
    return pl.pallas_call(body, grid=(N_CHIPS, T // tm),
                          in_specs=[pl.BlockSpec((tm, D_MODEL), lambda s, m: (m, 0)), wspec, wspec],
                          out_specs=[ospec, ospec, ospec], out_shape=(osd, osd, osd), name="ffn_up",
                          compiler_params=_cp(("parallel", "parallel")))(h2, wg, wu)


def _ffn_down_bwd(dx2, wd, a, b):
    T = dx2.shape[0]
    tm = 1024

    def body(d_ref, w_ref, a_ref, b_ref, da_ref, db_ref):
        df = lax.dot_general(d_ref[...].astype(bf16), w_ref[...], NT, preferred_element_type=f32)
        av = a_ref[...].astype(f32)
        sg = _sigmoid(av)
        da_ref[...] = (df * b_ref[...].astype(f32) * (sg * (1.0 + av * (1.0 - sg)))).astype(bf16)
        db_ref[...] = (df * av * sg).astype(bf16)

    aspec = pl.BlockSpec((None, tm, FF_S), lambda m, s: (s, m, 0))
    osd = SDS((N_CHIPS, T, FF_S), bf16)
    return pl.pallas_call(body, grid=(T // tm, N_CHIPS),
                          in_specs=[pl.BlockSpec((tm, D_MODEL), lambda m, s: (m, 0)),
                                    pl.BlockSpec((None, FF_S, D_MODEL), lambda m, s: (s, 0, 0)), aspec, aspec],
                          out_specs=[aspec, aspec], out_shape=(osd, osd), name="ffn_down_bwd",
                          compiler_params=_cp(("parallel", "parallel")))(dx2, wd, a, b)


def _merge_fwd(proj, gate_b, ya, yc):
    T = proj.shape[0]
    tm = 512

    def body(ga_ref, gc_ref, ba_ref, bc_ref, ya_ref, yc_ref, o_ref):
        sa = _sigmoid(ga_ref[...].astype(f32) + ba_ref[...])
        sc = _sigmoid(gc_ref[...].astype(f32) + bc_ref[...])
        o_ref[...] = (sa * ya_ref[...].astype(f32) + sc * yc_ref[...].astype(f32)).astype(bf16)

    blk = pl.BlockSpec((tm, CB), lambda i, j: (i, j))
    return pl.pallas_call(
        body, grid=(T // tm, 2),
        in_specs=[pl.BlockSpec((tm, CB), lambda i, j: (i, GA_CB + j)), pl.BlockSpec((tm, CB), lambda i, j: (i, GC_CB + j)),
                  pl.BlockSpec((1, CB), lambda i, j: (0, j)), pl.BlockSpec((1, CB), lambda i, j: (0, 2 + j)), blk, blk],
        out_specs=blk, out_shape=SDS((T, D_MODEL), bf16), name="merge_fwd",
        compiler_params=_cp(("parallel", "parallel")))(proj, proj, gate_b, gate_b, ya, yc)


def _merge_bwd(dm, proj, gate_b, ya, yc):
    T = proj.shape[0]
    tm = 512

    def body(dm_ref, ga_ref, gc_ref, ba_ref, bc_ref, ya_ref, yc_ref, dya_ref, dyc_ref, dga_ref, dgc_ref, dba_ref, dbc_ref):
        dmv = dm_ref[...].astype(f32)
        sa = _sigmoid(ga_ref[...].astype(f32) + ba_ref[...])
        sc = _sigmoid(gc_ref[...].astype(f32) + bc_ref[...])
        dya_ref[...] = (dmv * sa).astype(bf16)
        dyc_ref[...] = (dmv * sc).astype(bf16)
        dga = dmv * ya_ref[...].astype(f32) * sa * (1.0 - sa)
        dgc = dmv * yc_ref[...].astype(f32) * sc * (1.0 - sc)
        dga_ref[...] = dga.astype(bf16)
        dgc_ref[...] = dgc.astype(bf16)
        pa = jnp.sum(dga, axis=0, keepdims=True)
        pc = jnp.sum(dgc, axis=0, keepdims=True)

        @pl.when(pl.program_id(1) == 0)
        def _():
            dba_ref[...] = pa
            dbc_ref[...] = pc

        @pl.when(pl.program_id(1) > 0)
        def _():
            dba_ref[...] += pa
            dbc_ref[...] += pc

    blk = pl.BlockSpec((tm, CB), lambda j, i: (i, j))
    vec = pl.BlockSpec((1, CB), lambda j, i: (0, j))
    big = SDS((T, D_MODEL), bf16)
    small = SDS((1, D_MODEL), f32)
    return pl.pallas_call(
        body, grid=(2, T // tm),
        in_specs=[blk, pl.BlockSpec((tm, CB), lambda j, i: (i, GA_CB + j)), pl.BlockSpec((tm, CB), lambda j, i: (i, GC_CB + j)),
                  vec, pl.BlockSpec((1, CB), lambda j, i: (0, 2 + j)), blk, blk],
        out_specs=[blk, blk, blk, blk, vec, vec], out_shape=(big, big, big, big, small, small), name="merge_bwd",
        compiler_params=_cp(("parallel", "arbitrary")))(dm, proj, proj, gate_b, gate_b, ya, yc)


CONV_TS = 256
CONV_HALO = 32
CONV_RC = 64
CONV_WIN = CONV_TS + CONV_HALO
SUBLANES = 8


def _fill_shifted(win, sh):
    for b in range(1, SUBLANES):
        sh[b - 1] = win[pl.ds(b, CONV_WIN - SUBLANES), :]


def _rows_at(win, sh, row):
    a, b = divmod(row, SUBLANES)
    if b == 0:
        return win[pl.ds(row, CONV_RC), :]
    return sh[b - 1, pl.ds(a * SUBLANES, CONV_RC), :]


def _glu_conv_fwd(proj3, w, bias, comm=None):
    B = proj3.shape[0]
    nt = SEQ // CONV_TS
    hb = CONV_TS // CONV_HALO

    def body(ua_ref, ub_ref, ha_ref, hb_ref, w_ref, b_ref, o_ref, win, sh):
        i = pl.program_id(2)
        c0 = ua_ref[...].astype(f32) * _sigmoid(ub_ref[...].astype(f32))
        halo = ha_ref[...].astype(f32) * _sigmoid(hb_ref[...].astype(f32))
        win[0:CONV_HALO, :] = jnp.where(i > 0, halo, 0.0)
        win[CONV_HALO:, :] = c0
        _fill_shifted(win, sh)
        for r0 in range(0, CONV_TS, CONV_RC):
            acc = jnp.zeros((CONV_RC, CB), f32) + b_ref[...]
            for k in range(CONV_K):
                acc = acc + _rows_at(win, sh, r0 + CONV_HALO - (CONV_K - 1) + k) * w_ref[k:k + 1, :]
            o_ref[r0:r0 + CONV_RC, :] = acc

    def cur(cb):
        return pl.BlockSpec((None, CONV_TS, CB), lambda b, j, i: (b, i, cb + j))

    def prev(cb):
        return pl.BlockSpec((None, CONV_HALO, CB), lambda b, j, i: (b, jnp.maximum(i * hb - 1, 0), cb + j))

    return _launch(
        body, name="glu_conv_fwd", grid=(B, 2, nt),
        in_specs=[cur(UA_CB), cur(UB_CB), prev(UA_CB), prev(UB_CB),
                  pl.BlockSpec((CONV_K, CB), lambda b, j, i: (0, j)), pl.BlockSpec((1, CB), lambda b, j, i: (0, j))],
        out_specs=pl.BlockSpec((None, CONV_TS, CB), lambda b, j, i: (b, i, j)),
        out_shape=SDS((B, SEQ, D_MODEL), f32), args=(proj3, proj3, proj3, proj3, w, bias),
        scratch_shapes=[pltpu.VMEM((CONV_WIN, CB), f32), pltpu.VMEM((SUBLANES - 1, CONV_WIN - SUBLANES, CB), f32)],
        sem=("parallel", "parallel", "parallel"), comm=comm)


def _glu_conv_bwd(dc1, proj3, w, comm=None):
    B = proj3.shape[0]
    nt = SEQ // CONV_TS
    hb = CONV_TS // CONV_HALO

    def body(d_ref, dn_ref, ua_ref, ub_ref, ha_ref, hb_ref, w_ref, dua_ref, dub_ref, dw_ref, db_ref, winc, wind, accw, shc, shd):
        b = pl.program_id(1)
        i = pl.program_id(2)
        first = jnp.logical_and(b == 0, i == 0)
        last = jnp.logical_and(b == B - 1, i == nt - 1)

        @pl.when(first)
        def _():
            accw[...] = jnp.zeros_like(accw)
            db_ref[...] = jnp.zeros_like(db_ref)

        halo = ha_ref[...].astype(f32) * _sigmoid(hb_ref[...].astype(f32))
        winc[0:CONV_HALO, :] = jnp.where(i > 0, halo, 0.0)
        winc[CONV_HALO:, :] = ua_ref[...].astype(f32) * _sigmoid(ub_ref[...].astype(f32))
        wind[0:CONV_TS, :] = d_ref[...]
        wind[CONV_TS:, :] = jnp.where(i < nt - 1, dn_ref[...], 0.0)
        db_ref[...] += jnp.sum(d_ref[...], axis=0, keepdims=True)
        _fill_shifted(winc, shc)
        _fill_shifted(wind, shd)
        for r0 in range(0, CONV_TS, CONV_RC):
            dc0 = jnp.zeros((CONV_RC, CB), f32)
            for k in range(CONV_K):
                dc0 = dc0 + _rows_at(wind, shd, r0 + (CONV_K - 1) - k) * w_ref[k:k + 1, :]
            uav = ua_ref[r0:r0 + CONV_RC, :].astype(f32)
            sg = _sigmoid(ub_ref[r0:r0 + CONV_RC, :].astype(f32))
            dua_ref[r0:r0 + CONV_RC, :] = (dc0 * sg).astype(bf16)
            dub_ref[r0:r0 + CONV_RC, :] = (dc0 * uav * sg * (1.0 - sg)).astype(bf16)
            dv = wind[r0:r0 + CONV_RC, :]
            for k in range(CONV_K):
                prod = dv * _rows_at(winc, shc, r0 + CONV_HALO - (CONV_K - 1) + k)
                accw[k] += jnp.sum(prod.reshape(CONV_RC // 8, 8, CB), axis=0)

        @pl.when(last)
        def _():
            for k in range(CONV_K):
                dw_ref[k:k + 1, :] = jnp.sum(accw[k], axis=0, keepdims=True)
            dw_ref[CONV_K:, :] = jnp.zeros((CONV_HALO - CONV_K, CB), f32)

    def cur(cb):
        return pl.BlockSpec((None, CONV_TS, CB), lambda j, b, i: (b, i, cb + j))

    def prev(cb):
        return pl.BlockSpec((None, CONV_HALO, CB), lambda j, b, i: (b, jnp.maximum(i * hb - 1, 0), cb + j))

    nxt = pl.BlockSpec((None, CONV_HALO, CB), lambda j, b, i: (b, jnp.minimum((i + 1) * hb, SEQ // CONV_HALO - 1), j))
    big = SDS((B, SEQ, D_MODEL), bf16)
    return _launch(
        body, name="glu_conv_bwd", grid=(2, B, nt),
        in_specs=[cur(0), nxt, cur(UA_CB), cur(UB_CB), prev(UA_CB), prev(UB_CB), pl.BlockSpec((CONV_K, CB), lambda j, b, i: (0, j))],
        out_specs=[cur(0), cur(0), pl.BlockSpec((CONV_HALO, CB), lambda j, b, i: (0, j)), pl.BlockSpec((1, CB), lambda j, b, i: (0, j))],
        out_shape=(big, big, SDS((CONV_HALO, D_MODEL), f32), SDS((1, D_MODEL), f32)),
        args=(dc1, dc1, proj3, proj3, proj3, proj3, w),
        scratch_shapes=[pltpu.VMEM((CONV_WIN, CB), f32), pltpu.VMEM((CONV_WIN, CB), f32), pltpu.VMEM((CONV_K, SUBLANES, CB), f32),
                        pltpu.VMEM((SUBLANES - 1, CONV_WIN - SUBLANES, CB), f32),
                        pltpu.VMEM((SUBLANES - 1, CONV_WIN - SUBLANES, CB), f32)],
        sem=("parallel", "arbitrary", "arbitrary"), comm=comm)


def _band(first, dil):
    kw = Q_BLOCK if first else 2 * Q_BLOCK
    qi = lax.broadcasted_iota(jnp.int32, (Q_BLOCK, kw), 0)
    kj = lax.broadcasted_iota(jnp.int32, (Q_BLOCK, kw), 1)
    rel = qi - kj + (0 if first else Q_BLOCK)
    valid = jnp.logical_and(rel >= 0, rel <= Q_BLOCK)
    return valid, rel.astype(f32) * float(dil)


def _bias(first, dil, slope):
    valid, dist = _band(first, dil)
    return jnp.where(valid, -slope * dist, NEG)


def _scores(q, k, bias):
    return lax.dot_general(q, k, NT, preferred_element_type=f32) * (HEAD_DIM ** -0.5) + bias


def _pair_cols(hp):
    return slice(hp * LANES, (hp + 1) * LANES)


def _half(x2, e):
    lane = lax.broadcasted_iota(jnp.int32, (1, LANES), 1)
    keep = (lane < HEAD_DIM) if e == 0 else (lane >= HEAD_DIM)
    return jnp.where(keep, x2, jnp.zeros_like(x2))


def _attn_fwd(g, q4, k4, v4, qcb, kcb, vcb, comm=None):
    _, dil = GROUPS[g]
    B, r, L, _ = q4.shape
    nb = L // Q_BLOCK
    slopes = [float(s) for s in _SLOPES[g]]

    def body(q_ref, k_ref, v_ref, o_ref, lse_ref, bias, s_scr, p_scr):
        lane = lax.broadcasted_iota(jnp.int32, (Q_BLOCK, LANES), 1)
        if nb > 1:
            for h in range(HEADS):
                bias[h] = _bias(False, dil, slopes[h])

        def block(n, first):
            q0 = 0 if first else pl.multiple_of(n * Q_BLOCK, Q_BLOCK)
            k0 = 0 if first else pl.multiple_of((n - 1) * Q_BLOCK, Q_BLOCK)
            kw = Q_BLOCK if first else 2 * Q_BLOCK
            for hp in range(HEADS // 2):
                q2 = q_ref[pl.ds(q0, Q_BLOCK), _pair_cols(hp)]
                k2 = k_ref[pl.ds(k0, kw), _pair_cols(hp)]
                for e in range(2):
                    h = 2 * hp + e
                    b_h = _bias(True, dil, slopes[h]) if first else bias[h]
                    s_scr[h, :, :kw] = _scores(_half(q2, e), k2, b_h)
            st = jnp.zeros((Q_BLOCK, LANES), f32)
            dens = jnp.ones((Q_BLOCK, LANES), f32)
            for h in range(HEADS):
                s = s_scr[h, :, :kw]
                m = jnp.max(s, axis=-1, keepdims=True)
                p = jnp.exp(s - m)
                den = jnp.sum(p, axis=-1, keepdims=True)
                p_scr[h, :, :kw] = p.astype(bf16)
                st = jnp.where(lane == h, m + jnp.log(den), st)
                dens = jnp.where(lane == h, den, dens)
            lse_ref[pl.ds(q0, Q_BLOCK), :] = st
            inv = 1.0 / dens
            for hp in range(HEADS // 2):
                v2 = v_ref[pl.ds(k0, kw), _pair_cols(hp)]
                o2 = sum(jnp.dot(p_scr[2 * hp + e, :, :kw], _half(v2, e), preferred_element_type=f32) * inv[:, 2 * hp + e:2 * hp + e + 1]
                         for e in range(2))
                o_ref[pl.ds(q0, Q_BLOCK), _pair_cols(hp)] = o2.astype(bf16)

        block(0, True)
        if nb > 1:
            def step(n, carry):
                block(n, False)
                return carry
            lax.fori_loop(1, nb, step, 0)

    def spec(cb):
        return pl.BlockSpec((None, None, L, GW), lambda b, c: (b, c, 0, cb))

    return _launch(
        body, name=f"attn_fwd_g{g}", grid=(B, r), in_specs=[spec(qcb), spec(kcb), spec(vcb)],
        out_specs=[spec(0), pl.BlockSpec((None, None, L, LANES), lambda b, c: (b, c, 0, 0))],
        out_shape=(SDS((B, r, L, GW), bf16), SDS((B, r, L, LANES), f32)), args=(q4, k4, v4),
        scratch_shapes=[pltpu.VMEM((HEADS, Q_BLOCK, 2 * Q_BLOCK), f32), pltpu.VMEM((HEADS, Q_BLOCK, 2 * Q_BLOCK), f32),
                        pltpu.VMEM((HEADS, Q_BLOCK, 2 * Q_BLOCK), bf16)],
        sem=("parallel", "parallel"), comm=comm)


def _attn_bwd(g, q4, k4, v4, qcb, kcb, vcb, do4, lse4, dl4, comm=None):
    _, dil = GROUPS[g]
    B, r, L, _ = q4.shape
    nb = L // Q_BLOCK
    slopes = [float(s) for s in _SLOPES[g]]
    scale = HEAD_DIM ** -0.5

    def body(q_ref, k_ref, v_ref, do_ref, lse_ref, dl_ref, dq_ref, dk_ref, dv_ref, dk_acc, dv_acc, bias, s_scr, dp_scr, p_scr, ds_scr):
        dk_acc[...] = jnp.zeros_like(dk_acc)
        dv_acc[...] = jnp.zeros_like(dv_acc)
        if nb > 1:
            for h in range(HEADS):
                bias[h] = _bias(False, dil, slopes[h])

        def block(n, first):
            q0 = 0 if first else pl.multiple_of(n * Q_BLOCK, Q_BLOCK)
            k0 = 0 if first else pl.multiple_of((n - 1) * Q_BLOCK, Q_BLOCK)
            kw = Q_BLOCK if first else 2 * Q_BLOCK
            for hp in range(HEADS // 2):
                q2 = q_ref[pl.ds(q0, Q_BLOCK), _pair_cols(hp)]
                k2 = k_ref[pl.ds(k0, kw), _pair_cols(hp)]
                v2 = v_ref[pl.ds(k0, kw), _pair_cols(hp)]
                do2 = do_ref[pl.ds(q0, Q_BLOCK), _pair_cols(hp)]
                for e in range(2):
                    h = 2 * hp + e
                    b_h = _bias(True, dil, slopes[h]) if first else bias[h]
                    s_scr[h, :, :kw] = _scores(_half(q2, e), k2, b_h)
                    dp_scr[h, :, :kw] = lax.dot_general(_half(do2, e), v2, NT, preferred_element_type=f32)
            for h in range(HEADS):
                p = jnp.exp(s_scr[h, :, :kw] - lse_ref[pl.ds(q0, Q_BLOCK), h:h + 1])
                p_scr[h, :, :kw] = p.astype(bf16)
                ds_scr[h, :, :kw] = (p * (dp_scr[h, :, :kw] - dl_ref[pl.ds(q0, Q_BLOCK), h:h + 1])).astype(bf16)
            for hp in range(HEADS // 2):
                cols = _pair_cols(hp)
                q2 = q_ref[pl.ds(q0, Q_BLOCK), cols]
                k2 = k_ref[pl.ds(k0, kw), cols]
                do2 = do_ref[pl.ds(q0, Q_BLOCK), cols]
                ds = [ds_scr[2 * hp + e, :, :kw] for e in range(2)]
                dq2 = sum(jnp.dot(ds[e], _half(k2, e), preferred_element_type=f32) for e in range(2))
                dq_ref[pl.ds(q0, Q_BLOCK), cols] = (dq2 * scale).astype(bf16)
                dk2 = sum(lax.dot_general(ds[e], _half(q2, e), TN, preferred_element_type=f32) for e in range(2))
                dk_acc[pl.ds(k0, kw), cols] += dk2 * scale
                dv2 = sum(lax.dot_general(p_scr[2 * hp + e, :, :kw], _half(do2, e), TN, preferred_element_type=f32) for e in range(2))
                dv_acc[pl.ds(k0, kw), cols] += dv2

        block(0, True)
        if nb > 1:
            def step(n, carry):
                block(n, False)
                return carry
            lax.fori_loop(1, nb, step, 0)
        dk_ref[...] = dk_acc[...].astype(bf16)
        dv_ref[...] = dv_acc[...].astype(bf16)

    def spec(cb):
        return pl.BlockSpec((None, None, L, GW), lambda b, c: (b, c, 0, cb))

    st = pl.BlockSpec((None, None, L, LANES), lambda b, c: (b, c, 0, 0))
    osd = SDS((B, r, L, GW), bf16)
    return _launch(
        body, name=f"attn_bwd_g{g}", grid=(B, r), in_specs=[spec(qcb), spec(kcb), spec(vcb), spec(0), st, st],
        out_specs=[spec(0), spec(0), spec(0)], out_shape=(osd, osd, osd), args=(q4, k4, v4, do4, lse4, dl4),
        scratch_shapes=[pltpu.VMEM((L, GW), f32), pltpu.VMEM((L, GW), f32)]
        + [pltpu.VMEM((HEADS, Q_BLOCK, 2 * Q_BLOCK), f32)] * 3 + [pltpu.VMEM((HEADS, Q_BLOCK, 2 * Q_BLOCK), bf16)] * 2,
        sem=("parallel", "parallel"), comm=comm)


RT = 512
RCH = GW // LANES
DILS = tuple(d for _, d in GROUPS[1:])


def _res_spec(r, width):
    return pl.BlockSpec((None, r, RT // r, width), lambda b, i, *_: (b, 0, i, 0))


def _tok_spec(width, cb=0):
    return pl.BlockSpec((None, RT, width), lambda b, i, *_: (b, i, cb))


def _to_residues(res_ref, scr, r, width):
    for c in range(r):
        for jj in range(width // LANES):
            res_ref[c, :, jj * LANES:(jj + 1) * LANES] = scr[jj, pl.ds(c, RT // r, stride=r), :].astype(res_ref.dtype)


def _from_residues(scr, res_ref, r, width):
    for c in range(r):
        for jj in range(width // LANES):
            scr[jj, pl.ds(c, RT // r, stride=r), :] = res_ref[c, :, jj * LANES:(jj + 1) * LANES].astype(f32)


def _qkv_to_residues(proj3, g):
    r = GROUPS[g][1]
    B = proj3.shape[0]

    def body(q_ref, k_ref, v_ref, o_ref, scr):
        for p, x_ref in enumerate((q_ref, k_ref, v_ref)):
            x = x_ref[...].astype(f32)
            for jj in range(RCH):
                scr[p * RCH + jj] = x[:, jj * LANES:(jj + 1) * LANES]
        _to_residues(o_ref, scr, r, ATTN_W)

    return pl.pallas_call(
        body, grid=(B, SEQ // RT), in_specs=[_tok_spec(GW, 3 * p + g) for p in range(3)], out_specs=_res_spec(r, ATTN_W),
        out_shape=SDS((B, r, SEQ // r, ATTN_W), bf16), scratch_shapes=[pltpu.VMEM((3 * RCH, RT, LANES), f32)],
        name=f"qkv_to_residues_g{g}", compiler_params=_cp(("parallel", "parallel")))(proj3, proj3, proj3)


def _attn_mix(o0, l0, o_res, l_res):
    B = o0.shape[0]

    def body(o0_ref, l0_ref, o1_ref, o2_ref, l1_ref, l2_ref, y_ref, lt_ref, lt1_ref, lt2_ref, so, sl):
        for gi, (o_ref, l_ref, r) in enumerate(((o1_ref, l1_ref, DILS[0]), (o2_ref, l2_ref, DILS[1]))):
            _from_residues(so.at[gi], o_ref, r, GW)
            _from_residues(sl.at[gi:gi + 1], l_ref, r, LANES)
        ls = [l0_ref[...], sl[0], sl[1]]
        m = functools.reduce(jnp.maximum, ls)
        ws = [jnp.exp(l - m) for l in ls]
        den = ws[0] + ws[1] + ws[2]
        alphas = [w / den for w in ws]
        lt = m + jnp.log(den)
        lt_ref[...] = lt
        sl[2] = lt
        _to_residues(lt1_ref, sl.at[2:3], DILS[0], LANES)
        _to_residues(lt2_ref, sl.at[2:3], DILS[1], LANES)
        for h in range(HEADS):
            cols = slice(h * HEAD_DIM, (h + 1) * HEAD_DIM)
            jj, lo = divmod(h * HEAD_DIM, LANES)
            acc = alphas[0][:, h:h + 1] * o0_ref[:, cols].astype(f32)
            for gi in range(2):
                acc = acc + alphas[gi + 1][:, h:h + 1] * so[gi, jj, :, lo:lo + HEAD_DIM]
            y_ref[:, cols] = acc.astype(bf16)

    in_specs = [_tok_spec(GW), _tok_spec(LANES), _res_spec(DILS[0], GW), _res_spec(DILS[1], GW), _res_spec(DILS[0], LANES), _res_spec(DILS[1], LANES)]
    out_specs = [_tok_spec(GW), _tok_spec(LANES), _res_spec(DILS[0], LANES), _res_spec(DILS[1], LANES)]
    return pl.pallas_call(
        body, grid=(B, SEQ // RT), in_specs=in_specs, out_specs=out_specs,
        out_shape=(SDS((B, SEQ, GW), bf16), SDS((B, SEQ, LANES), f32)) + tuple(SDS((B, r, SEQ // r, LANES), f32) for r in DILS),
        scratch_shapes=[pltpu.VMEM((2, RCH, RT, LANES), f32), pltpu.VMEM((3, RT, LANES), f32)],
        name="attn_mix", compiler_params=_cp(("parallel", "parallel")))(o0, l0, *o_res, *l_res)


def _attn_delta(dmix, mix):
    B = dmix.shape[0]

    def body(d_ref, y_ref, dl_ref, dl1_ref, dl2_ref, dm1_ref, dm2_ref, sx, sd):
        lane = lax.broadcasted_iota(jnp.int32, (RT, LANES), 1)
        acc = jnp.zeros((RT, LANES), f32)
        dv = d_ref[...].astype(f32)
        for jj in range(RCH):
            sx[jj] = dv[:, jj * LANES:(jj + 1) * LANES]
        for h in range(HEADS):
            cols = slice(h * HEAD_DIM, (h + 1) * HEAD_DIM)
            dl = jnp.sum(dv[:, cols] * y_ref[:, cols].astype(f32), axis=-1, keepdims=True)
            acc = jnp.where(lane == h, dl, acc)
        dl_ref[...] = acc
        sd[0] = acc
        _to_residues(dl1_ref, sd, DILS[0], LANES)
        _to_residues(dl2_ref, sd, DILS[1], LANES)
        _to_residues(dm1_ref, sx, DILS[0], GW)
        _to_residues(dm2_ref, sx, DILS[1], GW)

    return pl.pallas_call(
        body, grid=(B, SEQ // RT), in_specs=[_tok_spec(GW), _tok_spec(GW)],
        out_specs=[_tok_spec(LANES), _res_spec(DILS[0], LANES), _res_spec(DILS[1], LANES), _res_spec(DILS[0], GW), _res_spec(DILS[1], GW)],
        out_shape=(SDS((B, SEQ, LANES), f32),) + tuple(SDS((B, r, SEQ // r, LANES), f32) for r in DILS)
        + tuple(SDS((B, r, SEQ // r, GW), bf16) for r in DILS),
        scratch_shapes=[pltpu.VMEM((RCH, RT, LANES), f32), pltpu.VMEM((1, RT, LANES), f32)],
        name="attn_delta", compiler_params=_cp(("parallel", "parallel")))(dmix, mix)


N_CB = IN_W // CB


def _assemble_dproj(dqkv, dua, dub, dga, dgc):
    B = dua.shape[0]
    ng = len(GROUPS)
    flat = [dqkv[g][p] for p in range(3) for g in range(ng)]
    wide = [dua, dub, dga, dgc]

    def body(*refs):
        srcs, wides, o_ref, scr = refs[:3 * ng], refs[3 * ng:3 * ng + 4], refs[3 * ng + 4], refs[3 * ng + 5]
        for jv in range(3 * ng):
            g = jv % ng
            if g == 0:
                o_ref[:, jv * GW:(jv + 1) * GW] = srcs[jv][...]
            else:
                _from_residues(scr, srcs[jv], GROUPS[g][1], GW)
                for jj in range(RCH):
                    o_ref[:, jv * GW + jj * LANES:jv * GW + (jj + 1) * LANES] = scr[jj].astype(bf16)
        for wv in range(4):
            lo = 3 * ATTN_W + wv * D_MODEL
            o_ref[:, lo:lo + D_MODEL] = wides[wv][...]

    in_specs = [_tok_spec(GW) if (jv % ng) == 0 else _res_spec(GROUPS[jv % ng][1], GW) for jv in range(3 * ng)]
    in_specs += [_tok_spec(D_MODEL)] * 4
    return pl.pallas_call(
        body, grid=(B, SEQ // RT), in_specs=in_specs, out_specs=_tok_spec(IN_W),
        out_shape=SDS((B, SEQ, IN_W), bf16), scratch_shapes=[pltpu.VMEM((RCH, RT, LANES), f32)],
        name="assemble_dproj", compiler_params=_cp(("parallel", "parallel")))(*flat, *wide)


def _same_shape_groups(names, arrays):
    groups = {}
    for n, a in zip(names, arrays):
        groups.setdefault(a.shape, ([], []))
        groups[a.shape][0].append(n)
        groups[a.shape][1].append(a)
    return list(groups.values())


def _add_pairs(parts, gots, core, name):
    k = len(parts)
    n, h, C = gots[0].shape

    def body(c_ref, *refs):
        for i in range(k):
            refs[2 * k + i][...] = (refs[i][...].astype(f32) + refs[k + i][...].astype(f32)).astype(bf16)

    blk = pl.BlockSpec((None, h, C), lambda s, c_ref: (s, 0, 0))
    own = pl.BlockSpec((None, h, C), lambda s, c_ref: (s, c_ref[0], 0))
    spec = pltpu.PrefetchScalarGridSpec(num_scalar_prefetch=1, grid=(n,), in_specs=[own] * k + [blk] * k, out_specs=[blk] * k)
    return pl.pallas_call(body, grid_spec=spec, out_shape=tuple(SDS((n, h, C), bf16) for _ in range(k)), name=name,
                          compiler_params=_cp(("parallel",)))(core, *parts, *gots)


def _sum4(sums, gots, chip, core, name):
    k = len(sums)
    _, h, C = sums[0].shape

    def body(s_ref, c_ref, *refs):
        for i in range(k):
            q_ref = refs[k + i]
            t = refs[i][...].astype(f32) + q_ref[0].astype(f32)
            t = t + q_ref[1].astype(f32)
            refs[2 * k + i][...] = t + q_ref[2].astype(f32)

    spec = pltpu.PrefetchScalarGridSpec(
        num_scalar_prefetch=2, grid=(1,),
        in_specs=[pl.BlockSpec((None, h, C), lambda i, s_ref, c_ref: (s_ref[0], 0, 0))] * k
        + [pl.BlockSpec((N_CHIPS - 1, h, C), lambda i, s_ref, c_ref: (0, 0, 0))] * k,
        out_specs=[pl.BlockSpec((h, C), lambda i, s_ref, c_ref: (c_ref[0], 0))] * k)
    return pl.pallas_call(body, grid_spec=spec, out_shape=tuple(SDS((2 * h, C), f32) for _ in range(k)), name=name,
                          compiler_params=_cp(("arbitrary",)))(chip, core, *sums, *gots)


def _adamw(ws, gs, ms, vs, name):
    k = len(ws)
    R, C = ws[0].shape
    rt = R
    for cand in (512, 256, 128, 64, 32, 16, 8):
        if R % cand == 0 and cand * C * 4 <= 2 ** 21:
            rt = cand
            break
    c1 = 1.0 / (1.0 - B1 ** STEP)
    c2 = 1.0 / (1.0 - B2 ** STEP)

    def body(*refs):
        for i in range(k):
            w_ref, g_ref, m_ref, v_ref = (refs[j * k + i] for j in range(4))
            d_ref, nm_ref, nv_ref = (refs[(4 + j) * k + i] for j in range(3))
            gv = g_ref[...]
            nm = B1 * m_ref[...] + (1.0 - B1) * gv
            nv = B2 * v_ref[...] + (1.0 - B2) * (gv * gv)
            nm_ref[...] = nm
            nv_ref[...] = nv
            d_ref[...] = -LR * ((nm * c1) / (jnp.sqrt(nv * c2) + ADAM_EPS) + WD * w_ref[...])

    blk = pl.BlockSpec((rt, C), lambda i: (i, 0))
    sd = SDS((R, C), f32)
    res = pl.pallas_call(body, grid=(R // rt,), in_specs=[blk] * (4 * k), out_specs=[blk] * (3 * k), out_shape=(sd,) * (3 * k),
                         name=name, compiler_params=_cp(("parallel",)))(*ws, *gs, *ms, *vs)
    return res[:k], res[k:2 * k], res[2 * k:]


def _coords():
    return lax.axis_index("x"), lax.axis_index("y"), lax.axis_index("c")


def _other_chips(x, y):
    return [(1 - x, y), (x, 1 - y), (1 - x, 1 - y)]


def _allgather_weights(shards, slots):
    n = len(shards)
    n_slot = 12

    def body(*refs):
        ins, outs = refs[:n], refs[2 * n:3 * n]
        send, recv = refs[3 * n:]
        x, y, c = _coords()
        me, cx, cy, cd = 2 * x + y, 2 * (1 - x) + y, 2 * x + (1 - y), 2 * (1 - x) + (1 - y)
        dev_x, dev_y, dev_s = (1 - x, y, c), (x, 1 - y, c), (x, y, 1 - c)
        started = []

        def copy(w, slot, src, dst, dev):
            return pltpu.make_async_remote_copy(src_ref=src, dst_ref=dst, send_sem=send.at[w, slot], recv_sem=recv.at[w, slot],
                                                device_id=dev, device_id_type=MESH)

        def go(cp):
            cp.start()
            started.append(cp)

        for w in range(n):
            q = shards[w].shape[0] // 4
            rows = [pl.ds(c * 2 * q + k * q, q) for k in range(2)]
            theirs = [pl.ds((1 - c) * 2 * q + k * q, q) for k in range(2)]
            own = [(ins[w].at[r, :], outs[w].at[me, r, :]) for r in rows]
            go(copy(w, 0, *own[0], dev_x))
            go(copy(w, 2, *own[1], dev_y))
            go(copy(w, 1, *own[1], dev_x))
            go(copy(w, 3, *own[0], dev_y))
            arrivals = [(0, cx, 0, (4, dev_y)), (2, cy, 1, (5, dev_x)), (1, cx, 1, None), (3, cy, 0, None), (4, cd, 0, None), (5, cd, 1, None)]
            for k, (slot, chip, quarter, onward) in enumerate(arrivals):
                blk = outs[w].at[chip, rows[quarter], :]
                copy(w, slot, blk, blk, dev_s).wait_recv()
                if onward is not None:
                    go(copy(w, onward[0], blk, blk, onward[1]))
                go(copy(w, 6 + k, blk, blk, dev_s))
            for k, (slot, chip, quarter, onward) in enumerate(arrivals):
                blk = outs[w].at[chip, theirs[quarter], :]
                copy(w, 6 + k, blk, blk, dev_s).wait_recv()
        for cp in started:
            cp.wait_send()

    return pl.pallas_call(
        body, in_specs=[HBM] * (2 * n), out_specs=[HBM] * n,
        out_shape=tuple(SDS((N_CHIPS,) + s.shape, s.dtype) for s in shards),
        input_output_aliases={n + w: w for w in range(n)},
        scratch_shapes=[pltpu.SemaphoreType.DMA((n, n_slot))] * 2,
        name="allgather_weights", compiler_params=pltpu.CompilerParams(has_side_effects=True))(*shards, *slots)


def _x_pair_exchange(parts):
    n = len(parts)
    halves = [p.shape[1] // 2 for p in parts]

    def copies(ins, outs, sems):
        send, recv = sems
        x, y, c = _coords()
        return [pltpu.make_async_remote_copy(src_ref=ins[w].at[:, pl.ds((1 - c) * halves[w], halves[w]), :], dst_ref=outs[w],
                                             send_sem=send.at[w], recv_sem=recv.at[w], device_id=(x, y, 1 - c), device_id_type=MESH)
                for w in range(n)]

    def start(ins, outs, sems):
        for cp in copies(ins, outs, sems):
            cp.start()

    def wait(ins, outs, sems):
        for cp in copies(ins, outs, sems):
            cp.wait()

    return _Exchange(ins=list(parts), out_shape=[SDS((N_CHIPS, p.shape[1] // 2, p.shape[2]), p.dtype) for p in parts],
                     scratch=[pltpu.SemaphoreType.DMA((n,))] * 2, aliases={}, start=start, wait=wait)


def _x_chip_exchange(sums):
    n = len(sums)

    def copies(ins, outs, sems):
        send, recv = sems
        x, y, c = _coords()
        return [pltpu.make_async_remote_copy(src_ref=ins[w].at[2 * px + py], dst_ref=outs[w].at[j], send_sem=send.at[w, j],
                                             recv_sem=recv.at[w, j], device_id=(px, py, c), device_id_type=MESH)
                for w in range(n) for j, (px, py) in enumerate(_other_chips(x, y))]

    def start(ins, outs, sems):
        for cp in copies(ins, outs, sems):
            cp.start()

    def wait(ins, outs, sems):
        for cp in copies(ins, outs, sems):
            cp.wait()

    return _Exchange(ins=list(sums), out_shape=[SDS((N_CHIPS - 1,) + s.shape[1:], s.dtype) for s in sums],
                     scratch=[pltpu.SemaphoreType.DMA((n, 3)), pltpu.SemaphoreType.DMA((n, 3))], aliases={}, start=start, wait=wait)


def _x_gather_ici(shards, slots):
    n = len(shards)
    halves = [s.shape[0] // 2 for s in shards]

    def copies(ins, outs, sems):
        send, recv = sems
        x, y, c = _coords()
        me = 2 * x + y
        out = []
        for w in range(n):
            mine = pl.ds(c * halves[w], halves[w])
            for j, (px, py) in enumerate(_other_chips(x, y)):
                snd = pltpu.make_async_remote_copy(src_ref=ins[w].at[mine, :], dst_ref=outs[w].at[me, mine, :], send_sem=send.at[w, j],
                                                   recv_sem=recv.at[w, j], device_id=(px, py, c), device_id_type=MESH)
                got = outs[w].at[2 * px + py, mine, :]
                rcv = pltpu.make_async_remote_copy(src_ref=got, dst_ref=got, send_sem=send.at[w, j], recv_sem=recv.at[w, j],
                                                   device_id=(px, py, c), device_id_type=MESH)
                out.append((snd, rcv))
        return out

    def start(ins, outs, sems):
        for snd, _ in copies(ins, outs, sems):
            snd.start()

    def wait(ins, outs, sems):
        for snd, rcv in copies(ins, outs, sems):
            rcv.wait_recv()
            snd.wait_send()

    return _Exchange(ins=list(shards) + list(slots), out_shape=[SDS(s.shape, s.dtype) for s in slots],
                     scratch=[pltpu.SemaphoreType.DMA((n, 3)), pltpu.SemaphoreType.DMA((n, 3))],
                     aliases={n + w: w for w in range(n)}, start=start, wait=wait)


def _x_gather_fwd(bufs):
    n = len(bufs)
    halves = [b.shape[1] // 2 for b in bufs]

    def copies(ins, outs, sems):
        send, recv = sems
        x, y, c = _coords()
        out = []
        for w in range(n):
            for j, (px, py) in enumerate(_other_chips(x, y)):
                mine = outs[w].at[2 * px + py, pl.ds(c * halves[w], halves[w]), :]
                theirs = outs[w].at[2 * px + py, pl.ds((1 - c) * halves[w], halves[w]), :]
                snd = pltpu.make_async_remote_copy(src_ref=mine, dst_ref=mine, send_sem=send.at[w, j], recv_sem=recv.at[w, j],
                                                   device_id=(x, y, 1 - c), device_id_type=MESH)
                rcv = pltpu.make_async_remote_copy(src_ref=theirs, dst_ref=theirs, send_sem=send.at[w, j], recv_sem=recv.at[w, j],
                                                   device_id=(x, y, 1 - c), device_id_type=MESH)
                out.append((snd, rcv))
        return out

    def start(ins, outs, sems):
        for snd, _ in copies(ins, outs, sems):
            snd.start()

    def wait(ins, outs, sems):
        for snd, rcv in copies(ins, outs, sems):
            rcv.wait_recv()
            snd.wait_send()

    return _Exchange(ins=list(bufs), out_shape=[SDS(b.shape, b.dtype) for b in bufs],
                     scratch=[pltpu.SemaphoreType.DMA((n, 3)), pltpu.SemaphoreType.DMA((n, 3))],
                     aliases={w: w for w in range(n)}, start=start, wait=wait)


def _pair_gather(bufs):
    n = len(bufs)

    def body(*refs):
        outs = refs[n:2 * n]
        send, recv = refs[2 * n:]
        x, y, c = _coords()
        cps = []
        for w in range(n):
            h = bufs[w].shape[0] // 2
            rows = outs[w].at[pl.ds(c * h, h), :]
            rc = pltpu.make_async_remote_copy(src_ref=rows, dst_ref=rows, send_sem=send.at[w], recv_sem=recv.at[w],
                                              device_id=(x, y, 1 - c), device_id_type=MESH)
            rc.start()
            cps.append(rc)
        for w, rc in enumerate(cps):
            h = bufs[w].shape[0] // 2
            other = outs[w].at[pl.ds((1 - c) * h, h), :]
            pltpu.make_async_remote_copy(src_ref=other, dst_ref=other, send_sem=send.at[w], recv_sem=recv.at[w],
                                         device_id=(x, y, 1 - c), device_id_type=MESH).wait_recv()
            rc.wait_send()

    return pl.pallas_call(body, in_specs=[HBM] * n, out_specs=[HBM] * n, out_shape=tuple(SDS(a.shape, a.dtype) for a in bufs),
                          input_output_aliases={w: w for w in range(n)},
                          scratch_shapes=[pltpu.SemaphoreType.DMA((n,))] * 2, name="grad_pair_gather",
                          compiler_params=pltpu.CompilerParams(has_side_effects=True))(*bufs)


def _small_allreduce(vec, name):
    R = vec.shape[0]
    nd = 8

    def body(v_ref, o_ref, buf, send, recv):
        x, y, c = _coords()
        me = 4 * x + 2 * y + c
        buf[me] = v_ref[...]
        cps = []
        for k in range(1, nd):
            kx, ky, kc = (k >> 2) & 1, (k >> 1) & 1, k & 1
            tx = x + kx - 2 * x * kx
            ty = y + ky - 2 * y * ky
            tc = c + kc - 2 * c * kc
            cp = pltpu.make_async_remote_copy(src_ref=v_ref, dst_ref=buf.at[me], send_sem=send.at[k], recv_sem=recv.at[k],
                                              device_id=(tx, ty, tc), device_id_type=MESH)
            cp.start()
            cps.append((cp, 4 * tx + 2 * ty + tc))
        for k, (cp, src) in zip(range(1, nd), cps):
            pltpu.make_async_remote_copy(src_ref=v_ref, dst_ref=buf.at[src], send_sem=send.at[k], recv_sem=recv.at[k],
                                         device_id=(x, y, c), device_id_type=MESH).wait_recv()
        for cp, _ in cps:
            cp.wait_send()
        acc = buf[0]
        for d in range(1, nd):
            acc = acc + buf[d]
        o_ref[...] = acc

    vm = pl.BlockSpec(memory_space=pltpu.VMEM)
    return pl.pallas_call(body, in_specs=[vm], out_specs=vm, out_shape=SDS((R, LANES), f32),
                          scratch_shapes=[pltpu.VMEM((nd, R, LANES), f32), pltpu.SemaphoreType.DMA((nd,)), pltpu.SemaphoreType.DMA((nd,))],
                          name=name, compiler_params=pltpu.CompilerParams(has_side_effects=True))(vec)


def _rows128(a, rows):
    flat = a.reshape(-1)
    return jnp.pad(flat, (0, rows * LANES - flat.shape[0])).reshape(rows, LANES)


GATHER_1 = ("w_conv_out", "w_attn_out", "w_o", "w_ffn_gate")
GATHER_2 = ("w_ffn_up", "w_ffn_down")
REDUCE_A = ("w_ffn_gate", "w_ffn_up", "w_ffn_down")
REDUCE_B = ("w_conv_out", "w_attn_out", "w_o")
REDUCE_C = ("w_in",)


def _step(x, target, norm1_g, gate_b, conv_w, conv_b, conv_ln_g, conv_ln_b, norm2_g, norm_f_g, w_in, shards, slots, chip1, core):
    B = x.shape[0]
    T = B * SEQ
    xf = x.reshape(T, D_MODEL)
    tf = target.reshape(T, D_MODEL)

    h, h_t = _rms_fwd(xf, norm1_g, "rms1_fwd")
    proj, got1 = _mm("in_proj", h, w_in, grid=(N_CHIPS, T // 1024),
                     a_spec=pl.BlockSpec((1024, D_MODEL), lambda s, m: (m, 0)),
                     b_spec=pl.BlockSpec((None, D_MODEL, IN_S), lambda s, m: (s, 0, 0)),
                     o_spec=pl.BlockSpec((1024, IN_S), lambda s, m: (m, s)), o_shape=(T, IN_W), o_dtype=bf16, dims=NN,
                     comm=_x_gather_ici([shards[n] for n in GATHER_1], [slots[n] for n in GATHER_1]))
    proj3 = proj.reshape(B, SEQ, IN_W)
    proj4 = proj.reshape(B, 1, SEQ, IN_W)

    qkv = [None] + [_qkv_to_residues(proj3, g) for g in range(1, len(GROUPS))]

    def qkv_args(g):
        return (proj4, proj4, proj4, 0, 3, 6) if g == 0 else (qkv[g], qkv[g], qkv[g], 0, 1, 2)

    (o4_0, l4_0), full1 = _attn_fwd(0, *qkv_args(0), comm=_x_gather_fwd(list(got1)))
    o4_1, l4_1 = _attn_fwd(1, *qkv_args(1))
    o4_2, l4_2 = _attn_fwd(2, *qkv_args(2))
    full = dict(zip(GATHER_1, full1))
    w_conv_out, w_attn_out, w_o, w_gate = (full[n] for n in GATHER_1)
    w_conv_out_f = w_conv_out.reshape(D_MODEL, D_MODEL)
    w_o_f = w_o.reshape(D_MODEL, D_MODEL)
    mix3, lse3, lse_r1, lse_r2 = _attn_mix(o4_0.reshape(B, SEQ, GW), l4_0.reshape(B, SEQ, LANES), [o4_1, o4_2], [l4_1, l4_2])
    mix = mix3.reshape(T, GW)
    y_attn = _mm("attn_out", mix, w_attn_out, grid=(N_CHIPS, T // 512),
                 a_spec=pl.BlockSpec((512, GW), lambda s, m: (m, 0)),
                 b_spec=pl.BlockSpec((None, GW, D_MODEL // N_CHIPS), lambda s, m: (s, 0, 0)),
                 o_spec=pl.BlockSpec((512, D_MODEL // N_CHIPS), lambda s, m: (m, s)), o_shape=(T, D_MODEL), o_dtype=bf16,
                 dims=NN, sem=("parallel", "parallel"))

    c1, got2 = _glu_conv_fwd(proj3, conv_w, conv_b, comm=_x_gather_ici([shards[n] for n in GATHER_2], [slots[n] for n in GATHER_2]))
    c1 = c1.reshape(T, D_MODEL)
    c3 = _ln_silu_fwd(c1, conv_ln_g, conv_ln_b)
    y_conv, (w_up, w_down) = _mm_nn_full("conv_out", c3, w_conv_out_f, bf16, comm=_x_gather_fwd(list(got2)))

    merged = _merge_fwd(proj, gate_b, y_attn, y_conv)
    x1, h2 = _o_proj_rms(merged, w_o_f, xf, norm2_g)
    fa, fb, ff = _ffn_up(h2, w_gate, w_up)
    loss, dx2, d_gf = _ffn_down_loss(ff, w_down, x1, norm_f_g.reshape(1, D_MODEL), tf)

    d_w_down = _mm("d_w_down", ff, dx2, grid=(N_CHIPS, T // 1024),
                   a_spec=pl.BlockSpec((None, 1024, FF_S), lambda s, k: (s, k, 0)),
                   b_spec=pl.BlockSpec((1024, D_MODEL), lambda s, k: (k, 0)),
                   o_spec=pl.BlockSpec((None, FF_S, D_MODEL), lambda s, k: (s, 0, 0)), o_shape=(N_CHIPS, FF_S, D_MODEL),
                   o_dtype=bf16, dims=TN, acc_shape=(FF_S, D_MODEL), k_axis=1, sem=("parallel", "arbitrary"))
    da, db = _ffn_down_bwd(dx2, w_down, fa, fb)

    def d_w_ff(name, dz):
        return _mm(name, dz, h2, grid=(N_CHIPS, T // 2048),
                   a_spec=pl.BlockSpec((None, 2048, FF_S), lambda s, k: (s, k, 0)),
                   b_spec=pl.BlockSpec((2048, D_MODEL), lambda s, k: (k, 0)),
                   o_spec=pl.BlockSpec((None, FF_S, D_MODEL), lambda s, k: (s, 0, 0)), o_shape=(N_CHIPS, FF_S, D_MODEL),
                   o_dtype=bf16, dims=TN, acc_shape=(FF_S, D_MODEL), k_axis=1, sem=("parallel", "arbitrary"))

    d_w_gate = d_w_ff("d_w_gate", da)
    d_w_up = d_w_ff("d_w_up", db)
    part = dict(w_ffn_gate=d_w_gate, w_ffn_up=d_w_up, w_ffn_down=d_w_down)

    def pair_sums(names, got):
        out = {}
        for ns, gs in _same_shape_groups(names, got):
            out.update(zip(ns, _add_pairs([part[n] for n in ns], gs, core, "pair_sum_" + ns[0])))
        return [out[n] for n in names]

    (dx1, d_g2), got = _mm_rms_bwd("ffn_dh2", [(da, w_gate), (db, w_up)], x1, norm2_g, dx2, tm=1024,
                                   a_spec=pl.BlockSpec((None, 1024, FF_S), lambda m, s: (s, m, 0)),
                                   b_spec=pl.BlockSpec((None, FF_S, D_MODEL), lambda m, s: (s, 0, 0)), dims=NN,
                                   comm=_x_pair_exchange([part[n] for n in REDUCE_A]), vmem=VMEM_LIMIT_BIG)
    sums_a = pair_sums(REDUCE_A, got)

    d_w_o = _mm_tn_tokens("d_w_o", merged, dx1, bf16).reshape(N_CHIPS, D_MODEL // N_CHIPS, D_MODEL)
    dmerged = _mm_nt_full("d_merged", dx1, w_o_f, bf16)
    dya, dyc, dga, dgc, d_gba, d_gbc = _merge_bwd(dmerged, proj, gate_b, y_attn, y_conv)

    d_w_conv_out = _mm_tn_tokens("d_w_conv_out", c3, dyc, bf16).reshape(N_CHIPS, D_MODEL // N_CHIPS, D_MODEL)
    dc1, d_ln_g, d_ln_b = _conv_out_bwd(dyc, w_conv_out_f, c1, conv_ln_g, conv_ln_b)
    (dua, dub, d_conv_w, d_conv_b), got_a = _glu_conv_bwd(dc1.reshape(B, SEQ, D_MODEL), proj3, conv_w, comm=_x_chip_exchange(sums_a))

    d_w_attn_out = _mm("d_w_attn_out", mix, dya, grid=(N_CHIPS, T // 512),
                       a_spec=pl.BlockSpec((512, GW), lambda s, k: (k, 0)),
                       b_spec=pl.BlockSpec((512, D_MODEL // N_CHIPS), lambda s, k: (k, s)),
                       o_spec=pl.BlockSpec((None, GW, D_MODEL // N_CHIPS), lambda s, k: (s, 0, 0)),
                       o_shape=(N_CHIPS, GW, D_MODEL // N_CHIPS), o_dtype=bf16, dims=TN, acc_shape=(GW, D_MODEL // N_CHIPS),
                       k_axis=1, sem=("parallel", "arbitrary"))
    part.update(w_conv_out=d_w_conv_out, w_attn_out=d_w_attn_out, w_o=d_w_o)
    dmix, got = _mm("d_mix", dya, w_attn_out, grid=(T // 1024, N_CHIPS),
                    a_spec=pl.BlockSpec((1024, D_MODEL // N_CHIPS), lambda m, s: (m, s)),
                    b_spec=pl.BlockSpec((None, GW, D_MODEL // N_CHIPS), lambda m, s: (s, 0, 0)),
                    o_spec=pl.BlockSpec((1024, GW), lambda m, s: (m, 0)), o_shape=(T, GW), o_dtype=bf16, dims=NT,
                    acc_shape=(1024, GW), k_axis=1, comm=_x_pair_exchange([part[n] for n in REDUCE_B]))
    sums_b = pair_sums(REDUCE_B, got)
    dmix3 = dmix.reshape(B, SEQ, GW)
    delta3, delta_r1, delta_r2, dmix_r1, dmix_r2 = _attn_delta(dmix3, mix3)
    one = (B, 1, SEQ)
    (dq0, dk0, dv0), got_b = _attn_bwd(0, *qkv_args(0), dmix3.reshape(one + (GW,)), lse3.reshape(one + (LANES,)),
                                       delta3.reshape(one + (LANES,)), comm=_x_chip_exchange(sums_b))
    dqkv = [tuple(t.reshape(B, SEQ, GW) for t in (dq0, dk0, dv0)),
            _attn_bwd(1, *qkv_args(1), dmix_r1, lse_r1, delta_r1), _attn_bwd(2, *qkv_args(2), dmix_r2, lse_r2, delta_r2)]
    dproj = _assemble_dproj(dqkv, dua, dub, dga.reshape(B, SEQ, D_MODEL), dgc.reshape(B, SEQ, D_MODEL)).reshape(T, IN_W)

    d_w_in = _mm("d_w_in", h_t, dproj, grid=(N_CHIPS, T // 1024),
                 a_spec=pl.BlockSpec((D_MODEL, 1024), lambda s, k: (0, k)),
                 b_spec=pl.BlockSpec((1024, IN_S), lambda s, k: (k, s)),
                 o_spec=pl.BlockSpec((None, D_MODEL, IN_S), lambda s, k: (s, 0, 0)), o_shape=(N_CHIPS, D_MODEL, IN_S),
                 o_dtype=bf16, dims=NN, acc_shape=(D_MODEL, IN_S), k_axis=1, sem=("parallel", "arbitrary"))
    part.update(w_in=d_w_in)
    sums_c = pair_sums(REDUCE_C, _run_exchange(_x_pair_exchange([d_w_in]), "grad_pair_exchange_c"))
    (dx, d_g1), got_c = _mm_rms_bwd("d_h", [(dproj, w_in)], xf, norm1_g, dx1, tm=512,
                                    a_spec=pl.BlockSpec((512, IN_S), lambda m, s: (m, s)),
                                    b_spec=pl.BlockSpec((None, D_MODEL, IN_S), lambda m, s: (s, 0, 0)), dims=NT,
                                    comm=_x_chip_exchange(sums_c))

    names = REDUCE_A + REDUCE_B + REDUCE_C
    sums = dict(zip(names, sums_a + sums_b + sums_c))
    halves = {}
    for ns, gs in _same_shape_groups(names, got_a + got_b + got_c):
        halves.update(zip(ns, _sum4([sums[n] for n in ns], gs, chip1, core, "chip_sum_" + ns[0])))
    big = dict(zip(names, _pair_gather([halves[n] for n in names])))
    small = dict(norm1_g=d_g1, gate_b=jnp.concatenate([d_gba, d_gbc], axis=-1), conv_b=d_conv_b, conv_ln_g=d_ln_g,
                 conv_ln_b=d_ln_b, norm2_g=d_g2, norm_f_g=d_gf, conv_w=d_conv_w)
    return loss, dx.reshape(B, SEQ, D_MODEL), big, small


BIG = ("w_in", "w_conv_out", "w_attn_out", "w_o", "w_ffn_gate", "w_ffn_up", "w_ffn_down")
TRANSPOSED = ("w_ffn_gate", "w_ffn_up")
SMALL = ("norm1_g", "gate_b", "conv_b", "conv_ln_g", "conv_ln_b", "norm2_g", "norm_f_g")
SMALL_ROWS = {"norm1_g": 8, "gate_b": 16, "conv_b": 8, "conv_ln_g": 8, "conv_ln_b": 8, "norm2_g": 8, "norm_f_g": 8}
LOSS_ROWS = 8
CONVW_ROWS = 32 * D_MODEL // LANES


def kernel(x, norm1_g, w_in, gate_b, conv_w, conv_b, conv_ln_g, conv_ln_b, w_conv_out, w_attn_out, w_o, norm2_g, w_ffn_gate, w_ffn_up, w_ffn_down, norm_f_g, loss_target, m_norm1_g, m_w_in, m_gate_b, m_conv_w, m_conv_b, m_conv_ln_g, m_conv_ln_b, m_w_conv_out, m_w_attn_out, m_w_o, m_norm2_g, m_w_ffn_gate, m_w_ffn_up, m_w_ffn_down, m_norm_f_g, v_norm1_g, v_w_in, v_gate_b, v_conv_w, v_conv_b, v_conv_ln_g, v_conv_ln_b, v_w_conv_out, v_w_attn_out, v_w_o, v_norm2_g, v_w_ffn_gate, v_w_ffn_up, v_w_ffn_down, v_norm_f_g):
    W = dict(norm1_g=norm1_g, w_in=w_in, gate_b=gate_b, conv_w=conv_w, conv_b=conv_b, conv_ln_g=conv_ln_g, conv_ln_b=conv_ln_b,
             w_conv_out=w_conv_out, w_attn_out=w_attn_out, w_o=w_o, norm2_g=norm2_g, w_ffn_gate=w_ffn_gate, w_ffn_up=w_ffn_up,
             w_ffn_down=w_ffn_down, norm_f_g=norm_f_g)
    M = dict(norm1_g=m_norm1_g, w_in=m_w_in, gate_b=m_gate_b, conv_w=m_conv_w, conv_b=m_conv_b, conv_ln_g=m_conv_ln_g,
             conv_ln_b=m_conv_ln_b, w_conv_out=m_w_conv_out, w_attn_out=m_w_attn_out, w_o=m_w_o, norm2_g=m_norm2_g,
             w_ffn_gate=m_w_ffn_gate, w_ffn_up=m_w_ffn_up, w_ffn_down=m_w_ffn_down, norm_f_g=m_norm_f_g)
    V = dict(norm1_g=v_norm1_g, w_in=v_w_in, gate_b=v_gate_b, conv_w=v_conv_w, conv_b=v_conv_b, conv_ln_g=v_conv_ln_g,
             conv_ln_b=v_conv_ln_b, w_conv_out=v_w_conv_out, w_attn_out=v_w_attn_out, w_o=v_w_o, norm2_g=v_norm2_g,
             w_ffn_gate=v_w_ffn_gate, w_ffn_up=v_w_ffn_up, w_ffn_down=v_w_ffn_down, norm_f_g=v_norm_f_g)
    order = list(W)

    def as2d(n, a):
        a = a.reshape(a.shape[-2:])
        return a.T if n in TRANSPOSED else a

    def from2d(n, a):
        return (a.T if n in TRANSPOSED else a).reshape(W[n].shape)

    shard2d = {n: as2d(n, W[n]) for n in BIG}
    chip = 2 * lax.axis_index("x") + lax.axis_index("y")

    core = lax.axis_index("c").astype(jnp.int32).reshape(1)
    chip1 = chip.astype(jnp.int32).reshape(1)
    shards = {n: shard2d[n].astype(bf16) for n in BIG}

    def slot_for(s):
        return lax.dynamic_update_slice(lax.empty((N_CHIPS,) + s.shape, s.dtype), s[None], (chip, 0, 0))

    slots = {n: slot_for(s) for n, s in shards.items()}
    cw = jnp.pad(conv_w.reshape(CONV_K, D_MODEL // N_CHIPS), ((0, CONV_HALO - CONV_K), (0, 0)))
    w_in_full, cw4 = _allgather_weights([shards["w_in"], cw], [slots["w_in"], slot_for(cw)])
    conv_w_full = cw4.transpose(1, 0, 2).reshape(CONV_HALO, D_MODEL)[:CONV_K]

    loss, grad_x, grads, gsmall = _step(x, loss_target, norm1_g, gate_b, conv_w_full, conv_b, conv_ln_g, conv_ln_b, norm2_g,
                                        norm_f_g, w_in_full, shards, slots, chip1, core)

    pieces = [_rows128(loss, LOSS_ROWS)] + [_rows128(gsmall[n], SMALL_ROWS[n]) for n in SMALL] + [_rows128(gsmall["conv_w"], CONVW_ROWS)]
    tot = _small_allreduce(jnp.concatenate(pieces, axis=0), "small_allreduce")
    loss_out = tot[0, 0]
    row = LOSS_ROWS
    for n in SMALL:
        grads[n] = tot[row: row + W[n].size // LANES].reshape(W[n].shape)
        row += SMALL_ROWS[n]
    dcw = tot[row: row + CONV_K * D_MODEL // LANES].reshape(CONV_K, D_MODEL)
    grads["conv_w"] = lax.dynamic_slice(dcw, (0, chip * (D_MODEL // N_CHIPS)), (CONV_K, D_MODEL // N_CHIPS))

    delta, new_m, new_v = {}, {}, {}
    for ns, ws in _same_shape_groups(BIG, [shard2d[n] for n in BIG]):
        ds, nms, nvs = _adamw(ws, [grads[n] for n in ns], [as2d(n, M[n]) for n in ns], [as2d(n, V[n]) for n in ns], "adamw_" + ns[0])
        for n, d, nm, nv in zip(ns, ds, nms, nvs):
            delta[n], new_m[n], new_v[n], grads[n] = (from2d(n, t) for t in (d, nm, nv, grads[n]))

    def pack(src):
        return jnp.concatenate([_rows128(src[n], SMALL_ROWS[n]) for n in SMALL], axis=0)

    (d,), (nm,), (nv,) = _adamw([pack(W)], [pack(grads)], [pack(M)], [pack(V)], "adamw_small")
    row = 0
    for n in SMALL:
        k = W[n].size // LANES
        delta[n], new_m[n], new_v[n] = (t[row: row + k].reshape(W[n].shape) for t in (d, nm, nv))
        row += SMALL_ROWS[n]

    def pad32(a):
        return jnp.pad(a.reshape(CONV_K, D_MODEL // N_CHIPS), ((0, 1), (0, 0)))

    (d,), (nm,), (nv,) = _adamw([pad32(conv_w)], [pad32(grads["conv_w"])], [pad32(m_conv_w)], [pad32(v_conv_w)], "adamw_conv_w")
    delta["conv_w"], new_m["conv_w"], new_v["conv_w"] = (t[:CONV_K].reshape(conv_w.shape) for t in (d, nm, nv))
    grads["conv_w"] = grads["conv_w"].reshape(conv_w.shape)

    return (loss_out, grad_x, *[grads[n] for n in order], *[delta[n] for n in order],
            *[new_m[n] for n in order], *[new_v[n] for n in order])
```

```python
import functools
import math
from typing import Callable, NamedTuple

import numpy as np
import jax
import jax.numpy as jnp
from jax import lax
from jax.experimental import pallas as pl
from jax.experimental.pallas import tpu as pltpu

f32 = jnp.float32
bf16 = jnp.bfloat16
SDS = jax.ShapeDtypeStruct
MESH = pl.DeviceIdType.MESH

D_MODEL = 1024
SEQ = 2048
HEAD_DIM = 64
HEADS = 8
GROUPS = ((128, 1), (512, 4), (2048, 16))
GW = HEADS * HEAD_DIM
ATTN_W = len(GROUPS) * GW
Q_BLOCK = 128
CONV_K = 31
D_FF = 2816
IN_W = 3 * ATTN_W + 2 * D_MODEL + 2 * D_MODEL
N_CHIPS = 4
IN_S = IN_W // N_CHIPS
FF_S = D_FF // N_CHIPS
RMS_EPS = 1e-6
LN_EPS = 1e-5
LR, B1, B2, ADAM_EPS, WD, STEP = 0.001, 0.9, 0.999, 1e-08, 0.01, 10
NEG = -1e30
LANES = 128
VMEM_LIMIT = 48 * 2 ** 20
VMEM_LIMIT_BIG = 52 * 2 ** 20
CB = 512
UA_CB, UB_CB, GA_CB, GC_CB = 9, 11, 13, 15


def _alibi_slope_list(n):
    def pow2(m):
        start = 2.0 ** (-8.0 / m)
        return [start ** (i + 1) for i in range(m)]
    if math.log2(n).is_integer():
        return pow2(n)
    c = 2 ** math.floor(math.log2(n))
    return pow2(c) + _alibi_slope_list(2 * c)[0::2][: n - c]


_SLOPES = np.asarray(sorted(_alibi_slope_list(len(GROUPS) * HEADS), reverse=True), dtype=np.float32).reshape(len(GROUPS), HEADS)


def _cp(sem=None, vmem=VMEM_LIMIT):
    return pltpu.CompilerParams(dimension_semantics=sem, vmem_limit_bytes=vmem)


def _sigmoid(x):
    return 1.0 / (1.0 + jnp.exp(-x))


HBM = pl.BlockSpec(memory_space=pl.ANY)


class _Exchange(NamedTuple):
    ins: list
    out_shape: list
    scratch: list
    aliases: dict
    start: Callable
    wait: Callable


def _launch(body, *, name, grid, in_specs, out_specs, out_shape, args, scratch_shapes=(), sem=None, comm=None, vmem=VMEM_LIMIT):
    if comm is None:
        return pl.pallas_call(body, grid=grid, in_specs=in_specs, out_specs=out_specs, out_shape=out_shape,
                              scratch_shapes=list(scratch_shapes), name=name, compiler_params=_cp(sem, vmem))(*args)
    multi = isinstance(out_shape, (tuple, list))
    m_out = list(out_shape) if multi else [out_shape]
    m_ospec = list(out_specs) if multi else [out_specs]
    n_in, n_out, n_scr = len(in_specs), len(m_out), len(scratch_shapes)
    nc_in, nc_out = len(comm.ins), len(comm.out_shape)

    def hosted(*refs):
        bounds = np.cumsum([0, n_in, nc_in, n_out, nc_out, n_scr])
        mi, ci, mo, co, ms = (refs[a:b] for a, b in zip(bounds[:-1], bounds[1:]))
        cs = refs[bounds[-1]:]
        ids = [pl.program_id(a) for a in range(len(grid))]
        first = functools.reduce(jnp.logical_and, [i == 0 for i in ids])
        last = functools.reduce(jnp.logical_and, [i == g - 1 for i, g in zip(ids, grid)])

        @pl.when(first)
        def _():
            comm.start(ci, co, cs)

        body(*mi, *mo, *ms)

        @pl.when(last)
        def _():
            comm.wait(ci, co, cs)

    res = pl.pallas_call(
        hosted, grid=grid, in_specs=list(in_specs) + [HBM] * nc_in, out_specs=m_ospec + [HBM] * nc_out,
        out_shape=tuple(m_out) + tuple(comm.out_shape), scratch_shapes=list(scratch_shapes) + list(comm.scratch),
        input_output_aliases={n_in + i: n_out + o for i, o in comm.aliases.items()}, name=name + "_comm",
        compiler_params=pltpu.CompilerParams(dimension_semantics=("arbitrary",) * len(grid), vmem_limit_bytes=vmem,
                                             has_side_effects=True))(*args, *comm.ins)
    return (tuple(res[:n_out]) if multi else res[0]), tuple(res[n_out:])


def _run_exchange(ex, name):
    n_in = len(ex.ins)

    def body(*refs):
        ins, outs, sems = refs[:n_in], refs[n_in:n_in + len(ex.out_shape)], refs[n_in + len(ex.out_shape):]
        ex.start(ins, outs, sems)
        ex.wait(ins, outs, sems)

    return pl.pallas_call(body, in_specs=[HBM] * n_in, out_specs=[HBM] * len(ex.out_shape), out_shape=tuple(ex.out_shape),
                          scratch_shapes=list(ex.scratch), input_output_aliases=dict(ex.aliases), name=name,
                          compiler_params=pltpu.CompilerParams(has_side_effects=True))(*ex.ins)


def _rmsnorm(xv, gv):
    return xv * lax.rsqrt(jnp.mean(xv * xv, axis=-1, keepdims=True) + RMS_EPS) * gv


def _rms_fwd(x, g, name):
    T = x.shape[0]
    tm = 512

    def body(x_ref, g_ref, o_ref, ot_ref):
        hv = _rmsnorm(x_ref[...], g_ref[...])
        o_ref[...] = hv.astype(bf16)
        ot_ref[...] = hv.T.astype(bf16)

    row = pl.BlockSpec((tm, D_MODEL), lambda i: (i, 0))
    vec = pl.BlockSpec((1, D_MODEL), lambda i: (0, 0))
    return pl.pallas_call(body, grid=(T // tm,), in_specs=[row, vec], out_specs=[row, pl.BlockSpec((D_MODEL, tm), lambda i: (0, i))],
                          out_shape=(SDS((T, D_MODEL), bf16), SDS((D_MODEL, T), bf16)), name=name,
                          compiler_params=_cp(("parallel",)))(x, g)


def _rms_bwd_tile(dyv, xv, gv, dres):
    r = lax.rsqrt(jnp.mean(xv * xv, axis=-1, keepdims=True) + RMS_EPS)
    xh = xv * r
    dxh = dyv * gv
    dx = dres + r * (dxh - xh * jnp.mean(dxh * xh, axis=-1, keepdims=True))
    return dx, jnp.sum(dyv * xh, axis=0, keepdims=True)


def _accumulate(first, refs_parts):
    @pl.when(first)
    def _():
        for ref, part in refs_parts:
            ref[...] = part

    @pl.when(jnp.logical_not(first))
    def _():
        for ref, part in refs_parts:
            ref[...] += part


def _mm_rms_bwd(name, ops, x, g, dres, *, tm, a_spec, b_spec, dims, comm=None, vmem=VMEM_LIMIT):
    T = x.shape[0]
    n = len(ops)

    def body(*refs):
        ab_refs = refs[:2 * n]
        x_ref, g_ref, r_ref, dx_ref, dg_ref, acc = refs[2 * n:]
        m, s = pl.program_id(0), pl.program_id(1)
        p = sum(lax.dot_general(ab_refs[2 * i][...], ab_refs[2 * i + 1][...], dims, preferred_element_type=f32) for i in range(n))

        @pl.when(s == 0)
        def _():
            acc[...] = p

        @pl.when(s > 0)
        def _():
            acc[...] += p

        @pl.when(s == N_CHIPS - 1)
        def _():
            dx, part = _rms_bwd_tile(acc[...], x_ref[...], g_ref[...], r_ref[...])
            dx_ref[...] = dx
            _accumulate(m == 0, [(dg_ref, part)])

    row = pl.BlockSpec((tm, D_MODEL), lambda m, s: (m, 0))
    vec = pl.BlockSpec((1, D_MODEL), lambda m, s: (0, 0))
    return _launch(body, name=name, grid=(T // tm, N_CHIPS), in_specs=[a_spec, b_spec] * n + [row, vec, row], out_specs=[row, vec],
                   out_shape=(SDS((T, D_MODEL), f32), SDS((1, D_MODEL), f32)), args=tuple(t for ab in ops for t in ab) + (x, g, dres),
                   scratch_shapes=[pltpu.VMEM((tm, D_MODEL), f32)], sem=("arbitrary", "arbitrary"), comm=comm, vmem=vmem)


def _ffn_down_loss(ff, wd, x1, gf, target):
    T = x1.shape[0]
    tm = 1024
    nm = T // tm

    def body(f_ref, w_ref, x_ref, g_ref, t_ref, loss_ref, dx_ref, dg_ref, acc):
        m, s = pl.program_id(0), pl.program_id(1)
        p = jnp.dot(f_ref[...], w_ref[...], preferred_element_type=f32)

        @pl.when(s == 0)
        def _():
            acc[...] = p

        @pl.when(s > 0)
        def _():
            acc[...] += p

        @pl.when(s == N_CHIPS - 1)
        def _():
            xv = acc[...] + x_ref[...]
            gv = g_ref[...]
            r = lax.rsqrt(jnp.mean(xv * xv, axis=-1, keepdims=True) + RMS_EPS)
            xh = xv * r
            e = xh * gv - t_ref[...]
            part_l = jnp.broadcast_to(0.5 * jnp.sum(jnp.mean(e * e, axis=-1, keepdims=True), axis=0, keepdims=True), (1, LANES))
            dy = e * (1.0 / D_MODEL)
            dxh = dy * gv
            dx_ref[...] = r * (dxh - xh * jnp.mean(dxh * xh, axis=-1, keepdims=True))
            part_g = jnp.sum(dy * xh, axis=0, keepdims=True)

            @pl.when(m == 0)
            def _():
                loss_ref[...] = part_l
                dg_ref[...] = part_g

            @pl.when(m > 0)
            def _():
                loss_ref[...] += part_l
                dg_ref[...] += part_g

    row = pl.BlockSpec((tm, D_MODEL), lambda m, s: (m, 0))
    vec = pl.BlockSpec((1, D_MODEL), lambda m, s: (0, 0))
    return pl.pallas_call(
        body, grid=(nm, N_CHIPS),
        in_specs=[pl.BlockSpec((None, tm, FF_S), lambda m, s: (s, m, 0)), pl.BlockSpec((None, FF_S, D_MODEL), lambda m, s: (s, 0, 0)),
                  row, vec, row],
        out_specs=[pl.BlockSpec((1, LANES), lambda m, s: (0, 0)), row, vec],
        out_shape=(SDS((1, LANES), f32), SDS((T, D_MODEL), f32), SDS((1, D_MODEL), f32)),
        scratch_shapes=[pltpu.VMEM((tm, D_MODEL), f32)], name="ffn_down_loss", compiler_params=_cp(("arbitrary", "arbitrary")))(
            ff, wd, x1, gf, target)


def _o_proj_rms(merged, w_o, x, g2):
    T = x.shape[0]
    tm = 1024

    def body(a_ref, w_ref, x_ref, g_ref, x1_ref, h2_ref):
        x1 = jnp.dot(a_ref[...], w_ref[...], preferred_element_type=f32) + x_ref[...]
        x1_ref[...] = x1
        h2_ref[...] = _rmsnorm(x1, g_ref[...]).astype(bf16)

    row = pl.BlockSpec((tm, D_MODEL), lambda i: (i, 0))
    return pl.pallas_call(body, grid=(T // tm,),
                          in_specs=[row, pl.BlockSpec((D_MODEL, D_MODEL), lambda i: (0, 0)), row, pl.BlockSpec((1, D_MODEL), lambda i: (0, 0))],
                          out_specs=[row, row], out_shape=(SDS((T, D_MODEL), f32), SDS((T, D_MODEL), bf16)), name="o_proj_rms",
                          compiler_params=_cp(("parallel",)))(merged, w_o, x, g2)


def _ln_silu_fwd(c1, g, b):
    T = c1.shape[0]
    tm = 512

    def body(c_ref, g_ref, b_ref, o_ref):
        cv = c_ref[...]
        mu = jnp.mean(cv, axis=-1, keepdims=True)
        cc = cv - mu
        var = jnp.mean(cc * cc, axis=-1, keepdims=True)
        c2 = cc * lax.rsqrt(var + LN_EPS) * g_ref[...] + b_ref[...]
        o_ref[...] = (c2 * _sigmoid(c2)).astype(o_ref.dtype)

    row = pl.BlockSpec((tm, D_MODEL), lambda i: (i, 0))
    vec = pl.BlockSpec((1, D_MODEL), lambda i: (0, 0))
    return pl.pallas_call(body, grid=(T // tm,), in_specs=[row, vec, vec], out_specs=row,
                          out_shape=SDS((T, D_MODEL), bf16), name="ln_silu_fwd", compiler_params=_cp(("parallel",)))(c1, g, b)


def _conv_out_bwd(dyc, w, c1, g, b):
    T = c1.shape[0]
    tm = 1024

    def body(d_ref, w_ref, c_ref, g_ref, b_ref, dc_ref, dg_ref, db_ref):
        dc3 = lax.dot_general(d_ref[...], w_ref[...], NT, preferred_element_type=f32)
        cv = c_ref[...]
        gv = g_ref[...]
        mu = jnp.mean(cv, axis=-1, keepdims=True)
        cc = cv - mu
        var = jnp.mean(cc * cc, axis=-1, keepdims=True)
        rs = lax.rsqrt(var + LN_EPS)
        xh = cc * rs
        c2 = xh * gv + b_ref[...]
        sg = _sigmoid(c2)
        dc2 = dc3 * (sg * (1.0 + c2 * (1.0 - sg)))
        dxh = dc2 * gv
        dc_ref[...] = rs * (dxh - jnp.mean(dxh, axis=-1, keepdims=True) - xh * jnp.mean(dxh * xh, axis=-1, keepdims=True))
        _accumulate(pl.program_id(0) == 0, [(dg_ref, jnp.sum(dc2 * xh, axis=0, keepdims=True)), (db_ref, jnp.sum(dc2, axis=0, keepdims=True))])

    row = pl.BlockSpec((tm, D_MODEL), lambda i: (i, 0))
    vec = pl.BlockSpec((1, D_MODEL), lambda i: (0, 0))
    return pl.pallas_call(body, grid=(T // tm,), in_specs=[row, pl.BlockSpec((D_MODEL, D_MODEL), lambda i: (0, 0)), row, vec, vec],
                          out_specs=[row, vec, vec],
                          out_shape=(SDS((T, D_MODEL), f32), SDS((1, D_MODEL), f32), SDS((1, D_MODEL), f32)),
                          name="conv_out_bwd", compiler_params=_cp(("arbitrary",)))(dyc, w, c1, g, b)


NN = (((1,), (0,)), ((), ()))
NT = (((1,), (1,)), ((), ()))
TN = (((0,), (0,)), ((), ()))


def _mm(name, a, b, *, grid, a_spec, b_spec, o_spec, o_shape, o_dtype, dims, acc_shape=None, k_axis=None,
        res=None, res_spec=None, sem=None, comm=None):
    nk = 1 if k_axis is None else grid[k_axis]

    def body(*refs):
        if res is None:
            a_ref, b_ref, o_ref = refs[:3]
            r_ref, scr = None, refs[3:]
        else:
            a_ref, b_ref, r_ref, o_ref = refs[:4]
            scr = refs[4:]
        p = lax.dot_general(a_ref[...].astype(bf16), b_ref[...].astype(bf16), dims, preferred_element_type=f32)
        if nk == 1:
            if r_ref is not None:
                p = p + r_ref[...]
            o_ref[...] = p.astype(o_dtype)
            return
        acc = scr[0]
        k = pl.program_id(k_axis)

        @pl.when(k == 0)
        def _():
            acc[...] = p

        @pl.when(k > 0)
        def _():
            acc[...] += p

        @pl.when(k == nk - 1)
        def _():
            t = acc[...]
            if r_ref is not None:
                t = t + r_ref[...]
            o_ref[...] = t.astype(o_dtype)

    ins = [a, b] + ([] if res is None else [res])
    specs = [a_spec, b_spec] + ([] if res is None else [res_spec])
    scratch = [] if nk == 1 else [pltpu.VMEM(acc_shape, f32)]
    return _launch(body, name=name, grid=grid, in_specs=specs, out_specs=o_spec, out_shape=SDS(o_shape, o_dtype),
                   args=ins, scratch_shapes=scratch, sem=sem, comm=comm)


def _mm_nn_full(name, a, b, o_dtype, res=None, tm=1024, comm=None):
    T, K = a.shape
    N = b.shape[1]
    return _mm(name, a, b, grid=(T // tm,), a_spec=pl.BlockSpec((tm, K), lambda i: (i, 0)),
               b_spec=pl.BlockSpec((K, N), lambda i: (0, 0)), o_spec=pl.BlockSpec((tm, N), lambda i: (i, 0)),
               o_shape=(T, N), o_dtype=o_dtype, dims=NN, res=res,
               res_spec=None if res is None else pl.BlockSpec((tm, N), lambda i: (i, 0)), sem=("parallel",), comm=comm)


def _mm_nt_full(name, a, b, o_dtype, tm=1024):
    T, N = a.shape
    K = b.shape[0]
    return _mm(name, a, b, grid=(T // tm,), a_spec=pl.BlockSpec((tm, N), lambda i: (i, 0)),
               b_spec=pl.BlockSpec((K, N), lambda i: (0, 0)), o_spec=pl.BlockSpec((tm, K), lambda i: (i, 0)),
               o_shape=(T, K), o_dtype=o_dtype, dims=NT, sem=("parallel",))


def _mm_tn_tokens(name, a, b, o_dtype, tk=1024):
    T, K = a.shape
    N = b.shape[1]
    return _mm(name, a, b, grid=(T // tk,), a_spec=pl.BlockSpec((tk, K), lambda k: (k, 0)),
               b_spec=pl.BlockSpec((tk, N), lambda k: (k, 0)), o_spec=pl.BlockSpec((K, N), lambda k: (0, 0)),
               o_shape=(K, N), o_dtype=o_dtype, dims=TN, acc_shape=(K, N), k_axis=0, sem=("arbitrary",))


def _ffn_up(h2, wg, wu):
    T = h2.shape[0]
    tm = 1024

    def body(h_ref, wg_ref, wu_ref, a_ref, b_ref, f_ref):
        hv = h_ref[...]
        av = lax.dot_general(hv, wg_ref[...], NT, preferred_element_type=f32)
        bv = lax.dot_general(hv, wu_ref[...], NT, preferred_element_type=f32)
        a_ref[...] = av.astype(bf16)
        b_ref[...] = bv.astype(bf16)
        f_ref[...] = (av * _sigmoid(av) * bv).astype(bf16)

    wspec = pl.BlockSpec((None, FF_S, D_MODEL), lambda s, m: (s, 0, 0))
    ospec = pl.BlockSpec((None, tm, FF_S), lambda s, m: (s, m, 0))
    osd = SDS((N_CHIPS, T, FF_S), bf16)
    return pl.pallas_call(body, grid=(N_CHIPS, T // tm),
                          in_specs=[pl.BlockSpec((tm, D_MODEL), lambda s, m: (m, 0)), wspec, wspec],
                          out_specs=[ospec, ospec, ospec], out_shape=(osd, osd, osd), name="ffn_up",
                          compiler_params=_cp(("parallel", "parallel")))(h2, wg, wu)


def _ffn_down_bwd(dx2, wd, a, b):
    T = dx2.shape[0]
    tm = 1024

    def body(d_ref, w_ref, a_ref, b_ref, da_ref, db_ref):
        df = lax.dot_general(d_ref[...].astype(bf16), w_ref[...], NT, preferred_element_type=f32)
        av = a_ref[...].astype(f32)
        sg = _sigmoid(av)
        da_ref[...] = (df * b_ref[...].astype(f32) * (sg * (1.0 + av * (1.0 - sg)))).astype(bf16)
        db_ref[...] = (df * av * sg).astype(bf16)

    aspec = pl.BlockSpec((None, tm, FF_S), lambda m, s: (s, m, 0))
    osd = SDS((N_CHIPS, T, FF_S), bf16)
    return pl.pallas_call(body, grid=(T // tm, N_CHIPS),
                          in_specs=[pl.BlockSpec((tm, D_MODEL), lambda m, s: (m, 0)),
                                    pl.BlockSpec((None, FF_S, D_MODEL), lambda m, s: (s, 0, 0)), aspec, aspec],
                          out_specs=[aspec, aspec], out_shape=(osd, osd), name="ffn_down_bwd",
                          compiler_params=_cp(("parallel", "parallel")))(dx2, wd, a, b)


def _merge_fwd(proj, gate_b, ya, yc):
    T = proj.shape[0]
    tm = 512

    def body(ga_ref, gc_ref, ba_ref, bc_ref, ya_ref, yc_ref, o_ref):
        sa = _sigmoid(ga_ref[...].astype(f32) + ba_ref[...])
        sc = _sigmoid(gc_ref[...].astype(f32) + bc_ref[...])
        o_ref[...] = (sa * ya_ref[...].astype(f32) + sc * yc_ref[...].astype(f32)).astype(bf16)

    blk = pl.BlockSpec((tm, CB), lambda i, j: (i, j))
    return pl.pallas_call(
        body, grid=(T // tm, 2),
        in_specs=[pl.BlockSpec((tm, CB), lambda i, j: (i, GA_CB + j)), pl.BlockSpec((tm, CB), lambda i, j: (i, GC_CB + j)),
                  pl.BlockSpec((1, CB), lambda i, j: (0, j)), pl.BlockSpec((1, CB), lambda i, j: (0, 2 + j)), blk, blk],
        out_specs=blk, out_shape=SDS((T, D_MODEL), bf16), name="merge_fwd",
        compiler_params=_cp(("parallel", "parallel")))(proj, proj, gate_b, gate_b, ya, yc)


def _merge_bwd(dm, proj, gate_b, ya, yc):
    T = proj.shape[0]
    tm = 512

    def body(dm_ref, ga_ref, gc_ref, ba_ref, bc_ref, ya_ref, yc_ref, dya_ref, dyc_ref, dga_ref, dgc_ref, dba_ref, dbc_ref):
        dmv = dm_ref[...].astype(f32)
        sa = _sigmoid(ga_ref[...].astype(f32) + ba_ref[...])
        sc = _sigmoid(gc_ref[...].astype(f32) + bc_ref[...])
        dya_ref[...] = (dmv * sa).astype(bf16)
        dyc_ref[...] = (dmv * sc).astype(bf16)
        dga = dmv * ya_ref[...].astype(f32) * sa * (1.0 - sa)
        dgc = dmv * yc_ref[...].astype(f32) * sc * (1.0 - sc)
        dga_ref[...] = dga.astype(bf16)
        dgc_ref[...] = dgc.astype(bf16)
        pa = jnp.sum(dga, axis=0, keepdims=True)
        pc = jnp.sum(dgc, axis=0, keepdims=True)

        @pl.when(pl.program_id(1) == 0)
        def _():
            dba_ref[...] = pa
            dbc_ref[...] = pc

        @pl.when(pl.program_id(1) > 0)
        def _():
            dba_ref[...] += pa
            dbc_ref[...] += pc

    blk = pl.BlockSpec((tm, CB), lambda j, i: (i, j))
    vec = pl.BlockSpec((1, CB), lambda j, i: (0, j))
    big = SDS((T, D_MODEL), bf16)
    small = SDS((1, D_MODEL), f32)
    return pl.pallas_call(
        body, grid=(2, T // tm),
        in_specs=[blk, pl.BlockSpec((tm, CB), lambda j, i: (i, GA_CB + j)), pl.BlockSpec((tm, CB), lambda j, i: (i, GC_CB + j)),
                  vec, pl.BlockSpec((1, CB), lambda j, i: (0, 2 + j)), blk, blk],
        out_specs=[blk, blk, blk, blk, vec, vec], out_shape=(big, big, big, big, small, small), name="merge_bwd",
        compiler_params=_cp(("parallel", "arbitrary")))(dm, proj, proj, gate_b, gate_b, ya, yc)


CONV_TS = 512
CONV_HALO = 32
CONV_RC = 64
CONV_WIN = CONV_TS + CONV_HALO
SUBLANES = 8


def _fill_shifted(win, sh):
    for b in range(1, SUBLANES):
        sh[b - 1] = win[pl.ds(b, CONV_WIN - SUBLANES), :]


def _rows_at(win, sh, row):
    a, b = divmod(row, SUBLANES)
    if b == 0:
        return win[pl.ds(row, CONV_RC), :]
    return sh[b - 1, pl.ds(a * SUBLANES, CONV_RC), :]


def _glu_conv_fwd(proj3, w, bias, comm=None):
    B = proj3.shape[0]
    nt = SEQ // CONV_TS
    hb = CONV_TS // CONV_HALO

    def body(ua_ref, ub_ref, ha_ref, hb_ref, w_ref, b_ref, o_ref, win, sh):
        i = pl.program_id(2)
        c0 = ua_ref[...].astype(f32) * _sigmoid(ub_ref[...].astype(f32))
        halo = ha_ref[...].astype(f32) * _sigmoid(hb_ref[...].astype(f32))
        win[0:CONV_HALO, :] = jnp.where(i > 0, halo, 0.0)
        win[CONV_HALO:, :] = c0
        _fill_shifted(win, sh)
        for r0 in range(0, CONV_TS, CONV_RC):
            acc = jnp.zeros((CONV_RC, CB), f32) + b_ref[...]
            for k in range(CONV_K):
                acc = acc + _rows_at(win, sh, r0 + CONV_HALO - (CONV_K - 1) + k) * w_ref[k:k + 1, :]
            o_ref[r0:r0 + CONV_RC, :] = acc

    def cur(cb):
        return pl.BlockSpec((None, CONV_TS, CB), lambda b, j, i: (b, i, cb + j))

    def prev(cb):
        return pl.BlockSpec((None, CONV_HALO, CB), lambda b, j, i: (b, jnp.maximum(i * hb - 1, 0), cb + j))

    return _launch(
        body, name="glu_conv_fwd", grid=(B, 2, nt),
        in_specs=[cur(UA_CB), cur(UB_CB), prev(UA_CB), prev(UB_CB),
                  pl.BlockSpec((CONV_K, CB), lambda b, j, i: (0, j)), pl.BlockSpec((1, CB), lambda b, j, i: (0, j))],
        out_specs=pl.BlockSpec((None, CONV_TS, CB), lambda b, j, i: (b, i, j)),
        out_shape=SDS((B, SEQ, D_MODEL), f32), args=(proj3, proj3, proj3, proj3, w, bias),
        scratch_shapes=[pltpu.VMEM((CONV_WIN, CB), f32), pltpu.VMEM((SUBLANES - 1, CONV_WIN - SUBLANES, CB), f32)],
        sem=("parallel", "parallel", "parallel"), comm=comm)


def _glu_conv_bwd(dc1, proj3, w, comm=None):
    B = proj3.shape[0]
    nt = SEQ // CONV_TS
    hb = CONV_TS // CONV_HALO

    def body(d_ref, dn_ref, ua_ref, ub_ref, ha_ref, hb_ref, w_ref, dua_ref, dub_ref, dw_ref, db_ref, winc, wind, accw, shc, shd):
        b = pl.program_id(1)
        i = pl.program_id(2)
        first = jnp.logical_and(b == 0, i == 0)
        last = jnp.logical_and(b == B - 1, i == nt - 1)

        @pl.when(first)
        def _():
            accw[...] = jnp.zeros_like(accw)
            db_ref[...] = jnp.zeros_like(db_ref)

        halo = ha_ref[...].astype(f32) * _sigmoid(hb_ref[...].astype(f32))
        winc[0:CONV_HALO, :] = jnp.where(i > 0, halo, 0.0)
        winc[CONV_HALO:, :] = ua_ref[...].astype(f32) * _sigmoid(ub_ref[...].astype(f32))
        wind[0:CONV_TS, :] = d_ref[...]
        wind[CONV_TS:, :] = jnp.where(i < nt - 1, dn_ref[...], 0.0)
        db_ref[...] += jnp.sum(d_ref[...], axis=0, keepdims=True)
        _fill_shifted(winc, shc)
        _fill_shifted(wind, shd)
        for r0 in range(0, CONV_TS, CONV_RC):
            dc0 = jnp.zeros((CONV_RC, CB), f32)
            for k in range(CONV_K):
                dc0 = dc0 + _rows_at(wind, shd, r0 + (CONV_K - 1) - k) * w_ref[k:k + 1, :]
            uav = ua_ref[r0:r0 + CONV_RC, :].astype(f32)
            sg = _sigmoid(ub_ref[r0:r0 + CONV_RC, :].astype(f32))
            dua_ref[r0:r0 + CONV_RC, :] = (dc0 * sg).astype(bf16)
            dub_ref[r0:r0 + CONV_RC, :] = (dc0 * uav * sg * (1.0 - sg)).astype(bf16)
            dv = wind[r0:r0 + CONV_RC, :]
            for k in range(CONV_K):
                prod = dv * _rows_at(winc, shc, r0 + CONV_HALO - (CONV_K - 1) + k)
                accw[k] += jnp.sum(prod.reshape(CONV_RC // 8, 8, CB), axis=0)

        @pl.when(last)
        def _():
            for k in range(CONV_K):
                dw_ref[k:k + 1, :] = jnp.sum(accw[k], axis=0, keepdims=True)
            dw_ref[CONV_K:, :] = jnp.zeros((CONV_HALO - CONV_K, CB), f32)

    def cur(cb):
        return pl.BlockSpec((None, CONV_TS, CB), lambda j, b, i: (b, i, cb + j))

    def prev(cb):
        return pl.BlockSpec((None, CONV_HALO, CB), lambda j, b, i: (b, jnp.maximum(i * hb - 1, 0), cb + j))

    nxt = pl.BlockSpec((None, CONV_HALO, CB), lambda j, b, i: (b, jnp.minimum((i + 1) * hb, SEQ // CONV_HALO - 1), j))
    big = SDS((B, SEQ, D_MODEL), bf16)
    return _launch(
        body, name="glu_conv_bwd", grid=(2, B, nt),
        in_specs=[cur(0), nxt, cur(UA_CB), cur(UB_CB), prev(UA_CB), prev(UB_CB), pl.BlockSpec((CONV_K, CB), lambda j, b, i: (0, j))],
        out_specs=[cur(0), cur(0), pl.BlockSpec((CONV_HALO, CB), lambda j, b, i: (0, j)), pl.BlockSpec((1, CB), lambda j, b, i: (0, j))],
        out_shape=(big, big, SDS((CONV_HALO, D_MODEL), f32), SDS((1, D_MODEL), f32)),
        args=(dc1, dc1, proj3, proj3, proj3, proj3, w),
        scratch_shapes=[pltpu.VMEM((CONV_WIN, CB), f32), pltpu.VMEM((CONV_WIN, CB), f32), pltpu.VMEM((CONV_K, SUBLANES, CB), f32),
                        pltpu.VMEM((SUBLANES - 1, CONV_WIN - SUBLANES, CB), f32),
                        pltpu.VMEM((SUBLANES - 1, CONV_WIN - SUBLANES, CB), f32)],
        sem=("parallel", "arbitrary", "arbitrary"), comm=comm)


def _band(first, dil):
    kw = Q_BLOCK if first else 2 * Q_BLOCK
    qi = lax.broadcasted_iota(jnp.int32, (Q_BLOCK, kw), 0)
    kj = lax.broadcasted_iota(jnp.int32, (Q_BLOCK, kw), 1)
    rel = qi - kj + (0 if first else Q_BLOCK)
    valid = jnp.logical_and(rel >= 0, rel <= Q_BLOCK)
    return valid, rel.astype(f32) * float(dil)


def _bias(first, dil, slope):
    valid, dist = _band(first, dil)
    return jnp.where(valid, -slope * dist, NEG)


def _scores(q, k, bias):
    return lax.dot_general(q, k, NT, preferred_element_type=f32) * (HEAD_DIM ** -0.5) + bias


def _pair_cols(hp):
    return slice(hp * LANES, (hp + 1) * LANES)


def _half(x2, e):
    lane = lax.broadcasted_iota(jnp.int32, (1, LANES), 1)
    keep = (lane < HEAD_DIM) if e == 0 else (lane >= HEAD_DIM)
    return jnp.where(keep, x2, jnp.zeros_like(x2))


def _attn_fwd(g, q4, k4, v4, qcb, kcb, vcb, comm=None):
    _, dil = GROUPS[g]
    B, r, L, _ = q4.shape
    nb = L // Q_BLOCK
    slopes = [float(s) for s in _SLOPES[g]]

    def body(q_ref, k_ref, v_ref, o_ref, lse_ref, bias, s_scr, p_scr):
        lane = lax.broadcasted_iota(jnp.int32, (Q_BLOCK, LANES), 1)
        if nb > 1:
            for h in range(HEADS):
                bias[h] = _bias(False, dil, slopes[h])

        def block(n, first):
            q0 = 0 if first else pl.multiple_of(n * Q_BLOCK, Q_BLOCK)
            k0 = 0 if first else pl.multiple_of((n - 1) * Q_BLOCK, Q_BLOCK)
            kw = Q_BLOCK if first else 2 * Q_BLOCK
            for hp in range(HEADS // 2):
                q2 = q_ref[pl.ds(q0, Q_BLOCK), _pair_cols(hp)]
                k2 = k_ref[pl.ds(k0, kw), _pair_cols(hp)]
                for e in range(2):
                    h = 2 * hp + e
                    b_h = _bias(True, dil, slopes[h]) if first else bias[h]
                    s_scr[h, :, :kw] = _scores(_half(q2, e), k2, b_h)
            st = jnp.zeros((Q_BLOCK, LANES), f32)
            dens = jnp.ones((Q_BLOCK, LANES), f32)
            for h in range(HEADS):
                s = s_scr[h, :, :kw]
                m = jnp.max(s, axis=-1, keepdims=True)
                p = jnp.exp(s - m)
                den = jnp.sum(p, axis=-1, keepdims=True)
                p_scr[h, :, :kw] = p.astype(bf16)
                st = jnp.where(lane == h, m + jnp.log(den), st)
                dens = jnp.where(lane == h, den, dens)
            lse_ref[pl.ds(q0, Q_BLOCK), :] = st
            inv = 1.0 / dens
            for hp in range(HEADS // 2):
                v2 = v_ref[pl.ds(k0, kw), _pair_cols(hp)]
                o2 = sum(jnp.dot(p_scr[2 * hp + e, :, :kw], _half(v2, e), preferred_element_type=f32) * inv[:, 2 * hp + e:2 * hp + e + 1]
                         for e in range(2))
                o_ref[pl.ds(q0, Q_BLOCK), _pair_cols(hp)] = o2.astype(bf16)

        block(0, True)
        if nb > 1:
            def step(n, carry):
                block(n, False)
                return carry
            lax.fori_loop(1, nb, step, 0)

    def spec(cb):
        return pl.BlockSpec((None, None, L, GW), lambda b, c: (b, c, 0, cb))

    return _launch(
        body, name=f"attn_fwd_g{g}", grid=(B, r), in_specs=[spec(qcb), spec(kcb), spec(vcb)],
        out_specs=[spec(0), pl.BlockSpec((None, None, L, LANES), lambda b, c: (b, c, 0, 0))],
        out_shape=(SDS((B, r, L, GW), bf16), SDS((B, r, L, LANES), f32)), args=(q4, k4, v4),
        scratch_shapes=[pltpu.VMEM((HEADS, Q_BLOCK, 2 * Q_BLOCK), f32), pltpu.VMEM((HEADS, Q_BLOCK, 2 * Q_BLOCK), f32),
                        pltpu.VMEM((HEADS, Q_BLOCK, 2 * Q_BLOCK), bf16)],
        sem=("parallel", "parallel"), comm=comm)


def _attn_bwd(g, q4, k4, v4, qcb, kcb, vcb, do4, lse4, dl4, comm=None):
    _, dil = GROUPS[g]
    B, r, L, _ = q4.shape
    nb = L // Q_BLOCK
    slopes = [float(s) for s in _SLOPES[g]]
    scale = HEAD_DIM ** -0.5

    def body(q_ref, k_ref, v_ref, do_ref, lse_ref, dl_ref, dq_ref, dk_ref, dv_ref, dk_acc, dv_acc, bias, s_scr, dp_scr, p_scr, ds_scr):
        dk_acc[...] = jnp.zeros_like(dk_acc)
        dv_acc[...] = jnp.zeros_like(dv_acc)
        if nb > 1:
            for h in range(HEADS):
                bias[h] = _bias(False, dil, slopes[h])

        def block(n, first):
            q0 = 0 if first else pl.multiple_of(n * Q_BLOCK, Q_BLOCK)
            k0 = 0 if first else pl.multiple_of((n - 1) * Q_BLOCK, Q_BLOCK)
            kw = Q_BLOCK if first else 2 * Q_BLOCK
            for hp in range(HEADS // 2):
                q2 = q_ref[pl.ds(q0, Q_BLOCK), _pair_cols(hp)]
                k2 = k_ref[pl.ds(k0, kw), _pair_cols(hp)]
                v2 = v_ref[pl.ds(k0, kw), _pair_cols(hp)]
                do2 = do_ref[pl.ds(q0, Q_BLOCK), _pair_cols(hp)]
                for e in range(2):
                    h = 2 * hp + e
                    b_h = _bias(True, dil, slopes[h]) if first else bias[h]
                    s_scr[h, :, :kw] = _scores(_half(q2, e), k2, b_h)
                    dp_scr[h, :, :kw] = lax.dot_general(_half(do2, e), v2, NT, preferred_element_type=f32)
            for h in range(HEADS):
                p = jnp.exp(s_scr[h, :, :kw] - lse_ref[pl.ds(q0, Q_BLOCK), h:h + 1])
                p_scr[h, :, :kw] = p.astype(bf16)
                ds_scr[h, :, :kw] = (p * (dp_scr[h, :, :kw] - dl_ref[pl.ds(q0, Q_BLOCK), h:h + 1])).astype(bf16)
            for hp in range(HEADS // 2):
                cols = _pair_cols(hp)
                q2 = q_ref[pl.ds(q0, Q_BLOCK), cols]
                k2 = k_ref[pl.ds(k0, kw), cols]
                do2 = do_ref[pl.ds(q0, Q_BLOCK), cols]
                ds = [ds_scr[2 * hp + e, :, :kw] for e in range(2)]
                dq2 = sum(jnp.dot(ds[e], _half(k2, e), preferred_element_type=f32) for e in range(2))
                dq_ref[pl.ds(q0, Q_BLOCK), cols] = (dq2 * scale).astype(bf16)
                dk2 = sum(lax.dot_general(ds[e], _half(q2, e), TN, preferred_element_type=f32) for e in range(2))
                dk_acc[pl.ds(k0, kw), cols] += dk2 * scale
                dv2 = sum(lax.dot_general(p_scr[2 * hp + e, :, :kw], _half(do2, e), TN, preferred_element_type=f32) for e in range(2))
                dv_acc[pl.ds(k0, kw), cols] += dv2

        block(0, True)
        if nb > 1:
            def step(n, carry):
                block(n, False)
                return carry
            lax.fori_loop(1, nb, step, 0)
        dk_ref[...] = dk_acc[...].astype(bf16)
        dv_ref[...] = dv_acc[...].astype(bf16)

    def spec(cb):
        return pl.BlockSpec((None, None, L, GW), lambda b, c: (b, c, 0, cb))

    st = pl.BlockSpec((None, None, L, LANES), lambda b, c: (b, c, 0, 0))
    osd = SDS((B, r, L, GW), bf16)
    return _launch(
        body, name=f"attn_bwd_g{g}", grid=(B, r), in_specs=[spec(qcb), spec(kcb), spec(vcb), spec(0), st, st],
        out_specs=[spec(0), spec(0), spec(0)], out_shape=(osd, osd, osd), args=(q4, k4, v4, do4, lse4, dl4),
        scratch_shapes=[pltpu.VMEM((L, GW), f32), pltpu.VMEM((L, GW), f32)]
        + [pltpu.VMEM((HEADS, Q_BLOCK, 2 * Q_BLOCK), f32)] * 3 + [pltpu.VMEM((HEADS, Q_BLOCK, 2 * Q_BLOCK), bf16)] * 2,
        sem=("parallel", "parallel"), comm=comm)


RT = 512
RCH = GW // LANES
DILS = tuple(d for _, d in GROUPS[1:])


def _res_spec(r, width):
    return pl.BlockSpec((None, r, RT // r, width), lambda b, i, *_: (b, 0, i, 0))


def _tok_spec(width, cb=0):
    return pl.BlockSpec((None, RT, width), lambda b, i, *_: (b, i, cb))


def _to_residues(res_ref, scr, r, width):
    for c in range(r):
        for jj in range(width // LANES):
            res_ref[c, :, jj * LANES:(jj + 1) * LANES] = scr[jj, pl.ds(c, RT // r, stride=r), :].astype(res_ref.dtype)


def _from_residues(scr, res_ref, r, width):
    for c in range(r):
        for jj in range(width // LANES):
            scr[jj, pl.ds(c, RT // r, stride=r), :] = res_ref[c, :, jj * LANES:(jj + 1) * LANES].astype(f32)


def _qkv_to_residues(proj3, g):
    r = GROUPS[g][1]
    B = proj3.shape[0]

    def body(q_ref, k_ref, v_ref, o_ref, scr):
        for p, x_ref in enumerate((q_ref, k_ref, v_ref)):
            x = x_ref[...].astype(f32)
            for jj in range(RCH):
                scr[p * RCH + jj] = x[:, jj * LANES:(jj + 1) * LANES]
        _to_residues(o_ref, scr, r, ATTN_W)

    return pl.pallas_call(
        body, grid=(B, SEQ // RT), in_specs=[_tok_spec(GW, 3 * p + g) for p in range(3)], out_specs=_res_spec(r, ATTN_W),
        out_shape=SDS((B, r, SEQ // r, ATTN_W), bf16), scratch_shapes=[pltpu.VMEM((3 * RCH, RT, LANES), f32)],
        name=f"qkv_to_residues_g{g}", compiler_params=_cp(("parallel", "parallel")))(proj3, proj3, proj3)


def _attn_mix(o0, l0, o_res, l_res):
    B = o0.shape[0]

    def body(o0_ref, l0_ref, o1_ref, o2_ref, l1_ref, l2_ref, y_ref, lt_ref, lt1_ref, lt2_ref, so, sl):
        for gi, (o_ref, l_ref, r) in enumerate(((o1_ref, l1_ref, DILS[0]), (o2_ref, l2_ref, DILS[1]))):
            _from_residues(so.at[gi], o_ref, r, GW)
            _from_residues(sl.at[gi:gi + 1], l_ref, r, LANES)
        ls = [l0_ref[...], sl[0], sl[1]]
        m = functools.reduce(jnp.maximum, ls)
        ws = [jnp.exp(l - m) for l in ls]
        den = ws[0] + ws[1] + ws[2]
        alphas = [w / den for w in ws]
        lt = m + jnp.log(den)
        lt_ref[...] = lt
        sl[2] = lt
        _to_residues(lt1_ref, sl.at[2:3], DILS[0], LANES)
        _to_residues(lt2_ref, sl.at[2:3], DILS[1], LANES)
        for h in range(HEADS):
            cols = slice(h * HEAD_DIM, (h + 1) * HEAD_DIM)
            jj, lo = divmod(h * HEAD_DIM, LANES)
            acc = alphas[0][:, h:h + 1] * o0_ref[:, cols].astype(f32)
            for gi in range(2):
                acc = acc + alphas[gi + 1][:, h:h + 1] * so[gi, jj, :, lo:lo + HEAD_DIM]
            y_ref[:, cols] = acc.astype(bf16)

    in_specs = [_tok_spec(GW), _tok_spec(LANES), _res_spec(DILS[0], GW), _res_spec(DILS[1], GW), _res_spec(DILS[0], LANES), _res_spec(DILS[1], LANES)]
    out_specs = [_tok_spec(GW), _tok_spec(LANES), _res_spec(DILS[0], LANES), _res_spec(DILS[1], LANES)]
    return pl.pallas_call(
        body, grid=(B, SEQ // RT), in_specs=in_specs, out_specs=out_specs,
        out_shape=(SDS((B, SEQ, GW), bf16), SDS((B, SEQ, LANES), f32)) + tuple(SDS((B, r, SEQ // r, LANES), f32) for r in DILS),
        scratch_shapes=[pltpu.VMEM((2, RCH, RT, LANES), f32), pltpu.VMEM((3, RT, LANES), f32)],
        name="attn_mix", compiler_params=_cp(("parallel", "parallel")))(o0, l0, *o_res, *l_res)


def _attn_delta(dmix, mix):
    B = dmix.shape[0]

    def body(d_ref, y_ref, dl_ref, dl1_ref, dl2_ref, dm1_ref, dm2_ref, sx, sd):
        lane = lax.broadcasted_iota(jnp.int32, (RT, LANES), 1)
        acc = jnp.zeros((RT, LANES), f32)
        dv = d_ref[...].astype(f32)
        for jj in range(RCH):
            sx[jj] = dv[:, jj * LANES:(jj + 1) * LANES]
        for h in range(HEADS):
            cols = slice(h * HEAD_DIM, (h + 1) * HEAD_DIM)
            dl = jnp.sum(dv[:, cols] * y_ref[:, cols].astype(f32), axis=-1, keepdims=True)
            acc = jnp.where(lane == h, dl, acc)
        dl_ref[...] = acc
        sd[0] = acc
        _to_residues(dl1_ref, sd, DILS[0], LANES)
        _to_residues(dl2_ref, sd, DILS[1], LANES)
        _to_residues(dm1_ref, sx, DILS[0], GW)
        _to_residues(dm2_ref, sx, DILS[1], GW)

    return pl.pallas_call(
        body, grid=(B, SEQ // RT), in_specs=[_tok_spec(GW), _tok_spec(GW)],
        out_specs=[_tok_spec(LANES), _res_spec(DILS[0], LANES), _res_spec(DILS[1], LANES), _res_spec(DILS[0], GW), _res_spec(DILS[1], GW)],
        out_shape=(SDS((B, SEQ, LANES), f32),) + tuple(SDS((B, r, SEQ // r, LANES), f32) for r in DILS)
        + tuple(SDS((B, r, SEQ // r, GW), bf16) for r in DILS),
        scratch_shapes=[pltpu.VMEM((RCH, RT, LANES), f32), pltpu.VMEM((1, RT, LANES), f32)],
        name="attn_delta", compiler_params=_cp(("parallel", "parallel")))(dmix, mix)


N_CB = IN_W // CB


def _assemble_dproj(dqkv, dua, dub, dga, dgc):
    B = dua.shape[0]
    ng = len(GROUPS)
    flat = [dqkv[g][p] for p in range(3) for g in range(ng)]
    wide = [dua, dub, dga, dgc]

    def body(*refs):
        srcs, wides, o_ref, scr = refs[:3 * ng], refs[3 * ng:3 * ng + 4], refs[3 * ng + 4], refs[3 * ng + 5]
        for jv in range(3 * ng):
            g = jv % ng
            if g == 0:
                o_ref[:, jv * GW:(jv + 1) * GW] = srcs[jv][...]
            else:
                _from_residues(scr, srcs[jv], GROUPS[g][1], GW)
                for jj in range(RCH):
                    o_ref[:, jv * GW + jj * LANES:jv * GW + (jj + 1) * LANES] = scr[jj].astype(bf16)
        for wv in range(4):
            lo = 3 * ATTN_W + wv * D_MODEL
            o_ref[:, lo:lo + D_MODEL] = wides[wv][...]

    in_specs = [_tok_spec(GW) if (jv % ng) == 0 else _res_spec(GROUPS[jv % ng][1], GW) for jv in range(3 * ng)]
    in_specs += [_tok_spec(D_MODEL)] * 4
    return pl.pallas_call(
        body, grid=(B, SEQ // RT), in_specs=in_specs, out_specs=_tok_spec(IN_W),
        out_shape=SDS((B, SEQ, IN_W), bf16), scratch_shapes=[pltpu.VMEM((RCH, RT, LANES), f32)],
        name="assemble_dproj", compiler_params=_cp(("parallel", "parallel")))(*flat, *wide)


def _same_shape_groups(names, arrays):
    groups = {}
    for n, a in zip(names, arrays):
        groups.setdefault(a.shape, ([], []))
        groups[a.shape][0].append(n)
        groups[a.shape][1].append(a)
    return list(groups.values())


def _add_pairs(parts, gots, core, name):
    k = len(parts)
    n, h, C = gots[0].shape

    def body(c_ref, *refs):
        for i in range(k):
            refs[2 * k + i][...] = (refs[i][...].astype(f32) + refs[k + i][...].astype(f32)).astype(bf16)

    blk = pl.BlockSpec((None, h, C), lambda s, c_ref: (s, 0, 0))
    own = pl.BlockSpec((None, h, C), lambda s, c_ref: (s, c_ref[0], 0))
    spec = pltpu.PrefetchScalarGridSpec(num_scalar_prefetch=1, grid=(n,), in_specs=[own] * k + [blk] * k, out_specs=[blk] * k)
    return pl.pallas_call(body, grid_spec=spec, out_shape=tuple(SDS((n, h, C), bf16) for _ in range(k)), name=name,
                          compiler_params=_cp(("parallel",)))(core, *parts, *gots)


def _sum4(sums, gots, chip, core, name):
    k = len(sums)
    _, h, C = sums[0].shape

    def body(s_ref, c_ref, *refs):
        for i in range(k):
            q_ref = refs[k + i]
            t = refs[i][...].astype(f32) + q_ref[0].astype(f32)
            t = t + q_ref[1].astype(f32)
            refs[2 * k + i][...] = t + q_ref[2].astype(f32)

    spec = pltpu.PrefetchScalarGridSpec(
        num_scalar_prefetch=2, grid=(1,),
        in_specs=[pl.BlockSpec((None, h, C), lambda i, s_ref, c_ref: (s_ref[0], 0, 0))] * k
        + [pl.BlockSpec((N_CHIPS - 1, h, C), lambda i, s_ref, c_ref: (0, 0, 0))] * k,
        out_specs=[pl.BlockSpec((h, C), lambda i, s_ref, c_ref: (c_ref[0], 0))] * k)
    return pl.pallas_call(body, grid_spec=spec, out_shape=tuple(SDS((2 * h, C), f32) for _ in range(k)), name=name,
                          compiler_params=_cp(("arbitrary",)))(chip, core, *sums, *gots)


def _adamw(ws, gs, ms, vs, name):
    k = len(ws)
    R, C = ws[0].shape
    rt = R
    for cand in (512, 256, 128, 64, 32, 16, 8):
        if R % cand == 0 and cand * C * 4 <= 2 ** 21:
            rt = cand
            break
    c1 = 1.0 / (1.0 - B1 ** STEP)
    c2 = 1.0 / (1.0 - B2 ** STEP)

    def body(*refs):
        for i in range(k):
            w_ref, g_ref, m_ref, v_ref = (refs[j * k + i] for j in range(4))
            d_ref, nm_ref, nv_ref = (refs[(4 + j) * k + i] for j in range(3))
            gv = g_ref[...]
            nm = B1 * m_ref[...] + (1.0 - B1) * gv
            nv = B2 * v_ref[...] + (1.0 - B2) * (gv * gv)
            nm_ref[...] = nm
            nv_ref[...] = nv
            d_ref[...] = -LR * ((nm * c1) / (jnp.sqrt(nv * c2) + ADAM_EPS) + WD * w_ref[...])

    blk = pl.BlockSpec((rt, C), lambda i: (i, 0))
    sd = SDS((R, C), f32)
    res = pl.pallas_call(body, grid=(R // rt,), in_specs=[blk] * (4 * k), out_specs=[blk] * (3 * k), out_shape=(sd,) * (3 * k),
                         name=name, compiler_params=_cp(("parallel",)))(*ws, *gs, *ms, *vs)
    return res[:k], res[k:2 * k], res[2 * k:]


def _coords():
    return lax.axis_index("x"), lax.axis_index("y"), lax.axis_index("c")


def _other_chips(x, y):
    return [(1 - x, y), (x, 1 - y), (1 - x, 1 - y)]


def _allgather_weights(shards, slots):
    n = len(shards)
    n_slot = 12

    def body(*refs):
        ins, outs = refs[:n], refs[2 * n:3 * n]
        send, recv = refs[3 * n:]
        x, y, c = _coords()
        me, cx, cy, cd = 2 * x + y, 2 * (1 - x) + y, 2 * x + (1 - y), 2 * (1 - x) + (1 - y)
        dev_x, dev_y, dev_s = (1 - x, y, c), (x, 1 - y, c), (x, y, 1 - c)
        started = []

        def copy(w, slot, src, dst, dev):
            return pltpu.make_async_remote_copy(src_ref=src, dst_ref=dst, send_sem=send.at[w, slot], recv_sem=recv.at[w, slot],
                                                device_id=dev, device_id_type=MESH)

        def go(cp):
            cp.start()
            started.append(cp)

        for w in range(n):
            q = shards[w].shape[0] // 4
            rows = [pl.ds(c * 2 * q + k * q, q) for k in range(2)]
            theirs = [pl.ds((1 - c) * 2 * q + k * q, q) for k in range(2)]
            own = [(ins[w].at[r, :], outs[w].at[me, r, :]) for r in rows]
            go(copy(w, 0, *own[0], dev_x))
            go(copy(w, 2, *own[1], dev_y))
            go(copy(w, 1, *own[1], dev_x))
            go(copy(w, 3, *own[0], dev_y))
            arrivals = [(0, cx, 0, (4, dev_y)), (2, cy, 1, (5, dev_x)), (1, cx, 1, None), (3, cy, 0, None), (4, cd, 0, None), (5, cd, 1, None)]
            for k, (slot, chip, quarter, onward) in enumerate(arrivals):
                blk = outs[w].at[chip, rows[quarter], :]
                copy(w, slot, blk, blk, dev_s).wait_recv()
                if onward is not None:
                    go(copy(w, onward[0], blk, blk, onward[1]))
                go(copy(w, 6 + k, blk, blk, dev_s))
            for k, (slot, chip, quarter, onward) in enumerate(arrivals):
                blk = outs[w].at[chip, theirs[quarter], :]
                copy(w, 6 + k, blk, blk, dev_s).wait_recv()
        for cp in started:
            cp.wait_send()

    return pl.pallas_call(
        body, in_specs=[HBM] * (2 * n), out_specs=[HBM] * n,
        out_shape=tuple(SDS((N_CHIPS,) + s.shape, s.dtype) for s in shards),
        input_output_aliases={n + w: w for w in range(n)},
        scratch_shapes=[pltpu.SemaphoreType.DMA((n, n_slot))] * 2,
        name="allgather_weights", compiler_params=pltpu.CompilerParams(has_side_effects=True))(*shards, *slots)


def _x_pair_exchange(parts):
    n = len(parts)
    halves = [p.shape[1] // 2 for p in parts]

    def copies(ins, outs, sems):
        send, recv = sems
        x, y, c = _coords()
        return [pltpu.make_async_remote_copy(src_ref=ins[w].at[:, pl.ds((1 - c) * halves[w], halves[w]), :], dst_ref=outs[w],
                                             send_sem=send.at[w], recv_sem=recv.at[w], device_id=(x, y, 1 - c), device_id_type=MESH)
                for w in range(n)]

    def start(ins, outs, sems):
        for cp in copies(ins, outs, sems):
            cp.start()

    def wait(ins, outs, sems):
        for cp in copies(ins, outs, sems):
            cp.wait()

    return _Exchange(ins=list(parts), out_shape=[SDS((N_CHIPS, p.shape[1] // 2, p.shape[2]), p.dtype) for p in parts],
                     scratch=[pltpu.SemaphoreType.DMA((n,))] * 2, aliases={}, start=start, wait=wait)


def _x_chip_exchange(sums):
    n = len(sums)

    def copies(ins, outs, sems):
        send, recv = sems
        x, y, c = _coords()
        return [pltpu.make_async_remote_copy(src_ref=ins[w].at[2 * px + py], dst_ref=outs[w].at[j], send_sem=send.at[w, j],
                                             recv_sem=recv.at[w, j], device_id=(px, py, c), device_id_type=MESH)
                for w in range(n) for j, (px, py) in enumerate(_other_chips(x, y))]

    def start(ins, outs, sems):
        for cp in copies(ins, outs, sems):
            cp.start()

    def wait(ins, outs, sems):
        for cp in copies(ins, outs, sems):
            cp.wait()

    return _Exchange(ins=list(sums), out_shape=[SDS((N_CHIPS - 1,) + s.shape[1:], s.dtype) for s in sums],
                     scratch=[pltpu.SemaphoreType.DMA((n, 3)), pltpu.SemaphoreType.DMA((n, 3))], aliases={}, start=start, wait=wait)


def _x_gather_ici(shards, slots):
    n = len(shards)
    halves = [s.shape[0] // 2 for s in shards]

    def copies(ins, outs, sems):
        send, recv = sems
        x, y, c = _coords()
        me = 2 * x + y
        out = []
        for w in range(n):
            mine = pl.ds(c * halves[w], halves[w])
            for j, (px, py) in enumerate(_other_chips(x, y)):
                snd = pltpu.make_async_remote_copy(src_ref=ins[w].at[mine, :], dst_ref=outs[w].at[me, mine, :], send_sem=send.at[w, j],
                                                   recv_sem=recv.at[w, j], device_id=(px, py, c), device_id_type=MESH)
                got = outs[w].at[2 * px + py, mine, :]
                rcv = pltpu.make_async_remote_copy(src_ref=got, dst_ref=got, send_sem=send.at[w, j], recv_sem=recv.at[w, j],
                                                   device_id=(px, py, c), device_id_type=MESH)
                out.append((snd, rcv))
        return out

    def start(ins, outs, sems):
        for snd, _ in copies(ins, outs, sems):
            snd.start()

    def wait(ins, outs, sems):
        for snd, rcv in copies(ins, outs, sems):
            rcv.wait_recv()
            snd.wait_send()

    return _Exchange(ins=list(shards) + list(slots), out_shape=[SDS(s.shape, s.dtype) for s in slots],
                     scratch=[pltpu.SemaphoreType.DMA((n, 3)), pltpu.SemaphoreType.DMA((n, 3))],
                     aliases={n + w: w for w in range(n)}, start=start, wait=wait)


def _x_gather_fwd(bufs):
    n = len(bufs)
    halves = [b.shape[1] // 2 for b in bufs]

    def copies(ins, outs, sems):
        send, recv = sems
        x, y, c = _coords()
        out = []
        for w in range(n):
            for j, (px, py) in enumerate(_other_chips(x, y)):
                mine = outs[w].at[2 * px + py, pl.ds(c * halves[w], halves[w]), :]
                theirs = outs[w].at[2 * px + py, pl.ds((1 - c) * halves[w], halves[w]), :]
                snd = pltpu.make_async_remote_copy(src_ref=mine, dst_ref=mine, send_sem=send.at[w, j], recv_sem=recv.at[w, j],
                                                   device_id=(x, y, 1 - c), device_id_type=MESH)
                rcv = pltpu.make_async_remote_copy(src_ref=theirs, dst_ref=theirs, send_sem=send.at[w, j], recv_sem=recv.at[w, j],
                                                   device_id=(x, y, 1 - c), device_id_type=MESH)
                out.append((snd, rcv))
        return out

    def start(ins, outs, sems):
        for snd, _ in copies(ins, outs, sems):
            snd.start()

    def wait(ins, outs, sems):
        for snd, rcv in copies(ins, outs, sems):
            rcv.wait_recv()
            snd.wait_send()

    return _Exchange(ins=list(bufs), out_shape=[SDS(b.shape, b.dtype) for b in bufs],
                     scratch=[pltpu.SemaphoreType.DMA((n, 3)), pltpu.SemaphoreType.DMA((n, 3))],
                     aliases={w: w for w in range(n)}, start=start, wait=wait)


def _final_exchange(bufs, vec):
    n = len(bufs)
    R = vec.shape[0]
    nd = 8

    def body(*refs):
        v_ref = refs[n]
        outs = refs[n + 1:2 * n + 1]
        o_ref, sbuf, send, recv, ssend, srecv = refs[2 * n + 1:]
        x, y, c = _coords()
        me = 4 * x + 2 * y + c
        sbuf[me] = v_ref[...]
        small = []
        for k in range(1, nd):
            kx, ky, kc = (k >> 2) & 1, (k >> 1) & 1, k & 1
            tx = x + kx - 2 * x * kx
            ty = y + ky - 2 * y * ky
            tc = c + kc - 2 * c * kc
            cp = pltpu.make_async_remote_copy(src_ref=v_ref, dst_ref=sbuf.at[me], send_sem=ssend.at[k], recv_sem=srecv.at[k],
                                              device_id=(tx, ty, tc), device_id_type=MESH)
            cp.start()
            small.append((cp, 4 * tx + 2 * ty + tc))
        cps = []
        for w in range(n):
            h = bufs[w].shape[0] // 2
            rows = outs[w].at[pl.ds(c * h, h), :]
            rc = pltpu.make_async_remote_copy(src_ref=rows, dst_ref=rows, send_sem=send.at[w], recv_sem=recv.at[w],
                                              device_id=(x, y, 1 - c), device_id_type=MESH)
            rc.start()
            cps.append(rc)
        for k, (cp, src) in zip(range(1, nd), small):
            pltpu.make_async_remote_copy(src_ref=v_ref, dst_ref=sbuf.at[src], send_sem=ssend.at[k], recv_sem=srecv.at[k],
                                         device_id=(x, y, c), device_id_type=MESH).wait_recv()
        for cp, _ in small:
            cp.wait_send()
        acc = sbuf[0]
        for d in range(1, nd):
            acc = acc + sbuf[d]
        o_ref[...] = acc
        for w, rc in enumerate(cps):
            h = bufs[w].shape[0] // 2
            other = outs[w].at[pl.ds((1 - c) * h, h), :]
            pltpu.make_async_remote_copy(src_ref=other, dst_ref=other, send_sem=send.at[w], recv_sem=recv.at[w],
                                         device_id=(x, y, 1 - c), device_id_type=MESH).wait_recv()
            rc.wait_send()

    vm = pl.BlockSpec(memory_space=pltpu.VMEM)
    res = pl.pallas_call(body, in_specs=[HBM] * n + [vm], out_specs=[HBM] * n + [vm],
                         out_shape=tuple(SDS(a.shape, a.dtype) for a in bufs) + (SDS((R, LANES), f32),),
                         input_output_aliases={w: w for w in range(n)},
                         scratch_shapes=[pltpu.VMEM((nd, R, LANES), f32)] + [pltpu.SemaphoreType.DMA((n,))] * 2
                         + [pltpu.SemaphoreType.DMA((nd,))] * 2, name="final_exchange",
                         compiler_params=pltpu.CompilerParams(has_side_effects=True))(*bufs, vec)
    return res[:n], res[n]


def _rows128(a, rows):
    flat = a.reshape(-1)
    return jnp.pad(flat, (0, rows * LANES - flat.shape[0])).reshape(rows, LANES)


GATHER_1 = ("w_conv_out", "w_attn_out", "w_o", "w_ffn_gate")
GATHER_2 = ("w_ffn_up", "w_ffn_down")
REDUCE_A = ("w_ffn_gate", "w_ffn_up", "w_ffn_down")
REDUCE_B = ("w_conv_out", "w_attn_out", "w_o")
REDUCE_C = ("w_in",)


def _step(x, target, norm1_g, gate_b, conv_w, conv_b, conv_ln_g, conv_ln_b, norm2_g, norm_f_g, w_in, shards, slots, chip1, core):
    B = x.shape[0]
    T = B * SEQ
    xf = x.reshape(T, D_MODEL)
    tf = target.reshape(T, D_MODEL)

    h, h_t = _rms_fwd(xf, norm1_g, "rms1_fwd")
    proj, got1 = _mm("in_proj", h, w_in, grid=(N_CHIPS, T // 1024),
                     a_spec=pl.BlockSpec((1024, D_MODEL), lambda s, m: (m, 0)),
                     b_spec=pl.BlockSpec((None, D_MODEL, IN_S), lambda s, m: (s, 0, 0)),
                     o_spec=pl.BlockSpec((1024, IN_S), lambda s, m: (m, s)), o_shape=(T, IN_W), o_dtype=bf16, dims=NN,
                     comm=_x_gather_ici([shards[n] for n in GATHER_1], [slots[n] for n in GATHER_1]))
    proj3 = proj.reshape(B, SEQ, IN_W)
    proj4 = proj.reshape(B, 1, SEQ, IN_W)

    qkv = [None] + [_qkv_to_residues(proj3, g) for g in range(1, len(GROUPS))]

    def qkv_args(g):
        return (proj4, proj4, proj4, 0, 3, 6) if g == 0 else (qkv[g], qkv[g], qkv[g], 0, 1, 2)

    (o4_0, l4_0), full1 = _attn_fwd(0, *qkv_args(0), comm=_x_gather_fwd(list(got1)))
    o4_1, l4_1 = _attn_fwd(1, *qkv_args(1))
    o4_2, l4_2 = _attn_fwd(2, *qkv_args(2))
    full = dict(zip(GATHER_1, full1))
    w_conv_out, w_attn_out, w_o, w_gate = (full[n] for n in GATHER_1)
    w_conv_out_f = w_conv_out.reshape(D_MODEL, D_MODEL)
    w_o_f = w_o.reshape(D_MODEL, D_MODEL)
    mix3, lse3, lse_r1, lse_r2 = _attn_mix(o4_0.reshape(B, SEQ, GW), l4_0.reshape(B, SEQ, LANES), [o4_1, o4_2], [l4_1, l4_2])
    mix = mix3.reshape(T, GW)
    y_attn = _mm("attn_out", mix, w_attn_out, grid=(N_CHIPS, T // 512),
                 a_spec=pl.BlockSpec((512, GW), lambda s, m: (m, 0)),
                 b_spec=pl.BlockSpec((None, GW, D_MODEL // N_CHIPS), lambda s, m: (s, 0, 0)),
                 o_spec=pl.BlockSpec((512, D_MODEL // N_CHIPS), lambda s, m: (m, s)), o_shape=(T, D_MODEL), o_dtype=bf16,
                 dims=NN, sem=("parallel", "parallel"))

    c1, got2 = _glu_conv_fwd(proj3, conv_w, conv_b, comm=_x_gather_ici([shards[n] for n in GATHER_2], [slots[n] for n in GATHER_2]))
    c1 = c1.reshape(T, D_MODEL)
    c3 = _ln_silu_fwd(c1, conv_ln_g, conv_ln_b)
    y_conv, (w_up, w_down) = _mm_nn_full("conv_out", c3, w_conv_out_f, bf16, comm=_x_gather_fwd(list(got2)))

    merged = _merge_fwd(proj, gate_b, y_attn, y_conv)
    x1, h2 = _o_proj_rms(merged, w_o_f, xf, norm2_g)
    fa, fb, ff = _ffn_up(h2, w_gate, w_up)
    loss, dx2, d_gf = _ffn_down_loss(ff, w_down, x1, norm_f_g.reshape(1, D_MODEL), tf)

    d_w_down = _mm("d_w_down", ff, dx2, grid=(N_CHIPS, T // 1024),
                   a_spec=pl.BlockSpec((None, 1024, FF_S), lambda s, k: (s, k, 0)),
                   b_spec=pl.BlockSpec((1024, D_MODEL), lambda s, k: (k, 0)),
                   o_spec=pl.BlockSpec((None, FF_S, D_MODEL), lambda s, k: (s, 0, 0)), o_shape=(N_CHIPS, FF_S, D_MODEL),
                   o_dtype=bf16, dims=TN, acc_shape=(FF_S, D_MODEL), k_axis=1, sem=("parallel", "arbitrary"))
    da, db = _ffn_down_bwd(dx2, w_down, fa, fb)

    def d_w_ff(name, dz):
        return _mm(name, dz, h2, grid=(N_CHIPS, T // 2048),
                   a_spec=pl.BlockSpec((None, 2048, FF_S), lambda s, k: (s, k, 0)),
                   b_spec=pl.BlockSpec((2048, D_MODEL), lambda s, k: (k, 0)),
                   o_spec=pl.BlockSpec((None, FF_S, D_MODEL), lambda s, k: (s, 0, 0)), o_shape=(N_CHIPS, FF_S, D_MODEL),
                   o_dtype=bf16, dims=TN, acc_shape=(FF_S, D_MODEL), k_axis=1, sem=("parallel", "arbitrary"))

    d_w_gate = d_w_ff("d_w_gate", da)
    d_w_up = d_w_ff("d_w_up", db)
    part = dict(w_ffn_gate=d_w_gate, w_ffn_up=d_w_up, w_ffn_down=d_w_down)

    def pair_sums(names, got):
        out = {}
        for ns, gs in _same_shape_groups(names, got):
            out.update(zip(ns, _add_pairs([part[n] for n in ns], gs, core, "pair_sum_" + ns[0])))
        return [out[n] for n in names]

    (dx1, d_g2), got = _mm_rms_bwd("ffn_dh2", [(da, w_gate), (db, w_up)], x1, norm2_g, dx2, tm=1024,
                                   a_spec=pl.BlockSpec((None, 1024, FF_S), lambda m, s: (s, m, 0)),
                                   b_spec=pl.BlockSpec((None, FF_S, D_MODEL), lambda m, s: (s, 0, 0)), dims=NN,
                                   comm=_x_pair_exchange([part[n] for n in REDUCE_A]), vmem=VMEM_LIMIT_BIG)
    sums_a = pair_sums(REDUCE_A, got)

    d_w_o = _mm_tn_tokens("d_w_o", merged, dx1, bf16).reshape(N_CHIPS, D_MODEL // N_CHIPS, D_MODEL)
    dmerged = _mm_nt_full("d_merged", dx1, w_o_f, bf16)
    dya, dyc, dga, dgc, d_gba, d_gbc = _merge_bwd(dmerged, proj, gate_b, y_attn, y_conv)

    d_w_conv_out = _mm_tn_tokens("d_w_conv_out", c3, dyc, bf16).reshape(N_CHIPS, D_MODEL // N_CHIPS, D_MODEL)
    dc1, d_ln_g, d_ln_b = _conv_out_bwd(dyc, w_conv_out_f, c1, conv_ln_g, conv_ln_b)
    (dua, dub, d_conv_w, d_conv_b), got_a = _glu_conv_bwd(dc1.reshape(B, SEQ, D_MODEL), proj3, conv_w, comm=_x_chip_exchange(sums_a))

    d_w_attn_out = _mm("d_w_attn_out", mix, dya, grid=(N_CHIPS, T // 512),
                       a_spec=pl.BlockSpec((512, GW), lambda s, k: (k, 0)),
                       b_spec=pl.BlockSpec((512, D_MODEL // N_CHIPS), lambda s, k: (k, s)),
                       o_spec=pl.BlockSpec((None, GW, D_MODEL // N_CHIPS), lambda s, k: (s, 0, 0)),
                       o_shape=(N_CHIPS, GW, D_MODEL // N_CHIPS), o_dtype=bf16, dims=TN, acc_shape=(GW, D_MODEL // N_CHIPS),
                       k_axis=1, sem=("parallel", "arbitrary"))
    part.update(w_conv_out=d_w_conv_out, w_attn_out=d_w_attn_out, w_o=d_w_o)
    dmix, got = _mm("d_mix", dya, w_attn_out, grid=(T // 1024, N_CHIPS),
                    a_spec=pl.BlockSpec((1024, D_MODEL // N_CHIPS), lambda m, s: (m, s)),
                    b_spec=pl.BlockSpec((None, GW, D_MODEL // N_CHIPS), lambda m, s: (s, 0, 0)),
                    o_spec=pl.BlockSpec((1024, GW), lambda m, s: (m, 0)), o_shape=(T, GW), o_dtype=bf16, dims=NT,
                    acc_shape=(1024, GW), k_axis=1, comm=_x_pair_exchange([part[n] for n in REDUCE_B]))
    sums_b = pair_sums(REDUCE_B, got)
    dmix3 = dmix.reshape(B, SEQ, GW)
    delta3, delta_r1, delta_r2, dmix_r1, dmix_r2 = _attn_delta(dmix3, mix3)
    one = (B, 1, SEQ)
    (dq0, dk0, dv0), got_b = _attn_bwd(0, *qkv_args(0), dmix3.reshape(one + (GW,)), lse3.reshape(one + (LANES,)),
                                       delta3.reshape(one + (LANES,)), comm=_x_chip_exchange(sums_b))
    dqkv = [tuple(t.reshape(B, SEQ, GW) for t in (dq0, dk0, dv0)),
            _attn_bwd(1, *qkv_args(1), dmix_r1, lse_r1, delta_r1), _attn_bwd(2, *qkv_args(2), dmix_r2, lse_r2, delta_r2)]
    dproj = _assemble_dproj(dqkv, dua, dub, dga.reshape(B, SEQ, D_MODEL), dgc.reshape(B, SEQ, D_MODEL)).reshape(T, IN_W)

    d_w_in = _mm("d_w_in", h_t, dproj, grid=(N_CHIPS, T // 1024),
                 a_spec=pl.BlockSpec((D_MODEL, 1024), lambda s, k: (0, k)),
                 b_spec=pl.BlockSpec((1024, IN_S), lambda s, k: (k, s)),
                 o_spec=pl.BlockSpec((None, D_MODEL, IN_S), lambda s, k: (s, 0, 0)), o_shape=(N_CHIPS, D_MODEL, IN_S),
                 o_dtype=bf16, dims=NN, acc_shape=(D_MODEL, IN_S), k_axis=1, sem=("parallel", "arbitrary"))
    part.update(w_in=d_w_in)
    sums_c = pair_sums(REDUCE_C, _run_exchange(_x_pair_exchange([d_w_in]), "grad_pair_exchange_c"))
    (dx, d_g1), got_c = _mm_rms_bwd("d_h", [(dproj, w_in)], xf, norm1_g, dx1, tm=512,
                                    a_spec=pl.BlockSpec((512, IN_S), lambda m, s: (m, s)),
                                    b_spec=pl.BlockSpec((None, D_MODEL, IN_S), lambda m, s: (s, 0, 0)), dims=NT,
                                    comm=_x_chip_exchange(sums_c))

    names = REDUCE_A + REDUCE_B + REDUCE_C
    sums = dict(zip(names, sums_a + sums_b + sums_c))
    halves = {}
    for ns, gs in _same_shape_groups(names, got_a + got_b + got_c):
        halves.update(zip(ns, _sum4([sums[n] for n in ns], gs, chip1, core, "chip_sum_" + ns[0])))
    big = {n: halves[n] for n in names}
    small = dict(norm1_g=d_g1, gate_b=jnp.concatenate([d_gba, d_gbc], axis=-1), conv_b=d_conv_b, conv_ln_g=d_ln_g,
                 conv_ln_b=d_ln_b, norm2_g=d_g2, norm_f_g=d_gf, conv_w=d_conv_w)
    return loss, dx.reshape(B, SEQ, D_MODEL), big, small


BIG = ("w_in", "w_conv_out", "w_attn_out", "w_o", "w_ffn_gate", "w_ffn_up", "w_ffn_down")
TRANSPOSED = ("w_ffn_gate", "w_ffn_up")
SMALL = ("norm1_g", "gate_b", "conv_b", "conv_ln_g", "conv_ln_b", "norm2_g", "norm_f_g")
SMALL_ROWS = {"norm1_g": 8, "gate_b": 16, "conv_b": 8, "conv_ln_g": 8, "conv_ln_b": 8, "norm2_g": 8, "norm_f_g": 8}
LOSS_ROWS = 8
CONVW_ROWS = 32 * D_MODEL // LANES


def kernel(x, norm1_g, w_in, gate_b, conv_w, conv_b, conv_ln_g, conv_ln_b, w_conv_out, w_attn_out, w_o, norm2_g, w_ffn_gate, w_ffn_up, w_ffn_down, norm_f_g, loss_target, m_norm1_g, m_w_in, m_gate_b, m_conv_w, m_conv_b, m_conv_ln_g, m_conv_ln_b, m_w_conv_out, m_w_attn_out, m_w_o, m_norm2_g, m_w_ffn_gate, m_w_ffn_up, m_w_ffn_down, m_norm_f_g, v_norm1_g, v_w_in, v_gate_b, v_conv_w, v_conv_b, v_conv_ln_g, v_conv_ln_b, v_w_conv_out, v_w_attn_out, v_w_o, v_norm2_g, v_w_ffn_gate, v_w_ffn_up, v_w_ffn_down, v_norm_f_g):
    W = dict(norm1_g=norm1_g, w_in=w_in, gate_b=gate_b, conv_w=conv_w, conv_b=conv_b, conv_ln_g=conv_ln_g, conv_ln_b=conv_ln_b,
             w_conv_out=w_conv_out, w_attn_out=w_attn_out, w_o=w_o, norm2_g=norm2_g, w_ffn_gate=w_ffn_gate, w_ffn_up=w_ffn_up,
             w_ffn_down=w_ffn_down, norm_f_g=norm_f_g)
    M = dict(norm1_g=m_norm1_g, w_in=m_w_in, gate_b=m_gate_b, conv_w=m_conv_w, conv_b=m_conv_b, conv_ln_g=m_conv_ln_g,
             conv_ln_b=m_conv_ln_b, w_conv_out=m_w_conv_out, w_attn_out=m_w_attn_out, w_o=m_w_o, norm2_g=m_norm2_g,
             w_ffn_gate=m_w_ffn_gate, w_ffn_up=m_w_ffn_up, w_ffn_down=m_w_ffn_down, norm_f_g=m_norm_f_g)
    V = dict(norm1_g=v_norm1_g, w_in=v_w_in, gate_b=v_gate_b, conv_w=v_conv_w, conv_b=v_conv_b, conv_ln_g=v_conv_ln_g,
             conv_ln_b=v_conv_ln_b, w_conv_out=v_w_conv_out, w_attn_out=v_w_attn_out, w_o=v_w_o, norm2_g=v_norm2_g,
             w_ffn_gate=v_w_ffn_gate, w_ffn_up=v_w_ffn_up, w_ffn_down=v_w_ffn_down, norm_f_g=v_norm_f_g)
    order = list(W)

    def as2d(n, a):
        a = a.reshape(a.shape[-2:])
        return a.T if n in TRANSPOSED else a

    def from2d(n, a):
        return (a.T if n in TRANSPOSED else a).reshape(W[n].shape)

    shard2d = {n: as2d(n, W[n]) for n in BIG}
    chip = 2 * lax.axis_index("x") + lax.axis_index("y")

    core = lax.axis_index("c").astype(jnp.int32).reshape(1)
    chip1 = chip.astype(jnp.int32).reshape(1)
    shards = {n: shard2d[n].astype(bf16) for n in BIG}

    def slot_for(s):
        return lax.dynamic_update_slice(lax.empty((N_CHIPS,) + s.shape, s.dtype), s[None], (chip, 0, 0))

    slots = {n: slot_for(s) for n, s in shards.items()}
    cw = jnp.pad(conv_w.reshape(CONV_K, D_MODEL // N_CHIPS), ((0, CONV_HALO - CONV_K), (0, 0)))
    w_in_full, cw4 = _allgather_weights([shards["w_in"], cw], [slots["w_in"], slot_for(cw)])
    conv_w_full = cw4.transpose(1, 0, 2).reshape(CONV_HALO, D_MODEL)[:CONV_K]

    loss, grad_x, grads, gsmall = _step(x, loss_target, norm1_g, gate_b, conv_w_full, conv_b, conv_ln_g, conv_ln_b, norm2_g,
                                        norm_f_g, w_in_full, shards, slots, chip1, core)

    pieces = [_rows128(loss, LOSS_ROWS)] + [_rows128(gsmall[n], SMALL_ROWS[n]) for n in SMALL] + [_rows128(gsmall["conv_w"], CONVW_ROWS)]
    full_rows, tot = _final_exchange([grads[n] for n in BIG], jnp.concatenate(pieces, axis=0))
    grads = dict(zip(BIG, full_rows))
    loss_out = tot[0, 0]
    row = LOSS_ROWS
    for n in SMALL:
        grads[n] = tot[row: row + W[n].size // LANES].reshape(W[n].shape)
        row += SMALL_ROWS[n]
    dcw = tot[row: row + CONV_K * D_MODEL // LANES].reshape(CONV_K, D_MODEL)
    grads["conv_w"] = lax.dynamic_slice(dcw, (0, chip * (D_MODEL // N_CHIPS)), (CONV_K, D_MODEL // N_CHIPS))

    delta, new_m, new_v = {}, {}, {}
    for ns, ws in _same_shape_groups(BIG, [shard2d[n] for n in BIG]):
        ds, nms, nvs = _adamw(ws, [grads[n] for n in ns], [as2d(n, M[n]) for n in ns], [as2d(n, V[n]) for n in ns], "adamw_" + ns[0])
        for n, d, nm, nv in zip(ns, ds, nms, nvs):
            delta[n], new_m[n], new_v[n], grads[n] = (from2d(n, t) for t in (d, nm, nv, grads[n]))

    def pack(src):
        return jnp.concatenate([_rows128(src[n], SMALL_ROWS[n]) for n in SMALL], axis=0)

    (d,), (nm,), (nv,) = _adamw([pack(W)], [pack(grads)], [pack(M)], [pack(V)], "adamw_small")
    row = 0
    for n in SMALL:
        k = W[n].size // LANES
        delta[n], new_m[n], new_v[n] = (t[row: row + k].reshape(W[n].shape) for t in (d, nm, nv))
        row += SMALL_ROWS[n]

    def pad32(a):
        return jnp.pad(a.reshape(CONV_K, D_MODEL // N_CHIPS), ((0, 1), (0, 0)))

    (d,), (nm,), (nv,) = _adamw([pad32(conv_w)], [pad32(grads["conv_w"])], [pad32(m_conv_w)], [pad32(v_conv_w)], "adamw_conv_w")
    delta["conv_w"], new_m["conv_w"], new_v["conv_w"] = (t[:CONV_K].reshape(conv_w.shape) for t in (d, nm, nv))
    grads["conv_w"] = grads["conv_w"].reshape(conv_w.shape)

    return (loss_out, grad_x, *[grads[n] for n in order], *[delta[n] for n in order],
            *[new_m[n] for n in order], *[new_v[n] for n in order])
```

```python
import functools
import math
from typing import Callable, NamedTuple

import numpy as np
import jax
import jax.numpy as jnp
from jax import lax
from jax.experimental import pallas as pl
from jax.experimental.pallas import tpu as pltpu

f32 = jnp.float32
bf16 = jnp.bfloat16
SDS = jax.ShapeDtypeStruct
MESH = pl.DeviceIdType.MESH

D_MODEL = 1024
SEQ = 2048
HEAD_DIM = 64
HEADS = 8
GROUPS = ((128, 1), (512, 4), (2048, 16))
GW = HEADS * HEAD_DIM
ATTN_W = len(GROUPS) * GW
Q_BLOCK = 128
CONV_K = 31
D_FF = 2816
IN_W = 3 * ATTN_W + 2 * D_MODEL + 2 * D_MODEL
N_CHIPS = 4
IN_S = IN_W // N_CHIPS
FF_S = D_FF // N_CHIPS
RMS_EPS = 1e-6
LN_EPS = 1e-5
LR, B1, B2, ADAM_EPS, WD, STEP = 0.001, 0.9, 0.999, 1e-08, 0.01, 10
NEG = -1e30
LANES = 128
VMEM_LIMIT = 48 * 2 ** 20
VMEM_LIMIT_BIG = 52 * 2 ** 20
CB = 512
UA_CB, UB_CB, GA_CB, GC_CB = 9, 11, 13, 15


def _alibi_slope_list(n):
    def pow2(m):
        start = 2.0 ** (-8.0 / m)
        return [start ** (i + 1) for i in range(m)]
    if math.log2(n).is_integer():
        return pow2(n)
    c = 2 ** math.floor(math.log2(n))
    return pow2(c) + _alibi_slope_list(2 * c)[0::2][: n - c]


_SLOPES = np.asarray(sorted(_alibi_slope_list(len(GROUPS) * HEADS), reverse=True), dtype=np.float32).reshape(len(GROUPS), HEADS)


def _cp(sem=None, vmem=VMEM_LIMIT):
    return pltpu.CompilerParams(dimension_semantics=sem, vmem_limit_bytes=vmem)


def _sigmoid(x):
    return 1.0 / (1.0 + jnp.exp(-x))


HBM = pl.BlockSpec(memory_space=pl.ANY)


class _Exchange(NamedTuple):
    ins: list
    out_shape: list
    scratch: list
    aliases: dict
    start: Callable
    wait: Callable


def _launch(body, *, name, grid, in_specs, out_specs, out_shape, args, scratch_shapes=(), sem=None, comm=None, vmem=VMEM_LIMIT):
    if comm is None:
        return pl.pallas_call(body, grid=grid, in_specs=in_specs, out_specs=out_specs, out_shape=out_shape,
                              scratch_shapes=list(scratch_shapes), name=name, compiler_params=_cp(sem, vmem))(*args)
    multi = isinstance(out_shape, (tuple, list))
    m_out = list(out_shape) if multi else [out_shape]
    m_ospec = list(out_specs) if multi else [out_specs]
    n_in, n_out, n_scr = len(in_specs), len(m_out), len(scratch_shapes)
    nc_in, nc_out = len(comm.ins), len(comm.out_shape)

    def hosted(*refs):
        bounds = np.cumsum([0, n_in, nc_in, n_out, nc_out, n_scr])
        mi, ci, mo, co, ms = (refs[a:b] for a, b in zip(bounds[:-1], bounds[1:]))
        cs = refs[bounds[-1]:]
        ids = [pl.program_id(a) for a in range(len(grid))]
        first = functools.reduce(jnp.logical_and, [i == 0 for i in ids])
        last = functools.reduce(jnp.logical_and, [i == g - 1 for i, g in zip(ids, grid)])

        @pl.when(first)
        def _():
            comm.start(ci, co, cs)

        body(*mi, *mo, *ms)

        @pl.when(last)
        def _():
            comm.wait(ci, co, cs)

    res = pl.pallas_call(
        hosted, grid=grid, in_specs=list(in_specs) + [HBM] * nc_in, out_specs=m_ospec + [HBM] * nc_out,
        out_shape=tuple(m_out) + tuple(comm.out_shape), scratch_shapes=list(scratch_shapes) + list(comm.scratch),
        input_output_aliases={n_in + i: n_out + o for i, o in comm.aliases.items()}, name=name + "_comm",
        compiler_params=pltpu.CompilerParams(dimension_semantics=("arbitrary",) * len(grid), vmem_limit_bytes=vmem,
                                             has_side_effects=True))(*args, *comm.ins)
    return (tuple(res[:n_out]) if multi else res[0]), tuple(res[n_out:])


def _run_exchange(ex, name):
    n_in = len(ex.ins)

    def body(*refs):
        ins, outs, sems = refs[:n_in], refs[n_in:n_in + len(ex.out_shape)], refs[n_in + len(ex.out_shape):]
        ex.start(ins, outs, sems)
        ex.wait(ins, outs, sems)

    return pl.pallas_call(body, in_specs=[HBM] * n_in, out_specs=[HBM] * len(ex.out_shape), out_shape=tuple(ex.out_shape),
                          scratch_shapes=list(ex.scratch), input_output_aliases=dict(ex.aliases), name=name,
                          compiler_params=pltpu.CompilerParams(has_side_effects=True))(*ex.ins)


def _rmsnorm(xv, gv):
    return xv * lax.rsqrt(jnp.mean(xv * xv, axis=-1, keepdims=True) + RMS_EPS) * gv


def _rms_fwd(x, g, name):
    T = x.shape[0]
    tm = 512

    def body(x_ref, g_ref, o_ref, ot_ref):
        hv = _rmsnorm(x_ref[...], g_ref[...])
        o_ref[...] = hv.astype(bf16)
        ot_ref[...] = hv.T.astype(bf16)

    row = pl.BlockSpec((tm, D_MODEL), lambda i: (i, 0))
    vec = pl.BlockSpec((1, D_MODEL), lambda i: (0, 0))
    return pl.pallas_call(body, grid=(T // tm,), in_specs=[row, vec], out_specs=[row, pl.BlockSpec((D_MODEL, tm), lambda i: (0, i))],
                          out_shape=(SDS((T, D_MODEL), bf16), SDS((D_MODEL, T), bf16)), name=name,
                          compiler_params=_cp(("parallel",)))(x, g)


def _rms_bwd_tile(dyv, xv, gv, dres):
    r = lax.rsqrt(jnp.mean(xv * xv, axis=-1, keepdims=True) + RMS_EPS)
    xh = xv * r
    dxh = dyv * gv
    dx = dres + r * (dxh - xh * jnp.mean(dxh * xh, axis=-1, keepdims=True))
    return dx, jnp.sum(dyv * xh, axis=0, keepdims=True)


def _accumulate(first, refs_parts):
    @pl.when(first)
    def _():
        for ref, part in refs_parts:
            ref[...] = part

    @pl.when(jnp.logical_not(first))
    def _():
        for ref, part in refs_parts:
            ref[...] += part


def _mm_rms_bwd(name, ops, x, g, dres, *, tm, a_spec, b_spec, dims, comm=None, vmem=VMEM_LIMIT):
    T = x.shape[0]
    n = len(ops)

    def body(*refs):
        ab_refs = refs[:2 * n]
        x_ref, g_ref, r_ref, dx_ref, dg_ref, acc = refs[2 * n:]
        m, s = pl.program_id(0), pl.program_id(1)
        p = sum(lax.dot_general(ab_refs[2 * i][...], ab_refs[2 * i + 1][...], dims, preferred_element_type=f32) for i in range(n))

        @pl.when(s == 0)
        def _():
            acc[...] = p

        @pl.when(s > 0)
        def _():
            acc[...] += p

        @pl.when(s == N_CHIPS - 1)
        def _():
            dx, part = _rms_bwd_tile(acc[...], x_ref[...], g_ref[...], r_ref[...])
            dx_ref[...] = dx
            _accumulate(m == 0, [(dg_ref, part)])

    row = pl.BlockSpec((tm, D_MODEL), lambda m, s: (m, 0))
    vec = pl.BlockSpec((1, D_MODEL), lambda m, s: (0, 0))
    return _launch(body, name=name, grid=(T // tm, N_CHIPS), in_specs=[a_spec, b_spec] * n + [row, vec, row], out_specs=[row, vec],
                   out_shape=(SDS((T, D_MODEL), f32), SDS((1, D_MODEL), f32)), args=tuple(t for ab in ops for t in ab) + (x, g, dres),
                   scratch_shapes=[pltpu.VMEM((tm, D_MODEL), f32)], sem=("arbitrary", "arbitrary"), comm=comm, vmem=vmem)


def _ffn_down_loss(ff, wd, x1, gf, target):
    T = x1.shape[0]
    tm = 1024
    nm = T // tm

    def body(f_ref, w_ref, x_ref, g_ref, t_ref, loss_ref, dx_ref, dg_ref, acc):
        m, s = pl.program_id(0), pl.program_id(1)
        p = jnp.dot(f_ref[...], w_ref[...], preferred_element_type=f32)

        @pl.when(s == 0)
        def _():
            acc[...] = p

        @pl.when(s > 0)
        def _():
            acc[...] += p

        @pl.when(s == N_CHIPS - 1)
        def _():
            xv = acc[...] + x_ref[...]
            gv = g_ref[...]
            r = lax.rsqrt(jnp.mean(xv * xv, axis=-1, keepdims=True) + RMS_EPS)
            xh = xv * r
            e = xh * gv - t_ref[...]
            part_l = jnp.broadcast_to(0.5 * jnp.sum(jnp.mean(e * e, axis=-1, keepdims=True), axis=0, keepdims=True), (1, LANES))
            dy = e * (1.0 / D_MODEL)
            dxh = dy * gv
            dx_ref[...] = r * (dxh - xh * jnp.mean(dxh * xh, axis=-1, keepdims=True))
            part_g = jnp.sum(dy * xh, axis=0, keepdims=True)

            @pl.when(m == 0)
            def _():
                loss_ref[...] = part_l
                dg_ref[...] = part_g

            @pl.when(m > 0)
            def _():
                loss_ref[...] += part_l
                dg_ref[...] += part_g

    row = pl.BlockSpec((tm, D_MODEL), lambda m, s: (m, 0))
    vec = pl.BlockSpec((1, D_MODEL), lambda m, s: (0, 0))
    return pl.pallas_call(
        body, grid=(nm, N_CHIPS),
        in_specs=[pl.BlockSpec((None, tm, FF_S), lambda m, s: (s, m, 0)), pl.BlockSpec((None, FF_S, D_MODEL), lambda m, s: (s, 0, 0)),
                  row, vec, row],
        out_specs=[pl.BlockSpec((1, LANES), lambda m, s: (0, 0)), row, vec],
        out_shape=(SDS((1, LANES), f32), SDS((T, D_MODEL), f32), SDS((1, D_MODEL), f32)),
        scratch_shapes=[pltpu.VMEM((tm, D_MODEL), f32)], name="ffn_down_loss", compiler_params=_cp(("arbitrary", "arbitrary")))(
            ff, wd, x1, gf, target)


def _merge_o_proj_rms(proj, gate_b, ya, yc, w_o, x, g2):
    T = x.shape[0]
    tm = 512

    def body(ga0, ga1, gc0, gc1, b_ref, ya_ref, yc_ref, w_ref, x_ref, g_ref, m_ref, x1_ref, h2_ref):
        bv = b_ref[...]
        ga = jnp.concatenate([ga0[...], ga1[...]], axis=-1).astype(f32)
        gc = jnp.concatenate([gc0[...], gc1[...]], axis=-1).astype(f32)
        merged = (_sigmoid(ga + bv[:, :D_MODEL]) * ya_ref[...].astype(f32)
                  + _sigmoid(gc + bv[:, D_MODEL:]) * yc_ref[...].astype(f32)).astype(bf16)
        m_ref[...] = merged
        x1 = jnp.dot(merged, w_ref[...], preferred_element_type=f32) + x_ref[...]
        x1_ref[...] = x1
        h2_ref[...] = _rmsnorm(x1, g_ref[...]).astype(bf16)

    row = pl.BlockSpec((tm, D_MODEL), lambda i: (i, 0))
    return pl.pallas_call(
        body, grid=(T // tm,),
        in_specs=[pl.BlockSpec((tm, CB), lambda i, cb=cb: (i, cb)) for cb in (GA_CB, GA_CB + 1, GC_CB, GC_CB + 1)]
        + [pl.BlockSpec((1, 2 * D_MODEL), lambda i: (0, 0)), row, row, pl.BlockSpec((D_MODEL, D_MODEL), lambda i: (0, 0)), row,
           pl.BlockSpec((1, D_MODEL), lambda i: (0, 0))],
        out_specs=[row, row, row], out_shape=(SDS((T, D_MODEL), bf16), SDS((T, D_MODEL), f32), SDS((T, D_MODEL), bf16)),
        name="merge_o_proj_rms", compiler_params=_cp(("parallel",)))(proj, proj, proj, proj, gate_b, ya, yc, w_o, x, g2)


def _conv_out_fwd(c1, g, b, w, comm=None):
    T = c1.shape[0]
    tm = 1024

    def body(c_ref, g_ref, b_ref, w_ref, c3_ref, y_ref):
        cv = c_ref[...]
        mu = jnp.mean(cv, axis=-1, keepdims=True)
        cc = cv - mu
        var = jnp.mean(cc * cc, axis=-1, keepdims=True)
        c2 = cc * lax.rsqrt(var + LN_EPS) * g_ref[...] + b_ref[...]
        c3 = (c2 * _sigmoid(c2)).astype(bf16)
        c3_ref[...] = c3
        y_ref[...] = jnp.dot(c3, w_ref[...], preferred_element_type=f32).astype(bf16)

    row = pl.BlockSpec((tm, D_MODEL), lambda i: (i, 0))
    vec = pl.BlockSpec((1, D_MODEL), lambda i: (0, 0))
    sd = SDS((T, D_MODEL), bf16)
    return _launch(body, name="conv_out_fwd", grid=(T // tm,), in_specs=[row, vec, vec, pl.BlockSpec((D_MODEL, D_MODEL), lambda i: (0, 0))],
                   out_specs=[row, row], out_shape=(sd, sd), args=(c1, g, b, w), sem=("parallel",), comm=comm)


def _conv_out_bwd(dyc, w, c1, g, b):
    T = c1.shape[0]
    tm = 1024

    def body(d_ref, w_ref, c_ref, g_ref, b_ref, dc_ref, dg_ref, db_ref):
        dc3 = lax.dot_general(d_ref[...], w_ref[...], NT, preferred_element_type=f32)
        cv = c_ref[...]
        gv = g_ref[...]
        mu = jnp.mean(cv, axis=-1, keepdims=True)
        cc = cv - mu
        var = jnp.mean(cc * cc, axis=-1, keepdims=True)
        rs = lax.rsqrt(var + LN_EPS)
        xh = cc * rs
        c2 = xh * gv + b_ref[...]
        sg = _sigmoid(c2)
        dc2 = dc3 * (sg * (1.0 + c2 * (1.0 - sg)))
        dxh = dc2 * gv
        dc_ref[...] = rs * (dxh - jnp.mean(dxh, axis=-1, keepdims=True) - xh * jnp.mean(dxh * xh, axis=-1, keepdims=True))
        _accumulate(pl.program_id(0) == 0, [(dg_ref, jnp.sum(dc2 * xh, axis=0, keepdims=True)), (db_ref, jnp.sum(dc2, axis=0, keepdims=True))])

    row = pl.BlockSpec((tm, D_MODEL), lambda i: (i, 0))
    vec = pl.BlockSpec((1, D_MODEL), lambda i: (0, 0))
    return pl.pallas_call(body, grid=(T // tm,), in_specs=[row, pl.BlockSpec((D_MODEL, D_MODEL), lambda i: (0, 0)), row, vec, vec],
                          out_specs=[row, vec, vec],
                          out_shape=(SDS((T, D_MODEL), f32), SDS((1, D_MODEL), f32), SDS((1, D_MODEL), f32)),
                          name="conv_out_bwd", compiler_params=_cp(("arbitrary",)))(dyc, w, c1, g, b)


NN = (((1,), (0,)), ((), ()))
NT = (((1,), (1,)), ((), ()))
TN = (((0,), (0,)), ((), ()))


def _mm(name, a, b, *, grid, a_spec, b_spec, o_spec, o_shape, o_dtype, dims, acc_shape=None, k_axis=None,
        res=None, res_spec=None, sem=None, comm=None):
    nk = 1 if k_axis is None else grid[k_axis]

    def body(*refs):
        if res is None:
            a_ref, b_ref, o_ref = refs[:3]
            r_ref, scr = None, refs[3:]
        else:
            a_ref, b_ref, r_ref, o_ref = refs[:4]
            scr = refs[4:]
        p = lax.dot_general(a_ref[...].astype(bf16), b_ref[...].astype(bf16), dims, preferred_element_type=f32)
        if nk == 1:
            if r_ref is not None:
                p = p + r_ref[...]
            o_ref[...] = p.astype(o_dtype)
            return
        acc = scr[0]
        k = pl.program_id(k_axis)

        @pl.when(k == 0)
        def _():
            acc[...] = p

        @pl.when(k > 0)
        def _():
            acc[...] += p

        @pl.when(k == nk - 1)
        def _():
            t = acc[...]
            if r_ref is not None:
                t = t + r_ref[...]
            o_ref[...] = t.astype(o_dtype)

    ins = [a, b] + ([] if res is None else [res])
    specs = [a_spec, b_spec] + ([] if res is None else [res_spec])
    scratch = [] if nk == 1 else [pltpu.VMEM(acc_shape, f32)]
    return _launch(body, name=name, grid=grid, in_specs=specs, out_specs=o_spec, out_shape=SDS(o_shape, o_dtype),
                   args=ins, scratch_shapes=scratch, sem=sem, comm=comm)


def _mm_nn_full(name, a, b, o_dtype, res=None, tm=1024, comm=None):
    T, K = a.shape
    N = b.shape[1]
    return _mm(name, a, b, grid=(T // tm,), a_spec=pl.BlockSpec((tm, K), lambda i: (i, 0)),
               b_spec=pl.BlockSpec((K, N), lambda i: (0, 0)), o_spec=pl.BlockSpec((tm, N), lambda i: (i, 0)),
               o_shape=(T, N), o_dtype=o_dtype, dims=NN, res=res,
               res_spec=None if res is None else pl.BlockSpec((tm, N), lambda i: (i, 0)), sem=("parallel",), comm=comm)


def _mm_nt_full(name, a, b, o_dtype, tm=1024):
    T, N = a.shape
    K = b.shape[0]
    return _mm(name, a, b, grid=(T // tm,), a_spec=pl.BlockSpec((tm, N), lambda i: (i, 0)),
               b_spec=pl.BlockSpec((K, N), lambda i: (0, 0)), o_spec=pl.BlockSpec((tm, K), lambda i: (i, 0)),
               o_shape=(T, K), o_dtype=o_dtype, dims=NT, sem=("parallel",))


def _mm_tn_tokens(name, a, b, o_dtype, tk=1024):
    T, K = a.shape
    N = b.shape[1]
    return _mm(name, a, b, grid=(T // tk,), a_spec=pl.BlockSpec((tk, K), lambda k: (k, 0)),
               b_spec=pl.BlockSpec((tk, N), lambda k: (k, 0)), o_spec=pl.BlockSpec((K, N), lambda k: (0, 0)),
               o_shape=(K, N), o_dtype=o_dtype, dims=TN, acc_shape=(K, N), k_axis=0, sem=("arbitrary",))


def _ffn_up(h2, wg, wu):
    T = h2.shape[0]
    tm = 1024

    def body(h_ref, wg_ref, wu_ref, a_ref, b_ref, f_ref):
        hv = h_ref[...]
        av = lax.dot_general(hv, wg_ref[...], NT, preferred_element_type=f32)
        bv = lax.dot_general(hv, wu_ref[...], NT, preferred_element_type=f32)
        a_ref[...] = av.astype(bf16)
        b_ref[...] = bv.astype(bf16)
        f_ref[...] = (av * _sigmoid(av) * bv).astype(bf16)

    wspec = pl.BlockSpec((None, FF_S, D_MODEL), lambda s, m: (s, 0, 0))
    ospec = pl.BlockSpec((None, tm, FF_S), lambda s, m: (s, m, 0))
    osd = SDS((N_CHIPS, T, FF_S), bf16)
    return pl.pallas_call(body, grid=(N_CHIPS, T // tm),
                          in_specs=[pl.BlockSpec((tm, D_MODEL), lambda s, m: (m, 0)), wspec, wspec],
                          out_specs=[ospec, ospec, ospec], out_shape=(osd, osd, osd), name="ffn_up",
                          compiler_params=_cp(("parallel", "parallel")))(h2, wg, wu)


def _ffn_down_bwd(dx2, wd, a, b):
    T = dx2.shape[0]
    tm = 1024

    def body(d_ref, w_ref, a_ref, b_ref, da_ref, db_ref):
        df = lax.dot_general(d_ref[...].astype(bf16), w_ref[...], NT, preferred_element_type=f32)
        av = a_ref[...].astype(f32)
        sg = _sigmoid(av)
        da_ref[...] = (df * b_ref[...].astype(f32) * (sg * (1.0 + av * (1.0 - sg)))).astype(bf16)
        db_ref[...] = (df * av * sg).astype(bf16)

    aspec = pl.BlockSpec((None, tm, FF_S), lambda m, s: (s, m, 0))
    osd = SDS((N_CHIPS, T, FF_S), bf16)
    return pl.pallas_call(body, grid=(T // tm, N_CHIPS),
                          in_specs=[pl.BlockSpec((tm, D_MODEL), lambda m, s: (m, 0)),
                                    pl.BlockSpec((None, FF_S, D_MODEL), lambda m, s: (s, 0, 0)), aspec, aspec],
                          out_specs=[aspec, aspec], out_shape=(osd, osd), name="ffn_down_bwd",
                          compiler_params=_cp(("parallel", "parallel")))(dx2, wd, a, b)


def _merge_bwd(dm, proj, gate_b, ya, yc):
    T = proj.shape[0]
    tm = 512

    def body(dm_ref, ga_ref, gc_ref, ba_ref, bc_ref, ya_ref, yc_ref, dya_ref, dyc_ref, dga_ref, dgc_ref, dba_ref, dbc_ref):
        dmv = dm_ref[...].astype(f32)
        sa = _sigmoid(ga_ref[...].astype(f32) + ba_ref[...])
        sc = _sigmoid(gc_ref[...].astype(f32) + bc_ref[...])
        dya_ref[...] = (dmv * sa).astype(bf16)
        dyc_ref[...] = (dmv * sc).astype(bf16)
        dga = dmv * ya_ref[...].astype(f32) * sa * (1.0 - sa)
        dgc = dmv * yc_ref[...].astype(f32) * sc * (1.0 - sc)
        dga_ref[...] = dga.astype(bf16)
        dgc_ref[...] = dgc.astype(bf16)
        pa = jnp.sum(dga, axis=0, keepdims=True)
        pc = jnp.sum(dgc, axis=0, keepdims=True)

        @pl.when(pl.program_id(1) == 0)
        def _():
            dba_ref[...] = pa
            dbc_ref[...] = pc

        @pl.when(pl.program_id(1) > 0)
        def _():
            dba_ref[...] += pa
            dbc_ref[...] += pc

    blk = pl.BlockSpec((tm, CB), lambda j, i: (i, j))
    vec = pl.BlockSpec((1, CB), lambda j, i: (0, j))
    big = SDS((T, D_MODEL), bf16)
    small = SDS((1, D_MODEL), f32)
    return pl.pallas_call(
        body, grid=(2, T // tm),
        in_specs=[blk, pl.BlockSpec((tm, CB), lambda j, i: (i, GA_CB + j)), pl.BlockSpec((tm, CB), lambda j, i: (i, GC_CB + j)),
                  vec, pl.BlockSpec((1, CB), lambda j, i: (0, 2 + j)), blk, blk],
        out_specs=[blk, blk, blk, blk, vec, vec], out_shape=(big, big, big, big, small, small), name="merge_bwd",
        compiler_params=_cp(("parallel", "arbitrary")))(dm, proj, proj, gate_b, gate_b, ya, yc)


CONV_TS = 512
CONV_HALO = 32
CONV_RC = 64
CONV_WIN = CONV_TS + CONV_HALO
SUBLANES = 8


def _fill_shifted(win, sh):
    for b in range(1, SUBLANES):
        sh[b - 1] = win[pl.ds(b, CONV_WIN - SUBLANES), :]


def _rows_at(win, sh, row):
    a, b = divmod(row, SUBLANES)
    if b == 0:
        return win[pl.ds(row, CONV_RC), :]
    return sh[b - 1, pl.ds(a * SUBLANES, CONV_RC), :]


def _glu_conv_fwd(proj3, w, bias, comm=None):
    B = proj3.shape[0]
    nt = SEQ // CONV_TS
    hb = CONV_TS // CONV_HALO

    def body(ua_ref, ub_ref, ha_ref, hb_ref, w_ref, b_ref, o_ref, win, sh):
        i = pl.program_id(2)
        c0 = ua_ref[...].astype(f32) * _sigmoid(ub_ref[...].astype(f32))
        halo = ha_ref[...].astype(f32) * _sigmoid(hb_ref[...].astype(f32))
        win[0:CONV_HALO, :] = jnp.where(i > 0, halo, 0.0)
        win[CONV_HALO:, :] = c0
        _fill_shifted(win, sh)
        for r0 in range(0, CONV_TS, CONV_RC):
            acc = jnp.zeros((CONV_RC, CB), f32) + b_ref[...]
            for k in range(CONV_K):
                acc = acc + _rows_at(win, sh, r0 + CONV_HALO - (CONV_K - 1) + k) * w_ref[k:k + 1, :]
            o_ref[r0:r0 + CONV_RC, :] = acc

    def cur(cb):
        return pl.BlockSpec((None, CONV_TS, CB), lambda b, j, i: (b, i, cb + j))

    def prev(cb):
        return pl.BlockSpec((None, CONV_HALO, CB), lambda b, j, i: (b, jnp.maximum(i * hb - 1, 0), cb + j))

    return _launch(
        body, name="glu_conv_fwd", grid=(B, 2, nt),
        in_specs=[cur(UA_CB), cur(UB_CB), prev(UA_CB), prev(UB_CB),
                  pl.BlockSpec((CONV_K, CB), lambda b, j, i: (0, j)), pl.BlockSpec((1, CB), lambda b, j, i: (0, j))],
        out_specs=pl.BlockSpec((None, CONV_TS, CB), lambda b, j, i: (b, i, j)),
        out_shape=SDS((B, SEQ, D_MODEL), f32), args=(proj3, proj3, proj3, proj3, w, bias),
        scratch_shapes=[pltpu.VMEM((CONV_WIN, CB), f32), pltpu.VMEM((SUBLANES - 1, CONV_WIN - SUBLANES, CB), f32)],
        sem=("parallel", "parallel", "parallel"), comm=comm)


def _glu_conv_bwd(dc1, proj3, w, comm=None):
    B = proj3.shape[0]
    nt = SEQ // CONV_TS
    hb = CONV_TS // CONV_HALO

    def body(d_ref, dn_ref, ua_ref, ub_ref, ha_ref, hb_ref, w_ref, dua_ref, dub_ref, dw_ref, db_ref, winc, wind, accw, shc, shd):
        b = pl.program_id(1)
        i = pl.program_id(2)
        first = jnp.logical_and(b == 0, i == 0)
        last = jnp.logical_and(b == B - 1, i == nt - 1)

        @pl.when(first)
        def _():
            accw[...] = jnp.zeros_like(accw)
            db_ref[...] = jnp.zeros_like(db_ref)

        halo = ha_ref[...].astype(f32) * _sigmoid(hb_ref[...].astype(f32))
        winc[0:CONV_HALO, :] = jnp.where(i > 0, halo, 0.0)
        winc[CONV_HALO:, :] = ua_ref[...].astype(f32) * _sigmoid(ub_ref[...].astype(f32))
        wind[0:CONV_TS, :] = d_ref[...]
        wind[CONV_TS:, :] = jnp.where(i < nt - 1, dn_ref[...], 0.0)
        db_ref[...] += jnp.sum(d_ref[...], axis=0, keepdims=True)
        _fill_shifted(winc, shc)
        _fill_shifted(wind, shd)
        for r0 in range(0, CONV_TS, CONV_RC):
            dc0 = jnp.zeros((CONV_RC, CB), f32)
            for k in range(CONV_K):
                dc0 = dc0 + _rows_at(wind, shd, r0 + (CONV_K - 1) - k) * w_ref[k:k + 1, :]
            uav = ua_ref[r0:r0 + CONV_RC, :].astype(f32)
            sg = _sigmoid(ub_ref[r0:r0 + CONV_RC, :].astype(f32))
            dua_ref[r0:r0 + CONV_RC, :] = (dc0 * sg).astype(bf16)
            dub_ref[r0:r0 + CONV_RC, :] = (dc0 * uav * sg * (1.0 - sg)).astype(bf16)
            dv = wind[r0:r0 + CONV_RC, :]
            for k in range(CONV_K):
                prod = dv * _rows_at(winc, shc, r0 + CONV_HALO - (CONV_K - 1) + k)
                accw[k] += jnp.sum(prod.reshape(CONV_RC // 8, 8, CB), axis=0)

        @pl.when(last)
        def _():
            for k in range(CONV_K):
                dw_ref[k:k + 1, :] = jnp.sum(accw[k], axis=0, keepdims=True)
            dw_ref[CONV_K:, :] = jnp.zeros((CONV_HALO - CONV_K, CB), f32)

    def cur(cb):
        return pl.BlockSpec((None, CONV_TS, CB), lambda j, b, i: (b, i, cb + j))

    def prev(cb):
        return pl.BlockSpec((None, CONV_HALO, CB), lambda j, b, i: (b, jnp.maximum(i * hb - 1, 0), cb + j))

    nxt = pl.BlockSpec((None, CONV_HALO, CB), lambda j, b, i: (b, jnp.minimum((i + 1) * hb, SEQ // CONV_HALO - 1), j))
    big = SDS((B, SEQ, D_MODEL), bf16)
    return _launch(
        body, name="glu_conv_bwd", grid=(2, B, nt),
        in_specs=[cur(0), nxt, cur(UA_CB), cur(UB_CB), prev(UA_CB), prev(UB_CB), pl.BlockSpec((CONV_K, CB), lambda j, b, i: (0, j))],
        out_specs=[cur(0), cur(0), pl.BlockSpec((CONV_HALO, CB), lambda j, b, i: (0, j)), pl.BlockSpec((1, CB), lambda j, b, i: (0, j))],
        out_shape=(big, big, SDS((CONV_HALO, D_MODEL), f32), SDS((1, D_MODEL), f32)),
        args=(dc1, dc1, proj3, proj3, proj3, proj3, w),
        scratch_shapes=[pltpu.VMEM((CONV_WIN, CB), f32), pltpu.VMEM((CONV_WIN, CB), f32), pltpu.VMEM((CONV_K, SUBLANES, CB), f32),
                        pltpu.VMEM((SUBLANES - 1, CONV_WIN - SUBLANES, CB), f32),
                        pltpu.VMEM((SUBLANES - 1, CONV_WIN - SUBLANES, CB), f32)],
        sem=("parallel", "arbitrary", "arbitrary"), comm=comm)


def _band(first, dil):
    kw = Q_BLOCK if first else 2 * Q_BLOCK
    qi = lax.broadcasted_iota(jnp.int32, (Q_BLOCK, kw), 0)
    kj = lax.broadcasted_iota(jnp.int32, (Q_BLOCK, kw), 1)
    rel = qi - kj + (0 if first else Q_BLOCK)
    valid = jnp.logical_and(rel >= 0, rel <= Q_BLOCK)
    return valid, rel.astype(f32) * float(dil)


def _bias(first, dil, slope):
    valid, dist = _band(first, dil)
    return jnp.where(valid, -slope * dist, NEG)


def _scores(q, k, bias):
    return lax.dot_general(q, k, NT, preferred_element_type=f32) * (HEAD_DIM ** -0.5) + bias


def _pair_cols(hp):
    return slice(hp * LANES, (hp + 1) * LANES)


def _half(x2, e):
    lane = lax.broadcasted_iota(jnp.int32, (1, LANES), 1)
    keep = (lane < HEAD_DIM) if e == 0 else (lane >= HEAD_DIM)
    return jnp.where(keep, x2, jnp.zeros_like(x2))


def _attn_fwd(g, q4, k4, v4, qcb, kcb, vcb, comm=None):
    _, dil = GROUPS[g]
    B, r, L, _ = q4.shape
    nb = L // Q_BLOCK
    slopes = [float(s) for s in _SLOPES[g]]

    def body(q_ref, k_ref, v_ref, o_ref, lse_ref, bias, s_scr, p_scr):
        lane = lax.broadcasted_iota(jnp.int32, (Q_BLOCK, LANES), 1)
        if nb > 1:
            for h in range(HEADS):
                bias[h] = _bias(False, dil, slopes[h])

        def block(n, first):
            q0 = 0 if first else pl.multiple_of(n * Q_BLOCK, Q_BLOCK)
            k0 = 0 if first else pl.multiple_of((n - 1) * Q_BLOCK, Q_BLOCK)
            kw = Q_BLOCK if first else 2 * Q_BLOCK
            for hp in range(HEADS // 2):
                q2 = q_ref[pl.ds(q0, Q_BLOCK), _pair_cols(hp)]
                k2 = k_ref[pl.ds(k0, kw), _pair_cols(hp)]
                for e in range(2):
                    h = 2 * hp + e
                    b_h = _bias(True, dil, slopes[h]) if first else bias[h]
                    s_scr[h, :, :kw] = _scores(_half(q2, e), k2, b_h)
            st = jnp.zeros((Q_BLOCK, LANES), f32)
            dens = jnp.ones((Q_BLOCK, LANES), f32)
            for h in range(HEADS):
                s = s_scr[h, :, :kw]
                m = jnp.max(s, axis=-1, keepdims=True)
                p = jnp.exp(s - m)
                den = jnp.sum(p, axis=-1, keepdims=True)
                p_scr[h, :, :kw] = p.astype(bf16)
                st = jnp.where(lane == h, m + jnp.log(den), st)
                dens = jnp.where(lane == h, den, dens)
            lse_ref[pl.ds(q0, Q_BLOCK), :] = st
            inv = 1.0 / dens
            for hp in range(HEADS // 2):
                v2 = v_ref[pl.ds(k0, kw), _pair_cols(hp)]
                o2 = sum(jnp.dot(p_scr[2 * hp + e, :, :kw], _half(v2, e), preferred_element_type=f32) * inv[:, 2 * hp + e:2 * hp + e + 1]
                         for e in range(2))
                o_ref[pl.ds(q0, Q_BLOCK), _pair_cols(hp)] = o2.astype(bf16)

        block(0, True)
        if nb > 1:
            def step(n, carry):
                block(n, False)
                return carry
            lax.fori_loop(1, nb, step, 0)

    def spec(cb):
        return pl.BlockSpec((None, None, L, GW), lambda b, c: (b, c, 0, cb))

    return _launch(
        body, name=f"attn_fwd_g{g}", grid=(B, r), in_specs=[spec(qcb), spec(kcb), spec(vcb)],
        out_specs=[spec(0), pl.BlockSpec((None, None, L, LANES), lambda b, c: (b, c, 0, 0))],
        out_shape=(SDS((B, r, L, GW), bf16), SDS((B, r, L, LANES), f32)), args=(q4, k4, v4),
        scratch_shapes=[pltpu.VMEM((HEADS, Q_BLOCK, 2 * Q_BLOCK), f32), pltpu.VMEM((HEADS, Q_BLOCK, 2 * Q_BLOCK), f32),
                        pltpu.VMEM((HEADS, Q_BLOCK, 2 * Q_BLOCK), bf16)],
        sem=("parallel", "parallel"), comm=comm)


def _attn_bwd(g, q4, k4, v4, qcb, kcb, vcb, do4, lse4, dl4, comm=None):
    _, dil = GROUPS[g]
    B, r, L, _ = q4.shape
    nb = L // Q_BLOCK
    slopes = [float(s) for s in _SLOPES[g]]
    scale = HEAD_DIM ** -0.5

    def body(q_ref, k_ref, v_ref, do_ref, lse_ref, dl_ref, dq_ref, dk_ref, dv_ref, dk_acc, dv_acc, bias, s_scr, dp_scr, p_scr, ds_scr):
        dk_acc[...] = jnp.zeros_like(dk_acc)
        dv_acc[...] = jnp.zeros_like(dv_acc)
        if nb > 1:
            for h in range(HEADS):
                bias[h] = _bias(False, dil, slopes[h])

        def block(n, first):
            q0 = 0 if first else pl.multiple_of(n * Q_BLOCK, Q_BLOCK)
            k0 = 0 if first else pl.multiple_of((n - 1) * Q_BLOCK, Q_BLOCK)
            kw = Q_BLOCK if first else 2 * Q_BLOCK
            for hp in range(HEADS // 2):
                q2 = q_ref[pl.ds(q0, Q_BLOCK), _pair_cols(hp)]
                k2 = k_ref[pl.ds(k0, kw), _pair_cols(hp)]
                v2 = v_ref[pl.ds(k0, kw), _pair_cols(hp)]
                do2 = do_ref[pl.ds(q0, Q_BLOCK), _pair_cols(hp)]
                for e in range(2):
                    h = 2 * hp + e
                    b_h = _bias(True, dil, slopes[h]) if first else bias[h]
                    s_scr[h, :, :kw] = _scores(_half(q2, e), k2, b_h)
                    dp_scr[h, :, :kw] = lax.dot_general(_half(do2, e), v2, NT, preferred_element_type=f32)
            for h in range(HEADS):
                p = jnp.exp(s_scr[h, :, :kw] - lse_ref[pl.ds(q0, Q_BLOCK), h:h + 1])
                p_scr[h, :, :kw] = p.astype(bf16)
                ds_scr[h, :, :kw] = (p * (dp_scr[h, :, :kw] - dl_ref[pl.ds(q0, Q_BLOCK), h:h + 1])).astype(bf16)
            for hp in range(HEADS // 2):
                cols = _pair_cols(hp)
                q2 = q_ref[pl.ds(q0, Q_BLOCK), cols]
                k2 = k_ref[pl.ds(k0, kw), cols]
                do2 = do_ref[pl.ds(q0, Q_BLOCK), cols]
                ds = [ds_scr[2 * hp + e, :, :kw] for e in range(2)]
                dq2 = sum(jnp.dot(ds[e], _half(k2, e), preferred_element_type=f32) for e in range(2))
                dq_ref[pl.ds(q0, Q_BLOCK), cols] = (dq2 * scale).astype(bf16)
                dk2 = sum(lax.dot_general(ds[e], _half(q2, e), TN, preferred_element_type=f32) for e in range(2))
                dk_acc[pl.ds(k0, kw), cols] += dk2 * scale
                dv2 = sum(lax.dot_general(p_scr[2 * hp + e, :, :kw], _half(do2, e), TN, preferred_element_type=f32) for e in range(2))
                dv_acc[pl.ds(k0, kw), cols] += dv2

        block(0, True)
        if nb > 1:
            def step(n, carry):
                block(n, False)
                return carry
            lax.fori_loop(1, nb, step, 0)
        dk_ref[...] = dk_acc[...].astype(bf16)
        dv_ref[...] = dv_acc[...].astype(bf16)

    def spec(cb):
        return pl.BlockSpec((None, None, L, GW), lambda b, c: (b, c, 0, cb))

    st = pl.BlockSpec((None, None, L, LANES), lambda b, c: (b, c, 0, 0))
    osd = SDS((B, r, L, GW), bf16)
    return _launch(
        body, name=f"attn_bwd_g{g}", grid=(B, r), in_specs=[spec(qcb), spec(kcb), spec(vcb), spec(0), st, st],
        out_specs=[spec(0), spec(0), spec(0)], out_shape=(osd, osd, osd), args=(q4, k4, v4, do4, lse4, dl4),
        scratch_shapes=[pltpu.VMEM((L, GW), f32), pltpu.VMEM((L, GW), f32)]
        + [pltpu.VMEM((HEADS, Q_BLOCK, 2 * Q_BLOCK), f32)] * 3 + [pltpu.VMEM((HEADS, Q_BLOCK, 2 * Q_BLOCK), bf16)] * 2,
        sem=("parallel", "parallel"), comm=comm)


RT = 512
RCH = GW // LANES
DILS = tuple(d for _, d in GROUPS[1:])


def _res_spec(r, width):
    return pl.BlockSpec((None, r, RT // r, width), lambda b, i, *_: (b, 0, i, 0))


def _tok_spec(width, cb=0):
    return pl.BlockSpec((None, RT, width), lambda b, i, *_: (b, i, cb))


def _to_residues(res_ref, scr, r, width):
    for c in range(r):
        for jj in range(width // LANES):
            res_ref[c, :, jj * LANES:(jj + 1) * LANES] = scr[jj, pl.ds(c, RT // r, stride=r), :].astype(res_ref.dtype)


def _from_residues(scr, res_ref, r, width):
    for c in range(r):
        for jj in range(width // LANES):
            scr[jj, pl.ds(c, RT // r, stride=r), :] = res_ref[c, :, jj * LANES:(jj + 1) * LANES].astype(f32)


def _qkv_to_residues(proj3, g):
    r = GROUPS[g][1]
    B = proj3.shape[0]

    def body(q_ref, k_ref, v_ref, o_ref, scr):
        for p, x_ref in enumerate((q_ref, k_ref, v_ref)):
            x = x_ref[...].astype(f32)
            for jj in range(RCH):
                scr[p * RCH + jj] = x[:, jj * LANES:(jj + 1) * LANES]
        _to_residues(o_ref, scr, r, ATTN_W)

    return pl.pallas_call(
        body, grid=(B, SEQ // RT), in_specs=[_tok_spec(GW, 3 * p + g) for p in range(3)], out_specs=_res_spec(r, ATTN_W),
        out_shape=SDS((B, r, SEQ // r, ATTN_W), bf16), scratch_shapes=[pltpu.VMEM((3 * RCH, RT, LANES), f32)],
        name=f"qkv_to_residues_g{g}", compiler_params=_cp(("parallel", "parallel")))(proj3, proj3, proj3)


def _attn_mix(o0, l0, o_res, l_res):
    B = o0.shape[0]

    def body(o0_ref, l0_ref, o1_ref, o2_ref, l1_ref, l2_ref, y_ref, lt_ref, lt1_ref, lt2_ref, so, sl):
        for gi, (o_ref, l_ref, r) in enumerate(((o1_ref, l1_ref, DILS[0]), (o2_ref, l2_ref, DILS[1]))):
            _from_residues(so.at[gi], o_ref, r, GW)
            _from_residues(sl.at[gi:gi + 1], l_ref, r, LANES)
        ls = [l0_ref[...], sl[0], sl[1]]
        m = functools.reduce(jnp.maximum, ls)
        ws = [jnp.exp(l - m) for l in ls]
        den = ws[0] + ws[1] + ws[2]
        alphas = [w / den for w in ws]
        lt = m + jnp.log(den)
        lt_ref[...] = lt
        sl[2] = lt
        _to_residues(lt1_ref, sl.at[2:3], DILS[0], LANES)
        _to_residues(lt2_ref, sl.at[2:3], DILS[1], LANES)
        for h in range(HEADS):
            cols = slice(h * HEAD_DIM, (h + 1) * HEAD_DIM)
            jj, lo = divmod(h * HEAD_DIM, LANES)
            acc = alphas[0][:, h:h + 1] * o0_ref[:, cols].astype(f32)
            for gi in range(2):
                acc = acc + alphas[gi + 1][:, h:h + 1] * so[gi, jj, :, lo:lo + HEAD_DIM]
            y_ref[:, cols] = acc.astype(bf16)

    in_specs = [_tok_spec(GW), _tok_spec(LANES), _res_spec(DILS[0], GW), _res_spec(DILS[1], GW), _res_spec(DILS[0], LANES), _res_spec(DILS[1], LANES)]
    out_specs = [_tok_spec(GW), _tok_spec(LANES), _res_spec(DILS[0], LANES), _res_spec(DILS[1], LANES)]
    return pl.pallas_call(
        body, grid=(B, SEQ // RT), in_specs=in_specs, out_specs=out_specs,
        out_shape=(SDS((B, SEQ, GW), bf16), SDS((B, SEQ, LANES), f32)) + tuple(SDS((B, r, SEQ // r, LANES), f32) for r in DILS),
        scratch_shapes=[pltpu.VMEM((2, RCH, RT, LANES), f32), pltpu.VMEM((3, RT, LANES), f32)],
        name="attn_mix", compiler_params=_cp(("parallel", "parallel")))(o0, l0, *o_res, *l_res)


def _attn_delta(dmix, mix):
    B = dmix.shape[0]

    def body(d_ref, y_ref, dl_ref, dl1_ref, dl2_ref, dm1_ref, dm2_ref, sx, sd):
        lane = lax.broadcasted_iota(jnp.int32, (RT, LANES), 1)
        acc = jnp.zeros((RT, LANES), f32)
        dv = d_ref[...].astype(f32)
        for jj in range(RCH):
            sx[jj] = dv[:, jj * LANES:(jj + 1) * LANES]
        for h in range(HEADS):
            cols = slice(h * HEAD_DIM, (h + 1) * HEAD_DIM)
            dl = jnp.sum(dv[:, cols] * y_ref[:, cols].astype(f32), axis=-1, keepdims=True)
            acc = jnp.where(lane == h, dl, acc)
        dl_ref[...] = acc
        sd[0] = acc
        _to_residues(dl1_ref, sd, DILS[0], LANES)
        _to_residues(dl2_ref, sd, DILS[1], LANES)
        _to_residues(dm1_ref, sx, DILS[0], GW)
        _to_residues(dm2_ref, sx, DILS[1], GW)

    return pl.pallas_call(
        body, grid=(B, SEQ // RT), in_specs=[_tok_spec(GW), _tok_spec(GW)],
        out_specs=[_tok_spec(LANES), _res_spec(DILS[0], LANES), _res_spec(DILS[1], LANES), _res_spec(DILS[0], GW), _res_spec(DILS[1], GW)],
        out_shape=(SDS((B, SEQ, LANES), f32),) + tuple(SDS((B, r, SEQ // r, LANES), f32) for r in DILS)
        + tuple(SDS((B, r, SEQ // r, GW), bf16) for r in DILS),
        scratch_shapes=[pltpu.VMEM((RCH, RT, LANES), f32), pltpu.VMEM((1, RT, LANES), f32)],
        name="attn_delta", compiler_params=_cp(("parallel", "parallel")))(dmix, mix)


N_CB = IN_W // CB


def _assemble_dproj(dqkv, dua, dub, dga, dgc):
    B = dua.shape[0]
    ng = len(GROUPS)
    flat = [dqkv[g][p] for p in range(3) for g in range(ng)]
    wide = [dua, dub, dga, dgc]

    def body(*refs):
        srcs, wides, o_ref, scr = refs[:3 * ng], refs[3 * ng:3 * ng + 4], refs[3 * ng + 4], refs[3 * ng + 5]
        for jv in range(3 * ng):
            g = jv % ng
            if g == 0:
                o_ref[:, jv * GW:(jv + 1) * GW] = srcs[jv][...]
            else:
                _from_residues(scr, srcs[jv], GROUPS[g][1], GW)
                for jj in range(RCH):
                    o_ref[:, jv * GW + jj * LANES:jv * GW + (jj + 1) * LANES] = scr[jj].astype(bf16)
        for wv in range(4):
            lo = 3 * ATTN_W + wv * D_MODEL
            o_ref[:, lo:lo + D_MODEL] = wides[wv][...]

    in_specs = [_tok_spec(GW) if (jv % ng) == 0 else _res_spec(GROUPS[jv % ng][1], GW) for jv in range(3 * ng)]
    in_specs += [_tok_spec(D_MODEL)] * 4
    return pl.pallas_call(
        body, grid=(B, SEQ // RT), in_specs=in_specs, out_specs=_tok_spec(IN_W),
        out_shape=SDS((B, SEQ, IN_W), bf16), scratch_shapes=[pltpu.VMEM((RCH, RT, LANES), f32)],
        name="assemble_dproj", compiler_params=_cp(("parallel", "parallel")))(*flat, *wide)


def _same_shape_groups(names, arrays):
    groups = {}
    for n, a in zip(names, arrays):
        groups.setdefault(a.shape, ([], []))
        groups[a.shape][0].append(n)
        groups[a.shape][1].append(a)
    return list(groups.values())


def _add_pairs(parts, gots, core, name):
    k = len(parts)
    n, h, C = gots[0].shape

    def body(c_ref, *refs):
        for i in range(k):
            refs[2 * k + i][...] = (refs[i][...].astype(f32) + refs[k + i][...].astype(f32)).astype(bf16)

    blk = pl.BlockSpec((None, h, C), lambda s, c_ref: (s, 0, 0))
    own = pl.BlockSpec((None, h, C), lambda s, c_ref: (s, c_ref[0], 0))
    spec = pltpu.PrefetchScalarGridSpec(num_scalar_prefetch=1, grid=(n,), in_specs=[own] * k + [blk] * k, out_specs=[blk] * k)
    return pl.pallas_call(body, grid_spec=spec, out_shape=tuple(SDS((n, h, C), bf16) for _ in range(k)), name=name,
                          compiler_params=_cp(("parallel",)))(core, *parts, *gots)


def _sum4(sums, gots, chip, core, name):
    k = len(sums)
    _, h, C = sums[0].shape

    def body(s_ref, c_ref, *refs):
        for i in range(k):
            q_ref = refs[k + i]
            t = refs[i][...].astype(f32) + q_ref[0].astype(f32)
            t = t + q_ref[1].astype(f32)
            refs[2 * k + i][...] = t + q_ref[2].astype(f32)

    spec = pltpu.PrefetchScalarGridSpec(
        num_scalar_prefetch=2, grid=(1,),
        in_specs=[pl.BlockSpec((None, h, C), lambda i, s_ref, c_ref: (s_ref[0], 0, 0))] * k
        + [pl.BlockSpec((N_CHIPS - 1, h, C), lambda i, s_ref, c_ref: (0, 0, 0))] * k,
        out_specs=[pl.BlockSpec((h, C), lambda i, s_ref, c_ref: (c_ref[0], 0))] * k)
    return pl.pallas_call(body, grid_spec=spec, out_shape=tuple(SDS((2 * h, C), f32) for _ in range(k)), name=name,
                          compiler_params=_cp(("arbitrary",)))(chip, core, *sums, *gots)


def _adamw(ws, gs, ms, vs, name):
    k = len(ws)
    R, C = ws[0].shape
    rt = R
    for cand in (512, 256, 128, 64, 32, 16, 8):
        if R % cand == 0 and cand * C * 4 <= 2 ** 21:
            rt = cand
            break
    c1 = 1.0 / (1.0 - B1 ** STEP)
    c2 = 1.0 / (1.0 - B2 ** STEP)

    def body(*refs):
        for i in range(k):
            w_ref, g_ref, m_ref, v_ref = (refs[j * k + i] for j in range(4))
            d_ref, nm_ref, nv_ref = (refs[(4 + j) * k + i] for j in range(3))
            gv = g_ref[...]
            nm = B1 * m_ref[...] + (1.0 - B1) * gv
            nv = B2 * v_ref[...] + (1.0 - B2) * (gv * gv)
            nm_ref[...] = nm
            nv_ref[...] = nv
            d_ref[...] = -LR * ((nm * c1) / (jnp.sqrt(nv * c2) + ADAM_EPS) + WD * w_ref[...])

    blk = pl.BlockSpec((rt, C), lambda i: (i, 0))
    sd = SDS((R, C), f32)
    res = pl.pallas_call(body, grid=(R // rt,), in_specs=[blk] * (4 * k), out_specs=[blk] * (3 * k), out_shape=(sd,) * (3 * k),
                         name=name, compiler_params=_cp(("parallel",)))(*ws, *gs, *ms, *vs)
    return res[:k], res[k:2 * k], res[2 * k:]


def _coords():
    return lax.axis_index("x"), lax.axis_index("y"), lax.axis_index("c")


def _other_chips(x, y):
    return [(1 - x, y), (x, 1 - y), (1 - x, 1 - y)]


def _allgather_weights(shards, slots):
    n = len(shards)
    n_slot = 12

    def body(*refs):
        ins, outs = refs[:n], refs[2 * n:3 * n]
        send, recv = refs[3 * n:]
        x, y, c = _coords()
        me, cx, cy, cd = 2 * x + y, 2 * (1 - x) + y, 2 * x + (1 - y), 2 * (1 - x) + (1 - y)
        dev_x, dev_y, dev_s = (1 - x, y, c), (x, 1 - y, c), (x, y, 1 - c)
        started = []

        def copy(w, slot, src, dst, dev):
            return pltpu.make_async_remote_copy(src_ref=src, dst_ref=dst, send_sem=send.at[w, slot], recv_sem=recv.at[w, slot],
                                                device_id=dev, device_id_type=MESH)

        def go(cp):
            cp.start()
            started.append(cp)

        for w in range(n):
            q = shards[w].shape[0] // 4
            rows = [pl.ds(c * 2 * q + k * q, q) for k in range(2)]
            theirs = [pl.ds((1 - c) * 2 * q + k * q, q) for k in range(2)]
            own = [(ins[w].at[r, :], outs[w].at[me, r, :]) for r in rows]
            go(copy(w, 0, *own[0], dev_x))
            go(copy(w, 2, *own[1], dev_y))
            go(copy(w, 1, *own[1], dev_x))
            go(copy(w, 3, *own[0], dev_y))
            arrivals = [(0, cx, 0, (4, dev_y)), (2, cy, 1, (5, dev_x)), (1, cx, 1, None), (3, cy, 0, None), (4, cd, 0, None), (5, cd, 1, None)]
            for k, (slot, chip, quarter, onward) in enumerate(arrivals):
                blk = outs[w].at[chip, rows[quarter], :]
                copy(w, slot, blk, blk, dev_s).wait_recv()
                if onward is not None:
                    go(copy(w, onward[0], blk, blk, onward[1]))
                go(copy(w, 6 + k, blk, blk, dev_s))
            for k, (slot, chip, quarter, onward) in enumerate(arrivals):
                blk = outs[w].at[chip, theirs[quarter], :]
                copy(w, 6 + k, blk, blk, dev_s).wait_recv()
        for cp in started:
            cp.wait_send()

    return pl.pallas_call(
        body, in_specs=[HBM] * (2 * n), out_specs=[HBM] * n,
        out_shape=tuple(SDS((N_CHIPS,) + s.shape, s.dtype) for s in shards),
        input_output_aliases={n + w: w for w in range(n)},
        scratch_shapes=[pltpu.SemaphoreType.DMA((n, n_slot))] * 2,
        name="allgather_weights", compiler_params=pltpu.CompilerParams(has_side_effects=True))(*shards, *slots)


def _x_pair_exchange(parts):
    n = len(parts)
    halves = [p.shape[1] // 2 for p in parts]

    def copies(ins, outs, sems):
        send, recv = sems
        x, y, c = _coords()
        return [pltpu.make_async_remote_copy(src_ref=ins[w].at[:, pl.ds((1 - c) * halves[w], halves[w]), :], dst_ref=outs[w],
                                             send_sem=send.at[w], recv_sem=recv.at[w], device_id=(x, y, 1 - c), device_id_type=MESH)
                for w in range(n)]

    def start(ins, outs, sems):
        for cp in copies(ins, outs, sems):
            cp.start()

    def wait(ins, outs, sems):
        for cp in copies(ins, outs, sems):
            cp.wait()

    return _Exchange(ins=list(parts), out_shape=[SDS((N_CHIPS, p.shape[1] // 2, p.shape[2]), p.dtype) for p in parts],
                     scratch=[pltpu.SemaphoreType.DMA((n,))] * 2, aliases={}, start=start, wait=wait)


def _x_chip_exchange(sums):
    n = len(sums)

    def copies(ins, outs, sems):
        send, recv = sems
        x, y, c = _coords()
        return [pltpu.make_async_remote_copy(src_ref=ins[w].at[2 * px + py], dst_ref=outs[w].at[j], send_sem=send.at[w, j],
                                             recv_sem=recv.at[w, j], device_id=(px, py, c), device_id_type=MESH)
                for w in range(n) for j, (px, py) in enumerate(_other_chips(x, y))]

    def start(ins, outs, sems):
        for cp in copies(ins, outs, sems):
            cp.start()

    def wait(ins, outs, sems):
        for cp in copies(ins, outs, sems):
            cp.wait()

    return _Exchange(ins=list(sums), out_shape=[SDS((N_CHIPS - 1,) + s.shape[1:], s.dtype) for s in sums],
                     scratch=[pltpu.SemaphoreType.DMA((n, 3)), pltpu.SemaphoreType.DMA((n, 3))], aliases={}, start=start, wait=wait)


def _x_gather_ici(shards, slots):
    n = len(shards)
    halves = [s.shape[0] // 2 for s in shards]

    def copies(ins, outs, sems):
        send, recv = sems
        x, y, c = _coords()
        me = 2 * x + y
        out = []
        for w in range(n):
            mine = pl.ds(c * halves[w], halves[w])
            for j, (px, py) in enumerate(_other_chips(x, y)):
                snd = pltpu.make_async_remote_copy(src_ref=ins[w].at[mine, :], dst_ref=outs[w].at[me, mine, :], send_sem=send.at[w, j],
                                                   recv_sem=recv.at[w, j], device_id=(px, py, c), device_id_type=MESH)
                got = outs[w].at[2 * px + py, mine, :]
                rcv = pltpu.make_async_remote_copy(src_ref=got, dst_ref=got, send_sem=send.at[w, j], recv_sem=recv.at[w, j],
                                                   device_id=(px, py, c), device_id_type=MESH)
                out.append((snd, rcv))
        return out

    def start(ins, outs, sems):
        for snd, _ in copies(ins, outs, sems):
            snd.start()

    def wait(ins, outs, sems):
        for snd, rcv in copies(ins, outs, sems):
            rcv.wait_recv()
            snd.wait_send()

    return _Exchange(ins=list(shards) + list(slots), out_shape=[SDS(s.shape, s.dtype) for s in slots],
                     scratch=[pltpu.SemaphoreType.DMA((n, 3)), pltpu.SemaphoreType.DMA((n, 3))],
                     aliases={n + w: w for w in range(n)}, start=start, wait=wait)


def _x_gather_fwd(bufs):
    n = len(bufs)
    halves = [b.shape[1] // 2 for b in bufs]

    def copies(ins, outs, sems):
        send, recv = sems
        x, y, c = _coords()
        out = []
        for w in range(n):
            for j, (px, py) in enumerate(_other_chips(x, y)):
                mine = outs[w].at[2 * px + py, pl.ds(c * halves[w], halves[w]), :]
                theirs = outs[w].at[2 * px + py, pl.ds((1 - c) * halves[w], halves[w]), :]
                snd = pltpu.make_async_remote_copy(src_ref=mine, dst_ref=mine, send_sem=send.at[w, j], recv_sem=recv.at[w, j],
                                                   device_id=(x, y, 1 - c), device_id_type=MESH)
                rcv = pltpu.make_async_remote_copy(src_ref=theirs, dst_ref=theirs, send_sem=send.at[w, j], recv_sem=recv.at[w, j],
                                                   device_id=(x, y, 1 - c), device_id_type=MESH)
                out.append((snd, rcv))
        return out

    def start(ins, outs, sems):
        for snd, _ in copies(ins, outs, sems):
            snd.start()

    def wait(ins, outs, sems):
        for snd, rcv in copies(ins, outs, sems):
            rcv.wait_recv()
            snd.wait_send()

    return _Exchange(ins=list(bufs), out_shape=[SDS(b.shape, b.dtype) for b in bufs],
                     scratch=[pltpu.SemaphoreType.DMA((n, 3)), pltpu.SemaphoreType.DMA((n, 3))],
                     aliases={w: w for w in range(n)}, start=start, wait=wait)


def _final_exchange(bufs, vec):
    n = len(bufs)
    R = vec.shape[0]
    nd = 8

    def body(*refs):
        v_ref = refs[n]
        outs = refs[n + 1:2 * n + 1]
        o_ref, sbuf, send, recv, ssend, srecv = refs[2 * n + 1:]
        x, y, c = _coords()
        me = 4 * x + 2 * y + c
        sbuf[me] = v_ref[...]
        small = []
        for k in range(1, nd):
            kx, ky, kc = (k >> 2) & 1, (k >> 1) & 1, k & 1
            tx = x + kx - 2 * x * kx
            ty = y + ky - 2 * y * ky
            tc = c + kc - 2 * c * kc
            cp = pltpu.make_async_remote_copy(src_ref=v_ref, dst_ref=sbuf.at[me], send_sem=ssend.at[k], recv_sem=srecv.at[k],
                                              device_id=(tx, ty, tc), device_id_type=MESH)
            cp.start()
            small.append((cp, 4 * tx + 2 * ty + tc))
        cps = []
        for w in range(n):
            h = bufs[w].shape[0] // 2
            rows = outs[w].at[pl.ds(c * h, h), :]
            rc = pltpu.make_async_remote_copy(src_ref=rows, dst_ref=rows, send_sem=send.at[w], recv_sem=recv.at[w],
                                              device_id=(x, y, 1 - c), device_id_type=MESH)
            rc.start()
            cps.append(rc)
        for k, (cp, src) in zip(range(1, nd), small):
            pltpu.make_async_remote_copy(src_ref=v_ref, dst_ref=sbuf.at[src], send_sem=ssend.at[k], recv_sem=srecv.at[k],
                                         device_id=(x, y, c), device_id_type=MESH).wait_recv()
        for cp, _ in small:
            cp.wait_send()
        acc = sbuf[0]
        for d in range(1, nd):
            acc = acc + sbuf[d]
        o_ref[...] = acc
        for w, rc in enumerate(cps):
            h = bufs[w].shape[0] // 2
            other = outs[w].at[pl.ds((1 - c) * h, h), :]
            pltpu.make_async_remote_copy(src_ref=other, dst_ref=other, send_sem=send.at[w], recv_sem=recv.at[w],
                                         device_id=(x, y, 1 - c), device_id_type=MESH).wait_recv()
            rc.wait_send()

    vm = pl.BlockSpec(memory_space=pltpu.VMEM)
    res = pl.pallas_call(body, in_specs=[HBM] * n + [vm], out_specs=[HBM] * n + [vm],
                         out_shape=tuple(SDS(a.shape, a.dtype) for a in bufs) + (SDS((R, LANES), f32),),
                         input_output_aliases={w: w for w in range(n)},
                         scratch_shapes=[pltpu.VMEM((nd, R, LANES), f32)] + [pltpu.SemaphoreType.DMA((n,))] * 2
                         + [pltpu.SemaphoreType.DMA((nd,))] * 2, name="final_exchange",
                         compiler_params=pltpu.CompilerParams(has_side_effects=True))(*bufs, vec)
    return res[:n], res[n]


def _rows128(a, rows):
    flat = a.reshape(-1)
    return jnp.pad(flat, (0, rows * LANES - flat.shape[0])).reshape(rows, LANES)


GATHER_1 = ("w_conv_out", "w_attn_out", "w_o", "w_ffn_gate")
GATHER_2 = ("w_ffn_up", "w_ffn_down")
REDUCE_A = ("w_ffn_gate", "w_ffn_up", "w_ffn_down")
REDUCE_B = ("w_conv_out", "w_attn_out", "w_o")
REDUCE_C = ("w_in",)


def _step(x, target, norm1_g, gate_b, conv_w, conv_b, conv_ln_g, conv_ln_b, norm2_g, norm_f_g, w_in, shards, slots, chip1, core):
    B = x.shape[0]
    T = B * SEQ
    xf = x.reshape(T, D_MODEL)
    tf = target.reshape(T, D_MODEL)

    h, h_t = _rms_fwd(xf, norm1_g, "rms1_fwd")
    proj, got1 = _mm("in_proj", h, w_in, grid=(N_CHIPS, T // 1024),
                     a_spec=pl.BlockSpec((1024, D_MODEL), lambda s, m: (m, 0)),
                     b_spec=pl.BlockSpec((None, D_MODEL, IN_S), lambda s, m: (s, 0, 0)),
                     o_spec=pl.BlockSpec((1024, IN_S), lambda s, m: (m, s)), o_shape=(T, IN_W), o_dtype=bf16, dims=NN,
                     comm=_x_gather_ici([shards[n] for n in GATHER_1], [slots[n] for n in GATHER_1]))
    proj3 = proj.reshape(B, SEQ, IN_W)
    proj4 = proj.reshape(B, 1, SEQ, IN_W)

    qkv = [None] + [_qkv_to_residues(proj3, g) for g in range(1, len(GROUPS))]

    def qkv_args(g):
        return (proj4, proj4, proj4, 0, 3, 6) if g == 0 else (qkv[g], qkv[g], qkv[g], 0, 1, 2)

    (o4_0, l4_0), full1 = _attn_fwd(0, *qkv_args(0), comm=_x_gather_fwd(list(got1)))
    o4_1, l4_1 = _attn_fwd(1, *qkv_args(1))
    o4_2, l4_2 = _attn_fwd(2, *qkv_args(2))
    full = dict(zip(GATHER_1, full1))
    w_conv_out, w_attn_out, w_o, w_gate = (full[n] for n in GATHER_1)
    w_conv_out_f = w_conv_out.reshape(D_MODEL, D_MODEL)
    w_o_f = w_o.reshape(D_MODEL, D_MODEL)
    mix3, lse3, lse_r1, lse_r2 = _attn_mix(o4_0.reshape(B, SEQ, GW), l4_0.reshape(B, SEQ, LANES), [o4_1, o4_2], [l4_1, l4_2])
    mix = mix3.reshape(T, GW)
    y_attn = _mm("attn_out", mix, w_attn_out, grid=(N_CHIPS, T // 512),
                 a_spec=pl.BlockSpec((512, GW), lambda s, m: (m, 0)),
                 b_spec=pl.BlockSpec((None, GW, D_MODEL // N_CHIPS), lambda s, m: (s, 0, 0)),
                 o_spec=pl.BlockSpec((512, D_MODEL // N_CHIPS), lambda s, m: (m, s)), o_shape=(T, D_MODEL), o_dtype=bf16,
                 dims=NN, sem=("parallel", "parallel"))

    c1, got2 = _glu_conv_fwd(proj3, conv_w, conv_b, comm=_x_gather_ici([shards[n] for n in GATHER_2], [slots[n] for n in GATHER_2]))
    c1 = c1.reshape(T, D_MODEL)
    (c3, y_conv), (w_up, w_down) = _conv_out_fwd(c1, conv_ln_g, conv_ln_b, w_conv_out_f, comm=_x_gather_fwd(list(got2)))

    merged, x1, h2 = _merge_o_proj_rms(proj, gate_b, y_attn, y_conv, w_o_f, xf, norm2_g)
    fa, fb, ff = _ffn_up(h2, w_gate, w_up)
    loss, dx2, d_gf = _ffn_down_loss(ff, w_down, x1, norm_f_g.reshape(1, D_MODEL), tf)

    d_w_down = _mm("d_w_down", ff, dx2, grid=(N_CHIPS, T // 1024),
                   a_spec=pl.BlockSpec((None, 1024, FF_S), lambda s, k: (s, k, 0)),
                   b_spec=pl.BlockSpec((1024, D_MODEL), lambda s, k: (k, 0)),
                   o_spec=pl.BlockSpec((None, FF_S, D_MODEL), lambda s, k: (s, 0, 0)), o_shape=(N_CHIPS, FF_S, D_MODEL),
                   o_dtype=bf16, dims=TN, acc_shape=(FF_S, D_MODEL), k_axis=1, sem=("parallel", "arbitrary"))
    da, db = _ffn_down_bwd(dx2, w_down, fa, fb)

    def d_w_ff(name, dz):
        return _mm(name, dz, h2, grid=(N_CHIPS, T // 2048),
                   a_spec=pl.BlockSpec((None, 2048, FF_S), lambda s, k: (s, k, 0)),
                   b_spec=pl.BlockSpec((2048, D_MODEL), lambda s, k: (k, 0)),
                   o_spec=pl.BlockSpec((None, FF_S, D_MODEL), lambda s, k: (s, 0, 0)), o_shape=(N_CHIPS, FF_S, D_MODEL),
                   o_dtype=bf16, dims=TN, acc_shape=(FF_S, D_MODEL), k_axis=1, sem=("parallel", "arbitrary"))

    d_w_gate = d_w_ff("d_w_gate", da)
    d_w_up = d_w_ff("d_w_up", db)
    part = dict(w_ffn_gate=d_w_gate, w_ffn_up=d_w_up, w_ffn_down=d_w_down)

    def pair_sums(names, got):
        out = {}
        for ns, gs in _same_shape_groups(names, got):
            out.update(zip(ns, _add_pairs([part[n] for n in ns], gs, core, "pair_sum_" + ns[0])))
        return [out[n] for n in names]

    (dx1, d_g2), got = _mm_rms_bwd("ffn_dh2", [(da, w_gate), (db, w_up)], x1, norm2_g, dx2, tm=1024,
                                   a_spec=pl.BlockSpec((None, 1024, FF_S), lambda m, s: (s, m, 0)),
                                   b_spec=pl.BlockSpec((None, FF_S, D_MODEL), lambda m, s: (s, 0, 0)), dims=NN,
                                   comm=_x_pair_exchange([part[n] for n in REDUCE_A]), vmem=VMEM_LIMIT_BIG)
    sums_a = pair_sums(REDUCE_A, got)

    d_w_o = _mm_tn_tokens("d_w_o", merged, dx1, bf16).reshape(N_CHIPS, D_MODEL // N_CHIPS, D_MODEL)
    dmerged = _mm_nt_full("d_merged", dx1, w_o_f, bf16)
    dya, dyc, dga, dgc, d_gba, d_gbc = _merge_bwd(dmerged, proj, gate_b, y_attn, y_conv)

    d_w_conv_out = _mm_tn_tokens("d_w_conv_out", c3, dyc, bf16).reshape(N_CHIPS, D_MODEL // N_CHIPS, D_MODEL)
    dc1, d_ln_g, d_ln_b = _conv_out_bwd(dyc, w_conv_out_f, c1, conv_ln_g, conv_ln_b)
    (dua, dub, d_conv_w, d_conv_b), got_a = _glu_conv_bwd(dc1.reshape(B, SEQ, D_MODEL), proj3, conv_w, comm=_x_chip_exchange(sums_a))

    d_w_attn_out = _mm("d_w_attn_out", mix, dya, grid=(N_CHIPS, T // 512),
                       a_spec=pl.BlockSpec((512, GW), lambda s, k: (k, 0)),
                       b_spec=pl.BlockSpec((512, D_MODEL // N_CHIPS), lambda s, k: (k, s)),
                       o_spec=pl.BlockSpec((None, GW, D_MODEL // N_CHIPS), lambda s, k: (s, 0, 0)),
                       o_shape=(N_CHIPS, GW, D_MODEL // N_CHIPS), o_dtype=bf16, dims=TN, acc_shape=(GW, D_MODEL // N_CHIPS),
                       k_axis=1, sem=("parallel", "arbitrary"))
    part.update(w_conv_out=d_w_conv_out, w_attn_out=d_w_attn_out, w_o=d_w_o)
    dmix, got = _mm("d_mix", dya, w_attn_out, grid=(T // 1024, N_CHIPS),
                    a_spec=pl.BlockSpec((1024, D_MODEL // N_CHIPS), lambda m, s: (m, s)),
                    b_spec=pl.BlockSpec((None, GW, D_MODEL // N_CHIPS), lambda m, s: (s, 0, 0)),
                    o_spec=pl.BlockSpec((1024, GW), lambda m, s: (m, 0)), o_shape=(T, GW), o_dtype=bf16, dims=NT,
                    acc_shape=(1024, GW), k_axis=1, comm=_x_pair_exchange([part[n] for n in REDUCE_B]))
    sums_b = pair_sums(REDUCE_B, got)
    dmix3 = dmix.reshape(B, SEQ, GW)
    delta3, delta_r1, delta_r2, dmix_r1, dmix_r2 = _attn_delta(dmix3, mix3)
    one = (B, 1, SEQ)
    (dq0, dk0, dv0), got_b = _attn_bwd(0, *qkv_args(0), dmix3.reshape(one + (GW,)), lse3.reshape(one + (LANES,)),
                                       delta3.reshape(one + (LANES,)), comm=_x_chip_exchange(sums_b))
    dqkv = [tuple(t.reshape(B, SEQ, GW) for t in (dq0, dk0, dv0)),
            _attn_bwd(1, *qkv_args(1), dmix_r1, lse_r1, delta_r1), _attn_bwd(2, *qkv_args(2), dmix_r2, lse_r2, delta_r2)]
    dproj = _assemble_dproj(dqkv, dua, dub, dga.reshape(B, SEQ, D_MODEL), dgc.reshape(B, SEQ, D_MODEL)).reshape(T, IN_W)

    d_w_in = _mm("d_w_in", h_t, dproj, grid=(N_CHIPS, T // 1024),
                 a_spec=pl.BlockSpec((D_MODEL, 1024), lambda s, k: (0, k)),
                 b_spec=pl.BlockSpec((1024, IN_S), lambda s, k: (k, s)),
                 o_spec=pl.BlockSpec((None, D_MODEL, IN_S), lambda s, k: (s, 0, 0)), o_shape=(N_CHIPS, D_MODEL, IN_S),
                 o_dtype=bf16, dims=NN, acc_shape=(D_MODEL, IN_S), k_axis=1, sem=("parallel", "arbitrary"))
    part.update(w_in=d_w_in)
    sums_c = pair_sums(REDUCE_C, _run_exchange(_x_pair_exchange([d_w_in]), "grad_pair_exchange_c"))
    (dx, d_g1), got_c = _mm_rms_bwd("d_h", [(dproj, w_in)], xf, norm1_g, dx1, tm=512,
                                    a_spec=pl.BlockSpec((512, IN_S), lambda m, s: (m, s)),
                                    b_spec=pl.BlockSpec((None, D_MODEL, IN_S), lambda m, s: (s, 0, 0)), dims=NT,
                                    comm=_x_chip_exchange(sums_c))

    names = REDUCE_A + REDUCE_B + REDUCE_C
    sums = dict(zip(names, sums_a + sums_b + sums_c))
    halves = {}
    for ns, gs in _same_shape_groups(names, got_a + got_b + got_c):
        halves.update(zip(ns, _sum4([sums[n] for n in ns], gs, chip1, core, "chip_sum_" + ns[0])))
    big = {n: halves[n] for n in names}
    small = dict(norm1_g=d_g1, gate_b=jnp.concatenate([d_gba, d_gbc], axis=-1), conv_b=d_conv_b, conv_ln_g=d_ln_g,
                 conv_ln_b=d_ln_b, norm2_g=d_g2, norm_f_g=d_gf, conv_w=d_conv_w)
    return loss, dx.reshape(B, SEQ, D_MODEL), big, small


BIG = ("w_in", "w_conv_out", "w_attn_out", "w_o", "w_ffn_gate", "w_ffn_up", "w_ffn_down")
TRANSPOSED = ("w_ffn_gate", "w_ffn_up")
SMALL = ("norm1_g", "gate_b", "conv_b", "conv_ln_g", "conv_ln_b", "norm2_g", "norm_f_g")
SMALL_ROWS = {"norm1_g": 8, "gate_b": 16, "conv_b": 8, "conv_ln_g": 8, "conv_ln_b": 8, "norm2_g": 8, "norm_f_g": 8}
LOSS_ROWS = 8
CONVW_ROWS = 32 * D_MODEL // LANES


def kernel(x, norm1_g, w_in, gate_b, conv_w, conv_b, conv_ln_g, conv_ln_b, w_conv_out, w_attn_out, w_o, norm2_g, w_ffn_gate, w_ffn_up, w_ffn_down, norm_f_g, loss_target, m_norm1_g, m_w_in, m_gate_b, m_conv_w, m_conv_b, m_conv_ln_g, m_conv_ln_b, m_w_conv_out, m_w_attn_out, m_w_o, m_norm2_g, m_w_ffn_gate, m_w_ffn_up, m_w_ffn_down, m_norm_f_g, v_norm1_g, v_w_in, v_gate_b, v_conv_w, v_conv_b, v_conv_ln_g, v_conv_ln_b, v_w_conv_out, v_w_attn_out, v_w_o, v_norm2_g, v_w_ffn_gate, v_w_ffn_up, v_w_ffn_down, v_norm_f_g):
    W = dict(norm1_g=norm1_g, w_in=w_in, gate_b=gate_b, conv_w=conv_w, conv_b=conv_b, conv_ln_g=conv_ln_g, conv_ln_b=conv_ln_b,
             w_conv_out=w_conv_out, w_attn_out=w_attn_out, w_o=w_o, norm2_g=norm2_g, w_ffn_gate=w_ffn_gate, w_ffn_up=w_ffn_up,
             w_ffn_down=w_ffn_down, norm_f_g=norm_f_g)
    M = dict(norm1_g=m_norm1_g, w_in=m_w_in, gate_b=m_gate_b, conv_w=m_conv_w, conv_b=m_conv_b, conv_ln_g=m_conv_ln_g,
             conv_ln_b=m_conv_ln_b, w_conv_out=m_w_conv_out, w_attn_out=m_w_attn_out, w_o=m_w_o, norm2_g=m_norm2_g,
             w_ffn_gate=m_w_ffn_gate, w_ffn_up=m_w_ffn_up, w_ffn_down=m_w_ffn_down, norm_f_g=m_norm_f_g)
    V = dict(norm1_g=v_norm1_g, w_in=v_w_in, gate_b=v_gate_b, conv_w=v_conv_w, conv_b=v_conv_b, conv_ln_g=v_conv_ln_g,
             conv_ln_b=v_conv_ln_b, w_conv_out=v_w_conv_out, w_attn_out=v_w_attn_out, w_o=v_w_o, norm2_g=v_norm2_g,
             w_ffn_gate=v_w_ffn_gate, w_ffn_up=v_w_ffn_up, w_ffn_down=v_w_ffn_down, norm_f_g=v_norm_f_g)
    order = list(W)

    def as2d(n, a):
        a = a.reshape(a.shape[-2:])
        return a.T if n in TRANSPOSED else a

    def from2d(n, a):
        return (a.T if n in TRANSPOSED else a).reshape(W[n].shape)

    shard2d = {n: as2d(n, W[n]) for n in BIG}
    chip = 2 * lax.axis_index("x") + lax.axis_index("y")

    core = lax.axis_index("c").astype(jnp.int32).reshape(1)
    chip1 = chip.astype(jnp.int32).reshape(1)
    shards = {n: shard2d[n].astype(bf16) for n in BIG}

    def slot_for(s):
        return lax.dynamic_update_slice(lax.empty((N_CHIPS,) + s.shape, s.dtype), s[None], (chip, 0, 0))

    slots = {n: slot_for(s) for n, s in shards.items()}
    cw = jnp.pad(conv_w.reshape(CONV_K, D_MODEL // N_CHIPS), ((0, CONV_HALO - CONV_K), (0, 0)))
    w_in_full, cw4 = _allgather_weights([shards["w_in"], cw], [slots["w_in"], slot_for(cw)])
    conv_w_full = cw4.transpose(1, 0, 2).reshape(CONV_HALO, D_MODEL)[:CONV_K]

    loss, grad_x, grads, gsmall = _step(x, loss_target, norm1_g, gate_b, conv_w_full, conv_b, conv_ln_g, conv_ln_b, norm2_g,
                                        norm_f_g, w_in_full, shards, slots, chip1, core)

    pieces = [_rows128(loss, LOSS_ROWS)] + [_rows128(gsmall[n], SMALL_ROWS[n]) for n in SMALL] + [_rows128(gsmall["conv_w"], CONVW_ROWS)]
    full_rows, tot = _final_exchange([grads[n] for n in BIG], jnp.concatenate(pieces, axis=0))
    grads = dict(zip(BIG, full_rows))
    loss_out = tot[0, 0]
    row = LOSS_ROWS
    for n in SMALL:
        grads[n] = tot[row: row + W[n].size // LANES].reshape(W[n].shape)
        row += SMALL_ROWS[n]
    dcw = tot[row: row + CONV_K * D_MODEL // LANES].reshape(CONV_K, D_MODEL)
    grads["conv_w"] = lax.dynamic_slice(dcw, (0, chip * (D_MODEL // N_CHIPS)), (CONV_K, D_MODEL // N_CHIPS))

    delta, new_m, new_v = {}, {}, {}
    for ns, ws in _same_shape_groups(BIG, [shard2d[n] for n in BIG]):
        ds, nms, nvs = _adamw(ws, [grads[n] for n in ns], [as2d(n, M[n]) for n in ns], [as2d(n, V[n]) for n in ns], "adamw_" + ns[0])
        for n, d, nm, nv in zip(ns, ds, nms, nvs):
            delta[n], new_m[n], new_v[n], grads[n] = (from2d(n, t) for t in (d, nm, nv, grads[n]))

    def pack(src):
        return jnp.concatenate([_rows128(src[n], SMALL_ROWS[n]) for n in SMALL], axis=0)

    (d,), (nm,), (nv,) = _adamw([pack(W)], [pack(grads)], [pack(M)], [pack(V)], "adamw_small")
    row = 0
    for n in SMALL:
        k = W[n].size // LANES
        delta[n], new_m[n], new_v[n] = (t[row: row + k].reshape(W[n].shape) for t in (d, nm, nv))
        row += SMALL_ROWS[n]

    def pad32(a):
        return jnp.pad(a.reshape(CONV_K, D_MODEL // N_CHIPS), ((0, 1), (0, 0)))

    (d,), (nm,), (nv,) = _adamw([pad32(conv_w)], [pad32(grads["conv_w"])], [pad32(m_conv_w)], [pad32(v_conv_w)], "adamw_conv_w")
    delta["conv_w"], new_m["conv_w"], new_v["conv_w"] = (t[:CONV_K].reshape(conv_w.shape) for t in (d, nm, nv))
    grads["conv_w"] = grads["conv_w"].reshape(conv_w.shape)

    return (loss_out, grad_x, *[grads[n] for n in order], *[delta[n] for n in order],
            *[new_m[n] for n in order], *[new_v[n] for n in order])
```

```python
import functools
import math
from typing import Callable, NamedTuple

import numpy as np
import jax
import jax.numpy as jnp
from jax import lax
from jax.experimental import pallas as pl
from jax.experimental.pallas import tpu as pltpu

f32 = jnp.float32
bf16 = jnp.bfloat16
SDS = jax.ShapeDtypeStruct
MESH = pl.DeviceIdType.MESH

D_MODEL = 1024
SEQ = 2048
HEAD_DIM = 64
HEADS = 8
GROUPS = ((128, 1), (512, 4), (2048, 16))
GW = HEADS * HEAD_DIM
ATTN_W = len(GROUPS) * GW
Q_BLOCK = 128
CONV_K = 31
D_FF = 2816
IN_W = 3 * ATTN_W + 2 * D_MODEL + 2 * D_MODEL
N_CHIPS = 4
IN_S = IN_W // N_CHIPS
FF_S = D_FF // N_CHIPS
RMS_EPS = 1e-6
LN_EPS = 1e-5
LR, B1, B2, ADAM_EPS, WD, STEP = 0.001, 0.9, 0.999, 1e-08, 0.01, 10
NEG = -1e30
LANES = 128
VMEM_LIMIT = 48 * 2 ** 20
VMEM_LIMIT_BIG = 52 * 2 ** 20
CB = 512
UA_CB, UB_CB, GA_CB, GC_CB = 9, 11, 13, 15


def _alibi_slope_list(n):
    def pow2(m):
        start = 2.0 ** (-8.0 / m)
        return [start ** (i + 1) for i in range(m)]
    if math.log2(n).is_integer():
        return pow2(n)
    c = 2 ** math.floor(math.log2(n))
    return pow2(c) + _alibi_slope_list(2 * c)[0::2][: n - c]


_SLOPES = np.asarray(sorted(_alibi_slope_list(len(GROUPS) * HEADS), reverse=True), dtype=np.float32).reshape(len(GROUPS), HEADS)


def _cp(sem=None, vmem=VMEM_LIMIT):
    return pltpu.CompilerParams(dimension_semantics=sem, vmem_limit_bytes=vmem)


def _sigmoid(x):
    return 1.0 / (1.0 + jnp.exp(-x))


HBM = pl.BlockSpec(memory_space=pl.ANY)


class _Exchange(NamedTuple):
    ins: list
    out_shape: list
    scratch: list
    aliases: dict
    start: Callable
    wait: Callable


def _launch(body, *, name, grid, in_specs, out_specs, out_shape, args, scratch_shapes=(), sem=None, comm=None, vmem=VMEM_LIMIT):
    if comm is None:
        return pl.pallas_call(body, grid=grid, in_specs=in_specs, out_specs=out_specs, out_shape=out_shape,
                              scratch_shapes=list(scratch_shapes), name=name, compiler_params=_cp(sem, vmem))(*args)
    multi = isinstance(out_shape, (tuple, list))
    m_out = list(out_shape) if multi else [out_shape]
    m_ospec = list(out_specs) if multi else [out_specs]
    n_in, n_out, n_scr = len(in_specs), len(m_out), len(scratch_shapes)
    nc_in, nc_out = len(comm.ins), len(comm.out_shape)

    def hosted(*refs):
        bounds = np.cumsum([0, n_in, nc_in, n_out, nc_out, n_scr])
        mi, ci, mo, co, ms = (refs[a:b] for a, b in zip(bounds[:-1], bounds[1:]))
        cs = refs[bounds[-1]:]
        ids = [pl.program_id(a) for a in range(len(grid))]
        first = functools.reduce(jnp.logical_and, [i == 0 for i in ids])
        last = functools.reduce(jnp.logical_and, [i == g - 1 for i, g in zip(ids, grid)])

        @pl.when(first)
        def _():
            comm.start(ci, co, cs)

        body(*mi, *mo, *ms)

        @pl.when(last)
        def _():
            comm.wait(ci, co, cs)

    res = pl.pallas_call(
        hosted, grid=grid, in_specs=list(in_specs) + [HBM] * nc_in, out_specs=m_ospec + [HBM] * nc_out,
        out_shape=tuple(m_out) + tuple(comm.out_shape), scratch_shapes=list(scratch_shapes) + list(comm.scratch),
        input_output_aliases={n_in + i: n_out + o for i, o in comm.aliases.items()}, name=name + "_comm",
        compiler_params=pltpu.CompilerParams(dimension_semantics=("arbitrary",) * len(grid), vmem_limit_bytes=vmem,
                                             has_side_effects=True))(*args, *comm.ins)
    return (tuple(res[:n_out]) if multi else res[0]), tuple(res[n_out:])


def _run_exchange(ex, name):
    n_in = len(ex.ins)

    def body(*refs):
        ins, outs, sems = refs[:n_in], refs[n_in:n_in + len(ex.out_shape)], refs[n_in + len(ex.out_shape):]
        ex.start(ins, outs, sems)
        ex.wait(ins, outs, sems)

    return pl.pallas_call(body, in_specs=[HBM] * n_in, out_specs=[HBM] * len(ex.out_shape), out_shape=tuple(ex.out_shape),
                          scratch_shapes=list(ex.scratch), input_output_aliases=dict(ex.aliases), name=name,
                          compiler_params=pltpu.CompilerParams(has_side_effects=True))(*ex.ins)


def _rmsnorm(xv, gv):
    return xv * lax.rsqrt(jnp.mean(xv * xv, axis=-1, keepdims=True) + RMS_EPS) * gv


def _rms_fwd(x, g, name):
    T = x.shape[0]
    tm = 512

    def body(x_ref, g_ref, o_ref, ot_ref):
        hv = _rmsnorm(x_ref[...], g_ref[...])
        o_ref[...] = hv.astype(bf16)
        ot_ref[...] = hv.T.astype(bf16)

    row = pl.BlockSpec((tm, D_MODEL), lambda i: (i, 0))
    vec = pl.BlockSpec((1, D_MODEL), lambda i: (0, 0))
    return pl.pallas_call(body, grid=(T // tm,), in_specs=[row, vec], out_specs=[row, pl.BlockSpec((D_MODEL, tm), lambda i: (0, i))],
                          out_shape=(SDS((T, D_MODEL), bf16), SDS((D_MODEL, T), bf16)), name=name,
                          compiler_params=_cp(("parallel",)))(x, g)


def _rms_bwd_tile(dyv, xv, gv, dres):
    r = lax.rsqrt(jnp.mean(xv * xv, axis=-1, keepdims=True) + RMS_EPS)
    xh = xv * r
    dxh = dyv * gv
    dx = dres + r * (dxh - xh * jnp.mean(dxh * xh, axis=-1, keepdims=True))
    return dx, jnp.sum(dyv * xh, axis=0, keepdims=True)


def _accumulate(first, refs_parts):
    @pl.when(first)
    def _():
        for ref, part in refs_parts:
            ref[...] = part

    @pl.when(jnp.logical_not(first))
    def _():
        for ref, part in refs_parts:
            ref[...] += part


def _mm_rms_bwd(name, ops, x, g, dres, *, tm, a_spec, b_spec, dims, comm=None, vmem=VMEM_LIMIT):
    T = x.shape[0]
    n = len(ops)

    def body(*refs):
        ab_refs = refs[:2 * n]
        x_ref, g_ref, r_ref, dx_ref, dg_ref, acc = refs[2 * n:]
        m, s = pl.program_id(0), pl.program_id(1)
        p = sum(lax.dot_general(ab_refs[2 * i][...], ab_refs[2 * i + 1][...], dims, preferred_element_type=f32) for i in range(n))

        @pl.when(s == 0)
        def _():
            acc[...] = p

        @pl.when(s > 0)
        def _():
            acc[...] += p

        @pl.when(s == N_CHIPS - 1)
        def _():
            dx, part = _rms_bwd_tile(acc[...], x_ref[...], g_ref[...], r_ref[...])
            dx_ref[...] = dx
            _accumulate(m == 0, [(dg_ref, part)])

    row = pl.BlockSpec((tm, D_MODEL), lambda m, s: (m, 0))
    vec = pl.BlockSpec((1, D_MODEL), lambda m, s: (0, 0))
    return _launch(body, name=name, grid=(T // tm, N_CHIPS), in_specs=[a_spec, b_spec] * n + [row, vec, row], out_specs=[row, vec],
                   out_shape=(SDS((T, D_MODEL), f32), SDS((1, D_MODEL), f32)), args=tuple(t for ab in ops for t in ab) + (x, g, dres),
                   scratch_shapes=[pltpu.VMEM((tm, D_MODEL), f32)], sem=("arbitrary", "arbitrary"), comm=comm, vmem=vmem)


def _ffn_down_loss(ff, wd, x1, gf, target):
    T = x1.shape[0]
    tm = 1024
    nm = T // tm

    def body(f_ref, w_ref, x_ref, g_ref, t_ref, loss_ref, dx_ref, dg_ref, acc):
        m, s = pl.program_id(0), pl.program_id(1)
        p = jnp.dot(f_ref[...], w_ref[...], preferred_element_type=f32)

        @pl.when(s == 0)
        def _():
            acc[...] = p

        @pl.when(s > 0)
        def _():
            acc[...] += p

        @pl.when(s == N_CHIPS - 1)
        def _():
            xv = acc[...] + x_ref[...]
            gv = g_ref[...]
            r = lax.rsqrt(jnp.mean(xv * xv, axis=-1, keepdims=True) + RMS_EPS)
            xh = xv * r
            e = xh * gv - t_ref[...]
            part_l = jnp.broadcast_to(0.5 * jnp.sum(jnp.mean(e * e, axis=-1, keepdims=True), axis=0, keepdims=True), (1, LANES))
            dy = e * (1.0 / D_MODEL)
            dxh = dy * gv
            dx_ref[...] = r * (dxh - xh * jnp.mean(dxh * xh, axis=-1, keepdims=True))
            part_g = jnp.sum(dy * xh, axis=0, keepdims=True)

            @pl.when(m == 0)
            def _():
                loss_ref[...] = part_l
                dg_ref[...] = part_g

            @pl.when(m > 0)
            def _():
                loss_ref[...] += part_l
                dg_ref[...] += part_g

    row = pl.BlockSpec((tm, D_MODEL), lambda m, s: (m, 0))
    vec = pl.BlockSpec((1, D_MODEL), lambda m, s: (0, 0))
    return pl.pallas_call(
        body, grid=(nm, N_CHIPS),
        in_specs=[pl.BlockSpec((None, tm, FF_S), lambda m, s: (s, m, 0)), pl.BlockSpec((None, FF_S, D_MODEL), lambda m, s: (s, 0, 0)),
                  row, vec, row],
        out_specs=[pl.BlockSpec((1, LANES), lambda m, s: (0, 0)), row, vec],
        out_shape=(SDS((1, LANES), f32), SDS((T, D_MODEL), f32), SDS((1, D_MODEL), f32)),
        scratch_shapes=[pltpu.VMEM((tm, D_MODEL), f32)], name="ffn_down_loss", compiler_params=_cp(("arbitrary", "arbitrary")))(
            ff, wd, x1, gf, target)


def _merge_o_proj_rms(proj, gate_b, ya, yc, w_o, x, g2):
    T = x.shape[0]
    tm = 512

    def body(ga0, ga1, gc0, gc1, b_ref, ya_ref, yc_ref, w_ref, x_ref, g_ref, m_ref, x1_ref, h2_ref):
        bv = b_ref[...]
        ga = jnp.concatenate([ga0[...], ga1[...]], axis=-1).astype(f32)
        gc = jnp.concatenate([gc0[...], gc1[...]], axis=-1).astype(f32)
        merged = (_sigmoid(ga + bv[:, :D_MODEL]) * ya_ref[...].astype(f32)
                  + _sigmoid(gc + bv[:, D_MODEL:]) * yc_ref[...].astype(f32)).astype(bf16)
        m_ref[...] = merged
        x1 = jnp.dot(merged, w_ref[...], preferred_element_type=f32) + x_ref[...]
        x1_ref[...] = x1
        h2_ref[...] = _rmsnorm(x1, g_ref[...]).astype(bf16)

    row = pl.BlockSpec((tm, D_MODEL), lambda i: (i, 0))
    return pl.pallas_call(
        body, grid=(T // tm,),
        in_specs=[pl.BlockSpec((tm, CB), lambda i, cb=cb: (i, cb)) for cb in (GA_CB, GA_CB + 1, GC_CB, GC_CB + 1)]
        + [pl.BlockSpec((1, 2 * D_MODEL), lambda i: (0, 0)), row, row, pl.BlockSpec((D_MODEL, D_MODEL), lambda i: (0, 0)), row,
           pl.BlockSpec((1, D_MODEL), lambda i: (0, 0))],
        out_specs=[row, row, row], out_shape=(SDS((T, D_MODEL), bf16), SDS((T, D_MODEL), f32), SDS((T, D_MODEL), bf16)),
        name="merge_o_proj_rms", compiler_params=_cp(("parallel",)))(proj, proj, proj, proj, gate_b, ya, yc, w_o, x, g2)


def _conv_out_fwd(c1, g, b, w, comm=None):
    T = c1.shape[0]
    tm = 1024

    def body(c_ref, g_ref, b_ref, w_ref, c3_ref, y_ref):
        cv = c_ref[...]
        mu = jnp.mean(cv, axis=-1, keepdims=True)
        cc = cv - mu
        var = jnp.mean(cc * cc, axis=-1, keepdims=True)
        c2 = cc * lax.rsqrt(var + LN_EPS) * g_ref[...] + b_ref[...]
        c3 = (c2 * _sigmoid(c2)).astype(bf16)
        c3_ref[...] = c3
        y_ref[...] = jnp.dot(c3, w_ref[...], preferred_element_type=f32).astype(bf16)

    row = pl.BlockSpec((tm, D_MODEL), lambda i: (i, 0))
    vec = pl.BlockSpec((1, D_MODEL), lambda i: (0, 0))
    sd = SDS((T, D_MODEL), bf16)
    return _launch(body, name="conv_out_fwd", grid=(T // tm,), in_specs=[row, vec, vec, pl.BlockSpec((D_MODEL, D_MODEL), lambda i: (0, 0))],
                   out_specs=[row, row], out_shape=(sd, sd), args=(c1, g, b, w), sem=("parallel",), comm=comm)


def _conv_out_bwd(dyc, w, c1, g, b):
    T = c1.shape[0]
    tm = 1024

    def body(d_ref, w_ref, c_ref, g_ref, b_ref, dc_ref, dg_ref, db_ref):
        dc3 = lax.dot_general(d_ref[...], w_ref[...], NT, preferred_element_type=f32)
        cv = c_ref[...]
        gv = g_ref[...]
        mu = jnp.mean(cv, axis=-1, keepdims=True)
        cc = cv - mu
        var = jnp.mean(cc * cc, axis=-1, keepdims=True)
        rs = lax.rsqrt(var + LN_EPS)
        xh = cc * rs
        c2 = xh * gv + b_ref[...]
        sg = _sigmoid(c2)
        dc2 = dc3 * (sg * (1.0 + c2 * (1.0 - sg)))
        dxh = dc2 * gv
        dc_ref[...] = rs * (dxh - jnp.mean(dxh, axis=-1, keepdims=True) - xh * jnp.mean(dxh * xh, axis=-1, keepdims=True))
        _accumulate(pl.program_id(0) == 0, [(dg_ref, jnp.sum(dc2 * xh, axis=0, keepdims=True)), (db_ref, jnp.sum(dc2, axis=0, keepdims=True))])

    row = pl.BlockSpec((tm, D_MODEL), lambda i: (i, 0))
    vec = pl.BlockSpec((1, D_MODEL), lambda i: (0, 0))
    return pl.pallas_call(body, grid=(T // tm,), in_specs=[row, pl.BlockSpec((D_MODEL, D_MODEL), lambda i: (0, 0)), row, vec, vec],
                          out_specs=[row, vec, vec],
                          out_shape=(SDS((T, D_MODEL), f32), SDS((1, D_MODEL), f32), SDS((1, D_MODEL), f32)),
                          name="conv_out_bwd", compiler_params=_cp(("arbitrary",)))(dyc, w, c1, g, b)


NN = (((1,), (0,)), ((), ()))
NT = (((1,), (1,)), ((), ()))
TN = (((0,), (0,)), ((), ()))


def _mm(name, a, b, *, grid, a_spec, b_spec, o_spec, o_shape, o_dtype, dims, acc_shape=None, k_axis=None,
        res=None, res_spec=None, sem=None, comm=None):
    nk = 1 if k_axis is None else grid[k_axis]

    def body(*refs):
        if res is None:
            a_ref, b_ref, o_ref = refs[:3]
            r_ref, scr = None, refs[3:]
        else:
            a_ref, b_ref, r_ref, o_ref = refs[:4]
            scr = refs[4:]
        p = lax.dot_general(a_ref[...].astype(bf16), b_ref[...].astype(bf16), dims, preferred_element_type=f32)
        if nk == 1:
            if r_ref is not None:
                p = p + r_ref[...]
            o_ref[...] = p.astype(o_dtype)
            return
        acc = scr[0]
        k = pl.program_id(k_axis)

        @pl.when(k == 0)
        def _():
            acc[...] = p

        @pl.when(k > 0)
        def _():
            acc[...] += p

        @pl.when(k == nk - 1)
        def _():
            t = acc[...]
            if r_ref is not None:
                t = t + r_ref[...]
            o_ref[...] = t.astype(o_dtype)

    ins = [a, b] + ([] if res is None else [res])
    specs = [a_spec, b_spec] + ([] if res is None else [res_spec])
    scratch = [] if nk == 1 else [pltpu.VMEM(acc_shape, f32)]
    return _launch(body, name=name, grid=grid, in_specs=specs, out_specs=o_spec, out_shape=SDS(o_shape, o_dtype),
                   args=ins, scratch_shapes=scratch, sem=sem, comm=comm)


def _mm_nn_full(name, a, b, o_dtype, res=None, tm=1024, comm=None):
    T, K = a.shape
    N = b.shape[1]
    return _mm(name, a, b, grid=(T // tm,), a_spec=pl.BlockSpec((tm, K), lambda i: (i, 0)),
               b_spec=pl.BlockSpec((K, N), lambda i: (0, 0)), o_spec=pl.BlockSpec((tm, N), lambda i: (i, 0)),
               o_shape=(T, N), o_dtype=o_dtype, dims=NN, res=res,
               res_spec=None if res is None else pl.BlockSpec((tm, N), lambda i: (i, 0)), sem=("parallel",), comm=comm)


def _mm_nt_full(name, a, b, o_dtype, tm=1024):
    T, N = a.shape
    K = b.shape[0]
    return _mm(name, a, b, grid=(T // tm,), a_spec=pl.BlockSpec((tm, N), lambda i: (i, 0)),
               b_spec=pl.BlockSpec((K, N), lambda i: (0, 0)), o_spec=pl.BlockSpec((tm, K), lambda i: (i, 0)),
               o_shape=(T, K), o_dtype=o_dtype, dims=NT, sem=("parallel",))


def _mm_tn_tokens(name, a, b, o_dtype, tk=1024):
    T, K = a.shape
    N = b.shape[1]
    return _mm(name, a, b, grid=(T // tk,), a_spec=pl.BlockSpec((tk, K), lambda k: (k, 0)),
               b_spec=pl.BlockSpec((tk, N), lambda k: (k, 0)), o_spec=pl.BlockSpec((K, N), lambda k: (0, 0)),
               o_shape=(K, N), o_dtype=o_dtype, dims=TN, acc_shape=(K, N), k_axis=0, sem=("arbitrary",))


AO_S = D_MODEL // N_CHIPS


def _attn_out_cat(w_ref):
    return jnp.concatenate([w_ref[s] for s in range(N_CHIPS)], axis=-1)


def _attn_out_fwd(mix, w4):
    T = mix.shape[0]
    tm = 1024

    def body(a_ref, w_ref, o_ref):
        o_ref[...] = jnp.dot(a_ref[...], _attn_out_cat(w_ref), preferred_element_type=f32).astype(bf16)

    return pl.pallas_call(body, grid=(T // tm,),
                          in_specs=[pl.BlockSpec((tm, GW), lambda i: (i, 0)), pl.BlockSpec((N_CHIPS, GW, AO_S), lambda i: (0, 0, 0))],
                          out_specs=pl.BlockSpec((tm, D_MODEL), lambda i: (i, 0)), out_shape=SDS((T, D_MODEL), bf16), name="attn_out",
                          compiler_params=_cp(("parallel",)))(mix, w4)


def _attn_out_bwd_x(dya, w4, comm=None):
    T = dya.shape[0]
    tm = 1024

    def body(d_ref, w_ref, o_ref):
        o_ref[...] = lax.dot_general(d_ref[...], _attn_out_cat(w_ref), NT, preferred_element_type=f32).astype(bf16)

    return _launch(body, name="d_mix", grid=(T // tm,),
                   in_specs=[pl.BlockSpec((tm, D_MODEL), lambda i: (i, 0)), pl.BlockSpec((N_CHIPS, GW, AO_S), lambda i: (0, 0, 0))],
                   out_specs=pl.BlockSpec((tm, GW), lambda i: (i, 0)), out_shape=SDS((T, GW), bf16), args=(dya, w4),
                   sem=("parallel",), comm=comm)


def _attn_out_bwd_w(mix, dya):
    T = mix.shape[0]
    tk = 1024
    nk = T // tk

    def body(a_ref, d_ref, o_ref, acc):
        k = pl.program_id(0)
        p = lax.dot_general(a_ref[...], d_ref[...], TN, preferred_element_type=f32)

        @pl.when(k == 0)
        def _():
            acc[...] = p

        @pl.when(k > 0)
        def _():
            acc[...] += p

        @pl.when(k == nk - 1)
        def _():
            for s in range(N_CHIPS):
                o_ref[s] = acc[:, s * AO_S:(s + 1) * AO_S].astype(bf16)

    return pl.pallas_call(body, grid=(nk,),
                          in_specs=[pl.BlockSpec((tk, GW), lambda k: (k, 0)), pl.BlockSpec((tk, D_MODEL), lambda k: (k, 0))],
                          out_specs=pl.BlockSpec((N_CHIPS, GW, AO_S), lambda k: (0, 0, 0)), out_shape=SDS((N_CHIPS, GW, AO_S), bf16),
                          scratch_shapes=[pltpu.VMEM((GW, D_MODEL), f32)], name="d_w_attn_out",
                          compiler_params=_cp(("arbitrary",)))(mix, dya)


def _ffn_up(h2, wg, wu):
    T = h2.shape[0]
    tm = 1024

    def body(h_ref, wg_ref, wu_ref, a_ref, b_ref, f_ref):
        hv = h_ref[...]
        av = lax.dot_general(hv, wg_ref[...], NT, preferred_element_type=f32)
        bv = lax.dot_general(hv, wu_ref[...], NT, preferred_element_type=f32)
        a_ref[...] = av.astype(bf16)
        b_ref[...] = bv.astype(bf16)
        f_ref[...] = (av * _sigmoid(av) * bv).astype(bf16)

    wspec = pl.BlockSpec((None, FF_S, D_MODEL), lambda s, m: (s, 0, 0))
    ospec = pl.BlockSpec((None, tm, FF_S), lambda s, m: (s, m, 0))
    osd = SDS((N_CHIPS, T, FF_S), bf16)
    return pl.pallas_call(body, grid=(N_CHIPS, T // tm),
                          in_specs=[pl.BlockSpec((tm, D_MODEL), lambda s, m: (m, 0)), wspec, wspec],
                          out_specs=[ospec, ospec, ospec], out_shape=(osd, osd, osd), name="ffn_up",
                          compiler_params=_cp(("parallel", "parallel")))(h2, wg, wu)


def _ffn_down_bwd(dx2, wd, a, b):
    T = dx2.shape[0]
    tm = 1024

    def body(d_ref, w_ref, a_ref, b_ref, da_ref, db_ref):
        df = lax.dot_general(d_ref[...].astype(bf16), w_ref[...], NT, preferred_element_type=f32)
        av = a_ref[...].astype(f32)
        sg = _sigmoid(av)
        da_ref[...] = (df * b_ref[...].astype(f32) * (sg * (1.0 + av * (1.0 - sg)))).astype(bf16)
        db_ref[...] = (df * av * sg).astype(bf16)

    aspec = pl.BlockSpec((None, tm, FF_S), lambda m, s: (s, m, 0))
    osd = SDS((N_CHIPS, T, FF_S), bf16)
    return pl.pallas_call(body, grid=(T // tm, N_CHIPS),
                          in_specs=[pl.BlockSpec((tm, D_MODEL), lambda m, s: (m, 0)),
                                    pl.BlockSpec((None, FF_S, D_MODEL), lambda m, s: (s, 0, 0)), aspec, aspec],
                          out_specs=[aspec, aspec], out_shape=(osd, osd), name="ffn_down_bwd",
                          compiler_params=_cp(("parallel", "parallel")))(dx2, wd, a, b)


def _merge_bwd(dm, proj, gate_b, ya, yc):
    T = proj.shape[0]
    tm = 512

    def body(dm_ref, ga_ref, gc_ref, ba_ref, bc_ref, ya_ref, yc_ref, dya_ref, dyc_ref, dga_ref, dgc_ref, dba_ref, dbc_ref):
        dmv = dm_ref[...].astype(f32)
        sa = _sigmoid(ga_ref[...].astype(f32) + ba_ref[...])
        sc = _sigmoid(gc_ref[...].astype(f32) + bc_ref[...])
        dya_ref[...] = (dmv * sa).astype(bf16)
        dyc_ref[...] = (dmv * sc).astype(bf16)
        dga = dmv * ya_ref[...].astype(f32) * sa * (1.0 - sa)
        dgc = dmv * yc_ref[...].astype(f32) * sc * (1.0 - sc)
        dga_ref[...] = dga.astype(bf16)
        dgc_ref[...] = dgc.astype(bf16)
        pa = jnp.sum(dga, axis=0, keepdims=True)
        pc = jnp.sum(dgc, axis=0, keepdims=True)

        @pl.when(pl.program_id(1) == 0)
        def _():
            dba_ref[...] = pa
            dbc_ref[...] = pc

        @pl.when(pl.program_id(1) > 0)
        def _():
            dba_ref[...] += pa
            dbc_ref[...] += pc

    blk = pl.BlockSpec((tm, CB), lambda j, i: (i, j))
    vec = pl.BlockSpec((1, CB), lambda j, i: (0, j))
    big = SDS((T, D_MODEL), bf16)
    small = SDS((1, D_MODEL), f32)
    return pl.pallas_call(
        body, grid=(2, T // tm),
        in_specs=[blk, pl.BlockSpec((tm, CB), lambda j, i: (i, GA_CB + j)), pl.BlockSpec((tm, CB), lambda j, i: (i, GC_CB + j)),
                  vec, pl.BlockSpec((1, CB), lambda j, i: (0, 2 + j)), blk, blk],
        out_specs=[blk, blk, blk, blk, vec, vec], out_shape=(big, big, big, big, small, small), name="merge_bwd",
        compiler_params=_cp(("parallel", "arbitrary")))(dm, proj, proj, gate_b, gate_b, ya, yc)


CONV_TS = 512
CONV_HALO = 32
CONV_RC = 64
CONV_WIN = CONV_TS + CONV_HALO
SUBLANES = 8


def _fill_shifted(win, sh):
    for b in range(1, SUBLANES):
        sh[b - 1] = win[pl.ds(b, CONV_WIN - SUBLANES), :]


def _rows_at(win, sh, row):
    a, b = divmod(row, SUBLANES)
    if b == 0:
        return win[pl.ds(row, CONV_RC), :]
    return sh[b - 1, pl.ds(a * SUBLANES, CONV_RC), :]


def _glu_conv_fwd(proj3, w, bias, comm=None):
    B = proj3.shape[0]
    nt = SEQ // CONV_TS
    hb = CONV_TS // CONV_HALO

    def body(ua_ref, ub_ref, ha_ref, hb_ref, w_ref, b_ref, o_ref, win, sh):
        i = pl.program_id(2)
        c0 = ua_ref[...].astype(f32) * _sigmoid(ub_ref[...].astype(f32))
        halo = ha_ref[...].astype(f32) * _sigmoid(hb_ref[...].astype(f32))
        win[0:CONV_HALO, :] = jnp.where(i > 0, halo, 0.0)
        win[CONV_HALO:, :] = c0
        _fill_shifted(win, sh)
        for r0 in range(0, CONV_TS, CONV_RC):
            acc = jnp.zeros((CONV_RC, CB), f32) + b_ref[...]
            for k in range(CONV_K):
                acc = acc + _rows_at(win, sh, r0 + CONV_HALO - (CONV_K - 1) + k) * w_ref[k:k + 1, :]
            o_ref[r0:r0 + CONV_RC, :] = acc

    def cur(cb):
        return pl.BlockSpec((None, CONV_TS, CB), lambda b, j, i: (b, i, cb + j))

    def prev(cb):
        return pl.BlockSpec((None, CONV_HALO, CB), lambda b, j, i: (b, jnp.maximum(i * hb - 1, 0), cb + j))

    return _launch(
        body, name="glu_conv_fwd", grid=(B, 2, nt),
        in_specs=[cur(UA_CB), cur(UB_CB), prev(UA_CB), prev(UB_CB),
                  pl.BlockSpec((CONV_K, CB), lambda b, j, i: (0, j)), pl.BlockSpec((1, CB), lambda b, j, i: (0, j))],
        out_specs=pl.BlockSpec((None, CONV_TS, CB), lambda b, j, i: (b, i, j)),
        out_shape=SDS((B, SEQ, D_MODEL), f32), args=(proj3, proj3, proj3, proj3, w, bias),
        scratch_shapes=[pltpu.VMEM((CONV_WIN, CB), f32), pltpu.VMEM((SUBLANES - 1, CONV_WIN - SUBLANES, CB), f32)],
        sem=("parallel", "parallel", "parallel"), comm=comm)


def _glu_conv_bwd(dc1, proj3, w, comm=None):
    B = proj3.shape[0]
    nt = SEQ // CONV_TS
    hb = CONV_TS // CONV_HALO

    def body(d_ref, dn_ref, ua_ref, ub_ref, ha_ref, hb_ref, w_ref, dua_ref, dub_ref, dw_ref, db_ref, winc, wind, accw, shc, shd):
        b = pl.program_id(1)
        i = pl.program_id(2)
        first = jnp.logical_and(b == 0, i == 0)
        last = jnp.logical_and(b == B - 1, i == nt - 1)

        @pl.when(first)
        def _():
            accw[...] = jnp.zeros_like(accw)
            db_ref[...] = jnp.zeros_like(db_ref)

        halo = ha_ref[...].astype(f32) * _sigmoid(hb_ref[...].astype(f32))
        winc[0:CONV_HALO, :] = jnp.where(i > 0, halo, 0.0)
        winc[CONV_HALO:, :] = ua_ref[...].astype(f32) * _sigmoid(ub_ref[...].astype(f32))
        wind[0:CONV_TS, :] = d_ref[...]
        wind[CONV_TS:, :] = jnp.where(i < nt - 1, dn_ref[...], 0.0)
        db_ref[...] += jnp.sum(d_ref[...], axis=0, keepdims=True)
        _fill_shifted(winc, shc)
        _fill_shifted(wind, shd)
        for r0 in range(0, CONV_TS, CONV_RC):
            dc0 = jnp.zeros((CONV_RC, CB), f32)
            for k in range(CONV_K):
                dc0 = dc0 + _rows_at(wind, shd, r0 + (CONV_K - 1) - k) * w_ref[k:k + 1, :]
            uav = ua_ref[r0:r0 + CONV_RC, :].astype(f32)
            sg = _sigmoid(ub_ref[r0:r0 + CONV_RC, :].astype(f32))
            dua_ref[r0:r0 + CONV_RC, :] = (dc0 * sg).astype(bf16)
            dub_ref[r0:r0 + CONV_RC, :] = (dc0 * uav * sg * (1.0 - sg)).astype(bf16)
            dv = wind[r0:r0 + CONV_RC, :]
            for k in range(CONV_K):
                prod = dv * _rows_at(winc, shc, r0 + CONV_HALO - (CONV_K - 1) + k)
                accw[k] += jnp.sum(prod.reshape(CONV_RC // 8, 8, CB), axis=0)

        @pl.when(last)
        def _():
            for k in range(CONV_K):
                dw_ref[k:k + 1, :] = jnp.sum(accw[k], axis=0, keepdims=True)
            dw_ref[CONV_K:, :] = jnp.zeros((CONV_HALO - CONV_K, CB), f32)

    def cur(cb):
        return pl.BlockSpec((None, CONV_TS, CB), lambda j, b, i: (b, i, cb + j))

    def prev(cb):
        return pl.BlockSpec((None, CONV_HALO, CB), lambda j, b, i: (b, jnp.maximum(i * hb - 1, 0), cb + j))

    nxt = pl.BlockSpec((None, CONV_HALO, CB), lambda j, b, i: (b, jnp.minimum((i + 1) * hb, SEQ // CONV_HALO - 1), j))
    big = SDS((B, SEQ, D_MODEL), bf16)
    return _launch(
        body, name="glu_conv_bwd", grid=(2, B, nt),
        in_specs=[cur(0), nxt, cur(UA_CB), cur(UB_CB), prev(UA_CB), prev(UB_CB), pl.BlockSpec((CONV_K, CB), lambda j, b, i: (0, j))],
        out_specs=[cur(0), cur(0), pl.BlockSpec((CONV_HALO, CB), lambda j, b, i: (0, j)), pl.BlockSpec((1, CB), lambda j, b, i: (0, j))],
        out_shape=(big, big, SDS((CONV_HALO, D_MODEL), f32), SDS((1, D_MODEL), f32)),
        args=(dc1, dc1, proj3, proj3, proj3, proj3, w),
        scratch_shapes=[pltpu.VMEM((CONV_WIN, CB), f32), pltpu.VMEM((CONV_WIN, CB), f32), pltpu.VMEM((CONV_K, SUBLANES, CB), f32),
                        pltpu.VMEM((SUBLANES - 1, CONV_WIN - SUBLANES, CB), f32),
                        pltpu.VMEM((SUBLANES - 1, CONV_WIN - SUBLANES, CB), f32)],
        sem=("parallel", "arbitrary", "arbitrary"), comm=comm)


def _band(first, dil):
    kw = Q_BLOCK if first else 2 * Q_BLOCK
    qi = lax.broadcasted_iota(jnp.int32, (Q_BLOCK, kw), 0)
    kj = lax.broadcasted_iota(jnp.int32, (Q_BLOCK, kw), 1)
    rel = qi - kj + (0 if first else Q_BLOCK)
    valid = jnp.logical_and(rel >= 0, rel <= Q_BLOCK)
    return valid, rel.astype(f32) * float(dil)


def _bias(first, dil, slope):
    valid, dist = _band(first, dil)
    return jnp.where(valid, -slope * dist, NEG)


def _scores(q, k, bias):
    return lax.dot_general(q, k, NT, preferred_element_type=f32) * (HEAD_DIM ** -0.5) + bias


def _pair_cols(hp):
    return slice(hp * LANES, (hp + 1) * LANES)


def _half(x2, e):
    lane = lax.broadcasted_iota(jnp.int32, (1, LANES), 1)
    keep = (lane < HEAD_DIM) if e == 0 else (lane >= HEAD_DIM)
    return jnp.where(keep, x2, jnp.zeros_like(x2))


def _attn_fwd(g, q4, k4, v4, qcb, kcb, vcb, comm=None):
    _, dil = GROUPS[g]
    B, r, L, _ = q4.shape
    nb = L // Q_BLOCK
    slopes = [float(s) for s in _SLOPES[g]]

    def body(q_ref, k_ref, v_ref, o_ref, lse_ref, bias, s_scr, p_scr):
        lane = lax.broadcasted_iota(jnp.int32, (Q_BLOCK, LANES), 1)
        if nb > 1:
            for h in range(HEADS):
                bias[h] = _bias(False, dil, slopes[h])

        def block(n, first):
            q0 = 0 if first else pl.multiple_of(n * Q_BLOCK, Q_BLOCK)
            k0 = 0 if first else pl.multiple_of((n - 1) * Q_BLOCK, Q_BLOCK)
            kw = Q_BLOCK if first else 2 * Q_BLOCK
            for hp in range(HEADS // 2):
                q2 = q_ref[pl.ds(q0, Q_BLOCK), _pair_cols(hp)]
                k2 = k_ref[pl.ds(k0, kw), _pair_cols(hp)]
                for e in range(2):
                    h = 2 * hp + e
                    b_h = _bias(True, dil, slopes[h]) if first else bias[h]
                    s_scr[h, :, :kw] = _scores(_half(q2, e), k2, b_h)
            st = jnp.zeros((Q_BLOCK, LANES), f32)
            dens = jnp.ones((Q_BLOCK, LANES), f32)
            for h in range(HEADS):
                s = s_scr[h, :, :kw]
                m = jnp.max(s, axis=-1, keepdims=True)
                p = jnp.exp(s - m)
                den = jnp.sum(p, axis=-1, keepdims=True)
                p_scr[h, :, :kw] = p.astype(bf16)
                st = jnp.where(lane == h, m + jnp.log(den), st)
                dens = jnp.where(lane == h, den, dens)
            lse_ref[pl.ds(q0, Q_BLOCK), :] = st
            inv = 1.0 / dens
            for hp in range(HEADS // 2):
                v2 = v_ref[pl.ds(k0, kw), _pair_cols(hp)]
                o2 = sum(jnp.dot(p_scr[2 * hp + e, :, :kw], _half(v2, e), preferred_element_type=f32) * inv[:, 2 * hp + e:2 * hp + e + 1]
                         for e in range(2))
                o_ref[pl.ds(q0, Q_BLOCK), _pair_cols(hp)] = o2.astype(bf16)

        block(0, True)
        if nb > 1:
            def step(n, carry):
                block(n, False)
                return carry
            lax.fori_loop(1, nb, step, 0)

    def spec(cb):
        return pl.BlockSpec((None, None, L, GW), lambda b, c: (b, c, 0, cb))

    return _launch(
        body, name=f"attn_fwd_g{g}", grid=(B, r), in_specs=[spec(qcb), spec(kcb), spec(vcb)],
        out_specs=[spec(0), pl.BlockSpec((None, None, L, LANES), lambda b, c: (b, c, 0, 0))],
        out_shape=(SDS((B, r, L, GW), bf16), SDS((B, r, L, LANES), f32)), args=(q4, k4, v4),
        scratch_shapes=[pltpu.VMEM((HEADS, Q_BLOCK, 2 * Q_BLOCK), f32), pltpu.VMEM((HEADS, Q_BLOCK, 2 * Q_BLOCK), f32),
                        pltpu.VMEM((HEADS, Q_BLOCK, 2 * Q_BLOCK), bf16)],
        sem=("parallel", "parallel"), comm=comm)


def _attn_bwd(g, q4, k4, v4, qcb, kcb, vcb, do4, lse4, dl4, comm=None):
    _, dil = GROUPS[g]
    B, r, L, _ = q4.shape
    nb = L // Q_BLOCK
    slopes = [float(s) for s in _SLOPES[g]]
    scale = HEAD_DIM ** -0.5

    def body(q_ref, k_ref, v_ref, do_ref, lse_ref, dl_ref, dq_ref, dk_ref, dv_ref, dk_acc, dv_acc, bias, s_scr, dp_scr, p_scr, ds_scr):
        dk_acc[...] = jnp.zeros_like(dk_acc)
        dv_acc[...] = jnp.zeros_like(dv_acc)
        if nb > 1:
            for h in range(HEADS):
                bias[h] = _bias(False, dil, slopes[h])

        def block(n, first):
            q0 = 0 if first else pl.multiple_of(n * Q_BLOCK, Q_BLOCK)
            k0 = 0 if first else pl.multiple_of((n - 1) * Q_BLOCK, Q_BLOCK)
            kw = Q_BLOCK if first else 2 * Q_BLOCK
            for hp in range(HEADS // 2):
                q2 = q_ref[pl.ds(q0, Q_BLOCK), _pair_cols(hp)]
                k2 = k_ref[pl.ds(k0, kw), _pair_cols(hp)]
                v2 = v_ref[pl.ds(k0, kw), _pair_cols(hp)]
                do2 = do_ref[pl.ds(q0, Q_BLOCK), _pair_cols(hp)]
                for e in range(2):
                    h = 2 * hp + e
                    b_h = _bias(True, dil, slopes[h]) if first else bias[h]
                    s_scr[h, :, :kw] = _scores(_half(q2, e), k2, b_h)
                    dp_scr[h, :, :kw] = lax.dot_general(_half(do2, e), v2, NT, preferred_element_type=f32)
            for h in range(HEADS):
                p = jnp.exp(s_scr[h, :, :kw] - lse_ref[pl.ds(q0, Q_BLOCK), h:h + 1])
                p_scr[h, :, :kw] = p.astype(bf16)
                ds_scr[h, :, :kw] = (p * (dp_scr[h, :, :kw] - dl_ref[pl.ds(q0, Q_BLOCK), h:h + 1])).astype(bf16)
            for hp in range(HEADS // 2):
                cols = _pair_cols(hp)
                q2 = q_ref[pl.ds(q0, Q_BLOCK), cols]
                k2 = k_ref[pl.ds(k0, kw), cols]
                do2 = do_ref[pl.ds(q0, Q_BLOCK), cols]
                ds = [ds_scr[2 * hp + e, :, :kw] for e in range(2)]
                dq2 = sum(jnp.dot(ds[e], _half(k2, e), preferred_element_type=f32) for e in range(2))
                dq_ref[pl.ds(q0, Q_BLOCK), cols] = (dq2 * scale).astype(bf16)
                dk2 = sum(lax.dot_general(ds[e], _half(q2, e), TN, preferred_element_type=f32) for e in range(2))
                dk_acc[pl.ds(k0, kw), cols] += dk2 * scale
                dv2 = sum(lax.dot_general(p_scr[2 * hp + e, :, :kw], _half(do2, e), TN, preferred_element_type=f32) for e in range(2))
                dv_acc[pl.ds(k0, kw), cols] += dv2

        block(0, True)
        if nb > 1:
            def step(n, carry):
                block(n, False)
                return carry
            lax.fori_loop(1, nb, step, 0)
        dk_ref[...] = dk_acc[...].astype(bf16)
        dv_ref[...] = dv_acc[...].astype(bf16)

    def spec(cb):
        return pl.BlockSpec((None, None, L, GW), lambda b, c: (b, c, 0, cb))

    st = pl.BlockSpec((None, None, L, LANES), lambda b, c: (b, c, 0, 0))
    osd = SDS((B, r, L, GW), bf16)
    return _launch(
        body, name=f"attn_bwd_g{g}", grid=(B, r), in_specs=[spec(qcb), spec(kcb), spec(vcb), spec(0), st, st],
        out_specs=[spec(0), spec(0), spec(0)], out_shape=(osd, osd, osd), args=(q4, k4, v4, do4, lse4, dl4),
        scratch_shapes=[pltpu.VMEM((L, GW), f32), pltpu.VMEM((L, GW), f32)]
        + [pltpu.VMEM((HEADS, Q_BLOCK, 2 * Q_BLOCK), f32)] * 3 + [pltpu.VMEM((HEADS, Q_BLOCK, 2 * Q_BLOCK), bf16)] * 2,
        sem=("parallel", "parallel"), comm=comm)


RT = 512
RCH = GW // LANES
DILS = tuple(d for _, d in GROUPS[1:])


def _res_spec(r, width):
    return pl.BlockSpec((None, r, RT // r, width), lambda b, i, *_: (b, 0, i, 0))


def _tok_spec(width, cb=0):
    return pl.BlockSpec((None, RT, width), lambda b, i, *_: (b, i, cb))


def _to_residues(res_ref, scr, r, width):
    for c in range(r):
        for jj in range(width // LANES):
            res_ref[c, :, jj * LANES:(jj + 1) * LANES] = scr[jj, pl.ds(c, RT // r, stride=r), :].astype(res_ref.dtype)


def _from_residues(scr, res_ref, r, width):
    for c in range(r):
        for jj in range(width // LANES):
            scr[jj, pl.ds(c, RT // r, stride=r), :] = res_ref[c, :, jj * LANES:(jj + 1) * LANES].astype(f32)


def _qkv_to_residues(proj3, g):
    r = GROUPS[g][1]
    B = proj3.shape[0]

    def body(q_ref, k_ref, v_ref, o_ref, scr):
        for p, x_ref in enumerate((q_ref, k_ref, v_ref)):
            x = x_ref[...].astype(f32)
            for jj in range(RCH):
                scr[p * RCH + jj] = x[:, jj * LANES:(jj + 1) * LANES]
        _to_residues(o_ref, scr, r, ATTN_W)

    return pl.pallas_call(
        body, grid=(B, SEQ // RT), in_specs=[_tok_spec(GW, 3 * p + g) for p in range(3)], out_specs=_res_spec(r, ATTN_W),
        out_shape=SDS((B, r, SEQ // r, ATTN_W), bf16), scratch_shapes=[pltpu.VMEM((3 * RCH, RT, LANES), f32)],
        name=f"qkv_to_residues_g{g}", compiler_params=_cp(("parallel", "parallel")))(proj3, proj3, proj3)


def _attn_mix(o0, l0, o_res, l_res):
    B = o0.shape[0]

    def body(o0_ref, l0_ref, o1_ref, o2_ref, l1_ref, l2_ref, y_ref, lt_ref, lt1_ref, lt2_ref, so, sl):
        for gi, (o_ref, l_ref, r) in enumerate(((o1_ref, l1_ref, DILS[0]), (o2_ref, l2_ref, DILS[1]))):
            _from_residues(so.at[gi], o_ref, r, GW)
            _from_residues(sl.at[gi:gi + 1], l_ref, r, LANES)
        ls = [l0_ref[...], sl[0], sl[1]]
        m = functools.reduce(jnp.maximum, ls)
        ws = [jnp.exp(l - m) for l in ls]
        den = ws[0] + ws[1] + ws[2]
        alphas = [w / den for w in ws]
        lt = m + jnp.log(den)
        lt_ref[...] = lt
        sl[2] = lt
        _to_residues(lt1_ref, sl.at[2:3], DILS[0], LANES)
        _to_residues(lt2_ref, sl.at[2:3], DILS[1], LANES)
        for h in range(HEADS):
            cols = slice(h * HEAD_DIM, (h + 1) * HEAD_DIM)
            jj, lo = divmod(h * HEAD_DIM, LANES)
            acc = alphas[0][:, h:h + 1] * o0_ref[:, cols].astype(f32)
            for gi in range(2):
                acc = acc + alphas[gi + 1][:, h:h + 1] * so[gi, jj, :, lo:lo + HEAD_DIM]
            y_ref[:, cols] = acc.astype(bf16)

    in_specs = [_tok_spec(GW), _tok_spec(LANES), _res_spec(DILS[0], GW), _res_spec(DILS[1], GW), _res_spec(DILS[0], LANES), _res_spec(DILS[1], LANES)]
    out_specs = [_tok_spec(GW), _tok_spec(LANES), _res_spec(DILS[0], LANES), _res_spec(DILS[1], LANES)]
    return pl.pallas_call(
        body, grid=(B, SEQ // RT), in_specs=in_specs, out_specs=out_specs,
        out_shape=(SDS((B, SEQ, GW), bf16), SDS((B, SEQ, LANES), f32)) + tuple(SDS((B, r, SEQ // r, LANES), f32) for r in DILS),
        scratch_shapes=[pltpu.VMEM((2, RCH, RT, LANES), f32), pltpu.VMEM((3, RT, LANES), f32)],
        name="attn_mix", compiler_params=_cp(("parallel", "parallel")))(o0, l0, *o_res, *l_res)


def _attn_delta(dmix, mix):
    B = dmix.shape[0]

    def body(d_ref, y_ref, dl_ref, dl1_ref, dl2_ref, dm1_ref, dm2_ref, sx, sd):
        lane = lax.broadcasted_iota(jnp.int32, (RT, LANES), 1)
        acc = jnp.zeros((RT, LANES), f32)
        dv = d_ref[...].astype(f32)
        for jj in range(RCH):
            sx[jj] = dv[:, jj * LANES:(jj + 1) * LANES]
        for h in range(HEADS):
            cols = slice(h * HEAD_DIM, (h + 1) * HEAD_DIM)
            dl = jnp.sum(dv[:, cols] * y_ref[:, cols].astype(f32), axis=-1, keepdims=True)
            acc = jnp.where(lane == h, dl, acc)
        dl_ref[...] = acc
        sd[0] = acc
        _to_residues(dl1_ref, sd, DILS[0], LANES)
        _to_residues(dl2_ref, sd, DILS[1], LANES)
        _to_residues(dm1_ref, sx, DILS[0], GW)
        _to_residues(dm2_ref, sx, DILS[1], GW)

    return pl.pallas_call(
        body, grid=(B, SEQ // RT), in_specs=[_tok_spec(GW), _tok_spec(GW)],
        out_specs=[_tok_spec(LANES), _res_spec(DILS[0], LANES), _res_spec(DILS[1], LANES), _res_spec(DILS[0], GW), _res_spec(DILS[1], GW)],
        out_shape=(SDS((B, SEQ, LANES), f32),) + tuple(SDS((B, r, SEQ // r, LANES), f32) for r in DILS)
        + tuple(SDS((B, r, SEQ // r, GW), bf16) for r in DILS),
        scratch_shapes=[pltpu.VMEM((RCH, RT, LANES), f32), pltpu.VMEM((1, RT, LANES), f32)],
        name="attn_delta", compiler_params=_cp(("parallel", "parallel")))(dmix, mix)


N_CB = IN_W // CB


def _assemble_dproj(dqkv, dua, dub, dga, dgc):
    B = dua.shape[0]
    ng = len(GROUPS)
    flat = [dqkv[g][p] for p in range(3) for g in range(ng)]
    wide = [dua, dub, dga, dgc]

    def body(*refs):
        srcs, wides, o_ref, scr = refs[:3 * ng], refs[3 * ng:3 * ng + 4], refs[3 * ng + 4], refs[3 * ng + 5]
        for jv in range(3 * ng):
            g = jv % ng
            if g == 0:
                o_ref[:, jv * GW:(jv + 1) * GW] = srcs[jv][...]
            else:
                _from_residues(scr, srcs[jv], GROUPS[g][1], GW)
                for jj in range(RCH):
                    o_ref[:, jv * GW + jj * LANES:jv * GW + (jj + 1) * LANES] = scr[jj].astype(bf16)
        for wv in range(4):
            lo = 3 * ATTN_W + wv * D_MODEL
            o_ref[:, lo:lo + D_MODEL] = wides[wv][...]

    in_specs = [_tok_spec(GW) if (jv % ng) == 0 else _res_spec(GROUPS[jv % ng][1], GW) for jv in range(3 * ng)]
    in_specs += [_tok_spec(D_MODEL)] * 4
    return pl.pallas_call(
        body, grid=(B, SEQ // RT), in_specs=in_specs, out_specs=_tok_spec(IN_W),
        out_shape=SDS((B, SEQ, IN_W), bf16), scratch_shapes=[pltpu.VMEM((RCH, RT, LANES), f32)],
        name="assemble_dproj", compiler_params=_cp(("parallel", "parallel")))(*flat, *wide)


def _same_shape_groups(names, arrays):
    groups = {}
    for n, a in zip(names, arrays):
        groups.setdefault(a.shape, ([], []))
        groups[a.shape][0].append(n)
        groups[a.shape][1].append(a)
    return list(groups.values())


def _add_pairs(parts, gots, core, name):
    k = len(parts)
    n, h, C = gots[0].shape

    def body(c_ref, *refs):
        for i in range(k):
            refs[2 * k + i][...] = (refs[i][...].astype(f32) + refs[k + i][...].astype(f32)).astype(bf16)

    blk = pl.BlockSpec((None, h, C), lambda s, c_ref: (s, 0, 0))
    own = pl.BlockSpec((None, h, C), lambda s, c_ref: (s, c_ref[0], 0))
    spec = pltpu.PrefetchScalarGridSpec(num_scalar_prefetch=1, grid=(n,), in_specs=[own] * k + [blk] * k, out_specs=[blk] * k)
    return pl.pallas_call(body, grid_spec=spec, out_shape=tuple(SDS((n, h, C), bf16) for _ in range(k)), name=name,
                          compiler_params=_cp(("parallel",)))(core, *parts, *gots)


def _sum4(sums, gots, chip, core, name):
    k = len(sums)
    _, h, C = sums[0].shape

    def body(s_ref, c_ref, *refs):
        for i in range(k):
            q_ref = refs[k + i]
            t = refs[i][...].astype(f32) + q_ref[0].astype(f32)
            t = t + q_ref[1].astype(f32)
            refs[2 * k + i][...] = t + q_ref[2].astype(f32)

    spec = pltpu.PrefetchScalarGridSpec(
        num_scalar_prefetch=2, grid=(1,),
        in_specs=[pl.BlockSpec((None, h, C), lambda i, s_ref, c_ref: (s_ref[0], 0, 0))] * k
        + [pl.BlockSpec((N_CHIPS - 1, h, C), lambda i, s_ref, c_ref: (0, 0, 0))] * k,
        out_specs=[pl.BlockSpec((h, C), lambda i, s_ref, c_ref: (c_ref[0], 0))] * k)
    return pl.pallas_call(body, grid_spec=spec, out_shape=tuple(SDS((2 * h, C), f32) for _ in range(k)), name=name,
                          compiler_params=_cp(("arbitrary",)))(chip, core, *sums, *gots)


def _adamw(ws, gs, ms, vs, name):
    k = len(ws)
    R, C = ws[0].shape
    rt = R
    for cand in (512, 256, 128, 64, 32, 16, 8):
        if R % cand == 0 and cand * C * 4 <= 2 ** 21:
            rt = cand
            break
    c1 = 1.0 / (1.0 - B1 ** STEP)
    c2 = 1.0 / (1.0 - B2 ** STEP)

    def body(*refs):
        for i in range(k):
            w_ref, g_ref, m_ref, v_ref = (refs[j * k + i] for j in range(4))
            d_ref, nm_ref, nv_ref = (refs[(4 + j) * k + i] for j in range(3))
            gv = g_ref[...]
            nm = B1 * m_ref[...] + (1.0 - B1) * gv
            nv = B2 * v_ref[...] + (1.0 - B2) * (gv * gv)
            nm_ref[...] = nm
            nv_ref[...] = nv
            d_ref[...] = -LR * ((nm * c1) / (jnp.sqrt(nv * c2) + ADAM_EPS) + WD * w_ref[...])

    blk = pl.BlockSpec((rt, C), lambda i: (i, 0))
    sd = SDS((R, C), f32)
    res = pl.pallas_call(body, grid=(R // rt,), in_specs=[blk] * (4 * k), out_specs=[blk] * (3 * k), out_shape=(sd,) * (3 * k),
                         name=name, compiler_params=_cp(("parallel",)))(*ws, *gs, *ms, *vs)
    return res[:k], res[k:2 * k], res[2 * k:]


def _coords():
    return lax.axis_index("x"), lax.axis_index("y"), lax.axis_index("c")


def _other_chips(x, y):
    return [(1 - x, y), (x, 1 - y), (1 - x, 1 - y)]


def _allgather_weights(shards, slots):
    n = len(shards)
    n_slot = 12

    def body(*refs):
        ins, outs = refs[:n], refs[2 * n:3 * n]
        send, recv = refs[3 * n:]
        x, y, c = _coords()
        me, cx, cy, cd = 2 * x + y, 2 * (1 - x) + y, 2 * x + (1 - y), 2 * (1 - x) + (1 - y)
        dev_x, dev_y, dev_s = (1 - x, y, c), (x, 1 - y, c), (x, y, 1 - c)
        started = []

        def copy(w, slot, src, dst, dev):
            return pltpu.make_async_remote_copy(src_ref=src, dst_ref=dst, send_sem=send.at[w, slot], recv_sem=recv.at[w, slot],
                                                device_id=dev, device_id_type=MESH)

        def go(cp):
            cp.start()
            started.append(cp)

        for w in range(n):
            q = shards[w].shape[0] // 4
            rows = [pl.ds(c * 2 * q + k * q, q) for k in range(2)]
            theirs = [pl.ds((1 - c) * 2 * q + k * q, q) for k in range(2)]
            own = [(ins[w].at[r, :], outs[w].at[me, r, :]) for r in rows]
            go(copy(w, 0, *own[0], dev_x))
            go(copy(w, 2, *own[1], dev_y))
            go(copy(w, 1, *own[1], dev_x))
            go(copy(w, 3, *own[0], dev_y))
            arrivals = [(0, cx, 0, (4, dev_y)), (2, cy, 1, (5, dev_x)), (1, cx, 1, None), (3, cy, 0, None), (4, cd, 0, None), (5, cd, 1, None)]
            for k, (slot, chip, quarter, onward) in enumerate(arrivals):
                blk = outs[w].at[chip, rows[quarter], :]
                copy(w, slot, blk, blk, dev_s).wait_recv()
                if onward is not None:
                    go(copy(w, onward[0], blk, blk, onward[1]))
                go(copy(w, 6 + k, blk, blk, dev_s))
            for k, (slot, chip, quarter, onward) in enumerate(arrivals):
                blk = outs[w].at[chip, theirs[quarter], :]
                copy(w, 6 + k, blk, blk, dev_s).wait_recv()
        for cp in started:
            cp.wait_send()

    return pl.pallas_call(
        body, in_specs=[HBM] * (2 * n), out_specs=[HBM] * n,
        out_shape=tuple(SDS((N_CHIPS,) + s.shape, s.dtype) for s in shards),
        input_output_aliases={n + w: w for w in range(n)},
        scratch_shapes=[pltpu.SemaphoreType.DMA((n, n_slot))] * 2,
        name="allgather_weights", compiler_params=pltpu.CompilerParams(has_side_effects=True))(*shards, *slots)


def _x_pair_exchange(parts):
    n = len(parts)
    halves = [p.shape[1] // 2 for p in parts]

    def copies(ins, outs, sems):
        send, recv = sems
        x, y, c = _coords()
        return [pltpu.make_async_remote_copy(src_ref=ins[w].at[:, pl.ds((1 - c) * halves[w], halves[w]), :], dst_ref=outs[w],
                                             send_sem=send.at[w], recv_sem=recv.at[w], device_id=(x, y, 1 - c), device_id_type=MESH)
                for w in range(n)]

    def start(ins, outs, sems):
        for cp in copies(ins, outs, sems):
            cp.start()

    def wait(ins, outs, sems):
        for cp in copies(ins, outs, sems):
            cp.wait()

    return _Exchange(ins=list(parts), out_shape=[SDS((N_CHIPS, p.shape[1] // 2, p.shape[2]), p.dtype) for p in parts],
                     scratch=[pltpu.SemaphoreType.DMA((n,))] * 2, aliases={}, start=start, wait=wait)


def _x_chip_exchange(sums):
    n = len(sums)

    def copies(ins, outs, sems):
        send, recv = sems
        x, y, c = _coords()
        return [pltpu.make_async_remote_copy(src_ref=ins[w].at[2 * px + py], dst_ref=outs[w].at[j], send_sem=send.at[w, j],
                                             recv_sem=recv.at[w, j], device_id=(px, py, c), device_id_type=MESH)
                for w in range(n) for j, (px, py) in enumerate(_other_chips(x, y))]

    def start(ins, outs, sems):
        for cp in copies(ins, outs, sems):
            cp.start()

    def wait(ins, outs, sems):
        for cp in copies(ins, outs, sems):
            cp.wait()

    return _Exchange(ins=list(sums), out_shape=[SDS((N_CHIPS - 1,) + s.shape[1:], s.dtype) for s in sums],
                     scratch=[pltpu.SemaphoreType.DMA((n, 3)), pltpu.SemaphoreType.DMA((n, 3))], aliases={}, start=start, wait=wait)


def _x_gather_ici(shards, slots):
    n = len(shards)
    halves = [s.shape[0] // 2 for s in shards]

    def copies(ins, outs, sems):
        send, recv = sems
        x, y, c = _coords()
        me = 2 * x + y
        out = []
        for w in range(n):
            mine = pl.ds(c * halves[w], halves[w])
            for j, (px, py) in enumerate(_other_chips(x, y)):
                snd = pltpu.make_async_remote_copy(src_ref=ins[w].at[mine, :], dst_ref=outs[w].at[me, mine, :], send_sem=send.at[w, j],
                                                   recv_sem=recv.at[w, j], device_id=(px, py, c), device_id_type=MESH)
                got = outs[w].at[2 * px + py, mine, :]
                rcv = pltpu.make_async_remote_copy(src_ref=got, dst_ref=got, send_sem=send.at[w, j], recv_sem=recv.at[w, j],
                                                   device_id=(px, py, c), device_id_type=MESH)
                out.append((snd, rcv))
        return out

    def start(ins, outs, sems):
        for snd, _ in copies(ins, outs, sems):
            snd.start()

    def wait(ins, outs, sems):
        for snd, rcv in copies(ins, outs, sems):
            rcv.wait_recv()
            snd.wait_send()

    return _Exchange(ins=list(shards) + list(slots), out_shape=[SDS(s.shape, s.dtype) for s in slots],
                     scratch=[pltpu.SemaphoreType.DMA((n, 3)), pltpu.SemaphoreType.DMA((n, 3))],
                     aliases={n + w: w for w in range(n)}, start=start, wait=wait)


def _x_gather_fwd(bufs):
    n = len(bufs)
    halves = [b.shape[1] // 2 for b in bufs]

    def copies(ins, outs, sems):
        send, recv = sems
        x, y, c = _coords()
        out = []
        for w in range(n):
            for j, (px, py) in enumerate(_other_chips(x, y)):
                mine = outs[w].at[2 * px + py, pl.ds(c * halves[w], halves[w]), :]
                theirs = outs[w].at[2 * px + py, pl.ds((1 - c) * halves[w], halves[w]), :]
                snd = pltpu.make_async_remote_copy(src_ref=mine, dst_ref=mine, send_sem=send.at[w, j], recv_sem=recv.at[w, j],
                                                   device_id=(x, y, 1 - c), device_id_type=MESH)
                rcv = pltpu.make_async_remote_copy(src_ref=theirs, dst_ref=theirs, send_sem=send.at[w, j], recv_sem=recv.at[w, j],
                                                   device_id=(x, y, 1 - c), device_id_type=MESH)
                out.append((snd, rcv))
        return out

    def start(ins, outs, sems):
        for snd, _ in copies(ins, outs, sems):
            snd.start()

    def wait(ins, outs, sems):
        for snd, rcv in copies(ins, outs, sems):
            rcv.wait_recv()
            snd.wait_send()

    return _Exchange(ins=list(bufs), out_shape=[SDS(b.shape, b.dtype) for b in bufs],
                     scratch=[pltpu.SemaphoreType.DMA((n, 3)), pltpu.SemaphoreType.DMA((n, 3))],
                     aliases={w: w for w in range(n)}, start=start, wait=wait)


def _final_exchange(bufs, vec):
    n = len(bufs)
    R = vec.shape[0]
    nd = 8

    def body(*refs):
        v_ref = refs[n]
        outs = refs[n + 1:2 * n + 1]
        o_ref, sbuf, send, recv, ssend, srecv = refs[2 * n + 1:]
        x, y, c = _coords()
        me = 4 * x + 2 * y + c
        sbuf[me] = v_ref[...]
        small = []
        for k in range(1, nd):
            kx, ky, kc = (k >> 2) & 1, (k >> 1) & 1, k & 1
            tx = x + kx - 2 * x * kx
            ty = y + ky - 2 * y * ky
            tc = c + kc - 2 * c * kc
            cp = pltpu.make_async_remote_copy(src_ref=v_ref, dst_ref=sbuf.at[me], send_sem=ssend.at[k], recv_sem=srecv.at[k],
                                              device_id=(tx, ty, tc), device_id_type=MESH)
            cp.start()
            small.append((cp, 4 * tx + 2 * ty + tc))
        cps = []
        for w in range(n):
            h = bufs[w].shape[0] // 2
            rows = outs[w].at[pl.ds(c * h, h), :]
            rc = pltpu.make_async_remote_copy(src_ref=rows, dst_ref=rows, send_sem=send.at[w], recv_sem=recv.at[w],
                                              device_id=(x, y, 1 - c), device_id_type=MESH)
            rc.start()
            cps.append(rc)
        for k, (cp, src) in zip(range(1, nd), small):
            pltpu.make_async_remote_copy(src_ref=v_ref, dst_ref=sbuf.at[src], send_sem=ssend.at[k], recv_sem=srecv.at[k],
                                         device_id=(x, y, c), device_id_type=MESH).wait_recv()
        for cp, _ in small:
            cp.wait_send()
        acc = sbuf[0]
        for d in range(1, nd):
            acc = acc + sbuf[d]
        o_ref[...] = acc
        for w, rc in enumerate(cps):
            h = bufs[w].shape[0] // 2
            other = outs[w].at[pl.ds((1 - c) * h, h), :]
            pltpu.make_async_remote_copy(src_ref=other, dst_ref=other, send_sem=send.at[w], recv_sem=recv.at[w],
                                         device_id=(x, y, 1 - c), device_id_type=MESH).wait_recv()
            rc.wait_send()

    vm = pl.BlockSpec(memory_space=pltpu.VMEM)
    res = pl.pallas_call(body, in_specs=[HBM] * n + [vm], out_specs=[HBM] * n + [vm],
                         out_shape=tuple(SDS(a.shape, a.dtype) for a in bufs) + (SDS((R, LANES), f32),),
                         input_output_aliases={w: w for w in range(n)},
                         scratch_shapes=[pltpu.VMEM((nd, R, LANES), f32)] + [pltpu.SemaphoreType.DMA((n,))] * 2
                         + [pltpu.SemaphoreType.DMA((nd,))] * 2, name="final_exchange",
                         compiler_params=pltpu.CompilerParams(has_side_effects=True))(*bufs, vec)
    return res[:n], res[n]


def _rows128(a, rows):
    flat = a.reshape(-1)
    return jnp.pad(flat, (0, rows * LANES - flat.shape[0])).reshape(rows, LANES)


GATHER_1 = ("w_conv_out", "w_attn_out", "w_o", "w_ffn_gate")
GATHER_2 = ("w_ffn_up", "w_ffn_down")
REDUCE_A = ("w_ffn_gate", "w_ffn_up", "w_ffn_down")
REDUCE_B = ("w_conv_out", "w_attn_out", "w_o")
REDUCE_C = ("w_in",)


def _step(x, target, norm1_g, gate_b, conv_w, conv_b, conv_ln_g, conv_ln_b, norm2_g, norm_f_g, w_in, shards, slots, chip1, core):
    B = x.shape[0]
    T = B * SEQ
    xf = x.reshape(T, D_MODEL)
    tf = target.reshape(T, D_MODEL)

    h, h_t = _rms_fwd(xf, norm1_g, "rms1_fwd")
    proj, got1 = _mm("in_proj", h, w_in, grid=(N_CHIPS, T // 1024),
                     a_spec=pl.BlockSpec((1024, D_MODEL), lambda s, m: (m, 0)),
                     b_spec=pl.BlockSpec((None, D_MODEL, IN_S), lambda s, m: (s, 0, 0)),
                     o_spec=pl.BlockSpec((1024, IN_S), lambda s, m: (m, s)), o_shape=(T, IN_W), o_dtype=bf16, dims=NN,
                     comm=_x_gather_ici([shards[n] for n in GATHER_1], [slots[n] for n in GATHER_1]))
    proj3 = proj.reshape(B, SEQ, IN_W)
    proj4 = proj.reshape(B, 1, SEQ, IN_W)

    qkv = [None] + [_qkv_to_residues(proj3, g) for g in range(1, len(GROUPS))]

    def qkv_args(g):
        return (proj4, proj4, proj4, 0, 3, 6) if g == 0 else (qkv[g], qkv[g], qkv[g], 0, 1, 2)

    (o4_0, l4_0), full1 = _attn_fwd(0, *qkv_args(0), comm=_x_gather_fwd(list(got1)))
    o4_1, l4_1 = _attn_fwd(1, *qkv_args(1))
    o4_2, l4_2 = _attn_fwd(2, *qkv_args(2))
    full = dict(zip(GATHER_1, full1))
    w_conv_out, w_attn_out, w_o, w_gate = (full[n] for n in GATHER_1)
    w_conv_out_f = w_conv_out.reshape(D_MODEL, D_MODEL)
    w_o_f = w_o.reshape(D_MODEL, D_MODEL)
    mix3, lse3, lse_r1, lse_r2 = _attn_mix(o4_0.reshape(B, SEQ, GW), l4_0.reshape(B, SEQ, LANES), [o4_1, o4_2], [l4_1, l4_2])
    mix = mix3.reshape(T, GW)
    y_attn = _attn_out_fwd(mix, w_attn_out)

    c1, got2 = _glu_conv_fwd(proj3, conv_w, conv_b, comm=_x_gather_ici([shards[n] for n in GATHER_2], [slots[n] for n in GATHER_2]))
    c1 = c1.reshape(T, D_MODEL)
    (c3, y_conv), (w_up, w_down) = _conv_out_fwd(c1, conv_ln_g, conv_ln_b, w_conv_out_f, comm=_x_gather_fwd(list(got2)))

    merged, x1, h2 = _merge_o_proj_rms(proj, gate_b, y_attn, y_conv, w_o_f, xf, norm2_g)
    fa, fb, ff = _ffn_up(h2, w_gate, w_up)
    loss, dx2, d_gf = _ffn_down_loss(ff, w_down, x1, norm_f_g.reshape(1, D_MODEL), tf)

    d_w_down = _mm("d_w_down", ff, dx2, grid=(N_CHIPS, T // 1024),
                   a_spec=pl.BlockSpec((None, 1024, FF_S), lambda s, k: (s, k, 0)),
                   b_spec=pl.BlockSpec((1024, D_MODEL), lambda s, k: (k, 0)),
                   o_spec=pl.BlockSpec((None, FF_S, D_MODEL), lambda s, k: (s, 0, 0)), o_shape=(N_CHIPS, FF_S, D_MODEL),
                   o_dtype=bf16, dims=TN, acc_shape=(FF_S, D_MODEL), k_axis=1, sem=("parallel", "arbitrary"))
    da, db = _ffn_down_bwd(dx2, w_down, fa, fb)

    def d_w_ff(name, dz):
        return _mm(name, dz, h2, grid=(N_CHIPS, T // 2048),
                   a_spec=pl.BlockSpec((None, 2048, FF_S), lambda s, k: (s, k, 0)),
                   b_spec=pl.BlockSpec((2048, D_MODEL), lambda s, k: (k, 0)),
                   o_spec=pl.BlockSpec((None, FF_S, D_MODEL), lambda s, k: (s, 0, 0)), o_shape=(N_CHIPS, FF_S, D_MODEL),
                   o_dtype=bf16, dims=TN, acc_shape=(FF_S, D_MODEL), k_axis=1, sem=("parallel", "arbitrary"))

    d_w_gate = d_w_ff("d_w_gate", da)
    d_w_up = d_w_ff("d_w_up", db)
    part = dict(w_ffn_gate=d_w_gate, w_ffn_up=d_w_up, w_ffn_down=d_w_down)

    def pair_sums(names, got):
        out = {}
        for ns, gs in _same_shape_groups(names, got):
            out.update(zip(ns, _add_pairs([part[n] for n in ns], gs, core, "pair_sum_" + ns[0])))
        return [out[n] for n in names]

    (dx1, d_g2), got = _mm_rms_bwd("ffn_dh2", [(da, w_gate), (db, w_up)], x1, norm2_g, dx2, tm=1024,
                                   a_spec=pl.BlockSpec((None, 1024, FF_S), lambda m, s: (s, m, 0)),
                                   b_spec=pl.BlockSpec((None, FF_S, D_MODEL), lambda m, s: (s, 0, 0)), dims=NN,
                                   comm=_x_pair_exchange([part[n] for n in REDUCE_A]), vmem=VMEM_LIMIT_BIG)
    sums_a = pair_sums(REDUCE_A, got)

    d_w_o = _mm_tn_tokens("d_w_o", merged, dx1, bf16).reshape(N_CHIPS, D_MODEL // N_CHIPS, D_MODEL)
    dmerged = _mm_nt_full("d_merged", dx1, w_o_f, bf16)
    dya, dyc, dga, dgc, d_gba, d_gbc = _merge_bwd(dmerged, proj, gate_b, y_attn, y_conv)

    d_w_conv_out = _mm_tn_tokens("d_w_conv_out", c3, dyc, bf16).reshape(N_CHIPS, D_MODEL // N_CHIPS, D_MODEL)
    dc1, d_ln_g, d_ln_b = _conv_out_bwd(dyc, w_conv_out_f, c1, conv_ln_g, conv_ln_b)
    (dua, dub, d_conv_w, d_conv_b), got_a = _glu_conv_bwd(dc1.reshape(B, SEQ, D_MODEL), proj3, conv_w, comm=_x_chip_exchange(sums_a))

    d_w_attn_out = _attn_out_bwd_w(mix, dya)
    part.update(w_conv_out=d_w_conv_out, w_attn_out=d_w_attn_out, w_o=d_w_o)
    dmix, got = _attn_out_bwd_x(dya, w_attn_out, comm=_x_pair_exchange([part[n] for n in REDUCE_B]))
    sums_b = pair_sums(REDUCE_B, got)
    dmix3 = dmix.reshape(B, SEQ, GW)
    delta3, delta_r1, delta_r2, dmix_r1, dmix_r2 = _attn_delta(dmix3, mix3)
    one = (B, 1, SEQ)
    (dq0, dk0, dv0), got_b = _attn_bwd(0, *qkv_args(0), dmix3.reshape(one + (GW,)), lse3.reshape(one + (LANES,)),
                                       delta3.reshape(one + (LANES,)), comm=_x_chip_exchange(sums_b))
    dqkv = [tuple(t.reshape(B, SEQ, GW) for t in (dq0, dk0, dv0)),
            _attn_bwd(1, *qkv_args(1), dmix_r1, lse_r1, delta_r1), _attn_bwd(2, *qkv_args(2), dmix_r2, lse_r2, delta_r2)]
    dproj = _assemble_dproj(dqkv, dua, dub, dga.reshape(B, SEQ, D_MODEL), dgc.reshape(B, SEQ, D_MODEL)).reshape(T, IN_W)

    d_w_in = _mm("d_w_in", h_t, dproj, grid=(N_CHIPS, T // 1024),
                 a_spec=pl.BlockSpec((D_MODEL, 1024), lambda s, k: (0, k)),
                 b_spec=pl.BlockSpec((1024, IN_S), lambda s, k: (k, s)),
                 o_spec=pl.BlockSpec((None, D_MODEL, IN_S), lambda s, k: (s, 0, 0)), o_shape=(N_CHIPS, D_MODEL, IN_S),
                 o_dtype=bf16, dims=NN, acc_shape=(D_MODEL, IN_S), k_axis=1, sem=("parallel", "arbitrary"))
    part.update(w_in=d_w_in)
    sums_c = pair_sums(REDUCE_C, _run_exchange(_x_pair_exchange([d_w_in]), "grad_pair_exchange_c"))
    (dx, d_g1), got_c = _mm_rms_bwd("d_h", [(dproj, w_in)], xf, norm1_g, dx1, tm=512,
                                    a_spec=pl.BlockSpec((512, IN_S), lambda m, s: (m, s)),
                                    b_spec=pl.BlockSpec((None, D_MODEL, IN_S), lambda m, s: (s, 0, 0)), dims=NT,
                                    comm=_x_chip_exchange(sums_c))

    names = REDUCE_A + REDUCE_B + REDUCE_C
    sums = dict(zip(names, sums_a + sums_b + sums_c))
    halves = {}
    for ns, gs in _same_shape_groups(names, got_a + got_b + got_c):
        halves.update(zip(ns, _sum4([sums[n] for n in ns], gs, chip1, core, "chip_sum_" + ns[0])))
    big = {n: halves[n] for n in names}
    small = dict(norm1_g=d_g1, gate_b=jnp.concatenate([d_gba, d_gbc], axis=-1), conv_b=d_conv_b, conv_ln_g=d_ln_g,
                 conv_ln_b=d_ln_b, norm2_g=d_g2, norm_f_g=d_gf, conv_w=d_conv_w)
    return loss, dx.reshape(B, SEQ, D_MODEL), big, small


BIG = ("w_in", "w_conv_out", "w_attn_out", "w_o", "w_ffn_gate", "w_ffn_up", "w_ffn_down")
TRANSPOSED = ("w_ffn_gate", "w_ffn_up")
SMALL = ("norm1_g", "gate_b", "conv_b", "conv_ln_g", "conv_ln_b", "norm2_g", "norm_f_g")
SMALL_ROWS = {"norm1_g": 8, "gate_b": 16, "conv_b": 8, "conv_ln_g": 8, "conv_ln_b": 8, "norm2_g": 8, "norm_f_g": 8}
LOSS_ROWS = 8
CONVW_ROWS = 32 * D_MODEL // LANES


def kernel(x, norm1_g, w_in, gate_b, conv_w, conv_b, conv_ln_g, conv_ln_b, w_conv_out, w_attn_out, w_o, norm2_g, w_ffn_gate, w_ffn_up, w_ffn_down, norm_f_g, loss_target, m_norm1_g, m_w_in, m_gate_b, m_conv_w, m_conv_b, m_conv_ln_g, m_conv_ln_b, m_w_conv_out, m_w_attn_out, m_w_o, m_norm2_g, m_w_ffn_gate, m_w_ffn_up, m_w_ffn_down, m_norm_f_g, v_norm1_g, v_w_in, v_gate_b, v_conv_w, v_conv_b, v_conv_ln_g, v_conv_ln_b, v_w_conv_out, v_w_attn_out, v_w_o, v_norm2_g, v_w_ffn_gate, v_w_ffn_up, v_w_ffn_down, v_norm_f_g):
    W = dict(norm1_g=norm1_g, w_in=w_in, gate_b=gate_b, conv_w=conv_w, conv_b=conv_b, conv_ln_g=conv_ln_g, conv_ln_b=conv_ln_b,
             w_conv_out=w_conv_out, w_attn_out=w_attn_out, w_o=w_o, norm2_g=norm2_g, w_ffn_gate=w_ffn_gate, w_ffn_up=w_ffn_up,
             w_ffn_down=w_ffn_down, norm_f_g=norm_f_g)
    M = dict(norm1_g=m_norm1_g, w_in=m_w_in, gate_b=m_gate_b, conv_w=m_conv_w, conv_b=m_conv_b, conv_ln_g=m_conv_ln_g,
             conv_ln_b=m_conv_ln_b, w_conv_out=m_w_conv_out, w_attn_out=m_w_attn_out, w_o=m_w_o, norm2_g=m_norm2_g,
             w_ffn_gate=m_w_ffn_gate, w_ffn_up=m_w_ffn_up, w_ffn_down=m_w_ffn_down, norm_f_g=m_norm_f_g)
    V = dict(norm1_g=v_norm1_g, w_in=v_w_in, gate_b=v_gate_b, conv_w=v_conv_w, conv_b=v_conv_b, conv_ln_g=v_conv_ln_g,
             conv_ln_b=v_conv_ln_b, w_conv_out=v_w_conv_out, w_attn_out=v_w_attn_out, w_o=v_w_o, norm2_g=v_norm2_g,
             w_ffn_gate=v_w_ffn_gate, w_ffn_up=v_w_ffn_up, w_ffn_down=v_w_ffn_down, norm_f_g=v_norm_f_g)
    order = list(W)

    def as2d(n, a):
        a = a.reshape(a.shape[-2:])
        return a.T if n in TRANSPOSED else a

    def from2d(n, a):
        return (a.T if n in TRANSPOSED else a).reshape(W[n].shape)

    shard2d = {n: as2d(n, W[n]) for n in BIG}
    chip = 2 * lax.axis_index("x") + lax.axis_index("y")

    core = lax.axis_index("c").astype(jnp.int32).reshape(1)
    chip1 = chip.astype(jnp.int32).reshape(1)
    shards = {n: shard2d[n].astype(bf16) for n in BIG}

    def slot_for(s):
        return lax.dynamic_update_slice(lax.empty((N_CHIPS,) + s.shape, s.dtype), s[None], (chip, 0, 0))

    slots = {n: slot_for(s) for n, s in shards.items()}
    cw = jnp.pad(conv_w.reshape(CONV_K, D_MODEL // N_CHIPS), ((0, CONV_HALO - CONV_K), (0, 0)))
    w_in_full, cw4 = _allgather_weights([shards["w_in"], cw], [slots["w_in"], slot_for(cw)])
    conv_w_full = cw4.transpose(1, 0, 2).reshape(CONV_HALO, D_MODEL)[:CONV_K]

    loss, grad_x, grads, gsmall = _step(x, loss_target, norm1_g, gate_b, conv_w_full, conv_b, conv_ln_g, conv_ln_b, norm2_g,
                                        norm_f_g, w_in_full, shards, slots, chip1, core)

    pieces = [_rows128(loss, LOSS_ROWS)] + [_rows128(gsmall[n], SMALL_ROWS[n]) for n in SMALL] + [_rows128(gsmall["conv_w"], CONVW_ROWS)]
    full_rows, tot = _final_exchange([grads[n] for n in BIG], jnp.concatenate(pieces, axis=0))
    grads = dict(zip(BIG, full_rows))
    loss_out = tot[0, 0]
    row = LOSS_ROWS
    for n in SMALL:
        grads[n] = tot[row: row + W[n].size // LANES].reshape(W[n].shape)
        row += SMALL_ROWS[n]
    dcw = tot[row: row + CONV_K * D_MODEL // LANES].reshape(CONV_K, D_MODEL)
    grads["conv_w"] = lax.dynamic_slice(dcw, (0, chip * (D_MODEL // N_CHIPS)), (CONV_K, D_MODEL // N_CHIPS))

    delta, new_m, new_v = {}, {}, {}
    for ns, ws in _same_shape_groups(BIG, [shard2d[n] for n in BIG]):
        ds, nms, nvs = _adamw(ws, [grads[n] for n in ns], [as2d(n, M[n]) for n in ns], [as2d(n, V[n]) for n in ns], "adamw_" + ns[0])
        for n, d, nm, nv in zip(ns, ds, nms, nvs):
            delta[n], new_m[n], new_v[n], grads[n] = (from2d(n, t) for t in (d, nm, nv, grads[n]))

    def pack(src):
        return jnp.concatenate([_rows128(src[n], SMALL_ROWS[n]) for n in SMALL], axis=0)

    (d,), (nm,), (nv,) = _adamw([pack(W)], [pack(grads)], [pack(M)], [pack(V)], "adamw_small")
    row = 0
    for n in SMALL:
        k = W[n].size // LANES
        delta[n], new_m[n], new_v[n] = (t[row: row + k].reshape(W[n].shape) for t in (d, nm, nv))
        row += SMALL_ROWS[n]

    def pad32(a):
        return jnp.pad(a.reshape(CONV_K, D_MODEL // N_CHIPS), ((0, 1), (0, 0)))

    (d,), (nm,), (nv,) = _adamw([pad32(conv_w)], [pad32(grads["conv_w"])], [pad32(m_conv_w)], [pad32(v_conv_w)], "adamw_conv_w")
    delta["conv_w"], new_m["conv_w"], new_v["conv_w"] = (t[:CONV_K].reshape(conv_w.shape) for t in (d, nm, nv))
    grads["conv_w"] = grads["conv_w"].reshape(conv_w.shape)

    return (loss_out, grad_x, *[grads[n] for n in order], *[delta[n] for n in order],
            *[new_m[n] for n in order], *[new_v[n] for n in order])
```

```python
import functools
import math
from typing import Callable, NamedTuple

import numpy as np
import jax
import jax.numpy as jnp
from jax import lax
from jax.experimental import pallas as pl
from jax.experimental.pallas import tpu as pltpu

f32 = jnp.float32
bf16 = jnp.bfloat16
SDS = jax.ShapeDtypeStruct
MESH = pl.DeviceIdType.MESH

D_MODEL = 1024
SEQ = 2048
HEAD_DIM = 64
HEADS = 8
GROUPS = ((128, 1), (512, 4), (2048, 16))
GW = HEADS * HEAD_DIM
ATTN_W = len(GROUPS) * GW
Q_BLOCK = 128
CONV_K = 31
D_FF = 2816
IN_W = 3 * ATTN_W + 2 * D_MODEL + 2 * D_MODEL
N_CHIPS = 4
IN_S = IN_W // N_CHIPS
FF_S = D_FF // N_CHIPS
RMS_EPS = 1e-6
LN_EPS = 1e-5
LR, B1, B2, ADAM_EPS, WD, STEP = 0.001, 0.9, 0.999, 1e-08, 0.01, 10
NEG = -1e30
LANES = 128
VMEM_LIMIT = 48 * 2 ** 20
VMEM_LIMIT_BIG = 52 * 2 ** 20
CB = 512
UA_CB, UB_CB, GA_CB, GC_CB = 9, 11, 13, 15


def _alibi_slope_list(n):
    def pow2(m):
        start = 2.0 ** (-8.0 / m)
        return [start ** (i + 1) for i in range(m)]
    if math.log2(n).is_integer():
        return pow2(n)
    c = 2 ** math.floor(math.log2(n))
    return pow2(c) + _alibi_slope_list(2 * c)[0::2][: n - c]


_SLOPES = np.asarray(sorted(_alibi_slope_list(len(GROUPS) * HEADS), reverse=True), dtype=np.float32).reshape(len(GROUPS), HEADS)


def _cp(sem=None, vmem=VMEM_LIMIT):
    return pltpu.CompilerParams(dimension_semantics=sem, vmem_limit_bytes=vmem)


def _sigmoid(x):
    return 1.0 / (1.0 + jnp.exp(-x))


HBM = pl.BlockSpec(memory_space=pl.ANY)


class _Exchange(NamedTuple):
    ins: list
    out_shape: list
    scratch: list
    aliases: dict
    start: Callable
    wait: Callable


def _launch(body, *, name, grid, in_specs, out_specs, out_shape, args, scratch_shapes=(), sem=None, comm=None, vmem=VMEM_LIMIT):
    if comm is None:
        return pl.pallas_call(body, grid=grid, in_specs=in_specs, out_specs=out_specs, out_shape=out_shape,
                              scratch_shapes=list(scratch_shapes), name=name, compiler_params=_cp(sem, vmem))(*args)
    multi = isinstance(out_shape, (tuple, list))
    m_out = list(out_shape) if multi else [out_shape]
    m_ospec = list(out_specs) if multi else [out_specs]
    n_in, n_out, n_scr = len(in_specs), len(m_out), len(scratch_shapes)
    nc_in, nc_out = len(comm.ins), len(comm.out_shape)

    def hosted(*refs):
        bounds = np.cumsum([0, n_in, nc_in, n_out, nc_out, n_scr])
        mi, ci, mo, co, ms = (refs[a:b] for a, b in zip(bounds[:-1], bounds[1:]))
        cs = refs[bounds[-1]:]
        ids = [pl.program_id(a) for a in range(len(grid))]
        first = functools.reduce(jnp.logical_and, [i == 0 for i in ids])
        last = functools.reduce(jnp.logical_and, [i == g - 1 for i, g in zip(ids, grid)])

        @pl.when(first)
        def _():
            comm.start(ci, co, cs)

        body(*mi, *mo, *ms)

        @pl.when(last)
        def _():
            comm.wait(ci, co, cs)

    res = pl.pallas_call(
        hosted, grid=grid, in_specs=list(in_specs) + [HBM] * nc_in, out_specs=m_ospec + [HBM] * nc_out,
        out_shape=tuple(m_out) + tuple(comm.out_shape), scratch_shapes=list(scratch_shapes) + list(comm.scratch),
        input_output_aliases={n_in + i: n_out + o for i, o in comm.aliases.items()}, name=name + "_comm",
        compiler_params=pltpu.CompilerParams(dimension_semantics=("arbitrary",) * len(grid), vmem_limit_bytes=vmem,
                                             has_side_effects=True))(*args, *comm.ins)
    return (tuple(res[:n_out]) if multi else res[0]), tuple(res[n_out:])


def _run_exchange(ex, name):
    n_in = len(ex.ins)

    def body(*refs):
        ins, outs, sems = refs[:n_in], refs[n_in:n_in + len(ex.out_shape)], refs[n_in + len(ex.out_shape):]
        ex.start(ins, outs, sems)
        ex.wait(ins, outs, sems)

    return pl.pallas_call(body, in_specs=[HBM] * n_in, out_specs=[HBM] * len(ex.out_shape), out_shape=tuple(ex.out_shape),
                          scratch_shapes=list(ex.scratch), input_output_aliases=dict(ex.aliases), name=name,
                          compiler_params=pltpu.CompilerParams(has_side_effects=True))(*ex.ins)


def _rmsnorm(xv, gv):
    return xv * lax.rsqrt(jnp.mean(xv * xv, axis=-1, keepdims=True) + RMS_EPS) * gv


def _rms_fwd(x, g, name):
    T = x.shape[0]
    tm = 512

    def body(x_ref, g_ref, o_ref, ot_ref):
        hv = _rmsnorm(x_ref[...], g_ref[...])
        o_ref[...] = hv.astype(bf16)
        ot_ref[...] = hv.T.astype(bf16)

    row = pl.BlockSpec((tm, D_MODEL), lambda i: (i, 0))
    vec = pl.BlockSpec((1, D_MODEL), lambda i: (0, 0))
    return pl.pallas_call(body, grid=(T // tm,), in_specs=[row, vec], out_specs=[row, pl.BlockSpec((D_MODEL, tm), lambda i: (0, i))],
                          out_shape=(SDS((T, D_MODEL), bf16), SDS((D_MODEL, T), bf16)), name=name,
                          compiler_params=_cp(("parallel",)))(x, g)


def _rms_bwd_tile(dyv, xv, gv, dres):
    r = lax.rsqrt(jnp.mean(xv * xv, axis=-1, keepdims=True) + RMS_EPS)
    xh = xv * r
    dxh = dyv * gv
    dx = dres + r * (dxh - xh * jnp.mean(dxh * xh, axis=-1, keepdims=True))
    return dx, jnp.sum(dyv * xh, axis=0, keepdims=True)


def _accumulate(first, refs_parts):
    @pl.when(first)
    def _():
        for ref, part in refs_parts:
            ref[...] = part

    @pl.when(jnp.logical_not(first))
    def _():
        for ref, part in refs_parts:
            ref[...] += part


def _mm_rms_bwd(name, ops, x, g, dres, *, tm, a_spec, b_spec, dims, comm=None, vmem=VMEM_LIMIT):
    T = x.shape[0]
    n = len(ops)

    def body(*refs):
        ab_refs = refs[:2 * n]
        x_ref, g_ref, r_ref, dx_ref, dg_ref, acc = refs[2 * n:]
        m, s = pl.program_id(0), pl.program_id(1)
        p = sum(lax.dot_general(ab_refs[2 * i][...], ab_refs[2 * i + 1][...], dims, preferred_element_type=f32) for i in range(n))

        @pl.when(s == 0)
        def _():
            acc[...] = p

        @pl.when(s > 0)
        def _():
            acc[...] += p

        @pl.when(s == N_CHIPS - 1)
        def _():
            dx, part = _rms_bwd_tile(acc[...], x_ref[...], g_ref[...], r_ref[...])
            dx_ref[...] = dx
            _accumulate(m == 0, [(dg_ref, part)])

    row = pl.BlockSpec((tm, D_MODEL), lambda m, s: (m, 0))
    vec = pl.BlockSpec((1, D_MODEL), lambda m, s: (0, 0))
    return _launch(body, name=name, grid=(T // tm, N_CHIPS), in_specs=[a_spec, b_spec] * n + [row, vec, row], out_specs=[row, vec],
                   out_shape=(SDS((T, D_MODEL), f32), SDS((1, D_MODEL), f32)), args=tuple(t for ab in ops for t in ab) + (x, g, dres),
                   scratch_shapes=[pltpu.VMEM((tm, D_MODEL), f32)], sem=("arbitrary", "arbitrary"), comm=comm, vmem=vmem)


def _ffn_down_loss(ff, wd, x1, gf, target):
    T = x1.shape[0]
    tm = 1024
    nm = T // tm

    def body(f_ref, w_ref, x_ref, g_ref, t_ref, loss_ref, dx_ref, dg_ref, acc):
        m, s = pl.program_id(0), pl.program_id(1)
        p = jnp.dot(f_ref[...], w_ref[...], preferred_element_type=f32)

        @pl.when(s == 0)
        def _():
            acc[...] = p

        @pl.when(s > 0)
        def _():
            acc[...] += p

        @pl.when(s == N_CHIPS - 1)
        def _():
            xv = acc[...] + x_ref[...]
            gv = g_ref[...]
            r = lax.rsqrt(jnp.mean(xv * xv, axis=-1, keepdims=True) + RMS_EPS)
            xh = xv * r
            e = xh * gv - t_ref[...]
            part_l = jnp.broadcast_to(0.5 * jnp.sum(jnp.mean(e * e, axis=-1, keepdims=True), axis=0, keepdims=True), (1, LANES))
            dy = e * (1.0 / D_MODEL)
            dxh = dy * gv
            dx_ref[...] = r * (dxh - xh * jnp.mean(dxh * xh, axis=-1, keepdims=True))
            part_g = jnp.sum(dy * xh, axis=0, keepdims=True)

            @pl.when(m == 0)
            def _():
                loss_ref[...] = part_l
                dg_ref[...] = part_g

            @pl.when(m > 0)
            def _():
                loss_ref[...] += part_l
                dg_ref[...] += part_g

    row = pl.BlockSpec((tm, D_MODEL), lambda m, s: (m, 0))
    vec = pl.BlockSpec((1, D_MODEL), lambda m, s: (0, 0))
    return pl.pallas_call(
        body, grid=(nm, N_CHIPS),
        in_specs=[pl.BlockSpec((None, tm, FF_S), lambda m, s: (s, m, 0)), pl.BlockSpec((None, FF_S, D_MODEL), lambda m, s: (s, 0, 0)),
                  row, vec, row],
        out_specs=[pl.BlockSpec((1, LANES), lambda m, s: (0, 0)), row, vec],
        out_shape=(SDS((1, LANES), f32), SDS((T, D_MODEL), f32), SDS((1, D_MODEL), f32)),
        scratch_shapes=[pltpu.VMEM((tm, D_MODEL), f32)], name="ffn_down_loss", compiler_params=_cp(("arbitrary", "arbitrary")))(
            ff, wd, x1, gf, target)


def _merge_o_proj_rms(proj, gate_b, ya, yc, w_o, x, g2):
    T = x.shape[0]
    tm = 512

    def body(ga0, ga1, gc0, gc1, b_ref, ya_ref, yc_ref, w_ref, x_ref, g_ref, m_ref, x1_ref, h2_ref):
        bv = b_ref[...]
        ga = jnp.concatenate([ga0[...], ga1[...]], axis=-1).astype(f32)
        gc = jnp.concatenate([gc0[...], gc1[...]], axis=-1).astype(f32)
        merged = (_sigmoid(ga + bv[:, :D_MODEL]) * ya_ref[...].astype(f32)
                  + _sigmoid(gc + bv[:, D_MODEL:]) * yc_ref[...].astype(f32)).astype(bf16)
        m_ref[...] = merged
        x1 = jnp.dot(merged, w_ref[...], preferred_element_type=f32) + x_ref[...]
        x1_ref[...] = x1
        h2_ref[...] = _rmsnorm(x1, g_ref[...]).astype(bf16)

    row = pl.BlockSpec((tm, D_MODEL), lambda i: (i, 0))
    return pl.pallas_call(
        body, grid=(T // tm,),
        in_specs=[pl.BlockSpec((tm, CB), lambda i, cb=cb: (i, cb)) for cb in (GA_CB, GA_CB + 1, GC_CB, GC_CB + 1)]
        + [pl.BlockSpec((1, 2 * D_MODEL), lambda i: (0, 0)), row, row, pl.BlockSpec((D_MODEL, D_MODEL), lambda i: (0, 0)), row,
           pl.BlockSpec((1, D_MODEL), lambda i: (0, 0))],
        out_specs=[row, row, row], out_shape=(SDS((T, D_MODEL), bf16), SDS((T, D_MODEL), f32), SDS((T, D_MODEL), bf16)),
        name="merge_o_proj_rms", compiler_params=_cp(("parallel",)))(proj, proj, proj, proj, gate_b, ya, yc, w_o, x, g2)


def _conv_out_fwd(c1, g, b, w, comm=None):
    T = c1.shape[0]
    tm = 1024

    def body(c_ref, g_ref, b_ref, w_ref, c3_ref, y_ref):
        cv = c_ref[...]
        mu = jnp.mean(cv, axis=-1, keepdims=True)
        cc = cv - mu
        var = jnp.mean(cc * cc, axis=-1, keepdims=True)
        c2 = cc * lax.rsqrt(var + LN_EPS) * g_ref[...] + b_ref[...]
        c3 = (c2 * _sigmoid(c2)).astype(bf16)
        c3_ref[...] = c3
        y_ref[...] = jnp.dot(c3, w_ref[...], preferred_element_type=f32).astype(bf16)

    row = pl.BlockSpec((tm, D_MODEL), lambda i: (i, 0))
    vec = pl.BlockSpec((1, D_MODEL), lambda i: (0, 0))
    sd = SDS((T, D_MODEL), bf16)
    return _launch(body, name="conv_out_fwd", grid=(T // tm,), in_specs=[row, vec, vec, pl.BlockSpec((D_MODEL, D_MODEL), lambda i: (0, 0))],
                   out_specs=[row, row], out_shape=(sd, sd), args=(c1, g, b, w), sem=("parallel",), comm=comm)


def _conv_out_bwd(dyc, w, c1, g, b):
    T = c1.shape[0]
    tm = 1024

    def body(d_ref, w_ref, c_ref, g_ref, b_ref, dc_ref, dg_ref, db_ref):
        dc3 = lax.dot_general(d_ref[...], w_ref[...], NT, preferred_element_type=f32)
        cv = c_ref[...]
        gv = g_ref[...]
        mu = jnp.mean(cv, axis=-1, keepdims=True)
        cc = cv - mu
        var = jnp.mean(cc * cc, axis=-1, keepdims=True)
        rs = lax.rsqrt(var + LN_EPS)
        xh = cc * rs
        c2 = xh * gv + b_ref[...]
        sg = _sigmoid(c2)
        dc2 = dc3 * (sg * (1.0 + c2 * (1.0 - sg)))
        dxh = dc2 * gv
        dc_ref[...] = rs * (dxh - jnp.mean(dxh, axis=-1, keepdims=True) - xh * jnp.mean(dxh * xh, axis=-1, keepdims=True))
        _accumulate(pl.program_id(0) == 0, [(dg_ref, jnp.sum(dc2 * xh, axis=0, keepdims=True)), (db_ref, jnp.sum(dc2, axis=0, keepdims=True))])

    row = pl.BlockSpec((tm, D_MODEL), lambda i: (i, 0))
    vec = pl.BlockSpec((1, D_MODEL), lambda i: (0, 0))
    return pl.pallas_call(body, grid=(T // tm,), in_specs=[row, pl.BlockSpec((D_MODEL, D_MODEL), lambda i: (0, 0)), row, vec, vec],
                          out_specs=[row, vec, vec],
                          out_shape=(SDS((T, D_MODEL), f32), SDS((1, D_MODEL), f32), SDS((1, D_MODEL), f32)),
                          name="conv_out_bwd", compiler_params=_cp(("arbitrary",)))(dyc, w, c1, g, b)


NN = (((1,), (0,)), ((), ()))
NT = (((1,), (1,)), ((), ()))
TN = (((0,), (0,)), ((), ()))


def _mm(name, a, b, *, grid, a_spec, b_spec, o_spec, o_shape, o_dtype, dims, acc_shape=None, k_axis=None,
        res=None, res_spec=None, sem=None, comm=None):
    nk = 1 if k_axis is None else grid[k_axis]

    def body(*refs):
        if res is None:
            a_ref, b_ref, o_ref = refs[:3]
            r_ref, scr = None, refs[3:]
        else:
            a_ref, b_ref, r_ref, o_ref = refs[:4]
            scr = refs[4:]
        p = lax.dot_general(a_ref[...].astype(bf16), b_ref[...].astype(bf16), dims, preferred_element_type=f32)
        if nk == 1:
            if r_ref is not None:
                p = p + r_ref[...]
            o_ref[...] = p.astype(o_dtype)
            return
        acc = scr[0]
        k = pl.program_id(k_axis)

        @pl.when(k == 0)
        def _():
            acc[...] = p

        @pl.when(k > 0)
        def _():
            acc[...] += p

        @pl.when(k == nk - 1)
        def _():
            t = acc[...]
            if r_ref is not None:
                t = t + r_ref[...]
            o_ref[...] = t.astype(o_dtype)

    ins = [a, b] + ([] if res is None else [res])
    specs = [a_spec, b_spec] + ([] if res is None else [res_spec])
    scratch = [] if nk == 1 else [pltpu.VMEM(acc_shape, f32)]
    return _launch(body, name=name, grid=grid, in_specs=specs, out_specs=o_spec, out_shape=SDS(o_shape, o_dtype),
                   args=ins, scratch_shapes=scratch, sem=sem, comm=comm)


def _mm_nn_full(name, a, b, o_dtype, res=None, tm=1024, comm=None):
    T, K = a.shape
    N = b.shape[1]
    return _mm(name, a, b, grid=(T // tm,), a_spec=pl.BlockSpec((tm, K), lambda i: (i, 0)),
               b_spec=pl.BlockSpec((K, N), lambda i: (0, 0)), o_spec=pl.BlockSpec((tm, N), lambda i: (i, 0)),
               o_shape=(T, N), o_dtype=o_dtype, dims=NN, res=res,
               res_spec=None if res is None else pl.BlockSpec((tm, N), lambda i: (i, 0)), sem=("parallel",), comm=comm)


def _mm_nt_full(name, a, b, o_dtype, tm=1024):
    T, N = a.shape
    K = b.shape[0]
    return _mm(name, a, b, grid=(T // tm,), a_spec=pl.BlockSpec((tm, N), lambda i: (i, 0)),
               b_spec=pl.BlockSpec((K, N), lambda i: (0, 0)), o_spec=pl.BlockSpec((tm, K), lambda i: (i, 0)),
               o_shape=(T, K), o_dtype=o_dtype, dims=NT, sem=("parallel",))


def _mm_tn_tokens(name, a, b, o_dtype, tk=1024):
    T, K = a.shape
    N = b.shape[1]
    return _mm(name, a, b, grid=(T // tk,), a_spec=pl.BlockSpec((tk, K), lambda k: (k, 0)),
               b_spec=pl.BlockSpec((tk, N), lambda k: (k, 0)), o_spec=pl.BlockSpec((K, N), lambda k: (0, 0)),
               o_shape=(K, N), o_dtype=o_dtype, dims=TN, acc_shape=(K, N), k_axis=0, sem=("arbitrary",))


AO_S = D_MODEL // N_CHIPS


def _attn_out_cat(w_ref):
    return jnp.concatenate([w_ref[s] for s in range(N_CHIPS)], axis=-1)


def _attn_out_bwd_w(mix, dya):
    T = mix.shape[0]
    tk = 1024
    nk = T // tk

    def body(a_ref, d_ref, o_ref, acc):
        k = pl.program_id(0)
        p = lax.dot_general(a_ref[...], d_ref[...], TN, preferred_element_type=f32)

        @pl.when(k == 0)
        def _():
            acc[...] = p

        @pl.when(k > 0)
        def _():
            acc[...] += p

        @pl.when(k == nk - 1)
        def _():
            for s in range(N_CHIPS):
                o_ref[s] = acc[:, s * AO_S:(s + 1) * AO_S].astype(bf16)

    return pl.pallas_call(body, grid=(nk,),
                          in_specs=[pl.BlockSpec((tk, GW), lambda k: (k, 0)), pl.BlockSpec((tk, D_MODEL), lambda k: (k, 0))],
                          out_specs=pl.BlockSpec((N_CHIPS, GW, AO_S), lambda k: (0, 0, 0)), out_shape=SDS((N_CHIPS, GW, AO_S), bf16),
                          scratch_shapes=[pltpu.VMEM((GW, D_MODEL), f32)], name="d_w_attn_out",
                          compiler_params=_cp(("arbitrary",)))(mix, dya)


def _ffn_up(h2, wg, wu):
    T = h2.shape[0]
    tm = 1024

    def body(h_ref, wg_ref, wu_ref, a_ref, b_ref, f_ref):
        hv = h_ref[...]
        av = lax.dot_general(hv, wg_ref[...], NT, preferred_element_type=f32)
        bv = lax.dot_general(hv, wu_ref[...], NT, preferred_element_type=f32)
        a_ref[...] = av.astype(bf16)
        b_ref[...] = bv.astype(bf16)
        f_ref[...] = (av * _sigmoid(av) * bv).astype(bf16)

    wspec = pl.BlockSpec((None, FF_S, D_MODEL), lambda s, m: (s, 0, 0))
    ospec = pl.BlockSpec((None, tm, FF_S), lambda s, m: (s, m, 0))
    osd = SDS((N_CHIPS, T, FF_S), bf16)
    return pl.pallas_call(body, grid=(N_CHIPS, T // tm),
                          in_specs=[pl.BlockSpec((tm, D_MODEL), lambda s, m: (m, 0)), wspec, wspec],
                          out_specs=[ospec, ospec, ospec], out_shape=(osd, osd, osd), name="ffn_up",
                          compiler_params=_cp(("parallel", "parallel")))(h2, wg, wu)


def _ffn_down_bwd(dx2, wd, a, b):
    T = dx2.shape[0]
    tm = 1024

    def body(d_ref, w_ref, a_ref, b_ref, da_ref, db_ref):
        df = lax.dot_general(d_ref[...].astype(bf16), w_ref[...], NT, preferred_element_type=f32)
        av = a_ref[...].astype(f32)
        sg = _sigmoid(av)
        da_ref[...] = (df * b_ref[...].astype(f32) * (sg * (1.0 + av * (1.0 - sg)))).astype(bf16)
        db_ref[...] = (df * av * sg).astype(bf16)

    aspec = pl.BlockSpec((None, tm, FF_S), lambda m, s: (s, m, 0))
    osd = SDS((N_CHIPS, T, FF_S), bf16)
    return pl.pallas_call(body, grid=(T // tm, N_CHIPS),
                          in_specs=[pl.BlockSpec((tm, D_MODEL), lambda m, s: (m, 0)),
                                    pl.BlockSpec((None, FF_S, D_MODEL), lambda m, s: (s, 0, 0)), aspec, aspec],
                          out_specs=[aspec, aspec], out_shape=(osd, osd), name="ffn_down_bwd",
                          compiler_params=_cp(("parallel", "parallel")))(dx2, wd, a, b)


def _merge_bwd(dm, proj, gate_b, ya, yc):
    T = proj.shape[0]
    tm = 512

    def body(dm_ref, ga_ref, gc_ref, ba_ref, bc_ref, ya_ref, yc_ref, dya_ref, dyc_ref, dga_ref, dgc_ref, dba_ref, dbc_ref):
        dmv = dm_ref[...].astype(f32)
        sa = _sigmoid(ga_ref[...].astype(f32) + ba_ref[...])
        sc = _sigmoid(gc_ref[...].astype(f32) + bc_ref[...])
        dya_ref[...] = (dmv * sa).astype(bf16)
        dyc_ref[...] = (dmv * sc).astype(bf16)
        dga = dmv * ya_ref[...].astype(f32) * sa * (1.0 - sa)
        dgc = dmv * yc_ref[...].astype(f32) * sc * (1.0 - sc)
        dga_ref[...] = dga.astype(bf16)
        dgc_ref[...] = dgc.astype(bf16)
        pa = jnp.sum(dga, axis=0, keepdims=True)
        pc = jnp.sum(dgc, axis=0, keepdims=True)

        @pl.when(pl.program_id(1) == 0)
        def _():
            dba_ref[...] = pa
            dbc_ref[...] = pc

        @pl.when(pl.program_id(1) > 0)
        def _():
            dba_ref[...] += pa
            dbc_ref[...] += pc

    blk = pl.BlockSpec((tm, CB), lambda j, i: (i, j))
    vec = pl.BlockSpec((1, CB), lambda j, i: (0, j))
    big = SDS((T, D_MODEL), bf16)
    small = SDS((1, D_MODEL), f32)
    return pl.pallas_call(
        body, grid=(2, T // tm),
        in_specs=[blk, pl.BlockSpec((tm, CB), lambda j, i: (i, GA_CB + j)), pl.BlockSpec((tm, CB), lambda j, i: (i, GC_CB + j)),
                  vec, pl.BlockSpec((1, CB), lambda j, i: (0, 2 + j)), blk, blk],
        out_specs=[blk, blk, blk, blk, vec, vec], out_shape=(big, big, big, big, small, small), name="merge_bwd",
        compiler_params=_cp(("parallel", "arbitrary")))(dm, proj, proj, gate_b, gate_b, ya, yc)


CONV_TS = 512
CONV_HALO = 32
CONV_RC = 64
CONV_WIN = CONV_TS + CONV_HALO
SUBLANES = 8


def _fill_shifted(win, sh):
    for b in range(1, SUBLANES):
        sh[b - 1] = win[pl.ds(b, CONV_WIN - SUBLANES), :]


def _rows_at(win, sh, row):
    a, b = divmod(row, SUBLANES)
    if b == 0:
        return win[pl.ds(row, CONV_RC), :]
    return sh[b - 1, pl.ds(a * SUBLANES, CONV_RC), :]


def _glu_conv_fwd(proj3, w, bias, comm=None):
    B = proj3.shape[0]
    nt = SEQ // CONV_TS
    hb = CONV_TS // CONV_HALO

    def body(ua_ref, ub_ref, ha_ref, hb_ref, w_ref, b_ref, o_ref, win, sh):
        i = pl.program_id(2)
        c0 = ua_ref[...].astype(f32) * _sigmoid(ub_ref[...].astype(f32))
        halo = ha_ref[...].astype(f32) * _sigmoid(hb_ref[...].astype(f32))
        win[0:CONV_HALO, :] = jnp.where(i > 0, halo, 0.0)
        win[CONV_HALO:, :] = c0
        _fill_shifted(win, sh)
        for r0 in range(0, CONV_TS, CONV_RC):
            acc = jnp.zeros((CONV_RC, CB), f32) + b_ref[...]
            for k in range(CONV_K):
                acc = acc + _rows_at(win, sh, r0 + CONV_HALO - (CONV_K - 1) + k) * w_ref[k:k + 1, :]
            o_ref[r0:r0 + CONV_RC, :] = acc

    def cur(cb):
        return pl.BlockSpec((None, CONV_TS, CB), lambda b, j, i: (b, i, cb + j))

    def prev(cb):
        return pl.BlockSpec((None, CONV_HALO, CB), lambda b, j, i: (b, jnp.maximum(i * hb - 1, 0), cb + j))

    return _launch(
        body, name="glu_conv_fwd", grid=(B, 2, nt),
        in_specs=[cur(UA_CB), cur(UB_CB), prev(UA_CB), prev(UB_CB),
                  pl.BlockSpec((CONV_K, CB), lambda b, j, i: (0, j)), pl.BlockSpec((1, CB), lambda b, j, i: (0, j))],
        out_specs=pl.BlockSpec((None, CONV_TS, CB), lambda b, j, i: (b, i, j)),
        out_shape=SDS((B, SEQ, D_MODEL), f32), args=(proj3, proj3, proj3, proj3, w, bias),
        scratch_shapes=[pltpu.VMEM((CONV_WIN, CB), f32), pltpu.VMEM((SUBLANES - 1, CONV_WIN - SUBLANES, CB), f32)],
        sem=("parallel", "parallel", "parallel"), comm=comm)


def _glu_conv_bwd(dc1, proj3, w, comm=None):
    B = proj3.shape[0]
    nt = SEQ // CONV_TS
    hb = CONV_TS // CONV_HALO

    def body(d_ref, dn_ref, ua_ref, ub_ref, ha_ref, hb_ref, w_ref, dua_ref, dub_ref, dw_ref, db_ref, winc, wind, accw, shc, shd):
        b = pl.program_id(1)
        i = pl.program_id(2)
        first = jnp.logical_and(b == 0, i == 0)
        last = jnp.logical_and(b == B - 1, i == nt - 1)

        @pl.when(first)
        def _():
            accw[...] = jnp.zeros_like(accw)
            db_ref[...] = jnp.zeros_like(db_ref)

        halo = ha_ref[...].astype(f32) * _sigmoid(hb_ref[...].astype(f32))
        winc[0:CONV_HALO, :] = jnp.where(i > 0, halo, 0.0)
        winc[CONV_HALO:, :] = ua_ref[...].astype(f32) * _sigmoid(ub_ref[...].astype(f32))
        wind[0:CONV_TS, :] = d_ref[...]
        wind[CONV_TS:, :] = jnp.where(i < nt - 1, dn_ref[...], 0.0)
        db_ref[...] += jnp.sum(d_ref[...], axis=0, keepdims=True)
        _fill_shifted(winc, shc)
        _fill_shifted(wind, shd)
        for r0 in range(0, CONV_TS, CONV_RC):
            dc0 = jnp.zeros((CONV_RC, CB), f32)
            for k in range(CONV_K):
                dc0 = dc0 + _rows_at(wind, shd, r0 + (CONV_K - 1) - k) * w_ref[k:k + 1, :]
            uav = ua_ref[r0:r0 + CONV_RC, :].astype(f32)
            sg = _sigmoid(ub_ref[r0:r0 + CONV_RC, :].astype(f32))
            dua_ref[r0:r0 + CONV_RC, :] = (dc0 * sg).astype(bf16)
            dub_ref[r0:r0 + CONV_RC, :] = (dc0 * uav * sg * (1.0 - sg)).astype(bf16)
            dv = wind[r0:r0 + CONV_RC, :]
            for k in range(CONV_K):
                prod = dv * _rows_at(winc, shc, r0 + CONV_HALO - (CONV_K - 1) + k)
                accw[k] += jnp.sum(prod.reshape(CONV_RC // 8, 8, CB), axis=0)

        @pl.when(last)
        def _():
            for k in range(CONV_K):
                dw_ref[k:k + 1, :] = jnp.sum(accw[k], axis=0, keepdims=True)
            dw_ref[CONV_K:, :] = jnp.zeros((CONV_HALO - CONV_K, CB), f32)

    def cur(cb):
        return pl.BlockSpec((None, CONV_TS, CB), lambda j, b, i: (b, i, cb + j))

    def prev(cb):
        return pl.BlockSpec((None, CONV_HALO, CB), lambda j, b, i: (b, jnp.maximum(i * hb - 1, 0), cb + j))

    nxt = pl.BlockSpec((None, CONV_HALO, CB), lambda j, b, i: (b, jnp.minimum((i + 1) * hb, SEQ // CONV_HALO - 1), j))
    big = SDS((B, SEQ, D_MODEL), bf16)
    return _launch(
        body, name="glu_conv_bwd", grid=(2, B, nt),
        in_specs=[cur(0), nxt, cur(UA_CB), cur(UB_CB), prev(UA_CB), prev(UB_CB), pl.BlockSpec((CONV_K, CB), lambda j, b, i: (0, j))],
        out_specs=[cur(0), cur(0), pl.BlockSpec((CONV_HALO, CB), lambda j, b, i: (0, j)), pl.BlockSpec((1, CB), lambda j, b, i: (0, j))],
        out_shape=(big, big, SDS((CONV_HALO, D_MODEL), f32), SDS((1, D_MODEL), f32)),
        args=(dc1, dc1, proj3, proj3, proj3, proj3, w),
        scratch_shapes=[pltpu.VMEM((CONV_WIN, CB), f32), pltpu.VMEM((CONV_WIN, CB), f32), pltpu.VMEM((CONV_K, SUBLANES, CB), f32),
                        pltpu.VMEM((SUBLANES - 1, CONV_WIN - SUBLANES, CB), f32),
                        pltpu.VMEM((SUBLANES - 1, CONV_WIN - SUBLANES, CB), f32)],
        sem=("parallel", "arbitrary", "arbitrary"), comm=comm)


def _band(first, dil):
    kw = Q_BLOCK if first else 2 * Q_BLOCK
    qi = lax.broadcasted_iota(jnp.int32, (Q_BLOCK, kw), 0)
    kj = lax.broadcasted_iota(jnp.int32, (Q_BLOCK, kw), 1)
    rel = qi - kj + (0 if first else Q_BLOCK)
    valid = jnp.logical_and(rel >= 0, rel <= Q_BLOCK)
    return valid, rel.astype(f32) * float(dil)


def _bias(first, dil, slope):
    valid, dist = _band(first, dil)
    return jnp.where(valid, -slope * dist, NEG)


def _scores(q, k, bias):
    return lax.dot_general(q, k, NT, preferred_element_type=f32) * (HEAD_DIM ** -0.5) + bias


def _pair_cols(hp):
    return slice(hp * LANES, (hp + 1) * LANES)


def _half(x2, e):
    lane = lax.broadcasted_iota(jnp.int32, (1, LANES), 1)
    keep = (lane < HEAD_DIM) if e == 0 else (lane >= HEAD_DIM)
    return jnp.where(keep, x2, jnp.zeros_like(x2))


def _attn_fwd(g, q4, k4, v4, qcb, kcb, vcb, comm=None):
    _, dil = GROUPS[g]
    B, r, L, _ = q4.shape
    nb = L // Q_BLOCK
    slopes = [float(s) for s in _SLOPES[g]]

    def body(q_ref, k_ref, v_ref, o_ref, lse_ref, bias, s_scr, p_scr):
        lane = lax.broadcasted_iota(jnp.int32, (Q_BLOCK, LANES), 1)
        if nb > 1:
            for h in range(HEADS):
                bias[h] = _bias(False, dil, slopes[h])

        def block(n, first):
            q0 = 0 if first else pl.multiple_of(n * Q_BLOCK, Q_BLOCK)
            k0 = 0 if first else pl.multiple_of((n - 1) * Q_BLOCK, Q_BLOCK)
            kw = Q_BLOCK if first else 2 * Q_BLOCK
            for hp in range(HEADS // 2):
                q2 = q_ref[pl.ds(q0, Q_BLOCK), _pair_cols(hp)]
                k2 = k_ref[pl.ds(k0, kw), _pair_cols(hp)]
                for e in range(2):
                    h = 2 * hp + e
                    b_h = _bias(True, dil, slopes[h]) if first else bias[h]
                    s_scr[h, :, :kw] = _scores(_half(q2, e), k2, b_h)
            st = jnp.zeros((Q_BLOCK, LANES), f32)
            dens = jnp.ones((Q_BLOCK, LANES), f32)
            for h in range(HEADS):
                s = s_scr[h, :, :kw]
                m = jnp.max(s, axis=-1, keepdims=True)
                p = jnp.exp(s - m)
                den = jnp.sum(p, axis=-1, keepdims=True)
                p_scr[h, :, :kw] = p.astype(bf16)
                st = jnp.where(lane == h, m + jnp.log(den), st)
                dens = jnp.where(lane == h, den, dens)
            lse_ref[pl.ds(q0, Q_BLOCK), :] = st
            inv = 1.0 / dens
            for hp in range(HEADS // 2):
                v2 = v_ref[pl.ds(k0, kw), _pair_cols(hp)]
                o2 = sum(jnp.dot(p_scr[2 * hp + e, :, :kw], _half(v2, e), preferred_element_type=f32) * inv[:, 2 * hp + e:2 * hp + e + 1]
                         for e in range(2))
                o_ref[pl.ds(q0, Q_BLOCK), _pair_cols(hp)] = o2.astype(bf16)

        block(0, True)
        if nb > 1:
            def step(n, carry):
                block(n, False)
                return carry
            lax.fori_loop(1, nb, step, 0)

    def spec(cb):
        return pl.BlockSpec((None, None, L, GW), lambda b, c: (b, c, 0, cb))

    return _launch(
        body, name=f"attn_fwd_g{g}", grid=(B, r), in_specs=[spec(qcb), spec(kcb), spec(vcb)],
        out_specs=[spec(0), pl.BlockSpec((None, None, L, LANES), lambda b, c: (b, c, 0, 0))],
        out_shape=(SDS((B, r, L, GW), bf16), SDS((B, r, L, LANES), f32)), args=(q4, k4, v4),
        scratch_shapes=[pltpu.VMEM((HEADS, Q_BLOCK, 2 * Q_BLOCK), f32), pltpu.VMEM((HEADS, Q_BLOCK, 2 * Q_BLOCK), f32),
                        pltpu.VMEM((HEADS, Q_BLOCK, 2 * Q_BLOCK), bf16)],
        sem=("parallel", "parallel"), comm=comm)


def _attn_bwd(g, q4, k4, v4, qcb, kcb, vcb, do4, lse4, dl4, comm=None):
    _, dil = GROUPS[g]
    B, r, L, _ = q4.shape
    nb = L // Q_BLOCK
    slopes = [float(s) for s in _SLOPES[g]]
    scale = HEAD_DIM ** -0.5

    def body(q_ref, k_ref, v_ref, do_ref, lse_ref, dl_ref, dq_ref, dk_ref, dv_ref, dk_acc, dv_acc, bias, s_scr, dp_scr, p_scr, ds_scr):
        dk_acc[...] = jnp.zeros_like(dk_acc)
        dv_acc[...] = jnp.zeros_like(dv_acc)
        if nb > 1:
            for h in range(HEADS):
                bias[h] = _bias(False, dil, slopes[h])

        def block(n, first):
            q0 = 0 if first else pl.multiple_of(n * Q_BLOCK, Q_BLOCK)
            k0 = 0 if first else pl.multiple_of((n - 1) * Q_BLOCK, Q_BLOCK)
            kw = Q_BLOCK if first else 2 * Q_BLOCK
            for hp in range(HEADS // 2):
                q2 = q_ref[pl.ds(q0, Q_BLOCK), _pair_cols(hp)]
                k2 = k_ref[pl.ds(k0, kw), _pair_cols(hp)]
                v2 = v_ref[pl.ds(k0, kw), _pair_cols(hp)]
                do2 = do_ref[pl.ds(q0, Q_BLOCK), _pair_cols(hp)]
                for e in range(2):
                    h = 2 * hp + e
                    b_h = _bias(True, dil, slopes[h]) if first else bias[h]
                    s_scr[h, :, :kw] = _scores(_half(q2, e), k2, b_h)
                    dp_scr[h, :, :kw] = lax.dot_general(_half(do2, e), v2, NT, preferred_element_type=f32)
            for h in range(HEADS):
                p = jnp.exp(s_scr[h, :, :kw] - lse_ref[pl.ds(q0, Q_BLOCK), h:h + 1])
                p_scr[h, :, :kw] = p.astype(bf16)
                ds_scr[h, :, :kw] = (p * (dp_scr[h, :, :kw] - dl_ref[pl.ds(q0, Q_BLOCK), h:h + 1])).astype(bf16)
            for hp in range(HEADS // 2):
                cols = _pair_cols(hp)
                q2 = q_ref[pl.ds(q0, Q_BLOCK), cols]
                k2 = k_ref[pl.ds(k0, kw), cols]
                do2 = do_ref[pl.ds(q0, Q_BLOCK), cols]
                ds = [ds_scr[2 * hp + e, :, :kw] for e in range(2)]
                dq2 = sum(jnp.dot(ds[e], _half(k2, e), preferred_element_type=f32) for e in range(2))
                dq_ref[pl.ds(q0, Q_BLOCK), cols] = (dq2 * scale).astype(bf16)
                dk2 = sum(lax.dot_general(ds[e], _half(q2, e), TN, preferred_element_type=f32) for e in range(2))
                dk_acc[pl.ds(k0, kw), cols] += dk2 * scale
                dv2 = sum(lax.dot_general(p_scr[2 * hp + e, :, :kw], _half(do2, e), TN, preferred_element_type=f32) for e in range(2))
                dv_acc[pl.ds(k0, kw), cols] += dv2

        block(0, True)
        if nb > 1:
            def step(n, carry):
                block(n, False)
                return carry
            lax.fori_loop(1, nb, step, 0)
        dk_ref[...] = dk_acc[...].astype(bf16)
        dv_ref[...] = dv_acc[...].astype(bf16)

    def spec(cb):
        return pl.BlockSpec((None, None, L, GW), lambda b, c: (b, c, 0, cb))

    st = pl.BlockSpec((None, None, L, LANES), lambda b, c: (b, c, 0, 0))
    osd = SDS((B, r, L, GW), bf16)
    return _launch(
        body, name=f"attn_bwd_g{g}", grid=(B, r), in_specs=[spec(qcb), spec(kcb), spec(vcb), spec(0), st, st],
        out_specs=[spec(0), spec(0), spec(0)], out_shape=(osd, osd, osd), args=(q4, k4, v4, do4, lse4, dl4),
        scratch_shapes=[pltpu.VMEM((L, GW), f32), pltpu.VMEM((L, GW), f32)]
        + [pltpu.VMEM((HEADS, Q_BLOCK, 2 * Q_BLOCK), f32)] * 3 + [pltpu.VMEM((HEADS, Q_BLOCK, 2 * Q_BLOCK), bf16)] * 2,
        sem=("parallel", "parallel"), comm=comm)


RT = 512
RCH = GW // LANES
DILS = tuple(d for _, d in GROUPS[1:])


def _res_spec(r, width):
    return pl.BlockSpec((None, r, RT // r, width), lambda b, i, *_: (b, 0, i, 0))


def _tok_spec(width, cb=0):
    return pl.BlockSpec((None, RT, width), lambda b, i, *_: (b, i, cb))


def _to_residues(res_ref, scr, r, width):
    for c in range(r):
        for jj in range(width // LANES):
            res_ref[c, :, jj * LANES:(jj + 1) * LANES] = scr[jj, pl.ds(c, RT // r, stride=r), :].astype(res_ref.dtype)


def _from_residues(scr, res_ref, r, width):
    for c in range(r):
        for jj in range(width // LANES):
            scr[jj, pl.ds(c, RT // r, stride=r), :] = res_ref[c, :, jj * LANES:(jj + 1) * LANES].astype(f32)


def _qkv_to_residues(proj3, g):
    r = GROUPS[g][1]
    B = proj3.shape[0]

    def body(q_ref, k_ref, v_ref, o_ref, scr):
        for p, x_ref in enumerate((q_ref, k_ref, v_ref)):
            x = x_ref[...].astype(f32)
            for jj in range(RCH):
                scr[p * RCH + jj] = x[:, jj * LANES:(jj + 1) * LANES]
        _to_residues(o_ref, scr, r, ATTN_W)

    return pl.pallas_call(
        body, grid=(B, SEQ // RT), in_specs=[_tok_spec(GW, 3 * p + g) for p in range(3)], out_specs=_res_spec(r, ATTN_W),
        out_shape=SDS((B, r, SEQ // r, ATTN_W), bf16), scratch_shapes=[pltpu.VMEM((3 * RCH, RT, LANES), f32)],
        name=f"qkv_to_residues_g{g}", compiler_params=_cp(("parallel", "parallel")))(proj3, proj3, proj3)


def _attn_mix(o0, l0, o_res, l_res, w4):
    B = o0.shape[0]

    def body(o0_ref, l0_ref, o1_ref, o2_ref, l1_ref, l2_ref, w_ref, y_ref, lt_ref, lt1_ref, lt2_ref, ya_ref, so, sl):
        for gi, (o_ref, l_ref, r) in enumerate(((o1_ref, l1_ref, DILS[0]), (o2_ref, l2_ref, DILS[1]))):
            _from_residues(so.at[gi], o_ref, r, GW)
            _from_residues(sl.at[gi:gi + 1], l_ref, r, LANES)
        ls = [l0_ref[...], sl[0], sl[1]]
        m = functools.reduce(jnp.maximum, ls)
        ws = [jnp.exp(l - m) for l in ls]
        den = ws[0] + ws[1] + ws[2]
        alphas = [w / den for w in ws]
        lt = m + jnp.log(den)
        lt_ref[...] = lt
        sl[2] = lt
        _to_residues(lt1_ref, sl.at[2:3], DILS[0], LANES)
        _to_residues(lt2_ref, sl.at[2:3], DILS[1], LANES)
        for h in range(HEADS):
            cols = slice(h * HEAD_DIM, (h + 1) * HEAD_DIM)
            jj, lo = divmod(h * HEAD_DIM, LANES)
            acc = alphas[0][:, h:h + 1] * o0_ref[:, cols].astype(f32)
            for gi in range(2):
                acc = acc + alphas[gi + 1][:, h:h + 1] * so[gi, jj, :, lo:lo + HEAD_DIM]
            y_ref[:, cols] = acc.astype(bf16)
        ya_ref[...] = jnp.dot(y_ref[...], _attn_out_cat(w_ref), preferred_element_type=f32).astype(bf16)

    in_specs = [_tok_spec(GW), _tok_spec(LANES), _res_spec(DILS[0], GW), _res_spec(DILS[1], GW), _res_spec(DILS[0], LANES), _res_spec(DILS[1], LANES),
                pl.BlockSpec((N_CHIPS, GW, AO_S), lambda b, i: (0, 0, 0))]
    out_specs = [_tok_spec(GW), _tok_spec(LANES), _res_spec(DILS[0], LANES), _res_spec(DILS[1], LANES), _tok_spec(D_MODEL)]
    return pl.pallas_call(
        body, grid=(B, SEQ // RT), in_specs=in_specs, out_specs=out_specs,
        out_shape=(SDS((B, SEQ, GW), bf16), SDS((B, SEQ, LANES), f32)) + tuple(SDS((B, r, SEQ // r, LANES), f32) for r in DILS)
        + (SDS((B, SEQ, D_MODEL), bf16),),
        scratch_shapes=[pltpu.VMEM((2, RCH, RT, LANES), f32), pltpu.VMEM((3, RT, LANES), f32)],
        name="attn_mix", compiler_params=_cp(("parallel", "parallel")))(o0, l0, *o_res, *l_res, w4)


def _attn_out_bwd_x(dya, w4, mix, comm=None):
    B = dya.shape[0]

    def body(g_ref, w_ref, y_ref, d_ref, dl_ref, dl1_ref, dl2_ref, dm1_ref, dm2_ref, sx, sd):
        lane = lax.broadcasted_iota(jnp.int32, (RT, LANES), 1)
        acc = jnp.zeros((RT, LANES), f32)
        db = lax.dot_general(g_ref[...], _attn_out_cat(w_ref), NT, preferred_element_type=f32).astype(bf16)
        d_ref[...] = db
        dv = db.astype(f32)
        for jj in range(RCH):
            sx[jj] = dv[:, jj * LANES:(jj + 1) * LANES]
        for h in range(HEADS):
            cols = slice(h * HEAD_DIM, (h + 1) * HEAD_DIM)
            dl = jnp.sum(dv[:, cols] * y_ref[:, cols].astype(f32), axis=-1, keepdims=True)
            acc = jnp.where(lane == h, dl, acc)
        dl_ref[...] = acc
        sd[0] = acc
        _to_residues(dl1_ref, sd, DILS[0], LANES)
        _to_residues(dl2_ref, sd, DILS[1], LANES)
        _to_residues(dm1_ref, sx, DILS[0], GW)
        _to_residues(dm2_ref, sx, DILS[1], GW)

    return _launch(
        body, name="d_mix", grid=(B, SEQ // RT),
        in_specs=[_tok_spec(D_MODEL), pl.BlockSpec((N_CHIPS, GW, AO_S), lambda b, i: (0, 0, 0)), _tok_spec(GW)],
        out_specs=[_tok_spec(GW), _tok_spec(LANES), _res_spec(DILS[0], LANES), _res_spec(DILS[1], LANES), _res_spec(DILS[0], GW),
                   _res_spec(DILS[1], GW)],
        out_shape=(SDS((B, SEQ, GW), bf16), SDS((B, SEQ, LANES), f32)) + tuple(SDS((B, r, SEQ // r, LANES), f32) for r in DILS)
        + tuple(SDS((B, r, SEQ // r, GW), bf16) for r in DILS), args=(dya, w4, mix),
        scratch_shapes=[pltpu.VMEM((RCH, RT, LANES), f32), pltpu.VMEM((1, RT, LANES), f32)], sem=("parallel", "parallel"), comm=comm)


N_CB = IN_W // CB


def _assemble_dproj(dqkv, dua, dub, dga, dgc):
    B = dua.shape[0]
    ng = len(GROUPS)
    flat = [dqkv[g][p] for p in range(3) for g in range(ng)]
    wide = [dua, dub, dga, dgc]

    def body(*refs):
        srcs, wides, o_ref, scr = refs[:3 * ng], refs[3 * ng:3 * ng + 4], refs[3 * ng + 4], refs[3 * ng + 5]
        for jv in range(3 * ng):
            g = jv % ng
            if g == 0:
                o_ref[:, jv * GW:(jv + 1) * GW] = srcs[jv][...]
            else:
                _from_residues(scr, srcs[jv], GROUPS[g][1], GW)
                for jj in range(RCH):
                    o_ref[:, jv * GW + jj * LANES:jv * GW + (jj + 1) * LANES] = scr[jj].astype(bf16)
        for wv in range(4):
            lo = 3 * ATTN_W + wv * D_MODEL
            o_ref[:, lo:lo + D_MODEL] = wides[wv][...]

    in_specs = [_tok_spec(GW) if (jv % ng) == 0 else _res_spec(GROUPS[jv % ng][1], GW) for jv in range(3 * ng)]
    in_specs += [_tok_spec(D_MODEL)] * 4
    return pl.pallas_call(
        body, grid=(B, SEQ // RT), in_specs=in_specs, out_specs=_tok_spec(IN_W),
        out_shape=SDS((B, SEQ, IN_W), bf16), scratch_shapes=[pltpu.VMEM((RCH, RT, LANES), f32)],
        name="assemble_dproj", compiler_params=_cp(("parallel", "parallel")))(*flat, *wide)


def _same_shape_groups(names, arrays):
    groups = {}
    for n, a in zip(names, arrays):
        groups.setdefault(a.shape, ([], []))
        groups[a.shape][0].append(n)
        groups[a.shape][1].append(a)
    return list(groups.values())


def _add_pairs(parts, gots, core, name):
    k = len(parts)
    n, h, C = gots[0].shape

    def body(c_ref, *refs):
        for i in range(k):
            refs[2 * k + i][...] = (refs[i][...].astype(f32) + refs[k + i][...].astype(f32)).astype(bf16)

    blk = pl.BlockSpec((None, h, C), lambda s, c_ref: (s, 0, 0))
    own = pl.BlockSpec((None, h, C), lambda s, c_ref: (s, c_ref[0], 0))
    spec = pltpu.PrefetchScalarGridSpec(num_scalar_prefetch=1, grid=(n,), in_specs=[own] * k + [blk] * k, out_specs=[blk] * k)
    return pl.pallas_call(body, grid_spec=spec, out_shape=tuple(SDS((n, h, C), bf16) for _ in range(k)), name=name,
                          compiler_params=_cp(("parallel",)))(core, *parts, *gots)


def _sum4(sums, gots, chip, core, name):
    k = len(sums)
    _, h, C = sums[0].shape

    def body(s_ref, c_ref, *refs):
        for i in range(k):
            q_ref = refs[k + i]
            t = refs[i][...].astype(f32) + q_ref[0].astype(f32)
            t = t + q_ref[1].astype(f32)
            refs[2 * k + i][...] = t + q_ref[2].astype(f32)

    spec = pltpu.PrefetchScalarGridSpec(
        num_scalar_prefetch=2, grid=(1,),
        in_specs=[pl.BlockSpec((None, h, C), lambda i, s_ref, c_ref: (s_ref[0], 0, 0))] * k
        + [pl.BlockSpec((N_CHIPS - 1, h, C), lambda i, s_ref, c_ref: (0, 0, 0))] * k,
        out_specs=[pl.BlockSpec((h, C), lambda i, s_ref, c_ref: (c_ref[0], 0))] * k)
    return pl.pallas_call(body, grid_spec=spec, out_shape=tuple(SDS((2 * h, C), f32) for _ in range(k)), name=name,
                          compiler_params=_cp(("arbitrary",)))(chip, core, *sums, *gots)


def _adamw(ws, gs, ms, vs, name):
    k = len(ws)
    R, C = ws[0].shape
    rt = R
    for cand in (512, 256, 128, 64, 32, 16, 8):
        if R % cand == 0 and cand * C * 4 <= 2 ** 21:
            rt = cand
            break
    c1 = 1.0 / (1.0 - B1 ** STEP)
    c2 = 1.0 / (1.0 - B2 ** STEP)

    def body(*refs):
        for i in range(k):
            w_ref, g_ref, m_ref, v_ref = (refs[j * k + i] for j in range(4))
            d_ref, nm_ref, nv_ref = (refs[(4 + j) * k + i] for j in range(3))
            gv = g_ref[...]
            nm = B1 * m_ref[...] + (1.0 - B1) * gv
            nv = B2 * v_ref[...] + (1.0 - B2) * (gv * gv)
            nm_ref[...] = nm
            nv_ref[...] = nv
            d_ref[...] = -LR * ((nm * c1) / (jnp.sqrt(nv * c2) + ADAM_EPS) + WD * w_ref[...])

    blk = pl.BlockSpec((rt, C), lambda i: (i, 0))
    sd = SDS((R, C), f32)
    res = pl.pallas_call(body, grid=(R // rt,), in_specs=[blk] * (4 * k), out_specs=[blk] * (3 * k), out_shape=(sd,) * (3 * k),
                         name=name, compiler_params=_cp(("parallel",)))(*ws, *gs, *ms, *vs)
    return res[:k], res[k:2 * k], res[2 * k:]


def _coords():
    return lax.axis_index("x"), lax.axis_index("y"), lax.axis_index("c")


def _other_chips(x, y):
    return [(1 - x, y), (x, 1 - y), (1 - x, 1 - y)]


def _allgather_weights(shards, slots):
    n = len(shards)
    n_slot = 12

    def body(*refs):
        ins, outs = refs[:n], refs[2 * n:3 * n]
        send, recv = refs[3 * n:]
        x, y, c = _coords()
        me, cx, cy, cd = 2 * x + y, 2 * (1 - x) + y, 2 * x + (1 - y), 2 * (1 - x) + (1 - y)
        dev_x, dev_y, dev_s = (1 - x, y, c), (x, 1 - y, c), (x, y, 1 - c)
        started = []

        def copy(w, slot, src, dst, dev):
            return pltpu.make_async_remote_copy(src_ref=src, dst_ref=dst, send_sem=send.at[w, slot], recv_sem=recv.at[w, slot],
                                                device_id=dev, device_id_type=MESH)

        def go(cp):
            cp.start()
            started.append(cp)

        for w in range(n):
            q = shards[w].shape[0] // 4
            rows = [pl.ds(c * 2 * q + k * q, q) for k in range(2)]
            theirs = [pl.ds((1 - c) * 2 * q + k * q, q) for k in range(2)]
            own = [(ins[w].at[r, :], outs[w].at[me, r, :]) for r in rows]
            go(copy(w, 0, *own[0], dev_x))
            go(copy(w, 2, *own[1], dev_y))
            go(copy(w, 1, *own[1], dev_x))
            go(copy(w, 3, *own[0], dev_y))
            arrivals = [(0, cx, 0, (4, dev_y)), (2, cy, 1, (5, dev_x)), (1, cx, 1, None), (3, cy, 0, None), (4, cd, 0, None), (5, cd, 1, None)]
            for k, (slot, chip, quarter, onward) in enumerate(arrivals):
                blk = outs[w].at[chip, rows[quarter], :]
                copy(w, slot, blk, blk, dev_s).wait_recv()
                if onward is not None:
                    go(copy(w, onward[0], blk, blk, onward[1]))
                go(copy(w, 6 + k, blk, blk, dev_s))
            for k, (slot, chip, quarter, onward) in enumerate(arrivals):
                blk = outs[w].at[chip, theirs[quarter], :]
                copy(w, 6 + k, blk, blk, dev_s).wait_recv()
        for cp in started:
            cp.wait_send()

    return pl.pallas_call(
        body, in_specs=[HBM] * (2 * n), out_specs=[HBM] * n,
        out_shape=tuple(SDS((N_CHIPS,) + s.shape, s.dtype) for s in shards),
        input_output_aliases={n + w: w for w in range(n)},
        scratch_shapes=[pltpu.SemaphoreType.DMA((n, n_slot))] * 2,
        name="allgather_weights", compiler_params=pltpu.CompilerParams(has_side_effects=True))(*shards, *slots)


def _x_pair_exchange(parts):
    n = len(parts)
    halves = [p.shape[1] // 2 for p in parts]

    def copies(ins, outs, sems):
        send, recv = sems
        x, y, c = _coords()
        return [pltpu.make_async_remote_copy(src_ref=ins[w].at[:, pl.ds((1 - c) * halves[w], halves[w]), :], dst_ref=outs[w],
                                             send_sem=send.at[w], recv_sem=recv.at[w], device_id=(x, y, 1 - c), device_id_type=MESH)
                for w in range(n)]

    def start(ins, outs, sems):
        for cp in copies(ins, outs, sems):
            cp.start()

    def wait(ins, outs, sems):
        for cp in copies(ins, outs, sems):
            cp.wait()

    return _Exchange(ins=list(parts), out_shape=[SDS((N_CHIPS, p.shape[1] // 2, p.shape[2]), p.dtype) for p in parts],
                     scratch=[pltpu.SemaphoreType.DMA((n,))] * 2, aliases={}, start=start, wait=wait)


def _x_chip_exchange(sums):
    n = len(sums)

    def copies(ins, outs, sems):
        send, recv = sems
        x, y, c = _coords()
        return [pltpu.make_async_remote_copy(src_ref=ins[w].at[2 * px + py], dst_ref=outs[w].at[j], send_sem=send.at[w, j],
                                             recv_sem=recv.at[w, j], device_id=(px, py, c), device_id_type=MESH)
                for w in range(n) for j, (px, py) in enumerate(_other_chips(x, y))]

    def start(ins, outs, sems):
        for cp in copies(ins, outs, sems):
            cp.start()

    def wait(ins, outs, sems):
        for cp in copies(ins, outs, sems):
            cp.wait()

    return _Exchange(ins=list(sums), out_shape=[SDS((N_CHIPS - 1,) + s.shape[1:], s.dtype) for s in sums],
                     scratch=[pltpu.SemaphoreType.DMA((n, 3)), pltpu.SemaphoreType.DMA((n, 3))], aliases={}, start=start, wait=wait)


def _x_gather_ici(shards, slots):
    n = len(shards)
    halves = [s.shape[0] // 2 for s in shards]

    def copies(ins, outs, sems):
        send, recv = sems
        x, y, c = _coords()
        me = 2 * x + y
        out = []
        for w in range(n):
            mine = pl.ds(c * halves[w], halves[w])
            for j, (px, py) in enumerate(_other_chips(x, y)):
                snd = pltpu.make_async_remote_copy(src_ref=ins[w].at[mine, :], dst_ref=outs[w].at[me, mine, :], send_sem=send.at[w, j],
                                                   recv_sem=recv.at[w, j], device_id=(px, py, c), device_id_type=MESH)
                got = outs[w].at[2 * px + py, mine, :]
                rcv = pltpu.make_async_remote_copy(src_ref=got, dst_ref=got, send_sem=send.at[w, j], recv_sem=recv.at[w, j],
                                                   device_id=(px, py, c), device_id_type=MESH)
                out.append((snd, rcv))
        return out

    def start(ins, outs, sems):
        for snd, _ in copies(ins, outs, sems):
            snd.start()

    def wait(ins, outs, sems):
        for snd, rcv in copies(ins, outs, sems):
            rcv.wait_recv()
            snd.wait_send()

    return _Exchange(ins=list(shards) + list(slots), out_shape=[SDS(s.shape, s.dtype) for s in slots],
                     scratch=[pltpu.SemaphoreType.DMA((n, 3)), pltpu.SemaphoreType.DMA((n, 3))],
                     aliases={n + w: w for w in range(n)}, start=start, wait=wait)


def _x_gather_fwd(bufs):
    n = len(bufs)
    halves = [b.shape[1] // 2 for b in bufs]

    def copies(ins, outs, sems):
        send, recv = sems
        x, y, c = _coords()
        out = []
        for w in range(n):
            for j, (px, py) in enumerate(_other_chips(x, y)):
                mine = outs[w].at[2 * px + py, pl.ds(c * halves[w], halves[w]), :]
                theirs = outs[w].at[2 * px + py, pl.ds((1 - c) * halves[w], halves[w]), :]
                snd = pltpu.make_async_remote_copy(src_ref=mine, dst_ref=mine, send_sem=send.at[w, j], recv_sem=recv.at[w, j],
                                                   device_id=(x, y, 1 - c), device_id_type=MESH)
                rcv = pltpu.make_async_remote_copy(src_ref=theirs, dst_ref=theirs, send_sem=send.at[w, j], recv_sem=recv.at[w, j],
                                                   device_id=(x, y, 1 - c), device_id_type=MESH)
                out.append((snd, rcv))
        return out

    def start(ins, outs, sems):
        for snd, _ in copies(ins, outs, sems):
            snd.start()

    def wait(ins, outs, sems):
        for snd, rcv in copies(ins, outs, sems):
            rcv.wait_recv()
            snd.wait_send()

    return _Exchange(ins=list(bufs), out_shape=[SDS(b.shape, b.dtype) for b in bufs],
                     scratch=[pltpu.SemaphoreType.DMA((n, 3)), pltpu.SemaphoreType.DMA((n, 3))],
                     aliases={w: w for w in range(n)}, start=start, wait=wait)


def _final_exchange(bufs, vec):
    n = len(bufs)
    R = vec.shape[0]
    nd = 8

    def body(*refs):
        v_ref = refs[n]
        outs = refs[n + 1:2 * n + 1]
        o_ref, sbuf, send, recv, ssend, srecv = refs[2 * n + 1:]
        x, y, c = _coords()
        me = 4 * x + 2 * y + c
        sbuf[me] = v_ref[...]
        small = []
        for k in range(1, nd):
            kx, ky, kc = (k >> 2) & 1, (k >> 1) & 1, k & 1
            tx = x + kx - 2 * x * kx
            ty = y + ky - 2 * y * ky
            tc = c + kc - 2 * c * kc
            cp = pltpu.make_async_remote_copy(src_ref=v_ref, dst_ref=sbuf.at[me], send_sem=ssend.at[k], recv_sem=srecv.at[k],
                                              device_id=(tx, ty, tc), device_id_type=MESH)
            cp.start()
            small.append((cp, 4 * tx + 2 * ty + tc))
        cps = []
        for w in range(n):
            h = bufs[w].shape[0] // 2
            rows = outs[w].at[pl.ds(c * h, h), :]
            rc = pltpu.make_async_remote_copy(src_ref=rows, dst_ref=rows, send_sem=send.at[w], recv_sem=recv.at[w],
                                              device_id=(x, y, 1 - c), device_id_type=MESH)
            rc.start()
            cps.append(rc)
        for k, (cp, src) in zip(range(1, nd), small):
            pltpu.make_async_remote_copy(src_ref=v_ref, dst_ref=sbuf.at[src], send_sem=ssend.at[k], recv_sem=srecv.at[k],
                                         device_id=(x, y, c), device_id_type=MESH).wait_recv()
        for cp, _ in small:
            cp.wait_send()
        acc = sbuf[0]
        for d in range(1, nd):
            acc = acc + sbuf[d]
        o_ref[...] = acc
        for w, rc in enumerate(cps):
            h = bufs[w].shape[0] // 2
            other = outs[w].at[pl.ds((1 - c) * h, h), :]
            pltpu.make_async_remote_copy(src_ref=other, dst_ref=other, send_sem=send.at[w], recv_sem=recv.at[w],
                                         device_id=(x, y, 1 - c), device_id_type=MESH).wait_recv()
            rc.wait_send()

    vm = pl.BlockSpec(memory_space=pltpu.VMEM)
    res = pl.pallas_call(body, in_specs=[HBM] * n + [vm], out_specs=[HBM] * n + [vm],
                         out_shape=tuple(SDS(a.shape, a.dtype) for a in bufs) + (SDS((R, LANES), f32),),
                         input_output_aliases={w: w for w in range(n)},
                         scratch_shapes=[pltpu.VMEM((nd, R, LANES), f32)] + [pltpu.SemaphoreType.DMA((n,))] * 2
                         + [pltpu.SemaphoreType.DMA((nd,))] * 2, name="final_exchange",
                         compiler_params=pltpu.CompilerParams(has_side_effects=True))(*bufs, vec)
    return res[:n], res[n]


def _rows128(a, rows):
    flat = a.reshape(-1)
    return jnp.pad(flat, (0, rows * LANES - flat.shape[0])).reshape(rows, LANES)


GATHER_1 = ("w_conv_out", "w_attn_out", "w_o", "w_ffn_gate")
GATHER_2 = ("w_ffn_up", "w_ffn_down")
REDUCE_A = ("w_ffn_gate", "w_ffn_up", "w_ffn_down")
REDUCE_B = ("w_conv_out", "w_attn_out", "w_o")
REDUCE_C = ("w_in",)


def _step(x, target, norm1_g, gate_b, conv_w, conv_b, conv_ln_g, conv_ln_b, norm2_g, norm_f_g, w_in, shards, slots, chip1, core):
    B = x.shape[0]
    T = B * SEQ
    xf = x.reshape(T, D_MODEL)
    tf = target.reshape(T, D_MODEL)

    h, h_t = _rms_fwd(xf, norm1_g, "rms1_fwd")
    proj, got1 = _mm("in_proj", h, w_in, grid=(N_CHIPS, T // 1024),
                     a_spec=pl.BlockSpec((1024, D_MODEL), lambda s, m: (m, 0)),
                     b_spec=pl.BlockSpec((None, D_MODEL, IN_S), lambda s, m: (s, 0, 0)),
                     o_spec=pl.BlockSpec((1024, IN_S), lambda s, m: (m, s)), o_shape=(T, IN_W), o_dtype=bf16, dims=NN,
                     comm=_x_gather_ici([shards[n] for n in GATHER_1], [slots[n] for n in GATHER_1]))
    proj3 = proj.reshape(B, SEQ, IN_W)
    proj4 = proj.reshape(B, 1, SEQ, IN_W)

    qkv = [None] + [_qkv_to_residues(proj3, g) for g in range(1, len(GROUPS))]

    def qkv_args(g):
        return (proj4, proj4, proj4, 0, 3, 6) if g == 0 else (qkv[g], qkv[g], qkv[g], 0, 1, 2)

    (o4_0, l4_0), full1 = _attn_fwd(0, *qkv_args(0), comm=_x_gather_fwd(list(got1)))
    o4_1, l4_1 = _attn_fwd(1, *qkv_args(1))
    o4_2, l4_2 = _attn_fwd(2, *qkv_args(2))
    full = dict(zip(GATHER_1, full1))
    w_conv_out, w_attn_out, w_o, w_gate = (full[n] for n in GATHER_1)
    w_conv_out_f = w_conv_out.reshape(D_MODEL, D_MODEL)
    w_o_f = w_o.reshape(D_MODEL, D_MODEL)
    mix3, lse3, lse_r1, lse_r2, y_attn = _attn_mix(o4_0.reshape(B, SEQ, GW), l4_0.reshape(B, SEQ, LANES), [o4_1, o4_2], [l4_1, l4_2],
                                                   w_attn_out)
    mix = mix3.reshape(T, GW)
    y_attn = y_attn.reshape(T, D_MODEL)

    c1, got2 = _glu_conv_fwd(proj3, conv_w, conv_b, comm=_x_gather_ici([shards[n] for n in GATHER_2], [slots[n] for n in GATHER_2]))
    c1 = c1.reshape(T, D_MODEL)
    (c3, y_conv), (w_up, w_down) = _conv_out_fwd(c1, conv_ln_g, conv_ln_b, w_conv_out_f, comm=_x_gather_fwd(list(got2)))

    merged, x1, h2 = _merge_o_proj_rms(proj, gate_b, y_attn, y_conv, w_o_f, xf, norm2_g)
    fa, fb, ff = _ffn_up(h2, w_gate, w_up)
    loss, dx2, d_gf = _ffn_down_loss(ff, w_down, x1, norm_f_g.reshape(1, D_MODEL), tf)

    d_w_down = _mm("d_w_down", ff, dx2, grid=(N_CHIPS, T // 1024),
                   a_spec=pl.BlockSpec((None, 1024, FF_S), lambda s, k: (s, k, 0)),
                   b_spec=pl.BlockSpec((1024, D_MODEL), lambda s, k: (k, 0)),
                   o_spec=pl.BlockSpec((None, FF_S, D_MODEL), lambda s, k: (s, 0, 0)), o_shape=(N_CHIPS, FF_S, D_MODEL),
                   o_dtype=bf16, dims=TN, acc_shape=(FF_S, D_MODEL), k_axis=1, sem=("parallel", "arbitrary"))
    da, db = _ffn_down_bwd(dx2, w_down, fa, fb)

    def d_w_ff(name, dz):
        return _mm(name, dz, h2, grid=(N_CHIPS, T // 2048),
                   a_spec=pl.BlockSpec((None, 2048, FF_S), lambda s, k: (s, k, 0)),
                   b_spec=pl.BlockSpec((2048, D_MODEL), lambda s, k: (k, 0)),
                   o_spec=pl.BlockSpec((None, FF_S, D_MODEL), lambda s, k: (s, 0, 0)), o_shape=(N_CHIPS, FF_S, D_MODEL),
                   o_dtype=bf16, dims=TN, acc_shape=(FF_S, D_MODEL), k_axis=1, sem=("parallel", "arbitrary"))

    d_w_gate = d_w_ff("d_w_gate", da)
    d_w_up = d_w_ff("d_w_up", db)
    part = dict(w_ffn_gate=d_w_gate, w_ffn_up=d_w_up, w_ffn_down=d_w_down)

    def pair_sums(names, got):
        out = {}
        for ns, gs in _same_shape_groups(names, got):
            out.update(zip(ns, _add_pairs([part[n] for n in ns], gs, core, "pair_sum_" + ns[0])))
        return [out[n] for n in names]

    (dx1, d_g2), got = _mm_rms_bwd("ffn_dh2", [(da, w_gate), (db, w_up)], x1, norm2_g, dx2, tm=1024,
                                   a_spec=pl.BlockSpec((None, 1024, FF_S), lambda m, s: (s, m, 0)),
                                   b_spec=pl.BlockSpec((None, FF_S, D_MODEL), lambda m, s: (s, 0, 0)), dims=NN,
                                   comm=_x_pair_exchange([part[n] for n in REDUCE_A]), vmem=VMEM_LIMIT_BIG)
    sums_a = pair_sums(REDUCE_A, got)

    d_w_o = _mm_tn_tokens("d_w_o", merged, dx1, bf16).reshape(N_CHIPS, D_MODEL // N_CHIPS, D_MODEL)
    dmerged = _mm_nt_full("d_merged", dx1, w_o_f, bf16)
    dya, dyc, dga, dgc, d_gba, d_gbc = _merge_bwd(dmerged, proj, gate_b, y_attn, y_conv)

    d_w_conv_out = _mm_tn_tokens("d_w_conv_out", c3, dyc, bf16).reshape(N_CHIPS, D_MODEL // N_CHIPS, D_MODEL)
    dc1, d_ln_g, d_ln_b = _conv_out_bwd(dyc, w_conv_out_f, c1, conv_ln_g, conv_ln_b)
    (dua, dub, d_conv_w, d_conv_b), got_a = _glu_conv_bwd(dc1.reshape(B, SEQ, D_MODEL), proj3, conv_w, comm=_x_chip_exchange(sums_a))

    d_w_attn_out = _attn_out_bwd_w(mix, dya)
    part.update(w_conv_out=d_w_conv_out, w_attn_out=d_w_attn_out, w_o=d_w_o)
    (dmix3, delta3, delta_r1, delta_r2, dmix_r1, dmix_r2), got = _attn_out_bwd_x(
        dya.reshape(B, SEQ, D_MODEL), w_attn_out, mix3, comm=_x_pair_exchange([part[n] for n in REDUCE_B]))
    sums_b = pair_sums(REDUCE_B, got)
    one = (B, 1, SEQ)
    (dq0, dk0, dv0), got_b = _attn_bwd(0, *qkv_args(0), dmix3.reshape(one + (GW,)), lse3.reshape(one + (LANES,)),
                                       delta3.reshape(one + (LANES,)), comm=_x_chip_exchange(sums_b))
    dqkv = [tuple(t.reshape(B, SEQ, GW) for t in (dq0, dk0, dv0)),
            _attn_bwd(1, *qkv_args(1), dmix_r1, lse_r1, delta_r1), _attn_bwd(2, *qkv_args(2), dmix_r2, lse_r2, delta_r2)]
    dproj = _assemble_dproj(dqkv, dua, dub, dga.reshape(B, SEQ, D_MODEL), dgc.reshape(B, SEQ, D_MODEL)).reshape(T, IN_W)

    d_w_in = _mm("d_w_in", h_t, dproj, grid=(N_CHIPS, T // 1024),
                 a_spec=pl.BlockSpec((D_MODEL, 1024), lambda s, k: (0, k)),
                 b_spec=pl.BlockSpec((1024, IN_S), lambda s, k: (k, s)),
                 o_spec=pl.BlockSpec((None, D_MODEL, IN_S), lambda s, k: (s, 0, 0)), o_shape=(N_CHIPS, D_MODEL, IN_S),
                 o_dtype=bf16, dims=NN, acc_shape=(D_MODEL, IN_S), k_axis=1, sem=("parallel", "arbitrary"))
    part.update(w_in=d_w_in)
    sums_c = pair_sums(REDUCE_C, _run_exchange(_x_pair_exchange([d_w_in]), "grad_pair_exchange_c"))
    (dx, d_g1), got_c = _mm_rms_bwd("d_h", [(dproj, w_in)], xf, norm1_g, dx1, tm=512,
                                    a_spec=pl.BlockSpec((512, IN_S), lambda m, s: (m, s)),
                                    b_spec=pl.BlockSpec((None, D_MODEL, IN_S), lambda m, s: (s, 0, 0)), dims=NT,
                                    comm=_x_chip_exchange(sums_c))

    names = REDUCE_A + REDUCE_B + REDUCE_C
    sums = dict(zip(names, sums_a + sums_b + sums_c))
    halves = {}
    for ns, gs in _same_shape_groups(names, got_a + got_b + got_c):
        halves.update(zip(ns, _sum4([sums[n] for n in ns], gs, chip1, core, "chip_sum_" + ns[0])))
    big = {n: halves[n] for n in names}
    small = dict(norm1_g=d_g1, gate_b=jnp.concatenate([d_gba, d_gbc], axis=-1), conv_b=d_conv_b, conv_ln_g=d_ln_g,
                 conv_ln_b=d_ln_b, norm2_g=d_g2, norm_f_g=d_gf, conv_w=d_conv_w)
    return loss, dx.reshape(B, SEQ, D_MODEL), big, small


BIG = ("w_in", "w_conv_out", "w_attn_out", "w_o", "w_ffn_gate", "w_ffn_up", "w_ffn_down")
TRANSPOSED = ("w_ffn_gate", "w_ffn_up")
SMALL = ("norm1_g", "gate_b", "conv_b", "conv_ln_g", "conv_ln_b", "norm2_g", "norm_f_g")
SMALL_ROWS = {"norm1_g": 8, "gate_b": 16, "conv_b": 8, "conv_ln_g": 8, "conv_ln_b": 8, "norm2_g": 8, "norm_f_g": 8}
LOSS_ROWS = 8
CONVW_ROWS = 32 * D_MODEL // LANES


def kernel(x, norm1_g, w_in, gate_b, conv_w, conv_b, conv_ln_g, conv_ln_b, w_conv_out, w_attn_out, w_o, norm2_g, w_ffn_gate, w_ffn_up, w_ffn_down, norm_f_g, loss_target, m_norm1_g, m_w_in, m_gate_b, m_conv_w, m_conv_b, m_conv_ln_g, m_conv_ln_b, m_w_conv_out, m_w_attn_out, m_w_o, m_norm2_g, m_w_ffn_gate, m_w_ffn_up, m_w_ffn_down, m_norm_f_g, v_norm1_g, v_w_in, v_gate_b, v_conv_w, v_conv_b, v_conv_ln_g, v_conv_ln_b, v_w_conv_out, v_w_attn_out, v_w_o, v_norm2_g, v_w_ffn_gate, v_w_ffn_up, v_w_ffn_down, v_norm_f_g):
    W = dict(norm1_g=norm1_g, w_in=w_in, gate_b=gate_b, conv_w=conv_w, conv_b=conv_b, conv_ln_g=conv_ln_g, conv_ln_b=conv_ln_b,
             w_conv_out=w_conv_out, w_attn_out=w_attn_out, w_o=w_o, norm2_g=norm2_g, w_ffn_gate=w_ffn_gate, w_ffn_up=w_ffn_up,
             w_ffn_down=w_ffn_down, norm_f_g=norm_f_g)
    M = dict(norm1_g=m_norm1_g, w_in=m_w_in, gate_b=m_gate_b, conv_w=m_conv_w, conv_b=m_conv_b, conv_ln_g=m_conv_ln_g,
             conv_ln_b=m_conv_ln_b, w_conv_out=m_w_conv_out, w_attn_out=m_w_attn_out, w_o=m_w_o, norm2_g=m_norm2_g,
             w_ffn_gate=m_w_ffn_gate, w_ffn_up=m_w_ffn_up, w_ffn_down=m_w_ffn_down, norm_f_g=m_norm_f_g)
    V = dict(norm1_g=v_norm1_g, w_in=v_w_in, gate_b=v_gate_b, conv_w=v_conv_w, conv_b=v_conv_b, conv_ln_g=v_conv_ln_g,
             conv_ln_b=v_conv_ln_b, w_conv_out=v_w_conv_out, w_attn_out=v_w_attn_out, w_o=v_w_o, norm2_g=v_norm2_g,
             w_ffn_gate=v_w_ffn_gate, w_ffn_up=v_w_ffn_up, w_ffn_down=v_w_ffn_down, norm_f_g=v_norm_f_g)
    order = list(W)

    def as2d(n, a):
        a = a.reshape(a.shape[-2:])
        return a.T if n in TRANSPOSED else a

    def from2d(n, a):
        return (a.T if n in TRANSPOSED else a).reshape(W[n].shape)

    shard2d = {n: as2d(n, W[n]) for n in BIG}
    chip = 2 * lax.axis_index("x") + lax.axis_index("y")

    core = lax.axis_index("c").astype(jnp.int32).reshape(1)
    chip1 = chip.astype(jnp.int32).reshape(1)
    shards = {n: shard2d[n].astype(bf16) for n in BIG}

    def slot_for(s):
        return lax.dynamic_update_slice(lax.empty((N_CHIPS,) + s.shape, s.dtype), s[None], (chip, 0, 0))

    slots = {n: slot_for(s) for n, s in shards.items()}
    cw = jnp.pad(conv_w.reshape(CONV_K, D_MODEL // N_CHIPS), ((0, CONV_HALO - CONV_K), (0, 0)))
    w_in_full, cw4 = _allgather_weights([shards["w_in"], cw], [slots["w_in"], slot_for(cw)])
    conv_w_full = cw4.transpose(1, 0, 2).reshape(CONV_HALO, D_MODEL)[:CONV_K]

    loss, grad_x, grads, gsmall = _step(x, loss_target, norm1_g, gate_b, conv_w_full, conv_b, conv_ln_g, conv_ln_b, norm2_g,
                                        norm_f_g, w_in_full, shards, slots, chip1, core)

    pieces = [_rows128(loss, LOSS_ROWS)] + [_rows128(gsmall[n], SMALL_ROWS[n]) for n in SMALL] + [_rows128(gsmall["conv_w"], CONVW_ROWS)]
    full_rows, tot = _final_exchange([grads[n] for n in BIG], jnp.concatenate(pieces, axis=0))
    grads = dict(zip(BIG, full_rows))
    loss_out = tot[0, 0]
    row = LOSS_ROWS
    for n in SMALL:
        grads[n] = tot[row: row + W[n].size // LANES].reshape(W[n].shape)
        row += SMALL_ROWS[n]
    dcw = tot[row: row + CONV_K * D_MODEL // LANES].reshape(CONV_K, D_MODEL)
    grads["conv_w"] = lax.dynamic_slice(dcw, (0, chip * (D_MODEL // N_CHIPS)), (CONV_K, D_MODEL // N_CHIPS))

    delta, new_m, new_v = {}, {}, {}
    for ns, ws in _same_shape_groups(BIG, [shard2d[n] for n in BIG]):
        ds, nms, nvs = _adamw(ws, [grads[n] for n in ns], [as2d(n, M[n]) for n in ns], [as2d(n, V[n]) for n in ns], "adamw_" + ns[0])
        for n, d, nm, nv in zip(ns, ds, nms, nvs):
            delta[n], new_m[n], new_v[n], grads[n] = (from2d(n, t) for t in (d, nm, nv, grads[n]))

    def pack(src):
        return jnp.concatenate([_rows128(src[n], SMALL_ROWS[n]) for n in SMALL], axis=0)

    (d,), (nm,), (nv,) = _adamw([pack(W)], [pack(grads)], [pack(M)], [pack(V)], "adamw_small")
    row = 0
    for n in SMALL:
        k = W[n].size // LANES
        delta[n], new_m[n], new_v[n] = (t[row: row + k].reshape(W[n].shape) for t in (d, nm, nv))
        row += SMALL_ROWS[n]

    def pad32(a):
        return jnp.pad(a.reshape(CONV_K, D_MODEL // N_CHIPS), ((0, 1), (0, 0)))

    (d,), (nm,), (nv,) = _adamw([pad32(conv_w)], [pad32(grads["conv_w"])], [pad32(m_conv_w)], [pad32(v_conv_w)], "adamw_conv_w")
    delta["conv_w"], new_m["conv_w"], new_v["conv_w"] = (t[:CONV_K].reshape(conv_w.shape) for t in (d, nm, nv))
    grads["conv_w"] = grads["conv_w"].reshape(conv_w.shape)

    return (loss_out, grad_x, *[grads[n] for n in order], *[delta[n] for n in order],
            *[new_m[n] for n in order], *[new_v[n] for n in order])
```

```python
import functools
import math
from typing import Callable, NamedTuple

import numpy as np
import jax
import jax.numpy as jnp
from jax import lax
from jax.experimental import pallas as pl
from jax.experimental.pallas import tpu as pltpu

f32 = jnp.float32
bf16 = jnp.bfloat16
SDS = jax.ShapeDtypeStruct
MESH = pl.DeviceIdType.MESH

D_MODEL = 1024
SEQ = 2048
HEAD_DIM = 64
HEADS = 8
GROUPS = ((128, 1), (512, 4), (2048, 16))
GW = HEADS * HEAD_DIM
ATTN_W = len(GROUPS) * GW
Q_BLOCK = 128
CONV_K = 31
D_FF = 2816
IN_W = 3 * ATTN_W + 2 * D_MODEL + 2 * D_MODEL
N_CHIPS = 4
IN_S = IN_W // N_CHIPS
FF_S = D_FF // N_CHIPS
RMS_EPS = 1e-6
LN_EPS = 1e-5
LR, B1, B2, ADAM_EPS, WD, STEP = 0.001, 0.9, 0.999, 1e-08, 0.01, 10
NEG = -1e30
LANES = 128
VMEM_LIMIT = 48 * 2 ** 20
VMEM_LIMIT_BIG = 54 * 2 ** 20
CB = 512
UA_CB, UB_CB, GA_CB, GC_CB = 9, 11, 13, 15


def _alibi_slope_list(n):
    def pow2(m):
        start = 2.0 ** (-8.0 / m)
        return [start ** (i + 1) for i in range(m)]
    if math.log2(n).is_integer():
        return pow2(n)
    c = 2 ** math.floor(math.log2(n))
    return pow2(c) + _alibi_slope_list(2 * c)[0::2][: n - c]


_SLOPES = np.asarray(sorted(_alibi_slope_list(len(GROUPS) * HEADS), reverse=True), dtype=np.float32).reshape(len(GROUPS), HEADS)


def _cp(sem=None, vmem=VMEM_LIMIT):
    return pltpu.CompilerParams(dimension_semantics=sem, vmem_limit_bytes=vmem)


def _sigmoid(x):
    return 1.0 / (1.0 + jnp.exp(-x))


HBM = pl.BlockSpec(memory_space=pl.ANY)


class _Exchange(NamedTuple):
    ins: list
    out_shape: list
    scratch: list
    aliases: dict
    start: Callable
    wait: Callable


def _launch(body, *, name, grid, in_specs, out_specs, out_shape, args, scratch_shapes=(), sem=None, comm=None, vmem=VMEM_LIMIT):
    if comm is None:
        return pl.pallas_call(body, grid=grid, in_specs=in_specs, out_specs=out_specs, out_shape=out_shape,
                              scratch_shapes=list(scratch_shapes), name=name, compiler_params=_cp(sem, vmem))(*args)
    multi = isinstance(out_shape, (tuple, list))
    m_out = list(out_shape) if multi else [out_shape]
    m_ospec = list(out_specs) if multi else [out_specs]
    n_in, n_out, n_scr = len(in_specs), len(m_out), len(scratch_shapes)
    nc_in, nc_out = len(comm.ins), len(comm.out_shape)

    def hosted(*refs):
        bounds = np.cumsum([0, n_in, nc_in, n_out, nc_out, n_scr])
        mi, ci, mo, co, ms = (refs[a:b] for a, b in zip(bounds[:-1], bounds[1:]))
        cs = refs[bounds[-1]:]
        ids = [pl.program_id(a) for a in range(len(grid))]
        first = functools.reduce(jnp.logical_and, [i == 0 for i in ids])
        last = functools.reduce(jnp.logical_and, [i == g - 1 for i, g in zip(ids, grid)])

        @pl.when(first)
        def _():
            comm.start(ci, co, cs)

        body(*mi, *mo, *ms)

        @pl.when(last)
        def _():
            comm.wait(ci, co, cs)

    res = pl.pallas_call(
        hosted, grid=grid, in_specs=list(in_specs) + [HBM] * nc_in, out_specs=m_ospec + [HBM] * nc_out,
        out_shape=tuple(m_out) + tuple(comm.out_shape), scratch_shapes=list(scratch_shapes) + list(comm.scratch),
        input_output_aliases={n_in + i: n_out + o for i, o in comm.aliases.items()}, name=name + "_comm",
        compiler_params=pltpu.CompilerParams(dimension_semantics=("arbitrary",) * len(grid), vmem_limit_bytes=vmem,
                                             has_side_effects=True))(*args, *comm.ins)
    return (tuple(res[:n_out]) if multi else res[0]), tuple(res[n_out:])


def _run_exchange(ex, name):
    n_in = len(ex.ins)

    def body(*refs):
        ins, outs, sems = refs[:n_in], refs[n_in:n_in + len(ex.out_shape)], refs[n_in + len(ex.out_shape):]
        ex.start(ins, outs, sems)
        ex.wait(ins, outs, sems)

    return pl.pallas_call(body, in_specs=[HBM] * n_in, out_specs=[HBM] * len(ex.out_shape), out_shape=tuple(ex.out_shape),
                          scratch_shapes=list(ex.scratch), input_output_aliases=dict(ex.aliases), name=name,
                          compiler_params=pltpu.CompilerParams(has_side_effects=True))(*ex.ins)


def _rmsnorm(xv, gv):
    return xv * lax.rsqrt(jnp.mean(xv * xv, axis=-1, keepdims=True) + RMS_EPS) * gv


def _rms_fwd(x, g, name):
    T = x.shape[0]
    tm = 512

    def body(x_ref, g_ref, o_ref, ot_ref):
        hv = _rmsnorm(x_ref[...], g_ref[...])
        o_ref[...] = hv.astype(bf16)
        ot_ref[...] = hv.T.astype(bf16)

    row = pl.BlockSpec((tm, D_MODEL), lambda i: (i, 0))
    vec = pl.BlockSpec((1, D_MODEL), lambda i: (0, 0))
    return pl.pallas_call(body, grid=(T // tm,), in_specs=[row, vec], out_specs=[row, pl.BlockSpec((D_MODEL, tm), lambda i: (0, i))],
                          out_shape=(SDS((T, D_MODEL), bf16), SDS((D_MODEL, T), bf16)), name=name,
                          compiler_params=_cp(("parallel",)))(x, g)


def _rms_bwd_tile(dyv, xv, gv, dres):
    r = lax.rsqrt(jnp.mean(xv * xv, axis=-1, keepdims=True) + RMS_EPS)
    xh = xv * r
    dxh = dyv * gv
    dx = dres + r * (dxh - xh * jnp.mean(dxh * xh, axis=-1, keepdims=True))
    return dx, jnp.sum(dyv * xh, axis=0, keepdims=True)


def _accumulate(first, refs_parts):
    @pl.when(first)
    def _():
        for ref, part in refs_parts:
            ref[...] = part

    @pl.when(jnp.logical_not(first))
    def _():
        for ref, part in refs_parts:
            ref[...] += part


def _mm_rms_bwd(name, ops, x, g, dres, *, tm, a_spec, b_spec, dims, comm=None, vmem=VMEM_LIMIT):
    T = x.shape[0]
    n = len(ops)

    def body(*refs):
        ab_refs = refs[:2 * n]
        x_ref, g_ref, r_ref, dx_ref, dg_ref, acc = refs[2 * n:]
        m, s = pl.program_id(0), pl.program_id(1)
        p = sum(lax.dot_general(ab_refs[2 * i][...], ab_refs[2 * i + 1][...], dims, preferred_element_type=f32) for i in range(n))

        @pl.when(s == 0)
        def _():
            acc[...] = p

        @pl.when(s > 0)
        def _():
            acc[...] += p

        @pl.when(s == N_CHIPS - 1)
        def _():
            dx, part = _rms_bwd_tile(acc[...], x_ref[...], g_ref[...], r_ref[...])
            dx_ref[...] = dx
            _accumulate(m == 0, [(dg_ref, part)])

    row = pl.BlockSpec((tm, D_MODEL), lambda m, s: (m, 0))
    vec = pl.BlockSpec((1, D_MODEL), lambda m, s: (0, 0))
    return _launch(body, name=name, grid=(T // tm, N_CHIPS), in_specs=[a_spec, b_spec] * n + [row, vec, row], out_specs=[row, vec],
                   out_shape=(SDS((T, D_MODEL), f32), SDS((1, D_MODEL), f32)), args=tuple(t for ab in ops for t in ab) + (x, g, dres),
                   scratch_shapes=[pltpu.VMEM((tm, D_MODEL), f32)], sem=("arbitrary", "arbitrary"), comm=comm, vmem=vmem)


def _ffn_down_loss(ff, wd, x1, gf, target):
    T = x1.shape[0]
    tm = 1024
    nm = T // tm

    def body(f_ref, w_ref, x_ref, g_ref, t_ref, loss_ref, dx_ref, dg_ref, acc):
        m, s = pl.program_id(0), pl.program_id(1)
        p = jnp.dot(f_ref[...], w_ref[...], preferred_element_type=f32)

        @pl.when(s == 0)
        def _():
            acc[...] = p

        @pl.when(s > 0)
        def _():
            acc[...] += p

        @pl.when(s == N_CHIPS - 1)
        def _():
            xv = acc[...] + x_ref[...]
            gv = g_ref[...]
            r = lax.rsqrt(jnp.mean(xv * xv, axis=-1, keepdims=True) + RMS_EPS)
            xh = xv * r
            e = xh * gv - t_ref[...]
            part_l = jnp.broadcast_to(0.5 * jnp.sum(jnp.mean(e * e, axis=-1, keepdims=True), axis=0, keepdims=True), (1, LANES))
            dy = e * (1.0 / D_MODEL)
            dxh = dy * gv
            dx_ref[...] = r * (dxh - xh * jnp.mean(dxh * xh, axis=-1, keepdims=True))
            part_g = jnp.sum(dy * xh, axis=0, keepdims=True)

            @pl.when(m == 0)
            def _():
                loss_ref[...] = part_l
                dg_ref[...] = part_g

            @pl.when(m > 0)
            def _():
                loss_ref[...] += part_l
                dg_ref[...] += part_g

    row = pl.BlockSpec((tm, D_MODEL), lambda m, s: (m, 0))
    vec = pl.BlockSpec((1, D_MODEL), lambda m, s: (0, 0))
    return pl.pallas_call(
        body, grid=(nm, N_CHIPS),
        in_specs=[pl.BlockSpec((None, tm, FF_S), lambda m, s: (s, m, 0)), pl.BlockSpec((None, FF_S, D_MODEL), lambda m, s: (s, 0, 0)),
                  row, vec, row],
        out_specs=[pl.BlockSpec((1, LANES), lambda m, s: (0, 0)), row, vec],
        out_shape=(SDS((1, LANES), f32), SDS((T, D_MODEL), f32), SDS((1, D_MODEL), f32)),
        scratch_shapes=[pltpu.VMEM((tm, D_MODEL), f32)], name="ffn_down_loss", compiler_params=_cp(("arbitrary", "arbitrary")))(
            ff, wd, x1, gf, target)


def _merge_o_proj_rms(proj, gate_b, ya, yc, w_o, x, g2):
    T = x.shape[0]
    tm = 512

    def body(ga0, ga1, gc0, gc1, b_ref, ya_ref, yc_ref, w_ref, x_ref, g_ref, m_ref, x1_ref, h2_ref):
        bv = b_ref[...]
        ga = jnp.concatenate([ga0[...], ga1[...]], axis=-1).astype(f32)
        gc = jnp.concatenate([gc0[...], gc1[...]], axis=-1).astype(f32)
        merged = (_sigmoid(ga + bv[:, :D_MODEL]) * ya_ref[...].astype(f32)
                  + _sigmoid(gc + bv[:, D_MODEL:]) * yc_ref[...].astype(f32)).astype(bf16)
        m_ref[...] = merged
        x1 = jnp.dot(merged, w_ref[...], preferred_element_type=f32) + x_ref[...]
        x1_ref[...] = x1
        h2_ref[...] = _rmsnorm(x1, g_ref[...]).astype(bf16)

    row = pl.BlockSpec((tm, D_MODEL), lambda i: (i, 0))
    return pl.pallas_call(
        body, grid=(T // tm,),
        in_specs=[pl.BlockSpec((tm, CB), lambda i, cb=cb: (i, cb)) for cb in (GA_CB, GA_CB + 1, GC_CB, GC_CB + 1)]
        + [pl.BlockSpec((1, 2 * D_MODEL), lambda i: (0, 0)), row, row, pl.BlockSpec((D_MODEL, D_MODEL), lambda i: (0, 0)), row,
           pl.BlockSpec((1, D_MODEL), lambda i: (0, 0))],
        out_specs=[row, row, row], out_shape=(SDS((T, D_MODEL), bf16), SDS((T, D_MODEL), f32), SDS((T, D_MODEL), bf16)),
        name="merge_o_proj_rms", compiler_params=_cp(("parallel",)))(proj, proj, proj, proj, gate_b, ya, yc, w_o, x, g2)


def _conv_out_fwd(c1, g, b, w, comm=None):
    T = c1.shape[0]
    tm = 1024

    def body(c_ref, g_ref, b_ref, w_ref, c3_ref, y_ref):
        cv = c_ref[...]
        mu = jnp.mean(cv, axis=-1, keepdims=True)
        cc = cv - mu
        var = jnp.mean(cc * cc, axis=-1, keepdims=True)
        c2 = cc * lax.rsqrt(var + LN_EPS) * g_ref[...] + b_ref[...]
        c3 = (c2 * _sigmoid(c2)).astype(bf16)
        c3_ref[...] = c3
        y_ref[...] = jnp.dot(c3, w_ref[...], preferred_element_type=f32).astype(bf16)

    row = pl.BlockSpec((tm, D_MODEL), lambda i: (i, 0))
    vec = pl.BlockSpec((1, D_MODEL), lambda i: (0, 0))
    sd = SDS((T, D_MODEL), bf16)
    return _launch(body, name="conv_out_fwd", grid=(T // tm,), in_specs=[row, vec, vec, pl.BlockSpec((D_MODEL, D_MODEL), lambda i: (0, 0))],
                   out_specs=[row, row], out_shape=(sd, sd), args=(c1, g, b, w), sem=("parallel",), comm=comm)


def _conv_out_bwd(dyc, w, c1, g, b):
    T = c1.shape[0]
    tm = 1024

    def body(d_ref, w_ref, c_ref, g_ref, b_ref, dc_ref, dg_ref, db_ref):
        dc3 = lax.dot_general(d_ref[...], w_ref[...], NT, preferred_element_type=f32)
        cv = c_ref[...]
        gv = g_ref[...]
        mu = jnp.mean(cv, axis=-1, keepdims=True)
        cc = cv - mu
        var = jnp.mean(cc * cc, axis=-1, keepdims=True)
        rs = lax.rsqrt(var + LN_EPS)
        xh = cc * rs
        c2 = xh * gv + b_ref[...]
        sg = _sigmoid(c2)
        dc2 = dc3 * (sg * (1.0 + c2 * (1.0 - sg)))
        dxh = dc2 * gv
        dc_ref[...] = rs * (dxh - jnp.mean(dxh, axis=-1, keepdims=True) - xh * jnp.mean(dxh * xh, axis=-1, keepdims=True))
        _accumulate(pl.program_id(0) == 0, [(dg_ref, jnp.sum(dc2 * xh, axis=0, keepdims=True)), (db_ref, jnp.sum(dc2, axis=0, keepdims=True))])

    row = pl.BlockSpec((tm, D_MODEL), lambda i: (i, 0))
    vec = pl.BlockSpec((1, D_MODEL), lambda i: (0, 0))
    return pl.pallas_call(body, grid=(T // tm,), in_specs=[row, pl.BlockSpec((D_MODEL, D_MODEL), lambda i: (0, 0)), row, vec, vec],
                          out_specs=[row, vec, vec],
                          out_shape=(SDS((T, D_MODEL), f32), SDS((1, D_MODEL), f32), SDS((1, D_MODEL), f32)),
                          name="conv_out_bwd", compiler_params=_cp(("arbitrary",)))(dyc, w, c1, g, b)


NN = (((1,), (0,)), ((), ()))
NT = (((1,), (1,)), ((), ()))
TN = (((0,), (0,)), ((), ()))


def _mm(name, a, b, *, grid, a_spec, b_spec, o_spec, o_shape, o_dtype, dims, acc_shape=None, k_axis=None,
        res=None, res_spec=None, sem=None, comm=None, vmem=VMEM_LIMIT):
    nk = 1 if k_axis is None else grid[k_axis]

    def body(*refs):
        if res is None:
            a_ref, b_ref, o_ref = refs[:3]
            r_ref, scr = None, refs[3:]
        else:
            a_ref, b_ref, r_ref, o_ref = refs[:4]
            scr = refs[4:]
        p = lax.dot_general(a_ref[...].astype(bf16), b_ref[...].astype(bf16), dims, preferred_element_type=f32)
        if nk == 1:
            if r_ref is not None:
                p = p + r_ref[...]
            o_ref[...] = p.astype(o_dtype)
            return
        acc = scr[0]
        k = pl.program_id(k_axis)

        @pl.when(k == 0)
        def _():
            acc[...] = p

        @pl.when(k > 0)
        def _():
            acc[...] += p

        @pl.when(k == nk - 1)
        def _():
            t = acc[...]
            if r_ref is not None:
                t = t + r_ref[...]
            o_ref[...] = t.astype(o_dtype)

    ins = [a, b] + ([] if res is None else [res])
    specs = [a_spec, b_spec] + ([] if res is None else [res_spec])
    scratch = [] if nk == 1 else [pltpu.VMEM(acc_shape, f32)]
    return _launch(body, name=name, grid=grid, in_specs=specs, out_specs=o_spec, out_shape=SDS(o_shape, o_dtype),
                   args=ins, scratch_shapes=scratch, sem=sem, comm=comm, vmem=vmem)


def _mm_nn_full(name, a, b, o_dtype, res=None, tm=1024, comm=None):
    T, K = a.shape
    N = b.shape[1]
    return _mm(name, a, b, grid=(T // tm,), a_spec=pl.BlockSpec((tm, K), lambda i: (i, 0)),
               b_spec=pl.BlockSpec((K, N), lambda i: (0, 0)), o_spec=pl.BlockSpec((tm, N), lambda i: (i, 0)),
               o_shape=(T, N), o_dtype=o_dtype, dims=NN, res=res,
               res_spec=None if res is None else pl.BlockSpec((tm, N), lambda i: (i, 0)), sem=("parallel",), comm=comm)


def _mm_nt_full(name, a, b, o_dtype, tm=1024):
    T, N = a.shape
    K = b.shape[0]
    return _mm(name, a, b, grid=(T // tm,), a_spec=pl.BlockSpec((tm, N), lambda i: (i, 0)),
               b_spec=pl.BlockSpec((K, N), lambda i: (0, 0)), o_spec=pl.BlockSpec((tm, K), lambda i: (i, 0)),
               o_shape=(T, K), o_dtype=o_dtype, dims=NT, sem=("parallel",))


def _mm_tn_tokens(name, a, b, o_dtype, tk=1024):
    T, K = a.shape
    N = b.shape[1]
    return _mm(name, a, b, grid=(T // tk,), a_spec=pl.BlockSpec((tk, K), lambda k: (k, 0)),
               b_spec=pl.BlockSpec((tk, N), lambda k: (k, 0)), o_spec=pl.BlockSpec((K, N), lambda k: (0, 0)),
               o_shape=(K, N), o_dtype=o_dtype, dims=TN, acc_shape=(K, N), k_axis=0, sem=("arbitrary",))


AO_S = D_MODEL // N_CHIPS


def _attn_out_cat(w_ref):
    return jnp.concatenate([w_ref[s] for s in range(N_CHIPS)], axis=-1)


def _attn_out_fwd(mix, w4):
    T = mix.shape[0]
    tm = 1024

    def body(a_ref, w_ref, o_ref):
        o_ref[...] = jnp.dot(a_ref[...], _attn_out_cat(w_ref), preferred_element_type=f32).astype(bf16)

    return pl.pallas_call(body, grid=(T // tm,),
                          in_specs=[pl.BlockSpec((tm, GW), lambda i: (i, 0)), pl.BlockSpec((N_CHIPS, GW, AO_S), lambda i: (0, 0, 0))],
                          out_specs=pl.BlockSpec((tm, D_MODEL), lambda i: (i, 0)), out_shape=SDS((T, D_MODEL), bf16), name="attn_out",
                          compiler_params=_cp(("parallel",)))(mix, w4)


def _attn_out_bwd_x(dya, w4, comm=None):
    T = dya.shape[0]
    tm = 1024

    def body(d_ref, w_ref, o_ref):
        o_ref[...] = lax.dot_general(d_ref[...], _attn_out_cat(w_ref), NT, preferred_element_type=f32).astype(bf16)

    return _launch(body, name="d_mix", grid=(T // tm,),
                   in_specs=[pl.BlockSpec((tm, D_MODEL), lambda i: (i, 0)), pl.BlockSpec((N_CHIPS, GW, AO_S), lambda i: (0, 0, 0))],
                   out_specs=pl.BlockSpec((tm, GW), lambda i: (i, 0)), out_shape=SDS((T, GW), bf16), args=(dya, w4),
                   sem=("parallel",), comm=comm)


def _attn_out_bwd_w(mix, dya):
    T = mix.shape[0]
    tk = 1024
    nk = T // tk

    def body(a_ref, d_ref, o_ref, acc):
        k = pl.program_id(0)
        p = lax.dot_general(a_ref[...], d_ref[...], TN, preferred_element_type=f32)

        @pl.when(k == 0)
        def _():
            acc[...] = p

        @pl.when(k > 0)
        def _():
            acc[...] += p

        @pl.when(k == nk - 1)
        def _():
            for s in range(N_CHIPS):
                o_ref[s] = acc[:, s * AO_S:(s + 1) * AO_S].astype(bf16)

    return pl.pallas_call(body, grid=(nk,),
                          in_specs=[pl.BlockSpec((tk, GW), lambda k: (k, 0)), pl.BlockSpec((tk, D_MODEL), lambda k: (k, 0))],
                          out_specs=pl.BlockSpec((N_CHIPS, GW, AO_S), lambda k: (0, 0, 0)), out_shape=SDS((N_CHIPS, GW, AO_S), bf16),
                          scratch_shapes=[pltpu.VMEM((GW, D_MODEL), f32)], name="d_w_attn_out",
                          compiler_params=_cp(("arbitrary",)))(mix, dya)


def _ffn_up(h2, wg, wu):
    T = h2.shape[0]
    tm = 1024

    def body(h_ref, wg_ref, wu_ref, a_ref, b_ref, f_ref):
        hv = h_ref[...]
        av = lax.dot_general(hv, wg_ref[...], NT, preferred_element_type=f32)
        bv = lax.dot_general(hv, wu_ref[...], NT, preferred_element_type=f32)
        a_ref[...] = av.astype(bf16)
        b_ref[...] = bv.astype(bf16)
        f_ref[...] = (av * _sigmoid(av) * bv).astype(bf16)

    wspec = pl.BlockSpec((None, FF_S, D_MODEL), lambda s, m: (s, 0, 0))
    ospec = pl.BlockSpec((None, tm, FF_S), lambda s, m: (s, m, 0))
    osd = SDS((N_CHIPS, T, FF_S), bf16)
    return pl.pallas_call(body, grid=(N_CHIPS, T // tm),
                          in_specs=[pl.BlockSpec((tm, D_MODEL), lambda s, m: (m, 0)), wspec, wspec],
                          out_specs=[ospec, ospec, ospec], out_shape=(osd, osd, osd), name="ffn_up",
                          compiler_params=_cp(("parallel", "parallel")))(h2, wg, wu)


def _ffn_down_bwd(dx2, wd, a, b):
    T = dx2.shape[0]
    tm = 1024

    def body(d_ref, w_ref, a_ref, b_ref, da_ref, db_ref):
        df = lax.dot_general(d_ref[...].astype(bf16), w_ref[...], NT, preferred_element_type=f32)
        av = a_ref[...].astype(f32)
        sg = _sigmoid(av)
        da_ref[...] = (df * b_ref[...].astype(f32) * (sg * (1.0 + av * (1.0 - sg)))).astype(bf16)
        db_ref[...] = (df * av * sg).astype(bf16)

    aspec = pl.BlockSpec((None, tm, FF_S), lambda m, s: (s, m, 0))
    osd = SDS((N_CHIPS, T, FF_S), bf16)
    return pl.pallas_call(body, grid=(T // tm, N_CHIPS),
                          in_specs=[pl.BlockSpec((tm, D_MODEL), lambda m, s: (m, 0)),
                                    pl.BlockSpec((None, FF_S, D_MODEL), lambda m, s: (s, 0, 0)), aspec, aspec],
                          out_specs=[aspec, aspec], out_shape=(osd, osd), name="ffn_down_bwd",
                          compiler_params=_cp(("parallel", "parallel")))(dx2, wd, a, b)


def _merge_bwd(dm, proj, gate_b, ya, yc):
    T = proj.shape[0]
    tm = 512

    def body(dm_ref, ga_ref, gc_ref, ba_ref, bc_ref, ya_ref, yc_ref, dya_ref, dyc_ref, dga_ref, dgc_ref, dba_ref, dbc_ref):
        dmv = dm_ref[...].astype(f32)
        sa = _sigmoid(ga_ref[...].astype(f32) + ba_ref[...])
        sc = _sigmoid(gc_ref[...].astype(f32) + bc_ref[...])
        dya_ref[...] = (dmv * sa).astype(bf16)
        dyc_ref[...] = (dmv * sc).astype(bf16)
        dga = dmv * ya_ref[...].astype(f32) * sa * (1.0 - sa)
        dgc = dmv * yc_ref[...].astype(f32) * sc * (1.0 - sc)
        dga_ref[...] = dga.astype(bf16)
        dgc_ref[...] = dgc.astype(bf16)
        pa = jnp.sum(dga, axis=0, keepdims=True)
        pc = jnp.sum(dgc, axis=0, keepdims=True)

        @pl.when(pl.program_id(1) == 0)
        def _():
            dba_ref[...] = pa
            dbc_ref[...] = pc

        @pl.when(pl.program_id(1) > 0)
        def _():
            dba_ref[...] += pa
            dbc_ref[...] += pc

    blk = pl.BlockSpec((tm, CB), lambda j, i: (i, j))
    vec = pl.BlockSpec((1, CB), lambda j, i: (0, j))
    big = SDS((T, D_MODEL), bf16)
    small = SDS((1, D_MODEL), f32)
    return pl.pallas_call(
        body, grid=(2, T // tm),
        in_specs=[blk, pl.BlockSpec((tm, CB), lambda j, i: (i, GA_CB + j)), pl.BlockSpec((tm, CB), lambda j, i: (i, GC_CB + j)),
                  vec, pl.BlockSpec((1, CB), lambda j, i: (0, 2 + j)), blk, blk],
        out_specs=[blk, blk, blk, blk, vec, vec], out_shape=(big, big, big, big, small, small), name="merge_bwd",
        compiler_params=_cp(("parallel", "arbitrary")))(dm, proj, proj, gate_b, gate_b, ya, yc)


CONV_TS = 512
CONV_HALO = 32
CONV_RC = 64
CONV_WIN = CONV_TS + CONV_HALO
SUBLANES = 8


def _fill_shifted(win, sh):
    for b in range(1, SUBLANES):
        sh[b - 1] = win[pl.ds(b, CONV_WIN - SUBLANES), :]


def _rows_at(win, sh, row):
    a, b = divmod(row, SUBLANES)
    if b == 0:
        return win[pl.ds(row, CONV_RC), :]
    return sh[b - 1, pl.ds(a * SUBLANES, CONV_RC), :]


def _glu_conv_fwd(proj3, w, bias, comm=None):
    B = proj3.shape[0]
    nt = SEQ // CONV_TS
    hb = CONV_TS // CONV_HALO

    def body(ua_ref, ub_ref, ha_ref, hb_ref, w_ref, b_ref, o_ref, win, sh):
        i = pl.program_id(2)
        c0 = ua_ref[...].astype(f32) * _sigmoid(ub_ref[...].astype(f32))
        halo = ha_ref[...].astype(f32) * _sigmoid(hb_ref[...].astype(f32))
        win[0:CONV_HALO, :] = jnp.where(i > 0, halo, 0.0)
        win[CONV_HALO:, :] = c0
        _fill_shifted(win, sh)
        for r0 in range(0, CONV_TS, CONV_RC):
            acc = jnp.zeros((CONV_RC, CB), f32) + b_ref[...]
            for k in range(CONV_K):
                acc = acc + _rows_at(win, sh, r0 + CONV_HALO - (CONV_K - 1) + k) * w_ref[k:k + 1, :]
            o_ref[r0:r0 + CONV_RC, :] = acc

    def cur(cb):
        return pl.BlockSpec((None, CONV_TS, CB), lambda b, j, i: (b, i, cb + j))

    def prev(cb):
        return pl.BlockSpec((None, CONV_HALO, CB), lambda b, j, i: (b, jnp.maximum(i * hb - 1, 0), cb + j))

    return _launch(
        body, name="glu_conv_fwd", grid=(B, 2, nt),
        in_specs=[cur(UA_CB), cur(UB_CB), prev(UA_CB), prev(UB_CB),
                  pl.BlockSpec((CONV_K, CB), lambda b, j, i: (0, j)), pl.BlockSpec((1, CB), lambda b, j, i: (0, j))],
        out_specs=pl.BlockSpec((None, CONV_TS, CB), lambda b, j, i: (b, i, j)),
        out_shape=SDS((B, SEQ, D_MODEL), f32), args=(proj3, proj3, proj3, proj3, w, bias),
        scratch_shapes=[pltpu.VMEM((CONV_WIN, CB), f32), pltpu.VMEM((SUBLANES - 1, CONV_WIN - SUBLANES, CB), f32)],
        sem=("parallel", "parallel", "parallel"), comm=comm)


def _glu_conv_bwd(dc1, proj3, w, comm=None):
    B = proj3.shape[0]
    nt = SEQ // CONV_TS
    hb = CONV_TS // CONV_HALO

    def body(d_ref, dn_ref, ua_ref, ub_ref, ha_ref, hb_ref, w_ref, dua_ref, dub_ref, dw_ref, db_ref, winc, wind, accw, shc, shd):
        b = pl.program_id(1)
        i = pl.program_id(2)
        first = jnp.logical_and(b == 0, i == 0)
        last = jnp.logical_and(b == B - 1, i == nt - 1)

        @pl.when(first)
        def _():
            accw[...] = jnp.zeros_like(accw)
            db_ref[...] = jnp.zeros_like(db_ref)

        halo = ha_ref[...].astype(f32) * _sigmoid(hb_ref[...].astype(f32))
        winc[0:CONV_HALO, :] = jnp.where(i > 0, halo, 0.0)
        winc[CONV_HALO:, :] = ua_ref[...].astype(f32) * _sigmoid(ub_ref[...].astype(f32))
        wind[0:CONV_TS, :] = d_ref[...]
        wind[CONV_TS:, :] = jnp.where(i < nt - 1, dn_ref[...], 0.0)
        db_ref[...] += jnp.sum(d_ref[...], axis=0, keepdims=True)
        _fill_shifted(winc, shc)
        _fill_shifted(wind, shd)
        for r0 in range(0, CONV_TS, CONV_RC):
            dc0 = jnp.zeros((CONV_RC, CB), f32)
            for k in range(CONV_K):
                dc0 = dc0 + _rows_at(wind, shd, r0 + (CONV_K - 1) - k) * w_ref[k:k + 1, :]
            uav = ua_ref[r0:r0 + CONV_RC, :].astype(f32)
            sg = _sigmoid(ub_ref[r0:r0 + CONV_RC, :].astype(f32))
            dua_ref[r0:r0 + CONV_RC, :] = (dc0 * sg).astype(bf16)
            dub_ref[r0:r0 + CONV_RC, :] = (dc0 * uav * sg * (1.0 - sg)).astype(bf16)
            dv = wind[r0:r0 + CONV_RC, :]
            for k in range(CONV_K):
                prod = dv * _rows_at(winc, shc, r0 + CONV_HALO - (CONV_K - 1) + k)
                accw[k] += jnp.sum(prod.reshape(CONV_RC // 8, 8, CB), axis=0)

        @pl.when(last)
        def _():
            for k in range(CONV_K):
                dw_ref[k:k + 1, :] = jnp.sum(accw[k], axis=0, keepdims=True)
            dw_ref[CONV_K:, :] = jnp.zeros((CONV_HALO - CONV_K, CB), f32)

    def cur(cb):
        return pl.BlockSpec((None, CONV_TS, CB), lambda j, b, i: (b, i, cb + j))

    def prev(cb):
        return pl.BlockSpec((None, CONV_HALO, CB), lambda j, b, i: (b, jnp.maximum(i * hb - 1, 0), cb + j))

    nxt = pl.BlockSpec((None, CONV_HALO, CB), lambda j, b, i: (b, jnp.minimum((i + 1) * hb, SEQ // CONV_HALO - 1), j))
    big = SDS((B, SEQ, D_MODEL), bf16)
    return _launch(
        body, name="glu_conv_bwd", grid=(2, B, nt),
        in_specs=[cur(0), nxt, cur(UA_CB), cur(UB_CB), prev(UA_CB), prev(UB_CB), pl.BlockSpec((CONV_K, CB), lambda j, b, i: (0, j))],
        out_specs=[cur(0), cur(0), pl.BlockSpec((CONV_HALO, CB), lambda j, b, i: (0, j)), pl.BlockSpec((1, CB), lambda j, b, i: (0, j))],
        out_shape=(big, big, SDS((CONV_HALO, D_MODEL), f32), SDS((1, D_MODEL), f32)),
        args=(dc1, dc1, proj3, proj3, proj3, proj3, w),
        scratch_shapes=[pltpu.VMEM((CONV_WIN, CB), f32), pltpu.VMEM((CONV_WIN, CB), f32), pltpu.VMEM((CONV_K, SUBLANES, CB), f32),
                        pltpu.VMEM((SUBLANES - 1, CONV_WIN - SUBLANES, CB), f32),
                        pltpu.VMEM((SUBLANES - 1, CONV_WIN - SUBLANES, CB), f32)],
        sem=("parallel", "arbitrary", "arbitrary"), comm=comm)


def _band(first, dil):
    kw = Q_BLOCK if first else 2 * Q_BLOCK
    qi = lax.broadcasted_iota(jnp.int32, (Q_BLOCK, kw), 0)
    kj = lax.broadcasted_iota(jnp.int32, (Q_BLOCK, kw), 1)
    rel = qi - kj + (0 if first else Q_BLOCK)
    valid = jnp.logical_and(rel >= 0, rel <= Q_BLOCK)
    return valid, rel.astype(f32) * float(dil)


def _bias(first, dil, slope):
    valid, dist = _band(first, dil)
    return jnp.where(valid, -slope * dist, NEG)


def _scores(q, k, bias):
    return lax.dot_general(q, k, NT, preferred_element_type=f32) * (HEAD_DIM ** -0.5) + bias


def _pair_cols(hp):
    return slice(hp * LANES, (hp + 1) * LANES)


def _half(x2, e):
    lane = lax.broadcasted_iota(jnp.int32, (1, LANES), 1)
    keep = (lane < HEAD_DIM) if e == 0 else (lane >= HEAD_DIM)
    return jnp.where(keep, x2, jnp.zeros_like(x2))


def _attn_fwd(g, q4, k4, v4, qcb, kcb, vcb, comm=None):
    _, dil = GROUPS[g]
    B, r, L, _ = q4.shape
    nb = L // Q_BLOCK
    slopes = [float(s) for s in _SLOPES[g]]

    def body(q_ref, k_ref, v_ref, o_ref, lse_ref, bias, s_scr, p_scr):
        lane = lax.broadcasted_iota(jnp.int32, (Q_BLOCK, LANES), 1)
        if nb > 1:
            for h in range(HEADS):
                bias[h] = _bias(False, dil, slopes[h])

        def block(n, first):
            q0 = 0 if first else pl.multiple_of(n * Q_BLOCK, Q_BLOCK)
            k0 = 0 if first else pl.multiple_of((n - 1) * Q_BLOCK, Q_BLOCK)
            kw = Q_BLOCK if first else 2 * Q_BLOCK
            for hp in range(HEADS // 2):
                q2 = q_ref[pl.ds(q0, Q_BLOCK), _pair_cols(hp)]
                k2 = k_ref[pl.ds(k0, kw), _pair_cols(hp)]
                for e in range(2):
                    h = 2 * hp + e
                    b_h = _bias(True, dil, slopes[h]) if first else bias[h]
                    s_scr[h, :, :kw] = _scores(_half(q2, e), k2, b_h)
            st = jnp.zeros((Q_BLOCK, LANES), f32)
            dens = jnp.ones((Q_BLOCK, LANES), f32)
            for h in range(HEADS):
                s = s_scr[h, :, :kw]
                m = jnp.max(s, axis=-1, keepdims=True)
                p = jnp.exp(s - m)
                den = jnp.sum(p, axis=-1, keepdims=True)
                p_scr[h, :, :kw] = p.astype(bf16)
                st = jnp.where(lane == h, m + jnp.log(den), st)
                dens = jnp.where(lane == h, den, dens)
            lse_ref[pl.ds(q0, Q_BLOCK), :] = st
            inv = 1.0 / dens
            for hp in range(HEADS // 2):
                v2 = v_ref[pl.ds(k0, kw), _pair_cols(hp)]
                o2 = sum(jnp.dot(p_scr[2 * hp + e, :, :kw], _half(v2, e), preferred_element_type=f32) * inv[:, 2 * hp + e:2 * hp + e + 1]
                         for e in range(2))
                o_ref[pl.ds(q0, Q_BLOCK), _pair_cols(hp)] = o2.astype(bf16)

        block(0, True)
        if nb > 1:
            def step(n, carry):
                block(n, False)
                return carry
            lax.fori_loop(1, nb, step, 0)

    def spec(cb):
        return pl.BlockSpec((None, None, L, GW), lambda b, c: (b, c, 0, cb))

    return _launch(
        body, name=f"attn_fwd_g{g}", grid=(B, r), in_specs=[spec(qcb), spec(kcb), spec(vcb)],
        out_specs=[spec(0), pl.BlockSpec((None, None, L, LANES), lambda b, c: (b, c, 0, 0))],
        out_shape=(SDS((B, r, L, GW), bf16), SDS((B, r, L, LANES), f32)), args=(q4, k4, v4),
        scratch_shapes=[pltpu.VMEM((HEADS, Q_BLOCK, 2 * Q_BLOCK), f32), pltpu.VMEM((HEADS, Q_BLOCK, 2 * Q_BLOCK), f32),
                        pltpu.VMEM((HEADS, Q_BLOCK, 2 * Q_BLOCK), bf16)],
        sem=("parallel", "parallel"), comm=comm)


def _attn_bwd(g, q4, k4, v4, qcb, kcb, vcb, do4, lse4, dl4, comm=None):
    _, dil = GROUPS[g]
    B, r, L, _ = q4.shape
    nb = L // Q_BLOCK
    slopes = [float(s) for s in _SLOPES[g]]
    scale = HEAD_DIM ** -0.5

    def body(q_ref, k_ref, v_ref, do_ref, lse_ref, dl_ref, dq_ref, dk_ref, dv_ref, dk_acc, dv_acc, bias, s_scr, dp_scr, p_scr, ds_scr):
        dk_acc[...] = jnp.zeros_like(dk_acc)
        dv_acc[...] = jnp.zeros_like(dv_acc)
        if nb > 1:
            for h in range(HEADS):
                bias[h] = _bias(False, dil, slopes[h])

        def block(n, first):
            q0 = 0 if first else pl.multiple_of(n * Q_BLOCK, Q_BLOCK)
            k0 = 0 if first else pl.multiple_of((n - 1) * Q_BLOCK, Q_BLOCK)
            kw = Q_BLOCK if first else 2 * Q_BLOCK
            for hp in range(HEADS // 2):
                q2 = q_ref[pl.ds(q0, Q_BLOCK), _pair_cols(hp)]
                k2 = k_ref[pl.ds(k0, kw), _pair_cols(hp)]
                v2 = v_ref[pl.ds(k0, kw), _pair_cols(hp)]
                do2 = do_ref[pl.ds(q0, Q_BLOCK), _pair_cols(hp)]
                for e in range(2):
                    h = 2 * hp + e
                    b_h = _bias(True, dil, slopes[h]) if first else bias[h]
                    s_scr[h, :, :kw] = _scores(_half(q2, e), k2, b_h)
                    dp_scr[h, :, :kw] = lax.dot_general(_half(do2, e), v2, NT, preferred_element_type=f32)
            for h in range(HEADS):
                p = jnp.exp(s_scr[h, :, :kw] - lse_ref[pl.ds(q0, Q_BLOCK), h:h + 1])
                p_scr[h, :, :kw] = p.astype(bf16)
                ds_scr[h, :, :kw] = (p * (dp_scr[h, :, :kw] - dl_ref[pl.ds(q0, Q_BLOCK), h:h + 1])).astype(bf16)
            for hp in range(HEADS // 2):
                cols = _pair_cols(hp)
                q2 = q_ref[pl.ds(q0, Q_BLOCK), cols]
                k2 = k_ref[pl.ds(k0, kw), cols]
                do2 = do_ref[pl.ds(q0, Q_BLOCK), cols]
                ds = [ds_scr[2 * hp + e, :, :kw] for e in range(2)]
                dq2 = sum(jnp.dot(ds[e], _half(k2, e), preferred_element_type=f32) for e in range(2))
                dq_ref[pl.ds(q0, Q_BLOCK), cols] = (dq2 * scale).astype(bf16)
                dk2 = sum(lax.dot_general(ds[e], _half(q2, e), TN, preferred_element_type=f32) for e in range(2))
                dk_acc[pl.ds(k0, kw), cols] += dk2 * scale
                dv2 = sum(lax.dot_general(p_scr[2 * hp + e, :, :kw], _half(do2, e), TN, preferred_element_type=f32) for e in range(2))
                dv_acc[pl.ds(k0, kw), cols] += dv2

        block(0, True)
        if nb > 1:
            def step(n, carry):
                block(n, False)
                return carry
            lax.fori_loop(1, nb, step, 0)
        dk_ref[...] = dk_acc[...].astype(bf16)
        dv_ref[...] = dv_acc[...].astype(bf16)

    def spec(cb):
        return pl.BlockSpec((None, None, L, GW), lambda b, c: (b, c, 0, cb))

    st = pl.BlockSpec((None, None, L, LANES), lambda b, c: (b, c, 0, 0))
    osd = SDS((B, r, L, GW), bf16)
    return _launch(
        body, name=f"attn_bwd_g{g}", grid=(B, r), in_specs=[spec(qcb), spec(kcb), spec(vcb), spec(0), st, st],
        out_specs=[spec(0), spec(0), spec(0)], out_shape=(osd, osd, osd), args=(q4, k4, v4, do4, lse4, dl4),
        scratch_shapes=[pltpu.VMEM((L, GW), f32), pltpu.VMEM((L, GW), f32)]
        + [pltpu.VMEM((HEADS, Q_BLOCK, 2 * Q_BLOCK), f32)] * 3 + [pltpu.VMEM((HEADS, Q_BLOCK, 2 * Q_BLOCK), bf16)] * 2,
        sem=("parallel", "parallel"), comm=comm)


RT = 512
RCH = GW // LANES
DILS = tuple(d for _, d in GROUPS[1:])


def _res_spec(r, width):
    return pl.BlockSpec((None, r, RT // r, width), lambda b, i, *_: (b, 0, i, 0))


def _tok_spec(width, cb=0):
    return pl.BlockSpec((None, RT, width), lambda b, i, *_: (b, i, cb))


def _to_residues(res_ref, scr, r, width):
    for c in range(r):
        for jj in range(width // LANES):
            res_ref[c, :, jj * LANES:(jj + 1) * LANES] = scr[jj, pl.ds(c, RT // r, stride=r), :].astype(res_ref.dtype)


def _from_residues(scr, res_ref, r, width):
    for c in range(r):
        for jj in range(width // LANES):
            scr[jj, pl.ds(c, RT // r, stride=r), :] = res_ref[c, :, jj * LANES:(jj + 1) * LANES].astype(f32)


def _qkv_to_residues(proj3, g):
    r = GROUPS[g][1]
    B = proj3.shape[0]

    def body(q_ref, k_ref, v_ref, o_ref, scr):
        for p, x_ref in enumerate((q_ref, k_ref, v_ref)):
            x = x_ref[...].astype(f32)
            for jj in range(RCH):
                scr[p * RCH + jj] = x[:, jj * LANES:(jj + 1) * LANES]
        _to_residues(o_ref, scr, r, ATTN_W)

    return pl.pallas_call(
        body, grid=(B, SEQ // RT), in_specs=[_tok_spec(GW, 3 * p + g) for p in range(3)], out_specs=_res_spec(r, ATTN_W),
        out_shape=SDS((B, r, SEQ // r, ATTN_W), bf16), scratch_shapes=[pltpu.VMEM((3 * RCH, RT, LANES), f32)],
        name=f"qkv_to_residues_g{g}", compiler_params=_cp(("parallel", "parallel")))(proj3, proj3, proj3)


def _attn_mix(o0, l0, o_res, l_res):
    B = o0.shape[0]

    def body(o0_ref, l0_ref, o1_ref, o2_ref, l1_ref, l2_ref, y_ref, lt_ref, lt1_ref, lt2_ref, so, sl):
        for gi, (o_ref, l_ref, r) in enumerate(((o1_ref, l1_ref, DILS[0]), (o2_ref, l2_ref, DILS[1]))):
            _from_residues(so.at[gi], o_ref, r, GW)
            _from_residues(sl.at[gi:gi + 1], l_ref, r, LANES)
        ls = [l0_ref[...], sl[0], sl[1]]
        m = functools.reduce(jnp.maximum, ls)
        ws = [jnp.exp(l - m) for l in ls]
        den = ws[0] + ws[1] + ws[2]
        alphas = [w / den for w in ws]
        lt = m + jnp.log(den)
        lt_ref[...] = lt
        sl[2] = lt
        _to_residues(lt1_ref, sl.at[2:3], DILS[0], LANES)
        _to_residues(lt2_ref, sl.at[2:3], DILS[1], LANES)
        for h in range(HEADS):
            cols = slice(h * HEAD_DIM, (h + 1) * HEAD_DIM)
            jj, lo = divmod(h * HEAD_DIM, LANES)
            acc = alphas[0][:, h:h + 1] * o0_ref[:, cols].astype(f32)
            for gi in range(2):
                acc = acc + alphas[gi + 1][:, h:h + 1] * so[gi, jj, :, lo:lo + HEAD_DIM]
            y_ref[:, cols] = acc.astype(bf16)

    in_specs = [_tok_spec(GW), _tok_spec(LANES), _res_spec(DILS[0], GW), _res_spec(DILS[1], GW), _res_spec(DILS[0], LANES), _res_spec(DILS[1], LANES)]
    out_specs = [_tok_spec(GW), _tok_spec(LANES), _res_spec(DILS[0], LANES), _res_spec(DILS[1], LANES)]
    return pl.pallas_call(
        body, grid=(B, SEQ // RT), in_specs=in_specs, out_specs=out_specs,
        out_shape=(SDS((B, SEQ, GW), bf16), SDS((B, SEQ, LANES), f32)) + tuple(SDS((B, r, SEQ // r, LANES), f32) for r in DILS),
        scratch_shapes=[pltpu.VMEM((2, RCH, RT, LANES), f32), pltpu.VMEM((3, RT, LANES), f32)],
        name="attn_mix", compiler_params=_cp(("parallel", "parallel")))(o0, l0, *o_res, *l_res)


def _attn_delta(dmix, mix):
    B = dmix.shape[0]

    def body(d_ref, y_ref, dl_ref, dl1_ref, dl2_ref, dm1_ref, dm2_ref, sx, sd):
        lane = lax.broadcasted_iota(jnp.int32, (RT, LANES), 1)
        acc = jnp.zeros((RT, LANES), f32)
        dv = d_ref[...].astype(f32)
        for jj in range(RCH):
            sx[jj] = dv[:, jj * LANES:(jj + 1) * LANES]
        for h in range(HEADS):
            cols = slice(h * HEAD_DIM, (h + 1) * HEAD_DIM)
            dl = jnp.sum(dv[:, cols] * y_ref[:, cols].astype(f32), axis=-1, keepdims=True)
            acc = jnp.where(lane == h, dl, acc)
        dl_ref[...] = acc
        sd[0] = acc
        _to_residues(dl1_ref, sd, DILS[0], LANES)
        _to_residues(dl2_ref, sd, DILS[1], LANES)
        _to_residues(dm1_ref, sx, DILS[0], GW)
        _to_residues(dm2_ref, sx, DILS[1], GW)

    return pl.pallas_call(
        body, grid=(B, SEQ // RT), in_specs=[_tok_spec(GW), _tok_spec(GW)],
        out_specs=[_tok_spec(LANES), _res_spec(DILS[0], LANES), _res_spec(DILS[1], LANES), _res_spec(DILS[0], GW), _res_spec(DILS[1], GW)],
        out_shape=(SDS((B, SEQ, LANES), f32),) + tuple(SDS((B, r, SEQ // r, LANES), f32) for r in DILS)
        + tuple(SDS((B, r, SEQ // r, GW), bf16) for r in DILS),
        scratch_shapes=[pltpu.VMEM((RCH, RT, LANES), f32), pltpu.VMEM((1, RT, LANES), f32)],
        name="attn_delta", compiler_params=_cp(("parallel", "parallel")))(dmix, mix)


N_CB = IN_W // CB


def _assemble_dproj(dqkv, dua, dub, dga, dgc):
    B = dua.shape[0]
    ng = len(GROUPS)
    flat = [dqkv[g][p] for p in range(3) for g in range(ng)]
    wide = [dua, dub, dga, dgc]

    def body(*refs):
        srcs, wides, o_ref, scr = refs[:3 * ng], refs[3 * ng:3 * ng + 4], refs[3 * ng + 4], refs[3 * ng + 5]
        for jv in range(3 * ng):
            g = jv % ng
            if g == 0:
                o_ref[:, jv * GW:(jv + 1) * GW] = srcs[jv][...]
            else:
                _from_residues(scr, srcs[jv], GROUPS[g][1], GW)
                for jj in range(RCH):
                    o_ref[:, jv * GW + jj * LANES:jv * GW + (jj + 1) * LANES] = scr[jj].astype(bf16)
        for wv in range(4):
            lo = 3 * ATTN_W + wv * D_MODEL
            o_ref[:, lo:lo + D_MODEL] = wides[wv][...]

    in_specs = [_tok_spec(GW) if (jv % ng) == 0 else _res_spec(GROUPS[jv % ng][1], GW) for jv in range(3 * ng)]
    in_specs += [_tok_spec(D_MODEL)] * 4
    return pl.pallas_call(
        body, grid=(B, SEQ // RT), in_specs=in_specs, out_specs=_tok_spec(IN_W),
        out_shape=SDS((B, SEQ, IN_W), bf16), scratch_shapes=[pltpu.VMEM((RCH, RT, LANES), f32)],
        name="assemble_dproj", compiler_params=_cp(("parallel", "parallel")))(*flat, *wide)


def _same_shape_groups(names, arrays):
    groups = {}
    for n, a in zip(names, arrays):
        groups.setdefault(a.shape, ([], []))
        groups[a.shape][0].append(n)
        groups[a.shape][1].append(a)
    return list(groups.values())


def _add_pairs(parts, gots, core, name):
    k = len(parts)
    n, h, C = gots[0].shape

    def body(c_ref, *refs):
        for i in range(k):
            refs[2 * k + i][...] = (refs[i][...].astype(f32) + refs[k + i][...].astype(f32)).astype(bf16)

    blk = pl.BlockSpec((None, h, C), lambda s, c_ref: (s, 0, 0))
    own = pl.BlockSpec((None, h, C), lambda s, c_ref: (s, c_ref[0], 0))
    spec = pltpu.PrefetchScalarGridSpec(num_scalar_prefetch=1, grid=(n,), in_specs=[own] * k + [blk] * k, out_specs=[blk] * k)
    return pl.pallas_call(body, grid_spec=spec, out_shape=tuple(SDS((n, h, C), bf16) for _ in range(k)), name=name,
                          compiler_params=_cp(("parallel",)))(core, *parts, *gots)


def _sum4(sums, gots, chip, core, name):
    k = len(sums)
    _, h, C = sums[0].shape

    def body(s_ref, c_ref, *refs):
        for i in range(k):
            q_ref = refs[k + i]
            t = refs[i][...].astype(f32) + q_ref[0].astype(f32)
            t = t + q_ref[1].astype(f32)
            refs[2 * k + i][...] = t + q_ref[2].astype(f32)

    spec = pltpu.PrefetchScalarGridSpec(
        num_scalar_prefetch=2, grid=(1,),
        in_specs=[pl.BlockSpec((None, h, C), lambda i, s_ref, c_ref: (s_ref[0], 0, 0))] * k
        + [pl.BlockSpec((N_CHIPS - 1, h, C), lambda i, s_ref, c_ref: (0, 0, 0))] * k,
        out_specs=[pl.BlockSpec((h, C), lambda i, s_ref, c_ref: (c_ref[0], 0))] * k)
    return pl.pallas_call(body, grid_spec=spec, out_shape=tuple(SDS((2 * h, C), f32) for _ in range(k)), name=name,
                          compiler_params=_cp(("arbitrary",)))(chip, core, *sums, *gots)


def _adamw(ws, gs, ms, vs, name):
    k = len(ws)
    R, C = ws[0].shape
    rt = R
    for cand in (512, 256, 128, 64, 32, 16, 8):
        if R % cand == 0 and cand * C * 4 <= 2 ** 21:
            rt = cand
            break
    c1 = 1.0 / (1.0 - B1 ** STEP)
    c2 = 1.0 / (1.0 - B2 ** STEP)

    def body(*refs):
        for i in range(k):
            w_ref, g_ref, m_ref, v_ref = (refs[j * k + i] for j in range(4))
            d_ref, nm_ref, nv_ref = (refs[(4 + j) * k + i] for j in range(3))
            gv = g_ref[...]
            nm = B1 * m_ref[...] + (1.0 - B1) * gv
            nv = B2 * v_ref[...] + (1.0 - B2) * (gv * gv)
            nm_ref[...] = nm
            nv_ref[...] = nv
            d_ref[...] = -LR * ((nm * c1) / (jnp.sqrt(nv * c2) + ADAM_EPS) + WD * w_ref[...])

    blk = pl.BlockSpec((rt, C), lambda i: (i, 0))
    sd = SDS((R, C), f32)
    res = pl.pallas_call(body, grid=(R // rt,), in_specs=[blk] * (4 * k), out_specs=[blk] * (3 * k), out_shape=(sd,) * (3 * k),
                         name=name, compiler_params=_cp(("parallel",)))(*ws, *gs, *ms, *vs)
    return res[:k], res[k:2 * k], res[2 * k:]


def _coords():
    return lax.axis_index("x"), lax.axis_index("y"), lax.axis_index("c")


def _other_chips(x, y):
    return [(1 - x, y), (x, 1 - y), (1 - x, 1 - y)]


def _allgather_weights(shards, slots):
    n = len(shards)
    n_slot = 12

    def body(*refs):
        ins, outs = refs[:n], refs[2 * n:3 * n]
        send, recv = refs[3 * n:]
        x, y, c = _coords()
        me, cx, cy, cd = 2 * x + y, 2 * (1 - x) + y, 2 * x + (1 - y), 2 * (1 - x) + (1 - y)
        dev_x, dev_y, dev_s = (1 - x, y, c), (x, 1 - y, c), (x, y, 1 - c)
        started = []

        def copy(w, slot, src, dst, dev):
            return pltpu.make_async_remote_copy(src_ref=src, dst_ref=dst, send_sem=send.at[w, slot], recv_sem=recv.at[w, slot],
                                                device_id=dev, device_id_type=MESH)

        def go(cp):
            cp.start()
            started.append(cp)

        for w in range(n):
            q = shards[w].shape[0] // 4
            rows = [pl.ds(c * 2 * q + k * q, q) for k in range(2)]
            theirs = [pl.ds((1 - c) * 2 * q + k * q, q) for k in range(2)]
            own = [(ins[w].at[r, :], outs[w].at[me, r, :]) for r in rows]
            go(copy(w, 0, *own[0], dev_x))
            go(copy(w, 2, *own[1], dev_y))
            go(copy(w, 1, *own[1], dev_x))
            go(copy(w, 3, *own[0], dev_y))
            arrivals = [(0, cx, 0, (4, dev_y)), (2, cy, 1, (5, dev_x)), (1, cx, 1, None), (3, cy, 0, None), (4, cd, 0, None), (5, cd, 1, None)]
            for k, (slot, chip, quarter, onward) in enumerate(arrivals):
                blk = outs[w].at[chip, rows[quarter], :]
                copy(w, slot, blk, blk, dev_s).wait_recv()
                if onward is not None:
                    go(copy(w, onward[0], blk, blk, onward[1]))
                go(copy(w, 6 + k, blk, blk, dev_s))
            for k, (slot, chip, quarter, onward) in enumerate(arrivals):
                blk = outs[w].at[chip, theirs[quarter], :]
                copy(w, 6 + k, blk, blk, dev_s).wait_recv()
        for cp in started:
            cp.wait_send()

    return pl.pallas_call(
        body, in_specs=[HBM] * (2 * n), out_specs=[HBM] * n,
        out_shape=tuple(SDS((N_CHIPS,) + s.shape, s.dtype) for s in shards),
        input_output_aliases={n + w: w for w in range(n)},
        scratch_shapes=[pltpu.SemaphoreType.DMA((n, n_slot))] * 2,
        name="allgather_weights", compiler_params=pltpu.CompilerParams(has_side_effects=True))(*shards, *slots)


def _x_pair_exchange(parts):
    n = len(parts)
    halves = [p.shape[1] // 2 for p in parts]

    def copies(ins, outs, sems):
        send, recv = sems
        x, y, c = _coords()
        return [pltpu.make_async_remote_copy(src_ref=ins[w].at[:, pl.ds((1 - c) * halves[w], halves[w]), :], dst_ref=outs[w],
                                             send_sem=send.at[w], recv_sem=recv.at[w], device_id=(x, y, 1 - c), device_id_type=MESH)
                for w in range(n)]

    def start(ins, outs, sems):
        for cp in copies(ins, outs, sems):
            cp.start()

    def wait(ins, outs, sems):
        for cp in copies(ins, outs, sems):
            cp.wait()

    return _Exchange(ins=list(parts), out_shape=[SDS((N_CHIPS, p.shape[1] // 2, p.shape[2]), p.dtype) for p in parts],
                     scratch=[pltpu.SemaphoreType.DMA((n,))] * 2, aliases={}, start=start, wait=wait)


def _x_chip_exchange(sums):
    n = len(sums)

    def copies(ins, outs, sems):
        send, recv = sems
        x, y, c = _coords()
        return [pltpu.make_async_remote_copy(src_ref=ins[w].at[2 * px + py], dst_ref=outs[w].at[j], send_sem=send.at[w, j],
                                             recv_sem=recv.at[w, j], device_id=(px, py, c), device_id_type=MESH)
                for w in range(n) for j, (px, py) in enumerate(_other_chips(x, y))]

    def start(ins, outs, sems):
        for cp in copies(ins, outs, sems):
            cp.start()

    def wait(ins, outs, sems):
        for cp in copies(ins, outs, sems):
            cp.wait()

    return _Exchange(ins=list(sums), out_shape=[SDS((N_CHIPS - 1,) + s.shape[1:], s.dtype) for s in sums],
                     scratch=[pltpu.SemaphoreType.DMA((n, 3)), pltpu.SemaphoreType.DMA((n, 3))], aliases={}, start=start, wait=wait)


def _x_gather_ici(shards, slots):
    n = len(shards)
    halves = [s.shape[0] // 2 for s in shards]

    def copies(ins, outs, sems):
        send, recv = sems
        x, y, c = _coords()
        me = 2 * x + y
        out = []
        for w in range(n):
            mine = pl.ds(c * halves[w], halves[w])
            for j, (px, py) in enumerate(_other_chips(x, y)):
                snd = pltpu.make_async_remote_copy(src_ref=ins[w].at[mine, :], dst_ref=outs[w].at[me, mine, :], send_sem=send.at[w, j],
                                                   recv_sem=recv.at[w, j], device_id=(px, py, c), device_id_type=MESH)
                got = outs[w].at[2 * px + py, mine, :]
                rcv = pltpu.make_async_remote_copy(src_ref=got, dst_ref=got, send_sem=send.at[w, j], recv_sem=recv.at[w, j],
                                                   device_id=(px, py, c), device_id_type=MESH)
                out.append((snd, rcv))
        return out

    def start(ins, outs, sems):
        for snd, _ in copies(ins, outs, sems):
            snd.start()

    def wait(ins, outs, sems):
        for snd, rcv in copies(ins, outs, sems):
            rcv.wait_recv()
            snd.wait_send()

    return _Exchange(ins=list(shards) + list(slots), out_shape=[SDS(s.shape, s.dtype) for s in slots],
                     scratch=[pltpu.SemaphoreType.DMA((n, 3)), pltpu.SemaphoreType.DMA((n, 3))],
                     aliases={n + w: w for w in range(n)}, start=start, wait=wait)


def _x_gather_fwd(bufs):
    n = len(bufs)
    halves = [b.shape[1] // 2 for b in bufs]

    def copies(ins, outs, sems):
        send, recv = sems
        x, y, c = _coords()
        out = []
        for w in range(n):
            for j, (px, py) in enumerate(_other_chips(x, y)):
                mine = outs[w].at[2 * px + py, pl.ds(c * halves[w], halves[w]), :]
                theirs = outs[w].at[2 * px + py, pl.ds((1 - c) * halves[w], halves[w]), :]
                snd = pltpu.make_async_remote_copy(src_ref=mine, dst_ref=mine, send_sem=send.at[w, j], recv_sem=recv.at[w, j],
                                                   device_id=(x, y, 1 - c), device_id_type=MESH)
                rcv = pltpu.make_async_remote_copy(src_ref=theirs, dst_ref=theirs, send_sem=send.at[w, j], recv_sem=recv.at[w, j],
                                                   device_id=(x, y, 1 - c), device_id_type=MESH)
                out.append((snd, rcv))
        return out

    def start(ins, outs, sems):
        for snd, _ in copies(ins, outs, sems):
            snd.start()

    def wait(ins, outs, sems):
        for snd, rcv in copies(ins, outs, sems):
            rcv.wait_recv()
            snd.wait_send()

    return _Exchange(ins=list(bufs), out_shape=[SDS(b.shape, b.dtype) for b in bufs],
                     scratch=[pltpu.SemaphoreType.DMA((n, 3)), pltpu.SemaphoreType.DMA((n, 3))],
                     aliases={w: w for w in range(n)}, start=start, wait=wait)


def _final_exchange(bufs, vec):
    n = len(bufs)
    R = vec.shape[0]
    nd = 8

    def body(*refs):
        v_ref = refs[n]
        outs = refs[n + 1:2 * n + 1]
        o_ref, sbuf, send, recv, ssend, srecv = refs[2 * n + 1:]
        x, y, c = _coords()
        me = 4 * x + 2 * y + c
        sbuf[me] = v_ref[...]
        small = []
        for k in range(1, nd):
            kx, ky, kc = (k >> 2) & 1, (k >> 1) & 1, k & 1
            tx = x + kx - 2 * x * kx
            ty = y + ky - 2 * y * ky
            tc = c + kc - 2 * c * kc
            cp = pltpu.make_async_remote_copy(src_ref=v_ref, dst_ref=sbuf.at[me], send_sem=ssend.at[k], recv_sem=srecv.at[k],
                                              device_id=(tx, ty, tc), device_id_type=MESH)
            cp.start()
            small.append((cp, 4 * tx + 2 * ty + tc))
        cps = []
        for w in range(n):
            h = bufs[w].shape[0] // 2
            rows = outs[w].at[pl.ds(c * h, h), :]
            rc = pltpu.make_async_remote_copy(src_ref=rows, dst_ref=rows, send_sem=send.at[w], recv_sem=recv.at[w],
                                              device_id=(x, y, 1 - c), device_id_type=MESH)
            rc.start()
            cps.append(rc)
        for k, (cp, src) in zip(range(1, nd), small):
            pltpu.make_async_remote_copy(src_ref=v_ref, dst_ref=sbuf.at[src], send_sem=ssend.at[k], recv_sem=srecv.at[k],
                                         device_id=(x, y, c), device_id_type=MESH).wait_recv()
        for cp, _ in small:
            cp.wait_send()
        acc = sbuf[0]
        for d in range(1, nd):
            acc = acc + sbuf[d]
        o_ref[...] = acc
        for w, rc in enumerate(cps):
            h = bufs[w].shape[0] // 2
            other = outs[w].at[pl.ds((1 - c) * h, h), :]
            pltpu.make_async_remote_copy(src_ref=other, dst_ref=other, send_sem=send.at[w], recv_sem=recv.at[w],
                                         device_id=(x, y, 1 - c), device_id_type=MESH).wait_recv()
            rc.wait_send()

    vm = pl.BlockSpec(memory_space=pltpu.VMEM)
    res = pl.pallas_call(body, in_specs=[HBM] * n + [vm], out_specs=[HBM] * n + [vm],
                         out_shape=tuple(SDS(a.shape, a.dtype) for a in bufs) + (SDS((R, LANES), f32),),
                         input_output_aliases={w: w for w in range(n)},
                         scratch_shapes=[pltpu.VMEM((nd, R, LANES), f32)] + [pltpu.SemaphoreType.DMA((n,))] * 2
                         + [pltpu.SemaphoreType.DMA((nd,))] * 2, name="final_exchange",
                         compiler_params=pltpu.CompilerParams(has_side_effects=True))(*bufs, vec)
    return res[:n], res[n]


def _rows128(a, rows):
    flat = a.reshape(-1)
    return jnp.pad(flat, (0, rows * LANES - flat.shape[0])).reshape(rows, LANES)


GATHER_1 = ("w_conv_out", "w_attn_out", "w_o", "w_ffn_gate")
REDUCE_A = ("w_ffn_gate", "w_ffn_up", "w_ffn_down")
REDUCE_B = ("w_conv_out", "w_attn_out", "w_o")
REDUCE_C = ("w_in",)


def _step(x, target, norm1_g, gate_b, conv_w, conv_b, conv_ln_g, conv_ln_b, norm2_g, norm_f_g, w_in, shards, slots, chip1, core):
    B = x.shape[0]
    T = B * SEQ
    xf = x.reshape(T, D_MODEL)
    tf = target.reshape(T, D_MODEL)

    h, h_t = _rms_fwd(xf, norm1_g, "rms1_fwd")
    proj, got1 = _mm("in_proj", h, w_in, grid=(N_CHIPS, T // 1024),
                     a_spec=pl.BlockSpec((1024, D_MODEL), lambda s, m: (m, 0)),
                     b_spec=pl.BlockSpec((None, D_MODEL, IN_S), lambda s, m: (s, 0, 0)),
                     o_spec=pl.BlockSpec((1024, IN_S), lambda s, m: (m, s)), o_shape=(T, IN_W), o_dtype=bf16, dims=NN,
                     comm=_x_gather_ici([shards[n] for n in GATHER_1], [slots[n] for n in GATHER_1]))
    proj3 = proj.reshape(B, SEQ, IN_W)
    proj4 = proj.reshape(B, 1, SEQ, IN_W)

    qkv = [None] + [_qkv_to_residues(proj3, g) for g in range(1, len(GROUPS))]

    def qkv_args(g):
        return (proj4, proj4, proj4, 0, 3, 6) if g == 0 else (qkv[g], qkv[g], qkv[g], 0, 1, 2)

    (o4_0, l4_0), full1 = _attn_fwd(0, *qkv_args(0), comm=_x_gather_fwd(list(got1)))
    (o4_1, l4_1), got_down = _attn_fwd(1, *qkv_args(1), comm=_x_gather_ici([shards["w_ffn_down"]], [slots["w_ffn_down"]]))
    (o4_2, l4_2), (w_down,) = _attn_fwd(2, *qkv_args(2), comm=_x_gather_fwd(list(got_down)))
    full = dict(zip(GATHER_1, full1))
    w_conv_out, w_attn_out, w_o, w_gate = (full[n] for n in GATHER_1)
    w_conv_out_f = w_conv_out.reshape(D_MODEL, D_MODEL)
    w_o_f = w_o.reshape(D_MODEL, D_MODEL)
    mix3, lse3, lse_r1, lse_r2 = _attn_mix(o4_0.reshape(B, SEQ, GW), l4_0.reshape(B, SEQ, LANES), [o4_1, o4_2], [l4_1, l4_2])
    mix = mix3.reshape(T, GW)
    y_attn = _attn_out_fwd(mix, w_attn_out)

    c1, got_up = _glu_conv_fwd(proj3, conv_w, conv_b, comm=_x_gather_ici([shards["w_ffn_up"]], [slots["w_ffn_up"]]))
    c1 = c1.reshape(T, D_MODEL)
    (c3, y_conv), (w_up,) = _conv_out_fwd(c1, conv_ln_g, conv_ln_b, w_conv_out_f, comm=_x_gather_fwd(list(got_up)))

    merged, x1, h2 = _merge_o_proj_rms(proj, gate_b, y_attn, y_conv, w_o_f, xf, norm2_g)
    fa, fb, ff = _ffn_up(h2, w_gate, w_up)
    loss, dx2, d_gf = _ffn_down_loss(ff, w_down, x1, norm_f_g.reshape(1, D_MODEL), tf)

    d_w_down = _mm("d_w_down", ff, dx2, grid=(N_CHIPS, T // 1024),
                   a_spec=pl.BlockSpec((None, 1024, FF_S), lambda s, k: (s, k, 0)),
                   b_spec=pl.BlockSpec((1024, D_MODEL), lambda s, k: (k, 0)),
                   o_spec=pl.BlockSpec((None, FF_S, D_MODEL), lambda s, k: (s, 0, 0)), o_shape=(N_CHIPS, FF_S, D_MODEL),
                   o_dtype=bf16, dims=TN, acc_shape=(FF_S, D_MODEL), k_axis=1, sem=("parallel", "arbitrary"))
    da, db = _ffn_down_bwd(dx2, w_down, fa, fb)

    def d_w_ff(name, dz):
        return _mm(name, dz, h2, grid=(N_CHIPS, T // 2048),
                   a_spec=pl.BlockSpec((None, 2048, FF_S), lambda s, k: (s, k, 0)),
                   b_spec=pl.BlockSpec((2048, D_MODEL), lambda s, k: (k, 0)),
                   o_spec=pl.BlockSpec((None, FF_S, D_MODEL), lambda s, k: (s, 0, 0)), o_shape=(N_CHIPS, FF_S, D_MODEL),
                   o_dtype=bf16, dims=TN, acc_shape=(FF_S, D_MODEL), k_axis=1, sem=("parallel", "arbitrary"))

    d_w_gate = d_w_ff("d_w_gate", da)
    d_w_up = d_w_ff("d_w_up", db)
    part = dict(w_ffn_gate=d_w_gate, w_ffn_up=d_w_up, w_ffn_down=d_w_down)

    def pair_sums(names, got):
        out = {}
        for ns, gs in _same_shape_groups(names, got):
            out.update(zip(ns, _add_pairs([part[n] for n in ns], gs, core, "pair_sum_" + ns[0])))
        return [out[n] for n in names]

    (dx1, d_g2), got = _mm_rms_bwd("ffn_dh2", [(da, w_gate), (db, w_up)], x1, norm2_g, dx2, tm=1024,
                                   a_spec=pl.BlockSpec((None, 1024, FF_S), lambda m, s: (s, m, 0)),
                                   b_spec=pl.BlockSpec((None, FF_S, D_MODEL), lambda m, s: (s, 0, 0)), dims=NN,
                                   comm=_x_pair_exchange([part[n] for n in REDUCE_A]), vmem=VMEM_LIMIT_BIG)
    sums_a = pair_sums(REDUCE_A, got)

    d_w_o = _mm_tn_tokens("d_w_o", merged, dx1, bf16).reshape(N_CHIPS, D_MODEL // N_CHIPS, D_MODEL)
    dmerged = _mm_nt_full("d_merged", dx1, w_o_f, bf16)
    dya, dyc, dga, dgc, d_gba, d_gbc = _merge_bwd(dmerged, proj, gate_b, y_attn, y_conv)

    d_w_conv_out = _mm_tn_tokens("d_w_conv_out", c3, dyc, bf16).reshape(N_CHIPS, D_MODEL // N_CHIPS, D_MODEL)
    dc1, d_ln_g, d_ln_b = _conv_out_bwd(dyc, w_conv_out_f, c1, conv_ln_g, conv_ln_b)
    (dua, dub, d_conv_w, d_conv_b), got_a = _glu_conv_bwd(dc1.reshape(B, SEQ, D_MODEL), proj3, conv_w, comm=_x_chip_exchange(sums_a))

    d_w_attn_out = _attn_out_bwd_w(mix, dya)
    part.update(w_conv_out=d_w_conv_out, w_attn_out=d_w_attn_out, w_o=d_w_o)
    dmix, got = _attn_out_bwd_x(dya, w_attn_out, comm=_x_pair_exchange([part[n] for n in REDUCE_B]))
    sums_b = pair_sums(REDUCE_B, got)
    dmix3 = dmix.reshape(B, SEQ, GW)
    delta3, delta_r1, delta_r2, dmix_r1, dmix_r2 = _attn_delta(dmix3, mix3)
    one = (B, 1, SEQ)
    (dq0, dk0, dv0), got_b = _attn_bwd(0, *qkv_args(0), dmix3.reshape(one + (GW,)), lse3.reshape(one + (LANES,)),
                                       delta3.reshape(one + (LANES,)), comm=_x_chip_exchange(sums_b))
    dqkv = [tuple(t.reshape(B, SEQ, GW) for t in (dq0, dk0, dv0)),
            _attn_bwd(1, *qkv_args(1), dmix_r1, lse_r1, delta_r1), _attn_bwd(2, *qkv_args(2), dmix_r2, lse_r2, delta_r2)]
    dproj = _assemble_dproj(dqkv, dua, dub, dga.reshape(B, SEQ, D_MODEL), dgc.reshape(B, SEQ, D_MODEL)).reshape(T, IN_W)

    d_w_in = _mm("d_w_in", h_t, dproj, grid=(N_CHIPS, T // 2048),
                 a_spec=pl.BlockSpec((D_MODEL, 2048), lambda s, k: (0, k)),
                 b_spec=pl.BlockSpec((2048, IN_S), lambda s, k: (k, s)),
                 o_spec=pl.BlockSpec((None, D_MODEL, IN_S), lambda s, k: (s, 0, 0)), o_shape=(N_CHIPS, D_MODEL, IN_S),
                 o_dtype=bf16, dims=NN, acc_shape=(D_MODEL, IN_S), k_axis=1, sem=("parallel", "arbitrary"), vmem=VMEM_LIMIT_BIG)
    part.update(w_in=d_w_in)
    sums_c = pair_sums(REDUCE_C, _run_exchange(_x_pair_exchange([d_w_in]), "grad_pair_exchange_c"))
    (dx, d_g1), got_c = _mm_rms_bwd("d_h", [(dproj, w_in)], xf, norm1_g, dx1, tm=512,
                                    a_spec=pl.BlockSpec((512, IN_S), lambda m, s: (m, s)),
                                    b_spec=pl.BlockSpec((None, D_MODEL, IN_S), lambda m, s: (s, 0, 0)), dims=NT,
                                    comm=_x_chip_exchange(sums_c))

    names = REDUCE_A + REDUCE_B + REDUCE_C
    sums = dict(zip(names, sums_a + sums_b + sums_c))
    halves = {}
    for ns, gs in _same_shape_groups(names, got_a + got_b + got_c):
        halves.update(zip(ns, _sum4([sums[n] for n in ns], gs, chip1, core, "chip_sum_" + ns[0])))
    big = {n: halves[n] for n in names}
    small = dict(norm1_g=d_g1, gate_b=jnp.concatenate([d_gba, d_gbc], axis=-1), conv_b=d_conv_b, conv_ln_g=d_ln_g,
                 conv_ln_b=d_ln_b, norm2_g=d_g2, norm_f_g=d_gf, conv_w=d_conv_w)
    return loss, dx.reshape(B, SEQ, D_MODEL), big, small


BIG = ("w_in", "w_conv_out", "w_attn_out", "w_o", "w_ffn_gate", "w_ffn_up", "w_ffn_down")
TRANSPOSED = ("w_ffn_gate", "w_ffn_up")
SMALL = ("norm1_g", "gate_b", "conv_b", "conv_ln_g", "conv_ln_b", "norm2_g", "norm_f_g")
SMALL_ROWS = {"norm1_g": 8, "gate_b": 16, "conv_b": 8, "conv_ln_g": 8, "conv_ln_b": 8, "norm2_g": 8, "norm_f_g": 8}
LOSS_ROWS = 8
CONVW_ROWS = 32 * D_MODEL // LANES


def kernel(x, norm1_g, w_in, gate_b, conv_w, conv_b, conv_ln_g, conv_ln_b, w_conv_out, w_attn_out, w_o, norm2_g, w_ffn_gate, w_ffn_up, w_ffn_down, norm_f_g, loss_target, m_norm1_g, m_w_in, m_gate_b, m_conv_w, m_conv_b, m_conv_ln_g, m_conv_ln_b, m_w_conv_out, m_w_attn_out, m_w_o, m_norm2_g, m_w_ffn_gate, m_w_ffn_up, m_w_ffn_down, m_norm_f_g, v_norm1_g, v_w_in, v_gate_b, v_conv_w, v_conv_b, v_conv_ln_g, v_conv_ln_b, v_w_conv_out, v_w_attn_out, v_w_o, v_norm2_g, v_w_ffn_gate, v_w_ffn_up, v_w_ffn_down, v_norm_f_g):
    W = dict(norm1_g=norm1_g, w_in=w_in, gate_b=gate_b, conv_w=conv_w, conv_b=conv_b, conv_ln_g=conv_ln_g, conv_ln_b=conv_ln_b,
             w_conv_out=w_conv_out, w_attn_out=w_attn_out, w_o=w_o, norm2_g=norm2_g, w_ffn_gate=w_ffn_gate, w_ffn_up=w_ffn_up,
             w_ffn_down=w_ffn_down, norm_f_g=norm_f_g)
    M = dict(norm1_g=m_norm1_g, w_in=m_w_in, gate_b=m_gate_b, conv_w=m_conv_w, conv_b=m_conv_b, conv_ln_g=m_conv_ln_g,
             conv_ln_b=m_conv_ln_b, w_conv_out=m_w_conv_out, w_attn_out=m_w_attn_out, w_o=m_w_o, norm2_g=m_norm2_g,
             w_ffn_gate=m_w_ffn_gate, w_ffn_up=m_w_ffn_up, w_ffn_down=m_w_ffn_down, norm_f_g=m_norm_f_g)
    V = dict(norm1_g=v_norm1_g, w_in=v_w_in, gate_b=v_gate_b, conv_w=v_conv_w, conv_b=v_conv_b, conv_ln_g=v_conv_ln_g,
             conv_ln_b=v_conv_ln_b, w_conv_out=v_w_conv_out, w_attn_out=v_w_attn_out, w_o=v_w_o, norm2_g=v_norm2_g,
             w_ffn_gate=v_w_ffn_gate, w_ffn_up=v_w_ffn_up, w_ffn_down=v_w_ffn_down, norm_f_g=v_norm_f_g)
    order = list(W)

    def as2d(n, a):
        a = a.reshape(a.shape[-2:])
        return a.T if n in TRANSPOSED else a

    def from2d(n, a):
        return (a.T if n in TRANSPOSED else a).reshape(W[n].shape)

    shard2d = {n: as2d(n, W[n]) for n in BIG}
    chip = 2 * lax.axis_index("x") + lax.axis_index("y")

    core = lax.axis_index("c").astype(jnp.int32).reshape(1)
    chip1 = chip.astype(jnp.int32).reshape(1)
    shards = {n: shard2d[n].astype(bf16) for n in BIG}

    def slot_for(s):
        return lax.dynamic_update_slice(lax.empty((N_CHIPS,) + s.shape, s.dtype), s[None], (chip, 0, 0))

    slots = {n: slot_for(s) for n, s in shards.items()}
    cw = jnp.pad(conv_w.reshape(CONV_K, D_MODEL // N_CHIPS), ((0, CONV_HALO - CONV_K), (0, 0)))
    w_in_full, cw4 = _allgather_weights([shards["w_in"], cw], [slots["w_in"], slot_for(cw)])
    conv_w_full = cw4.transpose(1, 0, 2).reshape(CONV_HALO, D_MODEL)[:CONV_K]

    loss, grad_x, grads, gsmall = _step(x, loss_target, norm1_g, gate_b, conv_w_full, conv_b, conv_ln_g, conv_ln_b, norm2_g,
                                        norm_f_g, w_in_full, shards, slots, chip1, core)

    pieces = [_rows128(loss, LOSS_ROWS)] + [_rows128(gsmall[n], SMALL_ROWS[n]) for n in SMALL] + [_rows128(gsmall["conv_w"], CONVW_ROWS)]
    full_rows, tot = _final_exchange([grads[n] for n in BIG], jnp.concatenate(pieces, axis=0))
    grads = dict(zip(BIG, full_rows))
    loss_out = tot[0, 0]
    row = LOSS_ROWS
    for n in SMALL:
        grads[n] = tot[row: row + W[n].size // LANES].reshape(W[n].shape)
        row += SMALL_ROWS[n]
    dcw = tot[row: row + CONV_K * D_MODEL // LANES].reshape(CONV_K, D_MODEL)
    grads["conv_w"] = lax.dynamic_slice(dcw, (0, chip * (D_MODEL // N_CHIPS)), (CONV_K, D_MODEL // N_CHIPS))

    delta, new_m, new_v = {}, {}, {}
    for ns, ws in _same_shape_groups(BIG, [shard2d[n] for n in BIG]):
        ds, nms, nvs = _adamw(ws, [grads[n] for n in ns], [as2d(n, M[n]) for n in ns], [as2d(n, V[n]) for n in ns], "adamw_" + ns[0])
        for n, d, nm, nv in zip(ns, ds, nms, nvs):
            delta[n], new_m[n], new_v[n], grads[n] = (from2d(n, t) for t in (d, nm, nv, grads[n]))

    def pack(src):
        return jnp.concatenate([_rows128(src[n], SMALL_ROWS[n]) for n in SMALL], axis=0)

    (d,), (nm,), (nv,) = _adamw([pack(W)], [pack(grads)], [pack(M)], [pack(V)], "adamw_small")
    row = 0
    for n in SMALL:
        k = W[n].size // LANES
        delta[n], new_m[n], new_v[n] = (t[row: row + k].reshape(W[n].shape) for t in (d, nm, nv))
        row += SMALL_ROWS[n]

    def pad32(a):
        return jnp.pad(a.reshape(CONV_K, D_MODEL // N_CHIPS), ((0, 1), (0, 0)))

    (d,), (nm,), (nv,) = _adamw([pad32(conv_w)], [pad32(grads["conv_w"])], [pad32(m_conv_w)], [pad32(v_conv_w)], "adamw_conv_w")
    delta["conv_w"], new_m["conv_w"], new_v["conv_w"] = (t[:CONV_K].reshape(conv_w.shape) for t in (d, nm, nv))
    grads["conv_w"] = grads["conv_w"].reshape(conv_w.shape)

    return (loss_out, grad_x, *[grads[n] for n in order], *[delta[n] for n in order],
            *[new_m[n] for n in order], *[new_v[n] for n in order])
```

```python
import functools
import math
from typing import Callable, NamedTuple

import numpy as np
import jax
import jax.numpy as jnp
from jax import lax
from jax.experimental import pallas as pl
from jax.experimental.pallas import tpu as pltpu

f32 = jnp.float32
bf16 = jnp.bfloat16
SDS = jax.ShapeDtypeStruct
MESH = pl.DeviceIdType.MESH

D_MODEL = 1024
SEQ = 2048
HEAD_DIM = 64
HEADS = 8
GROUPS = ((128, 1), (512, 4), (2048, 16))
GW = HEADS * HEAD_DIM
ATTN_W = len(GROUPS) * GW
Q_BLOCK = 128
CONV_K = 31
D_FF = 2816
IN_W = 3 * ATTN_W + 2 * D_MODEL + 2 * D_MODEL
N_CHIPS = 4
IN_S = IN_W // N_CHIPS
FF_S = D_FF // N_CHIPS
RMS_EPS = 1e-6
LN_EPS = 1e-5
LR, B1, B2, ADAM_EPS, WD, STEP = 0.001, 0.9, 0.999, 1e-08, 0.01, 10
NEG = -1e30
LANES = 128
VMEM_LIMIT = 48 * 2 ** 20
VMEM_LIMIT_BIG = 54 * 2 ** 20
CB = 512
UA_CB, UB_CB, GA_CB, GC_CB = 9, 11, 13, 15


def _alibi_slope_list(n):
    def pow2(m):
        start = 2.0 ** (-8.0 / m)
        return [start ** (i + 1) for i in range(m)]
    if math.log2(n).is_integer():
        return pow2(n)
    c = 2 ** math.floor(math.log2(n))
    return pow2(c) + _alibi_slope_list(2 * c)[0::2][: n - c]


_SLOPES = np.asarray(sorted(_alibi_slope_list(len(GROUPS) * HEADS), reverse=True), dtype=np.float32).reshape(len(GROUPS), HEADS)


def _cp(sem=None, vmem=VMEM_LIMIT):
    return pltpu.CompilerParams(dimension_semantics=sem, vmem_limit_bytes=vmem)


def _sigmoid(x):
    return 1.0 / (1.0 + jnp.exp(-x))


HBM = pl.BlockSpec(memory_space=pl.ANY)


class _Exchange(NamedTuple):
    ins: list
    out_shape: list
    scratch: list
    aliases: dict
    start: Callable
    wait: Callable


def _launch(body, *, name, grid, in_specs, out_specs, out_shape, args, scratch_shapes=(), sem=None, comm=None, vmem=VMEM_LIMIT):
    if comm is None:
        return pl.pallas_call(body, grid=grid, in_specs=in_specs, out_specs=out_specs, out_shape=out_shape,
                              scratch_shapes=list(scratch_shapes), name=name, compiler_params=_cp(sem, vmem))(*args)
    multi = isinstance(out_shape, (tuple, list))
    m_out = list(out_shape) if multi else [out_shape]
    m_ospec = list(out_specs) if multi else [out_specs]
    n_in, n_out, n_scr = len(in_specs), len(m_out), len(scratch_shapes)
    nc_in, nc_out = len(comm.ins), len(comm.out_shape)

    def hosted(*refs):
        bounds = np.cumsum([0, n_in, nc_in, n_out, nc_out, n_scr])
        mi, ci, mo, co, ms = (refs[a:b] for a, b in zip(bounds[:-1], bounds[1:]))
        cs = refs[bounds[-1]:]
        ids = [pl.program_id(a) for a in range(len(grid))]
        first = functools.reduce(jnp.logical_and, [i == 0 for i in ids])
        last = functools.reduce(jnp.logical_and, [i == g - 1 for i, g in zip(ids, grid)])

        @pl.when(first)
        def _():
            comm.start(ci, co, cs)

        body(*mi, *mo, *ms)

        @pl.when(last)
        def _():
            comm.wait(ci, co, cs)

    res = pl.pallas_call(
        hosted, grid=grid, in_specs=list(in_specs) + [HBM] * nc_in, out_specs=m_ospec + [HBM] * nc_out,
        out_shape=tuple(m_out) + tuple(comm.out_shape), scratch_shapes=list(scratch_shapes) + list(comm.scratch),
        input_output_aliases={n_in + i: n_out + o for i, o in comm.aliases.items()}, name=name + "_comm",
        compiler_params=pltpu.CompilerParams(dimension_semantics=("arbitrary",) * len(grid), vmem_limit_bytes=vmem,
                                             has_side_effects=True))(*args, *comm.ins)
    return (tuple(res[:n_out]) if multi else res[0]), tuple(res[n_out:])


def _run_exchange(ex, name):
    n_in = len(ex.ins)

    def body(*refs):
        ins, outs, sems = refs[:n_in], refs[n_in:n_in + len(ex.out_shape)], refs[n_in + len(ex.out_shape):]
        ex.start(ins, outs, sems)
        ex.wait(ins, outs, sems)

    return pl.pallas_call(body, in_specs=[HBM] * n_in, out_specs=[HBM] * len(ex.out_shape), out_shape=tuple(ex.out_shape),
                          scratch_shapes=list(ex.scratch), input_output_aliases=dict(ex.aliases), name=name,
                          compiler_params=pltpu.CompilerParams(has_side_effects=True))(*ex.ins)


def _rmsnorm(xv, gv):
    return xv * lax.rsqrt(jnp.mean(xv * xv, axis=-1, keepdims=True) + RMS_EPS) * gv


def _rms_fwd(x, g, name):
    T = x.shape[0]
    tm = 512

    def body(x_ref, g_ref, o_ref, ot_ref):
        hv = _rmsnorm(x_ref[...], g_ref[...])
        o_ref[...] = hv.astype(bf16)
        ot_ref[...] = hv.T.astype(bf16)

    row = pl.BlockSpec((tm, D_MODEL), lambda i: (i, 0))
    vec = pl.BlockSpec((1, D_MODEL), lambda i: (0, 0))
    return pl.pallas_call(body, grid=(T // tm,), in_specs=[row, vec], out_specs=[row, pl.BlockSpec((D_MODEL, tm), lambda i: (0, i))],
                          out_shape=(SDS((T, D_MODEL), bf16), SDS((D_MODEL, T), bf16)), name=name,
                          compiler_params=_cp(("parallel",)))(x, g)


def _rms_bwd_tile(dyv, xv, gv, dres):
    r = lax.rsqrt(jnp.mean(xv * xv, axis=-1, keepdims=True) + RMS_EPS)
    xh = xv * r
    dxh = dyv * gv
    dx = dres + r * (dxh - xh * jnp.mean(dxh * xh, axis=-1, keepdims=True))
    return dx, jnp.sum(dyv * xh, axis=0, keepdims=True)


def _accumulate(first, refs_parts):
    @pl.when(first)
    def _():
        for ref, part in refs_parts:
            ref[...] = part

    @pl.when(jnp.logical_not(first))
    def _():
        for ref, part in refs_parts:
            ref[...] += part


def _mm_rms_bwd(name, ops, x, g, dres, *, tm, a_spec, b_spec, dims, comm=None, vmem=VMEM_LIMIT):
    T = x.shape[0]
    n = len(ops)

    def body(*refs):
        ab_refs = refs[:2 * n]
        x_ref, g_ref, r_ref, dx_ref, dg_ref, acc = refs[2 * n:]
        m, s = pl.program_id(0), pl.program_id(1)
        p = sum(lax.dot_general(ab_refs[2 * i][...], ab_refs[2 * i + 1][...], dims, preferred_element_type=f32) for i in range(n))

        @pl.when(s == 0)
        def _():
            acc[...] = p

        @pl.when(s > 0)
        def _():
            acc[...] += p

        @pl.when(s == N_CHIPS - 1)
        def _():
            dx, part = _rms_bwd_tile(acc[...], x_ref[...], g_ref[...], r_ref[...])
            dx_ref[...] = dx
            _accumulate(m == 0, [(dg_ref, part)])

    row = pl.BlockSpec((tm, D_MODEL), lambda m, s: (m, 0))
    vec = pl.BlockSpec((1, D_MODEL), lambda m, s: (0, 0))
    return _launch(body, name=name, grid=(T // tm, N_CHIPS), in_specs=[a_spec, b_spec] * n + [row, vec, row], out_specs=[row, vec],
                   out_shape=(SDS((T, D_MODEL), f32), SDS((1, D_MODEL), f32)), args=tuple(t for ab in ops for t in ab) + (x, g, dres),
                   scratch_shapes=[pltpu.VMEM((tm, D_MODEL), f32)], sem=("arbitrary", "arbitrary"), comm=comm, vmem=vmem)


def _ffn_down_loss(ff, wd, x1, gf, target):
    T = x1.shape[0]
    tm = 1024
    nm = T // tm

    def body(f_ref, w_ref, x_ref, g_ref, t_ref, loss_ref, dx_ref, dg_ref, acc):
        m, s = pl.program_id(0), pl.program_id(1)
        p = jnp.dot(f_ref[...], w_ref[...], preferred_element_type=f32)

        @pl.when(s == 0)
        def _():
            acc[...] = p

        @pl.when(s > 0)
        def _():
            acc[...] += p

        @pl.when(s == N_CHIPS - 1)
        def _():
            xv = acc[...] + x_ref[...]
            gv = g_ref[...]
            r = lax.rsqrt(jnp.mean(xv * xv, axis=-1, keepdims=True) + RMS_EPS)
            xh = xv * r
            e = xh * gv - t_ref[...]
            part_l = jnp.broadcast_to(0.5 * jnp.sum(jnp.mean(e * e, axis=-1, keepdims=True), axis=0, keepdims=True), (1, LANES))
            dy = e * (1.0 / D_MODEL)
            dxh = dy * gv
            dx_ref[...] = r * (dxh - xh * jnp.mean(dxh * xh, axis=-1, keepdims=True))
            part_g = jnp.sum(dy * xh, axis=0, keepdims=True)

            @pl.when(m == 0)
            def _():
                loss_ref[...] = part_l
                dg_ref[...] = part_g

            @pl.when(m > 0)
            def _():
                loss_ref[...] += part_l
                dg_ref[...] += part_g

    row = pl.BlockSpec((tm, D_MODEL), lambda m, s: (m, 0))
    vec = pl.BlockSpec((1, D_MODEL), lambda m, s: (0, 0))
    return pl.pallas_call(
        body, grid=(nm, N_CHIPS),
        in_specs=[pl.BlockSpec((None, tm, FF_S), lambda m, s: (s, m, 0)), pl.BlockSpec((None, FF_S, D_MODEL), lambda m, s: (s, 0, 0)),
                  row, vec, row],
        out_specs=[pl.BlockSpec((1, LANES), lambda m, s: (0, 0)), row, vec],
        out_shape=(SDS((1, LANES), f32), SDS((T, D_MODEL), f32), SDS((1, D_MODEL), f32)),
        scratch_shapes=[pltpu.VMEM((tm, D_MODEL), f32)], name="ffn_down_loss", compiler_params=_cp(("arbitrary", "arbitrary")))(
            ff, wd, x1, gf, target)


def _merge_o_proj_rms(proj, gate_b, ya, yc, w_o, x, g2):
    T = x.shape[0]
    tm = 512

    def body(ga0, ga1, gc0, gc1, b_ref, ya_ref, yc_ref, w_ref, x_ref, g_ref, m_ref, x1_ref, h2_ref):
        bv = b_ref[...]
        ga = jnp.concatenate([ga0[...], ga1[...]], axis=-1).astype(f32)
        gc = jnp.concatenate([gc0[...], gc1[...]], axis=-1).astype(f32)
        merged = (_sigmoid(ga + bv[:, :D_MODEL]) * ya_ref[...].astype(f32)
                  + _sigmoid(gc + bv[:, D_MODEL:]) * yc_ref[...].astype(f32)).astype(bf16)
        m_ref[...] = merged
        x1 = jnp.dot(merged, w_ref[...], preferred_element_type=f32) + x_ref[...]
        x1_ref[...] = x1
        h2_ref[...] = _rmsnorm(x1, g_ref[...]).astype(bf16)

    row = pl.BlockSpec((tm, D_MODEL), lambda i: (i, 0))
    return pl.pallas_call(
        body, grid=(T // tm,),
        in_specs=[pl.BlockSpec((tm, CB), lambda i, cb=cb: (i, cb)) for cb in (GA_CB, GA_CB + 1, GC_CB, GC_CB + 1)]
        + [pl.BlockSpec((1, 2 * D_MODEL), lambda i: (0, 0)), row, row, pl.BlockSpec((D_MODEL, D_MODEL), lambda i: (0, 0)), row,
           pl.BlockSpec((1, D_MODEL), lambda i: (0, 0))],
        out_specs=[row, row, row], out_shape=(SDS((T, D_MODEL), bf16), SDS((T, D_MODEL), f32), SDS((T, D_MODEL), bf16)),
        name="merge_o_proj_rms", compiler_params=_cp(("parallel",)))(proj, proj, proj, proj, gate_b, ya, yc, w_o, x, g2)


def _conv_out_fwd(c1, g, b, w, comm=None):
    T = c1.shape[0]
    tm = 1024

    def body(c_ref, g_ref, b_ref, w_ref, c3_ref, y_ref):
        cv = c_ref[...]
        mu = jnp.mean(cv, axis=-1, keepdims=True)
        cc = cv - mu
        var = jnp.mean(cc * cc, axis=-1, keepdims=True)
        c2 = cc * lax.rsqrt(var + LN_EPS) * g_ref[...] + b_ref[...]
        c3 = (c2 * _sigmoid(c2)).astype(bf16)
        c3_ref[...] = c3
        y_ref[...] = jnp.dot(c3, w_ref[...], preferred_element_type=f32).astype(bf16)

    row = pl.BlockSpec((tm, D_MODEL), lambda i: (i, 0))
    vec = pl.BlockSpec((1, D_MODEL), lambda i: (0, 0))
    sd = SDS((T, D_MODEL), bf16)
    return _launch(body, name="conv_out_fwd", grid=(T // tm,), in_specs=[row, vec, vec, pl.BlockSpec((D_MODEL, D_MODEL), lambda i: (0, 0))],
                   out_specs=[row, row], out_shape=(sd, sd), args=(c1, g, b, w), sem=("parallel",), comm=comm)


def _conv_out_bwd(dyc, w, c1, g, b):
    T = c1.shape[0]
    tm = 1024

    def body(d_ref, w_ref, c_ref, g_ref, b_ref, dc_ref, dg_ref, db_ref):
        dc3 = lax.dot_general(d_ref[...], w_ref[...], NT, preferred_element_type=f32)
        cv = c_ref[...]
        gv = g_ref[...]
        mu = jnp.mean(cv, axis=-1, keepdims=True)
        cc = cv - mu
        var = jnp.mean(cc * cc, axis=-1, keepdims=True)
        rs = lax.rsqrt(var + LN_EPS)
        xh = cc * rs
        c2 = xh * gv + b_ref[...]
        sg = _sigmoid(c2)
        dc2 = dc3 * (sg * (1.0 + c2 * (1.0 - sg)))
        dxh = dc2 * gv
        dc_ref[...] = rs * (dxh - jnp.mean(dxh, axis=-1, keepdims=True) - xh * jnp.mean(dxh * xh, axis=-1, keepdims=True))
        _accumulate(pl.program_id(0) == 0, [(dg_ref, jnp.sum(dc2 * xh, axis=0, keepdims=True)), (db_ref, jnp.sum(dc2, axis=0, keepdims=True))])

    row = pl.BlockSpec((tm, D_MODEL), lambda i: (i, 0))
    vec = pl.BlockSpec((1, D_MODEL), lambda i: (0, 0))
    return pl.pallas_call(body, grid=(T // tm,), in_specs=[row, pl.BlockSpec((D_MODEL, D_MODEL), lambda i: (0, 0)), row, vec, vec],
                          out_specs=[row, vec, vec],
                          out_shape=(SDS((T, D_MODEL), f32), SDS((1, D_MODEL), f32), SDS((1, D_MODEL), f32)),
                          name="conv_out_bwd", compiler_params=_cp(("arbitrary",)))(dyc, w, c1, g, b)


NN = (((1,), (0,)), ((), ()))
NT = (((1,), (1,)), ((), ()))
TN = (((0,), (0,)), ((), ()))


def _mm(name, a, b, *, grid, a_spec, b_spec, o_spec, o_shape, o_dtype, dims, acc_shape=None, k_axis=None,
        res=None, res_spec=None, sem=None, comm=None, vmem=VMEM_LIMIT):
    nk = 1 if k_axis is None else grid[k_axis]

    def body(*refs):
        if res is None:
            a_ref, b_ref, o_ref = refs[:3]
            r_ref, scr = None, refs[3:]
        else:
            a_ref, b_ref, r_ref, o_ref = refs[:4]
            scr = refs[4:]
        p = lax.dot_general(a_ref[...].astype(bf16), b_ref[...].astype(bf16), dims, preferred_element_type=f32)
        if nk == 1:
            if r_ref is not None:
                p = p + r_ref[...]
            o_ref[...] = p.astype(o_dtype)
            return
        acc = scr[0]
        k = pl.program_id(k_axis)

        @pl.when(k == 0)
        def _():
            acc[...] = p

        @pl.when(k > 0)
        def _():
            acc[...] += p

        @pl.when(k == nk - 1)
        def _():
            t = acc[...]
            if r_ref is not None:
                t = t + r_ref[...]
            o_ref[...] = t.astype(o_dtype)

    ins = [a, b] + ([] if res is None else [res])
    specs = [a_spec, b_spec] + ([] if res is None else [res_spec])
    scratch = [] if nk == 1 else [pltpu.VMEM(acc_shape, f32)]
    return _launch(body, name=name, grid=grid, in_specs=specs, out_specs=o_spec, out_shape=SDS(o_shape, o_dtype),
                   args=ins, scratch_shapes=scratch, sem=sem, comm=comm, vmem=vmem)


def _mm_nn_full(name, a, b, o_dtype, res=None, tm=1024, comm=None):
    T, K = a.shape
    N = b.shape[1]
    return _mm(name, a, b, grid=(T // tm,), a_spec=pl.BlockSpec((tm, K), lambda i: (i, 0)),
               b_spec=pl.BlockSpec((K, N), lambda i: (0, 0)), o_spec=pl.BlockSpec((tm, N), lambda i: (i, 0)),
               o_shape=(T, N), o_dtype=o_dtype, dims=NN, res=res,
               res_spec=None if res is None else pl.BlockSpec((tm, N), lambda i: (i, 0)), sem=("parallel",), comm=comm)


def _mm_nt_full(name, a, b, o_dtype, tm=1024):
    T, N = a.shape
    K = b.shape[0]
    return _mm(name, a, b, grid=(T // tm,), a_spec=pl.BlockSpec((tm, N), lambda i: (i, 0)),
               b_spec=pl.BlockSpec((K, N), lambda i: (0, 0)), o_spec=pl.BlockSpec((tm, K), lambda i: (i, 0)),
               o_shape=(T, K), o_dtype=o_dtype, dims=NT, sem=("parallel",))


def _mm_tn_tokens(name, a, b, o_dtype, tk=1024):
    T, K = a.shape
    N = b.shape[1]
    return _mm(name, a, b, grid=(T // tk,), a_spec=pl.BlockSpec((tk, K), lambda k: (k, 0)),
               b_spec=pl.BlockSpec((tk, N), lambda k: (k, 0)), o_spec=pl.BlockSpec((K, N), lambda k: (0, 0)),
               o_shape=(K, N), o_dtype=o_dtype, dims=TN, acc_shape=(K, N), k_axis=0, sem=("arbitrary",))


AO_S = D_MODEL // N_CHIPS


def _attn_out_cat(w_ref):
    return jnp.concatenate([w_ref[s] for s in range(N_CHIPS)], axis=-1)


def _attn_out_fwd(mix, w4):
    T = mix.shape[0]
    tm = 1024

    def body(a_ref, w_ref, o_ref):
        o_ref[...] = jnp.dot(a_ref[...], _attn_out_cat(w_ref), preferred_element_type=f32).astype(bf16)

    return pl.pallas_call(body, grid=(T // tm,),
                          in_specs=[pl.BlockSpec((tm, GW), lambda i: (i, 0)), pl.BlockSpec((N_CHIPS, GW, AO_S), lambda i: (0, 0, 0))],
                          out_specs=pl.BlockSpec((tm, D_MODEL), lambda i: (i, 0)), out_shape=SDS((T, D_MODEL), bf16), name="attn_out",
                          compiler_params=_cp(("parallel",)))(mix, w4)


def _attn_out_bwd_x(dya, w4, comm=None):
    T = dya.shape[0]
    tm = 1024

    def body(d_ref, w_ref, o_ref):
        o_ref[...] = lax.dot_general(d_ref[...], _attn_out_cat(w_ref), NT, preferred_element_type=f32).astype(bf16)

    return _launch(body, name="d_mix", grid=(T // tm,),
                   in_specs=[pl.BlockSpec((tm, D_MODEL), lambda i: (i, 0)), pl.BlockSpec((N_CHIPS, GW, AO_S), lambda i: (0, 0, 0))],
                   out_specs=pl.BlockSpec((tm, GW), lambda i: (i, 0)), out_shape=SDS((T, GW), bf16), args=(dya, w4),
                   sem=("parallel",), comm=comm)


def _attn_out_bwd_w(mix, dya):
    T = mix.shape[0]
    tk = 1024
    nk = T // tk

    def body(a_ref, d_ref, o_ref, acc):
        k = pl.program_id(0)
        p = lax.dot_general(a_ref[...], d_ref[...], TN, preferred_element_type=f32)

        @pl.when(k == 0)
        def _():
            acc[...] = p

        @pl.when(k > 0)
        def _():
            acc[...] += p

        @pl.when(k == nk - 1)
        def _():
            for s in range(N_CHIPS):
                o_ref[s] = acc[:, s * AO_S:(s + 1) * AO_S].astype(bf16)

    return pl.pallas_call(body, grid=(nk,),
                          in_specs=[pl.BlockSpec((tk, GW), lambda k: (k, 0)), pl.BlockSpec((tk, D_MODEL), lambda k: (k, 0))],
                          out_specs=pl.BlockSpec((N_CHIPS, GW, AO_S), lambda k: (0, 0, 0)), out_shape=SDS((N_CHIPS, GW, AO_S), bf16),
                          scratch_shapes=[pltpu.VMEM((GW, D_MODEL), f32)], name="d_w_attn_out",
                          compiler_params=_cp(("arbitrary",)))(mix, dya)


def _ffn_up(h2, wg, wu):
    T = h2.shape[0]
    tm = 1024

    def body(h_ref, wg_ref, wu_ref, a_ref, b_ref, f_ref):
        hv = h_ref[...]
        av = lax.dot_general(hv, wg_ref[...], NT, preferred_element_type=f32)
        bv = lax.dot_general(hv, wu_ref[...], NT, preferred_element_type=f32)
        a_ref[...] = av.astype(bf16)
        b_ref[...] = bv.astype(bf16)
        f_ref[...] = (av * _sigmoid(av) * bv).astype(bf16)

    wspec = pl.BlockSpec((None, FF_S, D_MODEL), lambda s, m: (s, 0, 0))
    ospec = pl.BlockSpec((None, tm, FF_S), lambda s, m: (s, m, 0))
    osd = SDS((N_CHIPS, T, FF_S), bf16)
    return pl.pallas_call(body, grid=(N_CHIPS, T // tm),
                          in_specs=[pl.BlockSpec((tm, D_MODEL), lambda s, m: (m, 0)), wspec, wspec],
                          out_specs=[ospec, ospec, ospec], out_shape=(osd, osd, osd), name="ffn_up",
                          compiler_params=_cp(("parallel", "parallel")))(h2, wg, wu)


def _ffn_down_bwd(dx2, wd, a, b):
    T = dx2.shape[0]
    tm = 1024

    def body(d_ref, w_ref, a_ref, b_ref, da_ref, db_ref):
        df = lax.dot_general(d_ref[...].astype(bf16), w_ref[...], NT, preferred_element_type=f32)
        av = a_ref[...].astype(f32)
        sg = _sigmoid(av)
        da_ref[...] = (df * b_ref[...].astype(f32) * (sg * (1.0 + av * (1.0 - sg)))).astype(bf16)
        db_ref[...] = (df * av * sg).astype(bf16)

    aspec = pl.BlockSpec((None, tm, FF_S), lambda m, s: (s, m, 0))
    osd = SDS((N_CHIPS, T, FF_S), bf16)
    return pl.pallas_call(body, grid=(T // tm, N_CHIPS),
                          in_specs=[pl.BlockSpec((tm, D_MODEL), lambda m, s: (m, 0)),
                                    pl.BlockSpec((None, FF_S, D_MODEL), lambda m, s: (s, 0, 0)), aspec, aspec],
                          out_specs=[aspec, aspec], out_shape=(osd, osd), name="ffn_down_bwd",
                          compiler_params=_cp(("parallel", "parallel")))(dx2, wd, a, b)


def _merge_bwd(dm, proj, gate_b, ya, yc):
    T = proj.shape[0]
    tm = 512

    def body(dm_ref, ga_ref, gc_ref, ba_ref, bc_ref, ya_ref, yc_ref, dya_ref, dyc_ref, dga_ref, dgc_ref, dba_ref, dbc_ref):
        dmv = dm_ref[...].astype(f32)
        sa = _sigmoid(ga_ref[...].astype(f32) + ba_ref[...])
        sc = _sigmoid(gc_ref[...].astype(f32) + bc_ref[...])
        dya_ref[...] = (dmv * sa).astype(bf16)
        dyc_ref[...] = (dmv * sc).astype(bf16)
        dga = dmv * ya_ref[...].astype(f32) * sa * (1.0 - sa)
        dgc = dmv * yc_ref[...].astype(f32) * sc * (1.0 - sc)
        dga_ref[...] = dga.astype(bf16)
        dgc_ref[...] = dgc.astype(bf16)
        pa = jnp.sum(dga, axis=0, keepdims=True)
        pc = jnp.sum(dgc, axis=0, keepdims=True)

        @pl.when(pl.program_id(1) == 0)
        def _():
            dba_ref[...] = pa
            dbc_ref[...] = pc

        @pl.when(pl.program_id(1) > 0)
        def _():
            dba_ref[...] += pa
            dbc_ref[...] += pc

    blk = pl.BlockSpec((tm, CB), lambda j, i: (i, j))
    vec = pl.BlockSpec((1, CB), lambda j, i: (0, j))
    big = SDS((T, D_MODEL), bf16)
    small = SDS((1, D_MODEL), f32)
    return pl.pallas_call(
        body, grid=(2, T // tm),
        in_specs=[blk, pl.BlockSpec((tm, CB), lambda j, i: (i, GA_CB + j)), pl.BlockSpec((tm, CB), lambda j, i: (i, GC_CB + j)),
                  vec, pl.BlockSpec((1, CB), lambda j, i: (0, 2 + j)), blk, blk],
        out_specs=[blk, blk, blk, blk, vec, vec], out_shape=(big, big, big, big, small, small), name="merge_bwd",
        compiler_params=_cp(("parallel", "arbitrary")))(dm, proj, proj, gate_b, gate_b, ya, yc)


CONV_TS = 512
CONV_HALO = 32
CONV_RC = 64
CONV_WIN = CONV_TS + CONV_HALO
SUBLANES = 8


def _fill_shifted(win, sh):
    for b in range(1, SUBLANES):
        sh[b - 1] = win[pl.ds(b, CONV_WIN - SUBLANES), :]


def _rows_at(win, sh, row):
    a, b = divmod(row, SUBLANES)
    if b == 0:
        return win[pl.ds(row, CONV_RC), :]
    return sh[b - 1, pl.ds(a * SUBLANES, CONV_RC), :]


def _glu_conv_fwd(proj3, w, bias, comm=None):
    B = proj3.shape[0]
    nt = SEQ // CONV_TS
    hb = CONV_TS // CONV_HALO

    def body(ua_ref, ub_ref, ha_ref, hb_ref, w_ref, b_ref, o_ref, win, sh):
        i = pl.program_id(2)
        c0 = ua_ref[...].astype(f32) * _sigmoid(ub_ref[...].astype(f32))
        halo = ha_ref[...].astype(f32) * _sigmoid(hb_ref[...].astype(f32))
        win[0:CONV_HALO, :] = jnp.where(i > 0, halo, 0.0)
        win[CONV_HALO:, :] = c0
        _fill_shifted(win, sh)
        for r0 in range(0, CONV_TS, CONV_RC):
            acc = jnp.zeros((CONV_RC, CB), f32) + b_ref[...]
            for k in range(CONV_K):
                acc = acc + _rows_at(win, sh, r0 + CONV_HALO - (CONV_K - 1) + k) * w_ref[k:k + 1, :]
            o_ref[r0:r0 + CONV_RC, :] = acc

    def cur(cb):
        return pl.BlockSpec((None, CONV_TS, CB), lambda b, j, i: (b, i, cb + j))

    def prev(cb):
        return pl.BlockSpec((None, CONV_HALO, CB), lambda b, j, i: (b, jnp.maximum(i * hb - 1, 0), cb + j))

    return _launch(
        body, name="glu_conv_fwd", grid=(B, 2, nt),
        in_specs=[cur(UA_CB), cur(UB_CB), prev(UA_CB), prev(UB_CB),
                  pl.BlockSpec((CONV_K, CB), lambda b, j, i: (0, j)), pl.BlockSpec((1, CB), lambda b, j, i: (0, j))],
        out_specs=pl.BlockSpec((None, CONV_TS, CB), lambda b, j, i: (b, i, j)),
        out_shape=SDS((B, SEQ, D_MODEL), f32), args=(proj3, proj3, proj3, proj3, w, bias),
        scratch_shapes=[pltpu.VMEM((CONV_WIN, CB), f32), pltpu.VMEM((SUBLANES - 1, CONV_WIN - SUBLANES, CB), f32)],
        sem=("parallel", "parallel", "parallel"), comm=comm)


def _glu_conv_bwd(dc1, proj3, w, comm=None):
    B = proj3.shape[0]
    nt = SEQ // CONV_TS
    hb = CONV_TS // CONV_HALO

    def body(d_ref, dn_ref, ua_ref, ub_ref, ha_ref, hb_ref, w_ref, dua_ref, dub_ref, dw_ref, db_ref, winc, wind, accw, shc, shd):
        b = pl.program_id(1)
        i = pl.program_id(2)
        first = jnp.logical_and(b == 0, i == 0)
        last = jnp.logical_and(b == B - 1, i == nt - 1)

        @pl.when(first)
        def _():
            accw[...] = jnp.zeros_like(accw)
            db_ref[...] = jnp.zeros_like(db_ref)

        halo = ha_ref[...].astype(f32) * _sigmoid(hb_ref[...].astype(f32))
        winc[0:CONV_HALO, :] = jnp.where(i > 0, halo, 0.0)
        winc[CONV_HALO:, :] = ua_ref[...].astype(f32) * _sigmoid(ub_ref[...].astype(f32))
        wind[0:CONV_TS, :] = d_ref[...]
        wind[CONV_TS:, :] = jnp.where(i < nt - 1, dn_ref[...], 0.0)
        db_ref[...] += jnp.sum(d_ref[...], axis=0, keepdims=True)
        _fill_shifted(winc, shc)
        _fill_shifted(wind, shd)
        for r0 in range(0, CONV_TS, CONV_RC):
            dc0 = jnp.zeros((CONV_RC, CB), f32)
            for k in range(CONV_K):
                dc0 = dc0 + _rows_at(wind, shd, r0 + (CONV_K - 1) - k) * w_ref[k:k + 1, :]
            uav = ua_ref[r0:r0 + CONV_RC, :].astype(f32)
            sg = _sigmoid(ub_ref[r0:r0 + CONV_RC, :].astype(f32))
            dua_ref[r0:r0 + CONV_RC, :] = (dc0 * sg).astype(bf16)
            dub_ref[r0:r0 + CONV_RC, :] = (dc0 * uav * sg * (1.0 - sg)).astype(bf16)
            dv = wind[r0:r0 + CONV_RC, :]
            for k in range(CONV_K):
                prod = dv * _rows_at(winc, shc, r0 + CONV_HALO - (CONV_K - 1) + k)
                accw[k] += jnp.sum(prod.reshape(CONV_RC // 8, 8, CB), axis=0)

        @pl.when(last)
        def _():
            for k in range(CONV_K):
                dw_ref[k:k + 1, :] = jnp.sum(accw[k], axis=0, keepdims=True)
            dw_ref[CONV_K:, :] = jnp.zeros((CONV_HALO - CONV_K, CB), f32)

    def cur(cb):
        return pl.BlockSpec((None, CONV_TS, CB), lambda j, b, i: (b, i, cb + j))

    def prev(cb):
        return pl.BlockSpec((None, CONV_HALO, CB), lambda j, b, i: (b, jnp.maximum(i * hb - 1, 0), cb + j))

    nxt = pl.BlockSpec((None, CONV_HALO, CB), lambda j, b, i: (b, jnp.minimum((i + 1) * hb, SEQ // CONV_HALO - 1), j))
    big = SDS((B, SEQ, D_MODEL), bf16)
    return _launch(
        body, name="glu_conv_bwd", grid=(2, B, nt),
        in_specs=[cur(0), nxt, cur(UA_CB), cur(UB_CB), prev(UA_CB), prev(UB_CB), pl.BlockSpec((CONV_K, CB), lambda j, b, i: (0, j))],
        out_specs=[cur(0), cur(0), pl.BlockSpec((CONV_HALO, CB), lambda j, b, i: (0, j)), pl.BlockSpec((1, CB), lambda j, b, i: (0, j))],
        out_shape=(big, big, SDS((CONV_HALO, D_MODEL), f32), SDS((1, D_MODEL), f32)),
        args=(dc1, dc1, proj3, proj3, proj3, proj3, w),
        scratch_shapes=[pltpu.VMEM((CONV_WIN, CB), f32), pltpu.VMEM((CONV_WIN, CB), f32), pltpu.VMEM((CONV_K, SUBLANES, CB), f32),
                        pltpu.VMEM((SUBLANES - 1, CONV_WIN - SUBLANES, CB), f32),
                        pltpu.VMEM((SUBLANES - 1, CONV_WIN - SUBLANES, CB), f32)],
        sem=("parallel", "arbitrary", "arbitrary"), comm=comm)


def _band(first, dil):
    kw = Q_BLOCK if first else 2 * Q_BLOCK
    qi = lax.broadcasted_iota(jnp.int32, (Q_BLOCK, kw), 0)
    kj = lax.broadcasted_iota(jnp.int32, (Q_BLOCK, kw), 1)
    rel = qi - kj + (0 if first else Q_BLOCK)
    valid = jnp.logical_and(rel >= 0, rel <= Q_BLOCK)
    return valid, rel.astype(f32) * float(dil)


def _bias(first, dil, slope):
    valid, dist = _band(first, dil)
    return jnp.where(valid, -slope * dist, NEG)


def _scores(q, k, bias):
    return lax.dot_general(q, k, NT, preferred_element_type=f32) * (HEAD_DIM ** -0.5) + bias


def _pair_cols(hp):
    return slice(hp * LANES, (hp + 1) * LANES)


def _half(x2, e):
    lane = lax.broadcasted_iota(jnp.int32, (1, LANES), 1)
    keep = (lane < HEAD_DIM) if e == 0 else (lane >= HEAD_DIM)
    return jnp.where(keep, x2, jnp.zeros_like(x2))


def _attn_fwd(g, q4, k4, v4, qcb, kcb, vcb, comm=None):
    _, dil = GROUPS[g]
    B, r, L, _ = q4.shape
    nb = L // Q_BLOCK
    slopes = [float(s) for s in _SLOPES[g]]

    def body(q_ref, k_ref, v_ref, o_ref, lse_ref, bias, s_scr, p_scr):
        lane = lax.broadcasted_iota(jnp.int32, (Q_BLOCK, LANES), 1)
        if nb > 1:
            for h in range(HEADS):
                bias[h] = _bias(False, dil, slopes[h])

        def block(n, first):
            q0 = 0 if first else pl.multiple_of(n * Q_BLOCK, Q_BLOCK)
            k0 = 0 if first else pl.multiple_of((n - 1) * Q_BLOCK, Q_BLOCK)
            kw = Q_BLOCK if first else 2 * Q_BLOCK
            for hp in range(HEADS // 2):
                q2 = q_ref[pl.ds(q0, Q_BLOCK), _pair_cols(hp)]
                k2 = k_ref[pl.ds(k0, kw), _pair_cols(hp)]
                for e in range(2):
                    h = 2 * hp + e
                    b_h = _bias(True, dil, slopes[h]) if first else bias[h]
                    s_scr[h, :, :kw] = _scores(_half(q2, e), k2, b_h)
            st = jnp.zeros((Q_BLOCK, LANES), f32)
            dens = jnp.ones((Q_BLOCK, LANES), f32)
            for h in range(HEADS):
                s = s_scr[h, :, :kw]
                m = jnp.max(s, axis=-1, keepdims=True)
                p = jnp.exp(s - m)
                den = jnp.sum(p, axis=-1, keepdims=True)
                p_scr[h, :, :kw] = p.astype(bf16)
                st = jnp.where(lane == h, m + jnp.log(den), st)
                dens = jnp.where(lane == h, den, dens)
            lse_ref[pl.ds(q0, Q_BLOCK), :] = st
            inv = 1.0 / dens
            for hp in range(HEADS // 2):
                v2 = v_ref[pl.ds(k0, kw), _pair_cols(hp)]
                o2 = sum(jnp.dot(p_scr[2 * hp + e, :, :kw], _half(v2, e), preferred_element_type=f32) * inv[:, 2 * hp + e:2 * hp + e + 1]
                         for e in range(2))
                o_ref[pl.ds(q0, Q_BLOCK), _pair_cols(hp)] = o2.astype(bf16)

        block(0, True)
        if nb > 1:
            def step(n, carry):
                block(n, False)
                return carry
            lax.fori_loop(1, nb, step, 0)

    def spec(cb):
        return pl.BlockSpec((None, None, L, GW), lambda b, c: (b, c, 0, cb))

    return _launch(
        body, name=f"attn_fwd_g{g}", grid=(B, r), in_specs=[spec(qcb), spec(kcb), spec(vcb)],
        out_specs=[spec(0), pl.BlockSpec((None, None, L, LANES), lambda b, c: (b, c, 0, 0))],
        out_shape=(SDS((B, r, L, GW), bf16), SDS((B, r, L, LANES), f32)), args=(q4, k4, v4),
        scratch_shapes=[pltpu.VMEM((HEADS, Q_BLOCK, 2 * Q_BLOCK), f32), pltpu.VMEM((HEADS, Q_BLOCK, 2 * Q_BLOCK), f32),
                        pltpu.VMEM((HEADS, Q_BLOCK, 2 * Q_BLOCK), bf16)],
        sem=("parallel", "parallel"), comm=comm)


def _attn_bwd(g, q4, k4, v4, qcb, kcb, vcb, do4, lse4, dl4, comm=None):
    _, dil = GROUPS[g]
    B, r, L, _ = q4.shape
    nb = L // Q_BLOCK
    slopes = [float(s) for s in _SLOPES[g]]
    scale = HEAD_DIM ** -0.5

    def body(q_ref, k_ref, v_ref, do_ref, lse_ref, dl_ref, dq_ref, dk_ref, dv_ref, dk_acc, dv_acc, bias, s_scr, dp_scr, p_scr, ds_scr):
        dk_acc[...] = jnp.zeros_like(dk_acc)
        dv_acc[...] = jnp.zeros_like(dv_acc)
        if nb > 1:
            for h in range(HEADS):
                bias[h] = _bias(False, dil, slopes[h])

        def block(n, first):
            q0 = 0 if first else pl.multiple_of(n * Q_BLOCK, Q_BLOCK)
            k0 = 0 if first else pl.multiple_of((n - 1) * Q_BLOCK, Q_BLOCK)
            kw = Q_BLOCK if first else 2 * Q_BLOCK
            for hp in range(HEADS // 2):
                q2 = q_ref[pl.ds(q0, Q_BLOCK), _pair_cols(hp)]
                k2 = k_ref[pl.ds(k0, kw), _pair_cols(hp)]
                v2 = v_ref[pl.ds(k0, kw), _pair_cols(hp)]
                do2 = do_ref[pl.ds(q0, Q_BLOCK), _pair_cols(hp)]
                for e in range(2):
                    h = 2 * hp + e
                    b_h = _bias(True, dil, slopes[h]) if first else bias[h]
                    s_scr[h, :, :kw] = _scores(_half(q2, e), k2, b_h)
                    dp_scr[h, :, :kw] = lax.dot_general(_half(do2, e), v2, NT, preferred_element_type=f32)
            for h in range(HEADS):
                p = jnp.exp(s_scr[h, :, :kw] - lse_ref[pl.ds(q0, Q_BLOCK), h:h + 1])
                p_scr[h, :, :kw] = p.astype(bf16)
                ds_scr[h, :, :kw] = (p * (dp_scr[h, :, :kw] - dl_ref[pl.ds(q0, Q_BLOCK), h:h + 1])).astype(bf16)
            for hp in range(HEADS // 2):
                cols = _pair_cols(hp)
                q2 = q_ref[pl.ds(q0, Q_BLOCK), cols]
                k2 = k_ref[pl.ds(k0, kw), cols]
                do2 = do_ref[pl.ds(q0, Q_BLOCK), cols]
                ds = [ds_scr[2 * hp + e, :, :kw] for e in range(2)]
                dq2 = sum(jnp.dot(ds[e], _half(k2, e), preferred_element_type=f32) for e in range(2))
                dq_ref[pl.ds(q0, Q_BLOCK), cols] = (dq2 * scale).astype(bf16)
                dk2 = sum(lax.dot_general(ds[e], _half(q2, e), TN, preferred_element_type=f32) for e in range(2))
                dk_acc[pl.ds(k0, kw), cols] += dk2 * scale
                dv2 = sum(lax.dot_general(p_scr[2 * hp + e, :, :kw], _half(do2, e), TN, preferred_element_type=f32) for e in range(2))
                dv_acc[pl.ds(k0, kw), cols] += dv2

        block(0, True)
        if nb > 1:
            def step(n, carry):
                block(n, False)
                return carry
            lax.fori_loop(1, nb, step, 0)
        dk_ref[...] = dk_acc[...].astype(bf16)
        dv_ref[...] = dv_acc[...].astype(bf16)

    def spec(cb):
        return pl.BlockSpec((None, None, L, GW), lambda b, c: (b, c, 0, cb))

    st = pl.BlockSpec((None, None, L, LANES), lambda b, c: (b, c, 0, 0))
    osd = SDS((B, r, L, GW), bf16)
    return _launch(
        body, name=f"attn_bwd_g{g}", grid=(B, r), in_specs=[spec(qcb), spec(kcb), spec(vcb), spec(0), st, st],
        out_specs=[spec(0), spec(0), spec(0)], out_shape=(osd, osd, osd), args=(q4, k4, v4, do4, lse4, dl4),
        scratch_shapes=[pltpu.VMEM((L, GW), f32), pltpu.VMEM((L, GW), f32)]
        + [pltpu.VMEM((HEADS, Q_BLOCK, 2 * Q_BLOCK), f32)] * 3 + [pltpu.VMEM((HEADS, Q_BLOCK, 2 * Q_BLOCK), bf16)] * 2,
        sem=("parallel", "parallel"), comm=comm)


RT = 512
RCH = GW // LANES
DILS = tuple(d for _, d in GROUPS[1:])


def _res_spec(r, width):
    return pl.BlockSpec((None, r, RT // r, width), lambda b, i, *_: (b, 0, i, 0))


def _tok_spec(width, cb=0):
    return pl.BlockSpec((None, RT, width), lambda b, i, *_: (b, i, cb))


def _to_residues(res_ref, scr, r, width):
    for c in range(r):
        for jj in range(width // LANES):
            res_ref[c, :, jj * LANES:(jj + 1) * LANES] = scr[jj, pl.ds(c, RT // r, stride=r), :].astype(res_ref.dtype)


def _from_residues(scr, res_ref, r, width):
    for c in range(r):
        for jj in range(width // LANES):
            scr[jj, pl.ds(c, RT // r, stride=r), :] = res_ref[c, :, jj * LANES:(jj + 1) * LANES].astype(f32)


def _qkv_to_residues(proj3, g):
    r = GROUPS[g][1]
    B = proj3.shape[0]

    def body(q_ref, k_ref, v_ref, o_ref, scr):
        for p, x_ref in enumerate((q_ref, k_ref, v_ref)):
            x = x_ref[...].astype(f32)
            for jj in range(RCH):
                scr[p * RCH + jj] = x[:, jj * LANES:(jj + 1) * LANES]
        _to_residues(o_ref, scr, r, ATTN_W)

    return pl.pallas_call(
        body, grid=(B, SEQ // RT), in_specs=[_tok_spec(GW, 3 * p + g) for p in range(3)], out_specs=_res_spec(r, ATTN_W),
        out_shape=SDS((B, r, SEQ // r, ATTN_W), bf16), scratch_shapes=[pltpu.VMEM((3 * RCH, RT, LANES), f32)],
        name=f"qkv_to_residues_g{g}", compiler_params=_cp(("parallel", "parallel")))(proj3, proj3, proj3)


def _attn_mix(o0, l0, o_res, l_res):
    B = o0.shape[0]

    def body(o0_ref, l0_ref, o1_ref, o2_ref, l1_ref, l2_ref, y_ref, lt_ref, lt1_ref, lt2_ref, so, sl):
        for gi, (o_ref, l_ref, r) in enumerate(((o1_ref, l1_ref, DILS[0]), (o2_ref, l2_ref, DILS[1]))):
            _from_residues(so.at[gi], o_ref, r, GW)
            _from_residues(sl.at[gi:gi + 1], l_ref, r, LANES)
        ls = [l0_ref[...], sl[0], sl[1]]
        m = functools.reduce(jnp.maximum, ls)
        ws = [jnp.exp(l - m) for l in ls]
        den = ws[0] + ws[1] + ws[2]
        alphas = [w / den for w in ws]
        lt = m + jnp.log(den)
        lt_ref[...] = lt
        sl[2] = lt
        _to_residues(lt1_ref, sl.at[2:3], DILS[0], LANES)
        _to_residues(lt2_ref, sl.at[2:3], DILS[1], LANES)
        for h in range(HEADS):
            cols = slice(h * HEAD_DIM, (h + 1) * HEAD_DIM)
            jj, lo = divmod(h * HEAD_DIM, LANES)
            acc = alphas[0][:, h:h + 1] * o0_ref[:, cols].astype(f32)
            for gi in range(2):
                acc = acc + alphas[gi + 1][:, h:h + 1] * so[gi, jj, :, lo:lo + HEAD_DIM]
            y_ref[:, cols] = acc.astype(bf16)

    in_specs = [_tok_spec(GW), _tok_spec(LANES), _res_spec(DILS[0], GW), _res_spec(DILS[1], GW), _res_spec(DILS[0], LANES), _res_spec(DILS[1], LANES)]
    out_specs = [_tok_spec(GW), _tok_spec(LANES), _res_spec(DILS[0], LANES), _res_spec(DILS[1], LANES)]
    return pl.pallas_call(
        body, grid=(B, SEQ // RT), in_specs=in_specs, out_specs=out_specs,
        out_shape=(SDS((B, SEQ, GW), bf16), SDS((B, SEQ, LANES), f32)) + tuple(SDS((B, r, SEQ // r, LANES), f32) for r in DILS),
        scratch_shapes=[pltpu.VMEM((2, RCH, RT, LANES), f32), pltpu.VMEM((3, RT, LANES), f32)],
        name="attn_mix", compiler_params=_cp(("parallel", "parallel")))(o0, l0, *o_res, *l_res)


def _attn_delta(dmix, mix):
    B = dmix.shape[0]

    def body(d_ref, y_ref, dl_ref, dl1_ref, dl2_ref, dm1_ref, dm2_ref, sx, sd):
        lane = lax.broadcasted_iota(jnp.int32, (RT, LANES), 1)
        acc = jnp.zeros((RT, LANES), f32)
        dv = d_ref[...].astype(f32)
        for jj in range(RCH):
            sx[jj] = dv[:, jj * LANES:(jj + 1) * LANES]
        for h in range(HEADS):
            cols = slice(h * HEAD_DIM, (h + 1) * HEAD_DIM)
            dl = jnp.sum(dv[:, cols] * y_ref[:, cols].astype(f32), axis=-1, keepdims=True)
            acc = jnp.where(lane == h, dl, acc)
        dl_ref[...] = acc
        sd[0] = acc
        _to_residues(dl1_ref, sd, DILS[0], LANES)
        _to_residues(dl2_ref, sd, DILS[1], LANES)
        _to_residues(dm1_ref, sx, DILS[0], GW)
        _to_residues(dm2_ref, sx, DILS[1], GW)

    return pl.pallas_call(
        body, grid=(B, SEQ // RT), in_specs=[_tok_spec(GW), _tok_spec(GW)],
        out_specs=[_tok_spec(LANES), _res_spec(DILS[0], LANES), _res_spec(DILS[1], LANES), _res_spec(DILS[0], GW), _res_spec(DILS[1], GW)],
        out_shape=(SDS((B, SEQ, LANES), f32),) + tuple(SDS((B, r, SEQ // r, LANES), f32) for r in DILS)
        + tuple(SDS((B, r, SEQ // r, GW), bf16) for r in DILS),
        scratch_shapes=[pltpu.VMEM((RCH, RT, LANES), f32), pltpu.VMEM((1, RT, LANES), f32)],
        name="attn_delta", compiler_params=_cp(("parallel", "parallel")))(dmix, mix)


N_CB = IN_W // CB


def _assemble_dproj(dqkv, dua, dub, dga, dgc):
    B = dua.shape[0]
    ng = len(GROUPS)
    flat = [dqkv[g][p] for p in range(3) for g in range(ng)]
    wide = [dua, dub, dga, dgc]

    def body(*refs):
        srcs, wides, o_ref, scr = refs[:3 * ng], refs[3 * ng:3 * ng + 4], refs[3 * ng + 4], refs[3 * ng + 5]
        for jv in range(3 * ng):
            g = jv % ng
            if g == 0:
                o_ref[:, jv * GW:(jv + 1) * GW] = srcs[jv][...]
            else:
                _from_residues(scr, srcs[jv], GROUPS[g][1], GW)
                for jj in range(RCH):
                    o_ref[:, jv * GW + jj * LANES:jv * GW + (jj + 1) * LANES] = scr[jj].astype(bf16)
        for wv in range(4):
            lo = 3 * ATTN_W + wv * D_MODEL
            o_ref[:, lo:lo + D_MODEL] = wides[wv][...]

    in_specs = [_tok_spec(GW) if (jv % ng) == 0 else _res_spec(GROUPS[jv % ng][1], GW) for jv in range(3 * ng)]
    in_specs += [_tok_spec(D_MODEL)] * 4
    return pl.pallas_call(
        body, grid=(B, SEQ // RT), in_specs=in_specs, out_specs=_tok_spec(IN_W),
        out_shape=SDS((B, SEQ, IN_W), bf16), scratch_shapes=[pltpu.VMEM((RCH, RT, LANES), f32)],
        name="assemble_dproj", compiler_params=_cp(("parallel", "parallel")))(*flat, *wide)


def _same_shape_groups(names, arrays):
    groups = {}
    for n, a in zip(names, arrays):
        groups.setdefault(a.shape, ([], []))
        groups[a.shape][0].append(n)
        groups[a.shape][1].append(a)
    return list(groups.values())


def _add_pairs(parts, gots, core, name):
    k = len(parts)
    n, h, C = gots[0].shape

    def body(c_ref, *refs):
        for i in range(k):
            refs[2 * k + i][...] = (refs[i][...].astype(f32) + refs[k + i][...].astype(f32)).astype(bf16)

    blk = pl.BlockSpec((None, h, C), lambda s, c_ref: (s, 0, 0))
    own = pl.BlockSpec((None, h, C), lambda s, c_ref: (s, c_ref[0], 0))
    spec = pltpu.PrefetchScalarGridSpec(num_scalar_prefetch=1, grid=(n,), in_specs=[own] * k + [blk] * k, out_specs=[blk] * k)
    return pl.pallas_call(body, grid_spec=spec, out_shape=tuple(SDS((n, h, C), bf16) for _ in range(k)), name=name,
                          compiler_params=_cp(("parallel",)))(core, *parts, *gots)


def _sum4(sums, gots, chip, core, name):
    k = len(sums)
    _, h, C = sums[0].shape

    def body(s_ref, c_ref, *refs):
        for i in range(k):
            q_ref = refs[k + i]
            t = refs[i][...].astype(f32) + q_ref[0].astype(f32)
            t = t + q_ref[1].astype(f32)
            refs[2 * k + i][...] = t + q_ref[2].astype(f32)

    spec = pltpu.PrefetchScalarGridSpec(
        num_scalar_prefetch=2, grid=(1,),
        in_specs=[pl.BlockSpec((None, h, C), lambda i, s_ref, c_ref: (s_ref[0], 0, 0))] * k
        + [pl.BlockSpec((N_CHIPS - 1, h, C), lambda i, s_ref, c_ref: (0, 0, 0))] * k,
        out_specs=[pl.BlockSpec((h, C), lambda i, s_ref, c_ref: (c_ref[0], 0))] * k)
    return pl.pallas_call(body, grid_spec=spec, out_shape=tuple(SDS((2 * h, C), f32) for _ in range(k)), name=name,
                          compiler_params=_cp(("arbitrary",)))(chip, core, *sums, *gots)


def _adamw(ws, gs, ms, vs, name):
    k = len(ws)
    R, C = ws[0].shape
    rt = R
    for cand in (512, 256, 128, 64, 32, 16, 8):
        if R % cand == 0 and cand * C * 4 <= 2 ** 21:
            rt = cand
            break
    c1 = 1.0 / (1.0 - B1 ** STEP)
    c2 = 1.0 / (1.0 - B2 ** STEP)

    def body(*refs):
        for i in range(k):
            w_ref, g_ref, m_ref, v_ref = (refs[j * k + i] for j in range(4))
            d_ref, nm_ref, nv_ref = (refs[(4 + j) * k + i] for j in range(3))
            gv = g_ref[...]
            nm = B1 * m_ref[...] + (1.0 - B1) * gv
            nv = B2 * v_ref[...] + (1.0 - B2) * (gv * gv)
            nm_ref[...] = nm
            nv_ref[...] = nv
            d_ref[...] = -LR * ((nm * c1) / (jnp.sqrt(nv * c2) + ADAM_EPS) + WD * w_ref[...])

    blk = pl.BlockSpec((rt, C), lambda i: (i, 0))
    sd = SDS((R, C), f32)
    res = pl.pallas_call(body, grid=(R // rt,), in_specs=[blk] * (4 * k), out_specs=[blk] * (3 * k), out_shape=(sd,) * (3 * k),
                         name=name, compiler_params=_cp(("parallel",)))(*ws, *gs, *ms, *vs)
    return res[:k], res[k:2 * k], res[2 * k:]


def _coords():
    return lax.axis_index("x"), lax.axis_index("y"), lax.axis_index("c")


def _other_chips(x, y):
    return [(1 - x, y), (x, 1 - y), (1 - x, 1 - y)]


def _allgather_weights(shards, slots):
    n = len(shards)
    n_slot = 12

    def body(*refs):
        ins, outs = refs[:n], refs[2 * n:3 * n]
        send, recv = refs[3 * n:]
        x, y, c = _coords()
        me, cx, cy, cd = 2 * x + y, 2 * (1 - x) + y, 2 * x + (1 - y), 2 * (1 - x) + (1 - y)
        dev_x, dev_y, dev_s = (1 - x, y, c), (x, 1 - y, c), (x, y, 1 - c)
        started = []

        def copy(w, slot, src, dst, dev):
            return pltpu.make_async_remote_copy(src_ref=src, dst_ref=dst, send_sem=send.at[w, slot], recv_sem=recv.at[w, slot],
                                                device_id=dev, device_id_type=MESH)

        def go(cp):
            cp.start()
            started.append(cp)

        for w in range(n):
            q = shards[w].shape[0] // 4
            rows = [pl.ds(c * 2 * q + k * q, q) for k in range(2)]
            theirs = [pl.ds((1 - c) * 2 * q + k * q, q) for k in range(2)]
            own = [(ins[w].at[r, :], outs[w].at[me, r, :]) for r in rows]
            go(copy(w, 0, *own[0], dev_x))
            go(copy(w, 2, *own[1], dev_y))
            go(copy(w, 1, *own[1], dev_x))
            go(copy(w, 3, *own[0], dev_y))
            arrivals = [(0, cx, 0, (4, dev_y)), (2, cy, 1, (5, dev_x)), (1, cx, 1, None), (3, cy, 0, None), (4, cd, 0, None), (5, cd, 1, None)]
            for k, (slot, chip, quarter, onward) in enumerate(arrivals):
                blk = outs[w].at[chip, rows[quarter], :]
                copy(w, slot, blk, blk, dev_s).wait_recv()
                if onward is not None:
                    go(copy(w, onward[0], blk, blk, onward[1]))
                go(copy(w, 6 + k, blk, blk, dev_s))
            for k, (slot, chip, quarter, onward) in enumerate(arrivals):
                blk = outs[w].at[chip, theirs[quarter], :]
                copy(w, 6 + k, blk, blk, dev_s).wait_recv()
        for cp in started:
            cp.wait_send()

    return pl.pallas_call(
        body, in_specs=[HBM] * (2 * n), out_specs=[HBM] * n,
        out_shape=tuple(SDS((N_CHIPS,) + s.shape, s.dtype) for s in shards),
        input_output_aliases={n + w: w for w in range(n)},
        scratch_shapes=[pltpu.SemaphoreType.DMA((n, n_slot))] * 2,
        name="allgather_weights", compiler_params=pltpu.CompilerParams(has_side_effects=True))(*shards, *slots)


def _x_pair_exchange(parts):
    n = len(parts)
    halves = [p.shape[1] // 2 for p in parts]

    def copies(ins, outs, sems):
        send, recv = sems
        x, y, c = _coords()
        return [pltpu.make_async_remote_copy(src_ref=ins[w].at[:, pl.ds((1 - c) * halves[w], halves[w]), :], dst_ref=outs[w],
                                             send_sem=send.at[w], recv_sem=recv.at[w], device_id=(x, y, 1 - c), device_id_type=MESH)
                for w in range(n)]

    def start(ins, outs, sems):
        for cp in copies(ins, outs, sems):
            cp.start()

    def wait(ins, outs, sems):
        for cp in copies(ins, outs, sems):
            cp.wait()

    return _Exchange(ins=list(parts), out_shape=[SDS((N_CHIPS, p.shape[1] // 2, p.shape[2]), p.dtype) for p in parts],
                     scratch=[pltpu.SemaphoreType.DMA((n,))] * 2, aliases={}, start=start, wait=wait)


def _x_chip_exchange(sums):
    n = len(sums)

    def copies(ins, outs, sems):
        send, recv = sems
        x, y, c = _coords()
        return [pltpu.make_async_remote_copy(src_ref=ins[w].at[2 * px + py], dst_ref=outs[w].at[j], send_sem=send.at[w, j],
                                             recv_sem=recv.at[w, j], device_id=(px, py, c), device_id_type=MESH)
                for w in range(n) for j, (px, py) in enumerate(_other_chips(x, y))]

    def start(ins, outs, sems):
        for cp in copies(ins, outs, sems):
            cp.start()

    def wait(ins, outs, sems):
        for cp in copies(ins, outs, sems):
            cp.wait()

    return _Exchange(ins=list(sums), out_shape=[SDS((N_CHIPS - 1,) + s.shape[1:], s.dtype) for s in sums],
                     scratch=[pltpu.SemaphoreType.DMA((n, 3)), pltpu.SemaphoreType.DMA((n, 3))], aliases={}, start=start, wait=wait)


def _x_gather_ici(shards, slots):
    n = len(shards)
    halves = [s.shape[0] // 2 for s in shards]

    def copies(ins, outs, sems):
        send, recv = sems
        x, y, c = _coords()
        me = 2 * x + y
        out = []
        for w in range(n):
            mine = pl.ds(c * halves[w], halves[w])
            for j, (px, py) in enumerate(_other_chips(x, y)):
                snd = pltpu.make_async_remote_copy(src_ref=ins[w].at[mine, :], dst_ref=outs[w].at[me, mine, :], send_sem=send.at[w, j],
                                                   recv_sem=recv.at[w, j], device_id=(px, py, c), device_id_type=MESH)
                got = outs[w].at[2 * px + py, mine, :]
                rcv = pltpu.make_async_remote_copy(src_ref=got, dst_ref=got, send_sem=send.at[w, j], recv_sem=recv.at[w, j],
                                                   device_id=(px, py, c), device_id_type=MESH)
                out.append((snd, rcv))
        return out

    def start(ins, outs, sems):
        for snd, _ in copies(ins, outs, sems):
            snd.start()

    def wait(ins, outs, sems):
        for snd, rcv in copies(ins, outs, sems):
            rcv.wait_recv()
            snd.wait_send()

    return _Exchange(ins=list(shards) + list(slots), out_shape=[SDS(s.shape, s.dtype) for s in slots],
                     scratch=[pltpu.SemaphoreType.DMA((n, 3)), pltpu.SemaphoreType.DMA((n, 3))],
                     aliases={n + w: w for w in range(n)}, start=start, wait=wait)


def _x_gather_fwd(bufs):
    n = len(bufs)
    halves = [b.shape[1] // 2 for b in bufs]

    def copies(ins, outs, sems):
        send, recv = sems
        x, y, c = _coords()
        out = []
        for w in range(n):
            for j, (px, py) in enumerate(_other_chips(x, y)):
                mine = outs[w].at[2 * px + py, pl.ds(c * halves[w], halves[w]), :]
                theirs = outs[w].at[2 * px + py, pl.ds((1 - c) * halves[w], halves[w]), :]
                snd = pltpu.make_async_remote_copy(src_ref=mine, dst_ref=mine, send_sem=send.at[w, j], recv_sem=recv.at[w, j],
                                                   device_id=(x, y, 1 - c), device_id_type=MESH)
                rcv = pltpu.make_async_remote_copy(src_ref=theirs, dst_ref=theirs, send_sem=send.at[w, j], recv_sem=recv.at[w, j],
                                                   device_id=(x, y, 1 - c), device_id_type=MESH)
                out.append((snd, rcv))
        return out

    def start(ins, outs, sems):
        for snd, _ in copies(ins, outs, sems):
            snd.start()

    def wait(ins, outs, sems):
        for snd, rcv in copies(ins, outs, sems):
            rcv.wait_recv()
            snd.wait_send()

    return _Exchange(ins=list(bufs), out_shape=[SDS(b.shape, b.dtype) for b in bufs],
                     scratch=[pltpu.SemaphoreType.DMA((n, 3)), pltpu.SemaphoreType.DMA((n, 3))],
                     aliases={w: w for w in range(n)}, start=start, wait=wait)


def _final_exchange(bufs, vec):
    n = len(bufs)
    R = vec.shape[0]
    nd = 8

    def body(*refs):
        v_ref = refs[n]
        outs = refs[n + 1:2 * n + 1]
        o_ref, sbuf, send, recv, ssend, srecv = refs[2 * n + 1:]
        x, y, c = _coords()
        me = 4 * x + 2 * y + c
        sbuf[me] = v_ref[...]
        small = []
        for k in range(1, nd):
            kx, ky, kc = (k >> 2) & 1, (k >> 1) & 1, k & 1
            tx = x + kx - 2 * x * kx
            ty = y + ky - 2 * y * ky
            tc = c + kc - 2 * c * kc
            cp = pltpu.make_async_remote_copy(src_ref=v_ref, dst_ref=sbuf.at[me], send_sem=ssend.at[k], recv_sem=srecv.at[k],
                                              device_id=(tx, ty, tc), device_id_type=MESH)
            cp.start()
            small.append((cp, 4 * tx + 2 * ty + tc))
        cps = []
        for w in range(n):
            h = bufs[w].shape[0] // 2
            rows = outs[w].at[pl.ds(c * h, h), :]
            rc = pltpu.make_async_remote_copy(src_ref=rows, dst_ref=rows, send_sem=send.at[w], recv_sem=recv.at[w],
                                              device_id=(x, y, 1 - c), device_id_type=MESH)
            rc.start()
            cps.append(rc)
        for k, (cp, src) in zip(range(1, nd), small):
            pltpu.make_async_remote_copy(src_ref=v_ref, dst_ref=sbuf.at[src], send_sem=ssend.at[k], recv_sem=srecv.at[k],
                                         device_id=(x, y, c), device_id_type=MESH).wait_recv()
        for cp, _ in small:
            cp.wait_send()
        acc = sbuf[0]
        for d in range(1, nd):
            acc = acc + sbuf[d]
        o_ref[...] = acc
        for w, rc in enumerate(cps):
            h = bufs[w].shape[0] // 2
            other = outs[w].at[pl.ds((1 - c) * h, h), :]
            pltpu.make_async_remote_copy(src_ref=other, dst_ref=other, send_sem=send.at[w], recv_sem=recv.at[w],
                                         device_id=(x, y, 1 - c), device_id_type=MESH).wait_recv()
            rc.wait_send()

    vm = pl.BlockSpec(memory_space=pltpu.VMEM)
    res = pl.pallas_call(body, in_specs=[HBM] * n + [vm], out_specs=[HBM] * n + [vm],
                         out_shape=tuple(SDS(a.shape, a.dtype) for a in bufs) + (SDS((R, LANES), f32),),
                         input_output_aliases={w: w for w in range(n)},
                         scratch_shapes=[pltpu.VMEM((nd, R, LANES), f32)] + [pltpu.SemaphoreType.DMA((n,))] * 2
                         + [pltpu.SemaphoreType.DMA((nd,))] * 2, name="final_exchange",
                         compiler_params=pltpu.CompilerParams(has_side_effects=True))(*bufs, vec)
    return res[:n], res[n]


def _rows128(a, rows):
    flat = a.reshape(-1)
    return jnp.pad(flat, (0, rows * LANES - flat.shape[0])).reshape(rows, LANES)


GATHER_1 = ("w_conv_out", "w_attn_out", "w_o", "w_ffn_gate", "w_ffn_down")
REDUCE_A = ("w_ffn_gate", "w_ffn_up", "w_ffn_down")
REDUCE_B = ("w_conv_out", "w_attn_out", "w_o")
REDUCE_C = ("w_in",)


def _step(x, target, norm1_g, gate_b, conv_w, conv_b, conv_ln_g, conv_ln_b, norm2_g, norm_f_g, w_in, shards, slots, chip1, core):
    B = x.shape[0]
    T = B * SEQ
    xf = x.reshape(T, D_MODEL)
    tf = target.reshape(T, D_MODEL)

    h, h_t = _rms_fwd(xf, norm1_g, "rms1_fwd")
    proj, got1 = _mm("in_proj", h, w_in, grid=(N_CHIPS, T // 1024),
                     a_spec=pl.BlockSpec((1024, D_MODEL), lambda s, m: (m, 0)),
                     b_spec=pl.BlockSpec((None, D_MODEL, IN_S), lambda s, m: (s, 0, 0)),
                     o_spec=pl.BlockSpec((1024, IN_S), lambda s, m: (m, s)), o_shape=(T, IN_W), o_dtype=bf16, dims=NN,
                     comm=_x_gather_ici([shards[n] for n in GATHER_1], [slots[n] for n in GATHER_1]))
    proj3 = proj.reshape(B, SEQ, IN_W)
    proj4 = proj.reshape(B, 1, SEQ, IN_W)

    qkv = [None] + [_qkv_to_residues(proj3, g) for g in range(1, len(GROUPS))]

    def qkv_args(g):
        return (proj4, proj4, proj4, 0, 3, 6) if g == 0 else (qkv[g], qkv[g], qkv[g], 0, 1, 2)

    (o4_0, l4_0), full1 = _attn_fwd(0, *qkv_args(0), comm=_x_gather_fwd(list(got1)))
    o4_1, l4_1 = _attn_fwd(1, *qkv_args(1))
    o4_2, l4_2 = _attn_fwd(2, *qkv_args(2))
    full = dict(zip(GATHER_1, full1))
    w_conv_out, w_attn_out, w_o, w_gate, w_down = (full[n] for n in GATHER_1)
    w_conv_out_f = w_conv_out.reshape(D_MODEL, D_MODEL)
    w_o_f = w_o.reshape(D_MODEL, D_MODEL)
    mix3, lse3, lse_r1, lse_r2 = _attn_mix(o4_0.reshape(B, SEQ, GW), l4_0.reshape(B, SEQ, LANES), [o4_1, o4_2], [l4_1, l4_2])
    mix = mix3.reshape(T, GW)
    y_attn = _attn_out_fwd(mix, w_attn_out)

    c1, got_up = _glu_conv_fwd(proj3, conv_w, conv_b, comm=_x_gather_ici([shards["w_ffn_up"]], [slots["w_ffn_up"]]))
    c1 = c1.reshape(T, D_MODEL)
    (c3, y_conv), (w_up,) = _conv_out_fwd(c1, conv_ln_g, conv_ln_b, w_conv_out_f, comm=_x_gather_fwd(list(got_up)))

    merged, x1, h2 = _merge_o_proj_rms(proj, gate_b, y_attn, y_conv, w_o_f, xf, norm2_g)
    fa, fb, ff = _ffn_up(h2, w_gate, w_up)
    loss, dx2, d_gf = _ffn_down_loss(ff, w_down, x1, norm_f_g.reshape(1, D_MODEL), tf)

    d_w_down = _mm("d_w_down", ff, dx2, grid=(N_CHIPS, T // 1024),
                   a_spec=pl.BlockSpec((None, 1024, FF_S), lambda s, k: (s, k, 0)),
                   b_spec=pl.BlockSpec((1024, D_MODEL), lambda s, k: (k, 0)),
                   o_spec=pl.BlockSpec((None, FF_S, D_MODEL), lambda s, k: (s, 0, 0)), o_shape=(N_CHIPS, FF_S, D_MODEL),
                   o_dtype=bf16, dims=TN, acc_shape=(FF_S, D_MODEL), k_axis=1, sem=("parallel", "arbitrary"))
    da, db = _ffn_down_bwd(dx2, w_down, fa, fb)

    def d_w_ff(name, dz):
        return _mm(name, dz, h2, grid=(N_CHIPS, T // 2048),
                   a_spec=pl.BlockSpec((None, 2048, FF_S), lambda s, k: (s, k, 0)),
                   b_spec=pl.BlockSpec((2048, D_MODEL), lambda s, k: (k, 0)),
                   o_spec=pl.BlockSpec((None, FF_S, D_MODEL), lambda s, k: (s, 0, 0)), o_shape=(N_CHIPS, FF_S, D_MODEL),
                   o_dtype=bf16, dims=TN, acc_shape=(FF_S, D_MODEL), k_axis=1, sem=("parallel", "arbitrary"))

    d_w_gate = d_w_ff("d_w_gate", da)
    d_w_up = d_w_ff("d_w_up", db)
    part = dict(w_ffn_gate=d_w_gate, w_ffn_up=d_w_up, w_ffn_down=d_w_down)

    def pair_sums(names, got):
        out = {}
        for ns, gs in _same_shape_groups(names, got):
            out.update(zip(ns, _add_pairs([part[n] for n in ns], gs, core, "pair_sum_" + ns[0])))
        return [out[n] for n in names]

    (dx1, d_g2), got = _mm_rms_bwd("ffn_dh2", [(da, w_gate), (db, w_up)], x1, norm2_g, dx2, tm=1024,
                                   a_spec=pl.BlockSpec((None, 1024, FF_S), lambda m, s: (s, m, 0)),
                                   b_spec=pl.BlockSpec((None, FF_S, D_MODEL), lambda m, s: (s, 0, 0)), dims=NN,
                                   comm=_x_pair_exchange([part[n] for n in REDUCE_A]), vmem=VMEM_LIMIT_BIG)
    sums_a = pair_sums(REDUCE_A, got)

    d_w_o = _mm_tn_tokens("d_w_o", merged, dx1, bf16).reshape(N_CHIPS, D_MODEL // N_CHIPS, D_MODEL)
    dmerged = _mm_nt_full("d_merged", dx1, w_o_f, bf16)
    dya, dyc, dga, dgc, d_gba, d_gbc = _merge_bwd(dmerged, proj, gate_b, y_attn, y_conv)

    d_w_conv_out = _mm_tn_tokens("d_w_conv_out", c3, dyc, bf16).reshape(N_CHIPS, D_MODEL // N_CHIPS, D_MODEL)
    dc1, d_ln_g, d_ln_b = _conv_out_bwd(dyc, w_conv_out_f, c1, conv_ln_g, conv_ln_b)
    (dua, dub, d_conv_w, d_conv_b), got_a = _glu_conv_bwd(dc1.reshape(B, SEQ, D_MODEL), proj3, conv_w, comm=_x_chip_exchange(sums_a))

    d_w_attn_out = _attn_out_bwd_w(mix, dya)
    part.update(w_conv_out=d_w_conv_out, w_attn_out=d_w_attn_out, w_o=d_w_o)
    dmix, got = _attn_out_bwd_x(dya, w_attn_out, comm=_x_pair_exchange([part[n] for n in REDUCE_B]))
    sums_b = pair_sums(REDUCE_B, got)
    dmix3 = dmix.reshape(B, SEQ, GW)
    delta3, delta_r1, delta_r2, dmix_r1, dmix_r2 = _attn_delta(dmix3, mix3)
    one = (B, 1, SEQ)
    (dq0, dk0, dv0), got_b = _attn_bwd(0, *qkv_args(0), dmix3.reshape(one + (GW,)), lse3.reshape(one + (LANES,)),
                                       delta3.reshape(one + (LANES,)), comm=_x_chip_exchange(sums_b))
    dqkv = [tuple(t.reshape(B, SEQ, GW) for t in (dq0, dk0, dv0)),
            _attn_bwd(1, *qkv_args(1), dmix_r1, lse_r1, delta_r1), _attn_bwd(2, *qkv_args(2), dmix_r2, lse_r2, delta_r2)]
    dproj = _assemble_dproj(dqkv, dua, dub, dga.reshape(B, SEQ, D_MODEL), dgc.reshape(B, SEQ, D_MODEL)).reshape(T, IN_W)

    d_w_in = _mm("d_w_in", h_t, dproj, grid=(N_CHIPS, T // 2048),
                 a_spec=pl.BlockSpec((D_MODEL, 2048), lambda s, k: (0, k)),
                 b_spec=pl.BlockSpec((2048, IN_S), lambda s, k: (k, s)),
                 o_spec=pl.BlockSpec((None, D_MODEL, IN_S), lambda s, k: (s, 0, 0)), o_shape=(N_CHIPS, D_MODEL, IN_S),
                 o_dtype=bf16, dims=NN, acc_shape=(D_MODEL, IN_S), k_axis=1, sem=("parallel", "arbitrary"), vmem=VMEM_LIMIT_BIG)
    part.update(w_in=d_w_in)
    sums_c = pair_sums(REDUCE_C, _run_exchange(_x_pair_exchange([d_w_in]), "grad_pair_exchange_c"))
    (dx, d_g1), got_c = _mm_rms_bwd("d_h", [(dproj, w_in)], xf, norm1_g, dx1, tm=1024,
                                    a_spec=pl.BlockSpec((1024, IN_S), lambda m, s: (m, s)),
                                    b_spec=pl.BlockSpec((None, D_MODEL, IN_S), lambda m, s: (s, 0, 0)), dims=NT,
                                    comm=_x_chip_exchange(sums_c), vmem=VMEM_LIMIT_BIG)

    names = REDUCE_A + REDUCE_B + REDUCE_C
    sums = dict(zip(names, sums_a + sums_b + sums_c))
    halves = {}
    for ns, gs in _same_shape_groups(names, got_a + got_b + got_c):
        halves.update(zip(ns, _sum4([sums[n] for n in ns], gs, chip1, core, "chip_sum_" + ns[0])))
    big = {n: halves[n] for n in names}
    small = dict(norm1_g=d_g1, gate_b=jnp.concatenate([d_gba, d_gbc], axis=-1), conv_b=d_conv_b, conv_ln_g=d_ln_g,
                 conv_ln_b=d_ln_b, norm2_g=d_g2, norm_f_g=d_gf, conv_w=d_conv_w)
    return loss, dx.reshape(B, SEQ, D_MODEL), big, small


BIG = ("w_in", "w_conv_out", "w_attn_out", "w_o", "w_ffn_gate", "w_ffn_up", "w_ffn_down")
TRANSPOSED = ("w_ffn_gate", "w_ffn_up")
SMALL = ("norm1_g", "gate_b", "conv_b", "conv_ln_g", "conv_ln_b", "norm2_g", "norm_f_g")
SMALL_ROWS = {"norm1_g": 8, "gate_b": 16, "conv_b": 8, "conv_ln_g": 8, "conv_ln_b": 8, "norm2_g": 8, "norm_f_g": 8}
LOSS_ROWS = 8
CONVW_ROWS = 32 * D_MODEL // LANES


def kernel(x, norm1_g, w_in, gate_b, conv_w, conv_b, conv_ln_g, conv_ln_b, w_conv_out, w_attn_out, w_o, norm2_g, w_ffn_gate, w_ffn_up, w_ffn_down, norm_f_g, loss_target, m_norm1_g, m_w_in, m_gate_b, m_conv_w, m_conv_b, m_conv_ln_g, m_conv_ln_b, m_w_conv_out, m_w_attn_out, m_w_o, m_norm2_g, m_w_ffn_gate, m_w_ffn_up, m_w_ffn_down, m_norm_f_g, v_norm1_g, v_w_in, v_gate_b, v_conv_w, v_conv_b, v_conv_ln_g, v_conv_ln_b, v_w_conv_out, v_w_attn_out, v_w_o, v_norm2_g, v_w_ffn_gate, v_w_ffn_up, v_w_ffn_down, v_norm_f_g):
    W = dict(norm1_g=norm1_g, w_in=w_in, gate_b=gate_b, conv_w=conv_w, conv_b=conv_b, conv_ln_g=conv_ln_g, conv_ln_b=conv_ln_b,
             w_conv_out=w_conv_out, w_attn_out=w_attn_out, w_o=w_o, norm2_g=norm2_g, w_ffn_gate=w_ffn_gate, w_ffn_up=w_ffn_up,
             w_ffn_down=w_ffn_down, norm_f_g=norm_f_g)
    M = dict(norm1_g=m_norm1_g, w_in=m_w_in, gate_b=m_gate_b, conv_w=m_conv_w, conv_b=m_conv_b, conv_ln_g=m_conv_ln_g,
             conv_ln_b=m_conv_ln_b, w_conv_out=m_w_conv_out, w_attn_out=m_w_attn_out, w_o=m_w_o, norm2_g=m_norm2_g,
             w_ffn_gate=m_w_ffn_gate, w_ffn_up=m_w_ffn_up, w_ffn_down=m_w_ffn_down, norm_f_g=m_norm_f_g)
    V = dict(norm1_g=v_norm1_g, w_in=v_w_in, gate_b=v_gate_b, conv_w=v_conv_w, conv_b=v_conv_b, conv_ln_g=v_conv_ln_g,
             conv_ln_b=v_conv_ln_b, w_conv_out=v_w_conv_out, w_attn_out=v_w_attn_out, w_o=v_w_o, norm2_g=v_norm2_g,
             w_ffn_gate=v_w_ffn_gate, w_ffn_up=v_w_ffn_up, w_ffn_down=v_w_ffn_down, norm_f_g=v_norm_f_g)
    order = list(W)

    def as2d(n, a):
        a = a.reshape(a.shape[-2:])
        return a.T if n in TRANSPOSED else a

    def from2d(n, a):
        return (a.T if n in TRANSPOSED else a).reshape(W[n].shape)

    shard2d = {n: as2d(n, W[n]) for n in BIG}
    chip = 2 * lax.axis_index("x") + lax.axis_index("y")

    core = lax.axis_index("c").astype(jnp.int32).reshape(1)
    chip1 = chip.astype(jnp.int32).reshape(1)
    shards = {n: shard2d[n].astype(bf16) for n in BIG}

    def slot_for(s):
        return lax.dynamic_update_slice(lax.empty((N_CHIPS,) + s.shape, s.dtype), s[None], (chip, 0, 0))

    slots = {n: slot_for(s) for n, s in shards.items()}
    cw = jnp.pad(conv_w.reshape(CONV_K, D_MODEL // N_CHIPS), ((0, CONV_HALO - CONV_K), (0, 0)))
    w_in_full, cw4 = _allgather_weights([shards["w_in"], cw], [slots["w_in"], slot_for(cw)])
    conv_w_full = cw4.transpose(1, 0, 2).reshape(CONV_HALO, D_MODEL)[:CONV_K]

    loss, grad_x, grads, gsmall = _step(x, loss_target, norm1_g, gate_b, conv_w_full, conv_b, conv_ln_g, conv_ln_b, norm2_g,
                                        norm_f_g, w_in_full, shards, slots, chip1, core)

    pieces = [_rows128(loss, LOSS_ROWS)] + [_rows128(gsmall[n], SMALL_ROWS[n]) for n in SMALL] + [_rows128(gsmall["conv_w"], CONVW_ROWS)]
    full_rows, tot = _final_exchange([grads[n] for n in BIG], jnp.concatenate(pieces, axis=0))
    grads = dict(zip(BIG, full_rows))
    loss_out = tot[0, 0]
    row = LOSS_ROWS
    for n in SMALL:
        grads[n] = tot[row: row + W[n].size // LANES].reshape(W[n].shape)
        row += SMALL_ROWS[n]
    dcw = tot[row: row + CONV_K * D_MODEL // LANES].reshape(CONV_K, D_MODEL)
    grads["conv_w"] = lax.dynamic_slice(dcw, (0, chip * (D_MODEL // N_CHIPS)), (CONV_K, D_MODEL // N_CHIPS))

    delta, new_m, new_v = {}, {}, {}
    for ns, ws in _same_shape_groups(BIG, [shard2d[n] for n in BIG]):
        ds, nms, nvs = _adamw(ws, [grads[n] for n in ns], [as2d(n, M[n]) for n in ns], [as2d(n, V[n]) for n in ns], "adamw_" + ns[0])
        for n, d, nm, nv in zip(ns, ds, nms, nvs):
            delta[n], new_m[n], new_v[n], grads[n] = (from2d(n, t) for t in (d, nm, nv, grads[n]))

    def pack(src):
        return jnp.concatenate([_rows128(src[n], SMALL_ROWS[n]) for n in SMALL], axis=0)

    (d,), (nm,), (nv,) = _adamw([pack(W)], [pack(grads)], [pack(M)], [pack(V)], "adamw_small")
    row = 0
    for n in SMALL:
        k = W[n].size // LANES
        delta[n], new_m[n], new_v[n] = (t[row: row + k].reshape(W[n].shape) for t in (d, nm, nv))
        row += SMALL_ROWS[n]

    def pad32(a):
        return jnp.pad(a.reshape(CONV_K, D_MODEL // N_CHIPS), ((0, 1), (0, 0)))

    (d,), (nm,), (nv,) = _adamw([pad32(conv_w)], [pad32(grads["conv_w"])], [pad32(m_conv_w)], [pad32(v_conv_w)], "adamw_conv_w")
    delta["conv_w"], new_m["conv_w"], new_v["conv_w"] = (t[:CONV_K].reshape(conv_w.shape) for t in (d, nm, nv))
    grads["conv_w"] = grads["conv_w"].reshape(conv_w.shape)

    return (loss_out, grad_x, *[grads[n] for n in order], *[delta[n] for n in order],
            *[new_m[n] for n in order], *[new_v[n] for n in order])
```

```python
import functools
import math
from typing import Callable, NamedTuple

import numpy as np
import jax
import jax.numpy as jnp
from jax import lax
from jax.experimental import pallas as pl
from jax.experimental.pallas import tpu as pltpu

f32 = jnp.float32
bf16 = jnp.bfloat16
SDS = jax.ShapeDtypeStruct
MESH = pl.DeviceIdType.MESH

D_MODEL = 1024
SEQ = 2048
HEAD_DIM = 64
HEADS = 8
GROUPS = ((128, 1), (512, 4), (2048, 16))
GW = HEADS * HEAD_DIM
ATTN_W = len(GROUPS) * GW
Q_BLOCK = 128
CONV_K = 31
D_FF = 2816
IN_W = 3 * ATTN_W + 2 * D_MODEL + 2 * D_MODEL
N_CHIPS = 4
IN_S = IN_W // N_CHIPS
FF_S = D_FF // N_CHIPS
RMS_EPS = 1e-6
LN_EPS = 1e-5
LR, B1, B2, ADAM_EPS, WD, STEP = 0.001, 0.9, 0.999, 1e-08, 0.01, 10
NEG = -1e30
LANES = 128
VMEM_LIMIT = 48 * 2 ** 20
VMEM_LIMIT_BIG = 54 * 2 ** 20
CB = 512
UA_CB, UB_CB, GA_CB, GC_CB = 9, 11, 13, 15


def _alibi_slope_list(n):
    def pow2(m):
        start = 2.0 ** (-8.0 / m)
        return [start ** (i + 1) for i in range(m)]
    if math.log2(n).is_integer():
        return pow2(n)
    c = 2 ** math.floor(math.log2(n))
    return pow2(c) + _alibi_slope_list(2 * c)[0::2][: n - c]


_SLOPES = np.asarray(sorted(_alibi_slope_list(len(GROUPS) * HEADS), reverse=True), dtype=np.float32).reshape(len(GROUPS), HEADS)


def _cp(sem=None, vmem=VMEM_LIMIT):
    return pltpu.CompilerParams(dimension_semantics=sem, vmem_limit_bytes=vmem)


def _sigmoid(x):
    return 1.0 / (1.0 + jnp.exp(-x))


HBM = pl.BlockSpec(memory_space=pl.ANY)


class _Exchange(NamedTuple):
    ins: list
    out_shape: list
    scratch: list
    aliases: dict
    start: Callable
    wait: Callable


def _launch(body, *, name, grid, in_specs, out_specs, out_shape, args, scratch_shapes=(), sem=None, comm=None, vmem=VMEM_LIMIT):
    if comm is None:
        return pl.pallas_call(body, grid=grid, in_specs=in_specs, out_specs=out_specs, out_shape=out_shape,
                              scratch_shapes=list(scratch_shapes), name=name, compiler_params=_cp(sem, vmem))(*args)
    multi = isinstance(out_shape, (tuple, list))
    m_out = list(out_shape) if multi else [out_shape]
    m_ospec = list(out_specs) if multi else [out_specs]
    n_in, n_out, n_scr = len(in_specs), len(m_out), len(scratch_shapes)
    nc_in, nc_out = len(comm.ins), len(comm.out_shape)

    def hosted(*refs):
        bounds = np.cumsum([0, n_in, nc_in, n_out, nc_out, n_scr])
        mi, ci, mo, co, ms = (refs[a:b] for a, b in zip(bounds[:-1], bounds[1:]))
        cs = refs[bounds[-1]:]
        ids = [pl.program_id(a) for a in range(len(grid))]
        first = functools.reduce(jnp.logical_and, [i == 0 for i in ids])
        last = functools.reduce(jnp.logical_and, [i == g - 1 for i, g in zip(ids, grid)])

        @pl.when(first)
        def _():
            comm.start(ci, co, cs)

        body(*mi, *mo, *ms)

        @pl.when(last)
        def _():
            comm.wait(ci, co, cs)

    res = pl.pallas_call(
        hosted, grid=grid, in_specs=list(in_specs) + [HBM] * nc_in, out_specs=m_ospec + [HBM] * nc_out,
        out_shape=tuple(m_out) + tuple(comm.out_shape), scratch_shapes=list(scratch_shapes) + list(comm.scratch),
        input_output_aliases={n_in + i: n_out + o for i, o in comm.aliases.items()}, name=name + "_comm",
        compiler_params=pltpu.CompilerParams(dimension_semantics=("arbitrary",) * len(grid), vmem_limit_bytes=vmem,
                                             has_side_effects=True))(*args, *comm.ins)
    return (tuple(res[:n_out]) if multi else res[0]), tuple(res[n_out:])


def _run_exchange(ex, name):
    n_in = len(ex.ins)

    def body(*refs):
        ins, outs, sems = refs[:n_in], refs[n_in:n_in + len(ex.out_shape)], refs[n_in + len(ex.out_shape):]
        ex.start(ins, outs, sems)
        ex.wait(ins, outs, sems)

    return pl.pallas_call(body, in_specs=[HBM] * n_in, out_specs=[HBM] * len(ex.out_shape), out_shape=tuple(ex.out_shape),
                          scratch_shapes=list(ex.scratch), input_output_aliases=dict(ex.aliases), name=name,
                          compiler_params=pltpu.CompilerParams(has_side_effects=True))(*ex.ins)


def _rmsnorm(xv, gv):
    return xv * lax.rsqrt(jnp.mean(xv * xv, axis=-1, keepdims=True) + RMS_EPS) * gv


def _rms_fwd(x, g, name):
    T = x.shape[0]
    tm = 512

    def body(x_ref, g_ref, o_ref, ot_ref):
        hv = _rmsnorm(x_ref[...], g_ref[...])
        o_ref[...] = hv.astype(bf16)
        ot_ref[...] = hv.T.astype(bf16)

    row = pl.BlockSpec((tm, D_MODEL), lambda i: (i, 0))
    vec = pl.BlockSpec((1, D_MODEL), lambda i: (0, 0))
    return pl.pallas_call(body, grid=(T // tm,), in_specs=[row, vec], out_specs=[row, pl.BlockSpec((D_MODEL, tm), lambda i: (0, i))],
                          out_shape=(SDS((T, D_MODEL), bf16), SDS((D_MODEL, T), bf16)), name=name,
                          compiler_params=_cp(("parallel",)))(x, g)


def _rms_bwd_tile(dyv, xv, gv, dres):
    r = lax.rsqrt(jnp.mean(xv * xv, axis=-1, keepdims=True) + RMS_EPS)
    xh = xv * r
    dxh = dyv * gv
    dx = dres + r * (dxh - xh * jnp.mean(dxh * xh, axis=-1, keepdims=True))
    return dx, jnp.sum(dyv * xh, axis=0, keepdims=True)


def _accumulate(first, refs_parts):
    @pl.when(first)
    def _():
        for ref, part in refs_parts:
            ref[...] = part

    @pl.when(jnp.logical_not(first))
    def _():
        for ref, part in refs_parts:
            ref[...] += part


def _mm_rms_bwd(name, ops, x, g, dres, *, tm, a_spec, b_spec, dims, comm=None, vmem=VMEM_LIMIT):
    T = x.shape[0]
    n = len(ops)

    def body(*refs):
        ab_refs = refs[:2 * n]
        x_ref, g_ref, r_ref, dx_ref, dg_ref, acc = refs[2 * n:]
        m, s = pl.program_id(0), pl.program_id(1)
        p = sum(lax.dot_general(ab_refs[2 * i][...], ab_refs[2 * i + 1][...], dims, preferred_element_type=f32) for i in range(n))

        @pl.when(s == 0)
        def _():
            acc[...] = p

        @pl.when(s > 0)
        def _():
            acc[...] += p

        @pl.when(s == N_CHIPS - 1)
        def _():
            dx, part = _rms_bwd_tile(acc[...], x_ref[...], g_ref[...], r_ref[...])
            dx_ref[...] = dx
            _accumulate(m == 0, [(dg_ref, part)])

    row = pl.BlockSpec((tm, D_MODEL), lambda m, s: (m, 0))
    vec = pl.BlockSpec((1, D_MODEL), lambda m, s: (0, 0))
    return _launch(body, name=name, grid=(T // tm, N_CHIPS), in_specs=[a_spec, b_spec] * n + [row, vec, row], out_specs=[row, vec],
                   out_shape=(SDS((T, D_MODEL), f32), SDS((1, D_MODEL), f32)), args=tuple(t for ab in ops for t in ab) + (x, g, dres),
                   scratch_shapes=[pltpu.VMEM((tm, D_MODEL), f32)], sem=("arbitrary", "arbitrary"), comm=comm, vmem=vmem)


def _ffn_down_loss(ff, wd, x1, gf, target):
    T = x1.shape[0]
    tm = 1024
    nm = T // tm

    def body(f_ref, w_ref, x_ref, g_ref, t_ref, loss_ref, dx_ref, dg_ref, acc):
        m, s = pl.program_id(0), pl.program_id(1)
        p = jnp.dot(f_ref[...], w_ref[...], preferred_element_type=f32)

        @pl.when(s == 0)
        def _():
            acc[...] = p

        @pl.when(s > 0)
        def _():
            acc[...] += p

        @pl.when(s == N_CHIPS - 1)
        def _():
            xv = acc[...] + x_ref[...]
            gv = g_ref[...]
            r = lax.rsqrt(jnp.mean(xv * xv, axis=-1, keepdims=True) + RMS_EPS)
            xh = xv * r
            e = xh * gv - t_ref[...]
            part_l = jnp.broadcast_to(0.5 * jnp.sum(jnp.mean(e * e, axis=-1, keepdims=True), axis=0, keepdims=True), (1, LANES))
            dy = e * (1.0 / D_MODEL)
            dxh = dy * gv
            dx_ref[...] = r * (dxh - xh * jnp.mean(dxh * xh, axis=-1, keepdims=True))
            part_g = jnp.sum(dy * xh, axis=0, keepdims=True)

            @pl.when(m == 0)
            def _():
                loss_ref[...] = part_l
                dg_ref[...] = part_g

            @pl.when(m > 0)
            def _():
                loss_ref[...] += part_l
                dg_ref[...] += part_g

    row = pl.BlockSpec((tm, D_MODEL), lambda m, s: (m, 0))
    vec = pl.BlockSpec((1, D_MODEL), lambda m, s: (0, 0))
    return pl.pallas_call(
        body, grid=(nm, N_CHIPS),
        in_specs=[pl.BlockSpec((None, tm, FF_S), lambda m, s: (s, m, 0)), pl.BlockSpec((None, FF_S, D_MODEL), lambda m, s: (s, 0, 0)),
                  row, vec, row],
        out_specs=[pl.BlockSpec((1, LANES), lambda m, s: (0, 0)), row, vec],
        out_shape=(SDS((1, LANES), f32), SDS((T, D_MODEL), f32), SDS((1, D_MODEL), f32)),
        scratch_shapes=[pltpu.VMEM((tm, D_MODEL), f32)], name="ffn_down_loss", compiler_params=_cp(("arbitrary", "arbitrary")))(
            ff, wd, x1, gf, target)


def _merge_o_proj_rms(proj, gate_b, ya, yc, w_o, x, g2):
    T = x.shape[0]
    tm = 512

    def body(ga0, ga1, gc0, gc1, b_ref, ya_ref, yc_ref, w_ref, x_ref, g_ref, m_ref, x1_ref, h2_ref):
        bv = b_ref[...]
        ga = jnp.concatenate([ga0[...], ga1[...]], axis=-1).astype(f32)
        gc = jnp.concatenate([gc0[...], gc1[...]], axis=-1).astype(f32)
        merged = (_sigmoid(ga + bv[:, :D_MODEL]) * ya_ref[...].astype(f32)
                  + _sigmoid(gc + bv[:, D_MODEL:]) * yc_ref[...].astype(f32)).astype(bf16)
        m_ref[...] = merged
        x1 = jnp.dot(merged, w_ref[...], preferred_element_type=f32) + x_ref[...]
        x1_ref[...] = x1
        h2_ref[...] = _rmsnorm(x1, g_ref[...]).astype(bf16)

    row = pl.BlockSpec((tm, D_MODEL), lambda i: (i, 0))
    return pl.pallas_call(
        body, grid=(T // tm,),
        in_specs=[pl.BlockSpec((tm, CB), lambda i, cb=cb: (i, cb)) for cb in (GA_CB, GA_CB + 1, GC_CB, GC_CB + 1)]
        + [pl.BlockSpec((1, 2 * D_MODEL), lambda i: (0, 0)), row, row, pl.BlockSpec((D_MODEL, D_MODEL), lambda i: (0, 0)), row,
           pl.BlockSpec((1, D_MODEL), lambda i: (0, 0))],
        out_specs=[row, row, row], out_shape=(SDS((T, D_MODEL), bf16), SDS((T, D_MODEL), f32), SDS((T, D_MODEL), bf16)),
        name="merge_o_proj_rms", compiler_params=_cp(("parallel",)))(proj, proj, proj, proj, gate_b, ya, yc, w_o, x, g2)


def _conv_out_fwd(c1, g, b, w, comm=None):
    T = c1.shape[0]
    tm = 1024

    def body(c_ref, g_ref, b_ref, w_ref, c3_ref, y_ref):
        cv = c_ref[...]
        mu = jnp.mean(cv, axis=-1, keepdims=True)
        cc = cv - mu
        var = jnp.mean(cc * cc, axis=-1, keepdims=True)
        c2 = cc * lax.rsqrt(var + LN_EPS) * g_ref[...] + b_ref[...]
        c3 = (c2 * _sigmoid(c2)).astype(bf16)
        c3_ref[...] = c3
        y_ref[...] = jnp.dot(c3, w_ref[...], preferred_element_type=f32).astype(bf16)

    row = pl.BlockSpec((tm, D_MODEL), lambda i: (i, 0))
    vec = pl.BlockSpec((1, D_MODEL), lambda i: (0, 0))
    sd = SDS((T, D_MODEL), bf16)
    return _launch(body, name="conv_out_fwd", grid=(T // tm,), in_specs=[row, vec, vec, pl.BlockSpec((D_MODEL, D_MODEL), lambda i: (0, 0))],
                   out_specs=[row, row], out_shape=(sd, sd), args=(c1, g, b, w), sem=("parallel",), comm=comm)


def _conv_out_bwd(dyc, w, c1, g, b):
    T = c1.shape[0]
    tm = 1024

    def body(d_ref, w_ref, c_ref, g_ref, b_ref, dc_ref, dg_ref, db_ref):
        dc3 = lax.dot_general(d_ref[...], w_ref[...], NT, preferred_element_type=f32)
        cv = c_ref[...]
        gv = g_ref[...]
        mu = jnp.mean(cv, axis=-1, keepdims=True)
        cc = cv - mu
        var = jnp.mean(cc * cc, axis=-1, keepdims=True)
        rs = lax.rsqrt(var + LN_EPS)
        xh = cc * rs
        c2 = xh * gv + b_ref[...]
        sg = _sigmoid(c2)
        dc2 = dc3 * (sg * (1.0 + c2 * (1.0 - sg)))
        dxh = dc2 * gv
        dc_ref[...] = rs * (dxh - jnp.mean(dxh, axis=-1, keepdims=True) - xh * jnp.mean(dxh * xh, axis=-1, keepdims=True))
        _accumulate(pl.program_id(0) == 0, [(dg_ref, jnp.sum(dc2 * xh, axis=0, keepdims=True)), (db_ref, jnp.sum(dc2, axis=0, keepdims=True))])

    row = pl.BlockSpec((tm, D_MODEL), lambda i: (i, 0))
    vec = pl.BlockSpec((1, D_MODEL), lambda i: (0, 0))
    return pl.pallas_call(body, grid=(T // tm,), in_specs=[row, pl.BlockSpec((D_MODEL, D_MODEL), lambda i: (0, 0)), row, vec, vec],
                          out_specs=[row, vec, vec],
                          out_shape=(SDS((T, D_MODEL), f32), SDS((1, D_MODEL), f32), SDS((1, D_MODEL), f32)),
                          name="conv_out_bwd", compiler_params=_cp(("arbitrary",)))(dyc, w, c1, g, b)


NN = (((1,), (0,)), ((), ()))
NT = (((1,), (1,)), ((), ()))
TN = (((0,), (0,)), ((), ()))


def _mm(name, a, b, *, grid, a_spec, b_spec, o_spec, o_shape, o_dtype, dims, acc_shape=None, k_axis=None,
        res=None, res_spec=None, sem=None, comm=None, vmem=VMEM_LIMIT):
    nk = 1 if k_axis is None else grid[k_axis]

    def body(*refs):
        if res is None:
            a_ref, b_ref, o_ref = refs[:3]
            r_ref, scr = None, refs[3:]
        else:
            a_ref, b_ref, r_ref, o_ref = refs[:4]
            scr = refs[4:]
        p = lax.dot_general(a_ref[...].astype(bf16), b_ref[...].astype(bf16), dims, preferred_element_type=f32)
        if nk == 1:
            if r_ref is not None:
                p = p + r_ref[...]
            o_ref[...] = p.astype(o_dtype)
            return
        acc = scr[0]
        k = pl.program_id(k_axis)

        @pl.when(k == 0)
        def _():
            acc[...] = p

        @pl.when(k > 0)
        def _():
            acc[...] += p

        @pl.when(k == nk - 1)
        def _():
            t = acc[...]
            if r_ref is not None:
                t = t + r_ref[...]
            o_ref[...] = t.astype(o_dtype)

    ins = [a, b] + ([] if res is None else [res])
    specs = [a_spec, b_spec] + ([] if res is None else [res_spec])
    scratch = [] if nk == 1 else [pltpu.VMEM(acc_shape, f32)]
    return _launch(body, name=name, grid=grid, in_specs=specs, out_specs=o_spec, out_shape=SDS(o_shape, o_dtype),
                   args=ins, scratch_shapes=scratch, sem=sem, comm=comm, vmem=vmem)


def _mm_nn_full(name, a, b, o_dtype, res=None, tm=1024, comm=None):
    T, K = a.shape
    N = b.shape[1]
    return _mm(name, a, b, grid=(T // tm,), a_spec=pl.BlockSpec((tm, K), lambda i: (i, 0)),
               b_spec=pl.BlockSpec((K, N), lambda i: (0, 0)), o_spec=pl.BlockSpec((tm, N), lambda i: (i, 0)),
               o_shape=(T, N), o_dtype=o_dtype, dims=NN, res=res,
               res_spec=None if res is None else pl.BlockSpec((tm, N), lambda i: (i, 0)), sem=("parallel",), comm=comm)


def _mm_nt_full(name, a, b, o_dtype, tm=1024):
    T, N = a.shape
    K = b.shape[0]
    return _mm(name, a, b, grid=(T // tm,), a_spec=pl.BlockSpec((tm, N), lambda i: (i, 0)),
               b_spec=pl.BlockSpec((K, N), lambda i: (0, 0)), o_spec=pl.BlockSpec((tm, K), lambda i: (i, 0)),
               o_shape=(T, K), o_dtype=o_dtype, dims=NT, sem=("parallel",))


def _mm_tn_tokens(name, a, b, o_dtype, tk=1024):
    T, K = a.shape
    N = b.shape[1]
    return _mm(name, a, b, grid=(T // tk,), a_spec=pl.BlockSpec((tk, K), lambda k: (k, 0)),
               b_spec=pl.BlockSpec((tk, N), lambda k: (k, 0)), o_spec=pl.BlockSpec((K, N), lambda k: (0, 0)),
               o_shape=(K, N), o_dtype=o_dtype, dims=TN, acc_shape=(K, N), k_axis=0, sem=("arbitrary",))


AO_S = D_MODEL // N_CHIPS


def _attn_out_cat(w_ref):
    return jnp.concatenate([w_ref[s] for s in range(N_CHIPS)], axis=-1)


def _attn_out_fwd(mix, w4):
    T = mix.shape[0]
    tm = 1024

    def body(a_ref, w_ref, o_ref):
        o_ref[...] = jnp.dot(a_ref[...], _attn_out_cat(w_ref), preferred_element_type=f32).astype(bf16)

    return pl.pallas_call(body, grid=(T // tm,),
                          in_specs=[pl.BlockSpec((tm, GW), lambda i: (i, 0)), pl.BlockSpec((N_CHIPS, GW, AO_S), lambda i: (0, 0, 0))],
                          out_specs=pl.BlockSpec((tm, D_MODEL), lambda i: (i, 0)), out_shape=SDS((T, D_MODEL), bf16), name="attn_out",
                          compiler_params=_cp(("parallel",)))(mix, w4)


def _attn_out_bwd_x(dya, w4, comm=None):
    T = dya.shape[0]
    tm = 1024

    def body(d_ref, w_ref, o_ref):
        o_ref[...] = lax.dot_general(d_ref[...], _attn_out_cat(w_ref), NT, preferred_element_type=f32).astype(bf16)

    return _launch(body, name="d_mix", grid=(T // tm,),
                   in_specs=[pl.BlockSpec((tm, D_MODEL), lambda i: (i, 0)), pl.BlockSpec((N_CHIPS, GW, AO_S), lambda i: (0, 0, 0))],
                   out_specs=pl.BlockSpec((tm, GW), lambda i: (i, 0)), out_shape=SDS((T, GW), bf16), args=(dya, w4),
                   sem=("parallel",), comm=comm)


def _attn_out_bwd_w(mix, dya):
    T = mix.shape[0]
    tk = 1024
    nk = T // tk

    def body(a_ref, d_ref, o_ref, acc):
        k = pl.program_id(0)
        p = lax.dot_general(a_ref[...], d_ref[...], TN, preferred_element_type=f32)

        @pl.when(k == 0)
        def _():
            acc[...] = p

        @pl.when(k > 0)
        def _():
            acc[...] += p

        @pl.when(k == nk - 1)
        def _():
            for s in range(N_CHIPS):
                o_ref[s] = acc[:, s * AO_S:(s + 1) * AO_S].astype(bf16)

    return pl.pallas_call(body, grid=(nk,),
                          in_specs=[pl.BlockSpec((tk, GW), lambda k: (k, 0)), pl.BlockSpec((tk, D_MODEL), lambda k: (k, 0))],
                          out_specs=pl.BlockSpec((N_CHIPS, GW, AO_S), lambda k: (0, 0, 0)), out_shape=SDS((N_CHIPS, GW, AO_S), bf16),
                          scratch_shapes=[pltpu.VMEM((GW, D_MODEL), f32)], name="d_w_attn_out",
                          compiler_params=_cp(("arbitrary",)))(mix, dya)


def _ffn_up(h2, wg, wu):
    T = h2.shape[0]
    tm = 1024

    def body(h_ref, wg_ref, wu_ref, a_ref, b_ref, f_ref):
        hv = h_ref[...]
        av = lax.dot_general(hv, wg_ref[...], NT, preferred_element_type=f32)
        bv = lax.dot_general(hv, wu_ref[...], NT, preferred_element_type=f32)
        a_ref[...] = av.astype(bf16)
        b_ref[...] = bv.astype(bf16)
        f_ref[...] = (av * _sigmoid(av) * bv).astype(bf16)

    wspec = pl.BlockSpec((None, FF_S, D_MODEL), lambda s, m: (s, 0, 0))
    ospec = pl.BlockSpec((None, tm, FF_S), lambda s, m: (s, m, 0))
    osd = SDS((N_CHIPS, T, FF_S), bf16)
    return pl.pallas_call(body, grid=(N_CHIPS, T // tm),
                          in_specs=[pl.BlockSpec((tm, D_MODEL), lambda s, m: (m, 0)), wspec, wspec],
                          out_specs=[ospec, ospec, ospec], out_shape=(osd, osd, osd), name="ffn_up",
                          compiler_params=_cp(("parallel", "parallel")))(h2, wg, wu)


def _ffn_up_bwd_w(da, db, h2):
    T = h2.shape[0]
    tk = 2048
    nk = T // tk

    def body(a_ref, b_ref, h_ref, og_ref, ou_ref, accg, accu):
        k = pl.program_id(1)
        hv = h_ref[...]
        pg = lax.dot_general(a_ref[...], hv, TN, preferred_element_type=f32)
        pu = lax.dot_general(b_ref[...], hv, TN, preferred_element_type=f32)
        _accumulate(k == 0, [(accg, pg), (accu, pu)])

        @pl.when(k == nk - 1)
        def _():
            og_ref[...] = accg[...].astype(bf16)
            ou_ref[...] = accu[...].astype(bf16)

    aspec = pl.BlockSpec((None, tk, FF_S), lambda s, k: (s, k, 0))
    ospec = pl.BlockSpec((None, FF_S, D_MODEL), lambda s, k: (s, 0, 0))
    osd = SDS((N_CHIPS, FF_S, D_MODEL), bf16)
    return pl.pallas_call(body, grid=(N_CHIPS, nk), in_specs=[aspec, aspec, pl.BlockSpec((tk, D_MODEL), lambda s, k: (k, 0))],
                          out_specs=[ospec, ospec], out_shape=(osd, osd),
                          scratch_shapes=[pltpu.VMEM((FF_S, D_MODEL), f32), pltpu.VMEM((FF_S, D_MODEL), f32)], name="d_w_gate_up",
                          compiler_params=_cp(("parallel", "arbitrary")))(da, db, h2)


def _ffn_down_bwd(dx2, wd, a, b):
    T = dx2.shape[0]
    tm = 1024

    def body(d_ref, w_ref, a_ref, b_ref, da_ref, db_ref):
        df = lax.dot_general(d_ref[...].astype(bf16), w_ref[...], NT, preferred_element_type=f32)
        av = a_ref[...].astype(f32)
        sg = _sigmoid(av)
        da_ref[...] = (df * b_ref[...].astype(f32) * (sg * (1.0 + av * (1.0 - sg)))).astype(bf16)
        db_ref[...] = (df * av * sg).astype(bf16)

    aspec = pl.BlockSpec((None, tm, FF_S), lambda m, s: (s, m, 0))
    osd = SDS((N_CHIPS, T, FF_S), bf16)
    return pl.pallas_call(body, grid=(T // tm, N_CHIPS),
                          in_specs=[pl.BlockSpec((tm, D_MODEL), lambda m, s: (m, 0)),
                                    pl.BlockSpec((None, FF_S, D_MODEL), lambda m, s: (s, 0, 0)), aspec, aspec],
                          out_specs=[aspec, aspec], out_shape=(osd, osd), name="ffn_down_bwd",
                          compiler_params=_cp(("parallel", "parallel")))(dx2, wd, a, b)


def _merge_bwd(dm, proj, gate_b, ya, yc):
    T = proj.shape[0]
    tm = 512

    def body(dm_ref, ga_ref, gc_ref, ba_ref, bc_ref, ya_ref, yc_ref, dya_ref, dyc_ref, dga_ref, dgc_ref, dba_ref, dbc_ref):
        dmv = dm_ref[...].astype(f32)
        sa = _sigmoid(ga_ref[...].astype(f32) + ba_ref[...])
        sc = _sigmoid(gc_ref[...].astype(f32) + bc_ref[...])
        dya_ref[...] = (dmv * sa).astype(bf16)
        dyc_ref[...] = (dmv * sc).astype(bf16)
        dga = dmv * ya_ref[...].astype(f32) * sa * (1.0 - sa)
        dgc = dmv * yc_ref[...].astype(f32) * sc * (1.0 - sc)
        dga_ref[...] = dga.astype(bf16)
        dgc_ref[...] = dgc.astype(bf16)
        pa = jnp.sum(dga, axis=0, keepdims=True)
        pc = jnp.sum(dgc, axis=0, keepdims=True)

        @pl.when(pl.program_id(1) == 0)
        def _():
            dba_ref[...] = pa
            dbc_ref[...] = pc

        @pl.when(pl.program_id(1) > 0)
        def _():
            dba_ref[...] += pa
            dbc_ref[...] += pc

    blk = pl.BlockSpec((tm, CB), lambda j, i: (i, j))
    vec = pl.BlockSpec((1, CB), lambda j, i: (0, j))
    big = SDS((T, D_MODEL), bf16)
    small = SDS((1, D_MODEL), f32)
    return pl.pallas_call(
        body, grid=(2, T // tm),
        in_specs=[blk, pl.BlockSpec((tm, CB), lambda j, i: (i, GA_CB + j)), pl.BlockSpec((tm, CB), lambda j, i: (i, GC_CB + j)),
                  vec, pl.BlockSpec((1, CB), lambda j, i: (0, 2 + j)), blk, blk],
        out_specs=[blk, blk, blk, blk, vec, vec], out_shape=(big, big, big, big, small, small), name="merge_bwd",
        compiler_params=_cp(("parallel", "arbitrary")))(dm, proj, proj, gate_b, gate_b, ya, yc)


CONV_TS = 512
CONV_HALO = 32
CONV_RC = 64
CONV_WIN = CONV_TS + CONV_HALO
SUBLANES = 8


def _fill_shifted(win, sh):
    for b in range(1, SUBLANES):
        sh[b - 1] = win[pl.ds(b, CONV_WIN - SUBLANES), :]


def _rows_at(win, sh, row):
    a, b = divmod(row, SUBLANES)
    if b == 0:
        return win[pl.ds(row, CONV_RC), :]
    return sh[b - 1, pl.ds(a * SUBLANES, CONV_RC), :]


def _glu_conv_fwd(proj3, w, bias, comm=None):
    B = proj3.shape[0]
    nt = SEQ // CONV_TS
    hb = CONV_TS // CONV_HALO

    def body(ua_ref, ub_ref, ha_ref, hb_ref, w_ref, b_ref, o_ref, win, sh):
        i = pl.program_id(2)
        c0 = ua_ref[...].astype(f32) * _sigmoid(ub_ref[...].astype(f32))
        halo = ha_ref[...].astype(f32) * _sigmoid(hb_ref[...].astype(f32))
        win[0:CONV_HALO, :] = jnp.where(i > 0, halo, 0.0)
        win[CONV_HALO:, :] = c0
        _fill_shifted(win, sh)
        for r0 in range(0, CONV_TS, CONV_RC):
            acc = jnp.zeros((CONV_RC, CB), f32) + b_ref[...]
            for k in range(CONV_K):
                acc = acc + _rows_at(win, sh, r0 + CONV_HALO - (CONV_K - 1) + k) * w_ref[k:k + 1, :]
            o_ref[r0:r0 + CONV_RC, :] = acc

    def cur(cb):
        return pl.BlockSpec((None, CONV_TS, CB), lambda b, j, i: (b, i, cb + j))

    def prev(cb):
        return pl.BlockSpec((None, CONV_HALO, CB), lambda b, j, i: (b, jnp.maximum(i * hb - 1, 0), cb + j))

    return _launch(
        body, name="glu_conv_fwd", grid=(B, 2, nt),
        in_specs=[cur(UA_CB), cur(UB_CB), prev(UA_CB), prev(UB_CB),
                  pl.BlockSpec((CONV_K, CB), lambda b, j, i: (0, j)), pl.BlockSpec((1, CB), lambda b, j, i: (0, j))],
        out_specs=pl.BlockSpec((None, CONV_TS, CB), lambda b, j, i: (b, i, j)),
        out_shape=SDS((B, SEQ, D_MODEL), f32), args=(proj3, proj3, proj3, proj3, w, bias),
        scratch_shapes=[pltpu.VMEM((CONV_WIN, CB), f32), pltpu.VMEM((SUBLANES - 1, CONV_WIN - SUBLANES, CB), f32)],
        sem=("parallel", "parallel", "parallel"), comm=comm)


def _glu_conv_bwd(dc1, proj3, w, comm=None):
    B = proj3.shape[0]
    nt = SEQ // CONV_TS
    hb = CONV_TS // CONV_HALO

    def body(d_ref, dn_ref, ua_ref, ub_ref, ha_ref, hb_ref, w_ref, dua_ref, dub_ref, dw_ref, db_ref, winc, wind, accw, shc, shd):
        b = pl.program_id(1)
        i = pl.program_id(2)
        first = jnp.logical_and(b == 0, i == 0)
        last = jnp.logical_and(b == B - 1, i == nt - 1)

        @pl.when(first)
        def _():
            accw[...] = jnp.zeros_like(accw)
            db_ref[...] = jnp.zeros_like(db_ref)

        halo = ha_ref[...].astype(f32) * _sigmoid(hb_ref[...].astype(f32))
        winc[0:CONV_HALO, :] = jnp.where(i > 0, halo, 0.0)
        winc[CONV_HALO:, :] = ua_ref[...].astype(f32) * _sigmoid(ub_ref[...].astype(f32))
        wind[0:CONV_TS, :] = d_ref[...]
        wind[CONV_TS:, :] = jnp.where(i < nt - 1, dn_ref[...], 0.0)
        db_ref[...] += jnp.sum(d_ref[...], axis=0, keepdims=True)
        _fill_shifted(winc, shc)
        _fill_shifted(wind, shd)
        for r0 in range(0, CONV_TS, CONV_RC):
            dc0 = jnp.zeros((CONV_RC, CB), f32)
            for k in range(CONV_K):
                dc0 = dc0 + _rows_at(wind, shd, r0 + (CONV_K - 1) - k) * w_ref[k:k + 1, :]
            uav = ua_ref[r0:r0 + CONV_RC, :].astype(f32)
            sg = _sigmoid(ub_ref[r0:r0 + CONV_RC, :].astype(f32))
            dua_ref[r0:r0 + CONV_RC, :] = (dc0 * sg).astype(bf16)
            dub_ref[r0:r0 + CONV_RC, :] = (dc0 * uav * sg * (1.0 - sg)).astype(bf16)
            dv = wind[r0:r0 + CONV_RC, :]
            for k in range(CONV_K):
                prod = dv * _rows_at(winc, shc, r0 + CONV_HALO - (CONV_K - 1) + k)
                accw[k] += jnp.sum(prod.reshape(CONV_RC // 8, 8, CB), axis=0)

        @pl.when(last)
        def _():
            for k in range(CONV_K):
                dw_ref[k:k + 1, :] = jnp.sum(accw[k], axis=0, keepdims=True)
            dw_ref[CONV_K:, :] = jnp.zeros((CONV_HALO - CONV_K, CB), f32)

    def cur(cb):
        return pl.BlockSpec((None, CONV_TS, CB), lambda j, b, i: (b, i, cb + j))

    def prev(cb):
        return pl.BlockSpec((None, CONV_HALO, CB), lambda j, b, i: (b, jnp.maximum(i * hb - 1, 0), cb + j))

    nxt = pl.BlockSpec((None, CONV_HALO, CB), lambda j, b, i: (b, jnp.minimum((i + 1) * hb, SEQ // CONV_HALO - 1), j))
    big = SDS((B, SEQ, D_MODEL), bf16)
    return _launch(
        body, name="glu_conv_bwd", grid=(2, B, nt),
        in_specs=[cur(0), nxt, cur(UA_CB), cur(UB_CB), prev(UA_CB), prev(UB_CB), pl.BlockSpec((CONV_K, CB), lambda j, b, i: (0, j))],
        out_specs=[cur(0), cur(0), pl.BlockSpec((CONV_HALO, CB), lambda j, b, i: (0, j)), pl.BlockSpec((1, CB), lambda j, b, i: (0, j))],
        out_shape=(big, big, SDS((CONV_HALO, D_MODEL), f32), SDS((1, D_MODEL), f32)),
        args=(dc1, dc1, proj3, proj3, proj3, proj3, w),
        scratch_shapes=[pltpu.VMEM((CONV_WIN, CB), f32), pltpu.VMEM((CONV_WIN, CB), f32), pltpu.VMEM((CONV_K, SUBLANES, CB), f32),
                        pltpu.VMEM((SUBLANES - 1, CONV_WIN - SUBLANES, CB), f32),
                        pltpu.VMEM((SUBLANES - 1, CONV_WIN - SUBLANES, CB), f32)],
        sem=("parallel", "arbitrary", "arbitrary"), comm=comm)


def _band(first, dil):
    kw = Q_BLOCK if first else 2 * Q_BLOCK
    qi = lax.broadcasted_iota(jnp.int32, (Q_BLOCK, kw), 0)
    kj = lax.broadcasted_iota(jnp.int32, (Q_BLOCK, kw), 1)
    rel = qi - kj + (0 if first else Q_BLOCK)
    valid = jnp.logical_and(rel >= 0, rel <= Q_BLOCK)
    return valid, rel.astype(f32) * float(dil)


def _bias(first, dil, slope):
    valid, dist = _band(first, dil)
    return jnp.where(valid, -slope * dist, NEG)


def _scores(q, k, bias):
    return lax.dot_general(q, k, NT, preferred_element_type=f32) * (HEAD_DIM ** -0.5) + bias


def _pair_cols(hp):
    return slice(hp * LANES, (hp + 1) * LANES)


def _half(x2, e):
    lane = lax.broadcasted_iota(jnp.int32, (1, LANES), 1)
    keep = (lane < HEAD_DIM) if e == 0 else (lane >= HEAD_DIM)
    return jnp.where(keep, x2, jnp.zeros_like(x2))


def _attn_fwd(g, q4, k4, v4, qcb, kcb, vcb, comm=None):
    _, dil = GROUPS[g]
    B, r, L, _ = q4.shape
    nb = L // Q_BLOCK
    slopes = [float(s) for s in _SLOPES[g]]

    def body(q_ref, k_ref, v_ref, o_ref, lse_ref, bias, s_scr, p_scr):
        lane = lax.broadcasted_iota(jnp.int32, (Q_BLOCK, LANES), 1)
        if nb > 1:
            for h in range(HEADS):
                bias[h] = _bias(False, dil, slopes[h])

        def block(n, first):
            q0 = 0 if first else pl.multiple_of(n * Q_BLOCK, Q_BLOCK)
            k0 = 0 if first else pl.multiple_of((n - 1) * Q_BLOCK, Q_BLOCK)
            kw = Q_BLOCK if first else 2 * Q_BLOCK
            for hp in range(HEADS // 2):
                q2 = q_ref[pl.ds(q0, Q_BLOCK), _pair_cols(hp)]
                k2 = k_ref[pl.ds(k0, kw), _pair_cols(hp)]
                for e in range(2):
                    h = 2 * hp + e
                    b_h = _bias(True, dil, slopes[h]) if first else bias[h]
                    s_scr[h, :, :kw] = _scores(_half(q2, e), k2, b_h)
            st = jnp.zeros((Q_BLOCK, LANES), f32)
            dens = jnp.ones((Q_BLOCK, LANES), f32)
            for h in range(HEADS):
                s = s_scr[h, :, :kw]
                m = jnp.max(s, axis=-1, keepdims=True)
                p = jnp.exp(s - m)
                den = jnp.sum(p, axis=-1, keepdims=True)
                p_scr[h, :, :kw] = p.astype(bf16)
                st = jnp.where(lane == h, m + jnp.log(den), st)
                dens = jnp.where(lane == h, den, dens)
            lse_ref[pl.ds(q0, Q_BLOCK), :] = st
            inv = 1.0 / dens
            for hp in range(HEADS // 2):
                v2 = v_ref[pl.ds(k0, kw), _pair_cols(hp)]
                o2 = sum(jnp.dot(p_scr[2 * hp + e, :, :kw], _half(v2, e), preferred_element_type=f32) * inv[:, 2 * hp + e:2 * hp + e + 1]
                         for e in range(2))
                o_ref[pl.ds(q0, Q_BLOCK), _pair_cols(hp)] = o2.astype(bf16)

        block(0, True)
        if nb > 1:
            def step(n, carry):
                block(n, False)
                return carry
            lax.fori_loop(1, nb, step, 0)

    def spec(cb):
        return pl.BlockSpec((None, None, L, GW), lambda b, c: (b, c, 0, cb))

    return _launch(
        body, name=f"attn_fwd_g{g}", grid=(B, r), in_specs=[spec(qcb), spec(kcb), spec(vcb)],
        out_specs=[spec(0), pl.BlockSpec((None, None, L, LANES), lambda b, c: (b, c, 0, 0))],
        out_shape=(SDS((B, r, L, GW), bf16), SDS((B, r, L, LANES), f32)), args=(q4, k4, v4),
        scratch_shapes=[pltpu.VMEM((HEADS, Q_BLOCK, 2 * Q_BLOCK), f32), pltpu.VMEM((HEADS, Q_BLOCK, 2 * Q_BLOCK), f32),
                        pltpu.VMEM((HEADS, Q_BLOCK, 2 * Q_BLOCK), bf16)],
        sem=("parallel", "parallel"), comm=comm)


def _attn_bwd(g, q4, k4, v4, qcb, kcb, vcb, do4, lse4, dl4, comm=None):
    _, dil = GROUPS[g]
    B, r, L, _ = q4.shape
    nb = L // Q_BLOCK
    slopes = [float(s) for s in _SLOPES[g]]
    scale = HEAD_DIM ** -0.5

    def body(q_ref, k_ref, v_ref, do_ref, lse_ref, dl_ref, dq_ref, dk_ref, dv_ref, dk_acc, dv_acc, bias, s_scr, dp_scr, p_scr, ds_scr):
        dk_acc[...] = jnp.zeros_like(dk_acc)
        dv_acc[...] = jnp.zeros_like(dv_acc)
        if nb > 1:
            for h in range(HEADS):
                bias[h] = _bias(False, dil, slopes[h])

        def block(n, first):
            q0 = 0 if first else pl.multiple_of(n * Q_BLOCK, Q_BLOCK)
            k0 = 0 if first else pl.multiple_of((n - 1) * Q_BLOCK, Q_BLOCK)
            kw = Q_BLOCK if first else 2 * Q_BLOCK
            for hp in range(HEADS // 2):
                q2 = q_ref[pl.ds(q0, Q_BLOCK), _pair_cols(hp)]
                k2 = k_ref[pl.ds(k0, kw), _pair_cols(hp)]
                v2 = v_ref[pl.ds(k0, kw), _pair_cols(hp)]
                do2 = do_ref[pl.ds(q0, Q_BLOCK), _pair_cols(hp)]
                for e in range(2):
                    h = 2 * hp + e
                    b_h = _bias(True, dil, slopes[h]) if first else bias[h]
                    s_scr[h, :, :kw] = _scores(_half(q2, e), k2, b_h)
                    dp_scr[h, :, :kw] = lax.dot_general(_half(do2, e), v2, NT, preferred_element_type=f32)
            for h in range(HEADS):
                p = jnp.exp(s_scr[h, :, :kw] - lse_ref[pl.ds(q0, Q_BLOCK), h:h + 1])
                p_scr[h, :, :kw] = p.astype(bf16)
                ds_scr[h, :, :kw] = (p * (dp_scr[h, :, :kw] - dl_ref[pl.ds(q0, Q_BLOCK), h:h + 1])).astype(bf16)
            for hp in range(HEADS // 2):
                cols = _pair_cols(hp)
                q2 = q_ref[pl.ds(q0, Q_BLOCK), cols]
                k2 = k_ref[pl.ds(k0, kw), cols]
                do2 = do_ref[pl.ds(q0, Q_BLOCK), cols]
                ds = [ds_scr[2 * hp + e, :, :kw] for e in range(2)]
                dq2 = sum(jnp.dot(ds[e], _half(k2, e), preferred_element_type=f32) for e in range(2))
                dq_ref[pl.ds(q0, Q_BLOCK), cols] = (dq2 * scale).astype(bf16)
                dk2 = sum(lax.dot_general(ds[e], _half(q2, e), TN, preferred_element_type=f32) for e in range(2))
                dk_acc[pl.ds(k0, kw), cols] += dk2 * scale
                dv2 = sum(lax.dot_general(p_scr[2 * hp + e, :, :kw], _half(do2, e), TN, preferred_element_type=f32) for e in range(2))
                dv_acc[pl.ds(k0, kw), cols] += dv2

        block(0, True)
        if nb > 1:
            def step(n, carry):
                block(n, False)
                return carry
            lax.fori_loop(1, nb, step, 0)
        dk_ref[...] = dk_acc[...].astype(bf16)
        dv_ref[...] = dv_acc[...].astype(bf16)

    def spec(cb):
        return pl.BlockSpec((None, None, L, GW), lambda b, c: (b, c, 0, cb))

    st = pl.BlockSpec((None, None, L, LANES), lambda b, c: (b, c, 0, 0))
    osd = SDS((B, r, L, GW), bf16)
    return _launch(
        body, name=f"attn_bwd_g{g}", grid=(B, r), in_specs=[spec(qcb), spec(kcb), spec(vcb), spec(0), st, st],
        out_specs=[spec(0), spec(0), spec(0)], out_shape=(osd, osd, osd), args=(q4, k4, v4, do4, lse4, dl4),
        scratch_shapes=[pltpu.VMEM((L, GW), f32), pltpu.VMEM((L, GW), f32)]
        + [pltpu.VMEM((HEADS, Q_BLOCK, 2 * Q_BLOCK), f32)] * 3 + [pltpu.VMEM((HEADS, Q_BLOCK, 2 * Q_BLOCK), bf16)] * 2,
        sem=("parallel", "parallel"), comm=comm)


RT = 512
RCH = GW // LANES
DILS = tuple(d for _, d in GROUPS[1:])


def _res_spec(r, width):
    return pl.BlockSpec((None, r, RT // r, width), lambda b, i, *_: (b, 0, i, 0))


def _tok_spec(width, cb=0):
    return pl.BlockSpec((None, RT, width), lambda b, i, *_: (b, i, cb))


def _to_residues(res_ref, scr, r, width):
    for c in range(r):
        for jj in range(width // LANES):
            res_ref[c, :, jj * LANES:(jj + 1) * LANES] = scr[jj, pl.ds(c, RT // r, stride=r), :].astype(res_ref.dtype)


def _from_residues(scr, res_ref, r, width):
    for c in range(r):
        for jj in range(width // LANES):
            scr[jj, pl.ds(c, RT // r, stride=r), :] = res_ref[c, :, jj * LANES:(jj + 1) * LANES].astype(f32)


def _qkv_to_residues(proj3, g):
    r = GROUPS[g][1]
    B = proj3.shape[0]

    def body(q_ref, k_ref, v_ref, o_ref, scr):
        for p, x_ref in enumerate((q_ref, k_ref, v_ref)):
            x = x_ref[...].astype(f32)
            for jj in range(RCH):
                scr[p * RCH + jj] = x[:, jj * LANES:(jj + 1) * LANES]
        _to_residues(o_ref, scr, r, ATTN_W)

    return pl.pallas_call(
        body, grid=(B, SEQ // RT), in_specs=[_tok_spec(GW, 3 * p + g) for p in range(3)], out_specs=_res_spec(r, ATTN_W),
        out_shape=SDS((B, r, SEQ // r, ATTN_W), bf16), scratch_shapes=[pltpu.VMEM((3 * RCH, RT, LANES), f32)],
        name=f"qkv_to_residues_g{g}", compiler_params=_cp(("parallel", "parallel")))(proj3, proj3, proj3)


def _attn_mix(o0, l0, o_res, l_res):
    B = o0.shape[0]

    def body(o0_ref, l0_ref, o1_ref, o2_ref, l1_ref, l2_ref, y_ref, lt_ref, lt1_ref, lt2_ref, so, sl):
        for gi, (o_ref, l_ref, r) in enumerate(((o1_ref, l1_ref, DILS[0]), (o2_ref, l2_ref, DILS[1]))):
            _from_residues(so.at[gi], o_ref, r, GW)
            _from_residues(sl.at[gi:gi + 1], l_ref, r, LANES)
        ls = [l0_ref[...], sl[0], sl[1]]
        m = functools.reduce(jnp.maximum, ls)
        ws = [jnp.exp(l - m) for l in ls]
        den = ws[0] + ws[1] + ws[2]
        alphas = [w / den for w in ws]
        lt = m + jnp.log(den)
        lt_ref[...] = lt
        sl[2] = lt
        _to_residues(lt1_ref, sl.at[2:3], DILS[0], LANES)
        _to_residues(lt2_ref, sl.at[2:3], DILS[1], LANES)
        for h in range(HEADS):
            cols = slice(h * HEAD_DIM, (h + 1) * HEAD_DIM)
            jj, lo = divmod(h * HEAD_DIM, LANES)
            acc = alphas[0][:, h:h + 1] * o0_ref[:, cols].astype(f32)
            for gi in range(2):
                acc = acc + alphas[gi + 1][:, h:h + 1] * so[gi, jj, :, lo:lo + HEAD_DIM]
            y_ref[:, cols] = acc.astype(bf16)

    in_specs = [_tok_spec(GW), _tok_spec(LANES), _res_spec(DILS[0], GW), _res_spec(DILS[1], GW), _res_spec(DILS[0], LANES), _res_spec(DILS[1], LANES)]
    out_specs = [_tok_spec(GW), _tok_spec(LANES), _res_spec(DILS[0], LANES), _res_spec(DILS[1], LANES)]
    return pl.pallas_call(
        body, grid=(B, SEQ // RT), in_specs=in_specs, out_specs=out_specs,
        out_shape=(SDS((B, SEQ, GW), bf16), SDS((B, SEQ, LANES), f32)) + tuple(SDS((B, r, SEQ // r, LANES), f32) for r in DILS),
        scratch_shapes=[pltpu.VMEM((2, RCH, RT, LANES), f32), pltpu.VMEM((3, RT, LANES), f32)],
        name="attn_mix", compiler_params=_cp(("parallel", "parallel")))(o0, l0, *o_res, *l_res)


def _attn_delta(dmix, mix):
    B = dmix.shape[0]

    def body(d_ref, y_ref, dl_ref, dl1_ref, dl2_ref, dm1_ref, dm2_ref, sx, sd):
        lane = lax.broadcasted_iota(jnp.int32, (RT, LANES), 1)
        acc = jnp.zeros((RT, LANES), f32)
        dv = d_ref[...].astype(f32)
        for jj in range(RCH):
            sx[jj] = dv[:, jj * LANES:(jj + 1) * LANES]
        for h in range(HEADS):
            cols = slice(h * HEAD_DIM, (h + 1) * HEAD_DIM)
            dl = jnp.sum(dv[:, cols] * y_ref[:, cols].astype(f32), axis=-1, keepdims=True)
            acc = jnp.where(lane == h, dl, acc)
        dl_ref[...] = acc
        sd[0] = acc
        _to_residues(dl1_ref, sd, DILS[0], LANES)
        _to_residues(dl2_ref, sd, DILS[1], LANES)
        _to_residues(dm1_ref, sx, DILS[0], GW)
        _to_residues(dm2_ref, sx, DILS[1], GW)

    return pl.pallas_call(
        body, grid=(B, SEQ // RT), in_specs=[_tok_spec(GW), _tok_spec(GW)],
        out_specs=[_tok_spec(LANES), _res_spec(DILS[0], LANES), _res_spec(DILS[1], LANES), _res_spec(DILS[0], GW), _res_spec(DILS[1], GW)],
        out_shape=(SDS((B, SEQ, LANES), f32),) + tuple(SDS((B, r, SEQ // r, LANES), f32) for r in DILS)
        + tuple(SDS((B, r, SEQ // r, GW), bf16) for r in DILS),
        scratch_shapes=[pltpu.VMEM((RCH, RT, LANES), f32), pltpu.VMEM((1, RT, LANES), f32)],
        name="attn_delta", compiler_params=_cp(("parallel", "parallel")))(dmix, mix)


N_CB = IN_W // CB


def _assemble_dproj(dqkv, dua, dub, dga, dgc):
    B = dua.shape[0]
    ng = len(GROUPS)
    flat = [dqkv[g][p] for p in range(3) for g in range(ng)]
    wide = [dua, dub, dga, dgc]

    def body(*refs):
        srcs, wides, o_ref, scr = refs[:3 * ng], refs[3 * ng:3 * ng + 4], refs[3 * ng + 4], refs[3 * ng + 5]
        for jv in range(3 * ng):
            g = jv % ng
            if g == 0:
                o_ref[:, jv * GW:(jv + 1) * GW] = srcs[jv][...]
            else:
                _from_residues(scr, srcs[jv], GROUPS[g][1], GW)
                for jj in range(RCH):
                    o_ref[:, jv * GW + jj * LANES:jv * GW + (jj + 1) * LANES] = scr[jj].astype(bf16)
        for wv in range(4):
            lo = 3 * ATTN_W + wv * D_MODEL
            o_ref[:, lo:lo + D_MODEL] = wides[wv][...]

    in_specs = [_tok_spec(GW) if (jv % ng) == 0 else _res_spec(GROUPS[jv % ng][1], GW) for jv in range(3 * ng)]
    in_specs += [_tok_spec(D_MODEL)] * 4
    return pl.pallas_call(
        body, grid=(B, SEQ // RT), in_specs=in_specs, out_specs=_tok_spec(IN_W),
        out_shape=SDS((B, SEQ, IN_W), bf16), scratch_shapes=[pltpu.VMEM((RCH, RT, LANES), f32)],
        name="assemble_dproj", compiler_params=_cp(("parallel", "parallel")))(*flat, *wide)


def _same_shape_groups(names, arrays):
    groups = {}
    for n, a in zip(names, arrays):
        groups.setdefault(a.shape, ([], []))
        groups[a.shape][0].append(n)
        groups[a.shape][1].append(a)
    return list(groups.values())


def _add_pairs(parts, gots, core, name):
    k = len(parts)
    n, h, C = gots[0].shape

    def body(c_ref, *refs):
        for i in range(k):
            refs[2 * k + i][...] = (refs[i][...].astype(f32) + refs[k + i][...].astype(f32)).astype(bf16)

    blk = pl.BlockSpec((None, h, C), lambda s, c_ref: (s, 0, 0))
    own = pl.BlockSpec((None, h, C), lambda s, c_ref: (s, c_ref[0], 0))
    spec = pltpu.PrefetchScalarGridSpec(num_scalar_prefetch=1, grid=(n,), in_specs=[own] * k + [blk] * k, out_specs=[blk] * k)
    return pl.pallas_call(body, grid_spec=spec, out_shape=tuple(SDS((n, h, C), bf16) for _ in range(k)), name=name,
                          compiler_params=_cp(("parallel",)))(core, *parts, *gots)


def _sum4(sums, gots, chip, core, name):
    k = len(sums)
    _, h, C = sums[0].shape

    def body(s_ref, c_ref, *refs):
        for i in range(k):
            q_ref = refs[k + i]
            t = refs[i][...].astype(f32) + q_ref[0].astype(f32)
            t = t + q_ref[1].astype(f32)
            refs[2 * k + i][...] = t + q_ref[2].astype(f32)

    spec = pltpu.PrefetchScalarGridSpec(
        num_scalar_prefetch=2, grid=(1,),
        in_specs=[pl.BlockSpec((None, h, C), lambda i, s_ref, c_ref: (s_ref[0], 0, 0))] * k
        + [pl.BlockSpec((N_CHIPS - 1, h, C), lambda i, s_ref, c_ref: (0, 0, 0))] * k,
        out_specs=[pl.BlockSpec((h, C), lambda i, s_ref, c_ref: (c_ref[0], 0))] * k)
    return pl.pallas_call(body, grid_spec=spec, out_shape=tuple(SDS((2 * h, C), f32) for _ in range(k)), name=name,
                          compiler_params=_cp(("arbitrary",)))(chip, core, *sums, *gots)


def _adamw(ws, gs, ms, vs, name):
    k = len(ws)
    R, C = ws[0].shape
    rt = R
    for cand in (512, 256, 128, 64, 32, 16, 8):
        if R % cand == 0 and cand * C * 4 <= 2 ** 21:
            rt = cand
            break
    c1 = 1.0 / (1.0 - B1 ** STEP)
    c2 = 1.0 / (1.0 - B2 ** STEP)

    def body(*refs):
        for i in range(k):
            w_ref, g_ref, m_ref, v_ref = (refs[j * k + i] for j in range(4))
            d_ref, nm_ref, nv_ref = (refs[(4 + j) * k + i] for j in range(3))
            gv = g_ref[...]
            nm = B1 * m_ref[...] + (1.0 - B1) * gv
            nv = B2 * v_ref[...] + (1.0 - B2) * (gv * gv)
            nm_ref[...] = nm
            nv_ref[...] = nv
            d_ref[...] = -LR * ((nm * c1) / (jnp.sqrt(nv * c2) + ADAM_EPS) + WD * w_ref[...])

    blk = pl.BlockSpec((rt, C), lambda i: (i, 0))
    sd = SDS((R, C), f32)
    res = pl.pallas_call(body, grid=(R // rt,), in_specs=[blk] * (4 * k), out_specs=[blk] * (3 * k), out_shape=(sd,) * (3 * k),
                         name=name, compiler_params=_cp(("parallel",)))(*ws, *gs, *ms, *vs)
    return res[:k], res[k:2 * k], res[2 * k:]


def _coords():
    return lax.axis_index("x"), lax.axis_index("y"), lax.axis_index("c")


def _other_chips(x, y):
    return [(1 - x, y), (x, 1 - y), (1 - x, 1 - y)]


def _allgather_weights(shards, slots):
    n = len(shards)
    n_slot = 12

    def body(*refs):
        ins, outs = refs[:n], refs[2 * n:3 * n]
        send, recv = refs[3 * n:]
        x, y, c = _coords()
        me, cx, cy, cd = 2 * x + y, 2 * (1 - x) + y, 2 * x + (1 - y), 2 * (1 - x) + (1 - y)
        dev_x, dev_y, dev_s = (1 - x, y, c), (x, 1 - y, c), (x, y, 1 - c)
        started = []

        def copy(w, slot, src, dst, dev):
            return pltpu.make_async_remote_copy(src_ref=src, dst_ref=dst, send_sem=send.at[w, slot], recv_sem=recv.at[w, slot],
                                                device_id=dev, device_id_type=MESH)

        def go(cp):
            cp.start()
            started.append(cp)

        for w in range(n):
            q = shards[w].shape[0] // 4
            rows = [pl.ds(c * 2 * q + k * q, q) for k in range(2)]
            theirs = [pl.ds((1 - c) * 2 * q + k * q, q) for k in range(2)]
            own = [(ins[w].at[r, :], outs[w].at[me, r, :]) for r in rows]
            go(copy(w, 0, *own[0], dev_x))
            go(copy(w, 2, *own[1], dev_y))
            go(copy(w, 1, *own[1], dev_x))
            go(copy(w, 3, *own[0], dev_y))
            arrivals = [(0, cx, 0, (4, dev_y)), (2, cy, 1, (5, dev_x)), (1, cx, 1, None), (3, cy, 0, None), (4, cd, 0, None), (5, cd, 1, None)]
            for k, (slot, chip, quarter, onward) in enumerate(arrivals):
                blk = outs[w].at[chip, rows[quarter], :]
                copy(w, slot, blk, blk, dev_s).wait_recv()
                if onward is not None:
                    go(copy(w, onward[0], blk, blk, onward[1]))
                go(copy(w, 6 + k, blk, blk, dev_s))
            for k, (slot, chip, quarter, onward) in enumerate(arrivals):
                blk = outs[w].at[chip, theirs[quarter], :]
                copy(w, 6 + k, blk, blk, dev_s).wait_recv()
        for cp in started:
            cp.wait_send()

    return pl.pallas_call(
        body, in_specs=[HBM] * (2 * n), out_specs=[HBM] * n,
        out_shape=tuple(SDS((N_CHIPS,) + s.shape, s.dtype) for s in shards),
        input_output_aliases={n + w: w for w in range(n)},
        scratch_shapes=[pltpu.SemaphoreType.DMA((n, n_slot))] * 2,
        name="allgather_weights", compiler_params=pltpu.CompilerParams(has_side_effects=True))(*shards, *slots)


def _x_pair_exchange(parts):
    n = len(parts)
    halves = [p.shape[1] // 2 for p in parts]

    def copies(ins, outs, sems):
        send, recv = sems
        x, y, c = _coords()
        return [pltpu.make_async_remote_copy(src_ref=ins[w].at[:, pl.ds((1 - c) * halves[w], halves[w]), :], dst_ref=outs[w],
                                             send_sem=send.at[w], recv_sem=recv.at[w], device_id=(x, y, 1 - c), device_id_type=MESH)
                for w in range(n)]

    def start(ins, outs, sems):
        for cp in copies(ins, outs, sems):
            cp.start()

    def wait(ins, outs, sems):
        for cp in copies(ins, outs, sems):
            cp.wait()

    return _Exchange(ins=list(parts), out_shape=[SDS((N_CHIPS, p.shape[1] // 2, p.shape[2]), p.dtype) for p in parts],
                     scratch=[pltpu.SemaphoreType.DMA((n,))] * 2, aliases={}, start=start, wait=wait)


def _x_chip_exchange(sums):
    n = len(sums)

    def copies(ins, outs, sems):
        send, recv = sems
        x, y, c = _coords()
        return [pltpu.make_async_remote_copy(src_ref=ins[w].at[2 * px + py], dst_ref=outs[w].at[j], send_sem=send.at[w, j],
                                             recv_sem=recv.at[w, j], device_id=(px, py, c), device_id_type=MESH)
                for w in range(n) for j, (px, py) in enumerate(_other_chips(x, y))]

    def start(ins, outs, sems):
        for cp in copies(ins, outs, sems):
            cp.start()

    def wait(ins, outs, sems):
        for cp in copies(ins, outs, sems):
            cp.wait()

    return _Exchange(ins=list(sums), out_shape=[SDS((N_CHIPS - 1,) + s.shape[1:], s.dtype) for s in sums],
                     scratch=[pltpu.SemaphoreType.DMA((n, 3)), pltpu.SemaphoreType.DMA((n, 3))], aliases={}, start=start, wait=wait)


def _x_gather_ici(shards, slots):
    n = len(shards)
    halves = [s.shape[0] // 2 for s in shards]

    def copies(ins, outs, sems):
        send, recv = sems
        x, y, c = _coords()
        me = 2 * x + y
        out = []
        for w in range(n):
            mine = pl.ds(c * halves[w], halves[w])
            for j, (px, py) in enumerate(_other_chips(x, y)):
                snd = pltpu.make_async_remote_copy(src_ref=ins[w].at[mine, :], dst_ref=outs[w].at[me, mine, :], send_sem=send.at[w, j],
                                                   recv_sem=recv.at[w, j], device_id=(px, py, c), device_id_type=MESH)
                got = outs[w].at[2 * px + py, mine, :]
                rcv = pltpu.make_async_remote_copy(src_ref=got, dst_ref=got, send_sem=send.at[w, j], recv_sem=recv.at[w, j],
                                                   device_id=(px, py, c), device_id_type=MESH)
                out.append((snd, rcv))
        return out

    def start(ins, outs, sems):
        for snd, _ in copies(ins, outs, sems):
            snd.start()

    def wait(ins, outs, sems):
        for snd, rcv in copies(ins, outs, sems):
            rcv.wait_recv()
            snd.wait_send()

    return _Exchange(ins=list(shards) + list(slots), out_shape=[SDS(s.shape, s.dtype) for s in slots],
                     scratch=[pltpu.SemaphoreType.DMA((n, 3)), pltpu.SemaphoreType.DMA((n, 3))],
                     aliases={n + w: w for w in range(n)}, start=start, wait=wait)


def _x_gather_fwd(bufs):
    n = len(bufs)
    halves = [b.shape[1] // 2 for b in bufs]

    def copies(ins, outs, sems):
        send, recv = sems
        x, y, c = _coords()
        out = []
        for w in range(n):
            for j, (px, py) in enumerate(_other_chips(x, y)):
                mine = outs[w].at[2 * px + py, pl.ds(c * halves[w], halves[w]), :]
                theirs = outs[w].at[2 * px + py, pl.ds((1 - c) * halves[w], halves[w]), :]
                snd = pltpu.make_async_remote_copy(src_ref=mine, dst_ref=mine, send_sem=send.at[w, j], recv_sem=recv.at[w, j],
                                                   device_id=(x, y, 1 - c), device_id_type=MESH)
                rcv = pltpu.make_async_remote_copy(src_ref=theirs, dst_ref=theirs, send_sem=send.at[w, j], recv_sem=recv.at[w, j],
                                                   device_id=(x, y, 1 - c), device_id_type=MESH)
                out.append((snd, rcv))
        return out

    def start(ins, outs, sems):
        for snd, _ in copies(ins, outs, sems):
            snd.start()

    def wait(ins, outs, sems):
        for snd, rcv in copies(ins, outs, sems):
            rcv.wait_recv()
            snd.wait_send()

    return _Exchange(ins=list(bufs), out_shape=[SDS(b.shape, b.dtype) for b in bufs],
                     scratch=[pltpu.SemaphoreType.DMA((n, 3)), pltpu.SemaphoreType.DMA((n, 3))],
                     aliases={w: w for w in range(n)}, start=start, wait=wait)


def _final_exchange(bufs, vec):
    n = len(bufs)
    R = vec.shape[0]
    nd = 8

    def body(*refs):
        v_ref = refs[n]
        outs = refs[n + 1:2 * n + 1]
        o_ref, sbuf, send, recv, ssend, srecv = refs[2 * n + 1:]
        x, y, c = _coords()
        me = 4 * x + 2 * y + c
        sbuf[me] = v_ref[...]
        small = []
        for k in range(1, nd):
            kx, ky, kc = (k >> 2) & 1, (k >> 1) & 1, k & 1
            tx = x + kx - 2 * x * kx
            ty = y + ky - 2 * y * ky
            tc = c + kc - 2 * c * kc
            cp = pltpu.make_async_remote_copy(src_ref=v_ref, dst_ref=sbuf.at[me], send_sem=ssend.at[k], recv_sem=srecv.at[k],
                                              device_id=(tx, ty, tc), device_id_type=MESH)
            cp.start()
            small.append((cp, 4 * tx + 2 * ty + tc))
        cps = []
        for w in range(n):
            h = bufs[w].shape[0] // 2
            rows = outs[w].at[pl.ds(c * h, h), :]
            rc = pltpu.make_async_remote_copy(src_ref=rows, dst_ref=rows, send_sem=send.at[w], recv_sem=recv.at[w],
                                              device_id=(x, y, 1 - c), device_id_type=MESH)
            rc.start()
            cps.append(rc)
        for k, (cp, src) in zip(range(1, nd), small):
            pltpu.make_async_remote_copy(src_ref=v_ref, dst_ref=sbuf.at[src], send_sem=ssend.at[k], recv_sem=srecv.at[k],
                                         device_id=(x, y, c), device_id_type=MESH).wait_recv()
        for cp, _ in small:
            cp.wait_send()
        acc = sbuf[0]
        for d in range(1, nd):
            acc = acc + sbuf[d]
        o_ref[...] = acc
        for w, rc in enumerate(cps):
            h = bufs[w].shape[0] // 2
            other = outs[w].at[pl.ds((1 - c) * h, h), :]
            pltpu.make_async_remote_copy(src_ref=other, dst_ref=other, send_sem=send.at[w], recv_sem=recv.at[w],
                                         device_id=(x, y, 1 - c), device_id_type=MESH).wait_recv()
            rc.wait_send()

    vm = pl.BlockSpec(memory_space=pltpu.VMEM)
    res = pl.pallas_call(body, in_specs=[HBM] * n + [vm], out_specs=[HBM] * n + [vm],
                         out_shape=tuple(SDS(a.shape, a.dtype) for a in bufs) + (SDS((R, LANES), f32),),
                         input_output_aliases={w: w for w in range(n)},
                         scratch_shapes=[pltpu.VMEM((nd, R, LANES), f32)] + [pltpu.SemaphoreType.DMA((n,))] * 2
                         + [pltpu.SemaphoreType.DMA((nd,))] * 2, name="final_exchange",
                         compiler_params=pltpu.CompilerParams(has_side_effects=True))(*bufs, vec)
    return res[:n], res[n]


def _rows128(a, rows):
    flat = a.reshape(-1)
    return jnp.pad(flat, (0, rows * LANES - flat.shape[0])).reshape(rows, LANES)


GATHER_1 = ("w_conv_out", "w_attn_out", "w_o", "w_ffn_gate", "w_ffn_down")
REDUCE_A = ("w_ffn_gate", "w_ffn_up", "w_ffn_down")
REDUCE_B = ("w_conv_out", "w_attn_out", "w_o")
REDUCE_C = ("w_in",)


def _step(x, target, norm1_g, gate_b, conv_w, conv_b, conv_ln_g, conv_ln_b, norm2_g, norm_f_g, w_in, shards, slots, chip1, core):
    B = x.shape[0]
    T = B * SEQ
    xf = x.reshape(T, D_MODEL)
    tf = target.reshape(T, D_MODEL)

    h, h_t = _rms_fwd(xf, norm1_g, "rms1_fwd")
    proj, got1 = _mm("in_proj", h, w_in, grid=(N_CHIPS, T // 1024),
                     a_spec=pl.BlockSpec((1024, D_MODEL), lambda s, m: (m, 0)),
                     b_spec=pl.BlockSpec((None, D_MODEL, IN_S), lambda s, m: (s, 0, 0)),
                     o_spec=pl.BlockSpec((1024, IN_S), lambda s, m: (m, s)), o_shape=(T, IN_W), o_dtype=bf16, dims=NN,
                     comm=_x_gather_ici([shards[n] for n in GATHER_1], [slots[n] for n in GATHER_1]))
    proj3 = proj.reshape(B, SEQ, IN_W)
    proj4 = proj.reshape(B, 1, SEQ, IN_W)

    qkv = [None] + [_qkv_to_residues(proj3, g) for g in range(1, len(GROUPS))]

    def qkv_args(g):
        return (proj4, proj4, proj4, 0, 3, 6) if g == 0 else (qkv[g], qkv[g], qkv[g], 0, 1, 2)

    (o4_0, l4_0), full1 = _attn_fwd(0, *qkv_args(0), comm=_x_gather_fwd(list(got1)))
    o4_1, l4_1 = _attn_fwd(1, *qkv_args(1))
    o4_2, l4_2 = _attn_fwd(2, *qkv_args(2))
    full = dict(zip(GATHER_1, full1))
    w_conv_out, w_attn_out, w_o, w_gate, w_down = (full[n] for n in GATHER_1)
    w_conv_out_f = w_conv_out.reshape(D_MODEL, D_MODEL)
    w_o_f = w_o.reshape(D_MODEL, D_MODEL)
    mix3, lse3, lse_r1, lse_r2 = _attn_mix(o4_0.reshape(B, SEQ, GW), l4_0.reshape(B, SEQ, LANES), [o4_1, o4_2], [l4_1, l4_2])
    mix = mix3.reshape(T, GW)
    y_attn = _attn_out_fwd(mix, w_attn_out)

    c1, got_up = _glu_conv_fwd(proj3, conv_w, conv_b, comm=_x_gather_ici([shards["w_ffn_up"]], [slots["w_ffn_up"]]))
    c1 = c1.reshape(T, D_MODEL)
    (c3, y_conv), (w_up,) = _conv_out_fwd(c1, conv_ln_g, conv_ln_b, w_conv_out_f, comm=_x_gather_fwd(list(got_up)))

    merged, x1, h2 = _merge_o_proj_rms(proj, gate_b, y_attn, y_conv, w_o_f, xf, norm2_g)
    fa, fb, ff = _ffn_up(h2, w_gate, w_up)
    loss, dx2, d_gf = _ffn_down_loss(ff, w_down, x1, norm_f_g.reshape(1, D_MODEL), tf)

    d_w_down = _mm("d_w_down", ff, dx2, grid=(N_CHIPS, T // 1024),
                   a_spec=pl.BlockSpec((None, 1024, FF_S), lambda s, k: (s, k, 0)),
                   b_spec=pl.BlockSpec((1024, D_MODEL), lambda s, k: (k, 0)),
                   o_spec=pl.BlockSpec((None, FF_S, D_MODEL), lambda s, k: (s, 0, 0)), o_shape=(N_CHIPS, FF_S, D_MODEL),
                   o_dtype=bf16, dims=TN, acc_shape=(FF_S, D_MODEL), k_axis=1, sem=("parallel", "arbitrary"))
    da, db = _ffn_down_bwd(dx2, w_down, fa, fb)

    d_w_gate, d_w_up = _ffn_up_bwd_w(da, db, h2)
    part = dict(w_ffn_gate=d_w_gate, w_ffn_up=d_w_up, w_ffn_down=d_w_down)

    def pair_sums(names, got):
        out = {}
        for ns, gs in _same_shape_groups(names, got):
            out.update(zip(ns, _add_pairs([part[n] for n in ns], gs, core, "pair_sum_" + ns[0])))
        return [out[n] for n in names]

    (dx1, d_g2), got = _mm_rms_bwd("ffn_dh2", [(da, w_gate), (db, w_up)], x1, norm2_g, dx2, tm=1024,
                                   a_spec=pl.BlockSpec((None, 1024, FF_S), lambda m, s: (s, m, 0)),
                                   b_spec=pl.BlockSpec((None, FF_S, D_MODEL), lambda m, s: (s, 0, 0)), dims=NN,
                                   comm=_x_pair_exchange([part[n] for n in REDUCE_A]), vmem=VMEM_LIMIT_BIG)
    sums_a = pair_sums(REDUCE_A, got)

    d_w_o = _mm_tn_tokens("d_w_o", merged, dx1, bf16).reshape(N_CHIPS, D_MODEL // N_CHIPS, D_MODEL)
    dmerged = _mm_nt_full("d_merged", dx1, w_o_f, bf16)
    dya, dyc, dga, dgc, d_gba, d_gbc = _merge_bwd(dmerged, proj, gate_b, y_attn, y_conv)

    d_w_conv_out = _mm_tn_tokens("d_w_conv_out", c3, dyc, bf16).reshape(N_CHIPS, D_MODEL // N_CHIPS, D_MODEL)
    dc1, d_ln_g, d_ln_b = _conv_out_bwd(dyc, w_conv_out_f, c1, conv_ln_g, conv_ln_b)
    (dua, dub, d_conv_w, d_conv_b), got_a = _glu_conv_bwd(dc1.reshape(B, SEQ, D_MODEL), proj3, conv_w, comm=_x_chip_exchange(sums_a))

    d_w_attn_out = _attn_out_bwd_w(mix, dya)
    part.update(w_conv_out=d_w_conv_out, w_attn_out=d_w_attn_out, w_o=d_w_o)
    dmix, got = _attn_out_bwd_x(dya, w_attn_out, comm=_x_pair_exchange([part[n] for n in REDUCE_B]))
    sums_b = pair_sums(REDUCE_B, got)
    dmix3 = dmix.reshape(B, SEQ, GW)
    delta3, delta_r1, delta_r2, dmix_r1, dmix_r2 = _attn_delta(dmix3, mix3)
    one = (B, 1, SEQ)
    (dq0, dk0, dv0), got_b = _attn_bwd(0, *qkv_args(0), dmix3.reshape(one + (GW,)), lse3.reshape(one + (LANES,)),
                                       delta3.reshape(one + (LANES,)), comm=_x_chip_exchange(sums_b))
    dqkv = [tuple(t.reshape(B, SEQ, GW) for t in (dq0, dk0, dv0)),
            _attn_bwd(1, *qkv_args(1), dmix_r1, lse_r1, delta_r1), _attn_bwd(2, *qkv_args(2), dmix_r2, lse_r2, delta_r2)]
    dproj = _assemble_dproj(dqkv, dua, dub, dga.reshape(B, SEQ, D_MODEL), dgc.reshape(B, SEQ, D_MODEL)).reshape(T, IN_W)

    d_w_in = _mm("d_w_in", h_t, dproj, grid=(N_CHIPS, T // 2048),
                 a_spec=pl.BlockSpec((D_MODEL, 2048), lambda s, k: (0, k)),
                 b_spec=pl.BlockSpec((2048, IN_S), lambda s, k: (k, s)),
                 o_spec=pl.BlockSpec((None, D_MODEL, IN_S), lambda s, k: (s, 0, 0)), o_shape=(N_CHIPS, D_MODEL, IN_S),
                 o_dtype=bf16, dims=NN, acc_shape=(D_MODEL, IN_S), k_axis=1, sem=("parallel", "arbitrary"), vmem=VMEM_LIMIT_BIG)
    part.update(w_in=d_w_in)
    sums_c = pair_sums(REDUCE_C, _run_exchange(_x_pair_exchange([d_w_in]), "grad_pair_exchange_c"))
    (dx, d_g1), got_c = _mm_rms_bwd("d_h", [(dproj, w_in)], xf, norm1_g, dx1, tm=1024,
                                    a_spec=pl.BlockSpec((1024, IN_S), lambda m, s: (m, s)),
                                    b_spec=pl.BlockSpec((None, D_MODEL, IN_S), lambda m, s: (s, 0, 0)), dims=NT,
                                    comm=_x_chip_exchange(sums_c), vmem=VMEM_LIMIT_BIG)

    names = REDUCE_A + REDUCE_B + REDUCE_C
    sums = dict(zip(names, sums_a + sums_b + sums_c))
    halves = {}
    for ns, gs in _same_shape_groups(names, got_a + got_b + got_c):
        halves.update(zip(ns, _sum4([sums[n] for n in ns], gs, chip1, core, "chip_sum_" + ns[0])))
    big = {n: halves[n] for n in names}
    small = dict(norm1_g=d_g1, gate_b=jnp.concatenate([d_gba, d_gbc], axis=-1), conv_b=d_conv_b, conv_ln_g=d_ln_g,
                 conv_ln_b=d_ln_b, norm2_g=d_g2, norm_f_g=d_gf, conv_w=d_conv_w)
    return loss, dx.reshape(B, SEQ, D_MODEL), big, small


BIG = ("w_in", "w_conv_out", "w_attn_out", "w_o", "w_ffn_gate", "w_ffn_up", "w_ffn_down")
TRANSPOSED = ("w_ffn_gate", "w_ffn_up")
SMALL = ("norm1_g", "gate_b", "conv_b", "conv_ln_g", "conv_ln_b", "norm2_g", "norm_f_g")
SMALL_ROWS = {"norm1_g": 8, "gate_b": 16, "conv_b": 8, "conv_ln_g": 8, "conv_ln_b": 8, "norm2_g": 8, "norm_f_g": 8}
LOSS_ROWS = 8
CONVW_ROWS = 32 * D_MODEL // LANES


def kernel(x, norm1_g, w_in, gate_b, conv_w, conv_b, conv_ln_g, conv_ln_b, w_conv_out, w_attn_out, w_o, norm2_g, w_ffn_gate, w_ffn_up, w_ffn_down, norm_f_g, loss_target, m_norm1_g, m_w_in, m_gate_b, m_conv_w, m_conv_b, m_conv_ln_g, m_conv_ln_b, m_w_conv_out, m_w_attn_out, m_w_o, m_norm2_g, m_w_ffn_gate, m_w_ffn_up, m_w_ffn_down, m_norm_f_g, v_norm1_g, v_w_in, v_gate_b, v_conv_w, v_conv_b, v_conv_ln_g, v_conv_ln_b, v_w_conv_out, v_w_attn_out, v_w_o, v_norm2_g, v_w_ffn_gate, v_w_ffn_up, v_w_ffn_down, v_norm_f_g):
    W = dict(norm1_g=norm1_g, w_in=w_in, gate_b=gate_b, conv_w=conv_w, conv_b=conv_b, conv_ln_g=conv_ln_g, conv_ln_b=conv_ln_b,
             w_conv_out=w_conv_out, w_attn_out=w_attn_out, w_o=w_o, norm2_g=norm2_g, w_ffn_gate=w_ffn_gate, w_ffn_up=w_ffn_up,
             w_ffn_down=w_ffn_down, norm_f_g=norm_f_g)
    M = dict(norm1_g=m_norm1_g, w_in=m_w_in, gate_b=m_gate_b, conv_w=m_conv_w, conv_b=m_conv_b, conv_ln_g=m_conv_ln_g,
             conv_ln_b=m_conv_ln_b, w_conv_out=m_w_conv_out, w_attn_out=m_w_attn_out, w_o=m_w_o, norm2_g=m_norm2_g,
             w_ffn_gate=m_w_ffn_gate, w_ffn_up=m_w_ffn_up, w_ffn_down=m_w_ffn_down, norm_f_g=m_norm_f_g)
    V = dict(norm1_g=v_norm1_g, w_in=v_w_in, gate_b=v_gate_b, conv_w=v_conv_w, conv_b=v_conv_b, conv_ln_g=v_conv_ln_g,
             conv_ln_b=v_conv_ln_b, w_conv_out=v_w_conv_out, w_attn_out=v_w_attn_out, w_o=v_w_o, norm2_g=v_norm2_g,
             w_ffn_gate=v_w_ffn_gate, w_ffn_up=v_w_ffn_up, w_ffn_down=v_w_ffn_down, norm_f_g=v_norm_f_g)
    order = list(W)

    def as2d(n, a):
        a = a.reshape(a.shape[-2:])
        return a.T if n in TRANSPOSED else a

    def from2d(n, a):
        return (a.T if n in TRANSPOSED else a).reshape(W[n].shape)

    shard2d = {n: as2d(n, W[n]) for n in BIG}
    chip = 2 * lax.axis_index("x") + lax.axis_index("y")

    core = lax.axis_index("c").astype(jnp.int32).reshape(1)
    chip1 = chip.astype(jnp.int32).reshape(1)
    shards = {n: shard2d[n].astype(bf16) for n in BIG}

    def slot_for(s):
        return lax.dynamic_update_slice(lax.empty((N_CHIPS,) + s.shape, s.dtype), s[None], (chip, 0, 0))

    slots = {n: slot_for(s) for n, s in shards.items()}
    cw = jnp.pad(conv_w.reshape(CONV_K, D_MODEL // N_CHIPS), ((0, CONV_HALO - CONV_K), (0, 0)))
    w_in_full, cw4 = _allgather_weights([shards["w_in"], cw], [slots["w_in"], slot_for(cw)])
    conv_w_full = cw4.transpose(1, 0, 2).reshape(CONV_HALO, D_MODEL)[:CONV_K]

    loss, grad_x, grads, gsmall = _step(x, loss_target, norm1_g, gate_b, conv_w_full, conv_b, conv_ln_g, conv_ln_b, norm2_g,
                                        norm_f_g, w_in_full, shards, slots, chip1, core)

    pieces = [_rows128(loss, LOSS_ROWS)] + [_rows128(gsmall[n], SMALL_ROWS[n]) for n in SMALL] + [_rows128(gsmall["conv_w"], CONVW_ROWS)]
    full_rows, tot = _final_exchange([grads[n] for n in BIG], jnp.concatenate(pieces, axis=0))
    grads = dict(zip(BIG, full_rows))
    loss_out = tot[0, 0]
    row = LOSS_ROWS
    for n in SMALL:
        grads[n] = tot[row: row + W[n].size // LANES].reshape(W[n].shape)
        row += SMALL_ROWS[n]
    dcw = tot[row: row + CONV_K * D_MODEL // LANES].reshape(CONV_K, D_MODEL)
    grads["conv_w"] = lax.dynamic_slice(dcw, (0, chip * (D_MODEL // N_CHIPS)), (CONV_K, D_MODEL // N_CHIPS))

    delta, new_m, new_v = {}, {}, {}
    for ns, ws in _same_shape_groups(BIG, [shard2d[n] for n in BIG]):
        ds, nms, nvs = _adamw(ws, [grads[n] for n in ns], [as2d(n, M[n]) for n in ns], [as2d(n, V[n]) for n in ns], "adamw_" + ns[0])
        for n, d, nm, nv in zip(ns, ds, nms, nvs):
            delta[n], new_m[n], new_v[n], grads[n] = (from2d(n, t) for t in (d, nm, nv, grads[n]))

    def pack(src):
        return jnp.concatenate([_rows128(src[n], SMALL_ROWS[n]) for n in SMALL], axis=0)

    (d,), (nm,), (nv,) = _adamw([pack(W)], [pack(grads)], [pack(M)], [pack(V)], "adamw_small")
    row = 0
    for n in SMALL:
        k = W[n].size // LANES
        delta[n], new_m[n], new_v[n] = (t[row: row + k].reshape(W[n].shape) for t in (d, nm, nv))
        row += SMALL_ROWS[n]

    def pad32(a):
        return jnp.pad(a.reshape(CONV_K, D_MODEL // N_CHIPS), ((0, 1), (0, 0)))

    (d,), (nm,), (nv,) = _adamw([pad32(conv_w)], [pad32(grads["conv_w"])], [pad32(m_conv_w)], [pad32(v_conv_w)], "adamw_conv_w")
    delta["conv_w"], new_m["conv_w"], new_v["conv_w"] = (t[:CONV_K].reshape(conv_w.shape) for t in (d, nm, nv))
    grads["conv_w"] = grads["conv_w"].reshape(conv_w.shape)

    return (loss_out, grad_x, *[grads[n] for n in order], *[delta[n] for n in order],
            *[new_m[n] for n in order], *[new_v[n] for n in order])
```

```python
import functools
import math
from typing import Callable, NamedTuple

import numpy as np
import jax
import jax.numpy as jnp
from jax import lax
from jax.experimental import pallas as pl
from jax.experimental.pallas import tpu as pltpu

f32 = jnp.float32
bf16 = jnp.bfloat16
SDS = jax.ShapeDtypeStruct
MESH = pl.DeviceIdType.MESH

D_MODEL = 1024
SEQ = 2048
HEAD_DIM = 64
HEADS = 8
GROUPS = ((128, 1), (512, 4), (2048, 16))
GW = HEADS * HEAD_DIM
ATTN_W = len(GROUPS) * GW
Q_BLOCK = 128
CONV_K = 31
D_FF = 2816
IN_W = 3 * ATTN_W + 2 * D_MODEL + 2 * D_MODEL
N_CHIPS = 4
IN_S = IN_W // N_CHIPS
FF_S = D_FF // N_CHIPS
RMS_EPS = 1e-6
LN_EPS = 1e-5
LR, B1, B2, ADAM_EPS, WD, STEP = 0.001, 0.9, 0.999, 1e-08, 0.01, 10
NEG = -1e30
LANES = 128
VMEM_LIMIT = 48 * 2 ** 20
VMEM_LIMIT_BIG = 54 * 2 ** 20
CB = 512
UA_CB, UB_CB, GA_CB, GC_CB = 9, 11, 13, 15


def _alibi_slope_list(n):
    def pow2(m):
        start = 2.0 ** (-8.0 / m)
        return [start ** (i + 1) for i in range(m)]
    if math.log2(n).is_integer():
        return pow2(n)
    c = 2 ** math.floor(math.log2(n))
    return pow2(c) + _alibi_slope_list(2 * c)[0::2][: n - c]


_SLOPES = np.asarray(sorted(_alibi_slope_list(len(GROUPS) * HEADS), reverse=True), dtype=np.float32).reshape(len(GROUPS), HEADS)


def _cp(sem=None, vmem=VMEM_LIMIT):
    return pltpu.CompilerParams(dimension_semantics=sem, vmem_limit_bytes=vmem)


def _sigmoid(x):
    return 1.0 / (1.0 + jnp.exp(-x))


HBM = pl.BlockSpec(memory_space=pl.ANY)


class _Exchange(NamedTuple):
    ins: list
    out_shape: list
    scratch: list
    aliases: dict
    start: Callable
    wait: Callable


def _launch(body, *, name, grid, in_specs, out_specs, out_shape, args, scratch_shapes=(), sem=None, comm=None, vmem=VMEM_LIMIT):
    if comm is None:
        return pl.pallas_call(body, grid=grid, in_specs=in_specs, out_specs=out_specs, out_shape=out_shape,
                              scratch_shapes=list(scratch_shapes), name=name, compiler_params=_cp(sem, vmem))(*args)
    multi = isinstance(out_shape, (tuple, list))
    m_out = list(out_shape) if multi else [out_shape]
    m_ospec = list(out_specs) if multi else [out_specs]
    n_in, n_out, n_scr = len(in_specs), len(m_out), len(scratch_shapes)
    nc_in, nc_out = len(comm.ins), len(comm.out_shape)

    def hosted(*refs):
        bounds = np.cumsum([0, n_in, nc_in, n_out, nc_out, n_scr])
        mi, ci, mo, co, ms = (refs[a:b] for a, b in zip(bounds[:-1], bounds[1:]))
        cs = refs[bounds[-1]:]
        ids = [pl.program_id(a) for a in range(len(grid))]
        first = functools.reduce(jnp.logical_and, [i == 0 for i in ids])
        last = functools.reduce(jnp.logical_and, [i == g - 1 for i, g in zip(ids, grid)])

        @pl.when(first)
        def _():
            comm.start(ci, co, cs)

        body(*mi, *mo, *ms)

        @pl.when(last)
        def _():
            comm.wait(ci, co, cs)

    res = pl.pallas_call(
        hosted, grid=grid, in_specs=list(in_specs) + [HBM] * nc_in, out_specs=m_ospec + [HBM] * nc_out,
        out_shape=tuple(m_out) + tuple(comm.out_shape), scratch_shapes=list(scratch_shapes) + list(comm.scratch),
        input_output_aliases={n_in + i: n_out + o for i, o in comm.aliases.items()}, name=name + "_comm",
        compiler_params=pltpu.CompilerParams(dimension_semantics=("arbitrary",) * len(grid), vmem_limit_bytes=vmem,
                                             has_side_effects=True))(*args, *comm.ins)
    return (tuple(res[:n_out]) if multi else res[0]), tuple(res[n_out:])


def _run_exchange(ex, name):
    n_in = len(ex.ins)

    def body(*refs):
        ins, outs, sems = refs[:n_in], refs[n_in:n_in + len(ex.out_shape)], refs[n_in + len(ex.out_shape):]
        ex.start(ins, outs, sems)
        ex.wait(ins, outs, sems)

    return pl.pallas_call(body, in_specs=[HBM] * n_in, out_specs=[HBM] * len(ex.out_shape), out_shape=tuple(ex.out_shape),
                          scratch_shapes=list(ex.scratch), input_output_aliases=dict(ex.aliases), name=name,
                          compiler_params=pltpu.CompilerParams(has_side_effects=True))(*ex.ins)


def _rmsnorm(xv, gv):
    return xv * lax.rsqrt(jnp.mean(xv * xv, axis=-1, keepdims=True) + RMS_EPS) * gv


def _rms_fwd(x, g, name):
    T = x.shape[0]
    tm = 512

    def body(x_ref, g_ref, o_ref, ot_ref):
        hv = _rmsnorm(x_ref[...], g_ref[...])
        o_ref[...] = hv.astype(bf16)
        ot_ref[...] = hv.T.astype(bf16)

    row = pl.BlockSpec((tm, D_MODEL), lambda i: (i, 0))
    vec = pl.BlockSpec((1, D_MODEL), lambda i: (0, 0))
    return pl.pallas_call(body, grid=(T // tm,), in_specs=[row, vec], out_specs=[row, pl.BlockSpec((D_MODEL, tm), lambda i: (0, i))],
                          out_shape=(SDS((T, D_MODEL), bf16), SDS((D_MODEL, T), bf16)), name=name,
                          compiler_params=_cp(("parallel",)))(x, g)


def _rms_bwd_tile(dyv, xv, gv, dres):
    r = lax.rsqrt(jnp.mean(xv * xv, axis=-1, keepdims=True) + RMS_EPS)
    xh = xv * r
    dxh = dyv * gv
    dx = dres + r * (dxh - xh * jnp.mean(dxh * xh, axis=-1, keepdims=True))
    return dx, jnp.sum(dyv * xh, axis=0, keepdims=True)


def _accumulate(first, refs_parts):
    @pl.when(first)
    def _():
        for ref, part in refs_parts:
            ref[...] = part

    @pl.when(jnp.logical_not(first))
    def _():
        for ref, part in refs_parts:
            ref[...] += part


def _mm_rms_bwd(name, ops, x, g, dres, *, tm, a_spec, b_spec, dims, comm=None, vmem=VMEM_LIMIT):
    T = x.shape[0]
    n = len(ops)

    def body(*refs):
        ab_refs = refs[:2 * n]
        x_ref, g_ref, r_ref, dx_ref, dg_ref, acc = refs[2 * n:]
        m, s = pl.program_id(0), pl.program_id(1)
        p = sum(lax.dot_general(ab_refs[2 * i][...], ab_refs[2 * i + 1][...], dims, preferred_element_type=f32) for i in range(n))

        @pl.when(s == 0)
        def _():
            acc[...] = p

        @pl.when(s > 0)
        def _():
            acc[...] += p

        @pl.when(s == N_CHIPS - 1)
        def _():
            dx, part = _rms_bwd_tile(acc[...], x_ref[...], g_ref[...], r_ref[...])
            dx_ref[...] = dx
            _accumulate(m == 0, [(dg_ref, part)])

    row = pl.BlockSpec((tm, D_MODEL), lambda m, s: (m, 0))
    vec = pl.BlockSpec((1, D_MODEL), lambda m, s: (0, 0))
    return _launch(body, name=name, grid=(T // tm, N_CHIPS), in_specs=[a_spec, b_spec] * n + [row, vec, row], out_specs=[row, vec],
                   out_shape=(SDS((T, D_MODEL), f32), SDS((1, D_MODEL), f32)), args=tuple(t for ab in ops for t in ab) + (x, g, dres),
                   scratch_shapes=[pltpu.VMEM((tm, D_MODEL), f32)], sem=("arbitrary", "arbitrary"), comm=comm, vmem=vmem)


def _ffn_down_loss(ff, wd, x1, gf, target):
    T = x1.shape[0]
    tm = 1024
    nm = T // tm

    def body(f_ref, w_ref, x_ref, g_ref, t_ref, loss_ref, dx_ref, dg_ref, acc):
        m, s = pl.program_id(0), pl.program_id(1)
        p = jnp.dot(f_ref[...], w_ref[...], preferred_element_type=f32)

        @pl.when(s == 0)
        def _():
            acc[...] = p

        @pl.when(s > 0)
        def _():
            acc[...] += p

        @pl.when(s == N_CHIPS - 1)
        def _():
            xv = acc[...] + x_ref[...]
            gv = g_ref[...]
            r = lax.rsqrt(jnp.mean(xv * xv, axis=-1, keepdims=True) + RMS_EPS)
            xh = xv * r
            e = xh * gv - t_ref[...]
            part_l = jnp.broadcast_to(0.5 * jnp.sum(jnp.mean(e * e, axis=-1, keepdims=True), axis=0, keepdims=True), (1, LANES))
            dy = e * (1.0 / D_MODEL)
            dxh = dy * gv
            dx_ref[...] = r * (dxh - xh * jnp.mean(dxh * xh, axis=-1, keepdims=True))
            part_g = jnp.sum(dy * xh, axis=0, keepdims=True)

            @pl.when(m == 0)
            def _():
                loss_ref[...] = part_l
                dg_ref[...] = part_g

            @pl.when(m > 0)
            def _():
                loss_ref[...] += part_l
                dg_ref[...] += part_g

    row = pl.BlockSpec((tm, D_MODEL), lambda m, s: (m, 0))
    vec = pl.BlockSpec((1, D_MODEL), lambda m, s: (0, 0))
    return pl.pallas_call(
        body, grid=(nm, N_CHIPS),
        in_specs=[pl.BlockSpec((None, tm, FF_S), lambda m, s: (s, m, 0)), pl.BlockSpec((None, FF_S, D_MODEL), lambda m, s: (s, 0, 0)),
                  row, vec, row],
        out_specs=[pl.BlockSpec((1, LANES), lambda m, s: (0, 0)), row, vec],
        out_shape=(SDS((1, LANES), f32), SDS((T, D_MODEL), f32), SDS((1, D_MODEL), f32)),
        scratch_shapes=[pltpu.VMEM((tm, D_MODEL), f32)], name="ffn_down_loss", compiler_params=_cp(("arbitrary", "arbitrary")))(
            ff, wd, x1, gf, target)


def _merge_o_proj_rms(proj, gate_b, ya, yc, w_o, x, g2):
    T = x.shape[0]
    tm = 512

    def body(ga0, ga1, gc0, gc1, b_ref, ya_ref, yc_ref, w_ref, x_ref, g_ref, m_ref, x1_ref, h2_ref):
        bv = b_ref[...]
        ga = jnp.concatenate([ga0[...], ga1[...]], axis=-1).astype(f32)
        gc = jnp.concatenate([gc0[...], gc1[...]], axis=-1).astype(f32)
        merged = (_sigmoid(ga + bv[:, :D_MODEL]) * ya_ref[...].astype(f32)
                  + _sigmoid(gc + bv[:, D_MODEL:]) * yc_ref[...].astype(f32)).astype(bf16)
        m_ref[...] = merged
        x1 = jnp.dot(merged, w_ref[...], preferred_element_type=f32) + x_ref[...]
        x1_ref[...] = x1
        h2_ref[...] = _rmsnorm(x1, g_ref[...]).astype(bf16)

    row = pl.BlockSpec((tm, D_MODEL), lambda i: (i, 0))
    return pl.pallas_call(
        body, grid=(T // tm,),
        in_specs=[pl.BlockSpec((tm, CB), lambda i, cb=cb: (i, cb)) for cb in (GA_CB, GA_CB + 1, GC_CB, GC_CB + 1)]
        + [pl.BlockSpec((1, 2 * D_MODEL), lambda i: (0, 0)), row, row, pl.BlockSpec((D_MODEL, D_MODEL), lambda i: (0, 0)), row,
           pl.BlockSpec((1, D_MODEL), lambda i: (0, 0))],
        out_specs=[row, row, row], out_shape=(SDS((T, D_MODEL), bf16), SDS((T, D_MODEL), f32), SDS((T, D_MODEL), bf16)),
        name="merge_o_proj_rms", compiler_params=_cp(("parallel",)))(proj, proj, proj, proj, gate_b, ya, yc, w_o, x, g2)


def _conv_out_fwd(c1, g, b, w, comm=None):
    T = c1.shape[0]
    tm = 1024

    def body(c_ref, g_ref, b_ref, w_ref, c3_ref, y_ref):
        cv = c_ref[...]
        mu = jnp.mean(cv, axis=-1, keepdims=True)
        cc = cv - mu
        var = jnp.mean(cc * cc, axis=-1, keepdims=True)
        c2 = cc * lax.rsqrt(var + LN_EPS) * g_ref[...] + b_ref[...]
        c3 = (c2 * _sigmoid(c2)).astype(bf16)
        c3_ref[...] = c3
        y_ref[...] = jnp.dot(c3, w_ref[...], preferred_element_type=f32).astype(bf16)

    row = pl.BlockSpec((tm, D_MODEL), lambda i: (i, 0))
    vec = pl.BlockSpec((1, D_MODEL), lambda i: (0, 0))
    sd = SDS((T, D_MODEL), bf16)
    return _launch(body, name="conv_out_fwd", grid=(T // tm,), in_specs=[row, vec, vec, pl.BlockSpec((D_MODEL, D_MODEL), lambda i: (0, 0))],
                   out_specs=[row, row], out_shape=(sd, sd), args=(c1, g, b, w), sem=("parallel",), comm=comm)


def _conv_out_bwd(dyc, w, c1, g, b):
    T = c1.shape[0]
    tm = 1024

    def body(d_ref, w_ref, c_ref, g_ref, b_ref, dc_ref, dg_ref, db_ref):
        dc3 = lax.dot_general(d_ref[...], w_ref[...], NT, preferred_element_type=f32)
        cv = c_ref[...]
        gv = g_ref[...]
        mu = jnp.mean(cv, axis=-1, keepdims=True)
        cc = cv - mu
        var = jnp.mean(cc * cc, axis=-1, keepdims=True)
        rs = lax.rsqrt(var + LN_EPS)
        xh = cc * rs
        c2 = xh * gv + b_ref[...]
        sg = _sigmoid(c2)
        dc2 = dc3 * (sg * (1.0 + c2 * (1.0 - sg)))
        dxh = dc2 * gv
        dc_ref[...] = rs * (dxh - jnp.mean(dxh, axis=-1, keepdims=True) - xh * jnp.mean(dxh * xh, axis=-1, keepdims=True))
        _accumulate(pl.program_id(0) == 0, [(dg_ref, jnp.sum(dc2 * xh, axis=0, keepdims=True)), (db_ref, jnp.sum(dc2, axis=0, keepdims=True))])

    row = pl.BlockSpec((tm, D_MODEL), lambda i: (i, 0))
    vec = pl.BlockSpec((1, D_MODEL), lambda i: (0, 0))
    return pl.pallas_call(body, grid=(T // tm,), in_specs=[row, pl.BlockSpec((D_MODEL, D_MODEL), lambda i: (0, 0)), row, vec, vec],
                          out_specs=[row, vec, vec],
                          out_shape=(SDS((T, D_MODEL), f32), SDS((1, D_MODEL), f32), SDS((1, D_MODEL), f32)),
                          name="conv_out_bwd", compiler_params=_cp(("arbitrary",)))(dyc, w, c1, g, b)


NN = (((1,), (0,)), ((), ()))
NT = (((1,), (1,)), ((), ()))
TN = (((0,), (0,)), ((), ()))


def _mm(name, a, b, *, grid, a_spec, b_spec, o_spec, o_shape, o_dtype, dims, acc_shape=None, k_axis=None,
        res=None, res_spec=None, sem=None, comm=None, vmem=VMEM_LIMIT):
    nk = 1 if k_axis is None else grid[k_axis]

    def body(*refs):
        if res is None:
            a_ref, b_ref, o_ref = refs[:3]
            r_ref, scr = None, refs[3:]
        else:
            a_ref, b_ref, r_ref, o_ref = refs[:4]
            scr = refs[4:]
        p = lax.dot_general(a_ref[...].astype(bf16), b_ref[...].astype(bf16), dims, preferred_element_type=f32)
        if nk == 1:
            if r_ref is not None:
                p = p + r_ref[...]
            o_ref[...] = p.astype(o_dtype)
            return
        acc = scr[0]
        k = pl.program_id(k_axis)

        @pl.when(k == 0)
        def _():
            acc[...] = p

        @pl.when(k > 0)
        def _():
            acc[...] += p

        @pl.when(k == nk - 1)
        def _():
            t = acc[...]
            if r_ref is not None:
                t = t + r_ref[...]
            o_ref[...] = t.astype(o_dtype)

    ins = [a, b] + ([] if res is None else [res])
    specs = [a_spec, b_spec] + ([] if res is None else [res_spec])
    scratch = [] if nk == 1 else [pltpu.VMEM(acc_shape, f32)]
    return _launch(body, name=name, grid=grid, in_specs=specs, out_specs=o_spec, out_shape=SDS(o_shape, o_dtype),
                   args=ins, scratch_shapes=scratch, sem=sem, comm=comm, vmem=vmem)


def _mm_nn_full(name, a, b, o_dtype, res=None, tm=1024, comm=None):
    T, K = a.shape
    N = b.shape[1]
    return _mm(name, a, b, grid=(T // tm,), a_spec=pl.BlockSpec((tm, K), lambda i: (i, 0)),
               b_spec=pl.BlockSpec((K, N), lambda i: (0, 0)), o_spec=pl.BlockSpec((tm, N), lambda i: (i, 0)),
               o_shape=(T, N), o_dtype=o_dtype, dims=NN, res=res,
               res_spec=None if res is None else pl.BlockSpec((tm, N), lambda i: (i, 0)), sem=("parallel",), comm=comm)


def _mm_nt_full(name, a, b, o_dtype, tm=1024):
    T, N = a.shape
    K = b.shape[0]
    return _mm(name, a, b, grid=(T // tm,), a_spec=pl.BlockSpec((tm, N), lambda i: (i, 0)),
               b_spec=pl.BlockSpec((K, N), lambda i: (0, 0)), o_spec=pl.BlockSpec((tm, K), lambda i: (i, 0)),
               o_shape=(T, K), o_dtype=o_dtype, dims=NT, sem=("parallel",))


def _mm_tn_tokens(name, a, b, o_dtype, tk=1024):
    T, K = a.shape
    N = b.shape[1]
    return _mm(name, a, b, grid=(T // tk,), a_spec=pl.BlockSpec((tk, K), lambda k: (k, 0)),
               b_spec=pl.BlockSpec((tk, N), lambda k: (k, 0)), o_spec=pl.BlockSpec((K, N), lambda k: (0, 0)),
               o_shape=(K, N), o_dtype=o_dtype, dims=TN, acc_shape=(K, N), k_axis=0, sem=("arbitrary",))


AO_S = D_MODEL // N_CHIPS


def _attn_out_cat(w_ref):
    return jnp.concatenate([w_ref[s] for s in range(N_CHIPS)], axis=-1)


def _attn_out_fwd(mix, w4):
    T = mix.shape[0]
    tm = 1024

    def body(a_ref, w_ref, o_ref):
        o_ref[...] = jnp.dot(a_ref[...], _attn_out_cat(w_ref), preferred_element_type=f32).astype(bf16)

    return pl.pallas_call(body, grid=(T // tm,),
                          in_specs=[pl.BlockSpec((tm, GW), lambda i: (i, 0)), pl.BlockSpec((N_CHIPS, GW, AO_S), lambda i: (0, 0, 0))],
                          out_specs=pl.BlockSpec((tm, D_MODEL), lambda i: (i, 0)), out_shape=SDS((T, D_MODEL), bf16), name="attn_out",
                          compiler_params=_cp(("parallel",)))(mix, w4)


def _attn_out_bwd_x(dya, w4, comm=None):
    T = dya.shape[0]
    tm = 1024

    def body(d_ref, w_ref, o_ref):
        o_ref[...] = lax.dot_general(d_ref[...], _attn_out_cat(w_ref), NT, preferred_element_type=f32).astype(bf16)

    return _launch(body, name="d_mix", grid=(T // tm,),
                   in_specs=[pl.BlockSpec((tm, D_MODEL), lambda i: (i, 0)), pl.BlockSpec((N_CHIPS, GW, AO_S), lambda i: (0, 0, 0))],
                   out_specs=pl.BlockSpec((tm, GW), lambda i: (i, 0)), out_shape=SDS((T, GW), bf16), args=(dya, w4),
                   sem=("parallel",), comm=comm)


def _attn_out_bwd_w(mix, dya):
    T = mix.shape[0]
    tk = 1024
    nk = T // tk

    def body(a_ref, d_ref, o_ref, acc):
        k = pl.program_id(0)
        p = lax.dot_general(a_ref[...], d_ref[...], TN, preferred_element_type=f32)

        @pl.when(k == 0)
        def _():
            acc[...] = p

        @pl.when(k > 0)
        def _():
            acc[...] += p

        @pl.when(k == nk - 1)
        def _():
            for s in range(N_CHIPS):
                o_ref[s] = acc[:, s * AO_S:(s + 1) * AO_S].astype(bf16)

    return pl.pallas_call(body, grid=(nk,),
                          in_specs=[pl.BlockSpec((tk, GW), lambda k: (k, 0)), pl.BlockSpec((tk, D_MODEL), lambda k: (k, 0))],
                          out_specs=pl.BlockSpec((N_CHIPS, GW, AO_S), lambda k: (0, 0, 0)), out_shape=SDS((N_CHIPS, GW, AO_S), bf16),
                          scratch_shapes=[pltpu.VMEM((GW, D_MODEL), f32)], name="d_w_attn_out",
                          compiler_params=_cp(("arbitrary",)))(mix, dya)


def _ffn_up(h2, wg, wu):
    T = h2.shape[0]
    tm = 1024

    def body(h_ref, wg_ref, wu_ref, a_ref, b_ref, f_ref):
        hv = h_ref[...]
        av = lax.dot_general(hv, wg_ref[...], NT, preferred_element_type=f32)
        bv = lax.dot_general(hv, wu_ref[...], NT, preferred_element_type=f32)
        a_ref[...] = av.astype(bf16)
        b_ref[...] = bv.astype(bf16)
        f_ref[...] = (av * _sigmoid(av) * bv).astype(bf16)

    wspec = pl.BlockSpec((None, FF_S, D_MODEL), lambda s, m: (s, 0, 0))
    ospec = pl.BlockSpec((None, tm, FF_S), lambda s, m: (s, m, 0))
    osd = SDS((N_CHIPS, T, FF_S), bf16)
    return pl.pallas_call(body, grid=(N_CHIPS, T // tm),
                          in_specs=[pl.BlockSpec((tm, D_MODEL), lambda s, m: (m, 0)), wspec, wspec],
                          out_specs=[ospec, ospec, ospec], out_shape=(osd, osd, osd), name="ffn_up",
                          compiler_params=_cp(("parallel", "parallel")))(h2, wg, wu)


def _ffn_up_bwd_w(da, db, h2):
    T = h2.shape[0]
    tk = 2048
    nk = T // tk

    def body(a_ref, b_ref, h_ref, og_ref, ou_ref, accg, accu):
        k = pl.program_id(1)
        hv = h_ref[...]
        pg = lax.dot_general(a_ref[...], hv, TN, preferred_element_type=f32)
        pu = lax.dot_general(b_ref[...], hv, TN, preferred_element_type=f32)
        _accumulate(k == 0, [(accg, pg), (accu, pu)])

        @pl.when(k == nk - 1)
        def _():
            og_ref[...] = accg[...].astype(bf16)
            ou_ref[...] = accu[...].astype(bf16)

    aspec = pl.BlockSpec((None, tk, FF_S), lambda s, k: (s, k, 0))
    ospec = pl.BlockSpec((None, FF_S, D_MODEL), lambda s, k: (s, 0, 0))
    osd = SDS((N_CHIPS, FF_S, D_MODEL), bf16)
    return pl.pallas_call(body, grid=(N_CHIPS, nk), in_specs=[aspec, aspec, pl.BlockSpec((tk, D_MODEL), lambda s, k: (k, 0))],
                          out_specs=[ospec, ospec], out_shape=(osd, osd),
                          scratch_shapes=[pltpu.VMEM((FF_S, D_MODEL), f32), pltpu.VMEM((FF_S, D_MODEL), f32)], name="d_w_gate_up",
                          compiler_params=_cp(("parallel", "arbitrary")))(da, db, h2)


def _ffn_down_bwd(dx2, wd, a, b):
    T = dx2.shape[0]
    tm = 1024

    def body(d_ref, w_ref, a_ref, b_ref, da_ref, db_ref):
        df = lax.dot_general(d_ref[...].astype(bf16), w_ref[...], NT, preferred_element_type=f32)
        av = a_ref[...].astype(f32)
        sg = _sigmoid(av)
        da_ref[...] = (df * b_ref[...].astype(f32) * (sg * (1.0 + av * (1.0 - sg)))).astype(bf16)
        db_ref[...] = (df * av * sg).astype(bf16)

    aspec = pl.BlockSpec((None, tm, FF_S), lambda m, s: (s, m, 0))
    osd = SDS((N_CHIPS, T, FF_S), bf16)
    return pl.pallas_call(body, grid=(T // tm, N_CHIPS),
                          in_specs=[pl.BlockSpec((tm, D_MODEL), lambda m, s: (m, 0)),
                                    pl.BlockSpec((None, FF_S, D_MODEL), lambda m, s: (s, 0, 0)), aspec, aspec],
                          out_specs=[aspec, aspec], out_shape=(osd, osd), name="ffn_down_bwd",
                          compiler_params=_cp(("parallel", "parallel")))(dx2, wd, a, b)


def _merge_bwd(dm, proj, gate_b, ya, yc):
    T = proj.shape[0]
    tm = 1024

    def body(dm_ref, ga_ref, gc_ref, ba_ref, bc_ref, ya_ref, yc_ref, dya_ref, dyc_ref, dga_ref, dgc_ref, dba_ref, dbc_ref):
        dmv = dm_ref[...].astype(f32)
        sa = _sigmoid(ga_ref[...].astype(f32) + ba_ref[...])
        sc = _sigmoid(gc_ref[...].astype(f32) + bc_ref[...])
        dya_ref[...] = (dmv * sa).astype(bf16)
        dyc_ref[...] = (dmv * sc).astype(bf16)
        dga = dmv * ya_ref[...].astype(f32) * sa * (1.0 - sa)
        dgc = dmv * yc_ref[...].astype(f32) * sc * (1.0 - sc)
        dga_ref[...] = dga.astype(bf16)
        dgc_ref[...] = dgc.astype(bf16)
        pa = jnp.sum(dga, axis=0, keepdims=True)
        pc = jnp.sum(dgc, axis=0, keepdims=True)

        @pl.when(pl.program_id(1) == 0)
        def _():
            dba_ref[...] = pa
            dbc_ref[...] = pc

        @pl.when(pl.program_id(1) > 0)
        def _():
            dba_ref[...] += pa
            dbc_ref[...] += pc

    blk = pl.BlockSpec((tm, CB), lambda j, i: (i, j))
    vec = pl.BlockSpec((1, CB), lambda j, i: (0, j))
    big = SDS((T, D_MODEL), bf16)
    small = SDS((1, D_MODEL), f32)
    return pl.pallas_call(
        body, grid=(2, T // tm),
        in_specs=[blk, pl.BlockSpec((tm, CB), lambda j, i: (i, GA_CB + j)), pl.BlockSpec((tm, CB), lambda j, i: (i, GC_CB + j)),
                  vec, pl.BlockSpec((1, CB), lambda j, i: (0, 2 + j)), blk, blk],
        out_specs=[blk, blk, blk, blk, vec, vec], out_shape=(big, big, big, big, small, small), name="merge_bwd",
        compiler_params=_cp(("parallel", "arbitrary")))(dm, proj, proj, gate_b, gate_b, ya, yc)


CONV_TS = 512
CONV_HALO = 32
CONV_RC = 64
CONV_WIN = CONV_TS + CONV_HALO
SUBLANES = 8


def _fill_shifted(win, sh):
    for b in range(1, SUBLANES):
        sh[b - 1] = win[pl.ds(b, CONV_WIN - SUBLANES), :]


def _rows_at(win, sh, row):
    a, b = divmod(row, SUBLANES)
    if b == 0:
        return win[pl.ds(row, CONV_RC), :]
    return sh[b - 1, pl.ds(a * SUBLANES, CONV_RC), :]


def _glu_conv_fwd(proj3, w, bias, comm=None):
    B = proj3.shape[0]
    nt = SEQ // CONV_TS
    hb = CONV_TS // CONV_HALO

    def body(ua_ref, ub_ref, ha_ref, hb_ref, w_ref, b_ref, o_ref, win, sh):
        i = pl.program_id(2)
        c0 = ua_ref[...].astype(f32) * _sigmoid(ub_ref[...].astype(f32))
        halo = ha_ref[...].astype(f32) * _sigmoid(hb_ref[...].astype(f32))
        win[0:CONV_HALO, :] = jnp.where(i > 0, halo, 0.0)
        win[CONV_HALO:, :] = c0
        _fill_shifted(win, sh)
        for r0 in range(0, CONV_TS, CONV_RC):
            acc = jnp.zeros((CONV_RC, CB), f32) + b_ref[...]
            for k in range(CONV_K):
                acc = acc + _rows_at(win, sh, r0 + CONV_HALO - (CONV_K - 1) + k) * w_ref[k:k + 1, :]
            o_ref[r0:r0 + CONV_RC, :] = acc

    def cur(cb):
        return pl.BlockSpec((None, CONV_TS, CB), lambda b, j, i: (b, i, cb + j))

    def prev(cb):
        return pl.BlockSpec((None, CONV_HALO, CB), lambda b, j, i: (b, jnp.maximum(i * hb - 1, 0), cb + j))

    return _launch(
        body, name="glu_conv_fwd", grid=(B, 2, nt),
        in_specs=[cur(UA_CB), cur(UB_CB), prev(UA_CB), prev(UB_CB),
                  pl.BlockSpec((CONV_K, CB), lambda b, j, i: (0, j)), pl.BlockSpec((1, CB), lambda b, j, i: (0, j))],
        out_specs=pl.BlockSpec((None, CONV_TS, CB), lambda b, j, i: (b, i, j)),
        out_shape=SDS((B, SEQ, D_MODEL), f32), args=(proj3, proj3, proj3, proj3, w, bias),
        scratch_shapes=[pltpu.VMEM((CONV_WIN, CB), f32), pltpu.VMEM((SUBLANES - 1, CONV_WIN - SUBLANES, CB), f32)],
        sem=("parallel", "parallel", "parallel"), comm=comm)


def _glu_conv_bwd(dc1, proj3, w, comm=None):
    B = proj3.shape[0]
    nt = SEQ // CONV_TS
    hb = CONV_TS // CONV_HALO

    def body(d_ref, dn_ref, ua_ref, ub_ref, ha_ref, hb_ref, w_ref, dua_ref, dub_ref, dw_ref, db_ref, winc, wind, accw, shc, shd):
        b = pl.program_id(1)
        i = pl.program_id(2)
        first = jnp.logical_and(b == 0, i == 0)
        last = jnp.logical_and(b == B - 1, i == nt - 1)

        @pl.when(first)
        def _():
            accw[...] = jnp.zeros_like(accw)
            db_ref[...] = jnp.zeros_like(db_ref)

        halo = ha_ref[...].astype(f32) * _sigmoid(hb_ref[...].astype(f32))
        winc[0:CONV_HALO, :] = jnp.where(i > 0, halo, 0.0)
        winc[CONV_HALO:, :] = ua_ref[...].astype(f32) * _sigmoid(ub_ref[...].astype(f32))
        wind[0:CONV_TS, :] = d_ref[...]
        wind[CONV_TS:, :] = jnp.where(i < nt - 1, dn_ref[...], 0.0)
        db_ref[...] += jnp.sum(d_ref[...], axis=0, keepdims=True)
        _fill_shifted(winc, shc)
        _fill_shifted(wind, shd)
        for r0 in range(0, CONV_TS, CONV_RC):
            dc0 = jnp.zeros((CONV_RC, CB), f32)
            for k in range(CONV_K):
                dc0 = dc0 + _rows_at(wind, shd, r0 + (CONV_K - 1) - k) * w_ref[k:k + 1, :]
            uav = ua_ref[r0:r0 + CONV_RC, :].astype(f32)
            sg = _sigmoid(ub_ref[r0:r0 + CONV_RC, :].astype(f32))
            dua_ref[r0:r0 + CONV_RC, :] = (dc0 * sg).astype(bf16)
            dub_ref[r0:r0 + CONV_RC, :] = (dc0 * uav * sg * (1.0 - sg)).astype(bf16)
            dv = wind[r0:r0 + CONV_RC, :]
            for k in range(CONV_K):
                prod = dv * _rows_at(winc, shc, r0 + CONV_HALO - (CONV_K - 1) + k)
                accw[k] += jnp.sum(prod.reshape(CONV_RC // 8, 8, CB), axis=0)

        @pl.when(last)
        def _():
            for k in range(CONV_K):
                dw_ref[k:k + 1, :] = jnp.sum(accw[k], axis=0, keepdims=True)
            dw_ref[CONV_K:, :] = jnp.zeros((CONV_HALO - CONV_K, CB), f32)

    def cur(cb):
        return pl.BlockSpec((None, CONV_TS, CB), lambda j, b, i: (b, i, cb + j))

    def prev(cb):
        return pl.BlockSpec((None, CONV_HALO, CB), lambda j, b, i: (b, jnp.maximum(i * hb - 1, 0), cb + j))

    nxt = pl.BlockSpec((None, CONV_HALO, CB), lambda j, b, i: (b, jnp.minimum((i + 1) * hb, SEQ // CONV_HALO - 1), j))
    big = SDS((B, SEQ, D_MODEL), bf16)
    return _launch(
        body, name="glu_conv_bwd", grid=(2, B, nt),
        in_specs=[cur(0), nxt, cur(UA_CB), cur(UB_CB), prev(UA_CB), prev(UB_CB), pl.BlockSpec((CONV_K, CB), lambda j, b, i: (0, j))],
        out_specs=[cur(0), cur(0), pl.BlockSpec((CONV_HALO, CB), lambda j, b, i: (0, j)), pl.BlockSpec((1, CB), lambda j, b, i: (0, j))],
        out_shape=(big, big, SDS((CONV_HALO, D_MODEL), f32), SDS((1, D_MODEL), f32)),
        args=(dc1, dc1, proj3, proj3, proj3, proj3, w),
        scratch_shapes=[pltpu.VMEM((CONV_WIN, CB), f32), pltpu.VMEM((CONV_WIN, CB), f32), pltpu.VMEM((CONV_K, SUBLANES, CB), f32),
                        pltpu.VMEM((SUBLANES - 1, CONV_WIN - SUBLANES, CB), f32),
                        pltpu.VMEM((SUBLANES - 1, CONV_WIN - SUBLANES, CB), f32)],
        sem=("parallel", "arbitrary", "arbitrary"), comm=comm)


def _band(first, dil):
    kw = Q_BLOCK if first else 2 * Q_BLOCK
    qi = lax.broadcasted_iota(jnp.int32, (Q_BLOCK, kw), 0)
    kj = lax.broadcasted_iota(jnp.int32, (Q_BLOCK, kw), 1)
    rel = qi - kj + (0 if first else Q_BLOCK)
    valid = jnp.logical_and(rel >= 0, rel <= Q_BLOCK)
    return valid, rel.astype(f32) * float(dil)


def _bias(first, dil, slope):
    valid, dist = _band(first, dil)
    return jnp.where(valid, -slope * dist, NEG)


def _scores(q, k, bias):
    return lax.dot_general(q, k, NT, preferred_element_type=f32) * (HEAD_DIM ** -0.5) + bias


def _pair_cols(hp):
    return slice(hp * LANES, (hp + 1) * LANES)


def _half(x2, e):
    lane = lax.broadcasted_iota(jnp.int32, (1, LANES), 1)
    keep = (lane < HEAD_DIM) if e == 0 else (lane >= HEAD_DIM)
    return jnp.where(keep, x2, jnp.zeros_like(x2))


def _attn_fwd(g, q4, k4, v4, qcb, kcb, vcb, comm=None):
    _, dil = GROUPS[g]
    B, r, L, _ = q4.shape
    nb = L // Q_BLOCK
    slopes = [float(s) for s in _SLOPES[g]]

    def body(q_ref, k_ref, v_ref, o_ref, lse_ref, bias, s_scr, p_scr):
        lane = lax.broadcasted_iota(jnp.int32, (Q_BLOCK, LANES), 1)
        if nb > 1:
            for h in range(HEADS):
                bias[h] = _bias(False, dil, slopes[h])

        def block(n, first):
            q0 = 0 if first else pl.multiple_of(n * Q_BLOCK, Q_BLOCK)
            k0 = 0 if first else pl.multiple_of((n - 1) * Q_BLOCK, Q_BLOCK)
            kw = Q_BLOCK if first else 2 * Q_BLOCK
            for hp in range(HEADS // 2):
                q2 = q_ref[pl.ds(q0, Q_BLOCK), _pair_cols(hp)]
                k2 = k_ref[pl.ds(k0, kw), _pair_cols(hp)]
                for e in range(2):
                    h = 2 * hp + e
                    b_h = _bias(True, dil, slopes[h]) if first else bias[h]
                    s_scr[h, :, :kw] = _scores(_half(q2, e), k2, b_h)
            st = jnp.zeros((Q_BLOCK, LANES), f32)
            dens = jnp.ones((Q_BLOCK, LANES), f32)
            for h in range(HEADS):
                s = s_scr[h, :, :kw]
                m = jnp.max(s, axis=-1, keepdims=True)
                p = jnp.exp(s - m)
                den = jnp.sum(p, axis=-1, keepdims=True)
                p_scr[h, :, :kw] = p.astype(bf16)
                st = jnp.where(lane == h, m + jnp.log(den), st)
                dens = jnp.where(lane == h, den, dens)
            lse_ref[pl.ds(q0, Q_BLOCK), :] = st
            inv = 1.0 / dens
            for hp in range(HEADS // 2):
                v2 = v_ref[pl.ds(k0, kw), _pair_cols(hp)]
                o2 = sum(jnp.dot(p_scr[2 * hp + e, :, :kw], _half(v2, e), preferred_element_type=f32) * inv[:, 2 * hp + e:2 * hp + e + 1]
                         for e in range(2))
                o_ref[pl.ds(q0, Q_BLOCK), _pair_cols(hp)] = o2.astype(bf16)

        block(0, True)
        if nb > 1:
            def step(n, carry):
                block(n, False)
                return carry
            lax.fori_loop(1, nb, step, 0)

    def spec(cb):
        return pl.BlockSpec((None, None, L, GW), lambda b, c: (b, c, 0, cb))

    return _launch(
        body, name=f"attn_fwd_g{g}", grid=(B, r), in_specs=[spec(qcb), spec(kcb), spec(vcb)],
        out_specs=[spec(0), pl.BlockSpec((None, None, L, LANES), lambda b, c: (b, c, 0, 0))],
        out_shape=(SDS((B, r, L, GW), bf16), SDS((B, r, L, LANES), f32)), args=(q4, k4, v4),
        scratch_shapes=[pltpu.VMEM((HEADS, Q_BLOCK, 2 * Q_BLOCK), f32), pltpu.VMEM((HEADS, Q_BLOCK, 2 * Q_BLOCK), f32),
                        pltpu.VMEM((HEADS, Q_BLOCK, 2 * Q_BLOCK), bf16)],
        sem=("parallel", "parallel"), comm=comm)


def _attn_bwd(g, q4, k4, v4, qcb, kcb, vcb, do4, lse4, dl4, comm=None):
    _, dil = GROUPS[g]
    B, r, L, _ = q4.shape
    nb = L // Q_BLOCK
    slopes = [float(s) for s in _SLOPES[g]]
    scale = HEAD_DIM ** -0.5

    def body(q_ref, k_ref, v_ref, do_ref, lse_ref, dl_ref, dq_ref, dk_ref, dv_ref, dk_acc, dv_acc, bias, s_scr, dp_scr, p_scr, ds_scr):
        dk_acc[...] = jnp.zeros_like(dk_acc)
        dv_acc[...] = jnp.zeros_like(dv_acc)
        if nb > 1:
            for h in range(HEADS):
                bias[h] = _bias(False, dil, slopes[h])

        def block(n, first):
            q0 = 0 if first else pl.multiple_of(n * Q_BLOCK, Q_BLOCK)
            k0 = 0 if first else pl.multiple_of((n - 1) * Q_BLOCK, Q_BLOCK)
            kw = Q_BLOCK if first else 2 * Q_BLOCK
            for hp in range(HEADS // 2):
                q2 = q_ref[pl.ds(q0, Q_BLOCK), _pair_cols(hp)]
                k2 = k_ref[pl.ds(k0, kw), _pair_cols(hp)]
                v2 = v_ref[pl.ds(k0, kw), _pair_cols(hp)]
                do2 = do_ref[pl.ds(q0, Q_BLOCK), _pair_cols(hp)]
                for e in range(2):
                    h = 2 * hp + e
                    b_h = _bias(True, dil, slopes[h]) if first else bias[h]
                    s_scr[h, :, :kw] = _scores(_half(q2, e), k2, b_h)
                    dp_scr[h, :, :kw] = lax.dot_general(_half(do2, e), v2, NT, preferred_element_type=f32)
            for h in range(HEADS):
                p = jnp.exp(s_scr[h, :, :kw] - lse_ref[pl.ds(q0, Q_BLOCK), h:h + 1])
                p_scr[h, :, :kw] = p.astype(bf16)
                ds_scr[h, :, :kw] = (p * (dp_scr[h, :, :kw] - dl_ref[pl.ds(q0, Q_BLOCK), h:h + 1])).astype(bf16)
            for hp in range(HEADS // 2):
                cols = _pair_cols(hp)
                q2 = q_ref[pl.ds(q0, Q_BLOCK), cols]
                k2 = k_ref[pl.ds(k0, kw), cols]
                do2 = do_ref[pl.ds(q0, Q_BLOCK), cols]
                ds = [ds_scr[2 * hp + e, :, :kw] for e in range(2)]
                dq2 = sum(jnp.dot(ds[e], _half(k2, e), preferred_element_type=f32) for e in range(2))
                dq_ref[pl.ds(q0, Q_BLOCK), cols] = (dq2 * scale).astype(bf16)
                dk2 = sum(lax.dot_general(ds[e], _half(q2, e), TN, preferred_element_type=f32) for e in range(2))
                dk_acc[pl.ds(k0, kw), cols] += dk2 * scale
                dv2 = sum(lax.dot_general(p_scr[2 * hp + e, :, :kw], _half(do2, e), TN, preferred_element_type=f32) for e in range(2))
                dv_acc[pl.ds(k0, kw), cols] += dv2

        block(0, True)
        if nb > 1:
            def step(n, carry):
                block(n, False)
                return carry
            lax.fori_loop(1, nb, step, 0)
        dk_ref[...] = dk_acc[...].astype(bf16)
        dv_ref[...] = dv_acc[...].astype(bf16)

    def spec(cb):
        return pl.BlockSpec((None, None, L, GW), lambda b, c: (b, c, 0, cb))

    st = pl.BlockSpec((None, None, L, LANES), lambda b, c: (b, c, 0, 0))
    osd = SDS((B, r, L, GW), bf16)
    return _launch(
        body, name=f"attn_bwd_g{g}", grid=(B, r), in_specs=[spec(qcb), spec(kcb), spec(vcb), spec(0), st, st],
        out_specs=[spec(0), spec(0), spec(0)], out_shape=(osd, osd, osd), args=(q4, k4, v4, do4, lse4, dl4),
        scratch_shapes=[pltpu.VMEM((L, GW), f32), pltpu.VMEM((L, GW), f32)]
        + [pltpu.VMEM((HEADS, Q_BLOCK, 2 * Q_BLOCK), f32)] * 3 + [pltpu.VMEM((HEADS, Q_BLOCK, 2 * Q_BLOCK), bf16)] * 2,
        sem=("parallel", "parallel"), comm=comm)


RT = 512
RCH = GW // LANES
DILS = tuple(d for _, d in GROUPS[1:])


def _res_spec(r, width):
    return pl.BlockSpec((None, r, RT // r, width), lambda b, i, *_: (b, 0, i, 0))


def _tok_spec(width, cb=0):
    return pl.BlockSpec((None, RT, width), lambda b, i, *_: (b, i, cb))


def _to_residues(res_ref, scr, r, width):
    for c in range(r):
        for jj in range(width // LANES):
            res_ref[c, :, jj * LANES:(jj + 1) * LANES] = scr[jj, pl.ds(c, RT // r, stride=r), :].astype(res_ref.dtype)


def _from_residues(scr, res_ref, r, width):
    for c in range(r):
        for jj in range(width // LANES):
            scr[jj, pl.ds(c, RT // r, stride=r), :] = res_ref[c, :, jj * LANES:(jj + 1) * LANES].astype(f32)


def _qkv_to_residues(proj3, g):
    r = GROUPS[g][1]
    B = proj3.shape[0]

    def body(q_ref, k_ref, v_ref, o_ref, scr):
        for p, x_ref in enumerate((q_ref, k_ref, v_ref)):
            x = x_ref[...].astype(f32)
            for jj in range(RCH):
                scr[p * RCH + jj] = x[:, jj * LANES:(jj + 1) * LANES]
        _to_residues(o_ref, scr, r, ATTN_W)

    return pl.pallas_call(
        body, grid=(B, SEQ // RT), in_specs=[_tok_spec(GW, 3 * p + g) for p in range(3)], out_specs=_res_spec(r, ATTN_W),
        out_shape=SDS((B, r, SEQ // r, ATTN_W), bf16), scratch_shapes=[pltpu.VMEM((3 * RCH, RT, LANES), f32)],
        name=f"qkv_to_residues_g{g}", compiler_params=_cp(("parallel", "parallel")))(proj3, proj3, proj3)


def _attn_mix(o0, l0, o_res, l_res):
    B = o0.shape[0]

    def body(o0_ref, l0_ref, o1_ref, o2_ref, l1_ref, l2_ref, y_ref, lt_ref, lt1_ref, lt2_ref, so, sl):
        for gi, (o_ref, l_ref, r) in enumerate(((o1_ref, l1_ref, DILS[0]), (o2_ref, l2_ref, DILS[1]))):
            _from_residues(so.at[gi], o_ref, r, GW)
            _from_residues(sl.at[gi:gi + 1], l_ref, r, LANES)
        ls = [l0_ref[...], sl[0], sl[1]]
        m = functools.reduce(jnp.maximum, ls)
        ws = [jnp.exp(l - m) for l in ls]
        den = ws[0] + ws[1] + ws[2]
        alphas = [w / den for w in ws]
        lt = m + jnp.log(den)
        lt_ref[...] = lt
        sl[2] = lt
        _to_residues(lt1_ref, sl.at[2:3], DILS[0], LANES)
        _to_residues(lt2_ref, sl.at[2:3], DILS[1], LANES)
        for h in range(HEADS):
            cols = slice(h * HEAD_DIM, (h + 1) * HEAD_DIM)
            jj, lo = divmod(h * HEAD_DIM, LANES)
            acc = alphas[0][:, h:h + 1] * o0_ref[:, cols].astype(f32)
            for gi in range(2):
                acc = acc + alphas[gi + 1][:, h:h + 1] * so[gi, jj, :, lo:lo + HEAD_DIM]
            y_ref[:, cols] = acc.astype(bf16)

    in_specs = [_tok_spec(GW), _tok_spec(LANES), _res_spec(DILS[0], GW), _res_spec(DILS[1], GW), _res_spec(DILS[0], LANES), _res_spec(DILS[1], LANES)]
    out_specs = [_tok_spec(GW), _tok_spec(LANES), _res_spec(DILS[0], LANES), _res_spec(DILS[1], LANES)]
    return pl.pallas_call(
        body, grid=(B, SEQ // RT), in_specs=in_specs, out_specs=out_specs,
        out_shape=(SDS((B, SEQ, GW), bf16), SDS((B, SEQ, LANES), f32)) + tuple(SDS((B, r, SEQ // r, LANES), f32) for r in DILS),
        scratch_shapes=[pltpu.VMEM((2, RCH, RT, LANES), f32), pltpu.VMEM((3, RT, LANES), f32)],
        name="attn_mix", compiler_params=_cp(("parallel", "parallel")))(o0, l0, *o_res, *l_res)


def _attn_delta(dmix, mix):
    B = dmix.shape[0]

    def body(d_ref, y_ref, dl_ref, dl1_ref, dl2_ref, dm1_ref, dm2_ref, sx, sd):
        lane = lax.broadcasted_iota(jnp.int32, (RT, LANES), 1)
        acc = jnp.zeros((RT, LANES), f32)
        dv = d_ref[...].astype(f32)
        for jj in range(RCH):
            sx[jj] = dv[:, jj * LANES:(jj + 1) * LANES]
        for h in range(HEADS):
            cols = slice(h * HEAD_DIM, (h + 1) * HEAD_DIM)
            dl = jnp.sum(dv[:, cols] * y_ref[:, cols].astype(f32), axis=-1, keepdims=True)
            acc = jnp.where(lane == h, dl, acc)
        dl_ref[...] = acc
        sd[0] = acc
        _to_residues(dl1_ref, sd, DILS[0], LANES)
        _to_residues(dl2_ref, sd, DILS[1], LANES)
        _to_residues(dm1_ref, sx, DILS[0], GW)
        _to_residues(dm2_ref, sx, DILS[1], GW)

    return pl.pallas_call(
        body, grid=(B, SEQ // RT), in_specs=[_tok_spec(GW), _tok_spec(GW)],
        out_specs=[_tok_spec(LANES), _res_spec(DILS[0], LANES), _res_spec(DILS[1], LANES), _res_spec(DILS[0], GW), _res_spec(DILS[1], GW)],
        out_shape=(SDS((B, SEQ, LANES), f32),) + tuple(SDS((B, r, SEQ // r, LANES), f32) for r in DILS)
        + tuple(SDS((B, r, SEQ // r, GW), bf16) for r in DILS),
        scratch_shapes=[pltpu.VMEM((RCH, RT, LANES), f32), pltpu.VMEM((1, RT, LANES), f32)],
        name="attn_delta", compiler_params=_cp(("parallel", "parallel")))(dmix, mix)


N_CB = IN_W // CB


def _assemble_dproj(dqkv, dua, dub, dga, dgc):
    B = dua.shape[0]
    ng = len(GROUPS)
    flat = [dqkv[g][p] for p in range(3) for g in range(ng)]
    wide = [dua, dub, dga, dgc]

    def body(*refs):
        srcs, wides, o_ref, scr = refs[:3 * ng], refs[3 * ng:3 * ng + 4], refs[3 * ng + 4], refs[3 * ng + 5]
        for jv in range(3 * ng):
            g = jv % ng
            if g == 0:
                o_ref[:, jv * GW:(jv + 1) * GW] = srcs[jv][...]
            else:
                _from_residues(scr, srcs[jv], GROUPS[g][1], GW)
                for jj in range(RCH):
                    o_ref[:, jv * GW + jj * LANES:jv * GW + (jj + 1) * LANES] = scr[jj].astype(bf16)
        for wv in range(4):
            lo = 3 * ATTN_W + wv * D_MODEL
            o_ref[:, lo:lo + D_MODEL] = wides[wv][...]

    in_specs = [_tok_spec(GW) if (jv % ng) == 0 else _res_spec(GROUPS[jv % ng][1], GW) for jv in range(3 * ng)]
    in_specs += [_tok_spec(D_MODEL)] * 4
    return pl.pallas_call(
        body, grid=(B, SEQ // RT), in_specs=in_specs, out_specs=_tok_spec(IN_W),
        out_shape=SDS((B, SEQ, IN_W), bf16), scratch_shapes=[pltpu.VMEM((RCH, RT, LANES), f32)],
        name="assemble_dproj", compiler_params=_cp(("parallel", "parallel")))(*flat, *wide)


def _same_shape_groups(names, arrays):
    groups = {}
    for n, a in zip(names, arrays):
        groups.setdefault(a.shape, ([], []))
        groups[a.shape][0].append(n)
        groups[a.shape][1].append(a)
    return list(groups.values())


def _add_pairs(parts, gots, core, name):
    k = len(parts)
    n, h, C = gots[0].shape

    def body(c_ref, *refs):
        for i in range(k):
            refs[2 * k + i][...] = (refs[i][...].astype(f32) + refs[k + i][...].astype(f32)).astype(bf16)

    blk = pl.BlockSpec((None, h, C), lambda s, c_ref: (s, 0, 0))
    own = pl.BlockSpec((None, h, C), lambda s, c_ref: (s, c_ref[0], 0))
    spec = pltpu.PrefetchScalarGridSpec(num_scalar_prefetch=1, grid=(n,), in_specs=[own] * k + [blk] * k, out_specs=[blk] * k)
    return pl.pallas_call(body, grid_spec=spec, out_shape=tuple(SDS((n, h, C), bf16) for _ in range(k)), name=name,
                          compiler_params=_cp(("parallel",)))(core, *parts, *gots)


def _sum4(sums, gots, chip, core, name):
    k = len(sums)
    _, h, C = sums[0].shape

    def body(s_ref, c_ref, *refs):
        for i in range(k):
            q_ref = refs[k + i]
            t = refs[i][...].astype(f32) + q_ref[0].astype(f32)
            t = t + q_ref[1].astype(f32)
            refs[2 * k + i][...] = t + q_ref[2].astype(f32)

    spec = pltpu.PrefetchScalarGridSpec(
        num_scalar_prefetch=2, grid=(1,),
        in_specs=[pl.BlockSpec((None, h, C), lambda i, s_ref, c_ref: (s_ref[0], 0, 0))] * k
        + [pl.BlockSpec((N_CHIPS - 1, h, C), lambda i, s_ref, c_ref: (0, 0, 0))] * k,
        out_specs=[pl.BlockSpec((h, C), lambda i, s_ref, c_ref: (c_ref[0], 0))] * k)
    return pl.pallas_call(body, grid_spec=spec, out_shape=tuple(SDS((2 * h, C), f32) for _ in range(k)), name=name,
                          compiler_params=_cp(("arbitrary",)))(chip, core, *sums, *gots)


def _adamw(ws, gs, ms, vs, name):
    k = len(ws)
    R, C = ws[0].shape
    rt = R
    for cand in (512, 256, 128, 64, 32, 16, 8):
        if R % cand == 0 and cand * C * 4 <= 2 ** 21:
            rt = cand
            break
    c1 = 1.0 / (1.0 - B1 ** STEP)
    c2 = 1.0 / (1.0 - B2 ** STEP)

    def body(*refs):
        for i in range(k):
            w_ref, g_ref, m_ref, v_ref = (refs[j * k + i] for j in range(4))
            d_ref, nm_ref, nv_ref = (refs[(4 + j) * k + i] for j in range(3))
            gv = g_ref[...]
            nm = B1 * m_ref[...] + (1.0 - B1) * gv
            nv = B2 * v_ref[...] + (1.0 - B2) * (gv * gv)
            nm_ref[...] = nm
            nv_ref[...] = nv
            d_ref[...] = -LR * ((nm * c1) / (jnp.sqrt(nv * c2) + ADAM_EPS) + WD * w_ref[...])

    blk = pl.BlockSpec((rt, C), lambda i: (i, 0))
    sd = SDS((R, C), f32)
    res = pl.pallas_call(body, grid=(R // rt,), in_specs=[blk] * (4 * k), out_specs=[blk] * (3 * k), out_shape=(sd,) * (3 * k),
                         name=name, compiler_params=_cp(("parallel",)))(*ws, *gs, *ms, *vs)
    return res[:k], res[k:2 * k], res[2 * k:]


def _coords():
    return lax.axis_index("x"), lax.axis_index("y"), lax.axis_index("c")


def _other_chips(x, y):
    return [(1 - x, y), (x, 1 - y), (1 - x, 1 - y)]


def _allgather_weights(shards, slots):
    n = len(shards)
    n_slot = 12

    def body(*refs):
        ins, outs = refs[:n], refs[2 * n:3 * n]
        send, recv = refs[3 * n:]
        x, y, c = _coords()
        me, cx, cy, cd = 2 * x + y, 2 * (1 - x) + y, 2 * x + (1 - y), 2 * (1 - x) + (1 - y)
        dev_x, dev_y, dev_s = (1 - x, y, c), (x, 1 - y, c), (x, y, 1 - c)
        started = []

        def copy(w, slot, src, dst, dev):
            return pltpu.make_async_remote_copy(src_ref=src, dst_ref=dst, send_sem=send.at[w, slot], recv_sem=recv.at[w, slot],
                                                device_id=dev, device_id_type=MESH)

        def go(cp):
            cp.start()
            started.append(cp)

        for w in range(n):
            q = shards[w].shape[0] // 4
            rows = [pl.ds(c * 2 * q + k * q, q) for k in range(2)]
            theirs = [pl.ds((1 - c) * 2 * q + k * q, q) for k in range(2)]
            own = [(ins[w].at[r, :], outs[w].at[me, r, :]) for r in rows]
            go(copy(w, 0, *own[0], dev_x))
            go(copy(w, 2, *own[1], dev_y))
            go(copy(w, 1, *own[1], dev_x))
            go(copy(w, 3, *own[0], dev_y))
            arrivals = [(0, cx, 0, (4, dev_y)), (2, cy, 1, (5, dev_x)), (1, cx, 1, None), (3, cy, 0, None), (4, cd, 0, None), (5, cd, 1, None)]
            for k, (slot, chip, quarter, onward) in enumerate(arrivals):
                blk = outs[w].at[chip, rows[quarter], :]
                copy(w, slot, blk, blk, dev_s).wait_recv()
                if onward is not None:
                    go(copy(w, onward[0], blk, blk, onward[1]))
                go(copy(w, 6 + k, blk, blk, dev_s))
            for k, (slot, chip, quarter, onward) in enumerate(arrivals):
                blk = outs[w].at[chip, theirs[quarter], :]
                copy(w, 6 + k, blk, blk, dev_s).wait_recv()
        for cp in started:
            cp.wait_send()

    return pl.pallas_call(
        body, in_specs=[HBM] * (2 * n), out_specs=[HBM] * n,
        out_shape=tuple(SDS((N_CHIPS,) + s.shape, s.dtype) for s in shards),
        input_output_aliases={n + w: w for w in range(n)},
        scratch_shapes=[pltpu.SemaphoreType.DMA((n, n_slot))] * 2,
        name="allgather_weights", compiler_params=pltpu.CompilerParams(has_side_effects=True))(*shards, *slots)


def _x_pair_exchange(parts):
    n = len(parts)
    halves = [p.shape[1] // 2 for p in parts]

    def copies(ins, outs, sems):
        send, recv = sems
        x, y, c = _coords()
        return [pltpu.make_async_remote_copy(src_ref=ins[w].at[:, pl.ds((1 - c) * halves[w], halves[w]), :], dst_ref=outs[w],
                                             send_sem=send.at[w], recv_sem=recv.at[w], device_id=(x, y, 1 - c), device_id_type=MESH)
                for w in range(n)]

    def start(ins, outs, sems):
        for cp in copies(ins, outs, sems):
            cp.start()

    def wait(ins, outs, sems):
        for cp in copies(ins, outs, sems):
            cp.wait()

    return _Exchange(ins=list(parts), out_shape=[SDS((N_CHIPS, p.shape[1] // 2, p.shape[2]), p.dtype) for p in parts],
                     scratch=[pltpu.SemaphoreType.DMA((n,))] * 2, aliases={}, start=start, wait=wait)


def _x_chip_exchange(sums):
    n = len(sums)

    def copies(ins, outs, sems):
        send, recv = sems
        x, y, c = _coords()
        return [pltpu.make_async_remote_copy(src_ref=ins[w].at[2 * px + py], dst_ref=outs[w].at[j], send_sem=send.at[w, j],
                                             recv_sem=recv.at[w, j], device_id=(px, py, c), device_id_type=MESH)
                for w in range(n) for j, (px, py) in enumerate(_other_chips(x, y))]

    def start(ins, outs, sems):
        for cp in copies(ins, outs, sems):
            cp.start()

    def wait(ins, outs, sems):
        for cp in copies(ins, outs, sems):
            cp.wait()

    return _Exchange(ins=list(sums), out_shape=[SDS((N_CHIPS - 1,) + s.shape[1:], s.dtype) for s in sums],
                     scratch=[pltpu.SemaphoreType.DMA((n, 3)), pltpu.SemaphoreType.DMA((n, 3))], aliases={}, start=start, wait=wait)


def _x_gather_ici(shards, slots):
    n = len(shards)
    halves = [s.shape[0] // 2 for s in shards]

    def copies(ins, outs, sems):
        send, recv = sems
        x, y, c = _coords()
        me = 2 * x + y
        out = []
        for w in range(n):
            mine = pl.ds(c * halves[w], halves[w])
            for j, (px, py) in enumerate(_other_chips(x, y)):
                snd = pltpu.make_async_remote_copy(src_ref=ins[w].at[mine, :], dst_ref=outs[w].at[me, mine, :], send_sem=send.at[w, j],
                                                   recv_sem=recv.at[w, j], device_id=(px, py, c), device_id_type=MESH)
                got = outs[w].at[2 * px + py, mine, :]
                rcv = pltpu.make_async_remote_copy(src_ref=got, dst_ref=got, send_sem=send.at[w, j], recv_sem=recv.at[w, j],
                                                   device_id=(px, py, c), device_id_type=MESH)
                out.append((snd, rcv))
        return out

    def start(ins, outs, sems):
        for snd, _ in copies(ins, outs, sems):
            snd.start()

    def wait(ins, outs, sems):
        for snd, rcv in copies(ins, outs, sems):
            rcv.wait_recv()
            snd.wait_send()

    return _Exchange(ins=list(shards) + list(slots), out_shape=[SDS(s.shape, s.dtype) for s in slots],
                     scratch=[pltpu.SemaphoreType.DMA((n, 3)), pltpu.SemaphoreType.DMA((n, 3))],
                     aliases={n + w: w for w in range(n)}, start=start, wait=wait)


def _x_gather_fwd(bufs):
    n = len(bufs)
    halves = [b.shape[1] // 2 for b in bufs]

    def copies(ins, outs, sems):
        send, recv = sems
        x, y, c = _coords()
        out = []
        for w in range(n):
            for j, (px, py) in enumerate(_other_chips(x, y)):
                mine = outs[w].at[2 * px + py, pl.ds(c * halves[w], halves[w]), :]
                theirs = outs[w].at[2 * px + py, pl.ds((1 - c) * halves[w], halves[w]), :]
                snd = pltpu.make_async_remote_copy(src_ref=mine, dst_ref=mine, send_sem=send.at[w, j], recv_sem=recv.at[w, j],
                                                   device_id=(x, y, 1 - c), device_id_type=MESH)
                rcv = pltpu.make_async_remote_copy(src_ref=theirs, dst_ref=theirs, send_sem=send.at[w, j], recv_sem=recv.at[w, j],
                                                   device_id=(x, y, 1 - c), device_id_type=MESH)
                out.append((snd, rcv))
        return out

    def start(ins, outs, sems):
        for snd, _ in copies(ins, outs, sems):
            snd.start()

    def wait(ins, outs, sems):
        for snd, rcv in copies(ins, outs, sems):
            rcv.wait_recv()
            snd.wait_send()

    return _Exchange(ins=list(bufs), out_shape=[SDS(b.shape, b.dtype) for b in bufs],
                     scratch=[pltpu.SemaphoreType.DMA((n, 3)), pltpu.SemaphoreType.DMA((n, 3))],
                     aliases={w: w for w in range(n)}, start=start, wait=wait)


def _final_exchange(bufs, vec):
    n = len(bufs)
    R = vec.shape[0]
    nd = 8

    def body(*refs):
        v_ref = refs[n]
        outs = refs[n + 1:2 * n + 1]
        o_ref, sbuf, send, recv, ssend, srecv = refs[2 * n + 1:]
        x, y, c = _coords()
        me = 4 * x + 2 * y + c
        sbuf[me] = v_ref[...]
        small = []
        for k in range(1, nd):
            kx, ky, kc = (k >> 2) & 1, (k >> 1) & 1, k & 1
            tx = x + kx - 2 * x * kx
            ty = y + ky - 2 * y * ky
            tc = c + kc - 2 * c * kc
            cp = pltpu.make_async_remote_copy(src_ref=v_ref, dst_ref=sbuf.at[me], send_sem=ssend.at[k], recv_sem=srecv.at[k],
                                              device_id=(tx, ty, tc), device_id_type=MESH)
            cp.start()
            small.append((cp, 4 * tx + 2 * ty + tc))
        cps = []
        for w in range(n):
            h = bufs[w].shape[0] // 2
            rows = outs[w].at[pl.ds(c * h, h), :]
            rc = pltpu.make_async_remote_copy(src_ref=rows, dst_ref=rows, send_sem=send.at[w], recv_sem=recv.at[w],
                                              device_id=(x, y, 1 - c), device_id_type=MESH)
            rc.start()
            cps.append(rc)
        for k, (cp, src) in zip(range(1, nd), small):
            pltpu.make_async_remote_copy(src_ref=v_ref, dst_ref=sbuf.at[src], send_sem=ssend.at[k], recv_sem=srecv.at[k],
                                         device_id=(x, y, c), device_id_type=MESH).wait_recv()
        for cp, _ in small:
            cp.wait_send()
        acc = sbuf[0]
        for d in range(1, nd):
            acc = acc + sbuf[d]
        o_ref[...] = acc
        for w, rc in enumerate(cps):
            h = bufs[w].shape[0] // 2
            other = outs[w].at[pl.ds((1 - c) * h, h), :]
            pltpu.make_async_remote_copy(src_ref=other, dst_ref=other, send_sem=send.at[w], recv_sem=recv.at[w],
                                         device_id=(x, y, 1 - c), device_id_type=MESH).wait_recv()
            rc.wait_send()

    vm = pl.BlockSpec(memory_space=pltpu.VMEM)
    res = pl.pallas_call(body, in_specs=[HBM] * n + [vm], out_specs=[HBM] * n + [vm],
                         out_shape=tuple(SDS(a.shape, a.dtype) for a in bufs) + (SDS((R, LANES), f32),),
                         input_output_aliases={w: w for w in range(n)},
                         scratch_shapes=[pltpu.VMEM((nd, R, LANES), f32)] + [pltpu.SemaphoreType.DMA((n,))] * 2
                         + [pltpu.SemaphoreType.DMA((nd,))] * 2, name="final_exchange",
                         compiler_params=pltpu.CompilerParams(has_side_effects=True))(*bufs, vec)
    return res[:n], res[n]


def _rows128(a, rows):
    flat = a.reshape(-1)
    return jnp.pad(flat, (0, rows * LANES - flat.shape[0])).reshape(rows, LANES)


GATHER_1 = ("w_conv_out", "w_attn_out", "w_o", "w_ffn_gate", "w_ffn_down")
REDUCE_A = ("w_ffn_gate", "w_ffn_up", "w_ffn_down")
REDUCE_B = ("w_conv_out", "w_attn_out", "w_o")
REDUCE_C = ("w_in",)


def _step(x, target, norm1_g, gate_b, conv_w, conv_b, conv_ln_g, conv_ln_b, norm2_g, norm_f_g, w_in, shards, slots, chip1, core):
    B = x.shape[0]
    T = B * SEQ
    xf = x.reshape(T, D_MODEL)
    tf = target.reshape(T, D_MODEL)

    h, h_t = _rms_fwd(xf, norm1_g, "rms1_fwd")
    proj, got1 = _mm("in_proj", h, w_in, grid=(N_CHIPS, T // 1024),
                     a_spec=pl.BlockSpec((1024, D_MODEL), lambda s, m: (m, 0)),
                     b_spec=pl.BlockSpec((None, D_MODEL, IN_S), lambda s, m: (s, 0, 0)),
                     o_spec=pl.BlockSpec((1024, IN_S), lambda s, m: (m, s)), o_shape=(T, IN_W), o_dtype=bf16, dims=NN,
                     comm=_x_gather_ici([shards[n] for n in GATHER_1], [slots[n] for n in GATHER_1]))
    proj3 = proj.reshape(B, SEQ, IN_W)
    proj4 = proj.reshape(B, 1, SEQ, IN_W)

    qkv = [None] + [_qkv_to_residues(proj3, g) for g in range(1, len(GROUPS))]

    def qkv_args(g):
        return (proj4, proj4, proj4, 0, 3, 6) if g == 0 else (qkv[g], qkv[g], qkv[g], 0, 1, 2)

    (o4_0, l4_0), full1 = _attn_fwd(0, *qkv_args(0), comm=_x_gather_fwd(list(got1)))
    o4_1, l4_1 = _attn_fwd(1, *qkv_args(1))
    o4_2, l4_2 = _attn_fwd(2, *qkv_args(2))
    full = dict(zip(GATHER_1, full1))
    w_conv_out, w_attn_out, w_o, w_gate, w_down = (full[n] for n in GATHER_1)
    w_conv_out_f = w_conv_out.reshape(D_MODEL, D_MODEL)
    w_o_f = w_o.reshape(D_MODEL, D_MODEL)
    mix3, lse3, lse_r1, lse_r2 = _attn_mix(o4_0.reshape(B, SEQ, GW), l4_0.reshape(B, SEQ, LANES), [o4_1, o4_2], [l4_1, l4_2])
    mix = mix3.reshape(T, GW)
    y_attn = _attn_out_fwd(mix, w_attn_out)

    c1, got_up = _glu_conv_fwd(proj3, conv_w, conv_b, comm=_x_gather_ici([shards["w_ffn_up"]], [slots["w_ffn_up"]]))
    c1 = c1.reshape(T, D_MODEL)
    (c3, y_conv), (w_up,) = _conv_out_fwd(c1, conv_ln_g, conv_ln_b, w_conv_out_f, comm=_x_gather_fwd(list(got_up)))

    merged, x1, h2 = _merge_o_proj_rms(proj, gate_b, y_attn, y_conv, w_o_f, xf, norm2_g)
    fa, fb, ff = _ffn_up(h2, w_gate, w_up)
    loss, dx2, d_gf = _ffn_down_loss(ff, w_down, x1, norm_f_g.reshape(1, D_MODEL), tf)

    d_w_down = _mm("d_w_down", ff, dx2, grid=(N_CHIPS, T // 2048),
                   a_spec=pl.BlockSpec((None, 2048, FF_S), lambda s, k: (s, k, 0)),
                   b_spec=pl.BlockSpec((2048, D_MODEL), lambda s, k: (k, 0)),
                   o_spec=pl.BlockSpec((None, FF_S, D_MODEL), lambda s, k: (s, 0, 0)), o_shape=(N_CHIPS, FF_S, D_MODEL),
                   o_dtype=bf16, dims=TN, acc_shape=(FF_S, D_MODEL), k_axis=1, sem=("parallel", "arbitrary"))
    da, db = _ffn_down_bwd(dx2, w_down, fa, fb)

    d_w_gate, d_w_up = _ffn_up_bwd_w(da, db, h2)
    part = dict(w_ffn_gate=d_w_gate, w_ffn_up=d_w_up, w_ffn_down=d_w_down)

    def pair_sums(names, got):
        out = {}
        for ns, gs in _same_shape_groups(names, got):
            out.update(zip(ns, _add_pairs([part[n] for n in ns], gs, core, "pair_sum_" + ns[0])))
        return [out[n] for n in names]

    (dx1, d_g2), got = _mm_rms_bwd("ffn_dh2", [(da, w_gate), (db, w_up)], x1, norm2_g, dx2, tm=1024,
                                   a_spec=pl.BlockSpec((None, 1024, FF_S), lambda m, s: (s, m, 0)),
                                   b_spec=pl.BlockSpec((None, FF_S, D_MODEL), lambda m, s: (s, 0, 0)), dims=NN,
                                   comm=_x_pair_exchange([part[n] for n in REDUCE_A]), vmem=VMEM_LIMIT_BIG)
    sums_a = pair_sums(REDUCE_A, got)

    d_w_o = _mm_tn_tokens("d_w_o", merged, dx1, bf16).reshape(N_CHIPS, D_MODEL // N_CHIPS, D_MODEL)
    dmerged = _mm_nt_full("d_merged", dx1, w_o_f, bf16)
    dya, dyc, dga, dgc, d_gba, d_gbc = _merge_bwd(dmerged, proj, gate_b, y_attn, y_conv)

    d_w_conv_out = _mm_tn_tokens("d_w_conv_out", c3, dyc, bf16).reshape(N_CHIPS, D_MODEL // N_CHIPS, D_MODEL)
    dc1, d_ln_g, d_ln_b = _conv_out_bwd(dyc, w_conv_out_f, c1, conv_ln_g, conv_ln_b)
    (dua, dub, d_conv_w, d_conv_b), got_a = _glu_conv_bwd(dc1.reshape(B, SEQ, D_MODEL), proj3, conv_w, comm=_x_chip_exchange(sums_a))

    d_w_attn_out = _attn_out_bwd_w(mix, dya)
    part.update(w_conv_out=d_w_conv_out, w_attn_out=d_w_attn_out, w_o=d_w_o)
    dmix, got = _attn_out_bwd_x(dya, w_attn_out, comm=_x_pair_exchange([part[n] for n in REDUCE_B]))
    sums_b = pair_sums(REDUCE_B, got)
    dmix3 = dmix.reshape(B, SEQ, GW)
    delta3, delta_r1, delta_r2, dmix_r1, dmix_r2 = _attn_delta(dmix3, mix3)
    one = (B, 1, SEQ)
    (dq0, dk0, dv0), got_b = _attn_bwd(0, *qkv_args(0), dmix3.reshape(one + (GW,)), lse3.reshape(one + (LANES,)),
                                       delta3.reshape(one + (LANES,)), comm=_x_chip_exchange(sums_b))
    dqkv = [tuple(t.reshape(B, SEQ, GW) for t in (dq0, dk0, dv0)),
            _attn_bwd(1, *qkv_args(1), dmix_r1, lse_r1, delta_r1), _attn_bwd(2, *qkv_args(2), dmix_r2, lse_r2, delta_r2)]
    dproj = _assemble_dproj(dqkv, dua, dub, dga.reshape(B, SEQ, D_MODEL), dgc.reshape(B, SEQ, D_MODEL)).reshape(T, IN_W)

    d_w_in = _mm("d_w_in", h_t, dproj, grid=(N_CHIPS, T // 2048),
                 a_spec=pl.BlockSpec((D_MODEL, 2048), lambda s, k: (0, k)),
                 b_spec=pl.BlockSpec((2048, IN_S), lambda s, k: (k, s)),
                 o_spec=pl.BlockSpec((None, D_MODEL, IN_S), lambda s, k: (s, 0, 0)), o_shape=(N_CHIPS, D_MODEL, IN_S),
                 o_dtype=bf16, dims=NN, acc_shape=(D_MODEL, IN_S), k_axis=1, sem=("parallel", "arbitrary"), vmem=VMEM_LIMIT_BIG)
    part.update(w_in=d_w_in)
    sums_c = pair_sums(REDUCE_C, _run_exchange(_x_pair_exchange([d_w_in]), "grad_pair_exchange_c"))
    (dx, d_g1), got_c = _mm_rms_bwd("d_h", [(dproj, w_in)], xf, norm1_g, dx1, tm=1024,
                                    a_spec=pl.BlockSpec((1024, IN_S), lambda m, s: (m, s)),
                                    b_spec=pl.BlockSpec((None, D_MODEL, IN_S), lambda m, s: (s, 0, 0)), dims=NT,
                                    comm=_x_chip_exchange(sums_c), vmem=VMEM_LIMIT_BIG)

    names = REDUCE_A + REDUCE_B + REDUCE_C
    sums = dict(zip(names, sums_a + sums_b + sums_c))
    halves = {}
    for ns, gs in _same_shape_groups(names, got_a + got_b + got_c):
        halves.update(zip(ns, _sum4([sums[n] for n in ns], gs, chip1, core, "chip_sum_" + ns[0])))
    big = {n: halves[n] for n in names}
    small = dict(norm1_g=d_g1, gate_b=jnp.concatenate([d_gba, d_gbc], axis=-1), conv_b=d_conv_b, conv_ln_g=d_ln_g,
                 conv_ln_b=d_ln_b, norm2_g=d_g2, norm_f_g=d_gf, conv_w=d_conv_w)
    return loss, dx.reshape(B, SEQ, D_MODEL), big, small


BIG = ("w_in", "w_conv_out", "w_attn_out", "w_o", "w_ffn_gate", "w_ffn_up", "w_ffn_down")
TRANSPOSED = ("w_ffn_gate", "w_ffn_up")
SMALL = ("norm1_g", "gate_b", "conv_b", "conv_ln_g", "conv_ln_b", "norm2_g", "norm_f_g")
SMALL_ROWS = {"norm1_g": 8, "gate_b": 16, "conv_b": 8, "conv_ln_g": 8, "conv_ln_b": 8, "norm2_g": 8, "norm_f_g": 8}
LOSS_ROWS = 8
CONVW_ROWS = 32 * D_MODEL // LANES


def kernel(x, norm1_g, w_in, gate_b, conv_w, conv_b, conv_ln_g, conv_ln_b, w_conv_out, w_attn_out, w_o, norm2_g, w_ffn_gate, w_ffn_up, w_ffn_down, norm_f_g, loss_target, m_norm1_g, m_w_in, m_gate_b, m_conv_w, m_conv_b, m_conv_ln_g, m_conv_ln_b, m_w_conv_out, m_w_attn_out, m_w_o, m_norm2_g, m_w_ffn_gate, m_w_ffn_up, m_w_ffn_down, m_norm_f_g, v_norm1_g, v_w_in, v_gate_b, v_conv_w, v_conv_b, v_conv_ln_g, v_conv_ln_b, v_w_conv_out, v_w_attn_out, v_w_o, v_norm2_g, v_w_ffn_gate, v_w_ffn_up, v_w_ffn_down, v_norm_f_g):
    W = dict(norm1_g=norm1_g, w_in=w_in, gate_b=gate_b, conv_w=conv_w, conv_b=conv_b, conv_ln_g=conv_ln_g, conv_ln_b=conv_ln_b,
             w_conv_out=w_conv_out, w_attn_out=w_attn_out, w_o=w_o, norm2_g=norm2_g, w_ffn_gate=w_ffn_gate, w_ffn_up=w_ffn_up,
             w_ffn_down=w_ffn_down, norm_f_g=norm_f_g)
    M = dict(norm1_g=m_norm1_g, w_in=m_w_in, gate_b=m_gate_b, conv_w=m_conv_w, conv_b=m_conv_b, conv_ln_g=m_conv_ln_g,
             conv_ln_b=m_conv_ln_b, w_conv_out=m_w_conv_out, w_attn_out=m_w_attn_out, w_o=m_w_o, norm2_g=m_norm2_g,
             w_ffn_gate=m_w_ffn_gate, w_ffn_up=m_w_ffn_up, w_ffn_down=m_w_ffn_down, norm_f_g=m_norm_f_g)
    V = dict(norm1_g=v_norm1_g, w_in=v_w_in, gate_b=v_gate_b, conv_w=v_conv_w, conv_b=v_conv_b, conv_ln_g=v_conv_ln_g,
             conv_ln_b=v_conv_ln_b, w_conv_out=v_w_conv_out, w_attn_out=v_w_attn_out, w_o=v_w_o, norm2_g=v_norm2_g,
             w_ffn_gate=v_w_ffn_gate, w_ffn_up=v_w_ffn_up, w_ffn_down=v_w_ffn_down, norm_f_g=v_norm_f_g)
    order = list(W)

    def as2d(n, a):
        a = a.reshape(a.shape[-2:])
        return a.T if n in TRANSPOSED else a

    def from2d(n, a):
        return (a.T if n in TRANSPOSED else a).reshape(W[n].shape)

    shard2d = {n: as2d(n, W[n]) for n in BIG}
    chip = 2 * lax.axis_index("x") + lax.axis_index("y")

    core = lax.axis_index("c").astype(jnp.int32).reshape(1)
    chip1 = chip.astype(jnp.int32).reshape(1)
    shards = {n: shard2d[n].astype(bf16) for n in BIG}

    def slot_for(s):
        return lax.dynamic_update_slice(lax.empty((N_CHIPS,) + s.shape, s.dtype), s[None], (chip, 0, 0))

    slots = {n: slot_for(s) for n, s in shards.items()}
    cw = jnp.pad(conv_w.reshape(CONV_K, D_MODEL // N_CHIPS), ((0, CONV_HALO - CONV_K), (0, 0)))
    w_in_full, cw4 = _allgather_weights([shards["w_in"], cw], [slots["w_in"], slot_for(cw)])
    conv_w_full = cw4.transpose(1, 0, 2).reshape(CONV_HALO, D_MODEL)[:CONV_K]

    loss, grad_x, grads, gsmall = _step(x, loss_target, norm1_g, gate_b, conv_w_full, conv_b, conv_ln_g, conv_ln_b, norm2_g,
                                        norm_f_g, w_in_full, shards, slots, chip1, core)

    pieces = [_rows128(loss, LOSS_ROWS)] + [_rows128(gsmall[n], SMALL_ROWS[n]) for n in SMALL] + [_rows128(gsmall["conv_w"], CONVW_ROWS)]
    full_rows, tot = _final_exchange([grads[n] for n in BIG], jnp.concatenate(pieces, axis=0))
    grads = dict(zip(BIG, full_rows))
    loss_out = tot[0, 0]
    row = LOSS_ROWS
    for n in SMALL:
        grads[n] = tot[row: row + W[n].size // LANES].reshape(W[n].shape)
        row += SMALL_ROWS[n]
    dcw = tot[row: row + CONV_K * D_MODEL // LANES].reshape(CONV_K, D_MODEL)
    grads["conv_w"] = lax.dynamic_slice(dcw, (0, chip * (D_MODEL // N_CHIPS)), (CONV_K, D_MODEL // N_CHIPS))

    delta, new_m, new_v = {}, {}, {}
    for ns, ws in _same_shape_groups(BIG, [shard2d[n] for n in BIG]):
        ds, nms, nvs = _adamw(ws, [grads[n] for n in ns], [as2d(n, M[n]) for n in ns], [as2d(n, V[n]) for n in ns], "adamw_" + ns[0])
        for n, d, nm, nv in zip(ns, ds, nms, nvs):
            delta[n], new_m[n], new_v[n], grads[n] = (from2d(n, t) for t in (d, nm, nv, grads[n]))

    def pack(src):
        return jnp.concatenate([_rows128(src[n], SMALL_ROWS[n]) for n in SMALL], axis=0)

    (d,), (nm,), (nv,) = _adamw([pack(W)], [pack(grads)], [pack(M)], [pack(V)], "adamw_small")
    row = 0
    for n in SMALL:
        k = W[n].size // LANES
        delta[n], new_m[n], new_v[n] = (t[row: row + k].reshape(W[n].shape) for t in (d, nm, nv))
        row += SMALL_ROWS[n]

    def pad32(a):
        return jnp.pad(a.reshape(CONV_K, D_MODEL // N_CHIPS), ((0, 1), (0, 0)))

    (d,), (nm,), (nv,) = _adamw([pad32(conv_w)], [pad32(grads["conv_w"])], [pad32(m_conv_w)], [pad32(v_conv_w)], "adamw_conv_w")
    delta["conv_w"], new_m["conv_w"], new_v["conv_w"] = (t[:CONV_K].reshape(conv_w.shape) for t in (d, nm, nv))
    grads["conv_w"] = grads["conv_w"].reshape(conv_w.shape)

    return (loss_out, grad_x, *[grads[n] for n in order], *[delta[n] for n in order],
            *[new_m[n] for n in order], *[new_v[n] for n in order])
```

```python
import functools
import math
from typing import Callable, NamedTuple

import numpy as np
import jax
import jax.numpy as jnp
from jax import lax
from jax.experimental import pallas as pl
from jax.experimental.pallas import tpu as pltpu

f32 = jnp.float32
bf16 = jnp.bfloat16
SDS = jax.ShapeDtypeStruct
MESH = pl.DeviceIdType.MESH

D_MODEL = 1024
SEQ = 2048
HEAD_DIM = 64
HEADS = 8
GROUPS = ((128, 1), (512, 4), (2048, 16))
GW = HEADS * HEAD_DIM
ATTN_W = len(GROUPS) * GW
Q_BLOCK = 128
CONV_K = 31
D_FF = 2816
IN_W = 3 * ATTN_W + 2 * D_MODEL + 2 * D_MODEL
N_CHIPS = 4
IN_S = IN_W // N_CHIPS
FF_S = D_FF // N_CHIPS
RMS_EPS = 1e-6
LN_EPS = 1e-5
LR, B1, B2, ADAM_EPS, WD, STEP = 0.001, 0.9, 0.999, 1e-08, 0.01, 10
NEG = -1e30
LANES = 128
VMEM_LIMIT = 48 * 2 ** 20
VMEM_LIMIT_BIG = 54 * 2 ** 20
CB = 512
UA_CB, UB_CB, GA_CB, GC_CB = 9, 11, 13, 15


def _alibi_slope_list(n):
    def pow2(m):
        start = 2.0 ** (-8.0 / m)
        return [start ** (i + 1) for i in range(m)]
    if math.log2(n).is_integer():
        return pow2(n)
    c = 2 ** math.floor(math.log2(n))
    return pow2(c) + _alibi_slope_list(2 * c)[0::2][: n - c]


_SLOPES = np.asarray(sorted(_alibi_slope_list(len(GROUPS) * HEADS), reverse=True), dtype=np.float32).reshape(len(GROUPS), HEADS)


def _cp(sem=None, vmem=VMEM_LIMIT):
    return pltpu.CompilerParams(dimension_semantics=sem, vmem_limit_bytes=vmem)


def _sigmoid(x):
    return 1.0 / (1.0 + jnp.exp(-x))


HBM = pl.BlockSpec(memory_space=pl.ANY)


class _Exchange(NamedTuple):
    ins: list
    out_shape: list
    scratch: list
    aliases: dict
    start: Callable
    wait: Callable


def _launch(body, *, name, grid, in_specs, out_specs, out_shape, args, scratch_shapes=(), sem=None, comm=None, vmem=VMEM_LIMIT):
    if comm is None:
        return pl.pallas_call(body, grid=grid, in_specs=in_specs, out_specs=out_specs, out_shape=out_shape,
                              scratch_shapes=list(scratch_shapes), name=name, compiler_params=_cp(sem, vmem))(*args)
    multi = isinstance(out_shape, (tuple, list))
    m_out = list(out_shape) if multi else [out_shape]
    m_ospec = list(out_specs) if multi else [out_specs]
    n_in, n_out, n_scr = len(in_specs), len(m_out), len(scratch_shapes)
    nc_in, nc_out = len(comm.ins), len(comm.out_shape)

    def hosted(*refs):
        bounds = np.cumsum([0, n_in, nc_in, n_out, nc_out, n_scr])
        mi, ci, mo, co, ms = (refs[a:b] for a, b in zip(bounds[:-1], bounds[1:]))
        cs = refs[bounds[-1]:]
        ids = [pl.program_id(a) for a in range(len(grid))]
        first = functools.reduce(jnp.logical_and, [i == 0 for i in ids])
        last = functools.reduce(jnp.logical_and, [i == g - 1 for i, g in zip(ids, grid)])

        @pl.when(first)
        def _():
            comm.start(ci, co, cs)

        body(*mi, *mo, *ms)

        @pl.when(last)
        def _():
            comm.wait(ci, co, cs)

    res = pl.pallas_call(
        hosted, grid=grid, in_specs=list(in_specs) + [HBM] * nc_in, out_specs=m_ospec + [HBM] * nc_out,
        out_shape=tuple(m_out) + tuple(comm.out_shape), scratch_shapes=list(scratch_shapes) + list(comm.scratch),
        input_output_aliases={n_in + i: n_out + o for i, o in comm.aliases.items()}, name=name + "_comm",
        compiler_params=pltpu.CompilerParams(dimension_semantics=("arbitrary",) * len(grid), vmem_limit_bytes=vmem,
                                             has_side_effects=True))(*args, *comm.ins)
    return (tuple(res[:n_out]) if multi else res[0]), tuple(res[n_out:])


def _run_exchange(ex, name):
    n_in = len(ex.ins)

    def body(*refs):
        ins, outs, sems = refs[:n_in], refs[n_in:n_in + len(ex.out_shape)], refs[n_in + len(ex.out_shape):]
        ex.start(ins, outs, sems)
        ex.wait(ins, outs, sems)

    return pl.pallas_call(body, in_specs=[HBM] * n_in, out_specs=[HBM] * len(ex.out_shape), out_shape=tuple(ex.out_shape),
                          scratch_shapes=list(ex.scratch), input_output_aliases=dict(ex.aliases), name=name,
                          compiler_params=pltpu.CompilerParams(has_side_effects=True))(*ex.ins)


def _rmsnorm(xv, gv):
    return xv * lax.rsqrt(jnp.mean(xv * xv, axis=-1, keepdims=True) + RMS_EPS) * gv


def _rms_fwd(x, g, name):
    T = x.shape[0]
    tm = 512

    def body(x_ref, g_ref, o_ref, ot_ref):
        hv = _rmsnorm(x_ref[...], g_ref[...])
        o_ref[...] = hv.astype(bf16)
        ot_ref[...] = hv.T.astype(bf16)

    row = pl.BlockSpec((tm, D_MODEL), lambda i: (i, 0))
    vec = pl.BlockSpec((1, D_MODEL), lambda i: (0, 0))
    return pl.pallas_call(body, grid=(T // tm,), in_specs=[row, vec], out_specs=[row, pl.BlockSpec((D_MODEL, tm), lambda i: (0, i))],
                          out_shape=(SDS((T, D_MODEL), bf16), SDS((D_MODEL, T), bf16)), name=name,
                          compiler_params=_cp(("parallel",)))(x, g)


def _rms_bwd_tile(dyv, xv, gv, dres):
    r = lax.rsqrt(jnp.mean(xv * xv, axis=-1, keepdims=True) + RMS_EPS)
    xh = xv * r
    dxh = dyv * gv
    dx = dres + r * (dxh - xh * jnp.mean(dxh * xh, axis=-1, keepdims=True))
    return dx, jnp.sum(dyv * xh, axis=0, keepdims=True)


def _accumulate(first, refs_parts):
    @pl.when(first)
    def _():
        for ref, part in refs_parts:
            ref[...] = part

    @pl.when(jnp.logical_not(first))
    def _():
        for ref, part in refs_parts:
            ref[...] += part


def _mm_rms_bwd(name, ops, x, g, dres, *, tm, a_spec, b_spec, dims, comm=None, vmem=VMEM_LIMIT):
    T = x.shape[0]
    n = len(ops)

    def body(*refs):
        ab_refs = refs[:2 * n]
        x_ref, g_ref, r_ref, dx_ref, dg_ref, acc = refs[2 * n:]
        m, s = pl.program_id(0), pl.program_id(1)
        p = sum(lax.dot_general(ab_refs[2 * i][...], ab_refs[2 * i + 1][...], dims, preferred_element_type=f32) for i in range(n))

        @pl.when(s == 0)
        def _():
            acc[...] = p

        @pl.when(s > 0)
        def _():
            acc[...] += p

        @pl.when(s == N_CHIPS - 1)
        def _():
            dx, part = _rms_bwd_tile(acc[...], x_ref[...], g_ref[...], r_ref[...])
            dx_ref[...] = dx
            _accumulate(m == 0, [(dg_ref, part)])

    row = pl.BlockSpec((tm, D_MODEL), lambda m, s: (m, 0))
    vec = pl.BlockSpec((1, D_MODEL), lambda m, s: (0, 0))
    return _launch(body, name=name, grid=(T // tm, N_CHIPS), in_specs=[a_spec, b_spec] * n + [row, vec, row], out_specs=[row, vec],
                   out_shape=(SDS((T, D_MODEL), f32), SDS((1, D_MODEL), f32)), args=tuple(t for ab in ops for t in ab) + (x, g, dres),
                   scratch_shapes=[pltpu.VMEM((tm, D_MODEL), f32)], sem=("arbitrary", "arbitrary"), comm=comm, vmem=vmem)


def _ffn_down_loss(ff, wd, x1, gf, target):
    T = x1.shape[0]
    tm = 1024
    nm = T // tm

    def body(f_ref, w_ref, x_ref, g_ref, t_ref, loss_ref, dx_ref, dg_ref, acc):
        m, s = pl.program_id(0), pl.program_id(1)
        p = jnp.dot(f_ref[...], w_ref[...], preferred_element_type=f32)

        @pl.when(s == 0)
        def _():
            acc[...] = p

        @pl.when(s > 0)
        def _():
            acc[...] += p

        @pl.when(s == N_CHIPS - 1)
        def _():
            xv = acc[...] + x_ref[...]
            gv = g_ref[...]
            r = lax.rsqrt(jnp.mean(xv * xv, axis=-1, keepdims=True) + RMS_EPS)
            xh = xv * r
            e = xh * gv - t_ref[...]
            part_l = jnp.broadcast_to(0.5 * jnp.sum(jnp.mean(e * e, axis=-1, keepdims=True), axis=0, keepdims=True), (1, LANES))
            dy = e * (1.0 / D_MODEL)
            dxh = dy * gv
            dx_ref[...] = r * (dxh - xh * jnp.mean(dxh * xh, axis=-1, keepdims=True))
            part_g = jnp.sum(dy * xh, axis=0, keepdims=True)

            @pl.when(m == 0)
            def _():
                loss_ref[...] = part_l
                dg_ref[...] = part_g

            @pl.when(m > 0)
            def _():
                loss_ref[...] += part_l
                dg_ref[...] += part_g

    row = pl.BlockSpec((tm, D_MODEL), lambda m, s: (m, 0))
    vec = pl.BlockSpec((1, D_MODEL), lambda m, s: (0, 0))
    return pl.pallas_call(
        body, grid=(nm, N_CHIPS),
        in_specs=[pl.BlockSpec((None, tm, FF_S), lambda m, s: (s, m, 0)), pl.BlockSpec((None, FF_S, D_MODEL), lambda m, s: (s, 0, 0)),
                  row, vec, row],
        out_specs=[pl.BlockSpec((1, LANES), lambda m, s: (0, 0)), row, vec],
        out_shape=(SDS((1, LANES), f32), SDS((T, D_MODEL), f32), SDS((1, D_MODEL), f32)),
        scratch_shapes=[pltpu.VMEM((tm, D_MODEL), f32)], name="ffn_down_loss", compiler_params=_cp(("arbitrary", "arbitrary")))(
            ff, wd, x1, gf, target)


def _merge_o_proj_rms(proj, gate_b, ya, yc, w_o, x, g2):
    T = x.shape[0]
    tm = 512

    def body(ga0, ga1, gc0, gc1, b_ref, ya_ref, yc_ref, w_ref, x_ref, g_ref, m_ref, x1_ref, h2_ref):
        bv = b_ref[...]
        ga = jnp.concatenate([ga0[...], ga1[...]], axis=-1).astype(f32)
        gc = jnp.concatenate([gc0[...], gc1[...]], axis=-1).astype(f32)
        merged = (_sigmoid(ga + bv[:, :D_MODEL]) * ya_ref[...].astype(f32)
                  + _sigmoid(gc + bv[:, D_MODEL:]) * yc_ref[...].astype(f32)).astype(bf16)
        m_ref[...] = merged
        x1 = jnp.dot(merged, w_ref[...], preferred_element_type=f32) + x_ref[...]
        x1_ref[...] = x1
        h2_ref[...] = _rmsnorm(x1, g_ref[...]).astype(bf16)

    row = pl.BlockSpec((tm, D_MODEL), lambda i: (i, 0))
    return pl.pallas_call(
        body, grid=(T // tm,),
        in_specs=[pl.BlockSpec((tm, CB), lambda i, cb=cb: (i, cb)) for cb in (GA_CB, GA_CB + 1, GC_CB, GC_CB + 1)]
        + [pl.BlockSpec((1, 2 * D_MODEL), lambda i: (0, 0)), row, row, pl.BlockSpec((D_MODEL, D_MODEL), lambda i: (0, 0)), row,
           pl.BlockSpec((1, D_MODEL), lambda i: (0, 0))],
        out_specs=[row, row, row], out_shape=(SDS((T, D_MODEL), bf16), SDS((T, D_MODEL), f32), SDS((T, D_MODEL), bf16)),
        name="merge_o_proj_rms", compiler_params=_cp(("parallel",)))(proj, proj, proj, proj, gate_b, ya, yc, w_o, x, g2)


def _conv_out_fwd(c1, g, b, w, comm=None):
    T = c1.shape[0]
    tm = 1024

    def body(c_ref, g_ref, b_ref, w_ref, c3_ref, y_ref):
        cv = c_ref[...]
        mu = jnp.mean(cv, axis=-1, keepdims=True)
        cc = cv - mu
        var = jnp.mean(cc * cc, axis=-1, keepdims=True)
        c2 = cc * lax.rsqrt(var + LN_EPS) * g_ref[...] + b_ref[...]
        c3 = (c2 * _sigmoid(c2)).astype(bf16)
        c3_ref[...] = c3
        y_ref[...] = jnp.dot(c3, w_ref[...], preferred_element_type=f32).astype(bf16)

    row = pl.BlockSpec((tm, D_MODEL), lambda i: (i, 0))
    vec = pl.BlockSpec((1, D_MODEL), lambda i: (0, 0))
    sd = SDS((T, D_MODEL), bf16)
    return _launch(body, name="conv_out_fwd", grid=(T // tm,), in_specs=[row, vec, vec, pl.BlockSpec((D_MODEL, D_MODEL), lambda i: (0, 0))],
                   out_specs=[row, row], out_shape=(sd, sd), args=(c1, g, b, w), sem=("parallel",), comm=comm)


def _conv_out_bwd(dyc, w, c1, g, b):
    T = c1.shape[0]
    tm = 1024

    def body(d_ref, w_ref, c_ref, g_ref, b_ref, dc_ref, dg_ref, db_ref):
        dc3 = lax.dot_general(d_ref[...], w_ref[...], NT, preferred_element_type=f32)
        cv = c_ref[...]
        gv = g_ref[...]
        mu = jnp.mean(cv, axis=-1, keepdims=True)
        cc = cv - mu
        var = jnp.mean(cc * cc, axis=-1, keepdims=True)
        rs = lax.rsqrt(var + LN_EPS)
        xh = cc * rs
        c2 = xh * gv + b_ref[...]
        sg = _sigmoid(c2)
        dc2 = dc3 * (sg * (1.0 + c2 * (1.0 - sg)))
        dxh = dc2 * gv
        dc_ref[...] = rs * (dxh - jnp.mean(dxh, axis=-1, keepdims=True) - xh * jnp.mean(dxh * xh, axis=-1, keepdims=True))
        _accumulate(pl.program_id(0) == 0, [(dg_ref, jnp.sum(dc2 * xh, axis=0, keepdims=True)), (db_ref, jnp.sum(dc2, axis=0, keepdims=True))])

    row = pl.BlockSpec((tm, D_MODEL), lambda i: (i, 0))
    vec = pl.BlockSpec((1, D_MODEL), lambda i: (0, 0))
    return pl.pallas_call(body, grid=(T // tm,), in_specs=[row, pl.BlockSpec((D_MODEL, D_MODEL), lambda i: (0, 0)), row, vec, vec],
                          out_specs=[row, vec, vec],
                          out_shape=(SDS((T, D_MODEL), f32), SDS((1, D_MODEL), f32), SDS((1, D_MODEL), f32)),
                          name="conv_out_bwd", compiler_params=_cp(("arbitrary",)))(dyc, w, c1, g, b)


NN = (((1,), (0,)), ((), ()))
NT = (((1,), (1,)), ((), ()))
TN = (((0,), (0,)), ((), ()))


def _mm(name, a, b, *, grid, a_spec, b_spec, o_spec, o_shape, o_dtype, dims, acc_shape=None, k_axis=None,
        res=None, res_spec=None, sem=None, comm=None, vmem=VMEM_LIMIT):
    nk = 1 if k_axis is None else grid[k_axis]

    def body(*refs):
        if res is None:
            a_ref, b_ref, o_ref = refs[:3]
            r_ref, scr = None, refs[3:]
        else:
            a_ref, b_ref, r_ref, o_ref = refs[:4]
            scr = refs[4:]
        p = lax.dot_general(a_ref[...].astype(bf16), b_ref[...].astype(bf16), dims, preferred_element_type=f32)
        if nk == 1:
            if r_ref is not None:
                p = p + r_ref[...]
            o_ref[...] = p.astype(o_dtype)
            return
        acc = scr[0]
        k = pl.program_id(k_axis)

        @pl.when(k == 0)
        def _():
            acc[...] = p

        @pl.when(k > 0)
        def _():
            acc[...] += p

        @pl.when(k == nk - 1)
        def _():
            t = acc[...]
            if r_ref is not None:
                t = t + r_ref[...]
            o_ref[...] = t.astype(o_dtype)

    ins = [a, b] + ([] if res is None else [res])
    specs = [a_spec, b_spec] + ([] if res is None else [res_spec])
    scratch = [] if nk == 1 else [pltpu.VMEM(acc_shape, f32)]
    return _launch(body, name=name, grid=grid, in_specs=specs, out_specs=o_spec, out_shape=SDS(o_shape, o_dtype),
                   args=ins, scratch_shapes=scratch, sem=sem, comm=comm, vmem=vmem)


def _mm_nn_full(name, a, b, o_dtype, res=None, tm=1024, comm=None):
    T, K = a.shape
    N = b.shape[1]
    return _mm(name, a, b, grid=(T // tm,), a_spec=pl.BlockSpec((tm, K), lambda i: (i, 0)),
               b_spec=pl.BlockSpec((K, N), lambda i: (0, 0)), o_spec=pl.BlockSpec((tm, N), lambda i: (i, 0)),
               o_shape=(T, N), o_dtype=o_dtype, dims=NN, res=res,
               res_spec=None if res is None else pl.BlockSpec((tm, N), lambda i: (i, 0)), sem=("parallel",), comm=comm)


def _mm_nt_full(name, a, b, o_dtype, tm=1024):
    T, N = a.shape
    K = b.shape[0]
    return _mm(name, a, b, grid=(T // tm,), a_spec=pl.BlockSpec((tm, N), lambda i: (i, 0)),
               b_spec=pl.BlockSpec((K, N), lambda i: (0, 0)), o_spec=pl.BlockSpec((tm, K), lambda i: (i, 0)),
               o_shape=(T, K), o_dtype=o_dtype, dims=NT, sem=("parallel",))


def _mm_tn_tokens(name, a, b, o_dtype, tk=2048):
    T, K = a.shape
    N = b.shape[1]
    return _mm(name, a, b, grid=(T // tk,), a_spec=pl.BlockSpec((tk, K), lambda k: (k, 0)),
               b_spec=pl.BlockSpec((tk, N), lambda k: (k, 0)), o_spec=pl.BlockSpec((K, N), lambda k: (0, 0)),
               o_shape=(K, N), o_dtype=o_dtype, dims=TN, acc_shape=(K, N), k_axis=0, sem=("arbitrary",))


AO_S = D_MODEL // N_CHIPS


def _attn_out_cat(w_ref):
    return jnp.concatenate([w_ref[s] for s in range(N_CHIPS)], axis=-1)


def _attn_out_fwd(mix, w4):
    T = mix.shape[0]
    tm = 1024

    def body(a_ref, w_ref, o_ref):
        o_ref[...] = jnp.dot(a_ref[...], _attn_out_cat(w_ref), preferred_element_type=f32).astype(bf16)

    return pl.pallas_call(body, grid=(T // tm,),
                          in_specs=[pl.BlockSpec((tm, GW), lambda i: (i, 0)), pl.BlockSpec((N_CHIPS, GW, AO_S), lambda i: (0, 0, 0))],
                          out_specs=pl.BlockSpec((tm, D_MODEL), lambda i: (i, 0)), out_shape=SDS((T, D_MODEL), bf16), name="attn_out",
                          compiler_params=_cp(("parallel",)))(mix, w4)


def _attn_out_bwd_x(dya, w4, comm=None):
    T = dya.shape[0]
    tm = 1024

    def body(d_ref, w_ref, o_ref):
        o_ref[...] = lax.dot_general(d_ref[...], _attn_out_cat(w_ref), NT, preferred_element_type=f32).astype(bf16)

    return _launch(body, name="d_mix", grid=(T // tm,),
                   in_specs=[pl.BlockSpec((tm, D_MODEL), lambda i: (i, 0)), pl.BlockSpec((N_CHIPS, GW, AO_S), lambda i: (0, 0, 0))],
                   out_specs=pl.BlockSpec((tm, GW), lambda i: (i, 0)), out_shape=SDS((T, GW), bf16), args=(dya, w4),
                   sem=("parallel",), comm=comm)


def _attn_out_bwd_w(mix, dya):
    T = mix.shape[0]
    tk = 1024
    nk = T // tk

    def body(a_ref, d_ref, o_ref, acc):
        k = pl.program_id(0)
        p = lax.dot_general(a_ref[...], d_ref[...], TN, preferred_element_type=f32)

        @pl.when(k == 0)
        def _():
            acc[...] = p

        @pl.when(k > 0)
        def _():
            acc[...] += p

        @pl.when(k == nk - 1)
        def _():
            for s in range(N_CHIPS):
                o_ref[s] = acc[:, s * AO_S:(s + 1) * AO_S].astype(bf16)

    return pl.pallas_call(body, grid=(nk,),
                          in_specs=[pl.BlockSpec((tk, GW), lambda k: (k, 0)), pl.BlockSpec((tk, D_MODEL), lambda k: (k, 0))],
                          out_specs=pl.BlockSpec((N_CHIPS, GW, AO_S), lambda k: (0, 0, 0)), out_shape=SDS((N_CHIPS, GW, AO_S), bf16),
                          scratch_shapes=[pltpu.VMEM((GW, D_MODEL), f32)], name="d_w_attn_out",
                          compiler_params=_cp(("arbitrary",)))(mix, dya)


def _ffn_up(h2, wg, wu):
    T = h2.shape[0]
    tm = 1024

    def body(h_ref, wg_ref, wu_ref, a_ref, b_ref, f_ref):
        hv = h_ref[...]
        av = lax.dot_general(hv, wg_ref[...], NT, preferred_element_type=f32)
        bv = lax.dot_general(hv, wu_ref[...], NT, preferred_element_type=f32)
        a_ref[...] = av.astype(bf16)
        b_ref[...] = bv.astype(bf16)
        f_ref[...] = (av * _sigmoid(av) * bv).astype(bf16)

    wspec = pl.BlockSpec((None, FF_S, D_MODEL), lambda s, m: (s, 0, 0))
    ospec = pl.BlockSpec((None, tm, FF_S), lambda s, m: (s, m, 0))
    osd = SDS((N_CHIPS, T, FF_S), bf16)
    return pl.pallas_call(body, grid=(N_CHIPS, T // tm),
                          in_specs=[pl.BlockSpec((tm, D_MODEL), lambda s, m: (m, 0)), wspec, wspec],
                          out_specs=[ospec, ospec, ospec], out_shape=(osd, osd, osd), name="ffn_up",
                          compiler_params=_cp(("parallel", "parallel")))(h2, wg, wu)


def _ffn_up_bwd_w(da, db, h2):
    T = h2.shape[0]
    tk = 2048
    nk = T // tk

    def body(a_ref, b_ref, h_ref, og_ref, ou_ref, accg, accu):
        k = pl.program_id(1)
        hv = h_ref[...]
        pg = lax.dot_general(a_ref[...], hv, TN, preferred_element_type=f32)
        pu = lax.dot_general(b_ref[...], hv, TN, preferred_element_type=f32)
        _accumulate(k == 0, [(accg, pg), (accu, pu)])

        @pl.when(k == nk - 1)
        def _():
            og_ref[...] = accg[...].astype(bf16)
            ou_ref[...] = accu[...].astype(bf16)

    aspec = pl.BlockSpec((None, tk, FF_S), lambda s, k: (s, k, 0))
    ospec = pl.BlockSpec((None, FF_S, D_MODEL), lambda s, k: (s, 0, 0))
    osd = SDS((N_CHIPS, FF_S, D_MODEL), bf16)
    return pl.pallas_call(body, grid=(N_CHIPS, nk), in_specs=[aspec, aspec, pl.BlockSpec((tk, D_MODEL), lambda s, k: (k, 0))],
                          out_specs=[ospec, ospec], out_shape=(osd, osd),
                          scratch_shapes=[pltpu.VMEM((FF_S, D_MODEL), f32), pltpu.VMEM((FF_S, D_MODEL), f32)], name="d_w_gate_up",
                          compiler_params=_cp(("parallel", "arbitrary")))(da, db, h2)


def _ffn_down_bwd(dx2, wd, a, b):
    T = dx2.shape[0]
    tm = 1024

    def body(d_ref, w_ref, a_ref, b_ref, da_ref, db_ref):
        df = lax.dot_general(d_ref[...].astype(bf16), w_ref[...], NT, preferred_element_type=f32)
        av = a_ref[...].astype(f32)
        sg = _sigmoid(av)
        da_ref[...] = (df * b_ref[...].astype(f32) * (sg * (1.0 + av * (1.0 - sg)))).astype(bf16)
        db_ref[...] = (df * av * sg).astype(bf16)

    aspec = pl.BlockSpec((None, tm, FF_S), lambda m, s: (s, m, 0))
    osd = SDS((N_CHIPS, T, FF_S), bf16)
    return pl.pallas_call(body, grid=(T // tm, N_CHIPS),
                          in_specs=[pl.BlockSpec((tm, D_MODEL), lambda m, s: (m, 0)),
                                    pl.BlockSpec((None, FF_S, D_MODEL), lambda m, s: (s, 0, 0)), aspec, aspec],
                          out_specs=[aspec, aspec], out_shape=(osd, osd), name="ffn_down_bwd",
                          compiler_params=_cp(("parallel", "parallel")))(dx2, wd, a, b)


def _merge_bwd(dm, proj, gate_b, ya, yc):
    T = proj.shape[0]
    tm = 1024

    def body(dm_ref, ga_ref, gc_ref, ba_ref, bc_ref, ya_ref, yc_ref, dya_ref, dyc_ref, dga_ref, dgc_ref, dba_ref, dbc_ref):
        dmv = dm_ref[...].astype(f32)
        sa = _sigmoid(ga_ref[...].astype(f32) + ba_ref[...])
        sc = _sigmoid(gc_ref[...].astype(f32) + bc_ref[...])
        dya_ref[...] = (dmv * sa).astype(bf16)
        dyc_ref[...] = (dmv * sc).astype(bf16)
        dga = dmv * ya_ref[...].astype(f32) * sa * (1.0 - sa)
        dgc = dmv * yc_ref[...].astype(f32) * sc * (1.0 - sc)
        dga_ref[...] = dga.astype(bf16)
        dgc_ref[...] = dgc.astype(bf16)
        pa = jnp.sum(dga, axis=0, keepdims=True)
        pc = jnp.sum(dgc, axis=0, keepdims=True)

        @pl.when(pl.program_id(1) == 0)
        def _():
            dba_ref[...] = pa
            dbc_ref[...] = pc

        @pl.when(pl.program_id(1) > 0)
        def _():
            dba_ref[...] += pa
            dbc_ref[...] += pc

    blk = pl.BlockSpec((tm, CB), lambda j, i: (i, j))
    vec = pl.BlockSpec((1, CB), lambda j, i: (0, j))
    big = SDS((T, D_MODEL), bf16)
    small = SDS((1, D_MODEL), f32)
    return pl.pallas_call(
        body, grid=(2, T // tm),
        in_specs=[blk, pl.BlockSpec((tm, CB), lambda j, i: (i, GA_CB + j)), pl.BlockSpec((tm, CB), lambda j, i: (i, GC_CB + j)),
                  vec, pl.BlockSpec((1, CB), lambda j, i: (0, 2 + j)), blk, blk],
        out_specs=[blk, blk, blk, blk, vec, vec], out_shape=(big, big, big, big, small, small), name="merge_bwd",
        compiler_params=_cp(("parallel", "arbitrary")))(dm, proj, proj, gate_b, gate_b, ya, yc)


CONV_TS = 512
CONV_HALO = 32
CONV_RC = 64
CONV_WIN = CONV_TS + CONV_HALO
SUBLANES = 8


def _fill_shifted(win, sh):
    for b in range(1, SUBLANES):
        sh[b - 1] = win[pl.ds(b, CONV_WIN - SUBLANES), :]


def _rows_at(win, sh, row):
    a, b = divmod(row, SUBLANES)
    if b == 0:
        return win[pl.ds(row, CONV_RC), :]
    return sh[b - 1, pl.ds(a * SUBLANES, CONV_RC), :]


def _glu_conv_fwd(proj3, w, bias, comm=None):
    B = proj3.shape[0]
    nt = SEQ // CONV_TS
    hb = CONV_TS // CONV_HALO

    def body(ua_ref, ub_ref, ha_ref, hb_ref, w_ref, b_ref, o_ref, win, sh):
        i = pl.program_id(2)
        c0 = ua_ref[...].astype(f32) * _sigmoid(ub_ref[...].astype(f32))
        halo = ha_ref[...].astype(f32) * _sigmoid(hb_ref[...].astype(f32))
        win[0:CONV_HALO, :] = jnp.where(i > 0, halo, 0.0)
        win[CONV_HALO:, :] = c0
        _fill_shifted(win, sh)
        for r0 in range(0, CONV_TS, CONV_RC):
            acc = jnp.zeros((CONV_RC, CB), f32) + b_ref[...]
            for k in range(CONV_K):
                acc = acc + _rows_at(win, sh, r0 + CONV_HALO - (CONV_K - 1) + k) * w_ref[k:k + 1, :]
            o_ref[r0:r0 + CONV_RC, :] = acc

    def cur(cb):
        return pl.BlockSpec((None, CONV_TS, CB), lambda b, j, i: (b, i, cb + j))

    def prev(cb):
        return pl.BlockSpec((None, CONV_HALO, CB), lambda b, j, i: (b, jnp.maximum(i * hb - 1, 0), cb + j))

    return _launch(
        body, name="glu_conv_fwd", grid=(B, 2, nt),
        in_specs=[cur(UA_CB), cur(UB_CB), prev(UA_CB), prev(UB_CB),
                  pl.BlockSpec((CONV_K, CB), lambda b, j, i: (0, j)), pl.BlockSpec((1, CB), lambda b, j, i: (0, j))],
        out_specs=pl.BlockSpec((None, CONV_TS, CB), lambda b, j, i: (b, i, j)),
        out_shape=SDS((B, SEQ, D_MODEL), f32), args=(proj3, proj3, proj3, proj3, w, bias),
        scratch_shapes=[pltpu.VMEM((CONV_WIN, CB), f32), pltpu.VMEM((SUBLANES - 1, CONV_WIN - SUBLANES, CB), f32)],
        sem=("parallel", "parallel", "parallel"), comm=comm)


def _glu_conv_bwd(dc1, proj3, w, comm=None):
    B = proj3.shape[0]
    nt = SEQ // CONV_TS
    hb = CONV_TS // CONV_HALO

    def body(d_ref, dn_ref, ua_ref, ub_ref, ha_ref, hb_ref, w_ref, dua_ref, dub_ref, dw_ref, db_ref, winc, wind, accw, shc, shd):
        b = pl.program_id(1)
        i = pl.program_id(2)
        first = jnp.logical_and(b == 0, i == 0)
        last = jnp.logical_and(b == B - 1, i == nt - 1)

        @pl.when(first)
        def _():
            accw[...] = jnp.zeros_like(accw)
            db_ref[...] = jnp.zeros_like(db_ref)

        halo = ha_ref[...].astype(f32) * _sigmoid(hb_ref[...].astype(f32))
        winc[0:CONV_HALO, :] = jnp.where(i > 0, halo, 0.0)
        winc[CONV_HALO:, :] = ua_ref[...].astype(f32) * _sigmoid(ub_ref[...].astype(f32))
        wind[0:CONV_TS, :] = d_ref[...]
        wind[CONV_TS:, :] = jnp.where(i < nt - 1, dn_ref[...], 0.0)
        db_ref[...] += jnp.sum(d_ref[...], axis=0, keepdims=True)
        _fill_shifted(winc, shc)
        _fill_shifted(wind, shd)
        for r0 in range(0, CONV_TS, CONV_RC):
            dc0 = jnp.zeros((CONV_RC, CB), f32)
            for k in range(CONV_K):
                dc0 = dc0 + _rows_at(wind, shd, r0 + (CONV_K - 1) - k) * w_ref[k:k + 1, :]
            uav = ua_ref[r0:r0 + CONV_RC, :].astype(f32)
            sg = _sigmoid(ub_ref[r0:r0 + CONV_RC, :].astype(f32))
            dua_ref[r0:r0 + CONV_RC, :] = (dc0 * sg).astype(bf16)
            dub_ref[r0:r0 + CONV_RC, :] = (dc0 * uav * sg * (1.0 - sg)).astype(bf16)
            dv = wind[r0:r0 + CONV_RC, :]
            for k in range(CONV_K):
                prod = dv * _rows_at(winc, shc, r0 + CONV_HALO - (CONV_K - 1) + k)
                accw[k] += jnp.sum(prod.reshape(CONV_RC // 8, 8, CB), axis=0)

        @pl.when(last)
        def _():
            for k in range(CONV_K):
                dw_ref[k:k + 1, :] = jnp.sum(accw[k], axis=0, keepdims=True)
            dw_ref[CONV_K:, :] = jnp.zeros((CONV_HALO - CONV_K, CB), f32)

    def cur(cb):
        return pl.BlockSpec((None, CONV_TS, CB), lambda j, b, i: (b, i, cb + j))

    def prev(cb):
        return pl.BlockSpec((None, CONV_HALO, CB), lambda j, b, i: (b, jnp.maximum(i * hb - 1, 0), cb + j))

    nxt = pl.BlockSpec((None, CONV_HALO, CB), lambda j, b, i: (b, jnp.minimum((i + 1) * hb, SEQ // CONV_HALO - 1), j))
    big = SDS((B, SEQ, D_MODEL), bf16)
    return _launch(
        body, name="glu_conv_bwd", grid=(2, B, nt),
        in_specs=[cur(0), nxt, cur(UA_CB), cur(UB_CB), prev(UA_CB), prev(UB_CB), pl.BlockSpec((CONV_K, CB), lambda j, b, i: (0, j))],
        out_specs=[cur(0), cur(0), pl.BlockSpec((CONV_HALO, CB), lambda j, b, i: (0, j)), pl.BlockSpec((1, CB), lambda j, b, i: (0, j))],
        out_shape=(big, big, SDS((CONV_HALO, D_MODEL), f32), SDS((1, D_MODEL), f32)),
        args=(dc1, dc1, proj3, proj3, proj3, proj3, w),
        scratch_shapes=[pltpu.VMEM((CONV_WIN, CB), f32), pltpu.VMEM((CONV_WIN, CB), f32), pltpu.VMEM((CONV_K, SUBLANES, CB), f32),
                        pltpu.VMEM((SUBLANES - 1, CONV_WIN - SUBLANES, CB), f32),
                        pltpu.VMEM((SUBLANES - 1, CONV_WIN - SUBLANES, CB), f32)],
        sem=("parallel", "arbitrary", "arbitrary"), comm=comm)


def _band(first, dil):
    kw = Q_BLOCK if first else 2 * Q_BLOCK
    qi = lax.broadcasted_iota(jnp.int32, (Q_BLOCK, kw), 0)
    kj = lax.broadcasted_iota(jnp.int32, (Q_BLOCK, kw), 1)
    rel = qi - kj + (0 if first else Q_BLOCK)
    valid = jnp.logical_and(rel >= 0, rel <= Q_BLOCK)
    return valid, rel.astype(f32) * float(dil)


def _bias(first, dil, slope):
    valid, dist = _band(first, dil)
    return jnp.where(valid, -slope * dist, NEG)


def _scores(q, k, bias):
    return lax.dot_general(q, k, NT, preferred_element_type=f32) * (HEAD_DIM ** -0.5) + bias


def _pair_cols(hp):
    return slice(hp * LANES, (hp + 1) * LANES)


def _half(x2, e):
    lane = lax.broadcasted_iota(jnp.int32, (1, LANES), 1)
    keep = (lane < HEAD_DIM) if e == 0 else (lane >= HEAD_DIM)
    return jnp.where(keep, x2, jnp.zeros_like(x2))


def _attn_fwd(g, q4, k4, v4, qcb, kcb, vcb, comm=None):
    _, dil = GROUPS[g]
    B, r, L, _ = q4.shape
    nb = L // Q_BLOCK
    slopes = [float(s) for s in _SLOPES[g]]

    def body(q_ref, k_ref, v_ref, o_ref, lse_ref, bias, s_scr, p_scr):
        lane = lax.broadcasted_iota(jnp.int32, (Q_BLOCK, LANES), 1)
        if nb > 1:
            for h in range(HEADS):
                bias[h] = _bias(False, dil, slopes[h])

        def block(n, first):
            q0 = 0 if first else pl.multiple_of(n * Q_BLOCK, Q_BLOCK)
            k0 = 0 if first else pl.multiple_of((n - 1) * Q_BLOCK, Q_BLOCK)
            kw = Q_BLOCK if first else 2 * Q_BLOCK
            for hp in range(HEADS // 2):
                q2 = q_ref[pl.ds(q0, Q_BLOCK), _pair_cols(hp)]
                k2 = k_ref[pl.ds(k0, kw), _pair_cols(hp)]
                for e in range(2):
                    h = 2 * hp + e
                    b_h = _bias(True, dil, slopes[h]) if first else bias[h]
                    s_scr[h, :, :kw] = _scores(_half(q2, e), k2, b_h)
            st = jnp.zeros((Q_BLOCK, LANES), f32)
            dens = jnp.ones((Q_BLOCK, LANES), f32)
            for h in range(HEADS):
                s = s_scr[h, :, :kw]
                m = jnp.max(s, axis=-1, keepdims=True)
                p = jnp.exp(s - m)
                den = jnp.sum(p, axis=-1, keepdims=True)
                p_scr[h, :, :kw] = p.astype(bf16)
                st = jnp.where(lane == h, m + jnp.log(den), st)
                dens = jnp.where(lane == h, den, dens)
            lse_ref[pl.ds(q0, Q_BLOCK), :] = st
            inv = 1.0 / dens
            for hp in range(HEADS // 2):
                v2 = v_ref[pl.ds(k0, kw), _pair_cols(hp)]
                o2 = sum(jnp.dot(p_scr[2 * hp + e, :, :kw], _half(v2, e), preferred_element_type=f32) * inv[:, 2 * hp + e:2 * hp + e + 1]
                         for e in range(2))
                o_ref[pl.ds(q0, Q_BLOCK), _pair_cols(hp)] = o2.astype(bf16)

        block(0, True)
        if nb > 1:
            def step(n, carry):
                block(n, False)
                return carry
            lax.fori_loop(1, nb, step, 0)

    def spec(cb):
        return pl.BlockSpec((None, None, L, GW), lambda b, c: (b, c, 0, cb))

    return _launch(
        body, name=f"attn_fwd_g{g}", grid=(B, r), in_specs=[spec(qcb), spec(kcb), spec(vcb)],
        out_specs=[spec(0), pl.BlockSpec((None, None, L, LANES), lambda b, c: (b, c, 0, 0))],
        out_shape=(SDS((B, r, L, GW), bf16), SDS((B, r, L, LANES), f32)), args=(q4, k4, v4),
        scratch_shapes=[pltpu.VMEM((HEADS, Q_BLOCK, 2 * Q_BLOCK), f32), pltpu.VMEM((HEADS, Q_BLOCK, 2 * Q_BLOCK), f32),
                        pltpu.VMEM((HEADS, Q_BLOCK, 2 * Q_BLOCK), bf16)],
        sem=("parallel", "parallel"), comm=comm)


def _attn_bwd(g, q4, k4, v4, qcb, kcb, vcb, do4, lse4, dl4, comm=None):
    _, dil = GROUPS[g]
    B, r, L, _ = q4.shape
    nb = L // Q_BLOCK
    slopes = [float(s) for s in _SLOPES[g]]
    scale = HEAD_DIM ** -0.5

    def body(q_ref, k_ref, v_ref, do_ref, lse_ref, dl_ref, dq_ref, dk_ref, dv_ref, dk_acc, dv_acc, bias, s_scr, dp_scr, p_scr, ds_scr):
        dk_acc[...] = jnp.zeros_like(dk_acc)
        dv_acc[...] = jnp.zeros_like(dv_acc)
        if nb > 1:
            for h in range(HEADS):
                bias[h] = _bias(False, dil, slopes[h])

        def block(n, first):
            q0 = 0 if first else pl.multiple_of(n * Q_BLOCK, Q_BLOCK)
            k0 = 0 if first else pl.multiple_of((n - 1) * Q_BLOCK, Q_BLOCK)
            kw = Q_BLOCK if first else 2 * Q_BLOCK
            for hp in range(HEADS // 2):
                q2 = q_ref[pl.ds(q0, Q_BLOCK), _pair_cols(hp)]
                k2 = k_ref[pl.ds(k0, kw), _pair_cols(hp)]
                v2 = v_ref[pl.ds(k0, kw), _pair_cols(hp)]
                do2 = do_ref[pl.ds(q0, Q_BLOCK), _pair_cols(hp)]
                for e in range(2):
                    h = 2 * hp + e
                    b_h = _bias(True, dil, slopes[h]) if first else bias[h]
                    s_scr[h, :, :kw] = _scores(_half(q2, e), k2, b_h)
                    dp_scr[h, :, :kw] = lax.dot_general(_half(do2, e), v2, NT, preferred_element_type=f32)
            for h in range(HEADS):
                p = jnp.exp(s_scr[h, :, :kw] - lse_ref[pl.ds(q0, Q_BLOCK), h:h + 1])
                p_scr[h, :, :kw] = p.astype(bf16)
                ds_scr[h, :, :kw] = (p * (dp_scr[h, :, :kw] - dl_ref[pl.ds(q0, Q_BLOCK), h:h + 1])).astype(bf16)
            for hp in range(HEADS // 2):
                cols = _pair_cols(hp)
                q2 = q_ref[pl.ds(q0, Q_BLOCK), cols]
                k2 = k_ref[pl.ds(k0, kw), cols]
                do2 = do_ref[pl.ds(q0, Q_BLOCK), cols]
                ds = [ds_scr[2 * hp + e, :, :kw] for e in range(2)]
                dq2 = sum(jnp.dot(ds[e], _half(k2, e), preferred_element_type=f32) for e in range(2))
                dq_ref[pl.ds(q0, Q_BLOCK), cols] = (dq2 * scale).astype(bf16)
                dk2 = sum(lax.dot_general(ds[e], _half(q2, e), TN, preferred_element_type=f32) for e in range(2))
                dk_acc[pl.ds(k0, kw), cols] += dk2 * scale
                dv2 = sum(lax.dot_general(p_scr[2 * hp + e, :, :kw], _half(do2, e), TN, preferred_element_type=f32) for e in range(2))
                dv_acc[pl.ds(k0, kw), cols] += dv2

        block(0, True)
        if nb > 1:
            def step(n, carry):
                block(n, False)
                return carry
            lax.fori_loop(1, nb, step, 0)
        dk_ref[...] = dk_acc[...].astype(bf16)
        dv_ref[...] = dv_acc[...].astype(bf16)

    def spec(cb):
        return pl.BlockSpec((None, None, L, GW), lambda b, c: (b, c, 0, cb))

    st = pl.BlockSpec((None, None, L, LANES), lambda b, c: (b, c, 0, 0))
    osd = SDS((B, r, L, GW), bf16)
    return _launch(
        body, name=f"attn_bwd_g{g}", grid=(B, r), in_specs=[spec(qcb), spec(kcb), spec(vcb), spec(0), st, st],
        out_specs=[spec(0), spec(0), spec(0)], out_shape=(osd, osd, osd), args=(q4, k4, v4, do4, lse4, dl4),
        scratch_shapes=[pltpu.VMEM((L, GW), f32), pltpu.VMEM((L, GW), f32)]
        + [pltpu.VMEM((HEADS, Q_BLOCK, 2 * Q_BLOCK), f32)] * 3 + [pltpu.VMEM((HEADS, Q_BLOCK, 2 * Q_BLOCK), bf16)] * 2,
        sem=("parallel", "parallel"), comm=comm)


RT = 512
RCH = GW // LANES
DILS = tuple(d for _, d in GROUPS[1:])


def _res_spec(r, width):
    return pl.BlockSpec((None, r, RT // r, width), lambda b, i, *_: (b, 0, i, 0))


def _tok_spec(width, cb=0):
    return pl.BlockSpec((None, RT, width), lambda b, i, *_: (b, i, cb))


def _to_residues(res_ref, scr, r, width):
    for c in range(r):
        for jj in range(width // LANES):
            res_ref[c, :, jj * LANES:(jj + 1) * LANES] = scr[jj, pl.ds(c, RT // r, stride=r), :].astype(res_ref.dtype)


def _from_residues(scr, res_ref, r, width):
    for c in range(r):
        for jj in range(width // LANES):
            scr[jj, pl.ds(c, RT // r, stride=r), :] = res_ref[c, :, jj * LANES:(jj + 1) * LANES].astype(f32)


def _qkv_to_residues(proj3, g):
    r = GROUPS[g][1]
    B = proj3.shape[0]

    def body(q_ref, k_ref, v_ref, o_ref, scr):
        for p, x_ref in enumerate((q_ref, k_ref, v_ref)):
            x = x_ref[...].astype(f32)
            for jj in range(RCH):
                scr[p * RCH + jj] = x[:, jj * LANES:(jj + 1) * LANES]
        _to_residues(o_ref, scr, r, ATTN_W)

    return pl.pallas_call(
        body, grid=(B, SEQ // RT), in_specs=[_tok_spec(GW, 3 * p + g) for p in range(3)], out_specs=_res_spec(r, ATTN_W),
        out_shape=SDS((B, r, SEQ // r, ATTN_W), bf16), scratch_shapes=[pltpu.VMEM((3 * RCH, RT, LANES), f32)],
        name=f"qkv_to_residues_g{g}", compiler_params=_cp(("parallel", "parallel")))(proj3, proj3, proj3)


def _attn_mix(o0, l0, o_res, l_res):
    B = o0.shape[0]

    def body(o0_ref, l0_ref, o1_ref, o2_ref, l1_ref, l2_ref, y_ref, lt_ref, lt1_ref, lt2_ref, so, sl):
        for gi, (o_ref, l_ref, r) in enumerate(((o1_ref, l1_ref, DILS[0]), (o2_ref, l2_ref, DILS[1]))):
            _from_residues(so.at[gi], o_ref, r, GW)
            _from_residues(sl.at[gi:gi + 1], l_ref, r, LANES)
        ls = [l0_ref[...], sl[0], sl[1]]
        m = functools.reduce(jnp.maximum, ls)
        ws = [jnp.exp(l - m) for l in ls]
        den = ws[0] + ws[1] + ws[2]
        alphas = [w / den for w in ws]
        lt = m + jnp.log(den)
        lt_ref[...] = lt
        sl[2] = lt
        _to_residues(lt1_ref, sl.at[2:3], DILS[0], LANES)
        _to_residues(lt2_ref, sl.at[2:3], DILS[1], LANES)
        for h in range(HEADS):
            cols = slice(h * HEAD_DIM, (h + 1) * HEAD_DIM)
            jj, lo = divmod(h * HEAD_DIM, LANES)
            acc = alphas[0][:, h:h + 1] * o0_ref[:, cols].astype(f32)
            for gi in range(2):
                acc = acc + alphas[gi + 1][:, h:h + 1] * so[gi, jj, :, lo:lo + HEAD_DIM]
            y_ref[:, cols] = acc.astype(bf16)

    in_specs = [_tok_spec(GW), _tok_spec(LANES), _res_spec(DILS[0], GW), _res_spec(DILS[1], GW), _res_spec(DILS[0], LANES), _res_spec(DILS[1], LANES)]
    out_specs = [_tok_spec(GW), _tok_spec(LANES), _res_spec(DILS[0], LANES), _res_spec(DILS[1], LANES)]
    return pl.pallas_call(
        body, grid=(B, SEQ // RT), in_specs=in_specs, out_specs=out_specs,
        out_shape=(SDS((B, SEQ, GW), bf16), SDS((B, SEQ, LANES), f32)) + tuple(SDS((B, r, SEQ // r, LANES), f32) for r in DILS),
        scratch_shapes=[pltpu.VMEM((2, RCH, RT, LANES), f32), pltpu.VMEM((3, RT, LANES), f32)],
        name="attn_mix", compiler_params=_cp(("parallel", "parallel")))(o0, l0, *o_res, *l_res)


def _attn_delta(dmix, mix):
    B = dmix.shape[0]

    def body(d_ref, y_ref, dl_ref, dl1_ref, dl2_ref, dm1_ref, dm2_ref, sx, sd):
        lane = lax.broadcasted_iota(jnp.int32, (RT, LANES), 1)
        acc = jnp.zeros((RT, LANES), f32)
        dv = d_ref[...].astype(f32)
        for jj in range(RCH):
            sx[jj] = dv[:, jj * LANES:(jj + 1) * LANES]
        for h in range(HEADS):
            cols = slice(h * HEAD_DIM, (h + 1) * HEAD_DIM)
            dl = jnp.sum(dv[:, cols] * y_ref[:, cols].astype(f32), axis=-1, keepdims=True)
            acc = jnp.where(lane == h, dl, acc)
        dl_ref[...] = acc
        sd[0] = acc
        _to_residues(dl1_ref, sd, DILS[0], LANES)
        _to_residues(dl2_ref, sd, DILS[1], LANES)
        _to_residues(dm1_ref, sx, DILS[0], GW)
        _to_residues(dm2_ref, sx, DILS[1], GW)

    return pl.pallas_call(
        body, grid=(B, SEQ // RT), in_specs=[_tok_spec(GW), _tok_spec(GW)],
        out_specs=[_tok_spec(LANES), _res_spec(DILS[0], LANES), _res_spec(DILS[1], LANES), _res_spec(DILS[0], GW), _res_spec(DILS[1], GW)],
        out_shape=(SDS((B, SEQ, LANES), f32),) + tuple(SDS((B, r, SEQ // r, LANES), f32) for r in DILS)
        + tuple(SDS((B, r, SEQ // r, GW), bf16) for r in DILS),
        scratch_shapes=[pltpu.VMEM((RCH, RT, LANES), f32), pltpu.VMEM((1, RT, LANES), f32)],
        name="attn_delta", compiler_params=_cp(("parallel", "parallel")))(dmix, mix)


N_CB = IN_W // CB


def _assemble_dproj(dqkv, dua, dub, dga, dgc):
    B = dua.shape[0]
    ng = len(GROUPS)
    flat = [dqkv[g][p] for p in range(3) for g in range(ng)]
    wide = [dua, dub, dga, dgc]

    def body(*refs):
        srcs, wides, o_ref, scr = refs[:3 * ng], refs[3 * ng:3 * ng + 4], refs[3 * ng + 4], refs[3 * ng + 5]
        for jv in range(3 * ng):
            g = jv % ng
            if g == 0:
                o_ref[:, jv * GW:(jv + 1) * GW] = srcs[jv][...]
            else:
                _from_residues(scr, srcs[jv], GROUPS[g][1], GW)
                for jj in range(RCH):
                    o_ref[:, jv * GW + jj * LANES:jv * GW + (jj + 1) * LANES] = scr[jj].astype(bf16)
        for wv in range(4):
            lo = 3 * ATTN_W + wv * D_MODEL
            o_ref[:, lo:lo + D_MODEL] = wides[wv][...]

    in_specs = [_tok_spec(GW) if (jv % ng) == 0 else _res_spec(GROUPS[jv % ng][1], GW) for jv in range(3 * ng)]
    in_specs += [_tok_spec(D_MODEL)] * 4
    return pl.pallas_call(
        body, grid=(B, SEQ // RT), in_specs=in_specs, out_specs=_tok_spec(IN_W),
        out_shape=SDS((B, SEQ, IN_W), bf16), scratch_shapes=[pltpu.VMEM((RCH, RT, LANES), f32)],
        name="assemble_dproj", compiler_params=_cp(("parallel", "parallel")))(*flat, *wide)


def _same_shape_groups(names, arrays):
    groups = {}
    for n, a in zip(names, arrays):
        groups.setdefault(a.shape, ([], []))
        groups[a.shape][0].append(n)
        groups[a.shape][1].append(a)
    return list(groups.values())


def _add_pairs(parts, gots, core, name):
    k = len(parts)
    n, h, C = gots[0].shape

    def body(c_ref, *refs):
        for i in range(k):
            refs[2 * k + i][...] = (refs[i][...].astype(f32) + refs[k + i][...].astype(f32)).astype(bf16)

    blk = pl.BlockSpec((None, h, C), lambda s, c_ref: (s, 0, 0))
    own = pl.BlockSpec((None, h, C), lambda s, c_ref: (s, c_ref[0], 0))
    spec = pltpu.PrefetchScalarGridSpec(num_scalar_prefetch=1, grid=(n,), in_specs=[own] * k + [blk] * k, out_specs=[blk] * k)
    return pl.pallas_call(body, grid_spec=spec, out_shape=tuple(SDS((n, h, C), bf16) for _ in range(k)), name=name,
                          compiler_params=_cp(("parallel",)))(core, *parts, *gots)


def _sum4(sums, gots, chip, core, name):
    k = len(sums)
    _, h, C = sums[0].shape

    def body(s_ref, c_ref, *refs):
        for i in range(k):
            q_ref = refs[k + i]
            t = refs[i][...].astype(f32) + q_ref[0].astype(f32)
            t = t + q_ref[1].astype(f32)
            refs[2 * k + i][...] = t + q_ref[2].astype(f32)

    spec = pltpu.PrefetchScalarGridSpec(
        num_scalar_prefetch=2, grid=(1,),
        in_specs=[pl.BlockSpec((None, h, C), lambda i, s_ref, c_ref: (s_ref[0], 0, 0))] * k
        + [pl.BlockSpec((N_CHIPS - 1, h, C), lambda i, s_ref, c_ref: (0, 0, 0))] * k,
        out_specs=[pl.BlockSpec((h, C), lambda i, s_ref, c_ref: (c_ref[0], 0))] * k)
    return pl.pallas_call(body, grid_spec=spec, out_shape=tuple(SDS((2 * h, C), f32) for _ in range(k)), name=name,
                          compiler_params=_cp(("arbitrary",)))(chip, core, *sums, *gots)


def _adamw(ws, gs, ms, vs, name):
    k = len(ws)
    R, C = ws[0].shape
    rt = R
    for cand in (512, 256, 128, 64, 32, 16, 8):
        if R % cand == 0 and cand * C * 4 <= 2 ** 21:
            rt = cand
            break
    c1 = 1.0 / (1.0 - B1 ** STEP)
    c2 = 1.0 / (1.0 - B2 ** STEP)

    def body(*refs):
        for i in range(k):
            w_ref, g_ref, m_ref, v_ref = (refs[j * k + i] for j in range(4))
            d_ref, nm_ref, nv_ref = (refs[(4 + j) * k + i] for j in range(3))
            gv = g_ref[...]
            nm = B1 * m_ref[...] + (1.0 - B1) * gv
            nv = B2 * v_ref[...] + (1.0 - B2) * (gv * gv)
            nm_ref[...] = nm
            nv_ref[...] = nv
            d_ref[...] = -LR * ((nm * c1) / (jnp.sqrt(nv * c2) + ADAM_EPS) + WD * w_ref[...])

    blk = pl.BlockSpec((rt, C), lambda i: (i, 0))
    sd = SDS((R, C), f32)
    res = pl.pallas_call(body, grid=(R // rt,), in_specs=[blk] * (4 * k), out_specs=[blk] * (3 * k), out_shape=(sd,) * (3 * k),
                         name=name, compiler_params=_cp(("parallel",)))(*ws, *gs, *ms, *vs)
    return res[:k], res[k:2 * k], res[2 * k:]


def _coords():
    return lax.axis_index("x"), lax.axis_index("y"), lax.axis_index("c")


def _other_chips(x, y):
    return [(1 - x, y), (x, 1 - y), (1 - x, 1 - y)]


def _allgather_weights(shards, slots):
    n = len(shards)
    n_slot = 12

    def body(*refs):
        ins, outs = refs[:n], refs[2 * n:3 * n]
        send, recv = refs[3 * n:]
        x, y, c = _coords()
        me, cx, cy, cd = 2 * x + y, 2 * (1 - x) + y, 2 * x + (1 - y), 2 * (1 - x) + (1 - y)
        dev_x, dev_y, dev_s = (1 - x, y, c), (x, 1 - y, c), (x, y, 1 - c)
        started = []

        def copy(w, slot, src, dst, dev):
            return pltpu.make_async_remote_copy(src_ref=src, dst_ref=dst, send_sem=send.at[w, slot], recv_sem=recv.at[w, slot],
                                                device_id=dev, device_id_type=MESH)

        def go(cp):
            cp.start()
            started.append(cp)

        for w in range(n):
            q = shards[w].shape[0] // 4
            rows = [pl.ds(c * 2 * q + k * q, q) for k in range(2)]
            theirs = [pl.ds((1 - c) * 2 * q + k * q, q) for k in range(2)]
            own = [(ins[w].at[r, :], outs[w].at[me, r, :]) for r in rows]
            go(copy(w, 0, *own[0], dev_x))
            go(copy(w, 2, *own[1], dev_y))
            go(copy(w, 1, *own[1], dev_x))
            go(copy(w, 3, *own[0], dev_y))
            arrivals = [(0, cx, 0, (4, dev_y)), (2, cy, 1, (5, dev_x)), (1, cx, 1, None), (3, cy, 0, None), (4, cd, 0, None), (5, cd, 1, None)]
            for k, (slot, chip, quarter, onward) in enumerate(arrivals):
                blk = outs[w].at[chip, rows[quarter], :]
                copy(w, slot, blk, blk, dev_s).wait_recv()
                if onward is not None:
                    go(copy(w, onward[0], blk, blk, onward[1]))
                go(copy(w, 6 + k, blk, blk, dev_s))
            for k, (slot, chip, quarter, onward) in enumerate(arrivals):
                blk = outs[w].at[chip, theirs[quarter], :]
                copy(w, 6 + k, blk, blk, dev_s).wait_recv()
        for cp in started:
            cp.wait_send()

    return pl.pallas_call(
        body, in_specs=[HBM] * (2 * n), out_specs=[HBM] * n,
        out_shape=tuple(SDS((N_CHIPS,) + s.shape, s.dtype) for s in shards),
        input_output_aliases={n + w: w for w in range(n)},
        scratch_shapes=[pltpu.SemaphoreType.DMA((n, n_slot))] * 2,
        name="allgather_weights", compiler_params=pltpu.CompilerParams(has_side_effects=True))(*shards, *slots)


def _x_pair_exchange(parts):
    n = len(parts)
    halves = [p.shape[1] // 2 for p in parts]

    def copies(ins, outs, sems):
        send, recv = sems
        x, y, c = _coords()
        return [pltpu.make_async_remote_copy(src_ref=ins[w].at[:, pl.ds((1 - c) * halves[w], halves[w]), :], dst_ref=outs[w],
                                             send_sem=send.at[w], recv_sem=recv.at[w], device_id=(x, y, 1 - c), device_id_type=MESH)
                for w in range(n)]

    def start(ins, outs, sems):
        for cp in copies(ins, outs, sems):
            cp.start()

    def wait(ins, outs, sems):
        for cp in copies(ins, outs, sems):
            cp.wait()

    return _Exchange(ins=list(parts), out_shape=[SDS((N_CHIPS, p.shape[1] // 2, p.shape[2]), p.dtype) for p in parts],
                     scratch=[pltpu.SemaphoreType.DMA((n,))] * 2, aliases={}, start=start, wait=wait)


def _x_chip_exchange(sums):
    n = len(sums)

    def copies(ins, outs, sems):
        send, recv = sems
        x, y, c = _coords()
        return [pltpu.make_async_remote_copy(src_ref=ins[w].at[2 * px + py], dst_ref=outs[w].at[j], send_sem=send.at[w, j],
                                             recv_sem=recv.at[w, j], device_id=(px, py, c), device_id_type=MESH)
                for w in range(n) for j, (px, py) in enumerate(_other_chips(x, y))]

    def start(ins, outs, sems):
        for cp in copies(ins, outs, sems):
            cp.start()

    def wait(ins, outs, sems):
        for cp in copies(ins, outs, sems):
            cp.wait()

    return _Exchange(ins=list(sums), out_shape=[SDS((N_CHIPS - 1,) + s.shape[1:], s.dtype) for s in sums],
                     scratch=[pltpu.SemaphoreType.DMA((n, 3)), pltpu.SemaphoreType.DMA((n, 3))], aliases={}, start=start, wait=wait)


def _x_gather_ici(shards, slots):
    n = len(shards)
    halves = [s.shape[0] // 2 for s in shards]

    def copies(ins, outs, sems):
        send, recv = sems
        x, y, c = _coords()
        me = 2 * x + y
        out = []
        for w in range(n):
            mine = pl.ds(c * halves[w], halves[w])
            for j, (px, py) in enumerate(_other_chips(x, y)):
                snd = pltpu.make_async_remote_copy(src_ref=ins[w].at[mine, :], dst_ref=outs[w].at[me, mine, :], send_sem=send.at[w, j],
                                                   recv_sem=recv.at[w, j], device_id=(px, py, c), device_id_type=MESH)
                got = outs[w].at[2 * px + py, mine, :]
                rcv = pltpu.make_async_remote_copy(src_ref=got, dst_ref=got, send_sem=send.at[w, j], recv_sem=recv.at[w, j],
                                                   device_id=(px, py, c), device_id_type=MESH)
                out.append((snd, rcv))
        return out

    def start(ins, outs, sems):
        for snd, _ in copies(ins, outs, sems):
            snd.start()

    def wait(ins, outs, sems):
        for snd, rcv in copies(ins, outs, sems):
            rcv.wait_recv()
            snd.wait_send()

    return _Exchange(ins=list(shards) + list(slots), out_shape=[SDS(s.shape, s.dtype) for s in slots],
                     scratch=[pltpu.SemaphoreType.DMA((n, 3)), pltpu.SemaphoreType.DMA((n, 3))],
                     aliases={n + w: w for w in range(n)}, start=start, wait=wait)


def _x_gather_fwd(bufs):
    n = len(bufs)
    halves = [b.shape[1] // 2 for b in bufs]

    def copies(ins, outs, sems):
        send, recv = sems
        x, y, c = _coords()
        out = []
        for w in range(n):
            for j, (px, py) in enumerate(_other_chips(x, y)):
                mine = outs[w].at[2 * px + py, pl.ds(c * halves[w], halves[w]), :]
                theirs = outs[w].at[2 * px + py, pl.ds((1 - c) * halves[w], halves[w]), :]
                snd = pltpu.make_async_remote_copy(src_ref=mine, dst_ref=mine, send_sem=send.at[w, j], recv_sem=recv.at[w, j],
                                                   device_id=(x, y, 1 - c), device_id_type=MESH)
                rcv = pltpu.make_async_remote_copy(src_ref=theirs, dst_ref=theirs, send_sem=send.at[w, j], recv_sem=recv.at[w, j],
                                                   device_id=(x, y, 1 - c), device_id_type=MESH)
                out.append((snd, rcv))
        return out

    def start(ins, outs, sems):
        for snd, _ in copies(ins, outs, sems):
            snd.start()

    def wait(ins, outs, sems):
        for snd, rcv in copies(ins, outs, sems):
            rcv.wait_recv()
            snd.wait_send()

    return _Exchange(ins=list(bufs), out_shape=[SDS(b.shape, b.dtype) for b in bufs],
                     scratch=[pltpu.SemaphoreType.DMA((n, 3)), pltpu.SemaphoreType.DMA((n, 3))],
                     aliases={w: w for w in range(n)}, start=start, wait=wait)


def _final_exchange(bufs, vec):
    n = len(bufs)
    R = vec.shape[0]
    nd = 8

    def body(*refs):
        v_ref = refs[n]
        outs = refs[n + 1:2 * n + 1]
        o_ref, sbuf, send, recv, ssend, srecv = refs[2 * n + 1:]
        x, y, c = _coords()
        me = 4 * x + 2 * y + c
        sbuf[me] = v_ref[...]
        small = []
        for k in range(1, nd):
            kx, ky, kc = (k >> 2) & 1, (k >> 1) & 1, k & 1
            tx = x + kx - 2 * x * kx
            ty = y + ky - 2 * y * ky
            tc = c + kc - 2 * c * kc
            cp = pltpu.make_async_remote_copy(src_ref=v_ref, dst_ref=sbuf.at[me], send_sem=ssend.at[k], recv_sem=srecv.at[k],
                                              device_id=(tx, ty, tc), device_id_type=MESH)
            cp.start()
            small.append((cp, 4 * tx + 2 * ty + tc))
        cps = []
        for w in range(n):
            h = bufs[w].shape[0] // 2
            rows = outs[w].at[pl.ds(c * h, h), :]
            rc = pltpu.make_async_remote_copy(src_ref=rows, dst_ref=rows, send_sem=send.at[w], recv_sem=recv.at[w],
                                              device_id=(x, y, 1 - c), device_id_type=MESH)
            rc.start()
            cps.append(rc)
        for k, (cp, src) in zip(range(1, nd), small):
            pltpu.make_async_remote_copy(src_ref=v_ref, dst_ref=sbuf.at[src], send_sem=ssend.at[k], recv_sem=srecv.at[k],
                                         device_id=(x, y, c), device_id_type=MESH).wait_recv()
        for cp, _ in small:
            cp.wait_send()
        acc = sbuf[0]
        for d in range(1, nd):
            acc = acc + sbuf[d]
        o_ref[...] = acc
        for w, rc in enumerate(cps):
            h = bufs[w].shape[0] // 2
            other = outs[w].at[pl.ds((1 - c) * h, h), :]
            pltpu.make_async_remote_copy(src_ref=other, dst_ref=other, send_sem=send.at[w], recv_sem=recv.at[w],
                                         device_id=(x, y, 1 - c), device_id_type=MESH).wait_recv()
            rc.wait_send()

    vm = pl.BlockSpec(memory_space=pltpu.VMEM)
    res = pl.pallas_call(body, in_specs=[HBM] * n + [vm], out_specs=[HBM] * n + [vm],
                         out_shape=tuple(SDS(a.shape, a.dtype) for a in bufs) + (SDS((R, LANES), f32),),
                         input_output_aliases={w: w for w in range(n)},
                         scratch_shapes=[pltpu.VMEM((nd, R, LANES), f32)] + [pltpu.SemaphoreType.DMA((n,))] * 2
                         + [pltpu.SemaphoreType.DMA((nd,))] * 2, name="final_exchange",
                         compiler_params=pltpu.CompilerParams(has_side_effects=True))(*bufs, vec)
    return res[:n], res[n]


def _rows128(a, rows):
    flat = a.reshape(-1)
    return jnp.pad(flat, (0, rows * LANES - flat.shape[0])).reshape(rows, LANES)


GATHER_1 = ("w_conv_out", "w_attn_out", "w_o", "w_ffn_gate", "w_ffn_down")
REDUCE_A = ("w_ffn_gate", "w_ffn_up", "w_ffn_down")
REDUCE_B = ("w_conv_out", "w_attn_out", "w_o")
REDUCE_C = ("w_in",)


def _step(x, target, norm1_g, gate_b, conv_w, conv_b, conv_ln_g, conv_ln_b, norm2_g, norm_f_g, w_in, shards, slots, chip1, core):
    B = x.shape[0]
    T = B * SEQ
    xf = x.reshape(T, D_MODEL)
    tf = target.reshape(T, D_MODEL)

    h, h_t = _rms_fwd(xf, norm1_g, "rms1_fwd")
    proj, got1 = _mm("in_proj", h, w_in, grid=(N_CHIPS, T // 1024),
                     a_spec=pl.BlockSpec((1024, D_MODEL), lambda s, m: (m, 0)),
                     b_spec=pl.BlockSpec((None, D_MODEL, IN_S), lambda s, m: (s, 0, 0)),
                     o_spec=pl.BlockSpec((1024, IN_S), lambda s, m: (m, s)), o_shape=(T, IN_W), o_dtype=bf16, dims=NN,
                     comm=_x_gather_ici([shards[n] for n in GATHER_1], [slots[n] for n in GATHER_1]))
    proj3 = proj.reshape(B, SEQ, IN_W)
    proj4 = proj.reshape(B, 1, SEQ, IN_W)

    qkv = [None] + [_qkv_to_residues(proj3, g) for g in range(1, len(GROUPS))]

    def qkv_args(g):
        return (proj4, proj4, proj4, 0, 3, 6) if g == 0 else (qkv[g], qkv[g], qkv[g], 0, 1, 2)

    (o4_0, l4_0), full1 = _attn_fwd(0, *qkv_args(0), comm=_x_gather_fwd(list(got1)))
    o4_1, l4_1 = _attn_fwd(1, *qkv_args(1))
    o4_2, l4_2 = _attn_fwd(2, *qkv_args(2))
    full = dict(zip(GATHER_1, full1))
    w_conv_out, w_attn_out, w_o, w_gate, w_down = (full[n] for n in GATHER_1)
    w_conv_out_f = w_conv_out.reshape(D_MODEL, D_MODEL)
    w_o_f = w_o.reshape(D_MODEL, D_MODEL)
    mix3, lse3, lse_r1, lse_r2 = _attn_mix(o4_0.reshape(B, SEQ, GW), l4_0.reshape(B, SEQ, LANES), [o4_1, o4_2], [l4_1, l4_2])
    mix = mix3.reshape(T, GW)
    y_attn = _attn_out_fwd(mix, w_attn_out)

    c1, got_up = _glu_conv_fwd(proj3, conv_w, conv_b, comm=_x_gather_ici([shards["w_ffn_up"]], [slots["w_ffn_up"]]))
    c1 = c1.reshape(T, D_MODEL)
    (c3, y_conv), (w_up,) = _conv_out_fwd(c1, conv_ln_g, conv_ln_b, w_conv_out_f, comm=_x_gather_fwd(list(got_up)))

    merged, x1, h2 = _merge_o_proj_rms(proj, gate_b, y_attn, y_conv, w_o_f, xf, norm2_g)
    fa, fb, ff = _ffn_up(h2, w_gate, w_up)
    loss, dx2, d_gf = _ffn_down_loss(ff, w_down, x1, norm_f_g.reshape(1, D_MODEL), tf)

    d_w_down = _mm("d_w_down", ff, dx2, grid=(N_CHIPS, T // 2048),
                   a_spec=pl.BlockSpec((None, 2048, FF_S), lambda s, k: (s, k, 0)),
                   b_spec=pl.BlockSpec((2048, D_MODEL), lambda s, k: (k, 0)),
                   o_spec=pl.BlockSpec((None, FF_S, D_MODEL), lambda s, k: (s, 0, 0)), o_shape=(N_CHIPS, FF_S, D_MODEL),
                   o_dtype=bf16, dims=TN, acc_shape=(FF_S, D_MODEL), k_axis=1, sem=("parallel", "arbitrary"))
    da, db = _ffn_down_bwd(dx2, w_down, fa, fb)

    d_w_gate, d_w_up = _ffn_up_bwd_w(da, db, h2)
    part = dict(w_ffn_gate=d_w_gate, w_ffn_up=d_w_up, w_ffn_down=d_w_down)

    def pair_sums(names, got):
        out = {}
        for ns, gs in _same_shape_groups(names, got):
            out.update(zip(ns, _add_pairs([part[n] for n in ns], gs, core, "pair_sum_" + ns[0])))
        return [out[n] for n in names]

    (dx1, d_g2), got = _mm_rms_bwd("ffn_dh2", [(da, w_gate), (db, w_up)], x1, norm2_g, dx2, tm=1024,
                                   a_spec=pl.BlockSpec((None, 1024, FF_S), lambda m, s: (s, m, 0)),
                                   b_spec=pl.BlockSpec((None, FF_S, D_MODEL), lambda m, s: (s, 0, 0)), dims=NN,
                                   comm=_x_pair_exchange([part[n] for n in REDUCE_A]), vmem=VMEM_LIMIT_BIG)
    sums_a = pair_sums(REDUCE_A, got)

    d_w_o = _mm_tn_tokens("d_w_o", merged, dx1, bf16).reshape(N_CHIPS, D_MODEL // N_CHIPS, D_MODEL)
    dmerged = _mm_nt_full("d_merged", dx1, w_o_f, bf16)
    dya, dyc, dga, dgc, d_gba, d_gbc = _merge_bwd(dmerged, proj, gate_b, y_attn, y_conv)

    d_w_conv_out = _mm_tn_tokens("d_w_conv_out", c3, dyc, bf16).reshape(N_CHIPS, D_MODEL // N_CHIPS, D_MODEL)
    dc1, d_ln_g, d_ln_b = _conv_out_bwd(dyc, w_conv_out_f, c1, conv_ln_g, conv_ln_b)
    (dua, dub, d_conv_w, d_conv_b), got_a = _glu_conv_bwd(dc1.reshape(B, SEQ, D_MODEL), proj3, conv_w, comm=_x_chip_exchange(sums_a))

    d_w_attn_out = _attn_out_bwd_w(mix, dya)
    part.update(w_conv_out=d_w_conv_out, w_attn_out=d_w_attn_out, w_o=d_w_o)
    dmix, got = _attn_out_bwd_x(dya, w_attn_out, comm=_x_pair_exchange([part[n] for n in REDUCE_B]))
    sums_b = pair_sums(REDUCE_B, got)
    dmix3 = dmix.reshape(B, SEQ, GW)
    delta3, delta_r1, delta_r2, dmix_r1, dmix_r2 = _attn_delta(dmix3, mix3)
    one = (B, 1, SEQ)
    (dq0, dk0, dv0), got_b = _attn_bwd(0, *qkv_args(0), dmix3.reshape(one + (GW,)), lse3.reshape(one + (LANES,)),
                                       delta3.reshape(one + (LANES,)), comm=_x_chip_exchange(sums_b))
    dqkv = [tuple(t.reshape(B, SEQ, GW) for t in (dq0, dk0, dv0)),
            _attn_bwd(1, *qkv_args(1), dmix_r1, lse_r1, delta_r1), _attn_bwd(2, *qkv_args(2), dmix_r2, lse_r2, delta_r2)]
    dproj = _assemble_dproj(dqkv, dua, dub, dga.reshape(B, SEQ, D_MODEL), dgc.reshape(B, SEQ, D_MODEL)).reshape(T, IN_W)

    d_w_in = _mm("d_w_in", h_t, dproj, grid=(N_CHIPS, T // 2048),
                 a_spec=pl.BlockSpec((D_MODEL, 2048), lambda s, k: (0, k)),
                 b_spec=pl.BlockSpec((2048, IN_S), lambda s, k: (k, s)),
                 o_spec=pl.BlockSpec((None, D_MODEL, IN_S), lambda s, k: (s, 0, 0)), o_shape=(N_CHIPS, D_MODEL, IN_S),
                 o_dtype=bf16, dims=NN, acc_shape=(D_MODEL, IN_S), k_axis=1, sem=("parallel", "arbitrary"), vmem=VMEM_LIMIT_BIG)
    part.update(w_in=d_w_in)
    sums_c = pair_sums(REDUCE_C, _run_exchange(_x_pair_exchange([d_w_in]), "grad_pair_exchange_c"))
    (dx, d_g1), got_c = _mm_rms_bwd("d_h", [(dproj, w_in)], xf, norm1_g, dx1, tm=1024,
                                    a_spec=pl.BlockSpec((1024, IN_S), lambda m, s: (m, s)),
                                    b_spec=pl.BlockSpec((None, D_MODEL, IN_S), lambda m, s: (s, 0, 0)), dims=NT,
                                    comm=_x_chip_exchange(sums_c), vmem=VMEM_LIMIT_BIG)

    names = REDUCE_A + REDUCE_B + REDUCE_C
    sums = dict(zip(names, sums_a + sums_b + sums_c))
    halves = {}
    for ns, gs in _same_shape_groups(names, got_a + got_b + got_c):
        halves.update(zip(ns, _sum4([sums[n] for n in ns], gs, chip1, core, "chip_sum_" + ns[0])))
    big = {n: halves[n] for n in names}
    small = dict(norm1_g=d_g1, gate_b=jnp.concatenate([d_gba, d_gbc], axis=-1), conv_b=d_conv_b, conv_ln_g=d_ln_g,
                 conv_ln_b=d_ln_b, norm2_g=d_g2, norm_f_g=d_gf, conv_w=d_conv_w)
    return loss, dx.reshape(B, SEQ, D_MODEL), big, small


BIG = ("w_in", "w_conv_out", "w_attn_out", "w_o", "w_ffn_gate", "w_ffn_up", "w_ffn_down")
TRANSPOSED = ("w_ffn_gate", "w_ffn_up")
SMALL = ("norm1_g", "gate_b", "conv_b", "conv_ln_g", "conv_ln_b", "norm2_g", "norm_f_g")
SMALL_ROWS = {"norm1_g": 8, "gate_b": 16, "conv_b": 8, "conv_ln_g": 8, "conv_ln_b": 8, "norm2_g": 8, "norm_f_g": 8}
LOSS_ROWS = 8
CONVW_ROWS = 32 * D_MODEL // LANES


def kernel(x, norm1_g, w_in, gate_b, conv_w, conv_b, conv_ln_g, conv_ln_b, w_conv_out, w_attn_out, w_o, norm2_g, w_ffn_gate, w_ffn_up, w_ffn_down, norm_f_g, loss_target, m_norm1_g, m_w_in, m_gate_b, m_conv_w, m_conv_b, m_conv_ln_g, m_conv_ln_b, m_w_conv_out, m_w_attn_out, m_w_o, m_norm2_g, m_w_ffn_gate, m_w_ffn_up, m_w_ffn_down, m_norm_f_g, v_norm1_g, v_w_in, v_gate_b, v_conv_w, v_conv_b, v_conv_ln_g, v_conv_ln_b, v_w_conv_out, v_w_attn_out, v_w_o, v_norm2_g, v_w_ffn_gate, v_w_ffn_up, v_w_ffn_down, v_norm_f_g):
    W = dict(norm1_g=norm1_g, w_in=w_in, gate_b=gate_b, conv_w=conv_w, conv_b=conv_b, conv_ln_g=conv_ln_g, conv_ln_b=conv_ln_b,
             w_conv_out=w_conv_out, w_attn_out=w_attn_out, w_o=w_o, norm2_g=norm2_g, w_ffn_gate=w_ffn_gate, w_ffn_up=w_ffn_up,
             w_ffn_down=w_ffn_down, norm_f_g=norm_f_g)
    M = dict(norm1_g=m_norm1_g, w_in=m_w_in, gate_b=m_gate_b, conv_w=m_conv_w, conv_b=m_conv_b, conv_ln_g=m_conv_ln_g,
             conv_ln_b=m_conv_ln_b, w_conv_out=m_w_conv_out, w_attn_out=m_w_attn_out, w_o=m_w_o, norm2_g=m_norm2_g,
             w_ffn_gate=m_w_ffn_gate, w_ffn_up=m_w_ffn_up, w_ffn_down=m_w_ffn_down, norm_f_g=m_norm_f_g)
    V = dict(norm1_g=v_norm1_g, w_in=v_w_in, gate_b=v_gate_b, conv_w=v_conv_w, conv_b=v_conv_b, conv_ln_g=v_conv_ln_g,
             conv_ln_b=v_conv_ln_b, w_conv_out=v_w_conv_out, w_attn_out=v_w_attn_out, w_o=v_w_o, norm2_g=v_norm2_g,
             w_ffn_gate=v_w_ffn_gate, w_ffn_up=v_w_ffn_up, w_ffn_down=v_w_ffn_down, norm_f_g=v_norm_f_g)
    order = list(W)

    def as2d(n, a):
        a = a.reshape(a.shape[-2:])
        return a.T if n in TRANSPOSED else a

    def from2d(n, a):
        return (a.T if n in TRANSPOSED else a).reshape(W[n].shape)

    shard2d = {n: as2d(n, W[n]) for n in BIG}
    chip = 2 * lax.axis_index("x") + lax.axis_index("y")

    core = lax.axis_index("c").astype(jnp.int32).reshape(1)
    chip1 = chip.astype(jnp.int32).reshape(1)
    shards = {n: shard2d[n].astype(bf16) for n in BIG}

    def slot_for(s):
        return lax.dynamic_update_slice(lax.empty((N_CHIPS,) + s.shape, s.dtype), s[None], (chip, 0, 0))

    slots = {n: slot_for(s) for n, s in shards.items()}
    cw = jnp.pad(conv_w.reshape(CONV_K, D_MODEL // N_CHIPS), ((0, CONV_HALO - CONV_K), (0, 0)))
    w_in_full, cw4 = _allgather_weights([shards["w_in"], cw], [slots["w_in"], slot_for(cw)])
    conv_w_full = cw4.transpose(1, 0, 2).reshape(CONV_HALO, D_MODEL)[:CONV_K]

    loss, grad_x, grads, gsmall = _step(x, loss_target, norm1_g, gate_b, conv_w_full, conv_b, conv_ln_g, conv_ln_b, norm2_g,
                                        norm_f_g, w_in_full, shards, slots, chip1, core)

    pieces = [_rows128(loss, LOSS_ROWS)] + [_rows128(gsmall[n], SMALL_ROWS[n]) for n in SMALL] + [_rows128(gsmall["conv_w"], CONVW_ROWS)]
    full_rows, tot = _final_exchange([grads[n] for n in BIG], jnp.concatenate(pieces, axis=0))
    grads = dict(zip(BIG, full_rows))
    loss_out = tot[0, 0]
    row = LOSS_ROWS
    for n in SMALL:
        grads[n] = tot[row: row + W[n].size // LANES].reshape(W[n].shape)
        row += SMALL_ROWS[n]
    dcw = tot[row: row + CONV_K * D_MODEL // LANES].reshape(CONV_K, D_MODEL)
    grads["conv_w"] = lax.dynamic_slice(dcw, (0, chip * (D_MODEL // N_CHIPS)), (CONV_K, D_MODEL // N_CHIPS))

    delta, new_m, new_v = {}, {}, {}
    for ns, ws in _same_shape_groups(BIG, [shard2d[n] for n in BIG]):
        ds, nms, nvs = _adamw(ws, [grads[n] for n in ns], [as2d(n, M[n]) for n in ns], [as2d(n, V[n]) for n in ns], "adamw_" + ns[0])
        for n, d, nm, nv in zip(ns, ds, nms, nvs):
            delta[n], new_m[n], new_v[n], grads[n] = (from2d(n, t) for t in (d, nm, nv, grads[n]))

    def pack(src):
        return jnp.concatenate([_rows128(src[n], SMALL_ROWS[n]) for n in SMALL], axis=0)

    (d,), (nm,), (nv,) = _adamw([pack(W)], [pack(grads)], [pack(M)], [pack(V)], "adamw_small")
    row = 0
    for n in SMALL:
        k = W[n].size // LANES
        delta[n], new_m[n], new_v[n] = (t[row: row + k].reshape(W[n].shape) for t in (d, nm, nv))
        row += SMALL_ROWS[n]

    def pad32(a):
        return jnp.pad(a.reshape(CONV_K, D_MODEL // N_CHIPS), ((0, 1), (0, 0)))

    (d,), (nm,), (nv,) = _adamw([pad32(conv_w)], [pad32(grads["conv_w"])], [pad32(m_conv_w)], [pad32(v_conv_w)], "adamw_conv_w")
    delta["conv_w"], new_m["conv_w"], new_v["conv_w"] = (t[:CONV_K].reshape(conv_w.shape) for t in (d, nm, nv))
    grads["conv_w"] = grads["conv_w"].reshape(conv_w.shape)

    return (loss_out, grad_x, *[grads[n] for n in order], *[delta[n] for n in order],
            *[new_m[n] for n in order], *[new_v[n] for n in order])
```

```python
import functools
import math
from typing import Callable, NamedTuple

import numpy as np
import jax
import jax.numpy as jnp
from jax import lax
from jax.experimental import pallas as pl
from jax.experimental.pallas import tpu as pltpu

f32 = jnp.float32
bf16 = jnp.bfloat16
SDS = jax.ShapeDtypeStruct
MESH = pl.DeviceIdType.MESH

D_MODEL = 1024
SEQ = 2048
HEAD_DIM = 64
HEADS = 8
GROUPS = ((128, 1), (512, 4), (2048, 16))
GW = HEADS * HEAD_DIM
ATTN_W = len(GROUPS) * GW
Q_BLOCK = 128
CONV_K = 31
D_FF = 2816
IN_W = 3 * ATTN_W + 2 * D_MODEL + 2 * D_MODEL
N_CHIPS = 4
IN_S = IN_W // N_CHIPS
FF_S = D_FF // N_CHIPS
RMS_EPS = 1e-6
LN_EPS = 1e-5
LR, B1, B2, ADAM_EPS, WD, STEP = 0.001, 0.9, 0.999, 1e-08, 0.01, 10
NEG = -1e30
LANES = 128
VMEM_LIMIT = 48 * 2 ** 20
VMEM_LIMIT_BIG = 54 * 2 ** 20
CB = 512
UA_CB, UB_CB, GA_CB, GC_CB = 9, 11, 13, 15


def _alibi_slope_list(n):
    def pow2(m):
        start = 2.0 ** (-8.0 / m)
        return [start ** (i + 1) for i in range(m)]
    if math.log2(n).is_integer():
        return pow2(n)
    c = 2 ** math.floor(math.log2(n))
    return pow2(c) + _alibi_slope_list(2 * c)[0::2][: n - c]


_SLOPES = np.asarray(sorted(_alibi_slope_list(len(GROUPS) * HEADS), reverse=True), dtype=np.float32).reshape(len(GROUPS), HEADS)


def _cp(sem=None, vmem=VMEM_LIMIT):
    return pltpu.CompilerParams(dimension_semantics=sem, vmem_limit_bytes=vmem)


def _sigmoid(x):
    return 1.0 / (1.0 + jnp.exp(-x))


HBM = pl.BlockSpec(memory_space=pl.ANY)


class _Exchange(NamedTuple):
    ins: list
    out_shape: list
    scratch: list
    aliases: dict
    start: Callable
    wait: Callable


def _launch(body, *, name, grid, in_specs, out_specs, out_shape, args, scratch_shapes=(), sem=None, comm=None, vmem=VMEM_LIMIT):
    if comm is None:
        return pl.pallas_call(body, grid=grid, in_specs=in_specs, out_specs=out_specs, out_shape=out_shape,
                              scratch_shapes=list(scratch_shapes), name=name, compiler_params=_cp(sem, vmem))(*args)
    multi = isinstance(out_shape, (tuple, list))
    m_out = list(out_shape) if multi else [out_shape]
    m_ospec = list(out_specs) if multi else [out_specs]
    n_in, n_out, n_scr = len(in_specs), len(m_out), len(scratch_shapes)
    nc_in, nc_out = len(comm.ins), len(comm.out_shape)

    def hosted(*refs):
        bounds = np.cumsum([0, n_in, nc_in, n_out, nc_out, n_scr])
        mi, ci, mo, co, ms = (refs[a:b] for a, b in zip(bounds[:-1], bounds[1:]))
        cs = refs[bounds[-1]:]
        ids = [pl.program_id(a) for a in range(len(grid))]
        first = functools.reduce(jnp.logical_and, [i == 0 for i in ids])
        last = functools.reduce(jnp.logical_and, [i == g - 1 for i, g in zip(ids, grid)])

        @pl.when(first)
        def _():
            comm.start(ci, co, cs)

        body(*mi, *mo, *ms)

        @pl.when(last)
        def _():
            comm.wait(ci, co, cs)

    res = pl.pallas_call(
        hosted, grid=grid, in_specs=list(in_specs) + [HBM] * nc_in, out_specs=m_ospec + [HBM] * nc_out,
        out_shape=tuple(m_out) + tuple(comm.out_shape), scratch_shapes=list(scratch_shapes) + list(comm.scratch),
        input_output_aliases={n_in + i: n_out + o for i, o in comm.aliases.items()}, name=name + "_comm",
        compiler_params=pltpu.CompilerParams(dimension_semantics=("arbitrary",) * len(grid), vmem_limit_bytes=vmem,
                                             has_side_effects=True))(*args, *comm.ins)
    return (tuple(res[:n_out]) if multi else res[0]), tuple(res[n_out:])


def _run_exchange(ex, name):
    n_in = len(ex.ins)

    def body(*refs):
        ins, outs, sems = refs[:n_in], refs[n_in:n_in + len(ex.out_shape)], refs[n_in + len(ex.out_shape):]
        ex.start(ins, outs, sems)
        ex.wait(ins, outs, sems)

    return pl.pallas_call(body, in_specs=[HBM] * n_in, out_specs=[HBM] * len(ex.out_shape), out_shape=tuple(ex.out_shape),
                          scratch_shapes=list(ex.scratch), input_output_aliases=dict(ex.aliases), name=name,
                          compiler_params=pltpu.CompilerParams(has_side_effects=True))(*ex.ins)


def _rmsnorm(xv, gv):
    return xv * lax.rsqrt(jnp.mean(xv * xv, axis=-1, keepdims=True) + RMS_EPS) * gv


def _rms_fwd(x, g, name):
    T = x.shape[0]
    tm = 512

    def body(x_ref, g_ref, o_ref, ot_ref):
        hv = _rmsnorm(x_ref[...], g_ref[...])
        o_ref[...] = hv.astype(bf16)
        ot_ref[...] = hv.T.astype(bf16)

    row = pl.BlockSpec((tm, D_MODEL), lambda i: (i, 0))
    vec = pl.BlockSpec((1, D_MODEL), lambda i: (0, 0))
    return pl.pallas_call(body, grid=(T // tm,), in_specs=[row, vec], out_specs=[row, pl.BlockSpec((D_MODEL, tm), lambda i: (0, i))],
                          out_shape=(SDS((T, D_MODEL), bf16), SDS((D_MODEL, T), bf16)), name=name,
                          compiler_params=_cp(("parallel",)))(x, g)


def _rms_bwd_tile(dyv, xv, gv, dres):
    r = lax.rsqrt(jnp.mean(xv * xv, axis=-1, keepdims=True) + RMS_EPS)
    xh = xv * r
    dxh = dyv * gv
    dx = dres + r * (dxh - xh * jnp.mean(dxh * xh, axis=-1, keepdims=True))
    return dx, jnp.sum(dyv * xh, axis=0, keepdims=True)


def _accumulate(first, refs_parts):
    @pl.when(first)
    def _():
        for ref, part in refs_parts:
            ref[...] = part

    @pl.when(jnp.logical_not(first))
    def _():
        for ref, part in refs_parts:
            ref[...] += part


def _mm_rms_bwd(name, ops, x, g, dres, *, tm, a_spec, b_spec, dims, comm=None, vmem=VMEM_LIMIT):
    T = x.shape[0]
    n = len(ops)

    def body(*refs):
        ab_refs = refs[:2 * n]
        x_ref, g_ref, r_ref, dx_ref, dg_ref, acc = refs[2 * n:]
        m, s = pl.program_id(0), pl.program_id(1)
        p = sum(lax.dot_general(ab_refs[2 * i][...], ab_refs[2 * i + 1][...], dims, preferred_element_type=f32) for i in range(n))

        @pl.when(s == 0)
        def _():
            acc[...] = p

        @pl.when(s > 0)
        def _():
            acc[...] += p

        @pl.when(s == N_CHIPS - 1)
        def _():
            dx, part = _rms_bwd_tile(acc[...], x_ref[...], g_ref[...], r_ref[...])
            dx_ref[...] = dx
            _accumulate(m == 0, [(dg_ref, part)])

    row = pl.BlockSpec((tm, D_MODEL), lambda m, s: (m, 0))
    vec = pl.BlockSpec((1, D_MODEL), lambda m, s: (0, 0))
    return _launch(body, name=name, grid=(T // tm, N_CHIPS), in_specs=[a_spec, b_spec] * n + [row, vec, row], out_specs=[row, vec],
                   out_shape=(SDS((T, D_MODEL), f32), SDS((1, D_MODEL), f32)), args=tuple(t for ab in ops for t in ab) + (x, g, dres),
                   scratch_shapes=[pltpu.VMEM((tm, D_MODEL), f32)], sem=("arbitrary", "arbitrary"), comm=comm, vmem=vmem)


def _ffn_down_loss(ff, wd, x1, gf, target):
    T = x1.shape[0]
    tm = 1024
    nm = T // tm

    def body(f_ref, w_ref, x_ref, g_ref, t_ref, loss_ref, dx_ref, dg_ref, acc):
        m, s = pl.program_id(0), pl.program_id(1)
        p = jnp.dot(f_ref[...], w_ref[...], preferred_element_type=f32)

        @pl.when(s == 0)
        def _():
            acc[...] = p

        @pl.when(s > 0)
        def _():
            acc[...] += p

        @pl.when(s == N_CHIPS - 1)
        def _():
            xv = acc[...] + x_ref[...]
            gv = g_ref[...]
            r = lax.rsqrt(jnp.mean(xv * xv, axis=-1, keepdims=True) + RMS_EPS)
            xh = xv * r
            e = xh * gv - t_ref[...]
            part_l = jnp.broadcast_to(0.5 * jnp.sum(jnp.mean(e * e, axis=-1, keepdims=True), axis=0, keepdims=True), (1, LANES))
            dy = e * (1.0 / D_MODEL)
            dxh = dy * gv
            dx_ref[...] = r * (dxh - xh * jnp.mean(dxh * xh, axis=-1, keepdims=True))
            part_g = jnp.sum(dy * xh, axis=0, keepdims=True)

            @pl.when(m == 0)
            def _():
                loss_ref[...] = part_l
                dg_ref[...] = part_g

            @pl.when(m > 0)
            def _():
                loss_ref[...] += part_l
                dg_ref[...] += part_g

    row = pl.BlockSpec((tm, D_MODEL), lambda m, s: (m, 0))
    vec = pl.BlockSpec((1, D_MODEL), lambda m, s: (0, 0))
    return pl.pallas_call(
        body, grid=(nm, N_CHIPS),
        in_specs=[pl.BlockSpec((None, tm, FF_S), lambda m, s: (s, m, 0)), pl.BlockSpec((None, FF_S, D_MODEL), lambda m, s: (s, 0, 0)),
                  row, vec, row],
        out_specs=[pl.BlockSpec((1, LANES), lambda m, s: (0, 0)), row, vec],
        out_shape=(SDS((1, LANES), f32), SDS((T, D_MODEL), f32), SDS((1, D_MODEL), f32)),
        scratch_shapes=[pltpu.VMEM((tm, D_MODEL), f32)], name="ffn_down_loss", compiler_params=_cp(("arbitrary", "arbitrary")))(
            ff, wd, x1, gf, target)


def _merge_o_proj_rms(proj, gate_b, ya, yc, w_o, x, g2):
    T = x.shape[0]
    tm = 512

    def body(ga0, ga1, gc0, gc1, b_ref, ya_ref, yc_ref, w_ref, x_ref, g_ref, m_ref, x1_ref, h2_ref):
        bv = b_ref[...]
        ga = jnp.concatenate([ga0[...], ga1[...]], axis=-1).astype(f32)
        gc = jnp.concatenate([gc0[...], gc1[...]], axis=-1).astype(f32)
        merged = (_sigmoid(ga + bv[:, :D_MODEL]) * ya_ref[...].astype(f32)
                  + _sigmoid(gc + bv[:, D_MODEL:]) * yc_ref[...].astype(f32)).astype(bf16)
        m_ref[...] = merged
        x1 = jnp.dot(merged, w_ref[...], preferred_element_type=f32) + x_ref[...]
        x1_ref[...] = x1
        h2_ref[...] = _rmsnorm(x1, g_ref[...]).astype(bf16)

    row = pl.BlockSpec((tm, D_MODEL), lambda i: (i, 0))
    return pl.pallas_call(
        body, grid=(T // tm,),
        in_specs=[pl.BlockSpec((tm, CB), lambda i, cb=cb: (i, cb)) for cb in (GA_CB, GA_CB + 1, GC_CB, GC_CB + 1)]
        + [pl.BlockSpec((1, 2 * D_MODEL), lambda i: (0, 0)), row, row, pl.BlockSpec((D_MODEL, D_MODEL), lambda i: (0, 0)), row,
           pl.BlockSpec((1, D_MODEL), lambda i: (0, 0))],
        out_specs=[row, row, row], out_shape=(SDS((T, D_MODEL), bf16), SDS((T, D_MODEL), f32), SDS((T, D_MODEL), bf16)),
        name="merge_o_proj_rms", compiler_params=_cp(("parallel",)))(proj, proj, proj, proj, gate_b, ya, yc, w_o, x, g2)


def _conv_out_fwd(c1, g, b, w, comm=None):
    T = c1.shape[0]
    tm = 1024

    def body(c_ref, g_ref, b_ref, w_ref, c3_ref, y_ref):
        cv = c_ref[...]
        mu = jnp.mean(cv, axis=-1, keepdims=True)
        cc = cv - mu
        var = jnp.mean(cc * cc, axis=-1, keepdims=True)
        c2 = cc * lax.rsqrt(var + LN_EPS) * g_ref[...] + b_ref[...]
        c3 = (c2 * _sigmoid(c2)).astype(bf16)
        c3_ref[...] = c3
        y_ref[...] = jnp.dot(c3, w_ref[...], preferred_element_type=f32).astype(bf16)

    row = pl.BlockSpec((tm, D_MODEL), lambda i: (i, 0))
    vec = pl.BlockSpec((1, D_MODEL), lambda i: (0, 0))
    sd = SDS((T, D_MODEL), bf16)
    return _launch(body, name="conv_out_fwd", grid=(T // tm,), in_specs=[row, vec, vec, pl.BlockSpec((D_MODEL, D_MODEL), lambda i: (0, 0))],
                   out_specs=[row, row], out_shape=(sd, sd), args=(c1, g, b, w), sem=("parallel",), comm=comm)


def _conv_out_bwd(dyc, w, c1, g, b):
    T = c1.shape[0]
    tm = 1024

    def body(d_ref, w_ref, c_ref, g_ref, b_ref, dc_ref, dg_ref, db_ref):
        dc3 = lax.dot_general(d_ref[...], w_ref[...], NT, preferred_element_type=f32)
        cv = c_ref[...]
        gv = g_ref[...]
        mu = jnp.mean(cv, axis=-1, keepdims=True)
        cc = cv - mu
        var = jnp.mean(cc * cc, axis=-1, keepdims=True)
        rs = lax.rsqrt(var + LN_EPS)
        xh = cc * rs
        c2 = xh * gv + b_ref[...]
        sg = _sigmoid(c2)
        dc2 = dc3 * (sg * (1.0 + c2 * (1.0 - sg)))
        dxh = dc2 * gv
        dc_ref[...] = rs * (dxh - jnp.mean(dxh, axis=-1, keepdims=True) - xh * jnp.mean(dxh * xh, axis=-1, keepdims=True))
        _accumulate(pl.program_id(0) == 0, [(dg_ref, jnp.sum(dc2 * xh, axis=0, keepdims=True)), (db_ref, jnp.sum(dc2, axis=0, keepdims=True))])

    row = pl.BlockSpec((tm, D_MODEL), lambda i: (i, 0))
    vec = pl.BlockSpec((1, D_MODEL), lambda i: (0, 0))
    return pl.pallas_call(body, grid=(T // tm,), in_specs=[row, pl.BlockSpec((D_MODEL, D_MODEL), lambda i: (0, 0)), row, vec, vec],
                          out_specs=[row, vec, vec],
                          out_shape=(SDS((T, D_MODEL), f32), SDS((1, D_MODEL), f32), SDS((1, D_MODEL), f32)),
                          name="conv_out_bwd", compiler_params=_cp(("arbitrary",)))(dyc, w, c1, g, b)


NN = (((1,), (0,)), ((), ()))
NT = (((1,), (1,)), ((), ()))
TN = (((0,), (0,)), ((), ()))


def _mm(name, a, b, *, grid, a_spec, b_spec, o_spec, o_shape, o_dtype, dims, acc_shape=None, k_axis=None,
        res=None, res_spec=None, sem=None, comm=None, vmem=VMEM_LIMIT):
    nk = 1 if k_axis is None else grid[k_axis]

    def body(*refs):
        if res is None:
            a_ref, b_ref, o_ref = refs[:3]
            r_ref, scr = None, refs[3:]
        else:
            a_ref, b_ref, r_ref, o_ref = refs[:4]
            scr = refs[4:]
        p = lax.dot_general(a_ref[...].astype(bf16), b_ref[...].astype(bf16), dims, preferred_element_type=f32)
        if nk == 1:
            if r_ref is not None:
                p = p + r_ref[...]
            o_ref[...] = p.astype(o_dtype)
            return
        acc = scr[0]
        k = pl.program_id(k_axis)

        @pl.when(k == 0)
        def _():
            acc[...] = p

        @pl.when(k > 0)
        def _():
            acc[...] += p

        @pl.when(k == nk - 1)
        def _():
            t = acc[...]
            if r_ref is not None:
                t = t + r_ref[...]
            o_ref[...] = t.astype(o_dtype)

    ins = [a, b] + ([] if res is None else [res])
    specs = [a_spec, b_spec] + ([] if res is None else [res_spec])
    scratch = [] if nk == 1 else [pltpu.VMEM(acc_shape, f32)]
    return _launch(body, name=name, grid=grid, in_specs=specs, out_specs=o_spec, out_shape=SDS(o_shape, o_dtype),
                   args=ins, scratch_shapes=scratch, sem=sem, comm=comm, vmem=vmem)


def _mm_nn_full(name, a, b, o_dtype, res=None, tm=1024, comm=None):
    T, K = a.shape
    N = b.shape[1]
    return _mm(name, a, b, grid=(T // tm,), a_spec=pl.BlockSpec((tm, K), lambda i: (i, 0)),
               b_spec=pl.BlockSpec((K, N), lambda i: (0, 0)), o_spec=pl.BlockSpec((tm, N), lambda i: (i, 0)),
               o_shape=(T, N), o_dtype=o_dtype, dims=NN, res=res,
               res_spec=None if res is None else pl.BlockSpec((tm, N), lambda i: (i, 0)), sem=("parallel",), comm=comm)


def _mm_nt_full(name, a, b, o_dtype, tm=1024):
    T, N = a.shape
    K = b.shape[0]
    return _mm(name, a, b, grid=(T // tm,), a_spec=pl.BlockSpec((tm, N), lambda i: (i, 0)),
               b_spec=pl.BlockSpec((K, N), lambda i: (0, 0)), o_spec=pl.BlockSpec((tm, K), lambda i: (i, 0)),
               o_shape=(T, K), o_dtype=o_dtype, dims=NT, sem=("parallel",))


def _mm_tn_tokens(name, a, b, o_dtype, tk=1024):
    T, K = a.shape
    N = b.shape[1]
    return _mm(name, a, b, grid=(T // tk,), a_spec=pl.BlockSpec((tk, K), lambda k: (k, 0)),
               b_spec=pl.BlockSpec((tk, N), lambda k: (k, 0)), o_spec=pl.BlockSpec((K, N), lambda k: (0, 0)),
               o_shape=(K, N), o_dtype=o_dtype, dims=TN, acc_shape=(K, N), k_axis=0, sem=("arbitrary",))


AO_S = D_MODEL // N_CHIPS


def _attn_out_cat(w_ref):
    return jnp.concatenate([w_ref[s] for s in range(N_CHIPS)], axis=-1)


def _attn_out_fwd(mix, w4):
    T = mix.shape[0]
    tm = 1024

    def body(a_ref, w_ref, o_ref):
        o_ref[...] = jnp.dot(a_ref[...], _attn_out_cat(w_ref), preferred_element_type=f32).astype(bf16)

    return pl.pallas_call(body, grid=(T // tm,),
                          in_specs=[pl.BlockSpec((tm, GW), lambda i: (i, 0)), pl.BlockSpec((N_CHIPS, GW, AO_S), lambda i: (0, 0, 0))],
                          out_specs=pl.BlockSpec((tm, D_MODEL), lambda i: (i, 0)), out_shape=SDS((T, D_MODEL), bf16), name="attn_out",
                          compiler_params=_cp(("parallel",)))(mix, w4)


def _attn_out_bwd_x(dya, w4, comm=None):
    T = dya.shape[0]
    tm = 1024

    def body(d_ref, w_ref, o_ref):
        o_ref[...] = lax.dot_general(d_ref[...], _attn_out_cat(w_ref), NT, preferred_element_type=f32).astype(bf16)

    return _launch(body, name="d_mix", grid=(T // tm,),
                   in_specs=[pl.BlockSpec((tm, D_MODEL), lambda i: (i, 0)), pl.BlockSpec((N_CHIPS, GW, AO_S), lambda i: (0, 0, 0))],
                   out_specs=pl.BlockSpec((tm, GW), lambda i: (i, 0)), out_shape=SDS((T, GW), bf16), args=(dya, w4),
                   sem=("parallel",), comm=comm)


def _attn_out_bwd_w(mix, dya):
    T = mix.shape[0]
    tk = 1024
    nk = T // tk

    def body(a_ref, d_ref, o_ref, acc):
        k = pl.program_id(0)
        p = lax.dot_general(a_ref[...], d_ref[...], TN, preferred_element_type=f32)

        @pl.when(k == 0)
        def _():
            acc[...] = p

        @pl.when(k > 0)
        def _():
            acc[...] += p

        @pl.when(k == nk - 1)
        def _():
            for s in range(N_CHIPS):
                o_ref[s] = acc[:, s * AO_S:(s + 1) * AO_S].astype(bf16)

    return pl.pallas_call(body, grid=(nk,),
                          in_specs=[pl.BlockSpec((tk, GW), lambda k: (k, 0)), pl.BlockSpec((tk, D_MODEL), lambda k: (k, 0))],
                          out_specs=pl.BlockSpec((N_CHIPS, GW, AO_S), lambda k: (0, 0, 0)), out_shape=SDS((N_CHIPS, GW, AO_S), bf16),
                          scratch_shapes=[pltpu.VMEM((GW, D_MODEL), f32)], name="d_w_attn_out",
                          compiler_params=_cp(("arbitrary",)))(mix, dya)


def _ffn_up(h2, wg, wu):
    T = h2.shape[0]
    tm = 1024

    def body(h_ref, wg_ref, wu_ref, a_ref, b_ref, f_ref):
        hv = h_ref[...]
        av = lax.dot_general(hv, wg_ref[...], NT, preferred_element_type=f32)
        bv = lax.dot_general(hv, wu_ref[...], NT, preferred_element_type=f32)
        a_ref[...] = av.astype(bf16)
        b_ref[...] = bv.astype(bf16)
        f_ref[...] = (av * _sigmoid(av) * bv).astype(bf16)

    wspec = pl.BlockSpec((None, FF_S, D_MODEL), lambda s, m: (s, 0, 0))
    ospec = pl.BlockSpec((None, tm, FF_S), lambda s, m: (s, m, 0))
    osd = SDS((N_CHIPS, T, FF_S), bf16)
    return pl.pallas_call(body, grid=(N_CHIPS, T // tm),
                          in_specs=[pl.BlockSpec((tm, D_MODEL), lambda s, m: (m, 0)), wspec, wspec],
                          out_specs=[ospec, ospec, ospec], out_shape=(osd, osd, osd), name="ffn_up",
                          compiler_params=_cp(("parallel", "parallel")))(h2, wg, wu)


def _ffn_up_bwd_w(da, db, h2):
    T = h2.shape[0]
    tk = 2048
    nk = T // tk

    def body(a_ref, b_ref, h_ref, og_ref, ou_ref, accg, accu):
        k = pl.program_id(1)
        hv = h_ref[...]
        pg = lax.dot_general(a_ref[...], hv, TN, preferred_element_type=f32)
        pu = lax.dot_general(b_ref[...], hv, TN, preferred_element_type=f32)
        _accumulate(k == 0, [(accg, pg), (accu, pu)])

        @pl.when(k == nk - 1)
        def _():
            og_ref[...] = accg[...].astype(bf16)
            ou_ref[...] = accu[...].astype(bf16)

    aspec = pl.BlockSpec((None, tk, FF_S), lambda s, k: (s, k, 0))
    ospec = pl.BlockSpec((None, FF_S, D_MODEL), lambda s, k: (s, 0, 0))
    osd = SDS((N_CHIPS, FF_S, D_MODEL), bf16)
    return pl.pallas_call(body, grid=(N_CHIPS, nk), in_specs=[aspec, aspec, pl.BlockSpec((tk, D_MODEL), lambda s, k: (k, 0))],
                          out_specs=[ospec, ospec], out_shape=(osd, osd),
                          scratch_shapes=[pltpu.VMEM((FF_S, D_MODEL), f32), pltpu.VMEM((FF_S, D_MODEL), f32)], name="d_w_gate_up",
                          compiler_params=_cp(("parallel", "arbitrary")))(da, db, h2)


def _ffn_down_bwd(dx2, wd, a, b):
    T = dx2.shape[0]
    tm = 1024

    def body(d_ref, w_ref, a_ref, b_ref, da_ref, db_ref):
        df = lax.dot_general(d_ref[...].astype(bf16), w_ref[...], NT, preferred_element_type=f32)
        av = a_ref[...].astype(f32)
        sg = _sigmoid(av)
        da_ref[...] = (df * b_ref[...].astype(f32) * (sg * (1.0 + av * (1.0 - sg)))).astype(bf16)
        db_ref[...] = (df * av * sg).astype(bf16)

    aspec = pl.BlockSpec((None, tm, FF_S), lambda m, s: (s, m, 0))
    osd = SDS((N_CHIPS, T, FF_S), bf16)
    return pl.pallas_call(body, grid=(T // tm, N_CHIPS),
                          in_specs=[pl.BlockSpec((tm, D_MODEL), lambda m, s: (m, 0)),
                                    pl.BlockSpec((None, FF_S, D_MODEL), lambda m, s: (s, 0, 0)), aspec, aspec],
                          out_specs=[aspec, aspec], out_shape=(osd, osd), name="ffn_down_bwd",
                          compiler_params=_cp(("parallel", "parallel")))(dx2, wd, a, b)


def _merge_bwd(dm, proj, gate_b, ya, yc):
    T = proj.shape[0]
    tm = 1024

    def body(dm_ref, ga_ref, gc_ref, ba_ref, bc_ref, ya_ref, yc_ref, dya_ref, dyc_ref, dga_ref, dgc_ref, dba_ref, dbc_ref):
        dmv = dm_ref[...].astype(f32)
        sa = _sigmoid(ga_ref[...].astype(f32) + ba_ref[...])
        sc = _sigmoid(gc_ref[...].astype(f32) + bc_ref[...])
        dya_ref[...] = (dmv * sa).astype(bf16)
        dyc_ref[...] = (dmv * sc).astype(bf16)
        dga = dmv * ya_ref[...].astype(f32) * sa * (1.0 - sa)
        dgc = dmv * yc_ref[...].astype(f32) * sc * (1.0 - sc)
        dga_ref[...] = dga.astype(bf16)
        dgc_ref[...] = dgc.astype(bf16)
        pa = jnp.sum(dga, axis=0, keepdims=True)
        pc = jnp.sum(dgc, axis=0, keepdims=True)

        @pl.when(pl.program_id(1) == 0)
        def _():
            dba_ref[...] = pa
            dbc_ref[...] = pc

        @pl.when(pl.program_id(1) > 0)
        def _():
            dba_ref[...] += pa
            dbc_ref[...] += pc

    blk = pl.BlockSpec((tm, CB), lambda j, i: (i, j))
    vec = pl.BlockSpec((1, CB), lambda j, i: (0, j))
    big = SDS((T, D_MODEL), bf16)
    small = SDS((1, D_MODEL), f32)
    return pl.pallas_call(
        body, grid=(2, T // tm),
        in_specs=[blk, pl.BlockSpec((tm, CB), lambda j, i: (i, GA_CB + j)), pl.BlockSpec((tm, CB), lambda j, i: (i, GC_CB + j)),
                  vec, pl.BlockSpec((1, CB), lambda j, i: (0, 2 + j)), blk, blk],
        out_specs=[blk, blk, blk, blk, vec, vec], out_shape=(big, big, big, big, small, small), name="merge_bwd",
        compiler_params=_cp(("parallel", "arbitrary")))(dm, proj, proj, gate_b, gate_b, ya, yc)


CONV_TS = 512
CONV_HALO = 32
CONV_RC = 64
CONV_WIN = CONV_TS + CONV_HALO
SUBLANES = 8


def _fill_shifted(win, sh):
    for b in range(1, SUBLANES):
        sh[b - 1] = win[pl.ds(b, CONV_WIN - SUBLANES), :]


def _rows_at(win, sh, row):
    a, b = divmod(row, SUBLANES)
    if b == 0:
        return win[pl.ds(row, CONV_RC), :]
    return sh[b - 1, pl.ds(a * SUBLANES, CONV_RC), :]


def _glu_conv_fwd(proj3, w, bias, comm=None):
    B = proj3.shape[0]
    nt = SEQ // CONV_TS
    hb = CONV_TS // CONV_HALO

    def body(ua_ref, ub_ref, ha_ref, hb_ref, w_ref, b_ref, o_ref, win, sh):
        i = pl.program_id(2)
        c0 = ua_ref[...].astype(f32) * _sigmoid(ub_ref[...].astype(f32))
        halo = ha_ref[...].astype(f32) * _sigmoid(hb_ref[...].astype(f32))
        win[0:CONV_HALO, :] = jnp.where(i > 0, halo, 0.0)
        win[CONV_HALO:, :] = c0
        _fill_shifted(win, sh)
        for r0 in range(0, CONV_TS, CONV_RC):
            acc = jnp.zeros((CONV_RC, CB), f32) + b_ref[...]
            for k in range(CONV_K):
                acc = acc + _rows_at(win, sh, r0 + CONV_HALO - (CONV_K - 1) + k) * w_ref[k:k + 1, :]
            o_ref[r0:r0 + CONV_RC, :] = acc

    def cur(cb):
        return pl.BlockSpec((None, CONV_TS, CB), lambda b, j, i: (b, i, cb + j))

    def prev(cb):
        return pl.BlockSpec((None, CONV_HALO, CB), lambda b, j, i: (b, jnp.maximum(i * hb - 1, 0), cb + j))

    return _launch(
        body, name="glu_conv_fwd", grid=(B, 2, nt),
        in_specs=[cur(UA_CB), cur(UB_CB), prev(UA_CB), prev(UB_CB),
                  pl.BlockSpec((CONV_K, CB), lambda b, j, i: (0, j)), pl.BlockSpec((1, CB), lambda b, j, i: (0, j))],
        out_specs=pl.BlockSpec((None, CONV_TS, CB), lambda b, j, i: (b, i, j)),
        out_shape=SDS((B, SEQ, D_MODEL), f32), args=(proj3, proj3, proj3, proj3, w, bias),
        scratch_shapes=[pltpu.VMEM((CONV_WIN, CB), f32), pltpu.VMEM((SUBLANES - 1, CONV_WIN - SUBLANES, CB), f32)],
        sem=("parallel", "parallel", "parallel"), comm=comm)


def _glu_conv_bwd(dc1, proj3, w, comm=None):
    B = proj3.shape[0]
    nt = SEQ // CONV_TS
    hb = CONV_TS // CONV_HALO

    def body(d_ref, dn_ref, ua_ref, ub_ref, ha_ref, hb_ref, w_ref, dua_ref, dub_ref, dw_ref, db_ref, winc, wind, accw, shc, shd):
        b = pl.program_id(1)
        i = pl.program_id(2)
        first = jnp.logical_and(b == 0, i == 0)
        last = jnp.logical_and(b == B - 1, i == nt - 1)

        @pl.when(first)
        def _():
            accw[...] = jnp.zeros_like(accw)
            db_ref[...] = jnp.zeros_like(db_ref)

        halo = ha_ref[...].astype(f32) * _sigmoid(hb_ref[...].astype(f32))
        winc[0:CONV_HALO, :] = jnp.where(i > 0, halo, 0.0)
        winc[CONV_HALO:, :] = ua_ref[...].astype(f32) * _sigmoid(ub_ref[...].astype(f32))
        wind[0:CONV_TS, :] = d_ref[...]
        wind[CONV_TS:, :] = jnp.where(i < nt - 1, dn_ref[...], 0.0)
        db_ref[...] += jnp.sum(d_ref[...], axis=0, keepdims=True)
        _fill_shifted(winc, shc)
        _fill_shifted(wind, shd)
        for r0 in range(0, CONV_TS, CONV_RC):
            dc0 = jnp.zeros((CONV_RC, CB), f32)
            for k in range(CONV_K):
                dc0 = dc0 + _rows_at(wind, shd, r0 + (CONV_K - 1) - k) * w_ref[k:k + 1, :]
            uav = ua_ref[r0:r0 + CONV_RC, :].astype(f32)
            sg = _sigmoid(ub_ref[r0:r0 + CONV_RC, :].astype(f32))
            dua_ref[r0:r0 + CONV_RC, :] = (dc0 * sg).astype(bf16)
            dub_ref[r0:r0 + CONV_RC, :] = (dc0 * uav * sg * (1.0 - sg)).astype(bf16)
            dv = wind[r0:r0 + CONV_RC, :]
            for k in range(CONV_K):
                prod = dv * _rows_at(winc, shc, r0 + CONV_HALO - (CONV_K - 1) + k)
                accw[k] += jnp.sum(prod.reshape(CONV_RC // 8, 8, CB), axis=0)

        @pl.when(last)
        def _():
            for k in range(CONV_K):
                dw_ref[k:k + 1, :] = jnp.sum(accw[k], axis=0, keepdims=True)
            dw_ref[CONV_K:, :] = jnp.zeros((CONV_HALO - CONV_K, CB), f32)

    def cur(cb):
        return pl.BlockSpec((None, CONV_TS, CB), lambda j, b, i: (b, i, cb + j))

    def prev(cb):
        return pl.BlockSpec((None, CONV_HALO, CB), lambda j, b, i: (b, jnp.maximum(i * hb - 1, 0), cb + j))

    nxt = pl.BlockSpec((None, CONV_HALO, CB), lambda j, b, i: (b, jnp.minimum((i + 1) * hb, SEQ // CONV_HALO - 1), j))
    big = SDS((B, SEQ, D_MODEL), bf16)
    return _launch(
        body, name="glu_conv_bwd", grid=(2, B, nt),
        in_specs=[cur(0), nxt, cur(UA_CB), cur(UB_CB), prev(UA_CB), prev(UB_CB), pl.BlockSpec((CONV_K, CB), lambda j, b, i: (0, j))],
        out_specs=[cur(0), cur(0), pl.BlockSpec((CONV_HALO, CB), lambda j, b, i: (0, j)), pl.BlockSpec((1, CB), lambda j, b, i: (0, j))],
        out_shape=(big, big, SDS((CONV_HALO, D_MODEL), f32), SDS((1, D_MODEL), f32)),
        args=(dc1, dc1, proj3, proj3, proj3, proj3, w),
        scratch_shapes=[pltpu.VMEM((CONV_WIN, CB), f32), pltpu.VMEM((CONV_WIN, CB), f32), pltpu.VMEM((CONV_K, SUBLANES, CB), f32),
                        pltpu.VMEM((SUBLANES - 1, CONV_WIN - SUBLANES, CB), f32),
                        pltpu.VMEM((SUBLANES - 1, CONV_WIN - SUBLANES, CB), f32)],
        sem=("parallel", "arbitrary", "arbitrary"), comm=comm)


def _band(first, dil):
    kw = Q_BLOCK if first else 2 * Q_BLOCK
    qi = lax.broadcasted_iota(jnp.int32, (Q_BLOCK, kw), 0)
    kj = lax.broadcasted_iota(jnp.int32, (Q_BLOCK, kw), 1)
    rel = qi - kj + (0 if first else Q_BLOCK)
    valid = jnp.logical_and(rel >= 0, rel <= Q_BLOCK)
    return valid, rel.astype(f32) * float(dil)


def _bias(first, dil, slope):
    valid, dist = _band(first, dil)
    return jnp.where(valid, -slope * dist, NEG)


def _scores(q, k, bias):
    return lax.dot_general(q, k, NT, preferred_element_type=f32) * (HEAD_DIM ** -0.5) + bias


def _pair_cols(hp):
    return slice(hp * LANES, (hp + 1) * LANES)


def _half(x2, e):
    lane = lax.broadcasted_iota(jnp.int32, (1, LANES), 1)
    keep = (lane < HEAD_DIM) if e == 0 else (lane >= HEAD_DIM)
    return jnp.where(keep, x2, jnp.zeros_like(x2))


def _attn_fwd(g, q4, k4, v4, qcb, kcb, vcb, comm=None):
    _, dil = GROUPS[g]
    B, r, L, _ = q4.shape
    nb = L // Q_BLOCK
    slopes = [float(s) for s in _SLOPES[g]]

    def body(q_ref, k_ref, v_ref, o_ref, lse_ref, bias, s_scr, p_scr):
        lane = lax.broadcasted_iota(jnp.int32, (Q_BLOCK, LANES), 1)
        if nb > 1:
            for h in range(HEADS):
                bias[h] = _bias(False, dil, slopes[h])

        def block(n, first):
            q0 = 0 if first else pl.multiple_of(n * Q_BLOCK, Q_BLOCK)
            k0 = 0 if first else pl.multiple_of((n - 1) * Q_BLOCK, Q_BLOCK)
            kw = Q_BLOCK if first else 2 * Q_BLOCK
            for hp in range(HEADS // 2):
                q2 = q_ref[pl.ds(q0, Q_BLOCK), _pair_cols(hp)]
                k2 = k_ref[pl.ds(k0, kw), _pair_cols(hp)]
                for e in range(2):
                    h = 2 * hp + e
                    b_h = _bias(True, dil, slopes[h]) if first else bias[h]
                    s_scr[h, :, :kw] = _scores(_half(q2, e), k2, b_h)
            st = jnp.zeros((Q_BLOCK, LANES), f32)
            dens = jnp.ones((Q_BLOCK, LANES), f32)
            for h in range(HEADS):
                s = s_scr[h, :, :kw]
                m = jnp.max(s, axis=-1, keepdims=True)
                p = jnp.exp(s - m)
                den = jnp.sum(p, axis=-1, keepdims=True)
                p_scr[h, :, :kw] = p.astype(bf16)
                st = jnp.where(lane == h, m + jnp.log(den), st)
                dens = jnp.where(lane == h, den, dens)
            lse_ref[pl.ds(q0, Q_BLOCK), :] = st
            inv = 1.0 / dens
            for hp in range(HEADS // 2):
                v2 = v_ref[pl.ds(k0, kw), _pair_cols(hp)]
                o2 = sum(jnp.dot(p_scr[2 * hp + e, :, :kw], _half(v2, e), preferred_element_type=f32) * inv[:, 2 * hp + e:2 * hp + e + 1]
                         for e in range(2))
                o_ref[pl.ds(q0, Q_BLOCK), _pair_cols(hp)] = o2.astype(bf16)

        block(0, True)
        if nb > 1:
            def step(n, carry):
                block(n, False)
                return carry
            lax.fori_loop(1, nb, step, 0)

    def spec(cb):
        return pl.BlockSpec((None, None, L, GW), lambda b, c: (b, c, 0, cb))

    return _launch(
        body, name=f"attn_fwd_g{g}", grid=(B, r), in_specs=[spec(qcb), spec(kcb), spec(vcb)],
        out_specs=[spec(0), pl.BlockSpec((None, None, L, LANES), lambda b, c: (b, c, 0, 0))],
        out_shape=(SDS((B, r, L, GW), bf16), SDS((B, r, L, LANES), f32)), args=(q4, k4, v4),
        scratch_shapes=[pltpu.VMEM((HEADS, Q_BLOCK, 2 * Q_BLOCK), f32), pltpu.VMEM((HEADS, Q_BLOCK, 2 * Q_BLOCK), f32),
                        pltpu.VMEM((HEADS, Q_BLOCK, 2 * Q_BLOCK), bf16)],
        sem=("parallel", "parallel"), comm=comm)


def _attn_bwd(g, q4, k4, v4, qcb, kcb, vcb, do4, lse4, dl4, comm=None):
    _, dil = GROUPS[g]
    B, r, L, _ = q4.shape
    nb = L // Q_BLOCK
    slopes = [float(s) for s in _SLOPES[g]]
    scale = HEAD_DIM ** -0.5

    def body(q_ref, k_ref, v_ref, do_ref, lse_ref, dl_ref, dq_ref, dk_ref, dv_ref, dk_acc, dv_acc, bias, s_scr, dp_scr, p_scr, ds_scr):
        dk_acc[...] = jnp.zeros_like(dk_acc)
        dv_acc[...] = jnp.zeros_like(dv_acc)
        if nb > 1:
            for h in range(HEADS):
                bias[h] = _bias(False, dil, slopes[h])

        def block(n, first):
            q0 = 0 if first else pl.multiple_of(n * Q_BLOCK, Q_BLOCK)
            k0 = 0 if first else pl.multiple_of((n - 1) * Q_BLOCK, Q_BLOCK)
            kw = Q_BLOCK if first else 2 * Q_BLOCK
            for hp in range(HEADS // 2):
                q2 = q_ref[pl.ds(q0, Q_BLOCK), _pair_cols(hp)]
                k2 = k_ref[pl.ds(k0, kw), _pair_cols(hp)]
                v2 = v_ref[pl.ds(k0, kw), _pair_cols(hp)]
                do2 = do_ref[pl.ds(q0, Q_BLOCK), _pair_cols(hp)]
                for e in range(2):
                    h = 2 * hp + e
                    b_h = _bias(True, dil, slopes[h]) if first else bias[h]
                    s_scr[h, :, :kw] = _scores(_half(q2, e), k2, b_h)
                    dp_scr[h, :, :kw] = lax.dot_general(_half(do2, e), v2, NT, preferred_element_type=f32)
            for h in range(HEADS):
                p = jnp.exp(s_scr[h, :, :kw] - lse_ref[pl.ds(q0, Q_BLOCK), h:h + 1])
                p_scr[h, :, :kw] = p.astype(bf16)
                ds_scr[h, :, :kw] = (p * (dp_scr[h, :, :kw] - dl_ref[pl.ds(q0, Q_BLOCK), h:h + 1])).astype(bf16)
            for hp in range(HEADS // 2):
                cols = _pair_cols(hp)
                q2 = q_ref[pl.ds(q0, Q_BLOCK), cols]
                k2 = k_ref[pl.ds(k0, kw), cols]
                do2 = do_ref[pl.ds(q0, Q_BLOCK), cols]
                ds = [ds_scr[2 * hp + e, :, :kw] for e in range(2)]
                dq2 = sum(jnp.dot(ds[e], _half(k2, e), preferred_element_type=f32) for e in range(2))
                dq_ref[pl.ds(q0, Q_BLOCK), cols] = (dq2 * scale).astype(bf16)
                dk2 = sum(lax.dot_general(ds[e], _half(q2, e), TN, preferred_element_type=f32) for e in range(2))
                dk_acc[pl.ds(k0, kw), cols] += dk2 * scale
                dv2 = sum(lax.dot_general(p_scr[2 * hp + e, :, :kw], _half(do2, e), TN, preferred_element_type=f32) for e in range(2))
                dv_acc[pl.ds(k0, kw), cols] += dv2

        block(0, True)
        if nb > 1:
            def step(n, carry):
                block(n, False)
                return carry
            lax.fori_loop(1, nb, step, 0)
        dk_ref[...] = dk_acc[...].astype(bf16)
        dv_ref[...] = dv_acc[...].astype(bf16)

    def spec(cb):
        return pl.BlockSpec((None, None, L, GW), lambda b, c: (b, c, 0, cb))

    st = pl.BlockSpec((None, None, L, LANES), lambda b, c: (b, c, 0, 0))
    osd = SDS((B, r, L, GW), bf16)
    return _launch(
        body, name=f"attn_bwd_g{g}", grid=(B, r), in_specs=[spec(qcb), spec(kcb), spec(vcb), spec(0), st, st],
        out_specs=[spec(0), spec(0), spec(0)], out_shape=(osd, osd, osd), args=(q4, k4, v4, do4, lse4, dl4),
        scratch_shapes=[pltpu.VMEM((L, GW), f32), pltpu.VMEM((L, GW), f32)]
        + [pltpu.VMEM((HEADS, Q_BLOCK, 2 * Q_BLOCK), f32)] * 3 + [pltpu.VMEM((HEADS, Q_BLOCK, 2 * Q_BLOCK), bf16)] * 2,
        sem=("parallel", "parallel"), comm=comm)


RT = 512
RCH = GW // LANES
DILS = tuple(d for _, d in GROUPS[1:])


def _res_spec(r, width):
    return pl.BlockSpec((None, r, RT // r, width), lambda b, i, *_: (b, 0, i, 0))


def _tok_spec(width, cb=0):
    return pl.BlockSpec((None, RT, width), lambda b, i, *_: (b, i, cb))


def _to_residues(res_ref, scr, r, width):
    for c in range(r):
        for jj in range(width // LANES):
            res_ref[c, :, jj * LANES:(jj + 1) * LANES] = scr[jj, pl.ds(c, RT // r, stride=r), :].astype(res_ref.dtype)


def _from_residues(scr, res_ref, r, width):
    for c in range(r):
        for jj in range(width // LANES):
            scr[jj, pl.ds(c, RT // r, stride=r), :] = res_ref[c, :, jj * LANES:(jj + 1) * LANES].astype(f32)


def _qkv_to_residues(proj3, g):
    r = GROUPS[g][1]
    B = proj3.shape[0]

    def body(q_ref, k_ref, v_ref, o_ref, scr):
        for p, x_ref in enumerate((q_ref, k_ref, v_ref)):
            x = x_ref[...].astype(f32)
            for jj in range(RCH):
                scr[p * RCH + jj] = x[:, jj * LANES:(jj + 1) * LANES]
        _to_residues(o_ref, scr, r, ATTN_W)

    return pl.pallas_call(
        body, grid=(B, SEQ // RT), in_specs=[_tok_spec(GW, 3 * p + g) for p in range(3)], out_specs=_res_spec(r, ATTN_W),
        out_shape=SDS((B, r, SEQ // r, ATTN_W), bf16), scratch_shapes=[pltpu.VMEM((3 * RCH, RT, LANES), f32)],
        name=f"qkv_to_residues_g{g}", compiler_params=_cp(("parallel", "parallel")))(proj3, proj3, proj3)


def _attn_mix(o0, l0, o_res, l_res):
    B = o0.shape[0]

    def body(o0_ref, l0_ref, o1_ref, o2_ref, l1_ref, l2_ref, y_ref, lt_ref, lt1_ref, lt2_ref, so, sl):
        for gi, (o_ref, l_ref, r) in enumerate(((o1_ref, l1_ref, DILS[0]), (o2_ref, l2_ref, DILS[1]))):
            _from_residues(so.at[gi], o_ref, r, GW)
            _from_residues(sl.at[gi:gi + 1], l_ref, r, LANES)
        ls = [l0_ref[...], sl[0], sl[1]]
        m = functools.reduce(jnp.maximum, ls)
        ws = [jnp.exp(l - m) for l in ls]
        den = ws[0] + ws[1] + ws[2]
        alphas = [w / den for w in ws]
        lt = m + jnp.log(den)
        lt_ref[...] = lt
        sl[2] = lt
        _to_residues(lt1_ref, sl.at[2:3], DILS[0], LANES)
        _to_residues(lt2_ref, sl.at[2:3], DILS[1], LANES)
        for h in range(HEADS):
            cols = slice(h * HEAD_DIM, (h + 1) * HEAD_DIM)
            jj, lo = divmod(h * HEAD_DIM, LANES)
            acc = alphas[0][:, h:h + 1] * o0_ref[:, cols].astype(f32)
            for gi in range(2):
                acc = acc + alphas[gi + 1][:, h:h + 1] * so[gi, jj, :, lo:lo + HEAD_DIM]
            y_ref[:, cols] = acc.astype(bf16)

    in_specs = [_tok_spec(GW), _tok_spec(LANES), _res_spec(DILS[0], GW), _res_spec(DILS[1], GW), _res_spec(DILS[0], LANES), _res_spec(DILS[1], LANES)]
    out_specs = [_tok_spec(GW), _tok_spec(LANES), _res_spec(DILS[0], LANES), _res_spec(DILS[1], LANES)]
    return pl.pallas_call(
        body, grid=(B, SEQ // RT), in_specs=in_specs, out_specs=out_specs,
        out_shape=(SDS((B, SEQ, GW), bf16), SDS((B, SEQ, LANES), f32)) + tuple(SDS((B, r, SEQ // r, LANES), f32) for r in DILS),
        scratch_shapes=[pltpu.VMEM((2, RCH, RT, LANES), f32), pltpu.VMEM((3, RT, LANES), f32)],
        name="attn_mix", compiler_params=_cp(("parallel", "parallel")))(o0, l0, *o_res, *l_res)


def _attn_delta(dmix, mix):
    B = dmix.shape[0]

    def body(d_ref, y_ref, dl_ref, dl1_ref, dl2_ref, dm1_ref, dm2_ref, sx, sd):
        lane = lax.broadcasted_iota(jnp.int32, (RT, LANES), 1)
        acc = jnp.zeros((RT, LANES), f32)
        dv = d_ref[...].astype(f32)
        for jj in range(RCH):
            sx[jj] = dv[:, jj * LANES:(jj + 1) * LANES]
        for h in range(HEADS):
            cols = slice(h * HEAD_DIM, (h + 1) * HEAD_DIM)
            dl = jnp.sum(dv[:, cols] * y_ref[:, cols].astype(f32), axis=-1, keepdims=True)
            acc = jnp.where(lane == h, dl, acc)
        dl_ref[...] = acc
        sd[0] = acc
        _to_residues(dl1_ref, sd, DILS[0], LANES)
        _to_residues(dl2_ref, sd, DILS[1], LANES)
        _to_residues(dm1_ref, sx, DILS[0], GW)
        _to_residues(dm2_ref, sx, DILS[1], GW)

    return pl.pallas_call(
        body, grid=(B, SEQ // RT), in_specs=[_tok_spec(GW), _tok_spec(GW)],
        out_specs=[_tok_spec(LANES), _res_spec(DILS[0], LANES), _res_spec(DILS[1], LANES), _res_spec(DILS[0], GW), _res_spec(DILS[1], GW)],
        out_shape=(SDS((B, SEQ, LANES), f32),) + tuple(SDS((B, r, SEQ // r, LANES), f32) for r in DILS)
        + tuple(SDS((B, r, SEQ // r, GW), bf16) for r in DILS),
        scratch_shapes=[pltpu.VMEM((RCH, RT, LANES), f32), pltpu.VMEM((1, RT, LANES), f32)],
        name="attn_delta", compiler_params=_cp(("parallel", "parallel")))(dmix, mix)


N_CB = IN_W // CB


def _assemble_dproj(dqkv, dua, dub, dga, dgc):
    B = dua.shape[0]
    ng = len(GROUPS)
    flat = [dqkv[g][p] for p in range(3) for g in range(ng)]
    wide = [dua, dub, dga, dgc]

    def body(*refs):
        srcs, wides, o_ref, scr = refs[:3 * ng], refs[3 * ng:3 * ng + 4], refs[3 * ng + 4], refs[3 * ng + 5]
        for jv in range(3 * ng):
            g = jv % ng
            if g == 0:
                o_ref[:, jv * GW:(jv + 1) * GW] = srcs[jv][...]
            else:
                _from_residues(scr, srcs[jv], GROUPS[g][1], GW)
                for jj in range(RCH):
                    o_ref[:, jv * GW + jj * LANES:jv * GW + (jj + 1) * LANES] = scr[jj].astype(bf16)
        for wv in range(4):
            lo = 3 * ATTN_W + wv * D_MODEL
            o_ref[:, lo:lo + D_MODEL] = wides[wv][...]

    in_specs = [_tok_spec(GW) if (jv % ng) == 0 else _res_spec(GROUPS[jv % ng][1], GW) for jv in range(3 * ng)]
    in_specs += [_tok_spec(D_MODEL)] * 4
    return pl.pallas_call(
        body, grid=(B, SEQ // RT), in_specs=in_specs, out_specs=_tok_spec(IN_W),
        out_shape=SDS((B, SEQ, IN_W), bf16), scratch_shapes=[pltpu.VMEM((RCH, RT, LANES), f32)],
        name="assemble_dproj", compiler_params=_cp(("parallel", "parallel")))(*flat, *wide)


def _same_shape_groups(names, arrays):
    groups = {}
    for n, a in zip(names, arrays):
        groups.setdefault(a.shape, ([], []))
        groups[a.shape][0].append(n)
        groups[a.shape][1].append(a)
    return list(groups.values())


def _add_pairs(parts, gots, core, name):
    k = len(parts)
    n, h, C = gots[0].shape

    def body(c_ref, *refs):
        for i in range(k):
            refs[2 * k + i][...] = (refs[i][...].astype(f32) + refs[k + i][...].astype(f32)).astype(bf16)

    blk = pl.BlockSpec((None, h, C), lambda s, c_ref: (s, 0, 0))
    own = pl.BlockSpec((None, h, C), lambda s, c_ref: (s, c_ref[0], 0))
    spec = pltpu.PrefetchScalarGridSpec(num_scalar_prefetch=1, grid=(n,), in_specs=[own] * k + [blk] * k, out_specs=[blk] * k)
    return pl.pallas_call(body, grid_spec=spec, out_shape=tuple(SDS((n, h, C), bf16) for _ in range(k)), name=name,
                          compiler_params=_cp(("parallel",)))(core, *parts, *gots)


def _sum4(sums, gots, chip, core, name):
    k = len(sums)
    _, h, C = sums[0].shape

    def body(s_ref, c_ref, *refs):
        for i in range(k):
            q_ref = refs[k + i]
            t = refs[i][...].astype(f32) + q_ref[0].astype(f32)
            t = t + q_ref[1].astype(f32)
            refs[2 * k + i][...] = t + q_ref[2].astype(f32)

    spec = pltpu.PrefetchScalarGridSpec(
        num_scalar_prefetch=2, grid=(1,),
        in_specs=[pl.BlockSpec((None, h, C), lambda i, s_ref, c_ref: (s_ref[0], 0, 0))] * k
        + [pl.BlockSpec((N_CHIPS - 1, h, C), lambda i, s_ref, c_ref: (0, 0, 0))] * k,
        out_specs=[pl.BlockSpec((h, C), lambda i, s_ref, c_ref: (c_ref[0], 0))] * k)
    return pl.pallas_call(body, grid_spec=spec, out_shape=tuple(SDS((2 * h, C), f32) for _ in range(k)), name=name,
                          compiler_params=_cp(("arbitrary",)))(chip, core, *sums, *gots)


def _adamw(ws, gs, ms, vs, name):
    k = len(ws)
    R, C = ws[0].shape
    rt = R
    for cand in (512, 256, 128, 64, 32, 16, 8):
        if R % cand == 0 and cand * C * 4 <= 2 ** 21:
            rt = cand
            break
    c1 = 1.0 / (1.0 - B1 ** STEP)
    c2 = 1.0 / (1.0 - B2 ** STEP)

    def body(*refs):
        for i in range(k):
            w_ref, g_ref, m_ref, v_ref = (refs[j * k + i] for j in range(4))
            d_ref, nm_ref, nv_ref = (refs[(4 + j) * k + i] for j in range(3))
            gv = g_ref[...]
            nm = B1 * m_ref[...] + (1.0 - B1) * gv
            nv = B2 * v_ref[...] + (1.0 - B2) * (gv * gv)
            nm_ref[...] = nm
            nv_ref[...] = nv
            d_ref[...] = -LR * ((nm * c1) / (jnp.sqrt(nv * c2) + ADAM_EPS) + WD * w_ref[...])

    blk = pl.BlockSpec((rt, C), lambda i: (i, 0))
    sd = SDS((R, C), f32)
    res = pl.pallas_call(body, grid=(R // rt,), in_specs=[blk] * (4 * k), out_specs=[blk] * (3 * k), out_shape=(sd,) * (3 * k),
                         name=name, compiler_params=_cp(("parallel",)))(*ws, *gs, *ms, *vs)
    return res[:k], res[k:2 * k], res[2 * k:]


def _coords():
    return lax.axis_index("x"), lax.axis_index("y"), lax.axis_index("c")


def _other_chips(x, y):
    return [(1 - x, y), (x, 1 - y), (1 - x, 1 - y)]


def _allgather_weights(shards, slots):
    n = len(shards)
    n_slot = 12

    def body(*refs):
        ins, outs = refs[:n], refs[2 * n:3 * n]
        send, recv = refs[3 * n:]
        x, y, c = _coords()
        me, cx, cy, cd = 2 * x + y, 2 * (1 - x) + y, 2 * x + (1 - y), 2 * (1 - x) + (1 - y)
        dev_x, dev_y, dev_s = (1 - x, y, c), (x, 1 - y, c), (x, y, 1 - c)
        started = []

        def copy(w, slot, src, dst, dev):
            return pltpu.make_async_remote_copy(src_ref=src, dst_ref=dst, send_sem=send.at[w, slot], recv_sem=recv.at[w, slot],
                                                device_id=dev, device_id_type=MESH)

        def go(cp):
            cp.start()
            started.append(cp)

        for w in range(n):
            q = shards[w].shape[0] // 4
            rows = [pl.ds(c * 2 * q + k * q, q) for k in range(2)]
            theirs = [pl.ds((1 - c) * 2 * q + k * q, q) for k in range(2)]
            own = [(ins[w].at[r, :], outs[w].at[me, r, :]) for r in rows]
            go(copy(w, 0, *own[0], dev_x))
            go(copy(w, 2, *own[1], dev_y))
            go(copy(w, 1, *own[1], dev_x))
            go(copy(w, 3, *own[0], dev_y))
            arrivals = [(0, cx, 0, (4, dev_y)), (2, cy, 1, (5, dev_x)), (1, cx, 1, None), (3, cy, 0, None), (4, cd, 0, None), (5, cd, 1, None)]
            for k, (slot, chip, quarter, onward) in enumerate(arrivals):
                blk = outs[w].at[chip, rows[quarter], :]
                copy(w, slot, blk, blk, dev_s).wait_recv()
                if onward is not None:
                    go(copy(w, onward[0], blk, blk, onward[1]))
                go(copy(w, 6 + k, blk, blk, dev_s))
            for k, (slot, chip, quarter, onward) in enumerate(arrivals):
                blk = outs[w].at[chip, theirs[quarter], :]
                copy(w, 6 + k, blk, blk, dev_s).wait_recv()
        for cp in started:
            cp.wait_send()

    return pl.pallas_call(
        body, in_specs=[HBM] * (2 * n), out_specs=[HBM] * n,
        out_shape=tuple(SDS((N_CHIPS,) + s.shape, s.dtype) for s in shards),
        input_output_aliases={n + w: w for w in range(n)},
        scratch_shapes=[pltpu.SemaphoreType.DMA((n, n_slot))] * 2,
        name="allgather_weights", compiler_params=pltpu.CompilerParams(has_side_effects=True))(*shards, *slots)


def _x_pair_exchange(parts):
    n = len(parts)
    halves = [p.shape[1] // 2 for p in parts]

    def copies(ins, outs, sems):
        send, recv = sems
        x, y, c = _coords()
        return [pltpu.make_async_remote_copy(src_ref=ins[w].at[:, pl.ds((1 - c) * halves[w], halves[w]), :], dst_ref=outs[w],
                                             send_sem=send.at[w], recv_sem=recv.at[w], device_id=(x, y, 1 - c), device_id_type=MESH)
                for w in range(n)]

    def start(ins, outs, sems):
        for cp in copies(ins, outs, sems):
            cp.start()

    def wait(ins, outs, sems):
        for cp in copies(ins, outs, sems):
            cp.wait()

    return _Exchange(ins=list(parts), out_shape=[SDS((N_CHIPS, p.shape[1] // 2, p.shape[2]), p.dtype) for p in parts],
                     scratch=[pltpu.SemaphoreType.DMA((n,))] * 2, aliases={}, start=start, wait=wait)


def _x_chip_exchange(sums):
    n = len(sums)

    def copies(ins, outs, sems):
        send, recv = sems
        x, y, c = _coords()
        return [pltpu.make_async_remote_copy(src_ref=ins[w].at[2 * px + py], dst_ref=outs[w].at[j], send_sem=send.at[w, j],
                                             recv_sem=recv.at[w, j], device_id=(px, py, c), device_id_type=MESH)
                for w in range(n) for j, (px, py) in enumerate(_other_chips(x, y))]

    def start(ins, outs, sems):
        for cp in copies(ins, outs, sems):
            cp.start()

    def wait(ins, outs, sems):
        for cp in copies(ins, outs, sems):
            cp.wait()

    return _Exchange(ins=list(sums), out_shape=[SDS((N_CHIPS - 1,) + s.shape[1:], s.dtype) for s in sums],
                     scratch=[pltpu.SemaphoreType.DMA((n, 3)), pltpu.SemaphoreType.DMA((n, 3))], aliases={}, start=start, wait=wait)


def _x_gather_ici(shards, slots):
    n = len(shards)
    halves = [s.shape[0] // 2 for s in shards]

    def copies(ins, outs, sems):
        send, recv = sems
        x, y, c = _coords()
        me = 2 * x + y
        out = []
        for w in range(n):
            mine = pl.ds(c * halves[w], halves[w])
            for j, (px, py) in enumerate(_other_chips(x, y)):
                snd = pltpu.make_async_remote_copy(src_ref=ins[w].at[mine, :], dst_ref=outs[w].at[me, mine, :], send_sem=send.at[w, j],
                                                   recv_sem=recv.at[w, j], device_id=(px, py, c), device_id_type=MESH)
                got = outs[w].at[2 * px + py, mine, :]
                rcv = pltpu.make_async_remote_copy(src_ref=got, dst_ref=got, send_sem=send.at[w, j], recv_sem=recv.at[w, j],
                                                   device_id=(px, py, c), device_id_type=MESH)
                out.append((snd, rcv))
        return out

    def start(ins, outs, sems):
        for snd, _ in copies(ins, outs, sems):
            snd.start()

    def wait(ins, outs, sems):
        for snd, rcv in copies(ins, outs, sems):
            rcv.wait_recv()
            snd.wait_send()

    return _Exchange(ins=list(shards) + list(slots), out_shape=[SDS(s.shape, s.dtype) for s in slots],
                     scratch=[pltpu.SemaphoreType.DMA((n, 3)), pltpu.SemaphoreType.DMA((n, 3))],
                     aliases={n + w: w for w in range(n)}, start=start, wait=wait)


def _x_gather_fwd(bufs):
    n = len(bufs)
    halves = [b.shape[1] // 2 for b in bufs]

    def copies(ins, outs, sems):
        send, recv = sems
        x, y, c = _coords()
        out = []
        for w in range(n):
            for j, (px, py) in enumerate(_other_chips(x, y)):
                mine = outs[w].at[2 * px + py, pl.ds(c * halves[w], halves[w]), :]
                theirs = outs[w].at[2 * px + py, pl.ds((1 - c) * halves[w], halves[w]), :]
                snd = pltpu.make_async_remote_copy(src_ref=mine, dst_ref=mine, send_sem=send.at[w, j], recv_sem=recv.at[w, j],
                                                   device_id=(x, y, 1 - c), device_id_type=MESH)
                rcv = pltpu.make_async_remote_copy(src_ref=theirs, dst_ref=theirs, send_sem=send.at[w, j], recv_sem=recv.at[w, j],
                                                   device_id=(x, y, 1 - c), device_id_type=MESH)
                out.append((snd, rcv))
        return out

    def start(ins, outs, sems):
        for snd, _ in copies(ins, outs, sems):
            snd.start()

    def wait(ins, outs, sems):
        for snd, rcv in copies(ins, outs, sems):
            rcv.wait_recv()
            snd.wait_send()

    return _Exchange(ins=list(bufs), out_shape=[SDS(b.shape, b.dtype) for b in bufs],
                     scratch=[pltpu.SemaphoreType.DMA((n, 3)), pltpu.SemaphoreType.DMA((n, 3))],
                     aliases={w: w for w in range(n)}, start=start, wait=wait)


def _final_exchange(bufs, vec):
    n = len(bufs)
    R = vec.shape[0]
    nd = 8

    def body(*refs):
        v_ref = refs[n]
        outs = refs[n + 1:2 * n + 1]
        o_ref, sbuf, send, recv, ssend, srecv = refs[2 * n + 1:]
        x, y, c = _coords()
        me = 4 * x + 2 * y + c
        sbuf[me] = v_ref[...]
        small = []
        for k in range(1, nd):
            kx, ky, kc = (k >> 2) & 1, (k >> 1) & 1, k & 1
            tx = x + kx - 2 * x * kx
            ty = y + ky - 2 * y * ky
            tc = c + kc - 2 * c * kc
            cp = pltpu.make_async_remote_copy(src_ref=v_ref, dst_ref=sbuf.at[me], send_sem=ssend.at[k], recv_sem=srecv.at[k],
                                              device_id=(tx, ty, tc), device_id_type=MESH)
            cp.start()
            small.append((cp, 4 * tx + 2 * ty + tc))
        cps = []
        for w in range(n):
            h = bufs[w].shape[0] // 2
            rows = outs[w].at[pl.ds(c * h, h), :]
            rc = pltpu.make_async_remote_copy(src_ref=rows, dst_ref=rows, send_sem=send.at[w], recv_sem=recv.at[w],
                                              device_id=(x, y, 1 - c), device_id_type=MESH)
            rc.start()
            cps.append(rc)
        for k, (cp, src) in zip(range(1, nd), small):
            pltpu.make_async_remote_copy(src_ref=v_ref, dst_ref=sbuf.at[src], send_sem=ssend.at[k], recv_sem=srecv.at[k],
                                         device_id=(x, y, c), device_id_type=MESH).wait_recv()
        for cp, _ in small:
            cp.wait_send()
        acc = sbuf[0]
        for d in range(1, nd):
            acc = acc + sbuf[d]
        o_ref[...] = acc
        for w, rc in enumerate(cps):
            h = bufs[w].shape[0] // 2
            other = outs[w].at[pl.ds((1 - c) * h, h), :]
            pltpu.make_async_remote_copy(src_ref=other, dst_ref=other, send_sem=send.at[w], recv_sem=recv.at[w],
                                         device_id=(x, y, 1 - c), device_id_type=MESH).wait_recv()
            rc.wait_send()

    vm = pl.BlockSpec(memory_space=pltpu.VMEM)
    res = pl.pallas_call(body, in_specs=[HBM] * n + [vm], out_specs=[HBM] * n + [vm],
                         out_shape=tuple(SDS(a.shape, a.dtype) for a in bufs) + (SDS((R, LANES), f32),),
                         input_output_aliases={w: w for w in range(n)},
                         scratch_shapes=[pltpu.VMEM((nd, R, LANES), f32)] + [pltpu.SemaphoreType.DMA((n,))] * 2
                         + [pltpu.SemaphoreType.DMA((nd,))] * 2, name="final_exchange",
                         compiler_params=pltpu.CompilerParams(has_side_effects=True))(*bufs, vec)
    return res[:n], res[n]


def _rows128(a, rows):
    flat = a.reshape(-1)
    return jnp.pad(flat, (0, rows * LANES - flat.shape[0])).reshape(rows, LANES)


GATHER_1 = ("w_conv_out", "w_attn_out", "w_ffn_gate", "w_ffn_down")
GATHER_2 = ("w_ffn_up", "w_o")
REDUCE_A = ("w_ffn_gate", "w_ffn_up", "w_ffn_down")
REDUCE_B = ("w_conv_out", "w_attn_out", "w_o")
REDUCE_C = ("w_in",)


def _step(x, target, norm1_g, gate_b, conv_w, conv_b, conv_ln_g, conv_ln_b, norm2_g, norm_f_g, w_in, shards, slots, chip1, core):
    B = x.shape[0]
    T = B * SEQ
    xf = x.reshape(T, D_MODEL)
    tf = target.reshape(T, D_MODEL)

    h, h_t = _rms_fwd(xf, norm1_g, "rms1_fwd")
    proj, got1 = _mm("in_proj", h, w_in, grid=(N_CHIPS, T // 1024),
                     a_spec=pl.BlockSpec((1024, D_MODEL), lambda s, m: (m, 0)),
                     b_spec=pl.BlockSpec((None, D_MODEL, IN_S), lambda s, m: (s, 0, 0)),
                     o_spec=pl.BlockSpec((1024, IN_S), lambda s, m: (m, s)), o_shape=(T, IN_W), o_dtype=bf16, dims=NN,
                     comm=_x_gather_ici([shards[n] for n in GATHER_1], [slots[n] for n in GATHER_1]))
    proj3 = proj.reshape(B, SEQ, IN_W)
    proj4 = proj.reshape(B, 1, SEQ, IN_W)

    qkv = [None] + [_qkv_to_residues(proj3, g) for g in range(1, len(GROUPS))]

    def qkv_args(g):
        return (proj4, proj4, proj4, 0, 3, 6) if g == 0 else (qkv[g], qkv[g], qkv[g], 0, 1, 2)

    (o4_0, l4_0), full1 = _attn_fwd(0, *qkv_args(0), comm=_x_gather_fwd(list(got1)))
    o4_1, l4_1 = _attn_fwd(1, *qkv_args(1))
    o4_2, l4_2 = _attn_fwd(2, *qkv_args(2))
    full = dict(zip(GATHER_1, full1))
    w_conv_out, w_attn_out, w_gate, w_down = (full[n] for n in GATHER_1)
    w_conv_out_f = w_conv_out.reshape(D_MODEL, D_MODEL)
    mix3, lse3, lse_r1, lse_r2 = _attn_mix(o4_0.reshape(B, SEQ, GW), l4_0.reshape(B, SEQ, LANES), [o4_1, o4_2], [l4_1, l4_2])
    mix = mix3.reshape(T, GW)
    y_attn = _attn_out_fwd(mix, w_attn_out)

    c1, got2 = _glu_conv_fwd(proj3, conv_w, conv_b, comm=_x_gather_ici([shards[n] for n in GATHER_2], [slots[n] for n in GATHER_2]))
    c1 = c1.reshape(T, D_MODEL)
    (c3, y_conv), (w_up, w_o) = _conv_out_fwd(c1, conv_ln_g, conv_ln_b, w_conv_out_f, comm=_x_gather_fwd(list(got2)))
    w_o_f = w_o.reshape(D_MODEL, D_MODEL)

    merged, x1, h2 = _merge_o_proj_rms(proj, gate_b, y_attn, y_conv, w_o_f, xf, norm2_g)
    fa, fb, ff = _ffn_up(h2, w_gate, w_up)
    loss, dx2, d_gf = _ffn_down_loss(ff, w_down, x1, norm_f_g.reshape(1, D_MODEL), tf)

    d_w_down = _mm("d_w_down", ff, dx2, grid=(N_CHIPS, T // 2048),
                   a_spec=pl.BlockSpec((None, 2048, FF_S), lambda s, k: (s, k, 0)),
                   b_spec=pl.BlockSpec((2048, D_MODEL), lambda s, k: (k, 0)),
                   o_spec=pl.BlockSpec((None, FF_S, D_MODEL), lambda s, k: (s, 0, 0)), o_shape=(N_CHIPS, FF_S, D_MODEL),
                   o_dtype=bf16, dims=TN, acc_shape=(FF_S, D_MODEL), k_axis=1, sem=("parallel", "arbitrary"))
    da, db = _ffn_down_bwd(dx2, w_down, fa, fb)

    d_w_gate, d_w_up = _ffn_up_bwd_w(da, db, h2)
    part = dict(w_ffn_gate=d_w_gate, w_ffn_up=d_w_up, w_ffn_down=d_w_down)

    def pair_sums(names, got):
        out = {}
        for ns, gs in _same_shape_groups(names, got):
            out.update(zip(ns, _add_pairs([part[n] for n in ns], gs, core, "pair_sum_" + ns[0])))
        return [out[n] for n in names]

    (dx1, d_g2), got = _mm_rms_bwd("ffn_dh2", [(da, w_gate), (db, w_up)], x1, norm2_g, dx2, tm=1024,
                                   a_spec=pl.BlockSpec((None, 1024, FF_S), lambda m, s: (s, m, 0)),
                                   b_spec=pl.BlockSpec((None, FF_S, D_MODEL), lambda m, s: (s, 0, 0)), dims=NN,
                                   comm=_x_pair_exchange([part[n] for n in REDUCE_A]), vmem=VMEM_LIMIT_BIG)
    sums_a = pair_sums(REDUCE_A, got)

    d_w_o = _mm_tn_tokens("d_w_o", merged, dx1, bf16).reshape(N_CHIPS, D_MODEL // N_CHIPS, D_MODEL)
    dmerged = _mm_nt_full("d_merged", dx1, w_o_f, bf16)
    dya, dyc, dga, dgc, d_gba, d_gbc = _merge_bwd(dmerged, proj, gate_b, y_attn, y_conv)

    d_w_conv_out = _mm_tn_tokens("d_w_conv_out", c3, dyc, bf16).reshape(N_CHIPS, D_MODEL // N_CHIPS, D_MODEL)
    dc1, d_ln_g, d_ln_b = _conv_out_bwd(dyc, w_conv_out_f, c1, conv_ln_g, conv_ln_b)
    (dua, dub, d_conv_w, d_conv_b), got_a = _glu_conv_bwd(dc1.reshape(B, SEQ, D_MODEL), proj3, conv_w, comm=_x_chip_exchange(sums_a))

    d_w_attn_out = _attn_out_bwd_w(mix, dya)
    part.update(w_conv_out=d_w_conv_out, w_attn_out=d_w_attn_out, w_o=d_w_o)
    dmix, got = _attn_out_bwd_x(dya, w_attn_out, comm=_x_pair_exchange([part[n] for n in REDUCE_B]))
    sums_b = pair_sums(REDUCE_B, got)
    dmix3 = dmix.reshape(B, SEQ, GW)
    delta3, delta_r1, delta_r2, dmix_r1, dmix_r2 = _attn_delta(dmix3, mix3)
    one = (B, 1, SEQ)
    (dq0, dk0, dv0), got_b = _attn_bwd(0, *qkv_args(0), dmix3.reshape(one + (GW,)), lse3.reshape(one + (LANES,)),
                                       delta3.reshape(one + (LANES,)), comm=_x_chip_exchange(sums_b))
    dqkv = [tuple(t.reshape(B, SEQ, GW) for t in (dq0, dk0, dv0)),
            _attn_bwd(1, *qkv_args(1), dmix_r1, lse_r1, delta_r1), _attn_bwd(2, *qkv_args(2), dmix_r2, lse_r2, delta_r2)]
    dproj = _assemble_dproj(dqkv, dua, dub, dga.reshape(B, SEQ, D_MODEL), dgc.reshape(B, SEQ, D_MODEL)).reshape(T, IN_W)

    d_w_in = _mm("d_w_in", h_t, dproj, grid=(N_CHIPS, T // 2048),
                 a_spec=pl.BlockSpec((D_MODEL, 2048), lambda s, k: (0, k)),
                 b_spec=pl.BlockSpec((2048, IN_S), lambda s, k: (k, s)),
                 o_spec=pl.BlockSpec((None, D_MODEL, IN_S), lambda s, k: (s, 0, 0)), o_shape=(N_CHIPS, D_MODEL, IN_S),
                 o_dtype=bf16, dims=NN, acc_shape=(D_MODEL, IN_S), k_axis=1, sem=("parallel", "arbitrary"), vmem=VMEM_LIMIT_BIG)
    part.update(w_in=d_w_in)
    sums_c = pair_sums(REDUCE_C, _run_exchange(_x_pair_exchange([d_w_in]), "grad_pair_exchange_c"))
    (dx, d_g1), got_c = _mm_rms_bwd("d_h", [(dproj, w_in)], xf, norm1_g, dx1, tm=1024,
                                    a_spec=pl.BlockSpec((1024, IN_S), lambda m, s: (m, s)),
                                    b_spec=pl.BlockSpec((None, D_MODEL, IN_S), lambda m, s: (s, 0, 0)), dims=NT,
                                    comm=_x_chip_exchange(sums_c), vmem=VMEM_LIMIT_BIG)

    names = REDUCE_A + REDUCE_B + REDUCE_C
    sums = dict(zip(names, sums_a + sums_b + sums_c))
    halves = {}
    for ns, gs in _same_shape_groups(names, got_a + got_b + got_c):
        halves.update(zip(ns, _sum4([sums[n] for n in ns], gs, chip1, core, "chip_sum_" + ns[0])))
    big = {n: halves[n] for n in names}
    small = dict(norm1_g=d_g1, gate_b=jnp.concatenate([d_gba, d_gbc], axis=-1), conv_b=d_conv_b, conv_ln_g=d_ln_g,
                 conv_ln_b=d_ln_b, norm2_g=d_g2, norm_f_g=d_gf, conv_w=d_conv_w)
    return loss, dx.reshape(B, SEQ, D_MODEL), big, small


BIG = ("w_in", "w_conv_out", "w_attn_out", "w_o", "w_ffn_gate", "w_ffn_up", "w_ffn_down")
TRANSPOSED = ("w_ffn_gate", "w_ffn_up")
SMALL = ("norm1_g", "gate_b", "conv_b", "conv_ln_g", "conv_ln_b", "norm2_g", "norm_f_g")
SMALL_ROWS = {"norm1_g": 8, "gate_b": 16, "conv_b": 8, "conv_ln_g": 8, "conv_ln_b": 8, "norm2_g": 8, "norm_f_g": 8}
LOSS_ROWS = 8
CONVW_ROWS = 32 * D_MODEL // LANES


def kernel(x, norm1_g, w_in, gate_b, conv_w, conv_b, conv_ln_g, conv_ln_b, w_conv_out, w_attn_out, w_o, norm2_g, w_ffn_gate, w_ffn_up, w_ffn_down, norm_f_g, loss_target, m_norm1_g, m_w_in, m_gate_b, m_conv_w, m_conv_b, m_conv_ln_g, m_conv_ln_b, m_w_conv_out, m_w_attn_out, m_w_o, m_norm2_g, m_w_ffn_gate, m_w_ffn_up, m_w_ffn_down, m_norm_f_g, v_norm1_g, v_w_in, v_gate_b, v_conv_w, v_conv_b, v_conv_ln_g, v_conv_ln_b, v_w_conv_out, v_w_attn_out, v_w_o, v_norm2_g, v_w_ffn_gate, v_w_ffn_up, v_w_ffn_down, v_norm_f_g):
    W = dict(norm1_g=norm1_g, w_in=w_in, gate_b=gate_b, conv_w=conv_w, conv_b=conv_b, conv_ln_g=conv_ln_g, conv_ln_b=conv_ln_b,
             w_conv_out=w_conv_out, w_attn_out=w_attn_out, w_o=w_o, norm2_g=norm2_g, w_ffn_gate=w_ffn_gate, w_ffn_up=w_ffn_up,
             w_ffn_down=w_ffn_down, norm_f_g=norm_f_g)
    M = dict(norm1_g=m_norm1_g, w_in=m_w_in, gate_b=m_gate_b, conv_w=m_conv_w, conv_b=m_conv_b, conv_ln_g=m_conv_ln_g,
             conv_ln_b=m_conv_ln_b, w_conv_out=m_w_conv_out, w_attn_out=m_w_attn_out, w_o=m_w_o, norm2_g=m_norm2_g,
             w_ffn_gate=m_w_ffn_gate, w_ffn_up=m_w_ffn_up, w_ffn_down=m_w_ffn_down, norm_f_g=m_norm_f_g)
    V = dict(norm1_g=v_norm1_g, w_in=v_w_in, gate_b=v_gate_b, conv_w=v_conv_w, conv_b=v_conv_b, conv_ln_g=v_conv_ln_g,
             conv_ln_b=v_conv_ln_b, w_conv_out=v_w_conv_out, w_attn_out=v_w_attn_out, w_o=v_w_o, norm2_g=v_norm2_g,
             w_ffn_gate=v_w_ffn_gate, w_ffn_up=v_w_ffn_up, w_ffn_down=v_w_ffn_down, norm_f_g=v_norm_f_g)
    order = list(W)

    def as2d(n, a):
        a = a.reshape(a.shape[-2:])
        return a.T if n in TRANSPOSED else a

    def from2d(n, a):
        return (a.T if n in TRANSPOSED else a).reshape(W[n].shape)

    shard2d = {n: as2d(n, W[n]) for n in BIG}
    chip = 2 * lax.axis_index("x") + lax.axis_index("y")

    core = lax.axis_index("c").astype(jnp.int32).reshape(1)
    chip1 = chip.astype(jnp.int32).reshape(1)
    shards = {n: shard2d[n].astype(bf16) for n in BIG}

    def slot_for(s):
        return lax.dynamic_update_slice(lax.empty((N_CHIPS,) + s.shape, s.dtype), s[None], (chip, 0, 0))

    slots = {n: slot_for(s) for n, s in shards.items()}
    cw = jnp.pad(conv_w.reshape(CONV_K, D_MODEL // N_CHIPS), ((0, CONV_HALO - CONV_K), (0, 0)))
    w_in_full, cw4 = _allgather_weights([shards["w_in"], cw], [slots["w_in"], slot_for(cw)])
    conv_w_full = cw4.transpose(1, 0, 2).reshape(CONV_HALO, D_MODEL)[:CONV_K]

    loss, grad_x, grads, gsmall = _step(x, loss_target, norm1_g, gate_b, conv_w_full, conv_b, conv_ln_g, conv_ln_b, norm2_g,
                                        norm_f_g, w_in_full, shards, slots, chip1, core)

    pieces = [_rows128(loss, LOSS_ROWS)] + [_rows128(gsmall[n], SMALL_ROWS[n]) for n in SMALL] + [_rows128(gsmall["conv_w"], CONVW_ROWS)]
    full_rows, tot = _final_exchange([grads[n] for n in BIG], jnp.concatenate(pieces, axis=0))
    grads = dict(zip(BIG, full_rows))
    loss_out = tot[0, 0]
    row = LOSS_ROWS
    for n in SMALL:
        grads[n] = tot[row: row + W[n].size // LANES].reshape(W[n].shape)
        row += SMALL_ROWS[n]
    dcw = tot[row: row + CONV_K * D_MODEL // LANES].reshape(CONV_K, D_MODEL)
    grads["conv_w"] = lax.dynamic_slice(dcw, (0, chip * (D_MODEL // N_CHIPS)), (CONV_K, D_MODEL // N_CHIPS))

    delta, new_m, new_v = {}, {}, {}
    for ns, ws in _same_shape_groups(BIG, [shard2d[n] for n in BIG]):
        ds, nms, nvs = _adamw(ws, [grads[n] for n in ns], [as2d(n, M[n]) for n in ns], [as2d(n, V[n]) for n in ns], "adamw_" + ns[0])
        for n, d, nm, nv in zip(ns, ds, nms, nvs):
            delta[n], new_m[n], new_v[n], grads[n] = (from2d(n, t) for t in (d, nm, nv, grads[n]))

    def pack(src):
        return jnp.concatenate([_rows128(src[n], SMALL_ROWS[n]) for n in SMALL], axis=0)

    (d,), (nm,), (nv,) = _adamw([pack(W)], [pack(grads)], [pack(M)], [pack(V)], "adamw_small")
    row = 0
    for n in SMALL:
        k = W[n].size // LANES
        delta[n], new_m[n], new_v[n] = (t[row: row + k].reshape(W[n].shape) for t in (d, nm, nv))
        row += SMALL_ROWS[n]

    def pad32(a):
        return jnp.pad(a.reshape(CONV_K, D_MODEL // N_CHIPS), ((0, 1), (0, 0)))

    (d,), (nm,), (nv,) = _adamw([pad32(conv_w)], [pad32(grads["conv_w"])], [pad32(m_conv_w)], [pad32(v_conv_w)], "adamw_conv_w")
    delta["conv_w"], new_m["conv_w"], new_v["conv_w"] = (t[:CONV_K].reshape(conv_w.shape) for t in (d, nm, nv))
    grads["conv_w"] = grads["conv_w"].reshape(conv_w.shape)

    return (loss_out, grad_x, *[grads[n] for n in order], *[delta[n] for n in order],
            *[new_m[n] for n in order], *[new_v[n] for n in order])
```

```python
import functools
import math
from typing import Callable, NamedTuple

import numpy as np
import jax
import jax.numpy as jnp
from jax import lax
from jax.experimental import pallas as pl
from jax.experimental.pallas import tpu as pltpu

f32 = jnp.float32
bf16 = jnp.bfloat16
SDS = jax.ShapeDtypeStruct
MESH = pl.DeviceIdType.MESH

D_MODEL = 1024
SEQ = 2048
HEAD_DIM = 64
HEADS = 8
GROUPS = ((128, 1), (512, 4), (2048, 16))
GW = HEADS * HEAD_DIM
ATTN_W = len(GROUPS) * GW
Q_BLOCK = 128
CONV_K = 31
D_FF = 2816
IN_W = 3 * ATTN_W + 2 * D_MODEL + 2 * D_MODEL
N_CHIPS = 4
IN_S = IN_W // N_CHIPS
FF_S = D_FF // N_CHIPS
RMS_EPS = 1e-6
LN_EPS = 1e-5
LR, B1, B2, ADAM_EPS, WD, STEP = 0.001, 0.9, 0.999, 1e-08, 0.01, 10
NEG = -1e30
LANES = 128
VMEM_LIMIT = 48 * 2 ** 20
VMEM_LIMIT_BIG = 54 * 2 ** 20
CB = 512
UA_CB, UB_CB, GA_CB, GC_CB = 9, 11, 13, 15


def _alibi_slope_list(n):
    def pow2(m):
        start = 2.0 ** (-8.0 / m)
        return [start ** (i + 1) for i in range(m)]
    if math.log2(n).is_integer():
        return pow2(n)
    c = 2 ** math.floor(math.log2(n))
    return pow2(c) + _alibi_slope_list(2 * c)[0::2][: n - c]


_SLOPES = np.asarray(sorted(_alibi_slope_list(len(GROUPS) * HEADS), reverse=True), dtype=np.float32).reshape(len(GROUPS), HEADS)


def _cp(sem=None, vmem=VMEM_LIMIT):
    return pltpu.CompilerParams(dimension_semantics=sem, vmem_limit_bytes=vmem)


def _sigmoid(x):
    return 1.0 / (1.0 + jnp.exp(-x))


HBM = pl.BlockSpec(memory_space=pl.ANY)


class _Exchange(NamedTuple):
    ins: list
    out_shape: list
    scratch: list
    aliases: dict
    start: Callable
    wait: Callable


def _launch(body, *, name, grid, in_specs, out_specs, out_shape, args, scratch_shapes=(), sem=None, comm=None, vmem=VMEM_LIMIT):
    if comm is None:
        return pl.pallas_call(body, grid=grid, in_specs=in_specs, out_specs=out_specs, out_shape=out_shape,
                              scratch_shapes=list(scratch_shapes), name=name, compiler_params=_cp(sem, vmem))(*args)
    multi = isinstance(out_shape, (tuple, list))
    m_out = list(out_shape) if multi else [out_shape]
    m_ospec = list(out_specs) if multi else [out_specs]
    n_in, n_out, n_scr = len(in_specs), len(m_out), len(scratch_shapes)
    nc_in, nc_out = len(comm.ins), len(comm.out_shape)

    def hosted(*refs):
        bounds = np.cumsum([0, n_in, nc_in, n_out, nc_out, n_scr])
        mi, ci, mo, co, ms = (refs[a:b] for a, b in zip(bounds[:-1], bounds[1:]))
        cs = refs[bounds[-1]:]
        ids = [pl.program_id(a) for a in range(len(grid))]
        first = functools.reduce(jnp.logical_and, [i == 0 for i in ids])
        last = functools.reduce(jnp.logical_and, [i == g - 1 for i, g in zip(ids, grid)])

        @pl.when(first)
        def _():
            comm.start(ci, co, cs)

        body(*mi, *mo, *ms)

        @pl.when(last)
        def _():
            comm.wait(ci, co, cs)

    res = pl.pallas_call(
        hosted, grid=grid, in_specs=list(in_specs) + [HBM] * nc_in, out_specs=m_ospec + [HBM] * nc_out,
        out_shape=tuple(m_out) + tuple(comm.out_shape), scratch_shapes=list(scratch_shapes) + list(comm.scratch),
        input_output_aliases={n_in + i: n_out + o for i, o in comm.aliases.items()}, name=name + "_comm",
        compiler_params=pltpu.CompilerParams(dimension_semantics=("arbitrary",) * len(grid), vmem_limit_bytes=vmem,
                                             has_side_effects=True))(*args, *comm.ins)
    return (tuple(res[:n_out]) if multi else res[0]), tuple(res[n_out:])


def _run_exchange(ex, name):
    n_in = len(ex.ins)

    def body(*refs):
        ins, outs, sems = refs[:n_in], refs[n_in:n_in + len(ex.out_shape)], refs[n_in + len(ex.out_shape):]
        ex.start(ins, outs, sems)
        ex.wait(ins, outs, sems)

    return pl.pallas_call(body, in_specs=[HBM] * n_in, out_specs=[HBM] * len(ex.out_shape), out_shape=tuple(ex.out_shape),
                          scratch_shapes=list(ex.scratch), input_output_aliases=dict(ex.aliases), name=name,
                          compiler_params=pltpu.CompilerParams(has_side_effects=True))(*ex.ins)


def _rmsnorm(xv, gv):
    return xv * lax.rsqrt(jnp.mean(xv * xv, axis=-1, keepdims=True) + RMS_EPS) * gv


def _rms_fwd(x, g, name, comm=None):
    T = x.shape[0]
    tm = 512

    def body(x_ref, g_ref, o_ref, ot_ref):
        hv = _rmsnorm(x_ref[...], g_ref[...])
        o_ref[...] = hv.astype(bf16)
        ot_ref[...] = hv.T.astype(bf16)

    row = pl.BlockSpec((tm, D_MODEL), lambda i: (i, 0))
    vec = pl.BlockSpec((1, D_MODEL), lambda i: (0, 0))
    return _launch(body, name=name, grid=(T // tm,), in_specs=[row, vec], out_specs=[row, pl.BlockSpec((D_MODEL, tm), lambda i: (0, i))],
                   out_shape=(SDS((T, D_MODEL), bf16), SDS((D_MODEL, T), bf16)), args=(x, g), sem=("parallel",), comm=comm)


def _rms_bwd_tile(dyv, xv, gv, dres):
    r = lax.rsqrt(jnp.mean(xv * xv, axis=-1, keepdims=True) + RMS_EPS)
    xh = xv * r
    dxh = dyv * gv
    dx = dres + r * (dxh - xh * jnp.mean(dxh * xh, axis=-1, keepdims=True))
    return dx, jnp.sum(dyv * xh, axis=0, keepdims=True)


def _accumulate(first, refs_parts):
    @pl.when(first)
    def _():
        for ref, part in refs_parts:
            ref[...] = part

    @pl.when(jnp.logical_not(first))
    def _():
        for ref, part in refs_parts:
            ref[...] += part


def _mm_rms_bwd(name, ops, x, g, dres, *, tm, a_spec, b_spec, dims, comm=None, vmem=VMEM_LIMIT):
    T = x.shape[0]
    n = len(ops)

    def body(*refs):
        ab_refs = refs[:2 * n]
        x_ref, g_ref, r_ref, dx_ref, dg_ref, acc = refs[2 * n:]
        m, s = pl.program_id(0), pl.program_id(1)
        p = sum(lax.dot_general(ab_refs[2 * i][...], ab_refs[2 * i + 1][...], dims, preferred_element_type=f32) for i in range(n))

        @pl.when(s == 0)
        def _():
            acc[...] = p

        @pl.when(s > 0)
        def _():
            acc[...] += p

        @pl.when(s == N_CHIPS - 1)
        def _():
            dx, part = _rms_bwd_tile(acc[...], x_ref[...], g_ref[...], r_ref[...])
            dx_ref[...] = dx
            _accumulate(m == 0, [(dg_ref, part)])

    row = pl.BlockSpec((tm, D_MODEL), lambda m, s: (m, 0))
    vec = pl.BlockSpec((1, D_MODEL), lambda m, s: (0, 0))
    return _launch(body, name=name, grid=(T // tm, N_CHIPS), in_specs=[a_spec, b_spec] * n + [row, vec, row], out_specs=[row, vec],
                   out_shape=(SDS((T, D_MODEL), f32), SDS((1, D_MODEL), f32)), args=tuple(t for ab in ops for t in ab) + (x, g, dres),
                   scratch_shapes=[pltpu.VMEM((tm, D_MODEL), f32)], sem=("arbitrary", "arbitrary"), comm=comm, vmem=vmem)


def _ffn_down_loss(ff, wd, x1, gf, target):
    T = x1.shape[0]
    tm = 1024
    nm = T // tm

    def body(f_ref, w_ref, x_ref, g_ref, t_ref, loss_ref, dx_ref, dg_ref, acc):
        m, s = pl.program_id(0), pl.program_id(1)
        p = jnp.dot(f_ref[...], w_ref[...], preferred_element_type=f32)

        @pl.when(s == 0)
        def _():
            acc[...] = p

        @pl.when(s > 0)
        def _():
            acc[...] += p

        @pl.when(s == N_CHIPS - 1)
        def _():
            xv = acc[...] + x_ref[...]
            gv = g_ref[...]
            r = lax.rsqrt(jnp.mean(xv * xv, axis=-1, keepdims=True) + RMS_EPS)
            xh = xv * r
            e = xh * gv - t_ref[...]
            part_l = jnp.broadcast_to(0.5 * jnp.sum(jnp.mean(e * e, axis=-1, keepdims=True), axis=0, keepdims=True), (1, LANES))
            dy = e * (1.0 / D_MODEL)
            dxh = dy * gv
            dx_ref[...] = r * (dxh - xh * jnp.mean(dxh * xh, axis=-1, keepdims=True))
            part_g = jnp.sum(dy * xh, axis=0, keepdims=True)

            @pl.when(m == 0)
            def _():
                loss_ref[...] = part_l
                dg_ref[...] = part_g

            @pl.when(m > 0)
            def _():
                loss_ref[...] += part_l
                dg_ref[...] += part_g

    row = pl.BlockSpec((tm, D_MODEL), lambda m, s: (m, 0))
    vec = pl.BlockSpec((1, D_MODEL), lambda m, s: (0, 0))
    return pl.pallas_call(
        body, grid=(nm, N_CHIPS),
        in_specs=[pl.BlockSpec((None, tm, FF_S), lambda m, s: (s, m, 0)), pl.BlockSpec((None, FF_S, D_MODEL), lambda m, s: (s, 0, 0)),
                  row, vec, row],
        out_specs=[pl.BlockSpec((1, LANES), lambda m, s: (0, 0)), row, vec],
        out_shape=(SDS((1, LANES), f32), SDS((T, D_MODEL), f32), SDS((1, D_MODEL), f32)),
        scratch_shapes=[pltpu.VMEM((tm, D_MODEL), f32)], name="ffn_down_loss", compiler_params=_cp(("arbitrary", "arbitrary")))(
            ff, wd, x1, gf, target)


def _merge_o_proj_rms(proj, gate_b, ya, yc, w_o, x, g2):
    T = x.shape[0]
    tm = 512

    def body(ga0, ga1, gc0, gc1, b_ref, ya_ref, yc_ref, w_ref, x_ref, g_ref, m_ref, x1_ref, h2_ref):
        bv = b_ref[...]
        ga = jnp.concatenate([ga0[...], ga1[...]], axis=-1).astype(f32)
        gc = jnp.concatenate([gc0[...], gc1[...]], axis=-1).astype(f32)
        merged = (_sigmoid(ga + bv[:, :D_MODEL]) * ya_ref[...].astype(f32)
                  + _sigmoid(gc + bv[:, D_MODEL:]) * yc_ref[...].astype(f32)).astype(bf16)
        m_ref[...] = merged
        x1 = jnp.dot(merged, w_ref[...], preferred_element_type=f32) + x_ref[...]
        x1_ref[...] = x1
        h2_ref[...] = _rmsnorm(x1, g_ref[...]).astype(bf16)

    row = pl.BlockSpec((tm, D_MODEL), lambda i: (i, 0))
    return pl.pallas_call(
        body, grid=(T // tm,),
        in_specs=[pl.BlockSpec((tm, CB), lambda i, cb=cb: (i, cb)) for cb in (GA_CB, GA_CB + 1, GC_CB, GC_CB + 1)]
        + [pl.BlockSpec((1, 2 * D_MODEL), lambda i: (0, 0)), row, row, pl.BlockSpec((D_MODEL, D_MODEL), lambda i: (0, 0)), row,
           pl.BlockSpec((1, D_MODEL), lambda i: (0, 0))],
        out_specs=[row, row, row], out_shape=(SDS((T, D_MODEL), bf16), SDS((T, D_MODEL), f32), SDS((T, D_MODEL), bf16)),
        name="merge_o_proj_rms", compiler_params=_cp(("parallel",)))(proj, proj, proj, proj, gate_b, ya, yc, w_o, x, g2)


def _conv_out_fwd(c1, g, b, w, comm=None):
    T = c1.shape[0]
    tm = 1024

    def body(c_ref, g_ref, b_ref, w_ref, c3_ref, y_ref):
        cv = c_ref[...]
        mu = jnp.mean(cv, axis=-1, keepdims=True)
        cc = cv - mu
        var = jnp.mean(cc * cc, axis=-1, keepdims=True)
        c2 = cc * lax.rsqrt(var + LN_EPS) * g_ref[...] + b_ref[...]
        c3 = (c2 * _sigmoid(c2)).astype(bf16)
        c3_ref[...] = c3
        y_ref[...] = jnp.dot(c3, w_ref[...], preferred_element_type=f32).astype(bf16)

    row = pl.BlockSpec((tm, D_MODEL), lambda i: (i, 0))
    vec = pl.BlockSpec((1, D_MODEL), lambda i: (0, 0))
    sd = SDS((T, D_MODEL), bf16)
    return _launch(body, name="conv_out_fwd", grid=(T // tm,), in_specs=[row, vec, vec, pl.BlockSpec((D_MODEL, D_MODEL), lambda i: (0, 0))],
                   out_specs=[row, row], out_shape=(sd, sd), args=(c1, g, b, w), sem=("parallel",), comm=comm)


def _conv_out_bwd(dyc, w, c1, g, b):
    T = c1.shape[0]
    tm = 1024

    def body(d_ref, w_ref, c_ref, g_ref, b_ref, dc_ref, dg_ref, db_ref):
        dc3 = lax.dot_general(d_ref[...], w_ref[...], NT, preferred_element_type=f32)
        cv = c_ref[...]
        gv = g_ref[...]
        mu = jnp.mean(cv, axis=-1, keepdims=True)
        cc = cv - mu
        var = jnp.mean(cc * cc, axis=-1, keepdims=True)
        rs = lax.rsqrt(var + LN_EPS)
        xh = cc * rs
        c2 = xh * gv + b_ref[...]
        sg = _sigmoid(c2)
        dc2 = dc3 * (sg * (1.0 + c2 * (1.0 - sg)))
        dxh = dc2 * gv
        dc_ref[...] = rs * (dxh - jnp.mean(dxh, axis=-1, keepdims=True) - xh * jnp.mean(dxh * xh, axis=-1, keepdims=True))
        _accumulate(pl.program_id(0) == 0, [(dg_ref, jnp.sum(dc2 * xh, axis=0, keepdims=True)), (db_ref, jnp.sum(dc2, axis=0, keepdims=True))])

    row = pl.BlockSpec((tm, D_MODEL), lambda i: (i, 0))
    vec = pl.BlockSpec((1, D_MODEL), lambda i: (0, 0))
    return pl.pallas_call(body, grid=(T // tm,), in_specs=[row, pl.BlockSpec((D_MODEL, D_MODEL), lambda i: (0, 0)), row, vec, vec],
                          out_specs=[row, vec, vec],
                          out_shape=(SDS((T, D_MODEL), f32), SDS((1, D_MODEL), f32), SDS((1, D_MODEL), f32)),
                          name="conv_out_bwd", compiler_params=_cp(("arbitrary",)))(dyc, w, c1, g, b)


NN = (((1,), (0,)), ((), ()))
NT = (((1,), (1,)), ((), ()))
TN = (((0,), (0,)), ((), ()))


def _mm(name, a, b, *, grid, a_spec, b_spec, o_spec, o_shape, o_dtype, dims, acc_shape=None, k_axis=None,
        res=None, res_spec=None, sem=None, comm=None, vmem=VMEM_LIMIT):
    nk = 1 if k_axis is None else grid[k_axis]

    def body(*refs):
        if res is None:
            a_ref, b_ref, o_ref = refs[:3]
            r_ref, scr = None, refs[3:]
        else:
            a_ref, b_ref, r_ref, o_ref = refs[:4]
            scr = refs[4:]
        p = lax.dot_general(a_ref[...].astype(bf16), b_ref[...].astype(bf16), dims, preferred_element_type=f32)
        if nk == 1:
            if r_ref is not None:
                p = p + r_ref[...]
            o_ref[...] = p.astype(o_dtype)
            return
        acc = scr[0]
        k = pl.program_id(k_axis)

        @pl.when(k == 0)
        def _():
            acc[...] = p

        @pl.when(k > 0)
        def _():
            acc[...] += p

        @pl.when(k == nk - 1)
        def _():
            t = acc[...]
            if r_ref is not None:
                t = t + r_ref[...]
            o_ref[...] = t.astype(o_dtype)

    ins = [a, b] + ([] if res is None else [res])
    specs = [a_spec, b_spec] + ([] if res is None else [res_spec])
    scratch = [] if nk == 1 else [pltpu.VMEM(acc_shape, f32)]
    return _launch(body, name=name, grid=grid, in_specs=specs, out_specs=o_spec, out_shape=SDS(o_shape, o_dtype),
                   args=ins, scratch_shapes=scratch, sem=sem, comm=comm, vmem=vmem)


def _mm_nn_full(name, a, b, o_dtype, res=None, tm=1024, comm=None):
    T, K = a.shape
    N = b.shape[1]
    return _mm(name, a, b, grid=(T // tm,), a_spec=pl.BlockSpec((tm, K), lambda i: (i, 0)),
               b_spec=pl.BlockSpec((K, N), lambda i: (0, 0)), o_spec=pl.BlockSpec((tm, N), lambda i: (i, 0)),
               o_shape=(T, N), o_dtype=o_dtype, dims=NN, res=res,
               res_spec=None if res is None else pl.BlockSpec((tm, N), lambda i: (i, 0)), sem=("parallel",), comm=comm)


def _mm_nt_full(name, a, b, o_dtype, tm=1024):
    T, N = a.shape
    K = b.shape[0]
    return _mm(name, a, b, grid=(T // tm,), a_spec=pl.BlockSpec((tm, N), lambda i: (i, 0)),
               b_spec=pl.BlockSpec((K, N), lambda i: (0, 0)), o_spec=pl.BlockSpec((tm, K), lambda i: (i, 0)),
               o_shape=(T, K), o_dtype=o_dtype, dims=NT, sem=("parallel",))


def _mm_tn_tokens(name, a, b, o_dtype, tk=1024):
    T, K = a.shape
    N = b.shape[1]
    return _mm(name, a, b, grid=(T // tk,), a_spec=pl.BlockSpec((tk, K), lambda k: (k, 0)),
               b_spec=pl.BlockSpec((tk, N), lambda k: (k, 0)), o_spec=pl.BlockSpec((K, N), lambda k: (0, 0)),
               o_shape=(K, N), o_dtype=o_dtype, dims=TN, acc_shape=(K, N), k_axis=0, sem=("arbitrary",))


AO_S = D_MODEL // N_CHIPS


def _attn_out_cat(w_ref):
    return jnp.concatenate([w_ref[s] for s in range(N_CHIPS)], axis=-1)


def _attn_out_fwd(mix, w4):
    T = mix.shape[0]
    tm = 1024

    def body(a_ref, w_ref, o_ref):
        o_ref[...] = jnp.dot(a_ref[...], _attn_out_cat(w_ref), preferred_element_type=f32).astype(bf16)

    return pl.pallas_call(body, grid=(T // tm,),
                          in_specs=[pl.BlockSpec((tm, GW), lambda i: (i, 0)), pl.BlockSpec((N_CHIPS, GW, AO_S), lambda i: (0, 0, 0))],
                          out_specs=pl.BlockSpec((tm, D_MODEL), lambda i: (i, 0)), out_shape=SDS((T, D_MODEL), bf16), name="attn_out",
                          compiler_params=_cp(("parallel",)))(mix, w4)


def _attn_out_bwd_x(dya, w4, comm=None):
    T = dya.shape[0]
    tm = 1024

    def body(d_ref, w_ref, o_ref):
        o_ref[...] = lax.dot_general(d_ref[...], _attn_out_cat(w_ref), NT, preferred_element_type=f32).astype(bf16)

    return _launch(body, name="d_mix", grid=(T // tm,),
                   in_specs=[pl.BlockSpec((tm, D_MODEL), lambda i: (i, 0)), pl.BlockSpec((N_CHIPS, GW, AO_S), lambda i: (0, 0, 0))],
                   out_specs=pl.BlockSpec((tm, GW), lambda i: (i, 0)), out_shape=SDS((T, GW), bf16), args=(dya, w4),
                   sem=("parallel",), comm=comm)


def _attn_out_bwd_w(mix, dya):
    T = mix.shape[0]
    tk = 1024
    nk = T // tk

    def body(a_ref, d_ref, o_ref, acc):
        k = pl.program_id(0)
        p = lax.dot_general(a_ref[...], d_ref[...], TN, preferred_element_type=f32)

        @pl.when(k == 0)
        def _():
            acc[...] = p

        @pl.when(k > 0)
        def _():
            acc[...] += p

        @pl.when(k == nk - 1)
        def _():
            for s in range(N_CHIPS):
                o_ref[s] = acc[:, s * AO_S:(s + 1) * AO_S].astype(bf16)

    return pl.pallas_call(body, grid=(nk,),
                          in_specs=[pl.BlockSpec((tk, GW), lambda k: (k, 0)), pl.BlockSpec((tk, D_MODEL), lambda k: (k, 0))],
                          out_specs=pl.BlockSpec((N_CHIPS, GW, AO_S), lambda k: (0, 0, 0)), out_shape=SDS((N_CHIPS, GW, AO_S), bf16),
                          scratch_shapes=[pltpu.VMEM((GW, D_MODEL), f32)], name="d_w_attn_out",
                          compiler_params=_cp(("arbitrary",)))(mix, dya)


def _ffn_up(h2, wg, wu):
    T = h2.shape[0]
    tm = 1024

    def body(h_ref, wg_ref, wu_ref, a_ref, b_ref, f_ref):
        hv = h_ref[...]
        av = lax.dot_general(hv, wg_ref[...], NT, preferred_element_type=f32)
        bv = lax.dot_general(hv, wu_ref[...], NT, preferred_element_type=f32)
        a_ref[...] = av.astype(bf16)
        b_ref[...] = bv.astype(bf16)
        f_ref[...] = (av * _sigmoid(av) * bv).astype(bf16)

    wspec = pl.BlockSpec((None, FF_S, D_MODEL), lambda s, m: (s, 0, 0))
    ospec = pl.BlockSpec((None, tm, FF_S), lambda s, m: (s, m, 0))
    osd = SDS((N_CHIPS, T, FF_S), bf16)
    return pl.pallas_call(body, grid=(N_CHIPS, T // tm),
                          in_specs=[pl.BlockSpec((tm, D_MODEL), lambda s, m: (m, 0)), wspec, wspec],
                          out_specs=[ospec, ospec, ospec], out_shape=(osd, osd, osd), name="ffn_up",
                          compiler_params=_cp(("parallel", "parallel")))(h2, wg, wu)


def _ffn_up_bwd_w(da, db, h2):
    T = h2.shape[0]
    tk = 2048
    nk = T // tk

    def body(a_ref, b_ref, h_ref, og_ref, ou_ref, accg, accu):
        k = pl.program_id(1)
        hv = h_ref[...]
        pg = lax.dot_general(a_ref[...], hv, TN, preferred_element_type=f32)
        pu = lax.dot_general(b_ref[...], hv, TN, preferred_element_type=f32)
        _accumulate(k == 0, [(accg, pg), (accu, pu)])

        @pl.when(k == nk - 1)
        def _():
            og_ref[...] = accg[...].astype(bf16)
            ou_ref[...] = accu[...].astype(bf16)

    aspec = pl.BlockSpec((None, tk, FF_S), lambda s, k: (s, k, 0))
    ospec = pl.BlockSpec((None, FF_S, D_MODEL), lambda s, k: (s, 0, 0))
    osd = SDS((N_CHIPS, FF_S, D_MODEL), bf16)
    return pl.pallas_call(body, grid=(N_CHIPS, nk), in_specs=[aspec, aspec, pl.BlockSpec((tk, D_MODEL), lambda s, k: (k, 0))],
                          out_specs=[ospec, ospec], out_shape=(osd, osd),
                          scratch_shapes=[pltpu.VMEM((FF_S, D_MODEL), f32), pltpu.VMEM((FF_S, D_MODEL), f32)], name="d_w_gate_up",
                          compiler_params=_cp(("parallel", "arbitrary")))(da, db, h2)


def _ffn_down_bwd(dx2, wd, a, b):
    T = dx2.shape[0]
    tm = 1024

    def body(d_ref, w_ref, a_ref, b_ref, da_ref, db_ref):
        df = lax.dot_general(d_ref[...].astype(bf16), w_ref[...], NT, preferred_element_type=f32)
        av = a_ref[...].astype(f32)
        sg = _sigmoid(av)
        da_ref[...] = (df * b_ref[...].astype(f32) * (sg * (1.0 + av * (1.0 - sg)))).astype(bf16)
        db_ref[...] = (df * av * sg).astype(bf16)

    aspec = pl.BlockSpec((None, tm, FF_S), lambda m, s: (s, m, 0))
    osd = SDS((N_CHIPS, T, FF_S), bf16)
    return pl.pallas_call(body, grid=(T // tm, N_CHIPS),
                          in_specs=[pl.BlockSpec((tm, D_MODEL), lambda m, s: (m, 0)),
                                    pl.BlockSpec((None, FF_S, D_MODEL), lambda m, s: (s, 0, 0)), aspec, aspec],
                          out_specs=[aspec, aspec], out_shape=(osd, osd), name="ffn_down_bwd",
                          compiler_params=_cp(("parallel", "parallel")))(dx2, wd, a, b)


def _merge_bwd(dm, proj, gate_b, ya, yc):
    T = proj.shape[0]
    tm = 1024

    def body(dm_ref, ga_ref, gc_ref, ba_ref, bc_ref, ya_ref, yc_ref, dya_ref, dyc_ref, dga_ref, dgc_ref, dba_ref, dbc_ref):
        dmv = dm_ref[...].astype(f32)
        sa = _sigmoid(ga_ref[...].astype(f32) + ba_ref[...])
        sc = _sigmoid(gc_ref[...].astype(f32) + bc_ref[...])
        dya_ref[...] = (dmv * sa).astype(bf16)
        dyc_ref[...] = (dmv * sc).astype(bf16)
        dga = dmv * ya_ref[...].astype(f32) * sa * (1.0 - sa)
        dgc = dmv * yc_ref[...].astype(f32) * sc * (1.0 - sc)
        dga_ref[...] = dga.astype(bf16)
        dgc_ref[...] = dgc.astype(bf16)
        pa = jnp.sum(dga, axis=0, keepdims=True)
        pc = jnp.sum(dgc, axis=0, keepdims=True)

        @pl.when(pl.program_id(1) == 0)
        def _():
            dba_ref[...] = pa
            dbc_ref[...] = pc

        @pl.when(pl.program_id(1) > 0)
        def _():
            dba_ref[...] += pa
            dbc_ref[...] += pc

    blk = pl.BlockSpec((tm, CB), lambda j, i: (i, j))
    vec = pl.BlockSpec((1, CB), lambda j, i: (0, j))
    big = SDS((T, D_MODEL), bf16)
    small = SDS((1, D_MODEL), f32)
    return pl.pallas_call(
        body, grid=(2, T // tm),
        in_specs=[blk, pl.BlockSpec((tm, CB), lambda j, i: (i, GA_CB + j)), pl.BlockSpec((tm, CB), lambda j, i: (i, GC_CB + j)),
                  vec, pl.BlockSpec((1, CB), lambda j, i: (0, 2 + j)), blk, blk],
        out_specs=[blk, blk, blk, blk, vec, vec], out_shape=(big, big, big, big, small, small), name="merge_bwd",
        compiler_params=_cp(("parallel", "arbitrary")))(dm, proj, proj, gate_b, gate_b, ya, yc)


CONV_TS = 512
CONV_HALO = 32
CONV_RC = 64
CONV_WIN = CONV_TS + CONV_HALO
SUBLANES = 8


def _fill_shifted(win, sh):
    for b in range(1, SUBLANES):
        sh[b - 1] = win[pl.ds(b, CONV_WIN - SUBLANES), :]


def _rows_at(win, sh, row):
    a, b = divmod(row, SUBLANES)
    if b == 0:
        return win[pl.ds(row, CONV_RC), :]
    return sh[b - 1, pl.ds(a * SUBLANES, CONV_RC), :]


def _glu_conv_fwd(proj3, w, bias, comm=None):
    B = proj3.shape[0]
    nt = SEQ // CONV_TS
    hb = CONV_TS // CONV_HALO

    def body(ua_ref, ub_ref, ha_ref, hb_ref, w_ref, b_ref, o_ref, win, sh):
        i = pl.program_id(2)
        c0 = ua_ref[...].astype(f32) * _sigmoid(ub_ref[...].astype(f32))
        halo = ha_ref[...].astype(f32) * _sigmoid(hb_ref[...].astype(f32))
        win[0:CONV_HALO, :] = jnp.where(i > 0, halo, 0.0)
        win[CONV_HALO:, :] = c0
        _fill_shifted(win, sh)
        for r0 in range(0, CONV_TS, CONV_RC):
            acc = jnp.zeros((CONV_RC, CB), f32) + b_ref[...]
            for k in range(CONV_K):
                acc = acc + _rows_at(win, sh, r0 + CONV_HALO - (CONV_K - 1) + k) * w_ref[k:k + 1, :]
            o_ref[r0:r0 + CONV_RC, :] = acc

    def cur(cb):
        return pl.BlockSpec((None, CONV_TS, CB), lambda b, j, i: (b, i, cb + j))

    def prev(cb):
        return pl.BlockSpec((None, CONV_HALO, CB), lambda b, j, i: (b, jnp.maximum(i * hb - 1, 0), cb + j))

    return _launch(
        body, name="glu_conv_fwd", grid=(B, 2, nt),
        in_specs=[cur(UA_CB), cur(UB_CB), prev(UA_CB), prev(UB_CB),
                  pl.BlockSpec((CONV_K, CB), lambda b, j, i: (0, j)), pl.BlockSpec((1, CB), lambda b, j, i: (0, j))],
        out_specs=pl.BlockSpec((None, CONV_TS, CB), lambda b, j, i: (b, i, j)),
        out_shape=SDS((B, SEQ, D_MODEL), f32), args=(proj3, proj3, proj3, proj3, w, bias),
        scratch_shapes=[pltpu.VMEM((CONV_WIN, CB), f32), pltpu.VMEM((SUBLANES - 1, CONV_WIN - SUBLANES, CB), f32)],
        sem=("parallel", "parallel", "parallel"), comm=comm)


def _glu_conv_bwd(dc1, proj3, w, comm=None):
    B = proj3.shape[0]
    nt = SEQ // CONV_TS
    hb = CONV_TS // CONV_HALO

    def body(d_ref, dn_ref, ua_ref, ub_ref, ha_ref, hb_ref, w_ref, dua_ref, dub_ref, dw_ref, db_ref, winc, wind, accw, shc, shd):
        b = pl.program_id(1)
        i = pl.program_id(2)
        first = jnp.logical_and(b == 0, i == 0)
        last = jnp.logical_and(b == B - 1, i == nt - 1)

        @pl.when(first)
        def _():
            accw[...] = jnp.zeros_like(accw)
            db_ref[...] = jnp.zeros_like(db_ref)

        halo = ha_ref[...].astype(f32) * _sigmoid(hb_ref[...].astype(f32))
        winc[0:CONV_HALO, :] = jnp.where(i > 0, halo, 0.0)
        winc[CONV_HALO:, :] = ua_ref[...].astype(f32) * _sigmoid(ub_ref[...].astype(f32))
        wind[0:CONV_TS, :] = d_ref[...]
        wind[CONV_TS:, :] = jnp.where(i < nt - 1, dn_ref[...], 0.0)
        db_ref[...] += jnp.sum(d_ref[...], axis=0, keepdims=True)
        _fill_shifted(winc, shc)
        _fill_shifted(wind, shd)
        for r0 in range(0, CONV_TS, CONV_RC):
            dc0 = jnp.zeros((CONV_RC, CB), f32)
            for k in range(CONV_K):
                dc0 = dc0 + _rows_at(wind, shd, r0 + (CONV_K - 1) - k) * w_ref[k:k + 1, :]
            uav = ua_ref[r0:r0 + CONV_RC, :].astype(f32)
            sg = _sigmoid(ub_ref[r0:r0 + CONV_RC, :].astype(f32))
            dua_ref[r0:r0 + CONV_RC, :] = (dc0 * sg).astype(bf16)
            dub_ref[r0:r0 + CONV_RC, :] = (dc0 * uav * sg * (1.0 - sg)).astype(bf16)
            dv = wind[r0:r0 + CONV_RC, :]
            for k in range(CONV_K):
                prod = dv * _rows_at(winc, shc, r0 + CONV_HALO - (CONV_K - 1) + k)
                accw[k] += jnp.sum(prod.reshape(CONV_RC // 8, 8, CB), axis=0)

        @pl.when(last)
        def _():
            for k in range(CONV_K):
                dw_ref[k:k + 1, :] = jnp.sum(accw[k], axis=0, keepdims=True)
            dw_ref[CONV_K:, :] = jnp.zeros((CONV_HALO - CONV_K, CB), f32)

    def cur(cb):
        return pl.BlockSpec((None, CONV_TS, CB), lambda j, b, i: (b, i, cb + j))

    def prev(cb):
        return pl.BlockSpec((None, CONV_HALO, CB), lambda j, b, i: (b, jnp.maximum(i * hb - 1, 0), cb + j))

    nxt = pl.BlockSpec((None, CONV_HALO, CB), lambda j, b, i: (b, jnp.minimum((i + 1) * hb, SEQ // CONV_HALO - 1), j))
    big = SDS((B, SEQ, D_MODEL), bf16)
    return _launch(
        body, name="glu_conv_bwd", grid=(2, B, nt),
        in_specs=[cur(0), nxt, cur(UA_CB), cur(UB_CB), prev(UA_CB), prev(UB_CB), pl.BlockSpec((CONV_K, CB), lambda j, b, i: (0, j))],
        out_specs=[cur(0), cur(0), pl.BlockSpec((CONV_HALO, CB), lambda j, b, i: (0, j)), pl.BlockSpec((1, CB), lambda j, b, i: (0, j))],
        out_shape=(big, big, SDS((CONV_HALO, D_MODEL), f32), SDS((1, D_MODEL), f32)),
        args=(dc1, dc1, proj3, proj3, proj3, proj3, w),
        scratch_shapes=[pltpu.VMEM((CONV_WIN, CB), f32), pltpu.VMEM((CONV_WIN, CB), f32), pltpu.VMEM((CONV_K, SUBLANES, CB), f32),
                        pltpu.VMEM((SUBLANES - 1, CONV_WIN - SUBLANES, CB), f32),
                        pltpu.VMEM((SUBLANES - 1, CONV_WIN - SUBLANES, CB), f32)],
        sem=("parallel", "arbitrary", "arbitrary"), comm=comm)


def _band(first, dil):
    kw = Q_BLOCK if first else 2 * Q_BLOCK
    qi = lax.broadcasted_iota(jnp.int32, (Q_BLOCK, kw), 0)
    kj = lax.broadcasted_iota(jnp.int32, (Q_BLOCK, kw), 1)
    rel = qi - kj + (0 if first else Q_BLOCK)
    valid = jnp.logical_and(rel >= 0, rel <= Q_BLOCK)
    return valid, rel.astype(f32) * float(dil)


def _bias(first, dil, slope):
    valid, dist = _band(first, dil)
    return jnp.where(valid, -slope * dist, NEG)


def _scores(q, k, bias):
    return lax.dot_general(q, k, NT, preferred_element_type=f32) * (HEAD_DIM ** -0.5) + bias


def _pair_cols(hp):
    return slice(hp * LANES, (hp + 1) * LANES)


def _half(x2, e):
    lane = lax.broadcasted_iota(jnp.int32, (1, LANES), 1)
    keep = (lane < HEAD_DIM) if e == 0 else (lane >= HEAD_DIM)
    return jnp.where(keep, x2, jnp.zeros_like(x2))


def _attn_fwd(g, q4, k4, v4, qcb, kcb, vcb, comm=None):
    _, dil = GROUPS[g]
    B, r, L, _ = q4.shape
    nb = L // Q_BLOCK
    slopes = [float(s) for s in _SLOPES[g]]

    def body(q_ref, k_ref, v_ref, o_ref, lse_ref, bias, s_scr, p_scr):
        lane = lax.broadcasted_iota(jnp.int32, (Q_BLOCK, LANES), 1)
        if nb > 1:
            for h in range(HEADS):
                bias[h] = _bias(False, dil, slopes[h])

        def block(n, first):
            q0 = 0 if first else pl.multiple_of(n * Q_BLOCK, Q_BLOCK)
            k0 = 0 if first else pl.multiple_of((n - 1) * Q_BLOCK, Q_BLOCK)
            kw = Q_BLOCK if first else 2 * Q_BLOCK
            for hp in range(HEADS // 2):
                q2 = q_ref[pl.ds(q0, Q_BLOCK), _pair_cols(hp)]
                k2 = k_ref[pl.ds(k0, kw), _pair_cols(hp)]
                for e in range(2):
                    h = 2 * hp + e
                    b_h = _bias(True, dil, slopes[h]) if first else bias[h]
                    s_scr[h, :, :kw] = _scores(_half(q2, e), k2, b_h)
            st = jnp.zeros((Q_BLOCK, LANES), f32)
            dens = jnp.ones((Q_BLOCK, LANES), f32)
            for h in range(HEADS):
                s = s_scr[h, :, :kw]
                m = jnp.max(s, axis=-1, keepdims=True)
                p = jnp.exp(s - m)
                den = jnp.sum(p, axis=-1, keepdims=True)
                p_scr[h, :, :kw] = p.astype(bf16)
                st = jnp.where(lane == h, m + jnp.log(den), st)
                dens = jnp.where(lane == h, den, dens)
            lse_ref[pl.ds(q0, Q_BLOCK), :] = st
            inv = 1.0 / dens
            for hp in range(HEADS // 2):
                v2 = v_ref[pl.ds(k0, kw), _pair_cols(hp)]
                o2 = sum(jnp.dot(p_scr[2 * hp + e, :, :kw], _half(v2, e), preferred_element_type=f32) * inv[:, 2 * hp + e:2 * hp + e + 1]
                         for e in range(2))
                o_ref[pl.ds(q0, Q_BLOCK), _pair_cols(hp)] = o2.astype(bf16)

        block(0, True)
        if nb > 1:
            def step(n, carry):
                block(n, False)
                return carry
            lax.fori_loop(1, nb, step, 0)

    def spec(cb):
        return pl.BlockSpec((None, None, L, GW), lambda b, c: (b, c, 0, cb))

    return _launch(
        body, name=f"attn_fwd_g{g}", grid=(B, r), in_specs=[spec(qcb), spec(kcb), spec(vcb)],
        out_specs=[spec(0), pl.BlockSpec((None, None, L, LANES), lambda b, c: (b, c, 0, 0))],
        out_shape=(SDS((B, r, L, GW), bf16), SDS((B, r, L, LANES), f32)), args=(q4, k4, v4),
        scratch_shapes=[pltpu.VMEM((HEADS, Q_BLOCK, 2 * Q_BLOCK), f32), pltpu.VMEM((HEADS, Q_BLOCK, 2 * Q_BLOCK), f32),
                        pltpu.VMEM((HEADS, Q_BLOCK, 2 * Q_BLOCK), bf16)],
        sem=("parallel", "parallel"), comm=comm)


def _attn_bwd(g, q4, k4, v4, qcb, kcb, vcb, do4, lse4, dl4, comm=None):
    _, dil = GROUPS[g]
    B, r, L, _ = q4.shape
    nb = L // Q_BLOCK
    slopes = [float(s) for s in _SLOPES[g]]
    scale = HEAD_DIM ** -0.5

    def body(q_ref, k_ref, v_ref, do_ref, lse_ref, dl_ref, dq_ref, dk_ref, dv_ref, dk_acc, dv_acc, bias, s_scr, dp_scr, p_scr, ds_scr):
        dk_acc[...] = jnp.zeros_like(dk_acc)
        dv_acc[...] = jnp.zeros_like(dv_acc)
        if nb > 1:
            for h in range(HEADS):
                bias[h] = _bias(False, dil, slopes[h])

        def block(n, first):
            q0 = 0 if first else pl.multiple_of(n * Q_BLOCK, Q_BLOCK)
            k0 = 0 if first else pl.multiple_of((n - 1) * Q_BLOCK, Q_BLOCK)
            kw = Q_BLOCK if first else 2 * Q_BLOCK
            for hp in range(HEADS // 2):
                q2 = q_ref[pl.ds(q0, Q_BLOCK), _pair_cols(hp)]
                k2 = k_ref[pl.ds(k0, kw), _pair_cols(hp)]
                v2 = v_ref[pl.ds(k0, kw), _pair_cols(hp)]
                do2 = do_ref[pl.ds(q0, Q_BLOCK), _pair_cols(hp)]
                for e in range(2):
                    h = 2 * hp + e
                    b_h = _bias(True, dil, slopes[h]) if first else bias[h]
                    s_scr[h, :, :kw] = _scores(_half(q2, e), k2, b_h)
                    dp_scr[h, :, :kw] = lax.dot_general(_half(do2, e), v2, NT, preferred_element_type=f32)
            for h in range(HEADS):
                p = jnp.exp(s_scr[h, :, :kw] - lse_ref[pl.ds(q0, Q_BLOCK), h:h + 1])
                p_scr[h, :, :kw] = p.astype(bf16)
                ds_scr[h, :, :kw] = (p * (dp_scr[h, :, :kw] - dl_ref[pl.ds(q0, Q_BLOCK), h:h + 1])).astype(bf16)
            for hp in range(HEADS // 2):
                cols = _pair_cols(hp)
                q2 = q_ref[pl.ds(q0, Q_BLOCK), cols]
                k2 = k_ref[pl.ds(k0, kw), cols]
                do2 = do_ref[pl.ds(q0, Q_BLOCK), cols]
                ds = [ds_scr[2 * hp + e, :, :kw] for e in range(2)]
                dq2 = sum(jnp.dot(ds[e], _half(k2, e), preferred_element_type=f32) for e in range(2))
                dq_ref[pl.ds(q0, Q_BLOCK), cols] = (dq2 * scale).astype(bf16)
                dk2 = sum(lax.dot_general(ds[e], _half(q2, e), TN, preferred_element_type=f32) for e in range(2))
                dk_acc[pl.ds(k0, kw), cols] += dk2 * scale
                dv2 = sum(lax.dot_general(p_scr[2 * hp + e, :, :kw], _half(do2, e), TN, preferred_element_type=f32) for e in range(2))
                dv_acc[pl.ds(k0, kw), cols] += dv2

        block(0, True)
        if nb > 1:
            def step(n, carry):
                block(n, False)
                return carry
            lax.fori_loop(1, nb, step, 0)
        dk_ref[...] = dk_acc[...].astype(bf16)
        dv_ref[...] = dv_acc[...].astype(bf16)

    def spec(cb):
        return pl.BlockSpec((None, None, L, GW), lambda b, c: (b, c, 0, cb))

    st = pl.BlockSpec((None, None, L, LANES), lambda b, c: (b, c, 0, 0))
    osd = SDS((B, r, L, GW), bf16)
    return _launch(
        body, name=f"attn_bwd_g{g}", grid=(B, r), in_specs=[spec(qcb), spec(kcb), spec(vcb), spec(0), st, st],
        out_specs=[spec(0), spec(0), spec(0)], out_shape=(osd, osd, osd), args=(q4, k4, v4, do4, lse4, dl4),
        scratch_shapes=[pltpu.VMEM((L, GW), f32), pltpu.VMEM((L, GW), f32)]
        + [pltpu.VMEM((HEADS, Q_BLOCK, 2 * Q_BLOCK), f32)] * 3 + [pltpu.VMEM((HEADS, Q_BLOCK, 2 * Q_BLOCK), bf16)] * 2,
        sem=("parallel", "parallel"), comm=comm)


RT = 512
RCH = GW // LANES
DILS = tuple(d for _, d in GROUPS[1:])


def _res_spec(r, width):
    return pl.BlockSpec((None, r, RT // r, width), lambda b, i, *_: (b, 0, i, 0))


def _tok_spec(width, cb=0):
    return pl.BlockSpec((None, RT, width), lambda b, i, *_: (b, i, cb))


def _to_residues(res_ref, scr, r, width):
    for c in range(r):
        for jj in range(width // LANES):
            res_ref[c, :, jj * LANES:(jj + 1) * LANES] = scr[jj, pl.ds(c, RT // r, stride=r), :].astype(res_ref.dtype)


def _from_residues(scr, res_ref, r, width):
    for c in range(r):
        for jj in range(width // LANES):
            scr[jj, pl.ds(c, RT // r, stride=r), :] = res_ref[c, :, jj * LANES:(jj + 1) * LANES].astype(f32)


def _qkv_to_residues(proj3, g):
    r = GROUPS[g][1]
    B = proj3.shape[0]

    def body(q_ref, k_ref, v_ref, o_ref, scr):
        for p, x_ref in enumerate((q_ref, k_ref, v_ref)):
            x = x_ref[...].astype(f32)
            for jj in range(RCH):
                scr[p * RCH + jj] = x[:, jj * LANES:(jj + 1) * LANES]
        _to_residues(o_ref, scr, r, ATTN_W)

    return pl.pallas_call(
        body, grid=(B, SEQ // RT), in_specs=[_tok_spec(GW, 3 * p + g) for p in range(3)], out_specs=_res_spec(r, ATTN_W),
        out_shape=SDS((B, r, SEQ // r, ATTN_W), bf16), scratch_shapes=[pltpu.VMEM((3 * RCH, RT, LANES), f32)],
        name=f"qkv_to_residues_g{g}", compiler_params=_cp(("parallel", "parallel")))(proj3, proj3, proj3)


def _attn_mix(o0, l0, o_res, l_res):
    B = o0.shape[0]

    def body(o0_ref, l0_ref, o1_ref, o2_ref, l1_ref, l2_ref, y_ref, lt_ref, lt1_ref, lt2_ref, so, sl):
        for gi, (o_ref, l_ref, r) in enumerate(((o1_ref, l1_ref, DILS[0]), (o2_ref, l2_ref, DILS[1]))):
            _from_residues(so.at[gi], o_ref, r, GW)
            _from_residues(sl.at[gi:gi + 1], l_ref, r, LANES)
        ls = [l0_ref[...], sl[0], sl[1]]
        m = functools.reduce(jnp.maximum, ls)
        ws = [jnp.exp(l - m) for l in ls]
        den = ws[0] + ws[1] + ws[2]
        alphas = [w / den for w in ws]
        lt = m + jnp.log(den)
        lt_ref[...] = lt
        sl[2] = lt
        _to_residues(lt1_ref, sl.at[2:3], DILS[0], LANES)
        _to_residues(lt2_ref, sl.at[2:3], DILS[1], LANES)
        for h in range(HEADS):
            cols = slice(h * HEAD_DIM, (h + 1) * HEAD_DIM)
            jj, lo = divmod(h * HEAD_DIM, LANES)
            acc = alphas[0][:, h:h + 1] * o0_ref[:, cols].astype(f32)
            for gi in range(2):
                acc = acc + alphas[gi + 1][:, h:h + 1] * so[gi, jj, :, lo:lo + HEAD_DIM]
            y_ref[:, cols] = acc.astype(bf16)

    in_specs = [_tok_spec(GW), _tok_spec(LANES), _res_spec(DILS[0], GW), _res_spec(DILS[1], GW), _res_spec(DILS[0], LANES), _res_spec(DILS[1], LANES)]
    out_specs = [_tok_spec(GW), _tok_spec(LANES), _res_spec(DILS[0], LANES), _res_spec(DILS[1], LANES)]
    return pl.pallas_call(
        body, grid=(B, SEQ // RT), in_specs=in_specs, out_specs=out_specs,
        out_shape=(SDS((B, SEQ, GW), bf16), SDS((B, SEQ, LANES), f32)) + tuple(SDS((B, r, SEQ // r, LANES), f32) for r in DILS),
        scratch_shapes=[pltpu.VMEM((2, RCH, RT, LANES), f32), pltpu.VMEM((3, RT, LANES), f32)],
        name="attn_mix", compiler_params=_cp(("parallel", "parallel")))(o0, l0, *o_res, *l_res)


def _attn_delta(dmix, mix):
    B = dmix.shape[0]

    def body(d_ref, y_ref, dl_ref, dl1_ref, dl2_ref, dm1_ref, dm2_ref, sx, sd):
        lane = lax.broadcasted_iota(jnp.int32, (RT, LANES), 1)
        acc = jnp.zeros((RT, LANES), f32)
        dv = d_ref[...].astype(f32)
        for jj in range(RCH):
            sx[jj] = dv[:, jj * LANES:(jj + 1) * LANES]
        for h in range(HEADS):
            cols = slice(h * HEAD_DIM, (h + 1) * HEAD_DIM)
            dl = jnp.sum(dv[:, cols] * y_ref[:, cols].astype(f32), axis=-1, keepdims=True)
            acc = jnp.where(lane == h, dl, acc)
        dl_ref[...] = acc
        sd[0] = acc
        _to_residues(dl1_ref, sd, DILS[0], LANES)
        _to_residues(dl2_ref, sd, DILS[1], LANES)
        _to_residues(dm1_ref, sx, DILS[0], GW)
        _to_residues(dm2_ref, sx, DILS[1], GW)

    return pl.pallas_call(
        body, grid=(B, SEQ // RT), in_specs=[_tok_spec(GW), _tok_spec(GW)],
        out_specs=[_tok_spec(LANES), _res_spec(DILS[0], LANES), _res_spec(DILS[1], LANES), _res_spec(DILS[0], GW), _res_spec(DILS[1], GW)],
        out_shape=(SDS((B, SEQ, LANES), f32),) + tuple(SDS((B, r, SEQ // r, LANES), f32) for r in DILS)
        + tuple(SDS((B, r, SEQ // r, GW), bf16) for r in DILS),
        scratch_shapes=[pltpu.VMEM((RCH, RT, LANES), f32), pltpu.VMEM((1, RT, LANES), f32)],
        name="attn_delta", compiler_params=_cp(("parallel", "parallel")))(dmix, mix)


N_CB = IN_W // CB


def _assemble_dproj(dqkv, dua, dub, dga, dgc):
    B = dua.shape[0]
    ng = len(GROUPS)
    flat = [dqkv[g][p] for p in range(3) for g in range(ng)]
    wide = [dua, dub, dga, dgc]

    def body(*refs):
        srcs, wides, o_ref, scr = refs[:3 * ng], refs[3 * ng:3 * ng + 4], refs[3 * ng + 4], refs[3 * ng + 5]
        for jv in range(3 * ng):
            g = jv % ng
            if g == 0:
                o_ref[:, jv * GW:(jv + 1) * GW] = srcs[jv][...]
            else:
                _from_residues(scr, srcs[jv], GROUPS[g][1], GW)
                for jj in range(RCH):
                    o_ref[:, jv * GW + jj * LANES:jv * GW + (jj + 1) * LANES] = scr[jj].astype(bf16)
        for wv in range(4):
            lo = 3 * ATTN_W + wv * D_MODEL
            o_ref[:, lo:lo + D_MODEL] = wides[wv][...]

    in_specs = [_tok_spec(GW) if (jv % ng) == 0 else _res_spec(GROUPS[jv % ng][1], GW) for jv in range(3 * ng)]
    in_specs += [_tok_spec(D_MODEL)] * 4
    return pl.pallas_call(
        body, grid=(B, SEQ // RT), in_specs=in_specs, out_specs=_tok_spec(IN_W),
        out_shape=SDS((B, SEQ, IN_W), bf16), scratch_shapes=[pltpu.VMEM((RCH, RT, LANES), f32)],
        name="assemble_dproj", compiler_params=_cp(("parallel", "parallel")))(*flat, *wide)


def _same_shape_groups(names, arrays):
    groups = {}
    for n, a in zip(names, arrays):
        groups.setdefault(a.shape, ([], []))
        groups[a.shape][0].append(n)
        groups[a.shape][1].append(a)
    return list(groups.values())


def _add_pairs(parts, gots, core, name):
    k = len(parts)
    n, h, C = gots[0].shape

    def body(c_ref, *refs):
        for i in range(k):
            refs[2 * k + i][...] = (refs[i][...].astype(f32) + refs[k + i][...].astype(f32)).astype(bf16)

    blk = pl.BlockSpec((None, h, C), lambda s, c_ref: (s, 0, 0))
    own = pl.BlockSpec((None, h, C), lambda s, c_ref: (s, c_ref[0], 0))
    spec = pltpu.PrefetchScalarGridSpec(num_scalar_prefetch=1, grid=(n,), in_specs=[own] * k + [blk] * k, out_specs=[blk] * k)
    return pl.pallas_call(body, grid_spec=spec, out_shape=tuple(SDS((n, h, C), bf16) for _ in range(k)), name=name,
                          compiler_params=_cp(("parallel",)))(core, *parts, *gots)


def _sum4(sums, gots, chip, core, name):
    k = len(sums)
    _, h, C = sums[0].shape

    def body(s_ref, c_ref, *refs):
        for i in range(k):
            q_ref = refs[k + i]
            t = refs[i][...].astype(f32) + q_ref[0].astype(f32)
            t = t + q_ref[1].astype(f32)
            refs[2 * k + i][...] = t + q_ref[2].astype(f32)

    spec = pltpu.PrefetchScalarGridSpec(
        num_scalar_prefetch=2, grid=(1,),
        in_specs=[pl.BlockSpec((None, h, C), lambda i, s_ref, c_ref: (s_ref[0], 0, 0))] * k
        + [pl.BlockSpec((N_CHIPS - 1, h, C), lambda i, s_ref, c_ref: (0, 0, 0))] * k,
        out_specs=[pl.BlockSpec((h, C), lambda i, s_ref, c_ref: (c_ref[0], 0))] * k)
    return pl.pallas_call(body, grid_spec=spec, out_shape=tuple(SDS((2 * h, C), f32) for _ in range(k)), name=name,
                          compiler_params=_cp(("arbitrary",)))(chip, core, *sums, *gots)


def _adamw(ws, gs, ms, vs, name):
    k = len(ws)
    R, C = ws[0].shape
    rt = R
    for cand in (512, 256, 128, 64, 32, 16, 8):
        if R % cand == 0 and cand * C * 4 <= 2 ** 21:
            rt = cand
            break
    c1 = 1.0 / (1.0 - B1 ** STEP)
    c2 = 1.0 / (1.0 - B2 ** STEP)

    def body(*refs):
        for i in range(k):
            w_ref, g_ref, m_ref, v_ref = (refs[j * k + i] for j in range(4))
            d_ref, nm_ref, nv_ref = (refs[(4 + j) * k + i] for j in range(3))
            gv = g_ref[...]
            nm = B1 * m_ref[...] + (1.0 - B1) * gv
            nv = B2 * v_ref[...] + (1.0 - B2) * (gv * gv)
            nm_ref[...] = nm
            nv_ref[...] = nv
            d_ref[...] = -LR * ((nm * c1) / (jnp.sqrt(nv * c2) + ADAM_EPS) + WD * w_ref[...])

    blk = pl.BlockSpec((rt, C), lambda i: (i, 0))
    sd = SDS((R, C), f32)
    res = pl.pallas_call(body, grid=(R // rt,), in_specs=[blk] * (4 * k), out_specs=[blk] * (3 * k), out_shape=(sd,) * (3 * k),
                         name=name, compiler_params=_cp(("parallel",)))(*ws, *gs, *ms, *vs)
    return res[:k], res[k:2 * k], res[2 * k:]


def _coords():
    return lax.axis_index("x"), lax.axis_index("y"), lax.axis_index("c")


def _other_chips(x, y):
    return [(1 - x, y), (x, 1 - y), (1 - x, 1 - y)]


def _allgather_weights(shards, slots):
    n = len(shards)
    n_slot = 12

    def run(ins, outs, sems, first_hop_only):
        send, recv = sems
        x, y, c = _coords()
        me, cx, cy, cd = 2 * x + y, 2 * (1 - x) + y, 2 * x + (1 - y), 2 * (1 - x) + (1 - y)
        dev_x, dev_y, dev_s = (1 - x, y, c), (x, 1 - y, c), (x, y, 1 - c)
        started = []

        def copy(w, slot, src, dst, dev):
            return pltpu.make_async_remote_copy(src_ref=src, dst_ref=dst, send_sem=send.at[w, slot], recv_sem=recv.at[w, slot],
                                                device_id=dev, device_id_type=MESH)

        def go(cp):
            cp.start()
            started.append(cp)

        for w in range(n):
            q = shards[w].shape[0] // 4
            rows = [pl.ds(c * 2 * q + k * q, q) for k in range(2)]
            theirs = [pl.ds((1 - c) * 2 * q + k * q, q) for k in range(2)]
            own = [(ins[w].at[r, :], outs[w].at[me, r, :]) for r in rows]
            mine = [copy(w, 0, *own[0], dev_x), copy(w, 2, *own[1], dev_y), copy(w, 1, *own[1], dev_x), copy(w, 3, *own[0], dev_y)]
            if first_hop_only:
                for cp in mine:
                    cp.start()
                continue
            started.extend(mine)
            arrivals = [(0, cx, 0, (4, dev_y)), (2, cy, 1, (5, dev_x)), (1, cx, 1, None), (3, cy, 0, None), (4, cd, 0, None), (5, cd, 1, None)]
            for k, (slot, chip, quarter, onward) in enumerate(arrivals):
                blk = outs[w].at[chip, rows[quarter], :]
                copy(w, slot, blk, blk, dev_s).wait_recv()
                if onward is not None:
                    go(copy(w, onward[0], blk, blk, onward[1]))
                go(copy(w, 6 + k, blk, blk, dev_s))
            for k, (slot, chip, quarter, onward) in enumerate(arrivals):
                blk = outs[w].at[chip, theirs[quarter], :]
                copy(w, 6 + k, blk, blk, dev_s).wait_recv()
        for cp in started:
            cp.wait_send()

    return _Exchange(ins=list(shards) + list(slots), out_shape=[SDS((N_CHIPS,) + s.shape, s.dtype) for s in shards],
                     scratch=[pltpu.SemaphoreType.DMA((n, n_slot))] * 2, aliases={n + w: w for w in range(n)},
                     start=functools.partial(run, first_hop_only=True), wait=functools.partial(run, first_hop_only=False))


def _x_pair_exchange(parts):
    n = len(parts)
    halves = [p.shape[1] // 2 for p in parts]

    def copies(ins, outs, sems):
        send, recv = sems
        x, y, c = _coords()
        return [pltpu.make_async_remote_copy(src_ref=ins[w].at[:, pl.ds((1 - c) * halves[w], halves[w]), :], dst_ref=outs[w],
                                             send_sem=send.at[w], recv_sem=recv.at[w], device_id=(x, y, 1 - c), device_id_type=MESH)
                for w in range(n)]

    def start(ins, outs, sems):
        for cp in copies(ins, outs, sems):
            cp.start()

    def wait(ins, outs, sems):
        for cp in copies(ins, outs, sems):
            cp.wait()

    return _Exchange(ins=list(parts), out_shape=[SDS((N_CHIPS, p.shape[1] // 2, p.shape[2]), p.dtype) for p in parts],
                     scratch=[pltpu.SemaphoreType.DMA((n,))] * 2, aliases={}, start=start, wait=wait)


def _x_chip_exchange(sums):
    n = len(sums)

    def copies(ins, outs, sems):
        send, recv = sems
        x, y, c = _coords()
        return [pltpu.make_async_remote_copy(src_ref=ins[w].at[2 * px + py], dst_ref=outs[w].at[j], send_sem=send.at[w, j],
                                             recv_sem=recv.at[w, j], device_id=(px, py, c), device_id_type=MESH)
                for w in range(n) for j, (px, py) in enumerate(_other_chips(x, y))]

    def start(ins, outs, sems):
        for cp in copies(ins, outs, sems):
            cp.start()

    def wait(ins, outs, sems):
        for cp in copies(ins, outs, sems):
            cp.wait()

    return _Exchange(ins=list(sums), out_shape=[SDS((N_CHIPS - 1,) + s.shape[1:], s.dtype) for s in sums],
                     scratch=[pltpu.SemaphoreType.DMA((n, 3)), pltpu.SemaphoreType.DMA((n, 3))], aliases={}, start=start, wait=wait)


def _x_gather_ici(shards, slots):
    n = len(shards)
    halves = [s.shape[0] // 2 for s in shards]

    def copies(ins, outs, sems):
        send, recv = sems
        x, y, c = _coords()
        me = 2 * x + y
        out = []
        for w in range(n):
            mine = pl.ds(c * halves[w], halves[w])
            for j, (px, py) in enumerate(_other_chips(x, y)):
                snd = pltpu.make_async_remote_copy(src_ref=ins[w].at[mine, :], dst_ref=outs[w].at[me, mine, :], send_sem=send.at[w, j],
                                                   recv_sem=recv.at[w, j], device_id=(px, py, c), device_id_type=MESH)
                got = outs[w].at[2 * px + py, mine, :]
                rcv = pltpu.make_async_remote_copy(src_ref=got, dst_ref=got, send_sem=send.at[w, j], recv_sem=recv.at[w, j],
                                                   device_id=(px, py, c), device_id_type=MESH)
                out.append((snd, rcv))
        return out

    def start(ins, outs, sems):
        for snd, _ in copies(ins, outs, sems):
            snd.start()

    def wait(ins, outs, sems):
        for snd, rcv in copies(ins, outs, sems):
            rcv.wait_recv()
            snd.wait_send()

    return _Exchange(ins=list(shards) + list(slots), out_shape=[SDS(s.shape, s.dtype) for s in slots],
                     scratch=[pltpu.SemaphoreType.DMA((n, 3)), pltpu.SemaphoreType.DMA((n, 3))],
                     aliases={n + w: w for w in range(n)}, start=start, wait=wait)


def _x_gather_fwd(bufs):
    n = len(bufs)
    halves = [b.shape[1] // 2 for b in bufs]

    def copies(ins, outs, sems):
        send, recv = sems
        x, y, c = _coords()
        out = []
        for w in range(n):
            for j, (px, py) in enumerate(_other_chips(x, y)):
                mine = outs[w].at[2 * px + py, pl.ds(c * halves[w], halves[w]), :]
                theirs = outs[w].at[2 * px + py, pl.ds((1 - c) * halves[w], halves[w]), :]
                snd = pltpu.make_async_remote_copy(src_ref=mine, dst_ref=mine, send_sem=send.at[w, j], recv_sem=recv.at[w, j],
                                                   device_id=(x, y, 1 - c), device_id_type=MESH)
                rcv = pltpu.make_async_remote_copy(src_ref=theirs, dst_ref=theirs, send_sem=send.at[w, j], recv_sem=recv.at[w, j],
                                                   device_id=(x, y, 1 - c), device_id_type=MESH)
                out.append((snd, rcv))
        return out

    def start(ins, outs, sems):
        for snd, _ in copies(ins, outs, sems):
            snd.start()

    def wait(ins, outs, sems):
        for snd, rcv in copies(ins, outs, sems):
            rcv.wait_recv()
            snd.wait_send()

    return _Exchange(ins=list(bufs), out_shape=[SDS(b.shape, b.dtype) for b in bufs],
                     scratch=[pltpu.SemaphoreType.DMA((n, 3)), pltpu.SemaphoreType.DMA((n, 3))],
                     aliases={w: w for w in range(n)}, start=start, wait=wait)


def _final_exchange(bufs, vec):
    n = len(bufs)
    R = vec.shape[0]
    nd = 8

    def body(*refs):
        v_ref = refs[n]
        outs = refs[n + 1:2 * n + 1]
        o_ref, sbuf, send, recv, ssend, srecv = refs[2 * n + 1:]
        x, y, c = _coords()
        me = 4 * x + 2 * y + c
        sbuf[me] = v_ref[...]
        small = []
        for k in range(1, nd):
            kx, ky, kc = (k >> 2) & 1, (k >> 1) & 1, k & 1
            tx = x + kx - 2 * x * kx
            ty = y + ky - 2 * y * ky
            tc = c + kc - 2 * c * kc
            cp = pltpu.make_async_remote_copy(src_ref=v_ref, dst_ref=sbuf.at[me], send_sem=ssend.at[k], recv_sem=srecv.at[k],
                                              device_id=(tx, ty, tc), device_id_type=MESH)
            cp.start()
            small.append((cp, 4 * tx + 2 * ty + tc))
        cps = []
        for w in range(n):
            h = bufs[w].shape[0] // 2
            rows = outs[w].at[pl.ds(c * h, h), :]
            rc = pltpu.make_async_remote_copy(src_ref=rows, dst_ref=rows, send_sem=send.at[w], recv_sem=recv.at[w],
                                              device_id=(x, y, 1 - c), device_id_type=MESH)
            rc.start()
            cps.append(rc)
        for k, (cp, src) in zip(range(1, nd), small):
            pltpu.make_async_remote_copy(src_ref=v_ref, dst_ref=sbuf.at[src], send_sem=ssend.at[k], recv_sem=srecv.at[k],
                                         device_id=(x, y, c), device_id_type=MESH).wait_recv()
        for cp, _ in small:
            cp.wait_send()
        acc = sbuf[0]
        for d in range(1, nd):
            acc = acc + sbuf[d]
        o_ref[...] = acc
        for w, rc in enumerate(cps):
            h = bufs[w].shape[0] // 2
            other = outs[w].at[pl.ds((1 - c) * h, h), :]
            pltpu.make_async_remote_copy(src_ref=other, dst_ref=other, send_sem=send.at[w], recv_sem=recv.at[w],
                                         device_id=(x, y, 1 - c), device_id_type=MESH).wait_recv()
            rc.wait_send()

    vm = pl.BlockSpec(memory_space=pltpu.VMEM)
    res = pl.pallas_call(body, in_specs=[HBM] * n + [vm], out_specs=[HBM] * n + [vm],
                         out_shape=tuple(SDS(a.shape, a.dtype) for a in bufs) + (SDS((R, LANES), f32),),
                         input_output_aliases={w: w for w in range(n)},
                         scratch_shapes=[pltpu.VMEM((nd, R, LANES), f32)] + [pltpu.SemaphoreType.DMA((n,))] * 2
                         + [pltpu.SemaphoreType.DMA((nd,))] * 2, name="final_exchange",
                         compiler_params=pltpu.CompilerParams(has_side_effects=True))(*bufs, vec)
    return res[:n], res[n]


def _rows128(a, rows):
    flat = a.reshape(-1)
    return jnp.pad(flat, (0, rows * LANES - flat.shape[0])).reshape(rows, LANES)


GATHER_1 = ("w_conv_out", "w_attn_out", "w_ffn_gate", "w_ffn_down")
GATHER_2 = ("w_ffn_up", "w_o")
REDUCE_A = ("w_ffn_gate", "w_ffn_up", "w_ffn_down")
REDUCE_B = ("w_conv_out", "w_attn_out", "w_o")
REDUCE_C = ("w_in",)


def _step(x, target, h, h_t, norm1_g, gate_b, conv_w, conv_b, conv_ln_g, conv_ln_b, norm2_g, norm_f_g, w_in, shards, slots, chip1, core):
    B = x.shape[0]
    T = B * SEQ
    xf = x.reshape(T, D_MODEL)
    tf = target.reshape(T, D_MODEL)

    proj, got1 = _mm("in_proj", h, w_in, grid=(N_CHIPS, T // 1024),
                     a_spec=pl.BlockSpec((1024, D_MODEL), lambda s, m: (m, 0)),
                     b_spec=pl.BlockSpec((None, D_MODEL, IN_S), lambda s, m: (s, 0, 0)),
                     o_spec=pl.BlockSpec((1024, IN_S), lambda s, m: (m, s)), o_shape=(T, IN_W), o_dtype=bf16, dims=NN,
                     comm=_x_gather_ici([shards[n] for n in GATHER_1], [slots[n] for n in GATHER_1]))
    proj3 = proj.reshape(B, SEQ, IN_W)
    proj4 = proj.reshape(B, 1, SEQ, IN_W)

    qkv = [None] + [_qkv_to_residues(proj3, g) for g in range(1, len(GROUPS))]

    def qkv_args(g):
        return (proj4, proj4, proj4, 0, 3, 6) if g == 0 else (qkv[g], qkv[g], qkv[g], 0, 1, 2)

    (o4_0, l4_0), full1 = _attn_fwd(0, *qkv_args(0), comm=_x_gather_fwd(list(got1)))
    o4_1, l4_1 = _attn_fwd(1, *qkv_args(1))
    o4_2, l4_2 = _attn_fwd(2, *qkv_args(2))
    full = dict(zip(GATHER_1, full1))
    w_conv_out, w_attn_out, w_gate, w_down = (full[n] for n in GATHER_1)
    w_conv_out_f = w_conv_out.reshape(D_MODEL, D_MODEL)
    mix3, lse3, lse_r1, lse_r2 = _attn_mix(o4_0.reshape(B, SEQ, GW), l4_0.reshape(B, SEQ, LANES), [o4_1, o4_2], [l4_1, l4_2])
    mix = mix3.reshape(T, GW)
    y_attn = _attn_out_fwd(mix, w_attn_out)

    c1, got2 = _glu_conv_fwd(proj3, conv_w, conv_b, comm=_x_gather_ici([shards[n] for n in GATHER_2], [slots[n] for n in GATHER_2]))
    c1 = c1.reshape(T, D_MODEL)
    (c3, y_conv), (w_up, w_o) = _conv_out_fwd(c1, conv_ln_g, conv_ln_b, w_conv_out_f, comm=_x_gather_fwd(list(got2)))
    w_o_f = w_o.reshape(D_MODEL, D_MODEL)

    merged, x1, h2 = _merge_o_proj_rms(proj, gate_b, y_attn, y_conv, w_o_f, xf, norm2_g)
    fa, fb, ff = _ffn_up(h2, w_gate, w_up)
    loss, dx2, d_gf = _ffn_down_loss(ff, w_down, x1, norm_f_g.reshape(1, D_MODEL), tf)

    d_w_down = _mm("d_w_down", ff, dx2, grid=(N_CHIPS, T // 2048),
                   a_spec=pl.BlockSpec((None, 2048, FF_S), lambda s, k: (s, k, 0)),
                   b_spec=pl.BlockSpec((2048, D_MODEL), lambda s, k: (k, 0)),
                   o_spec=pl.BlockSpec((None, FF_S, D_MODEL), lambda s, k: (s, 0, 0)), o_shape=(N_CHIPS, FF_S, D_MODEL),
                   o_dtype=bf16, dims=TN, acc_shape=(FF_S, D_MODEL), k_axis=1, sem=("parallel", "arbitrary"))
    da, db = _ffn_down_bwd(dx2, w_down, fa, fb)

    d_w_gate, d_w_up = _ffn_up_bwd_w(da, db, h2)
    part = dict(w_ffn_gate=d_w_gate, w_ffn_up=d_w_up, w_ffn_down=d_w_down)

    def pair_sums(names, got):
        out = {}
        for ns, gs in _same_shape_groups(names, got):
            out.update(zip(ns, _add_pairs([part[n] for n in ns], gs, core, "pair_sum_" + ns[0])))
        return [out[n] for n in names]

    (dx1, d_g2), got = _mm_rms_bwd("ffn_dh2", [(da, w_gate), (db, w_up)], x1, norm2_g, dx2, tm=1024,
                                   a_spec=pl.BlockSpec((None, 1024, FF_S), lambda m, s: (s, m, 0)),
                                   b_spec=pl.BlockSpec((None, FF_S, D_MODEL), lambda m, s: (s, 0, 0)), dims=NN,
                                   comm=_x_pair_exchange([part[n] for n in REDUCE_A]), vmem=VMEM_LIMIT_BIG)
    sums_a = pair_sums(REDUCE_A, got)

    d_w_o = _mm_tn_tokens("d_w_o", merged, dx1, bf16).reshape(N_CHIPS, D_MODEL // N_CHIPS, D_MODEL)
    dmerged = _mm_nt_full("d_merged", dx1, w_o_f, bf16)
    dya, dyc, dga, dgc, d_gba, d_gbc = _merge_bwd(dmerged, proj, gate_b, y_attn, y_conv)

    d_w_conv_out = _mm_tn_tokens("d_w_conv_out", c3, dyc, bf16).reshape(N_CHIPS, D_MODEL // N_CHIPS, D_MODEL)
    dc1, d_ln_g, d_ln_b = _conv_out_bwd(dyc, w_conv_out_f, c1, conv_ln_g, conv_ln_b)
    (dua, dub, d_conv_w, d_conv_b), got_a = _glu_conv_bwd(dc1.reshape(B, SEQ, D_MODEL), proj3, conv_w, comm=_x_chip_exchange(sums_a))

    d_w_attn_out = _attn_out_bwd_w(mix, dya)
    part.update(w_conv_out=d_w_conv_out, w_attn_out=d_w_attn_out, w_o=d_w_o)
    dmix, got = _attn_out_bwd_x(dya, w_attn_out, comm=_x_pair_exchange([part[n] for n in REDUCE_B]))
    sums_b = pair_sums(REDUCE_B, got)
    dmix3 = dmix.reshape(B, SEQ, GW)
    delta3, delta_r1, delta_r2, dmix_r1, dmix_r2 = _attn_delta(dmix3, mix3)
    one = (B, 1, SEQ)
    (dq0, dk0, dv0), got_b = _attn_bwd(0, *qkv_args(0), dmix3.reshape(one + (GW,)), lse3.reshape(one + (LANES,)),
                                       delta3.reshape(one + (LANES,)), comm=_x_chip_exchange(sums_b))
    dqkv = [tuple(t.reshape(B, SEQ, GW) for t in (dq0, dk0, dv0)),
            _attn_bwd(1, *qkv_args(1), dmix_r1, lse_r1, delta_r1), _attn_bwd(2, *qkv_args(2), dmix_r2, lse_r2, delta_r2)]
    dproj = _assemble_dproj(dqkv, dua, dub, dga.reshape(B, SEQ, D_MODEL), dgc.reshape(B, SEQ, D_MODEL)).reshape(T, IN_W)

    d_w_in = _mm("d_w_in", h_t, dproj, grid=(N_CHIPS, T // 2048),
                 a_spec=pl.BlockSpec((D_MODEL, 2048), lambda s, k: (0, k)),
                 b_spec=pl.BlockSpec((2048, IN_S), lambda s, k: (k, s)),
                 o_spec=pl.BlockSpec((None, D_MODEL, IN_S), lambda s, k: (s, 0, 0)), o_shape=(N_CHIPS, D_MODEL, IN_S),
                 o_dtype=bf16, dims=NN, acc_shape=(D_MODEL, IN_S), k_axis=1, sem=("parallel", "arbitrary"), vmem=VMEM_LIMIT_BIG)
    part.update(w_in=d_w_in)
    sums_c = pair_sums(REDUCE_C, _run_exchange(_x_pair_exchange([d_w_in]), "grad_pair_exchange_c"))
    (dx, d_g1), got_c = _mm_rms_bwd("d_h", [(dproj, w_in)], xf, norm1_g, dx1, tm=1024,
                                    a_spec=pl.BlockSpec((1024, IN_S), lambda m, s: (m, s)),
                                    b_spec=pl.BlockSpec((None, D_MODEL, IN_S), lambda m, s: (s, 0, 0)), dims=NT,
                                    comm=_x_chip_exchange(sums_c), vmem=VMEM_LIMIT_BIG)

    names = REDUCE_A + REDUCE_B + REDUCE_C
    sums = dict(zip(names, sums_a + sums_b + sums_c))
    halves = {}
    for ns, gs in _same_shape_groups(names, got_a + got_b + got_c):
        halves.update(zip(ns, _sum4([sums[n] for n in ns], gs, chip1, core, "chip_sum_" + ns[0])))
    big = {n: halves[n] for n in names}
    small = dict(norm1_g=d_g1, gate_b=jnp.concatenate([d_gba, d_gbc], axis=-1), conv_b=d_conv_b, conv_ln_g=d_ln_g,
                 conv_ln_b=d_ln_b, norm2_g=d_g2, norm_f_g=d_gf, conv_w=d_conv_w)
    return loss, dx.reshape(B, SEQ, D_MODEL), big, small


BIG = ("w_in", "w_conv_out", "w_attn_out", "w_o", "w_ffn_gate", "w_ffn_up", "w_ffn_down")
TRANSPOSED = ("w_ffn_gate", "w_ffn_up")
SMALL = ("norm1_g", "gate_b", "conv_b", "conv_ln_g", "conv_ln_b", "norm2_g", "norm_f_g")
SMALL_ROWS = {"norm1_g": 8, "gate_b": 16, "conv_b": 8, "conv_ln_g": 8, "conv_ln_b": 8, "norm2_g": 8, "norm_f_g": 8}
LOSS_ROWS = 8
CONVW_ROWS = 32 * D_MODEL // LANES


def kernel(x, norm1_g, w_in, gate_b, conv_w, conv_b, conv_ln_g, conv_ln_b, w_conv_out, w_attn_out, w_o, norm2_g, w_ffn_gate, w_ffn_up, w_ffn_down, norm_f_g, loss_target, m_norm1_g, m_w_in, m_gate_b, m_conv_w, m_conv_b, m_conv_ln_g, m_conv_ln_b, m_w_conv_out, m_w_attn_out, m_w_o, m_norm2_g, m_w_ffn_gate, m_w_ffn_up, m_w_ffn_down, m_norm_f_g, v_norm1_g, v_w_in, v_gate_b, v_conv_w, v_conv_b, v_conv_ln_g, v_conv_ln_b, v_w_conv_out, v_w_attn_out, v_w_o, v_norm2_g, v_w_ffn_gate, v_w_ffn_up, v_w_ffn_down, v_norm_f_g):
    W = dict(norm1_g=norm1_g, w_in=w_in, gate_b=gate_b, conv_w=conv_w, conv_b=conv_b, conv_ln_g=conv_ln_g, conv_ln_b=conv_ln_b,
             w_conv_out=w_conv_out, w_attn_out=w_attn_out, w_o=w_o, norm2_g=norm2_g, w_ffn_gate=w_ffn_gate, w_ffn_up=w_ffn_up,
             w_ffn_down=w_ffn_down, norm_f_g=norm_f_g)
    M = dict(norm1_g=m_norm1_g, w_in=m_w_in, gate_b=m_gate_b, conv_w=m_conv_w, conv_b=m_conv_b, conv_ln_g=m_conv_ln_g,
             conv_ln_b=m_conv_ln_b, w_conv_out=m_w_conv_out, w_attn_out=m_w_attn_out, w_o=m_w_o, norm2_g=m_norm2_g,
             w_ffn_gate=m_w_ffn_gate, w_ffn_up=m_w_ffn_up, w_ffn_down=m_w_ffn_down, norm_f_g=m_norm_f_g)
    V = dict(norm1_g=v_norm1_g, w_in=v_w_in, gate_b=v_gate_b, conv_w=v_conv_w, conv_b=v_conv_b, conv_ln_g=v_conv_ln_g,
             conv_ln_b=v_conv_ln_b, w_conv_out=v_w_conv_out, w_attn_out=v_w_attn_out, w_o=v_w_o, norm2_g=v_norm2_g,
             w_ffn_gate=v_w_ffn_gate, w_ffn_up=v_w_ffn_up, w_ffn_down=v_w_ffn_down, norm_f_g=v_norm_f_g)
    order = list(W)

    def as2d(n, a):
        a = a.reshape(a.shape[-2:])
        return a.T if n in TRANSPOSED else a

    def from2d(n, a):
        return (a.T if n in TRANSPOSED else a).reshape(W[n].shape)

    shard2d = {n: as2d(n, W[n]) for n in BIG}
    chip = 2 * lax.axis_index("x") + lax.axis_index("y")

    core = lax.axis_index("c").astype(jnp.int32).reshape(1)
    chip1 = chip.astype(jnp.int32).reshape(1)
    shards = {n: shard2d[n].astype(bf16) for n in BIG}

    def slot_for(s):
        return lax.dynamic_update_slice(lax.empty((N_CHIPS,) + s.shape, s.dtype), s[None], (chip, 0, 0))

    slots = {n: slot_for(s) for n, s in shards.items()}
    cw = jnp.pad(conv_w.reshape(CONV_K, D_MODEL // N_CHIPS), ((0, CONV_HALO - CONV_K), (0, 0)))
    (h, h_t), (w_in_full, cw4) = _rms_fwd(x.reshape(-1, D_MODEL), norm1_g, "rms1_fwd",
                                          comm=_allgather_weights([shards["w_in"], cw], [slots["w_in"], slot_for(cw)]))
    conv_w_full = cw4.transpose(1, 0, 2).reshape(CONV_HALO, D_MODEL)[:CONV_K]

    loss, grad_x, grads, gsmall = _step(x, loss_target, h, h_t, norm1_g, gate_b, conv_w_full, conv_b, conv_ln_g, conv_ln_b, norm2_g,
                                        norm_f_g, w_in_full, shards, slots, chip1, core)

    pieces = [_rows128(loss, LOSS_ROWS)] + [_rows128(gsmall[n], SMALL_ROWS[n]) for n in SMALL] + [_rows128(gsmall["conv_w"], CONVW_ROWS)]
    full_rows, tot = _final_exchange([grads[n] for n in BIG], jnp.concatenate(pieces, axis=0))
    grads = dict(zip(BIG, full_rows))
    loss_out = tot[0, 0]
    row = LOSS_ROWS
    for n in SMALL:
        grads[n] = tot[row: row + W[n].size // LANES].reshape(W[n].shape)
        row += SMALL_ROWS[n]
    dcw = tot[row: row + CONV_K * D_MODEL // LANES].reshape(CONV_K, D_MODEL)
    grads["conv_w"] = lax.dynamic_slice(dcw, (0, chip * (D_MODEL // N_CHIPS)), (CONV_K, D_MODEL // N_CHIPS))

    delta, new_m, new_v = {}, {}, {}
    for ns, ws in _same_shape_groups(BIG, [shard2d[n] for n in BIG]):
        ds, nms, nvs = _adamw(ws, [grads[n] for n in ns], [as2d(n, M[n]) for n in ns], [as2d(n, V[n]) for n in ns], "adamw_" + ns[0])
        for n, d, nm, nv in zip(ns, ds, nms, nvs):
            delta[n], new_m[n], new_v[n], grads[n] = (from2d(n, t) for t in (d, nm, nv, grads[n]))

    def pack(src):
        return jnp.concatenate([_rows128(src[n], SMALL_ROWS[n]) for n in SMALL], axis=0)

    (d,), (nm,), (nv,) = _adamw([pack(W)], [pack(grads)], [pack(M)], [pack(V)], "adamw_small")
    row = 0
    for n in SMALL:
        k = W[n].size // LANES
        delta[n], new_m[n], new_v[n] = (t[row: row + k].reshape(W[n].shape) for t in (d, nm, nv))
        row += SMALL_ROWS[n]

    def pad32(a):
        return jnp.pad(a.reshape(CONV_K, D_MODEL // N_CHIPS), ((0, 1), (0, 0)))

    (d,), (nm,), (nv,) = _adamw([pad32(conv_w)], [pad32(grads["conv_w"])], [pad32(m_conv_w)], [pad32(v_conv_w)], "adamw_conv_w")
    delta["conv_w"], new_m["conv_w"], new_v["conv_w"] = (t[:CONV_K].reshape(conv_w.shape) for t in (d, nm, nv))
    grads["conv_w"] = grads["conv_w"].reshape(conv_w.shape)

    return (loss_out, grad_x, *[grads[n] for n in order], *[delta[n] for n in order],
            *[new_m[n] for n in order], *[new_v[n] for n in order])
```
